```python
import math
import jax, jax.numpy as jnp
from jax import lax
import numpy as np

D_MODEL = 1024
BATCH = 8
SEQ = 4096
DEPTH = 2

N_MIXERS = 2
EXPAND = 2
D_INNER = EXPAND * D_MODEL
NORM_EPS = 1e-6
S5_GROUP = 16
S5_GROUPS = D_INNER // S5_GROUP
S5_STATE = 64
S5_CHUNK = 128
DT_MIN = 1e-3
DT_MAX = 1e-1
GDN_HEADS = 8
GDN_DK = D_MODEL // GDN_HEADS
GDN_DV = D_INNER // GDN_HEADS
GDN_CONV = 4
GDN_CHUNK = 64
GDN_QK = GDN_HEADS * GDN_DK
GDN_CONV_CH = 2 * GDN_QK + D_INNER
GDN_PROJ = GDN_CONV_CH + D_INNER + 2 * GDN_HEADS
N_S5 = (DEPTH + 1) // 2
N_GDN = DEPTH // 2

kernel_name = "hybrid_s5_gated_deltanet_adaln"

F32 = jnp.float32


def rms_norm(x, w):
    xf = x.astype(F32)
    y = xf * lax.rsqrt(jnp.mean(xf * xf, axis=-1, keepdims=True) + NORM_EPS) * w.astype(F32)
    return y.astype(x.dtype)


def l2norm(x):
    return x * lax.rsqrt(jnp.sum(x * x, axis=-1, keepdims=True) + NORM_EPS)


def _diag_combine(e1, e2):
    a1r, a1i, b1r, b1i = e1
    a2r, a2i, b2r, b2i = e2
    return (a2r * a1r - a2i * a1i,
            a2r * a1i + a2i * a1r,
            a2r * b1r - a2i * b1i + b2r,
            a2r * b1i + a2i * b1r + b2i)


def s5_mixer(h, w_in, lam_re, lam_im, log_dt, b_re, b_im, c_re, c_im, d_skip, w_glu, w_out):
    bsz, seqlen, _ = h.shape
    u, z = jnp.split(h @ w_in, 2, axis=-1)
    u = u.astype(F32)
    lam_re = lam_re.astype(F32); lam_im = lam_im.astype(F32)
    b_re = b_re.astype(F32); b_im = b_im.astype(F32)
    c_re = c_re.astype(F32); c_im = c_im.astype(F32)
    dt = jnp.exp(log_dt.astype(F32))[:, None]
    mag = jnp.exp(lam_re * dt)
    ab_re = mag * jnp.cos(lam_im * dt)
    ab_im = mag * jnp.sin(lam_im * dt)
    den = lam_re * lam_re + lam_im * lam_im
    nr = ab_re - 1.0
    ni = ab_im
    q_re = (nr * lam_re + ni * lam_im) / den
    q_im = (ni * lam_re - nr * lam_im) / den
    bb_re = q_re[..., None] * b_re - q_im[..., None] * b_im
    bb_im = q_re[..., None] * b_im + q_im[..., None] * b_re

    n_chunks = seqlen // S5_CHUNK
    u_chunks = u.reshape(bsz, n_chunks, S5_CHUNK, S5_GROUPS, S5_GROUP).transpose(1, 0, 2, 3, 4)
    a_re = jnp.broadcast_to(ab_re, (bsz, S5_CHUNK, S5_GROUPS, S5_STATE))
    a_im = jnp.broadcast_to(ab_im, (bsz, S5_CHUNK, S5_GROUPS, S5_STATE))

    def chunk_step(carry, u_c):
        s_re, s_im = carry
        bu_re = jnp.einsum('bcgm,gpm->bcgp', u_c, bb_re)
        bu_im = jnp.einsum('bcgm,gpm->bcgp', u_c, bb_im)
        bu_re = bu_re.at[:, 0].add(ab_re * s_re - ab_im * s_im)
        bu_im = bu_im.at[:, 0].add(ab_re * s_im + ab_im * s_re)
        _, _, x_re, x_im = lax.associative_scan(_diag_combine, (a_re, a_im, bu_re, bu_im), axis=1)
        y_c = (jnp.einsum('bcgp,gmp->bcgm', x_re, c_re)
               - jnp.einsum('bcgp,gmp->bcgm', x_im, c_im))
        return (x_re[:, -1], x_im[:, -1]), y_c

    init = (jnp.zeros((bsz, S5_GROUPS, S5_STATE), F32), jnp.zeros((bsz, S5_GROUPS, S5_STATE), F32))
    _, y = lax.scan(chunk_step, init, u_chunks)
    y = y.transpose(1, 0, 2, 3, 4).reshape(bsz, seqlen, D_INNER) + d_skip.astype(F32) * u
    y = jax.nn.gelu(y)
    y = y * jax.nn.sigmoid(y @ w_glu.astype(F32))
    y = y.astype(h.dtype) * jax.nn.silu(z)
    return y @ w_out


def _chunk_gated_delta_rule(q, k, v, beta, g):
    bsz, seqlen, nh, dk = q.shape
    dv = v.shape[-1]
    C = GDN_CHUNK
    nc = seqlen // C

    def to_chunks(t):
        return t.reshape((bsz, nc, C, nh) + t.shape[3:]).swapaxes(2, 3)

    q, k, v, beta, g = (to_chunks(t) for t in (q, k, v, beta, g))
    gc = jnp.cumsum(g, axis=-1)
    causal = jnp.tril(jnp.ones((C, C), bool))
    strict = jnp.tril(jnp.ones((C, C), bool), -1)
    decay = jnp.exp(jnp.where(causal, gc[..., :, None] - gc[..., None, :], -jnp.inf))
    kk = jnp.einsum('bnhid,bnhjd->bnhij', k, k)
    a_mat = jnp.where(strict, beta[..., None] * kk * decay, 0.0)
    lhs = a_mat + jnp.eye(C, dtype=F32)
    rhs_w = (beta * jnp.exp(gc))[..., None] * k
    rhs_u = beta[..., None] * v
    w = lax.linalg.triangular_solve(lhs, rhs_w, left_side=True, lower=True, unit_diagonal=True)
    u = lax.linalg.triangular_solve(lhs, rhs_u, left_side=True, lower=True, unit_diagonal=True)
    qk = jnp.einsum('bnhid,bnhjd->bnhij', q, k) * decay
    q_dec = q * jnp.exp(gc)[..., None]
    g_last = gc[..., -1]
    k_dec = k * jnp.exp(g_last[..., None] - gc)[..., None]

    def step(state, xs):
        q_c, w_c, u_c, qk_c, k_c, gl_c = xs
        v_new = u_c - jnp.einsum('bhcd,bhde->bhce', w_c, state)
        o_c = (jnp.einsum('bhcd,bhde->bhce', q_c, state)
               + jnp.einsum('bhij,bhje->bhie', qk_c, v_new))
        state = (jnp.exp(gl_c)[..., None, None] * state
                 + jnp.einsum('bhcd,bhce->bhde', k_c, v_new))
        return state, o_c

    xs = tuple(jnp.moveaxis(t, 1, 0) for t in (q_dec, w, u, qk, k_dec, g_last))
    init = jnp.zeros((bsz, nh, dk, dv), F32)
    _, o = lax.scan(step, init, xs)
    return o.transpose(1, 0, 3, 2, 4).reshape(bsz, seqlen, nh, dv)


def gdn_mixer(h, w_in, conv_w, a_log, dt_bias, norm_w, w_out):
    bsz, seqlen, _ = h.shape
    proj = h @ w_in
    qkv, z, b_logit, a_logit = jnp.split(
        proj, [GDN_CONV_CH, GDN_CONV_CH + D_INNER, GDN_CONV_CH + D_INNER + GDN_HEADS], axis=-1)
    qkv = lax.conv_general_dilated(
        qkv.astype(F32), conv_w.astype(F32)[:, None, :], (1,), [(GDN_CONV - 1, 0)],
        dimension_numbers=('NWC', 'WIO', 'NWC'), feature_group_count=GDN_CONV_CH)
    qkv = jax.nn.silu(qkv)
    q, k, v = jnp.split(qkv, [GDN_QK, 2 * GDN_QK], axis=-1)
    q = l2norm(q.reshape(bsz, seqlen, GDN_HEADS, GDN_DK)) * (GDN_DK ** -0.5)
    k = l2norm(k.reshape(bsz, seqlen, GDN_HEADS, GDN_DK))
    v = v.reshape(bsz, seqlen, GDN_HEADS, GDN_DV)
    beta = jax.nn.sigmoid(b_logit.astype(F32))
    g = -jnp.exp(a_log.astype(F32)) * jax.nn.softplus(a_logit.astype(F32) + dt_bias.astype(F32))
    o = _chunk_gated_delta_rule(q, k, v, beta, g)
    o = o * lax.rsqrt(jnp.mean(o * o, axis=-1, keepdims=True) + NORM_EPS) * norm_w.astype(F32)
    o = o.reshape(bsz, seqlen, D_INNER).astype(h.dtype) * jax.nn.silu(z)
    return o @ w_out


def _fwd_setup_inputs(seed: int = 0) -> dict:
    key = jax.random.key(seed)
    ks = iter(jax.random.split(key, 32))

    def nrm(shape, scale):
        return scale * jax.random.normal(next(ks), shape, F32)

    s5_lambda_im = (math.pi * jnp.broadcast_to(jnp.arange(S5_STATE, dtype=F32), (N_S5, S5_GROUPS, S5_STATE))
                    + nrm((N_S5, S5_GROUPS, S5_STATE), 0.01))
    gdn_dt = jnp.exp(jax.random.uniform(next(ks), (N_GDN, GDN_HEADS), F32, math.log(DT_MIN), math.log(DT_MAX)))
    return {
        "x": nrm((BATCH, SEQ, D_MODEL), 1.0),
        "c": nrm((BATCH, D_MODEL), 1.0),
        "ada_w": nrm((DEPTH, D_MODEL, 3 * D_MODEL), D_MODEL ** -0.5),
        "ada_b": nrm((DEPTH, 3 * D_MODEL), 0.02),
        "norm_w": 1.0 + nrm((DEPTH, D_MODEL), 0.02),
        "s5_w_in": nrm((N_S5, D_MODEL, 2 * D_INNER), D_MODEL ** -0.5),
        "s5_lambda_re": -0.5 + nrm((N_S5, S5_GROUPS, S5_STATE), 0.01),
        "s5_lambda_im": s5_lambda_im,
        "s5_log_dt": jax.random.uniform(next(ks), (N_S5, S5_GROUPS), F32, math.log(DT_MIN), math.log(DT_MAX)),
        "s5_b_re": nrm((N_S5, S5_GROUPS, S5_STATE, S5_GROUP), (2 * S5_GROUP) ** -0.5),
        "s5_b_im": nrm((N_S5, S5_GROUPS, S5_STATE, S5_GROUP), (2 * S5_GROUP) ** -0.5),
        "s5_c_re": nrm((N_S5, S5_GROUPS, S5_GROUP, S5_STATE), S5_STATE ** -0.5),
        "s5_c_im": nrm((N_S5, S5_GROUPS, S5_GROUP, S5_STATE), S5_STATE ** -0.5),
        "s5_d": nrm((N_S5, D_INNER), 1.0),
        "s5_w_glu": nrm((N_S5, D_INNER, D_INNER), D_INNER ** -0.5),
        "s5_w_out": nrm((N_S5, D_INNER, D_MODEL), D_INNER ** -0.5),
        "gdn_w_in": nrm((N_GDN, D_MODEL, GDN_PROJ), D_MODEL ** -0.5),
        "gdn_conv_w": nrm((N_GDN, GDN_CONV, GDN_CONV_CH), GDN_CONV ** -0.5),
        "gdn_a_log": jnp.log(jax.random.uniform(next(ks), (N_GDN, GDN_HEADS), F32, 1.0, 16.0)),
        "gdn_dt_bias": gdn_dt + jnp.log(-jnp.expm1(-gdn_dt)),
        "gdn_norm_w": 1.0 + nrm((N_GDN, GDN_DV), 0.02),
        "gdn_w_out": nrm((N_GDN, D_INNER, D_MODEL), D_INNER ** -0.5),
        "final_norm_w": 1.0 + nrm((D_MODEL,), 0.02),
    }


def _fwd_reference(x, c, ada_w, ada_b, norm_w, s5_w_in, s5_lambda_re, s5_lambda_im, s5_log_dt,
              s5_b_re, s5_b_im, s5_c_re, s5_c_im, s5_d, s5_w_glu, s5_w_out,
              gdn_w_in, gdn_conv_w, gdn_a_log, gdn_dt_bias, gdn_norm_w, gdn_w_out, final_norm_w):
    c_act = jax.nn.silu(c)
    for layer in range(DEPTH):
        mod = c_act @ ada_w[layer] + ada_b[layer]
        shift, scale, gate = jnp.split(mod, 3, axis=-1)
        h = rms_norm(x, norm_w[layer]) * (1.0 + scale[:, None, :]) + shift[:, None, :]
        j = layer // N_MIXERS
        if layer % N_MIXERS == 0:
            y = s5_mixer(h, s5_w_in[j], s5_lambda_re[j], s5_lambda_im[j], s5_log_dt[j],
                         s5_b_re[j], s5_b_im[j], s5_c_re[j], s5_c_im[j], s5_d[j],
                         s5_w_glu[j], s5_w_out[j])
        else:
            y = gdn_mixer(h, gdn_w_in[j], gdn_conv_w[j], gdn_a_log[j], gdn_dt_bias[j],
                          gdn_norm_w[j], gdn_w_out[j])
        x = x + (gate[:, None, :] * y).astype(x.dtype)
    return rms_norm(x, final_norm_w)


import jax as _jax
import jax.numpy as _jnp

TWIN_FORMAT = 'train_step'
FWD_PARAMS = ['x', 'c', 'ada_w', 'ada_b', 'norm_w', 's5_w_in', 's5_lambda_re', 's5_lambda_im', 's5_log_dt', 's5_b_re', 's5_b_im', 's5_c_re', 's5_c_im', 's5_d', 's5_w_glu', 's5_w_out', 'gdn_w_in', 'gdn_conv_w', 'gdn_a_log', 'gdn_dt_bias', 'gdn_norm_w', 'gdn_w_out', 'final_norm_w']
TWIN_WEIGHTS = ['ada_w', 'ada_b', 'norm_w', 's5_w_in', 's5_lambda_re', 's5_lambda_im', 's5_log_dt', 's5_b_re', 's5_b_im', 's5_c_re', 's5_c_im', 's5_d', 's5_w_glu', 's5_w_out', 'gdn_w_in', 'gdn_conv_w', 'gdn_a_log', 'gdn_dt_bias', 'gdn_norm_w', 'gdn_w_out', 'final_norm_w']
TWIN_DIFF_INPUT = 'x'
TWIN_INPUTS = ['x', 'c', 'ada_w', 'ada_b', 'norm_w', 's5_w_in', 's5_lambda_re', 's5_lambda_im', 's5_log_dt', 's5_b_re', 's5_b_im', 's5_c_re', 's5_c_im', 's5_d', 's5_w_glu', 's5_w_out', 'gdn_w_in', 'gdn_conv_w', 'gdn_a_log', 'gdn_dt_bias', 'gdn_norm_w', 'gdn_w_out', 'final_norm_w', 'loss_target', 'm_ada_w', 'm_ada_b', 'm_norm_w', 'm_s5_w_in', 'm_s5_lambda_re', 'm_s5_lambda_im', 'm_s5_log_dt', 'm_s5_b_re', 'm_s5_b_im', 'm_s5_c_re', 'm_s5_c_im', 'm_s5_d', 'm_s5_w_glu', 'm_s5_w_out', 'm_gdn_w_in', 'm_gdn_conv_w', 'm_gdn_a_log', 'm_gdn_dt_bias', 'm_gdn_norm_w', 'm_gdn_w_out', 'm_final_norm_w', 'v_ada_w', 'v_ada_b', 'v_norm_w', 'v_s5_w_in', 'v_s5_lambda_re', 'v_s5_lambda_im', 'v_s5_log_dt', 'v_s5_b_re', 'v_s5_b_im', 'v_s5_c_re', 'v_s5_c_im', 'v_s5_d', 'v_s5_w_glu', 'v_s5_w_out', 'v_gdn_w_in', 'v_gdn_conv_w', 'v_gdn_a_log', 'v_gdn_dt_bias', 'v_gdn_norm_w', 'v_gdn_w_out', 'v_final_norm_w']
TWIN_OUTPUTS = ['loss', 'grad_x', 'grad_ada_w', 'grad_ada_b', 'grad_norm_w', 'grad_s5_w_in', 'grad_s5_lambda_re', 'grad_s5_lambda_im', 'grad_s5_log_dt', 'grad_s5_b_re', 'grad_s5_b_im', 'grad_s5_c_re', 'grad_s5_c_im', 'grad_s5_d', 'grad_s5_w_glu', 'grad_s5_w_out', 'grad_gdn_w_in', 'grad_gdn_conv_w', 'grad_gdn_a_log', 'grad_gdn_dt_bias', 'grad_gdn_norm_w', 'grad_gdn_w_out', 'grad_final_norm_w', 'delta_ada_w', 'delta_ada_b', 'delta_norm_w', 'delta_s5_w_in', 'delta_s5_lambda_re', 'delta_s5_lambda_im', 'delta_s5_log_dt', 'delta_s5_b_re', 'delta_s5_b_im', 'delta_s5_c_re', 'delta_s5_c_im', 'delta_s5_d', 'delta_s5_w_glu', 'delta_s5_w_out', 'delta_gdn_w_in', 'delta_gdn_conv_w', 'delta_gdn_a_log', 'delta_gdn_dt_bias', 'delta_gdn_norm_w', 'delta_gdn_w_out', 'delta_final_norm_w', 'new_m_ada_w', 'new_m_ada_b', 'new_m_norm_w', 'new_m_s5_w_in', 'new_m_s5_lambda_re', 'new_m_s5_lambda_im', 'new_m_s5_log_dt', 'new_m_s5_b_re', 'new_m_s5_b_im', 'new_m_s5_c_re', 'new_m_s5_c_im', 'new_m_s5_d', 'new_m_s5_w_glu', 'new_m_s5_w_out', 'new_m_gdn_w_in', 'new_m_gdn_conv_w', 'new_m_gdn_a_log', 'new_m_gdn_dt_bias', 'new_m_gdn_norm_w', 'new_m_gdn_w_out', 'new_m_final_norm_w', 'new_v_ada_w', 'new_v_ada_b', 'new_v_norm_w', 'new_v_s5_w_in', 'new_v_s5_lambda_re', 'new_v_s5_lambda_im', 'new_v_s5_log_dt', 'new_v_s5_b_re', 'new_v_s5_b_im', 'new_v_s5_c_re', 'new_v_s5_c_im', 'new_v_s5_d', 'new_v_s5_w_glu', 'new_v_s5_w_out', 'new_v_gdn_w_in', 'new_v_gdn_conv_w', 'new_v_gdn_a_log', 'new_v_gdn_dt_bias', 'new_v_gdn_norm_w', 'new_v_gdn_w_out', 'new_v_final_norm_w']
TWIN_LEAF_KINDS = {'loss': 'loss', 'grad_x': 'grad_x', 'grad_ada_w': 'grad_w', 'grad_ada_b': 'grad_w', 'grad_norm_w': 'grad_w', 'grad_s5_w_in': 'grad_w', 'grad_s5_lambda_re': 'grad_w', 'grad_s5_lambda_im': 'grad_w', 'grad_s5_log_dt': 'grad_w', 'grad_s5_b_re': 'grad_w', 'grad_s5_b_im': 'grad_w', 'grad_s5_c_re': 'grad_w', 'grad_s5_c_im': 'grad_w', 'grad_s5_d': 'grad_w', 'grad_s5_w_glu': 'grad_w', 'grad_s5_w_out': 'grad_w', 'grad_gdn_w_in': 'grad_w', 'grad_gdn_conv_w': 'grad_w', 'grad_gdn_a_log': 'grad_w', 'grad_gdn_dt_bias': 'grad_w', 'grad_gdn_norm_w': 'grad_w', 'grad_gdn_w_out': 'grad_w', 'grad_final_norm_w': 'grad_w', 'delta_ada_w': 'delta_w', 'delta_ada_b': 'delta_w', 'delta_norm_w': 'delta_w', 'delta_s5_w_in': 'delta_w', 'delta_s5_lambda_re': 'delta_w', 'delta_s5_lambda_im': 'delta_w', 'delta_s5_log_dt': 'delta_w', 'delta_s5_b_re': 'delta_w', 'delta_s5_b_im': 'delta_w', 'delta_s5_c_re': 'delta_w', 'delta_s5_c_im': 'delta_w', 'delta_s5_d': 'delta_w', 'delta_s5_w_glu': 'delta_w', 'delta_s5_w_out': 'delta_w', 'delta_gdn_w_in': 'delta_w', 'delta_gdn_conv_w': 'delta_w', 'delta_gdn_a_log': 'delta_w', 'delta_gdn_dt_bias': 'delta_w', 'delta_gdn_norm_w': 'delta_w', 'delta_gdn_w_out': 'delta_w', 'delta_final_norm_w': 'delta_w', 'new_m_ada_w': 'new_m', 'new_m_ada_b': 'new_m', 'new_m_norm_w': 'new_m', 'new_m_s5_w_in': 'new_m', 'new_m_s5_lambda_re': 'new_m', 'new_m_s5_lambda_im': 'new_m', 'new_m_s5_log_dt': 'new_m', 'new_m_s5_b_re': 'new_m', 'new_m_s5_b_im': 'new_m', 'new_m_s5_c_re': 'new_m', 'new_m_s5_c_im': 'new_m', 'new_m_s5_d': 'new_m', 'new_m_s5_w_glu': 'new_m', 'new_m_s5_w_out': 'new_m', 'new_m_gdn_w_in': 'new_m', 'new_m_gdn_conv_w': 'new_m', 'new_m_gdn_a_log': 'new_m', 'new_m_gdn_dt_bias': 'new_m', 'new_m_gdn_norm_w': 'new_m', 'new_m_gdn_w_out': 'new_m', 'new_m_final_norm_w': 'new_m', 'new_v_ada_w': 'new_v', 'new_v_ada_b': 'new_v', 'new_v_norm_w': 'new_v', 'new_v_s5_w_in': 'new_v', 'new_v_s5_lambda_re': 'new_v', 'new_v_s5_lambda_im': 'new_v', 'new_v_s5_log_dt': 'new_v', 'new_v_s5_b_re': 'new_v', 'new_v_s5_b_im': 'new_v', 'new_v_s5_c_re': 'new_v', 'new_v_s5_c_im': 'new_v', 'new_v_s5_d': 'new_v', 'new_v_s5_w_glu': 'new_v', 'new_v_s5_w_out': 'new_v', 'new_v_gdn_w_in': 'new_v', 'new_v_gdn_conv_w': 'new_v', 'new_v_gdn_a_log': 'new_v', 'new_v_gdn_dt_bias': 'new_v', 'new_v_gdn_norm_w': 'new_v', 'new_v_gdn_w_out': 'new_v', 'new_v_final_norm_w': 'new_v'}


def _forward(args):
    return _fwd_reference(*[args[k] for k in FWD_PARAMS])


def _output_shape():
    out = _jax.eval_shape(lambda: _forward(_fwd_setup_inputs(0)))
    return out.shape, out.dtype

N_MICROBATCH = 1
ADAM_LR = 0.001
ADAM_B1 = 0.9
ADAM_B2 = 0.999
ADAM_EPS = 1e-08
ADAM_WD = 0.01
ADAM_STEP = 10
PER_EXAMPLE_BATCH_AXIS = {'x': 0, 'c': 0, 'loss_target': 0}
SHARED_INPUTS = []
_WEIGHT_DTYPES = {'ada_w': _jnp.float32, 'ada_b': _jnp.float32, 'norm_w': _jnp.float32, 's5_w_in': _jnp.float32, 's5_lambda_re': _jnp.float32, 's5_lambda_im': _jnp.float32, 's5_log_dt': _jnp.float32, 's5_b_re': _jnp.float32, 's5_b_im': _jnp.float32, 's5_c_re': _jnp.float32, 's5_c_im': _jnp.float32, 's5_d': _jnp.float32, 's5_w_glu': _jnp.float32, 's5_w_out': _jnp.float32, 'gdn_w_in': _jnp.float32, 'gdn_conv_w': _jnp.float32, 'gdn_a_log': _jnp.float32, 'gdn_dt_bias': _jnp.float32, 'gdn_norm_w': _jnp.float32, 'gdn_w_out': _jnp.float32, 'final_norm_w': _jnp.float32}
MOMENT_SCALE = {'ada_w': 5.997080e-02, 'ada_b': 1.028002e-01, 'norm_w': 9.211091e-02, 's5_w_in': 3.444423e-02, 's5_lambda_re': 5.615760e-03, 's5_lambda_im': 5.047737e-03, 's5_log_dt': 2.019816e+00, 's5_b_re': 2.808032e-03, 's5_b_im': 2.594528e-03, 's5_c_re': 3.676770e-03, 's5_c_im': 3.836474e-03, 's5_d': 3.689240e-02, 's5_w_glu': 1.235403e-02, 's5_w_out': 4.570594e-02, 'gdn_w_in': 5.208645e-02, 'gdn_conv_w': 5.181125e-02, 'gdn_a_log': 1.755805e-01, 'gdn_dt_bias': 1.745715e-01, 'gdn_norm_w': 1.474455e-01, 'gdn_w_out': 7.589857e-02, 'final_norm_w': 3.216129e+01}


def _to_microbatches(a, axis):
    t = _jnp.moveaxis(a, axis, 0)
    t = t.reshape((N_MICROBATCH, t.shape[0] // N_MICROBATCH) + t.shape[1:])
    return _jnp.moveaxis(t, 1, axis + 1)


def setup_inputs(seed: int = 0) -> dict:
    inp = _fwd_setup_inputs(seed)
    key = _jax.random.fold_in(_jax.random.key(seed), 7919)
    shape, _ = _output_shape()
    out = dict(inp)
    out["loss_target"] = _jax.random.normal(_jax.random.fold_in(key, 0), shape, _jnp.float32)
    for i, name in enumerate(TWIN_WEIGHTS):
        w = inp[name].astype(_jnp.float32)
        if MOMENT_SCALE is None:
            s = _jnp.sqrt(_jnp.mean(_jnp.square(w)) + 1e-30)
        else:
            s = MOMENT_SCALE[name]
        km, kv = _jax.random.split(_jax.random.fold_in(key, i + 1))
        out[name] = w
        out["m_" + name] = s * _jax.random.normal(km, w.shape, _jnp.float32)
        out["v_" + name] = (s * s) * _jax.random.uniform(kv, w.shape, _jnp.float32, 0.5, 1.5)
    if N_MICROBATCH > 1:
        for name, axis in PER_EXAMPLE_BATCH_AXIS.items():
            out[name] = _to_microbatches(out[name], axis)
    return {'x': out['x'], 'c': out['c'], 'ada_w': out['ada_w'], 'ada_b': out['ada_b'], 'norm_w': out['norm_w'], 's5_w_in': out['s5_w_in'], 's5_lambda_re': out['s5_lambda_re'], 's5_lambda_im': out['s5_lambda_im'], 's5_log_dt': out['s5_log_dt'], 's5_b_re': out['s5_b_re'], 's5_b_im': out['s5_b_im'], 's5_c_re': out['s5_c_re'], 's5_c_im': out['s5_c_im'], 's5_d': out['s5_d'], 's5_w_glu': out['s5_w_glu'], 's5_w_out': out['s5_w_out'], 'gdn_w_in': out['gdn_w_in'], 'gdn_conv_w': out['gdn_conv_w'], 'gdn_a_log': out['gdn_a_log'], 'gdn_dt_bias': out['gdn_dt_bias'], 'gdn_norm_w': out['gdn_norm_w'], 'gdn_w_out': out['gdn_w_out'], 'final_norm_w': out['final_norm_w'], 'loss_target': out['loss_target'], 'm_ada_w': out['m_ada_w'], 'm_ada_b': out['m_ada_b'], 'm_norm_w': out['m_norm_w'], 'm_s5_w_in': out['m_s5_w_in'], 'm_s5_lambda_re': out['m_s5_lambda_re'], 'm_s5_lambda_im': out['m_s5_lambda_im'], 'm_s5_log_dt': out['m_s5_log_dt'], 'm_s5_b_re': out['m_s5_b_re'], 'm_s5_b_im': out['m_s5_b_im'], 'm_s5_c_re': out['m_s5_c_re'], 'm_s5_c_im': out['m_s5_c_im'], 'm_s5_d': out['m_s5_d'], 'm_s5_w_glu': out['m_s5_w_glu'], 'm_s5_w_out': out['m_s5_w_out'], 'm_gdn_w_in': out['m_gdn_w_in'], 'm_gdn_conv_w': out['m_gdn_conv_w'], 'm_gdn_a_log': out['m_gdn_a_log'], 'm_gdn_dt_bias': out['m_gdn_dt_bias'], 'm_gdn_norm_w': out['m_gdn_norm_w'], 'm_gdn_w_out': out['m_gdn_w_out'], 'm_final_norm_w': out['m_final_norm_w'], 'v_ada_w': out['v_ada_w'], 'v_ada_b': out['v_ada_b'], 'v_norm_w': out['v_norm_w'], 'v_s5_w_in': out['v_s5_w_in'], 'v_s5_lambda_re': out['v_s5_lambda_re'], 'v_s5_lambda_im': out['v_s5_lambda_im'], 'v_s5_log_dt': out['v_s5_log_dt'], 'v_s5_b_re': out['v_s5_b_re'], 'v_s5_b_im': out['v_s5_b_im'], 'v_s5_c_re': out['v_s5_c_re'], 'v_s5_c_im': out['v_s5_c_im'], 'v_s5_d': out['v_s5_d'], 'v_s5_w_glu': out['v_s5_w_glu'], 'v_s5_w_out': out['v_s5_w_out'], 'v_gdn_w_in': out['v_gdn_w_in'], 'v_gdn_conv_w': out['v_gdn_conv_w'], 'v_gdn_a_log': out['v_gdn_a_log'], 'v_gdn_dt_bias': out['v_gdn_dt_bias'], 'v_gdn_norm_w': out['v_gdn_norm_w'], 'v_gdn_w_out': out['v_gdn_w_out'], 'v_final_norm_w': out['v_final_norm_w']}


def _loss(weights, diff, rest, loss_target):
    with _jax.named_scope("forward"):
        args = {**rest, TWIN_DIFF_INPUT: diff, **{k: w.astype(_WEIGHT_DTYPES[k]) for k, w in weights.items()}}
        y = _forward(args)
    with _jax.named_scope("loss_head"):
        err = _jnp.square(y.astype(_jnp.float32) - loss_target)
        return 0.5 * _jnp.sum(_jnp.mean(err, axis=-1)) if err.ndim else 0.5 * err


def _adamw(w, g, m, v):
    m = ADAM_B1 * m + (1.0 - ADAM_B1) * g
    v = ADAM_B2 * v + (1.0 - ADAM_B2) * _jnp.square(g)
    m_hat = m / (1.0 - ADAM_B1 ** ADAM_STEP)
    v_hat = v / (1.0 - ADAM_B2 ** ADAM_STEP)
    delta = -ADAM_LR * (m_hat / (_jnp.sqrt(v_hat) + ADAM_EPS) + ADAM_WD * w)
    return delta, m, v


def reference(x, c, ada_w, ada_b, norm_w, s5_w_in, s5_lambda_re, s5_lambda_im, s5_log_dt, s5_b_re, s5_b_im, s5_c_re, s5_c_im, s5_d, s5_w_glu, s5_w_out, gdn_w_in, gdn_conv_w, gdn_a_log, gdn_dt_bias, gdn_norm_w, gdn_w_out, final_norm_w, loss_target, m_ada_w, m_ada_b, m_norm_w, m_s5_w_in, m_s5_lambda_re, m_s5_lambda_im, m_s5_log_dt, m_s5_b_re, m_s5_b_im, m_s5_c_re, m_s5_c_im, m_s5_d, m_s5_w_glu, m_s5_w_out, m_gdn_w_in, m_gdn_conv_w, m_gdn_a_log, m_gdn_dt_bias, m_gdn_norm_w, m_gdn_w_out, m_final_norm_w, v_ada_w, v_ada_b, v_norm_w, v_s5_w_in, v_s5_lambda_re, v_s5_lambda_im, v_s5_log_dt, v_s5_b_re, v_s5_b_im, v_s5_c_re, v_s5_c_im, v_s5_d, v_s5_w_glu, v_s5_w_out, v_gdn_w_in, v_gdn_conv_w, v_gdn_a_log, v_gdn_dt_bias, v_gdn_norm_w, v_gdn_w_out, v_final_norm_w):
    given = dict(x=x, c=c, ada_w=ada_w, ada_b=ada_b, norm_w=norm_w, s5_w_in=s5_w_in, s5_lambda_re=s5_lambda_re, s5_lambda_im=s5_lambda_im, s5_log_dt=s5_log_dt, s5_b_re=s5_b_re, s5_b_im=s5_b_im, s5_c_re=s5_c_re, s5_c_im=s5_c_im, s5_d=s5_d, s5_w_glu=s5_w_glu, s5_w_out=s5_w_out, gdn_w_in=gdn_w_in, gdn_conv_w=gdn_conv_w, gdn_a_log=gdn_a_log, gdn_dt_bias=gdn_dt_bias, gdn_norm_w=gdn_norm_w, gdn_w_out=gdn_w_out, final_norm_w=final_norm_w, loss_target=loss_target, m_ada_w=m_ada_w, m_ada_b=m_ada_b, m_norm_w=m_norm_w, m_s5_w_in=m_s5_w_in, m_s5_lambda_re=m_s5_lambda_re, m_s5_lambda_im=m_s5_lambda_im, m_s5_log_dt=m_s5_log_dt, m_s5_b_re=m_s5_b_re, m_s5_b_im=m_s5_b_im, m_s5_c_re=m_s5_c_re, m_s5_c_im=m_s5_c_im, m_s5_d=m_s5_d, m_s5_w_glu=m_s5_w_glu, m_s5_w_out=m_s5_w_out, m_gdn_w_in=m_gdn_w_in, m_gdn_conv_w=m_gdn_conv_w, m_gdn_a_log=m_gdn_a_log, m_gdn_dt_bias=m_gdn_dt_bias, m_gdn_norm_w=m_gdn_norm_w, m_gdn_w_out=m_gdn_w_out, m_final_norm_w=m_final_norm_w, v_ada_w=v_ada_w, v_ada_b=v_ada_b, v_norm_w=v_norm_w, v_s5_w_in=v_s5_w_in, v_s5_lambda_re=v_s5_lambda_re, v_s5_lambda_im=v_s5_lambda_im, v_s5_log_dt=v_s5_log_dt, v_s5_b_re=v_s5_b_re, v_s5_b_im=v_s5_b_im, v_s5_c_re=v_s5_c_re, v_s5_c_im=v_s5_c_im, v_s5_d=v_s5_d, v_s5_w_glu=v_s5_w_glu, v_s5_w_out=v_s5_w_out, v_gdn_w_in=v_gdn_w_in, v_gdn_conv_w=v_gdn_conv_w, v_gdn_a_log=v_gdn_a_log, v_gdn_dt_bias=v_gdn_dt_bias, v_gdn_norm_w=v_gdn_norm_w, v_gdn_w_out=v_gdn_w_out, v_final_norm_w=v_final_norm_w)
    weights = {n: given[n] for n in TWIN_WEIGHTS}
    shared = {n: given[n] for n in SHARED_INPUTS}
    per_example = {n: given[n] for n in ['x', 'c']}
    grad_fn = _jax.value_and_grad(_loss, argnums=(0, 1))

    def one_microbatch(ex, loss_target):
        ex = dict(ex)
        diff = ex.pop(TWIN_DIFF_INPUT)
        return grad_fn(weights, diff, {**shared, **ex}, loss_target)

    if N_MICROBATCH == 1:
        loss, (grad_w, grad_x) = one_microbatch(per_example, given["loss_target"])
    else:
        def body(carry, xs):
            loss_sum, grad_sum = carry
            l_k, (gw_k, gx_k) = one_microbatch(xs[0], xs[1])
            with _jax.named_scope("update"):
                return (loss_sum + l_k, _jax.tree.map(_jnp.add, grad_sum, gw_k)), gx_k

        init = (_jnp.zeros((), _jnp.float32), _jax.tree.map(_jnp.zeros_like, weights))
        (loss, grad_w), grad_x = _jax.lax.scan(body, init, (per_example, given["loss_target"]))
    with _jax.named_scope("update"):
        delta_w, new_m, new_v = {}, {}, {}
        for n in TWIN_WEIGHTS:
            delta_w[n], new_m[n], new_v[n] = _adamw(weights[n], grad_w[n], given["m_" + n], given["v_" + n])
    return (loss, grad_x, *[grad_w[n] for n in TWIN_WEIGHTS], *[delta_w[n] for n in TWIN_WEIGHTS],
            *[new_m[n] for n in TWIN_WEIGHTS], *[new_v[n] for n in TWIN_WEIGHTS])
```

```python
import functools
import math

import jax
import jax.numpy as jnp
from jax import lax
from jax.experimental import pallas as pl
from jax.experimental.pallas import tpu as pltpu

F32 = jnp.float32
BF16 = jnp.bfloat16
SDS = jax.ShapeDtypeStruct

D_MODEL = 1024
D_INNER = 2048
NORM_EPS = 1e-6
S5_GROUP = 16
S5_GROUPS = 128
S5_STATE = 64
GDN_HEADS = 8
GDN_DK = 128
GDN_DV = 256
GDN_CONV = 4
GDN_CHUNK = 64
GDN_QK = 1024
GDN_CONV_CH = 4096
GDN_PROJ = 6160
ADAM_LR = 0.001
ADAM_B1 = 0.9
ADAM_B2 = 0.999
ADAM_EPS = 1e-08
ADAM_WD = 0.01
ADAM_STEP = 10

N_DEV = 8
LANES = 128
SUBLANES = 8
VMEM_BIG = 56 << 20
VMEM_MID = 40 << 20
S5_GB = 8
S5_TL = 1024
MESH_AXES = ("x", "y", "c")


def _params(sem, vmem=None):
    return pltpu.CompilerParams(dimension_semantics=sem, vmem_limit_bytes=vmem)


def _bdot(a, b, dims=(((1,), (0,)), ((), ()))):
    return lax.dot_general(a.astype(BF16), b.astype(BF16), dims, preferred_element_type=F32)


def _hdot(a, b, dims=(((1,), (0,)), ((), ()))):
    return lax.dot_general(a, b, dims, preferred_element_type=F32, precision=lax.Precision.HIGHEST)


NN = (((1,), (0,)), ((), ()))
NT = (((1,), (1,)), ((), ()))
TN = (((0,), (0,)), ((), ()))


def _tile(n, pref):
    for t in (pref, 512, 256, 128):
        if t <= n and n % t == 0:
            return t
    return n


def _matmul(a, b, mode, name):
    if mode == "nn":
        (m, k), (_, n) = a.shape, b.shape
    elif mode == "nt":
        (m, k), (n, _) = a.shape, b.shape
    else:
        (k, m), (_, n) = a.shape, b.shape
    tm, tn, tk = _tile(m, 512), _tile(n, 512), _tile(k, 512)
    nk = k // tk
    dims = {"nn": NN, "nt": NT, "tn": TN}[mode]

    def body(a_ref, b_ref, o_ref, acc_ref):
        kk = pl.program_id(2)

        @pl.when(kk == 0)
        def _():
            acc_ref[...] = jnp.zeros_like(acc_ref)
        acc_ref[...] += _bdot(a_ref[...], b_ref[...], dims)

        @pl.when(kk == nk - 1)
        def _():
            o_ref[...] = acc_ref[...]

    a_spec = pl.BlockSpec((tk, tm), lambda i, j, q: (q, i)) if mode == "tn" else pl.BlockSpec((tm, tk), lambda i, j, q: (i, q))
    b_spec = pl.BlockSpec((tn, tk), lambda i, j, q: (j, q)) if mode == "nt" else pl.BlockSpec((tk, tn), lambda i, j, q: (q, j))
    return pl.pallas_call(
        body, name=name, grid=(m // tm, n // tn, nk),
        in_specs=[a_spec, b_spec], out_specs=pl.BlockSpec((tm, tn), lambda i, j, q: (i, j)),
        out_shape=SDS((m, n), F32), scratch_shapes=[pltpu.VMEM((tm, tn), F32)],
        compiler_params=_params(("parallel", "parallel", "arbitrary"), VMEM_MID),
    )(a, b)


def make_mm(name):
    @jax.custom_vjp
    def mm(a, w):
        return _matmul(a, w, "nn", name + "_fwd")

    def fwd(a, w):
        return _matmul(a, w, "nn", name + "_fwd"), (a, w)

    def bwd(res, g):
        a, w = res
        return _matmul(g, w, "nt", name + "_dx"), _matmul(a, g, "tn", name + "_dw")

    mm.defvjp(fwd, bwd)
    return mm


def make_rowwise(f, name, tm, n_rows, n_params, vmem=VMEM_MID):
    def specs_of(arrs, blocked):
        if blocked:
            return [pl.BlockSpec((tm, a.shape[1]), lambda i: (i, 0)) for a in arrs]
        return [pl.BlockSpec(a.shape, lambda i: (0, 0)) for a in arrs]

    def out_structs(rows, params):
        blk = [SDS((tm, r.shape[1]), r.dtype) for r in rows] + [SDS(p.shape, p.dtype) for p in params]
        return jax.eval_shape(f, *blk)

    def run_fwd(rows, params):
        L = rows[0].shape[0]
        outs = out_structs(rows, params)

        def body(*refs):
            ins = [r[...] for r in refs[:n_rows + n_params]]
            res = f(*ins)
            for o_ref, val in zip(refs[n_rows + n_params:], res):
                o_ref[...] = val

        return pl.pallas_call(
            body, name=name + "_fwd", grid=(L // tm,),
            in_specs=specs_of(rows, True) + specs_of(params, False),
            out_specs=[pl.BlockSpec((tm, o.shape[1]), lambda i: (i, 0)) for o in outs],
            out_shape=[SDS((L, o.shape[1]), o.dtype) for o in outs],
            compiler_params=_params(("parallel",), vmem),
        )(*rows, *params)

    def run_bwd(rows, params, gs):
        L = rows[0].shape[0]
        n_g = len(gs)

        def body(*refs):
            i = pl.program_id(0)
            ins = [r[...] for r in refs[:n_rows + n_params]]
            cts = tuple(r[...] for r in refs[n_rows + n_params:n_rows + n_params + n_g])
            outs = refs[n_rows + n_params + n_g:]
            _, vjp = jax.vjp(f, *ins)
            grads = vjp(cts)
            for o_ref, val in zip(outs[:n_rows], grads[:n_rows]):
                o_ref[...] = val

            if n_params:
                @pl.when(i == 0)
                def _():
                    for o_ref in outs[n_rows:]:
                        o_ref[...] = jnp.zeros_like(o_ref)
                for o_ref, val in zip(outs[n_rows:], grads[n_rows:]):
                    o_ref[...] += val

        res = pl.pallas_call(
            body, name=name + "_bwd", grid=(L // tm,),
            in_specs=specs_of(rows, True) + specs_of(params, False) + specs_of(gs, True),
            out_specs=specs_of(rows, True) + specs_of(params, False),
            out_shape=[SDS(r.shape, r.dtype) for r in rows] + [SDS(p.shape, p.dtype) for p in params],
            compiler_params=_params(("arbitrary",), vmem),
        )(*rows, *params, *gs)
        return tuple(res[:n_rows]), tuple(res[n_rows:])

    @jax.custom_vjp
    def op(rows, params):
        return tuple(run_fwd(rows, params))

    def fwd(rows, params):
        return tuple(run_fwd(rows, params)), (rows, params)

    def bwd(res, gs):
        rows, params = res
        return run_bwd(rows, params, tuple(gs))

    op.defvjp(fwd, bwd)
    return op


def _s5_scan_rows(xr_ref, xi_ref, ar, ai, x0r, x0i, tl):
    def step(t8, carry):
        xr, xi = carry
        base = pl.multiple_of(t8 * SUBLANES, SUBLANES)
        for r in range(SUBLANES):
            br = xr_ref[pl.ds(base + r, 1), :]
            bi = xi_ref[pl.ds(base + r, 1), :]
            nr = ar * xr - ai * xi + br
            ni = ar * xi + ai * xr + bi
            xr, xi = nr, ni
            xr_ref[pl.ds(base + r, 1), :] = xr
            xi_ref[pl.ds(base + r, 1), :] = xi
        return xr, xi
    return lax.fori_loop(0, tl // SUBLANES, step, (x0r, x0i))


def _s5_fwd_call(u, bre, bim, cre, cim, a, tl):
    L, e = u.shape
    nb = e // LANES
    ns = bre.shape[2]
    nc = L // tl

    def body(u_ref, bre_ref, bim_ref, cre_ref, cim_ref, a_ref, ys_ref, xb_ref, xr_ref, xi_ref, carry_ref):
        c = pl.program_id(1)

        @pl.when(c == 0)
        def _():
            carry_ref[...] = jnp.zeros_like(carry_ref)
        xb_ref[0, 0] = carry_ref[...]
        ub = u_ref[...]
        xr_ref[...] = _bdot(ub, bre_ref[0])
        xi_ref[...] = _bdot(ub, bim_ref[0])
        ar = a_ref[0, 0:1, :]
        ai = a_ref[0, 1:2, :]
        xr, xi = _s5_scan_rows(xr_ref, xi_ref, ar, ai, carry_ref[0:1, :], carry_ref[1:2, :], tl)
        carry_ref[0:1, :] = xr
        carry_ref[1:2, :] = xi
        ys_ref[...] = _bdot(xr_ref[...], cre_ref[0]) - _bdot(xi_ref[...], cim_ref[0])

    return pl.pallas_call(
        body, name="s5_core_fwd", grid=(nb, nc),
        in_specs=[pl.BlockSpec((tl, LANES), lambda j, c: (c, j)),
                  pl.BlockSpec((1, LANES, ns), lambda j, c: (j, 0, 0)), pl.BlockSpec((1, LANES, ns), lambda j, c: (j, 0, 0)),
                  pl.BlockSpec((1, ns, LANES), lambda j, c: (j, 0, 0)), pl.BlockSpec((1, ns, LANES), lambda j, c: (j, 0, 0)),
                  pl.BlockSpec((1, SUBLANES, ns), lambda j, c: (j, 0, 0))],
        out_specs=[pl.BlockSpec((tl, LANES), lambda j, c: (c, j)),
                   pl.BlockSpec((1, 1, SUBLANES, ns), lambda j, c: (j, c, 0, 0))],
        out_shape=[SDS((L, e), F32), SDS((nb, nc, SUBLANES, ns), F32)],
        scratch_shapes=[pltpu.VMEM((tl, ns), F32), pltpu.VMEM((tl, ns), F32), pltpu.VMEM((SUBLANES, ns), F32)],
        compiler_params=_params(("arbitrary", "arbitrary"), VMEM_MID),
    )(u, bre, bim, cre, cim, a)


def _s5_bwd_call(u, dys, bre, bim, cre, cim, a, xb, tl):
    L, e = u.shape
    nb = e // LANES
    ns = bre.shape[2]
    nc = L // tl

    def body(u_ref, dys_ref, bre_ref, bim_ref, cre_ref, cim_ref, a_ref, xb_ref,
             du_ref, dbre_ref, dbim_ref, dcre_ref, dcim_ref, da_ref,
             xr_ref, xi_ref, gr_ref, gi_ref, gcarry_ref):
        c = pl.program_id(1)

        @pl.when(c == 0)
        def _():
            gcarry_ref[...] = jnp.zeros_like(gcarry_ref)
            dbre_ref[...] = jnp.zeros_like(dbre_ref)
            dbim_ref[...] = jnp.zeros_like(dbim_ref)
            dcre_ref[...] = jnp.zeros_like(dcre_ref)
            dcim_ref[...] = jnp.zeros_like(dcim_ref)
            da_ref[...] = jnp.zeros_like(da_ref)

        ub = u_ref[...]
        dy = dys_ref[...]
        ar = a_ref[0, 0:1, :]
        ai = a_ref[0, 1:2, :]
        x0r = xb_ref[0, 0, 0:1, :]
        x0i = xb_ref[0, 0, 1:2, :]
        xr_ref[...] = _bdot(ub, bre_ref[0])
        xi_ref[...] = _bdot(ub, bim_ref[0])
        _s5_scan_rows(xr_ref, xi_ref, ar, ai, x0r, x0i, tl)
        dcre_ref[0] += _bdot(xr_ref[...], dy, TN)
        dcim_ref[0] -= _bdot(xi_ref[...], dy, TN)
        gr_ref[...] = _bdot(dy, cre_ref[0], NT)
        gi_ref[...] = -_bdot(dy, cim_ref[0], NT)

        def step(t8, carry):
            cr, ci = carry
            base = pl.multiple_of((tl // SUBLANES - 1 - t8) * SUBLANES, SUBLANES)
            for r in range(SUBLANES - 1, -1, -1):
                gr = gr_ref[pl.ds(base + r, 1), :] + cr
                gi = gi_ref[pl.ds(base + r, 1), :] + ci
                gr_ref[pl.ds(base + r, 1), :] = gr
                gi_ref[pl.ds(base + r, 1), :] = gi
                cr = ar * gr + ai * gi
                ci = ar * gi - ai * gr
            return cr, ci
        cr, ci = lax.fori_loop(0, tl // SUBLANES, step, (gcarry_ref[0:1, :], gcarry_ref[1:2, :]))
        gcarry_ref[0:1, :] = cr
        gcarry_ref[1:2, :] = ci
        row = lax.broadcasted_iota(jnp.int32, (tl, ns), 0)
        gr = gr_ref[...]
        gi = gi_ref[...]
        xpr = jnp.where(row == 0, x0r, pltpu.roll(xr_ref[...], 1, 0))
        xpi = jnp.where(row == 0, x0i, pltpu.roll(xi_ref[...], 1, 0))
        da_ref[0, 0:1, :] += jnp.sum(gr * xpr + gi * xpi, axis=0, keepdims=True)
        da_ref[0, 1:2, :] += jnp.sum(gi * xpr - gr * xpi, axis=0, keepdims=True)
        du_ref[...] = _bdot(gr, bre_ref[0], NT) + _bdot(gi, bim_ref[0], NT)
        dbre_ref[0] += _bdot(ub, gr, TN)
        dbim_ref[0] += _bdot(ub, gi, TN)

    rev = lambda c: nc - 1 - c
    return pl.pallas_call(
        body, name="s5_core_bwd", grid=(nb, nc),
        in_specs=[pl.BlockSpec((tl, LANES), lambda j, c: (rev(c), j)), pl.BlockSpec((tl, LANES), lambda j, c: (rev(c), j)),
                  pl.BlockSpec((1, LANES, ns), lambda j, c: (j, 0, 0)), pl.BlockSpec((1, LANES, ns), lambda j, c: (j, 0, 0)),
                  pl.BlockSpec((1, ns, LANES), lambda j, c: (j, 0, 0)), pl.BlockSpec((1, ns, LANES), lambda j, c: (j, 0, 0)),
                  pl.BlockSpec((1, SUBLANES, ns), lambda j, c: (j, 0, 0)),
                  pl.BlockSpec((1, 1, SUBLANES, ns), lambda j, c: (j, rev(c), 0, 0))],
        out_specs=[pl.BlockSpec((tl, LANES), lambda j, c: (rev(c), j)),
                   pl.BlockSpec((1, LANES, ns), lambda j, c: (j, 0, 0)), pl.BlockSpec((1, LANES, ns), lambda j, c: (j, 0, 0)),
                   pl.BlockSpec((1, ns, LANES), lambda j, c: (j, 0, 0)), pl.BlockSpec((1, ns, LANES), lambda j, c: (j, 0, 0)),
                   pl.BlockSpec((1, SUBLANES, ns), lambda j, c: (j, 0, 0))],
        out_shape=[SDS((L, e), F32), SDS(bre.shape, F32), SDS(bim.shape, F32), SDS(cre.shape, F32), SDS(cim.shape, F32),
                   SDS(a.shape, F32)],
        scratch_shapes=[pltpu.VMEM((tl, ns), F32) for _ in range(4)] + [pltpu.VMEM((SUBLANES, ns), F32)],
        compiler_params=_params(("arbitrary", "arbitrary"), VMEM_MID),
    )(u, dys, bre, bim, cre, cim, a, xb)


def make_s5_core(tl):
    @jax.custom_vjp
    def s5_core(u, bre, bim, cre, cim, a):
        return _s5_fwd_call(u, bre, bim, cre, cim, a, tl)[0]

    def fwd(u, bre, bim, cre, cim, a):
        ys, xb = _s5_fwd_call(u, bre, bim, cre, cim, a, tl)
        return ys, (u, bre, bim, cre, cim, a, xb)

    def bwd(res, dys):
        u, bre, bim, cre, cim, a, xb = res
        return tuple(_s5_bwd_call(u, dys, bre, bim, cre, cim, a, xb, tl))

    s5_core.defvjp(fwd, bwd)
    return s5_core


def _s5_block_params(lam_re, lam_im, log_dt, b_re, b_im, c_re, c_im):
    dt = jnp.exp(log_dt)[:, None]
    mag = jnp.exp(lam_re * dt)
    ab_re = mag * jnp.cos(lam_im * dt)
    ab_im = mag * jnp.sin(lam_im * dt)
    den = lam_re * lam_re + lam_im * lam_im
    nr = ab_re - 1.0
    ni = ab_im
    q_re = (nr * lam_re + ni * lam_im) / den
    q_im = (ni * lam_re - nr * lam_im) / den
    bb_re = q_re[..., None] * b_re - q_im[..., None] * b_im
    bb_im = q_re[..., None] * b_im + q_im[..., None] * b_re
    nb = S5_GROUPS // S5_GB
    eye = jnp.eye(S5_GB, dtype=F32)

    def bdiag_in(bb):
        t = bb.reshape(nb, S5_GB, S5_STATE, S5_GROUP)
        t = jnp.einsum("jgpm,gh->jgmhp", t, eye)
        return t.reshape(nb, S5_GB * S5_GROUP, S5_GB * S5_STATE)

    def bdiag_out(cc):
        t = cc.reshape(nb, S5_GB, S5_GROUP, S5_STATE)
        t = jnp.einsum("jgmp,gh->jgphm", t, eye)
        return t.reshape(nb, S5_GB * S5_STATE, S5_GB * S5_GROUP)

    a = jnp.stack([ab_re.reshape(nb, S5_GB * S5_STATE), ab_im.reshape(nb, S5_GB * S5_STATE)], axis=1)
    a = jnp.concatenate([a, jnp.zeros((nb, SUBLANES - 2, S5_GB * S5_STATE), F32)], axis=1)
    return bdiag_in(bb_re), bdiag_in(bb_im), bdiag_out(c_re), bdiag_out(c_im), a


def _shift_down(x, s, row):
    if s == 0:
        return x
    return jnp.where(row >= s, pltpu.roll(x, s, 0), 0.0)


def _shift_up(x, s, row, n):
    if s == 0:
        return x
    return jnp.where(row < n - s, pltpu.roll(x, n - s, 0), 0.0)


def _conv_fwd_call(x, w):
    L, ch = x.shape

    def body(x_ref, w_ref, y_ref):
        xv = x_ref[...]
        row = lax.broadcasted_iota(jnp.int32, xv.shape, 0)
        acc = jnp.zeros_like(xv)
        for j in range(GDN_CONV):
            acc += w_ref[j:j + 1, :] * _shift_down(xv, GDN_CONV - 1 - j, row)
        y_ref[...] = acc

    return pl.pallas_call(
        body, name="gdn_conv_fwd", grid=(ch // LANES,),
        in_specs=[pl.BlockSpec((L, LANES), lambda j: (0, j)), pl.BlockSpec((SUBLANES, LANES), lambda j: (0, j))],
        out_specs=pl.BlockSpec((L, LANES), lambda j: (0, j)), out_shape=SDS((L, ch), F32),
        compiler_params=_params(("parallel",), VMEM_MID),
    )(x, w)


def _conv_bwd_call(x, w, dy):
    L, ch = x.shape

    def body(x_ref, w_ref, dy_ref, dx_ref, dw_ref):
        xv = x_ref[...]
        g = dy_ref[...]
        row = lax.broadcasted_iota(jnp.int32, xv.shape, 0)
        acc = jnp.zeros_like(xv)
        dws = []
        for j in range(GDN_CONV):
            s = GDN_CONV - 1 - j
            acc += w_ref[j:j + 1, :] * _shift_up(g, s, row, L)
            dws.append(jnp.sum(g * _shift_down(xv, s, row), axis=0, keepdims=True))
        dx_ref[...] = acc
        dw_ref[...] = jnp.concatenate(dws + [jnp.zeros((SUBLANES - GDN_CONV, LANES), F32)], axis=0)

    return pl.pallas_call(
        body, name="gdn_conv_bwd", grid=(ch // LANES,),
        in_specs=[pl.BlockSpec((L, LANES), lambda j: (0, j)), pl.BlockSpec((SUBLANES, LANES), lambda j: (0, j)),
                  pl.BlockSpec((L, LANES), lambda j: (0, j))],
        out_specs=[pl.BlockSpec((L, LANES), lambda j: (0, j)), pl.BlockSpec((SUBLANES, LANES), lambda j: (0, j))],
        out_shape=[SDS((L, ch), F32), SDS((SUBLANES, ch), F32)],
        compiler_params=_params(("parallel",), VMEM_MID),
    )(x, w, dy)


@jax.custom_vjp
def gdn_conv(x, w):
    return _conv_fwd_call(x, w)


def _gdn_conv_f(x, w):
    return _conv_fwd_call(x, w), (x, w)


def _gdn_conv_b(res, dy):
    x, w = res
    return tuple(_conv_bwd_call(x, w, dy))


gdn_conv.defvjp(_gdn_conv_f, _gdn_conv_b)


def _gdn_prep_math(q, k, v, beta, g):
    C = q.shape[0]
    ri = lax.broadcasted_iota(jnp.int32, (C, C), 0)
    ci = lax.broadcasted_iota(jnp.int32, (C, C), 1)
    causal = ri >= ci
    strict = ri > ci
    eye = (ri == ci).astype(F32)
    gb = jnp.broadcast_to(g, (C, C))
    g_row = jnp.sum(gb * eye, axis=0, keepdims=True)
    gc_col = jnp.sum(jnp.where(causal, jnp.broadcast_to(g_row, (C, C)), 0.0), axis=1, keepdims=True)
    gc_row = jnp.sum(jnp.where(ri <= ci, gb, 0.0), axis=0, keepdims=True)
    decay = jnp.exp(jnp.where(causal, gc_col - gc_row, -jnp.inf))
    kk = _bdot(k, k, NT)
    a_mat = jnp.where(strict, beta * kk * decay, 0.0)
    n = -a_mat
    t = eye + n
    for _ in range(int(math.log2(C)) - 1):
        n = _hdot(n, n)
        t = t + _hdot(t, n)
    e_gc = jnp.exp(gc_col)
    w = _hdot(t, beta * e_gc * k)
    u = _hdot(t, beta * v)
    qk = _bdot(q, k, NT) * decay
    q_dec = q * e_gc
    g_last = gc_col[C - 1:C, :]
    k_dec = k * jnp.exp(g_last - gc_col)
    return q_dec, w, u, qk, k_dec, gc_col


def _gdn_prep_specs(L):
    C = GDN_CHUNK
    ins = [pl.BlockSpec((C, GDN_DK), lambda h, c: (c, h)), pl.BlockSpec((C, GDN_DK), lambda h, c: (c, h)),
           pl.BlockSpec((C, GDN_DV), lambda h, c: (c, h)),
           pl.BlockSpec((1, C, 1), lambda h, c: (h, c, 0)), pl.BlockSpec((1, C, 1), lambda h, c: (h, c, 0))]
    outs = [pl.BlockSpec((1, C, GDN_DK), lambda h, c: (h, c, 0)), pl.BlockSpec((1, C, GDN_DK), lambda h, c: (h, c, 0)),
            pl.BlockSpec((1, C, GDN_DV), lambda h, c: (h, c, 0)), pl.BlockSpec((1, C, C), lambda h, c: (h, c, 0)),
            pl.BlockSpec((1, C, GDN_DK), lambda h, c: (h, c, 0)), pl.BlockSpec((1, C, 1), lambda h, c: (h, c, 0))]
    shapes = [SDS((GDN_HEADS, L, GDN_DK), F32), SDS((GDN_HEADS, L, GDN_DK), F32), SDS((GDN_HEADS, L, GDN_DV), F32),
              SDS((GDN_HEADS, L, C), F32), SDS((GDN_HEADS, L, GDN_DK), F32), SDS((GDN_HEADS, L, 1), F32)]
    return ins, outs, shapes


def _gdn_prep_fwd_call(q, k, v, beta, g):
    L = q.shape[0]
    ins, outs, shapes = _gdn_prep_specs(L)

    def body(q_ref, k_ref, v_ref, b_ref, g_ref, *o_refs):
        res = _gdn_prep_math(q_ref[...], k_ref[...], v_ref[...], b_ref[0], g_ref[0])
        for o_ref, val in zip(o_refs, res):
            o_ref[0] = val

    return pl.pallas_call(
        body, name="gdn_prep_fwd", grid=(GDN_HEADS, L // GDN_CHUNK), in_specs=ins, out_specs=outs, out_shape=shapes,
        compiler_params=_params(("parallel", "parallel")),
    )(q, k, v, beta, g)


def _gdn_prep_bwd_call(q, k, v, beta, g, cts):
    L = q.shape[0]
    ins, outs, _ = _gdn_prep_specs(L)

    def body(q_ref, k_ref, v_ref, b_ref, g_ref, c0, c1, c2, c3, c4, c5, dq_ref, dk_ref, dv_ref, db_ref, dg_ref):
        _, vjp = jax.vjp(_gdn_prep_math, q_ref[...], k_ref[...], v_ref[...], b_ref[0], g_ref[0])
        dq, dk, dv, db, dg = vjp((c0[0], c1[0], c2[0], c3[0], c4[0], c5[0]))
        dq_ref[...] = dq
        dk_ref[...] = dk
        dv_ref[...] = dv
        db_ref[0] = db
        dg_ref[0] = dg

    return pl.pallas_call(
        body, name="gdn_prep_bwd", grid=(GDN_HEADS, L // GDN_CHUNK), in_specs=ins + outs, out_specs=ins,
        out_shape=[SDS(q.shape, F32), SDS(k.shape, F32), SDS(v.shape, F32), SDS(beta.shape, F32), SDS(g.shape, F32)],
        compiler_params=_params(("parallel", "parallel")),
    )(q, k, v, beta, g, *cts)


@jax.custom_vjp
def gdn_prep(q, k, v, beta, g):
    return tuple(_gdn_prep_fwd_call(q, k, v, beta, g))


def _gdn_prep_f(q, k, v, beta, g):
    return tuple(_gdn_prep_fwd_call(q, k, v, beta, g)), (q, k, v, beta, g)


def _gdn_prep_b(res, cts):
    return tuple(_gdn_prep_bwd_call(*res, tuple(cts)))


gdn_prep.defvjp(_gdn_prep_f, _gdn_prep_b)


def _gdn_step_math(q_dec, w, u, qk, k_dec, gc, state):
    C = q_dec.shape[0]
    v_new = u - _bdot(w, state)
    o = _bdot(q_dec, state) + _bdot(qk, v_new)
    gl = gc[C - 1:C, :]
    new_state = jnp.exp(gl) * state + _bdot(k_dec, v_new, TN)
    return o, new_state


def _gdn_scan_specs(L, rev):
    C = GDN_CHUNK
    nc = L // C
    cc = (lambda c: nc - 1 - c) if rev else (lambda c: c)
    ins = [pl.BlockSpec((1, C, GDN_DK), lambda h, c: (h, cc(c), 0)), pl.BlockSpec((1, C, GDN_DK), lambda h, c: (h, cc(c), 0)),
           pl.BlockSpec((1, C, GDN_DV), lambda h, c: (h, cc(c), 0)), pl.BlockSpec((1, C, C), lambda h, c: (h, cc(c), 0)),
           pl.BlockSpec((1, C, GDN_DK), lambda h, c: (h, cc(c), 0)), pl.BlockSpec((1, C, 1), lambda h, c: (h, cc(c), 0))]
    o_spec = pl.BlockSpec((C, GDN_DV), lambda h, c: (cc(c), h))
    s_spec = pl.BlockSpec((1, 1, GDN_DK, GDN_DV), lambda h, c: (h, cc(c), 0, 0))
    return ins, o_spec, s_spec, nc


def _gdn_scan_fwd_call(q_dec, w, u, qk, k_dec, gc):
    L = q_dec.shape[1]
    ins, o_spec, s_spec, nc = _gdn_scan_specs(L, False)

    def body(qd_ref, w_ref, u_ref, qk_ref, kd_ref, gc_ref, o_ref, sin_ref, s_ref):
        c = pl.program_id(1)

        @pl.when(c == 0)
        def _():
            s_ref[...] = jnp.zeros_like(s_ref)
        st = s_ref[...]
        sin_ref[0, 0] = st
        o, ns = _gdn_step_math(qd_ref[0], w_ref[0], u_ref[0], qk_ref[0], kd_ref[0], gc_ref[0], st)
        o_ref[...] = o
        s_ref[...] = ns

    return pl.pallas_call(
        body, name="gdn_scan_fwd", grid=(GDN_HEADS, nc), in_specs=ins, out_specs=[o_spec, s_spec],
        out_shape=[SDS((L, GDN_HEADS * GDN_DV), F32), SDS((GDN_HEADS, nc, GDN_DK, GDN_DV), F32)],
        scratch_shapes=[pltpu.VMEM((GDN_DK, GDN_DV), F32)],
        compiler_params=_params(("arbitrary", "arbitrary")),
    )(q_dec, w, u, qk, k_dec, gc)


def _gdn_scan_bwd_call(q_dec, w, u, qk, k_dec, gc, s_in, do):
    L = q_dec.shape[1]
    ins, o_spec, s_spec, nc = _gdn_scan_specs(L, True)

    def body(qd_ref, w_ref, u_ref, qk_ref, kd_ref, gc_ref, sin_ref, do_ref,
             dqd_ref, dw_ref, du_ref, dqk_ref, dkd_ref, dgc_ref, ds_ref):
        c = pl.program_id(1)

        @pl.when(c == 0)
        def _():
            ds_ref[...] = jnp.zeros_like(ds_ref)
        _, vjp = jax.vjp(_gdn_step_math, qd_ref[0], w_ref[0], u_ref[0], qk_ref[0], kd_ref[0], gc_ref[0], sin_ref[0, 0])
        dqd, dw, du, dqk, dkd, dgc, dst = vjp((do_ref[...], ds_ref[...]))
        dqd_ref[0] = dqd
        dw_ref[0] = dw
        du_ref[0] = du
        dqk_ref[0] = dqk
        dkd_ref[0] = dkd
        dgc_ref[0] = dgc
        ds_ref[...] = dst

    return pl.pallas_call(
        body, name="gdn_scan_bwd", grid=(GDN_HEADS, nc), in_specs=ins + [s_spec, o_spec], out_specs=ins,
        out_shape=[SDS(t.shape, F32) for t in (q_dec, w, u, qk, k_dec, gc)],
        scratch_shapes=[pltpu.VMEM((GDN_DK, GDN_DV), F32)],
        compiler_params=_params(("arbitrary", "arbitrary")),
    )(q_dec, w, u, qk, k_dec, gc, s_in, do)


@jax.custom_vjp
def gdn_scan(q_dec, w, u, qk, k_dec, gc):
    return _gdn_scan_fwd_call(q_dec, w, u, qk, k_dec, gc)[0]


def _gdn_scan_f(*args):
    o, s_in = _gdn_scan_fwd_call(*args)
    return o, (*args, s_in)


def _gdn_scan_b(res, do):
    return tuple(_gdn_scan_bwd_call(*res, do))


gdn_scan.defvjp(_gdn_scan_f, _gdn_scan_b)


def _silu(x):
    return x * jax.nn.sigmoid(x)


def _gelu_tanh(x):
    return 0.5 * x * (1.0 + jnp.tanh(math.sqrt(2.0 / math.pi) * (x + 0.044715 * (x * x * x))))


def _f_lnmod(x, nw, sc, sh, bsc, bsh):
    xn = x * lax.rsqrt(jnp.mean(x * x, axis=-1, keepdims=True) + NORM_EPS) * nw
    return (xn * (1.0 + (sc + bsc)) + (sh + bsh),)


def _f_s5_act(ys, u, d):
    return (_gelu_tanh(ys + d * u),)


def _f_s5_gate(y2, t, z):
    return (y2 * jax.nn.sigmoid(t) * _silu(z),)


def _f_res(x, y, gate, bgate):
    return (x + (gate + bgate) * y,)


def _heads(x, width, fn):
    return jnp.concatenate([fn(x[:, i * width:(i + 1) * width]) for i in range(x.shape[1] // width)], axis=1)


def _l2n(x):
    return x * lax.rsqrt(jnp.sum(x * x, axis=-1, keepdims=True) + NORM_EPS)


def _f_qnorm(x):
    return (_heads(_silu(x), GDN_DK, _l2n) * (GDN_DK ** -0.5),)


def _f_knorm(x):
    return (_heads(_silu(x), GDN_DK, _l2n),)


def _f_vact(x):
    return (_silu(x),)


def _f_betag(ba, alog, dtb):
    col = lax.broadcasted_iota(jnp.int32, ba.shape, 1)
    t = ba + dtb
    softplus = jnp.maximum(t, 0.0) + jnp.log1p(jnp.exp(-jnp.abs(t)))
    g = -jnp.exp(alog) * softplus
    return (jnp.where(col < GDN_HEADS, jax.nn.sigmoid(ba), jnp.where(col < 2 * GDN_HEADS, g, 0.0)),)


def _f_gdn_post(o, z, nw):
    on = _heads(o, GDN_DV, lambda t: t * lax.rsqrt(jnp.mean(t * t, axis=-1, keepdims=True) + NORM_EPS))
    return (on * nw * _silu(z),)


def _f_loss(x, tgt, fw):
    y = x * lax.rsqrt(jnp.mean(x * x, axis=-1, keepdims=True) + NORM_EPS) * fw
    err = y - tgt
    return (0.5 * jnp.mean(err * err, axis=-1, keepdims=True),)


def _ada_mod_call(c_all, ada_w):
    n = ada_w.shape[2]

    def body(c_ref, w_ref, o_ref):
        ca = _silu(c_ref[...])
        for l in range(ada_w.shape[0]):
            o_ref[l] = _bdot(ca, w_ref[l])

    return pl.pallas_call(body, name="ada_mod", out_shape=SDS((ada_w.shape[0], N_DEV, n), F32),
                          compiler_params=_params(None, VMEM_MID))(c_all, ada_w)


def _ada_grad_call(c_all, dmod):
    nl, _, n = dmod.shape

    def body(c_ref, d_ref, o_ref):
        ca = _silu(c_ref[...])
        for l in range(nl):
            o_ref[l] = _hdot(ca, d_ref[l], TN)

    return pl.pallas_call(body, name="ada_grad", out_shape=SDS((nl, c_all.shape[1], n), F32),
                          compiler_params=_params(None, VMEM_MID))(c_all, dmod)


ADAM_ROWS = 512


def _adam_call(gs, w, m, v, name):
    n, r, _ = gs.shape

    def body(g_ref, w_ref, m_ref, v_ref, go_ref, d_ref, mo_ref, vo_ref):
        g = g_ref[0]
        for s in range(1, n):
            g = g + g_ref[s]
        m2 = ADAM_B1 * m_ref[...] + (1.0 - ADAM_B1) * g
        v2 = ADAM_B2 * v_ref[...] + (1.0 - ADAM_B2) * (g * g)
        m_hat = m2 / (1.0 - ADAM_B1 ** ADAM_STEP)
        v_hat = v2 / (1.0 - ADAM_B2 ** ADAM_STEP)
        go_ref[...] = g
        d_ref[...] = -ADAM_LR * (m_hat / (jnp.sqrt(v_hat) + ADAM_EPS) + ADAM_WD * w_ref[...])
        mo_ref[...] = m2
        vo_ref[...] = v2

    blk = pl.BlockSpec((ADAM_ROWS, LANES), lambda i: (i, 0))
    return pl.pallas_call(
        body, name=name, grid=(r // ADAM_ROWS,),
        in_specs=[pl.BlockSpec((n, ADAM_ROWS, LANES), lambda i: (0, i, 0)), blk, blk, blk],
        out_specs=[blk, blk, blk, blk], out_shape=[SDS((r, LANES), F32)] * 4,
        compiler_params=_params(("parallel",), VMEM_MID),
    )(gs, w, m, v)


def _sum_call(gs, name):
    n, r, _ = gs.shape

    def body(g_ref, o_ref):
        g = g_ref[0]
        for s in range(1, n):
            g = g + g_ref[s]
        o_ref[...] = g

    return pl.pallas_call(
        body, name=name, grid=(r // ADAM_ROWS,),
        in_specs=[pl.BlockSpec((n, ADAM_ROWS, LANES), lambda i: (0, i, 0))],
        out_specs=pl.BlockSpec((ADAM_ROWS, LANES), lambda i: (i, 0)), out_shape=SDS((r, LANES), F32),
        compiler_params=_params(("parallel",), VMEM_MID),
    )(gs)


def _allgather_call(x_shard, name, in_hbm):
    m_per, n = x_shard.shape

    def body(x_ref, out_ref, send_sems, recv_sems, local_sem):
        x, y, c = lax.axis_index("x"), lax.axis_index("y"), lax.axis_index("c")
        me, sibling = (x, y, c), (x, y, 1 - c)
        chips = [(1 - x, y), (x, 1 - y), (1 - x, 1 - y)]

        def rows(px, py, pc):
            return out_ref.at[pl.ds((4 * px + 2 * py + pc) * m_per, m_per), :]

        def copy(k, block, to, src=None):
            return pltpu.make_async_remote_copy(
                src_ref=rows(*block) if src is None else src, dst_ref=rows(*block),
                send_sem=send_sems.at[k], recv_sem=recv_sems.at[k], device_id=to, device_id_type=pl.DeviceIdType.MESH)

        mine = pltpu.make_async_copy(x_ref, rows(*me), local_sem)
        mine.start()
        first = [copy(0, me, sibling, src=x_ref)]
        first += [copy(1 + j, me, (*chip, c), src=x_ref) for j, chip in enumerate(chips)]
        for cp in first:
            cp.start()
        passed = [copy(4 + j, (*chip, c), sibling) for j, chip in enumerate(chips)]
        for j, chip in enumerate(chips):
            copy(1 + j, (*chip, c), me).wait_recv()
            passed[j].start()
        copy(0, sibling, me).wait_recv()
        for j, chip in enumerate(chips):
            copy(4 + j, (*chip, 1 - c), me).wait_recv()
        for cp in first + passed:
            cp.wait_send()
        mine.wait()

    space = pl.ANY if in_hbm else pltpu.VMEM
    return pl.pallas_call(
        body, name=name, out_shape=SDS((N_DEV * m_per, n), x_shard.dtype),
        in_specs=[pl.BlockSpec(memory_space=space)], out_specs=pl.BlockSpec(memory_space=space),
        scratch_shapes=[pltpu.SemaphoreType.DMA((7,)), pltpu.SemaphoreType.DMA((7,)), pltpu.SemaphoreType.DMA],
        compiler_params=_params(None, None if in_hbm else VMEM_BIG),
    )(x_shard)


def _alltoall_call(g, name):
    def body(g_ref, out_ref, send_sems, recv_sems, local_sem):
        x, y, c = lax.axis_index("x"), lax.axis_index("y"), lax.axis_index("c")
        me = 4 * x + 2 * y + c
        mine = pltpu.make_async_copy(g_ref.at[me], out_ref.at[me], local_sem)
        mine.start()
        copies = []
        for k in range(1, N_DEV):
            px = 1 - x if k & 4 else x
            py = 1 - y if k & 2 else y
            pc = 1 - c if k & 1 else c
            cp = pltpu.make_async_remote_copy(
                src_ref=g_ref.at[4 * px + 2 * py + pc], dst_ref=out_ref.at[me],
                send_sem=send_sems.at[k - 1], recv_sem=recv_sems.at[k - 1],
                device_id=(px, py, pc), device_id_type=pl.DeviceIdType.MESH)
            cp.start()
            copies.append(cp)
        for cp in copies:
            cp.wait()
        mine.wait()

    return pl.pallas_call(
        body, name=name, out_shape=SDS(g.shape, g.dtype),
        in_specs=[pl.BlockSpec(memory_space=pl.ANY)], out_specs=pl.BlockSpec(memory_space=pl.ANY),
        scratch_shapes=[pltpu.SemaphoreType.DMA((N_DEV - 1,)), pltpu.SemaphoreType.DMA((N_DEV - 1,)), pltpu.SemaphoreType.DMA],
    )(g)


def _pack(parts, rows_multiple):
    flat = jnp.concatenate([p.reshape(-1) for p in parts])
    unit = rows_multiple * LANES
    padded = -(-flat.shape[0] // unit) * unit
    flat = jnp.concatenate([flat, jnp.zeros((padded - flat.shape[0],), F32)])
    return flat.reshape(-1, LANES)


def _unpack(buf, shapes):
    flat = buf.reshape(-1)
    out, off = [], 0
    for s in shapes:
        n = math.prod(s)
        out.append(flat[off:off + n].reshape(s))
        off += n
    return out


def _cols_to_devs(w):
    k, n8 = w.shape
    return w.reshape(k, N_DEV, n8 // N_DEV).transpose(1, 0, 2).reshape(N_DEV, -1)


def _devs_to_cols(flat, k):
    n = flat.shape[1] // k
    return flat.reshape(N_DEV, k, n).transpose(1, 0, 2).reshape(k, N_DEV * n)


def _local_loss(diff, const, L):
    (x, mod_raw, norm_w, lam_re, lam_im, log_dt, b_re, b_im, c_re, c_im, s5_d, conv_w, a_log, dt_bias, gdn_nw, final_nw,
     w_u, w_z, w_glu, w_o5, w_q, w_k, w_v, w_gz, w_ba, w_og) = diff
    tgt, ada_b = const
    tm = 256 if L % 256 == 0 else L
    mods = mod_raw.reshape(2, 3, 1, D_MODEL)
    biases = ada_b.reshape(2, 3, 1, D_MODEL)

    op_ln0 = make_rowwise(_f_lnmod, "ln0", tm, 1, 5)
    (h,) = op_ln0((x,), (norm_w[0:1], mods[0, 1], mods[0, 0], biases[0, 1], biases[0, 0]))
    u = make_mm("s5_in_u")(h, w_u)
    z = make_mm("s5_in_z")(h, w_z)
    blocks = _s5_block_params(lam_re, lam_im, log_dt, b_re, b_im, c_re, c_im)
    ys = make_s5_core(min(S5_TL, L))(u, *blocks)
    (y2,) = make_rowwise(_f_s5_act, "s5_act", tm, 2, 1)((ys, u), (s5_d.reshape(1, D_INNER),))
    t = make_mm("s5_glu")(y2, w_glu)
    (y4,) = make_rowwise(_f_s5_gate, "s5_gate", tm, 3, 0)((y2, t, z), ())
    o = make_mm("s5_out")(y4, w_o5)
    (x1,) = make_rowwise(_f_res, "res0", tm, 2, 2)((x, o), (mods[0, 2], biases[0, 2]))

    op_ln1 = make_rowwise(_f_lnmod, "ln1", tm, 1, 5)
    (h,) = op_ln1((x1,), (norm_w[1:2], mods[1, 1], mods[1, 0], biases[1, 1], biases[1, 0]))
    q0 = make_mm("gdn_in_q")(h, w_q)
    k0 = make_mm("gdn_in_k")(h, w_k)
    v0 = make_mm("gdn_in_v")(h, w_v)
    gz = make_mm("gdn_in_z")(h, w_gz)
    ba = make_mm("gdn_in_ba")(h, w_ba)
    cw = jnp.concatenate([conv_w, jnp.zeros((SUBLANES - GDN_CONV, GDN_CONV_CH), F32)], axis=0)
    (q,) = make_rowwise(_f_qnorm, "gdn_qn", tm, 1, 0)((gdn_conv(q0, cw[:, :GDN_QK]),), ())
    (k,) = make_rowwise(_f_knorm, "gdn_kn", tm, 1, 0)((gdn_conv(k0, cw[:, GDN_QK:2 * GDN_QK]),), ())
    (v,) = make_rowwise(_f_vact, "gdn_va", tm, 1, 0)((gdn_conv(v0, cw[:, 2 * GDN_QK:]),), ())
    pad = jnp.zeros((LANES - 2 * GDN_HEADS,), F32)
    alog_row = jnp.concatenate([jnp.zeros((GDN_HEADS,), F32), a_log, pad]).reshape(1, LANES)
    dtb_row = jnp.concatenate([jnp.zeros((GDN_HEADS,), F32), dt_bias, pad]).reshape(1, LANES)
    (bg,) = make_rowwise(_f_betag, "gdn_bg", tm, 1, 2)((ba,), (alog_row, dtb_row))
    beta = bg[:, :GDN_HEADS].T.reshape(GDN_HEADS, L, 1)
    g = bg[:, GDN_HEADS:2 * GDN_HEADS].T.reshape(GDN_HEADS, L, 1)
    og = gdn_scan(*gdn_prep(q, k, v, beta, g))
    nw_row = jnp.tile(gdn_nw, GDN_HEADS).reshape(1, D_INNER)
    (on,) = make_rowwise(_f_gdn_post, "gdn_post", tm, 2, 1)((og, gz), (nw_row,))
    y = make_mm("gdn_out")(on, w_og)
    (x2,) = make_rowwise(_f_res, "res1", tm, 2, 2)((x1, y), (mods[1, 2], biases[1, 2]))

    (lt,) = make_rowwise(_f_loss, "loss", tm, 2, 1)((x2, tgt), (final_nw.reshape(1, D_MODEL),))
    return jnp.sum(lt)


BIG_ROWS = 512
SMALL_NAMES = ("ada_b", "norm_w", "s5_lambda_re", "s5_lambda_im", "s5_log_dt", "s5_b_re", "s5_b_im", "s5_c_re", "s5_c_im",
               "s5_d", "gdn_a_log", "gdn_dt_bias", "final_norm_w")
BIG_NAMES = ("s5_w_in", "s5_w_glu", "s5_w_out", "gdn_w_in", "gdn_w_out")
WEIGHT_ORDER = ("ada_w", "ada_b", "norm_w", "s5_w_in", "s5_lambda_re", "s5_lambda_im", "s5_log_dt", "s5_b_re", "s5_b_im",
                "s5_c_re", "s5_c_im", "s5_d", "s5_w_glu", "s5_w_out", "gdn_w_in", "gdn_conv_w", "gdn_a_log", "gdn_dt_bias",
                "gdn_norm_w", "gdn_w_out", "final_norm_w")


def _step(x, c, W, M, V, tgt):
    L = x.shape[1]
    ix, iy, ic = lax.axis_index("x"), lax.axis_index("y"), lax.axis_index("c")
    me = 4 * ix + 2 * iy + ic
    n_ada = W["ada_w"].shape[2]
    n_conv = W["gdn_conv_w"].shape[2]
    n_gnw = W["gdn_norm_w"].shape[1]

    g1 = _allgather_call(_pack([c, W["gdn_conv_w"], W["gdn_norm_w"]], SUBLANES), "gather_small_in", False)
    g1 = g1.reshape(N_DEV, -1)
    c_all = g1[:, :D_MODEL]
    conv_w = g1[:, D_MODEL:D_MODEL + GDN_CONV * n_conv].reshape(N_DEV, GDN_CONV, n_conv).transpose(1, 0, 2).reshape(GDN_CONV, -1)
    gdn_nw = g1[:, D_MODEL + GDN_CONV * n_conv:D_MODEL + GDN_CONV * n_conv + n_gnw].reshape(-1)
    mod_part = _ada_mod_call(c_all, W["ada_w"])
    g2 = _allgather_call(_pack([mod_part], SUBLANES), "gather_mod", False).reshape(N_DEV, -1)
    mod_all = g2[:, :2 * N_DEV * n_ada].reshape(N_DEV, 2, N_DEV, n_ada)
    mod_raw = lax.dynamic_index_in_dim(mod_all, me, axis=2, keepdims=False)
    mod_raw = mod_raw.transpose(1, 0, 2).reshape(2, 3 * D_MODEL)

    big_shapes = [W[n].shape[1:] for n in BIG_NAMES]
    wg = _allgather_call(_pack([W[n] for n in BIG_NAMES], BIG_ROWS), "gather_weights", True)
    wg = wg.reshape(N_DEV, -1)
    full, off = {}, 0
    for name, s in zip(BIG_NAMES, big_shapes):
        n = math.prod(s)
        piece = wg[:, off:off + n]
        off += n
        full[name] = _devs_to_cols(piece, s[0]) if name in ("s5_w_in", "gdn_w_in") else piece.reshape(N_DEV * s[0], s[1])
    w_in5, w_ing = full["s5_w_in"], full["gdn_w_in"]
    w_ba = jnp.concatenate([w_ing[:, GDN_CONV_CH + D_INNER:], jnp.zeros((D_MODEL, LANES - 2 * GDN_HEADS), F32)], axis=1)
    diff = (x[0], mod_raw, W["norm_w"], W["s5_lambda_re"][0], W["s5_lambda_im"][0], W["s5_log_dt"][0], W["s5_b_re"][0],
            W["s5_b_im"][0], W["s5_c_re"][0], W["s5_c_im"][0], W["s5_d"][0], conv_w, W["gdn_a_log"][0], W["gdn_dt_bias"][0],
            gdn_nw, W["final_norm_w"],
            w_in5[:, :D_INNER], w_in5[:, D_INNER:], full["s5_w_glu"], full["s5_w_out"],
            w_ing[:, :GDN_QK], w_ing[:, GDN_QK:2 * GDN_QK], w_ing[:, 2 * GDN_QK:GDN_CONV_CH],
            w_ing[:, GDN_CONV_CH:GDN_CONV_CH + D_INNER], w_ba, full["gdn_w_out"])

    loss_local, grads = jax.value_and_grad(_local_loss)(diff, (tgt[0], W["ada_b"]), L)
    (dx, dmod, d_norm_w, d_lre, d_lim, d_logdt, d_bre, d_bim, d_cre, d_cim, d_s5d, d_conv, d_alog, d_dtb, d_gnw, d_fnw,
     d_wu, d_wz, d_wglu, d_wo5, d_wq, d_wk, d_wv, d_wgz, d_wba, d_wog) = grads
    loss = lax.psum(loss_local, MESH_AXES)

    d_in5 = _cols_to_devs(jnp.concatenate([d_wu, d_wz], axis=1))
    d_ing = _cols_to_devs(jnp.concatenate([d_wq, d_wk, d_wv, d_wgz, d_wba[:, :2 * GDN_HEADS]], axis=1))
    per_dev = jnp.concatenate([d_in5, d_wglu.reshape(N_DEV, -1), d_wo5.reshape(N_DEV, -1), d_ing, d_wog.reshape(N_DEV, -1)], axis=1)
    n_big = per_dev.shape[1]
    r_big = -(-n_big // (BIG_ROWS * LANES)) * BIG_ROWS
    per_dev = jnp.concatenate([per_dev, jnp.zeros((N_DEV, r_big * LANES - n_big), F32)], axis=1).reshape(N_DEV, r_big, LANES)
    recv = _alltoall_call(per_dev, "scatter_grads")
    big = _adam_call(recv, _pack([W[n] for n in BIG_NAMES], BIG_ROWS), _pack([M[n] for n in BIG_NAMES], BIG_ROWS),
                     _pack([V[n] for n in BIG_NAMES], BIG_ROWS), "adam_big")
    big = [_unpack(b, [W[n].shape for n in BIG_NAMES]) for b in big]

    small_parts = [dmod, d_norm_w, d_lre, d_lim, d_logdt, d_bre, d_bim, d_cre, d_cim, d_s5d, d_alog, d_dtb, d_fnw, d_conv, d_gnw]
    small_shapes = [p.shape for p in small_parts]
    sg = _allgather_call(_pack(small_parts, ADAM_ROWS), "gather_small_grads", False)
    sg = sg.reshape(N_DEV, -1, LANES)
    tot = _unpack(_sum_call(sg, "sum_small_grads"), small_shapes)
    (g_adab, g_norm_w, g_lre, g_lim, g_logdt, g_bre, g_bim, g_cre, g_cim, g_s5d, g_alog, g_dtb, g_fnw, g_conv, g_gnw) = tot
    g_conv_mine = lax.dynamic_slice_in_dim(g_conv, me * n_conv, n_conv, axis=1)
    g_gnw_mine = lax.dynamic_slice_in_dim(g_gnw, me * n_gnw, n_gnw, axis=0)
    small_g = {"ada_b": g_adab.reshape(W["ada_b"].shape), "norm_w": g_norm_w, "s5_lambda_re": g_lre[None], "s5_lambda_im": g_lim[None],
               "s5_log_dt": g_logdt[None], "s5_b_re": g_bre[None], "s5_b_im": g_bim[None], "s5_c_re": g_cre[None],
               "s5_c_im": g_cim[None], "s5_d": g_s5d[None], "gdn_a_log": g_alog[None], "gdn_dt_bias": g_dtb[None],
               "final_norm_w": g_fnw, "gdn_conv_w": g_conv_mine[None], "gdn_norm_w": g_gnw_mine[None]}
    small_names = SMALL_NAMES + ("gdn_conv_w", "gdn_norm_w")
    small = _adam_call(_pack([small_g[n] for n in small_names], ADAM_ROWS)[None], _pack([W[n] for n in small_names], ADAM_ROWS),
                       _pack([M[n] for n in small_names], ADAM_ROWS), _pack([V[n] for n in small_names], ADAM_ROWS), "adam_small")
    small = [_unpack(b, [W[n].shape for n in small_names]) for b in small]

    dmod_all = sg.reshape(N_DEV, -1)[:, :2 * 3 * D_MODEL].reshape(N_DEV, 2, N_DEV, n_ada)
    dmod_mine = lax.dynamic_index_in_dim(dmod_all, me, axis=2, keepdims=False).transpose(1, 0, 2)
    g_ada_w = _ada_grad_call(c_all, dmod_mine)
    ada = _adam_call(g_ada_w.reshape(1, -1, LANES), W["ada_w"].reshape(-1, LANES), M["ada_w"].reshape(-1, LANES),
                     V["ada_w"].reshape(-1, LANES), "adam_ada")
    ada = [a.reshape(W["ada_w"].shape) for a in ada]

    res = {}
    for i, n in enumerate(BIG_NAMES):
        res[n] = [b[i] for b in big]
    for i, n in enumerate(small_names):
        res[n] = [b[i] for b in small]
    res["ada_w"] = ada
    outs = [loss, dx[None]]
    for j in range(4):
        outs += [res[n][j] for n in WEIGHT_ORDER]
    return tuple(outs)


def kernel(x, c, ada_w, ada_b, norm_w, s5_w_in, s5_lambda_re, s5_lambda_im, s5_log_dt, s5_b_re, s5_b_im, s5_c_re, s5_c_im, s5_d, s5_w_glu, s5_w_out, gdn_w_in, gdn_conv_w, gdn_a_log, gdn_dt_bias, gdn_norm_w, gdn_w_out, final_norm_w, loss_target, m_ada_w, m_ada_b, m_norm_w, m_s5_w_in, m_s5_lambda_re, m_s5_lambda_im, m_s5_log_dt, m_s5_b_re, m_s5_b_im, m_s5_c_re, m_s5_c_im, m_s5_d, m_s5_w_glu, m_s5_w_out, m_gdn_w_in, m_gdn_conv_w, m_gdn_a_log, m_gdn_dt_bias, m_gdn_norm_w, m_gdn_w_out, m_final_norm_w, v_ada_w, v_ada_b, v_norm_w, v_s5_w_in, v_s5_lambda_re, v_s5_lambda_im, v_s5_log_dt, v_s5_b_re, v_s5_b_im, v_s5_c_re, v_s5_c_im, v_s5_d, v_s5_w_glu, v_s5_w_out, v_gdn_w_in, v_gdn_conv_w, v_gdn_a_log, v_gdn_dt_bias, v_gdn_norm_w, v_gdn_w_out, v_final_norm_w):
    W = dict(ada_w=ada_w, ada_b=ada_b, norm_w=norm_w, s5_w_in=s5_w_in, s5_lambda_re=s5_lambda_re, s5_lambda_im=s5_lambda_im,
             s5_log_dt=s5_log_dt, s5_b_re=s5_b_re, s5_b_im=s5_b_im, s5_c_re=s5_c_re, s5_c_im=s5_c_im, s5_d=s5_d,
             s5_w_glu=s5_w_glu, s5_w_out=s5_w_out, gdn_w_in=gdn_w_in, gdn_conv_w=gdn_conv_w, gdn_a_log=gdn_a_log,
             gdn_dt_bias=gdn_dt_bias, gdn_norm_w=gdn_norm_w, gdn_w_out=gdn_w_out, final_norm_w=final_norm_w)
    M = dict(ada_w=m_ada_w, ada_b=m_ada_b, norm_w=m_norm_w, s5_w_in=m_s5_w_in, s5_lambda_re=m_s5_lambda_re,
             s5_lambda_im=m_s5_lambda_im, s5_log_dt=m_s5_log_dt, s5_b_re=m_s5_b_re, s5_b_im=m_s5_b_im, s5_c_re=m_s5_c_re,
             s5_c_im=m_s5_c_im, s5_d=m_s5_d, s5_w_glu=m_s5_w_glu, s5_w_out=m_s5_w_out, gdn_w_in=m_gdn_w_in,
             gdn_conv_w=m_gdn_conv_w, gdn_a_log=m_gdn_a_log, gdn_dt_bias=m_gdn_dt_bias, gdn_norm_w=m_gdn_norm_w,
             gdn_w_out=m_gdn_w_out, final_norm_w=m_final_norm_w)
    V = dict(ada_w=v_ada_w, ada_b=v_ada_b, norm_w=v_norm_w, s5_w_in=v_s5_w_in, s5_lambda_re=v_s5_lambda_re,
             s5_lambda_im=v_s5_lambda_im, s5_log_dt=v_s5_log_dt, s5_b_re=v_s5_b_re, s5_b_im=v_s5_b_im, s5_c_re=v_s5_c_re,
             s5_c_im=v_s5_c_im, s5_d=v_s5_d, s5_w_glu=v_s5_w_glu, s5_w_out=v_s5_w_out, gdn_w_in=v_gdn_w_in,
             gdn_conv_w=v_gdn_conv_w, gdn_a_log=v_gdn_a_log, gdn_dt_bias=v_gdn_dt_bias, gdn_norm_w=v_gdn_norm_w,
             gdn_w_out=v_gdn_w_out, final_norm_w=v_final_norm_w)
    return _step(x, c, W, M, V, loss_target)
```

```python
import functools
import math

import jax
import jax.numpy as jnp
from jax import lax
from jax.experimental import pallas as pl
from jax.experimental.pallas import tpu as pltpu

F32 = jnp.float32
BF16 = jnp.bfloat16
SDS = jax.ShapeDtypeStruct

D_MODEL = 1024
D_INNER = 2048
NORM_EPS = 1e-6
S5_GROUP = 16
S5_GROUPS = 128
S5_STATE = 64
GDN_HEADS = 8
GDN_DK = 128
GDN_DV = 256
GDN_CONV = 4
GDN_CHUNK = 64
GDN_QK = 1024
GDN_CONV_CH = 4096
GDN_PROJ = 6160
ADAM_LR = 0.001
ADAM_B1 = 0.9
ADAM_B2 = 0.999
ADAM_EPS = 1e-08
ADAM_WD = 0.01
ADAM_STEP = 10

N_DEV = 8
LANES = 128
SUBLANES = 8
VMEM_BIG = 56 << 20
VMEM_MID = 40 << 20
S5_GB = 8
S5_TL = 1024
MESH_AXES = ("x", "y", "c")


def _params(sem, vmem=None):
    return pltpu.CompilerParams(dimension_semantics=sem, vmem_limit_bytes=vmem)


def _bdot(a, b, dims=(((1,), (0,)), ((), ()))):
    return lax.dot_general(a.astype(BF16), b.astype(BF16), dims, preferred_element_type=F32)


def _hdot(a, b, dims=(((1,), (0,)), ((), ()))):
    return lax.dot_general(a, b, dims, preferred_element_type=F32, precision=lax.Precision.HIGHEST)


NN = (((1,), (0,)), ((), ()))
NT = (((1,), (1,)), ((), ()))
TN = (((0,), (0,)), ((), ()))


def _tile(n, pref):
    for t in (pref, 512, 256, 128):
        if t <= n and n % t == 0:
            return t
    return n


def _matmul(a, b, mode, name):
    if mode == "nn":
        (m, k), (_, n) = a.shape, b.shape
    elif mode == "nt":
        (m, k), (n, _) = a.shape, b.shape
    else:
        (k, m), (_, n) = a.shape, b.shape
    tm, tn, tk = _tile(m, 512), _tile(n, 512), (k if k <= 2048 else _tile(k, 512))
    if mode == "tn":
        tm, tn = _tile(m, 1024), _tile(n, 1024)
    nk = k // tk
    dims = {"nn": NN, "nt": NT, "tn": TN}[mode]

    def body(a_ref, b_ref, o_ref, acc_ref):
        kk = pl.program_id(2)

        @pl.when(kk == 0)
        def _():
            acc_ref[...] = jnp.zeros_like(acc_ref)
        acc_ref[...] += _bdot(a_ref[...], b_ref[...], dims)

        @pl.when(kk == nk - 1)
        def _():
            o_ref[...] = acc_ref[...]

    a_spec = pl.BlockSpec((tk, tm), lambda i, j, q: (q, i)) if mode == "tn" else pl.BlockSpec((tm, tk), lambda i, j, q: (i, q))
    b_spec = pl.BlockSpec((tn, tk), lambda i, j, q: (j, q)) if mode == "nt" else pl.BlockSpec((tk, tn), lambda i, j, q: (q, j))
    return pl.pallas_call(
        body, name=name, grid=(m // tm, n // tn, nk),
        in_specs=[a_spec, b_spec], out_specs=pl.BlockSpec((tm, tn), lambda i, j, q: (i, j)),
        out_shape=SDS((m, n), F32), scratch_shapes=[pltpu.VMEM((tm, tn), F32)],
        compiler_params=_params(("parallel", "parallel", "arbitrary"), VMEM_MID),
    )(a, b)


def make_mm(name):
    @jax.custom_vjp
    def mm(a, w, grad_slot):
        return _matmul(a, w, "nn", name + "_fwd")

    def fwd(a, w, grad_slot):
        return _matmul(a, w, "nn", name + "_fwd"), (a, w)

    def bwd(res, g):
        a, w = res
        return _matmul(g, w, "nt", name + "_dx"), jnp.zeros_like(w), _matmul(a, g, "tn", name + "_dw")

    mm.defvjp(fwd, bwd)
    return mm


def make_rowwise(f, name, tm, n_rows, n_params, vmem=VMEM_MID):
    def specs_of(arrs, blocked):
        if blocked:
            return [pl.BlockSpec((tm, a.shape[1]), lambda i: (i, 0)) for a in arrs]
        return [pl.BlockSpec(a.shape, lambda i: (0, 0)) for a in arrs]

    def out_structs(rows, params):
        blk = [SDS((tm, r.shape[1]), r.dtype) for r in rows] + [SDS(p.shape, p.dtype) for p in params]
        return jax.eval_shape(f, *blk)

    def run_fwd(rows, params):
        L = rows[0].shape[0]
        outs = out_structs(rows, params)

        def body(*refs):
            ins = [r[...] for r in refs[:n_rows + n_params]]
            res = f(*ins)
            for o_ref, val in zip(refs[n_rows + n_params:], res):
                o_ref[...] = val

        return pl.pallas_call(
            body, name=name + "_fwd", grid=(L // tm,),
            in_specs=specs_of(rows, True) + specs_of(params, False),
            out_specs=[pl.BlockSpec((tm, o.shape[1]), lambda i: (i, 0)) for o in outs],
            out_shape=[SDS((L, o.shape[1]), o.dtype) for o in outs],
            compiler_params=_params(("parallel",), vmem),
        )(*rows, *params)

    def run_bwd(rows, params, gs):
        L = rows[0].shape[0]
        n_g = len(gs)

        def body(*refs):
            i = pl.program_id(0)
            ins = [r[...] for r in refs[:n_rows + n_params]]
            cts = tuple(r[...] for r in refs[n_rows + n_params:n_rows + n_params + n_g])
            outs = refs[n_rows + n_params + n_g:]
            _, vjp = jax.vjp(f, *ins)
            grads = vjp(cts)
            for o_ref, val in zip(outs[:n_rows], grads[:n_rows]):
                o_ref[...] = val

            if n_params:
                @pl.when(i == 0)
                def _():
                    for o_ref in outs[n_rows:]:
                        o_ref[...] = jnp.zeros_like(o_ref)
                for o_ref, val in zip(outs[n_rows:], grads[n_rows:]):
                    o_ref[...] += val

        res = pl.pallas_call(
            body, name=name + "_bwd", grid=(L // tm,),
            in_specs=specs_of(rows, True) + specs_of(params, False) + specs_of(gs, True),
            out_specs=specs_of(rows, True) + specs_of(params, False),
            out_shape=[SDS(r.shape, r.dtype) for r in rows] + [SDS(p.shape, p.dtype) for p in params],
            compiler_params=_params(("arbitrary",), vmem),
        )(*rows, *params, *gs)
        return tuple(res[:n_rows]), tuple(res[n_rows:])

    @jax.custom_vjp
    def op(rows, params):
        return tuple(run_fwd(rows, params))

    def fwd(rows, params):
        return tuple(run_fwd(rows, params)), (rows, params)

    def bwd(res, gs):
        rows, params = res
        return run_bwd(rows, params, tuple(gs))

    op.defvjp(fwd, bwd)
    return op


def _s5_scan_rows(xr_ref, xi_ref, ar, ai, x0r, x0i, tl):
    def step(t8, carry):
        xr, xi = carry
        base = pl.multiple_of(t8 * SUBLANES, SUBLANES)
        for r in range(SUBLANES):
            br = xr_ref[pl.ds(base + r, 1), :]
            bi = xi_ref[pl.ds(base + r, 1), :]
            nr = ar * xr - ai * xi + br
            ni = ar * xi + ai * xr + bi
            xr, xi = nr, ni
            xr_ref[pl.ds(base + r, 1), :] = xr
            xi_ref[pl.ds(base + r, 1), :] = xi
        return xr, xi
    return lax.fori_loop(0, tl // SUBLANES, step, (x0r, x0i))


def _s5_fwd_call(u, bre, bim, cre, cim, a, tl):
    L, e = u.shape
    nb = e // LANES
    ns = bre.shape[2]
    nc = L // tl

    def body(u_ref, bre_ref, bim_ref, cre_ref, cim_ref, a_ref, ys_ref, xb_ref, xr_ref, xi_ref, carry_ref):
        c = pl.program_id(1)

        @pl.when(c == 0)
        def _():
            carry_ref[...] = jnp.zeros_like(carry_ref)
        xb_ref[0, 0] = carry_ref[...]
        ub = u_ref[...]
        xr_ref[...] = _bdot(ub, bre_ref[0])
        xi_ref[...] = _bdot(ub, bim_ref[0])
        ar = a_ref[0, 0:1, :]
        ai = a_ref[0, 1:2, :]
        xr, xi = _s5_scan_rows(xr_ref, xi_ref, ar, ai, carry_ref[0:1, :], carry_ref[1:2, :], tl)
        carry_ref[0:1, :] = xr
        carry_ref[1:2, :] = xi
        ys_ref[...] = _bdot(xr_ref[...], cre_ref[0]) - _bdot(xi_ref[...], cim_ref[0])

    return pl.pallas_call(
        body, name="s5_core_fwd", grid=(nb, nc),
        in_specs=[pl.BlockSpec((tl, LANES), lambda j, c: (c, j)),
                  pl.BlockSpec((1, LANES, ns), lambda j, c: (j, 0, 0)), pl.BlockSpec((1, LANES, ns), lambda j, c: (j, 0, 0)),
                  pl.BlockSpec((1, ns, LANES), lambda j, c: (j, 0, 0)), pl.BlockSpec((1, ns, LANES), lambda j, c: (j, 0, 0)),
                  pl.BlockSpec((1, SUBLANES, ns), lambda j, c: (j, 0, 0))],
        out_specs=[pl.BlockSpec((tl, LANES), lambda j, c: (c, j)),
                   pl.BlockSpec((1, 1, SUBLANES, ns), lambda j, c: (j, c, 0, 0))],
        out_shape=[SDS((L, e), F32), SDS((nb, nc, SUBLANES, ns), F32)],
        scratch_shapes=[pltpu.VMEM((tl, ns), F32), pltpu.VMEM((tl, ns), F32), pltpu.VMEM((SUBLANES, ns), F32)],
        compiler_params=_params(("arbitrary", "arbitrary"), VMEM_MID),
    )(u, bre, bim, cre, cim, a)


def _s5_bwd_call(u, dys, bre, bim, cre, cim, a, xb, tl):
    L, e = u.shape
    nb = e // LANES
    ns = bre.shape[2]
    nc = L // tl

    def body(u_ref, dys_ref, bre_ref, bim_ref, cre_ref, cim_ref, a_ref, xb_ref,
             du_ref, dbre_ref, dbim_ref, dcre_ref, dcim_ref, da_ref,
             xr_ref, xi_ref, gr_ref, gi_ref, gcarry_ref):
        c = pl.program_id(1)

        @pl.when(c == 0)
        def _():
            gcarry_ref[...] = jnp.zeros_like(gcarry_ref)
            dbre_ref[...] = jnp.zeros_like(dbre_ref)
            dbim_ref[...] = jnp.zeros_like(dbim_ref)
            dcre_ref[...] = jnp.zeros_like(dcre_ref)
            dcim_ref[...] = jnp.zeros_like(dcim_ref)
            da_ref[...] = jnp.zeros_like(da_ref)

        ub = u_ref[...]
        dy = dys_ref[...]
        ar = a_ref[0, 0:1, :]
        ai = a_ref[0, 1:2, :]
        x0r = xb_ref[0, 0, 0:1, :]
        x0i = xb_ref[0, 0, 1:2, :]
        xr_ref[...] = _bdot(ub, bre_ref[0])
        xi_ref[...] = _bdot(ub, bim_ref[0])
        _s5_scan_rows(xr_ref, xi_ref, ar, ai, x0r, x0i, tl)
        dcre_ref[0] += _bdot(xr_ref[...], dy, TN)
        dcim_ref[0] -= _bdot(xi_ref[...], dy, TN)
        gr_ref[...] = _bdot(dy, cre_ref[0], NT)
        gi_ref[...] = -_bdot(dy, cim_ref[0], NT)

        def step(t8, carry):
            cr, ci = carry
            base = pl.multiple_of((tl // SUBLANES - 1 - t8) * SUBLANES, SUBLANES)
            for r in range(SUBLANES - 1, -1, -1):
                gr = gr_ref[pl.ds(base + r, 1), :] + cr
                gi = gi_ref[pl.ds(base + r, 1), :] + ci
                gr_ref[pl.ds(base + r, 1), :] = gr
                gi_ref[pl.ds(base + r, 1), :] = gi
                cr = ar * gr + ai * gi
                ci = ar * gi - ai * gr
            return cr, ci
        cr, ci = lax.fori_loop(0, tl // SUBLANES, step, (gcarry_ref[0:1, :], gcarry_ref[1:2, :]))
        gcarry_ref[0:1, :] = cr
        gcarry_ref[1:2, :] = ci
        row = lax.broadcasted_iota(jnp.int32, (tl, ns), 0)
        gr = gr_ref[...]
        gi = gi_ref[...]
        xpr = jnp.where(row == 0, x0r, pltpu.roll(xr_ref[...], 1, 0))
        xpi = jnp.where(row == 0, x0i, pltpu.roll(xi_ref[...], 1, 0))
        da_ref[0, 0:1, :] += jnp.sum(gr * xpr + gi * xpi, axis=0, keepdims=True)
        da_ref[0, 1:2, :] += jnp.sum(gi * xpr - gr * xpi, axis=0, keepdims=True)
        du_ref[...] = _bdot(gr, bre_ref[0], NT) + _bdot(gi, bim_ref[0], NT)
        dbre_ref[0] += _bdot(ub, gr, TN)
        dbim_ref[0] += _bdot(ub, gi, TN)

    rev = lambda c: nc - 1 - c
    return pl.pallas_call(
        body, name="s5_core_bwd", grid=(nb, nc),
        in_specs=[pl.BlockSpec((tl, LANES), lambda j, c: (rev(c), j)), pl.BlockSpec((tl, LANES), lambda j, c: (rev(c), j)),
                  pl.BlockSpec((1, LANES, ns), lambda j, c: (j, 0, 0)), pl.BlockSpec((1, LANES, ns), lambda j, c: (j, 0, 0)),
                  pl.BlockSpec((1, ns, LANES), lambda j, c: (j, 0, 0)), pl.BlockSpec((1, ns, LANES), lambda j, c: (j, 0, 0)),
                  pl.BlockSpec((1, SUBLANES, ns), lambda j, c: (j, 0, 0)),
                  pl.BlockSpec((1, 1, SUBLANES, ns), lambda j, c: (j, rev(c), 0, 0))],
        out_specs=[pl.BlockSpec((tl, LANES), lambda j, c: (rev(c), j)),
                   pl.BlockSpec((1, LANES, ns), lambda j, c: (j, 0, 0)), pl.BlockSpec((1, LANES, ns), lambda j, c: (j, 0, 0)),
                   pl.BlockSpec((1, ns, LANES), lambda j, c: (j, 0, 0)), pl.BlockSpec((1, ns, LANES), lambda j, c: (j, 0, 0)),
                   pl.BlockSpec((1, SUBLANES, ns), lambda j, c: (j, 0, 0))],
        out_shape=[SDS((L, e), F32), SDS(bre.shape, F32), SDS(bim.shape, F32), SDS(cre.shape, F32), SDS(cim.shape, F32),
                   SDS(a.shape, F32)],
        scratch_shapes=[pltpu.VMEM((tl, ns), F32) for _ in range(4)] + [pltpu.VMEM((SUBLANES, ns), F32)],
        compiler_params=_params(("arbitrary", "arbitrary"), VMEM_MID),
    )(u, dys, bre, bim, cre, cim, a, xb)


def make_s5_core(tl):
    @jax.custom_vjp
    def s5_core(u, bre, bim, cre, cim, a):
        return _s5_fwd_call(u, bre, bim, cre, cim, a, tl)[0]

    def fwd(u, bre, bim, cre, cim, a):
        ys, xb = _s5_fwd_call(u, bre, bim, cre, cim, a, tl)
        return ys, (u, bre, bim, cre, cim, a, xb)

    def bwd(res, dys):
        u, bre, bim, cre, cim, a, xb = res
        return tuple(_s5_bwd_call(u, dys, bre, bim, cre, cim, a, xb, tl))

    s5_core.defvjp(fwd, bwd)
    return s5_core


def _s5_block_params(lam_re, lam_im, log_dt, b_re, b_im, c_re, c_im):
    dt = jnp.exp(log_dt)[:, None]
    mag = jnp.exp(lam_re * dt)
    ab_re = mag * jnp.cos(lam_im * dt)
    ab_im = mag * jnp.sin(lam_im * dt)
    den = lam_re * lam_re + lam_im * lam_im
    nr = ab_re - 1.0
    ni = ab_im
    q_re = (nr * lam_re + ni * lam_im) / den
    q_im = (ni * lam_re - nr * lam_im) / den
    bb_re = q_re[..., None] * b_re - q_im[..., None] * b_im
    bb_im = q_re[..., None] * b_im + q_im[..., None] * b_re
    nb = S5_GROUPS // S5_GB
    eye = jnp.eye(S5_GB, dtype=F32)

    def bdiag_in(bb):
        t = bb.reshape(nb, S5_GB, S5_STATE, S5_GROUP)
        t = jnp.einsum("jgpm,gh->jgmhp", t, eye)
        return t.reshape(nb, S5_GB * S5_GROUP, S5_GB * S5_STATE)

    def bdiag_out(cc):
        t = cc.reshape(nb, S5_GB, S5_GROUP, S5_STATE)
        t = jnp.einsum("jgmp,gh->jgphm", t, eye)
        return t.reshape(nb, S5_GB * S5_STATE, S5_GB * S5_GROUP)

    a = jnp.stack([ab_re.reshape(nb, S5_GB * S5_STATE), ab_im.reshape(nb, S5_GB * S5_STATE)], axis=1)
    a = jnp.concatenate([a, jnp.zeros((nb, SUBLANES - 2, S5_GB * S5_STATE), F32)], axis=1)
    return bdiag_in(bb_re), bdiag_in(bb_im), bdiag_out(c_re), bdiag_out(c_im), a


def _shift_down(x, s, row):
    if s == 0:
        return x
    return jnp.where(row >= s, pltpu.roll(x, s, 0), 0.0)


def _shift_up(x, s, row, n):
    if s == 0:
        return x
    return jnp.where(row < n - s, pltpu.roll(x, n - s, 0), 0.0)


def _conv_fwd_call(x, w):
    L, ch = x.shape

    def body(x_ref, w_ref, y_ref):
        xv = x_ref[...]
        row = lax.broadcasted_iota(jnp.int32, xv.shape, 0)
        acc = jnp.zeros_like(xv)
        for j in range(GDN_CONV):
            acc += w_ref[j:j + 1, :] * _shift_down(xv, GDN_CONV - 1 - j, row)
        y_ref[...] = acc

    return pl.pallas_call(
        body, name="gdn_conv_fwd", grid=(ch // LANES,),
        in_specs=[pl.BlockSpec((L, LANES), lambda j: (0, j)), pl.BlockSpec((SUBLANES, LANES), lambda j: (0, j))],
        out_specs=pl.BlockSpec((L, LANES), lambda j: (0, j)), out_shape=SDS((L, ch), F32),
        compiler_params=_params(("parallel",), VMEM_MID),
    )(x, w)


def _conv_bwd_call(x, w, dy):
    L, ch = x.shape

    def body(x_ref, w_ref, dy_ref, dx_ref, dw_ref):
        xv = x_ref[...]
        g = dy_ref[...]
        row = lax.broadcasted_iota(jnp.int32, xv.shape, 0)
        acc = jnp.zeros_like(xv)
        dws = []
        for j in range(GDN_CONV):
            s = GDN_CONV - 1 - j
            acc += w_ref[j:j + 1, :] * _shift_up(g, s, row, L)
            dws.append(jnp.sum(g * _shift_down(xv, s, row), axis=0, keepdims=True))
        dx_ref[...] = acc
        dw_ref[...] = jnp.concatenate(dws + [jnp.zeros((SUBLANES - GDN_CONV, LANES), F32)], axis=0)

    return pl.pallas_call(
        body, name="gdn_conv_bwd", grid=(ch // LANES,),
        in_specs=[pl.BlockSpec((L, LANES), lambda j: (0, j)), pl.BlockSpec((SUBLANES, LANES), lambda j: (0, j)),
                  pl.BlockSpec((L, LANES), lambda j: (0, j))],
        out_specs=[pl.BlockSpec((L, LANES), lambda j: (0, j)), pl.BlockSpec((SUBLANES, LANES), lambda j: (0, j))],
        out_shape=[SDS((L, ch), F32), SDS((SUBLANES, ch), F32)],
        compiler_params=_params(("parallel",), VMEM_MID),
    )(x, w, dy)


@jax.custom_vjp
def gdn_conv(x, w):
    return _conv_fwd_call(x, w)


def _gdn_conv_f(x, w):
    return _conv_fwd_call(x, w), (x, w)


def _gdn_conv_b(res, dy):
    x, w = res
    return tuple(_conv_bwd_call(x, w, dy))


gdn_conv.defvjp(_gdn_conv_f, _gdn_conv_b)


BNN = (((2,), (1,)), ((0,), (0,)))
BNT = (((2,), (2,)), ((0,), (0,)))
BTN = (((1,), (1,)), ((0,), (0,)))
GDN_PREP_BATCH = 8


def _gdn_prep_math(q, k, v, beta, g):
    B, C = q.shape[0], q.shape[1]
    ri = lax.broadcasted_iota(jnp.int32, (B, C, C), 1)
    ci = lax.broadcasted_iota(jnp.int32, (B, C, C), 2)
    causal = ri >= ci
    strict = ri > ci
    eye = (ri == ci).astype(F32)
    gb = jnp.broadcast_to(g, (B, C, C))
    g_row = jnp.sum(gb * eye, axis=1, keepdims=True)
    gc_col = jnp.sum(jnp.where(causal, jnp.broadcast_to(g_row, (B, C, C)), 0.0), axis=2, keepdims=True)
    gc_row = jnp.sum(jnp.where(ri <= ci, gb, 0.0), axis=1, keepdims=True)
    decay = jnp.exp(jnp.where(causal, gc_col - gc_row, -jnp.inf))
    kk = _bdot(k, k, BNT)
    a_mat = jnp.where(strict, beta * kk * decay, 0.0)
    n = -a_mat
    t = eye + n
    for _ in range(int(math.log2(C)) - 1):
        n = _hdot(n, n, BNN)
        t = t + _hdot(t, n, BNN)
    e_gc = jnp.exp(gc_col)
    w = _hdot(t, beta * e_gc * k, BNN)
    u = _hdot(t, beta * v, BNN)
    qk = _bdot(q, k, BNT) * decay
    q_dec = q * e_gc
    g_last = gc_col[:, C - 1:C, :]
    k_dec = k * jnp.exp(g_last - gc_col)
    return q_dec, w, u, qk, k_dec, gc_col


def _gdn_prep_specs(L):
    C = GDN_CHUNK
    nb = min(GDN_PREP_BATCH, L // C)
    R = nb * C
    ins = [pl.BlockSpec((R, GDN_DK), lambda h, c: (c, h)), pl.BlockSpec((R, GDN_DK), lambda h, c: (c, h)),
           pl.BlockSpec((R, GDN_DV), lambda h, c: (c, h)),
           pl.BlockSpec((1, R, 1), lambda h, c: (h, c, 0)), pl.BlockSpec((1, R, 1), lambda h, c: (h, c, 0))]
    outs = [pl.BlockSpec((1, R, GDN_DK), lambda h, c: (h, c, 0)), pl.BlockSpec((1, R, GDN_DK), lambda h, c: (h, c, 0)),
            pl.BlockSpec((1, R, GDN_DV), lambda h, c: (h, c, 0)), pl.BlockSpec((1, R, C), lambda h, c: (h, c, 0)),
            pl.BlockSpec((1, R, GDN_DK), lambda h, c: (h, c, 0)), pl.BlockSpec((1, R, 1), lambda h, c: (h, c, 0))]
    shapes = [SDS((GDN_HEADS, L, GDN_DK), F32), SDS((GDN_HEADS, L, GDN_DK), F32), SDS((GDN_HEADS, L, GDN_DV), F32),
              SDS((GDN_HEADS, L, C), F32), SDS((GDN_HEADS, L, GDN_DK), F32), SDS((GDN_HEADS, L, 1), F32)]
    return ins, outs, shapes, nb


def _chunks(x, nb):
    return x.reshape(nb, x.shape[0] // nb, x.shape[1])


def _gdn_prep_fwd_call(q, k, v, beta, g):
    L = q.shape[0]
    ins, outs, shapes, nb = _gdn_prep_specs(L)

    def body(q_ref, k_ref, v_ref, b_ref, g_ref, *o_refs):
        res = _gdn_prep_math(_chunks(q_ref[...], nb), _chunks(k_ref[...], nb), _chunks(v_ref[...], nb),
                             _chunks(b_ref[0], nb), _chunks(g_ref[0], nb))
        for o_ref, val in zip(o_refs, res):
            o_ref[0] = val.reshape(val.shape[0] * val.shape[1], val.shape[2])

    return pl.pallas_call(
        body, name="gdn_prep_fwd", grid=(GDN_HEADS, L // (nb * GDN_CHUNK)), in_specs=ins, out_specs=outs, out_shape=shapes,
        compiler_params=_params(("parallel", "parallel"), VMEM_MID),
    )(q, k, v, beta, g)


def _gdn_prep_bwd_call(q, k, v, beta, g, cts):
    L = q.shape[0]
    ins, outs, _, nb = _gdn_prep_specs(L)

    def body(q_ref, k_ref, v_ref, b_ref, g_ref, c0, c1, c2, c3, c4, c5, dq_ref, dk_ref, dv_ref, db_ref, dg_ref):
        _, vjp = jax.vjp(_gdn_prep_math, _chunks(q_ref[...], nb), _chunks(k_ref[...], nb), _chunks(v_ref[...], nb),
                         _chunks(b_ref[0], nb), _chunks(g_ref[0], nb))
        dq, dk, dv, db, dg = vjp(tuple(_chunks(c[0], nb) for c in (c0, c1, c2, c3, c4, c5)))
        flat = lambda t: t.reshape(t.shape[0] * t.shape[1], t.shape[2])
        dq_ref[...] = flat(dq)
        dk_ref[...] = flat(dk)
        dv_ref[...] = flat(dv)
        db_ref[0] = flat(db)
        dg_ref[0] = flat(dg)

    return pl.pallas_call(
        body, name="gdn_prep_bwd", grid=(GDN_HEADS, L // (nb * GDN_CHUNK)), in_specs=ins + outs, out_specs=ins,
        out_shape=[SDS(q.shape, F32), SDS(k.shape, F32), SDS(v.shape, F32), SDS(beta.shape, F32), SDS(g.shape, F32)],
        compiler_params=_params(("parallel", "parallel"), VMEM_MID),
    )(q, k, v, beta, g, *cts)


@jax.custom_vjp
def gdn_prep(q, k, v, beta, g):
    return tuple(_gdn_prep_fwd_call(q, k, v, beta, g))


def _gdn_prep_f(q, k, v, beta, g):
    return tuple(_gdn_prep_fwd_call(q, k, v, beta, g)), (q, k, v, beta, g)


def _gdn_prep_b(res, cts):
    return tuple(_gdn_prep_bwd_call(*res, tuple(cts)))


gdn_prep.defvjp(_gdn_prep_f, _gdn_prep_b)


def _gdn_step_math(q_dec, w, u, qk, k_dec, gc, state):
    H, C = q_dec.shape[0], q_dec.shape[1]
    v_new = u - _bdot(w, state, BNN)
    o = _bdot(q_dec, state, BNN) + _bdot(qk, v_new, BNN)
    gl = gc[:, C - 1:C, :]
    new_state = jnp.exp(gl) * state + _bdot(k_dec, v_new, BTN)
    return jnp.concatenate([o[h] for h in range(H)], axis=1), new_state


def _gdn_scan_specs(L, rev):
    C, H = GDN_CHUNK, GDN_HEADS
    nc = L // C
    cc = (lambda c: nc - 1 - c) if rev else (lambda c: c)
    ins = [pl.BlockSpec((H, C, GDN_DK), lambda c: (0, cc(c), 0)), pl.BlockSpec((H, C, GDN_DK), lambda c: (0, cc(c), 0)),
           pl.BlockSpec((H, C, GDN_DV), lambda c: (0, cc(c), 0)), pl.BlockSpec((H, C, C), lambda c: (0, cc(c), 0)),
           pl.BlockSpec((H, C, GDN_DK), lambda c: (0, cc(c), 0)), pl.BlockSpec((H, C, 1), lambda c: (0, cc(c), 0))]
    o_spec = pl.BlockSpec((C, H * GDN_DV), lambda c: (cc(c), 0))
    s_spec = pl.BlockSpec((1, H, GDN_DK, GDN_DV), lambda c: (cc(c), 0, 0, 0))
    return ins, o_spec, s_spec, nc


def _gdn_scan_fwd_call(q_dec, w, u, qk, k_dec, gc):
    L = q_dec.shape[1]
    ins, o_spec, s_spec, nc = _gdn_scan_specs(L, False)

    def body(qd_ref, w_ref, u_ref, qk_ref, kd_ref, gc_ref, o_ref, sin_ref, s_ref):
        c = pl.program_id(0)

        @pl.when(c == 0)
        def _():
            s_ref[...] = jnp.zeros_like(s_ref)
        st = s_ref[...]
        sin_ref[0] = st
        o, ns = _gdn_step_math(qd_ref[...], w_ref[...], u_ref[...], qk_ref[...], kd_ref[...], gc_ref[...], st)
        o_ref[...] = o
        s_ref[...] = ns

    return pl.pallas_call(
        body, name="gdn_scan_fwd", grid=(nc,), in_specs=ins, out_specs=[o_spec, s_spec],
        out_shape=[SDS((L, GDN_HEADS * GDN_DV), F32), SDS((nc, GDN_HEADS, GDN_DK, GDN_DV), F32)],
        scratch_shapes=[pltpu.VMEM((GDN_HEADS, GDN_DK, GDN_DV), F32)],
        compiler_params=_params(("arbitrary",), VMEM_MID),
    )(q_dec, w, u, qk, k_dec, gc)


def _gdn_scan_bwd_call(q_dec, w, u, qk, k_dec, gc, s_in, do):
    L = q_dec.shape[1]
    ins, o_spec, s_spec, nc = _gdn_scan_specs(L, True)

    def body(qd_ref, w_ref, u_ref, qk_ref, kd_ref, gc_ref, sin_ref, do_ref,
             dqd_ref, dw_ref, du_ref, dqk_ref, dkd_ref, dgc_ref, ds_ref):
        c = pl.program_id(0)

        @pl.when(c == 0)
        def _():
            ds_ref[...] = jnp.zeros_like(ds_ref)
        _, vjp = jax.vjp(_gdn_step_math, qd_ref[...], w_ref[...], u_ref[...], qk_ref[...], kd_ref[...], gc_ref[...], sin_ref[0])
        dqd, dw, du, dqk, dkd, dgc, dst = vjp((do_ref[...], ds_ref[...]))
        dqd_ref[...] = dqd
        dw_ref[...] = dw
        du_ref[...] = du
        dqk_ref[...] = dqk
        dkd_ref[...] = dkd
        dgc_ref[...] = dgc
        ds_ref[...] = dst

    return pl.pallas_call(
        body, name="gdn_scan_bwd", grid=(nc,), in_specs=ins + [s_spec, o_spec], out_specs=ins,
        out_shape=[SDS(t.shape, F32) for t in (q_dec, w, u, qk, k_dec, gc)],
        scratch_shapes=[pltpu.VMEM((GDN_HEADS, GDN_DK, GDN_DV), F32)],
        compiler_params=_params(("arbitrary",), VMEM_MID),
    )(q_dec, w, u, qk, k_dec, gc, s_in, do)


@jax.custom_vjp
def gdn_scan(q_dec, w, u, qk, k_dec, gc):
    return _gdn_scan_fwd_call(q_dec, w, u, qk, k_dec, gc)[0]


def _gdn_scan_f(*args):
    o, s_in = _gdn_scan_fwd_call(*args)
    return o, (*args, s_in)


def _gdn_scan_b(res, do):
    return tuple(_gdn_scan_bwd_call(*res, do))


gdn_scan.defvjp(_gdn_scan_f, _gdn_scan_b)


def _silu(x):
    return x * jax.nn.sigmoid(x)


def _gelu_tanh(x):
    return 0.5 * x * (1.0 + jnp.tanh(math.sqrt(2.0 / math.pi) * (x + 0.044715 * (x * x * x))))


def _f_lnmod(x, nw, sc, sh, bsc, bsh):
    xn = x * lax.rsqrt(jnp.mean(x * x, axis=-1, keepdims=True) + NORM_EPS) * nw
    return (xn * (1.0 + (sc + bsc)) + (sh + bsh),)


def _f_s5_act(ys, u, d):
    return (_gelu_tanh(ys + d * u),)


def _f_s5_gate(y2, t, z):
    return (y2 * jax.nn.sigmoid(t) * _silu(z),)


def _f_res(x, y, gate, bgate):
    return (x + (gate + bgate) * y,)


def _heads(x, width, fn):
    return jnp.concatenate([fn(x[:, i * width:(i + 1) * width]) for i in range(x.shape[1] // width)], axis=1)


def _l2n(x):
    return x * lax.rsqrt(jnp.sum(x * x, axis=-1, keepdims=True) + NORM_EPS)


def _f_qnorm(x):
    return (_heads(_silu(x), GDN_DK, _l2n) * (GDN_DK ** -0.5),)


def _f_knorm(x):
    return (_heads(_silu(x), GDN_DK, _l2n),)


def _f_vact(x):
    return (_silu(x),)


def _f_betag(ba, alog, dtb):
    col = lax.broadcasted_iota(jnp.int32, ba.shape, 1)
    t = ba + dtb
    softplus = jnp.maximum(t, 0.0) + jnp.log1p(jnp.exp(-jnp.abs(t)))
    g = -jnp.exp(alog) * softplus
    return (jnp.where(col < GDN_HEADS, jax.nn.sigmoid(ba), jnp.where(col < 2 * GDN_HEADS, g, 0.0)),)


def _f_gdn_post(o, z, nw):
    on = _heads(o, GDN_DV, lambda t: t * lax.rsqrt(jnp.mean(t * t, axis=-1, keepdims=True) + NORM_EPS))
    return (on * nw * _silu(z),)


def _f_loss(x, tgt, fw):
    y = x * lax.rsqrt(jnp.mean(x * x, axis=-1, keepdims=True) + NORM_EPS) * fw
    err = y - tgt
    return (0.5 * jnp.mean(err * err, axis=-1, keepdims=True),)


def _ada_mod_call(c_all, ada_w):
    n = ada_w.shape[2]

    def body(c_ref, w_ref, o_ref):
        ca = _silu(c_ref[...])
        for l in range(ada_w.shape[0]):
            o_ref[l] = _bdot(ca, w_ref[l])

    return pl.pallas_call(body, name="ada_mod", out_shape=SDS((ada_w.shape[0], N_DEV, n), F32),
                          compiler_params=_params(None, VMEM_MID))(c_all, ada_w)


def _ada_grad_call(c_all, dmod):
    nl, _, n = dmod.shape

    def body(c_ref, d_ref, o_ref):
        ca = _silu(c_ref[...])
        for l in range(nl):
            o_ref[l] = _hdot(ca, d_ref[l], TN)

    return pl.pallas_call(body, name="ada_grad", out_shape=SDS((nl, c_all.shape[1], n), F32),
                          compiler_params=_params(None, VMEM_MID))(c_all, dmod)


ADAM_ROWS = 512


def _adam_call(gs, w, m, v, name, rows=None):
    n, r, cols = gs.shape
    rows = rows or ADAM_ROWS

    def body(g_ref, w_ref, m_ref, v_ref, go_ref, d_ref, mo_ref, vo_ref):
        g = g_ref[0]
        for s in range(1, n):
            g = g + g_ref[s]
        m2 = ADAM_B1 * m_ref[...] + (1.0 - ADAM_B1) * g
        v2 = ADAM_B2 * v_ref[...] + (1.0 - ADAM_B2) * (g * g)
        m_hat = m2 / (1.0 - ADAM_B1 ** ADAM_STEP)
        v_hat = v2 / (1.0 - ADAM_B2 ** ADAM_STEP)
        go_ref[...] = g
        d_ref[...] = -ADAM_LR * (m_hat / (jnp.sqrt(v_hat) + ADAM_EPS) + ADAM_WD * w_ref[...])
        mo_ref[...] = m2
        vo_ref[...] = v2

    blk = pl.BlockSpec((rows, cols), lambda i: (i, 0))
    return pl.pallas_call(
        body, name=name, grid=(r // rows,),
        in_specs=[pl.BlockSpec((n, rows, cols), lambda i: (0, i, 0)), blk, blk, blk],
        out_specs=[blk, blk, blk, blk], out_shape=[SDS((r, cols), F32)] * 4,
        compiler_params=_params(("parallel",), VMEM_MID),
    )(gs, w, m, v)


def _sum_call(gs, name):
    n, r, _ = gs.shape

    def body(g_ref, o_ref):
        g = g_ref[0]
        for s in range(1, n):
            g = g + g_ref[s]
        o_ref[...] = g

    return pl.pallas_call(
        body, name=name, grid=(r // ADAM_ROWS,),
        in_specs=[pl.BlockSpec((n, ADAM_ROWS, LANES), lambda i: (0, i, 0))],
        out_specs=pl.BlockSpec((ADAM_ROWS, LANES), lambda i: (i, 0)), out_shape=SDS((r, LANES), F32),
        compiler_params=_params(("parallel",), VMEM_MID),
    )(gs)


def _allgather_call(x_shard, name, in_hbm):
    m_per, n = x_shard.shape

    def body(x_ref, out_ref, send_sems, recv_sems, local_sem):
        x, y, c = lax.axis_index("x"), lax.axis_index("y"), lax.axis_index("c")
        me, sibling = (x, y, c), (x, y, 1 - c)
        chips = [(1 - x, y), (x, 1 - y), (1 - x, 1 - y)]

        def rows(px, py, pc):
            return out_ref.at[pl.ds((4 * px + 2 * py + pc) * m_per, m_per), :]

        def copy(k, block, to, src=None):
            return pltpu.make_async_remote_copy(
                src_ref=rows(*block) if src is None else src, dst_ref=rows(*block),
                send_sem=send_sems.at[k], recv_sem=recv_sems.at[k], device_id=to, device_id_type=pl.DeviceIdType.MESH)

        mine = pltpu.make_async_copy(x_ref, rows(*me), local_sem)
        mine.start()
        first = [copy(0, me, sibling, src=x_ref)]
        first += [copy(1 + j, me, (*chip, c), src=x_ref) for j, chip in enumerate(chips)]
        for cp in first:
            cp.start()
        passed = [copy(4 + j, (*chip, c), sibling) for j, chip in enumerate(chips)]
        for j, chip in enumerate(chips):
            copy(1 + j, (*chip, c), me).wait_recv()
            passed[j].start()
        copy(0, sibling, me).wait_recv()
        for j, chip in enumerate(chips):
            copy(4 + j, (*chip, 1 - c), me).wait_recv()
        for cp in first + passed:
            cp.wait_send()
        mine.wait()

    space = pl.ANY if in_hbm else pltpu.VMEM
    return pl.pallas_call(
        body, name=name, out_shape=SDS((N_DEV * m_per, n), x_shard.dtype),
        in_specs=[pl.BlockSpec(memory_space=space)], out_specs=pl.BlockSpec(memory_space=space),
        scratch_shapes=[pltpu.SemaphoreType.DMA((7,)), pltpu.SemaphoreType.DMA((7,)), pltpu.SemaphoreType.DMA],
        compiler_params=_params(None, None if in_hbm else VMEM_BIG),
    )(x_shard)


def _gather_weights_call(shards, name):
    nw = len(shards)

    def body(*refs):
        x_refs, out_refs = refs[:nw], refs[nw:2 * nw]
        send_sems, recv_sems, local_sems = refs[2 * nw:]
        x, y, c = lax.axis_index("x"), lax.axis_index("y"), lax.axis_index("c")
        me, sibling = (x, y, c), (x, y, 1 - c)
        chips = [(1 - x, y), (x, 1 - y), (1 - x, 1 - y)]

        def slot(w, px, py, pc):
            return out_refs[w].at[4 * px + 2 * py + pc]

        def copy(w, k, block, to, src=None):
            dst = slot(w, *block)
            return pltpu.make_async_remote_copy(
                src_ref=dst if src is None else src, dst_ref=dst, send_sem=send_sems.at[7 * w + k],
                recv_sem=recv_sems.at[7 * w + k], device_id=to, device_id_type=pl.DeviceIdType.MESH)

        mines = [pltpu.make_async_copy(x_refs[w], slot(w, *me), local_sems.at[w]) for w in range(nw)]
        for cp in mines:
            cp.start()
        first = [copy(w, 0, me, sibling, src=x_refs[w]) for w in range(nw)]
        first += [copy(w, 1 + j, me, (*chip, c), src=x_refs[w]) for w in range(nw) for j, chip in enumerate(chips)]
        for cp in first:
            cp.start()
        passed = []
        for w in range(nw):
            for j, chip in enumerate(chips):
                copy(w, 1 + j, (*chip, c), me).wait_recv()
                fwd = copy(w, 4 + j, (*chip, c), sibling)
                fwd.start()
                passed.append(fwd)
        for w in range(nw):
            copy(w, 0, sibling, me).wait_recv()
            for j, chip in enumerate(chips):
                copy(w, 4 + j, (*chip, 1 - c), me).wait_recv()
        for cp in first + passed:
            cp.wait_send()
        for cp in mines:
            cp.wait()

    hbm = pl.BlockSpec(memory_space=pl.ANY)
    return pl.pallas_call(
        body, name=name, out_shape=[SDS((N_DEV,) + s.shape, s.dtype) for s in shards],
        in_specs=[hbm] * nw, out_specs=[hbm] * nw,
        scratch_shapes=[pltpu.SemaphoreType.DMA((7 * nw,)), pltpu.SemaphoreType.DMA((7 * nw,)), pltpu.SemaphoreType.DMA((nw,))],
    )(*shards)


def _pair_exchange_call(grads, name):
    nw = len(grads)

    def body(*refs):
        g_refs, own_refs, got_refs = refs[:nw], refs[nw:2 * nw], refs[2 * nw:3 * nw]
        send_sems, recv_sems, local_sems = refs[3 * nw:]
        x, y, c = lax.axis_index("x"), lax.axis_index("y"), lax.axis_index("c")
        copies = []
        for w in range(nw):
            for j in range(4):
                keep = pltpu.make_async_copy(g_refs[w].at[2 * j + c], own_refs[w].at[j], local_sems.at[4 * w + j])
                give = pltpu.make_async_remote_copy(
                    src_ref=g_refs[w].at[2 * j + 1 - c], dst_ref=got_refs[w].at[j], send_sem=send_sems.at[4 * w + j],
                    recv_sem=recv_sems.at[4 * w + j], device_id=(x, y, 1 - c), device_id_type=pl.DeviceIdType.MESH)
                keep.start()
                give.start()
                copies += [keep, give]
        for cp in copies:
            cp.wait()

    hbm = pl.BlockSpec(memory_space=pl.ANY)
    quarter = [SDS((4,) + g.shape[1:], g.dtype) for g in grads]
    res = pl.pallas_call(
        body, name=name, out_shape=quarter + quarter, in_specs=[hbm] * nw, out_specs=[hbm] * (2 * nw),
        scratch_shapes=[pltpu.SemaphoreType.DMA((4 * nw,)), pltpu.SemaphoreType.DMA((4 * nw,)), pltpu.SemaphoreType.DMA((4 * nw,))],
    )(*grads)
    return res[:nw], res[nw:]


def _chip_exchange_call(parts, name):
    nw = len(parts)

    def body(*refs):
        p_refs, out_refs = refs[:nw], refs[nw:2 * nw]
        send_sems, recv_sems, local_sems = refs[2 * nw:]
        x, y, c = lax.axis_index("x"), lax.axis_index("y"), lax.axis_index("c")
        mine = 2 * x + y
        chips = [(1 - x, y), (x, 1 - y), (1 - x, 1 - y)]
        copies = []
        for w in range(nw):
            keep = pltpu.make_async_copy(p_refs[w].at[mine], out_refs[w].at[mine], local_sems.at[w])
            keep.start()
            copies.append(keep)
            for j, (px, py) in enumerate(chips):
                give = pltpu.make_async_remote_copy(
                    src_ref=p_refs[w].at[2 * px + py], dst_ref=out_refs[w].at[mine], send_sem=send_sems.at[3 * w + j],
                    recv_sem=recv_sems.at[3 * w + j], device_id=(px, py, c), device_id_type=pl.DeviceIdType.MESH)
                give.start()
                copies.append(give)
        for cp in copies:
            cp.wait()

    hbm = pl.BlockSpec(memory_space=pl.ANY)
    return pl.pallas_call(
        body, name=name, out_shape=[SDS(p.shape, p.dtype) for p in parts], in_specs=[hbm] * nw, out_specs=[hbm] * nw,
        scratch_shapes=[pltpu.SemaphoreType.DMA((3 * nw,)), pltpu.SemaphoreType.DMA((3 * nw,)), pltpu.SemaphoreType.DMA((nw,))],
    )(*parts)


def _add_call(a, b, name):
    _, k, n = a.shape
    tr = _tile(k, 256)
    blk = pl.BlockSpec((1, tr, n), lambda j, i: (j, i, 0))

    def body(a_ref, b_ref, o_ref):
        o_ref[...] = a_ref[...] + b_ref[...]

    return pl.pallas_call(body, name=name, grid=(a.shape[0], k // tr), in_specs=[blk, blk], out_specs=blk,
                          out_shape=SDS(a.shape, a.dtype), compiler_params=_params(("parallel", "parallel"), VMEM_MID))(a, b)


def _join_cols_call(w8, name):
    _, k, n = w8.shape
    tk = _tile(k, 256)

    def body(w_ref, o_ref):
        for s in range(N_DEV):
            o_ref[:, n * s:n * (s + 1)] = w_ref[s]

    return pl.pallas_call(body, name=name, grid=(k // tk,), in_specs=[pl.BlockSpec((N_DEV, tk, n), lambda i: (0, i, 0))],
                          out_specs=pl.BlockSpec((tk, N_DEV * n), lambda i: (i, 0)), out_shape=SDS((k, N_DEV * n), w8.dtype),
                          compiler_params=_params(("parallel",), VMEM_MID))(w8)


def _split_cols_call(g, name):
    k, n8 = g.shape
    n = n8 // N_DEV
    tk = _tile(k, 256)

    def body(g_ref, o_ref):
        for s in range(N_DEV):
            o_ref[s] = g_ref[:, n * s:n * (s + 1)]

    return pl.pallas_call(body, name=name, grid=(k // tk,), in_specs=[pl.BlockSpec((tk, n8), lambda i: (i, 0))],
                          out_specs=pl.BlockSpec((N_DEV, tk, n), lambda i: (0, i, 0)), out_shape=SDS((N_DEV, k, n), g.dtype),
                          compiler_params=_params(("parallel",), VMEM_MID))(g)


def _pack(parts, rows_multiple):
    flat = jnp.concatenate([p.reshape(-1) for p in parts])
    unit = rows_multiple * LANES
    padded = -(-flat.shape[0] // unit) * unit
    flat = jnp.concatenate([flat, jnp.zeros((padded - flat.shape[0],), F32)])
    return flat.reshape(-1, LANES)


def _unpack(buf, shapes):
    flat = buf.reshape(-1)
    out, off = [], 0
    for s in shapes:
        n = math.prod(s)
        out.append(flat[off:off + n].reshape(s))
        off += n
    return out


def _local_loss(diff, const, L):
    (x, mod_raw, norm_w, lam_re, lam_im, log_dt, b_re, b_im, c_re, c_im, s5_d, conv_w, a_log, dt_bias, gdn_nw, final_nw,
     *slots) = diff
    tgt, ada_b, weights = const
    lin = {n: (lambda a, n=n, i=i: make_mm(n)(a, weights[i], slots[i])) for i, n in enumerate(MM_NAMES)}
    tm = 256 if L % 256 == 0 else L
    mods = mod_raw.reshape(2, 3, 1, D_MODEL)
    biases = ada_b.reshape(2, 3, 1, D_MODEL)

    op_ln0 = make_rowwise(_f_lnmod, "ln0", tm, 1, 5)
    (h,) = op_ln0((x,), (norm_w[0:1], mods[0, 1], mods[0, 0], biases[0, 1], biases[0, 0]))
    u = lin["s5_in_u"](h)
    z = lin["s5_in_z"](h)
    blocks = _s5_block_params(lam_re, lam_im, log_dt, b_re, b_im, c_re, c_im)
    ys = make_s5_core(min(S5_TL, L))(u, *blocks)
    (y2,) = make_rowwise(_f_s5_act, "s5_act", tm, 2, 1)((ys, u), (s5_d.reshape(1, D_INNER),))
    t = lin["s5_glu"](y2)
    (y4,) = make_rowwise(_f_s5_gate, "s5_gate", tm, 3, 0)((y2, t, z), ())
    o = lin["s5_out"](y4)
    (x1,) = make_rowwise(_f_res, "res0", tm, 2, 2)((x, o), (mods[0, 2], biases[0, 2]))

    op_ln1 = make_rowwise(_f_lnmod, "ln1", tm, 1, 5)
    (h,) = op_ln1((x1,), (norm_w[1:2], mods[1, 1], mods[1, 0], biases[1, 1], biases[1, 0]))
    q0 = lin["gdn_in_q"](h)
    k0 = lin["gdn_in_k"](h)
    v0 = lin["gdn_in_v"](h)
    gz = lin["gdn_in_z"](h)
    ba = lin["gdn_in_ba"](h)
    cw = jnp.concatenate([conv_w, jnp.zeros((SUBLANES - GDN_CONV, GDN_CONV_CH), F32)], axis=0)
    (q,) = make_rowwise(_f_qnorm, "gdn_qn", tm, 1, 0)((gdn_conv(q0, cw[:, :GDN_QK]),), ())
    (k,) = make_rowwise(_f_knorm, "gdn_kn", tm, 1, 0)((gdn_conv(k0, cw[:, GDN_QK:2 * GDN_QK]),), ())
    (v,) = make_rowwise(_f_vact, "gdn_va", tm, 1, 0)((gdn_conv(v0, cw[:, 2 * GDN_QK:]),), ())
    pad = jnp.zeros((LANES - 2 * GDN_HEADS,), F32)
    alog_row = jnp.concatenate([jnp.zeros((GDN_HEADS,), F32), a_log, pad]).reshape(1, LANES)
    dtb_row = jnp.concatenate([jnp.zeros((GDN_HEADS,), F32), dt_bias, pad]).reshape(1, LANES)
    (bg,) = make_rowwise(_f_betag, "gdn_bg", tm, 1, 2)((ba,), (alog_row, dtb_row))
    beta = bg[:, :GDN_HEADS].T.reshape(GDN_HEADS, L, 1)
    g = bg[:, GDN_HEADS:2 * GDN_HEADS].T.reshape(GDN_HEADS, L, 1)
    og = gdn_scan(*gdn_prep(q, k, v, beta, g))
    nw_row = jnp.tile(gdn_nw, GDN_HEADS).reshape(1, D_INNER)
    (on,) = make_rowwise(_f_gdn_post, "gdn_post", tm, 2, 1)((og, gz), (nw_row,))
    y = lin["gdn_out"](on)
    (x2,) = make_rowwise(_f_res, "res1", tm, 2, 2)((x1, y), (mods[1, 2], biases[1, 2]))

    (lt,) = make_rowwise(_f_loss, "loss", tm, 2, 1)((x2, tgt), (final_nw.reshape(1, D_MODEL),))
    return jnp.sum(lt)


MM_NAMES = ("s5_in_u", "s5_in_z", "s5_glu", "s5_out", "gdn_in_q", "gdn_in_k", "gdn_in_v", "gdn_in_z", "gdn_in_ba", "gdn_out")
SMALL_NAMES = ("ada_b", "norm_w", "s5_lambda_re", "s5_lambda_im", "s5_log_dt", "s5_b_re", "s5_b_im", "s5_c_re", "s5_c_im",
               "s5_d", "gdn_a_log", "gdn_dt_bias", "final_norm_w")
BIG_NAMES = ("s5_w_in", "s5_w_glu", "s5_w_out", "gdn_w_in", "gdn_w_out")
WEIGHT_ORDER = ("ada_w", "ada_b", "norm_w", "s5_w_in", "s5_lambda_re", "s5_lambda_im", "s5_log_dt", "s5_b_re", "s5_b_im",
                "s5_c_re", "s5_c_im", "s5_d", "s5_w_glu", "s5_w_out", "gdn_w_in", "gdn_conv_w", "gdn_a_log", "gdn_dt_bias",
                "gdn_norm_w", "gdn_w_out", "final_norm_w")


def _step(x, c, W, M, V, tgt):
    L = x.shape[1]
    ix, iy, ic = lax.axis_index("x"), lax.axis_index("y"), lax.axis_index("c")
    me = 4 * ix + 2 * iy + ic
    n_ada = W["ada_w"].shape[2]
    n_conv = W["gdn_conv_w"].shape[2]
    n_gnw = W["gdn_norm_w"].shape[1]

    g1 = _allgather_call(_pack([c, W["gdn_conv_w"], W["gdn_norm_w"]], SUBLANES), "gather_small_in", False)
    g1 = g1.reshape(N_DEV, -1)
    c_all = g1[:, :D_MODEL]
    conv_w = g1[:, D_MODEL:D_MODEL + GDN_CONV * n_conv].reshape(N_DEV, GDN_CONV, n_conv).transpose(1, 0, 2).reshape(GDN_CONV, -1)
    gdn_nw = g1[:, D_MODEL + GDN_CONV * n_conv:D_MODEL + GDN_CONV * n_conv + n_gnw].reshape(-1)
    mod_part = _ada_mod_call(c_all, W["ada_w"])
    g2 = _allgather_call(_pack([mod_part], SUBLANES), "gather_mod", False).reshape(N_DEV, -1)
    mod_all = g2[:, :2 * N_DEV * n_ada].reshape(N_DEV, 2, N_DEV, n_ada)
    mod_raw = lax.dynamic_index_in_dim(mod_all, me, axis=2, keepdims=False)
    mod_raw = mod_raw.transpose(1, 0, 2).reshape(2, 3 * D_MODEL)

    gathered = _gather_weights_call([W[n][0].astype(BF16) for n in BIG_NAMES], "gather_weights")
    full = dict(zip(BIG_NAMES, gathered))
    w_in5 = _join_cols_call(full["s5_w_in"], "join_s5_w_in")
    w_ing = _join_cols_call(full["gdn_w_in"], "join_gdn_w_in")
    w_ba = jnp.concatenate([w_ing[:, GDN_CONV_CH + D_INNER:], jnp.zeros((D_MODEL, LANES - 2 * GDN_HEADS), BF16)], axis=1)
    weights = (w_in5[:, :D_INNER], w_in5[:, D_INNER:], full["s5_w_glu"].reshape(D_INNER, D_INNER),
               full["s5_w_out"].reshape(D_INNER, D_MODEL),
               w_ing[:, :GDN_QK], w_ing[:, GDN_QK:2 * GDN_QK], w_ing[:, 2 * GDN_QK:GDN_CONV_CH],
               w_ing[:, GDN_CONV_CH:GDN_CONV_CH + D_INNER], w_ba, full["gdn_w_out"].reshape(D_INNER, D_MODEL))
    slots = tuple(jnp.zeros(w.shape, F32) for w in weights)
    diff = (x[0], mod_raw, W["norm_w"], W["s5_lambda_re"][0], W["s5_lambda_im"][0], W["s5_log_dt"][0], W["s5_b_re"][0],
            W["s5_b_im"][0], W["s5_c_re"][0], W["s5_c_im"][0], W["s5_d"][0], conv_w, W["gdn_a_log"][0], W["gdn_dt_bias"][0],
            gdn_nw, W["final_norm_w"], *slots)

    loss_local, grads = jax.value_and_grad(_local_loss)(diff, (tgt[0], W["ada_b"], weights), L)
    (dx, dmod, d_norm_w, d_lre, d_lim, d_logdt, d_bre, d_bim, d_cre, d_cim, d_s5d, d_conv, d_alog, d_dtb, d_gnw, d_fnw,
     d_wu, d_wz, d_wglu, d_wo5, d_wq, d_wk, d_wv, d_wgz, d_wba, d_wog) = grads
    loss = lax.psum(loss_local, MESH_AXES)

    d_in5 = _split_cols_call(jnp.concatenate([d_wu, d_wz], axis=1), "split_s5_w_in")
    d_ing = _split_cols_call(jnp.concatenate([d_wq, d_wk, d_wv, d_wgz, d_wba[:, :2 * GDN_HEADS]], axis=1), "split_gdn_w_in")
    rows = lambda d: d.reshape(N_DEV, d.shape[0] // N_DEV, d.shape[1])
    per_dev = [d_in5, rows(d_wglu), rows(d_wo5), d_ing, rows(d_wog)]
    own, got = _pair_exchange_call(per_dev, "scatter_grads_pair")
    pair = [_add_call(a, b, "pair_sum_" + n) for a, b, n in zip(own, got, BIG_NAMES)]
    recv = _chip_exchange_call(pair, "scatter_grads_chips")
    big = [_adam_call(r, W[n][0], M[n][0], V[n][0], "adam_" + n, rows=_tile(W[n].shape[1], 128)) for r, n in zip(recv, BIG_NAMES)]
    big = [[o[None] for o in outs] for outs in big]

    small_parts = [dmod, d_norm_w, d_lre, d_lim, d_logdt, d_bre, d_bim, d_cre, d_cim, d_s5d, d_alog, d_dtb, d_fnw, d_conv, d_gnw]
    small_shapes = [p.shape for p in small_parts]
    sg = _allgather_call(_pack(small_parts, ADAM_ROWS), "gather_small_grads", False)
    sg = sg.reshape(N_DEV, -1, LANES)
    tot = _unpack(_sum_call(sg, "sum_small_grads"), small_shapes)
    (g_adab, g_norm_w, g_lre, g_lim, g_logdt, g_bre, g_bim, g_cre, g_cim, g_s5d, g_alog, g_dtb, g_fnw, g_conv, g_gnw) = tot
    g_conv_mine = lax.dynamic_slice_in_dim(g_conv, me * n_conv, n_conv, axis=1)
    g_gnw_mine = lax.dynamic_slice_in_dim(g_gnw, me * n_gnw, n_gnw, axis=0)
    small_g = {"ada_b": g_adab.reshape(W["ada_b"].shape), "norm_w": g_norm_w, "s5_lambda_re": g_lre[None], "s5_lambda_im": g_lim[None],
               "s5_log_dt": g_logdt[None], "s5_b_re": g_bre[None], "s5_b_im": g_bim[None], "s5_c_re": g_cre[None],
               "s5_c_im": g_cim[None], "s5_d": g_s5d[None], "gdn_a_log": g_alog[None], "gdn_dt_bias": g_dtb[None],
               "final_norm_w": g_fnw, "gdn_conv_w": g_conv_mine[None], "gdn_norm_w": g_gnw_mine[None]}
    small_names = SMALL_NAMES + ("gdn_conv_w", "gdn_norm_w")
    small = _adam_call(_pack([small_g[n] for n in small_names], ADAM_ROWS)[None], _pack([W[n] for n in small_names], ADAM_ROWS),
                       _pack([M[n] for n in small_names], ADAM_ROWS), _pack([V[n] for n in small_names], ADAM_ROWS), "adam_small")
    small = [_unpack(b, [W[n].shape for n in small_names]) for b in small]

    dmod_all = sg.reshape(N_DEV, -1)[:, :2 * 3 * D_MODEL].reshape(N_DEV, 2, N_DEV, n_ada)
    dmod_mine = lax.dynamic_index_in_dim(dmod_all, me, axis=2, keepdims=False).transpose(1, 0, 2)
    g_ada_w = _ada_grad_call(c_all, dmod_mine)
    ada = _adam_call(g_ada_w.reshape(1, -1, LANES), W["ada_w"].reshape(-1, LANES), M["ada_w"].reshape(-1, LANES),
                     V["ada_w"].reshape(-1, LANES), "adam_ada")
    ada = [a.reshape(W["ada_w"].shape) for a in ada]

    res = {}
    for i, n in enumerate(BIG_NAMES):
        res[n] = big[i]
    for i, n in enumerate(small_names):
        res[n] = [b[i] for b in small]
    res["ada_w"] = ada
    outs = [loss, dx[None]]
    for j in range(4):
        outs += [res[n][j] for n in WEIGHT_ORDER]
    return tuple(outs)


def kernel(x, c, ada_w, ada_b, norm_w, s5_w_in, s5_lambda_re, s5_lambda_im, s5_log_dt, s5_b_re, s5_b_im, s5_c_re, s5_c_im, s5_d, s5_w_glu, s5_w_out, gdn_w_in, gdn_conv_w, gdn_a_log, gdn_dt_bias, gdn_norm_w, gdn_w_out, final_norm_w, loss_target, m_ada_w, m_ada_b, m_norm_w, m_s5_w_in, m_s5_lambda_re, m_s5_lambda_im, m_s5_log_dt, m_s5_b_re, m_s5_b_im, m_s5_c_re, m_s5_c_im, m_s5_d, m_s5_w_glu, m_s5_w_out, m_gdn_w_in, m_gdn_conv_w, m_gdn_a_log, m_gdn_dt_bias, m_gdn_norm_w, m_gdn_w_out, m_final_norm_w, v_ada_w, v_ada_b, v_norm_w, v_s5_w_in, v_s5_lambda_re, v_s5_lambda_im, v_s5_log_dt, v_s5_b_re, v_s5_b_im, v_s5_c_re, v_s5_c_im, v_s5_d, v_s5_w_glu, v_s5_w_out, v_gdn_w_in, v_gdn_conv_w, v_gdn_a_log, v_gdn_dt_bias, v_gdn_norm_w, v_gdn_w_out, v_final_norm_w):
    W = dict(ada_w=ada_w, ada_b=ada_b, norm_w=norm_w, s5_w_in=s5_w_in, s5_lambda_re=s5_lambda_re, s5_lambda_im=s5_lambda_im,
             s5_log_dt=s5_log_dt, s5_b_re=s5_b_re, s5_b_im=s5_b_im, s5_c_re=s5_c_re, s5_c_im=s5_c_im, s5_d=s5_d,
             s5_w_glu=s5_w_glu, s5_w_out=s5_w_out, gdn_w_in=gdn_w_in, gdn_conv_w=gdn_conv_w, gdn_a_log=gdn_a_log,
             gdn_dt_bias=gdn_dt_bias, gdn_norm_w=gdn_norm_w, gdn_w_out=gdn_w_out, final_norm_w=final_norm_w)
    M = dict(ada_w=m_ada_w, ada_b=m_ada_b, norm_w=m_norm_w, s5_w_in=m_s5_w_in, s5_lambda_re=m_s5_lambda_re,
             s5_lambda_im=m_s5_lambda_im, s5_log_dt=m_s5_log_dt, s5_b_re=m_s5_b_re, s5_b_im=m_s5_b_im, s5_c_re=m_s5_c_re,
             s5_c_im=m_s5_c_im, s5_d=m_s5_d, s5_w_glu=m_s5_w_glu, s5_w_out=m_s5_w_out, gdn_w_in=m_gdn_w_in,
             gdn_conv_w=m_gdn_conv_w, gdn_a_log=m_gdn_a_log, gdn_dt_bias=m_gdn_dt_bias, gdn_norm_w=m_gdn_norm_w,
             gdn_w_out=m_gdn_w_out, final_norm_w=m_final_norm_w)
    V = dict(ada_w=v_ada_w, ada_b=v_ada_b, norm_w=v_norm_w, s5_w_in=v_s5_w_in, s5_lambda_re=v_s5_lambda_re,
             s5_lambda_im=v_s5_lambda_im, s5_log_dt=v_s5_log_dt, s5_b_re=v_s5_b_re, s5_b_im=v_s5_b_im, s5_c_re=v_s5_c_re,
             s5_c_im=v_s5_c_im, s5_d=v_s5_d, s5_w_glu=v_s5_w_glu, s5_w_out=v_s5_w_out, gdn_w_in=v_gdn_w_in,
             gdn_conv_w=v_gdn_conv_w, gdn_a_log=v_gdn_a_log, gdn_dt_bias=v_gdn_dt_bias, gdn_norm_w=v_gdn_norm_w,
             gdn_w_out=v_gdn_w_out, final_norm_w=v_final_norm_w)
    return _step(x, c, W, M, V, loss_target)
```

```python
import functools
import math

import jax
import jax.numpy as jnp
from jax import lax
from jax.experimental import pallas as pl
from jax.experimental.pallas import tpu as pltpu

F32 = jnp.float32
BF16 = jnp.bfloat16
SDS = jax.ShapeDtypeStruct

D_MODEL = 1024
D_INNER = 2048
NORM_EPS = 1e-6
S5_GROUP = 16
S5_GROUPS = 128
S5_STATE = 64
GDN_HEADS = 8
GDN_DK = 128
GDN_DV = 256
GDN_CONV = 4
GDN_CHUNK = 64
GDN_QK = 1024
GDN_CONV_CH = 4096
GDN_PROJ = 6160
ADAM_LR = 0.001
ADAM_B1 = 0.9
ADAM_B2 = 0.999
ADAM_EPS = 1e-08
ADAM_WD = 0.01
ADAM_STEP = 10

N_DEV = 8
LANES = 128
SUBLANES = 8
VMEM_BIG = 56 << 20
VMEM_MID = 40 << 20
S5_GB = 8
S5_TL = 1024
MESH_AXES = ("x", "y", "c")


def _params(sem, vmem=None):
    return pltpu.CompilerParams(dimension_semantics=sem, vmem_limit_bytes=vmem)


def _bdot(a, b, dims=(((1,), (0,)), ((), ()))):
    return lax.dot_general(a.astype(BF16), b.astype(BF16), dims, preferred_element_type=F32)


def _hdot(a, b, dims=(((1,), (0,)), ((), ()))):
    a_hi = a.astype(BF16)
    b_hi = b.astype(BF16)
    a_lo = (a - a_hi.astype(F32)).astype(BF16)
    b_lo = (b - b_hi.astype(F32)).astype(BF16)
    dot = functools.partial(lax.dot_general, dimension_numbers=dims, preferred_element_type=F32)
    return dot(a_hi, b_hi) + (dot(a_hi, b_lo) + dot(a_lo, b_hi))


@jax.custom_vjp
def _hdot_bnn(a, b):
    return _hdot(a, b, (((2,), (1,)), ((0,), (0,))))


def _hdot_bnn_fwd(a, b):
    return _hdot_bnn(a, b), (a, b)


def _hdot_bnn_bwd(res, g):
    a, b = res
    return _hdot(g, b, (((2,), (2,)), ((0,), (0,)))), _hdot(a, g, (((1,), (1,)), ((0,), (0,))))


_hdot_bnn.defvjp(_hdot_bnn_fwd, _hdot_bnn_bwd)


NN = (((1,), (0,)), ((), ()))
NT = (((1,), (1,)), ((), ()))
TN = (((0,), (0,)), ((), ()))


def _tile(n, pref):
    for t in (pref, 512, 256, 128):
        if t <= n and n % t == 0:
            return t
    return n


def _matmul(a, b, mode, name):
    if mode == "nn":
        (m, k), (_, n) = a.shape, b.shape
    elif mode == "nt":
        (m, k), (n, _) = a.shape, b.shape
    else:
        (k, m), (_, n) = a.shape, b.shape
    tm, tn, tk = _tile(m, 512), _tile(n, 512), (k if k <= 2048 else _tile(k, 512))
    if mode == "tn":
        tm, tn = _tile(m, 1024), _tile(n, 1024)
    nk = k // tk
    dims = {"nn": NN, "nt": NT, "tn": TN}[mode]

    def body(a_ref, b_ref, o_ref, acc_ref):
        kk = pl.program_id(2)

        @pl.when(kk == 0)
        def _():
            acc_ref[...] = jnp.zeros_like(acc_ref)
        acc_ref[...] += _bdot(a_ref[...], b_ref[...], dims)

        @pl.when(kk == nk - 1)
        def _():
            o_ref[...] = acc_ref[...]

    a_spec = pl.BlockSpec((tk, tm), lambda i, j, q: (q, i)) if mode == "tn" else pl.BlockSpec((tm, tk), lambda i, j, q: (i, q))
    b_spec = pl.BlockSpec((tn, tk), lambda i, j, q: (j, q)) if mode == "nt" else pl.BlockSpec((tk, tn), lambda i, j, q: (q, j))
    return pl.pallas_call(
        body, name=name, grid=(m // tm, n // tn, nk),
        in_specs=[a_spec, b_spec], out_specs=pl.BlockSpec((tm, tn), lambda i, j, q: (i, j)),
        out_shape=SDS((m, n), F32), scratch_shapes=[pltpu.VMEM((tm, tn), F32)],
        compiler_params=_params(("parallel", "parallel", "arbitrary"), VMEM_MID),
    )(a, b)


def make_mm(name):
    @jax.custom_vjp
    def mm(a, w, grad_slot):
        return _matmul(a, w, "nn", name + "_fwd")

    def fwd(a, w, grad_slot):
        return _matmul(a, w, "nn", name + "_fwd"), (a, w)

    def bwd(res, g):
        a, w = res
        return _matmul(g, w, "nt", name + "_dx"), jnp.zeros_like(w), _matmul(a, g, "tn", name + "_dw")

    mm.defvjp(fwd, bwd)
    return mm


def make_rowwise(f, name, tm, n_rows, n_params, vmem=VMEM_MID):
    def specs_of(arrs, blocked):
        if blocked:
            return [pl.BlockSpec((tm, a.shape[1]), lambda i: (i, 0)) for a in arrs]
        return [pl.BlockSpec(a.shape, lambda i: (0, 0)) for a in arrs]

    def out_structs(rows, params):
        blk = [SDS((tm, r.shape[1]), r.dtype) for r in rows] + [SDS(p.shape, p.dtype) for p in params]
        return jax.eval_shape(f, *blk)

    def run_fwd(rows, params):
        L = rows[0].shape[0]
        outs = out_structs(rows, params)

        def body(*refs):
            ins = [r[...] for r in refs[:n_rows + n_params]]
            res = f(*ins)
            for o_ref, val in zip(refs[n_rows + n_params:], res):
                o_ref[...] = val

        return pl.pallas_call(
            body, name=name + "_fwd", grid=(L // tm,),
            in_specs=specs_of(rows, True) + specs_of(params, False),
            out_specs=[pl.BlockSpec((tm, o.shape[1]), lambda i: (i, 0)) for o in outs],
            out_shape=[SDS((L, o.shape[1]), o.dtype) for o in outs],
            compiler_params=_params(("parallel",), vmem),
        )(*rows, *params)

    def run_bwd(rows, params, gs):
        L = rows[0].shape[0]
        n_g = len(gs)

        def body(*refs):
            i = pl.program_id(0)
            ins = [r[...] for r in refs[:n_rows + n_params]]
            cts = tuple(r[...] for r in refs[n_rows + n_params:n_rows + n_params + n_g])
            outs = refs[n_rows + n_params + n_g:]
            _, vjp = jax.vjp(f, *ins)
            grads = vjp(cts)
            for o_ref, val in zip(outs[:n_rows], grads[:n_rows]):
                o_ref[...] = val

            if n_params:
                @pl.when(i == 0)
                def _():
                    for o_ref in outs[n_rows:]:
                        o_ref[...] = jnp.zeros_like(o_ref)
                for o_ref, val in zip(outs[n_rows:], grads[n_rows:]):
                    o_ref[...] += val

        res = pl.pallas_call(
            body, name=name + "_bwd", grid=(L // tm,),
            in_specs=specs_of(rows, True) + specs_of(params, False) + specs_of(gs, True),
            out_specs=specs_of(rows, True) + specs_of(params, False),
            out_shape=[SDS(r.shape, r.dtype) for r in rows] + [SDS(p.shape, p.dtype) for p in params],
            compiler_params=_params(("arbitrary",), vmem),
        )(*rows, *params, *gs)
        return tuple(res[:n_rows]), tuple(res[n_rows:])

    @jax.custom_vjp
    def op(rows, params):
        return tuple(run_fwd(rows, params))

    def fwd(rows, params):
        return tuple(run_fwd(rows, params)), (rows, params)

    def bwd(res, gs):
        rows, params = res
        return run_bwd(rows, params, tuple(gs))

    op.defvjp(fwd, bwd)
    return op


def _s5_scan_rows(xr_ref, xi_ref, ar, ai, x0r, x0i, tl):
    def step(t8, carry):
        xr, xi = carry
        base = pl.multiple_of(t8 * SUBLANES, SUBLANES)
        for r in range(SUBLANES):
            br = xr_ref[pl.ds(base + r, 1), :]
            bi = xi_ref[pl.ds(base + r, 1), :]
            nr = ar * xr - ai * xi + br
            ni = ar * xi + ai * xr + bi
            xr, xi = nr, ni
            xr_ref[pl.ds(base + r, 1), :] = xr
            xi_ref[pl.ds(base + r, 1), :] = xi
        return xr, xi
    return lax.fori_loop(0, tl // SUBLANES, step, (x0r, x0i))


def _s5_fwd_call(u, bre, bim, cre, cim, a, tl):
    L, e = u.shape
    nb = e // LANES
    ns = bre.shape[2]
    nc = L // tl

    def body(u_ref, bre_ref, bim_ref, cre_ref, cim_ref, a_ref, ys_ref, xb_ref, xr_ref, xi_ref, carry_ref):
        c = pl.program_id(1)

        @pl.when(c == 0)
        def _():
            carry_ref[...] = jnp.zeros_like(carry_ref)
        xb_ref[0, 0] = carry_ref[...]
        ub = u_ref[...]
        xr_ref[...] = _bdot(ub, bre_ref[0])
        xi_ref[...] = _bdot(ub, bim_ref[0])
        ar = a_ref[0, 0:1, :]
        ai = a_ref[0, 1:2, :]
        xr, xi = _s5_scan_rows(xr_ref, xi_ref, ar, ai, carry_ref[0:1, :], carry_ref[1:2, :], tl)
        carry_ref[0:1, :] = xr
        carry_ref[1:2, :] = xi
        ys_ref[...] = _bdot(xr_ref[...], cre_ref[0]) - _bdot(xi_ref[...], cim_ref[0])

    return pl.pallas_call(
        body, name="s5_core_fwd", grid=(nb, nc),
        in_specs=[pl.BlockSpec((tl, LANES), lambda j, c: (c, j)),
                  pl.BlockSpec((1, LANES, ns), lambda j, c: (j, 0, 0)), pl.BlockSpec((1, LANES, ns), lambda j, c: (j, 0, 0)),
                  pl.BlockSpec((1, ns, LANES), lambda j, c: (j, 0, 0)), pl.BlockSpec((1, ns, LANES), lambda j, c: (j, 0, 0)),
                  pl.BlockSpec((1, SUBLANES, ns), lambda j, c: (j, 0, 0))],
        out_specs=[pl.BlockSpec((tl, LANES), lambda j, c: (c, j)),
                   pl.BlockSpec((1, 1, SUBLANES, ns), lambda j, c: (j, c, 0, 0))],
        out_shape=[SDS((L, e), F32), SDS((nb, nc, SUBLANES, ns), F32)],
        scratch_shapes=[pltpu.VMEM((tl, ns), F32), pltpu.VMEM((tl, ns), F32), pltpu.VMEM((SUBLANES, ns), F32)],
        compiler_params=_params(("arbitrary", "arbitrary"), VMEM_MID),
    )(u, bre, bim, cre, cim, a)


def _s5_bwd_call(u, dys, bre, bim, cre, cim, a, xb, tl):
    L, e = u.shape
    nb = e // LANES
    ns = bre.shape[2]
    nc = L // tl

    def body(u_ref, dys_ref, bre_ref, bim_ref, cre_ref, cim_ref, a_ref, xb_ref,
             du_ref, dbre_ref, dbim_ref, dcre_ref, dcim_ref, da_ref,
             xr_ref, xi_ref, gr_ref, gi_ref, gcarry_ref):
        c = pl.program_id(1)

        @pl.when(c == 0)
        def _():
            gcarry_ref[...] = jnp.zeros_like(gcarry_ref)
            dbre_ref[...] = jnp.zeros_like(dbre_ref)
            dbim_ref[...] = jnp.zeros_like(dbim_ref)
            dcre_ref[...] = jnp.zeros_like(dcre_ref)
            dcim_ref[...] = jnp.zeros_like(dcim_ref)
            da_ref[...] = jnp.zeros_like(da_ref)

        ub = u_ref[...]
        dy = dys_ref[...]
        ar = a_ref[0, 0:1, :]
        ai = a_ref[0, 1:2, :]
        x0r = xb_ref[0, 0, 0:1, :]
        x0i = xb_ref[0, 0, 1:2, :]
        xr_ref[...] = _bdot(ub, bre_ref[0])
        xi_ref[...] = _bdot(ub, bim_ref[0])
        _s5_scan_rows(xr_ref, xi_ref, ar, ai, x0r, x0i, tl)
        dcre_ref[0] += _bdot(xr_ref[...], dy, TN)
        dcim_ref[0] -= _bdot(xi_ref[...], dy, TN)
        gr_ref[...] = _bdot(dy, cre_ref[0], NT)
        gi_ref[...] = -_bdot(dy, cim_ref[0], NT)

        def step(t8, carry):
            cr, ci = carry
            base = pl.multiple_of((tl // SUBLANES - 1 - t8) * SUBLANES, SUBLANES)
            for r in range(SUBLANES - 1, -1, -1):
                gr = gr_ref[pl.ds(base + r, 1), :] + cr
                gi = gi_ref[pl.ds(base + r, 1), :] + ci
                gr_ref[pl.ds(base + r, 1), :] = gr
                gi_ref[pl.ds(base + r, 1), :] = gi
                cr = ar * gr + ai * gi
                ci = ar * gi - ai * gr
            return cr, ci
        cr, ci = lax.fori_loop(0, tl // SUBLANES, step, (gcarry_ref[0:1, :], gcarry_ref[1:2, :]))
        gcarry_ref[0:1, :] = cr
        gcarry_ref[1:2, :] = ci
        row = lax.broadcasted_iota(jnp.int32, (tl, ns), 0)
        gr = gr_ref[...]
        gi = gi_ref[...]
        xpr = jnp.where(row == 0, x0r, pltpu.roll(xr_ref[...], 1, 0))
        xpi = jnp.where(row == 0, x0i, pltpu.roll(xi_ref[...], 1, 0))
        da_ref[0, 0:1, :] += jnp.sum(gr * xpr + gi * xpi, axis=0, keepdims=True)
        da_ref[0, 1:2, :] += jnp.sum(gi * xpr - gr * xpi, axis=0, keepdims=True)
        du_ref[...] = _bdot(gr, bre_ref[0], NT) + _bdot(gi, bim_ref[0], NT)
        dbre_ref[0] += _bdot(ub, gr, TN)
        dbim_ref[0] += _bdot(ub, gi, TN)

    rev = lambda c: nc - 1 - c
    return pl.pallas_call(
        body, name="s5_core_bwd", grid=(nb, nc),
        in_specs=[pl.BlockSpec((tl, LANES), lambda j, c: (rev(c), j)), pl.BlockSpec((tl, LANES), lambda j, c: (rev(c), j)),
                  pl.BlockSpec((1, LANES, ns), lambda j, c: (j, 0, 0)), pl.BlockSpec((1, LANES, ns), lambda j, c: (j, 0, 0)),
                  pl.BlockSpec((1, ns, LANES), lambda j, c: (j, 0, 0)), pl.BlockSpec((1, ns, LANES), lambda j, c: (j, 0, 0)),
                  pl.BlockSpec((1, SUBLANES, ns), lambda j, c: (j, 0, 0)),
                  pl.BlockSpec((1, 1, SUBLANES, ns), lambda j, c: (j, rev(c), 0, 0))],
        out_specs=[pl.BlockSpec((tl, LANES), lambda j, c: (rev(c), j)),
                   pl.BlockSpec((1, LANES, ns), lambda j, c: (j, 0, 0)), pl.BlockSpec((1, LANES, ns), lambda j, c: (j, 0, 0)),
                   pl.BlockSpec((1, ns, LANES), lambda j, c: (j, 0, 0)), pl.BlockSpec((1, ns, LANES), lambda j, c: (j, 0, 0)),
                   pl.BlockSpec((1, SUBLANES, ns), lambda j, c: (j, 0, 0))],
        out_shape=[SDS((L, e), F32), SDS(bre.shape, F32), SDS(bim.shape, F32), SDS(cre.shape, F32), SDS(cim.shape, F32),
                   SDS(a.shape, F32)],
        scratch_shapes=[pltpu.VMEM((tl, ns), F32) for _ in range(4)] + [pltpu.VMEM((SUBLANES, ns), F32)],
        compiler_params=_params(("arbitrary", "arbitrary"), VMEM_MID),
    )(u, dys, bre, bim, cre, cim, a, xb)


def make_s5_core(tl):
    @jax.custom_vjp
    def s5_core(u, bre, bim, cre, cim, a):
        return _s5_fwd_call(u, bre, bim, cre, cim, a, tl)[0]

    def fwd(u, bre, bim, cre, cim, a):
        ys, xb = _s5_fwd_call(u, bre, bim, cre, cim, a, tl)
        return ys, (u, bre, bim, cre, cim, a, xb)

    def bwd(res, dys):
        u, bre, bim, cre, cim, a, xb = res
        return tuple(_s5_bwd_call(u, dys, bre, bim, cre, cim, a, xb, tl))

    s5_core.defvjp(fwd, bwd)
    return s5_core


def _s5_block_params(lam_re, lam_im, log_dt, b_re, b_im, c_re, c_im):
    dt = jnp.exp(log_dt)[:, None]
    mag = jnp.exp(lam_re * dt)
    ab_re = mag * jnp.cos(lam_im * dt)
    ab_im = mag * jnp.sin(lam_im * dt)
    den = lam_re * lam_re + lam_im * lam_im
    nr = ab_re - 1.0
    ni = ab_im
    q_re = (nr * lam_re + ni * lam_im) / den
    q_im = (ni * lam_re - nr * lam_im) / den
    bb_re = q_re[..., None] * b_re - q_im[..., None] * b_im
    bb_im = q_re[..., None] * b_im + q_im[..., None] * b_re
    nb = S5_GROUPS // S5_GB
    eye = jnp.eye(S5_GB, dtype=F32)

    def bdiag_in(bb):
        t = bb.reshape(nb, S5_GB, S5_STATE, S5_GROUP)
        t = jnp.einsum("jgpm,gh->jgmhp", t, eye)
        return t.reshape(nb, S5_GB * S5_GROUP, S5_GB * S5_STATE)

    def bdiag_out(cc):
        t = cc.reshape(nb, S5_GB, S5_GROUP, S5_STATE)
        t = jnp.einsum("jgmp,gh->jgphm", t, eye)
        return t.reshape(nb, S5_GB * S5_STATE, S5_GB * S5_GROUP)

    a = jnp.stack([ab_re.reshape(nb, S5_GB * S5_STATE), ab_im.reshape(nb, S5_GB * S5_STATE)], axis=1)
    a = jnp.concatenate([a, jnp.zeros((nb, SUBLANES - 2, S5_GB * S5_STATE), F32)], axis=1)
    return bdiag_in(bb_re), bdiag_in(bb_im), bdiag_out(c_re), bdiag_out(c_im), a


def _shift_down(x, s, row):
    if s == 0:
        return x
    return jnp.where(row >= s, pltpu.roll(x, s, 0), 0.0)


def _shift_up(x, s, row, n):
    if s == 0:
        return x
    return jnp.where(row < n - s, pltpu.roll(x, n - s, 0), 0.0)


def _conv_fwd_call(x, w):
    L, ch = x.shape

    def body(x_ref, w_ref, y_ref):
        xv = x_ref[...]
        row = lax.broadcasted_iota(jnp.int32, xv.shape, 0)
        acc = jnp.zeros_like(xv)
        for j in range(GDN_CONV):
            acc += w_ref[j:j + 1, :] * _shift_down(xv, GDN_CONV - 1 - j, row)
        y_ref[...] = acc

    return pl.pallas_call(
        body, name="gdn_conv_fwd", grid=(ch // LANES,),
        in_specs=[pl.BlockSpec((L, LANES), lambda j: (0, j)), pl.BlockSpec((SUBLANES, LANES), lambda j: (0, j))],
        out_specs=pl.BlockSpec((L, LANES), lambda j: (0, j)), out_shape=SDS((L, ch), F32),
        compiler_params=_params(("parallel",), VMEM_MID),
    )(x, w)


def _conv_bwd_call(x, w, dy):
    L, ch = x.shape

    def body(x_ref, w_ref, dy_ref, dx_ref, dw_ref):
        xv = x_ref[...]
        g = dy_ref[...]
        row = lax.broadcasted_iota(jnp.int32, xv.shape, 0)
        acc = jnp.zeros_like(xv)
        dws = []
        for j in range(GDN_CONV):
            s = GDN_CONV - 1 - j
            acc += w_ref[j:j + 1, :] * _shift_up(g, s, row, L)
            dws.append(jnp.sum(g * _shift_down(xv, s, row), axis=0, keepdims=True))
        dx_ref[...] = acc
        dw_ref[...] = jnp.concatenate(dws + [jnp.zeros((SUBLANES - GDN_CONV, LANES), F32)], axis=0)

    return pl.pallas_call(
        body, name="gdn_conv_bwd", grid=(ch // LANES,),
        in_specs=[pl.BlockSpec((L, LANES), lambda j: (0, j)), pl.BlockSpec((SUBLANES, LANES), lambda j: (0, j)),
                  pl.BlockSpec((L, LANES), lambda j: (0, j))],
        out_specs=[pl.BlockSpec((L, LANES), lambda j: (0, j)), pl.BlockSpec((SUBLANES, LANES), lambda j: (0, j))],
        out_shape=[SDS((L, ch), F32), SDS((SUBLANES, ch), F32)],
        compiler_params=_params(("parallel",), VMEM_MID),
    )(x, w, dy)


@jax.custom_vjp
def gdn_conv(x, w):
    return _conv_fwd_call(x, w)


def _gdn_conv_f(x, w):
    return _conv_fwd_call(x, w), (x, w)


def _gdn_conv_b(res, dy):
    x, w = res
    return tuple(_conv_bwd_call(x, w, dy))


gdn_conv.defvjp(_gdn_conv_f, _gdn_conv_b)


BNN = (((2,), (1,)), ((0,), (0,)))
BNT = (((2,), (2,)), ((0,), (0,)))
BTN = (((1,), (1,)), ((0,), (0,)))
GDN_PREP_BATCH = 8


def _gdn_prep_math(q, k, v, beta, g):
    B, C = q.shape[0], q.shape[1]
    ri = lax.broadcasted_iota(jnp.int32, (B, C, C), 1)
    ci = lax.broadcasted_iota(jnp.int32, (B, C, C), 2)
    causal = ri >= ci
    strict = ri > ci
    eye = (ri == ci).astype(F32)
    gb = jnp.broadcast_to(g, (B, C, C))
    g_row = jnp.sum(gb * eye, axis=1, keepdims=True)
    gc_col = jnp.sum(jnp.where(causal, jnp.broadcast_to(g_row, (B, C, C)), 0.0), axis=2, keepdims=True)
    gc_row = jnp.sum(jnp.where(ri <= ci, gb, 0.0), axis=1, keepdims=True)
    decay = jnp.exp(jnp.where(causal, gc_col - gc_row, -jnp.inf))
    kk = _bdot(k, k, BNT)
    a_mat = jnp.where(strict, beta * kk * decay, 0.0)
    n = -a_mat
    t = eye + n
    for _ in range(int(math.log2(C)) - 1):
        n = _hdot_bnn(n, n)
        t = t + _hdot_bnn(t, n)
    e_gc = jnp.exp(gc_col)
    w = _hdot_bnn(t, beta * e_gc * k)
    u = _hdot_bnn(t, beta * v)
    qk = _bdot(q, k, BNT) * decay
    q_dec = q * e_gc
    g_last = gc_col[:, C - 1:C, :]
    k_dec = k * jnp.exp(g_last - gc_col)
    return q_dec, w, u, qk, k_dec, gc_col


def _gdn_prep_specs(L):
    C = GDN_CHUNK
    nb = min(GDN_PREP_BATCH, L // C)
    R = nb * C
    ins = [pl.BlockSpec((R, GDN_DK), lambda h, c: (c, h)), pl.BlockSpec((R, GDN_DK), lambda h, c: (c, h)),
           pl.BlockSpec((R, GDN_DV), lambda h, c: (c, h)),
           pl.BlockSpec((1, R, 1), lambda h, c: (h, c, 0)), pl.BlockSpec((1, R, 1), lambda h, c: (h, c, 0))]
    outs = [pl.BlockSpec((1, R, GDN_DK), lambda h, c: (h, c, 0)), pl.BlockSpec((1, R, GDN_DK), lambda h, c: (h, c, 0)),
            pl.BlockSpec((1, R, GDN_DV), lambda h, c: (h, c, 0)), pl.BlockSpec((1, R, C), lambda h, c: (h, c, 0)),
            pl.BlockSpec((1, R, GDN_DK), lambda h, c: (h, c, 0)), pl.BlockSpec((1, R, 1), lambda h, c: (h, c, 0))]
    shapes = [SDS((GDN_HEADS, L, GDN_DK), F32), SDS((GDN_HEADS, L, GDN_DK), F32), SDS((GDN_HEADS, L, GDN_DV), F32),
              SDS((GDN_HEADS, L, C), F32), SDS((GDN_HEADS, L, GDN_DK), F32), SDS((GDN_HEADS, L, 1), F32)]
    return ins, outs, shapes, nb


def _chunks(x, nb):
    return x.reshape(nb, x.shape[0] // nb, x.shape[1])


def _gdn_prep_fwd_call(q, k, v, beta, g):
    L = q.shape[0]
    ins, outs, shapes, nb = _gdn_prep_specs(L)

    def body(q_ref, k_ref, v_ref, b_ref, g_ref, *o_refs):
        res = _gdn_prep_math(_chunks(q_ref[...], nb), _chunks(k_ref[...], nb), _chunks(v_ref[...], nb),
                             _chunks(b_ref[0], nb), _chunks(g_ref[0], nb))
        for o_ref, val in zip(o_refs, res):
            o_ref[0] = val.reshape(val.shape[0] * val.shape[1], val.shape[2])

    return pl.pallas_call(
        body, name="gdn_prep_fwd", grid=(GDN_HEADS, L // (nb * GDN_CHUNK)), in_specs=ins, out_specs=outs, out_shape=shapes,
        compiler_params=_params(("parallel", "parallel"), VMEM_MID),
    )(q, k, v, beta, g)


def _gdn_prep_bwd_call(q, k, v, beta, g, cts):
    L = q.shape[0]
    ins, outs, _, nb = _gdn_prep_specs(L)

    def body(q_ref, k_ref, v_ref, b_ref, g_ref, c0, c1, c2, c3, c4, c5, dq_ref, dk_ref, dv_ref, db_ref, dg_ref):
        _, vjp = jax.vjp(_gdn_prep_math, _chunks(q_ref[...], nb), _chunks(k_ref[...], nb), _chunks(v_ref[...], nb),
                         _chunks(b_ref[0], nb), _chunks(g_ref[0], nb))
        dq, dk, dv, db, dg = vjp(tuple(_chunks(c[0], nb) for c in (c0, c1, c2, c3, c4, c5)))
        flat = lambda t: t.reshape(t.shape[0] * t.shape[1], t.shape[2])
        dq_ref[...] = flat(dq)
        dk_ref[...] = flat(dk)
        dv_ref[...] = flat(dv)
        db_ref[0] = flat(db)
        dg_ref[0] = flat(dg)

    return pl.pallas_call(
        body, name="gdn_prep_bwd", grid=(GDN_HEADS, L // (nb * GDN_CHUNK)), in_specs=ins + outs, out_specs=ins,
        out_shape=[SDS(q.shape, F32), SDS(k.shape, F32), SDS(v.shape, F32), SDS(beta.shape, F32), SDS(g.shape, F32)],
        compiler_params=_params(("parallel", "parallel"), VMEM_MID),
    )(q, k, v, beta, g, *cts)


@jax.custom_vjp
def gdn_prep(q, k, v, beta, g):
    return tuple(_gdn_prep_fwd_call(q, k, v, beta, g))


def _gdn_prep_f(q, k, v, beta, g):
    return tuple(_gdn_prep_fwd_call(q, k, v, beta, g)), (q, k, v, beta, g)


def _gdn_prep_b(res, cts):
    return tuple(_gdn_prep_bwd_call(*res, tuple(cts)))


gdn_prep.defvjp(_gdn_prep_f, _gdn_prep_b)


def _gdn_step_math(q_dec, w, u, qk, k_dec, gc, state):
    H, C = q_dec.shape[0], q_dec.shape[1]
    v_new = u - _bdot(w, state, BNN)
    o = _bdot(q_dec, state, BNN) + _bdot(qk, v_new, BNN)
    gl = gc[:, C - 1:C, :]
    new_state = jnp.exp(gl) * state + _bdot(k_dec, v_new, BTN)
    return jnp.concatenate([o[h] for h in range(H)], axis=1), new_state


def _gdn_scan_specs(L, rev):
    C, H = GDN_CHUNK, GDN_HEADS
    nc = L // C
    cc = (lambda c: nc - 1 - c) if rev else (lambda c: c)
    ins = [pl.BlockSpec((H, C, GDN_DK), lambda c: (0, cc(c), 0)), pl.BlockSpec((H, C, GDN_DK), lambda c: (0, cc(c), 0)),
           pl.BlockSpec((H, C, GDN_DV), lambda c: (0, cc(c), 0)), pl.BlockSpec((H, C, C), lambda c: (0, cc(c), 0)),
           pl.BlockSpec((H, C, GDN_DK), lambda c: (0, cc(c), 0)), pl.BlockSpec((H, C, 1), lambda c: (0, cc(c), 0))]
    o_spec = pl.BlockSpec((C, H * GDN_DV), lambda c: (cc(c), 0))
    s_spec = pl.BlockSpec((1, H, GDN_DK, GDN_DV), lambda c: (cc(c), 0, 0, 0))
    return ins, o_spec, s_spec, nc


def _gdn_scan_fwd_call(q_dec, w, u, qk, k_dec, gc):
    L = q_dec.shape[1]
    ins, o_spec, s_spec, nc = _gdn_scan_specs(L, False)

    def body(qd_ref, w_ref, u_ref, qk_ref, kd_ref, gc_ref, o_ref, sin_ref, s_ref):
        c = pl.program_id(0)

        @pl.when(c == 0)
        def _():
            s_ref[...] = jnp.zeros_like(s_ref)
        st = s_ref[...]
        sin_ref[0] = st
        o, ns = _gdn_step_math(qd_ref[...], w_ref[...], u_ref[...], qk_ref[...], kd_ref[...], gc_ref[...], st)
        o_ref[...] = o
        s_ref[...] = ns

    return pl.pallas_call(
        body, name="gdn_scan_fwd", grid=(nc,), in_specs=ins, out_specs=[o_spec, s_spec],
        out_shape=[SDS((L, GDN_HEADS * GDN_DV), F32), SDS((nc, GDN_HEADS, GDN_DK, GDN_DV), F32)],
        scratch_shapes=[pltpu.VMEM((GDN_HEADS, GDN_DK, GDN_DV), F32)],
        compiler_params=_params(("arbitrary",), VMEM_MID),
    )(q_dec, w, u, qk, k_dec, gc)


def _gdn_scan_bwd_call(q_dec, w, u, qk, k_dec, gc, s_in, do):
    L = q_dec.shape[1]
    ins, o_spec, s_spec, nc = _gdn_scan_specs(L, True)

    def body(qd_ref, w_ref, u_ref, qk_ref, kd_ref, gc_ref, sin_ref, do_ref,
             dqd_ref, dw_ref, du_ref, dqk_ref, dkd_ref, dgc_ref, ds_ref):
        c = pl.program_id(0)

        @pl.when(c == 0)
        def _():
            ds_ref[...] = jnp.zeros_like(ds_ref)
        _, vjp = jax.vjp(_gdn_step_math, qd_ref[...], w_ref[...], u_ref[...], qk_ref[...], kd_ref[...], gc_ref[...], sin_ref[0])
        dqd, dw, du, dqk, dkd, dgc, dst = vjp((do_ref[...], ds_ref[...]))
        dqd_ref[...] = dqd
        dw_ref[...] = dw
        du_ref[...] = du
        dqk_ref[...] = dqk
        dkd_ref[...] = dkd
        dgc_ref[...] = dgc
        ds_ref[...] = dst

    return pl.pallas_call(
        body, name="gdn_scan_bwd", grid=(nc,), in_specs=ins + [s_spec, o_spec], out_specs=ins,
        out_shape=[SDS(t.shape, F32) for t in (q_dec, w, u, qk, k_dec, gc)],
        scratch_shapes=[pltpu.VMEM((GDN_HEADS, GDN_DK, GDN_DV), F32)],
        compiler_params=_params(("arbitrary",), VMEM_MID),
    )(q_dec, w, u, qk, k_dec, gc, s_in, do)


@jax.custom_vjp
def gdn_scan(q_dec, w, u, qk, k_dec, gc):
    return _gdn_scan_fwd_call(q_dec, w, u, qk, k_dec, gc)[0]


def _gdn_scan_f(*args):
    o, s_in = _gdn_scan_fwd_call(*args)
    return o, (*args, s_in)


def _gdn_scan_b(res, do):
    return tuple(_gdn_scan_bwd_call(*res, do))


gdn_scan.defvjp(_gdn_scan_f, _gdn_scan_b)


def _silu(x):
    return x * jax.nn.sigmoid(x)


def _gelu_tanh(x):
    return 0.5 * x * (1.0 + jnp.tanh(math.sqrt(2.0 / math.pi) * (x + 0.044715 * (x * x * x))))


def _f_lnmod(x, nw, sc, sh, bsc, bsh):
    xn = x * lax.rsqrt(jnp.mean(x * x, axis=-1, keepdims=True) + NORM_EPS) * nw
    return (xn * (1.0 + (sc + bsc)) + (sh + bsh),)


def _f_s5_act(ys, u, d):
    return (_gelu_tanh(ys + d * u),)


def _f_s5_gate(y2, t, z):
    return (y2 * jax.nn.sigmoid(t) * _silu(z),)


def _f_res(x, y, gate, bgate):
    return (x + (gate + bgate) * y,)


def _heads(x, width, fn):
    return jnp.concatenate([fn(x[:, i * width:(i + 1) * width]) for i in range(x.shape[1] // width)], axis=1)


def _l2n(x):
    return x * lax.rsqrt(jnp.sum(x * x, axis=-1, keepdims=True) + NORM_EPS)


def _f_qnorm(x):
    return (_heads(_silu(x), GDN_DK, _l2n) * (GDN_DK ** -0.5),)


def _f_knorm(x):
    return (_heads(_silu(x), GDN_DK, _l2n),)


def _f_vact(x):
    return (_silu(x),)


def _f_betag(ba, alog, dtb):
    col = lax.broadcasted_iota(jnp.int32, ba.shape, 1)
    t = ba + dtb
    softplus = jnp.maximum(t, 0.0) + jnp.log1p(jnp.exp(-jnp.abs(t)))
    g = -jnp.exp(alog) * softplus
    return (jnp.where(col < GDN_HEADS, jax.nn.sigmoid(ba), jnp.where(col < 2 * GDN_HEADS, g, 0.0)),)


def _f_gdn_post(o, z, nw):
    on = _heads(o, GDN_DV, lambda t: t * lax.rsqrt(jnp.mean(t * t, axis=-1, keepdims=True) + NORM_EPS))
    return (on * nw * _silu(z),)


def _f_loss(x, tgt, fw):
    y = x * lax.rsqrt(jnp.mean(x * x, axis=-1, keepdims=True) + NORM_EPS) * fw
    err = y - tgt
    return (0.5 * jnp.mean(err * err, axis=-1, keepdims=True),)


def _ada_mod_call(c_all, ada_w):
    n = ada_w.shape[2]

    def body(c_ref, w_ref, o_ref):
        ca = _silu(c_ref[...])
        for l in range(ada_w.shape[0]):
            o_ref[l] = _bdot(ca, w_ref[l])

    return pl.pallas_call(body, name="ada_mod", out_shape=SDS((ada_w.shape[0], N_DEV, n), F32),
                          compiler_params=_params(None, VMEM_MID))(c_all, ada_w)


def _ada_grad_call(c_all, dmod):
    nl, _, n = dmod.shape

    def body(c_ref, d_ref, o_ref):
        ca = _silu(c_ref[...])
        for l in range(nl):
            o_ref[l] = _hdot(ca, d_ref[l], TN)

    return pl.pallas_call(body, name="ada_grad", out_shape=SDS((nl, c_all.shape[1], n), F32),
                          compiler_params=_params(None, VMEM_MID))(c_all, dmod)


ADAM_ROWS = 512


def _adamw(g, w, m, v):
    m2 = ADAM_B1 * m + (1.0 - ADAM_B1) * g
    v2 = ADAM_B2 * v + (1.0 - ADAM_B2) * (g * g)
    m_hat = m2 / (1.0 - ADAM_B1 ** ADAM_STEP)
    v_hat = v2 / (1.0 - ADAM_B2 ** ADAM_STEP)
    return g, -ADAM_LR * (m_hat / (jnp.sqrt(v_hat) + ADAM_EPS) + ADAM_WD * w), m2, v2


def _adam_call(gs, w, m, v, name, rows=None):
    n, r, cols = gs.shape
    rows = rows or ADAM_ROWS

    def body(g_ref, w_ref, m_ref, v_ref, go_ref, d_ref, mo_ref, vo_ref):
        g = g_ref[0]
        for s in range(1, n):
            g = g + g_ref[s]
        for o_ref, val in zip((go_ref, d_ref, mo_ref, vo_ref), _adamw(g, w_ref[...], m_ref[...], v_ref[...])):
            o_ref[...] = val

    blk = pl.BlockSpec((rows, cols), lambda i: (i, 0))
    return pl.pallas_call(
        body, name=name, grid=(r // rows,),
        in_specs=[pl.BlockSpec((n, rows, cols), lambda i: (0, i, 0)), blk, blk, blk],
        out_specs=[blk, blk, blk, blk], out_shape=[SDS((r, cols), F32)] * 4,
        compiler_params=_params(("parallel",), VMEM_MID),
    )(gs, w, m, v)


def _sum_call(gs, name):
    n, r, _ = gs.shape

    def body(g_ref, o_ref):
        g = g_ref[0]
        for s in range(1, n):
            g = g + g_ref[s]
        o_ref[...] = g

    return pl.pallas_call(
        body, name=name, grid=(r // ADAM_ROWS,),
        in_specs=[pl.BlockSpec((n, ADAM_ROWS, LANES), lambda i: (0, i, 0))],
        out_specs=pl.BlockSpec((ADAM_ROWS, LANES), lambda i: (i, 0)), out_shape=SDS((r, LANES), F32),
        compiler_params=_params(("parallel",), VMEM_MID),
    )(gs)


def _allgather_call(x_shard, name, in_hbm):
    m_per, n = x_shard.shape

    def body(x_ref, out_ref, send_sems, recv_sems, local_sem):
        x, y, c = lax.axis_index("x"), lax.axis_index("y"), lax.axis_index("c")
        me, sibling = (x, y, c), (x, y, 1 - c)
        chips = [(1 - x, y), (x, 1 - y), (1 - x, 1 - y)]

        def rows(px, py, pc):
            return out_ref.at[pl.ds((4 * px + 2 * py + pc) * m_per, m_per), :]

        def copy(k, block, to, src=None):
            return pltpu.make_async_remote_copy(
                src_ref=rows(*block) if src is None else src, dst_ref=rows(*block),
                send_sem=send_sems.at[k], recv_sem=recv_sems.at[k], device_id=to, device_id_type=pl.DeviceIdType.MESH)

        mine = pltpu.make_async_copy(x_ref, rows(*me), local_sem)
        mine.start()
        first = [copy(0, me, sibling, src=x_ref)]
        first += [copy(1 + j, me, (*chip, c), src=x_ref) for j, chip in enumerate(chips)]
        for cp in first:
            cp.start()
        passed = [copy(4 + j, (*chip, c), sibling) for j, chip in enumerate(chips)]
        for j, chip in enumerate(chips):
            copy(1 + j, (*chip, c), me).wait_recv()
            passed[j].start()
        copy(0, sibling, me).wait_recv()
        for j, chip in enumerate(chips):
            copy(4 + j, (*chip, 1 - c), me).wait_recv()
        for cp in first + passed:
            cp.wait_send()
        mine.wait()

    space = pl.ANY if in_hbm else pltpu.VMEM
    return pl.pallas_call(
        body, name=name, out_shape=SDS((N_DEV * m_per, n), x_shard.dtype),
        in_specs=[pl.BlockSpec(memory_space=space)], out_specs=pl.BlockSpec(memory_space=space),
        scratch_shapes=[pltpu.SemaphoreType.DMA((7,)), pltpu.SemaphoreType.DMA((7,)), pltpu.SemaphoreType.DMA],
        compiler_params=_params(None, None if in_hbm else VMEM_BIG),
    )(x_shard)


def _gather_weights_call(shards, name):
    nw = len(shards)

    def body(*refs):
        x_refs, out_refs = refs[:nw], refs[nw:2 * nw]
        send_sems, recv_sems, local_sems = refs[2 * nw:]
        x, y, c = lax.axis_index("x"), lax.axis_index("y"), lax.axis_index("c")
        me, sibling = (x, y, c), (x, y, 1 - c)
        chips = [(1 - x, y), (x, 1 - y), (1 - x, 1 - y)]

        def slot(w, px, py, pc):
            return out_refs[w].at[4 * px + 2 * py + pc]

        def copy(w, k, block, to, src=None):
            dst = slot(w, *block)
            return pltpu.make_async_remote_copy(
                src_ref=dst if src is None else src, dst_ref=dst, send_sem=send_sems.at[7 * w + k],
                recv_sem=recv_sems.at[7 * w + k], device_id=to, device_id_type=pl.DeviceIdType.MESH)

        mines = [pltpu.make_async_copy(x_refs[w], slot(w, *me), local_sems.at[w]) for w in range(nw)]
        for cp in mines:
            cp.start()
        first = [copy(w, 0, me, sibling, src=x_refs[w]) for w in range(nw)]
        first += [copy(w, 1 + j, me, (*chip, c), src=x_refs[w]) for w in range(nw) for j, chip in enumerate(chips)]
        for cp in first:
            cp.start()
        passed = []
        for w in range(nw):
            for j, chip in enumerate(chips):
                copy(w, 1 + j, (*chip, c), me).wait_recv()
                fwd = copy(w, 4 + j, (*chip, c), sibling)
                fwd.start()
                passed.append(fwd)
        for w in range(nw):
            copy(w, 0, sibling, me).wait_recv()
            for j, chip in enumerate(chips):
                copy(w, 4 + j, (*chip, 1 - c), me).wait_recv()
        for cp in first + passed:
            cp.wait_send()
        for cp in mines:
            cp.wait()

    hbm = pl.BlockSpec(memory_space=pl.ANY)
    return pl.pallas_call(
        body, name=name, out_shape=[SDS((N_DEV,) + s.shape, s.dtype) for s in shards],
        in_specs=[hbm] * nw, out_specs=[hbm] * nw,
        scratch_shapes=[pltpu.SemaphoreType.DMA((7 * nw,)), pltpu.SemaphoreType.DMA((7 * nw,)), pltpu.SemaphoreType.DMA((nw,))],
    )(*shards)


def _pair_exchange_call(grads, name):
    nw = len(grads)

    def body(*refs):
        g_refs, got_refs = refs[:nw], refs[nw:2 * nw]
        send_sems, recv_sems = refs[2 * nw:]
        x, y, c = lax.axis_index("x"), lax.axis_index("y"), lax.axis_index("c")
        copies = []
        for w in range(nw):
            for j in range(4):
                give = pltpu.make_async_remote_copy(
                    src_ref=g_refs[w].at[2 * j + 1 - c], dst_ref=got_refs[w].at[j], send_sem=send_sems.at[4 * w + j],
                    recv_sem=recv_sems.at[4 * w + j], device_id=(x, y, 1 - c), device_id_type=pl.DeviceIdType.MESH)
                give.start()
                copies.append(give)
        for cp in copies:
            cp.wait()

    hbm = pl.BlockSpec(memory_space=pl.ANY)
    return pl.pallas_call(
        body, name=name, out_shape=[SDS((4,) + g.shape[1:], g.dtype) for g in grads], in_specs=[hbm] * nw, out_specs=[hbm] * nw,
        scratch_shapes=[pltpu.SemaphoreType.DMA((4 * nw,)), pltpu.SemaphoreType.DMA((4 * nw,))],
    )(*grads)


def _chip_exchange_call(parts, name):
    nw = len(parts)

    def body(*refs):
        p_refs, out_refs = refs[:nw], refs[nw:2 * nw]
        send_sems, recv_sems = refs[2 * nw:]
        x, y, c = lax.axis_index("x"), lax.axis_index("y"), lax.axis_index("c")
        chips = [(1 - x, y), (x, 1 - y), (1 - x, 1 - y)]
        copies = []
        for w in range(nw):
            for j, (px, py) in enumerate(chips):
                give = pltpu.make_async_remote_copy(
                    src_ref=p_refs[w].at[2 * px + py], dst_ref=out_refs[w].at[j], send_sem=send_sems.at[3 * w + j],
                    recv_sem=recv_sems.at[3 * w + j], device_id=(px, py, c), device_id_type=pl.DeviceIdType.MESH)
                give.start()
                copies.append(give)
        for cp in copies:
            cp.wait()

    hbm = pl.BlockSpec(memory_space=pl.ANY)
    return pl.pallas_call(
        body, name=name, out_shape=[SDS((3,) + p.shape[1:], p.dtype) for p in parts], in_specs=[hbm] * nw, out_specs=[hbm] * nw,
        scratch_shapes=[pltpu.SemaphoreType.DMA((3 * nw,)), pltpu.SemaphoreType.DMA((3 * nw,))],
    )(*parts)


def _pair_sum_call(g, got, core, name):
    _, k, n = got.shape
    tr = _tile(k, 256)

    def body(c_ref, g_ref, got_ref, o_ref):
        o_ref[...] = g_ref[...] + got_ref[...]

    spec = pltpu.PrefetchScalarGridSpec(
        num_scalar_prefetch=1, grid=(4, k // tr),
        in_specs=[pl.BlockSpec((1, tr, n), lambda j, i, c: (2 * j + c[0], i, 0)), pl.BlockSpec((1, tr, n), lambda j, i, c: (j, i, 0))],
        out_specs=pl.BlockSpec((1, tr, n), lambda j, i, c: (j, i, 0)))
    return pl.pallas_call(body, name=name, grid_spec=spec, out_shape=SDS(got.shape, got.dtype),
                          compiler_params=_params(("parallel", "parallel"), VMEM_MID))(core, g, got)


def _adam_own_call(pair, chip, recv, w, m, v, name, rows):
    _, r, cols = recv.shape

    def body(chip_ref, p_ref, g_ref, w_ref, m_ref, v_ref, go_ref, d_ref, mo_ref, vo_ref):
        g = ((p_ref[0] + g_ref[0]) + g_ref[1]) + g_ref[2]
        for o_ref, val in zip((go_ref, d_ref, mo_ref, vo_ref), _adamw(g, w_ref[...], m_ref[...], v_ref[...])):
            o_ref[...] = val

    blk = pl.BlockSpec((rows, cols), lambda i, s: (i, 0))
    spec = pltpu.PrefetchScalarGridSpec(
        num_scalar_prefetch=1, grid=(r // rows,),
        in_specs=[pl.BlockSpec((1, rows, cols), lambda i, s: (s[0], i, 0)), pl.BlockSpec((3, rows, cols), lambda i, s: (0, i, 0)),
                  blk, blk, blk],
        out_specs=[blk, blk, blk, blk])
    return pl.pallas_call(body, name=name, grid_spec=spec, out_shape=[SDS((r, cols), F32)] * 4,
                          compiler_params=_params(("parallel",), VMEM_MID))(chip, pair, recv, w, m, v)


def _join_cols_call(w8, name):
    _, k, n = w8.shape
    tk = _tile(k, 256)

    def body(w_ref, o_ref):
        for s in range(N_DEV):
            o_ref[:, n * s:n * (s + 1)] = w_ref[s]

    return pl.pallas_call(body, name=name, grid=(k // tk,), in_specs=[pl.BlockSpec((N_DEV, tk, n), lambda i: (0, i, 0))],
                          out_specs=pl.BlockSpec((tk, N_DEV * n), lambda i: (i, 0)), out_shape=SDS((k, N_DEV * n), w8.dtype),
                          compiler_params=_params(("parallel",), VMEM_MID))(w8)


def _split_cols_call(g, name):
    k, n8 = g.shape
    n = n8 // N_DEV
    tk = _tile(k, 256)

    def body(g_ref, o_ref):
        for s in range(N_DEV):
            o_ref[s] = g_ref[:, n * s:n * (s + 1)]

    return pl.pallas_call(body, name=name, grid=(k // tk,), in_specs=[pl.BlockSpec((tk, n8), lambda i: (i, 0))],
                          out_specs=pl.BlockSpec((N_DEV, tk, n), lambda i: (0, i, 0)), out_shape=SDS((N_DEV, k, n), g.dtype),
                          compiler_params=_params(("parallel",), VMEM_MID))(g)


def _pack(parts, rows_multiple):
    flat = jnp.concatenate([p.reshape(-1) for p in parts])
    unit = rows_multiple * LANES
    padded = -(-flat.shape[0] // unit) * unit
    flat = jnp.concatenate([flat, jnp.zeros((padded - flat.shape[0],), F32)])
    return flat.reshape(-1, LANES)


def _unpack(buf, shapes):
    flat = buf.reshape(-1)
    out, off = [], 0
    for s in shapes:
        n = math.prod(s)
        out.append(flat[off:off + n].reshape(s))
        off += n
    return out


def _local_loss(diff, const, L):
    (x, mod_raw, norm_w, lam_re, lam_im, log_dt, b_re, b_im, c_re, c_im, s5_d, conv_w, a_log, dt_bias, gdn_nw, final_nw,
     *slots) = diff
    tgt, ada_b, weights = const
    lin = {n: (lambda a, n=n, i=i: make_mm(n)(a, weights[i], slots[i])) for i, n in enumerate(MM_NAMES)}
    tm = 256 if L % 256 == 0 else L
    mods = mod_raw.reshape(2, 3, 1, D_MODEL)
    biases = ada_b.reshape(2, 3, 1, D_MODEL)

    op_ln0 = make_rowwise(_f_lnmod, "ln0", tm, 1, 5)
    (h,) = op_ln0((x,), (norm_w[0:1], mods[0, 1], mods[0, 0], biases[0, 1], biases[0, 0]))
    u = lin["s5_in_u"](h)
    z = lin["s5_in_z"](h)
    blocks = _s5_block_params(lam_re, lam_im, log_dt, b_re, b_im, c_re, c_im)
    ys = make_s5_core(min(S5_TL, L))(u, *blocks)
    (y2,) = make_rowwise(_f_s5_act, "s5_act", tm, 2, 1)((ys, u), (s5_d.reshape(1, D_INNER),))
    t = lin["s5_glu"](y2)
    (y4,) = make_rowwise(_f_s5_gate, "s5_gate", tm, 3, 0)((y2, t, z), ())
    o = lin["s5_out"](y4)
    (x1,) = make_rowwise(_f_res, "res0", tm, 2, 2)((x, o), (mods[0, 2], biases[0, 2]))

    op_ln1 = make_rowwise(_f_lnmod, "ln1", tm, 1, 5)
    (h,) = op_ln1((x1,), (norm_w[1:2], mods[1, 1], mods[1, 0], biases[1, 1], biases[1, 0]))
    q0 = lin["gdn_in_q"](h)
    k0 = lin["gdn_in_k"](h)
    v0 = lin["gdn_in_v"](h)
    gz = lin["gdn_in_z"](h)
    ba = lin["gdn_in_ba"](h)
    cw = jnp.concatenate([conv_w, jnp.zeros((SUBLANES - GDN_CONV, GDN_CONV_CH), F32)], axis=0)
    (q,) = make_rowwise(_f_qnorm, "gdn_qn", tm, 1, 0)((gdn_conv(q0, cw[:, :GDN_QK]),), ())
    (k,) = make_rowwise(_f_knorm, "gdn_kn", tm, 1, 0)((gdn_conv(k0, cw[:, GDN_QK:2 * GDN_QK]),), ())
    (v,) = make_rowwise(_f_vact, "gdn_va", tm, 1, 0)((gdn_conv(v0, cw[:, 2 * GDN_QK:]),), ())
    pad = jnp.zeros((LANES - 2 * GDN_HEADS,), F32)
    alog_row = jnp.concatenate([jnp.zeros((GDN_HEADS,), F32), a_log, pad]).reshape(1, LANES)
    dtb_row = jnp.concatenate([jnp.zeros((GDN_HEADS,), F32), dt_bias, pad]).reshape(1, LANES)
    (bg,) = make_rowwise(_f_betag, "gdn_bg", tm, 1, 2)((ba,), (alog_row, dtb_row))
    beta = bg[:, :GDN_HEADS].T.reshape(GDN_HEADS, L, 1)
    g = bg[:, GDN_HEADS:2 * GDN_HEADS].T.reshape(GDN_HEADS, L, 1)
    og = gdn_scan(*gdn_prep(q, k, v, beta, g))
    nw_row = jnp.tile(gdn_nw, GDN_HEADS).reshape(1, D_INNER)
    (on,) = make_rowwise(_f_gdn_post, "gdn_post", tm, 2, 1)((og, gz), (nw_row,))
    y = lin["gdn_out"](on)
    (x2,) = make_rowwise(_f_res, "res1", tm, 2, 2)((x1, y), (mods[1, 2], biases[1, 2]))

    (lt,) = make_rowwise(_f_loss, "loss", tm, 2, 1)((x2, tgt), (final_nw.reshape(1, D_MODEL),))
    return jnp.sum(lt)


MM_NAMES = ("s5_in_u", "s5_in_z", "s5_glu", "s5_out", "gdn_in_q", "gdn_in_k", "gdn_in_v", "gdn_in_z", "gdn_in_ba", "gdn_out")
SMALL_NAMES = ("ada_b", "norm_w", "s5_lambda_re", "s5_lambda_im", "s5_log_dt", "s5_b_re", "s5_b_im", "s5_c_re", "s5_c_im",
               "s5_d", "gdn_a_log", "gdn_dt_bias", "final_norm_w")
BIG_NAMES = ("s5_w_in", "s5_w_glu", "s5_w_out", "gdn_w_in", "gdn_w_out")
WEIGHT_ORDER = ("ada_w", "ada_b", "norm_w", "s5_w_in", "s5_lambda_re", "s5_lambda_im", "s5_log_dt", "s5_b_re", "s5_b_im",
                "s5_c_re", "s5_c_im", "s5_d", "s5_w_glu", "s5_w_out", "gdn_w_in", "gdn_conv_w", "gdn_a_log", "gdn_dt_bias",
                "gdn_norm_w", "gdn_w_out", "final_norm_w")


def _step(x, c, W, M, V, tgt):
    L = x.shape[1]
    ix, iy, ic = lax.axis_index("x"), lax.axis_index("y"), lax.axis_index("c")
    me = 4 * ix + 2 * iy + ic
    n_ada = W["ada_w"].shape[2]
    n_conv = W["gdn_conv_w"].shape[2]
    n_gnw = W["gdn_norm_w"].shape[1]

    g1 = _allgather_call(_pack([c, W["gdn_conv_w"], W["gdn_norm_w"]], SUBLANES), "gather_small_in", False)
    g1 = g1.reshape(N_DEV, -1)
    c_all = g1[:, :D_MODEL]
    conv_w = g1[:, D_MODEL:D_MODEL + GDN_CONV * n_conv].reshape(N_DEV, GDN_CONV, n_conv).transpose(1, 0, 2).reshape(GDN_CONV, -1)
    gdn_nw = g1[:, D_MODEL + GDN_CONV * n_conv:D_MODEL + GDN_CONV * n_conv + n_gnw].reshape(-1)
    mod_part = _ada_mod_call(c_all, W["ada_w"])
    g2 = _allgather_call(_pack([mod_part], SUBLANES), "gather_mod", False).reshape(N_DEV, -1)
    mod_all = g2[:, :2 * N_DEV * n_ada].reshape(N_DEV, 2, N_DEV, n_ada)
    mod_raw = lax.dynamic_index_in_dim(mod_all, me, axis=2, keepdims=False)
    mod_raw = mod_raw.transpose(1, 0, 2).reshape(2, 3 * D_MODEL)

    gathered = _gather_weights_call([W[n][0].astype(BF16) for n in BIG_NAMES], "gather_weights")
    full = dict(zip(BIG_NAMES, gathered))
    w_in5 = _join_cols_call(full["s5_w_in"], "join_s5_w_in")
    w_ing = _join_cols_call(full["gdn_w_in"], "join_gdn_w_in")
    w_ba = jnp.concatenate([w_ing[:, GDN_CONV_CH + D_INNER:], jnp.zeros((D_MODEL, LANES - 2 * GDN_HEADS), BF16)], axis=1)
    weights = (w_in5[:, :D_INNER], w_in5[:, D_INNER:], full["s5_w_glu"].reshape(D_INNER, D_INNER),
               full["s5_w_out"].reshape(D_INNER, D_MODEL),
               w_ing[:, :GDN_QK], w_ing[:, GDN_QK:2 * GDN_QK], w_ing[:, 2 * GDN_QK:GDN_CONV_CH],
               w_ing[:, GDN_CONV_CH:GDN_CONV_CH + D_INNER], w_ba, full["gdn_w_out"].reshape(D_INNER, D_MODEL))
    slots = tuple(jnp.zeros(w.shape, F32) for w in weights)
    diff = (x[0], mod_raw, W["norm_w"], W["s5_lambda_re"][0], W["s5_lambda_im"][0], W["s5_log_dt"][0], W["s5_b_re"][0],
            W["s5_b_im"][0], W["s5_c_re"][0], W["s5_c_im"][0], W["s5_d"][0], conv_w, W["gdn_a_log"][0], W["gdn_dt_bias"][0],
            gdn_nw, W["final_norm_w"], *slots)

    loss_local, grads = jax.value_and_grad(_local_loss)(diff, (tgt[0], W["ada_b"], weights), L)
    (dx, dmod, d_norm_w, d_lre, d_lim, d_logdt, d_bre, d_bim, d_cre, d_cim, d_s5d, d_conv, d_alog, d_dtb, d_gnw, d_fnw,
     d_wu, d_wz, d_wglu, d_wo5, d_wq, d_wk, d_wv, d_wgz, d_wba, d_wog) = grads
    loss = lax.psum(loss_local, MESH_AXES)

    d_in5 = _split_cols_call(jnp.concatenate([d_wu, d_wz], axis=1), "split_s5_w_in")
    d_ing = _split_cols_call(jnp.concatenate([d_wq, d_wk, d_wv, d_wgz, d_wba[:, :2 * GDN_HEADS]], axis=1), "split_gdn_w_in")
    rows = lambda d: d.reshape(N_DEV, d.shape[0] // N_DEV, d.shape[1])
    per_dev = [d_in5, rows(d_wglu), rows(d_wo5), d_ing, rows(d_wog)]
    got = _pair_exchange_call(per_dev, "scatter_grads_pair")
    core = jnp.reshape(ic, (1,)).astype(jnp.int32)
    chip = jnp.reshape(2 * ix + iy, (1,)).astype(jnp.int32)
    pair = [_pair_sum_call(g, r, core, "pair_sum_" + n) for g, r, n in zip(per_dev, got, BIG_NAMES)]
    recv = _chip_exchange_call(pair, "scatter_grads_chips")
    big = [_adam_own_call(p, chip, r, W[n][0], M[n][0], V[n][0], "adam_" + n, _tile(W[n].shape[1], 128))
           for p, r, n in zip(pair, recv, BIG_NAMES)]
    big = [[o[None] for o in outs] for outs in big]

    small_parts = [dmod, d_norm_w, d_lre, d_lim, d_logdt, d_bre, d_bim, d_cre, d_cim, d_s5d, d_alog, d_dtb, d_fnw, d_conv, d_gnw]
    small_shapes = [p.shape for p in small_parts]
    sg = _allgather_call(_pack(small_parts, ADAM_ROWS), "gather_small_grads", False)
    sg = sg.reshape(N_DEV, -1, LANES)
    tot = _unpack(_sum_call(sg, "sum_small_grads"), small_shapes)
    (g_adab, g_norm_w, g_lre, g_lim, g_logdt, g_bre, g_bim, g_cre, g_cim, g_s5d, g_alog, g_dtb, g_fnw, g_conv, g_gnw) = tot
    g_conv_mine = lax.dynamic_slice_in_dim(g_conv, me * n_conv, n_conv, axis=1)
    g_gnw_mine = lax.dynamic_slice_in_dim(g_gnw, me * n_gnw, n_gnw, axis=0)
    small_g = {"ada_b": g_adab.reshape(W["ada_b"].shape), "norm_w": g_norm_w, "s5_lambda_re": g_lre[None], "s5_lambda_im": g_lim[None],
               "s5_log_dt": g_logdt[None], "s5_b_re": g_bre[None], "s5_b_im": g_bim[None], "s5_c_re": g_cre[None],
               "s5_c_im": g_cim[None], "s5_d": g_s5d[None], "gdn_a_log": g_alog[None], "gdn_dt_bias": g_dtb[None],
               "final_norm_w": g_fnw, "gdn_conv_w": g_conv_mine[None], "gdn_norm_w": g_gnw_mine[None]}
    small_names = SMALL_NAMES + ("gdn_conv_w", "gdn_norm_w")
    small = _adam_call(_pack([small_g[n] for n in small_names], ADAM_ROWS)[None], _pack([W[n] for n in small_names], ADAM_ROWS),
                       _pack([M[n] for n in small_names], ADAM_ROWS), _pack([V[n] for n in small_names], ADAM_ROWS), "adam_small")
    small = [_unpack(b, [W[n].shape for n in small_names]) for b in small]

    dmod_all = sg.reshape(N_DEV, -1)[:, :2 * 3 * D_MODEL].reshape(N_DEV, 2, N_DEV, n_ada)
    dmod_mine = lax.dynamic_index_in_dim(dmod_all, me, axis=2, keepdims=False).transpose(1, 0, 2)
    g_ada_w = _ada_grad_call(c_all, dmod_mine)
    ada = _adam_call(g_ada_w.reshape(1, -1, LANES), W["ada_w"].reshape(-1, LANES), M["ada_w"].reshape(-1, LANES),
                     V["ada_w"].reshape(-1, LANES), "adam_ada")
    ada = [a.reshape(W["ada_w"].shape) for a in ada]

    res = {}
    for i, n in enumerate(BIG_NAMES):
        res[n] = big[i]
    for i, n in enumerate(small_names):
        res[n] = [b[i] for b in small]
    res["ada_w"] = ada
    outs = [loss, dx[None]]
    for j in range(4):
        outs += [res[n][j] for n in WEIGHT_ORDER]
    return tuple(outs)


def kernel(x, c, ada_w, ada_b, norm_w, s5_w_in, s5_lambda_re, s5_lambda_im, s5_log_dt, s5_b_re, s5_b_im, s5_c_re, s5_c_im, s5_d, s5_w_glu, s5_w_out, gdn_w_in, gdn_conv_w, gdn_a_log, gdn_dt_bias, gdn_norm_w, gdn_w_out, final_norm_w, loss_target, m_ada_w, m_ada_b, m_norm_w, m_s5_w_in, m_s5_lambda_re, m_s5_lambda_im, m_s5_log_dt, m_s5_b_re, m_s5_b_im, m_s5_c_re, m_s5_c_im, m_s5_d, m_s5_w_glu, m_s5_w_out, m_gdn_w_in, m_gdn_conv_w, m_gdn_a_log, m_gdn_dt_bias, m_gdn_norm_w, m_gdn_w_out, m_final_norm_w, v_ada_w, v_ada_b, v_norm_w, v_s5_w_in, v_s5_lambda_re, v_s5_lambda_im, v_s5_log_dt, v_s5_b_re, v_s5_b_im, v_s5_c_re, v_s5_c_im, v_s5_d, v_s5_w_glu, v_s5_w_out, v_gdn_w_in, v_gdn_conv_w, v_gdn_a_log, v_gdn_dt_bias, v_gdn_norm_w, v_gdn_w_out, v_final_norm_w):
    W = dict(ada_w=ada_w, ada_b=ada_b, norm_w=norm_w, s5_w_in=s5_w_in, s5_lambda_re=s5_lambda_re, s5_lambda_im=s5_lambda_im,
             s5_log_dt=s5_log_dt, s5_b_re=s5_b_re, s5_b_im=s5_b_im, s5_c_re=s5_c_re, s5_c_im=s5_c_im, s5_d=s5_d,
             s5_w_glu=s5_w_glu, s5_w_out=s5_w_out, gdn_w_in=gdn_w_in, gdn_conv_w=gdn_conv_w, gdn_a_log=gdn_a_log,
             gdn_dt_bias=gdn_dt_bias, gdn_norm_w=gdn_norm_w, gdn_w_out=gdn_w_out, final_norm_w=final_norm_w)
    M = dict(ada_w=m_ada_w, ada_b=m_ada_b, norm_w=m_norm_w, s5_w_in=m_s5_w_in, s5_lambda_re=m_s5_lambda_re,
             s5_lambda_im=m_s5_lambda_im, s5_log_dt=m_s5_log_dt, s5_b_re=m_s5_b_re, s5_b_im=m_s5_b_im, s5_c_re=m_s5_c_re,
             s5_c_im=m_s5_c_im, s5_d=m_s5_d, s5_w_glu=m_s5_w_glu, s5_w_out=m_s5_w_out, gdn_w_in=m_gdn_w_in,
             gdn_conv_w=m_gdn_conv_w, gdn_a_log=m_gdn_a_log, gdn_dt_bias=m_gdn_dt_bias, gdn_norm_w=m_gdn_norm_w,
             gdn_w_out=m_gdn_w_out, final_norm_w=m_final_norm_w)
    V = dict(ada_w=v_ada_w, ada_b=v_ada_b, norm_w=v_norm_w, s5_w_in=v_s5_w_in, s5_lambda_re=v_s5_lambda_re,
             s5_lambda_im=v_s5_lambda_im, s5_log_dt=v_s5_log_dt, s5_b_re=v_s5_b_re, s5_b_im=v_s5_b_im, s5_c_re=v_s5_c_re,
             s5_c_im=v_s5_c_im, s5_d=v_s5_d, s5_w_glu=v_s5_w_glu, s5_w_out=v_s5_w_out, gdn_w_in=v_gdn_w_in,
             gdn_conv_w=v_gdn_conv_w, gdn_a_log=v_gdn_a_log, gdn_dt_bias=v_gdn_dt_bias, gdn_norm_w=v_gdn_norm_w,
             gdn_w_out=v_gdn_w_out, final_norm_w=v_final_norm_w)
    return _step(x, c, W, M, V, loss_target)
```

```python
import functools
import math

import jax
import jax.numpy as jnp
from jax import lax
from jax.experimental import pallas as pl
from jax.experimental.pallas import tpu as pltpu

F32 = jnp.float32
BF16 = jnp.bfloat16
SDS = jax.ShapeDtypeStruct

D_MODEL = 1024
D_INNER = 2048
NORM_EPS = 1e-6
S5_GROUP = 16
S5_GROUPS = 128
S5_STATE = 64
GDN_HEADS = 8
GDN_DK = 128
GDN_DV = 256
GDN_CONV = 4
GDN_CHUNK = 64
GDN_QK = 1024
GDN_CONV_CH = 4096
GDN_PROJ = 6160
ADAM_LR = 0.001
ADAM_B1 = 0.9
ADAM_B2 = 0.999
ADAM_EPS = 1e-08
ADAM_WD = 0.01
ADAM_STEP = 10

N_DEV = 8
LANES = 128
SUBLANES = 8
VMEM_BIG = 56 << 20
VMEM_MID = 40 << 20
S5_GB = 8
S5_TL = 1024
MESH_AXES = ("x", "y", "c")


def _params(sem, vmem=None):
    return pltpu.CompilerParams(dimension_semantics=sem, vmem_limit_bytes=vmem)


def _bdot(a, b, dims=(((1,), (0,)), ((), ()))):
    return lax.dot_general(a.astype(BF16), b.astype(BF16), dims, preferred_element_type=F32)


def _hdot(a, b, dims=(((1,), (0,)), ((), ()))):
    return lax.dot_general(a, b, dims, preferred_element_type=F32, precision=lax.Precision.HIGHEST)


@jax.custom_vjp
def _hdot_bnn(a, b):
    return _hdot(a, b, (((2,), (1,)), ((0,), (0,))))


def _hdot_bnn_fwd(a, b):
    return _hdot_bnn(a, b), (a, b)


def _hdot_bnn_bwd(res, g):
    a, b = res
    return _hdot(g, b, (((2,), (2,)), ((0,), (0,)))), _hdot(a, g, (((1,), (1,)), ((0,), (0,))))


_hdot_bnn.defvjp(_hdot_bnn_fwd, _hdot_bnn_bwd)


NN = (((1,), (0,)), ((), ()))
NT = (((1,), (1,)), ((), ()))
TN = (((0,), (0,)), ((), ()))


def _tile(n, pref):
    for t in (pref, 512, 256, 128):
        if t <= n and n % t == 0:
            return t
    return n


def _matmul(a, b, mode, name):
    if mode == "nn":
        (m, k), (_, n) = a.shape, b.shape
    elif mode == "nt":
        (m, k), (n, _) = a.shape, b.shape
    else:
        (k, m), (_, n) = a.shape, b.shape
    tm, tn, tk = _tile(m, 512), _tile(n, 512), (k if k <= 2048 else _tile(k, 512))
    if mode == "tn":
        tm, tn = _tile(m, 1024), _tile(n, 1024)
    nk = k // tk
    dims = {"nn": NN, "nt": NT, "tn": TN}[mode]

    def body(a_ref, b_ref, o_ref, acc_ref):
        kk = pl.program_id(2)

        @pl.when(kk == 0)
        def _():
            acc_ref[...] = jnp.zeros_like(acc_ref)
        acc_ref[...] += _bdot(a_ref[...], b_ref[...], dims)

        @pl.when(kk == nk - 1)
        def _():
            o_ref[...] = acc_ref[...]

    a_spec = pl.BlockSpec((tk, tm), lambda i, j, q: (q, i)) if mode == "tn" else pl.BlockSpec((tm, tk), lambda i, j, q: (i, q))
    b_spec = pl.BlockSpec((tn, tk), lambda i, j, q: (j, q)) if mode == "nt" else pl.BlockSpec((tk, tn), lambda i, j, q: (q, j))
    return pl.pallas_call(
        body, name=name, grid=(m // tm, n // tn, nk),
        in_specs=[a_spec, b_spec], out_specs=pl.BlockSpec((tm, tn), lambda i, j, q: (i, j)),
        out_shape=SDS((m, n), F32), scratch_shapes=[pltpu.VMEM((tm, tn), F32)],
        compiler_params=_params(("parallel", "parallel", "arbitrary"), VMEM_MID),
    )(a, b)


def make_mm(name):
    @jax.custom_vjp
    def mm(a, w, grad_slot):
        return _matmul(a, w, "nn", name + "_fwd")

    def fwd(a, w, grad_slot):
        return _matmul(a, w, "nn", name + "_fwd"), (a, w)

    def bwd(res, g):
        a, w = res
        return _matmul(g, w, "nt", name + "_dx"), jnp.zeros_like(w), _matmul(a, g, "tn", name + "_dw")

    mm.defvjp(fwd, bwd)
    return mm


def make_rowwise(f, name, tm, n_rows, n_params, vmem=VMEM_MID):
    def specs_of(arrs, blocked):
        if blocked:
            return [pl.BlockSpec((tm, a.shape[1]), lambda i: (i, 0)) for a in arrs]
        return [pl.BlockSpec(a.shape, lambda i: (0, 0)) for a in arrs]

    def out_structs(rows, params):
        blk = [SDS((tm, r.shape[1]), r.dtype) for r in rows] + [SDS(p.shape, p.dtype) for p in params]
        return jax.eval_shape(f, *blk)

    def run_fwd(rows, params):
        L = rows[0].shape[0]
        outs = out_structs(rows, params)

        def body(*refs):
            ins = [r[...] for r in refs[:n_rows + n_params]]
            res = f(*ins)
            for o_ref, val in zip(refs[n_rows + n_params:], res):
                o_ref[...] = val

        return pl.pallas_call(
            body, name=name + "_fwd", grid=(L // tm,),
            in_specs=specs_of(rows, True) + specs_of(params, False),
            out_specs=[pl.BlockSpec((tm, o.shape[1]), lambda i: (i, 0)) for o in outs],
            out_shape=[SDS((L, o.shape[1]), o.dtype) for o in outs],
            compiler_params=_params(("parallel",), vmem),
        )(*rows, *params)

    def run_bwd(rows, params, gs):
        L = rows[0].shape[0]
        n_g = len(gs)

        def body(*refs):
            i = pl.program_id(0)
            ins = [r[...] for r in refs[:n_rows + n_params]]
            cts = tuple(r[...] for r in refs[n_rows + n_params:n_rows + n_params + n_g])
            outs = refs[n_rows + n_params + n_g:]
            _, vjp = jax.vjp(f, *ins)
            grads = vjp(cts)
            for o_ref, val in zip(outs[:n_rows], grads[:n_rows]):
                o_ref[...] = val

            if n_params:
                @pl.when(i == 0)
                def _():
                    for o_ref in outs[n_rows:]:
                        o_ref[...] = jnp.zeros_like(o_ref)
                for o_ref, val in zip(outs[n_rows:], grads[n_rows:]):
                    o_ref[...] += val

        res = pl.pallas_call(
            body, name=name + "_bwd", grid=(L // tm,),
            in_specs=specs_of(rows, True) + specs_of(params, False) + specs_of(gs, True),
            out_specs=specs_of(rows, True) + specs_of(params, False),
            out_shape=[SDS(r.shape, r.dtype) for r in rows] + [SDS(p.shape, p.dtype) for p in params],
            compiler_params=_params(("arbitrary",), vmem),
        )(*rows, *params, *gs)
        return tuple(res[:n_rows]), tuple(res[n_rows:])

    @jax.custom_vjp
    def op(rows, params):
        return tuple(run_fwd(rows, params))

    def fwd(rows, params):
        return tuple(run_fwd(rows, params)), (rows, params)

    def bwd(res, gs):
        rows, params = res
        return run_bwd(rows, params, tuple(gs))

    op.defvjp(fwd, bwd)
    return op


def _s5_scan_rows(xr_ref, xi_ref, ar, ai, x0r, x0i, tl, reverse=False):
    n = xr_ref.shape[1]
    T = SUBLANES
    row = lax.broadcasted_iota(jnp.int32, (T, n), 0)
    pr, pi = [ar], [ai]
    for _ in range(T - 1):
        pr, pi = pr + [pr[-1] * ar - pi[-1] * ai], pi + [pr[-1] * ai + pi[-1] * ar]
    levels = []
    for d in (1, 2, 4):
        mask = (row < T - d) if reverse else (row >= d)
        levels.append((T - d if reverse else d, jnp.where(mask, pr[d - 1], 0.0), jnp.where(mask, pi[d - 1], 0.0)))
    cr = jnp.zeros((T, n), F32)
    ci = jnp.zeros((T, n), F32)
    for r in range(T):
        k = (T - r) if reverse else (r + 1)
        cr = jnp.where(row == r, pr[k - 1], cr)
        ci = jnp.where(row == r, pi[k - 1], ci)
    nt = tl // T
    last = 0 if reverse else T - 1

    def step(t, carry):
        sr, si = carry
        base = pl.multiple_of((nt - 1 - t if reverse else t) * T, T)
        br = xr_ref[pl.ds(base, T), :]
        bi = xi_ref[pl.ds(base, T), :]
        for shift, mr, mi in levels:
            qr = pltpu.roll(br, shift, 0)
            qi = pltpu.roll(bi, shift, 0)
            br, bi = br + (mr * qr - mi * qi), bi + (mr * qi + mi * qr)
        xr = br + (cr * sr - ci * si)
        xi = bi + (cr * si + ci * sr)
        xr_ref[pl.ds(base, T), :] = xr
        xi_ref[pl.ds(base, T), :] = xi
        return xr[last:last + 1, :], xi[last:last + 1, :]
    return lax.fori_loop(0, nt, step, (x0r, x0i))


def _s5_fwd_call(u, bre, bim, cre, cim, a, tl):
    L, e = u.shape
    nb = e // LANES
    ns = bre.shape[2]
    nc = L // tl

    def body(u_ref, bre_ref, bim_ref, cre_ref, cim_ref, a_ref, ys_ref, xb_ref, xr_ref, xi_ref, carry_ref):
        c = pl.program_id(1)

        @pl.when(c == 0)
        def _():
            carry_ref[...] = jnp.zeros_like(carry_ref)
        xb_ref[0, 0] = carry_ref[...]
        ub = u_ref[...]
        xr_ref[...] = _bdot(ub, bre_ref[0])
        xi_ref[...] = _bdot(ub, bim_ref[0])
        ar = a_ref[0, 0:1, :]
        ai = a_ref[0, 1:2, :]
        xr, xi = _s5_scan_rows(xr_ref, xi_ref, ar, ai, carry_ref[0:1, :], carry_ref[1:2, :], tl)
        carry_ref[0:1, :] = xr
        carry_ref[1:2, :] = xi
        ys_ref[...] = _bdot(xr_ref[...], cre_ref[0]) - _bdot(xi_ref[...], cim_ref[0])

    return pl.pallas_call(
        body, name="s5_core_fwd", grid=(nb, nc),
        in_specs=[pl.BlockSpec((tl, LANES), lambda j, c: (c, j)),
                  pl.BlockSpec((1, LANES, ns), lambda j, c: (j, 0, 0)), pl.BlockSpec((1, LANES, ns), lambda j, c: (j, 0, 0)),
                  pl.BlockSpec((1, ns, LANES), lambda j, c: (j, 0, 0)), pl.BlockSpec((1, ns, LANES), lambda j, c: (j, 0, 0)),
                  pl.BlockSpec((1, SUBLANES, ns), lambda j, c: (j, 0, 0))],
        out_specs=[pl.BlockSpec((tl, LANES), lambda j, c: (c, j)),
                   pl.BlockSpec((1, 1, SUBLANES, ns), lambda j, c: (j, c, 0, 0))],
        out_shape=[SDS((L, e), F32), SDS((nb, nc, SUBLANES, ns), F32)],
        scratch_shapes=[pltpu.VMEM((tl, ns), F32), pltpu.VMEM((tl, ns), F32), pltpu.VMEM((SUBLANES, ns), F32)],
        compiler_params=_params(("arbitrary", "arbitrary"), VMEM_MID),
    )(u, bre, bim, cre, cim, a)


def _s5_bwd_call(u, dys, bre, bim, cre, cim, a, xb, tl):
    L, e = u.shape
    nb = e // LANES
    ns = bre.shape[2]
    nc = L // tl

    def body(u_ref, dys_ref, bre_ref, bim_ref, cre_ref, cim_ref, a_ref, xb_ref,
             du_ref, dbre_ref, dbim_ref, dcre_ref, dcim_ref, da_ref,
             xr_ref, xi_ref, gr_ref, gi_ref, gcarry_ref):
        c = pl.program_id(1)

        @pl.when(c == 0)
        def _():
            gcarry_ref[...] = jnp.zeros_like(gcarry_ref)
            dbre_ref[...] = jnp.zeros_like(dbre_ref)
            dbim_ref[...] = jnp.zeros_like(dbim_ref)
            dcre_ref[...] = jnp.zeros_like(dcre_ref)
            dcim_ref[...] = jnp.zeros_like(dcim_ref)
            da_ref[...] = jnp.zeros_like(da_ref)

        ub = u_ref[...]
        dy = dys_ref[...]
        ar = a_ref[0, 0:1, :]
        ai = a_ref[0, 1:2, :]
        x0r = xb_ref[0, 0, 0:1, :]
        x0i = xb_ref[0, 0, 1:2, :]
        xr_ref[...] = _bdot(ub, bre_ref[0])
        xi_ref[...] = _bdot(ub, bim_ref[0])
        _s5_scan_rows(xr_ref, xi_ref, ar, ai, x0r, x0i, tl)
        dcre_ref[0] += _bdot(xr_ref[...], dy, TN)
        dcim_ref[0] -= _bdot(xi_ref[...], dy, TN)
        gr_ref[...] = _bdot(dy, cre_ref[0], NT)
        gi_ref[...] = -_bdot(dy, cim_ref[0], NT)

        g0r, g0i = _s5_scan_rows(gr_ref, gi_ref, ar, -ai, gcarry_ref[0:1, :], gcarry_ref[1:2, :], tl, reverse=True)
        gcarry_ref[0:1, :] = g0r
        gcarry_ref[1:2, :] = g0i
        row = lax.broadcasted_iota(jnp.int32, (tl, ns), 0)
        gr = gr_ref[...]
        gi = gi_ref[...]
        xpr = jnp.where(row == 0, x0r, pltpu.roll(xr_ref[...], 1, 0))
        xpi = jnp.where(row == 0, x0i, pltpu.roll(xi_ref[...], 1, 0))
        da_ref[0, 0:1, :] += jnp.sum(gr * xpr + gi * xpi, axis=0, keepdims=True)
        da_ref[0, 1:2, :] += jnp.sum(gi * xpr - gr * xpi, axis=0, keepdims=True)
        du_ref[...] = _bdot(gr, bre_ref[0], NT) + _bdot(gi, bim_ref[0], NT)
        dbre_ref[0] += _bdot(ub, gr, TN)
        dbim_ref[0] += _bdot(ub, gi, TN)

    rev = lambda c: nc - 1 - c
    return pl.pallas_call(
        body, name="s5_core_bwd", grid=(nb, nc),
        in_specs=[pl.BlockSpec((tl, LANES), lambda j, c: (rev(c), j)), pl.BlockSpec((tl, LANES), lambda j, c: (rev(c), j)),
                  pl.BlockSpec((1, LANES, ns), lambda j, c: (j, 0, 0)), pl.BlockSpec((1, LANES, ns), lambda j, c: (j, 0, 0)),
                  pl.BlockSpec((1, ns, LANES), lambda j, c: (j, 0, 0)), pl.BlockSpec((1, ns, LANES), lambda j, c: (j, 0, 0)),
                  pl.BlockSpec((1, SUBLANES, ns), lambda j, c: (j, 0, 0)),
                  pl.BlockSpec((1, 1, SUBLANES, ns), lambda j, c: (j, rev(c), 0, 0))],
        out_specs=[pl.BlockSpec((tl, LANES), lambda j, c: (rev(c), j)),
                   pl.BlockSpec((1, LANES, ns), lambda j, c: (j, 0, 0)), pl.BlockSpec((1, LANES, ns), lambda j, c: (j, 0, 0)),
                   pl.BlockSpec((1, ns, LANES), lambda j, c: (j, 0, 0)), pl.BlockSpec((1, ns, LANES), lambda j, c: (j, 0, 0)),
                   pl.BlockSpec((1, SUBLANES, ns), lambda j, c: (j, 0, 0))],
        out_shape=[SDS((L, e), F32), SDS(bre.shape, F32), SDS(bim.shape, F32), SDS(cre.shape, F32), SDS(cim.shape, F32),
                   SDS(a.shape, F32)],
        scratch_shapes=[pltpu.VMEM((tl, ns), F32) for _ in range(4)] + [pltpu.VMEM((SUBLANES, ns), F32)],
        compiler_params=_params(("arbitrary", "arbitrary"), VMEM_MID),
    )(u, dys, bre, bim, cre, cim, a, xb)


def make_s5_core(tl):
    @jax.custom_vjp
    def s5_core(u, bre, bim, cre, cim, a):
        return _s5_fwd_call(u, bre, bim, cre, cim, a, tl)[0]

    def fwd(u, bre, bim, cre, cim, a):
        ys, xb = _s5_fwd_call(u, bre, bim, cre, cim, a, tl)
        return ys, (u, bre, bim, cre, cim, a, xb)

    def bwd(res, dys):
        u, bre, bim, cre, cim, a, xb = res
        return tuple(_s5_bwd_call(u, dys, bre, bim, cre, cim, a, xb, tl))

    s5_core.defvjp(fwd, bwd)
    return s5_core


def _s5_block_params(lam_re, lam_im, log_dt, b_re, b_im, c_re, c_im):
    dt = jnp.exp(log_dt)[:, None]
    mag = jnp.exp(lam_re * dt)
    ab_re = mag * jnp.cos(lam_im * dt)
    ab_im = mag * jnp.sin(lam_im * dt)
    den = lam_re * lam_re + lam_im * lam_im
    nr = ab_re - 1.0
    ni = ab_im
    q_re = (nr * lam_re + ni * lam_im) / den
    q_im = (ni * lam_re - nr * lam_im) / den
    bb_re = q_re[..., None] * b_re - q_im[..., None] * b_im
    bb_im = q_re[..., None] * b_im + q_im[..., None] * b_re
    nb = S5_GROUPS // S5_GB
    eye = jnp.eye(S5_GB, dtype=F32)

    def bdiag_in(bb):
        t = bb.reshape(nb, S5_GB, S5_STATE, S5_GROUP)
        t = jnp.einsum("jgpm,gh->jgmhp", t, eye)
        return t.reshape(nb, S5_GB * S5_GROUP, S5_GB * S5_STATE)

    def bdiag_out(cc):
        t = cc.reshape(nb, S5_GB, S5_GROUP, S5_STATE)
        t = jnp.einsum("jgmp,gh->jgphm", t, eye)
        return t.reshape(nb, S5_GB * S5_STATE, S5_GB * S5_GROUP)

    a = jnp.stack([ab_re.reshape(nb, S5_GB * S5_STATE), ab_im.reshape(nb, S5_GB * S5_STATE)], axis=1)
    a = jnp.concatenate([a, jnp.zeros((nb, SUBLANES - 2, S5_GB * S5_STATE), F32)], axis=1)
    return bdiag_in(bb_re), bdiag_in(bb_im), bdiag_out(c_re), bdiag_out(c_im), a


def _shift_down(x, s, row):
    if s == 0:
        return x
    return jnp.where(row >= s, pltpu.roll(x, s, 0), 0.0)


def _shift_up(x, s, row, n):
    if s == 0:
        return x
    return jnp.where(row < n - s, pltpu.roll(x, n - s, 0), 0.0)


def _conv_fwd_call(x, w):
    L, ch = x.shape

    def body(x_ref, w_ref, y_ref):
        xv = x_ref[...]
        row = lax.broadcasted_iota(jnp.int32, xv.shape, 0)
        acc = jnp.zeros_like(xv)
        for j in range(GDN_CONV):
            acc += w_ref[j:j + 1, :] * _shift_down(xv, GDN_CONV - 1 - j, row)
        y_ref[...] = acc

    return pl.pallas_call(
        body, name="gdn_conv_fwd", grid=(ch // LANES,),
        in_specs=[pl.BlockSpec((L, LANES), lambda j: (0, j)), pl.BlockSpec((SUBLANES, LANES), lambda j: (0, j))],
        out_specs=pl.BlockSpec((L, LANES), lambda j: (0, j)), out_shape=SDS((L, ch), F32),
        compiler_params=_params(("parallel",), VMEM_MID),
    )(x, w)


def _conv_bwd_call(x, w, dy):
    L, ch = x.shape

    def body(x_ref, w_ref, dy_ref, dx_ref, dw_ref):
        xv = x_ref[...]
        g = dy_ref[...]
        row = lax.broadcasted_iota(jnp.int32, xv.shape, 0)
        acc = jnp.zeros_like(xv)
        dws = []
        for j in range(GDN_CONV):
            s = GDN_CONV - 1 - j
            acc += w_ref[j:j + 1, :] * _shift_up(g, s, row, L)
            dws.append(jnp.sum(g * _shift_down(xv, s, row), axis=0, keepdims=True))
        dx_ref[...] = acc
        dw_ref[...] = jnp.concatenate(dws + [jnp.zeros((SUBLANES - GDN_CONV, LANES), F32)], axis=0)

    return pl.pallas_call(
        body, name="gdn_conv_bwd", grid=(ch // LANES,),
        in_specs=[pl.BlockSpec((L, LANES), lambda j: (0, j)), pl.BlockSpec((SUBLANES, LANES), lambda j: (0, j)),
                  pl.BlockSpec((L, LANES), lambda j: (0, j))],
        out_specs=[pl.BlockSpec((L, LANES), lambda j: (0, j)), pl.BlockSpec((SUBLANES, LANES), lambda j: (0, j))],
        out_shape=[SDS((L, ch), F32), SDS((SUBLANES, ch), F32)],
        compiler_params=_params(("parallel",), VMEM_MID),
    )(x, w, dy)


@jax.custom_vjp
def gdn_conv(x, w):
    return _conv_fwd_call(x, w)


def _gdn_conv_f(x, w):
    return _conv_fwd_call(x, w), (x, w)


def _gdn_conv_b(res, dy):
    x, w = res
    return tuple(_conv_bwd_call(x, w, dy))


gdn_conv.defvjp(_gdn_conv_f, _gdn_conv_b)


BNN = (((2,), (1,)), ((0,), (0,)))
BNT = (((2,), (2,)), ((0,), (0,)))
BTN = (((1,), (1,)), ((0,), (0,)))
GDN_PREP_BATCH = 8


def _gdn_prep_math(q, k, v, beta, g):
    B, C = q.shape[0], q.shape[1]
    ri = lax.broadcasted_iota(jnp.int32, (B, C, C), 1)
    ci = lax.broadcasted_iota(jnp.int32, (B, C, C), 2)
    causal = ri >= ci
    strict = ri > ci
    eye = (ri == ci).astype(F32)
    gb = jnp.broadcast_to(g, (B, C, C))
    g_row = jnp.sum(gb * eye, axis=1, keepdims=True)
    gc_col = jnp.sum(jnp.where(causal, jnp.broadcast_to(g_row, (B, C, C)), 0.0), axis=2, keepdims=True)
    gc_row = jnp.sum(jnp.where(ri <= ci, gb, 0.0), axis=1, keepdims=True)
    decay = jnp.exp(jnp.where(causal, gc_col - gc_row, -jnp.inf))
    kk = _bdot(k, k, BNT)
    a_mat = jnp.where(strict, beta * kk * decay, 0.0)
    n = -a_mat
    t = eye + n
    for _ in range(int(math.log2(C)) - 1):
        n = _hdot_bnn(n, n)
        t = t + _hdot_bnn(t, n)
    e_gc = jnp.exp(gc_col)
    w = _hdot_bnn(t, beta * e_gc * k)
    u = _hdot_bnn(t, beta * v)
    qk = _bdot(q, k, BNT) * decay
    q_dec = q * e_gc
    g_last = gc_col[:, C - 1:C, :]
    k_dec = k * jnp.exp(g_last - gc_col)
    return q_dec, w, u, qk, k_dec, gc_col


def _gdn_prep_specs(L):
    C = GDN_CHUNK
    nb = min(GDN_PREP_BATCH, L // C)
    R = nb * C
    ins = [pl.BlockSpec((R, GDN_DK), lambda h, c: (c, h)), pl.BlockSpec((R, GDN_DK), lambda h, c: (c, h)),
           pl.BlockSpec((R, GDN_DV), lambda h, c: (c, h)),
           pl.BlockSpec((1, R, 1), lambda h, c: (h, c, 0)), pl.BlockSpec((1, R, 1), lambda h, c: (h, c, 0))]
    outs = [pl.BlockSpec((1, R, GDN_DK), lambda h, c: (h, c, 0)), pl.BlockSpec((1, R, GDN_DK), lambda h, c: (h, c, 0)),
            pl.BlockSpec((1, R, GDN_DV), lambda h, c: (h, c, 0)), pl.BlockSpec((1, R, C), lambda h, c: (h, c, 0)),
            pl.BlockSpec((1, R, GDN_DK), lambda h, c: (h, c, 0)), pl.BlockSpec((1, R, 1), lambda h, c: (h, c, 0))]
    shapes = [SDS((GDN_HEADS, L, GDN_DK), F32), SDS((GDN_HEADS, L, GDN_DK), F32), SDS((GDN_HEADS, L, GDN_DV), F32),
              SDS((GDN_HEADS, L, C), F32), SDS((GDN_HEADS, L, GDN_DK), F32), SDS((GDN_HEADS, L, 1), F32)]
    return ins, outs, shapes, nb


def _chunks(x, nb):
    return x.reshape(nb, x.shape[0] // nb, x.shape[1])


def _gdn_prep_fwd_call(q, k, v, beta, g):
    L = q.shape[0]
    ins, outs, shapes, nb = _gdn_prep_specs(L)

    def body(q_ref, k_ref, v_ref, b_ref, g_ref, *o_refs):
        res = _gdn_prep_math(_chunks(q_ref[...], nb), _chunks(k_ref[...], nb), _chunks(v_ref[...], nb),
                             _chunks(b_ref[0], nb), _chunks(g_ref[0], nb))
        for o_ref, val in zip(o_refs, res):
            o_ref[0] = val.reshape(val.shape[0] * val.shape[1], val.shape[2])

    return pl.pallas_call(
        body, name="gdn_prep_fwd", grid=(GDN_HEADS, L // (nb * GDN_CHUNK)), in_specs=ins, out_specs=outs, out_shape=shapes,
        compiler_params=_params(("parallel", "parallel"), VMEM_MID),
    )(q, k, v, beta, g)


def _gdn_prep_bwd_call(q, k, v, beta, g, cts):
    L = q.shape[0]
    ins, outs, _, nb = _gdn_prep_specs(L)

    def body(q_ref, k_ref, v_ref, b_ref, g_ref, c0, c1, c2, c3, c4, c5, dq_ref, dk_ref, dv_ref, db_ref, dg_ref):
        _, vjp = jax.vjp(_gdn_prep_math, _chunks(q_ref[...], nb), _chunks(k_ref[...], nb), _chunks(v_ref[...], nb),
                         _chunks(b_ref[0], nb), _chunks(g_ref[0], nb))
        dq, dk, dv, db, dg = vjp(tuple(_chunks(c[0], nb) for c in (c0, c1, c2, c3, c4, c5)))
        flat = lambda t: t.reshape(t.shape[0] * t.shape[1], t.shape[2])
        dq_ref[...] = flat(dq)
        dk_ref[...] = flat(dk)
        dv_ref[...] = flat(dv)
        db_ref[0] = flat(db)
        dg_ref[0] = flat(dg)

    return pl.pallas_call(
        body, name="gdn_prep_bwd", grid=(GDN_HEADS, L // (nb * GDN_CHUNK)), in_specs=ins + outs, out_specs=ins,
        out_shape=[SDS(q.shape, F32), SDS(k.shape, F32), SDS(v.shape, F32), SDS(beta.shape, F32), SDS(g.shape, F32)],
        compiler_params=_params(("parallel", "parallel"), VMEM_MID),
    )(q, k, v, beta, g, *cts)


@jax.custom_vjp
def gdn_prep(q, k, v, beta, g):
    return tuple(_gdn_prep_fwd_call(q, k, v, beta, g))


def _gdn_prep_f(q, k, v, beta, g):
    return tuple(_gdn_prep_fwd_call(q, k, v, beta, g)), (q, k, v, beta, g)


def _gdn_prep_b(res, cts):
    return tuple(_gdn_prep_bwd_call(*res, tuple(cts)))


gdn_prep.defvjp(_gdn_prep_f, _gdn_prep_b)


def _gdn_step_math(q_dec, w, u, qk, k_dec, gc, state):
    H, C = q_dec.shape[0], q_dec.shape[1]
    v_new = u - _bdot(w, state, BNN)
    o = _bdot(q_dec, state, BNN) + _bdot(qk, v_new, BNN)
    gl = gc[:, C - 1:C, :]
    new_state = jnp.exp(gl) * state + _bdot(k_dec, v_new, BTN)
    return jnp.concatenate([o[h] for h in range(H)], axis=1), new_state


def _gdn_scan_specs(L, rev):
    C, H = GDN_CHUNK, GDN_HEADS
    nc = L // C
    cc = (lambda c: nc - 1 - c) if rev else (lambda c: c)
    ins = [pl.BlockSpec((H, C, GDN_DK), lambda c: (0, cc(c), 0)), pl.BlockSpec((H, C, GDN_DK), lambda c: (0, cc(c), 0)),
           pl.BlockSpec((H, C, GDN_DV), lambda c: (0, cc(c), 0)), pl.BlockSpec((H, C, C), lambda c: (0, cc(c), 0)),
           pl.BlockSpec((H, C, GDN_DK), lambda c: (0, cc(c), 0)), pl.BlockSpec((H, C, 1), lambda c: (0, cc(c), 0))]
    o_spec = pl.BlockSpec((C, H * GDN_DV), lambda c: (cc(c), 0))
    s_spec = pl.BlockSpec((1, H, GDN_DK, GDN_DV), lambda c: (cc(c), 0, 0, 0))
    return ins, o_spec, s_spec, nc


def _gdn_scan_fwd_call(q_dec, w, u, qk, k_dec, gc):
    L = q_dec.shape[1]
    ins, o_spec, s_spec, nc = _gdn_scan_specs(L, False)

    def body(qd_ref, w_ref, u_ref, qk_ref, kd_ref, gc_ref, o_ref, sin_ref, s_ref):
        c = pl.program_id(0)

        @pl.when(c == 0)
        def _():
            s_ref[...] = jnp.zeros_like(s_ref)
        st = s_ref[...]
        sin_ref[0] = st
        o, ns = _gdn_step_math(qd_ref[...], w_ref[...], u_ref[...], qk_ref[...], kd_ref[...], gc_ref[...], st)
        o_ref[...] = o
        s_ref[...] = ns

    return pl.pallas_call(
        body, name="gdn_scan_fwd", grid=(nc,), in_specs=ins, out_specs=[o_spec, s_spec],
        out_shape=[SDS((L, GDN_HEADS * GDN_DV), F32), SDS((nc, GDN_HEADS, GDN_DK, GDN_DV), F32)],
        scratch_shapes=[pltpu.VMEM((GDN_HEADS, GDN_DK, GDN_DV), F32)],
        compiler_params=_params(("arbitrary",), VMEM_MID),
    )(q_dec, w, u, qk, k_dec, gc)


def _gdn_scan_bwd_call(q_dec, w, u, qk, k_dec, gc, s_in, do):
    L = q_dec.shape[1]
    ins, o_spec, s_spec, nc = _gdn_scan_specs(L, True)

    def body(qd_ref, w_ref, u_ref, qk_ref, kd_ref, gc_ref, sin_ref, do_ref,
             dqd_ref, dw_ref, du_ref, dqk_ref, dkd_ref, dgc_ref, ds_ref):
        c = pl.program_id(0)

        @pl.when(c == 0)
        def _():
            ds_ref[...] = jnp.zeros_like(ds_ref)
        _, vjp = jax.vjp(_gdn_step_math, qd_ref[...], w_ref[...], u_ref[...], qk_ref[...], kd_ref[...], gc_ref[...], sin_ref[0])
        dqd, dw, du, dqk, dkd, dgc, dst = vjp((do_ref[...], ds_ref[...]))
        dqd_ref[...] = dqd
        dw_ref[...] = dw
        du_ref[...] = du
        dqk_ref[...] = dqk
        dkd_ref[...] = dkd
        dgc_ref[...] = dgc
        ds_ref[...] = dst

    return pl.pallas_call(
        body, name="gdn_scan_bwd", grid=(nc,), in_specs=ins + [s_spec, o_spec], out_specs=ins,
        out_shape=[SDS(t.shape, F32) for t in (q_dec, w, u, qk, k_dec, gc)],
        scratch_shapes=[pltpu.VMEM((GDN_HEADS, GDN_DK, GDN_DV), F32)],
        compiler_params=_params(("arbitrary",), VMEM_MID),
    )(q_dec, w, u, qk, k_dec, gc, s_in, do)


@jax.custom_vjp
def gdn_scan(q_dec, w, u, qk, k_dec, gc):
    return _gdn_scan_fwd_call(q_dec, w, u, qk, k_dec, gc)[0]


def _gdn_scan_f(*args):
    o, s_in = _gdn_scan_fwd_call(*args)
    return o, (*args, s_in)


def _gdn_scan_b(res, do):
    return tuple(_gdn_scan_bwd_call(*res, do))


gdn_scan.defvjp(_gdn_scan_f, _gdn_scan_b)


def _silu(x):
    return x * jax.nn.sigmoid(x)


def _gelu_tanh(x):
    return 0.5 * x * (1.0 + jnp.tanh(math.sqrt(2.0 / math.pi) * (x + 0.044715 * (x * x * x))))


def _f_lnmod(x, nw, sc, sh, bsc, bsh):
    xn = x * lax.rsqrt(jnp.mean(x * x, axis=-1, keepdims=True) + NORM_EPS) * nw
    return (xn * (1.0 + (sc + bsc)) + (sh + bsh),)


def _f_s5_act(ys, u, d):
    return (_gelu_tanh(ys + d * u),)


def _f_s5_gate(y2, t, z):
    return (y2 * jax.nn.sigmoid(t) * _silu(z),)


def _f_res(x, y, gate, bgate):
    return (x + (gate + bgate) * y,)


def _heads(x, width, fn):
    return jnp.concatenate([fn(x[:, i * width:(i + 1) * width]) for i in range(x.shape[1] // width)], axis=1)


def _l2n(x):
    return x * lax.rsqrt(jnp.sum(x * x, axis=-1, keepdims=True) + NORM_EPS)


def _f_qnorm(x):
    return (_heads(_silu(x), GDN_DK, _l2n) * (GDN_DK ** -0.5),)


def _f_knorm(x):
    return (_heads(_silu(x), GDN_DK, _l2n),)


def _f_vact(x):
    return (_silu(x),)


def _f_betag(ba, alog, dtb):
    col = lax.broadcasted_iota(jnp.int32, ba.shape, 1)
    t = ba + dtb
    softplus = jnp.maximum(t, 0.0) + jnp.log1p(jnp.exp(-jnp.abs(t)))
    g = -jnp.exp(alog) * softplus
    return (jnp.where(col < GDN_HEADS, jax.nn.sigmoid(ba), jnp.where(col < 2 * GDN_HEADS, g, 0.0)),)


def _f_gdn_post(o, z, nw):
    on = _heads(o, GDN_DV, lambda t: t * lax.rsqrt(jnp.mean(t * t, axis=-1, keepdims=True) + NORM_EPS))
    return (on * nw * _silu(z),)


def _f_loss(x, tgt, fw):
    y = x * lax.rsqrt(jnp.mean(x * x, axis=-1, keepdims=True) + NORM_EPS) * fw
    err = y - tgt
    return (0.5 * jnp.mean(err * err, axis=-1, keepdims=True),)


def _ada_mod_call(c_all, ada_w):
    n = ada_w.shape[2]

    def body(c_ref, w_ref, o_ref):
        ca = _silu(c_ref[...])
        for l in range(ada_w.shape[0]):
            o_ref[l] = _bdot(ca, w_ref[l])

    return pl.pallas_call(body, name="ada_mod", out_shape=SDS((ada_w.shape[0], N_DEV, n), F32),
                          compiler_params=_params(None, VMEM_MID))(c_all, ada_w)


def _ada_grad_call(c_all, dmod):
    nl, _, n = dmod.shape

    def body(c_ref, d_ref, o_ref):
        ca = _silu(c_ref[...])
        for l in range(nl):
            o_ref[l] = _hdot(ca, d_ref[l], TN)

    return pl.pallas_call(body, name="ada_grad", out_shape=SDS((nl, c_all.shape[1], n), F32),
                          compiler_params=_params(None, VMEM_MID))(c_all, dmod)


ADAM_ROWS = 512


def _adamw(g, w, m, v):
    m2 = ADAM_B1 * m + (1.0 - ADAM_B1) * g
    v2 = ADAM_B2 * v + (1.0 - ADAM_B2) * (g * g)
    m_hat = m2 / (1.0 - ADAM_B1 ** ADAM_STEP)
    v_hat = v2 / (1.0 - ADAM_B2 ** ADAM_STEP)
    return g, -ADAM_LR * (m_hat / (jnp.sqrt(v_hat) + ADAM_EPS) + ADAM_WD * w), m2, v2


def _adam_call(gs, w, m, v, name, rows=None):
    n, r, cols = gs.shape
    rows = rows or ADAM_ROWS

    def body(g_ref, w_ref, m_ref, v_ref, go_ref, d_ref, mo_ref, vo_ref):
        g = g_ref[0]
        for s in range(1, n):
            g = g + g_ref[s]
        for o_ref, val in zip((go_ref, d_ref, mo_ref, vo_ref), _adamw(g, w_ref[...], m_ref[...], v_ref[...])):
            o_ref[...] = val

    blk = pl.BlockSpec((rows, cols), lambda i: (i, 0))
    return pl.pallas_call(
        body, name=name, grid=(r // rows,),
        in_specs=[pl.BlockSpec((n, rows, cols), lambda i: (0, i, 0)), blk, blk, blk],
        out_specs=[blk, blk, blk, blk], out_shape=[SDS((r, cols), F32)] * 4,
        compiler_params=_params(("parallel",), VMEM_MID),
    )(gs, w, m, v)


def _sum_call(gs, name):
    n, r, _ = gs.shape

    def body(g_ref, o_ref):
        g = g_ref[0]
        for s in range(1, n):
            g = g + g_ref[s]
        o_ref[...] = g

    return pl.pallas_call(
        body, name=name, grid=(r // ADAM_ROWS,),
        in_specs=[pl.BlockSpec((n, ADAM_ROWS, LANES), lambda i: (0, i, 0))],
        out_specs=pl.BlockSpec((ADAM_ROWS, LANES), lambda i: (i, 0)), out_shape=SDS((r, LANES), F32),
        compiler_params=_params(("parallel",), VMEM_MID),
    )(gs)


def _allgather_call(x_shard, name, in_hbm):
    m_per, n = x_shard.shape

    def body(x_ref, out_ref, send_sems, recv_sems, local_sem):
        x, y, c = lax.axis_index("x"), lax.axis_index("y"), lax.axis_index("c")
        me, sibling = (x, y, c), (x, y, 1 - c)
        chips = [(1 - x, y), (x, 1 - y), (1 - x, 1 - y)]

        def rows(px, py, pc):
            return out_ref.at[pl.ds((4 * px + 2 * py + pc) * m_per, m_per), :]

        def copy(k, block, to, src=None):
            return pltpu.make_async_remote_copy(
                src_ref=rows(*block) if src is None else src, dst_ref=rows(*block),
                send_sem=send_sems.at[k], recv_sem=recv_sems.at[k], device_id=to, device_id_type=pl.DeviceIdType.MESH)

        mine = pltpu.make_async_copy(x_ref, rows(*me), local_sem)
        mine.start()
        first = [copy(0, me, sibling, src=x_ref)]
        first += [copy(1 + j, me, (*chip, c), src=x_ref) for j, chip in enumerate(chips)]
        for cp in first:
            cp.start()
        passed = [copy(4 + j, (*chip, c), sibling) for j, chip in enumerate(chips)]
        for j, chip in enumerate(chips):
            copy(1 + j, (*chip, c), me).wait_recv()
            passed[j].start()
        copy(0, sibling, me).wait_recv()
        for j, chip in enumerate(chips):
            copy(4 + j, (*chip, 1 - c), me).wait_recv()
        for cp in first + passed:
            cp.wait_send()
        mine.wait()

    space = pl.ANY if in_hbm else pltpu.VMEM
    return pl.pallas_call(
        body, name=name, out_shape=SDS((N_DEV * m_per, n), x_shard.dtype),
        in_specs=[pl.BlockSpec(memory_space=space)], out_specs=pl.BlockSpec(memory_space=space),
        scratch_shapes=[pltpu.SemaphoreType.DMA((7,)), pltpu.SemaphoreType.DMA((7,)), pltpu.SemaphoreType.DMA],
        compiler_params=_params(None, None if in_hbm else VMEM_BIG),
    )(x_shard)


def _gather_weights_call(shards, name):
    nw = len(shards)

    def body(*refs):
        x_refs, out_refs = refs[:nw], refs[nw:2 * nw]
        send_sems, recv_sems, local_sems = refs[2 * nw:]
        x, y, c = lax.axis_index("x"), lax.axis_index("y"), lax.axis_index("c")
        me, sibling = (x, y, c), (x, y, 1 - c)
        chips = [(1 - x, y), (x, 1 - y), (1 - x, 1 - y)]

        def slot(w, px, py, pc):
            return out_refs[w].at[4 * px + 2 * py + pc]

        def copy(w, k, block, to, src=None):
            dst = slot(w, *block)
            return pltpu.make_async_remote_copy(
                src_ref=dst if src is None else src, dst_ref=dst, send_sem=send_sems.at[7 * w + k],
                recv_sem=recv_sems.at[7 * w + k], device_id=to, device_id_type=pl.DeviceIdType.MESH)

        mines = [pltpu.make_async_copy(x_refs[w], slot(w, *me), local_sems.at[w]) for w in range(nw)]
        for cp in mines:
            cp.start()
        first = [copy(w, 0, me, sibling, src=x_refs[w]) for w in range(nw)]
        first += [copy(w, 1 + j, me, (*chip, c), src=x_refs[w]) for w in range(nw) for j, chip in enumerate(chips)]
        for cp in first:
            cp.start()
        passed = []
        for w in range(nw):
            for j, chip in enumerate(chips):
                copy(w, 1 + j, (*chip, c), me).wait_recv()
                fwd = copy(w, 4 + j, (*chip, c), sibling)
                fwd.start()
                passed.append(fwd)
        for w in range(nw):
            copy(w, 0, sibling, me).wait_recv()
            for j, chip in enumerate(chips):
                copy(w, 4 + j, (*chip, 1 - c), me).wait_recv()
        for cp in first + passed:
            cp.wait_send()
        for cp in mines:
            cp.wait()

    hbm = pl.BlockSpec(memory_space=pl.ANY)
    return pl.pallas_call(
        body, name=name, out_shape=[SDS((N_DEV,) + s.shape, s.dtype) for s in shards],
        in_specs=[hbm] * nw, out_specs=[hbm] * nw,
        scratch_shapes=[pltpu.SemaphoreType.DMA((7 * nw,)), pltpu.SemaphoreType.DMA((7 * nw,)), pltpu.SemaphoreType.DMA((nw,))],
    )(*shards)


def _pair_exchange_call(grads, name):
    nw = len(grads)

    def body(*refs):
        g_refs, got_refs = refs[:nw], refs[nw:2 * nw]
        send_sems, recv_sems = refs[2 * nw:]
        x, y, c = lax.axis_index("x"), lax.axis_index("y"), lax.axis_index("c")
        copies = []
        for w in range(nw):
            for j in range(4):
                give = pltpu.make_async_remote_copy(
                    src_ref=g_refs[w].at[2 * j + 1 - c], dst_ref=got_refs[w].at[j], send_sem=send_sems.at[4 * w + j],
                    recv_sem=recv_sems.at[4 * w + j], device_id=(x, y, 1 - c), device_id_type=pl.DeviceIdType.MESH)
                give.start()
                copies.append(give)
        for cp in copies:
            cp.wait()

    hbm = pl.BlockSpec(memory_space=pl.ANY)
    return pl.pallas_call(
        body, name=name, out_shape=[SDS((4,) + g.shape[1:], g.dtype) for g in grads], in_specs=[hbm] * nw, out_specs=[hbm] * nw,
        scratch_shapes=[pltpu.SemaphoreType.DMA((4 * nw,)), pltpu.SemaphoreType.DMA((4 * nw,))],
    )(*grads)


def _chip_exchange_call(parts, name):
    nw = len(parts)

    def body(*refs):
        p_refs, out_refs = refs[:nw], refs[nw:2 * nw]
        send_sems, recv_sems = refs[2 * nw:]
        x, y, c = lax.axis_index("x"), lax.axis_index("y"), lax.axis_index("c")
        chips = [(1 - x, y), (x, 1 - y), (1 - x, 1 - y)]
        copies = []
        for w in range(nw):
            for j, (px, py) in enumerate(chips):
                give = pltpu.make_async_remote_copy(
                    src_ref=p_refs[w].at[2 * px + py], dst_ref=out_refs[w].at[j], send_sem=send_sems.at[3 * w + j],
                    recv_sem=recv_sems.at[3 * w + j], device_id=(px, py, c), device_id_type=pl.DeviceIdType.MESH)
                give.start()
                copies.append(give)
        for cp in copies:
            cp.wait()

    hbm = pl.BlockSpec(memory_space=pl.ANY)
    return pl.pallas_call(
        body, name=name, out_shape=[SDS((3,) + p.shape[1:], p.dtype) for p in parts], in_specs=[hbm] * nw, out_specs=[hbm] * nw,
        scratch_shapes=[pltpu.SemaphoreType.DMA((3 * nw,)), pltpu.SemaphoreType.DMA((3 * nw,))],
    )(*parts)


def _pair_sum_call(g, got, core, name):
    _, k, n = got.shape
    tr = _tile(k, 256)

    def body(c_ref, g_ref, got_ref, o_ref):
        o_ref[...] = (g_ref[...] + got_ref[...]).astype(o_ref.dtype)

    spec = pltpu.PrefetchScalarGridSpec(
        num_scalar_prefetch=1, grid=(4, k // tr),
        in_specs=[pl.BlockSpec((1, tr, n), lambda j, i, c: (2 * j + c[0], i, 0)), pl.BlockSpec((1, tr, n), lambda j, i, c: (j, i, 0))],
        out_specs=pl.BlockSpec((1, tr, n), lambda j, i, c: (j, i, 0)))
    return pl.pallas_call(body, name=name, grid_spec=spec, out_shape=SDS(got.shape, BF16),
                          compiler_params=_params(("parallel", "parallel"), VMEM_MID))(core, g, got)


def _adam_own_call(pair, chip, recv, w, m, v, name, rows):
    _, r, cols = recv.shape

    def body(chip_ref, p_ref, g_ref, w_ref, m_ref, v_ref, go_ref, d_ref, mo_ref, vo_ref):
        g = ((p_ref[0].astype(F32) + g_ref[0].astype(F32)) + g_ref[1].astype(F32)) + g_ref[2].astype(F32)
        for o_ref, val in zip((go_ref, d_ref, mo_ref, vo_ref), _adamw(g, w_ref[...], m_ref[...], v_ref[...])):
            o_ref[...] = val

    blk = pl.BlockSpec((rows, cols), lambda i, s: (i, 0))
    spec = pltpu.PrefetchScalarGridSpec(
        num_scalar_prefetch=1, grid=(r // rows,),
        in_specs=[pl.BlockSpec((1, rows, cols), lambda i, s: (s[0], i, 0)), pl.BlockSpec((3, rows, cols), lambda i, s: (0, i, 0)),
                  blk, blk, blk],
        out_specs=[blk, blk, blk, blk])
    return pl.pallas_call(body, name=name, grid_spec=spec, out_shape=[SDS((r, cols), F32)] * 4,
                          compiler_params=_params(("parallel",), VMEM_MID))(chip, pair, recv, w, m, v)


def _join_cols_call(w8, name):
    _, k, n = w8.shape
    tk = _tile(k, 256)

    def body(w_ref, o_ref):
        for s in range(N_DEV):
            o_ref[:, n * s:n * (s + 1)] = w_ref[s]

    return pl.pallas_call(body, name=name, grid=(k // tk,), in_specs=[pl.BlockSpec((N_DEV, tk, n), lambda i: (0, i, 0))],
                          out_specs=pl.BlockSpec((tk, N_DEV * n), lambda i: (i, 0)), out_shape=SDS((k, N_DEV * n), w8.dtype),
                          compiler_params=_params(("parallel",), VMEM_MID))(w8)


def _split_cols_call(g, name):
    k, n8 = g.shape
    n = n8 // N_DEV
    tk = _tile(k, 256)

    def body(g_ref, o_ref):
        for s in range(N_DEV):
            o_ref[s] = g_ref[:, n * s:n * (s + 1)]

    return pl.pallas_call(body, name=name, grid=(k // tk,), in_specs=[pl.BlockSpec((tk, n8), lambda i: (i, 0))],
                          out_specs=pl.BlockSpec((N_DEV, tk, n), lambda i: (0, i, 0)), out_shape=SDS((N_DEV, k, n), g.dtype),
                          compiler_params=_params(("parallel",), VMEM_MID))(g)


def _pack(parts, rows_multiple):
    flat = jnp.concatenate([p.reshape(-1) for p in parts])
    unit = rows_multiple * LANES
    padded = -(-flat.shape[0] // unit) * unit
    flat = jnp.concatenate([flat, jnp.zeros((padded - flat.shape[0],), F32)])
    return flat.reshape(-1, LANES)


def _unpack(buf, shapes):
    flat = buf.reshape(-1)
    out, off = [], 0
    for s in shapes:
        n = math.prod(s)
        out.append(flat[off:off + n].reshape(s))
        off += n
    return out


def _local_loss(diff, const, L):
    (x, mod_raw, norm_w, lam_re, lam_im, log_dt, b_re, b_im, c_re, c_im, s5_d, conv_w, a_log, dt_bias, gdn_nw, final_nw,
     *slots) = diff
    tgt, ada_b, weights = const
    lin = {n: (lambda a, n=n, i=i: make_mm(n)(a, weights[i], slots[i])) for i, n in enumerate(MM_NAMES)}
    tm = 256 if L % 256 == 0 else L
    mods = mod_raw.reshape(2, 3, 1, D_MODEL)
    biases = ada_b.reshape(2, 3, 1, D_MODEL)

    op_ln0 = make_rowwise(_f_lnmod, "ln0", tm, 1, 5)
    (h,) = op_ln0((x,), (norm_w[0:1], mods[0, 1], mods[0, 0], biases[0, 1], biases[0, 0]))
    u = lin["s5_in_u"](h)
    z = lin["s5_in_z"](h)
    blocks = _s5_block_params(lam_re, lam_im, log_dt, b_re, b_im, c_re, c_im)
    ys = make_s5_core(min(S5_TL, L))(u, *blocks)
    (y2,) = make_rowwise(_f_s5_act, "s5_act", tm, 2, 1)((ys, u), (s5_d.reshape(1, D_INNER),))
    t = lin["s5_glu"](y2)
    (y4,) = make_rowwise(_f_s5_gate, "s5_gate", tm, 3, 0)((y2, t, z), ())
    o = lin["s5_out"](y4)
    (x1,) = make_rowwise(_f_res, "res0", tm, 2, 2)((x, o), (mods[0, 2], biases[0, 2]))

    op_ln1 = make_rowwise(_f_lnmod, "ln1", tm, 1, 5)
    (h,) = op_ln1((x1,), (norm_w[1:2], mods[1, 1], mods[1, 0], biases[1, 1], biases[1, 0]))
    q0 = lin["gdn_in_q"](h)
    k0 = lin["gdn_in_k"](h)
    v0 = lin["gdn_in_v"](h)
    gz = lin["gdn_in_z"](h)
    ba = lin["gdn_in_ba"](h)
    cw = jnp.concatenate([conv_w, jnp.zeros((SUBLANES - GDN_CONV, GDN_CONV_CH), F32)], axis=0)
    (q,) = make_rowwise(_f_qnorm, "gdn_qn", tm, 1, 0)((gdn_conv(q0, cw[:, :GDN_QK]),), ())
    (k,) = make_rowwise(_f_knorm, "gdn_kn", tm, 1, 0)((gdn_conv(k0, cw[:, GDN_QK:2 * GDN_QK]),), ())
    (v,) = make_rowwise(_f_vact, "gdn_va", tm, 1, 0)((gdn_conv(v0, cw[:, 2 * GDN_QK:]),), ())
    pad = jnp.zeros((LANES - 2 * GDN_HEADS,), F32)
    alog_row = jnp.concatenate([jnp.zeros((GDN_HEADS,), F32), a_log, pad]).reshape(1, LANES)
    dtb_row = jnp.concatenate([jnp.zeros((GDN_HEADS,), F32), dt_bias, pad]).reshape(1, LANES)
    (bg,) = make_rowwise(_f_betag, "gdn_bg", tm, 1, 2)((ba,), (alog_row, dtb_row))
    beta = bg[:, :GDN_HEADS].T.reshape(GDN_HEADS, L, 1)
    g = bg[:, GDN_HEADS:2 * GDN_HEADS].T.reshape(GDN_HEADS, L, 1)
    og = gdn_scan(*gdn_prep(q, k, v, beta, g))
    nw_row = jnp.tile(gdn_nw, GDN_HEADS).reshape(1, D_INNER)
    (on,) = make_rowwise(_f_gdn_post, "gdn_post", tm, 2, 1)((og, gz), (nw_row,))
    y = lin["gdn_out"](on)
    (x2,) = make_rowwise(_f_res, "res1", tm, 2, 2)((x1, y), (mods[1, 2], biases[1, 2]))

    (lt,) = make_rowwise(_f_loss, "loss", tm, 2, 1)((x2, tgt), (final_nw.reshape(1, D_MODEL),))
    return jnp.sum(lt)


MM_NAMES = ("s5_in_u", "s5_in_z", "s5_glu", "s5_out", "gdn_in_q", "gdn_in_k", "gdn_in_v", "gdn_in_z", "gdn_in_ba", "gdn_out")
SMALL_NAMES = ("ada_b", "norm_w", "s5_lambda_re", "s5_lambda_im", "s5_log_dt", "s5_b_re", "s5_b_im", "s5_c_re", "s5_c_im",
               "s5_d", "gdn_a_log", "gdn_dt_bias", "final_norm_w")
BIG_NAMES = ("s5_w_in", "s5_w_glu", "s5_w_out", "gdn_w_in", "gdn_w_out")
WEIGHT_ORDER = ("ada_w", "ada_b", "norm_w", "s5_w_in", "s5_lambda_re", "s5_lambda_im", "s5_log_dt", "s5_b_re", "s5_b_im",
                "s5_c_re", "s5_c_im", "s5_d", "s5_w_glu", "s5_w_out", "gdn_w_in", "gdn_conv_w", "gdn_a_log", "gdn_dt_bias",
                "gdn_norm_w", "gdn_w_out", "final_norm_w")


def _step(x, c, W, M, V, tgt):
    L = x.shape[1]
    ix, iy, ic = lax.axis_index("x"), lax.axis_index("y"), lax.axis_index("c")
    me = 4 * ix + 2 * iy + ic
    n_ada = W["ada_w"].shape[2]
    n_conv = W["gdn_conv_w"].shape[2]
    n_gnw = W["gdn_norm_w"].shape[1]

    g1 = _allgather_call(_pack([c, W["gdn_conv_w"], W["gdn_norm_w"]], SUBLANES), "gather_small_in", False)
    g1 = g1.reshape(N_DEV, -1)
    c_all = g1[:, :D_MODEL]
    conv_w = g1[:, D_MODEL:D_MODEL + GDN_CONV * n_conv].reshape(N_DEV, GDN_CONV, n_conv).transpose(1, 0, 2).reshape(GDN_CONV, -1)
    gdn_nw = g1[:, D_MODEL + GDN_CONV * n_conv:D_MODEL + GDN_CONV * n_conv + n_gnw].reshape(-1)
    mod_part = _ada_mod_call(c_all, W["ada_w"])
    g2 = _allgather_call(_pack([mod_part], SUBLANES), "gather_mod", False).reshape(N_DEV, -1)
    mod_all = g2[:, :2 * N_DEV * n_ada].reshape(N_DEV, 2, N_DEV, n_ada)
    mod_raw = lax.dynamic_index_in_dim(mod_all, me, axis=2, keepdims=False)
    mod_raw = mod_raw.transpose(1, 0, 2).reshape(2, 3 * D_MODEL)

    gathered = _gather_weights_call([W[n][0].astype(BF16) for n in BIG_NAMES], "gather_weights")
    full = dict(zip(BIG_NAMES, gathered))
    w_in5 = _join_cols_call(full["s5_w_in"], "join_s5_w_in")
    w_ing = _join_cols_call(full["gdn_w_in"], "join_gdn_w_in")
    w_ba = jnp.concatenate([w_ing[:, GDN_CONV_CH + D_INNER:], jnp.zeros((D_MODEL, LANES - 2 * GDN_HEADS), BF16)], axis=1)
    weights = (w_in5[:, :D_INNER], w_in5[:, D_INNER:], full["s5_w_glu"].reshape(D_INNER, D_INNER),
               full["s5_w_out"].reshape(D_INNER, D_MODEL),
               w_ing[:, :GDN_QK], w_ing[:, GDN_QK:2 * GDN_QK], w_ing[:, 2 * GDN_QK:GDN_CONV_CH],
               w_ing[:, GDN_CONV_CH:GDN_CONV_CH + D_INNER], w_ba, full["gdn_w_out"].reshape(D_INNER, D_MODEL))
    slots = tuple(jnp.zeros(w.shape, F32) for w in weights)
    diff = (x[0], mod_raw, W["norm_w"], W["s5_lambda_re"][0], W["s5_lambda_im"][0], W["s5_log_dt"][0], W["s5_b_re"][0],
            W["s5_b_im"][0], W["s5_c_re"][0], W["s5_c_im"][0], W["s5_d"][0], conv_w, W["gdn_a_log"][0], W["gdn_dt_bias"][0],
            gdn_nw, W["final_norm_w"], *slots)

    loss_local, grads = jax.value_and_grad(_local_loss)(diff, (tgt[0], W["ada_b"], weights), L)
    (dx, dmod, d_norm_w, d_lre, d_lim, d_logdt, d_bre, d_bim, d_cre, d_cim, d_s5d, d_conv, d_alog, d_dtb, d_gnw, d_fnw,
     d_wu, d_wz, d_wglu, d_wo5, d_wq, d_wk, d_wv, d_wgz, d_wba, d_wog) = grads
    loss = lax.psum(loss_local, MESH_AXES)

    d_in5 = _split_cols_call(jnp.concatenate([d_wu, d_wz], axis=1), "split_s5_w_in")
    d_ing = _split_cols_call(jnp.concatenate([d_wq, d_wk, d_wv, d_wgz, d_wba[:, :2 * GDN_HEADS]], axis=1), "split_gdn_w_in")
    rows = lambda d: d.reshape(N_DEV, d.shape[0] // N_DEV, d.shape[1])
    per_dev = [d_in5, rows(d_wglu), rows(d_wo5), d_ing, rows(d_wog)]
    got = _pair_exchange_call(per_dev, "scatter_grads_pair")
    core = jnp.reshape(ic, (1,)).astype(jnp.int32)
    chip = jnp.reshape(2 * ix + iy, (1,)).astype(jnp.int32)
    pair = [_pair_sum_call(g, r, core, "pair_sum_" + n) for g, r, n in zip(per_dev, got, BIG_NAMES)]
    recv = _chip_exchange_call(pair, "scatter_grads_chips")
    big = [_adam_own_call(p, chip, r, W[n][0], M[n][0], V[n][0], "adam_" + n, _tile(W[n].shape[1], 128))
           for p, r, n in zip(pair, recv, BIG_NAMES)]
    big = [[o[None] for o in outs] for outs in big]

    small_parts = [dmod, d_norm_w, d_lre, d_lim, d_logdt, d_bre, d_bim, d_cre, d_cim, d_s5d, d_alog, d_dtb, d_fnw, d_conv, d_gnw]
    small_shapes = [p.shape for p in small_parts]
    sg = _allgather_call(_pack(small_parts, ADAM_ROWS), "gather_small_grads", False)
    sg = sg.reshape(N_DEV, -1, LANES)
    tot = _unpack(_sum_call(sg, "sum_small_grads"), small_shapes)
    (g_adab, g_norm_w, g_lre, g_lim, g_logdt, g_bre, g_bim, g_cre, g_cim, g_s5d, g_alog, g_dtb, g_fnw, g_conv, g_gnw) = tot
    g_conv_mine = lax.dynamic_slice_in_dim(g_conv, me * n_conv, n_conv, axis=1)
    g_gnw_mine = lax.dynamic_slice_in_dim(g_gnw, me * n_gnw, n_gnw, axis=0)
    small_g = {"ada_b": g_adab.reshape(W["ada_b"].shape), "norm_w": g_norm_w, "s5_lambda_re": g_lre[None], "s5_lambda_im": g_lim[None],
               "s5_log_dt": g_logdt[None], "s5_b_re": g_bre[None], "s5_b_im": g_bim[None], "s5_c_re": g_cre[None],
               "s5_c_im": g_cim[None], "s5_d": g_s5d[None], "gdn_a_log": g_alog[None], "gdn_dt_bias": g_dtb[None],
               "final_norm_w": g_fnw, "gdn_conv_w": g_conv_mine[None], "gdn_norm_w": g_gnw_mine[None]}
    small_names = SMALL_NAMES + ("gdn_conv_w", "gdn_norm_w")
    small = _adam_call(_pack([small_g[n] for n in small_names], ADAM_ROWS)[None], _pack([W[n] for n in small_names], ADAM_ROWS),
                       _pack([M[n] for n in small_names], ADAM_ROWS), _pack([V[n] for n in small_names], ADAM_ROWS), "adam_small")
    small = [_unpack(b, [W[n].shape for n in small_names]) for b in small]

    dmod_all = sg.reshape(N_DEV, -1)[:, :2 * 3 * D_MODEL].reshape(N_DEV, 2, N_DEV, n_ada)
    dmod_mine = lax.dynamic_index_in_dim(dmod_all, me, axis=2, keepdims=False).transpose(1, 0, 2)
    g_ada_w = _ada_grad_call(c_all, dmod_mine)
    ada = _adam_call(g_ada_w.reshape(1, -1, LANES), W["ada_w"].reshape(-1, LANES), M["ada_w"].reshape(-1, LANES),
                     V["ada_w"].reshape(-1, LANES), "adam_ada")
    ada = [a.reshape(W["ada_w"].shape) for a in ada]

    res = {}
    for i, n in enumerate(BIG_NAMES):
        res[n] = big[i]
    for i, n in enumerate(small_names):
        res[n] = [b[i] for b in small]
    res["ada_w"] = ada
    outs = [loss, dx[None]]
    for j in range(4):
        outs += [res[n][j] for n in WEIGHT_ORDER]
    return tuple(outs)


def kernel(x, c, ada_w, ada_b, norm_w, s5_w_in, s5_lambda_re, s5_lambda_im, s5_log_dt, s5_b_re, s5_b_im, s5_c_re, s5_c_im, s5_d, s5_w_glu, s5_w_out, gdn_w_in, gdn_conv_w, gdn_a_log, gdn_dt_bias, gdn_norm_w, gdn_w_out, final_norm_w, loss_target, m_ada_w, m_ada_b, m_norm_w, m_s5_w_in, m_s5_lambda_re, m_s5_lambda_im, m_s5_log_dt, m_s5_b_re, m_s5_b_im, m_s5_c_re, m_s5_c_im, m_s5_d, m_s5_w_glu, m_s5_w_out, m_gdn_w_in, m_gdn_conv_w, m_gdn_a_log, m_gdn_dt_bias, m_gdn_norm_w, m_gdn_w_out, m_final_norm_w, v_ada_w, v_ada_b, v_norm_w, v_s5_w_in, v_s5_lambda_re, v_s5_lambda_im, v_s5_log_dt, v_s5_b_re, v_s5_b_im, v_s5_c_re, v_s5_c_im, v_s5_d, v_s5_w_glu, v_s5_w_out, v_gdn_w_in, v_gdn_conv_w, v_gdn_a_log, v_gdn_dt_bias, v_gdn_norm_w, v_gdn_w_out, v_final_norm_w):
    W = dict(ada_w=ada_w, ada_b=ada_b, norm_w=norm_w, s5_w_in=s5_w_in, s5_lambda_re=s5_lambda_re, s5_lambda_im=s5_lambda_im,
             s5_log_dt=s5_log_dt, s5_b_re=s5_b_re, s5_b_im=s5_b_im, s5_c_re=s5_c_re, s5_c_im=s5_c_im, s5_d=s5_d,
             s5_w_glu=s5_w_glu, s5_w_out=s5_w_out, gdn_w_in=gdn_w_in, gdn_conv_w=gdn_conv_w, gdn_a_log=gdn_a_log,
             gdn_dt_bias=gdn_dt_bias, gdn_norm_w=gdn_norm_w, gdn_w_out=gdn_w_out, final_norm_w=final_norm_w)
    M = dict(ada_w=m_ada_w, ada_b=m_ada_b, norm_w=m_norm_w, s5_w_in=m_s5_w_in, s5_lambda_re=m_s5_lambda_re,
             s5_lambda_im=m_s5_lambda_im, s5_log_dt=m_s5_log_dt, s5_b_re=m_s5_b_re, s5_b_im=m_s5_b_im, s5_c_re=m_s5_c_re,
             s5_c_im=m_s5_c_im, s5_d=m_s5_d, s5_w_glu=m_s5_w_glu, s5_w_out=m_s5_w_out, gdn_w_in=m_gdn_w_in,
             gdn_conv_w=m_gdn_conv_w, gdn_a_log=m_gdn_a_log, gdn_dt_bias=m_gdn_dt_bias, gdn_norm_w=m_gdn_norm_w,
             gdn_w_out=m_gdn_w_out, final_norm_w=m_final_norm_w)
    V = dict(ada_w=v_ada_w, ada_b=v_ada_b, norm_w=v_norm_w, s5_w_in=v_s5_w_in, s5_lambda_re=v_s5_lambda_re,
             s5_lambda_im=v_s5_lambda_im, s5_log_dt=v_s5_log_dt, s5_b_re=v_s5_b_re, s5_b_im=v_s5_b_im, s5_c_re=v_s5_c_re,
             s5_c_im=v_s5_c_im, s5_d=v_s5_d, s5_w_glu=v_s5_w_glu, s5_w_out=v_s5_w_out, gdn_w_in=v_gdn_w_in,
             gdn_conv_w=v_gdn_conv_w, gdn_a_log=v_gdn_a_log, gdn_dt_bias=v_gdn_dt_bias, gdn_norm_w=v_gdn_norm_w,
             gdn_w_out=v_gdn_w_out, final_norm_w=v_final_norm_w)
    return _step(x, c, W, M, V, loss_target)
```

```python
import functools
import math

import jax
import jax.numpy as jnp
from jax import lax
from jax.experimental import pallas as pl
from jax.experimental.pallas import tpu as pltpu

F32 = jnp.float32
BF16 = jnp.bfloat16
SDS = jax.ShapeDtypeStruct

D_MODEL = 1024
D_INNER = 2048
NORM_EPS = 1e-6
S5_GROUP = 16
S5_GROUPS = 128
S5_STATE = 64
GDN_HEADS = 8
GDN_DK = 128
GDN_DV = 256
GDN_CONV = 4
GDN_CHUNK = 64
GDN_QK = 1024
GDN_CONV_CH = 4096
GDN_PROJ = 6160
ADAM_LR = 0.001
ADAM_B1 = 0.9
ADAM_B2 = 0.999
ADAM_EPS = 1e-08
ADAM_WD = 0.01
ADAM_STEP = 10

N_DEV = 8
LANES = 128
SUBLANES = 8
VMEM_BIG = 56 << 20
VMEM_MID = 40 << 20
S5_GB = 8
S5_TL = 1024
MESH_AXES = ("x", "y", "c")


def _params(sem, vmem=None):
    return pltpu.CompilerParams(dimension_semantics=sem, vmem_limit_bytes=vmem)


def _bdot(a, b, dims=(((1,), (0,)), ((), ()))):
    return lax.dot_general(a.astype(BF16), b.astype(BF16), dims, preferred_element_type=F32)


def _hdot(a, b, dims=(((1,), (0,)), ((), ()))):
    return lax.dot_general(a, b, dims, preferred_element_type=F32, precision=lax.Precision.HIGHEST)


_BNN = (((2,), (1,)), ((0,), (0,)))
_BNT = (((2,), (2,)), ((0,), (0,)))
_BTN = (((1,), (1,)), ((0,), (0,)))


@jax.custom_vjp
def _unit_lower_inverse(a):
    c = a.shape[-1]
    ri = lax.broadcasted_iota(jnp.int32, a.shape, 1)
    ci = lax.broadcasted_iota(jnp.int32, a.shape, 2)
    n = -a
    t = (ri == ci).astype(F32) + n
    for _ in range(int(math.log2(c)) - 1):
        n = _hdot(n, n, _BNN)
        t = t + _hdot(t, n, _BNN)
    return t


def _unit_lower_inverse_fwd(a):
    t = _unit_lower_inverse(a)
    return t, t


def _unit_lower_inverse_bwd(t, g):
    return (-_hdot(_hdot(t, g, _BTN), t, _BNT),)


_unit_lower_inverse.defvjp(_unit_lower_inverse_fwd, _unit_lower_inverse_bwd)


NN = (((1,), (0,)), ((), ()))
NT = (((1,), (1,)), ((), ()))
TN = (((0,), (0,)), ((), ()))


def _tile(n, pref):
    for t in (pref, 512, 256, 128):
        if t <= n and n % t == 0:
            return t
    return n


def _matmul(a, b, mode, name):
    if mode == "nn":
        (m, k), (_, n) = a.shape, b.shape
    elif mode == "nt":
        (m, k), (n, _) = a.shape, b.shape
    else:
        (k, m), (_, n) = a.shape, b.shape
    tm, tn, tk = _tile(m, 512), _tile(n, 512), (k if k <= 2048 else _tile(k, 512))
    if mode == "tn":
        tm, tn = _tile(m, 1024), _tile(n, 1024)
    nk = k // tk
    dims = {"nn": NN, "nt": NT, "tn": TN}[mode]

    def body(a_ref, b_ref, o_ref, acc_ref):
        kk = pl.program_id(2)

        @pl.when(kk == 0)
        def _():
            acc_ref[...] = jnp.zeros_like(acc_ref)
        acc_ref[...] += _bdot(a_ref[...], b_ref[...], dims)

        @pl.when(kk == nk - 1)
        def _():
            o_ref[...] = acc_ref[...]

    a_spec = pl.BlockSpec((tk, tm), lambda i, j, q: (q, i)) if mode == "tn" else pl.BlockSpec((tm, tk), lambda i, j, q: (i, q))
    b_spec = pl.BlockSpec((tn, tk), lambda i, j, q: (j, q)) if mode == "nt" else pl.BlockSpec((tk, tn), lambda i, j, q: (q, j))
    return pl.pallas_call(
        body, name=name, grid=(m // tm, n // tn, nk),
        in_specs=[a_spec, b_spec], out_specs=pl.BlockSpec((tm, tn), lambda i, j, q: (i, j)),
        out_shape=SDS((m, n), F32), scratch_shapes=[pltpu.VMEM((tm, tn), F32)],
        compiler_params=_params(("parallel", "parallel", "arbitrary"), VMEM_MID),
    )(a, b)


def make_mm(name):
    @jax.custom_vjp
    def mm(a, w, grad_slot):
        return _matmul(a, w, "nn", name + "_fwd")

    def fwd(a, w, grad_slot):
        return _matmul(a, w, "nn", name + "_fwd"), (a, w)

    def bwd(res, g):
        a, w = res
        return _matmul(g, w, "nt", name + "_dx"), jnp.zeros_like(w), _matmul(a, g, "tn", name + "_dw")

    mm.defvjp(fwd, bwd)
    return mm


PROJ_ROWS = 256


def _proj_fwd_call(a, ws, name):
    m, k = a.shape
    tm = _tile(m, PROJ_ROWS)
    nw = len(ws)

    def body(*refs):
        ab = refs[0][...].astype(BF16)
        for w_ref, o_ref in zip(refs[1:1 + nw], refs[1 + nw:]):
            o_ref[...] = lax.dot_general(ab, w_ref[...], NN, preferred_element_type=F32)

    return pl.pallas_call(
        body, name=name, grid=(m // tm,),
        in_specs=[pl.BlockSpec((tm, k), lambda i: (i, 0))] + [pl.BlockSpec(w.shape, lambda i: (0, 0)) for w in ws],
        out_specs=[pl.BlockSpec((tm, w.shape[1]), lambda i: (i, 0)) for w in ws],
        out_shape=[SDS((m, w.shape[1]), F32) for w in ws],
        compiler_params=_params(("parallel",), VMEM_BIG),
    )(a, *ws)


def _proj_dx_call(gs, ws, name):
    m = gs[0].shape[0]
    k = ws[0].shape[0]
    tm = _tile(m, PROJ_ROWS)
    nw = len(ws)

    def body(*refs):
        acc = None
        for g_ref, w_ref in zip(refs[:nw], refs[nw:2 * nw]):
            part = _bdot(g_ref[...], w_ref[...], NT)
            acc = part if acc is None else acc + part
        refs[2 * nw][...] = acc

    return pl.pallas_call(
        body, name=name, grid=(m // tm,),
        in_specs=[pl.BlockSpec((tm, g.shape[1]), lambda i: (i, 0)) for g in gs] + [pl.BlockSpec(w.shape, lambda i: (0, 0)) for w in ws],
        out_specs=pl.BlockSpec((tm, k), lambda i: (i, 0)), out_shape=SDS((m, k), F32),
        compiler_params=_params(("parallel",), VMEM_BIG),
    )(*gs, *ws)


def make_proj(name):
    @jax.custom_vjp
    def proj(a, ws, grad_slots):
        return tuple(_proj_fwd_call(a, ws, name + "_fwd"))

    def fwd(a, ws, grad_slots):
        return tuple(_proj_fwd_call(a, ws, name + "_fwd")), (a, ws)

    def bwd(res, gs):
        a, ws = res
        dws = tuple(_matmul(a, g, "tn", "%s_dw%d" % (name, i)) for i, g in enumerate(gs))
        return _proj_dx_call(tuple(gs), ws, name + "_dx"), tuple(jnp.zeros_like(w) for w in ws), dws

    proj.defvjp(fwd, bwd)
    return proj


def make_rowwise(f, name, tm, n_rows, n_params, vmem=VMEM_MID):
    def specs_of(arrs, blocked):
        if blocked:
            return [pl.BlockSpec((tm, a.shape[1]), lambda i: (i, 0)) for a in arrs]
        return [pl.BlockSpec(a.shape, lambda i: (0, 0)) for a in arrs]

    def out_structs(rows, params):
        blk = [SDS((tm, r.shape[1]), r.dtype) for r in rows] + [SDS(p.shape, p.dtype) for p in params]
        return jax.eval_shape(f, *blk)

    def run_fwd(rows, params):
        L = rows[0].shape[0]
        outs = out_structs(rows, params)

        def body(*refs):
            ins = [r[...] for r in refs[:n_rows + n_params]]
            res = f(*ins)
            for o_ref, val in zip(refs[n_rows + n_params:], res):
                o_ref[...] = val

        return pl.pallas_call(
            body, name=name + "_fwd", grid=(L // tm,),
            in_specs=specs_of(rows, True) + specs_of(params, False),
            out_specs=[pl.BlockSpec((tm, o.shape[1]), lambda i: (i, 0)) for o in outs],
            out_shape=[SDS((L, o.shape[1]), o.dtype) for o in outs],
            compiler_params=_params(("parallel",), vmem),
        )(*rows, *params)

    def run_bwd(rows, params, gs):
        L = rows[0].shape[0]
        n_g = len(gs)

        def body(*refs):
            i = pl.program_id(0)
            ins = [r[...] for r in refs[:n_rows + n_params]]
            cts = tuple(r[...] for r in refs[n_rows + n_params:n_rows + n_params + n_g])
            outs = refs[n_rows + n_params + n_g:]
            _, vjp = jax.vjp(f, *ins)
            grads = vjp(cts)
            for o_ref, val in zip(outs[:n_rows], grads[:n_rows]):
                o_ref[...] = val

            if n_params:
                @pl.when(i == 0)
                def _():
                    for o_ref in outs[n_rows:]:
                        o_ref[...] = jnp.zeros_like(o_ref)
                for o_ref, val in zip(outs[n_rows:], grads[n_rows:]):
                    o_ref[...] += val

        res = pl.pallas_call(
            body, name=name + "_bwd", grid=(L // tm,),
            in_specs=specs_of(rows, True) + specs_of(params, False) + specs_of(gs, True),
            out_specs=specs_of(rows, True) + specs_of(params, False),
            out_shape=[SDS(r.shape, r.dtype) for r in rows] + [SDS(p.shape, p.dtype) for p in params],
            compiler_params=_params(("arbitrary",), vmem),
        )(*rows, *params, *gs)
        return tuple(res[:n_rows]), tuple(res[n_rows:])

    @jax.custom_vjp
    def op(rows, params):
        return tuple(run_fwd(rows, params))

    def fwd(rows, params):
        return tuple(run_fwd(rows, params)), (rows, params)

    def bwd(res, gs):
        rows, params = res
        return run_bwd(rows, params, tuple(gs))

    op.defvjp(fwd, bwd)
    return op


def _s5_scan_rows(xr_ref, xi_ref, ar, ai, x0r, x0i, tl, reverse=False):
    n = xr_ref.shape[1]
    T = SUBLANES
    row = lax.broadcasted_iota(jnp.int32, (T, n), 0)
    pr, pi = [ar], [ai]
    for _ in range(T - 1):
        pr, pi = pr + [pr[-1] * ar - pi[-1] * ai], pi + [pr[-1] * ai + pi[-1] * ar]
    levels = []
    for d in (1, 2, 4):
        mask = (row < T - d) if reverse else (row >= d)
        levels.append((T - d if reverse else d, jnp.where(mask, pr[d - 1], 0.0), jnp.where(mask, pi[d - 1], 0.0)))
    cr = jnp.zeros((T, n), F32)
    ci = jnp.zeros((T, n), F32)
    for r in range(T):
        k = (T - r) if reverse else (r + 1)
        cr = jnp.where(row == r, pr[k - 1], cr)
        ci = jnp.where(row == r, pi[k - 1], ci)
    nt = tl // T
    last = 0 if reverse else T - 1

    def step(t, carry):
        sr, si = carry
        base = pl.multiple_of((nt - 1 - t if reverse else t) * T, T)
        br = xr_ref[pl.ds(base, T), :]
        bi = xi_ref[pl.ds(base, T), :]
        for shift, mr, mi in levels:
            qr = pltpu.roll(br, shift, 0)
            qi = pltpu.roll(bi, shift, 0)
            br, bi = br + (mr * qr - mi * qi), bi + (mr * qi + mi * qr)
        xr = br + (cr * sr - ci * si)
        xi = bi + (cr * si + ci * sr)
        xr_ref[pl.ds(base, T), :] = xr
        xi_ref[pl.ds(base, T), :] = xi
        return xr[last:last + 1, :], xi[last:last + 1, :]
    return lax.fori_loop(0, nt, step, (x0r, x0i))


def _s5_fwd_call(u, bre, bim, cre, cim, a, tl):
    L, e = u.shape
    nb = e // LANES
    ns = bre.shape[2]
    nc = L // tl

    def body(u_ref, bre_ref, bim_ref, cre_ref, cim_ref, a_ref, ys_ref, xb_ref, xr_ref, xi_ref, carry_ref):
        c = pl.program_id(1)

        @pl.when(c == 0)
        def _():
            carry_ref[...] = jnp.zeros_like(carry_ref)
        xb_ref[0, 0] = carry_ref[...]
        ub = u_ref[...]
        xr_ref[...] = _bdot(ub, bre_ref[0])
        xi_ref[...] = _bdot(ub, bim_ref[0])
        ar = a_ref[0, 0:1, :]
        ai = a_ref[0, 1:2, :]
        xr, xi = _s5_scan_rows(xr_ref, xi_ref, ar, ai, carry_ref[0:1, :], carry_ref[1:2, :], tl)
        carry_ref[0:1, :] = xr
        carry_ref[1:2, :] = xi
        ys_ref[...] = _bdot(xr_ref[...], cre_ref[0]) - _bdot(xi_ref[...], cim_ref[0])

    return pl.pallas_call(
        body, name="s5_core_fwd", grid=(nb, nc),
        in_specs=[pl.BlockSpec((tl, LANES), lambda j, c: (c, j)),
                  pl.BlockSpec((1, LANES, ns), lambda j, c: (j, 0, 0)), pl.BlockSpec((1, LANES, ns), lambda j, c: (j, 0, 0)),
                  pl.BlockSpec((1, ns, LANES), lambda j, c: (j, 0, 0)), pl.BlockSpec((1, ns, LANES), lambda j, c: (j, 0, 0)),
                  pl.BlockSpec((1, SUBLANES, ns), lambda j, c: (j, 0, 0))],
        out_specs=[pl.BlockSpec((tl, LANES), lambda j, c: (c, j)),
                   pl.BlockSpec((1, 1, SUBLANES, ns), lambda j, c: (j, c, 0, 0))],
        out_shape=[SDS((L, e), F32), SDS((nb, nc, SUBLANES, ns), F32)],
        scratch_shapes=[pltpu.VMEM((tl, ns), F32), pltpu.VMEM((tl, ns), F32), pltpu.VMEM((SUBLANES, ns), F32)],
        compiler_params=_params(("arbitrary", "arbitrary"), VMEM_MID),
    )(u, bre, bim, cre, cim, a)


def _s5_bwd_call(u, dys, bre, bim, cre, cim, a, xb, tl):
    L, e = u.shape
    nb = e // LANES
    ns = bre.shape[2]
    nc = L // tl

    def body(u_ref, dys_ref, bre_ref, bim_ref, cre_ref, cim_ref, a_ref, xb_ref,
             du_ref, dbre_ref, dbim_ref, dcre_ref, dcim_ref, da_ref,
             xr_ref, xi_ref, gr_ref, gi_ref, gcarry_ref):
        c = pl.program_id(1)

        @pl.when(c == 0)
        def _():
            gcarry_ref[...] = jnp.zeros_like(gcarry_ref)
            dbre_ref[...] = jnp.zeros_like(dbre_ref)
            dbim_ref[...] = jnp.zeros_like(dbim_ref)
            dcre_ref[...] = jnp.zeros_like(dcre_ref)
            dcim_ref[...] = jnp.zeros_like(dcim_ref)
            da_ref[...] = jnp.zeros_like(da_ref)

        ub = u_ref[...]
        dy = dys_ref[...]
        ar = a_ref[0, 0:1, :]
        ai = a_ref[0, 1:2, :]
        x0r = xb_ref[0, 0, 0:1, :]
        x0i = xb_ref[0, 0, 1:2, :]
        xr_ref[...] = _bdot(ub, bre_ref[0])
        xi_ref[...] = _bdot(ub, bim_ref[0])
        _s5_scan_rows(xr_ref, xi_ref, ar, ai, x0r, x0i, tl)
        dcre_ref[0] += _bdot(xr_ref[...], dy, TN)
        dcim_ref[0] -= _bdot(xi_ref[...], dy, TN)
        gr_ref[...] = _bdot(dy, cre_ref[0], NT)
        gi_ref[...] = -_bdot(dy, cim_ref[0], NT)

        g0r, g0i = _s5_scan_rows(gr_ref, gi_ref, ar, -ai, gcarry_ref[0:1, :], gcarry_ref[1:2, :], tl, reverse=True)
        gcarry_ref[0:1, :] = g0r
        gcarry_ref[1:2, :] = g0i
        row = lax.broadcasted_iota(jnp.int32, (tl, ns), 0)
        gr = gr_ref[...]
        gi = gi_ref[...]
        xpr = jnp.where(row == 0, x0r, pltpu.roll(xr_ref[...], 1, 0))
        xpi = jnp.where(row == 0, x0i, pltpu.roll(xi_ref[...], 1, 0))
        da_ref[0, 0:1, :] += jnp.sum(gr * xpr + gi * xpi, axis=0, keepdims=True)
        da_ref[0, 1:2, :] += jnp.sum(gi * xpr - gr * xpi, axis=0, keepdims=True)
        du_ref[...] = _bdot(gr, bre_ref[0], NT) + _bdot(gi, bim_ref[0], NT)
        dbre_ref[0] += _bdot(ub, gr, TN)
        dbim_ref[0] += _bdot(ub, gi, TN)

    rev = lambda c: nc - 1 - c
    return pl.pallas_call(
        body, name="s5_core_bwd", grid=(nb, nc),
        in_specs=[pl.BlockSpec((tl, LANES), lambda j, c: (rev(c), j)), pl.BlockSpec((tl, LANES), lambda j, c: (rev(c), j)),
                  pl.BlockSpec((1, LANES, ns), lambda j, c: (j, 0, 0)), pl.BlockSpec((1, LANES, ns), lambda j, c: (j, 0, 0)),
                  pl.BlockSpec((1, ns, LANES), lambda j, c: (j, 0, 0)), pl.BlockSpec((1, ns, LANES), lambda j, c: (j, 0, 0)),
                  pl.BlockSpec((1, SUBLANES, ns), lambda j, c: (j, 0, 0)),
                  pl.BlockSpec((1, 1, SUBLANES, ns), lambda j, c: (j, rev(c), 0, 0))],
        out_specs=[pl.BlockSpec((tl, LANES), lambda j, c: (rev(c), j)),
                   pl.BlockSpec((1, LANES, ns), lambda j, c: (j, 0, 0)), pl.BlockSpec((1, LANES, ns), lambda j, c: (j, 0, 0)),
                   pl.BlockSpec((1, ns, LANES), lambda j, c: (j, 0, 0)), pl.BlockSpec((1, ns, LANES), lambda j, c: (j, 0, 0)),
                   pl.BlockSpec((1, SUBLANES, ns), lambda j, c: (j, 0, 0))],
        out_shape=[SDS((L, e), F32), SDS(bre.shape, F32), SDS(bim.shape, F32), SDS(cre.shape, F32), SDS(cim.shape, F32),
                   SDS(a.shape, F32)],
        scratch_shapes=[pltpu.VMEM((tl, ns), F32) for _ in range(4)] + [pltpu.VMEM((SUBLANES, ns), F32)],
        compiler_params=_params(("arbitrary", "arbitrary"), VMEM_MID),
    )(u, dys, bre, bim, cre, cim, a, xb)


def make_s5_core(tl):
    @jax.custom_vjp
    def s5_core(u, bre, bim, cre, cim, a):
        return _s5_fwd_call(u, bre, bim, cre, cim, a, tl)[0]

    def fwd(u, bre, bim, cre, cim, a):
        ys, xb = _s5_fwd_call(u, bre, bim, cre, cim, a, tl)
        return ys, (u, bre, bim, cre, cim, a, xb)

    def bwd(res, dys):
        u, bre, bim, cre, cim, a, xb = res
        return tuple(_s5_bwd_call(u, dys, bre, bim, cre, cim, a, xb, tl))

    s5_core.defvjp(fwd, bwd)
    return s5_core


def _s5_block_params(lam_re, lam_im, log_dt, b_re, b_im, c_re, c_im):
    dt = jnp.exp(log_dt)[:, None]
    mag = jnp.exp(lam_re * dt)
    ab_re = mag * jnp.cos(lam_im * dt)
    ab_im = mag * jnp.sin(lam_im * dt)
    den = lam_re * lam_re + lam_im * lam_im
    nr = ab_re - 1.0
    ni = ab_im
    q_re = (nr * lam_re + ni * lam_im) / den
    q_im = (ni * lam_re - nr * lam_im) / den
    bb_re = q_re[..., None] * b_re - q_im[..., None] * b_im
    bb_im = q_re[..., None] * b_im + q_im[..., None] * b_re
    nb = S5_GROUPS // S5_GB
    eye = jnp.eye(S5_GB, dtype=F32)

    def bdiag_in(bb):
        t = bb.reshape(nb, S5_GB, S5_STATE, S5_GROUP)
        t = jnp.einsum("jgpm,gh->jgmhp", t, eye)
        return t.reshape(nb, S5_GB * S5_GROUP, S5_GB * S5_STATE)

    def bdiag_out(cc):
        t = cc.reshape(nb, S5_GB, S5_GROUP, S5_STATE)
        t = jnp.einsum("jgmp,gh->jgphm", t, eye)
        return t.reshape(nb, S5_GB * S5_STATE, S5_GB * S5_GROUP)

    a = jnp.stack([ab_re.reshape(nb, S5_GB * S5_STATE), ab_im.reshape(nb, S5_GB * S5_STATE)], axis=1)
    a = jnp.concatenate([a, jnp.zeros((nb, SUBLANES - 2, S5_GB * S5_STATE), F32)], axis=1)
    return bdiag_in(bb_re), bdiag_in(bb_im), bdiag_out(c_re), bdiag_out(c_im), a


def _shift_down(x, s, row):
    if s == 0:
        return x
    return jnp.where(row >= s, pltpu.roll(x, s, 0), 0.0)


def _shift_up(x, s, row, n):
    if s == 0:
        return x
    return jnp.where(row < n - s, pltpu.roll(x, n - s, 0), 0.0)


def _conv_fwd_call(x, w):
    L, ch = x.shape

    def body(x_ref, w_ref, y_ref):
        xv = x_ref[...]
        row = lax.broadcasted_iota(jnp.int32, xv.shape, 0)
        acc = jnp.zeros_like(xv)
        for j in range(GDN_CONV):
            acc += w_ref[j:j + 1, :] * _shift_down(xv, GDN_CONV - 1 - j, row)
        y_ref[...] = acc

    return pl.pallas_call(
        body, name="gdn_conv_fwd", grid=(ch // LANES,),
        in_specs=[pl.BlockSpec((L, LANES), lambda j: (0, j)), pl.BlockSpec((SUBLANES, LANES), lambda j: (0, j))],
        out_specs=pl.BlockSpec((L, LANES), lambda j: (0, j)), out_shape=SDS((L, ch), F32),
        compiler_params=_params(("parallel",), VMEM_MID),
    )(x, w)


def _conv_bwd_call(x, w, dy):
    L, ch = x.shape

    def body(x_ref, w_ref, dy_ref, dx_ref, dw_ref):
        xv = x_ref[...]
        g = dy_ref[...]
        row = lax.broadcasted_iota(jnp.int32, xv.shape, 0)
        acc = jnp.zeros_like(xv)
        dws = []
        for j in range(GDN_CONV):
            s = GDN_CONV - 1 - j
            acc += w_ref[j:j + 1, :] * _shift_up(g, s, row, L)
            dws.append(jnp.sum(g * _shift_down(xv, s, row), axis=0, keepdims=True))
        dx_ref[...] = acc
        dw_ref[...] = jnp.concatenate(dws + [jnp.zeros((SUBLANES - GDN_CONV, LANES), F32)], axis=0)

    return pl.pallas_call(
        body, name="gdn_conv_bwd", grid=(ch // LANES,),
        in_specs=[pl.BlockSpec((L, LANES), lambda j: (0, j)), pl.BlockSpec((SUBLANES, LANES), lambda j: (0, j)),
                  pl.BlockSpec((L, LANES), lambda j: (0, j))],
        out_specs=[pl.BlockSpec((L, LANES), lambda j: (0, j)), pl.BlockSpec((SUBLANES, LANES), lambda j: (0, j))],
        out_shape=[SDS((L, ch), F32), SDS((SUBLANES, ch), F32)],
        compiler_params=_params(("parallel",), VMEM_MID),
    )(x, w, dy)


@jax.custom_vjp
def gdn_conv(x, w):
    return _conv_fwd_call(x, w)


def _gdn_conv_f(x, w):
    return _conv_fwd_call(x, w), (x, w)


def _gdn_conv_b(res, dy):
    x, w = res
    return tuple(_conv_bwd_call(x, w, dy))


gdn_conv.defvjp(_gdn_conv_f, _gdn_conv_b)


BNN = (((2,), (1,)), ((0,), (0,)))
BNT = (((2,), (2,)), ((0,), (0,)))
BTN = (((1,), (1,)), ((0,), (0,)))
GDN_PREP_BATCH = 8


def _gdn_prep_math(q, k, v, beta, g):
    B, C = q.shape[0], q.shape[1]
    ri = lax.broadcasted_iota(jnp.int32, (B, C, C), 1)
    ci = lax.broadcasted_iota(jnp.int32, (B, C, C), 2)
    causal = ri >= ci
    strict = ri > ci
    eye = (ri == ci).astype(F32)
    gb = jnp.broadcast_to(g, (B, C, C))
    g_row = jnp.sum(gb * eye, axis=1, keepdims=True)
    gc_col = jnp.sum(jnp.where(causal, jnp.broadcast_to(g_row, (B, C, C)), 0.0), axis=2, keepdims=True)
    gc_row = jnp.sum(jnp.where(ri <= ci, gb, 0.0), axis=1, keepdims=True)
    decay = jnp.exp(jnp.where(causal, gc_col - gc_row, -jnp.inf))
    kk = _bdot(k, k, BNT)
    a_mat = jnp.where(strict, beta * kk * decay, 0.0)
    t = _unit_lower_inverse(a_mat)
    e_gc = jnp.exp(gc_col)
    w = _hdot(t, beta * e_gc * k, BNN)
    u = _hdot(t, beta * v, BNN)
    qk = _bdot(q, k, BNT) * decay
    q_dec = q * e_gc
    g_last = gc_col[:, C - 1:C, :]
    k_dec = k * jnp.exp(g_last - gc_col)
    return q_dec, w, u, qk, k_dec, gc_col


def _gdn_prep_specs(L):
    C = GDN_CHUNK
    nb = min(GDN_PREP_BATCH, L // C)
    R = nb * C
    ins = [pl.BlockSpec((R, GDN_DK), lambda h, c: (c, h)), pl.BlockSpec((R, GDN_DK), lambda h, c: (c, h)),
           pl.BlockSpec((R, GDN_DV), lambda h, c: (c, h)),
           pl.BlockSpec((1, R, 1), lambda h, c: (h, c, 0)), pl.BlockSpec((1, R, 1), lambda h, c: (h, c, 0))]
    outs = [pl.BlockSpec((1, R, GDN_DK), lambda h, c: (h, c, 0)), pl.BlockSpec((1, R, GDN_DK), lambda h, c: (h, c, 0)),
            pl.BlockSpec((1, R, GDN_DV), lambda h, c: (h, c, 0)), pl.BlockSpec((1, R, C), lambda h, c: (h, c, 0)),
            pl.BlockSpec((1, R, GDN_DK), lambda h, c: (h, c, 0)), pl.BlockSpec((1, R, 1), lambda h, c: (h, c, 0))]
    shapes = [SDS((GDN_HEADS, L, GDN_DK), F32), SDS((GDN_HEADS, L, GDN_DK), F32), SDS((GDN_HEADS, L, GDN_DV), F32),
              SDS((GDN_HEADS, L, C), F32), SDS((GDN_HEADS, L, GDN_DK), F32), SDS((GDN_HEADS, L, 1), F32)]
    return ins, outs, shapes, nb


def _chunks(x, nb):
    return x.reshape(nb, x.shape[0] // nb, x.shape[1])


def _gdn_prep_fwd_call(q, k, v, beta, g):
    L = q.shape[0]
    ins, outs, shapes, nb = _gdn_prep_specs(L)

    def body(q_ref, k_ref, v_ref, b_ref, g_ref, *o_refs):
        res = _gdn_prep_math(_chunks(q_ref[...], nb), _chunks(k_ref[...], nb), _chunks(v_ref[...], nb),
                             _chunks(b_ref[0], nb), _chunks(g_ref[0], nb))
        for o_ref, val in zip(o_refs, res):
            o_ref[0] = val.reshape(val.shape[0] * val.shape[1], val.shape[2])

    return pl.pallas_call(
        body, name="gdn_prep_fwd", grid=(GDN_HEADS, L // (nb * GDN_CHUNK)), in_specs=ins, out_specs=outs, out_shape=shapes,
        compiler_params=_params(("parallel", "parallel"), VMEM_MID),
    )(q, k, v, beta, g)


def _gdn_prep_bwd_call(q, k, v, beta, g, cts):
    L = q.shape[0]
    ins, outs, _, nb = _gdn_prep_specs(L)

    def body(q_ref, k_ref, v_ref, b_ref, g_ref, c0, c1, c2, c3, c4, c5, dq_ref, dk_ref, dv_ref, db_ref, dg_ref):
        _, vjp = jax.vjp(_gdn_prep_math, _chunks(q_ref[...], nb), _chunks(k_ref[...], nb), _chunks(v_ref[...], nb),
                         _chunks(b_ref[0], nb), _chunks(g_ref[0], nb))
        dq, dk, dv, db, dg = vjp(tuple(_chunks(c[0], nb) for c in (c0, c1, c2, c3, c4, c5)))
        flat = lambda t: t.reshape(t.shape[0] * t.shape[1], t.shape[2])
        dq_ref[...] = flat(dq)
        dk_ref[...] = flat(dk)
        dv_ref[...] = flat(dv)
        db_ref[0] = flat(db)
        dg_ref[0] = flat(dg)

    return pl.pallas_call(
        body, name="gdn_prep_bwd", grid=(GDN_HEADS, L // (nb * GDN_CHUNK)), in_specs=ins + outs, out_specs=ins,
        out_shape=[SDS(q.shape, F32), SDS(k.shape, F32), SDS(v.shape, F32), SDS(beta.shape, F32), SDS(g.shape, F32)],
        compiler_params=_params(("parallel", "parallel"), VMEM_MID),
    )(q, k, v, beta, g, *cts)


@jax.custom_vjp
def gdn_prep(q, k, v, beta, g):
    return tuple(_gdn_prep_fwd_call(q, k, v, beta, g))


def _gdn_prep_f(q, k, v, beta, g):
    return tuple(_gdn_prep_fwd_call(q, k, v, beta, g)), (q, k, v, beta, g)


def _gdn_prep_b(res, cts):
    return tuple(_gdn_prep_bwd_call(*res, tuple(cts)))


gdn_prep.defvjp(_gdn_prep_f, _gdn_prep_b)


def _gdn_step_math(q_dec, w, u, qk, k_dec, gc, state):
    H, C = q_dec.shape[0], q_dec.shape[1]
    v_new = u - _bdot(w, state, BNN)
    o = _bdot(q_dec, state, BNN) + _bdot(qk, v_new, BNN)
    gl = gc[:, C - 1:C, :]
    new_state = jnp.exp(gl) * state + _bdot(k_dec, v_new, BTN)
    return jnp.concatenate([o[h] for h in range(H)], axis=1), new_state


def _gdn_scan_specs(L, rev):
    C, H = GDN_CHUNK, GDN_HEADS
    nc = L // C
    cc = (lambda c: nc - 1 - c) if rev else (lambda c: c)
    ins = [pl.BlockSpec((H, C, GDN_DK), lambda c: (0, cc(c), 0)), pl.BlockSpec((H, C, GDN_DK), lambda c: (0, cc(c), 0)),
           pl.BlockSpec((H, C, GDN_DV), lambda c: (0, cc(c), 0)), pl.BlockSpec((H, C, C), lambda c: (0, cc(c), 0)),
           pl.BlockSpec((H, C, GDN_DK), lambda c: (0, cc(c), 0)), pl.BlockSpec((H, C, 1), lambda c: (0, cc(c), 0))]
    o_spec = pl.BlockSpec((C, H * GDN_DV), lambda c: (cc(c), 0))
    s_spec = pl.BlockSpec((1, H, GDN_DK, GDN_DV), lambda c: (cc(c), 0, 0, 0))
    return ins, o_spec, s_spec, nc


def _gdn_scan_fwd_call(q_dec, w, u, qk, k_dec, gc):
    L = q_dec.shape[1]
    ins, o_spec, s_spec, nc = _gdn_scan_specs(L, False)

    def body(qd_ref, w_ref, u_ref, qk_ref, kd_ref, gc_ref, o_ref, sin_ref, s_ref):
        c = pl.program_id(0)

        @pl.when(c == 0)
        def _():
            s_ref[...] = jnp.zeros_like(s_ref)
        st = s_ref[...]
        sin_ref[0] = st
        o, ns = _gdn_step_math(qd_ref[...], w_ref[...], u_ref[...], qk_ref[...], kd_ref[...], gc_ref[...], st)
        o_ref[...] = o
        s_ref[...] = ns

    return pl.pallas_call(
        body, name="gdn_scan_fwd", grid=(nc,), in_specs=ins, out_specs=[o_spec, s_spec],
        out_shape=[SDS((L, GDN_HEADS * GDN_DV), F32), SDS((nc, GDN_HEADS, GDN_DK, GDN_DV), F32)],
        scratch_shapes=[pltpu.VMEM((GDN_HEADS, GDN_DK, GDN_DV), F32)],
        compiler_params=_params(("arbitrary",), VMEM_MID),
    )(q_dec, w, u, qk, k_dec, gc)


def _gdn_scan_bwd_call(q_dec, w, u, qk, k_dec, gc, s_in, do):
    L = q_dec.shape[1]
    ins, o_spec, s_spec, nc = _gdn_scan_specs(L, True)

    def body(qd_ref, w_ref, u_ref, qk_ref, kd_ref, gc_ref, sin_ref, do_ref,
             dqd_ref, dw_ref, du_ref, dqk_ref, dkd_ref, dgc_ref, ds_ref):
        c = pl.program_id(0)

        @pl.when(c == 0)
        def _():
            ds_ref[...] = jnp.zeros_like(ds_ref)
        _, vjp = jax.vjp(_gdn_step_math, qd_ref[...], w_ref[...], u_ref[...], qk_ref[...], kd_ref[...], gc_ref[...], sin_ref[0])
        dqd, dw, du, dqk, dkd, dgc, dst = vjp((do_ref[...], ds_ref[...]))
        dqd_ref[...] = dqd
        dw_ref[...] = dw
        du_ref[...] = du
        dqk_ref[...] = dqk
        dkd_ref[...] = dkd
        dgc_ref[...] = dgc
        ds_ref[...] = dst

    return pl.pallas_call(
        body, name="gdn_scan_bwd", grid=(nc,), in_specs=ins + [s_spec, o_spec], out_specs=ins,
        out_shape=[SDS(t.shape, F32) for t in (q_dec, w, u, qk, k_dec, gc)],
        scratch_shapes=[pltpu.VMEM((GDN_HEADS, GDN_DK, GDN_DV), F32)],
        compiler_params=_params(("arbitrary",), VMEM_MID),
    )(q_dec, w, u, qk, k_dec, gc, s_in, do)


@jax.custom_vjp
def gdn_scan(q_dec, w, u, qk, k_dec, gc):
    return _gdn_scan_fwd_call(q_dec, w, u, qk, k_dec, gc)[0]


def _gdn_scan_f(*args):
    o, s_in = _gdn_scan_fwd_call(*args)
    return o, (*args, s_in)


def _gdn_scan_b(res, do):
    return tuple(_gdn_scan_bwd_call(*res, do))


gdn_scan.defvjp(_gdn_scan_f, _gdn_scan_b)


def _silu(x):
    return x * jax.nn.sigmoid(x)


def _gelu_tanh(x):
    return 0.5 * x * (1.0 + jnp.tanh(math.sqrt(2.0 / math.pi) * (x + 0.044715 * (x * x * x))))


def _f_lnmod(x, nw, sc, sh, bsc, bsh):
    xn = x * lax.rsqrt(jnp.mean(x * x, axis=-1, keepdims=True) + NORM_EPS) * nw
    return (xn * (1.0 + (sc + bsc)) + (sh + bsh),)


def _f_s5_act(ys, u, d):
    return (_gelu_tanh(ys + d * u),)


def _f_s5_gate(y2, t, z):
    return (y2 * jax.nn.sigmoid(t) * _silu(z),)


def _f_res(x, y, gate, bgate):
    return (x + (gate + bgate) * y,)


def _heads(x, width, fn):
    return jnp.concatenate([fn(x[:, i * width:(i + 1) * width]) for i in range(x.shape[1] // width)], axis=1)


def _l2n(x):
    return x * lax.rsqrt(jnp.sum(x * x, axis=-1, keepdims=True) + NORM_EPS)


def _f_qnorm(x):
    return (_heads(_silu(x), GDN_DK, _l2n) * (GDN_DK ** -0.5),)


def _f_knorm(x):
    return (_heads(_silu(x), GDN_DK, _l2n),)


def _f_vact(x):
    return (_silu(x),)


def _f_betag(ba, alog, dtb):
    col = lax.broadcasted_iota(jnp.int32, ba.shape, 1)
    t = ba + dtb
    softplus = jnp.maximum(t, 0.0) + jnp.log1p(jnp.exp(-jnp.abs(t)))
    g = -jnp.exp(alog) * softplus
    return (jnp.where(col < GDN_HEADS, jax.nn.sigmoid(ba), jnp.where(col < 2 * GDN_HEADS, g, 0.0)),)


def _f_gdn_post(o, z, nw):
    on = _heads(o, GDN_DV, lambda t: t * lax.rsqrt(jnp.mean(t * t, axis=-1, keepdims=True) + NORM_EPS))
    return (on * nw * _silu(z),)


def _f_loss(x, tgt, fw):
    y = x * lax.rsqrt(jnp.mean(x * x, axis=-1, keepdims=True) + NORM_EPS) * fw
    err = y - tgt
    return (0.5 * jnp.mean(err * err, axis=-1, keepdims=True),)


def _ada_mod_call(c_all, ada_w):
    n = ada_w.shape[2]

    def body(c_ref, w_ref, o_ref):
        ca = _silu(c_ref[...])
        for l in range(ada_w.shape[0]):
            o_ref[l] = _bdot(ca, w_ref[l])

    return pl.pallas_call(body, name="ada_mod", out_shape=SDS((ada_w.shape[0], N_DEV, n), F32),
                          compiler_params=_params(None, VMEM_MID))(c_all, ada_w)


def _ada_grad_call(c_all, dmod):
    nl, _, n = dmod.shape

    def body(c_ref, d_ref, o_ref):
        ca = _silu(c_ref[...])
        for l in range(nl):
            o_ref[l] = _hdot(ca, d_ref[l], TN)

    return pl.pallas_call(body, name="ada_grad", out_shape=SDS((nl, c_all.shape[1], n), F32),
                          compiler_params=_params(None, VMEM_MID))(c_all, dmod)


ADAM_ROWS = 512


def _adamw(g, w, m, v):
    m2 = ADAM_B1 * m + (1.0 - ADAM_B1) * g
    v2 = ADAM_B2 * v + (1.0 - ADAM_B2) * (g * g)
    m_hat = m2 / (1.0 - ADAM_B1 ** ADAM_STEP)
    v_hat = v2 / (1.0 - ADAM_B2 ** ADAM_STEP)
    return g, -ADAM_LR * (m_hat / (jnp.sqrt(v_hat) + ADAM_EPS) + ADAM_WD * w), m2, v2


def _adam_call(gs, w, m, v, name, rows=None):
    n, r, cols = gs.shape
    rows = rows or ADAM_ROWS

    def body(g_ref, w_ref, m_ref, v_ref, go_ref, d_ref, mo_ref, vo_ref):
        g = g_ref[0]
        for s in range(1, n):
            g = g + g_ref[s]
        for o_ref, val in zip((go_ref, d_ref, mo_ref, vo_ref), _adamw(g, w_ref[...], m_ref[...], v_ref[...])):
            o_ref[...] = val

    blk = pl.BlockSpec((rows, cols), lambda i: (i, 0))
    return pl.pallas_call(
        body, name=name, grid=(r // rows,),
        in_specs=[pl.BlockSpec((n, rows, cols), lambda i: (0, i, 0)), blk, blk, blk],
        out_specs=[blk, blk, blk, blk], out_shape=[SDS((r, cols), F32)] * 4,
        compiler_params=_params(("parallel",), VMEM_MID),
    )(gs, w, m, v)


def _sum_call(gs, name):
    n, r, _ = gs.shape

    def body(g_ref, o_ref):
        g = g_ref[0]
        for s in range(1, n):
            g = g + g_ref[s]
        o_ref[...] = g

    return pl.pallas_call(
        body, name=name, grid=(r // ADAM_ROWS,),
        in_specs=[pl.BlockSpec((n, ADAM_ROWS, LANES), lambda i: (0, i, 0))],
        out_specs=pl.BlockSpec((ADAM_ROWS, LANES), lambda i: (i, 0)), out_shape=SDS((r, LANES), F32),
        compiler_params=_params(("parallel",), VMEM_MID),
    )(gs)


def _allgather_call(x_shard, name, in_hbm):
    m_per, n = x_shard.shape

    def body(x_ref, out_ref, send_sems, recv_sems, local_sem):
        x, y, c = lax.axis_index("x"), lax.axis_index("y"), lax.axis_index("c")
        me, sibling = (x, y, c), (x, y, 1 - c)
        chips = [(1 - x, y), (x, 1 - y), (1 - x, 1 - y)]

        def rows(px, py, pc):
            return out_ref.at[pl.ds((4 * px + 2 * py + pc) * m_per, m_per), :]

        def copy(k, block, to, src=None):
            return pltpu.make_async_remote_copy(
                src_ref=rows(*block) if src is None else src, dst_ref=rows(*block),
                send_sem=send_sems.at[k], recv_sem=recv_sems.at[k], device_id=to, device_id_type=pl.DeviceIdType.MESH)

        mine = pltpu.make_async_copy(x_ref, rows(*me), local_sem)
        mine.start()
        first = [copy(0, me, sibling, src=x_ref)]
        first += [copy(1 + j, me, (*chip, c), src=x_ref) for j, chip in enumerate(chips)]
        for cp in first:
            cp.start()
        passed = [copy(4 + j, (*chip, c), sibling) for j, chip in enumerate(chips)]
        for j, chip in enumerate(chips):
            copy(1 + j, (*chip, c), me).wait_recv()
            passed[j].start()
        copy(0, sibling, me).wait_recv()
        for j, chip in enumerate(chips):
            copy(4 + j, (*chip, 1 - c), me).wait_recv()
        for cp in first + passed:
            cp.wait_send()
        mine.wait()

    space = pl.ANY if in_hbm else pltpu.VMEM
    return pl.pallas_call(
        body, name=name, out_shape=SDS((N_DEV * m_per, n), x_shard.dtype),
        in_specs=[pl.BlockSpec(memory_space=space)], out_specs=pl.BlockSpec(memory_space=space),
        scratch_shapes=[pltpu.SemaphoreType.DMA((7,)), pltpu.SemaphoreType.DMA((7,)), pltpu.SemaphoreType.DMA],
        compiler_params=_params(None, None if in_hbm else VMEM_BIG),
    )(x_shard)


def _gather_weights_call(shards, name):
    nw = len(shards)

    def body(*refs):
        x_refs, out_refs = refs[:nw], refs[nw:2 * nw]
        send_sems, recv_sems, local_sems = refs[2 * nw:]
        x, y, c = lax.axis_index("x"), lax.axis_index("y"), lax.axis_index("c")
        me, sibling = (x, y, c), (x, y, 1 - c)
        chips = [(1 - x, y), (x, 1 - y), (1 - x, 1 - y)]

        def slot(w, px, py, pc):
            return out_refs[w].at[4 * px + 2 * py + pc]

        def copy(w, k, block, to, src=None):
            dst = slot(w, *block)
            return pltpu.make_async_remote_copy(
                src_ref=dst if src is None else src, dst_ref=dst, send_sem=send_sems.at[7 * w + k],
                recv_sem=recv_sems.at[7 * w + k], device_id=to, device_id_type=pl.DeviceIdType.MESH)

        mines = [pltpu.make_async_copy(x_refs[w], slot(w, *me), local_sems.at[w]) for w in range(nw)]
        for cp in mines:
            cp.start()
        first = [copy(w, 0, me, sibling, src=x_refs[w]) for w in range(nw)]
        first += [copy(w, 1 + j, me, (*chip, c), src=x_refs[w]) for w in range(nw) for j, chip in enumerate(chips)]
        for cp in first:
            cp.start()
        passed = []
        for w in range(nw):
            for j, chip in enumerate(chips):
                copy(w, 1 + j, (*chip, c), me).wait_recv()
                fwd = copy(w, 4 + j, (*chip, c), sibling)
                fwd.start()
                passed.append(fwd)
        for w in range(nw):
            copy(w, 0, sibling, me).wait_recv()
            for j, chip in enumerate(chips):
                copy(w, 4 + j, (*chip, 1 - c), me).wait_recv()
        for cp in first + passed:
            cp.wait_send()
        for cp in mines:
            cp.wait()

    hbm = pl.BlockSpec(memory_space=pl.ANY)
    return pl.pallas_call(
        body, name=name, out_shape=[SDS((N_DEV,) + s.shape, s.dtype) for s in shards],
        in_specs=[hbm] * nw, out_specs=[hbm] * nw,
        scratch_shapes=[pltpu.SemaphoreType.DMA((7 * nw,)), pltpu.SemaphoreType.DMA((7 * nw,)), pltpu.SemaphoreType.DMA((nw,))],
    )(*shards)


def _pair_exchange_call(grads, name):
    nw = len(grads)

    def body(*refs):
        g_refs, got_refs = refs[:nw], refs[nw:2 * nw]
        send_sems, recv_sems = refs[2 * nw:]
        x, y, c = lax.axis_index("x"), lax.axis_index("y"), lax.axis_index("c")
        copies = []
        for w in range(nw):
            for j in range(4):
                give = pltpu.make_async_remote_copy(
                    src_ref=g_refs[w].at[2 * j + 1 - c], dst_ref=got_refs[w].at[j], send_sem=send_sems.at[4 * w + j],
                    recv_sem=recv_sems.at[4 * w + j], device_id=(x, y, 1 - c), device_id_type=pl.DeviceIdType.MESH)
                give.start()
                copies.append(give)
        for cp in copies:
            cp.wait()

    hbm = pl.BlockSpec(memory_space=pl.ANY)
    return pl.pallas_call(
        body, name=name, out_shape=[SDS((4,) + g.shape[1:], g.dtype) for g in grads], in_specs=[hbm] * nw, out_specs=[hbm] * nw,
        scratch_shapes=[pltpu.SemaphoreType.DMA((4 * nw,)), pltpu.SemaphoreType.DMA((4 * nw,))],
    )(*grads)


def _chip_exchange_call(parts, name):
    nw = len(parts)

    def body(*refs):
        p_refs, out_refs = refs[:nw], refs[nw:2 * nw]
        send_sems, recv_sems = refs[2 * nw:]
        x, y, c = lax.axis_index("x"), lax.axis_index("y"), lax.axis_index("c")
        chips = [(1 - x, y), (x, 1 - y), (1 - x, 1 - y)]
        copies = []
        for w in range(nw):
            for j, (px, py) in enumerate(chips):
                give = pltpu.make_async_remote_copy(
                    src_ref=p_refs[w].at[2 * px + py], dst_ref=out_refs[w].at[j], send_sem=send_sems.at[3 * w + j],
                    recv_sem=recv_sems.at[3 * w + j], device_id=(px, py, c), device_id_type=pl.DeviceIdType.MESH)
                give.start()
                copies.append(give)
        for cp in copies:
            cp.wait()

    hbm = pl.BlockSpec(memory_space=pl.ANY)
    return pl.pallas_call(
        body, name=name, out_shape=[SDS((3,) + p.shape[1:], p.dtype) for p in parts], in_specs=[hbm] * nw, out_specs=[hbm] * nw,
        scratch_shapes=[pltpu.SemaphoreType.DMA((3 * nw,)), pltpu.SemaphoreType.DMA((3 * nw,))],
    )(*parts)


def _pair_sum_call(g, got, core, name):
    _, k, n = got.shape
    tr = _tile(k, 256)

    def body(c_ref, g_ref, got_ref, o_ref):
        o_ref[...] = (g_ref[...] + got_ref[...]).astype(o_ref.dtype)

    spec = pltpu.PrefetchScalarGridSpec(
        num_scalar_prefetch=1, grid=(4, k // tr),
        in_specs=[pl.BlockSpec((1, tr, n), lambda j, i, c: (2 * j + c[0], i, 0)), pl.BlockSpec((1, tr, n), lambda j, i, c: (j, i, 0))],
        out_specs=pl.BlockSpec((1, tr, n), lambda j, i, c: (j, i, 0)))
    return pl.pallas_call(body, name=name, grid_spec=spec, out_shape=SDS(got.shape, BF16),
                          compiler_params=_params(("parallel", "parallel"), VMEM_MID))(core, g, got)


def _adam_own_call(pair, chip, recv, w, m, v, name, rows):
    _, r, cols = recv.shape

    def body(chip_ref, p_ref, g_ref, w_ref, m_ref, v_ref, go_ref, d_ref, mo_ref, vo_ref):
        g = ((p_ref[0].astype(F32) + g_ref[0].astype(F32)) + g_ref[1].astype(F32)) + g_ref[2].astype(F32)
        for o_ref, val in zip((go_ref, d_ref, mo_ref, vo_ref), _adamw(g, w_ref[...], m_ref[...], v_ref[...])):
            o_ref[...] = val

    blk = pl.BlockSpec((rows, cols), lambda i, s: (i, 0))
    spec = pltpu.PrefetchScalarGridSpec(
        num_scalar_prefetch=1, grid=(r // rows,),
        in_specs=[pl.BlockSpec((1, rows, cols), lambda i, s: (s[0], i, 0)), pl.BlockSpec((3, rows, cols), lambda i, s: (0, i, 0)),
                  blk, blk, blk],
        out_specs=[blk, blk, blk, blk])
    return pl.pallas_call(body, name=name, grid_spec=spec, out_shape=[SDS((r, cols), F32)] * 4,
                          compiler_params=_params(("parallel",), VMEM_MID))(chip, pair, recv, w, m, v)


def _join_cols_call(w8, name):
    _, k, n = w8.shape
    tk = _tile(k, 256)

    def body(w_ref, o_ref):
        for s in range(N_DEV):
            o_ref[:, n * s:n * (s + 1)] = w_ref[s]

    return pl.pallas_call(body, name=name, grid=(k // tk,), in_specs=[pl.BlockSpec((N_DEV, tk, n), lambda i: (0, i, 0))],
                          out_specs=pl.BlockSpec((tk, N_DEV * n), lambda i: (i, 0)), out_shape=SDS((k, N_DEV * n), w8.dtype),
                          compiler_params=_params(("parallel",), VMEM_MID))(w8)


def _split_cols_call(g, name):
    k, n8 = g.shape
    n = n8 // N_DEV
    tk = _tile(k, 256)

    def body(g_ref, o_ref):
        for s in range(N_DEV):
            o_ref[s] = g_ref[:, n * s:n * (s + 1)]

    return pl.pallas_call(body, name=name, grid=(k // tk,), in_specs=[pl.BlockSpec((tk, n8), lambda i: (i, 0))],
                          out_specs=pl.BlockSpec((N_DEV, tk, n), lambda i: (0, i, 0)), out_shape=SDS((N_DEV, k, n), g.dtype),
                          compiler_params=_params(("parallel",), VMEM_MID))(g)


def _pack(parts, rows_multiple):
    flat = jnp.concatenate([p.reshape(-1) for p in parts])
    unit = rows_multiple * LANES
    padded = -(-flat.shape[0] // unit) * unit
    flat = jnp.concatenate([flat, jnp.zeros((padded - flat.shape[0],), F32)])
    return flat.reshape(-1, LANES)


def _unpack(buf, shapes):
    flat = buf.reshape(-1)
    out, off = [], 0
    for s in shapes:
        n = math.prod(s)
        out.append(flat[off:off + n].reshape(s))
        off += n
    return out


def _local_loss(diff, const, L):
    (x, mod_raw, norm_w, lam_re, lam_im, log_dt, b_re, b_im, c_re, c_im, s5_d, conv_w, a_log, dt_bias, gdn_nw, final_nw,
     *slots) = diff
    tgt, ada_b, weights = const
    lin = {n: (lambda a, n=n, i=i: make_mm(n)(a, weights[i], slots[i])) for i, n in enumerate(MM_NAMES)}
    tm = 256 if L % 256 == 0 else L
    mods = mod_raw.reshape(2, 3, 1, D_MODEL)
    biases = ada_b.reshape(2, 3, 1, D_MODEL)

    op_ln0 = make_rowwise(_f_lnmod, "ln0", tm, 1, 5)
    (h,) = op_ln0((x,), (norm_w[0:1], mods[0, 1], mods[0, 0], biases[0, 1], biases[0, 0]))
    u, z = make_proj("s5_in")(h, tuple(weights[0:2]), tuple(slots[0:2]))
    blocks = _s5_block_params(lam_re, lam_im, log_dt, b_re, b_im, c_re, c_im)
    ys = make_s5_core(min(S5_TL, L))(u, *blocks)
    (y2,) = make_rowwise(_f_s5_act, "s5_act", tm, 2, 1)((ys, u), (s5_d.reshape(1, D_INNER),))
    t = lin["s5_glu"](y2)
    (y4,) = make_rowwise(_f_s5_gate, "s5_gate", tm, 3, 0)((y2, t, z), ())
    o = lin["s5_out"](y4)
    (x1,) = make_rowwise(_f_res, "res0", tm, 2, 2)((x, o), (mods[0, 2], biases[0, 2]))

    op_ln1 = make_rowwise(_f_lnmod, "ln1", tm, 1, 5)
    (h,) = op_ln1((x1,), (norm_w[1:2], mods[1, 1], mods[1, 0], biases[1, 1], biases[1, 0]))
    q0, k0, v0, gz, ba = make_proj("gdn_in")(h, tuple(weights[4:9]), tuple(slots[4:9]))
    cw = jnp.concatenate([conv_w, jnp.zeros((SUBLANES - GDN_CONV, GDN_CONV_CH), F32)], axis=0)
    (q,) = make_rowwise(_f_qnorm, "gdn_qn", tm, 1, 0)((gdn_conv(q0, cw[:, :GDN_QK]),), ())
    (k,) = make_rowwise(_f_knorm, "gdn_kn", tm, 1, 0)((gdn_conv(k0, cw[:, GDN_QK:2 * GDN_QK]),), ())
    (v,) = make_rowwise(_f_vact, "gdn_va", tm, 1, 0)((gdn_conv(v0, cw[:, 2 * GDN_QK:]),), ())
    pad = jnp.zeros((LANES - 2 * GDN_HEADS,), F32)
    alog_row = jnp.concatenate([jnp.zeros((GDN_HEADS,), F32), a_log, pad]).reshape(1, LANES)
    dtb_row = jnp.concatenate([jnp.zeros((GDN_HEADS,), F32), dt_bias, pad]).reshape(1, LANES)
    (bg,) = make_rowwise(_f_betag, "gdn_bg", tm, 1, 2)((ba,), (alog_row, dtb_row))
    beta = bg[:, :GDN_HEADS].T.reshape(GDN_HEADS, L, 1)
    g = bg[:, GDN_HEADS:2 * GDN_HEADS].T.reshape(GDN_HEADS, L, 1)
    og = gdn_scan(*gdn_prep(q, k, v, beta, g))
    nw_row = jnp.tile(gdn_nw, GDN_HEADS).reshape(1, D_INNER)
    (on,) = make_rowwise(_f_gdn_post, "gdn_post", tm, 2, 1)((og, gz), (nw_row,))
    y = lin["gdn_out"](on)
    (x2,) = make_rowwise(_f_res, "res1", tm, 2, 2)((x1, y), (mods[1, 2], biases[1, 2]))

    (lt,) = make_rowwise(_f_loss, "loss", tm, 2, 1)((x2, tgt), (final_nw.reshape(1, D_MODEL),))
    return jnp.sum(lt)


MM_NAMES = ("s5_in_u", "s5_in_z", "s5_glu", "s5_out", "gdn_in_q", "gdn_in_k", "gdn_in_v", "gdn_in_z", "gdn_in_ba", "gdn_out")
SMALL_NAMES = ("ada_b", "norm_w", "s5_lambda_re", "s5_lambda_im", "s5_log_dt", "s5_b_re", "s5_b_im", "s5_c_re", "s5_c_im",
               "s5_d", "gdn_a_log", "gdn_dt_bias", "final_norm_w")
BIG_NAMES = ("s5_w_in", "s5_w_glu", "s5_w_out", "gdn_w_in", "gdn_w_out")
WEIGHT_ORDER = ("ada_w", "ada_b", "norm_w", "s5_w_in", "s5_lambda_re", "s5_lambda_im", "s5_log_dt", "s5_b_re", "s5_b_im",
                "s5_c_re", "s5_c_im", "s5_d", "s5_w_glu", "s5_w_out", "gdn_w_in", "gdn_conv_w", "gdn_a_log", "gdn_dt_bias",
                "gdn_norm_w", "gdn_w_out", "final_norm_w")


def _step(x, c, W, M, V, tgt):
    L = x.shape[1]
    ix, iy, ic = lax.axis_index("x"), lax.axis_index("y"), lax.axis_index("c")
    me = 4 * ix + 2 * iy + ic
    n_ada = W["ada_w"].shape[2]
    n_conv = W["gdn_conv_w"].shape[2]
    n_gnw = W["gdn_norm_w"].shape[1]

    g1 = _allgather_call(_pack([c, W["gdn_conv_w"], W["gdn_norm_w"]], SUBLANES), "gather_small_in", False)
    g1 = g1.reshape(N_DEV, -1)
    c_all = g1[:, :D_MODEL]
    conv_w = g1[:, D_MODEL:D_MODEL + GDN_CONV * n_conv].reshape(N_DEV, GDN_CONV, n_conv).transpose(1, 0, 2).reshape(GDN_CONV, -1)
    gdn_nw = g1[:, D_MODEL + GDN_CONV * n_conv:D_MODEL + GDN_CONV * n_conv + n_gnw].reshape(-1)
    mod_part = _ada_mod_call(c_all, W["ada_w"])
    g2 = _allgather_call(_pack([mod_part], SUBLANES), "gather_mod", False).reshape(N_DEV, -1)
    mod_all = g2[:, :2 * N_DEV * n_ada].reshape(N_DEV, 2, N_DEV, n_ada)
    mod_raw = lax.dynamic_index_in_dim(mod_all, me, axis=2, keepdims=False)
    mod_raw = mod_raw.transpose(1, 0, 2).reshape(2, 3 * D_MODEL)

    gathered = _gather_weights_call([W[n][0].astype(BF16) for n in BIG_NAMES], "gather_weights")
    full = dict(zip(BIG_NAMES, gathered))
    w_in5 = _join_cols_call(full["s5_w_in"], "join_s5_w_in")
    w_ing = _join_cols_call(full["gdn_w_in"], "join_gdn_w_in")
    w_ba = jnp.concatenate([w_ing[:, GDN_CONV_CH + D_INNER:], jnp.zeros((D_MODEL, LANES - 2 * GDN_HEADS), BF16)], axis=1)
    weights = (w_in5[:, :D_INNER], w_in5[:, D_INNER:], full["s5_w_glu"].reshape(D_INNER, D_INNER),
               full["s5_w_out"].reshape(D_INNER, D_MODEL),
               w_ing[:, :GDN_QK], w_ing[:, GDN_QK:2 * GDN_QK], w_ing[:, 2 * GDN_QK:GDN_CONV_CH],
               w_ing[:, GDN_CONV_CH:GDN_CONV_CH + D_INNER], w_ba, full["gdn_w_out"].reshape(D_INNER, D_MODEL))
    slots = tuple(jnp.zeros(w.shape, F32) for w in weights)
    diff = (x[0], mod_raw, W["norm_w"], W["s5_lambda_re"][0], W["s5_lambda_im"][0], W["s5_log_dt"][0], W["s5_b_re"][0],
            W["s5_b_im"][0], W["s5_c_re"][0], W["s5_c_im"][0], W["s5_d"][0], conv_w, W["gdn_a_log"][0], W["gdn_dt_bias"][0],
            gdn_nw, W["final_norm_w"], *slots)

    loss_local, grads = jax.value_and_grad(_local_loss)(diff, (tgt[0], W["ada_b"], weights), L)
    (dx, dmod, d_norm_w, d_lre, d_lim, d_logdt, d_bre, d_bim, d_cre, d_cim, d_s5d, d_conv, d_alog, d_dtb, d_gnw, d_fnw,
     d_wu, d_wz, d_wglu, d_wo5, d_wq, d_wk, d_wv, d_wgz, d_wba, d_wog) = grads
    loss = lax.psum(loss_local, MESH_AXES)

    d_in5 = _split_cols_call(jnp.concatenate([d_wu, d_wz], axis=1), "split_s5_w_in")
    d_ing = _split_cols_call(jnp.concatenate([d_wq, d_wk, d_wv, d_wgz, d_wba[:, :2 * GDN_HEADS]], axis=1), "split_gdn_w_in")
    rows = lambda d: d.reshape(N_DEV, d.shape[0] // N_DEV, d.shape[1])
    per_dev = [d_in5, rows(d_wglu), rows(d_wo5), d_ing, rows(d_wog)]
    got = _pair_exchange_call(per_dev, "scatter_grads_pair")
    core = jnp.reshape(ic, (1,)).astype(jnp.int32)
    chip = jnp.reshape(2 * ix + iy, (1,)).astype(jnp.int32)
    pair = [_pair_sum_call(g, r, core, "pair_sum_" + n) for g, r, n in zip(per_dev, got, BIG_NAMES)]
    recv = _chip_exchange_call(pair, "scatter_grads_chips")
    big = [_adam_own_call(p, chip, r, W[n][0], M[n][0], V[n][0], "adam_" + n, _tile(W[n].shape[1], 128))
           for p, r, n in zip(pair, recv, BIG_NAMES)]
    big = [[o[None] for o in outs] for outs in big]

    small_parts = [dmod, d_norm_w, d_lre, d_lim, d_logdt, d_bre, d_bim, d_cre, d_cim, d_s5d, d_alog, d_dtb, d_fnw, d_conv, d_gnw]
    small_shapes = [p.shape for p in small_parts]
    sg = _allgather_call(_pack(small_parts, ADAM_ROWS), "gather_small_grads", False)
    sg = sg.reshape(N_DEV, -1, LANES)
    tot = _unpack(_sum_call(sg, "sum_small_grads"), small_shapes)
    (g_adab, g_norm_w, g_lre, g_lim, g_logdt, g_bre, g_bim, g_cre, g_cim, g_s5d, g_alog, g_dtb, g_fnw, g_conv, g_gnw) = tot
    g_conv_mine = lax.dynamic_slice_in_dim(g_conv, me * n_conv, n_conv, axis=1)
    g_gnw_mine = lax.dynamic_slice_in_dim(g_gnw, me * n_gnw, n_gnw, axis=0)
    small_g = {"ada_b": g_adab.reshape(W["ada_b"].shape), "norm_w": g_norm_w, "s5_lambda_re": g_lre[None], "s5_lambda_im": g_lim[None],
               "s5_log_dt": g_logdt[None], "s5_b_re": g_bre[None], "s5_b_im": g_bim[None], "s5_c_re": g_cre[None],
               "s5_c_im": g_cim[None], "s5_d": g_s5d[None], "gdn_a_log": g_alog[None], "gdn_dt_bias": g_dtb[None],
               "final_norm_w": g_fnw, "gdn_conv_w": g_conv_mine[None], "gdn_norm_w": g_gnw_mine[None]}
    small_names = SMALL_NAMES + ("gdn_conv_w", "gdn_norm_w")
    small = _adam_call(_pack([small_g[n] for n in small_names], ADAM_ROWS)[None], _pack([W[n] for n in small_names], ADAM_ROWS),
                       _pack([M[n] for n in small_names], ADAM_ROWS), _pack([V[n] for n in small_names], ADAM_ROWS), "adam_small")
    small = [_unpack(b, [W[n].shape for n in small_names]) for b in small]

    dmod_all = sg.reshape(N_DEV, -1)[:, :2 * 3 * D_MODEL].reshape(N_DEV, 2, N_DEV, n_ada)
    dmod_mine = lax.dynamic_index_in_dim(dmod_all, me, axis=2, keepdims=False).transpose(1, 0, 2)
    g_ada_w = _ada_grad_call(c_all, dmod_mine)
    ada = _adam_call(g_ada_w.reshape(1, -1, LANES), W["ada_w"].reshape(-1, LANES), M["ada_w"].reshape(-1, LANES),
                     V["ada_w"].reshape(-1, LANES), "adam_ada")
    ada = [a.reshape(W["ada_w"].shape) for a in ada]

    res = {}
    for i, n in enumerate(BIG_NAMES):
        res[n] = big[i]
    for i, n in enumerate(small_names):
        res[n] = [b[i] for b in small]
    res["ada_w"] = ada
    outs = [loss, dx[None]]
    for j in range(4):
        outs += [res[n][j] for n in WEIGHT_ORDER]
    return tuple(outs)


def kernel(x, c, ada_w, ada_b, norm_w, s5_w_in, s5_lambda_re, s5_lambda_im, s5_log_dt, s5_b_re, s5_b_im, s5_c_re, s5_c_im, s5_d, s5_w_glu, s5_w_out, gdn_w_in, gdn_conv_w, gdn_a_log, gdn_dt_bias, gdn_norm_w, gdn_w_out, final_norm_w, loss_target, m_ada_w, m_ada_b, m_norm_w, m_s5_w_in, m_s5_lambda_re, m_s5_lambda_im, m_s5_log_dt, m_s5_b_re, m_s5_b_im, m_s5_c_re, m_s5_c_im, m_s5_d, m_s5_w_glu, m_s5_w_out, m_gdn_w_in, m_gdn_conv_w, m_gdn_a_log, m_gdn_dt_bias, m_gdn_norm_w, m_gdn_w_out, m_final_norm_w, v_ada_w, v_ada_b, v_norm_w, v_s5_w_in, v_s5_lambda_re, v_s5_lambda_im, v_s5_log_dt, v_s5_b_re, v_s5_b_im, v_s5_c_re, v_s5_c_im, v_s5_d, v_s5_w_glu, v_s5_w_out, v_gdn_w_in, v_gdn_conv_w, v_gdn_a_log, v_gdn_dt_bias, v_gdn_norm_w, v_gdn_w_out, v_final_norm_w):
    W = dict(ada_w=ada_w, ada_b=ada_b, norm_w=norm_w, s5_w_in=s5_w_in, s5_lambda_re=s5_lambda_re, s5_lambda_im=s5_lambda_im,
             s5_log_dt=s5_log_dt, s5_b_re=s5_b_re, s5_b_im=s5_b_im, s5_c_re=s5_c_re, s5_c_im=s5_c_im, s5_d=s5_d,
             s5_w_glu=s5_w_glu, s5_w_out=s5_w_out, gdn_w_in=gdn_w_in, gdn_conv_w=gdn_conv_w, gdn_a_log=gdn_a_log,
             gdn_dt_bias=gdn_dt_bias, gdn_norm_w=gdn_norm_w, gdn_w_out=gdn_w_out, final_norm_w=final_norm_w)
    M = dict(ada_w=m_ada_w, ada_b=m_ada_b, norm_w=m_norm_w, s5_w_in=m_s5_w_in, s5_lambda_re=m_s5_lambda_re,
             s5_lambda_im=m_s5_lambda_im, s5_log_dt=m_s5_log_dt, s5_b_re=m_s5_b_re, s5_b_im=m_s5_b_im, s5_c_re=m_s5_c_re,
             s5_c_im=m_s5_c_im, s5_d=m_s5_d, s5_w_glu=m_s5_w_glu, s5_w_out=m_s5_w_out, gdn_w_in=m_gdn_w_in,
             gdn_conv_w=m_gdn_conv_w, gdn_a_log=m_gdn_a_log, gdn_dt_bias=m_gdn_dt_bias, gdn_norm_w=m_gdn_norm_w,
             gdn_w_out=m_gdn_w_out, final_norm_w=m_final_norm_w)
    V = dict(ada_w=v_ada_w, ada_b=v_ada_b, norm_w=v_norm_w, s5_w_in=v_s5_w_in, s5_lambda_re=v_s5_lambda_re,
             s5_lambda_im=v_s5_lambda_im, s5_log_dt=v_s5_log_dt, s5_b_re=v_s5_b_re, s5_b_im=v_s5_b_im, s5_c_re=v_s5_c_re,
             s5_c_im=v_s5_c_im, s5_d=v_s5_d, s5_w_glu=v_s5_w_glu, s5_w_out=v_s5_w_out, gdn_w_in=v_gdn_w_in,
             gdn_conv_w=v_gdn_conv_w, gdn_a_log=v_gdn_a_log, gdn_dt_bias=v_gdn_dt_bias, gdn_norm_w=v_gdn_norm_w,
             gdn_w_out=v_gdn_w_out, final_norm_w=v_final_norm_w)
    return _step(x, c, W, M, V, loss_target)
```

```python
import functools
import math

import jax
import jax.numpy as jnp
from jax import lax
from jax.experimental import pallas as pl
from jax.experimental.pallas import tpu as pltpu

F32 = jnp.float32
BF16 = jnp.bfloat16
SDS = jax.ShapeDtypeStruct

D_MODEL = 1024
D_INNER = 2048
NORM_EPS = 1e-6
S5_GROUP = 16
S5_GROUPS = 128
S5_STATE = 64
GDN_HEADS = 8
GDN_DK = 128
GDN_DV = 256
GDN_CONV = 4
GDN_CHUNK = 64
GDN_QK = 1024
GDN_CONV_CH = 4096
GDN_PROJ = 6160
ADAM_LR = 0.001
ADAM_B1 = 0.9
ADAM_B2 = 0.999
ADAM_EPS = 1e-08
ADAM_WD = 0.01
ADAM_STEP = 10

N_DEV = 8
LANES = 128
SUBLANES = 8
VMEM_BIG = 56 << 20
VMEM_MID = 40 << 20
S5_GB = 8
S5_TL = 1024
MESH_AXES = ("x", "y", "c")


def _params(sem, vmem=None):
    return pltpu.CompilerParams(dimension_semantics=sem, vmem_limit_bytes=vmem)


def _bdot(a, b, dims=(((1,), (0,)), ((), ()))):
    return lax.dot_general(a.astype(BF16), b.astype(BF16), dims, preferred_element_type=F32)


def _hdot(a, b, dims=(((1,), (0,)), ((), ()))):
    return lax.dot_general(a, b, dims, preferred_element_type=F32, precision=lax.Precision.HIGHEST)


_BNN = (((2,), (1,)), ((0,), (0,)))
_BNT = (((2,), (2,)), ((0,), (0,)))
_BTN = (((1,), (1,)), ((0,), (0,)))


@jax.custom_vjp
def _unit_lower_inverse(a):
    c = a.shape[-1]
    ri = lax.broadcasted_iota(jnp.int32, a.shape, 1)
    ci = lax.broadcasted_iota(jnp.int32, a.shape, 2)
    n = -a
    t = (ri == ci).astype(F32) + n
    for _ in range(int(math.log2(c)) - 1):
        n = _hdot(n, n, _BNN)
        t = t + _hdot(t, n, _BNN)
    return t


def _unit_lower_inverse_fwd(a):
    t = _unit_lower_inverse(a)
    return t, t


def _unit_lower_inverse_bwd(t, g):
    return (-_hdot(_hdot(t, g, _BTN), t, _BNT),)


_unit_lower_inverse.defvjp(_unit_lower_inverse_fwd, _unit_lower_inverse_bwd)


NN = (((1,), (0,)), ((), ()))
NT = (((1,), (1,)), ((), ()))
TN = (((0,), (0,)), ((), ()))


def _tile(n, pref):
    for t in (pref, 512, 256, 128):
        if t <= n and n % t == 0:
            return t
    return n


def _matmul(a, b, mode, name):
    if mode == "nn":
        (m, k), (_, n) = a.shape, b.shape
    elif mode == "nt":
        (m, k), (n, _) = a.shape, b.shape
    else:
        (k, m), (_, n) = a.shape, b.shape
    tm, tn, tk = _tile(m, 512), _tile(n, 512), (k if k <= 2048 else _tile(k, 512))
    if mode == "tn":
        tm, tn = _tile(m, 1024), _tile(n, 1024)
    nk = k // tk
    dims = {"nn": NN, "nt": NT, "tn": TN}[mode]

    def body(a_ref, b_ref, o_ref, acc_ref):
        kk = pl.program_id(2)

        @pl.when(kk == 0)
        def _():
            acc_ref[...] = jnp.zeros_like(acc_ref)
        acc_ref[...] += _bdot(a_ref[...], b_ref[...], dims)

        @pl.when(kk == nk - 1)
        def _():
            o_ref[...] = acc_ref[...]

    a_spec = pl.BlockSpec((tk, tm), lambda i, j, q: (q, i)) if mode == "tn" else pl.BlockSpec((tm, tk), lambda i, j, q: (i, q))
    b_spec = pl.BlockSpec((tn, tk), lambda i, j, q: (j, q)) if mode == "nt" else pl.BlockSpec((tk, tn), lambda i, j, q: (q, j))
    return pl.pallas_call(
        body, name=name, grid=(m // tm, n // tn, nk),
        in_specs=[a_spec, b_spec], out_specs=pl.BlockSpec((tm, tn), lambda i, j, q: (i, j)),
        out_shape=SDS((m, n), F32), scratch_shapes=[pltpu.VMEM((tm, tn), F32)],
        compiler_params=_params(("parallel", "parallel", "arbitrary"), VMEM_MID),
    )(a, b)


def make_mm(name):
    @jax.custom_vjp
    def mm(a, w, grad_slot):
        return _matmul(a, w, "nn", name + "_fwd")

    def fwd(a, w, grad_slot):
        return _matmul(a, w, "nn", name + "_fwd"), (a, w)

    def bwd(res, g):
        a, w = res
        return _matmul(g, w, "nt", name + "_dx"), jnp.zeros_like(w), _matmul(a, g, "tn", name + "_dw")

    mm.defvjp(fwd, bwd)
    return mm


PROJ_ROWS = 256


def _proj_fwd_call(a, ws, name):
    m, k = a.shape
    tm = _tile(m, PROJ_ROWS)
    nw = len(ws)

    def body(*refs):
        ab = refs[0][...].astype(BF16)
        for w_ref, o_ref in zip(refs[1:1 + nw], refs[1 + nw:]):
            o_ref[...] = lax.dot_general(ab, w_ref[...], NN, preferred_element_type=F32)

    return pl.pallas_call(
        body, name=name, grid=(m // tm,),
        in_specs=[pl.BlockSpec((tm, k), lambda i: (i, 0))] + [pl.BlockSpec(w.shape, lambda i: (0, 0)) for w in ws],
        out_specs=[pl.BlockSpec((tm, w.shape[1]), lambda i: (i, 0)) for w in ws],
        out_shape=[SDS((m, w.shape[1]), F32) for w in ws],
        compiler_params=_params(("parallel",), VMEM_BIG),
    )(a, *ws)


def _proj_dx_call(gs, ws, name):
    m = gs[0].shape[0]
    k = ws[0].shape[0]
    tm = _tile(m, PROJ_ROWS)
    nw = len(ws)

    def body(*refs):
        acc = None
        for g_ref, w_ref in zip(refs[:nw], refs[nw:2 * nw]):
            part = _bdot(g_ref[...], w_ref[...], NT)
            acc = part if acc is None else acc + part
        refs[2 * nw][...] = acc

    return pl.pallas_call(
        body, name=name, grid=(m // tm,),
        in_specs=[pl.BlockSpec((tm, g.shape[1]), lambda i: (i, 0)) for g in gs] + [pl.BlockSpec(w.shape, lambda i: (0, 0)) for w in ws],
        out_specs=pl.BlockSpec((tm, k), lambda i: (i, 0)), out_shape=SDS((m, k), F32),
        compiler_params=_params(("parallel",), VMEM_BIG),
    )(*gs, *ws)


def make_proj(name):
    @jax.custom_vjp
    def proj(a, ws, grad_slots):
        return tuple(_proj_fwd_call(a, ws, name + "_fwd"))

    def fwd(a, ws, grad_slots):
        return tuple(_proj_fwd_call(a, ws, name + "_fwd")), (a, ws)

    def bwd(res, gs):
        a, ws = res
        dws = tuple(_matmul(a, g, "tn", "%s_dw%d" % (name, i)) for i, g in enumerate(gs))
        return _proj_dx_call(tuple(gs), ws, name + "_dx"), tuple(jnp.zeros_like(w) for w in ws), dws

    proj.defvjp(fwd, bwd)
    return proj


def make_rowwise(f, name, tm, n_rows, n_params, vmem=VMEM_MID, pass_first=False):
    def specs_of(arrs, blocked):
        if blocked:
            return [pl.BlockSpec((tm, a.shape[1]), lambda i: (i, 0)) for a in arrs]
        return [pl.BlockSpec(a.shape, lambda i: (0, 0)) for a in arrs]

    def out_structs(rows, params):
        blk = [SDS((tm, r.shape[1]), r.dtype) for r in rows] + [SDS(p.shape, p.dtype) for p in params]
        return jax.eval_shape(f, *blk)

    def run_fwd(rows, params):
        L = rows[0].shape[0]
        outs = out_structs(rows, params)

        def body(*refs):
            ins = [r[...] for r in refs[:n_rows + n_params]]
            res = f(*ins)
            for o_ref, val in zip(refs[n_rows + n_params:], res):
                o_ref[...] = val

        return pl.pallas_call(
            body, name=name + "_fwd", grid=(L // tm,),
            in_specs=specs_of(rows, True) + specs_of(params, False),
            out_specs=[pl.BlockSpec((tm, o.shape[1]), lambda i: (i, 0)) for o in outs],
            out_shape=[SDS((L, o.shape[1]), o.dtype) for o in outs],
            compiler_params=_params(("parallel",), vmem),
        )(*rows, *params)

    def run_bwd(rows, params, gs):
        L = rows[0].shape[0]
        n_g = len(gs)

        def body(*refs):
            i = pl.program_id(0)
            ins = [r[...] for r in refs[:n_rows + n_params]]
            cts = tuple(r[...] for r in refs[n_rows + n_params:n_rows + n_params + n_g])
            outs = refs[n_rows + n_params + n_g:]
            _, vjp = jax.vjp(f, *ins)
            grads = vjp(cts[:-1] if pass_first else cts)
            if pass_first:
                grads = (grads[0] + cts[-1],) + tuple(grads[1:])
            for o_ref, val in zip(outs[:n_rows], grads[:n_rows]):
                o_ref[...] = val

            if n_params:
                @pl.when(i == 0)
                def _():
                    for o_ref in outs[n_rows:]:
                        o_ref[...] = jnp.zeros_like(o_ref)
                for o_ref, val in zip(outs[n_rows:], grads[n_rows:]):
                    o_ref[...] += val

        res = pl.pallas_call(
            body, name=name + "_bwd", grid=(L // tm,),
            in_specs=specs_of(rows, True) + specs_of(params, False) + specs_of(gs, True),
            out_specs=specs_of(rows, True) + specs_of(params, False),
            out_shape=[SDS(r.shape, r.dtype) for r in rows] + [SDS(p.shape, p.dtype) for p in params],
            compiler_params=_params(("arbitrary",), vmem),
        )(*rows, *params, *gs)
        return tuple(res[:n_rows]), tuple(res[n_rows:])

    def outputs(rows, params):
        outs = tuple(run_fwd(rows, params))
        return outs + (rows[0],) if pass_first else outs

    @jax.custom_vjp
    def op(rows, params):
        return outputs(rows, params)

    def fwd(rows, params):
        return outputs(rows, params), (rows, params)

    def bwd(res, gs):
        rows, params = res
        return run_bwd(rows, params, tuple(gs))

    op.defvjp(fwd, bwd)
    op.run_fwd, op.run_bwd = run_fwd, run_bwd
    return op


def make_residual(name, tm):
    full = make_rowwise(_f_res, name, tm, 2, 2)
    branch = make_rowwise(lambda y, gate, bgate: ((gate + bgate) * y,), name + "_branch", tm, 1, 2)

    @jax.custom_vjp
    def op(x, y, gate, bgate):
        return full.run_fwd((x, y), (gate, bgate))[0]

    def fwd(x, y, gate, bgate):
        return full.run_fwd((x, y), (gate, bgate))[0], (y, gate, bgate)

    def bwd(res, g):
        y, gate, bgate = res
        (dy,), (dgate, dbgate) = branch.run_bwd((y,), (gate, bgate), (g,))
        return g, dy, dgate, dbgate

    op.defvjp(fwd, bwd)
    return op


def _s5_scan_rows(xr_ref, xi_ref, ar, ai, x0r, x0i, tl, reverse=False):
    n = xr_ref.shape[1]
    T = SUBLANES
    row = lax.broadcasted_iota(jnp.int32, (T, n), 0)
    pr, pi = [ar], [ai]
    for _ in range(T - 1):
        pr, pi = pr + [pr[-1] * ar - pi[-1] * ai], pi + [pr[-1] * ai + pi[-1] * ar]
    levels = []
    for d in (1, 2, 4):
        mask = (row < T - d) if reverse else (row >= d)
        levels.append((T - d if reverse else d, jnp.where(mask, pr[d - 1], 0.0), jnp.where(mask, pi[d - 1], 0.0)))
    cr = jnp.zeros((T, n), F32)
    ci = jnp.zeros((T, n), F32)
    for r in range(T):
        k = (T - r) if reverse else (r + 1)
        cr = jnp.where(row == r, pr[k - 1], cr)
        ci = jnp.where(row == r, pi[k - 1], ci)
    nt = tl // T
    last = 0 if reverse else T - 1

    def step(t, carry):
        sr, si = carry
        base = pl.multiple_of((nt - 1 - t if reverse else t) * T, T)
        br = xr_ref[pl.ds(base, T), :]
        bi = xi_ref[pl.ds(base, T), :]
        for shift, mr, mi in levels:
            qr = pltpu.roll(br, shift, 0)
            qi = pltpu.roll(bi, shift, 0)
            br, bi = br + (mr * qr - mi * qi), bi + (mr * qi + mi * qr)
        xr = br + (cr * sr - ci * si)
        xi = bi + (cr * si + ci * sr)
        xr_ref[pl.ds(base, T), :] = xr
        xi_ref[pl.ds(base, T), :] = xi
        return xr[last:last + 1, :], xi[last:last + 1, :]
    return lax.fori_loop(0, nt, step, (x0r, x0i))


def _s5_fwd_call(u, bre, bim, cre, cim, a, tl):
    L, e = u.shape
    nb = e // LANES
    ns = bre.shape[2]
    nc = L // tl

    def body(u_ref, bre_ref, bim_ref, cre_ref, cim_ref, a_ref, ys_ref, xb_ref, xr_ref, xi_ref, carry_ref):
        c = pl.program_id(1)

        @pl.when(c == 0)
        def _():
            carry_ref[...] = jnp.zeros_like(carry_ref)
        xb_ref[0, 0] = carry_ref[...]
        ub = u_ref[...]
        xr_ref[...] = _bdot(ub, bre_ref[0])
        xi_ref[...] = _bdot(ub, bim_ref[0])
        ar = a_ref[0, 0:1, :]
        ai = a_ref[0, 1:2, :]
        xr, xi = _s5_scan_rows(xr_ref, xi_ref, ar, ai, carry_ref[0:1, :], carry_ref[1:2, :], tl)
        carry_ref[0:1, :] = xr
        carry_ref[1:2, :] = xi
        ys_ref[...] = _bdot(xr_ref[...], cre_ref[0]) - _bdot(xi_ref[...], cim_ref[0])

    return pl.pallas_call(
        body, name="s5_core_fwd", grid=(nb, nc),
        in_specs=[pl.BlockSpec((tl, LANES), lambda j, c: (c, j)),
                  pl.BlockSpec((1, LANES, ns), lambda j, c: (j, 0, 0)), pl.BlockSpec((1, LANES, ns), lambda j, c: (j, 0, 0)),
                  pl.BlockSpec((1, ns, LANES), lambda j, c: (j, 0, 0)), pl.BlockSpec((1, ns, LANES), lambda j, c: (j, 0, 0)),
                  pl.BlockSpec((1, SUBLANES, ns), lambda j, c: (j, 0, 0))],
        out_specs=[pl.BlockSpec((tl, LANES), lambda j, c: (c, j)),
                   pl.BlockSpec((1, 1, SUBLANES, ns), lambda j, c: (j, c, 0, 0))],
        out_shape=[SDS((L, e), F32), SDS((nb, nc, SUBLANES, ns), F32)],
        scratch_shapes=[pltpu.VMEM((tl, ns), F32), pltpu.VMEM((tl, ns), F32), pltpu.VMEM((SUBLANES, ns), F32)],
        compiler_params=_params(("arbitrary", "arbitrary"), VMEM_MID),
    )(u, bre, bim, cre, cim, a)


def _s5_bwd_call(u, dys, du_other, bre, bim, cre, cim, a, xb, tl):
    L, e = u.shape
    nb = e // LANES
    ns = bre.shape[2]
    nc = L // tl

    def body(u_ref, dys_ref, duo_ref, bre_ref, bim_ref, cre_ref, cim_ref, a_ref, xb_ref,
             du_ref, dbre_ref, dbim_ref, dcre_ref, dcim_ref, da_ref,
             xr_ref, xi_ref, gr_ref, gi_ref, gcarry_ref):
        c = pl.program_id(1)

        @pl.when(c == 0)
        def _():
            gcarry_ref[...] = jnp.zeros_like(gcarry_ref)
            dbre_ref[...] = jnp.zeros_like(dbre_ref)
            dbim_ref[...] = jnp.zeros_like(dbim_ref)
            dcre_ref[...] = jnp.zeros_like(dcre_ref)
            dcim_ref[...] = jnp.zeros_like(dcim_ref)
            da_ref[...] = jnp.zeros_like(da_ref)

        ub = u_ref[...]
        dy = dys_ref[...]
        ar = a_ref[0, 0:1, :]
        ai = a_ref[0, 1:2, :]
        x0r = xb_ref[0, 0, 0:1, :]
        x0i = xb_ref[0, 0, 1:2, :]
        xr_ref[...] = _bdot(ub, bre_ref[0])
        xi_ref[...] = _bdot(ub, bim_ref[0])
        _s5_scan_rows(xr_ref, xi_ref, ar, ai, x0r, x0i, tl)
        dcre_ref[0] += _bdot(xr_ref[...], dy, TN)
        dcim_ref[0] -= _bdot(xi_ref[...], dy, TN)
        gr_ref[...] = _bdot(dy, cre_ref[0], NT)
        gi_ref[...] = -_bdot(dy, cim_ref[0], NT)

        g0r, g0i = _s5_scan_rows(gr_ref, gi_ref, ar, -ai, gcarry_ref[0:1, :], gcarry_ref[1:2, :], tl, reverse=True)
        gcarry_ref[0:1, :] = g0r
        gcarry_ref[1:2, :] = g0i
        row = lax.broadcasted_iota(jnp.int32, (tl, ns), 0)
        gr = gr_ref[...]
        gi = gi_ref[...]
        xpr = jnp.where(row == 0, x0r, pltpu.roll(xr_ref[...], 1, 0))
        xpi = jnp.where(row == 0, x0i, pltpu.roll(xi_ref[...], 1, 0))
        da_ref[0, 0:1, :] += jnp.sum(gr * xpr + gi * xpi, axis=0, keepdims=True)
        da_ref[0, 1:2, :] += jnp.sum(gi * xpr - gr * xpi, axis=0, keepdims=True)
        du_ref[...] = (_bdot(gr, bre_ref[0], NT) + _bdot(gi, bim_ref[0], NT)) + duo_ref[...]
        dbre_ref[0] += _bdot(ub, gr, TN)
        dbim_ref[0] += _bdot(ub, gi, TN)

    rev = lambda c: nc - 1 - c
    return pl.pallas_call(
        body, name="s5_core_bwd", grid=(nb, nc),
        in_specs=[pl.BlockSpec((tl, LANES), lambda j, c: (rev(c), j)), pl.BlockSpec((tl, LANES), lambda j, c: (rev(c), j)),
                  pl.BlockSpec((tl, LANES), lambda j, c: (rev(c), j)),
                  pl.BlockSpec((1, LANES, ns), lambda j, c: (j, 0, 0)), pl.BlockSpec((1, LANES, ns), lambda j, c: (j, 0, 0)),
                  pl.BlockSpec((1, ns, LANES), lambda j, c: (j, 0, 0)), pl.BlockSpec((1, ns, LANES), lambda j, c: (j, 0, 0)),
                  pl.BlockSpec((1, SUBLANES, ns), lambda j, c: (j, 0, 0)),
                  pl.BlockSpec((1, 1, SUBLANES, ns), lambda j, c: (j, rev(c), 0, 0))],
        out_specs=[pl.BlockSpec((tl, LANES), lambda j, c: (rev(c), j)),
                   pl.BlockSpec((1, LANES, ns), lambda j, c: (j, 0, 0)), pl.BlockSpec((1, LANES, ns), lambda j, c: (j, 0, 0)),
                   pl.BlockSpec((1, ns, LANES), lambda j, c: (j, 0, 0)), pl.BlockSpec((1, ns, LANES), lambda j, c: (j, 0, 0)),
                   pl.BlockSpec((1, SUBLANES, ns), lambda j, c: (j, 0, 0))],
        out_shape=[SDS((L, e), F32), SDS(bre.shape, F32), SDS(bim.shape, F32), SDS(cre.shape, F32), SDS(cim.shape, F32),
                   SDS(a.shape, F32)],
        scratch_shapes=[pltpu.VMEM((tl, ns), F32) for _ in range(4)] + [pltpu.VMEM((SUBLANES, ns), F32)],
        compiler_params=_params(("arbitrary", "arbitrary"), VMEM_MID),
    )(u, dys, du_other, bre, bim, cre, cim, a, xb)


def make_s5_core(tl):
    @jax.custom_vjp
    def s5_core(u, bre, bim, cre, cim, a):
        return _s5_fwd_call(u, bre, bim, cre, cim, a, tl)[0], u

    def fwd(u, bre, bim, cre, cim, a):
        ys, xb = _s5_fwd_call(u, bre, bim, cre, cim, a, tl)
        return (ys, u), (u, bre, bim, cre, cim, a, xb)

    def bwd(res, cts):
        u, bre, bim, cre, cim, a, xb = res
        dys, du_other = cts
        return tuple(_s5_bwd_call(u, dys, du_other, bre, bim, cre, cim, a, xb, tl))

    s5_core.defvjp(fwd, bwd)
    return s5_core


def _s5_block_params(lam_re, lam_im, log_dt, b_re, b_im, c_re, c_im):
    dt = jnp.exp(log_dt)[:, None]
    mag = jnp.exp(lam_re * dt)
    ab_re = mag * jnp.cos(lam_im * dt)
    ab_im = mag * jnp.sin(lam_im * dt)
    den = lam_re * lam_re + lam_im * lam_im
    nr = ab_re - 1.0
    ni = ab_im
    q_re = (nr * lam_re + ni * lam_im) / den
    q_im = (ni * lam_re - nr * lam_im) / den
    bb_re = q_re[..., None] * b_re - q_im[..., None] * b_im
    bb_im = q_re[..., None] * b_im + q_im[..., None] * b_re
    nb = S5_GROUPS // S5_GB
    eye = jnp.eye(S5_GB, dtype=F32)

    def bdiag_in(bb):
        t = bb.reshape(nb, S5_GB, S5_STATE, S5_GROUP)
        t = jnp.einsum("jgpm,gh->jgmhp", t, eye)
        return t.reshape(nb, S5_GB * S5_GROUP, S5_GB * S5_STATE)

    def bdiag_out(cc):
        t = cc.reshape(nb, S5_GB, S5_GROUP, S5_STATE)
        t = jnp.einsum("jgmp,gh->jgphm", t, eye)
        return t.reshape(nb, S5_GB * S5_STATE, S5_GB * S5_GROUP)

    a = jnp.stack([ab_re.reshape(nb, S5_GB * S5_STATE), ab_im.reshape(nb, S5_GB * S5_STATE)], axis=1)
    a = jnp.concatenate([a, jnp.zeros((nb, SUBLANES - 2, S5_GB * S5_STATE), F32)], axis=1)
    return bdiag_in(bb_re), bdiag_in(bb_im), bdiag_out(c_re), bdiag_out(c_im), a


def _shift_down(x, s, row):
    if s == 0:
        return x
    return jnp.where(row >= s, pltpu.roll(x, s, 0), 0.0)


def _shift_up(x, s, row, n):
    if s == 0:
        return x
    return jnp.where(row < n - s, pltpu.roll(x, n - s, 0), 0.0)


def _conv_fwd_call(x, w):
    L, ch = x.shape

    def body(x_ref, w_ref, y_ref):
        xv = x_ref[...]
        row = lax.broadcasted_iota(jnp.int32, xv.shape, 0)
        acc = jnp.zeros_like(xv)
        for j in range(GDN_CONV):
            acc += w_ref[j:j + 1, :] * _shift_down(xv, GDN_CONV - 1 - j, row)
        y_ref[...] = acc

    return pl.pallas_call(
        body, name="gdn_conv_fwd", grid=(ch // LANES,),
        in_specs=[pl.BlockSpec((L, LANES), lambda j: (0, j)), pl.BlockSpec((SUBLANES, LANES), lambda j: (0, j))],
        out_specs=pl.BlockSpec((L, LANES), lambda j: (0, j)), out_shape=SDS((L, ch), F32),
        compiler_params=_params(("parallel",), VMEM_MID),
    )(x, w)


def _conv_bwd_call(x, w, dy):
    L, ch = x.shape

    def body(x_ref, w_ref, dy_ref, dx_ref, dw_ref):
        xv = x_ref[...]
        g = dy_ref[...]
        row = lax.broadcasted_iota(jnp.int32, xv.shape, 0)
        acc = jnp.zeros_like(xv)
        dws = []
        for j in range(GDN_CONV):
            s = GDN_CONV - 1 - j
            acc += w_ref[j:j + 1, :] * _shift_up(g, s, row, L)
            dws.append(jnp.sum(g * _shift_down(xv, s, row), axis=0, keepdims=True))
        dx_ref[...] = acc
        dw_ref[...] = jnp.concatenate(dws + [jnp.zeros((SUBLANES - GDN_CONV, LANES), F32)], axis=0)

    return pl.pallas_call(
        body, name="gdn_conv_bwd", grid=(ch // LANES,),
        in_specs=[pl.BlockSpec((L, LANES), lambda j: (0, j)), pl.BlockSpec((SUBLANES, LANES), lambda j: (0, j)),
                  pl.BlockSpec((L, LANES), lambda j: (0, j))],
        out_specs=[pl.BlockSpec((L, LANES), lambda j: (0, j)), pl.BlockSpec((SUBLANES, LANES), lambda j: (0, j))],
        out_shape=[SDS((L, ch), F32), SDS((SUBLANES, ch), F32)],
        compiler_params=_params(("parallel",), VMEM_MID),
    )(x, w, dy)


@jax.custom_vjp
def gdn_conv(x, w):
    return _conv_fwd_call(x, w)


def _gdn_conv_f(x, w):
    return _conv_fwd_call(x, w), (x, w)


def _gdn_conv_b(res, dy):
    x, w = res
    return tuple(_conv_bwd_call(x, w, dy))


gdn_conv.defvjp(_gdn_conv_f, _gdn_conv_b)


BNN = (((2,), (1,)), ((0,), (0,)))
BNT = (((2,), (2,)), ((0,), (0,)))
BTN = (((1,), (1,)), ((0,), (0,)))
GDN_PREP_BATCH = 8


@jax.custom_vjp
def _known_inverse(a, t):
    return t


def _known_inverse_fwd(a, t):
    return t, t


def _known_inverse_bwd(t, g):
    return -_hdot(_hdot(t, g, _BTN), t, _BNT), jnp.zeros_like(t)


_known_inverse.defvjp(_known_inverse_fwd, _known_inverse_bwd)


def _gdn_prep_math(q, k, v, beta, g, t_saved=None):
    B, C = q.shape[0], q.shape[1]
    ri = lax.broadcasted_iota(jnp.int32, (B, C, C), 1)
    ci = lax.broadcasted_iota(jnp.int32, (B, C, C), 2)
    causal = ri >= ci
    strict = ri > ci
    eye = (ri == ci).astype(F32)
    gb = jnp.broadcast_to(g, (B, C, C))
    g_row = jnp.sum(gb * eye, axis=1, keepdims=True)
    gc_col = jnp.sum(jnp.where(causal, jnp.broadcast_to(g_row, (B, C, C)), 0.0), axis=2, keepdims=True)
    gc_row = jnp.sum(jnp.where(ri <= ci, gb, 0.0), axis=1, keepdims=True)
    decay = jnp.exp(jnp.where(causal, gc_col - gc_row, -jnp.inf))
    kk = _bdot(k, k, BNT)
    a_mat = jnp.where(strict, beta * kk * decay, 0.0)
    t = _unit_lower_inverse(a_mat) if t_saved is None else _known_inverse(a_mat, t_saved)
    e_gc = jnp.exp(gc_col)
    w = _hdot(t, beta * e_gc * k, BNN)
    u = _hdot(t, beta * v, BNN)
    qk = _bdot(q, k, BNT) * decay
    q_dec = q * e_gc
    g_last = gc_col[:, C - 1:C, :]
    k_dec = k * jnp.exp(g_last - gc_col)
    return q_dec, w, u, qk, k_dec, gc_col, t


def _gdn_prep_specs(L):
    C = GDN_CHUNK
    nb = min(GDN_PREP_BATCH, L // C)
    R = nb * C
    ins = [pl.BlockSpec((R, GDN_DK), lambda c, h: (c, h)), pl.BlockSpec((R, GDN_DK), lambda c, h: (c, h)),
           pl.BlockSpec((R, GDN_DV), lambda c, h: (c, h)), pl.BlockSpec((R, LANES), lambda c, h: (c, 0))]
    outs = [pl.BlockSpec((1, R, GDN_DK), lambda c, h: (h, c, 0)), pl.BlockSpec((1, R, GDN_DK), lambda c, h: (h, c, 0)),
            pl.BlockSpec((1, R, GDN_DV), lambda c, h: (h, c, 0)), pl.BlockSpec((1, R, C), lambda c, h: (h, c, 0)),
            pl.BlockSpec((1, R, GDN_DK), lambda c, h: (h, c, 0)), pl.BlockSpec((1, R, 1), lambda c, h: (h, c, 0))]
    t_spec = pl.BlockSpec((1, R, C), lambda c, h: (h, c, 0))
    shapes = [SDS((GDN_HEADS, L, GDN_DK), F32), SDS((GDN_HEADS, L, GDN_DK), F32), SDS((GDN_HEADS, L, GDN_DV), F32),
              SDS((GDN_HEADS, L, C), F32), SDS((GDN_HEADS, L, GDN_DK), F32), SDS((GDN_HEADS, L, 1), F32)]
    return ins, outs, t_spec, shapes, nb


def _chunks(x, nb):
    return x.reshape(nb, x.shape[0] // nb, x.shape[1])


def _head_columns(bg, h):
    lane = lax.broadcasted_iota(jnp.int32, bg.shape, 1)
    beta = jnp.sum(jnp.where(lane == h, bg, 0.0), axis=1, keepdims=True)
    g = jnp.sum(jnp.where(lane == h + GDN_HEADS, bg, 0.0), axis=1, keepdims=True)
    return beta, g


def _gdn_prep_fwd_call(q, k, v, bg):
    L = q.shape[0]
    ins, outs, t_spec, shapes, nb = _gdn_prep_specs(L)

    def body(q_ref, k_ref, v_ref, bg_ref, *o_refs):
        beta, g = _head_columns(bg_ref[...], pl.program_id(1))
        res = _gdn_prep_math(_chunks(q_ref[...], nb), _chunks(k_ref[...], nb), _chunks(v_ref[...], nb),
                             _chunks(beta, nb), _chunks(g, nb))
        for o_ref, val in zip(o_refs, res):
            o_ref[0] = val.reshape(val.shape[0] * val.shape[1], val.shape[2])

    return pl.pallas_call(
        body, name="gdn_prep_fwd", grid=(L // (nb * GDN_CHUNK), GDN_HEADS), in_specs=ins, out_specs=outs + [t_spec],
        out_shape=shapes + [SDS((GDN_HEADS, L, GDN_CHUNK), F32)],
        compiler_params=_params(("parallel", "parallel"), VMEM_MID),
    )(q, k, v, bg)


def _gdn_prep_bwd_call(q, k, v, bg, t, cts):
    L = q.shape[0]
    ins, outs, t_spec, _, nb = _gdn_prep_specs(L)

    def body(q_ref, k_ref, v_ref, bg_ref, t_ref, c0, c1, c2, c3, c4, c5, dq_ref, dk_ref, dv_ref, dbg_ref):
        h = pl.program_id(1)
        beta, g = _head_columns(bg_ref[...], h)
        t_saved = _chunks(t_ref[0], nb)
        _, vjp = jax.vjp(lambda *a: _gdn_prep_math(*a, t_saved=t_saved)[:6], _chunks(q_ref[...], nb), _chunks(k_ref[...], nb),
                         _chunks(v_ref[...], nb), _chunks(beta, nb), _chunks(g, nb))
        dq, dk, dv, db, dg = vjp(tuple(_chunks(c[0], nb) for c in (c0, c1, c2, c3, c4, c5)))
        flat = lambda a: a.reshape(a.shape[0] * a.shape[1], a.shape[2])
        dq_ref[...] = flat(dq)
        dk_ref[...] = flat(dk)
        dv_ref[...] = flat(dv)

        @pl.when(h == 0)
        def _():
            dbg_ref[...] = jnp.zeros_like(dbg_ref)
        lane = lax.broadcasted_iota(jnp.int32, dbg_ref.shape, 1)
        dbg_ref[...] += jnp.where(lane == h, flat(db), 0.0) + jnp.where(lane == h + GDN_HEADS, flat(dg), 0.0)

    return pl.pallas_call(
        body, name="gdn_prep_bwd", grid=(L // (nb * GDN_CHUNK), GDN_HEADS), in_specs=ins + [t_spec] + outs, out_specs=ins,
        out_shape=[SDS(q.shape, F32), SDS(k.shape, F32), SDS(v.shape, F32), SDS(bg.shape, F32)],
        compiler_params=_params(("parallel", "arbitrary"), VMEM_MID),
    )(q, k, v, bg, t, *cts)


@jax.custom_vjp
def gdn_prep(q, k, v, bg):
    return tuple(_gdn_prep_fwd_call(q, k, v, bg)[:6])


def _gdn_prep_f(q, k, v, bg):
    res = _gdn_prep_fwd_call(q, k, v, bg)
    return tuple(res[:6]), (q, k, v, bg, res[6])


def _gdn_prep_b(res, cts):
    return tuple(_gdn_prep_bwd_call(*res, tuple(cts)))


gdn_prep.defvjp(_gdn_prep_f, _gdn_prep_b)


def _gdn_step_math(q_dec, w, u, qk, k_dec, gc, state):
    H, C = q_dec.shape[0], q_dec.shape[1]
    v_new = u - _bdot(w, state, BNN)
    o = _bdot(q_dec, state, BNN) + _bdot(qk, v_new, BNN)
    gl = gc[:, C - 1:C, :]
    new_state = jnp.exp(gl) * state + _bdot(k_dec, v_new, BTN)
    return jnp.concatenate([o[h] for h in range(H)], axis=1), new_state


def _gdn_scan_specs(L, rev):
    C, H = GDN_CHUNK, GDN_HEADS
    nc = L // C
    cc = (lambda c: nc - 1 - c) if rev else (lambda c: c)
    ins = [pl.BlockSpec((H, C, GDN_DK), lambda c: (0, cc(c), 0)), pl.BlockSpec((H, C, GDN_DK), lambda c: (0, cc(c), 0)),
           pl.BlockSpec((H, C, GDN_DV), lambda c: (0, cc(c), 0)), pl.BlockSpec((H, C, C), lambda c: (0, cc(c), 0)),
           pl.BlockSpec((H, C, GDN_DK), lambda c: (0, cc(c), 0)), pl.BlockSpec((H, C, 1), lambda c: (0, cc(c), 0))]
    o_spec = pl.BlockSpec((C, H * GDN_DV), lambda c: (cc(c), 0))
    s_spec = pl.BlockSpec((1, H, GDN_DK, GDN_DV), lambda c: (cc(c), 0, 0, 0))
    return ins, o_spec, s_spec, nc


def _gdn_scan_fwd_call(q_dec, w, u, qk, k_dec, gc):
    L = q_dec.shape[1]
    ins, o_spec, s_spec, nc = _gdn_scan_specs(L, False)

    def body(qd_ref, w_ref, u_ref, qk_ref, kd_ref, gc_ref, o_ref, sin_ref, s_ref):
        c = pl.program_id(0)

        @pl.when(c == 0)
        def _():
            s_ref[...] = jnp.zeros_like(s_ref)
        st = s_ref[...]
        sin_ref[0] = st
        o, ns = _gdn_step_math(qd_ref[...], w_ref[...], u_ref[...], qk_ref[...], kd_ref[...], gc_ref[...], st)
        o_ref[...] = o
        s_ref[...] = ns

    return pl.pallas_call(
        body, name="gdn_scan_fwd", grid=(nc,), in_specs=ins, out_specs=[o_spec, s_spec],
        out_shape=[SDS((L, GDN_HEADS * GDN_DV), F32), SDS((nc, GDN_HEADS, GDN_DK, GDN_DV), F32)],
        scratch_shapes=[pltpu.VMEM((GDN_HEADS, GDN_DK, GDN_DV), F32)],
        compiler_params=_params(("arbitrary",), VMEM_MID),
    )(q_dec, w, u, qk, k_dec, gc)


def _gdn_scan_bwd_call(q_dec, w, u, qk, k_dec, gc, s_in, do):
    L = q_dec.shape[1]
    ins, o_spec, s_spec, nc = _gdn_scan_specs(L, True)

    def body(qd_ref, w_ref, u_ref, qk_ref, kd_ref, gc_ref, sin_ref, do_ref,
             dqd_ref, dw_ref, du_ref, dqk_ref, dkd_ref, dgc_ref, ds_ref):
        c = pl.program_id(0)

        @pl.when(c == 0)
        def _():
            ds_ref[...] = jnp.zeros_like(ds_ref)
        _, vjp = jax.vjp(_gdn_step_math, qd_ref[...], w_ref[...], u_ref[...], qk_ref[...], kd_ref[...], gc_ref[...], sin_ref[0])
        dqd, dw, du, dqk, dkd, dgc, dst = vjp((do_ref[...], ds_ref[...]))
        dqd_ref[...] = dqd
        dw_ref[...] = dw
        du_ref[...] = du
        dqk_ref[...] = dqk
        dkd_ref[...] = dkd
        dgc_ref[...] = dgc
        ds_ref[...] = dst

    return pl.pallas_call(
        body, name="gdn_scan_bwd", grid=(nc,), in_specs=ins + [s_spec, o_spec], out_specs=ins,
        out_shape=[SDS(t.shape, F32) for t in (q_dec, w, u, qk, k_dec, gc)],
        scratch_shapes=[pltpu.VMEM((GDN_HEADS, GDN_DK, GDN_DV), F32)],
        compiler_params=_params(("arbitrary",), VMEM_MID),
    )(q_dec, w, u, qk, k_dec, gc, s_in, do)


@jax.custom_vjp
def gdn_scan(q_dec, w, u, qk, k_dec, gc):
    return _gdn_scan_fwd_call(q_dec, w, u, qk, k_dec, gc)[0]


def _gdn_scan_f(*args):
    o, s_in = _gdn_scan_fwd_call(*args)
    return o, (*args, s_in)


def _gdn_scan_b(res, do):
    return tuple(_gdn_scan_bwd_call(*res, do))


gdn_scan.defvjp(_gdn_scan_f, _gdn_scan_b)


def _silu(x):
    return x * jax.nn.sigmoid(x)


def _gelu_tanh(x):
    return 0.5 * x * (1.0 + jnp.tanh(math.sqrt(2.0 / math.pi) * (x + 0.044715 * (x * x * x))))


def _f_lnmod(x, nw, sc, sh, bsc, bsh):
    xn = x * lax.rsqrt(jnp.mean(x * x, axis=-1, keepdims=True) + NORM_EPS) * nw
    return (xn * (1.0 + (sc + bsc)) + (sh + bsh),)


def _f_s5_act(ys, u, d):
    return (_gelu_tanh(ys + d * u),)


def _f_s5_gate(y2, t, z):
    return (y2 * jax.nn.sigmoid(t) * _silu(z),)


def _f_res(x, y, gate, bgate):
    return (x + (gate + bgate) * y,)


def _heads(x, width, fn):
    return jnp.concatenate([fn(x[:, i * width:(i + 1) * width]) for i in range(x.shape[1] // width)], axis=1)


def _l2n(x):
    return x * lax.rsqrt(jnp.sum(x * x, axis=-1, keepdims=True) + NORM_EPS)


def _f_qnorm(x):
    return (_heads(_silu(x), GDN_DK, _l2n) * (GDN_DK ** -0.5),)


def _f_knorm(x):
    return (_heads(_silu(x), GDN_DK, _l2n),)


def _f_vact(x):
    return (_silu(x),)


def _f_betag(ba, alog, dtb):
    col = lax.broadcasted_iota(jnp.int32, ba.shape, 1)
    t = ba + dtb
    softplus = jnp.maximum(t, 0.0) + jnp.log1p(jnp.exp(-jnp.abs(t)))
    g = -jnp.exp(alog) * softplus
    return (jnp.where(col < GDN_HEADS, jax.nn.sigmoid(ba), jnp.where(col < 2 * GDN_HEADS, g, 0.0)),)


def _f_gdn_post(o, z, nw):
    on = _heads(o, GDN_DV, lambda t: t * lax.rsqrt(jnp.mean(t * t, axis=-1, keepdims=True) + NORM_EPS))
    return (on * nw * _silu(z),)


def _f_loss(x, tgt, fw):
    y = x * lax.rsqrt(jnp.mean(x * x, axis=-1, keepdims=True) + NORM_EPS) * fw
    err = y - tgt
    return (0.5 * jnp.mean(err * err, axis=-1, keepdims=True),)


def _ada_mod_call(c_all, ada_w):
    n = ada_w.shape[2]

    def body(c_ref, w_ref, o_ref):
        ca = _silu(c_ref[...])
        for l in range(ada_w.shape[0]):
            o_ref[l] = _bdot(ca, w_ref[l])

    return pl.pallas_call(body, name="ada_mod", out_shape=SDS((ada_w.shape[0], N_DEV, n), F32),
                          compiler_params=_params(None, VMEM_MID))(c_all, ada_w)


def _ada_grad_call(c_all, dmod):
    nl, _, n = dmod.shape

    def body(c_ref, d_ref, o_ref):
        ca = _silu(c_ref[...])
        for l in range(nl):
            o_ref[l] = _hdot(ca, d_ref[l], TN)

    return pl.pallas_call(body, name="ada_grad", out_shape=SDS((nl, c_all.shape[1], n), F32),
                          compiler_params=_params(None, VMEM_MID))(c_all, dmod)


ADAM_ROWS = 512


def _adamw(g, w, m, v):
    m2 = ADAM_B1 * m + (1.0 - ADAM_B1) * g
    v2 = ADAM_B2 * v + (1.0 - ADAM_B2) * (g * g)
    m_hat = m2 / (1.0 - ADAM_B1 ** ADAM_STEP)
    v_hat = v2 / (1.0 - ADAM_B2 ** ADAM_STEP)
    return g, -ADAM_LR * (m_hat / (jnp.sqrt(v_hat) + ADAM_EPS) + ADAM_WD * w), m2, v2


def _adam_call(gs, w, m, v, name, rows=None):
    n, r, cols = gs.shape
    rows = rows or ADAM_ROWS

    def body(g_ref, w_ref, m_ref, v_ref, go_ref, d_ref, mo_ref, vo_ref):
        g = g_ref[0]
        for s in range(1, n):
            g = g + g_ref[s]
        for o_ref, val in zip((go_ref, d_ref, mo_ref, vo_ref), _adamw(g, w_ref[...], m_ref[...], v_ref[...])):
            o_ref[...] = val

    blk = pl.BlockSpec((rows, cols), lambda i: (i, 0))
    return pl.pallas_call(
        body, name=name, grid=(r // rows,),
        in_specs=[pl.BlockSpec((n, rows, cols), lambda i: (0, i, 0)), blk, blk, blk],
        out_specs=[blk, blk, blk, blk], out_shape=[SDS((r, cols), F32)] * 4,
        compiler_params=_params(("parallel",), VMEM_MID),
    )(gs, w, m, v)


def _sum_call(gs, name):
    n, r, _ = gs.shape

    def body(g_ref, o_ref):
        g = g_ref[0]
        for s in range(1, n):
            g = g + g_ref[s]
        o_ref[...] = g

    return pl.pallas_call(
        body, name=name, grid=(r // ADAM_ROWS,),
        in_specs=[pl.BlockSpec((n, ADAM_ROWS, LANES), lambda i: (0, i, 0))],
        out_specs=pl.BlockSpec((ADAM_ROWS, LANES), lambda i: (i, 0)), out_shape=SDS((r, LANES), F32),
        compiler_params=_params(("parallel",), VMEM_MID),
    )(gs)


def _allgather_call(x_shard, name, in_hbm):
    m_per, n = x_shard.shape

    def body(x_ref, out_ref, send_sems, recv_sems, local_sem):
        x, y, c = lax.axis_index("x"), lax.axis_index("y"), lax.axis_index("c")
        me, sibling = (x, y, c), (x, y, 1 - c)
        chips = [(1 - x, y), (x, 1 - y), (1 - x, 1 - y)]

        def rows(px, py, pc):
            return out_ref.at[pl.ds((4 * px + 2 * py + pc) * m_per, m_per), :]

        def copy(k, block, to, src=None):
            return pltpu.make_async_remote_copy(
                src_ref=rows(*block) if src is None else src, dst_ref=rows(*block),
                send_sem=send_sems.at[k], recv_sem=recv_sems.at[k], device_id=to, device_id_type=pl.DeviceIdType.MESH)

        mine = pltpu.make_async_copy(x_ref, rows(*me), local_sem)
        mine.start()
        first = [copy(0, me, sibling, src=x_ref)]
        first += [copy(1 + j, me, (*chip, c), src=x_ref) for j, chip in enumerate(chips)]
        for cp in first:
            cp.start()
        passed = [copy(4 + j, (*chip, c), sibling) for j, chip in enumerate(chips)]
        for j, chip in enumerate(chips):
            copy(1 + j, (*chip, c), me).wait_recv()
            passed[j].start()
        copy(0, sibling, me).wait_recv()
        for j, chip in enumerate(chips):
            copy(4 + j, (*chip, 1 - c), me).wait_recv()
        for cp in first + passed:
            cp.wait_send()
        mine.wait()

    space = pl.ANY if in_hbm else pltpu.VMEM
    return pl.pallas_call(
        body, name=name, out_shape=SDS((N_DEV * m_per, n), x_shard.dtype),
        in_specs=[pl.BlockSpec(memory_space=space)], out_specs=pl.BlockSpec(memory_space=space),
        scratch_shapes=[pltpu.SemaphoreType.DMA((7,)), pltpu.SemaphoreType.DMA((7,)), pltpu.SemaphoreType.DMA],
        compiler_params=_params(None, None if in_hbm else VMEM_BIG),
    )(x_shard)


def _gather_weights_call(shards, name):
    nw = len(shards)

    def body(*refs):
        x_refs, out_refs = refs[:nw], refs[nw:2 * nw]
        send_sems, recv_sems, local_sems = refs[2 * nw:]
        x, y, c = lax.axis_index("x"), lax.axis_index("y"), lax.axis_index("c")
        me, sibling = (x, y, c), (x, y, 1 - c)
        chips = [(1 - x, y), (x, 1 - y), (1 - x, 1 - y)]

        def slot(w, px, py, pc):
            return out_refs[w].at[4 * px + 2 * py + pc]

        def copy(w, k, block, to, src=None):
            dst = slot(w, *block)
            return pltpu.make_async_remote_copy(
                src_ref=dst if src is None else src, dst_ref=dst, send_sem=send_sems.at[7 * w + k],
                recv_sem=recv_sems.at[7 * w + k], device_id=to, device_id_type=pl.DeviceIdType.MESH)

        mines = [pltpu.make_async_copy(x_refs[w], slot(w, *me), local_sems.at[w]) for w in range(nw)]
        for cp in mines:
            cp.start()
        first = [copy(w, 0, me, sibling, src=x_refs[w]) for w in range(nw)]
        first += [copy(w, 1 + j, me, (*chip, c), src=x_refs[w]) for w in range(nw) for j, chip in enumerate(chips)]
        for cp in first:
            cp.start()
        passed = []
        for w in range(nw):
            for j, chip in enumerate(chips):
                copy(w, 1 + j, (*chip, c), me).wait_recv()
                fwd = copy(w, 4 + j, (*chip, c), sibling)
                fwd.start()
                passed.append(fwd)
        for w in range(nw):
            copy(w, 0, sibling, me).wait_recv()
            for j, chip in enumerate(chips):
                copy(w, 4 + j, (*chip, 1 - c), me).wait_recv()
        for cp in first + passed:
            cp.wait_send()
        for cp in mines:
            cp.wait()

    hbm = pl.BlockSpec(memory_space=pl.ANY)
    return pl.pallas_call(
        body, name=name, out_shape=[SDS((N_DEV,) + s.shape, s.dtype) for s in shards],
        in_specs=[hbm] * nw, out_specs=[hbm] * nw,
        scratch_shapes=[pltpu.SemaphoreType.DMA((7 * nw,)), pltpu.SemaphoreType.DMA((7 * nw,)), pltpu.SemaphoreType.DMA((nw,))],
    )(*shards)


def _pair_exchange_call(grads, name):
    nw = len(grads)

    def body(*refs):
        g_refs, got_refs = refs[:nw], refs[nw:2 * nw]
        send_sems, recv_sems = refs[2 * nw:]
        x, y, c = lax.axis_index("x"), lax.axis_index("y"), lax.axis_index("c")
        copies = []
        for w in range(nw):
            for j in range(4):
                give = pltpu.make_async_remote_copy(
                    src_ref=g_refs[w].at[2 * j + 1 - c], dst_ref=got_refs[w].at[j], send_sem=send_sems.at[4 * w + j],
                    recv_sem=recv_sems.at[4 * w + j], device_id=(x, y, 1 - c), device_id_type=pl.DeviceIdType.MESH)
                give.start()
                copies.append(give)
        for cp in copies:
            cp.wait()

    hbm = pl.BlockSpec(memory_space=pl.ANY)
    return pl.pallas_call(
        body, name=name, out_shape=[SDS((4,) + g.shape[1:], g.dtype) for g in grads], in_specs=[hbm] * nw, out_specs=[hbm] * nw,
        scratch_shapes=[pltpu.SemaphoreType.DMA((4 * nw,)), pltpu.SemaphoreType.DMA((4 * nw,))],
    )(*grads)


def _chip_exchange_call(parts, name):
    nw = len(parts)

    def body(*refs):
        p_refs, out_refs = refs[:nw], refs[nw:2 * nw]
        send_sems, recv_sems = refs[2 * nw:]
        x, y, c = lax.axis_index("x"), lax.axis_index("y"), lax.axis_index("c")
        chips = [(1 - x, y), (x, 1 - y), (1 - x, 1 - y)]
        copies = []
        for w in range(nw):
            for j, (px, py) in enumerate(chips):
                give = pltpu.make_async_remote_copy(
                    src_ref=p_refs[w].at[2 * px + py], dst_ref=out_refs[w].at[j], send_sem=send_sems.at[3 * w + j],
                    recv_sem=recv_sems.at[3 * w + j], device_id=(px, py, c), device_id_type=pl.DeviceIdType.MESH)
                give.start()
                copies.append(give)
        for cp in copies:
            cp.wait()

    hbm = pl.BlockSpec(memory_space=pl.ANY)
    return pl.pallas_call(
        body, name=name, out_shape=[SDS((3,) + p.shape[1:], p.dtype) for p in parts], in_specs=[hbm] * nw, out_specs=[hbm] * nw,
        scratch_shapes=[pltpu.SemaphoreType.DMA((3 * nw,)), pltpu.SemaphoreType.DMA((3 * nw,))],
    )(*parts)


def _pair_sum_call(g, got, core, name):
    _, k, n = got.shape
    tr = _tile(k, 256)

    def body(c_ref, g_ref, got_ref, o_ref):
        o_ref[...] = (g_ref[...] + got_ref[...]).astype(o_ref.dtype)

    spec = pltpu.PrefetchScalarGridSpec(
        num_scalar_prefetch=1, grid=(4, k // tr),
        in_specs=[pl.BlockSpec((1, tr, n), lambda j, i, c: (2 * j + c[0], i, 0)), pl.BlockSpec((1, tr, n), lambda j, i, c: (j, i, 0))],
        out_specs=pl.BlockSpec((1, tr, n), lambda j, i, c: (j, i, 0)))
    return pl.pallas_call(body, name=name, grid_spec=spec, out_shape=SDS(got.shape, BF16),
                          compiler_params=_params(("parallel", "parallel"), VMEM_MID))(core, g, got)


def _adam_own_call(pair, chip, recv, w, m, v, name, rows):
    _, r, cols = recv.shape

    def body(chip_ref, p_ref, g_ref, w_ref, m_ref, v_ref, go_ref, d_ref, mo_ref, vo_ref):
        g = ((p_ref[0].astype(F32) + g_ref[0].astype(F32)) + g_ref[1].astype(F32)) + g_ref[2].astype(F32)
        for o_ref, val in zip((go_ref, d_ref, mo_ref, vo_ref), _adamw(g, w_ref[...], m_ref[...], v_ref[...])):
            o_ref[...] = val

    blk = pl.BlockSpec((rows, cols), lambda i, s: (i, 0))
    spec = pltpu.PrefetchScalarGridSpec(
        num_scalar_prefetch=1, grid=(r // rows,),
        in_specs=[pl.BlockSpec((1, rows, cols), lambda i, s: (s[0], i, 0)), pl.BlockSpec((3, rows, cols), lambda i, s: (0, i, 0)),
                  blk, blk, blk],
        out_specs=[blk, blk, blk, blk])
    return pl.pallas_call(body, name=name, grid_spec=spec, out_shape=[SDS((r, cols), F32)] * 4,
                          compiler_params=_params(("parallel",), VMEM_MID))(chip, pair, recv, w, m, v)


def _join_cols_call(w8, name):
    _, k, n = w8.shape
    tk = _tile(k, 256)

    def body(w_ref, o_ref):
        for s in range(N_DEV):
            o_ref[:, n * s:n * (s + 1)] = w_ref[s]

    return pl.pallas_call(body, name=name, grid=(k // tk,), in_specs=[pl.BlockSpec((N_DEV, tk, n), lambda i: (0, i, 0))],
                          out_specs=pl.BlockSpec((tk, N_DEV * n), lambda i: (i, 0)), out_shape=SDS((k, N_DEV * n), w8.dtype),
                          compiler_params=_params(("parallel",), VMEM_MID))(w8)


def _split_cols_call(g, name):
    k, n8 = g.shape
    n = n8 // N_DEV
    tk = _tile(k, 256)

    def body(g_ref, o_ref):
        for s in range(N_DEV):
            o_ref[s] = g_ref[:, n * s:n * (s + 1)]

    return pl.pallas_call(body, name=name, grid=(k // tk,), in_specs=[pl.BlockSpec((tk, n8), lambda i: (i, 0))],
                          out_specs=pl.BlockSpec((N_DEV, tk, n), lambda i: (0, i, 0)), out_shape=SDS((N_DEV, k, n), g.dtype),
                          compiler_params=_params(("parallel",), VMEM_MID))(g)


def _pack(parts, rows_multiple):
    flat = jnp.concatenate([p.reshape(-1) for p in parts])
    unit = rows_multiple * LANES
    padded = -(-flat.shape[0] // unit) * unit
    flat = jnp.concatenate([flat, jnp.zeros((padded - flat.shape[0],), F32)])
    return flat.reshape(-1, LANES)


def _unpack(buf, shapes):
    flat = buf.reshape(-1)
    out, off = [], 0
    for s in shapes:
        n = math.prod(s)
        out.append(flat[off:off + n].reshape(s))
        off += n
    return out


def _local_loss(diff, const, L):
    (x, mod_raw, norm_w, lam_re, lam_im, log_dt, b_re, b_im, c_re, c_im, s5_d, conv_w, a_log, dt_bias, gdn_nw, final_nw,
     *slots) = diff
    tgt, ada_b, weights = const
    lin = {n: (lambda a, n=n, i=i: make_mm(n)(a, weights[i], slots[i])) for i, n in enumerate(MM_NAMES)}
    tm = 256 if L % 256 == 0 else L
    mods = mod_raw.reshape(2, 3, 1, D_MODEL)
    biases = ada_b.reshape(2, 3, 1, D_MODEL)

    op_ln0 = make_rowwise(_f_lnmod, "ln0", tm, 1, 5, pass_first=True)
    h, x = op_ln0((x,), (norm_w[0:1], mods[0, 1], mods[0, 0], biases[0, 1], biases[0, 0]))
    u, z = make_proj("s5_in")(h, tuple(weights[0:2]), tuple(slots[0:2]))
    blocks = _s5_block_params(lam_re, lam_im, log_dt, b_re, b_im, c_re, c_im)
    ys, u = make_s5_core(min(S5_TL, L))(u, *blocks)
    (y2,) = make_rowwise(_f_s5_act, "s5_act", tm, 2, 1)((ys, u), (s5_d.reshape(1, D_INNER),))
    t = lin["s5_glu"](y2)
    (y4,) = make_rowwise(_f_s5_gate, "s5_gate", tm, 3, 0)((y2, t, z), ())
    o = lin["s5_out"](y4)
    x1 = make_residual("res0", tm)(x, o, mods[0, 2], biases[0, 2])

    op_ln1 = make_rowwise(_f_lnmod, "ln1", tm, 1, 5, pass_first=True)
    h, x1 = op_ln1((x1,), (norm_w[1:2], mods[1, 1], mods[1, 0], biases[1, 1], biases[1, 0]))
    q0, k0, v0, gz, ba = make_proj("gdn_in")(h, tuple(weights[4:9]), tuple(slots[4:9]))
    cw = jnp.concatenate([conv_w, jnp.zeros((SUBLANES - GDN_CONV, GDN_CONV_CH), F32)], axis=0)
    (q,) = make_rowwise(_f_qnorm, "gdn_qn", tm, 1, 0)((gdn_conv(q0, cw[:, :GDN_QK]),), ())
    (k,) = make_rowwise(_f_knorm, "gdn_kn", tm, 1, 0)((gdn_conv(k0, cw[:, GDN_QK:2 * GDN_QK]),), ())
    (v,) = make_rowwise(_f_vact, "gdn_va", tm, 1, 0)((gdn_conv(v0, cw[:, 2 * GDN_QK:]),), ())
    pad = jnp.zeros((LANES - 2 * GDN_HEADS,), F32)
    alog_row = jnp.concatenate([jnp.zeros((GDN_HEADS,), F32), a_log, pad]).reshape(1, LANES)
    dtb_row = jnp.concatenate([jnp.zeros((GDN_HEADS,), F32), dt_bias, pad]).reshape(1, LANES)
    (bg,) = make_rowwise(_f_betag, "gdn_bg", tm, 1, 2)((ba,), (alog_row, dtb_row))
    og = gdn_scan(*gdn_prep(q, k, v, bg))
    nw_row = jnp.tile(gdn_nw, GDN_HEADS).reshape(1, D_INNER)
    (on,) = make_rowwise(_f_gdn_post, "gdn_post", tm, 2, 1)((og, gz), (nw_row,))
    y = lin["gdn_out"](on)
    x2 = make_residual("res1", tm)(x1, y, mods[1, 2], biases[1, 2])

    (lt,) = make_rowwise(_f_loss, "loss", tm, 2, 1)((x2, tgt), (final_nw.reshape(1, D_MODEL),))
    return jnp.sum(lt)


MM_NAMES = ("s5_in_u", "s5_in_z", "s5_glu", "s5_out", "gdn_in_q", "gdn_in_k", "gdn_in_v", "gdn_in_z", "gdn_in_ba", "gdn_out")
SMALL_NAMES = ("ada_b", "norm_w", "s5_lambda_re", "s5_lambda_im", "s5_log_dt", "s5_b_re", "s5_b_im", "s5_c_re", "s5_c_im",
               "s5_d", "gdn_a_log", "gdn_dt_bias", "final_norm_w")
BIG_NAMES = ("s5_w_in", "s5_w_glu", "s5_w_out", "gdn_w_in", "gdn_w_out")
WEIGHT_ORDER = ("ada_w", "ada_b", "norm_w", "s5_w_in", "s5_lambda_re", "s5_lambda_im", "s5_log_dt", "s5_b_re", "s5_b_im",
                "s5_c_re", "s5_c_im", "s5_d", "s5_w_glu", "s5_w_out", "gdn_w_in", "gdn_conv_w", "gdn_a_log", "gdn_dt_bias",
                "gdn_norm_w", "gdn_w_out", "final_norm_w")


def _step(x, c, W, M, V, tgt):
    L = x.shape[1]
    ix, iy, ic = lax.axis_index("x"), lax.axis_index("y"), lax.axis_index("c")
    me = 4 * ix + 2 * iy + ic
    n_ada = W["ada_w"].shape[2]
    n_conv = W["gdn_conv_w"].shape[2]
    n_gnw = W["gdn_norm_w"].shape[1]

    g1 = _allgather_call(_pack([c, W["gdn_conv_w"], W["gdn_norm_w"]], SUBLANES), "gather_small_in", False)
    g1 = g1.reshape(N_DEV, -1)
    c_all = g1[:, :D_MODEL]
    conv_w = g1[:, D_MODEL:D_MODEL + GDN_CONV * n_conv].reshape(N_DEV, GDN_CONV, n_conv).transpose(1, 0, 2).reshape(GDN_CONV, -1)
    gdn_nw = g1[:, D_MODEL + GDN_CONV * n_conv:D_MODEL + GDN_CONV * n_conv + n_gnw].reshape(-1)
    mod_part = _ada_mod_call(c_all, W["ada_w"])
    g2 = _allgather_call(_pack([mod_part], SUBLANES), "gather_mod", False).reshape(N_DEV, -1)
    mod_all = g2[:, :2 * N_DEV * n_ada].reshape(N_DEV, 2, N_DEV, n_ada)
    mod_raw = lax.dynamic_index_in_dim(mod_all, me, axis=2, keepdims=False)
    mod_raw = mod_raw.transpose(1, 0, 2).reshape(2, 3 * D_MODEL)

    gathered = _gather_weights_call([W[n][0].astype(BF16) for n in BIG_NAMES], "gather_weights")
    full = dict(zip(BIG_NAMES, gathered))
    w_in5 = _join_cols_call(full["s5_w_in"], "join_s5_w_in")
    w_ing = _join_cols_call(full["gdn_w_in"], "join_gdn_w_in")
    w_ba = jnp.concatenate([w_ing[:, GDN_CONV_CH + D_INNER:], jnp.zeros((D_MODEL, LANES - 2 * GDN_HEADS), BF16)], axis=1)
    weights = (w_in5[:, :D_INNER], w_in5[:, D_INNER:], full["s5_w_glu"].reshape(D_INNER, D_INNER),
               full["s5_w_out"].reshape(D_INNER, D_MODEL),
               w_ing[:, :GDN_QK], w_ing[:, GDN_QK:2 * GDN_QK], w_ing[:, 2 * GDN_QK:GDN_CONV_CH],
               w_ing[:, GDN_CONV_CH:GDN_CONV_CH + D_INNER], w_ba, full["gdn_w_out"].reshape(D_INNER, D_MODEL))
    slots = tuple(jnp.zeros(w.shape, F32) for w in weights)
    diff = (x[0], mod_raw, W["norm_w"], W["s5_lambda_re"][0], W["s5_lambda_im"][0], W["s5_log_dt"][0], W["s5_b_re"][0],
            W["s5_b_im"][0], W["s5_c_re"][0], W["s5_c_im"][0], W["s5_d"][0], conv_w, W["gdn_a_log"][0], W["gdn_dt_bias"][0],
            gdn_nw, W["final_norm_w"], *slots)

    loss_local, grads = jax.value_and_grad(_local_loss)(diff, (tgt[0], W["ada_b"], weights), L)
    (dx, dmod, d_norm_w, d_lre, d_lim, d_logdt, d_bre, d_bim, d_cre, d_cim, d_s5d, d_conv, d_alog, d_dtb, d_gnw, d_fnw,
     d_wu, d_wz, d_wglu, d_wo5, d_wq, d_wk, d_wv, d_wgz, d_wba, d_wog) = grads
    loss = lax.psum(loss_local, MESH_AXES)

    d_in5 = _split_cols_call(jnp.concatenate([d_wu, d_wz], axis=1), "split_s5_w_in")
    d_ing = _split_cols_call(jnp.concatenate([d_wq, d_wk, d_wv, d_wgz, d_wba[:, :2 * GDN_HEADS]], axis=1), "split_gdn_w_in")
    rows = lambda d: d.reshape(N_DEV, d.shape[0] // N_DEV, d.shape[1])
    per_dev = [d_in5, rows(d_wglu), rows(d_wo5), d_ing, rows(d_wog)]
    got = _pair_exchange_call(per_dev, "scatter_grads_pair")
    core = jnp.reshape(ic, (1,)).astype(jnp.int32)
    chip = jnp.reshape(2 * ix + iy, (1,)).astype(jnp.int32)
    pair = [_pair_sum_call(g, r, core, "pair_sum_" + n) for g, r, n in zip(per_dev, got, BIG_NAMES)]
    recv = _chip_exchange_call(pair, "scatter_grads_chips")
    big = [_adam_own_call(p, chip, r, W[n][0], M[n][0], V[n][0], "adam_" + n, _tile(W[n].shape[1], 128))
           for p, r, n in zip(pair, recv, BIG_NAMES)]
    big = [[o[None] for o in outs] for outs in big]

    small_parts = [dmod, d_norm_w, d_lre, d_lim, d_logdt, d_bre, d_bim, d_cre, d_cim, d_s5d, d_alog, d_dtb, d_fnw, d_conv, d_gnw]
    small_shapes = [p.shape for p in small_parts]
    sg = _allgather_call(_pack(small_parts, ADAM_ROWS), "gather_small_grads", False)
    sg = sg.reshape(N_DEV, -1, LANES)
    tot = _unpack(_sum_call(sg, "sum_small_grads"), small_shapes)
    (g_adab, g_norm_w, g_lre, g_lim, g_logdt, g_bre, g_bim, g_cre, g_cim, g_s5d, g_alog, g_dtb, g_fnw, g_conv, g_gnw) = tot
    g_conv_mine = lax.dynamic_slice_in_dim(g_conv, me * n_conv, n_conv, axis=1)
    g_gnw_mine = lax.dynamic_slice_in_dim(g_gnw, me * n_gnw, n_gnw, axis=0)
    small_g = {"ada_b": g_adab.reshape(W["ada_b"].shape), "norm_w": g_norm_w, "s5_lambda_re": g_lre[None], "s5_lambda_im": g_lim[None],
               "s5_log_dt": g_logdt[None], "s5_b_re": g_bre[None], "s5_b_im": g_bim[None], "s5_c_re": g_cre[None],
               "s5_c_im": g_cim[None], "s5_d": g_s5d[None], "gdn_a_log": g_alog[None], "gdn_dt_bias": g_dtb[None],
               "final_norm_w": g_fnw, "gdn_conv_w": g_conv_mine[None], "gdn_norm_w": g_gnw_mine[None]}
    small_names = SMALL_NAMES + ("gdn_conv_w", "gdn_norm_w")
    small = _adam_call(_pack([small_g[n] for n in small_names], ADAM_ROWS)[None], _pack([W[n] for n in small_names], ADAM_ROWS),
                       _pack([M[n] for n in small_names], ADAM_ROWS), _pack([V[n] for n in small_names], ADAM_ROWS), "adam_small")
    small = [_unpack(b, [W[n].shape for n in small_names]) for b in small]

    dmod_all = sg.reshape(N_DEV, -1)[:, :2 * 3 * D_MODEL].reshape(N_DEV, 2, N_DEV, n_ada)
    dmod_mine = lax.dynamic_index_in_dim(dmod_all, me, axis=2, keepdims=False).transpose(1, 0, 2)
    g_ada_w = _ada_grad_call(c_all, dmod_mine)
    ada = _adam_call(g_ada_w.reshape(1, -1, LANES), W["ada_w"].reshape(-1, LANES), M["ada_w"].reshape(-1, LANES),
                     V["ada_w"].reshape(-1, LANES), "adam_ada")
    ada = [a.reshape(W["ada_w"].shape) for a in ada]

    res = {}
    for i, n in enumerate(BIG_NAMES):
        res[n] = big[i]
    for i, n in enumerate(small_names):
        res[n] = [b[i] for b in small]
    res["ada_w"] = ada
    outs = [loss, dx[None]]
    for j in range(4):
        outs += [res[n][j] for n in WEIGHT_ORDER]
    return tuple(outs)


def kernel(x, c, ada_w, ada_b, norm_w, s5_w_in, s5_lambda_re, s5_lambda_im, s5_log_dt, s5_b_re, s5_b_im, s5_c_re, s5_c_im, s5_d, s5_w_glu, s5_w_out, gdn_w_in, gdn_conv_w, gdn_a_log, gdn_dt_bias, gdn_norm_w, gdn_w_out, final_norm_w, loss_target, m_ada_w, m_ada_b, m_norm_w, m_s5_w_in, m_s5_lambda_re, m_s5_lambda_im, m_s5_log_dt, m_s5_b_re, m_s5_b_im, m_s5_c_re, m_s5_c_im, m_s5_d, m_s5_w_glu, m_s5_w_out, m_gdn_w_in, m_gdn_conv_w, m_gdn_a_log, m_gdn_dt_bias, m_gdn_norm_w, m_gdn_w_out, m_final_norm_w, v_ada_w, v_ada_b, v_norm_w, v_s5_w_in, v_s5_lambda_re, v_s5_lambda_im, v_s5_log_dt, v_s5_b_re, v_s5_b_im, v_s5_c_re, v_s5_c_im, v_s5_d, v_s5_w_glu, v_s5_w_out, v_gdn_w_in, v_gdn_conv_w, v_gdn_a_log, v_gdn_dt_bias, v_gdn_norm_w, v_gdn_w_out, v_final_norm_w):
    W = dict(ada_w=ada_w, ada_b=ada_b, norm_w=norm_w, s5_w_in=s5_w_in, s5_lambda_re=s5_lambda_re, s5_lambda_im=s5_lambda_im,
             s5_log_dt=s5_log_dt, s5_b_re=s5_b_re, s5_b_im=s5_b_im, s5_c_re=s5_c_re, s5_c_im=s5_c_im, s5_d=s5_d,
             s5_w_glu=s5_w_glu, s5_w_out=s5_w_out, gdn_w_in=gdn_w_in, gdn_conv_w=gdn_conv_w, gdn_a_log=gdn_a_log,
             gdn_dt_bias=gdn_dt_bias, gdn_norm_w=gdn_norm_w, gdn_w_out=gdn_w_out, final_norm_w=final_norm_w)
    M = dict(ada_w=m_ada_w, ada_b=m_ada_b, norm_w=m_norm_w, s5_w_in=m_s5_w_in, s5_lambda_re=m_s5_lambda_re,
             s5_lambda_im=m_s5_lambda_im, s5_log_dt=m_s5_log_dt, s5_b_re=m_s5_b_re, s5_b_im=m_s5_b_im, s5_c_re=m_s5_c_re,
             s5_c_im=m_s5_c_im, s5_d=m_s5_d, s5_w_glu=m_s5_w_glu, s5_w_out=m_s5_w_out, gdn_w_in=m_gdn_w_in,
             gdn_conv_w=m_gdn_conv_w, gdn_a_log=m_gdn_a_log, gdn_dt_bias=m_gdn_dt_bias, gdn_norm_w=m_gdn_norm_w,
             gdn_w_out=m_gdn_w_out, final_norm_w=m_final_norm_w)
    V = dict(ada_w=v_ada_w, ada_b=v_ada_b, norm_w=v_norm_w, s5_w_in=v_s5_w_in, s5_lambda_re=v_s5_lambda_re,
             s5_lambda_im=v_s5_lambda_im, s5_log_dt=v_s5_log_dt, s5_b_re=v_s5_b_re, s5_b_im=v_s5_b_im, s5_c_re=v_s5_c_re,
             s5_c_im=v_s5_c_im, s5_d=v_s5_d, s5_w_glu=v_s5_w_glu, s5_w_out=v_s5_w_out, gdn_w_in=v_gdn_w_in,
             gdn_conv_w=v_gdn_conv_w, gdn_a_log=v_gdn_a_log, gdn_dt_bias=v_gdn_dt_bias, gdn_norm_w=v_gdn_norm_w,
             gdn_w_out=v_gdn_w_out, final_norm_w=v_final_norm_w)
    return _step(x, c, W, M, V, loss_target)
```

```python
import functools
import math

import jax
import jax.numpy as jnp
from jax import lax
from jax.experimental import pallas as pl
from jax.experimental.pallas import tpu as pltpu

F32 = jnp.float32
BF16 = jnp.bfloat16
SDS = jax.ShapeDtypeStruct

D_MODEL = 1024
D_INNER = 2048
NORM_EPS = 1e-6
S5_GROUP = 16
S5_GROUPS = 128
S5_STATE = 64
GDN_HEADS = 8
GDN_DK = 128
GDN_DV = 256
GDN_CONV = 4
GDN_CHUNK = 64
GDN_QK = 1024
GDN_CONV_CH = 4096
GDN_PROJ = 6160
ADAM_LR = 0.001
ADAM_B1 = 0.9
ADAM_B2 = 0.999
ADAM_EPS = 1e-08
ADAM_WD = 0.01
ADAM_STEP = 10

N_DEV = 8
LANES = 128
SUBLANES = 8
VMEM_BIG = 56 << 20
VMEM_MID = 40 << 20
S5_GB = 8
S5_TL = 1024
MESH_AXES = ("x", "y", "c")


def _params(sem, vmem=None):
    return pltpu.CompilerParams(dimension_semantics=sem, vmem_limit_bytes=vmem)


def _bdot(a, b, dims=(((1,), (0,)), ((), ()))):
    return lax.dot_general(a.astype(BF16), b.astype(BF16), dims, preferred_element_type=F32)


def _hdot(a, b, dims=(((1,), (0,)), ((), ()))):
    return lax.dot_general(a, b, dims, preferred_element_type=F32, precision=lax.Precision.HIGHEST)


_BNN = (((2,), (1,)), ((0,), (0,)))
_BNT = (((2,), (2,)), ((0,), (0,)))
_BTN = (((1,), (1,)), ((0,), (0,)))


@jax.custom_vjp
def _unit_lower_inverse(a):
    c = a.shape[-1]
    ri = lax.broadcasted_iota(jnp.int32, a.shape, 1)
    ci = lax.broadcasted_iota(jnp.int32, a.shape, 2)
    n = -a
    t = (ri == ci).astype(F32) + n
    for _ in range(int(math.log2(c)) - 1):
        n = _hdot(n, n, _BNN)
        t = t + _hdot(t, n, _BNN)
    return t


def _unit_lower_inverse_fwd(a):
    t = _unit_lower_inverse(a)
    return t, t


def _unit_lower_inverse_bwd(t, g):
    return (-_hdot(_hdot(t, g, _BTN), t, _BNT),)


_unit_lower_inverse.defvjp(_unit_lower_inverse_fwd, _unit_lower_inverse_bwd)


NN = (((1,), (0,)), ((), ()))
NT = (((1,), (1,)), ((), ()))
TN = (((0,), (0,)), ((), ()))


def _tile(n, pref):
    for t in (pref, 512, 256, 128):
        if t <= n and n % t == 0:
            return t
    return n


def _matmul(a, b, mode, name):
    if mode == "nn":
        (m, k), (_, n) = a.shape, b.shape
    elif mode == "nt":
        (m, k), (n, _) = a.shape, b.shape
    else:
        (k, m), (_, n) = a.shape, b.shape
    tm, tn, tk = _tile(m, 512), _tile(n, 512), (k if k <= 2048 else _tile(k, 512))
    if mode == "tn":
        tm, tn = _tile(m, 1024), _tile(n, 1024)
    nk = k // tk
    dims = {"nn": NN, "nt": NT, "tn": TN}[mode]

    def body(a_ref, b_ref, o_ref, acc_ref):
        kk = pl.program_id(2)

        @pl.when(kk == 0)
        def _():
            acc_ref[...] = jnp.zeros_like(acc_ref)
        acc_ref[...] += _bdot(a_ref[...], b_ref[...], dims)

        @pl.when(kk == nk - 1)
        def _():
            o_ref[...] = acc_ref[...]

    a_spec = pl.BlockSpec((tk, tm), lambda i, j, q: (q, i)) if mode == "tn" else pl.BlockSpec((tm, tk), lambda i, j, q: (i, q))
    b_spec = pl.BlockSpec((tn, tk), lambda i, j, q: (j, q)) if mode == "nt" else pl.BlockSpec((tk, tn), lambda i, j, q: (q, j))
    return pl.pallas_call(
        body, name=name, grid=(m // tm, n // tn, nk),
        in_specs=[a_spec, b_spec], out_specs=pl.BlockSpec((tm, tn), lambda i, j, q: (i, j)),
        out_shape=SDS((m, n), F32), scratch_shapes=[pltpu.VMEM((tm, tn), F32)],
        compiler_params=_params(("parallel", "parallel", "arbitrary"), VMEM_MID),
    )(a, b)


def make_mm(name):
    @jax.custom_vjp
    def mm(a, w, grad_slot):
        return _matmul(a, w, "nn", name + "_fwd")

    def fwd(a, w, grad_slot):
        return _matmul(a, w, "nn", name + "_fwd"), (a, w)

    def bwd(res, g):
        a, w = res
        return _matmul(g, w, "nt", name + "_dx"), jnp.zeros_like(w), _matmul(a, g, "tn", name + "_dw")

    mm.defvjp(fwd, bwd)
    return mm


PROJ_ROWS = 256


def _proj_fwd_call(a, ws, name):
    m, k = a.shape
    tm = _tile(m, PROJ_ROWS)
    nw = len(ws)

    def body(*refs):
        ab = refs[0][...].astype(BF16)
        for w_ref, o_ref in zip(refs[1:1 + nw], refs[1 + nw:]):
            o_ref[...] = lax.dot_general(ab, w_ref[...], NN, preferred_element_type=F32)

    return pl.pallas_call(
        body, name=name, grid=(m // tm,),
        in_specs=[pl.BlockSpec((tm, k), lambda i: (i, 0))] + [pl.BlockSpec(w.shape, lambda i: (0, 0)) for w in ws],
        out_specs=[pl.BlockSpec((tm, w.shape[1]), lambda i: (i, 0)) for w in ws],
        out_shape=[SDS((m, w.shape[1]), F32) for w in ws],
        compiler_params=_params(("parallel",), VMEM_BIG),
    )(a, *ws)


def _proj_dx_call(gs, ws, name):
    m = gs[0].shape[0]
    k = ws[0].shape[0]
    tm = _tile(m, PROJ_ROWS)
    nw = len(ws)

    def body(*refs):
        acc = None
        for g_ref, w_ref in zip(refs[:nw], refs[nw:2 * nw]):
            part = _bdot(g_ref[...], w_ref[...], NT)
            acc = part if acc is None else acc + part
        refs[2 * nw][...] = acc

    return pl.pallas_call(
        body, name=name, grid=(m // tm,),
        in_specs=[pl.BlockSpec((tm, g.shape[1]), lambda i: (i, 0)) for g in gs] + [pl.BlockSpec(w.shape, lambda i: (0, 0)) for w in ws],
        out_specs=pl.BlockSpec((tm, k), lambda i: (i, 0)), out_shape=SDS((m, k), F32),
        compiler_params=_params(("parallel",), VMEM_BIG),
    )(*gs, *ws)


def make_proj(name):
    @jax.custom_vjp
    def proj(a, ws, grad_slots):
        return tuple(_proj_fwd_call(a, ws, name + "_fwd"))

    def fwd(a, ws, grad_slots):
        return tuple(_proj_fwd_call(a, ws, name + "_fwd")), (a, ws)

    def bwd(res, gs):
        a, ws = res
        dws = tuple(_matmul(a, g, "tn", "%s_dw%d" % (name, i)) for i, g in enumerate(gs))
        return _proj_dx_call(tuple(gs), ws, name + "_dx"), tuple(jnp.zeros_like(w) for w in ws), dws

    proj.defvjp(fwd, bwd)
    return proj


def make_rowwise(f, name, tm, n_rows, n_params, vmem=VMEM_MID, pass_first=False):
    def specs_of(arrs, blocked):
        if blocked:
            return [pl.BlockSpec((tm, a.shape[1]), lambda i: (i, 0)) for a in arrs]
        return [pl.BlockSpec(a.shape, lambda i: (0, 0)) for a in arrs]

    def out_structs(rows, params):
        blk = [SDS((tm, r.shape[1]), r.dtype) for r in rows] + [SDS(p.shape, p.dtype) for p in params]
        return jax.eval_shape(f, *blk)

    def run_fwd(rows, params):
        L = rows[0].shape[0]
        outs = out_structs(rows, params)

        def body(*refs):
            ins = [r[...] for r in refs[:n_rows + n_params]]
            res = f(*ins)
            for o_ref, val in zip(refs[n_rows + n_params:], res):
                o_ref[...] = val

        return pl.pallas_call(
            body, name=name + "_fwd", grid=(L // tm,),
            in_specs=specs_of(rows, True) + specs_of(params, False),
            out_specs=[pl.BlockSpec((tm, o.shape[1]), lambda i: (i, 0)) for o in outs],
            out_shape=[SDS((L, o.shape[1]), o.dtype) for o in outs],
            compiler_params=_params(("parallel",), vmem),
        )(*rows, *params)

    def run_bwd(rows, params, gs):
        L = rows[0].shape[0]
        n_g = len(gs)

        def body(*refs):
            i = pl.program_id(0)
            ins = [r[...] for r in refs[:n_rows + n_params]]
            cts = tuple(r[...] for r in refs[n_rows + n_params:n_rows + n_params + n_g])
            outs = refs[n_rows + n_params + n_g:]
            _, vjp = jax.vjp(f, *ins)
            grads = vjp(cts[:-1] if pass_first else cts)
            if pass_first:
                grads = (grads[0] + cts[-1],) + tuple(grads[1:])
            for o_ref, val in zip(outs[:n_rows], grads[:n_rows]):
                o_ref[...] = val

            if n_params:
                @pl.when(i == 0)
                def _():
                    for o_ref in outs[n_rows:]:
                        o_ref[...] = jnp.zeros_like(o_ref)
                for o_ref, val in zip(outs[n_rows:], grads[n_rows:]):
                    o_ref[...] += val

        res = pl.pallas_call(
            body, name=name + "_bwd", grid=(L // tm,),
            in_specs=specs_of(rows, True) + specs_of(params, False) + specs_of(gs, True),
            out_specs=specs_of(rows, True) + specs_of(params, False),
            out_shape=[SDS(r.shape, r.dtype) for r in rows] + [SDS(p.shape, p.dtype) for p in params],
            compiler_params=_params(("arbitrary",), vmem),
        )(*rows, *params, *gs)
        return tuple(res[:n_rows]), tuple(res[n_rows:])

    def outputs(rows, params):
        outs = tuple(run_fwd(rows, params))
        return outs + (rows[0],) if pass_first else outs

    @jax.custom_vjp
    def op(rows, params):
        return outputs(rows, params)

    def fwd(rows, params):
        return outputs(rows, params), (rows, params)

    def bwd(res, gs):
        rows, params = res
        return run_bwd(rows, params, tuple(gs))

    op.defvjp(fwd, bwd)
    op.run_fwd, op.run_bwd = run_fwd, run_bwd
    return op


def make_residual(name, tm):
    full = make_rowwise(_f_res, name, tm, 2, 2)
    branch = make_rowwise(lambda y, gate, bgate: ((gate + bgate) * y,), name + "_branch", tm, 1, 2)

    @jax.custom_vjp
    def op(x, y, gate, bgate):
        return full.run_fwd((x, y), (gate, bgate))[0]

    def fwd(x, y, gate, bgate):
        return full.run_fwd((x, y), (gate, bgate))[0], (y, gate, bgate)

    def bwd(res, g):
        y, gate, bgate = res
        (dy,), (dgate, dbgate) = branch.run_bwd((y,), (gate, bgate), (g,))
        return g, dy, dgate, dbgate

    op.defvjp(fwd, bwd)
    return op


def _s5_scan_rows(xr_ref, xi_ref, ar, ai, x0r, x0i, tl, reverse=False):
    n = xr_ref.shape[1]
    T = SUBLANES
    row = lax.broadcasted_iota(jnp.int32, (T, n), 0)
    pr, pi = [ar], [ai]
    for _ in range(T - 1):
        pr, pi = pr + [pr[-1] * ar - pi[-1] * ai], pi + [pr[-1] * ai + pi[-1] * ar]
    levels = []
    for d in (1, 2, 4):
        mask = (row < T - d) if reverse else (row >= d)
        levels.append((T - d if reverse else d, jnp.where(mask, pr[d - 1], 0.0), jnp.where(mask, pi[d - 1], 0.0)))
    cr = jnp.zeros((T, n), F32)
    ci = jnp.zeros((T, n), F32)
    for r in range(T):
        k = (T - r) if reverse else (r + 1)
        cr = jnp.where(row == r, pr[k - 1], cr)
        ci = jnp.where(row == r, pi[k - 1], ci)
    nt = tl // T
    last = 0 if reverse else T - 1

    def step(t, carry):
        sr, si = carry
        base = pl.multiple_of((nt - 1 - t if reverse else t) * T, T)
        br = xr_ref[pl.ds(base, T), :]
        bi = xi_ref[pl.ds(base, T), :]
        for shift, mr, mi in levels:
            qr = pltpu.roll(br, shift, 0)
            qi = pltpu.roll(bi, shift, 0)
            br, bi = br + (mr * qr - mi * qi), bi + (mr * qi + mi * qr)
        xr = br + (cr * sr - ci * si)
        xi = bi + (cr * si + ci * sr)
        xr_ref[pl.ds(base, T), :] = xr
        xi_ref[pl.ds(base, T), :] = xi
        return xr[last:last + 1, :], xi[last:last + 1, :]
    return lax.fori_loop(0, nt, step, (x0r, x0i))


def _s5_fwd_call(u, bre, bim, cre, cim, a, tl):
    L, e = u.shape
    nb = e // LANES
    ns = bre.shape[2]
    nc = L // tl

    def body(u_ref, bre_ref, bim_ref, cre_ref, cim_ref, a_ref, ys_ref, xb_ref, sr_ref, si_ref, xr_ref, xi_ref, carry_ref):
        c = pl.program_id(1)

        @pl.when(c == 0)
        def _():
            carry_ref[...] = jnp.zeros_like(carry_ref)
        xb_ref[0, 0] = carry_ref[...]
        ub = u_ref[...]
        xr_ref[...] = _bdot(ub, bre_ref[0])
        xi_ref[...] = _bdot(ub, bim_ref[0])
        ar = a_ref[0, 0:1, :]
        ai = a_ref[0, 1:2, :]
        xr, xi = _s5_scan_rows(xr_ref, xi_ref, ar, ai, carry_ref[0:1, :], carry_ref[1:2, :], tl)
        carry_ref[0:1, :] = xr
        carry_ref[1:2, :] = xi
        sr = xr_ref[...].astype(BF16)
        si = xi_ref[...].astype(BF16)
        sr_ref[...] = sr
        si_ref[...] = si
        ys_ref[...] = _bdot(sr, cre_ref[0]) - _bdot(si, cim_ref[0])

    return pl.pallas_call(
        body, name="s5_core_fwd", grid=(nb, nc),
        in_specs=[pl.BlockSpec((tl, LANES), lambda j, c: (c, j)),
                  pl.BlockSpec((1, LANES, ns), lambda j, c: (j, 0, 0)), pl.BlockSpec((1, LANES, ns), lambda j, c: (j, 0, 0)),
                  pl.BlockSpec((1, ns, LANES), lambda j, c: (j, 0, 0)), pl.BlockSpec((1, ns, LANES), lambda j, c: (j, 0, 0)),
                  pl.BlockSpec((1, SUBLANES, ns), lambda j, c: (j, 0, 0))],
        out_specs=[pl.BlockSpec((tl, LANES), lambda j, c: (c, j)),
                   pl.BlockSpec((1, 1, SUBLANES, ns), lambda j, c: (j, c, 0, 0)),
                   pl.BlockSpec((tl, ns), lambda j, c: (c, j)), pl.BlockSpec((tl, ns), lambda j, c: (c, j))],
        out_shape=[SDS((L, e), F32), SDS((nb, nc, SUBLANES, ns), F32), SDS((L, nb * ns), BF16), SDS((L, nb * ns), BF16)],
        scratch_shapes=[pltpu.VMEM((tl, ns), F32), pltpu.VMEM((tl, ns), F32), pltpu.VMEM((SUBLANES, ns), F32)],
        compiler_params=_params(("arbitrary", "arbitrary"), VMEM_MID),
    )(u, bre, bim, cre, cim, a)


def _s5_bwd_call(u, dys, du_other, bre, bim, cre, cim, a, xb, sr, si, tl):
    L, e = u.shape
    nb = e // LANES
    ns = bre.shape[2]
    nc = L // tl

    def body(u_ref, dys_ref, duo_ref, bre_ref, bim_ref, cre_ref, cim_ref, a_ref, xb_ref, sr_ref, si_ref,
             du_ref, dbre_ref, dbim_ref, dcre_ref, dcim_ref, da_ref,
             gr_ref, gi_ref, gcarry_ref):
        c = pl.program_id(1)

        @pl.when(c == 0)
        def _():
            gcarry_ref[...] = jnp.zeros_like(gcarry_ref)
            dbre_ref[...] = jnp.zeros_like(dbre_ref)
            dbim_ref[...] = jnp.zeros_like(dbim_ref)
            dcre_ref[...] = jnp.zeros_like(dcre_ref)
            dcim_ref[...] = jnp.zeros_like(dcim_ref)
            da_ref[...] = jnp.zeros_like(da_ref)

        ub = u_ref[...]
        dy = dys_ref[...]
        ar = a_ref[0, 0:1, :]
        ai = a_ref[0, 1:2, :]
        x0r = xb_ref[0, 0, 0:1, :]
        x0i = xb_ref[0, 0, 1:2, :]
        dcre_ref[0] += _bdot(sr_ref[...], dy, TN)
        dcim_ref[0] -= _bdot(si_ref[...], dy, TN)
        gr_ref[...] = _bdot(dy, cre_ref[0], NT)
        gi_ref[...] = -_bdot(dy, cim_ref[0], NT)

        g0r, g0i = _s5_scan_rows(gr_ref, gi_ref, ar, -ai, gcarry_ref[0:1, :], gcarry_ref[1:2, :], tl, reverse=True)
        gcarry_ref[0:1, :] = g0r
        gcarry_ref[1:2, :] = g0i
        row = lax.broadcasted_iota(jnp.int32, (tl, ns), 0)
        gr = gr_ref[...]
        gi = gi_ref[...]
        xpr = jnp.where(row == 0, x0r, pltpu.roll(sr_ref[...].astype(F32), 1, 0))
        xpi = jnp.where(row == 0, x0i, pltpu.roll(si_ref[...].astype(F32), 1, 0))
        da_ref[0, 0:1, :] += jnp.sum(gr * xpr + gi * xpi, axis=0, keepdims=True)
        da_ref[0, 1:2, :] += jnp.sum(gi * xpr - gr * xpi, axis=0, keepdims=True)
        du_ref[...] = (_bdot(gr, bre_ref[0], NT) + _bdot(gi, bim_ref[0], NT)) + duo_ref[...]
        dbre_ref[0] += _bdot(ub, gr, TN)
        dbim_ref[0] += _bdot(ub, gi, TN)

    rev = lambda c: nc - 1 - c
    return pl.pallas_call(
        body, name="s5_core_bwd", grid=(nb, nc),
        in_specs=[pl.BlockSpec((tl, LANES), lambda j, c: (rev(c), j)), pl.BlockSpec((tl, LANES), lambda j, c: (rev(c), j)),
                  pl.BlockSpec((tl, LANES), lambda j, c: (rev(c), j)),
                  pl.BlockSpec((1, LANES, ns), lambda j, c: (j, 0, 0)), pl.BlockSpec((1, LANES, ns), lambda j, c: (j, 0, 0)),
                  pl.BlockSpec((1, ns, LANES), lambda j, c: (j, 0, 0)), pl.BlockSpec((1, ns, LANES), lambda j, c: (j, 0, 0)),
                  pl.BlockSpec((1, SUBLANES, ns), lambda j, c: (j, 0, 0)),
                  pl.BlockSpec((1, 1, SUBLANES, ns), lambda j, c: (j, rev(c), 0, 0)),
                  pl.BlockSpec((tl, ns), lambda j, c: (rev(c), j)), pl.BlockSpec((tl, ns), lambda j, c: (rev(c), j))],
        out_specs=[pl.BlockSpec((tl, LANES), lambda j, c: (rev(c), j)),
                   pl.BlockSpec((1, LANES, ns), lambda j, c: (j, 0, 0)), pl.BlockSpec((1, LANES, ns), lambda j, c: (j, 0, 0)),
                   pl.BlockSpec((1, ns, LANES), lambda j, c: (j, 0, 0)), pl.BlockSpec((1, ns, LANES), lambda j, c: (j, 0, 0)),
                   pl.BlockSpec((1, SUBLANES, ns), lambda j, c: (j, 0, 0))],
        out_shape=[SDS((L, e), F32), SDS(bre.shape, F32), SDS(bim.shape, F32), SDS(cre.shape, F32), SDS(cim.shape, F32),
                   SDS(a.shape, F32)],
        scratch_shapes=[pltpu.VMEM((tl, ns), F32) for _ in range(2)] + [pltpu.VMEM((SUBLANES, ns), F32)],
        compiler_params=_params(("arbitrary", "arbitrary"), VMEM_MID),
    )(u, dys, du_other, bre, bim, cre, cim, a, xb, sr, si)


def make_s5_core(tl):
    @jax.custom_vjp
    def s5_core(u, bre, bim, cre, cim, a):
        return _s5_fwd_call(u, bre, bim, cre, cim, a, tl)[0], u

    def fwd(u, bre, bim, cre, cim, a):
        ys, xb, sr, si = _s5_fwd_call(u, bre, bim, cre, cim, a, tl)
        return (ys, u), (u, bre, bim, cre, cim, a, xb, sr, si)

    def bwd(res, cts):
        u, bre, bim, cre, cim, a, xb, sr, si = res
        dys, du_other = cts
        return tuple(_s5_bwd_call(u, dys, du_other, bre, bim, cre, cim, a, xb, sr, si, tl))

    s5_core.defvjp(fwd, bwd)
    return s5_core


def _s5_block_params(lam_re, lam_im, log_dt, b_re, b_im, c_re, c_im):
    dt = jnp.exp(log_dt)[:, None]
    mag = jnp.exp(lam_re * dt)
    ab_re = mag * jnp.cos(lam_im * dt)
    ab_im = mag * jnp.sin(lam_im * dt)
    den = lam_re * lam_re + lam_im * lam_im
    nr = ab_re - 1.0
    ni = ab_im
    q_re = (nr * lam_re + ni * lam_im) / den
    q_im = (ni * lam_re - nr * lam_im) / den
    bb_re = q_re[..., None] * b_re - q_im[..., None] * b_im
    bb_im = q_re[..., None] * b_im + q_im[..., None] * b_re
    nb = S5_GROUPS // S5_GB
    eye = jnp.eye(S5_GB, dtype=F32)

    def bdiag_in(bb):
        t = bb.reshape(nb, S5_GB, S5_STATE, S5_GROUP)
        t = jnp.einsum("jgpm,gh->jgmhp", t, eye)
        return t.reshape(nb, S5_GB * S5_GROUP, S5_GB * S5_STATE)

    def bdiag_out(cc):
        t = cc.reshape(nb, S5_GB, S5_GROUP, S5_STATE)
        t = jnp.einsum("jgmp,gh->jgphm", t, eye)
        return t.reshape(nb, S5_GB * S5_STATE, S5_GB * S5_GROUP)

    a = jnp.stack([ab_re.reshape(nb, S5_GB * S5_STATE), ab_im.reshape(nb, S5_GB * S5_STATE)], axis=1)
    a = jnp.concatenate([a, jnp.zeros((nb, SUBLANES - 2, S5_GB * S5_STATE), F32)], axis=1)
    return bdiag_in(bb_re), bdiag_in(bb_im), bdiag_out(c_re), bdiag_out(c_im), a


def _shift_down(x, s, row):
    if s == 0:
        return x
    return jnp.where(row >= s, pltpu.roll(x, s, 0), 0.0)


def _shift_up(x, s, row, n):
    if s == 0:
        return x
    return jnp.where(row < n - s, pltpu.roll(x, n - s, 0), 0.0)


def _conv_fwd_call(x, w):
    L, ch = x.shape

    def body(x_ref, w_ref, y_ref):
        xv = x_ref[...]
        row = lax.broadcasted_iota(jnp.int32, xv.shape, 0)
        acc = jnp.zeros_like(xv)
        for j in range(GDN_CONV):
            acc += w_ref[j:j + 1, :] * _shift_down(xv, GDN_CONV - 1 - j, row)
        y_ref[...] = acc

    return pl.pallas_call(
        body, name="gdn_conv_fwd", grid=(ch // LANES,),
        in_specs=[pl.BlockSpec((L, LANES), lambda j: (0, j)), pl.BlockSpec((SUBLANES, LANES), lambda j: (0, j))],
        out_specs=pl.BlockSpec((L, LANES), lambda j: (0, j)), out_shape=SDS((L, ch), F32),
        compiler_params=_params(("parallel",), VMEM_MID),
    )(x, w)


def _conv_bwd_call(x, w, dy):
    L, ch = x.shape

    def body(x_ref, w_ref, dy_ref, dx_ref, dw_ref):
        xv = x_ref[...]
        g = dy_ref[...]
        row = lax.broadcasted_iota(jnp.int32, xv.shape, 0)
        acc = jnp.zeros_like(xv)
        dws = []
        for j in range(GDN_CONV):
            s = GDN_CONV - 1 - j
            acc += w_ref[j:j + 1, :] * _shift_up(g, s, row, L)
            dws.append(jnp.sum(g * _shift_down(xv, s, row), axis=0, keepdims=True))
        dx_ref[...] = acc
        dw_ref[...] = jnp.concatenate(dws + [jnp.zeros((SUBLANES - GDN_CONV, LANES), F32)], axis=0)

    return pl.pallas_call(
        body, name="gdn_conv_bwd", grid=(ch // LANES,),
        in_specs=[pl.BlockSpec((L, LANES), lambda j: (0, j)), pl.BlockSpec((SUBLANES, LANES), lambda j: (0, j)),
                  pl.BlockSpec((L, LANES), lambda j: (0, j))],
        out_specs=[pl.BlockSpec((L, LANES), lambda j: (0, j)), pl.BlockSpec((SUBLANES, LANES), lambda j: (0, j))],
        out_shape=[SDS((L, ch), F32), SDS((SUBLANES, ch), F32)],
        compiler_params=_params(("parallel",), VMEM_MID),
    )(x, w, dy)


@jax.custom_vjp
def gdn_conv(x, w):
    return _conv_fwd_call(x, w)


def _gdn_conv_f(x, w):
    return _conv_fwd_call(x, w), (x, w)


def _gdn_conv_b(res, dy):
    x, w = res
    return tuple(_conv_bwd_call(x, w, dy))


gdn_conv.defvjp(_gdn_conv_f, _gdn_conv_b)


BNN = (((2,), (1,)), ((0,), (0,)))
BNT = (((2,), (2,)), ((0,), (0,)))
BTN = (((1,), (1,)), ((0,), (0,)))
GDN_PREP_BATCH = 8


@jax.custom_vjp
def _known_inverse(a, t):
    return t


def _known_inverse_fwd(a, t):
    return t, t


def _known_inverse_bwd(t, g):
    return -_hdot(_hdot(t, g, _BTN), t, _BNT), jnp.zeros_like(t)


_known_inverse.defvjp(_known_inverse_fwd, _known_inverse_bwd)


def _gdn_prep_math(q, k, v, beta, g, t_saved=None):
    B, C = q.shape[0], q.shape[1]
    ri = lax.broadcasted_iota(jnp.int32, (B, C, C), 1)
    ci = lax.broadcasted_iota(jnp.int32, (B, C, C), 2)
    causal = ri >= ci
    strict = ri > ci
    eye = (ri == ci).astype(F32)
    gb = jnp.broadcast_to(g, (B, C, C))
    g_row = jnp.sum(gb * eye, axis=1, keepdims=True)
    gc_col = jnp.sum(jnp.where(causal, jnp.broadcast_to(g_row, (B, C, C)), 0.0), axis=2, keepdims=True)
    gc_row = jnp.sum(jnp.where(ri <= ci, gb, 0.0), axis=1, keepdims=True)
    decay = jnp.exp(jnp.where(causal, gc_col - gc_row, -jnp.inf))
    kk = _bdot(k, k, BNT)
    a_mat = jnp.where(strict, beta * kk * decay, 0.0)
    t = _unit_lower_inverse(a_mat) if t_saved is None else _known_inverse(a_mat, t_saved)
    e_gc = jnp.exp(gc_col)
    w = _hdot(t, beta * e_gc * k, BNN)
    u = _hdot(t, beta * v, BNN)
    qk = _bdot(q, k, BNT) * decay
    q_dec = q * e_gc
    g_last = gc_col[:, C - 1:C, :]
    k_dec = k * jnp.exp(g_last - gc_col)
    return q_dec, w, u, qk, k_dec, gc_col, t


def _gdn_prep_specs(L):
    C = GDN_CHUNK
    nb = min(GDN_PREP_BATCH, L // C)
    R = nb * C
    ins = [pl.BlockSpec((R, GDN_DK), lambda c, h: (c, h)), pl.BlockSpec((R, GDN_DK), lambda c, h: (c, h)),
           pl.BlockSpec((R, GDN_DV), lambda c, h: (c, h)), pl.BlockSpec((R, LANES), lambda c, h: (c, 0))]
    outs = [pl.BlockSpec((1, R, GDN_DK), lambda c, h: (h, c, 0)), pl.BlockSpec((1, R, GDN_DK), lambda c, h: (h, c, 0)),
            pl.BlockSpec((1, R, GDN_DV), lambda c, h: (h, c, 0)), pl.BlockSpec((1, R, C), lambda c, h: (h, c, 0)),
            pl.BlockSpec((1, R, GDN_DK), lambda c, h: (h, c, 0)), pl.BlockSpec((1, R, 1), lambda c, h: (h, c, 0))]
    t_spec = pl.BlockSpec((1, R, C), lambda c, h: (h, c, 0))
    shapes = [SDS((GDN_HEADS, L, GDN_DK), F32), SDS((GDN_HEADS, L, GDN_DK), F32), SDS((GDN_HEADS, L, GDN_DV), F32),
              SDS((GDN_HEADS, L, C), F32), SDS((GDN_HEADS, L, GDN_DK), F32), SDS((GDN_HEADS, L, 1), F32)]
    return ins, outs, t_spec, shapes, nb


def _chunks(x, nb):
    return x.reshape(nb, x.shape[0] // nb, x.shape[1])


def _head_columns(bg, h):
    lane = lax.broadcasted_iota(jnp.int32, bg.shape, 1)
    beta = jnp.sum(jnp.where(lane == h, bg, 0.0), axis=1, keepdims=True)
    g = jnp.sum(jnp.where(lane == h + GDN_HEADS, bg, 0.0), axis=1, keepdims=True)
    return beta, g


def _gdn_prep_fwd_call(q, k, v, bg):
    L = q.shape[0]
    ins, outs, t_spec, shapes, nb = _gdn_prep_specs(L)

    def body(q_ref, k_ref, v_ref, bg_ref, *o_refs):
        beta, g = _head_columns(bg_ref[...], pl.program_id(1))
        res = _gdn_prep_math(_chunks(q_ref[...], nb), _chunks(k_ref[...], nb), _chunks(v_ref[...], nb),
                             _chunks(beta, nb), _chunks(g, nb))
        for o_ref, val in zip(o_refs, res):
            o_ref[0] = val.reshape(val.shape[0] * val.shape[1], val.shape[2])

    return pl.pallas_call(
        body, name="gdn_prep_fwd", grid=(L // (nb * GDN_CHUNK), GDN_HEADS), in_specs=ins, out_specs=outs + [t_spec],
        out_shape=shapes + [SDS((GDN_HEADS, L, GDN_CHUNK), F32)],
        compiler_params=_params(("parallel", "parallel"), VMEM_MID),
    )(q, k, v, bg)


def _gdn_prep_bwd_call(q, k, v, bg, t, cts):
    L = q.shape[0]
    ins, outs, t_spec, _, nb = _gdn_prep_specs(L)

    def body(q_ref, k_ref, v_ref, bg_ref, t_ref, c0, c1, c2, c3, c4, c5, dq_ref, dk_ref, dv_ref, dbg_ref):
        h = pl.program_id(1)
        beta, g = _head_columns(bg_ref[...], h)
        t_saved = _chunks(t_ref[0], nb)
        _, vjp = jax.vjp(lambda *a: _gdn_prep_math(*a, t_saved=t_saved)[:6], _chunks(q_ref[...], nb), _chunks(k_ref[...], nb),
                         _chunks(v_ref[...], nb), _chunks(beta, nb), _chunks(g, nb))
        dq, dk, dv, db, dg = vjp(tuple(_chunks(c[0], nb) for c in (c0, c1, c2, c3, c4, c5)))
        flat = lambda a: a.reshape(a.shape[0] * a.shape[1], a.shape[2])
        dq_ref[...] = flat(dq)
        dk_ref[...] = flat(dk)
        dv_ref[...] = flat(dv)

        @pl.when(h == 0)
        def _():
            dbg_ref[...] = jnp.zeros_like(dbg_ref)
        lane = lax.broadcasted_iota(jnp.int32, dbg_ref.shape, 1)
        dbg_ref[...] += jnp.where(lane == h, flat(db), 0.0) + jnp.where(lane == h + GDN_HEADS, flat(dg), 0.0)

    return pl.pallas_call(
        body, name="gdn_prep_bwd", grid=(L // (nb * GDN_CHUNK), GDN_HEADS), in_specs=ins + [t_spec] + outs, out_specs=ins,
        out_shape=[SDS(q.shape, F32), SDS(k.shape, F32), SDS(v.shape, F32), SDS(bg.shape, F32)],
        compiler_params=_params(("parallel", "arbitrary"), VMEM_MID),
    )(q, k, v, bg, t, *cts)


@jax.custom_vjp
def gdn_prep(q, k, v, bg):
    return tuple(_gdn_prep_fwd_call(q, k, v, bg)[:6])


def _gdn_prep_f(q, k, v, bg):
    res = _gdn_prep_fwd_call(q, k, v, bg)
    return tuple(res[:6]), (q, k, v, bg, res[6])


def _gdn_prep_b(res, cts):
    return tuple(_gdn_prep_bwd_call(*res, tuple(cts)))


gdn_prep.defvjp(_gdn_prep_f, _gdn_prep_b)


def _gdn_step_math(q_dec, w, u, qk, k_dec, gc, state):
    H, C = q_dec.shape[0], q_dec.shape[1]
    v_new = u - _bdot(w, state, BNN)
    o = _bdot(q_dec, state, BNN) + _bdot(qk, v_new, BNN)
    gl = gc[:, C - 1:C, :]
    new_state = jnp.exp(gl) * state + _bdot(k_dec, v_new, BTN)
    return jnp.concatenate([o[h] for h in range(H)], axis=1), new_state


def _gdn_scan_specs(L, rev):
    C, H = GDN_CHUNK, GDN_HEADS
    nc = L // C
    cc = (lambda c: nc - 1 - c) if rev else (lambda c: c)
    ins = [pl.BlockSpec((H, C, GDN_DK), lambda c: (0, cc(c), 0)), pl.BlockSpec((H, C, GDN_DK), lambda c: (0, cc(c), 0)),
           pl.BlockSpec((H, C, GDN_DV), lambda c: (0, cc(c), 0)), pl.BlockSpec((H, C, C), lambda c: (0, cc(c), 0)),
           pl.BlockSpec((H, C, GDN_DK), lambda c: (0, cc(c), 0)), pl.BlockSpec((H, C, 1), lambda c: (0, cc(c), 0))]
    o_spec = pl.BlockSpec((C, H * GDN_DV), lambda c: (cc(c), 0))
    s_spec = pl.BlockSpec((1, H, GDN_DK, GDN_DV), lambda c: (cc(c), 0, 0, 0))
    return ins, o_spec, s_spec, nc


def _gdn_scan_fwd_call(q_dec, w, u, qk, k_dec, gc):
    L = q_dec.shape[1]
    ins, o_spec, s_spec, nc = _gdn_scan_specs(L, False)

    def body(qd_ref, w_ref, u_ref, qk_ref, kd_ref, gc_ref, o_ref, sin_ref, s_ref):
        c = pl.program_id(0)

        @pl.when(c == 0)
        def _():
            s_ref[...] = jnp.zeros_like(s_ref)
        st = s_ref[...]
        sin_ref[0] = st
        o, ns = _gdn_step_math(qd_ref[...], w_ref[...], u_ref[...], qk_ref[...], kd_ref[...], gc_ref[...], st)
        o_ref[...] = o
        s_ref[...] = ns

    return pl.pallas_call(
        body, name="gdn_scan_fwd", grid=(nc,), in_specs=ins, out_specs=[o_spec, s_spec],
        out_shape=[SDS((L, GDN_HEADS * GDN_DV), F32), SDS((nc, GDN_HEADS, GDN_DK, GDN_DV), F32)],
        scratch_shapes=[pltpu.VMEM((GDN_HEADS, GDN_DK, GDN_DV), F32)],
        compiler_params=_params(("arbitrary",), VMEM_MID),
    )(q_dec, w, u, qk, k_dec, gc)


def _gdn_scan_bwd_call(q_dec, w, u, qk, k_dec, gc, s_in, do):
    L = q_dec.shape[1]
    ins, o_spec, s_spec, nc = _gdn_scan_specs(L, True)

    def body(qd_ref, w_ref, u_ref, qk_ref, kd_ref, gc_ref, sin_ref, do_ref,
             dqd_ref, dw_ref, du_ref, dqk_ref, dkd_ref, dgc_ref, ds_ref):
        c = pl.program_id(0)

        @pl.when(c == 0)
        def _():
            ds_ref[...] = jnp.zeros_like(ds_ref)
        _, vjp = jax.vjp(_gdn_step_math, qd_ref[...], w_ref[...], u_ref[...], qk_ref[...], kd_ref[...], gc_ref[...], sin_ref[0])
        dqd, dw, du, dqk, dkd, dgc, dst = vjp((do_ref[...], ds_ref[...]))
        dqd_ref[...] = dqd
        dw_ref[...] = dw
        du_ref[...] = du
        dqk_ref[...] = dqk
        dkd_ref[...] = dkd
        dgc_ref[...] = dgc
        ds_ref[...] = dst

    return pl.pallas_call(
        body, name="gdn_scan_bwd", grid=(nc,), in_specs=ins + [s_spec, o_spec], out_specs=ins,
        out_shape=[SDS(t.shape, F32) for t in (q_dec, w, u, qk, k_dec, gc)],
        scratch_shapes=[pltpu.VMEM((GDN_HEADS, GDN_DK, GDN_DV), F32)],
        compiler_params=_params(("arbitrary",), VMEM_MID),
    )(q_dec, w, u, qk, k_dec, gc, s_in, do)


@jax.custom_vjp
def gdn_scan(q_dec, w, u, qk, k_dec, gc):
    return _gdn_scan_fwd_call(q_dec, w, u, qk, k_dec, gc)[0]


def _gdn_scan_f(*args):
    o, s_in = _gdn_scan_fwd_call(*args)
    return o, (*args, s_in)


def _gdn_scan_b(res, do):
    return tuple(_gdn_scan_bwd_call(*res, do))


gdn_scan.defvjp(_gdn_scan_f, _gdn_scan_b)


def _silu(x):
    return x * jax.nn.sigmoid(x)


def _gelu_tanh(x):
    return 0.5 * x * (1.0 + jnp.tanh(math.sqrt(2.0 / math.pi) * (x + 0.044715 * (x * x * x))))


def _f_lnmod(x, nw, sc, sh, bsc, bsh):
    xn = x * lax.rsqrt(jnp.mean(x * x, axis=-1, keepdims=True) + NORM_EPS) * nw
    return (xn * (1.0 + (sc + bsc)) + (sh + bsh),)


def _f_s5_act(ys, u, d):
    return (_gelu_tanh(ys + d * u),)


def _f_s5_gate(y2, t, z):
    return (y2 * jax.nn.sigmoid(t) * _silu(z),)


def _f_res(x, y, gate, bgate):
    return (x + (gate + bgate) * y,)


def _heads(x, width, fn):
    return jnp.concatenate([fn(x[:, i * width:(i + 1) * width]) for i in range(x.shape[1] // width)], axis=1)


def _l2n(x):
    return x * lax.rsqrt(jnp.sum(x * x, axis=-1, keepdims=True) + NORM_EPS)


def _f_qnorm(x):
    return (_heads(_silu(x), GDN_DK, _l2n) * (GDN_DK ** -0.5),)


def _f_knorm(x):
    return (_heads(_silu(x), GDN_DK, _l2n),)


def _f_vact(x):
    return (_silu(x),)


def _f_betag(ba, alog, dtb):
    col = lax.broadcasted_iota(jnp.int32, ba.shape, 1)
    t = ba + dtb
    softplus = jnp.maximum(t, 0.0) + jnp.log1p(jnp.exp(-jnp.abs(t)))
    g = -jnp.exp(alog) * softplus
    return (jnp.where(col < GDN_HEADS, jax.nn.sigmoid(ba), jnp.where(col < 2 * GDN_HEADS, g, 0.0)),)


def _f_gdn_post(o, z, nw):
    on = _heads(o, GDN_DV, lambda t: t * lax.rsqrt(jnp.mean(t * t, axis=-1, keepdims=True) + NORM_EPS))
    return (on * nw * _silu(z),)


def _f_loss(x, tgt, fw):
    y = x * lax.rsqrt(jnp.mean(x * x, axis=-1, keepdims=True) + NORM_EPS) * fw
    err = y - tgt
    return (0.5 * jnp.mean(err * err, axis=-1, keepdims=True),)


def _ada_mod_call(c_all, ada_w):
    n = ada_w.shape[2]

    def body(c_ref, w_ref, o_ref):
        ca = _silu(c_ref[...])
        for l in range(ada_w.shape[0]):
            o_ref[l] = _bdot(ca, w_ref[l])

    return pl.pallas_call(body, name="ada_mod", out_shape=SDS((ada_w.shape[0], N_DEV, n), F32),
                          compiler_params=_params(None, VMEM_MID))(c_all, ada_w)


def _ada_grad_call(c_all, dmod):
    nl, _, n = dmod.shape

    def body(c_ref, d_ref, o_ref):
        ca = _silu(c_ref[...])
        for l in range(nl):
            o_ref[l] = _hdot(ca, d_ref[l], TN)

    return pl.pallas_call(body, name="ada_grad", out_shape=SDS((nl, c_all.shape[1], n), F32),
                          compiler_params=_params(None, VMEM_MID))(c_all, dmod)


ADAM_ROWS = 512


def _adamw(g, w, m, v):
    m2 = ADAM_B1 * m + (1.0 - ADAM_B1) * g
    v2 = ADAM_B2 * v + (1.0 - ADAM_B2) * (g * g)
    m_hat = m2 / (1.0 - ADAM_B1 ** ADAM_STEP)
    v_hat = v2 / (1.0 - ADAM_B2 ** ADAM_STEP)
    return g, -ADAM_LR * (m_hat / (jnp.sqrt(v_hat) + ADAM_EPS) + ADAM_WD * w), m2, v2


def _adam_call(gs, w, m, v, name, rows=None):
    n, r, cols = gs.shape
    rows = rows or ADAM_ROWS

    def body(g_ref, w_ref, m_ref, v_ref, go_ref, d_ref, mo_ref, vo_ref):
        g = g_ref[0]
        for s in range(1, n):
            g = g + g_ref[s]
        for o_ref, val in zip((go_ref, d_ref, mo_ref, vo_ref), _adamw(g, w_ref[...], m_ref[...], v_ref[...])):
            o_ref[...] = val

    blk = pl.BlockSpec((rows, cols), lambda i: (i, 0))
    return pl.pallas_call(
        body, name=name, grid=(r // rows,),
        in_specs=[pl.BlockSpec((n, rows, cols), lambda i: (0, i, 0)), blk, blk, blk],
        out_specs=[blk, blk, blk, blk], out_shape=[SDS((r, cols), F32)] * 4,
        compiler_params=_params(("parallel",), VMEM_MID),
    )(gs, w, m, v)


def _sum_call(gs, name, rows):
    n, r, _ = gs.shape

    def body(g_ref, o_ref):
        g = g_ref[0].astype(F32)
        for s in range(1, n):
            g = g + g_ref[s].astype(F32)
        o_ref[...] = g

    return pl.pallas_call(
        body, name=name, grid=(r // rows,),
        in_specs=[pl.BlockSpec((n, rows, LANES), lambda i: (0, i, 0))],
        out_specs=pl.BlockSpec((rows, LANES), lambda i: (i, 0)), out_shape=SDS((r, LANES), F32),
        compiler_params=_params(("parallel",), VMEM_MID),
    )(gs)


def _allgather_call(x_shard, name, in_hbm):
    m_per, n = x_shard.shape

    def body(x_ref, out_ref, send_sems, recv_sems, local_sem):
        x, y, c = lax.axis_index("x"), lax.axis_index("y"), lax.axis_index("c")
        me, sibling = (x, y, c), (x, y, 1 - c)
        chips = [(1 - x, y), (x, 1 - y), (1 - x, 1 - y)]

        def rows(px, py, pc):
            return out_ref.at[pl.ds((4 * px + 2 * py + pc) * m_per, m_per), :]

        def copy(k, block, to, src=None):
            return pltpu.make_async_remote_copy(
                src_ref=rows(*block) if src is None else src, dst_ref=rows(*block),
                send_sem=send_sems.at[k], recv_sem=recv_sems.at[k], device_id=to, device_id_type=pl.DeviceIdType.MESH)

        mine = pltpu.make_async_copy(x_ref, rows(*me), local_sem)
        mine.start()
        first = [copy(0, me, sibling, src=x_ref)]
        first += [copy(1 + j, me, (*chip, c), src=x_ref) for j, chip in enumerate(chips)]
        for cp in first:
            cp.start()
        passed = [copy(4 + j, (*chip, c), sibling) for j, chip in enumerate(chips)]
        for j, chip in enumerate(chips):
            copy(1 + j, (*chip, c), me).wait_recv()
            passed[j].start()
        copy(0, sibling, me).wait_recv()
        for j, chip in enumerate(chips):
            copy(4 + j, (*chip, 1 - c), me).wait_recv()
        for cp in first + passed:
            cp.wait_send()
        mine.wait()

    space = pl.ANY if in_hbm else pltpu.VMEM
    return pl.pallas_call(
        body, name=name, out_shape=SDS((N_DEV * m_per, n), x_shard.dtype),
        in_specs=[pl.BlockSpec(memory_space=space)], out_specs=pl.BlockSpec(memory_space=space),
        scratch_shapes=[pltpu.SemaphoreType.DMA((7,)), pltpu.SemaphoreType.DMA((7,)), pltpu.SemaphoreType.DMA],
        compiler_params=_params(None, None if in_hbm else VMEM_BIG),
    )(x_shard)


def _gather_weights_call(shards, name):
    nw = len(shards)

    def body(*refs):
        x_refs, out_refs = refs[:nw], refs[nw:2 * nw]
        send_sems, recv_sems, local_sems = refs[2 * nw:]
        x, y, c = lax.axis_index("x"), lax.axis_index("y"), lax.axis_index("c")
        me, sibling = (x, y, c), (x, y, 1 - c)
        chips = [(1 - x, y), (x, 1 - y), (1 - x, 1 - y)]

        def slot(w, px, py, pc):
            return out_refs[w].at[4 * px + 2 * py + pc]

        def copy(w, k, block, to, src=None):
            dst = slot(w, *block)
            return pltpu.make_async_remote_copy(
                src_ref=dst if src is None else src, dst_ref=dst, send_sem=send_sems.at[7 * w + k],
                recv_sem=recv_sems.at[7 * w + k], device_id=to, device_id_type=pl.DeviceIdType.MESH)

        mines = [pltpu.make_async_copy(x_refs[w], slot(w, *me), local_sems.at[w]) for w in range(nw)]
        for cp in mines:
            cp.start()
        first = [copy(w, 0, me, sibling, src=x_refs[w]) for w in range(nw)]
        first += [copy(w, 1 + j, me, (*chip, c), src=x_refs[w]) for w in range(nw) for j, chip in enumerate(chips)]
        for cp in first:
            cp.start()
        passed = []
        for w in range(nw):
            for j, chip in enumerate(chips):
                copy(w, 1 + j, (*chip, c), me).wait_recv()
                fwd = copy(w, 4 + j, (*chip, c), sibling)
                fwd.start()
                passed.append(fwd)
        for w in range(nw):
            copy(w, 0, sibling, me).wait_recv()
            for j, chip in enumerate(chips):
                copy(w, 4 + j, (*chip, 1 - c), me).wait_recv()
        for cp in first + passed:
            cp.wait_send()
        for cp in mines:
            cp.wait()

    hbm = pl.BlockSpec(memory_space=pl.ANY)
    return pl.pallas_call(
        body, name=name, out_shape=[SDS((N_DEV,) + s.shape, s.dtype) for s in shards],
        in_specs=[hbm] * nw, out_specs=[hbm] * nw,
        scratch_shapes=[pltpu.SemaphoreType.DMA((7 * nw,)), pltpu.SemaphoreType.DMA((7 * nw,)), pltpu.SemaphoreType.DMA((nw,))],
    )(*shards)


def _pair_exchange_call(grads, name):
    nw = len(grads)

    def body(*refs):
        g_refs, got_refs = refs[:nw], refs[nw:2 * nw]
        send_sems, recv_sems = refs[2 * nw:]
        x, y, c = lax.axis_index("x"), lax.axis_index("y"), lax.axis_index("c")
        copies = []
        for w in range(nw):
            for j in range(4):
                give = pltpu.make_async_remote_copy(
                    src_ref=g_refs[w].at[2 * j + 1 - c], dst_ref=got_refs[w].at[j], send_sem=send_sems.at[4 * w + j],
                    recv_sem=recv_sems.at[4 * w + j], device_id=(x, y, 1 - c), device_id_type=pl.DeviceIdType.MESH)
                give.start()
                copies.append(give)
        for cp in copies:
            cp.wait()

    hbm = pl.BlockSpec(memory_space=pl.ANY)
    return pl.pallas_call(
        body, name=name, out_shape=[SDS((4,) + g.shape[1:], g.dtype) for g in grads], in_specs=[hbm] * nw, out_specs=[hbm] * nw,
        scratch_shapes=[pltpu.SemaphoreType.DMA((4 * nw,)), pltpu.SemaphoreType.DMA((4 * nw,))],
    )(*grads)


def _chip_exchange_call(parts, name):
    nw = len(parts)

    def body(*refs):
        p_refs, out_refs = refs[:nw], refs[nw:2 * nw]
        send_sems, recv_sems = refs[2 * nw:]
        x, y, c = lax.axis_index("x"), lax.axis_index("y"), lax.axis_index("c")
        chips = [(1 - x, y), (x, 1 - y), (1 - x, 1 - y)]
        copies = []
        for w in range(nw):
            for j, (px, py) in enumerate(chips):
                give = pltpu.make_async_remote_copy(
                    src_ref=p_refs[w].at[2 * px + py], dst_ref=out_refs[w].at[j], send_sem=send_sems.at[3 * w + j],
                    recv_sem=recv_sems.at[3 * w + j], device_id=(px, py, c), device_id_type=pl.DeviceIdType.MESH)
                give.start()
                copies.append(give)
        for cp in copies:
            cp.wait()

    hbm = pl.BlockSpec(memory_space=pl.ANY)
    return pl.pallas_call(
        body, name=name, out_shape=[SDS((3,) + p.shape[1:], p.dtype) for p in parts], in_specs=[hbm] * nw, out_specs=[hbm] * nw,
        scratch_shapes=[pltpu.SemaphoreType.DMA((3 * nw,)), pltpu.SemaphoreType.DMA((3 * nw,))],
    )(*parts)


def _pair_sum_call(g, got, core, name):
    _, k, n = got.shape
    tr = _tile(k, 256)

    def body(c_ref, g_ref, got_ref, o_ref):
        o_ref[...] = (g_ref[...] + got_ref[...]).astype(o_ref.dtype)

    spec = pltpu.PrefetchScalarGridSpec(
        num_scalar_prefetch=1, grid=(4, k // tr),
        in_specs=[pl.BlockSpec((1, tr, n), lambda j, i, c: (2 * j + c[0], i, 0)), pl.BlockSpec((1, tr, n), lambda j, i, c: (j, i, 0))],
        out_specs=pl.BlockSpec((1, tr, n), lambda j, i, c: (j, i, 0)))
    return pl.pallas_call(body, name=name, grid_spec=spec, out_shape=SDS(got.shape, BF16),
                          compiler_params=_params(("parallel", "parallel"), VMEM_MID))(core, g, got)


def _adam_own_call(pair, chip, recv, w, m, v, name, rows):
    _, r, cols = recv.shape

    def body(chip_ref, p_ref, g_ref, w_ref, m_ref, v_ref, go_ref, d_ref, mo_ref, vo_ref):
        g = ((p_ref[0].astype(F32) + g_ref[0].astype(F32)) + g_ref[1].astype(F32)) + g_ref[2].astype(F32)
        for o_ref, val in zip((go_ref, d_ref, mo_ref, vo_ref), _adamw(g, w_ref[...], m_ref[...], v_ref[...])):
            o_ref[...] = val

    blk = pl.BlockSpec((rows, cols), lambda i, s: (i, 0))
    spec = pltpu.PrefetchScalarGridSpec(
        num_scalar_prefetch=1, grid=(r // rows,),
        in_specs=[pl.BlockSpec((1, rows, cols), lambda i, s: (s[0], i, 0)), pl.BlockSpec((3, rows, cols), lambda i, s: (0, i, 0)),
                  blk, blk, blk],
        out_specs=[blk, blk, blk, blk])
    return pl.pallas_call(body, name=name, grid_spec=spec, out_shape=[SDS((r, cols), F32)] * 4,
                          compiler_params=_params(("parallel",), VMEM_MID))(chip, pair, recv, w, m, v)


def _join_cols_call(w8, name):
    _, k, n = w8.shape
    tk = _tile(k, 256)

    def body(w_ref, o_ref):
        for s in range(N_DEV):
            o_ref[:, n * s:n * (s + 1)] = w_ref[s]

    return pl.pallas_call(body, name=name, grid=(k // tk,), in_specs=[pl.BlockSpec((N_DEV, tk, n), lambda i: (0, i, 0))],
                          out_specs=pl.BlockSpec((tk, N_DEV * n), lambda i: (i, 0)), out_shape=SDS((k, N_DEV * n), w8.dtype),
                          compiler_params=_params(("parallel",), VMEM_MID))(w8)


def _split_cols_call(g, name):
    k, n8 = g.shape
    n = n8 // N_DEV
    tk = _tile(k, 256)

    def body(g_ref, o_ref):
        for s in range(N_DEV):
            o_ref[s] = g_ref[:, n * s:n * (s + 1)]

    return pl.pallas_call(body, name=name, grid=(k // tk,), in_specs=[pl.BlockSpec((tk, n8), lambda i: (i, 0))],
                          out_specs=pl.BlockSpec((N_DEV, tk, n), lambda i: (0, i, 0)), out_shape=SDS((N_DEV, k, n), g.dtype),
                          compiler_params=_params(("parallel",), VMEM_MID))(g)


def _pack(parts, rows_multiple):
    flat = jnp.concatenate([p.reshape(-1) for p in parts])
    unit = rows_multiple * LANES
    padded = -(-flat.shape[0] // unit) * unit
    flat = jnp.concatenate([flat, jnp.zeros((padded - flat.shape[0],), F32)])
    return flat.reshape(-1, LANES)


def _unpack(buf, shapes):
    flat = buf.reshape(-1)
    out, off = [], 0
    for s in shapes:
        n = math.prod(s)
        out.append(flat[off:off + n].reshape(s))
        off += n
    return out


def _local_loss(diff, const, L):
    (x, mod_raw, norm_w, lam_re, lam_im, log_dt, b_re, b_im, c_re, c_im, s5_d, conv_w, a_log, dt_bias, gdn_nw, final_nw,
     *slots) = diff
    tgt, ada_b, weights = const
    lin = {n: (lambda a, n=n, i=i: make_mm(n)(a, weights[i], slots[i])) for i, n in enumerate(MM_NAMES)}
    tm = 256 if L % 256 == 0 else L
    mods = mod_raw.reshape(2, 3, 1, D_MODEL)
    biases = ada_b.reshape(2, 3, 1, D_MODEL)

    op_ln0 = make_rowwise(_f_lnmod, "ln0", tm, 1, 5, pass_first=True)
    h, x = op_ln0((x,), (norm_w[0:1], mods[0, 1], mods[0, 0], biases[0, 1], biases[0, 0]))
    u, z = make_proj("s5_in")(h, tuple(weights[0:2]), tuple(slots[0:2]))
    blocks = _s5_block_params(lam_re, lam_im, log_dt, b_re, b_im, c_re, c_im)
    ys, u = make_s5_core(min(S5_TL, L))(u, *blocks)
    (y2,) = make_rowwise(_f_s5_act, "s5_act", tm, 2, 1)((ys, u), (s5_d.reshape(1, D_INNER),))
    t = lin["s5_glu"](y2)
    (y4,) = make_rowwise(_f_s5_gate, "s5_gate", tm, 3, 0)((y2, t, z), ())
    o = lin["s5_out"](y4)
    x1 = make_residual("res0", tm)(x, o, mods[0, 2], biases[0, 2])

    op_ln1 = make_rowwise(_f_lnmod, "ln1", tm, 1, 5, pass_first=True)
    h, x1 = op_ln1((x1,), (norm_w[1:2], mods[1, 1], mods[1, 0], biases[1, 1], biases[1, 0]))
    q0, k0, v0, gz, ba = make_proj("gdn_in")(h, tuple(weights[4:9]), tuple(slots[4:9]))
    cw = jnp.concatenate([conv_w, jnp.zeros((SUBLANES - GDN_CONV, GDN_CONV_CH), F32)], axis=0)
    (q,) = make_rowwise(_f_qnorm, "gdn_qn", tm, 1, 0)((gdn_conv(q0, cw[:, :GDN_QK]),), ())
    (k,) = make_rowwise(_f_knorm, "gdn_kn", tm, 1, 0)((gdn_conv(k0, cw[:, GDN_QK:2 * GDN_QK]),), ())
    (v,) = make_rowwise(_f_vact, "gdn_va", tm, 1, 0)((gdn_conv(v0, cw[:, 2 * GDN_QK:]),), ())
    pad = jnp.zeros((LANES - 2 * GDN_HEADS,), F32)
    alog_row = jnp.concatenate([jnp.zeros((GDN_HEADS,), F32), a_log, pad]).reshape(1, LANES)
    dtb_row = jnp.concatenate([jnp.zeros((GDN_HEADS,), F32), dt_bias, pad]).reshape(1, LANES)
    (bg,) = make_rowwise(_f_betag, "gdn_bg", tm, 1, 2)((ba,), (alog_row, dtb_row))
    og = gdn_scan(*gdn_prep(q, k, v, bg))
    nw_row = jnp.tile(gdn_nw, GDN_HEADS).reshape(1, D_INNER)
    (on,) = make_rowwise(_f_gdn_post, "gdn_post", tm, 2, 1)((og, gz), (nw_row,))
    y = lin["gdn_out"](on)
    x2 = make_residual("res1", tm)(x1, y, mods[1, 2], biases[1, 2])

    (lt,) = make_rowwise(_f_loss, "loss", tm, 2, 1)((x2, tgt), (final_nw.reshape(1, D_MODEL),))
    return jnp.sum(lt)


MM_NAMES = ("s5_in_u", "s5_in_z", "s5_glu", "s5_out", "gdn_in_q", "gdn_in_k", "gdn_in_v", "gdn_in_z", "gdn_in_ba", "gdn_out")
VEC_NAMES = ("ada_b", "norm_w", "s5_lambda_re", "s5_lambda_im", "s5_log_dt", "s5_d", "gdn_a_log", "gdn_dt_bias", "final_norm_w")
MAT_NAMES = ("s5_b_re", "s5_b_im", "s5_c_re", "s5_c_im")
BIG_NAMES = ("s5_w_in", "s5_w_glu", "s5_w_out", "gdn_w_in", "gdn_w_out")
WEIGHT_ORDER = ("ada_w", "ada_b", "norm_w", "s5_w_in", "s5_lambda_re", "s5_lambda_im", "s5_log_dt", "s5_b_re", "s5_b_im",
                "s5_c_re", "s5_c_im", "s5_d", "s5_w_glu", "s5_w_out", "gdn_w_in", "gdn_conv_w", "gdn_a_log", "gdn_dt_bias",
                "gdn_norm_w", "gdn_w_out", "final_norm_w")


def _step(x, c, W, M, V, tgt):
    L = x.shape[1]
    ix, iy, ic = lax.axis_index("x"), lax.axis_index("y"), lax.axis_index("c")
    me = 4 * ix + 2 * iy + ic
    n_ada = W["ada_w"].shape[2]
    n_conv = W["gdn_conv_w"].shape[2]
    n_gnw = W["gdn_norm_w"].shape[1]

    g1 = _allgather_call(_pack([c, W["gdn_conv_w"], W["gdn_norm_w"]], SUBLANES), "gather_small_in", False)
    g1 = g1.reshape(N_DEV, -1)
    c_all = g1[:, :D_MODEL]
    conv_w = g1[:, D_MODEL:D_MODEL + GDN_CONV * n_conv].reshape(N_DEV, GDN_CONV, n_conv).transpose(1, 0, 2).reshape(GDN_CONV, -1)
    gdn_nw = g1[:, D_MODEL + GDN_CONV * n_conv:D_MODEL + GDN_CONV * n_conv + n_gnw].reshape(-1)
    mod_part = _ada_mod_call(c_all, W["ada_w"])
    g2 = _allgather_call(_pack([mod_part], SUBLANES), "gather_mod", False).reshape(N_DEV, -1)
    mod_all = g2[:, :2 * N_DEV * n_ada].reshape(N_DEV, 2, N_DEV, n_ada)
    mod_raw = lax.dynamic_index_in_dim(mod_all, me, axis=2, keepdims=False)
    mod_raw = mod_raw.transpose(1, 0, 2).reshape(2, 3 * D_MODEL)

    gathered = _gather_weights_call([W[n][0].astype(BF16) for n in BIG_NAMES], "gather_weights")
    full = dict(zip(BIG_NAMES, gathered))
    w_in5 = _join_cols_call(full["s5_w_in"], "join_s5_w_in")
    w_ing = _join_cols_call(full["gdn_w_in"], "join_gdn_w_in")
    w_ba = jnp.concatenate([w_ing[:, GDN_CONV_CH + D_INNER:], jnp.zeros((D_MODEL, LANES - 2 * GDN_HEADS), BF16)], axis=1)
    weights = (w_in5[:, :D_INNER], w_in5[:, D_INNER:], full["s5_w_glu"].reshape(D_INNER, D_INNER),
               full["s5_w_out"].reshape(D_INNER, D_MODEL),
               w_ing[:, :GDN_QK], w_ing[:, GDN_QK:2 * GDN_QK], w_ing[:, 2 * GDN_QK:GDN_CONV_CH],
               w_ing[:, GDN_CONV_CH:GDN_CONV_CH + D_INNER], w_ba, full["gdn_w_out"].reshape(D_INNER, D_MODEL))
    slots = tuple(jnp.zeros(w.shape, F32) for w in weights)
    diff = (x[0], mod_raw, W["norm_w"], W["s5_lambda_re"][0], W["s5_lambda_im"][0], W["s5_log_dt"][0], W["s5_b_re"][0],
            W["s5_b_im"][0], W["s5_c_re"][0], W["s5_c_im"][0], W["s5_d"][0], conv_w, W["gdn_a_log"][0], W["gdn_dt_bias"][0],
            gdn_nw, W["final_norm_w"], *slots)

    loss_local, grads = jax.value_and_grad(_local_loss)(diff, (tgt[0], W["ada_b"], weights), L)
    (dx, dmod, d_norm_w, d_lre, d_lim, d_logdt, d_bre, d_bim, d_cre, d_cim, d_s5d, d_conv, d_alog, d_dtb, d_gnw, d_fnw,
     d_wu, d_wz, d_wglu, d_wo5, d_wq, d_wk, d_wv, d_wgz, d_wba, d_wog) = grads
    loss = lax.psum(loss_local, MESH_AXES)

    d_in5 = _split_cols_call(jnp.concatenate([d_wu, d_wz], axis=1), "split_s5_w_in")
    d_ing = _split_cols_call(jnp.concatenate([d_wq, d_wk, d_wv, d_wgz, d_wba[:, :2 * GDN_HEADS]], axis=1), "split_gdn_w_in")
    rows = lambda d: d.reshape(N_DEV, d.shape[0] // N_DEV, d.shape[1])
    per_dev = [d_in5, rows(d_wglu), rows(d_wo5), d_ing, rows(d_wog)]
    got = _pair_exchange_call(per_dev, "scatter_grads_pair")
    core = jnp.reshape(ic, (1,)).astype(jnp.int32)
    chip = jnp.reshape(2 * ix + iy, (1,)).astype(jnp.int32)
    pair = [_pair_sum_call(g, r, core, "pair_sum_" + n) for g, r, n in zip(per_dev, got, BIG_NAMES)]
    recv = _chip_exchange_call(pair, "scatter_grads_chips")
    big = [_adam_own_call(p, chip, r, W[n][0], M[n][0], V[n][0], "adam_" + n, _tile(W[n].shape[1], 128))
           for p, r, n in zip(pair, recv, BIG_NAMES)]
    big = [[o[None] for o in outs] for outs in big]

    vec_parts = [dmod, d_norm_w, d_lre, d_lim, d_logdt, d_s5d, d_alog, d_dtb, d_fnw]
    tail_parts = [d_conv, d_gnw]
    mat_parts = [d_bre, d_bim, d_cre, d_cim]
    n_vec = sum(math.prod(p.shape) for p in vec_parts)
    sg_vec, sg_mat = _gather_weights_call(
        [_pack(vec_parts + tail_parts, ADAM_ROWS), _pack(mat_parts, SUBLANES).astype(BF16)], "gather_small_grads")
    tot_vec = _sum_call(sg_vec, "sum_vec_grads", ADAM_ROWS)
    tot_mat = _sum_call(sg_mat, "sum_mat_grads", ADAM_ROWS)
    g_conv, g_gnw = _unpack(tot_vec.reshape(-1)[n_vec:], [d_conv.shape, d_gnw.shape])
    g_conv_mine = lax.dynamic_slice_in_dim(g_conv, me * n_conv, n_conv, axis=1)
    g_gnw_mine = lax.dynamic_slice_in_dim(g_gnw, me * n_gnw, n_gnw, axis=0)
    vec_names = VEC_NAMES + ("gdn_conv_w", "gdn_norm_w")
    vec_g = _pack([tot_vec.reshape(-1)[:n_vec], g_conv_mine, g_gnw_mine], ADAM_ROWS)
    vec = _adam_call(vec_g[None], _pack([W[n] for n in vec_names], ADAM_ROWS), _pack([M[n] for n in vec_names], ADAM_ROWS),
                     _pack([V[n] for n in vec_names], ADAM_ROWS), "adam_vec")
    vec = [_unpack(b, [W[n].shape for n in vec_names]) for b in vec]
    mats = []
    for name, g_mat in zip(MAT_NAMES, _unpack(tot_mat, [p.shape for p in mat_parts])):
        two_d = (-1, W[name].shape[-1])
        outs = _adam_call(g_mat.reshape(two_d)[None], W[name].reshape(two_d), M[name].reshape(two_d), V[name].reshape(two_d),
                          "adam_" + name, rows=1024)
        mats.append([o.reshape(W[name].shape) for o in outs])

    dmod_all = sg_vec[:, :2 * 3 * D_MODEL // LANES].reshape(N_DEV, 2, N_DEV, n_ada // LANES, LANES)
    dmod_mine = lax.dynamic_index_in_dim(dmod_all, me, axis=2, keepdims=False).transpose(1, 0, 2, 3).reshape(2, N_DEV, n_ada)
    g_ada_w = _ada_grad_call(c_all, dmod_mine)
    ada = _adam_call(g_ada_w.reshape(1, -1, LANES), W["ada_w"].reshape(-1, LANES), M["ada_w"].reshape(-1, LANES),
                     V["ada_w"].reshape(-1, LANES), "adam_ada")
    ada = [a.reshape(W["ada_w"].shape) for a in ada]

    res = {}
    for i, n in enumerate(BIG_NAMES):
        res[n] = big[i]
    for i, n in enumerate(vec_names):
        res[n] = [b[i] for b in vec]
    for i, n in enumerate(MAT_NAMES):
        res[n] = mats[i]
    res["ada_w"] = ada
    outs = [loss, dx[None]]
    for j in range(4):
        outs += [res[n][j] for n in WEIGHT_ORDER]
    return tuple(outs)


def kernel(x, c, ada_w, ada_b, norm_w, s5_w_in, s5_lambda_re, s5_lambda_im, s5_log_dt, s5_b_re, s5_b_im, s5_c_re, s5_c_im, s5_d, s5_w_glu, s5_w_out, gdn_w_in, gdn_conv_w, gdn_a_log, gdn_dt_bias, gdn_norm_w, gdn_w_out, final_norm_w, loss_target, m_ada_w, m_ada_b, m_norm_w, m_s5_w_in, m_s5_lambda_re, m_s5_lambda_im, m_s5_log_dt, m_s5_b_re, m_s5_b_im, m_s5_c_re, m_s5_c_im, m_s5_d, m_s5_w_glu, m_s5_w_out, m_gdn_w_in, m_gdn_conv_w, m_gdn_a_log, m_gdn_dt_bias, m_gdn_norm_w, m_gdn_w_out, m_final_norm_w, v_ada_w, v_ada_b, v_norm_w, v_s5_w_in, v_s5_lambda_re, v_s5_lambda_im, v_s5_log_dt, v_s5_b_re, v_s5_b_im, v_s5_c_re, v_s5_c_im, v_s5_d, v_s5_w_glu, v_s5_w_out, v_gdn_w_in, v_gdn_conv_w, v_gdn_a_log, v_gdn_dt_bias, v_gdn_norm_w, v_gdn_w_out, v_final_norm_w):
    W = dict(ada_w=ada_w, ada_b=ada_b, norm_w=norm_w, s5_w_in=s5_w_in, s5_lambda_re=s5_lambda_re, s5_lambda_im=s5_lambda_im,
             s5_log_dt=s5_log_dt, s5_b_re=s5_b_re, s5_b_im=s5_b_im, s5_c_re=s5_c_re, s5_c_im=s5_c_im, s5_d=s5_d,
             s5_w_glu=s5_w_glu, s5_w_out=s5_w_out, gdn_w_in=gdn_w_in, gdn_conv_w=gdn_conv_w, gdn_a_log=gdn_a_log,
             gdn_dt_bias=gdn_dt_bias, gdn_norm_w=gdn_norm_w, gdn_w_out=gdn_w_out, final_norm_w=final_norm_w)
    M = dict(ada_w=m_ada_w, ada_b=m_ada_b, norm_w=m_norm_w, s5_w_in=m_s5_w_in, s5_lambda_re=m_s5_lambda_re,
             s5_lambda_im=m_s5_lambda_im, s5_log_dt=m_s5_log_dt, s5_b_re=m_s5_b_re, s5_b_im=m_s5_b_im, s5_c_re=m_s5_c_re,
             s5_c_im=m_s5_c_im, s5_d=m_s5_d, s5_w_glu=m_s5_w_glu, s5_w_out=m_s5_w_out, gdn_w_in=m_gdn_w_in,
             gdn_conv_w=m_gdn_conv_w, gdn_a_log=m_gdn_a_log, gdn_dt_bias=m_gdn_dt_bias, gdn_norm_w=m_gdn_norm_w,
             gdn_w_out=m_gdn_w_out, final_norm_w=m_final_norm_w)
    V = dict(ada_w=v_ada_w, ada_b=v_ada_b, norm_w=v_norm_w, s5_w_in=v_s5_w_in, s5_lambda_re=v_s5_lambda_re,
             s5_lambda_im=v_s5_lambda_im, s5_log_dt=v_s5_log_dt, s5_b_re=v_s5_b_re, s5_b_im=v_s5_b_im, s5_c_re=v_s5_c_re,
             s5_c_im=v_s5_c_im, s5_d=v_s5_d, s5_w_glu=v_s5_w_glu, s5_w_out=v_s5_w_out, gdn_w_in=v_gdn_w_in,
             gdn_conv_w=v_gdn_conv_w, gdn_a_log=v_gdn_a_log, gdn_dt_bias=v_gdn_dt_bias, gdn_norm_w=v_gdn_norm_w,
             gdn_w_out=v_gdn_w_out, final_norm_w=v_final_norm_w)
    return _step(x, c, W, M, V, loss_target)
```

```python
import functools
import math

import jax
import jax.numpy as jnp
from jax import lax
from jax.experimental import pallas as pl
from jax.experimental.pallas import tpu as pltpu

F32 = jnp.float32
BF16 = jnp.bfloat16
SDS = jax.ShapeDtypeStruct

D_MODEL = 1024
D_INNER = 2048
NORM_EPS = 1e-6
S5_GROUP = 16
S5_GROUPS = 128
S5_STATE = 64
GDN_HEADS = 8
GDN_DK = 128
GDN_DV = 256
GDN_CONV = 4
GDN_CHUNK = 64
GDN_QK = 1024
GDN_CONV_CH = 4096
GDN_PROJ = 6160
ADAM_LR = 0.001
ADAM_B1 = 0.9
ADAM_B2 = 0.999
ADAM_EPS = 1e-08
ADAM_WD = 0.01
ADAM_STEP = 10

N_DEV = 8
LANES = 128
SUBLANES = 8
VMEM_BIG = 56 << 20
VMEM_MID = 40 << 20
S5_GB = 8
S5_TL = 1024
MESH_AXES = ("x", "y", "c")


def _params(sem, vmem=None):
    return pltpu.CompilerParams(dimension_semantics=sem, vmem_limit_bytes=vmem)


def _bdot(a, b, dims=(((1,), (0,)), ((), ()))):
    return lax.dot_general(a.astype(BF16), b.astype(BF16), dims, preferred_element_type=F32)


def _hdot(a, b, dims=(((1,), (0,)), ((), ()))):
    return lax.dot_general(a, b, dims, preferred_element_type=F32, precision=lax.Precision.HIGHEST)


_BNN = (((2,), (1,)), ((0,), (0,)))
_BNT = (((2,), (2,)), ((0,), (0,)))
_BTN = (((1,), (1,)), ((0,), (0,)))


@jax.custom_vjp
def _unit_lower_inverse(a):
    c = a.shape[-1]
    ri = lax.broadcasted_iota(jnp.int32, a.shape, 1)
    ci = lax.broadcasted_iota(jnp.int32, a.shape, 2)
    n = -a
    t = (ri == ci).astype(F32) + n
    for _ in range(int(math.log2(c)) - 1):
        n = _hdot(n, n, _BNN)
        t = t + _hdot(t, n, _BNN)
    return t


def _unit_lower_inverse_fwd(a):
    t = _unit_lower_inverse(a)
    return t, t


def _unit_lower_inverse_bwd(t, g):
    return (-_hdot(_hdot(t, g, _BTN), t, _BNT),)


_unit_lower_inverse.defvjp(_unit_lower_inverse_fwd, _unit_lower_inverse_bwd)


NN = (((1,), (0,)), ((), ()))
NT = (((1,), (1,)), ((), ()))
TN = (((0,), (0,)), ((), ()))


def _tile(n, pref):
    for t in (pref, 512, 256, 128):
        if t <= n and n % t == 0:
            return t
    return n


def _matmul(a, b, mode, name):
    if mode == "nn":
        (m, k), (_, n) = a.shape, b.shape
    elif mode == "nt":
        (m, k), (n, _) = a.shape, b.shape
    else:
        (k, m), (_, n) = a.shape, b.shape
    tm, tn, tk = _tile(m, 512), _tile(n, 512), (k if k <= 2048 else _tile(k, 512))
    if mode == "tn":
        tm, tn = _tile(m, 1024), _tile(n, 1024)
    nk = k // tk
    dims = {"nn": NN, "nt": NT, "tn": TN}[mode]

    def body(a_ref, b_ref, o_ref, acc_ref):
        kk = pl.program_id(2)

        @pl.when(kk == 0)
        def _():
            acc_ref[...] = jnp.zeros_like(acc_ref)
        acc_ref[...] += _bdot(a_ref[...], b_ref[...], dims)

        @pl.when(kk == nk - 1)
        def _():
            o_ref[...] = acc_ref[...]

    a_spec = pl.BlockSpec((tk, tm), lambda i, j, q: (q, i)) if mode == "tn" else pl.BlockSpec((tm, tk), lambda i, j, q: (i, q))
    b_spec = pl.BlockSpec((tn, tk), lambda i, j, q: (j, q)) if mode == "nt" else pl.BlockSpec((tk, tn), lambda i, j, q: (q, j))
    return pl.pallas_call(
        body, name=name, grid=(m // tm, n // tn, nk),
        in_specs=[a_spec, b_spec], out_specs=pl.BlockSpec((tm, tn), lambda i, j, q: (i, j)),
        out_shape=SDS((m, n), F32), scratch_shapes=[pltpu.VMEM((tm, tn), F32)],
        compiler_params=_params(("parallel", "parallel", "arbitrary"), VMEM_MID),
    )(a, b)


def make_mm(name):
    @jax.custom_vjp
    def mm(a, w, grad_slot):
        return _matmul(a, w, "nn", name + "_fwd")

    def fwd(a, w, grad_slot):
        return _matmul(a, w, "nn", name + "_fwd"), (a, w)

    def bwd(res, g):
        a, w = res
        return _matmul(g, w, "nt", name + "_dx"), jnp.zeros_like(w), _matmul(a, g, "tn", name + "_dw")

    mm.defvjp(fwd, bwd)
    return mm


PROJ_ROWS = 256


def _proj_fwd_call(a, ws, name):
    m, k = a.shape
    tm = _tile(m, PROJ_ROWS)
    nw = len(ws)

    def body(*refs):
        ab = refs[0][...].astype(BF16)
        for w_ref, o_ref in zip(refs[1:1 + nw], refs[1 + nw:]):
            o_ref[...] = lax.dot_general(ab, w_ref[...], NN, preferred_element_type=F32)

    return pl.pallas_call(
        body, name=name, grid=(m // tm,),
        in_specs=[pl.BlockSpec((tm, k), lambda i: (i, 0))] + [pl.BlockSpec(w.shape, lambda i: (0, 0)) for w in ws],
        out_specs=[pl.BlockSpec((tm, w.shape[1]), lambda i: (i, 0)) for w in ws],
        out_shape=[SDS((m, w.shape[1]), F32) for w in ws],
        compiler_params=_params(("parallel",), VMEM_BIG),
    )(a, *ws)


def _proj_dx_call(gs, ws, name):
    m = gs[0].shape[0]
    k = ws[0].shape[0]
    tm = _tile(m, PROJ_ROWS)
    nw = len(ws)

    def body(*refs):
        acc = None
        for g_ref, w_ref in zip(refs[:nw], refs[nw:2 * nw]):
            part = _bdot(g_ref[...], w_ref[...], NT)
            acc = part if acc is None else acc + part
        refs[2 * nw][...] = acc

    return pl.pallas_call(
        body, name=name, grid=(m // tm,),
        in_specs=[pl.BlockSpec((tm, g.shape[1]), lambda i: (i, 0)) for g in gs] + [pl.BlockSpec(w.shape, lambda i: (0, 0)) for w in ws],
        out_specs=pl.BlockSpec((tm, k), lambda i: (i, 0)), out_shape=SDS((m, k), F32),
        compiler_params=_params(("parallel",), VMEM_BIG),
    )(*gs, *ws)


def make_proj(name):
    @jax.custom_vjp
    def proj(a, ws, grad_slots):
        return tuple(_proj_fwd_call(a, ws, name + "_fwd"))

    def fwd(a, ws, grad_slots):
        return tuple(_proj_fwd_call(a, ws, name + "_fwd")), (a, ws)

    def bwd(res, gs):
        a, ws = res
        dws = tuple(_matmul(a, g, "tn", "%s_dw%d" % (name, i)) for i, g in enumerate(gs))
        return _proj_dx_call(tuple(gs), ws, name + "_dx"), tuple(jnp.zeros_like(w) for w in ws), dws

    proj.defvjp(fwd, bwd)
    return proj


def make_rowwise(f, name, tm, n_rows, n_params, vmem=VMEM_MID, pass_first=False):
    def specs_of(arrs, blocked):
        if blocked:
            return [pl.BlockSpec((tm, a.shape[1]), lambda i: (i, 0)) for a in arrs]
        return [pl.BlockSpec(a.shape, lambda i: (0, 0)) for a in arrs]

    def out_structs(rows, params):
        blk = [SDS((tm, r.shape[1]), r.dtype) for r in rows] + [SDS(p.shape, p.dtype) for p in params]
        return jax.eval_shape(f, *blk)

    def run_fwd(rows, params):
        L = rows[0].shape[0]
        outs = out_structs(rows, params)

        def body(*refs):
            ins = [r[...] for r in refs[:n_rows + n_params]]
            res = f(*ins)
            for o_ref, val in zip(refs[n_rows + n_params:], res):
                o_ref[...] = val

        return pl.pallas_call(
            body, name=name + "_fwd", grid=(L // tm,),
            in_specs=specs_of(rows, True) + specs_of(params, False),
            out_specs=[pl.BlockSpec((tm, o.shape[1]), lambda i: (i, 0)) for o in outs],
            out_shape=[SDS((L, o.shape[1]), o.dtype) for o in outs],
            compiler_params=_params(("parallel",), vmem),
        )(*rows, *params)

    def run_bwd(rows, params, gs):
        L = rows[0].shape[0]
        n_g = len(gs)

        def body(*refs):
            i = pl.program_id(0)
            ins = [r[...] for r in refs[:n_rows + n_params]]
            cts = tuple(r[...] for r in refs[n_rows + n_params:n_rows + n_params + n_g])
            outs = refs[n_rows + n_params + n_g:]
            _, vjp = jax.vjp(f, *ins)
            grads = vjp(cts[:-1] if pass_first else cts)
            if pass_first:
                grads = (grads[0] + cts[-1],) + tuple(grads[1:])
            for o_ref, val in zip(outs[:n_rows], grads[:n_rows]):
                o_ref[...] = val

            if n_params:
                @pl.when(i == 0)
                def _():
                    for o_ref in outs[n_rows:]:
                        o_ref[...] = jnp.zeros_like(o_ref)
                for o_ref, val in zip(outs[n_rows:], grads[n_rows:]):
                    o_ref[...] += val

        res = pl.pallas_call(
            body, name=name + "_bwd", grid=(L // tm,),
            in_specs=specs_of(rows, True) + specs_of(params, False) + specs_of(gs, True),
            out_specs=specs_of(rows, True) + specs_of(params, False),
            out_shape=[SDS(r.shape, r.dtype) for r in rows] + [SDS(p.shape, p.dtype) for p in params],
            compiler_params=_params(("arbitrary",), vmem),
        )(*rows, *params, *gs)
        return tuple(res[:n_rows]), tuple(res[n_rows:])

    def outputs(rows, params):
        outs = tuple(run_fwd(rows, params))
        return outs + (rows[0],) if pass_first else outs

    @jax.custom_vjp
    def op(rows, params):
        return outputs(rows, params)

    def fwd(rows, params):
        return outputs(rows, params), (rows, params)

    def bwd(res, gs):
        rows, params = res
        return run_bwd(rows, params, tuple(gs))

    op.defvjp(fwd, bwd)
    op.run_fwd, op.run_bwd = run_fwd, run_bwd
    return op


def make_residual(name, tm):
    full = make_rowwise(_f_res, name, tm, 2, 2)
    branch = make_rowwise(lambda y, gate, bgate: ((gate + bgate) * y,), name + "_branch", tm, 1, 2)

    @jax.custom_vjp
    def op(x, y, gate, bgate):
        return full.run_fwd((x, y), (gate, bgate))[0]

    def fwd(x, y, gate, bgate):
        return full.run_fwd((x, y), (gate, bgate))[0], (y, gate, bgate)

    def bwd(res, g):
        y, gate, bgate = res
        (dy,), (dgate, dbgate) = branch.run_bwd((y,), (gate, bgate), (g,))
        return g, dy, dgate, dbgate

    op.defvjp(fwd, bwd)
    return op


def _s5_scan_rows(xr_ref, xi_ref, ar, ai, x0r, x0i, tl, reverse=False):
    n = xr_ref.shape[1]
    T = SUBLANES
    row = lax.broadcasted_iota(jnp.int32, (T, n), 0)
    pr, pi = [ar], [ai]
    for _ in range(T - 1):
        pr, pi = pr + [pr[-1] * ar - pi[-1] * ai], pi + [pr[-1] * ai + pi[-1] * ar]
    levels = []
    for d in (1, 2, 4):
        mask = (row < T - d) if reverse else (row >= d)
        levels.append((T - d if reverse else d, jnp.where(mask, pr[d - 1], 0.0), jnp.where(mask, pi[d - 1], 0.0)))
    cr = jnp.zeros((T, n), F32)
    ci = jnp.zeros((T, n), F32)
    for r in range(T):
        k = (T - r) if reverse else (r + 1)
        cr = jnp.where(row == r, pr[k - 1], cr)
        ci = jnp.where(row == r, pi[k - 1], ci)
    nt = tl // T
    last = 0 if reverse else T - 1

    def step(t, carry):
        sr, si = carry
        base = pl.multiple_of((nt - 1 - t if reverse else t) * T, T)
        br = xr_ref[pl.ds(base, T), :]
        bi = xi_ref[pl.ds(base, T), :]
        for shift, mr, mi in levels:
            qr = pltpu.roll(br, shift, 0)
            qi = pltpu.roll(bi, shift, 0)
            br, bi = br + (mr * qr - mi * qi), bi + (mr * qi + mi * qr)
        xr = br + (cr * sr - ci * si)
        xi = bi + (cr * si + ci * sr)
        xr_ref[pl.ds(base, T), :] = xr
        xi_ref[pl.ds(base, T), :] = xi
        return xr[last:last + 1, :], xi[last:last + 1, :]
    return lax.fori_loop(0, nt, step, (x0r, x0i))


def _s5_fwd_call(u, bre, bim, cre, cim, a, tl):
    L, e = u.shape
    nb = e // LANES
    ns = bre.shape[2]
    nc = L // tl

    def body(u_ref, bre_ref, bim_ref, cre_ref, cim_ref, a_ref, ys_ref, xb_ref, sr_ref, si_ref, xr_ref, xi_ref, carry_ref):
        c = pl.program_id(1)

        @pl.when(c == 0)
        def _():
            carry_ref[...] = jnp.zeros_like(carry_ref)
        xb_ref[0, 0] = carry_ref[...]
        ub = u_ref[...]
        xr_ref[...] = _bdot(ub, bre_ref[0])
        xi_ref[...] = _bdot(ub, bim_ref[0])
        ar = a_ref[0, 0:1, :]
        ai = a_ref[0, 1:2, :]
        xr, xi = _s5_scan_rows(xr_ref, xi_ref, ar, ai, carry_ref[0:1, :], carry_ref[1:2, :], tl)
        carry_ref[0:1, :] = xr
        carry_ref[1:2, :] = xi
        sr = xr_ref[...].astype(BF16)
        si = xi_ref[...].astype(BF16)
        sr_ref[...] = sr
        si_ref[...] = si
        ys_ref[...] = _bdot(sr, cre_ref[0]) - _bdot(si, cim_ref[0])

    return pl.pallas_call(
        body, name="s5_core_fwd", grid=(nb, nc),
        in_specs=[pl.BlockSpec((tl, LANES), lambda j, c: (c, j)),
                  pl.BlockSpec((1, LANES, ns), lambda j, c: (j, 0, 0)), pl.BlockSpec((1, LANES, ns), lambda j, c: (j, 0, 0)),
                  pl.BlockSpec((1, ns, LANES), lambda j, c: (j, 0, 0)), pl.BlockSpec((1, ns, LANES), lambda j, c: (j, 0, 0)),
                  pl.BlockSpec((1, SUBLANES, ns), lambda j, c: (j, 0, 0))],
        out_specs=[pl.BlockSpec((tl, LANES), lambda j, c: (c, j)),
                   pl.BlockSpec((1, 1, SUBLANES, ns), lambda j, c: (j, c, 0, 0)),
                   pl.BlockSpec((tl, ns), lambda j, c: (c, j)), pl.BlockSpec((tl, ns), lambda j, c: (c, j))],
        out_shape=[SDS((L, e), F32), SDS((nb, nc, SUBLANES, ns), F32), SDS((L, nb * ns), BF16), SDS((L, nb * ns), BF16)],
        scratch_shapes=[pltpu.VMEM((tl, ns), F32), pltpu.VMEM((tl, ns), F32), pltpu.VMEM((SUBLANES, ns), F32)],
        compiler_params=_params(("arbitrary", "arbitrary"), VMEM_MID),
    )(u, bre, bim, cre, cim, a)


def _s5_bwd_call(u, dys, du_other, bre, bim, cre, cim, a, xb, sr, si, tl):
    L, e = u.shape
    nb = e // LANES
    ns = bre.shape[2]
    nc = L // tl

    def body(u_ref, dys_ref, duo_ref, bre_ref, bim_ref, cre_ref, cim_ref, a_ref, xb_ref, sr_ref, si_ref,
             du_ref, dbre_ref, dbim_ref, dcre_ref, dcim_ref, da_ref,
             gr_ref, gi_ref, gcarry_ref):
        c = pl.program_id(1)

        @pl.when(c == 0)
        def _():
            gcarry_ref[...] = jnp.zeros_like(gcarry_ref)
            dbre_ref[...] = jnp.zeros_like(dbre_ref)
            dbim_ref[...] = jnp.zeros_like(dbim_ref)
            dcre_ref[...] = jnp.zeros_like(dcre_ref)
            dcim_ref[...] = jnp.zeros_like(dcim_ref)
            da_ref[...] = jnp.zeros_like(da_ref)

        ub = u_ref[...]
        dy = dys_ref[...]
        ar = a_ref[0, 0:1, :]
        ai = a_ref[0, 1:2, :]
        x0r = xb_ref[0, 0, 0:1, :]
        x0i = xb_ref[0, 0, 1:2, :]
        dcre_ref[0] += _bdot(sr_ref[...], dy, TN)
        dcim_ref[0] -= _bdot(si_ref[...], dy, TN)
        gr_ref[...] = _bdot(dy, cre_ref[0], NT)
        gi_ref[...] = -_bdot(dy, cim_ref[0], NT)

        g0r, g0i = _s5_scan_rows(gr_ref, gi_ref, ar, -ai, gcarry_ref[0:1, :], gcarry_ref[1:2, :], tl, reverse=True)
        gcarry_ref[0:1, :] = g0r
        gcarry_ref[1:2, :] = g0i
        row = lax.broadcasted_iota(jnp.int32, (tl, ns), 0)
        gr = gr_ref[...]
        gi = gi_ref[...]
        xpr = jnp.where(row == 0, x0r, pltpu.roll(sr_ref[...].astype(F32), 1, 0))
        xpi = jnp.where(row == 0, x0i, pltpu.roll(si_ref[...].astype(F32), 1, 0))
        da_ref[0, 0:1, :] += jnp.sum(gr * xpr + gi * xpi, axis=0, keepdims=True)
        da_ref[0, 1:2, :] += jnp.sum(gi * xpr - gr * xpi, axis=0, keepdims=True)
        du_ref[...] = (_bdot(gr, bre_ref[0], NT) + _bdot(gi, bim_ref[0], NT)) + duo_ref[...]
        dbre_ref[0] += _bdot(ub, gr, TN)
        dbim_ref[0] += _bdot(ub, gi, TN)

    rev = lambda c: nc - 1 - c
    return pl.pallas_call(
        body, name="s5_core_bwd", grid=(nb, nc),
        in_specs=[pl.BlockSpec((tl, LANES), lambda j, c: (rev(c), j)), pl.BlockSpec((tl, LANES), lambda j, c: (rev(c), j)),
                  pl.BlockSpec((tl, LANES), lambda j, c: (rev(c), j)),
                  pl.BlockSpec((1, LANES, ns), lambda j, c: (j, 0, 0)), pl.BlockSpec((1, LANES, ns), lambda j, c: (j, 0, 0)),
                  pl.BlockSpec((1, ns, LANES), lambda j, c: (j, 0, 0)), pl.BlockSpec((1, ns, LANES), lambda j, c: (j, 0, 0)),
                  pl.BlockSpec((1, SUBLANES, ns), lambda j, c: (j, 0, 0)),
                  pl.BlockSpec((1, 1, SUBLANES, ns), lambda j, c: (j, rev(c), 0, 0)),
                  pl.BlockSpec((tl, ns), lambda j, c: (rev(c), j)), pl.BlockSpec((tl, ns), lambda j, c: (rev(c), j))],
        out_specs=[pl.BlockSpec((tl, LANES), lambda j, c: (rev(c), j)),
                   pl.BlockSpec((1, LANES, ns), lambda j, c: (j, 0, 0)), pl.BlockSpec((1, LANES, ns), lambda j, c: (j, 0, 0)),
                   pl.BlockSpec((1, ns, LANES), lambda j, c: (j, 0, 0)), pl.BlockSpec((1, ns, LANES), lambda j, c: (j, 0, 0)),
                   pl.BlockSpec((1, SUBLANES, ns), lambda j, c: (j, 0, 0))],
        out_shape=[SDS((L, e), F32), SDS(bre.shape, F32), SDS(bim.shape, F32), SDS(cre.shape, F32), SDS(cim.shape, F32),
                   SDS(a.shape, F32)],
        scratch_shapes=[pltpu.VMEM((tl, ns), F32) for _ in range(2)] + [pltpu.VMEM((SUBLANES, ns), F32)],
        compiler_params=_params(("arbitrary", "arbitrary"), VMEM_MID),
    )(u, dys, du_other, bre, bim, cre, cim, a, xb, sr, si)


def make_s5_core(tl):
    @jax.custom_vjp
    def s5_core(u, bre, bim, cre, cim, a):
        return _s5_fwd_call(u, bre, bim, cre, cim, a, tl)[0], u

    def fwd(u, bre, bim, cre, cim, a):
        ys, xb, sr, si = _s5_fwd_call(u, bre, bim, cre, cim, a, tl)
        return (ys, u), (u, bre, bim, cre, cim, a, xb, sr, si)

    def bwd(res, cts):
        u, bre, bim, cre, cim, a, xb, sr, si = res
        dys, du_other = cts
        return tuple(_s5_bwd_call(u, dys, du_other, bre, bim, cre, cim, a, xb, sr, si, tl))

    s5_core.defvjp(fwd, bwd)
    return s5_core


def _s5_block_params(lam_re, lam_im, log_dt, b_re, b_im, c_re, c_im):
    dt = jnp.exp(log_dt)[:, None]
    mag = jnp.exp(lam_re * dt)
    ab_re = mag * jnp.cos(lam_im * dt)
    ab_im = mag * jnp.sin(lam_im * dt)
    den = lam_re * lam_re + lam_im * lam_im
    nr = ab_re - 1.0
    ni = ab_im
    q_re = (nr * lam_re + ni * lam_im) / den
    q_im = (ni * lam_re - nr * lam_im) / den
    bb_re = q_re[..., None] * b_re - q_im[..., None] * b_im
    bb_im = q_re[..., None] * b_im + q_im[..., None] * b_re
    nb = S5_GROUPS // S5_GB
    eye = jnp.eye(S5_GB, dtype=F32)

    def bdiag_in(bb):
        t = bb.reshape(nb, S5_GB, S5_STATE, S5_GROUP)
        t = jnp.einsum("jgpm,gh->jgmhp", t, eye)
        return t.reshape(nb, S5_GB * S5_GROUP, S5_GB * S5_STATE)

    def bdiag_out(cc):
        t = cc.reshape(nb, S5_GB, S5_GROUP, S5_STATE)
        t = jnp.einsum("jgmp,gh->jgphm", t, eye)
        return t.reshape(nb, S5_GB * S5_STATE, S5_GB * S5_GROUP)

    a = jnp.stack([ab_re.reshape(nb, S5_GB * S5_STATE), ab_im.reshape(nb, S5_GB * S5_STATE)], axis=1)
    a = jnp.concatenate([a, jnp.zeros((nb, SUBLANES - 2, S5_GB * S5_STATE), F32)], axis=1)
    return bdiag_in(bb_re), bdiag_in(bb_im), bdiag_out(c_re), bdiag_out(c_im), a


def _shift_down(x, s, row):
    if s == 0:
        return x
    return jnp.where(row >= s, pltpu.roll(x, s, 0), 0.0)


def _shift_up(x, s, row, n):
    if s == 0:
        return x
    return jnp.where(row < n - s, pltpu.roll(x, n - s, 0), 0.0)


def _conv_fwd_call(x, w):
    L, ch = x.shape

    def body(x_ref, w_ref, y_ref):
        xv = x_ref[...]
        row = lax.broadcasted_iota(jnp.int32, xv.shape, 0)
        acc = jnp.zeros_like(xv)
        for j in range(GDN_CONV):
            acc += w_ref[j:j + 1, :] * _shift_down(xv, GDN_CONV - 1 - j, row)
        y_ref[...] = acc

    return pl.pallas_call(
        body, name="gdn_conv_fwd", grid=(ch // LANES,),
        in_specs=[pl.BlockSpec((L, LANES), lambda j: (0, j)), pl.BlockSpec((SUBLANES, LANES), lambda j: (0, j))],
        out_specs=pl.BlockSpec((L, LANES), lambda j: (0, j)), out_shape=SDS((L, ch), F32),
        compiler_params=_params(("parallel",), VMEM_MID),
    )(x, w)


def _conv_bwd_call(x, w, dy):
    L, ch = x.shape

    def body(x_ref, w_ref, dy_ref, dx_ref, dw_ref):
        xv = x_ref[...]
        g = dy_ref[...]
        row = lax.broadcasted_iota(jnp.int32, xv.shape, 0)
        acc = jnp.zeros_like(xv)
        dws = []
        for j in range(GDN_CONV):
            s = GDN_CONV - 1 - j
            acc += w_ref[j:j + 1, :] * _shift_up(g, s, row, L)
            dws.append(jnp.sum(g * _shift_down(xv, s, row), axis=0, keepdims=True))
        dx_ref[...] = acc
        dw_ref[...] = jnp.concatenate(dws + [jnp.zeros((SUBLANES - GDN_CONV, LANES), F32)], axis=0)

    return pl.pallas_call(
        body, name="gdn_conv_bwd", grid=(ch // LANES,),
        in_specs=[pl.BlockSpec((L, LANES), lambda j: (0, j)), pl.BlockSpec((SUBLANES, LANES), lambda j: (0, j)),
                  pl.BlockSpec((L, LANES), lambda j: (0, j))],
        out_specs=[pl.BlockSpec((L, LANES), lambda j: (0, j)), pl.BlockSpec((SUBLANES, LANES), lambda j: (0, j))],
        out_shape=[SDS((L, ch), F32), SDS((SUBLANES, ch), F32)],
        compiler_params=_params(("parallel",), VMEM_MID),
    )(x, w, dy)


@jax.custom_vjp
def gdn_conv(x, w):
    return _conv_fwd_call(x, w)


def _gdn_conv_f(x, w):
    return _conv_fwd_call(x, w), (x, w)


def _gdn_conv_b(res, dy):
    x, w = res
    return tuple(_conv_bwd_call(x, w, dy))


gdn_conv.defvjp(_gdn_conv_f, _gdn_conv_b)


BNN = (((2,), (1,)), ((0,), (0,)))
BNT = (((2,), (2,)), ((0,), (0,)))
BTN = (((1,), (1,)), ((0,), (0,)))
GDN_PREP_BATCH = 8


@jax.custom_vjp
def _known_inverse(a, t):
    return t


def _known_inverse_fwd(a, t):
    return t, t


def _known_inverse_bwd(t, g):
    return -_hdot(_hdot(t, g, _BTN), t, _BNT), jnp.zeros_like(t)


_known_inverse.defvjp(_known_inverse_fwd, _known_inverse_bwd)


def _gdn_prep_math(q, k, v, beta, g, t_saved=None):
    B, C = q.shape[0], q.shape[1]
    ri = lax.broadcasted_iota(jnp.int32, (B, C, C), 1)
    ci = lax.broadcasted_iota(jnp.int32, (B, C, C), 2)
    causal = ri >= ci
    strict = ri > ci
    eye = (ri == ci).astype(F32)
    gb = jnp.broadcast_to(g, (B, C, C))
    g_row = jnp.sum(gb * eye, axis=1, keepdims=True)
    gc_col = jnp.sum(jnp.where(causal, jnp.broadcast_to(g_row, (B, C, C)), 0.0), axis=2, keepdims=True)
    gc_row = jnp.sum(jnp.where(ri <= ci, gb, 0.0), axis=1, keepdims=True)
    decay = jnp.exp(jnp.where(causal, gc_col - gc_row, -jnp.inf))
    kk = _bdot(k, k, BNT)
    a_mat = jnp.where(strict, beta * kk * decay, 0.0)
    t = _unit_lower_inverse(a_mat) if t_saved is None else _known_inverse(a_mat, t_saved)
    e_gc = jnp.exp(gc_col)
    w = _hdot(t, beta * e_gc * k, BNN)
    u = _hdot(t, beta * v, BNN)
    qk = _bdot(q, k, BNT) * decay
    q_dec = q * e_gc
    g_last = gc_col[:, C - 1:C, :]
    k_dec = k * jnp.exp(g_last - gc_col)
    return q_dec, w, u, qk, k_dec, gc_col, t


def _gdn_prep_specs(L):
    C = GDN_CHUNK
    nb = min(GDN_PREP_BATCH, L // C)
    R = nb * C
    ins = [pl.BlockSpec((R, GDN_DK), lambda c, h: (c, h)), pl.BlockSpec((R, GDN_DK), lambda c, h: (c, h)),
           pl.BlockSpec((R, GDN_DV), lambda c, h: (c, h)), pl.BlockSpec((R, LANES), lambda c, h: (c, 0))]
    outs = [pl.BlockSpec((1, R, GDN_DK), lambda c, h: (h, c, 0)), pl.BlockSpec((1, R, GDN_DK), lambda c, h: (h, c, 0)),
            pl.BlockSpec((1, R, GDN_DV), lambda c, h: (h, c, 0)), pl.BlockSpec((1, R, C), lambda c, h: (h, c, 0)),
            pl.BlockSpec((1, R, GDN_DK), lambda c, h: (h, c, 0)), pl.BlockSpec((1, R, 1), lambda c, h: (h, c, 0))]
    t_spec = pl.BlockSpec((1, R, C), lambda c, h: (h, c, 0))
    shapes = [SDS((GDN_HEADS, L, GDN_DK), F32), SDS((GDN_HEADS, L, GDN_DK), F32), SDS((GDN_HEADS, L, GDN_DV), F32),
              SDS((GDN_HEADS, L, C), F32), SDS((GDN_HEADS, L, GDN_DK), F32), SDS((GDN_HEADS, L, 1), F32)]
    return ins, outs, t_spec, shapes, nb


def _chunks(x, nb):
    return x.reshape(nb, x.shape[0] // nb, x.shape[1])


def _head_columns(bg, h):
    lane = lax.broadcasted_iota(jnp.int32, bg.shape, 1)
    beta = jnp.sum(jnp.where(lane == h, bg, 0.0), axis=1, keepdims=True)
    g = jnp.sum(jnp.where(lane == h + GDN_HEADS, bg, 0.0), axis=1, keepdims=True)
    return beta, g


def _gdn_prep_fwd_call(q, k, v, bg):
    L = q.shape[0]
    ins, outs, t_spec, shapes, nb = _gdn_prep_specs(L)

    def body(q_ref, k_ref, v_ref, bg_ref, *o_refs):
        beta, g = _head_columns(bg_ref[...], pl.program_id(1))
        res = _gdn_prep_math(_chunks(q_ref[...], nb), _chunks(k_ref[...], nb), _chunks(v_ref[...], nb),
                             _chunks(beta, nb), _chunks(g, nb))
        for o_ref, val in zip(o_refs, res):
            o_ref[0] = val.reshape(val.shape[0] * val.shape[1], val.shape[2])

    return pl.pallas_call(
        body, name="gdn_prep_fwd", grid=(L // (nb * GDN_CHUNK), GDN_HEADS), in_specs=ins, out_specs=outs + [t_spec],
        out_shape=shapes + [SDS((GDN_HEADS, L, GDN_CHUNK), F32)],
        compiler_params=_params(("parallel", "parallel"), VMEM_MID),
    )(q, k, v, bg)


def _gdn_prep_bwd_call(q, k, v, bg, t, cts):
    L = q.shape[0]
    ins, outs, t_spec, _, nb = _gdn_prep_specs(L)

    def body(q_ref, k_ref, v_ref, bg_ref, t_ref, c0, c1, c2, c3, c4, c5, dq_ref, dk_ref, dv_ref, dbg_ref):
        h = pl.program_id(1)
        beta, g = _head_columns(bg_ref[...], h)
        t_saved = _chunks(t_ref[0], nb)
        _, vjp = jax.vjp(lambda *a: _gdn_prep_math(*a, t_saved=t_saved)[:6], _chunks(q_ref[...], nb), _chunks(k_ref[...], nb),
                         _chunks(v_ref[...], nb), _chunks(beta, nb), _chunks(g, nb))
        dq, dk, dv, db, dg = vjp(tuple(_chunks(c[0], nb) for c in (c0, c1, c2, c3, c4, c5)))
        flat = lambda a: a.reshape(a.shape[0] * a.shape[1], a.shape[2])
        dq_ref[...] = flat(dq)
        dk_ref[...] = flat(dk)
        dv_ref[...] = flat(dv)

        @pl.when(h == 0)
        def _():
            dbg_ref[...] = jnp.zeros_like(dbg_ref)
        lane = lax.broadcasted_iota(jnp.int32, dbg_ref.shape, 1)
        dbg_ref[...] += jnp.where(lane == h, flat(db), 0.0) + jnp.where(lane == h + GDN_HEADS, flat(dg), 0.0)

    return pl.pallas_call(
        body, name="gdn_prep_bwd", grid=(L // (nb * GDN_CHUNK), GDN_HEADS), in_specs=ins + [t_spec] + outs, out_specs=ins,
        out_shape=[SDS(q.shape, F32), SDS(k.shape, F32), SDS(v.shape, F32), SDS(bg.shape, F32)],
        compiler_params=_params(("parallel", "arbitrary"), VMEM_MID),
    )(q, k, v, bg, t, *cts)


@jax.custom_vjp
def gdn_prep(q, k, v, bg):
    return tuple(_gdn_prep_fwd_call(q, k, v, bg)[:6])


def _gdn_prep_f(q, k, v, bg):
    res = _gdn_prep_fwd_call(q, k, v, bg)
    return tuple(res[:6]), (q, k, v, bg, res[6])


def _gdn_prep_b(res, cts):
    return tuple(_gdn_prep_bwd_call(*res, tuple(cts)))


gdn_prep.defvjp(_gdn_prep_f, _gdn_prep_b)


def _gdn_step_math(q_dec, w, u, qk, k_dec, gc, state):
    H, C = q_dec.shape[0], q_dec.shape[1]
    v_new = u - _bdot(w, state, BNN)
    o = _bdot(q_dec, state, BNN) + _bdot(qk, v_new, BNN)
    gl = gc[:, C - 1:C, :]
    new_state = jnp.exp(gl) * state + _bdot(k_dec, v_new, BTN)
    return jnp.concatenate([o[h] for h in range(H)], axis=1), new_state


def _gdn_scan_specs(L, rev):
    C, H = GDN_CHUNK, GDN_HEADS
    nc = L // C
    cc = (lambda c: nc - 1 - c) if rev else (lambda c: c)
    ins = [pl.BlockSpec((H, C, GDN_DK), lambda c: (0, cc(c), 0)), pl.BlockSpec((H, C, GDN_DK), lambda c: (0, cc(c), 0)),
           pl.BlockSpec((H, C, GDN_DV), lambda c: (0, cc(c), 0)), pl.BlockSpec((H, C, C), lambda c: (0, cc(c), 0)),
           pl.BlockSpec((H, C, GDN_DK), lambda c: (0, cc(c), 0)), pl.BlockSpec((H, C, 1), lambda c: (0, cc(c), 0))]
    o_spec = pl.BlockSpec((C, H * GDN_DV), lambda c: (cc(c), 0))
    s_spec = pl.BlockSpec((1, H, GDN_DK, GDN_DV), lambda c: (cc(c), 0, 0, 0))
    return ins, o_spec, s_spec, nc


def _gdn_scan_fwd_call(q_dec, w, u, qk, k_dec, gc):
    L = q_dec.shape[1]
    ins, o_spec, s_spec, nc = _gdn_scan_specs(L, False)

    def body(qd_ref, w_ref, u_ref, qk_ref, kd_ref, gc_ref, o_ref, sin_ref, s_ref):
        c = pl.program_id(0)

        @pl.when(c == 0)
        def _():
            s_ref[...] = jnp.zeros_like(s_ref)
        st = s_ref[...]
        sin_ref[0] = st
        o, ns = _gdn_step_math(qd_ref[...], w_ref[...], u_ref[...], qk_ref[...], kd_ref[...], gc_ref[...], st)
        o_ref[...] = o
        s_ref[...] = ns

    return pl.pallas_call(
        body, name="gdn_scan_fwd", grid=(nc,), in_specs=ins, out_specs=[o_spec, s_spec],
        out_shape=[SDS((L, GDN_HEADS * GDN_DV), F32), SDS((nc, GDN_HEADS, GDN_DK, GDN_DV), F32)],
        scratch_shapes=[pltpu.VMEM((GDN_HEADS, GDN_DK, GDN_DV), F32)],
        compiler_params=_params(("arbitrary",), VMEM_MID),
    )(q_dec, w, u, qk, k_dec, gc)


def _gdn_scan_bwd_call(q_dec, w, u, qk, k_dec, gc, s_in, do):
    L = q_dec.shape[1]
    ins, o_spec, s_spec, nc = _gdn_scan_specs(L, True)

    def body(qd_ref, w_ref, u_ref, qk_ref, kd_ref, gc_ref, sin_ref, do_ref,
             dqd_ref, dw_ref, du_ref, dqk_ref, dkd_ref, dgc_ref, ds_ref):
        c = pl.program_id(0)

        @pl.when(c == 0)
        def _():
            ds_ref[...] = jnp.zeros_like(ds_ref)
        _, vjp = jax.vjp(_gdn_step_math, qd_ref[...], w_ref[...], u_ref[...], qk_ref[...], kd_ref[...], gc_ref[...], sin_ref[0])
        dqd, dw, du, dqk, dkd, dgc, dst = vjp((do_ref[...], ds_ref[...]))
        dqd_ref[...] = dqd
        dw_ref[...] = dw
        du_ref[...] = du
        dqk_ref[...] = dqk
        dkd_ref[...] = dkd
        dgc_ref[...] = dgc
        ds_ref[...] = dst

    return pl.pallas_call(
        body, name="gdn_scan_bwd", grid=(nc,), in_specs=ins + [s_spec, o_spec], out_specs=ins,
        out_shape=[SDS(t.shape, F32) for t in (q_dec, w, u, qk, k_dec, gc)],
        scratch_shapes=[pltpu.VMEM((GDN_HEADS, GDN_DK, GDN_DV), F32)],
        compiler_params=_params(("arbitrary",), VMEM_MID),
    )(q_dec, w, u, qk, k_dec, gc, s_in, do)


@jax.custom_vjp
def gdn_scan(q_dec, w, u, qk, k_dec, gc):
    return _gdn_scan_fwd_call(q_dec, w, u, qk, k_dec, gc)[0]


def _gdn_scan_f(*args):
    o, s_in = _gdn_scan_fwd_call(*args)
    return o, (*args, s_in)


def _gdn_scan_b(res, do):
    return tuple(_gdn_scan_bwd_call(*res, do))


gdn_scan.defvjp(_gdn_scan_f, _gdn_scan_b)


def _silu(x):
    return x * jax.nn.sigmoid(x)


def _gelu_tanh(x):
    return 0.5 * x * (1.0 + jnp.tanh(math.sqrt(2.0 / math.pi) * (x + 0.044715 * (x * x * x))))


def _f_lnmod(x, nw, sc, sh, bsc, bsh):
    xn = x * lax.rsqrt(jnp.mean(x * x, axis=-1, keepdims=True) + NORM_EPS) * nw
    return (xn * (1.0 + (sc + bsc)) + (sh + bsh),)


def _f_s5_act(ys, u, d):
    return (_gelu_tanh(ys + d * u),)


def _f_s5_gate(y2, t, z):
    return (y2 * jax.nn.sigmoid(t) * _silu(z),)


def _f_res(x, y, gate, bgate):
    return (x + (gate + bgate) * y,)


def _heads(x, width, fn):
    return jnp.concatenate([fn(x[:, i * width:(i + 1) * width]) for i in range(x.shape[1] // width)], axis=1)


def _l2n(x):
    return x * lax.rsqrt(jnp.sum(x * x, axis=-1, keepdims=True) + NORM_EPS)


def _f_qnorm(x):
    return (_heads(_silu(x), GDN_DK, _l2n) * (GDN_DK ** -0.5),)


def _f_knorm(x):
    return (_heads(_silu(x), GDN_DK, _l2n),)


def _f_vact(x):
    return (_silu(x),)


def _f_betag(ba, alog, dtb):
    col = lax.broadcasted_iota(jnp.int32, ba.shape, 1)
    t = ba + dtb
    softplus = jnp.maximum(t, 0.0) + jnp.log1p(jnp.exp(-jnp.abs(t)))
    g = -jnp.exp(alog) * softplus
    return (jnp.where(col < GDN_HEADS, jax.nn.sigmoid(ba), jnp.where(col < 2 * GDN_HEADS, g, 0.0)),)


def _f_gdn_post(o, z, nw):
    on = _heads(o, GDN_DV, lambda t: t * lax.rsqrt(jnp.mean(t * t, axis=-1, keepdims=True) + NORM_EPS))
    return (on * nw * _silu(z),)


def _f_loss(x, tgt, fw):
    y = x * lax.rsqrt(jnp.mean(x * x, axis=-1, keepdims=True) + NORM_EPS) * fw
    err = y - tgt
    return (0.5 * jnp.mean(err * err, axis=-1, keepdims=True),)


def _ada_mod_call(c_all, ada_w):
    n = ada_w.shape[2]

    def body(c_ref, w_ref, o_ref):
        ca = _silu(c_ref[...])
        for l in range(ada_w.shape[0]):
            o_ref[l] = _bdot(ca, w_ref[l])

    return pl.pallas_call(body, name="ada_mod", out_shape=SDS((ada_w.shape[0], N_DEV, n), F32),
                          compiler_params=_params(None, VMEM_MID))(c_all, ada_w)


def _ada_grad_call(c_all, dmod):
    nl, _, n = dmod.shape

    def body(c_ref, d_ref, o_ref):
        ca = _silu(c_ref[...])
        for l in range(nl):
            o_ref[l] = _hdot(ca, d_ref[l], TN)

    return pl.pallas_call(body, name="ada_grad", out_shape=SDS((nl, c_all.shape[1], n), F32),
                          compiler_params=_params(None, VMEM_MID))(c_all, dmod)


ADAM_ROWS = 512


def _adamw(g, w, m, v):
    m2 = ADAM_B1 * m + (1.0 - ADAM_B1) * g
    v2 = ADAM_B2 * v + (1.0 - ADAM_B2) * (g * g)
    m_hat = m2 / (1.0 - ADAM_B1 ** ADAM_STEP)
    v_hat = v2 / (1.0 - ADAM_B2 ** ADAM_STEP)
    return g, -ADAM_LR * (m_hat / (jnp.sqrt(v_hat) + ADAM_EPS) + ADAM_WD * w), m2, v2


def _adam_call(gs, w, m, v, name, rows=None):
    n, r, cols = gs.shape
    rows = rows or ADAM_ROWS

    def body(g_ref, w_ref, m_ref, v_ref, go_ref, d_ref, mo_ref, vo_ref):
        g = g_ref[0].astype(F32)
        for s in range(1, n):
            g = g + g_ref[s].astype(F32)
        for o_ref, val in zip((go_ref, d_ref, mo_ref, vo_ref), _adamw(g, w_ref[...], m_ref[...], v_ref[...])):
            o_ref[...] = val

    blk = pl.BlockSpec((rows, cols), lambda i: (i, 0))
    return pl.pallas_call(
        body, name=name, grid=(r // rows,),
        in_specs=[pl.BlockSpec((n, rows, cols), lambda i: (0, i, 0)), blk, blk, blk],
        out_specs=[blk, blk, blk, blk], out_shape=[SDS((r, cols), F32)] * 4,
        compiler_params=_params(("parallel",), VMEM_MID),
    )(gs, w, m, v)


def _sum_call(gs, name, rows):
    n, r, _ = gs.shape

    def body(g_ref, o_ref):
        g = g_ref[0].astype(F32)
        for s in range(1, n):
            g = g + g_ref[s].astype(F32)
        o_ref[...] = g

    return pl.pallas_call(
        body, name=name, grid=(r // rows,),
        in_specs=[pl.BlockSpec((n, rows, LANES), lambda i: (0, i, 0))],
        out_specs=pl.BlockSpec((rows, LANES), lambda i: (i, 0)), out_shape=SDS((r, LANES), F32),
        compiler_params=_params(("parallel",), VMEM_MID),
    )(gs)


def _allgather_call(x_shard, name, in_hbm):
    m_per, n = x_shard.shape

    def body(x_ref, out_ref, send_sems, recv_sems, local_sem):
        x, y, c = lax.axis_index("x"), lax.axis_index("y"), lax.axis_index("c")
        me, sibling = (x, y, c), (x, y, 1 - c)
        chips = [(1 - x, y), (x, 1 - y), (1 - x, 1 - y)]

        def rows(px, py, pc):
            return out_ref.at[pl.ds((4 * px + 2 * py + pc) * m_per, m_per), :]

        def copy(k, block, to, src=None):
            return pltpu.make_async_remote_copy(
                src_ref=rows(*block) if src is None else src, dst_ref=rows(*block),
                send_sem=send_sems.at[k], recv_sem=recv_sems.at[k], device_id=to, device_id_type=pl.DeviceIdType.MESH)

        mine = pltpu.make_async_copy(x_ref, rows(*me), local_sem)
        mine.start()
        first = [copy(0, me, sibling, src=x_ref)]
        first += [copy(1 + j, me, (*chip, c), src=x_ref) for j, chip in enumerate(chips)]
        for cp in first:
            cp.start()
        passed = [copy(4 + j, (*chip, c), sibling) for j, chip in enumerate(chips)]
        for j, chip in enumerate(chips):
            copy(1 + j, (*chip, c), me).wait_recv()
            passed[j].start()
        copy(0, sibling, me).wait_recv()
        for j, chip in enumerate(chips):
            copy(4 + j, (*chip, 1 - c), me).wait_recv()
        for cp in first + passed:
            cp.wait_send()
        mine.wait()

    space = pl.ANY if in_hbm else pltpu.VMEM
    return pl.pallas_call(
        body, name=name, out_shape=SDS((N_DEV * m_per, n), x_shard.dtype),
        in_specs=[pl.BlockSpec(memory_space=space)], out_specs=pl.BlockSpec(memory_space=space),
        scratch_shapes=[pltpu.SemaphoreType.DMA((7,)), pltpu.SemaphoreType.DMA((7,)), pltpu.SemaphoreType.DMA],
        compiler_params=_params(None, None if in_hbm else VMEM_BIG),
    )(x_shard)


def _gather_weights_call(shards, name):
    nw = len(shards)

    def body(*refs):
        x_refs, out_refs = refs[:nw], refs[nw:2 * nw]
        send_sems, recv_sems, local_sems = refs[2 * nw:]
        x, y, c = lax.axis_index("x"), lax.axis_index("y"), lax.axis_index("c")
        me, sibling = (x, y, c), (x, y, 1 - c)
        chips = [(1 - x, y), (x, 1 - y), (1 - x, 1 - y)]

        def slot(w, px, py, pc):
            return out_refs[w].at[4 * px + 2 * py + pc]

        def copy(w, k, block, to, src=None):
            dst = slot(w, *block)
            return pltpu.make_async_remote_copy(
                src_ref=dst if src is None else src, dst_ref=dst, send_sem=send_sems.at[7 * w + k],
                recv_sem=recv_sems.at[7 * w + k], device_id=to, device_id_type=pl.DeviceIdType.MESH)

        mines = [pltpu.make_async_copy(x_refs[w], slot(w, *me), local_sems.at[w]) for w in range(nw)]
        for cp in mines:
            cp.start()
        first = [copy(w, 0, me, sibling, src=x_refs[w]) for w in range(nw)]
        first += [copy(w, 1 + j, me, (*chip, c), src=x_refs[w]) for w in range(nw) for j, chip in enumerate(chips)]
        for cp in first:
            cp.start()
        passed = []
        for w in range(nw):
            for j, chip in enumerate(chips):
                copy(w, 1 + j, (*chip, c), me).wait_recv()
                fwd = copy(w, 4 + j, (*chip, c), sibling)
                fwd.start()
                passed.append(fwd)
        for w in range(nw):
            copy(w, 0, sibling, me).wait_recv()
            for j, chip in enumerate(chips):
                copy(w, 4 + j, (*chip, 1 - c), me).wait_recv()
        for cp in first + passed:
            cp.wait_send()
        for cp in mines:
            cp.wait()

    hbm = pl.BlockSpec(memory_space=pl.ANY)
    return pl.pallas_call(
        body, name=name, out_shape=[SDS((N_DEV,) + s.shape, s.dtype) for s in shards],
        in_specs=[hbm] * nw, out_specs=[hbm] * nw,
        scratch_shapes=[pltpu.SemaphoreType.DMA((7 * nw,)), pltpu.SemaphoreType.DMA((7 * nw,)), pltpu.SemaphoreType.DMA((nw,))],
    )(*shards)


def _pair_exchange_call(grads, name):
    nw = len(grads)

    def body(*refs):
        g_refs, got_refs = refs[:nw], refs[nw:2 * nw]
        send_sems, recv_sems = refs[2 * nw:]
        x, y, c = lax.axis_index("x"), lax.axis_index("y"), lax.axis_index("c")
        copies = []
        for w in range(nw):
            for j in range(4):
                give = pltpu.make_async_remote_copy(
                    src_ref=g_refs[w].at[2 * j + 1 - c], dst_ref=got_refs[w].at[j], send_sem=send_sems.at[4 * w + j],
                    recv_sem=recv_sems.at[4 * w + j], device_id=(x, y, 1 - c), device_id_type=pl.DeviceIdType.MESH)
                give.start()
                copies.append(give)
        for cp in copies:
            cp.wait()

    hbm = pl.BlockSpec(memory_space=pl.ANY)
    return pl.pallas_call(
        body, name=name, out_shape=[SDS((4,) + g.shape[1:], g.dtype) for g in grads], in_specs=[hbm] * nw, out_specs=[hbm] * nw,
        scratch_shapes=[pltpu.SemaphoreType.DMA((4 * nw,)), pltpu.SemaphoreType.DMA((4 * nw,))],
    )(*grads)


def _chip_exchange_call(parts, name):
    nw = len(parts)

    def body(*refs):
        p_refs, out_refs = refs[:nw], refs[nw:2 * nw]
        send_sems, recv_sems = refs[2 * nw:]
        x, y, c = lax.axis_index("x"), lax.axis_index("y"), lax.axis_index("c")
        chips = [(1 - x, y), (x, 1 - y), (1 - x, 1 - y)]
        copies = []
        for w in range(nw):
            for j, (px, py) in enumerate(chips):
                give = pltpu.make_async_remote_copy(
                    src_ref=p_refs[w].at[2 * px + py], dst_ref=out_refs[w].at[j], send_sem=send_sems.at[3 * w + j],
                    recv_sem=recv_sems.at[3 * w + j], device_id=(px, py, c), device_id_type=pl.DeviceIdType.MESH)
                give.start()
                copies.append(give)
        for cp in copies:
            cp.wait()

    hbm = pl.BlockSpec(memory_space=pl.ANY)
    return pl.pallas_call(
        body, name=name, out_shape=[SDS((3,) + p.shape[1:], p.dtype) for p in parts], in_specs=[hbm] * nw, out_specs=[hbm] * nw,
        scratch_shapes=[pltpu.SemaphoreType.DMA((3 * nw,)), pltpu.SemaphoreType.DMA((3 * nw,))],
    )(*parts)


_HBM = pl.BlockSpec(memory_space=pltpu.HBM)
_SEM = pl.BlockSpec(memory_space=pltpu.SEMAPHORE)
_DATAFLOW = pltpu.SideEffectType.DATAFLOW_SIDE_EFFECTING


def _spread_start_call(srcs, per_peer, name):
    nw = len(srcs)
    lands = [lax.empty((N_DEV,) + (s.shape[1:] if per_peer else s.shape), s.dtype) for s in srcs]

    def body(*refs):
        src_refs, land_refs = refs[:nw], refs[nw:2 * nw]
        send_sems, recv_sems, token = refs[2 * nw], refs[2 * nw + 1], refs[-1]
        x, y, c = lax.axis_index("x"), lax.axis_index("y"), lax.axis_index("c")
        me = 4 * x + 2 * y + c
        for w in range(nw):
            for k in range(1, N_DEV):
                px = 1 - x if k & 4 else x
                py = 1 - y if k & 2 else y
                pc = 1 - c if k & 1 else c
                src = src_refs[w].at[4 * px + 2 * py + pc] if per_peer else src_refs[w]
                pltpu.make_async_remote_copy(
                    src_ref=src, dst_ref=land_refs[w].at[me], send_sem=send_sems.at[w], recv_sem=recv_sems.at[w],
                    device_id=(px, py, pc), device_id_type=pl.DeviceIdType.MESH).start()
        token[...] = jnp.zeros_like(token)

    hbm = lambda a: pltpu.with_memory_space_constraint(a, pltpu.HBM)
    res = pl.pallas_call(
        body, name=name,
        out_shape=(pltpu.SemaphoreType.DMA((nw,)), pltpu.SemaphoreType.DMA((nw,)))
        + tuple(pltpu.HBM(s.shape, s.dtype) for s in srcs) + tuple(pltpu.HBM(l.shape, l.dtype) for l in lands)
        + (SDS((SUBLANES, LANES), F32),),
        in_specs=[_HBM] * (2 * nw), out_specs=(_SEM, _SEM) + (_HBM,) * (2 * nw) + (pl.BlockSpec(memory_space=pltpu.VMEM),),
        input_output_aliases={i: i + 2 for i in range(2 * nw)},
        compiler_params=pltpu.CompilerParams(has_side_effects=_DATAFLOW),
    )(*[hbm(s) for s in srcs], *[hbm(l) for l in lands])
    return res[0], res[1], res[2:2 + nw], res[2 + nw:2 + 2 * nw], res[-1]


def _spread_wait_call(send_sems, recv_sems, srcs, lands, after, name):
    nw = len(lands)

    def body(*refs):
        land_refs = refs[nw:2 * nw]
        s_sems, r_sems = refs[2 * nw], refs[2 * nw + 1]
        x, y, c = lax.axis_index("x"), lax.axis_index("y"), lax.axis_index("c")
        for w in range(nw):
            seven = land_refs[w].at[pl.ds(0, N_DEV - 1)]
            all_seven = pltpu.make_async_remote_copy(
                src_ref=seven, dst_ref=seven, send_sem=s_sems.at[w], recv_sem=r_sems.at[w],
                device_id=(x, y, c), device_id_type=pl.DeviceIdType.MESH)
            all_seven.wait_send()
            all_seven.wait_recv()

    res = pl.pallas_call(
        body, name=name,
        out_shape=tuple(pltpu.HBM(s.shape, s.dtype) for s in srcs) + tuple(pltpu.HBM(l.shape, l.dtype) for l in lands),
        in_specs=[_HBM] * (2 * nw) + [_SEM, _SEM, pl.BlockSpec(memory_space=pl.ANY)], out_specs=(_HBM,) * (2 * nw),
        input_output_aliases={i: i for i in range(2 * nw)},
        compiler_params=pltpu.CompilerParams(has_side_effects=_DATAFLOW),
    )(*srcs, *lands, send_sems, recv_sems, after)
    return res[:nw], res[nw:]


def _pair_sum_call(g, got, core, name):
    _, k, n = got.shape
    tr = _tile(k, 256)

    def body(c_ref, g_ref, got_ref, o_ref):
        o_ref[...] = (g_ref[...] + got_ref[...]).astype(o_ref.dtype)

    spec = pltpu.PrefetchScalarGridSpec(
        num_scalar_prefetch=1, grid=(4, k // tr),
        in_specs=[pl.BlockSpec((1, tr, n), lambda j, i, c: (2 * j + c[0], i, 0)), pl.BlockSpec((1, tr, n), lambda j, i, c: (j, i, 0))],
        out_specs=pl.BlockSpec((1, tr, n), lambda j, i, c: (j, i, 0)))
    return pl.pallas_call(body, name=name, grid_spec=spec, out_shape=SDS(got.shape, BF16),
                          compiler_params=_params(("parallel", "parallel"), VMEM_MID))(core, g, got)


def _adam_own_call(pair, chip, recv, w, m, v, name, rows):
    _, r, cols = recv.shape

    def body(chip_ref, p_ref, g_ref, w_ref, m_ref, v_ref, go_ref, d_ref, mo_ref, vo_ref):
        g = ((p_ref[0].astype(F32) + g_ref[0].astype(F32)) + g_ref[1].astype(F32)) + g_ref[2].astype(F32)
        for o_ref, val in zip((go_ref, d_ref, mo_ref, vo_ref), _adamw(g, w_ref[...], m_ref[...], v_ref[...])):
            o_ref[...] = val

    blk = pl.BlockSpec((rows, cols), lambda i, s: (i, 0))
    spec = pltpu.PrefetchScalarGridSpec(
        num_scalar_prefetch=1, grid=(r // rows,),
        in_specs=[pl.BlockSpec((1, rows, cols), lambda i, s: (s[0], i, 0)), pl.BlockSpec((3, rows, cols), lambda i, s: (0, i, 0)),
                  blk, blk, blk],
        out_specs=[blk, blk, blk, blk])
    return pl.pallas_call(body, name=name, grid_spec=spec, out_shape=[SDS((r, cols), F32)] * 4,
                          compiler_params=_params(("parallel",), VMEM_MID))(chip, pair, recv, w, m, v)


def _join_cols_call(w8, name):
    _, k, n = w8.shape
    tk = _tile(k, 256)

    def body(w_ref, o_ref):
        for s in range(N_DEV):
            o_ref[:, n * s:n * (s + 1)] = w_ref[s]

    return pl.pallas_call(body, name=name, grid=(k // tk,), in_specs=[pl.BlockSpec((N_DEV, tk, n), lambda i: (0, i, 0))],
                          out_specs=pl.BlockSpec((tk, N_DEV * n), lambda i: (i, 0)), out_shape=SDS((k, N_DEV * n), w8.dtype),
                          compiler_params=_params(("parallel",), VMEM_MID))(w8)


def _split_cols_call(g, name, dtype):
    k, n8 = g.shape
    n = n8 // N_DEV
    tk = _tile(k, 256)

    def body(g_ref, o_ref):
        for s in range(N_DEV):
            o_ref[s] = g_ref[:, n * s:n * (s + 1)].astype(dtype)

    return pl.pallas_call(body, name=name, grid=(k // tk,), in_specs=[pl.BlockSpec((tk, n8), lambda i: (i, 0))],
                          out_specs=pl.BlockSpec((N_DEV, tk, n), lambda i: (0, i, 0)), out_shape=SDS((N_DEV, k, n), dtype),
                          compiler_params=_params(("parallel",), VMEM_MID))(g)


def _pack(parts, rows_multiple):
    flat = jnp.concatenate([p.reshape(-1) for p in parts])
    unit = rows_multiple * LANES
    padded = -(-flat.shape[0] // unit) * unit
    flat = jnp.concatenate([flat, jnp.zeros((padded - flat.shape[0],), F32)])
    return flat.reshape(-1, LANES)


def _unpack(buf, shapes):
    flat = buf.reshape(-1)
    out, off = [], 0
    for s in shapes:
        n = math.prod(s)
        out.append(flat[off:off + n].reshape(s))
        off += n
    return out


def _row_tile(L):
    return 256 if L % 256 == 0 else L


def _layer0(diff, const):
    x, mod, norm_w, lam_re, lam_im, log_dt, b_re, b_im, c_re, c_im, s5_d, *slots = diff
    ada_b, weights = const
    L = x.shape[0]
    tm = _row_tile(L)
    mods = mod.reshape(3, 1, D_MODEL)
    biases = ada_b.reshape(3, 1, D_MODEL)
    op_ln0 = make_rowwise(_f_lnmod, "ln0", tm, 1, 5, pass_first=True)
    h, x = op_ln0((x,), (norm_w.reshape(1, D_MODEL), mods[1], mods[0], biases[1], biases[0]))
    u, z = make_proj("s5_in")(h, tuple(weights[0:2]), tuple(slots[0:2]))
    blocks = _s5_block_params(lam_re, lam_im, log_dt, b_re, b_im, c_re, c_im)
    ys, u = make_s5_core(min(S5_TL, L))(u, *blocks)
    (y2,) = make_rowwise(_f_s5_act, "s5_act", tm, 2, 1)((ys, u), (s5_d.reshape(1, D_INNER),))
    t = make_mm("s5_glu")(y2, weights[2], slots[2])
    (y4,) = make_rowwise(_f_s5_gate, "s5_gate", tm, 3, 0)((y2, t, z), ())
    o = make_mm("s5_out")(y4, weights[3], slots[3])
    return make_residual("res0", tm)(x, o, mods[2], biases[2])


def _layer1_loss(diff, const):
    x1, mod, norm_w, conv_w, a_log, dt_bias, gdn_nw, final_nw, *slots = diff
    tgt, ada_b, weights = const
    L = x1.shape[0]
    tm = _row_tile(L)
    mods = mod.reshape(3, 1, D_MODEL)
    biases = ada_b.reshape(3, 1, D_MODEL)
    op_ln1 = make_rowwise(_f_lnmod, "ln1", tm, 1, 5, pass_first=True)
    h, x1 = op_ln1((x1,), (norm_w.reshape(1, D_MODEL), mods[1], mods[0], biases[1], biases[0]))
    q0, k0, v0, gz, ba = make_proj("gdn_in")(h, tuple(weights[0:5]), tuple(slots[0:5]))
    cw = jnp.concatenate([conv_w, jnp.zeros((SUBLANES - GDN_CONV, GDN_CONV_CH), F32)], axis=0)
    (q,) = make_rowwise(_f_qnorm, "gdn_qn", tm, 1, 0)((gdn_conv(q0, cw[:, :GDN_QK]),), ())
    (k,) = make_rowwise(_f_knorm, "gdn_kn", tm, 1, 0)((gdn_conv(k0, cw[:, GDN_QK:2 * GDN_QK]),), ())
    (v,) = make_rowwise(_f_vact, "gdn_va", tm, 1, 0)((gdn_conv(v0, cw[:, 2 * GDN_QK:]),), ())
    pad = jnp.zeros((LANES - 2 * GDN_HEADS,), F32)
    alog_row = jnp.concatenate([jnp.zeros((GDN_HEADS,), F32), a_log, pad]).reshape(1, LANES)
    dtb_row = jnp.concatenate([jnp.zeros((GDN_HEADS,), F32), dt_bias, pad]).reshape(1, LANES)
    (bg,) = make_rowwise(_f_betag, "gdn_bg", tm, 1, 2)((ba,), (alog_row, dtb_row))
    og = gdn_scan(*gdn_prep(q, k, v, bg))
    nw_row = jnp.tile(gdn_nw, GDN_HEADS).reshape(1, D_INNER)
    (on,) = make_rowwise(_f_gdn_post, "gdn_post", tm, 2, 1)((og, gz), (nw_row,))
    y = make_mm("gdn_out")(on, weights[5], slots[5])
    x2 = make_residual("res1", tm)(x1, y, mods[2], biases[2])

    (lt,) = make_rowwise(_f_loss, "loss", tm, 2, 1)((x2, tgt), (final_nw.reshape(1, D_MODEL),))
    return jnp.sum(lt)


VEC_NAMES = ("ada_b", "norm_w", "s5_lambda_re", "s5_lambda_im", "s5_log_dt", "s5_d", "gdn_a_log", "gdn_dt_bias", "final_norm_w")
MAT_NAMES = ("s5_b_re", "s5_b_im", "s5_c_re", "s5_c_im")
S5_BIG = ("s5_w_in", "s5_w_glu", "s5_w_out")
GDN_BIG = ("gdn_w_in", "gdn_w_out")
BIG_NAMES = S5_BIG + GDN_BIG
WEIGHT_ORDER = ("ada_w", "ada_b", "norm_w", "s5_w_in", "s5_lambda_re", "s5_lambda_im", "s5_log_dt", "s5_b_re", "s5_b_im",
                "s5_c_re", "s5_c_im", "s5_d", "s5_w_glu", "s5_w_out", "gdn_w_in", "gdn_conv_w", "gdn_a_log", "gdn_dt_bias",
                "gdn_norm_w", "gdn_w_out", "final_norm_w")


def _step(x, c, W, M, V, tgt):
    L = x.shape[1]
    ix, iy, ic = lax.axis_index("x"), lax.axis_index("y"), lax.axis_index("c")
    me = 4 * ix + 2 * iy + ic
    n_ada = W["ada_w"].shape[2]
    n_conv = W["gdn_conv_w"].shape[2]
    n_gnw = W["gdn_norm_w"].shape[1]

    g1 = _allgather_call(_pack([c, W["gdn_conv_w"], W["gdn_norm_w"]], SUBLANES), "gather_small_in", False)
    g1 = g1.reshape(N_DEV, -1)
    c_all = g1[:, :D_MODEL]
    conv_w = g1[:, D_MODEL:D_MODEL + GDN_CONV * n_conv].reshape(N_DEV, GDN_CONV, n_conv).transpose(1, 0, 2).reshape(GDN_CONV, -1)
    gdn_nw = g1[:, D_MODEL + GDN_CONV * n_conv:D_MODEL + GDN_CONV * n_conv + n_gnw].reshape(-1)
    mod_part = _ada_mod_call(c_all, W["ada_w"])
    g2 = _allgather_call(_pack([mod_part], SUBLANES), "gather_mod", False).reshape(N_DEV, -1)
    mod_all = g2[:, :2 * N_DEV * n_ada].reshape(N_DEV, 2, N_DEV, n_ada)
    mod_raw = lax.dynamic_index_in_dim(mod_all, me, axis=2, keepdims=False)
    mod_raw = mod_raw.transpose(1, 0, 2).reshape(2, 3 * D_MODEL)

    shard = lambda n: W[n][0].astype(BF16)
    full = dict(zip(S5_BIG, _gather_weights_call([shard(n) for n in S5_BIG], "gather_s5_weights")))
    g_send, g_recv, g_srcs, g_lands, g_token = _spread_start_call([shard(n) for n in GDN_BIG], False, "gather_gdn_start")
    w_in5 = _join_cols_call(full["s5_w_in"], "join_s5_w_in")
    weights0 = (w_in5[:, :D_INNER], w_in5[:, D_INNER:], full["s5_w_glu"].reshape(D_INNER, D_INNER),
                full["s5_w_out"].reshape(D_INNER, D_MODEL))
    slots0 = tuple(jnp.zeros(w.shape, F32) for w in weights0)
    diff0 = (x[0], mod_raw[0] + g_token[0, 0], W["norm_w"][0], W["s5_lambda_re"][0], W["s5_lambda_im"][0], W["s5_log_dt"][0],
             W["s5_b_re"][0], W["s5_b_im"][0], W["s5_c_re"][0], W["s5_c_im"][0], W["s5_d"][0], *slots0)

    x1, vjp0 = jax.vjp(lambda d: _layer0(d, (W["ada_b"][0], weights0)), diff0)
    g_srcs, g_lands = _spread_wait_call(g_send, g_recv, g_srcs, g_lands, x1, "gather_gdn_wait")
    gdn_full = [lax.dynamic_update_slice(land, src[None], (me, 0, 0)) for land, src in zip(g_lands, g_srcs)]
    w_ing = _join_cols_call(gdn_full[0], "join_gdn_w_in")
    w_ba = jnp.concatenate([w_ing[:, GDN_CONV_CH + D_INNER:], jnp.zeros((D_MODEL, LANES - 2 * GDN_HEADS), BF16)], axis=1)
    weights1 = (w_ing[:, :GDN_QK], w_ing[:, GDN_QK:2 * GDN_QK], w_ing[:, 2 * GDN_QK:GDN_CONV_CH],
                w_ing[:, GDN_CONV_CH:GDN_CONV_CH + D_INNER], w_ba, gdn_full[1].reshape(D_INNER, D_MODEL))
    slots1 = tuple(jnp.zeros(w.shape, F32) for w in weights1)
    diff1 = (x1, mod_raw[1], W["norm_w"][1], conv_w, W["gdn_a_log"][0], W["gdn_dt_bias"][0], gdn_nw, W["final_norm_w"], *slots1)
    loss_local, vjp1 = jax.vjp(lambda d: _layer1_loss(d, (tgt[0], W["ada_b"][1], weights1)), diff1)
    ((dx1, dmod1, d_norm_w1, d_conv, d_alog, d_dtb, d_gnw, d_fnw, d_wq, d_wk, d_wv, d_wgz, d_wba, d_wog),) = vjp1(jnp.ones((), F32))
    loss = lax.psum(loss_local, MESH_AXES)

    rows = lambda d: d.reshape(N_DEV, d.shape[0] // N_DEV, d.shape[1])
    d_ing = _split_cols_call(jnp.concatenate([d_wq, d_wk, d_wv, d_wgz, d_wba[:, :2 * GDN_HEADS]], axis=1), "split_gdn_w_in", BF16)
    s_send, s_recv, s_srcs, s_lands, s_token = _spread_start_call([d_ing, rows(d_wog).astype(BF16)], True, "scatter_gdn_start")
    ((dx, dmod0, d_norm_w0, d_lre, d_lim, d_logdt, d_bre, d_bim, d_cre, d_cim, d_s5d, d_wu, d_wz, d_wglu, d_wo5),) = vjp0(
        dx1.at[0, 0].add(s_token[0, 0]))
    dmod = jnp.stack([dmod0, dmod1])
    d_norm_w = jnp.stack([d_norm_w0, d_norm_w1])
    d_in5 = _split_cols_call(jnp.concatenate([d_wu, d_wz], axis=1), "split_s5_w_in", F32)
    per_dev = [d_in5, rows(d_wglu), rows(d_wo5)]
    got = _pair_exchange_call(per_dev, "scatter_s5_pair")
    core = jnp.reshape(ic, (1,)).astype(jnp.int32)
    chip = jnp.reshape(2 * ix + iy, (1,)).astype(jnp.int32)
    pair = [_pair_sum_call(g, r, core, "pair_sum_" + n) for g, r, n in zip(per_dev, got, S5_BIG)]
    recv = _chip_exchange_call(pair, "scatter_s5_chips")
    big = [_adam_own_call(p, chip, r, W[n][0], M[n][0], V[n][0], "adam_" + n, _tile(W[n].shape[1], 128))
           for p, r, n in zip(pair, recv, S5_BIG)]
    s_srcs, s_lands = _spread_wait_call(s_send, s_recv, s_srcs, s_lands, dx, "scatter_gdn_wait")
    for land, src, n in zip(s_lands, s_srcs, GDN_BIG):
        mine = lax.dynamic_index_in_dim(src, me, 0, keepdims=True)
        parts = lax.dynamic_update_slice(land, mine, (me, 0, 0))
        big.append(_adam_call(parts, W[n][0], M[n][0], V[n][0], "adam_" + n, rows=_tile(W[n].shape[1], 128)))
    big = [[o[None] for o in outs] for outs in big]

    vec_parts = [dmod, d_norm_w, d_lre, d_lim, d_logdt, d_s5d, d_alog, d_dtb, d_fnw]
    tail_parts = [d_conv, d_gnw]
    mat_parts = [d_bre, d_bim, d_cre, d_cim]
    n_vec = sum(math.prod(p.shape) for p in vec_parts)
    sg_vec, sg_mat = _gather_weights_call(
        [_pack(vec_parts + tail_parts, ADAM_ROWS), _pack(mat_parts, SUBLANES).astype(BF16)], "gather_small_grads")
    tot_vec = _sum_call(sg_vec, "sum_vec_grads", ADAM_ROWS)
    tot_mat = _sum_call(sg_mat, "sum_mat_grads", ADAM_ROWS)
    g_conv, g_gnw = _unpack(tot_vec.reshape(-1)[n_vec:], [d_conv.shape, d_gnw.shape])
    g_conv_mine = lax.dynamic_slice_in_dim(g_conv, me * n_conv, n_conv, axis=1)
    g_gnw_mine = lax.dynamic_slice_in_dim(g_gnw, me * n_gnw, n_gnw, axis=0)
    vec_names = VEC_NAMES + ("gdn_conv_w", "gdn_norm_w")
    vec_g = _pack([tot_vec.reshape(-1)[:n_vec], g_conv_mine, g_gnw_mine], ADAM_ROWS)
    vec = _adam_call(vec_g[None], _pack([W[n] for n in vec_names], ADAM_ROWS), _pack([M[n] for n in vec_names], ADAM_ROWS),
                     _pack([V[n] for n in vec_names], ADAM_ROWS), "adam_vec")
    vec = [_unpack(b, [W[n].shape for n in vec_names]) for b in vec]
    mats = []
    for name, g_mat in zip(MAT_NAMES, _unpack(tot_mat, [p.shape for p in mat_parts])):
        two_d = (-1, W[name].shape[-1])
        outs = _adam_call(g_mat.reshape(two_d)[None], W[name].reshape(two_d), M[name].reshape(two_d), V[name].reshape(two_d),
                          "adam_" + name, rows=1024)
        mats.append([o.reshape(W[name].shape) for o in outs])

    dmod_all = sg_vec[:, :2 * 3 * D_MODEL // LANES].reshape(N_DEV, 2, N_DEV, n_ada // LANES, LANES)
    dmod_mine = lax.dynamic_index_in_dim(dmod_all, me, axis=2, keepdims=False).transpose(1, 0, 2, 3).reshape(2, N_DEV, n_ada)
    g_ada_w = _ada_grad_call(c_all, dmod_mine)
    ada = _adam_call(g_ada_w.reshape(1, -1, LANES), W["ada_w"].reshape(-1, LANES), M["ada_w"].reshape(-1, LANES),
                     V["ada_w"].reshape(-1, LANES), "adam_ada")
    ada = [a.reshape(W["ada_w"].shape) for a in ada]

    res = {}
    for i, n in enumerate(BIG_NAMES):
        res[n] = big[i]
    for i, n in enumerate(vec_names):
        res[n] = [b[i] for b in vec]
    for i, n in enumerate(MAT_NAMES):
        res[n] = mats[i]
    res["ada_w"] = ada
    outs = [loss, dx[None]]
    for j in range(4):
        outs += [res[n][j] for n in WEIGHT_ORDER]
    return tuple(outs)


def kernel(x, c, ada_w, ada_b, norm_w, s5_w_in, s5_lambda_re, s5_lambda_im, s5_log_dt, s5_b_re, s5_b_im, s5_c_re, s5_c_im, s5_d, s5_w_glu, s5_w_out, gdn_w_in, gdn_conv_w, gdn_a_log, gdn_dt_bias, gdn_norm_w, gdn_w_out, final_norm_w, loss_target, m_ada_w, m_ada_b, m_norm_w, m_s5_w_in, m_s5_lambda_re, m_s5_lambda_im, m_s5_log_dt, m_s5_b_re, m_s5_b_im, m_s5_c_re, m_s5_c_im, m_s5_d, m_s5_w_glu, m_s5_w_out, m_gdn_w_in, m_gdn_conv_w, m_gdn_a_log, m_gdn_dt_bias, m_gdn_norm_w, m_gdn_w_out, m_final_norm_w, v_ada_w, v_ada_b, v_norm_w, v_s5_w_in, v_s5_lambda_re, v_s5_lambda_im, v_s5_log_dt, v_s5_b_re, v_s5_b_im, v_s5_c_re, v_s5_c_im, v_s5_d, v_s5_w_glu, v_s5_w_out, v_gdn_w_in, v_gdn_conv_w, v_gdn_a_log, v_gdn_dt_bias, v_gdn_norm_w, v_gdn_w_out, v_final_norm_w):
    W = dict(ada_w=ada_w, ada_b=ada_b, norm_w=norm_w, s5_w_in=s5_w_in, s5_lambda_re=s5_lambda_re, s5_lambda_im=s5_lambda_im,
             s5_log_dt=s5_log_dt, s5_b_re=s5_b_re, s5_b_im=s5_b_im, s5_c_re=s5_c_re, s5_c_im=s5_c_im, s5_d=s5_d,
             s5_w_glu=s5_w_glu, s5_w_out=s5_w_out, gdn_w_in=gdn_w_in, gdn_conv_w=gdn_conv_w, gdn_a_log=gdn_a_log,
             gdn_dt_bias=gdn_dt_bias, gdn_norm_w=gdn_norm_w, gdn_w_out=gdn_w_out, final_norm_w=final_norm_w)
    M = dict(ada_w=m_ada_w, ada_b=m_ada_b, norm_w=m_norm_w, s5_w_in=m_s5_w_in, s5_lambda_re=m_s5_lambda_re,
             s5_lambda_im=m_s5_lambda_im, s5_log_dt=m_s5_log_dt, s5_b_re=m_s5_b_re, s5_b_im=m_s5_b_im, s5_c_re=m_s5_c_re,
             s5_c_im=m_s5_c_im, s5_d=m_s5_d, s5_w_glu=m_s5_w_glu, s5_w_out=m_s5_w_out, gdn_w_in=m_gdn_w_in,
             gdn_conv_w=m_gdn_conv_w, gdn_a_log=m_gdn_a_log, gdn_dt_bias=m_gdn_dt_bias, gdn_norm_w=m_gdn_norm_w,
             gdn_w_out=m_gdn_w_out, final_norm_w=m_final_norm_w)
    V = dict(ada_w=v_ada_w, ada_b=v_ada_b, norm_w=v_norm_w, s5_w_in=v_s5_w_in, s5_lambda_re=v_s5_lambda_re,
             s5_lambda_im=v_s5_lambda_im, s5_log_dt=v_s5_log_dt, s5_b_re=v_s5_b_re, s5_b_im=v_s5_b_im, s5_c_re=v_s5_c_re,
             s5_c_im=v_s5_c_im, s5_d=v_s5_d, s5_w_glu=v_s5_w_glu, s5_w_out=v_s5_w_out, gdn_w_in=v_gdn_w_in,
             gdn_conv_w=v_gdn_conv_w, gdn_a_log=v_gdn_a_log, gdn_dt_bias=v_gdn_dt_bias, gdn_norm_w=v_gdn_norm_w,
             gdn_w_out=v_gdn_w_out, final_norm_w=v_final_norm_w)
    return _step(x, c, W, M, V, loss_target)
```

```python
import functools
import math

import jax
import jax.numpy as jnp
from jax import lax
from jax.experimental import pallas as pl
from jax.experimental.pallas import tpu as pltpu

F32 = jnp.float32
BF16 = jnp.bfloat16
SDS = jax.ShapeDtypeStruct

D_MODEL = 1024
D_INNER = 2048
NORM_EPS = 1e-6
S5_GROUP = 16
S5_GROUPS = 128
S5_STATE = 64
GDN_HEADS = 8
GDN_DK = 128
GDN_DV = 256
GDN_CONV = 4
GDN_CHUNK = 64
GDN_QK = 1024
GDN_CONV_CH = 4096
GDN_PROJ = 6160
ADAM_LR = 0.001
ADAM_B1 = 0.9
ADAM_B2 = 0.999
ADAM_EPS = 1e-08
ADAM_WD = 0.01
ADAM_STEP = 10

N_DEV = 8
LANES = 128
SUBLANES = 8
VMEM_BIG = 56 << 20
VMEM_MID = 40 << 20
S5_GB = 8
S5_TL = 1024
MESH_AXES = ("x", "y", "c")


def _params(sem, vmem=None):
    return pltpu.CompilerParams(dimension_semantics=sem, vmem_limit_bytes=vmem)


def _bdot(a, b, dims=(((1,), (0,)), ((), ()))):
    return lax.dot_general(a.astype(BF16), b.astype(BF16), dims, preferred_element_type=F32)


def _hdot(a, b, dims=(((1,), (0,)), ((), ()))):
    return lax.dot_general(a, b, dims, preferred_element_type=F32, precision=lax.Precision.HIGHEST)


_BNN = (((2,), (1,)), ((0,), (0,)))
_BNT = (((2,), (2,)), ((0,), (0,)))
_BTN = (((1,), (1,)), ((0,), (0,)))


@jax.custom_vjp
def _unit_lower_inverse(a):
    c = a.shape[-1]
    ri = lax.broadcasted_iota(jnp.int32, a.shape, 1)
    ci = lax.broadcasted_iota(jnp.int32, a.shape, 2)
    n = -a
    t = (ri == ci).astype(F32) + n
    for _ in range(int(math.log2(c)) - 1):
        n = _hdot(n, n, _BNN)
        t = t + _hdot(t, n, _BNN)
    return t


def _unit_lower_inverse_fwd(a):
    t = _unit_lower_inverse(a)
    return t, t


def _unit_lower_inverse_bwd(t, g):
    return (-_hdot(_hdot(t, g, _BTN), t, _BNT),)


_unit_lower_inverse.defvjp(_unit_lower_inverse_fwd, _unit_lower_inverse_bwd)


NN = (((1,), (0,)), ((), ()))
NT = (((1,), (1,)), ((), ()))
TN = (((0,), (0,)), ((), ()))


def _tile(n, pref):
    for t in (pref, 512, 256, 128):
        if t <= n and n % t == 0:
            return t
    return n


def _matmul(a, b, mode, name):
    if mode == "nn":
        (m, k), (_, n) = a.shape, b.shape
    elif mode == "nt":
        (m, k), (n, _) = a.shape, b.shape
    else:
        (k, m), (_, n) = a.shape, b.shape
    tm, tn, tk = _tile(m, 512), _tile(n, 512), (k if k <= 2048 else _tile(k, 512))
    if mode == "tn":
        tm, tn = _tile(m, 1024), _tile(n, 1024)
    nk = k // tk
    dims = {"nn": NN, "nt": NT, "tn": TN}[mode]

    def body(a_ref, b_ref, o_ref, acc_ref):
        kk = pl.program_id(2)

        @pl.when(kk == 0)
        def _():
            acc_ref[...] = jnp.zeros_like(acc_ref)
        acc_ref[...] += _bdot(a_ref[...], b_ref[...], dims)

        @pl.when(kk == nk - 1)
        def _():
            o_ref[...] = acc_ref[...]

    a_spec = pl.BlockSpec((tk, tm), lambda i, j, q: (q, i)) if mode == "tn" else pl.BlockSpec((tm, tk), lambda i, j, q: (i, q))
    b_spec = pl.BlockSpec((tn, tk), lambda i, j, q: (j, q)) if mode == "nt" else pl.BlockSpec((tk, tn), lambda i, j, q: (q, j))
    return pl.pallas_call(
        body, name=name, grid=(m // tm, n // tn, nk),
        in_specs=[a_spec, b_spec], out_specs=pl.BlockSpec((tm, tn), lambda i, j, q: (i, j)),
        out_shape=SDS((m, n), F32), scratch_shapes=[pltpu.VMEM((tm, tn), F32)],
        compiler_params=_params(("parallel", "parallel", "arbitrary"), VMEM_MID),
    )(a, b)


def make_mm(name):
    @jax.custom_vjp
    def mm(a, w, grad_slot):
        return _matmul(a, w, "nn", name + "_fwd")

    def fwd(a, w, grad_slot):
        return _matmul(a, w, "nn", name + "_fwd"), (a, w)

    def bwd(res, g):
        a, w = res
        return _matmul(g, w, "nt", name + "_dx"), jnp.zeros_like(w), _matmul(a, g, "tn", name + "_dw")

    mm.defvjp(fwd, bwd)
    return mm


PROJ_ROWS = 256


def _proj_fwd_call(a, ws, name):
    m, k = a.shape
    tm = _tile(m, PROJ_ROWS)
    nw = len(ws)

    def body(*refs):
        ab = refs[0][...].astype(BF16)
        for w_ref, o_ref in zip(refs[1:1 + nw], refs[1 + nw:]):
            o_ref[...] = lax.dot_general(ab, w_ref[...], NN, preferred_element_type=F32)

    return pl.pallas_call(
        body, name=name, grid=(m // tm,),
        in_specs=[pl.BlockSpec((tm, k), lambda i: (i, 0))] + [pl.BlockSpec(w.shape, lambda i: (0, 0)) for w in ws],
        out_specs=[pl.BlockSpec((tm, w.shape[1]), lambda i: (i, 0)) for w in ws],
        out_shape=[SDS((m, w.shape[1]), F32) for w in ws],
        compiler_params=_params(("parallel",), VMEM_BIG),
    )(a, *ws)


def _proj_dx_call(gs, ws, name):
    m = gs[0].shape[0]
    k = ws[0].shape[0]
    tm = _tile(m, PROJ_ROWS)
    nw = len(ws)

    def body(*refs):
        acc = None
        for g_ref, w_ref in zip(refs[:nw], refs[nw:2 * nw]):
            part = _bdot(g_ref[...], w_ref[...], NT)
            acc = part if acc is None else acc + part
        refs[2 * nw][...] = acc

    return pl.pallas_call(
        body, name=name, grid=(m // tm,),
        in_specs=[pl.BlockSpec((tm, g.shape[1]), lambda i: (i, 0)) for g in gs] + [pl.BlockSpec(w.shape, lambda i: (0, 0)) for w in ws],
        out_specs=pl.BlockSpec((tm, k), lambda i: (i, 0)), out_shape=SDS((m, k), F32),
        compiler_params=_params(("parallel",), VMEM_BIG),
    )(*gs, *ws)


def make_proj(name):
    @jax.custom_vjp
    def proj(a, ws, grad_slots):
        return tuple(_proj_fwd_call(a, ws, name + "_fwd"))

    def fwd(a, ws, grad_slots):
        return tuple(_proj_fwd_call(a, ws, name + "_fwd")), (a, ws)

    def bwd(res, gs):
        a, ws = res
        dws = tuple(_matmul(a, g, "tn", "%s_dw%d" % (name, i)) for i, g in enumerate(gs))
        return _proj_dx_call(tuple(gs), ws, name + "_dx"), tuple(jnp.zeros_like(w) for w in ws), dws

    proj.defvjp(fwd, bwd)
    return proj


def make_rowwise(f, name, tm, n_rows, n_params, vmem=VMEM_MID, pass_first=False):
    def specs_of(arrs, blocked):
        if blocked:
            return [pl.BlockSpec((tm, a.shape[1]), lambda i: (i, 0)) for a in arrs]
        return [pl.BlockSpec(a.shape, lambda i: (0, 0)) for a in arrs]

    def out_structs(rows, params):
        blk = [SDS((tm, r.shape[1]), r.dtype) for r in rows] + [SDS(p.shape, p.dtype) for p in params]
        return jax.eval_shape(f, *blk)

    def run_fwd(rows, params):
        L = rows[0].shape[0]
        outs = out_structs(rows, params)

        def body(*refs):
            ins = [r[...] for r in refs[:n_rows + n_params]]
            res = f(*ins)
            for o_ref, val in zip(refs[n_rows + n_params:], res):
                o_ref[...] = val

        return pl.pallas_call(
            body, name=name + "_fwd", grid=(L // tm,),
            in_specs=specs_of(rows, True) + specs_of(params, False),
            out_specs=[pl.BlockSpec((tm, o.shape[1]), lambda i: (i, 0)) for o in outs],
            out_shape=[SDS((L, o.shape[1]), o.dtype) for o in outs],
            compiler_params=_params(("parallel",), vmem),
        )(*rows, *params)

    def run_bwd(rows, params, gs):
        L = rows[0].shape[0]
        n_g = len(gs)

        def body(*refs):
            i = pl.program_id(0)
            ins = [r[...] for r in refs[:n_rows + n_params]]
            cts = tuple(r[...] for r in refs[n_rows + n_params:n_rows + n_params + n_g])
            outs = refs[n_rows + n_params + n_g:]
            _, vjp = jax.vjp(f, *ins)
            grads = vjp(cts[:-1] if pass_first else cts)
            if pass_first:
                grads = (grads[0] + cts[-1],) + tuple(grads[1:])
            for o_ref, val in zip(outs[:n_rows], grads[:n_rows]):
                o_ref[...] = val

            if n_params:
                @pl.when(i == 0)
                def _():
                    for o_ref in outs[n_rows:]:
                        o_ref[...] = jnp.zeros_like(o_ref)
                for o_ref, val in zip(outs[n_rows:], grads[n_rows:]):
                    o_ref[...] += val

        res = pl.pallas_call(
            body, name=name + "_bwd", grid=(L // tm,),
            in_specs=specs_of(rows, True) + specs_of(params, False) + specs_of(gs, True),
            out_specs=specs_of(rows, True) + specs_of(params, False),
            out_shape=[SDS(r.shape, r.dtype) for r in rows] + [SDS(p.shape, p.dtype) for p in params],
            compiler_params=_params(("arbitrary",), vmem),
        )(*rows, *params, *gs)
        return tuple(res[:n_rows]), tuple(res[n_rows:])

    def outputs(rows, params):
        outs = tuple(run_fwd(rows, params))
        return outs + (rows[0],) if pass_first else outs

    @jax.custom_vjp
    def op(rows, params):
        return outputs(rows, params)

    def fwd(rows, params):
        return outputs(rows, params), (rows, params)

    def bwd(res, gs):
        rows, params = res
        return run_bwd(rows, params, tuple(gs))

    op.defvjp(fwd, bwd)
    op.run_fwd, op.run_bwd = run_fwd, run_bwd
    return op


def make_residual(name, tm):
    full = make_rowwise(_f_res, name, tm, 2, 2)
    branch = make_rowwise(lambda y, gate, bgate: ((gate + bgate) * y,), name + "_branch", tm, 1, 2)

    @jax.custom_vjp
    def op(x, y, gate, bgate):
        return full.run_fwd((x, y), (gate, bgate))[0]

    def fwd(x, y, gate, bgate):
        return full.run_fwd((x, y), (gate, bgate))[0], (y, gate, bgate)

    def bwd(res, g):
        y, gate, bgate = res
        (dy,), (dgate, dbgate) = branch.run_bwd((y,), (gate, bgate), (g,))
        return g, dy, dgate, dbgate

    op.defvjp(fwd, bwd)
    return op


def _s5_scan_rows(xr_ref, xi_ref, ar, ai, x0r, x0i, tl, reverse=False):
    n = xr_ref.shape[1]
    T = SUBLANES
    row = lax.broadcasted_iota(jnp.int32, (T, n), 0)
    pr, pi = [ar], [ai]
    for _ in range(T - 1):
        pr, pi = pr + [pr[-1] * ar - pi[-1] * ai], pi + [pr[-1] * ai + pi[-1] * ar]
    levels = []
    for d in (1, 2, 4):
        mask = (row < T - d) if reverse else (row >= d)
        levels.append((T - d if reverse else d, jnp.where(mask, pr[d - 1], 0.0), jnp.where(mask, pi[d - 1], 0.0)))
    cr = jnp.zeros((T, n), F32)
    ci = jnp.zeros((T, n), F32)
    for r in range(T):
        k = (T - r) if reverse else (r + 1)
        cr = jnp.where(row == r, pr[k - 1], cr)
        ci = jnp.where(row == r, pi[k - 1], ci)
    nt = tl // T
    last = 0 if reverse else T - 1

    def step(t, carry):
        sr, si = carry
        base = pl.multiple_of((nt - 1 - t if reverse else t) * T, T)
        br = xr_ref[pl.ds(base, T), :]
        bi = xi_ref[pl.ds(base, T), :]
        for shift, mr, mi in levels:
            qr = pltpu.roll(br, shift, 0)
            qi = pltpu.roll(bi, shift, 0)
            br, bi = br + (mr * qr - mi * qi), bi + (mr * qi + mi * qr)
        xr = br + (cr * sr - ci * si)
        xi = bi + (cr * si + ci * sr)
        xr_ref[pl.ds(base, T), :] = xr
        xi_ref[pl.ds(base, T), :] = xi
        return xr[last:last + 1, :], xi[last:last + 1, :]
    return lax.fori_loop(0, nt, step, (x0r, x0i))


def _s5_fwd_call(u, bre, bim, cre, cim, a, tl):
    L, e = u.shape
    nb = e // LANES
    ns = bre.shape[2]
    nc = L // tl

    def body(u_ref, bre_ref, bim_ref, cre_ref, cim_ref, a_ref, ys_ref, xb_ref, sr_ref, si_ref, xr_ref, xi_ref, carry_ref):
        c = pl.program_id(1)

        @pl.when(c == 0)
        def _():
            carry_ref[...] = jnp.zeros_like(carry_ref)
        xb_ref[0, 0] = carry_ref[...]
        ub = u_ref[...]
        xr_ref[...] = _bdot(ub, bre_ref[0])
        xi_ref[...] = _bdot(ub, bim_ref[0])
        ar = a_ref[0, 0:1, :]
        ai = a_ref[0, 1:2, :]
        xr, xi = _s5_scan_rows(xr_ref, xi_ref, ar, ai, carry_ref[0:1, :], carry_ref[1:2, :], tl)
        carry_ref[0:1, :] = xr
        carry_ref[1:2, :] = xi
        sr = xr_ref[...].astype(BF16)
        si = xi_ref[...].astype(BF16)
        sr_ref[...] = sr
        si_ref[...] = si
        ys_ref[...] = _bdot(sr, cre_ref[0]) - _bdot(si, cim_ref[0])

    return pl.pallas_call(
        body, name="s5_core_fwd", grid=(nb, nc),
        in_specs=[pl.BlockSpec((tl, LANES), lambda j, c: (c, j)),
                  pl.BlockSpec((1, LANES, ns), lambda j, c: (j, 0, 0)), pl.BlockSpec((1, LANES, ns), lambda j, c: (j, 0, 0)),
                  pl.BlockSpec((1, ns, LANES), lambda j, c: (j, 0, 0)), pl.BlockSpec((1, ns, LANES), lambda j, c: (j, 0, 0)),
                  pl.BlockSpec((1, SUBLANES, ns), lambda j, c: (j, 0, 0))],
        out_specs=[pl.BlockSpec((tl, LANES), lambda j, c: (c, j)),
                   pl.BlockSpec((1, 1, SUBLANES, ns), lambda j, c: (j, c, 0, 0)),
                   pl.BlockSpec((tl, ns), lambda j, c: (c, j)), pl.BlockSpec((tl, ns), lambda j, c: (c, j))],
        out_shape=[SDS((L, e), F32), SDS((nb, nc, SUBLANES, ns), F32), SDS((L, nb * ns), BF16), SDS((L, nb * ns), BF16)],
        scratch_shapes=[pltpu.VMEM((tl, ns), F32), pltpu.VMEM((tl, ns), F32), pltpu.VMEM((SUBLANES, ns), F32)],
        compiler_params=_params(("arbitrary", "arbitrary"), VMEM_MID),
    )(u, bre, bim, cre, cim, a)


def _s5_bwd_call(u, dys, du_other, bre, bim, cre, cim, a, xb, sr, si, tl):
    L, e = u.shape
    nb = e // LANES
    ns = bre.shape[2]
    nc = L // tl

    def body(u_ref, dys_ref, duo_ref, bre_ref, bim_ref, cre_ref, cim_ref, a_ref, xb_ref, sr_ref, si_ref,
             du_ref, dbre_ref, dbim_ref, dcre_ref, dcim_ref, da_ref,
             gr_ref, gi_ref, gcarry_ref):
        c = pl.program_id(1)

        @pl.when(c == 0)
        def _():
            gcarry_ref[...] = jnp.zeros_like(gcarry_ref)
            dbre_ref[...] = jnp.zeros_like(dbre_ref)
            dbim_ref[...] = jnp.zeros_like(dbim_ref)
            dcre_ref[...] = jnp.zeros_like(dcre_ref)
            dcim_ref[...] = jnp.zeros_like(dcim_ref)
            da_ref[...] = jnp.zeros_like(da_ref)

        ub = u_ref[...]
        dy = dys_ref[...]
        ar = a_ref[0, 0:1, :]
        ai = a_ref[0, 1:2, :]
        x0r = xb_ref[0, 0, 0:1, :]
        x0i = xb_ref[0, 0, 1:2, :]
        dcre_ref[0] += _bdot(sr_ref[...], dy, TN)
        dcim_ref[0] -= _bdot(si_ref[...], dy, TN)
        gr_ref[...] = _bdot(dy, cre_ref[0], NT)
        gi_ref[...] = -_bdot(dy, cim_ref[0], NT)

        g0r, g0i = _s5_scan_rows(gr_ref, gi_ref, ar, -ai, gcarry_ref[0:1, :], gcarry_ref[1:2, :], tl, reverse=True)
        gcarry_ref[0:1, :] = g0r
        gcarry_ref[1:2, :] = g0i
        row = lax.broadcasted_iota(jnp.int32, (tl, ns), 0)
        gr = gr_ref[...]
        gi = gi_ref[...]
        xpr = jnp.where(row == 0, x0r, pltpu.roll(sr_ref[...].astype(F32), 1, 0))
        xpi = jnp.where(row == 0, x0i, pltpu.roll(si_ref[...].astype(F32), 1, 0))
        da_ref[0, 0:1, :] += jnp.sum(gr * xpr + gi * xpi, axis=0, keepdims=True)
        da_ref[0, 1:2, :] += jnp.sum(gi * xpr - gr * xpi, axis=0, keepdims=True)
        du_ref[...] = (_bdot(gr, bre_ref[0], NT) + _bdot(gi, bim_ref[0], NT)) + duo_ref[...]
        dbre_ref[0] += _bdot(ub, gr, TN)
        dbim_ref[0] += _bdot(ub, gi, TN)

    rev = lambda c: nc - 1 - c
    return pl.pallas_call(
        body, name="s5_core_bwd", grid=(nb, nc),
        in_specs=[pl.BlockSpec((tl, LANES), lambda j, c: (rev(c), j)), pl.BlockSpec((tl, LANES), lambda j, c: (rev(c), j)),
                  pl.BlockSpec((tl, LANES), lambda j, c: (rev(c), j)),
                  pl.BlockSpec((1, LANES, ns), lambda j, c: (j, 0, 0)), pl.BlockSpec((1, LANES, ns), lambda j, c: (j, 0, 0)),
                  pl.BlockSpec((1, ns, LANES), lambda j, c: (j, 0, 0)), pl.BlockSpec((1, ns, LANES), lambda j, c: (j, 0, 0)),
                  pl.BlockSpec((1, SUBLANES, ns), lambda j, c: (j, 0, 0)),
                  pl.BlockSpec((1, 1, SUBLANES, ns), lambda j, c: (j, rev(c), 0, 0)),
                  pl.BlockSpec((tl, ns), lambda j, c: (rev(c), j)), pl.BlockSpec((tl, ns), lambda j, c: (rev(c), j))],
        out_specs=[pl.BlockSpec((tl, LANES), lambda j, c: (rev(c), j)),
                   pl.BlockSpec((1, LANES, ns), lambda j, c: (j, 0, 0)), pl.BlockSpec((1, LANES, ns), lambda j, c: (j, 0, 0)),
                   pl.BlockSpec((1, ns, LANES), lambda j, c: (j, 0, 0)), pl.BlockSpec((1, ns, LANES), lambda j, c: (j, 0, 0)),
                   pl.BlockSpec((1, SUBLANES, ns), lambda j, c: (j, 0, 0))],
        out_shape=[SDS((L, e), F32), SDS(bre.shape, F32), SDS(bim.shape, F32), SDS(cre.shape, F32), SDS(cim.shape, F32),
                   SDS(a.shape, F32)],
        scratch_shapes=[pltpu.VMEM((tl, ns), F32) for _ in range(2)] + [pltpu.VMEM((SUBLANES, ns), F32)],
        compiler_params=_params(("arbitrary", "arbitrary"), VMEM_MID),
    )(u, dys, du_other, bre, bim, cre, cim, a, xb, sr, si)


def make_s5_core(tl):
    @jax.custom_vjp
    def s5_core(u, bre, bim, cre, cim, a):
        return _s5_fwd_call(u, bre, bim, cre, cim, a, tl)[0], u

    def fwd(u, bre, bim, cre, cim, a):
        ys, xb, sr, si = _s5_fwd_call(u, bre, bim, cre, cim, a, tl)
        return (ys, u), (u, bre, bim, cre, cim, a, xb, sr, si)

    def bwd(res, cts):
        u, bre, bim, cre, cim, a, xb, sr, si = res
        dys, du_other = cts
        return tuple(_s5_bwd_call(u, dys, du_other, bre, bim, cre, cim, a, xb, sr, si, tl))

    s5_core.defvjp(fwd, bwd)
    return s5_core


def _s5_block_params(lam_re, lam_im, log_dt, b_re, b_im, c_re, c_im):
    dt = jnp.exp(log_dt)[:, None]
    mag = jnp.exp(lam_re * dt)
    ab_re = mag * jnp.cos(lam_im * dt)
    ab_im = mag * jnp.sin(lam_im * dt)
    den = lam_re * lam_re + lam_im * lam_im
    nr = ab_re - 1.0
    ni = ab_im
    q_re = (nr * lam_re + ni * lam_im) / den
    q_im = (ni * lam_re - nr * lam_im) / den
    bb_re = q_re[..., None] * b_re - q_im[..., None] * b_im
    bb_im = q_re[..., None] * b_im + q_im[..., None] * b_re
    nb = S5_GROUPS // S5_GB
    eye = jnp.eye(S5_GB, dtype=F32)

    def bdiag_in(bb):
        t = bb.reshape(nb, S5_GB, S5_STATE, S5_GROUP)
        t = jnp.einsum("jgpm,gh->jgmhp", t, eye)
        return t.reshape(nb, S5_GB * S5_GROUP, S5_GB * S5_STATE)

    def bdiag_out(cc):
        t = cc.reshape(nb, S5_GB, S5_GROUP, S5_STATE)
        t = jnp.einsum("jgmp,gh->jgphm", t, eye)
        return t.reshape(nb, S5_GB * S5_STATE, S5_GB * S5_GROUP)

    a = jnp.stack([ab_re.reshape(nb, S5_GB * S5_STATE), ab_im.reshape(nb, S5_GB * S5_STATE)], axis=1)
    a = jnp.concatenate([a, jnp.zeros((nb, SUBLANES - 2, S5_GB * S5_STATE), F32)], axis=1)
    return bdiag_in(bb_re), bdiag_in(bb_im), bdiag_out(c_re), bdiag_out(c_im), a


def _shift_down(x, s, row):
    if s == 0:
        return x
    return jnp.where(row >= s, pltpu.roll(x, s, 0), 0.0)


def _shift_up(x, s, row, n):
    if s == 0:
        return x
    return jnp.where(row < n - s, pltpu.roll(x, n - s, 0), 0.0)


def _conv_fwd_call(x, w):
    L, ch = x.shape

    def body(x_ref, w_ref, y_ref):
        xv = x_ref[...]
        row = lax.broadcasted_iota(jnp.int32, xv.shape, 0)
        acc = jnp.zeros_like(xv)
        for j in range(GDN_CONV):
            acc += w_ref[j:j + 1, :] * _shift_down(xv, GDN_CONV - 1 - j, row)
        y_ref[...] = acc

    return pl.pallas_call(
        body, name="gdn_conv_fwd", grid=(ch // LANES,),
        in_specs=[pl.BlockSpec((L, LANES), lambda j: (0, j)), pl.BlockSpec((SUBLANES, LANES), lambda j: (0, j))],
        out_specs=pl.BlockSpec((L, LANES), lambda j: (0, j)), out_shape=SDS((L, ch), F32),
        compiler_params=_params(("parallel",), VMEM_MID),
    )(x, w)


def _conv_bwd_call(x, w, dy):
    L, ch = x.shape

    def body(x_ref, w_ref, dy_ref, dx_ref, dw_ref):
        xv = x_ref[...]
        g = dy_ref[...]
        row = lax.broadcasted_iota(jnp.int32, xv.shape, 0)
        acc = jnp.zeros_like(xv)
        dws = []
        for j in range(GDN_CONV):
            s = GDN_CONV - 1 - j
            acc += w_ref[j:j + 1, :] * _shift_up(g, s, row, L)
            dws.append(jnp.sum(g * _shift_down(xv, s, row), axis=0, keepdims=True))
        dx_ref[...] = acc
        dw_ref[...] = jnp.concatenate(dws + [jnp.zeros((SUBLANES - GDN_CONV, LANES), F32)], axis=0)

    return pl.pallas_call(
        body, name="gdn_conv_bwd", grid=(ch // LANES,),
        in_specs=[pl.BlockSpec((L, LANES), lambda j: (0, j)), pl.BlockSpec((SUBLANES, LANES), lambda j: (0, j)),
                  pl.BlockSpec((L, LANES), lambda j: (0, j))],
        out_specs=[pl.BlockSpec((L, LANES), lambda j: (0, j)), pl.BlockSpec((SUBLANES, LANES), lambda j: (0, j))],
        out_shape=[SDS((L, ch), F32), SDS((SUBLANES, ch), F32)],
        compiler_params=_params(("parallel",), VMEM_MID),
    )(x, w, dy)


@jax.custom_vjp
def gdn_conv(x, w):
    return _conv_fwd_call(x, w)


def _gdn_conv_f(x, w):
    return _conv_fwd_call(x, w), (x, w)


def _gdn_conv_b(res, dy):
    x, w = res
    return tuple(_conv_bwd_call(x, w, dy))


gdn_conv.defvjp(_gdn_conv_f, _gdn_conv_b)


BNN = (((2,), (1,)), ((0,), (0,)))
BNT = (((2,), (2,)), ((0,), (0,)))
BTN = (((1,), (1,)), ((0,), (0,)))
GDN_PREP_BATCH = 8


@jax.custom_vjp
def _known_inverse(a, t):
    return t


def _known_inverse_fwd(a, t):
    return t, t


def _known_inverse_bwd(t, g):
    return -_hdot(_hdot(t, g, _BTN), t, _BNT), jnp.zeros_like(t)


_known_inverse.defvjp(_known_inverse_fwd, _known_inverse_bwd)


def _gdn_prep_math(q, k, v, beta, g, t_saved=None):
    B, C = q.shape[0], q.shape[1]
    ri = lax.broadcasted_iota(jnp.int32, (B, C, C), 1)
    ci = lax.broadcasted_iota(jnp.int32, (B, C, C), 2)
    causal = ri >= ci
    strict = ri > ci
    eye = (ri == ci).astype(F32)
    gb = jnp.broadcast_to(g, (B, C, C))
    g_row = jnp.sum(gb * eye, axis=1, keepdims=True)
    gc_col = jnp.sum(jnp.where(causal, jnp.broadcast_to(g_row, (B, C, C)), 0.0), axis=2, keepdims=True)
    gc_row = jnp.sum(jnp.where(ri <= ci, gb, 0.0), axis=1, keepdims=True)
    decay = jnp.exp(jnp.where(causal, gc_col - gc_row, -jnp.inf))
    kk = _bdot(k, k, BNT)
    a_mat = jnp.where(strict, beta * kk * decay, 0.0)
    t = _unit_lower_inverse(a_mat) if t_saved is None else _known_inverse(a_mat, t_saved)
    e_gc = jnp.exp(gc_col)
    w = _hdot(t, beta * e_gc * k, BNN)
    u = _hdot(t, beta * v, BNN)
    qk = _bdot(q, k, BNT) * decay
    q_dec = q * e_gc
    g_last = gc_col[:, C - 1:C, :]
    k_dec = k * jnp.exp(g_last - gc_col)
    return q_dec, w, u, qk, k_dec, gc_col, t


def _gdn_prep_specs(L):
    C = GDN_CHUNK
    nb = min(GDN_PREP_BATCH, L // C)
    R = nb * C
    ins = [pl.BlockSpec((R, GDN_DK), lambda c, h: (c, h)), pl.BlockSpec((R, GDN_DK), lambda c, h: (c, h)),
           pl.BlockSpec((R, GDN_DV), lambda c, h: (c, h)), pl.BlockSpec((R, LANES), lambda c, h: (c, 0))]
    outs = [pl.BlockSpec((1, R, GDN_DK), lambda c, h: (h, c, 0)), pl.BlockSpec((1, R, GDN_DK), lambda c, h: (h, c, 0)),
            pl.BlockSpec((1, R, GDN_DV), lambda c, h: (h, c, 0)), pl.BlockSpec((1, R, C), lambda c, h: (h, c, 0)),
            pl.BlockSpec((1, R, GDN_DK), lambda c, h: (h, c, 0)), pl.BlockSpec((1, R, 1), lambda c, h: (h, c, 0))]
    t_spec = pl.BlockSpec((1, R, C), lambda c, h: (h, c, 0))
    shapes = [SDS((GDN_HEADS, L, GDN_DK), F32), SDS((GDN_HEADS, L, GDN_DK), F32), SDS((GDN_HEADS, L, GDN_DV), F32),
              SDS((GDN_HEADS, L, C), F32), SDS((GDN_HEADS, L, GDN_DK), F32), SDS((GDN_HEADS, L, 1), F32)]
    return ins, outs, t_spec, shapes, nb


def _chunks(x, nb):
    return x.reshape(nb, x.shape[0] // nb, x.shape[1])


def _head_columns(bg, h):
    lane = lax.broadcasted_iota(jnp.int32, bg.shape, 1)
    beta = jnp.sum(jnp.where(lane == h, bg, 0.0), axis=1, keepdims=True)
    g = jnp.sum(jnp.where(lane == h + GDN_HEADS, bg, 0.0), axis=1, keepdims=True)
    return beta, g


def _gdn_prep_fwd_call(q, k, v, bg):
    L = q.shape[0]
    ins, outs, t_spec, shapes, nb = _gdn_prep_specs(L)

    def body(q_ref, k_ref, v_ref, bg_ref, *o_refs):
        beta, g = _head_columns(bg_ref[...], pl.program_id(1))
        res = _gdn_prep_math(_chunks(q_ref[...], nb), _chunks(k_ref[...], nb), _chunks(v_ref[...], nb),
                             _chunks(beta, nb), _chunks(g, nb))
        for o_ref, val in zip(o_refs, res):
            o_ref[0] = val.reshape(val.shape[0] * val.shape[1], val.shape[2])

    return pl.pallas_call(
        body, name="gdn_prep_fwd", grid=(L // (nb * GDN_CHUNK), GDN_HEADS), in_specs=ins, out_specs=outs + [t_spec],
        out_shape=shapes + [SDS((GDN_HEADS, L, GDN_CHUNK), F32)],
        compiler_params=_params(("parallel", "parallel"), VMEM_MID),
    )(q, k, v, bg)


def _gdn_prep_bwd_call(q, k, v, bg, t, cts):
    L = q.shape[0]
    ins, outs, t_spec, _, nb = _gdn_prep_specs(L)

    def body(q_ref, k_ref, v_ref, bg_ref, t_ref, c0, c1, c2, c3, c4, c5, dq_ref, dk_ref, dv_ref, dbg_ref):
        h = pl.program_id(1)
        beta, g = _head_columns(bg_ref[...], h)
        t_saved = _chunks(t_ref[0], nb)
        _, vjp = jax.vjp(lambda *a: _gdn_prep_math(*a, t_saved=t_saved)[:6], _chunks(q_ref[...], nb), _chunks(k_ref[...], nb),
                         _chunks(v_ref[...], nb), _chunks(beta, nb), _chunks(g, nb))
        dq, dk, dv, db, dg = vjp(tuple(_chunks(c[0], nb) for c in (c0, c1, c2, c3, c4, c5)))
        flat = lambda a: a.reshape(a.shape[0] * a.shape[1], a.shape[2])
        dq_ref[...] = flat(dq)
        dk_ref[...] = flat(dk)
        dv_ref[...] = flat(dv)

        @pl.when(h == 0)
        def _():
            dbg_ref[...] = jnp.zeros_like(dbg_ref)
        lane = lax.broadcasted_iota(jnp.int32, dbg_ref.shape, 1)
        dbg_ref[...] += jnp.where(lane == h, flat(db), 0.0) + jnp.where(lane == h + GDN_HEADS, flat(dg), 0.0)

    return pl.pallas_call(
        body, name="gdn_prep_bwd", grid=(L // (nb * GDN_CHUNK), GDN_HEADS), in_specs=ins + [t_spec] + outs, out_specs=ins,
        out_shape=[SDS(q.shape, F32), SDS(k.shape, F32), SDS(v.shape, F32), SDS(bg.shape, F32)],
        compiler_params=_params(("parallel", "arbitrary"), VMEM_MID),
    )(q, k, v, bg, t, *cts)


@jax.custom_vjp
def gdn_prep(q, k, v, bg):
    return tuple(_gdn_prep_fwd_call(q, k, v, bg)[:6])


def _gdn_prep_f(q, k, v, bg):
    res = _gdn_prep_fwd_call(q, k, v, bg)
    return tuple(res[:6]), (q, k, v, bg, res[6])


def _gdn_prep_b(res, cts):
    return tuple(_gdn_prep_bwd_call(*res, tuple(cts)))


gdn_prep.defvjp(_gdn_prep_f, _gdn_prep_b)


def _gdn_step_math(q_dec, w, u, qk, k_dec, gc, state):
    H, C = q_dec.shape[0], q_dec.shape[1]
    v_new = u - _bdot(w, state, BNN)
    o = _bdot(q_dec, state, BNN) + _bdot(qk, v_new, BNN)
    gl = gc[:, C - 1:C, :]
    new_state = jnp.exp(gl) * state + _bdot(k_dec, v_new, BTN)
    return jnp.concatenate([o[h] for h in range(H)], axis=1), new_state


def _gdn_scan_specs(L, rev):
    C, H = GDN_CHUNK, GDN_HEADS
    nc = L // C
    cc = (lambda c: nc - 1 - c) if rev else (lambda c: c)
    ins = [pl.BlockSpec((H, C, GDN_DK), lambda c: (0, cc(c), 0)), pl.BlockSpec((H, C, GDN_DK), lambda c: (0, cc(c), 0)),
           pl.BlockSpec((H, C, GDN_DV), lambda c: (0, cc(c), 0)), pl.BlockSpec((H, C, C), lambda c: (0, cc(c), 0)),
           pl.BlockSpec((H, C, GDN_DK), lambda c: (0, cc(c), 0)), pl.BlockSpec((H, C, 1), lambda c: (0, cc(c), 0))]
    o_spec = pl.BlockSpec((C, H * GDN_DV), lambda c: (cc(c), 0))
    s_spec = pl.BlockSpec((1, H, GDN_DK, GDN_DV), lambda c: (cc(c), 0, 0, 0))
    return ins, o_spec, s_spec, nc


def _gdn_scan_fwd_call(q_dec, w, u, qk, k_dec, gc):
    L = q_dec.shape[1]
    ins, o_spec, s_spec, nc = _gdn_scan_specs(L, False)

    def body(qd_ref, w_ref, u_ref, qk_ref, kd_ref, gc_ref, o_ref, sin_ref, s_ref):
        c = pl.program_id(0)

        @pl.when(c == 0)
        def _():
            s_ref[...] = jnp.zeros_like(s_ref)
        st = s_ref[...]
        sin_ref[0] = st
        o, ns = _gdn_step_math(qd_ref[...], w_ref[...], u_ref[...], qk_ref[...], kd_ref[...], gc_ref[...], st)
        o_ref[...] = o
        s_ref[...] = ns

    return pl.pallas_call(
        body, name="gdn_scan_fwd", grid=(nc,), in_specs=ins, out_specs=[o_spec, s_spec],
        out_shape=[SDS((L, GDN_HEADS * GDN_DV), F32), SDS((nc, GDN_HEADS, GDN_DK, GDN_DV), F32)],
        scratch_shapes=[pltpu.VMEM((GDN_HEADS, GDN_DK, GDN_DV), F32)],
        compiler_params=_params(("arbitrary",), VMEM_MID),
    )(q_dec, w, u, qk, k_dec, gc)


def _gdn_scan_bwd_call(q_dec, w, u, qk, k_dec, gc, s_in, do):
    L = q_dec.shape[1]
    ins, o_spec, s_spec, nc = _gdn_scan_specs(L, True)

    def body(qd_ref, w_ref, u_ref, qk_ref, kd_ref, gc_ref, sin_ref, do_ref,
             dqd_ref, dw_ref, du_ref, dqk_ref, dkd_ref, dgc_ref, ds_ref):
        c = pl.program_id(0)

        @pl.when(c == 0)
        def _():
            ds_ref[...] = jnp.zeros_like(ds_ref)
        _, vjp = jax.vjp(_gdn_step_math, qd_ref[...], w_ref[...], u_ref[...], qk_ref[...], kd_ref[...], gc_ref[...], sin_ref[0])
        dqd, dw, du, dqk, dkd, dgc, dst = vjp((do_ref[...], ds_ref[...]))
        dqd_ref[...] = dqd
        dw_ref[...] = dw
        du_ref[...] = du
        dqk_ref[...] = dqk
        dkd_ref[...] = dkd
        dgc_ref[...] = dgc
        ds_ref[...] = dst

    return pl.pallas_call(
        body, name="gdn_scan_bwd", grid=(nc,), in_specs=ins + [s_spec, o_spec], out_specs=ins,
        out_shape=[SDS(t.shape, F32) for t in (q_dec, w, u, qk, k_dec, gc)],
        scratch_shapes=[pltpu.VMEM((GDN_HEADS, GDN_DK, GDN_DV), F32)],
        compiler_params=_params(("arbitrary",), VMEM_MID),
    )(q_dec, w, u, qk, k_dec, gc, s_in, do)


@jax.custom_vjp
def gdn_scan(q_dec, w, u, qk, k_dec, gc):
    return _gdn_scan_fwd_call(q_dec, w, u, qk, k_dec, gc)[0]


def _gdn_scan_f(*args):
    o, s_in = _gdn_scan_fwd_call(*args)
    return o, (*args, s_in)


def _gdn_scan_b(res, do):
    return tuple(_gdn_scan_bwd_call(*res, do))


gdn_scan.defvjp(_gdn_scan_f, _gdn_scan_b)


def _silu(x):
    return x * jax.nn.sigmoid(x)


def _gelu_tanh(x):
    return 0.5 * x * (1.0 + jnp.tanh(math.sqrt(2.0 / math.pi) * (x + 0.044715 * (x * x * x))))


def _f_lnmod(x, nw, sc, sh, bsc, bsh):
    xn = x * lax.rsqrt(jnp.mean(x * x, axis=-1, keepdims=True) + NORM_EPS) * nw
    return (xn * (1.0 + (sc + bsc)) + (sh + bsh),)


def _f_s5_act(ys, u, d):
    return (_gelu_tanh(ys + d * u),)


def _f_s5_gate(y2, t, z):
    return (y2 * jax.nn.sigmoid(t) * _silu(z),)


def _f_res(x, y, gate, bgate):
    return (x + (gate + bgate) * y,)


def _heads(x, width, fn):
    return jnp.concatenate([fn(x[:, i * width:(i + 1) * width]) for i in range(x.shape[1] // width)], axis=1)


def _l2n(x):
    return x * lax.rsqrt(jnp.sum(x * x, axis=-1, keepdims=True) + NORM_EPS)


def _f_qnorm(x):
    return (_heads(_silu(x), GDN_DK, _l2n) * (GDN_DK ** -0.5),)


def _f_knorm(x):
    return (_heads(_silu(x), GDN_DK, _l2n),)


def _f_vact(x):
    return (_silu(x),)


def _f_betag(ba, alog, dtb):
    col = lax.broadcasted_iota(jnp.int32, ba.shape, 1)
    t = ba + dtb
    softplus = jnp.maximum(t, 0.0) + jnp.log1p(jnp.exp(-jnp.abs(t)))
    g = -jnp.exp(alog) * softplus
    return (jnp.where(col < GDN_HEADS, jax.nn.sigmoid(ba), jnp.where(col < 2 * GDN_HEADS, g, 0.0)),)


def _f_gdn_post(o, z, nw):
    on = _heads(o, GDN_DV, lambda t: t * lax.rsqrt(jnp.mean(t * t, axis=-1, keepdims=True) + NORM_EPS))
    return (on * nw * _silu(z),)


def _f_loss(x, tgt, fw):
    y = x * lax.rsqrt(jnp.mean(x * x, axis=-1, keepdims=True) + NORM_EPS) * fw
    err = y - tgt
    return (0.5 * jnp.mean(err * err, axis=-1, keepdims=True),)


def _ada_mod_call(c_all, ada_w):
    n = ada_w.shape[2]

    def body(c_ref, w_ref, o_ref):
        ca = _silu(c_ref[...])
        for l in range(ada_w.shape[0]):
            o_ref[l] = _bdot(ca, w_ref[l])

    return pl.pallas_call(body, name="ada_mod", out_shape=SDS((ada_w.shape[0], N_DEV, n), F32),
                          compiler_params=_params(None, VMEM_MID))(c_all, ada_w)


def _ada_grad_call(c_all, dmod):
    nl, _, n = dmod.shape

    def body(c_ref, d_ref, o_ref):
        ca = _silu(c_ref[...])
        for l in range(nl):
            o_ref[l] = _hdot(ca, d_ref[l], TN)

    return pl.pallas_call(body, name="ada_grad", out_shape=SDS((nl, c_all.shape[1], n), F32),
                          compiler_params=_params(None, VMEM_MID))(c_all, dmod)


ADAM_ROWS = 512


def _adamw(g, w, m, v):
    m2 = ADAM_B1 * m + (1.0 - ADAM_B1) * g
    v2 = ADAM_B2 * v + (1.0 - ADAM_B2) * (g * g)
    m_hat = m2 / (1.0 - ADAM_B1 ** ADAM_STEP)
    v_hat = v2 / (1.0 - ADAM_B2 ** ADAM_STEP)
    return g, -ADAM_LR * (m_hat / (jnp.sqrt(v_hat) + ADAM_EPS) + ADAM_WD * w), m2, v2


def _adam_call(gs, w, m, v, name, rows=None):
    n, r, cols = gs.shape
    rows = rows or ADAM_ROWS

    def body(g_ref, w_ref, m_ref, v_ref, go_ref, d_ref, mo_ref, vo_ref):
        g = g_ref[0].astype(F32)
        for s in range(1, n):
            g = g + g_ref[s].astype(F32)
        for o_ref, val in zip((go_ref, d_ref, mo_ref, vo_ref), _adamw(g, w_ref[...], m_ref[...], v_ref[...])):
            o_ref[...] = val

    blk = pl.BlockSpec((rows, cols), lambda i: (i, 0))
    return pl.pallas_call(
        body, name=name, grid=(r // rows,),
        in_specs=[pl.BlockSpec((n, rows, cols), lambda i: (0, i, 0)), blk, blk, blk],
        out_specs=[blk, blk, blk, blk], out_shape=[SDS((r, cols), F32)] * 4,
        compiler_params=_params(("parallel",), VMEM_MID),
    )(gs, w, m, v)


def _sum_call(gs, name, rows):
    n, r, _ = gs.shape

    def body(g_ref, o_ref):
        g = g_ref[0].astype(F32)
        for s in range(1, n):
            g = g + g_ref[s].astype(F32)
        o_ref[...] = g

    return pl.pallas_call(
        body, name=name, grid=(r // rows,),
        in_specs=[pl.BlockSpec((n, rows, LANES), lambda i: (0, i, 0))],
        out_specs=pl.BlockSpec((rows, LANES), lambda i: (i, 0)), out_shape=SDS((r, LANES), F32),
        compiler_params=_params(("parallel",), VMEM_MID),
    )(gs)


def _allgather_call(x_shard, name, in_hbm):
    m_per, n = x_shard.shape

    def body(x_ref, out_ref, send_sems, recv_sems, local_sem):
        x, y, c = lax.axis_index("x"), lax.axis_index("y"), lax.axis_index("c")
        me, sibling = (x, y, c), (x, y, 1 - c)
        chips = [(1 - x, y), (x, 1 - y), (1 - x, 1 - y)]

        def rows(px, py, pc):
            return out_ref.at[pl.ds((4 * px + 2 * py + pc) * m_per, m_per), :]

        def copy(k, block, to, src=None):
            return pltpu.make_async_remote_copy(
                src_ref=rows(*block) if src is None else src, dst_ref=rows(*block),
                send_sem=send_sems.at[k], recv_sem=recv_sems.at[k], device_id=to, device_id_type=pl.DeviceIdType.MESH)

        mine = pltpu.make_async_copy(x_ref, rows(*me), local_sem)
        mine.start()
        first = [copy(0, me, sibling, src=x_ref)]
        first += [copy(1 + j, me, (*chip, c), src=x_ref) for j, chip in enumerate(chips)]
        for cp in first:
            cp.start()
        passed = [copy(4 + j, (*chip, c), sibling) for j, chip in enumerate(chips)]
        for j, chip in enumerate(chips):
            copy(1 + j, (*chip, c), me).wait_recv()
            passed[j].start()
        copy(0, sibling, me).wait_recv()
        for j, chip in enumerate(chips):
            copy(4 + j, (*chip, 1 - c), me).wait_recv()
        for cp in first + passed:
            cp.wait_send()
        mine.wait()

    space = pl.ANY if in_hbm else pltpu.VMEM
    return pl.pallas_call(
        body, name=name, out_shape=SDS((N_DEV * m_per, n), x_shard.dtype),
        in_specs=[pl.BlockSpec(memory_space=space)], out_specs=pl.BlockSpec(memory_space=space),
        scratch_shapes=[pltpu.SemaphoreType.DMA((7,)), pltpu.SemaphoreType.DMA((7,)), pltpu.SemaphoreType.DMA],
        compiler_params=_params(None, None if in_hbm else VMEM_BIG),
    )(x_shard)


def _gather_weights_call(shards, name):
    nw = len(shards)

    def body(*refs):
        x_refs, out_refs = refs[:nw], refs[nw:2 * nw]
        send_sems, recv_sems, local_sems = refs[2 * nw:]
        x, y, c = lax.axis_index("x"), lax.axis_index("y"), lax.axis_index("c")
        me, sibling = (x, y, c), (x, y, 1 - c)
        chips = [(1 - x, y), (x, 1 - y), (1 - x, 1 - y)]

        def slot(w, px, py, pc):
            return out_refs[w].at[4 * px + 2 * py + pc]

        def copy(w, k, block, to, src=None):
            dst = slot(w, *block)
            return pltpu.make_async_remote_copy(
                src_ref=dst if src is None else src, dst_ref=dst, send_sem=send_sems.at[7 * w + k],
                recv_sem=recv_sems.at[7 * w + k], device_id=to, device_id_type=pl.DeviceIdType.MESH)

        mines = [pltpu.make_async_copy(x_refs[w], slot(w, *me), local_sems.at[w]) for w in range(nw)]
        for cp in mines:
            cp.start()
        first = [copy(w, 0, me, sibling, src=x_refs[w]) for w in range(nw)]
        first += [copy(w, 1 + j, me, (*chip, c), src=x_refs[w]) for w in range(nw) for j, chip in enumerate(chips)]
        for cp in first:
            cp.start()
        passed = []
        for w in range(nw):
            for j, chip in enumerate(chips):
                copy(w, 1 + j, (*chip, c), me).wait_recv()
                fwd = copy(w, 4 + j, (*chip, c), sibling)
                fwd.start()
                passed.append(fwd)
        for w in range(nw):
            copy(w, 0, sibling, me).wait_recv()
            for j, chip in enumerate(chips):
                copy(w, 4 + j, (*chip, 1 - c), me).wait_recv()
        for cp in first + passed:
            cp.wait_send()
        for cp in mines:
            cp.wait()

    hbm = pl.BlockSpec(memory_space=pl.ANY)
    return pl.pallas_call(
        body, name=name, out_shape=[SDS((N_DEV,) + s.shape, s.dtype) for s in shards],
        in_specs=[hbm] * nw, out_specs=[hbm] * nw,
        scratch_shapes=[pltpu.SemaphoreType.DMA((7 * nw,)), pltpu.SemaphoreType.DMA((7 * nw,)), pltpu.SemaphoreType.DMA((nw,))],
    )(*shards)


def _pair_exchange_call(grads, name):
    nw = len(grads)

    def body(*refs):
        g_refs, got_refs = refs[:nw], refs[nw:2 * nw]
        send_sems, recv_sems = refs[2 * nw:]
        x, y, c = lax.axis_index("x"), lax.axis_index("y"), lax.axis_index("c")
        copies = []
        for w in range(nw):
            for j in range(4):
                give = pltpu.make_async_remote_copy(
                    src_ref=g_refs[w].at[2 * j + 1 - c], dst_ref=got_refs[w].at[j], send_sem=send_sems.at[4 * w + j],
                    recv_sem=recv_sems.at[4 * w + j], device_id=(x, y, 1 - c), device_id_type=pl.DeviceIdType.MESH)
                give.start()
                copies.append(give)
        for cp in copies:
            cp.wait()

    hbm = pl.BlockSpec(memory_space=pl.ANY)
    return pl.pallas_call(
        body, name=name, out_shape=[SDS((4,) + g.shape[1:], g.dtype) for g in grads], in_specs=[hbm] * nw, out_specs=[hbm] * nw,
        scratch_shapes=[pltpu.SemaphoreType.DMA((4 * nw,)), pltpu.SemaphoreType.DMA((4 * nw,))],
    )(*grads)


def _chip_exchange_call(parts, name):
    nw = len(parts)

    def body(*refs):
        p_refs, out_refs = refs[:nw], refs[nw:2 * nw]
        send_sems, recv_sems = refs[2 * nw:]
        x, y, c = lax.axis_index("x"), lax.axis_index("y"), lax.axis_index("c")
        chips = [(1 - x, y), (x, 1 - y), (1 - x, 1 - y)]
        copies = []
        for w in range(nw):
            for j, (px, py) in enumerate(chips):
                give = pltpu.make_async_remote_copy(
                    src_ref=p_refs[w].at[2 * px + py], dst_ref=out_refs[w].at[j], send_sem=send_sems.at[3 * w + j],
                    recv_sem=recv_sems.at[3 * w + j], device_id=(px, py, c), device_id_type=pl.DeviceIdType.MESH)
                give.start()
                copies.append(give)
        for cp in copies:
            cp.wait()

    hbm = pl.BlockSpec(memory_space=pl.ANY)
    return pl.pallas_call(
        body, name=name, out_shape=[SDS((3,) + p.shape[1:], p.dtype) for p in parts], in_specs=[hbm] * nw, out_specs=[hbm] * nw,
        scratch_shapes=[pltpu.SemaphoreType.DMA((3 * nw,)), pltpu.SemaphoreType.DMA((3 * nw,))],
    )(*parts)


_HBM = pl.BlockSpec(memory_space=pltpu.HBM)
_SEM = pl.BlockSpec(memory_space=pltpu.SEMAPHORE)
_DATAFLOW = pltpu.SideEffectType.DATAFLOW_SIDE_EFFECTING


def _spread_start_call(srcs, per_peer, name, after):
    nw = len(srcs)
    lands = [lax.empty((N_DEV,) + (s.shape[1:] if per_peer else s.shape), s.dtype) for s in srcs]

    def body(*refs):
        src_refs, land_refs = refs[:nw], refs[nw:2 * nw]
        send_sems, recv_sems, token = refs[2 * nw + 1], refs[2 * nw + 2], refs[-1]
        x, y, c = lax.axis_index("x"), lax.axis_index("y"), lax.axis_index("c")
        me = 4 * x + 2 * y + c
        for w in range(nw):
            for k in range(1, N_DEV):
                px = 1 - x if k & 4 else x
                py = 1 - y if k & 2 else y
                pc = 1 - c if k & 1 else c
                src = src_refs[w].at[4 * px + 2 * py + pc] if per_peer else src_refs[w]
                pltpu.make_async_remote_copy(
                    src_ref=src, dst_ref=land_refs[w].at[me], send_sem=send_sems.at[w], recv_sem=recv_sems.at[w],
                    device_id=(px, py, pc), device_id_type=pl.DeviceIdType.MESH).start()
        token[...] = jnp.zeros_like(token)

    hbm = lambda a: pltpu.with_memory_space_constraint(a, pltpu.HBM)
    res = pl.pallas_call(
        body, name=name,
        out_shape=(pltpu.SemaphoreType.DMA((nw,)), pltpu.SemaphoreType.DMA((nw,)))
        + tuple(pltpu.HBM(s.shape, s.dtype) for s in srcs) + tuple(pltpu.HBM(l.shape, l.dtype) for l in lands)
        + (SDS((SUBLANES, LANES), F32),),
        in_specs=[_HBM] * (2 * nw) + [pl.BlockSpec(memory_space=pl.ANY)],
        out_specs=(_SEM, _SEM) + (_HBM,) * (2 * nw) + (pl.BlockSpec(memory_space=pltpu.VMEM),),
        input_output_aliases={i: i + 2 for i in range(2 * nw)},
        compiler_params=pltpu.CompilerParams(has_side_effects=_DATAFLOW),
    )(*[hbm(s) for s in srcs], *[hbm(l) for l in lands], after)
    return res[0], res[1], res[2:2 + nw], res[2 + nw:2 + 2 * nw], res[-1]


def _spread_wait_call(send_sems, recv_sems, srcs, lands, after, name):
    nw = len(lands)

    def body(*refs):
        land_refs = refs[nw:2 * nw]
        s_sems, r_sems = refs[2 * nw], refs[2 * nw + 1]
        x, y, c = lax.axis_index("x"), lax.axis_index("y"), lax.axis_index("c")
        for w in range(nw):
            seven = land_refs[w].at[pl.ds(0, N_DEV - 1)]
            all_seven = pltpu.make_async_remote_copy(
                src_ref=seven, dst_ref=seven, send_sem=s_sems.at[w], recv_sem=r_sems.at[w],
                device_id=(x, y, c), device_id_type=pl.DeviceIdType.MESH)
            all_seven.wait_send()
            all_seven.wait_recv()

    res = pl.pallas_call(
        body, name=name,
        out_shape=tuple(pltpu.HBM(s.shape, s.dtype) for s in srcs) + tuple(pltpu.HBM(l.shape, l.dtype) for l in lands),
        in_specs=[_HBM] * (2 * nw) + [_SEM, _SEM, pl.BlockSpec(memory_space=pl.ANY)], out_specs=(_HBM,) * (2 * nw),
        input_output_aliases={i: i for i in range(2 * nw)},
        compiler_params=pltpu.CompilerParams(has_side_effects=_DATAFLOW),
    )(*srcs, *lands, send_sems, recv_sems, after)
    return res[:nw], res[nw:]


def _pair_sum_call(g, got, core, name):
    _, k, n = got.shape
    tr = _tile(k, 256)

    def body(c_ref, g_ref, got_ref, o_ref):
        o_ref[...] = (g_ref[...] + got_ref[...]).astype(o_ref.dtype)

    spec = pltpu.PrefetchScalarGridSpec(
        num_scalar_prefetch=1, grid=(4, k // tr),
        in_specs=[pl.BlockSpec((1, tr, n), lambda j, i, c: (2 * j + c[0], i, 0)), pl.BlockSpec((1, tr, n), lambda j, i, c: (j, i, 0))],
        out_specs=pl.BlockSpec((1, tr, n), lambda j, i, c: (j, i, 0)))
    return pl.pallas_call(body, name=name, grid_spec=spec, out_shape=SDS(got.shape, BF16),
                          compiler_params=_params(("parallel", "parallel"), VMEM_MID))(core, g, got)


def _adam_own_call(pair, chip, recv, w, m, v, name, rows):
    _, r, cols = recv.shape

    def body(chip_ref, p_ref, g_ref, w_ref, m_ref, v_ref, go_ref, d_ref, mo_ref, vo_ref):
        g = ((p_ref[0].astype(F32) + g_ref[0].astype(F32)) + g_ref[1].astype(F32)) + g_ref[2].astype(F32)
        for o_ref, val in zip((go_ref, d_ref, mo_ref, vo_ref), _adamw(g, w_ref[...], m_ref[...], v_ref[...])):
            o_ref[...] = val

    blk = pl.BlockSpec((rows, cols), lambda i, s: (i, 0))
    spec = pltpu.PrefetchScalarGridSpec(
        num_scalar_prefetch=1, grid=(r // rows,),
        in_specs=[pl.BlockSpec((1, rows, cols), lambda i, s: (s[0], i, 0)), pl.BlockSpec((3, rows, cols), lambda i, s: (0, i, 0)),
                  blk, blk, blk],
        out_specs=[blk, blk, blk, blk])
    return pl.pallas_call(body, name=name, grid_spec=spec, out_shape=[SDS((r, cols), F32)] * 4,
                          compiler_params=_params(("parallel",), VMEM_MID))(chip, pair, recv, w, m, v)


def _join_cols_call(w8, name):
    _, k, n = w8.shape
    tk = _tile(k, 256)

    def body(w_ref, o_ref):
        for s in range(N_DEV):
            o_ref[:, n * s:n * (s + 1)] = w_ref[s]

    return pl.pallas_call(body, name=name, grid=(k // tk,), in_specs=[pl.BlockSpec((N_DEV, tk, n), lambda i: (0, i, 0))],
                          out_specs=pl.BlockSpec((tk, N_DEV * n), lambda i: (i, 0)), out_shape=SDS((k, N_DEV * n), w8.dtype),
                          compiler_params=_params(("parallel",), VMEM_MID))(w8)


def _split_cols_call(g, name, dtype):
    k, n8 = g.shape
    n = n8 // N_DEV
    tk = _tile(k, 256)

    def body(g_ref, o_ref):
        for s in range(N_DEV):
            o_ref[s] = g_ref[:, n * s:n * (s + 1)].astype(dtype)

    return pl.pallas_call(body, name=name, grid=(k // tk,), in_specs=[pl.BlockSpec((tk, n8), lambda i: (i, 0))],
                          out_specs=pl.BlockSpec((N_DEV, tk, n), lambda i: (0, i, 0)), out_shape=SDS((N_DEV, k, n), dtype),
                          compiler_params=_params(("parallel",), VMEM_MID))(g)


def _pack(parts, rows_multiple):
    flat = jnp.concatenate([p.reshape(-1) for p in parts])
    unit = rows_multiple * LANES
    padded = -(-flat.shape[0] // unit) * unit
    flat = jnp.concatenate([flat, jnp.zeros((padded - flat.shape[0],), F32)])
    return flat.reshape(-1, LANES)


def _unpack(buf, shapes):
    flat = buf.reshape(-1)
    out, off = [], 0
    for s in shapes:
        n = math.prod(s)
        out.append(flat[off:off + n].reshape(s))
        off += n
    return out


def _row_tile(L):
    return 256 if L % 256 == 0 else L


def _layer0_mix(diff, const):
    x, mod, norm_w, lam_re, lam_im, log_dt, b_re, b_im, c_re, c_im, s5_d, *slots = diff
    ada_b, weights = const
    L = x.shape[0]
    tm = _row_tile(L)
    mods = mod.reshape(2, 1, D_MODEL)
    biases = ada_b.reshape(2, 1, D_MODEL)
    op_ln0 = make_rowwise(_f_lnmod, "ln0", tm, 1, 5, pass_first=True)
    h, x = op_ln0((x,), (norm_w.reshape(1, D_MODEL), mods[1], mods[0], biases[1], biases[0]))
    u, z = make_proj("s5_in")(h, tuple(weights), tuple(slots))
    blocks = _s5_block_params(lam_re, lam_im, log_dt, b_re, b_im, c_re, c_im)
    ys, u = make_s5_core(min(S5_TL, L))(u, *blocks)
    (y2,) = make_rowwise(_f_s5_act, "s5_act", tm, 2, 1)((ys, u), (s5_d.reshape(1, D_INNER),))
    return x, y2, z


def _layer0_out(diff, const):
    x, y2, z, mod, *slots = diff
    ada_b, weights = const
    tm = _row_tile(x.shape[0])
    t = make_mm("s5_glu")(y2, weights[0], slots[0])
    (y4,) = make_rowwise(_f_s5_gate, "s5_gate", tm, 3, 0)((y2, t, z), ())
    o = make_mm("s5_out")(y4, weights[1], slots[1])
    return make_residual("res0", tm)(x, o, mod.reshape(1, D_MODEL), ada_b.reshape(1, D_MODEL))


def _layer1_loss(diff, const):
    x1, mod, norm_w, conv_w, a_log, dt_bias, gdn_nw, final_nw, *slots = diff
    tgt, ada_b, weights = const
    L = x1.shape[0]
    tm = _row_tile(L)
    mods = mod.reshape(3, 1, D_MODEL)
    biases = ada_b.reshape(3, 1, D_MODEL)
    op_ln1 = make_rowwise(_f_lnmod, "ln1", tm, 1, 5, pass_first=True)
    h, x1 = op_ln1((x1,), (norm_w.reshape(1, D_MODEL), mods[1], mods[0], biases[1], biases[0]))
    q0, k0, v0, gz, ba = make_proj("gdn_in")(h, tuple(weights[0:5]), tuple(slots[0:5]))
    cw = jnp.concatenate([conv_w, jnp.zeros((SUBLANES - GDN_CONV, GDN_CONV_CH), F32)], axis=0)
    (q,) = make_rowwise(_f_qnorm, "gdn_qn", tm, 1, 0)((gdn_conv(q0, cw[:, :GDN_QK]),), ())
    (k,) = make_rowwise(_f_knorm, "gdn_kn", tm, 1, 0)((gdn_conv(k0, cw[:, GDN_QK:2 * GDN_QK]),), ())
    (v,) = make_rowwise(_f_vact, "gdn_va", tm, 1, 0)((gdn_conv(v0, cw[:, 2 * GDN_QK:]),), ())
    pad = jnp.zeros((LANES - 2 * GDN_HEADS,), F32)
    alog_row = jnp.concatenate([jnp.zeros((GDN_HEADS,), F32), a_log, pad]).reshape(1, LANES)
    dtb_row = jnp.concatenate([jnp.zeros((GDN_HEADS,), F32), dt_bias, pad]).reshape(1, LANES)
    (bg,) = make_rowwise(_f_betag, "gdn_bg", tm, 1, 2)((ba,), (alog_row, dtb_row))
    og = gdn_scan(*gdn_prep(q, k, v, bg))
    nw_row = jnp.tile(gdn_nw, GDN_HEADS).reshape(1, D_INNER)
    (on,) = make_rowwise(_f_gdn_post, "gdn_post", tm, 2, 1)((og, gz), (nw_row,))
    y = make_mm("gdn_out")(on, weights[5], slots[5])
    x2 = make_residual("res1", tm)(x1, y, mods[2], biases[2])

    (lt,) = make_rowwise(_f_loss, "loss", tm, 2, 1)((x2, tgt), (final_nw.reshape(1, D_MODEL),))
    return jnp.sum(lt)


VEC_NAMES = ("ada_b", "norm_w", "s5_lambda_re", "s5_lambda_im", "s5_log_dt", "s5_d", "gdn_a_log", "gdn_dt_bias", "final_norm_w")
MAT_NAMES = ("s5_b_re", "s5_b_im", "s5_c_re", "s5_c_im")
S5_BIG = ("s5_w_in", "s5_w_glu", "s5_w_out")
GDN_BIG = ("gdn_w_in", "gdn_w_out")
BIG_NAMES = S5_BIG + GDN_BIG
WEIGHT_ORDER = ("ada_w", "ada_b", "norm_w", "s5_w_in", "s5_lambda_re", "s5_lambda_im", "s5_log_dt", "s5_b_re", "s5_b_im",
                "s5_c_re", "s5_c_im", "s5_d", "s5_w_glu", "s5_w_out", "gdn_w_in", "gdn_conv_w", "gdn_a_log", "gdn_dt_bias",
                "gdn_norm_w", "gdn_w_out", "final_norm_w")


def _step(x, c, W, M, V, tgt):
    L = x.shape[1]
    ix, iy, ic = lax.axis_index("x"), lax.axis_index("y"), lax.axis_index("c")
    me = 4 * ix + 2 * iy + ic
    n_ada = W["ada_w"].shape[2]
    n_conv = W["gdn_conv_w"].shape[2]
    n_gnw = W["gdn_norm_w"].shape[1]

    g1 = _allgather_call(_pack([c, W["gdn_conv_w"], W["gdn_norm_w"]], SUBLANES), "gather_small_in", False)
    g1 = g1.reshape(N_DEV, -1)
    c_all = g1[:, :D_MODEL]
    conv_w = g1[:, D_MODEL:D_MODEL + GDN_CONV * n_conv].reshape(N_DEV, GDN_CONV, n_conv).transpose(1, 0, 2).reshape(GDN_CONV, -1)
    gdn_nw = g1[:, D_MODEL + GDN_CONV * n_conv:D_MODEL + GDN_CONV * n_conv + n_gnw].reshape(-1)
    mod_part = _ada_mod_call(c_all, W["ada_w"])
    g2 = _allgather_call(_pack([mod_part], SUBLANES), "gather_mod", False).reshape(N_DEV, -1)
    mod_all = g2[:, :2 * N_DEV * n_ada].reshape(N_DEV, 2, N_DEV, n_ada)
    mod_raw = lax.dynamic_index_in_dim(mod_all, me, axis=2, keepdims=False)
    mod_raw = mod_raw.transpose(1, 0, 2).reshape(2, 3 * D_MODEL)

    shard = lambda n: W[n][0].astype(BF16)
    (w_in5_parts,) = _gather_weights_call([shard("s5_w_in")], "gather_s5_w_in")
    late = _spread_start_call([shard("s5_w_glu"), shard("s5_w_out")], False, "gather_s5_late_start", w_in5_parts)
    g_send, g_recv, g_srcs, g_lands, g_token = _spread_start_call([shard(n) for n in GDN_BIG], False, "gather_gdn_start", late[4])
    w_in5 = _join_cols_call(w_in5_parts, "join_s5_w_in")
    slot = lambda *s: jnp.zeros(s, F32)
    two = 2 * D_MODEL
    diff_mix = (x[0], mod_raw[0, :two] + g_token[0, 0], W["norm_w"][0], W["s5_lambda_re"][0], W["s5_lambda_im"][0], W["s5_log_dt"][0],
                W["s5_b_re"][0], W["s5_b_im"][0], W["s5_c_re"][0], W["s5_c_im"][0], W["s5_d"][0],
                slot(D_MODEL, D_INNER), slot(D_MODEL, D_INNER))

    (xp, y2, z5), vjp_mix = jax.vjp(lambda d: _layer0_mix(d, (W["ada_b"][0, :two], (w_in5[:, :D_INNER], w_in5[:, D_INNER:]))), diff_mix)
    l_srcs, l_lands = _spread_wait_call(late[0], late[1], late[2], late[3], y2, "gather_s5_late_wait")
    w_glu, w_o5 = [lax.dynamic_update_slice(land, src[None], (me, 0, 0)).reshape(-1, src.shape[1]) for land, src in zip(l_lands, l_srcs)]
    diff_out = (xp, y2, z5, mod_raw[0, two:], slot(D_INNER, D_INNER), slot(D_INNER, D_MODEL))
    x1, vjp_out = jax.vjp(lambda d: _layer0_out(d, (W["ada_b"][0, two:], (w_glu, w_o5))), diff_out)
    g_srcs, g_lands = _spread_wait_call(g_send, g_recv, g_srcs, g_lands, x1, "gather_gdn_wait")
    gdn_full = [lax.dynamic_update_slice(land, src[None], (me, 0, 0)) for land, src in zip(g_lands, g_srcs)]
    w_ing = _join_cols_call(gdn_full[0], "join_gdn_w_in")
    w_ba = jnp.concatenate([w_ing[:, GDN_CONV_CH + D_INNER:], jnp.zeros((D_MODEL, LANES - 2 * GDN_HEADS), BF16)], axis=1)
    weights1 = (w_ing[:, :GDN_QK], w_ing[:, GDN_QK:2 * GDN_QK], w_ing[:, 2 * GDN_QK:GDN_CONV_CH],
                w_ing[:, GDN_CONV_CH:GDN_CONV_CH + D_INNER], w_ba, gdn_full[1].reshape(D_INNER, D_MODEL))
    slots1 = tuple(jnp.zeros(w.shape, F32) for w in weights1)
    diff1 = (x1, mod_raw[1], W["norm_w"][1], conv_w, W["gdn_a_log"][0], W["gdn_dt_bias"][0], gdn_nw, W["final_norm_w"], *slots1)
    loss_local, vjp1 = jax.vjp(lambda d: _layer1_loss(d, (tgt[0], W["ada_b"][1], weights1)), diff1)
    ((dx1, dmod1, d_norm_w1, d_conv, d_alog, d_dtb, d_gnw, d_fnw, d_wq, d_wk, d_wv, d_wgz, d_wba, d_wog),) = vjp1(jnp.ones((), F32))
    loss = lax.psum(loss_local, MESH_AXES)

    rows = lambda d: d.reshape(N_DEV, d.shape[0] // N_DEV, d.shape[1])
    d_ing = _split_cols_call(jnp.concatenate([d_wq, d_wk, d_wv, d_wgz, d_wba[:, :2 * GDN_HEADS]], axis=1), "split_gdn_w_in", BF16)
    s_send, s_recv, s_srcs, s_lands, s_token = _spread_start_call([d_ing, rows(d_wog).astype(BF16)], True, "scatter_gdn_start", dx1)
    ((dxp, dy2, dz5, dmod_gate, d_wglu, d_wo5),) = vjp_out(dx1.at[0, 0].add(s_token[0, 0]))
    t_send, t_recv, t_srcs, t_lands, t_token = _spread_start_call(
        [rows(d_wglu).astype(BF16), rows(d_wo5).astype(BF16)], True, "scatter_s5_late_start", dxp)
    ((dx, dmod_ss, d_norm_w0, d_lre, d_lim, d_logdt, d_bre, d_bim, d_cre, d_cim, d_s5d, d_wu, d_wz),) = vjp_mix(
        (dxp.at[0, 0].add(t_token[0, 0]), dy2, dz5))
    dmod = jnp.stack([jnp.concatenate([dmod_ss, dmod_gate]), dmod1])
    d_norm_w = jnp.stack([d_norm_w0, d_norm_w1])
    d_in5 = _split_cols_call(jnp.concatenate([d_wu, d_wz], axis=1), "split_s5_w_in", F32)
    (got,) = _pair_exchange_call([d_in5], "scatter_s5_pair")
    core = jnp.reshape(ic, (1,)).astype(jnp.int32)
    chip = jnp.reshape(2 * ix + iy, (1,)).astype(jnp.int32)
    pair = _pair_sum_call(d_in5, got, core, "pair_sum_s5_w_in")
    (recv,) = _chip_exchange_call([pair], "scatter_s5_chips")
    big = {"s5_w_in": _adam_own_call(pair, chip, recv, W["s5_w_in"][0], M["s5_w_in"][0], V["s5_w_in"][0], "adam_s5_w_in", 128)}
    t_srcs, t_lands = _spread_wait_call(t_send, t_recv, t_srcs, t_lands, dx, "scatter_s5_late_wait")
    s_srcs, s_lands = _spread_wait_call(s_send, s_recv, s_srcs, s_lands, t_lands[0], "scatter_gdn_wait")
    for land, src, n in zip(tuple(t_lands) + tuple(s_lands), tuple(t_srcs) + tuple(s_srcs), ("s5_w_glu", "s5_w_out") + GDN_BIG):
        mine = lax.dynamic_index_in_dim(src, me, 0, keepdims=True)
        parts = lax.dynamic_update_slice(land, mine, (me, 0, 0))
        big[n] = _adam_call(parts, W[n][0], M[n][0], V[n][0], "adam_" + n, rows=_tile(W[n].shape[1], 128))
    big = [[o[None] for o in big[n]] for n in BIG_NAMES]

    vec_parts = [dmod, d_norm_w, d_lre, d_lim, d_logdt, d_s5d, d_alog, d_dtb, d_fnw]
    tail_parts = [d_conv, d_gnw]
    mat_parts = [d_bre, d_bim, d_cre, d_cim]
    n_vec = sum(math.prod(p.shape) for p in vec_parts)
    sg_vec, sg_mat = _gather_weights_call(
        [_pack(vec_parts + tail_parts, ADAM_ROWS), _pack(mat_parts, SUBLANES).astype(BF16)], "gather_small_grads")
    tot_vec = _sum_call(sg_vec, "sum_vec_grads", ADAM_ROWS)
    tot_mat = _sum_call(sg_mat, "sum_mat_grads", ADAM_ROWS)
    g_conv, g_gnw = _unpack(tot_vec.reshape(-1)[n_vec:], [d_conv.shape, d_gnw.shape])
    g_conv_mine = lax.dynamic_slice_in_dim(g_conv, me * n_conv, n_conv, axis=1)
    g_gnw_mine = lax.dynamic_slice_in_dim(g_gnw, me * n_gnw, n_gnw, axis=0)
    vec_names = VEC_NAMES + ("gdn_conv_w", "gdn_norm_w")
    vec_g = _pack([tot_vec.reshape(-1)[:n_vec], g_conv_mine, g_gnw_mine], ADAM_ROWS)
    vec = _adam_call(vec_g[None], _pack([W[n] for n in vec_names], ADAM_ROWS), _pack([M[n] for n in vec_names], ADAM_ROWS),
                     _pack([V[n] for n in vec_names], ADAM_ROWS), "adam_vec")
    vec = [_unpack(b, [W[n].shape for n in vec_names]) for b in vec]
    mats = []
    for name, g_mat in zip(MAT_NAMES, _unpack(tot_mat, [p.shape for p in mat_parts])):
        two_d = (-1, W[name].shape[-1])
        outs = _adam_call(g_mat.reshape(two_d)[None], W[name].reshape(two_d), M[name].reshape(two_d), V[name].reshape(two_d),
                          "adam_" + name, rows=1024)
        mats.append([o.reshape(W[name].shape) for o in outs])

    dmod_all = sg_vec[:, :2 * 3 * D_MODEL // LANES].reshape(N_DEV, 2, N_DEV, n_ada // LANES, LANES)
    dmod_mine = lax.dynamic_index_in_dim(dmod_all, me, axis=2, keepdims=False).transpose(1, 0, 2, 3).reshape(2, N_DEV, n_ada)
    g_ada_w = _ada_grad_call(c_all, dmod_mine)
    ada = _adam_call(g_ada_w.reshape(1, -1, LANES), W["ada_w"].reshape(-1, LANES), M["ada_w"].reshape(-1, LANES),
                     V["ada_w"].reshape(-1, LANES), "adam_ada")
    ada = [a.reshape(W["ada_w"].shape) for a in ada]

    res = {}
    for i, n in enumerate(BIG_NAMES):
        res[n] = big[i]
    for i, n in enumerate(vec_names):
        res[n] = [b[i] for b in vec]
    for i, n in enumerate(MAT_NAMES):
        res[n] = mats[i]
    res["ada_w"] = ada
    outs = [loss, dx[None]]
    for j in range(4):
        outs += [res[n][j] for n in WEIGHT_ORDER]
    return tuple(outs)


def kernel(x, c, ada_w, ada_b, norm_w, s5_w_in, s5_lambda_re, s5_lambda_im, s5_log_dt, s5_b_re, s5_b_im, s5_c_re, s5_c_im, s5_d, s5_w_glu, s5_w_out, gdn_w_in, gdn_conv_w, gdn_a_log, gdn_dt_bias, gdn_norm_w, gdn_w_out, final_norm_w, loss_target, m_ada_w, m_ada_b, m_norm_w, m_s5_w_in, m_s5_lambda_re, m_s5_lambda_im, m_s5_log_dt, m_s5_b_re, m_s5_b_im, m_s5_c_re, m_s5_c_im, m_s5_d, m_s5_w_glu, m_s5_w_out, m_gdn_w_in, m_gdn_conv_w, m_gdn_a_log, m_gdn_dt_bias, m_gdn_norm_w, m_gdn_w_out, m_final_norm_w, v_ada_w, v_ada_b, v_norm_w, v_s5_w_in, v_s5_lambda_re, v_s5_lambda_im, v_s5_log_dt, v_s5_b_re, v_s5_b_im, v_s5_c_re, v_s5_c_im, v_s5_d, v_s5_w_glu, v_s5_w_out, v_gdn_w_in, v_gdn_conv_w, v_gdn_a_log, v_gdn_dt_bias, v_gdn_norm_w, v_gdn_w_out, v_final_norm_w):
    W = dict(ada_w=ada_w, ada_b=ada_b, norm_w=norm_w, s5_w_in=s5_w_in, s5_lambda_re=s5_lambda_re, s5_lambda_im=s5_lambda_im,
             s5_log_dt=s5_log_dt, s5_b_re=s5_b_re, s5_b_im=s5_b_im, s5_c_re=s5_c_re, s5_c_im=s5_c_im, s5_d=s5_d,
             s5_w_glu=s5_w_glu, s5_w_out=s5_w_out, gdn_w_in=gdn_w_in, gdn_conv_w=gdn_conv_w, gdn_a_log=gdn_a_log,
             gdn_dt_bias=gdn_dt_bias, gdn_norm_w=gdn_norm_w, gdn_w_out=gdn_w_out, final_norm_w=final_norm_w)
    M = dict(ada_w=m_ada_w, ada_b=m_ada_b, norm_w=m_norm_w, s5_w_in=m_s5_w_in, s5_lambda_re=m_s5_lambda_re,
             s5_lambda_im=m_s5_lambda_im, s5_log_dt=m_s5_log_dt, s5_b_re=m_s5_b_re, s5_b_im=m_s5_b_im, s5_c_re=m_s5_c_re,
             s5_c_im=m_s5_c_im, s5_d=m_s5_d, s5_w_glu=m_s5_w_glu, s5_w_out=m_s5_w_out, gdn_w_in=m_gdn_w_in,
             gdn_conv_w=m_gdn_conv_w, gdn_a_log=m_gdn_a_log, gdn_dt_bias=m_gdn_dt_bias, gdn_norm_w=m_gdn_norm_w,
             gdn_w_out=m_gdn_w_out, final_norm_w=m_final_norm_w)
    V = dict(ada_w=v_ada_w, ada_b=v_ada_b, norm_w=v_norm_w, s5_w_in=v_s5_w_in, s5_lambda_re=v_s5_lambda_re,
             s5_lambda_im=v_s5_lambda_im, s5_log_dt=v_s5_log_dt, s5_b_re=v_s5_b_re, s5_b_im=v_s5_b_im, s5_c_re=v_s5_c_re,
             s5_c_im=v_s5_c_im, s5_d=v_s5_d, s5_w_glu=v_s5_w_glu, s5_w_out=v_s5_w_out, gdn_w_in=v_gdn_w_in,
             gdn_conv_w=v_gdn_conv_w, gdn_a_log=v_gdn_a_log, gdn_dt_bias=v_gdn_dt_bias, gdn_norm_w=v_gdn_norm_w,
             gdn_w_out=v_gdn_w_out, final_norm_w=v_final_norm_w)
    return _step(x, c, W, M, V, loss_target)
```

```python
import functools
import math

import jax
import jax.numpy as jnp
from jax import lax
from jax.experimental import pallas as pl
from jax.experimental.pallas import tpu as pltpu

F32 = jnp.float32
BF16 = jnp.bfloat16
SDS = jax.ShapeDtypeStruct

D_MODEL = 1024
D_INNER = 2048
NORM_EPS = 1e-6
S5_GROUP = 16
S5_GROUPS = 128
S5_STATE = 64
GDN_HEADS = 8
GDN_DK = 128
GDN_DV = 256
GDN_CONV = 4
GDN_CHUNK = 64
GDN_QK = 1024
GDN_CONV_CH = 4096
GDN_PROJ = 6160
ADAM_LR = 0.001
ADAM_B1 = 0.9
ADAM_B2 = 0.999
ADAM_EPS = 1e-08
ADAM_WD = 0.01
ADAM_STEP = 10

N_DEV = 8
LANES = 128
SUBLANES = 8
VMEM_BIG = 56 << 20
VMEM_MID = 40 << 20
S5_GB = 8
S5_TL = 1024
MESH_AXES = ("x", "y", "c")


def _params(sem, vmem=None):
    return pltpu.CompilerParams(dimension_semantics=sem, vmem_limit_bytes=vmem)


def _bdot(a, b, dims=(((1,), (0,)), ((), ()))):
    return lax.dot_general(a.astype(BF16), b.astype(BF16), dims, preferred_element_type=F32)


def _hdot(a, b, dims=(((1,), (0,)), ((), ()))):
    return lax.dot_general(a, b, dims, preferred_element_type=F32, precision=lax.Precision.HIGHEST)


_BNN = (((2,), (1,)), ((0,), (0,)))
_BNT = (((2,), (2,)), ((0,), (0,)))
_BTN = (((1,), (1,)), ((0,), (0,)))


@jax.custom_vjp
def _unit_lower_inverse(a):
    c = a.shape[-1]
    ri = lax.broadcasted_iota(jnp.int32, a.shape, 1)
    ci = lax.broadcasted_iota(jnp.int32, a.shape, 2)
    n = -a
    t = (ri == ci).astype(F32) + n
    for _ in range(int(math.log2(c)) - 1):
        n = _hdot(n, n, _BNN)
        t = t + _hdot(t, n, _BNN)
    return t


def _unit_lower_inverse_fwd(a):
    t = _unit_lower_inverse(a)
    return t, t


def _unit_lower_inverse_bwd(t, g):
    return (-_hdot(_hdot(t, g, _BTN), t, _BNT),)


_unit_lower_inverse.defvjp(_unit_lower_inverse_fwd, _unit_lower_inverse_bwd)


NN = (((1,), (0,)), ((), ()))
NT = (((1,), (1,)), ((), ()))
TN = (((0,), (0,)), ((), ()))


def _tile(n, pref):
    for t in (pref, 512, 256, 128):
        if t <= n and n % t == 0:
            return t
    return n


def _matmul(a, b, mode, name, add=None):
    if mode == "nn":
        (m, k), (_, n) = a.shape, b.shape
    elif mode == "nt":
        (m, k), (n, _) = a.shape, b.shape
    else:
        (k, m), (_, n) = a.shape, b.shape
    tm, tn, tk = _tile(m, 512), _tile(n, 512), (k if k <= 2048 else _tile(k, 512))
    if mode == "tn":
        tm, tn = _tile(m, 1024), _tile(n, 1024)
    nk = k // tk
    dims = {"nn": NN, "nt": NT, "tn": TN}[mode]

    def body(a_ref, b_ref, *rest):
        o_ref, acc_ref = rest[-2], rest[-1]
        kk = pl.program_id(2)

        @pl.when(kk == 0)
        def _():
            acc_ref[...] = jnp.zeros_like(acc_ref) if add is None else rest[0][...]
        acc_ref[...] += _bdot(a_ref[...], b_ref[...], dims)

        @pl.when(kk == nk - 1)
        def _():
            o_ref[...] = acc_ref[...]

    a_spec = pl.BlockSpec((tk, tm), lambda i, j, q: (q, i)) if mode == "tn" else pl.BlockSpec((tm, tk), lambda i, j, q: (i, q))
    b_spec = pl.BlockSpec((tn, tk), lambda i, j, q: (j, q)) if mode == "nt" else pl.BlockSpec((tk, tn), lambda i, j, q: (q, j))
    o_spec = pl.BlockSpec((tm, tn), lambda i, j, q: (i, j))
    return pl.pallas_call(
        body, name=name, grid=(m // tm, n // tn, nk),
        in_specs=[a_spec, b_spec] + ([] if add is None else [o_spec]), out_specs=o_spec,
        out_shape=SDS((m, n), F32), scratch_shapes=[pltpu.VMEM((tm, tn), F32)],
        compiler_params=_params(("parallel", "parallel", "arbitrary"), VMEM_MID),
    )(a, b, *([] if add is None else [add]))


def make_mm(name, pass_input=False):
    def primal(a, w):
        out = _matmul(a, w, "nn", name + "_fwd")
        return (out, a) if pass_input else out

    @jax.custom_vjp
    def mm(a, w, grad_slot):
        return primal(a, w)

    def fwd(a, w, grad_slot):
        return primal(a, w), (a, w)

    def bwd(res, g):
        a, w = res
        g, g_other = g if pass_input else (g, None)
        return _matmul(g, w, "nt", name + "_dx", add=g_other), jnp.zeros_like(w), _matmul(a, g, "tn", name + "_dw")

    mm.defvjp(fwd, bwd)
    return mm


PROJ_ROWS = 256


def _proj_fwd_call(a, ws, name):
    m, k = a.shape
    tm = _tile(m, PROJ_ROWS)
    nw = len(ws)

    def body(*refs):
        ab = refs[0][...].astype(BF16)
        for w_ref, o_ref in zip(refs[1:1 + nw], refs[1 + nw:]):
            o_ref[...] = lax.dot_general(ab, w_ref[...], NN, preferred_element_type=F32)

    return pl.pallas_call(
        body, name=name, grid=(m // tm,),
        in_specs=[pl.BlockSpec((tm, k), lambda i: (i, 0))] + [pl.BlockSpec(w.shape, lambda i: (0, 0)) for w in ws],
        out_specs=[pl.BlockSpec((tm, w.shape[1]), lambda i: (i, 0)) for w in ws],
        out_shape=[SDS((m, w.shape[1]), F32) for w in ws],
        compiler_params=_params(("parallel",), VMEM_BIG),
    )(a, *ws)


def _proj_dx_call(gs, ws, name):
    m = gs[0].shape[0]
    k = ws[0].shape[0]
    tm = _tile(m, PROJ_ROWS)
    nw = len(ws)

    def body(*refs):
        acc = None
        for g_ref, w_ref in zip(refs[:nw], refs[nw:2 * nw]):
            part = _bdot(g_ref[...], w_ref[...], NT)
            acc = part if acc is None else acc + part
        refs[2 * nw][...] = acc

    return pl.pallas_call(
        body, name=name, grid=(m // tm,),
        in_specs=[pl.BlockSpec((tm, g.shape[1]), lambda i: (i, 0)) for g in gs] + [pl.BlockSpec(w.shape, lambda i: (0, 0)) for w in ws],
        out_specs=pl.BlockSpec((tm, k), lambda i: (i, 0)), out_shape=SDS((m, k), F32),
        compiler_params=_params(("parallel",), VMEM_BIG),
    )(*gs, *ws)


def make_proj(name):
    @jax.custom_vjp
    def proj(a, ws, grad_slots):
        return tuple(_proj_fwd_call(a, ws, name + "_fwd"))

    def fwd(a, ws, grad_slots):
        return tuple(_proj_fwd_call(a, ws, name + "_fwd")), (a, ws)

    def bwd(res, gs):
        a, ws = res
        dws = tuple(_matmul(a, g, "tn", "%s_dw%d" % (name, i)) for i, g in enumerate(gs))
        return _proj_dx_call(tuple(gs), ws, name + "_dx"), tuple(jnp.zeros_like(w) for w in ws), dws

    proj.defvjp(fwd, bwd)
    return proj


def make_rowwise(f, name, tm, n_rows, n_params, vmem=VMEM_MID, pass_first=False):
    def specs_of(arrs, blocked):
        if blocked:
            return [pl.BlockSpec((tm, a.shape[1]), lambda i: (i, 0)) for a in arrs]
        return [pl.BlockSpec(a.shape, lambda i: (0, 0)) for a in arrs]

    def out_structs(rows, params):
        blk = [SDS((tm, r.shape[1]), r.dtype) for r in rows] + [SDS(p.shape, p.dtype) for p in params]
        return jax.eval_shape(f, *blk)

    def run_fwd(rows, params):
        L = rows[0].shape[0]
        outs = out_structs(rows, params)

        def body(*refs):
            ins = [r[...] for r in refs[:n_rows + n_params]]
            res = f(*ins)
            for o_ref, val in zip(refs[n_rows + n_params:], res):
                o_ref[...] = val

        return pl.pallas_call(
            body, name=name + "_fwd", grid=(L // tm,),
            in_specs=specs_of(rows, True) + specs_of(params, False),
            out_specs=[pl.BlockSpec((tm, o.shape[1]), lambda i: (i, 0)) for o in outs],
            out_shape=[SDS((L, o.shape[1]), o.dtype) for o in outs],
            compiler_params=_params(("parallel",), vmem),
        )(*rows, *params)

    def run_bwd(rows, params, gs):
        L = rows[0].shape[0]
        n_g = len(gs)

        def body(*refs):
            i = pl.program_id(0)
            ins = [r[...] for r in refs[:n_rows + n_params]]
            cts = tuple(r[...] for r in refs[n_rows + n_params:n_rows + n_params + n_g])
            outs = refs[n_rows + n_params + n_g:]
            _, vjp = jax.vjp(f, *ins)
            grads = vjp(cts[:-1] if pass_first else cts)
            if pass_first:
                grads = (grads[0] + cts[-1],) + tuple(grads[1:])
            for o_ref, val in zip(outs[:n_rows], grads[:n_rows]):
                o_ref[...] = val

            if n_params:
                @pl.when(i == 0)
                def _():
                    for o_ref in outs[n_rows:]:
                        o_ref[...] = jnp.zeros_like(o_ref)
                for o_ref, val in zip(outs[n_rows:], grads[n_rows:]):
                    o_ref[...] += val

        res = pl.pallas_call(
            body, name=name + "_bwd", grid=(L // tm,),
            in_specs=specs_of(rows, True) + specs_of(params, False) + specs_of(gs, True),
            out_specs=specs_of(rows, True) + specs_of(params, False),
            out_shape=[SDS(r.shape, r.dtype) for r in rows] + [SDS(p.shape, p.dtype) for p in params],
            compiler_params=_params(("arbitrary",), vmem),
        )(*rows, *params, *gs)
        return tuple(res[:n_rows]), tuple(res[n_rows:])

    def outputs(rows, params):
        outs = tuple(run_fwd(rows, params))
        return outs + (rows[0],) if pass_first else outs

    @jax.custom_vjp
    def op(rows, params):
        return outputs(rows, params)

    def fwd(rows, params):
        return outputs(rows, params), (rows, params)

    def bwd(res, gs):
        rows, params = res
        return run_bwd(rows, params, tuple(gs))

    op.defvjp(fwd, bwd)
    op.run_fwd, op.run_bwd = run_fwd, run_bwd
    return op


def make_residual(name, tm):
    full = make_rowwise(_f_res, name, tm, 2, 2)
    branch = make_rowwise(lambda y, gate, bgate: ((gate + bgate) * y,), name + "_branch", tm, 1, 2)

    @jax.custom_vjp
    def op(x, y, gate, bgate):
        return full.run_fwd((x, y), (gate, bgate))[0]

    def fwd(x, y, gate, bgate):
        return full.run_fwd((x, y), (gate, bgate))[0], (y, gate, bgate)

    def bwd(res, g):
        y, gate, bgate = res
        (dy,), (dgate, dbgate) = branch.run_bwd((y,), (gate, bgate), (g,))
        return g, dy, dgate, dbgate

    op.defvjp(fwd, bwd)
    return op


def _s5_scan_rows(xr_ref, xi_ref, ar, ai, x0r, x0i, tl, reverse=False):
    n = xr_ref.shape[1]
    T = SUBLANES
    row = lax.broadcasted_iota(jnp.int32, (T, n), 0)
    pr, pi = [ar], [ai]
    for _ in range(T - 1):
        pr, pi = pr + [pr[-1] * ar - pi[-1] * ai], pi + [pr[-1] * ai + pi[-1] * ar]
    levels = []
    for d in (1, 2, 4):
        mask = (row < T - d) if reverse else (row >= d)
        levels.append((T - d if reverse else d, jnp.where(mask, pr[d - 1], 0.0), jnp.where(mask, pi[d - 1], 0.0)))
    cr = jnp.zeros((T, n), F32)
    ci = jnp.zeros((T, n), F32)
    for r in range(T):
        k = (T - r) if reverse else (r + 1)
        cr = jnp.where(row == r, pr[k - 1], cr)
        ci = jnp.where(row == r, pi[k - 1], ci)
    nt = tl // T
    last = 0 if reverse else T - 1

    def step(t, carry):
        sr, si = carry
        base = pl.multiple_of((nt - 1 - t if reverse else t) * T, T)
        br = xr_ref[pl.ds(base, T), :]
        bi = xi_ref[pl.ds(base, T), :]
        for shift, mr, mi in levels:
            qr = pltpu.roll(br, shift, 0)
            qi = pltpu.roll(bi, shift, 0)
            br, bi = br + (mr * qr - mi * qi), bi + (mr * qi + mi * qr)
        xr = br + (cr * sr - ci * si)
        xi = bi + (cr * si + ci * sr)
        xr_ref[pl.ds(base, T), :] = xr
        xi_ref[pl.ds(base, T), :] = xi
        return xr[last:last + 1, :], xi[last:last + 1, :]
    return lax.fori_loop(0, nt, step, (x0r, x0i))


def _s5_fwd_call(u, bre, bim, cre, cim, a, tl):
    L, e = u.shape
    nb = e // LANES
    ns = bre.shape[2]
    nc = L // tl

    def body(u_ref, bre_ref, bim_ref, cre_ref, cim_ref, a_ref, ys_ref, xb_ref, sr_ref, si_ref, xr_ref, xi_ref, carry_ref):
        c = pl.program_id(1)

        @pl.when(c == 0)
        def _():
            carry_ref[...] = jnp.zeros_like(carry_ref)
        xb_ref[0, 0] = carry_ref[...]
        ub = u_ref[...]
        xr_ref[...] = _bdot(ub, bre_ref[0])
        xi_ref[...] = _bdot(ub, bim_ref[0])
        ar = a_ref[0, 0:1, :]
        ai = a_ref[0, 1:2, :]
        xr, xi = _s5_scan_rows(xr_ref, xi_ref, ar, ai, carry_ref[0:1, :], carry_ref[1:2, :], tl)
        carry_ref[0:1, :] = xr
        carry_ref[1:2, :] = xi
        sr = xr_ref[...].astype(BF16)
        si = xi_ref[...].astype(BF16)
        sr_ref[...] = sr
        si_ref[...] = si
        ys_ref[...] = _bdot(sr, cre_ref[0]) - _bdot(si, cim_ref[0])

    return pl.pallas_call(
        body, name="s5_core_fwd", grid=(nb, nc),
        in_specs=[pl.BlockSpec((tl, LANES), lambda j, c: (c, j)),
                  pl.BlockSpec((1, LANES, ns), lambda j, c: (j, 0, 0)), pl.BlockSpec((1, LANES, ns), lambda j, c: (j, 0, 0)),
                  pl.BlockSpec((1, ns, LANES), lambda j, c: (j, 0, 0)), pl.BlockSpec((1, ns, LANES), lambda j, c: (j, 0, 0)),
                  pl.BlockSpec((1, SUBLANES, ns), lambda j, c: (j, 0, 0))],
        out_specs=[pl.BlockSpec((tl, LANES), lambda j, c: (c, j)),
                   pl.BlockSpec((1, 1, SUBLANES, ns), lambda j, c: (j, c, 0, 0)),
                   pl.BlockSpec((tl, ns), lambda j, c: (c, j)), pl.BlockSpec((tl, ns), lambda j, c: (c, j))],
        out_shape=[SDS((L, e), F32), SDS((nb, nc, SUBLANES, ns), F32), SDS((L, nb * ns), BF16), SDS((L, nb * ns), BF16)],
        scratch_shapes=[pltpu.VMEM((tl, ns), F32), pltpu.VMEM((tl, ns), F32), pltpu.VMEM((SUBLANES, ns), F32)],
        compiler_params=_params(("arbitrary", "arbitrary"), VMEM_MID),
    )(u, bre, bim, cre, cim, a)


def _s5_bwd_call(u, dys, du_other, bre, bim, cre, cim, a, xb, sr, si, tl):
    L, e = u.shape
    nb = e // LANES
    ns = bre.shape[2]
    nc = L // tl

    def body(u_ref, dys_ref, duo_ref, bre_ref, bim_ref, cre_ref, cim_ref, a_ref, xb_ref, sr_ref, si_ref,
             du_ref, dbre_ref, dbim_ref, dcre_ref, dcim_ref, da_ref,
             gr_ref, gi_ref, gcarry_ref):
        c = pl.program_id(1)

        @pl.when(c == 0)
        def _():
            gcarry_ref[...] = jnp.zeros_like(gcarry_ref)
            dbre_ref[...] = jnp.zeros_like(dbre_ref)
            dbim_ref[...] = jnp.zeros_like(dbim_ref)
            dcre_ref[...] = jnp.zeros_like(dcre_ref)
            dcim_ref[...] = jnp.zeros_like(dcim_ref)
            da_ref[...] = jnp.zeros_like(da_ref)

        ub = u_ref[...]
        dy = dys_ref[...]
        ar = a_ref[0, 0:1, :]
        ai = a_ref[0, 1:2, :]
        x0r = xb_ref[0, 0, 0:1, :]
        x0i = xb_ref[0, 0, 1:2, :]
        dcre_ref[0] += _bdot(sr_ref[...], dy, TN)
        dcim_ref[0] -= _bdot(si_ref[...], dy, TN)
        gr_ref[...] = _bdot(dy, cre_ref[0], NT)
        gi_ref[...] = -_bdot(dy, cim_ref[0], NT)

        g0r, g0i = _s5_scan_rows(gr_ref, gi_ref, ar, -ai, gcarry_ref[0:1, :], gcarry_ref[1:2, :], tl, reverse=True)
        gcarry_ref[0:1, :] = g0r
        gcarry_ref[1:2, :] = g0i
        row = lax.broadcasted_iota(jnp.int32, (tl, ns), 0)
        gr = gr_ref[...]
        gi = gi_ref[...]
        xpr = jnp.where(row == 0, x0r, pltpu.roll(sr_ref[...].astype(F32), 1, 0))
        xpi = jnp.where(row == 0, x0i, pltpu.roll(si_ref[...].astype(F32), 1, 0))
        da_ref[0, 0:1, :] += jnp.sum(gr * xpr + gi * xpi, axis=0, keepdims=True)
        da_ref[0, 1:2, :] += jnp.sum(gi * xpr - gr * xpi, axis=0, keepdims=True)
        du_ref[...] = (_bdot(gr, bre_ref[0], NT) + _bdot(gi, bim_ref[0], NT)) + duo_ref[...]
        dbre_ref[0] += _bdot(ub, gr, TN)
        dbim_ref[0] += _bdot(ub, gi, TN)

    rev = lambda c: nc - 1 - c
    return pl.pallas_call(
        body, name="s5_core_bwd", grid=(nb, nc),
        in_specs=[pl.BlockSpec((tl, LANES), lambda j, c: (rev(c), j)), pl.BlockSpec((tl, LANES), lambda j, c: (rev(c), j)),
                  pl.BlockSpec((tl, LANES), lambda j, c: (rev(c), j)),
                  pl.BlockSpec((1, LANES, ns), lambda j, c: (j, 0, 0)), pl.BlockSpec((1, LANES, ns), lambda j, c: (j, 0, 0)),
                  pl.BlockSpec((1, ns, LANES), lambda j, c: (j, 0, 0)), pl.BlockSpec((1, ns, LANES), lambda j, c: (j, 0, 0)),
                  pl.BlockSpec((1, SUBLANES, ns), lambda j, c: (j, 0, 0)),
                  pl.BlockSpec((1, 1, SUBLANES, ns), lambda j, c: (j, rev(c), 0, 0)),
                  pl.BlockSpec((tl, ns), lambda j, c: (rev(c), j)), pl.BlockSpec((tl, ns), lambda j, c: (rev(c), j))],
        out_specs=[pl.BlockSpec((tl, LANES), lambda j, c: (rev(c), j)),
                   pl.BlockSpec((1, LANES, ns), lambda j, c: (j, 0, 0)), pl.BlockSpec((1, LANES, ns), lambda j, c: (j, 0, 0)),
                   pl.BlockSpec((1, ns, LANES), lambda j, c: (j, 0, 0)), pl.BlockSpec((1, ns, LANES), lambda j, c: (j, 0, 0)),
                   pl.BlockSpec((1, SUBLANES, ns), lambda j, c: (j, 0, 0))],
        out_shape=[SDS((L, e), F32), SDS(bre.shape, F32), SDS(bim.shape, F32), SDS(cre.shape, F32), SDS(cim.shape, F32),
                   SDS(a.shape, F32)],
        scratch_shapes=[pltpu.VMEM((tl, ns), F32) for _ in range(2)] + [pltpu.VMEM((SUBLANES, ns), F32)],
        compiler_params=_params(("arbitrary", "arbitrary"), VMEM_MID),
    )(u, dys, du_other, bre, bim, cre, cim, a, xb, sr, si)


def make_s5_core(tl):
    @jax.custom_vjp
    def s5_core(u, bre, bim, cre, cim, a):
        return _s5_fwd_call(u, bre, bim, cre, cim, a, tl)[0], u

    def fwd(u, bre, bim, cre, cim, a):
        ys, xb, sr, si = _s5_fwd_call(u, bre, bim, cre, cim, a, tl)
        return (ys, u), (u, bre, bim, cre, cim, a, xb, sr, si)

    def bwd(res, cts):
        u, bre, bim, cre, cim, a, xb, sr, si = res
        dys, du_other = cts
        return tuple(_s5_bwd_call(u, dys, du_other, bre, bim, cre, cim, a, xb, sr, si, tl))

    s5_core.defvjp(fwd, bwd)
    return s5_core


def _s5_block_params(lam_re, lam_im, log_dt, b_re, b_im, c_re, c_im):
    dt = jnp.exp(log_dt)[:, None]
    mag = jnp.exp(lam_re * dt)
    ab_re = mag * jnp.cos(lam_im * dt)
    ab_im = mag * jnp.sin(lam_im * dt)
    den = lam_re * lam_re + lam_im * lam_im
    nr = ab_re - 1.0
    ni = ab_im
    q_re = (nr * lam_re + ni * lam_im) / den
    q_im = (ni * lam_re - nr * lam_im) / den
    bb_re = q_re[..., None] * b_re - q_im[..., None] * b_im
    bb_im = q_re[..., None] * b_im + q_im[..., None] * b_re
    nb = S5_GROUPS // S5_GB
    eye = jnp.eye(S5_GB, dtype=F32)

    def bdiag_in(bb):
        t = bb.reshape(nb, S5_GB, S5_STATE, S5_GROUP)
        t = jnp.einsum("jgpm,gh->jgmhp", t, eye)
        return t.reshape(nb, S5_GB * S5_GROUP, S5_GB * S5_STATE)

    def bdiag_out(cc):
        t = cc.reshape(nb, S5_GB, S5_GROUP, S5_STATE)
        t = jnp.einsum("jgmp,gh->jgphm", t, eye)
        return t.reshape(nb, S5_GB * S5_STATE, S5_GB * S5_GROUP)

    a = jnp.stack([ab_re.reshape(nb, S5_GB * S5_STATE), ab_im.reshape(nb, S5_GB * S5_STATE)], axis=1)
    a = jnp.concatenate([a, jnp.zeros((nb, SUBLANES - 2, S5_GB * S5_STATE), F32)], axis=1)
    return bdiag_in(bb_re), bdiag_in(bb_im), bdiag_out(c_re), bdiag_out(c_im), a


def _shift_down(x, s, row):
    if s == 0:
        return x
    return jnp.where(row >= s, pltpu.roll(x, s, 0), 0.0)


def _shift_up(x, s, row, n):
    if s == 0:
        return x
    return jnp.where(row < n - s, pltpu.roll(x, n - s, 0), 0.0)


def _causal_conv(xv, w_ref, row):
    acc = jnp.zeros_like(xv)
    for j in range(GDN_CONV):
        acc += w_ref[j:j + 1, :] * _shift_down(xv, GDN_CONV - 1 - j, row)
    return acc


def _conv_fwd_call(x, w, act, name):
    L, ch = x.shape

    def body(x_ref, w_ref, y_ref):
        xv = x_ref[...]
        row = lax.broadcasted_iota(jnp.int32, xv.shape, 0)
        y_ref[...] = act(_causal_conv(xv, w_ref, row))

    return pl.pallas_call(
        body, name=name + "_fwd", grid=(ch // LANES,),
        in_specs=[pl.BlockSpec((L, LANES), lambda j: (0, j)), pl.BlockSpec((SUBLANES, LANES), lambda j: (0, j))],
        out_specs=pl.BlockSpec((L, LANES), lambda j: (0, j)), out_shape=SDS((L, ch), F32),
        compiler_params=_params(("parallel",), VMEM_MID),
    )(x, w)


def _conv_bwd_call(x, w, dy, act, name):
    L, ch = x.shape

    def body(x_ref, w_ref, dy_ref, dx_ref, dw_ref):
        xv = x_ref[...]
        row = lax.broadcasted_iota(jnp.int32, xv.shape, 0)
        _, act_vjp = jax.vjp(act, _causal_conv(xv, w_ref, row))
        (g,) = act_vjp(dy_ref[...])
        acc = jnp.zeros_like(xv)
        dws = []
        for j in range(GDN_CONV):
            s = GDN_CONV - 1 - j
            acc += w_ref[j:j + 1, :] * _shift_up(g, s, row, L)
            dws.append(jnp.sum(g * _shift_down(xv, s, row), axis=0, keepdims=True))
        dx_ref[...] = acc
        dw_ref[...] = jnp.concatenate(dws + [jnp.zeros((SUBLANES - GDN_CONV, LANES), F32)], axis=0)

    return pl.pallas_call(
        body, name=name + "_bwd", grid=(ch // LANES,),
        in_specs=[pl.BlockSpec((L, LANES), lambda j: (0, j)), pl.BlockSpec((SUBLANES, LANES), lambda j: (0, j)),
                  pl.BlockSpec((L, LANES), lambda j: (0, j))],
        out_specs=[pl.BlockSpec((L, LANES), lambda j: (0, j)), pl.BlockSpec((SUBLANES, LANES), lambda j: (0, j))],
        out_shape=[SDS((L, ch), F32), SDS((SUBLANES, ch), F32)],
        compiler_params=_params(("parallel",), VMEM_MID),
    )(x, w, dy)


def make_conv_act(act, name):
    @jax.custom_vjp
    def op(x, w):
        return _conv_fwd_call(x, w, act, name)

    def fwd(x, w):
        return _conv_fwd_call(x, w, act, name), (x, w)

    def bwd(res, dy):
        x, w = res
        return tuple(_conv_bwd_call(x, w, dy, act, name))

    op.defvjp(fwd, bwd)
    return op


gdn_conv = make_conv_act(lambda c: c, "gdn_conv")


BNN = (((2,), (1,)), ((0,), (0,)))
BNT = (((2,), (2,)), ((0,), (0,)))
BTN = (((1,), (1,)), ((0,), (0,)))
GDN_PREP_BATCH = 8


@jax.custom_vjp
def _known_inverse(a, t):
    return t


def _known_inverse_fwd(a, t):
    return t, t


def _known_inverse_bwd(t, g):
    return -_hdot(_hdot(t, g, _BTN), t, _BNT), jnp.zeros_like(t)


_known_inverse.defvjp(_known_inverse_fwd, _known_inverse_bwd)


def _gdn_prep_math(q, k, v, beta, g, t_saved=None):
    B, C = q.shape[0], q.shape[1]
    ri = lax.broadcasted_iota(jnp.int32, (B, C, C), 1)
    ci = lax.broadcasted_iota(jnp.int32, (B, C, C), 2)
    causal = ri >= ci
    strict = ri > ci
    eye = (ri == ci).astype(F32)
    gb = jnp.broadcast_to(g, (B, C, C))
    g_row = jnp.sum(gb * eye, axis=1, keepdims=True)
    gc_col = jnp.sum(jnp.where(causal, jnp.broadcast_to(g_row, (B, C, C)), 0.0), axis=2, keepdims=True)
    gc_row = jnp.sum(jnp.where(ri <= ci, gb, 0.0), axis=1, keepdims=True)
    decay = jnp.exp(jnp.where(causal, gc_col - gc_row, -jnp.inf))
    kk = _bdot(k, k, BNT)
    a_mat = jnp.where(strict, beta * kk * decay, 0.0)
    t = _unit_lower_inverse(a_mat) if t_saved is None else _known_inverse(a_mat, t_saved)
    e_gc = jnp.exp(gc_col)
    w = _hdot(t, beta * e_gc * k, BNN)
    u = _hdot(t, beta * v, BNN)
    qk = _bdot(q, k, BNT) * decay
    q_dec = q * e_gc
    g_last = gc_col[:, C - 1:C, :]
    k_dec = k * jnp.exp(g_last - gc_col)
    return q_dec, w, u, qk, k_dec, gc_col, t


def _gdn_prep_specs(L):
    C = GDN_CHUNK
    nb = min(GDN_PREP_BATCH, L // C)
    R = nb * C
    ins = [pl.BlockSpec((R, GDN_DK), lambda c, h: (c, h)), pl.BlockSpec((R, GDN_DK), lambda c, h: (c, h)),
           pl.BlockSpec((R, GDN_DV), lambda c, h: (c, h)), pl.BlockSpec((R, LANES), lambda c, h: (c, 0))]
    outs = [pl.BlockSpec((1, R, GDN_DK), lambda c, h: (h, c, 0)), pl.BlockSpec((1, R, GDN_DK), lambda c, h: (h, c, 0)),
            pl.BlockSpec((1, R, GDN_DV), lambda c, h: (h, c, 0)), pl.BlockSpec((1, R, C), lambda c, h: (h, c, 0)),
            pl.BlockSpec((1, R, GDN_DK), lambda c, h: (h, c, 0)), pl.BlockSpec((1, R, 1), lambda c, h: (h, c, 0))]
    t_spec = pl.BlockSpec((1, R, C), lambda c, h: (h, c, 0))
    shapes = [SDS((GDN_HEADS, L, GDN_DK), F32), SDS((GDN_HEADS, L, GDN_DK), F32), SDS((GDN_HEADS, L, GDN_DV), F32),
              SDS((GDN_HEADS, L, C), F32), SDS((GDN_HEADS, L, GDN_DK), F32), SDS((GDN_HEADS, L, 1), F32)]
    return ins, outs, t_spec, shapes, nb


def _chunks(x, nb):
    return x.reshape(nb, x.shape[0] // nb, x.shape[1])


def _head_columns(bg, h):
    lane = lax.broadcasted_iota(jnp.int32, bg.shape, 1)
    beta = jnp.sum(jnp.where(lane == h, bg, 0.0), axis=1, keepdims=True)
    g = jnp.sum(jnp.where(lane == h + GDN_HEADS, bg, 0.0), axis=1, keepdims=True)
    return beta, g


def _gdn_prep_fwd_call(q, k, v, bg):
    L = q.shape[0]
    ins, outs, t_spec, shapes, nb = _gdn_prep_specs(L)

    def body(q_ref, k_ref, v_ref, bg_ref, *o_refs):
        beta, g = _head_columns(bg_ref[...], pl.program_id(1))
        res = _gdn_prep_math(_chunks(q_ref[...], nb), _chunks(k_ref[...], nb), _chunks(v_ref[...], nb),
                             _chunks(beta, nb), _chunks(g, nb))
        for o_ref, val in zip(o_refs, res):
            o_ref[0] = val.reshape(val.shape[0] * val.shape[1], val.shape[2])

    return pl.pallas_call(
        body, name="gdn_prep_fwd", grid=(L // (nb * GDN_CHUNK), GDN_HEADS), in_specs=ins, out_specs=outs + [t_spec],
        out_shape=shapes + [SDS((GDN_HEADS, L, GDN_CHUNK), F32)],
        compiler_params=_params(("parallel", "parallel"), VMEM_MID),
    )(q, k, v, bg)


def _gdn_prep_bwd_call(q, k, v, bg, t, cts):
    L = q.shape[0]
    ins, outs, t_spec, _, nb = _gdn_prep_specs(L)

    def body(q_ref, k_ref, v_ref, bg_ref, t_ref, c0, c1, c2, c3, c4, c5, dq_ref, dk_ref, dv_ref, dbg_ref):
        h = pl.program_id(1)
        beta, g = _head_columns(bg_ref[...], h)
        t_saved = _chunks(t_ref[0], nb)
        _, vjp = jax.vjp(lambda *a: _gdn_prep_math(*a, t_saved=t_saved)[:6], _chunks(q_ref[...], nb), _chunks(k_ref[...], nb),
                         _chunks(v_ref[...], nb), _chunks(beta, nb), _chunks(g, nb))
        dq, dk, dv, db, dg = vjp(tuple(_chunks(c[0], nb) for c in (c0, c1, c2, c3, c4, c5)))
        flat = lambda a: a.reshape(a.shape[0] * a.shape[1], a.shape[2])
        dq_ref[...] = flat(dq)
        dk_ref[...] = flat(dk)
        dv_ref[...] = flat(dv)

        @pl.when(h == 0)
        def _():
            dbg_ref[...] = jnp.zeros_like(dbg_ref)
        lane = lax.broadcasted_iota(jnp.int32, dbg_ref.shape, 1)
        dbg_ref[...] += jnp.where(lane == h, flat(db), 0.0) + jnp.where(lane == h + GDN_HEADS, flat(dg), 0.0)

    return pl.pallas_call(
        body, name="gdn_prep_bwd", grid=(L // (nb * GDN_CHUNK), GDN_HEADS), in_specs=ins + [t_spec] + outs, out_specs=ins,
        out_shape=[SDS(q.shape, F32), SDS(k.shape, F32), SDS(v.shape, F32), SDS(bg.shape, F32)],
        compiler_params=_params(("parallel", "arbitrary"), VMEM_MID),
    )(q, k, v, bg, t, *cts)


@jax.custom_vjp
def gdn_prep(q, k, v, bg):
    return tuple(_gdn_prep_fwd_call(q, k, v, bg)[:6])


def _gdn_prep_f(q, k, v, bg):
    res = _gdn_prep_fwd_call(q, k, v, bg)
    return tuple(res[:6]), (q, k, v, bg, res[6])


def _gdn_prep_b(res, cts):
    return tuple(_gdn_prep_bwd_call(*res, tuple(cts)))


gdn_prep.defvjp(_gdn_prep_f, _gdn_prep_b)


def _gdn_step_math(q_dec, w, u, qk, k_dec, gc, state):
    H, C = q_dec.shape[0], q_dec.shape[1]
    v_new = u - _bdot(w, state, BNN)
    o = _bdot(q_dec, state, BNN) + _bdot(qk, v_new, BNN)
    gl = gc[:, C - 1:C, :]
    new_state = jnp.exp(gl) * state + _bdot(k_dec, v_new, BTN)
    return jnp.concatenate([o[h] for h in range(H)], axis=1), new_state


def _gdn_scan_specs(L, rev):
    C, H = GDN_CHUNK, GDN_HEADS
    nc = L // C
    cc = (lambda c: nc - 1 - c) if rev else (lambda c: c)
    ins = [pl.BlockSpec((H, C, GDN_DK), lambda c: (0, cc(c), 0)), pl.BlockSpec((H, C, GDN_DK), lambda c: (0, cc(c), 0)),
           pl.BlockSpec((H, C, GDN_DV), lambda c: (0, cc(c), 0)), pl.BlockSpec((H, C, C), lambda c: (0, cc(c), 0)),
           pl.BlockSpec((H, C, GDN_DK), lambda c: (0, cc(c), 0)), pl.BlockSpec((H, C, 1), lambda c: (0, cc(c), 0))]
    o_spec = pl.BlockSpec((C, H * GDN_DV), lambda c: (cc(c), 0))
    s_spec = pl.BlockSpec((1, H, GDN_DK, GDN_DV), lambda c: (cc(c), 0, 0, 0))
    return ins, o_spec, s_spec, nc


def _gdn_scan_fwd_call(q_dec, w, u, qk, k_dec, gc):
    L = q_dec.shape[1]
    ins, o_spec, s_spec, nc = _gdn_scan_specs(L, False)

    def body(qd_ref, w_ref, u_ref, qk_ref, kd_ref, gc_ref, o_ref, sin_ref, s_ref):
        c = pl.program_id(0)

        @pl.when(c == 0)
        def _():
            s_ref[...] = jnp.zeros_like(s_ref)
        st = s_ref[...]
        sin_ref[0] = st
        o, ns = _gdn_step_math(qd_ref[...], w_ref[...], u_ref[...], qk_ref[...], kd_ref[...], gc_ref[...], st)
        o_ref[...] = o
        s_ref[...] = ns

    return pl.pallas_call(
        body, name="gdn_scan_fwd", grid=(nc,), in_specs=ins, out_specs=[o_spec, s_spec],
        out_shape=[SDS((L, GDN_HEADS * GDN_DV), F32), SDS((nc, GDN_HEADS, GDN_DK, GDN_DV), F32)],
        scratch_shapes=[pltpu.VMEM((GDN_HEADS, GDN_DK, GDN_DV), F32)],
        compiler_params=_params(("arbitrary",), VMEM_MID),
    )(q_dec, w, u, qk, k_dec, gc)


def _gdn_scan_bwd_call(q_dec, w, u, qk, k_dec, gc, s_in, do):
    L = q_dec.shape[1]
    ins, o_spec, s_spec, nc = _gdn_scan_specs(L, True)

    def body(qd_ref, w_ref, u_ref, qk_ref, kd_ref, gc_ref, sin_ref, do_ref,
             dqd_ref, dw_ref, du_ref, dqk_ref, dkd_ref, dgc_ref, ds_ref):
        c = pl.program_id(0)

        @pl.when(c == 0)
        def _():
            ds_ref[...] = jnp.zeros_like(ds_ref)
        _, vjp = jax.vjp(_gdn_step_math, qd_ref[...], w_ref[...], u_ref[...], qk_ref[...], kd_ref[...], gc_ref[...], sin_ref[0])
        dqd, dw, du, dqk, dkd, dgc, dst = vjp((do_ref[...], ds_ref[...]))
        dqd_ref[...] = dqd
        dw_ref[...] = dw
        du_ref[...] = du
        dqk_ref[...] = dqk
        dkd_ref[...] = dkd
        dgc_ref[...] = dgc
        ds_ref[...] = dst

    return pl.pallas_call(
        body, name="gdn_scan_bwd", grid=(nc,), in_specs=ins + [s_spec, o_spec], out_specs=ins,
        out_shape=[SDS(t.shape, F32) for t in (q_dec, w, u, qk, k_dec, gc)],
        scratch_shapes=[pltpu.VMEM((GDN_HEADS, GDN_DK, GDN_DV), F32)],
        compiler_params=_params(("arbitrary",), VMEM_MID),
    )(q_dec, w, u, qk, k_dec, gc, s_in, do)


@jax.custom_vjp
def gdn_scan(q_dec, w, u, qk, k_dec, gc):
    return _gdn_scan_fwd_call(q_dec, w, u, qk, k_dec, gc)[0]


def _gdn_scan_f(*args):
    o, s_in = _gdn_scan_fwd_call(*args)
    return o, (*args, s_in)


def _gdn_scan_b(res, do):
    return tuple(_gdn_scan_bwd_call(*res, do))


gdn_scan.defvjp(_gdn_scan_f, _gdn_scan_b)


def _silu(x):
    return x * jax.nn.sigmoid(x)


def _gelu_tanh(x):
    return 0.5 * x * (1.0 + jnp.tanh(math.sqrt(2.0 / math.pi) * (x + 0.044715 * (x * x * x))))


def _f_lnmod(x, nw, sc, sh, bsc, bsh):
    xn = x * lax.rsqrt(jnp.mean(x * x, axis=-1, keepdims=True) + NORM_EPS) * nw
    return (xn * (1.0 + (sc + bsc)) + (sh + bsh),)


def _f_s5_act(ys, u, d):
    return (_gelu_tanh(ys + d * u),)


def _f_s5_gate(y2, t, z):
    return (y2 * jax.nn.sigmoid(t) * _silu(z),)


def _f_res(x, y, gate, bgate):
    return (x + (gate + bgate) * y,)


def _heads(x, width, fn):
    return jnp.concatenate([fn(x[:, i * width:(i + 1) * width]) for i in range(x.shape[1] // width)], axis=1)


def _l2n(x):
    return x * lax.rsqrt(jnp.sum(x * x, axis=-1, keepdims=True) + NORM_EPS)


def _f_qnorm(x):
    return (_heads(_silu(x), GDN_DK, _l2n) * (GDN_DK ** -0.5),)


def _f_knorm(x):
    return (_heads(_silu(x), GDN_DK, _l2n),)


def _f_vact(x):
    return (_silu(x),)


def _f_betag(ba, alog, dtb):
    col = lax.broadcasted_iota(jnp.int32, ba.shape, 1)
    t = ba + dtb
    softplus = jnp.maximum(t, 0.0) + jnp.log1p(jnp.exp(-jnp.abs(t)))
    g = -jnp.exp(alog) * softplus
    return (jnp.where(col < GDN_HEADS, jax.nn.sigmoid(ba), jnp.where(col < 2 * GDN_HEADS, g, 0.0)),)


def _f_gdn_post(o, z, nw):
    on = _heads(o, GDN_DV, lambda t: t * lax.rsqrt(jnp.mean(t * t, axis=-1, keepdims=True) + NORM_EPS))
    return (on * nw * _silu(z),)


def _f_loss(x, tgt, fw):
    y = x * lax.rsqrt(jnp.mean(x * x, axis=-1, keepdims=True) + NORM_EPS) * fw
    err = y - tgt
    return (0.5 * jnp.mean(err * err, axis=-1, keepdims=True),)


def _ada_mod_call(c_all, ada_w):
    n = ada_w.shape[2]

    def body(c_ref, w_ref, o_ref):
        ca = _silu(c_ref[...])
        for l in range(ada_w.shape[0]):
            o_ref[l] = _bdot(ca, w_ref[l])

    return pl.pallas_call(body, name="ada_mod", out_shape=SDS((ada_w.shape[0], N_DEV, n), F32),
                          compiler_params=_params(None, VMEM_MID))(c_all, ada_w)


def _ada_grad_call(c_all, dmod):
    nl, _, n = dmod.shape

    def body(c_ref, d_ref, o_ref):
        ca = _silu(c_ref[...])
        for l in range(nl):
            o_ref[l] = _hdot(ca, d_ref[l], TN)

    return pl.pallas_call(body, name="ada_grad", out_shape=SDS((nl, c_all.shape[1], n), F32),
                          compiler_params=_params(None, VMEM_MID))(c_all, dmod)


ADAM_ROWS = 512


def _adamw(g, w, m, v):
    m2 = ADAM_B1 * m + (1.0 - ADAM_B1) * g
    v2 = ADAM_B2 * v + (1.0 - ADAM_B2) * (g * g)
    m_hat = m2 / (1.0 - ADAM_B1 ** ADAM_STEP)
    v_hat = v2 / (1.0 - ADAM_B2 ** ADAM_STEP)
    return g, -ADAM_LR * (m_hat / (jnp.sqrt(v_hat) + ADAM_EPS) + ADAM_WD * w), m2, v2


def _adam_call(gs, w, m, v, name, rows=None):
    n, r, cols = gs.shape
    rows = rows or ADAM_ROWS

    def body(g_ref, w_ref, m_ref, v_ref, go_ref, d_ref, mo_ref, vo_ref):
        g = g_ref[0].astype(F32)
        for s in range(1, n):
            g = g + g_ref[s].astype(F32)
        for o_ref, val in zip((go_ref, d_ref, mo_ref, vo_ref), _adamw(g, w_ref[...], m_ref[...], v_ref[...])):
            o_ref[...] = val

    blk = pl.BlockSpec((rows, cols), lambda i: (i, 0))
    return pl.pallas_call(
        body, name=name, grid=(r // rows,),
        in_specs=[pl.BlockSpec((n, rows, cols), lambda i: (0, i, 0)), blk, blk, blk],
        out_specs=[blk, blk, blk, blk], out_shape=[SDS((r, cols), F32)] * 4,
        compiler_params=_params(("parallel",), VMEM_MID),
    )(gs, w, m, v)


def _sum_call(gs, name, rows):
    n, r, _ = gs.shape

    def body(g_ref, o_ref):
        g = g_ref[0].astype(F32)
        for s in range(1, n):
            g = g + g_ref[s].astype(F32)
        o_ref[...] = g

    return pl.pallas_call(
        body, name=name, grid=(r // rows,),
        in_specs=[pl.BlockSpec((n, rows, LANES), lambda i: (0, i, 0))],
        out_specs=pl.BlockSpec((rows, LANES), lambda i: (i, 0)), out_shape=SDS((r, LANES), F32),
        compiler_params=_params(("parallel",), VMEM_MID),
    )(gs)


def _allgather_call(x_shard, name, in_hbm):
    m_per, n = x_shard.shape

    def body(x_ref, out_ref, send_sems, recv_sems, local_sem):
        x, y, c = lax.axis_index("x"), lax.axis_index("y"), lax.axis_index("c")
        me, sibling = (x, y, c), (x, y, 1 - c)
        chips = [(1 - x, y), (x, 1 - y), (1 - x, 1 - y)]

        def rows(px, py, pc):
            return out_ref.at[pl.ds((4 * px + 2 * py + pc) * m_per, m_per), :]

        def copy(k, block, to, src=None):
            return pltpu.make_async_remote_copy(
                src_ref=rows(*block) if src is None else src, dst_ref=rows(*block),
                send_sem=send_sems.at[k], recv_sem=recv_sems.at[k], device_id=to, device_id_type=pl.DeviceIdType.MESH)

        mine = pltpu.make_async_copy(x_ref, rows(*me), local_sem)
        mine.start()
        first = [copy(0, me, sibling, src=x_ref)]
        first += [copy(1 + j, me, (*chip, c), src=x_ref) for j, chip in enumerate(chips)]
        for cp in first:
            cp.start()
        passed = [copy(4 + j, (*chip, c), sibling) for j, chip in enumerate(chips)]
        for j, chip in enumerate(chips):
            copy(1 + j, (*chip, c), me).wait_recv()
            passed[j].start()
        copy(0, sibling, me).wait_recv()
        for j, chip in enumerate(chips):
            copy(4 + j, (*chip, 1 - c), me).wait_recv()
        for cp in first + passed:
            cp.wait_send()
        mine.wait()

    space = pl.ANY if in_hbm else pltpu.VMEM
    return pl.pallas_call(
        body, name=name, out_shape=SDS((N_DEV * m_per, n), x_shard.dtype),
        in_specs=[pl.BlockSpec(memory_space=space)], out_specs=pl.BlockSpec(memory_space=space),
        scratch_shapes=[pltpu.SemaphoreType.DMA((7,)), pltpu.SemaphoreType.DMA((7,)), pltpu.SemaphoreType.DMA],
        compiler_params=_params(None, None if in_hbm else VMEM_BIG),
    )(x_shard)


def _gather_weights_call(shards, name):
    nw = len(shards)

    def body(*refs):
        x_refs, out_refs = refs[:nw], refs[nw:2 * nw]
        send_sems, recv_sems, local_sems = refs[2 * nw:]
        x, y, c = lax.axis_index("x"), lax.axis_index("y"), lax.axis_index("c")
        me, sibling = (x, y, c), (x, y, 1 - c)
        chips = [(1 - x, y), (x, 1 - y), (1 - x, 1 - y)]

        def slot(w, px, py, pc):
            return out_refs[w].at[4 * px + 2 * py + pc]

        def copy(w, k, block, to, src=None):
            dst = slot(w, *block)
            return pltpu.make_async_remote_copy(
                src_ref=dst if src is None else src, dst_ref=dst, send_sem=send_sems.at[7 * w + k],
                recv_sem=recv_sems.at[7 * w + k], device_id=to, device_id_type=pl.DeviceIdType.MESH)

        mines = [pltpu.make_async_copy(x_refs[w], slot(w, *me), local_sems.at[w]) for w in range(nw)]
        for cp in mines:
            cp.start()
        first = [copy(w, 0, me, sibling, src=x_refs[w]) for w in range(nw)]
        first += [copy(w, 1 + j, me, (*chip, c), src=x_refs[w]) for w in range(nw) for j, chip in enumerate(chips)]
        for cp in first:
            cp.start()
        passed = []
        for w in range(nw):
            for j, chip in enumerate(chips):
                copy(w, 1 + j, (*chip, c), me).wait_recv()
                fwd = copy(w, 4 + j, (*chip, c), sibling)
                fwd.start()
                passed.append(fwd)
        for w in range(nw):
            copy(w, 0, sibling, me).wait_recv()
            for j, chip in enumerate(chips):
                copy(w, 4 + j, (*chip, 1 - c), me).wait_recv()
        for cp in first + passed:
            cp.wait_send()
        for cp in mines:
            cp.wait()

    hbm = pl.BlockSpec(memory_space=pl.ANY)
    return pl.pallas_call(
        body, name=name, out_shape=[SDS((N_DEV,) + s.shape, s.dtype) for s in shards],
        in_specs=[hbm] * nw, out_specs=[hbm] * nw,
        scratch_shapes=[pltpu.SemaphoreType.DMA((7 * nw,)), pltpu.SemaphoreType.DMA((7 * nw,)), pltpu.SemaphoreType.DMA((nw,))],
    )(*shards)


def _pair_exchange_call(grads, name):
    nw = len(grads)

    def body(*refs):
        g_refs, got_refs = refs[:nw], refs[nw:2 * nw]
        send_sems, recv_sems = refs[2 * nw:]
        x, y, c = lax.axis_index("x"), lax.axis_index("y"), lax.axis_index("c")
        copies = []
        for w in range(nw):
            for j in range(4):
                give = pltpu.make_async_remote_copy(
                    src_ref=g_refs[w].at[2 * j + 1 - c], dst_ref=got_refs[w].at[j], send_sem=send_sems.at[4 * w + j],
                    recv_sem=recv_sems.at[4 * w + j], device_id=(x, y, 1 - c), device_id_type=pl.DeviceIdType.MESH)
                give.start()
                copies.append(give)
        for cp in copies:
            cp.wait()

    hbm = pl.BlockSpec(memory_space=pl.ANY)
    return pl.pallas_call(
        body, name=name, out_shape=[SDS((4,) + g.shape[1:], g.dtype) for g in grads], in_specs=[hbm] * nw, out_specs=[hbm] * nw,
        scratch_shapes=[pltpu.SemaphoreType.DMA((4 * nw,)), pltpu.SemaphoreType.DMA((4 * nw,))],
    )(*grads)


def _chip_exchange_call(parts, name):
    nw = len(parts)

    def body(*refs):
        p_refs, out_refs = refs[:nw], refs[nw:2 * nw]
        send_sems, recv_sems = refs[2 * nw:]
        x, y, c = lax.axis_index("x"), lax.axis_index("y"), lax.axis_index("c")
        chips = [(1 - x, y), (x, 1 - y), (1 - x, 1 - y)]
        copies = []
        for w in range(nw):
            for j, (px, py) in enumerate(chips):
                give = pltpu.make_async_remote_copy(
                    src_ref=p_refs[w].at[2 * px + py], dst_ref=out_refs[w].at[j], send_sem=send_sems.at[3 * w + j],
                    recv_sem=recv_sems.at[3 * w + j], device_id=(px, py, c), device_id_type=pl.DeviceIdType.MESH)
                give.start()
                copies.append(give)
        for cp in copies:
            cp.wait()

    hbm = pl.BlockSpec(memory_space=pl.ANY)
    return pl.pallas_call(
        body, name=name, out_shape=[SDS((3,) + p.shape[1:], p.dtype) for p in parts], in_specs=[hbm] * nw, out_specs=[hbm] * nw,
        scratch_shapes=[pltpu.SemaphoreType.DMA((3 * nw,)), pltpu.SemaphoreType.DMA((3 * nw,))],
    )(*parts)


_HBM = pl.BlockSpec(memory_space=pltpu.HBM)
_SEM = pl.BlockSpec(memory_space=pltpu.SEMAPHORE)
_DATAFLOW = pltpu.SideEffectType.DATAFLOW_SIDE_EFFECTING


def _spread_start_call(srcs, per_peer, name, after):
    nw = len(srcs)
    lands = [lax.empty((N_DEV,) + (s.shape[1:] if per_peer else s.shape), s.dtype) for s in srcs]

    def body(*refs):
        src_refs, land_refs = refs[:nw], refs[nw:2 * nw]
        send_sems, recv_sems, token = refs[2 * nw + 1], refs[2 * nw + 2], refs[-1]
        x, y, c = lax.axis_index("x"), lax.axis_index("y"), lax.axis_index("c")
        me = 4 * x + 2 * y + c
        for w in range(nw):
            for k in range(1, N_DEV):
                px = 1 - x if k & 4 else x
                py = 1 - y if k & 2 else y
                pc = 1 - c if k & 1 else c
                src = src_refs[w].at[4 * px + 2 * py + pc] if per_peer else src_refs[w]
                pltpu.make_async_remote_copy(
                    src_ref=src, dst_ref=land_refs[w].at[me], send_sem=send_sems.at[w], recv_sem=recv_sems.at[w],
                    device_id=(px, py, pc), device_id_type=pl.DeviceIdType.MESH).start()
        token[...] = jnp.zeros_like(token)

    hbm = lambda a: pltpu.with_memory_space_constraint(a, pltpu.HBM)
    res = pl.pallas_call(
        body, name=name,
        out_shape=(pltpu.SemaphoreType.DMA((nw,)), pltpu.SemaphoreType.DMA((nw,)))
        + tuple(pltpu.HBM(s.shape, s.dtype) for s in srcs) + tuple(pltpu.HBM(l.shape, l.dtype) for l in lands)
        + (SDS((SUBLANES, LANES), F32),),
        in_specs=[_HBM] * (2 * nw) + [pl.BlockSpec(memory_space=pl.ANY)],
        out_specs=(_SEM, _SEM) + (_HBM,) * (2 * nw) + (pl.BlockSpec(memory_space=pltpu.VMEM),),
        input_output_aliases={i: i + 2 for i in range(2 * nw)},
        compiler_params=pltpu.CompilerParams(has_side_effects=_DATAFLOW),
    )(*[hbm(s) for s in srcs], *[hbm(l) for l in lands], after)
    return res[0], res[1], res[2:2 + nw], res[2 + nw:2 + 2 * nw], res[-1]


def _spread_wait_call(send_sems, recv_sems, srcs, lands, after, name):
    nw = len(lands)

    def body(*refs):
        land_refs = refs[nw:2 * nw]
        s_sems, r_sems = refs[2 * nw], refs[2 * nw + 1]
        x, y, c = lax.axis_index("x"), lax.axis_index("y"), lax.axis_index("c")
        for w in range(nw):
            seven = land_refs[w].at[pl.ds(0, N_DEV - 1)]
            all_seven = pltpu.make_async_remote_copy(
                src_ref=seven, dst_ref=seven, send_sem=s_sems.at[w], recv_sem=r_sems.at[w],
                device_id=(x, y, c), device_id_type=pl.DeviceIdType.MESH)
            all_seven.wait_send()
            all_seven.wait_recv()

    res = pl.pallas_call(
        body, name=name,
        out_shape=tuple(pltpu.HBM(s.shape, s.dtype) for s in srcs) + tuple(pltpu.HBM(l.shape, l.dtype) for l in lands),
        in_specs=[_HBM] * (2 * nw) + [_SEM, _SEM, pl.BlockSpec(memory_space=pl.ANY)], out_specs=(_HBM,) * (2 * nw),
        input_output_aliases={i: i for i in range(2 * nw)},
        compiler_params=pltpu.CompilerParams(has_side_effects=_DATAFLOW),
    )(*srcs, *lands, send_sems, recv_sems, after)
    return res[:nw], res[nw:]


def _pair_sum_call(g, got, core, name):
    _, k, n = got.shape
    tr = _tile(k, 256)

    def body(c_ref, g_ref, got_ref, o_ref):
        o_ref[...] = (g_ref[...] + got_ref[...]).astype(o_ref.dtype)

    spec = pltpu.PrefetchScalarGridSpec(
        num_scalar_prefetch=1, grid=(4, k // tr),
        in_specs=[pl.BlockSpec((1, tr, n), lambda j, i, c: (2 * j + c[0], i, 0)), pl.BlockSpec((1, tr, n), lambda j, i, c: (j, i, 0))],
        out_specs=pl.BlockSpec((1, tr, n), lambda j, i, c: (j, i, 0)))
    return pl.pallas_call(body, name=name, grid_spec=spec, out_shape=SDS(got.shape, BF16),
                          compiler_params=_params(("parallel", "parallel"), VMEM_MID))(core, g, got)


def _adam_own_call(pair, chip, recv, w, m, v, name, rows):
    _, r, cols = recv.shape

    def body(chip_ref, p_ref, g_ref, w_ref, m_ref, v_ref, go_ref, d_ref, mo_ref, vo_ref):
        g = ((p_ref[0].astype(F32) + g_ref[0].astype(F32)) + g_ref[1].astype(F32)) + g_ref[2].astype(F32)
        for o_ref, val in zip((go_ref, d_ref, mo_ref, vo_ref), _adamw(g, w_ref[...], m_ref[...], v_ref[...])):
            o_ref[...] = val

    blk = pl.BlockSpec((rows, cols), lambda i, s: (i, 0))
    spec = pltpu.PrefetchScalarGridSpec(
        num_scalar_prefetch=1, grid=(r // rows,),
        in_specs=[pl.BlockSpec((1, rows, cols), lambda i, s: (s[0], i, 0)), pl.BlockSpec((3, rows, cols), lambda i, s: (0, i, 0)),
                  blk, blk, blk],
        out_specs=[blk, blk, blk, blk])
    return pl.pallas_call(body, name=name, grid_spec=spec, out_shape=[SDS((r, cols), F32)] * 4,
                          compiler_params=_params(("parallel",), VMEM_MID))(chip, pair, recv, w, m, v)


def _join_cols_call(w8, name):
    _, k, n = w8.shape
    tk = _tile(k, 256)

    def body(w_ref, o_ref):
        for s in range(N_DEV):
            o_ref[:, n * s:n * (s + 1)] = w_ref[s]

    return pl.pallas_call(body, name=name, grid=(k // tk,), in_specs=[pl.BlockSpec((N_DEV, tk, n), lambda i: (0, i, 0))],
                          out_specs=pl.BlockSpec((tk, N_DEV * n), lambda i: (i, 0)), out_shape=SDS((k, N_DEV * n), w8.dtype),
                          compiler_params=_params(("parallel",), VMEM_MID))(w8)


def _split_cols_call(g, name, dtype):
    k, n8 = g.shape
    n = n8 // N_DEV
    tk = _tile(k, 256)

    def body(g_ref, o_ref):
        for s in range(N_DEV):
            o_ref[s] = g_ref[:, n * s:n * (s + 1)].astype(dtype)

    return pl.pallas_call(body, name=name, grid=(k // tk,), in_specs=[pl.BlockSpec((tk, n8), lambda i: (i, 0))],
                          out_specs=pl.BlockSpec((N_DEV, tk, n), lambda i: (0, i, 0)), out_shape=SDS((N_DEV, k, n), dtype),
                          compiler_params=_params(("parallel",), VMEM_MID))(g)


def _pack(parts, rows_multiple):
    flat = jnp.concatenate([p.reshape(-1) for p in parts])
    unit = rows_multiple * LANES
    padded = -(-flat.shape[0] // unit) * unit
    flat = jnp.concatenate([flat, jnp.zeros((padded - flat.shape[0],), F32)])
    return flat.reshape(-1, LANES)


def _unpack(buf, shapes):
    flat = buf.reshape(-1)
    out, off = [], 0
    for s in shapes:
        n = math.prod(s)
        out.append(flat[off:off + n].reshape(s))
        off += n
    return out


def _row_tile(L):
    return 256 if L % 256 == 0 else L


def _layer0_mix(diff, const):
    x, mod, norm_w, lam_re, lam_im, log_dt, b_re, b_im, c_re, c_im, s5_d, *slots = diff
    ada_b, weights = const
    L = x.shape[0]
    tm = _row_tile(L)
    mods = mod.reshape(2, 1, D_MODEL)
    biases = ada_b.reshape(2, 1, D_MODEL)
    op_ln0 = make_rowwise(_f_lnmod, "ln0", tm, 1, 5, pass_first=True)
    h, x = op_ln0((x,), (norm_w.reshape(1, D_MODEL), mods[1], mods[0], biases[1], biases[0]))
    u, z = make_proj("s5_in")(h, tuple(weights), tuple(slots))
    blocks = _s5_block_params(lam_re, lam_im, log_dt, b_re, b_im, c_re, c_im)
    ys, u = make_s5_core(min(S5_TL, L))(u, *blocks)
    (y2,) = make_rowwise(_f_s5_act, "s5_act", tm, 2, 1)((ys, u), (s5_d.reshape(1, D_INNER),))
    return x, y2, z


def _layer0_out(diff, const):
    x, y2, z, mod, *slots = diff
    ada_b, weights = const
    tm = _row_tile(x.shape[0])
    t, y2 = make_mm("s5_glu", pass_input=True)(y2, weights[0], slots[0])
    (y4,) = make_rowwise(_f_s5_gate, "s5_gate", tm, 3, 0)((y2, t, z), ())
    o = make_mm("s5_out")(y4, weights[1], slots[1])
    return make_residual("res0", tm)(x, o, mod.reshape(1, D_MODEL), ada_b.reshape(1, D_MODEL))


def _layer1_loss(diff, const):
    x1, mod, norm_w, conv_w, a_log, dt_bias, gdn_nw, final_nw, *slots = diff
    tgt, ada_b, weights = const
    L = x1.shape[0]
    tm = _row_tile(L)
    mods = mod.reshape(3, 1, D_MODEL)
    biases = ada_b.reshape(3, 1, D_MODEL)
    op_ln1 = make_rowwise(_f_lnmod, "ln1", tm, 1, 5, pass_first=True)
    h, x1 = op_ln1((x1,), (norm_w.reshape(1, D_MODEL), mods[1], mods[0], biases[1], biases[0]))
    q0, k0, v0, gz, ba = make_proj("gdn_in")(h, tuple(weights[0:5]), tuple(slots[0:5]))
    cw = jnp.concatenate([conv_w, jnp.zeros((SUBLANES - GDN_CONV, GDN_CONV_CH), F32)], axis=0)
    q = make_conv_act(lambda t: _l2n(_silu(t)) * (GDN_DK ** -0.5), "gdn_conv_q")(q0, cw[:, :GDN_QK])
    k = make_conv_act(lambda t: _l2n(_silu(t)), "gdn_conv_k")(k0, cw[:, GDN_QK:2 * GDN_QK])
    v = make_conv_act(_silu, "gdn_conv_v")(v0, cw[:, 2 * GDN_QK:])
    pad = jnp.zeros((LANES - 2 * GDN_HEADS,), F32)
    alog_row = jnp.concatenate([jnp.zeros((GDN_HEADS,), F32), a_log, pad]).reshape(1, LANES)
    dtb_row = jnp.concatenate([jnp.zeros((GDN_HEADS,), F32), dt_bias, pad]).reshape(1, LANES)
    (bg,) = make_rowwise(_f_betag, "gdn_bg", tm, 1, 2)((ba,), (alog_row, dtb_row))
    og = gdn_scan(*gdn_prep(q, k, v, bg))
    nw_row = jnp.tile(gdn_nw, GDN_HEADS).reshape(1, D_INNER)
    (on,) = make_rowwise(_f_gdn_post, "gdn_post", tm, 2, 1)((og, gz), (nw_row,))
    y = make_mm("gdn_out")(on, weights[5], slots[5])
    x2 = make_residual("res1", tm)(x1, y, mods[2], biases[2])

    (lt,) = make_rowwise(_f_loss, "loss", tm, 2, 1)((x2, tgt), (final_nw.reshape(1, D_MODEL),))
    return jnp.sum(lt)


VEC_NAMES = ("ada_b", "norm_w", "s5_lambda_re", "s5_lambda_im", "s5_log_dt", "s5_d", "gdn_a_log", "gdn_dt_bias", "final_norm_w")
MAT_NAMES = ("s5_b_re", "s5_b_im", "s5_c_re", "s5_c_im")
S5_BIG = ("s5_w_in", "s5_w_glu", "s5_w_out")
GDN_BIG = ("gdn_w_in", "gdn_w_out")
BIG_NAMES = S5_BIG + GDN_BIG
WEIGHT_ORDER = ("ada_w", "ada_b", "norm_w", "s5_w_in", "s5_lambda_re", "s5_lambda_im", "s5_log_dt", "s5_b_re", "s5_b_im",
                "s5_c_re", "s5_c_im", "s5_d", "s5_w_glu", "s5_w_out", "gdn_w_in", "gdn_conv_w", "gdn_a_log", "gdn_dt_bias",
                "gdn_norm_w", "gdn_w_out", "final_norm_w")


def _step(x, c, W, M, V, tgt):
    L = x.shape[1]
    ix, iy, ic = lax.axis_index("x"), lax.axis_index("y"), lax.axis_index("c")
    me = 4 * ix + 2 * iy + ic
    n_ada = W["ada_w"].shape[2]
    n_conv = W["gdn_conv_w"].shape[2]
    n_gnw = W["gdn_norm_w"].shape[1]

    g1 = _allgather_call(_pack([c, W["gdn_conv_w"], W["gdn_norm_w"]], SUBLANES), "gather_small_in", False)
    g1 = g1.reshape(N_DEV, -1)
    c_all = g1[:, :D_MODEL]
    conv_w = g1[:, D_MODEL:D_MODEL + GDN_CONV * n_conv].reshape(N_DEV, GDN_CONV, n_conv).transpose(1, 0, 2).reshape(GDN_CONV, -1)
    gdn_nw = g1[:, D_MODEL + GDN_CONV * n_conv:D_MODEL + GDN_CONV * n_conv + n_gnw].reshape(-1)
    mod_part = _ada_mod_call(c_all, W["ada_w"])
    g2 = _allgather_call(_pack([mod_part], SUBLANES), "gather_mod", False).reshape(N_DEV, -1)
    mod_all = g2[:, :2 * N_DEV * n_ada].reshape(N_DEV, 2, N_DEV, n_ada)
    mod_raw = lax.dynamic_index_in_dim(mod_all, me, axis=2, keepdims=False)
    mod_raw = mod_raw.transpose(1, 0, 2).reshape(2, 3 * D_MODEL)

    shard = lambda n: W[n][0].astype(BF16)
    (w_in5_parts,) = _gather_weights_call([shard("s5_w_in")], "gather_s5_w_in")
    late = _spread_start_call([shard("s5_w_glu"), shard("s5_w_out")], False, "gather_s5_late_start", w_in5_parts)
    g_send, g_recv, g_srcs, g_lands, g_token = _spread_start_call([shard(n) for n in GDN_BIG], False, "gather_gdn_start", late[4])
    w_in5 = _join_cols_call(w_in5_parts, "join_s5_w_in")
    slot = lambda *s: jnp.zeros(s, F32)
    two = 2 * D_MODEL
    diff_mix = (x[0], mod_raw[0, :two] + g_token[0, 0], W["norm_w"][0], W["s5_lambda_re"][0], W["s5_lambda_im"][0], W["s5_log_dt"][0],
                W["s5_b_re"][0], W["s5_b_im"][0], W["s5_c_re"][0], W["s5_c_im"][0], W["s5_d"][0],
                slot(D_MODEL, D_INNER), slot(D_MODEL, D_INNER))

    (xp, y2, z5), vjp_mix = jax.vjp(lambda d: _layer0_mix(d, (W["ada_b"][0, :two], (w_in5[:, :D_INNER], w_in5[:, D_INNER:]))), diff_mix)
    l_srcs, l_lands = _spread_wait_call(late[0], late[1], late[2], late[3], y2, "gather_s5_late_wait")
    w_glu, w_o5 = [lax.dynamic_update_slice(land, src[None], (me, 0, 0)).reshape(-1, src.shape[1]) for land, src in zip(l_lands, l_srcs)]
    diff_out = (xp, y2, z5, mod_raw[0, two:], slot(D_INNER, D_INNER), slot(D_INNER, D_MODEL))
    x1, vjp_out = jax.vjp(lambda d: _layer0_out(d, (W["ada_b"][0, two:], (w_glu, w_o5))), diff_out)
    g_srcs, g_lands = _spread_wait_call(g_send, g_recv, g_srcs, g_lands, x1, "gather_gdn_wait")
    gdn_full = [lax.dynamic_update_slice(land, src[None], (me, 0, 0)) for land, src in zip(g_lands, g_srcs)]
    w_ing = _join_cols_call(gdn_full[0], "join_gdn_w_in")
    w_ba = jnp.concatenate([w_ing[:, GDN_CONV_CH + D_INNER:], jnp.zeros((D_MODEL, LANES - 2 * GDN_HEADS), BF16)], axis=1)
    weights1 = (w_ing[:, :GDN_QK], w_ing[:, GDN_QK:2 * GDN_QK], w_ing[:, 2 * GDN_QK:GDN_CONV_CH],
                w_ing[:, GDN_CONV_CH:GDN_CONV_CH + D_INNER], w_ba, gdn_full[1].reshape(D_INNER, D_MODEL))
    slots1 = tuple(jnp.zeros(w.shape, F32) for w in weights1)
    diff1 = (x1, mod_raw[1], W["norm_w"][1], conv_w, W["gdn_a_log"][0], W["gdn_dt_bias"][0], gdn_nw, W["final_norm_w"], *slots1)
    loss_local, vjp1 = jax.vjp(lambda d: _layer1_loss(d, (tgt[0], W["ada_b"][1], weights1)), diff1)
    ((dx1, dmod1, d_norm_w1, d_conv, d_alog, d_dtb, d_gnw, d_fnw, d_wq, d_wk, d_wv, d_wgz, d_wba, d_wog),) = vjp1(jnp.ones((), F32))
    loss = lax.psum(loss_local, MESH_AXES)

    rows = lambda d: d.reshape(N_DEV, d.shape[0] // N_DEV, d.shape[1])
    d_ing = _split_cols_call(jnp.concatenate([d_wq, d_wk, d_wv, d_wgz, d_wba[:, :2 * GDN_HEADS]], axis=1), "split_gdn_w_in", BF16)
    s_send, s_recv, s_srcs, s_lands, s_token = _spread_start_call([d_ing, rows(d_wog).astype(BF16)], True, "scatter_gdn_start", dx1)
    ((dxp, dy2, dz5, dmod_gate, d_wglu, d_wo5),) = vjp_out(dx1.at[0, 0].add(s_token[0, 0]))
    t_send, t_recv, t_srcs, t_lands, t_token = _spread_start_call(
        [rows(d_wglu).astype(BF16), rows(d_wo5).astype(BF16)], True, "scatter_s5_late_start", dxp)
    ((dx, dmod_ss, d_norm_w0, d_lre, d_lim, d_logdt, d_bre, d_bim, d_cre, d_cim, d_s5d, d_wu, d_wz),) = vjp_mix(
        (dxp.at[0, 0].add(t_token[0, 0]), dy2, dz5))
    dmod = jnp.stack([jnp.concatenate([dmod_ss, dmod_gate]), dmod1])
    d_norm_w = jnp.stack([d_norm_w0, d_norm_w1])
    d_in5 = _split_cols_call(jnp.concatenate([d_wu, d_wz], axis=1), "split_s5_w_in", F32)
    (got,) = _pair_exchange_call([d_in5], "scatter_s5_pair")
    core = jnp.reshape(ic, (1,)).astype(jnp.int32)
    chip = jnp.reshape(2 * ix + iy, (1,)).astype(jnp.int32)
    pair = _pair_sum_call(d_in5, got, core, "pair_sum_s5_w_in")
    (recv,) = _chip_exchange_call([pair], "scatter_s5_chips")
    big = {"s5_w_in": _adam_own_call(pair, chip, recv, W["s5_w_in"][0], M["s5_w_in"][0], V["s5_w_in"][0], "adam_s5_w_in", 128)}
    t_srcs, t_lands = _spread_wait_call(t_send, t_recv, t_srcs, t_lands, dx, "scatter_s5_late_wait")
    s_srcs, s_lands = _spread_wait_call(s_send, s_recv, s_srcs, s_lands, t_lands[0], "scatter_gdn_wait")
    for land, src, n in zip(tuple(t_lands) + tuple(s_lands), tuple(t_srcs) + tuple(s_srcs), ("s5_w_glu", "s5_w_out") + GDN_BIG):
        mine = lax.dynamic_index_in_dim(src, me, 0, keepdims=True)
        parts = lax.dynamic_update_slice(land, mine, (me, 0, 0))
        big[n] = _adam_call(parts, W[n][0], M[n][0], V[n][0], "adam_" + n, rows=_tile(W[n].shape[1], 128))
    big = [[o[None] for o in big[n]] for n in BIG_NAMES]

    vec_parts = [dmod, d_norm_w, d_lre, d_lim, d_logdt, d_s5d, d_alog, d_dtb, d_fnw]
    tail_parts = [d_conv, d_gnw]
    mat_parts = [d_bre, d_bim, d_cre, d_cim]
    n_vec = sum(math.prod(p.shape) for p in vec_parts)
    sg_vec, sg_mat = _gather_weights_call(
        [_pack(vec_parts + tail_parts, ADAM_ROWS), _pack(mat_parts, SUBLANES).astype(BF16)], "gather_small_grads")
    tot_vec = _sum_call(sg_vec, "sum_vec_grads", ADAM_ROWS)
    tot_mat = _sum_call(sg_mat, "sum_mat_grads", ADAM_ROWS)
    g_conv, g_gnw = _unpack(tot_vec.reshape(-1)[n_vec:], [d_conv.shape, d_gnw.shape])
    g_conv_mine = lax.dynamic_slice_in_dim(g_conv, me * n_conv, n_conv, axis=1)
    g_gnw_mine = lax.dynamic_slice_in_dim(g_gnw, me * n_gnw, n_gnw, axis=0)
    vec_names = VEC_NAMES + ("gdn_conv_w", "gdn_norm_w")
    vec_g = _pack([tot_vec.reshape(-1)[:n_vec], g_conv_mine, g_gnw_mine], ADAM_ROWS)
    vec = _adam_call(vec_g[None], _pack([W[n] for n in vec_names], ADAM_ROWS), _pack([M[n] for n in vec_names], ADAM_ROWS),
                     _pack([V[n] for n in vec_names], ADAM_ROWS), "adam_vec")
    vec = [_unpack(b, [W[n].shape for n in vec_names]) for b in vec]
    mats = []
    for name, g_mat in zip(MAT_NAMES, _unpack(tot_mat, [p.shape for p in mat_parts])):
        two_d = (-1, W[name].shape[-1])
        outs = _adam_call(g_mat.reshape(two_d)[None], W[name].reshape(two_d), M[name].reshape(two_d), V[name].reshape(two_d),
                          "adam_" + name, rows=1024)
        mats.append([o.reshape(W[name].shape) for o in outs])

    dmod_all = sg_vec[:, :2 * 3 * D_MODEL // LANES].reshape(N_DEV, 2, N_DEV, n_ada // LANES, LANES)
    dmod_mine = lax.dynamic_index_in_dim(dmod_all, me, axis=2, keepdims=False).transpose(1, 0, 2, 3).reshape(2, N_DEV, n_ada)
    g_ada_w = _ada_grad_call(c_all, dmod_mine)
    ada = _adam_call(g_ada_w.reshape(1, -1, LANES), W["ada_w"].reshape(-1, LANES), M["ada_w"].reshape(-1, LANES),
                     V["ada_w"].reshape(-1, LANES), "adam_ada")
    ada = [a.reshape(W["ada_w"].shape) for a in ada]

    res = {}
    for i, n in enumerate(BIG_NAMES):
        res[n] = big[i]
    for i, n in enumerate(vec_names):
        res[n] = [b[i] for b in vec]
    for i, n in enumerate(MAT_NAMES):
        res[n] = mats[i]
    res["ada_w"] = ada
    outs = [loss, dx[None]]
    for j in range(4):
        outs += [res[n][j] for n in WEIGHT_ORDER]
    return tuple(outs)


def kernel(x, c, ada_w, ada_b, norm_w, s5_w_in, s5_lambda_re, s5_lambda_im, s5_log_dt, s5_b_re, s5_b_im, s5_c_re, s5_c_im, s5_d, s5_w_glu, s5_w_out, gdn_w_in, gdn_conv_w, gdn_a_log, gdn_dt_bias, gdn_norm_w, gdn_w_out, final_norm_w, loss_target, m_ada_w, m_ada_b, m_norm_w, m_s5_w_in, m_s5_lambda_re, m_s5_lambda_im, m_s5_log_dt, m_s5_b_re, m_s5_b_im, m_s5_c_re, m_s5_c_im, m_s5_d, m_s5_w_glu, m_s5_w_out, m_gdn_w_in, m_gdn_conv_w, m_gdn_a_log, m_gdn_dt_bias, m_gdn_norm_w, m_gdn_w_out, m_final_norm_w, v_ada_w, v_ada_b, v_norm_w, v_s5_w_in, v_s5_lambda_re, v_s5_lambda_im, v_s5_log_dt, v_s5_b_re, v_s5_b_im, v_s5_c_re, v_s5_c_im, v_s5_d, v_s5_w_glu, v_s5_w_out, v_gdn_w_in, v_gdn_conv_w, v_gdn_a_log, v_gdn_dt_bias, v_gdn_norm_w, v_gdn_w_out, v_final_norm_w):
    W = dict(ada_w=ada_w, ada_b=ada_b, norm_w=norm_w, s5_w_in=s5_w_in, s5_lambda_re=s5_lambda_re, s5_lambda_im=s5_lambda_im,
             s5_log_dt=s5_log_dt, s5_b_re=s5_b_re, s5_b_im=s5_b_im, s5_c_re=s5_c_re, s5_c_im=s5_c_im, s5_d=s5_d,
             s5_w_glu=s5_w_glu, s5_w_out=s5_w_out, gdn_w_in=gdn_w_in, gdn_conv_w=gdn_conv_w, gdn_a_log=gdn_a_log,
             gdn_dt_bias=gdn_dt_bias, gdn_norm_w=gdn_norm_w, gdn_w_out=gdn_w_out, final_norm_w=final_norm_w)
    M = dict(ada_w=m_ada_w, ada_b=m_ada_b, norm_w=m_norm_w, s5_w_in=m_s5_w_in, s5_lambda_re=m_s5_lambda_re,
             s5_lambda_im=m_s5_lambda_im, s5_log_dt=m_s5_log_dt, s5_b_re=m_s5_b_re, s5_b_im=m_s5_b_im, s5_c_re=m_s5_c_re,
             s5_c_im=m_s5_c_im, s5_d=m_s5_d, s5_w_glu=m_s5_w_glu, s5_w_out=m_s5_w_out, gdn_w_in=m_gdn_w_in,
             gdn_conv_w=m_gdn_conv_w, gdn_a_log=m_gdn_a_log, gdn_dt_bias=m_gdn_dt_bias, gdn_norm_w=m_gdn_norm_w,
             gdn_w_out=m_gdn_w_out, final_norm_w=m_final_norm_w)
    V = dict(ada_w=v_ada_w, ada_b=v_ada_b, norm_w=v_norm_w, s5_w_in=v_s5_w_in, s5_lambda_re=v_s5_lambda_re,
             s5_lambda_im=v_s5_lambda_im, s5_log_dt=v_s5_log_dt, s5_b_re=v_s5_b_re, s5_b_im=v_s5_b_im, s5_c_re=v_s5_c_re,
             s5_c_im=v_s5_c_im, s5_d=v_s5_d, s5_w_glu=v_s5_w_glu, s5_w_out=v_s5_w_out, gdn_w_in=v_gdn_w_in,
             gdn_conv_w=v_gdn_conv_w, gdn_a_log=v_gdn_a_log, gdn_dt_bias=v_gdn_dt_bias, gdn_norm_w=v_gdn_norm_w,
             gdn_w_out=v_gdn_w_out, final_norm_w=v_final_norm_w)
    return _step(x, c, W, M, V, loss_target)
```

```python
import functools
import math

import jax
import jax.numpy as jnp
from jax import lax
from jax.experimental import pallas as pl
from jax.experimental.pallas import tpu as pltpu

F32 = jnp.float32
BF16 = jnp.bfloat16
SDS = jax.ShapeDtypeStruct

D_MODEL = 1024
D_INNER = 2048
NORM_EPS = 1e-6
S5_GROUP = 16
S5_GROUPS = 128
S5_STATE = 64
GDN_HEADS = 8
GDN_DK = 128
GDN_DV = 256
GDN_CONV = 4
GDN_CHUNK = 64
GDN_QK = 1024
GDN_CONV_CH = 4096
GDN_PROJ = 6160
ADAM_LR = 0.001
ADAM_B1 = 0.9
ADAM_B2 = 0.999
ADAM_EPS = 1e-08
ADAM_WD = 0.01
ADAM_STEP = 10

N_DEV = 8
LANES = 128
SUBLANES = 8
VMEM_BIG = 56 << 20
VMEM_MID = 40 << 20
S5_GB = 8
S5_TL = 1024
MESH_AXES = ("x", "y", "c")


def _params(sem, vmem=None):
    return pltpu.CompilerParams(dimension_semantics=sem, vmem_limit_bytes=vmem)


def _bdot(a, b, dims=(((1,), (0,)), ((), ()))):
    return lax.dot_general(a.astype(BF16), b.astype(BF16), dims, preferred_element_type=F32)


def _hdot(a, b, dims=(((1,), (0,)), ((), ()))):
    return lax.dot_general(a, b, dims, preferred_element_type=F32, precision=lax.Precision.HIGHEST)


_BNN = (((2,), (1,)), ((0,), (0,)))
_BNT = (((2,), (2,)), ((0,), (0,)))
_BTN = (((1,), (1,)), ((0,), (0,)))


@jax.custom_vjp
def _unit_lower_inverse(a):
    c = a.shape[-1]
    ri = lax.broadcasted_iota(jnp.int32, a.shape, 1)
    ci = lax.broadcasted_iota(jnp.int32, a.shape, 2)
    n = -a
    t = (ri == ci).astype(F32) + n
    for _ in range(int(math.log2(c)) - 1):
        n = _hdot(n, n, _BNN)
        t = t + _hdot(t, n, _BNN)
    return t


def _unit_lower_inverse_fwd(a):
    t = _unit_lower_inverse(a)
    return t, t


def _unit_lower_inverse_bwd(t, g):
    return (-_hdot(_hdot(t, g, _BTN), t, _BNT),)


_unit_lower_inverse.defvjp(_unit_lower_inverse_fwd, _unit_lower_inverse_bwd)


NN = (((1,), (0,)), ((), ()))
NT = (((1,), (1,)), ((), ()))
TN = (((0,), (0,)), ((), ()))


def _tile(n, pref):
    for t in (pref, 512, 256, 128):
        if t <= n and n % t == 0:
            return t
    return n


def _matmul(a, b, mode, name, add=None):
    if mode == "nn":
        (m, k), (_, n) = a.shape, b.shape
    elif mode == "nt":
        (m, k), (n, _) = a.shape, b.shape
    else:
        (k, m), (_, n) = a.shape, b.shape
    tm, tn, tk = _tile(m, 512), _tile(n, 512), (k if k <= 2048 else _tile(k, 512))
    if mode == "tn":
        tm, tn = _tile(m, 1024), _tile(n, 1024)
    nk = k // tk
    dims = {"nn": NN, "nt": NT, "tn": TN}[mode]

    def body(a_ref, b_ref, *rest):
        o_ref, acc_ref = rest[-2], rest[-1]
        kk = pl.program_id(2)

        @pl.when(kk == 0)
        def _():
            acc_ref[...] = jnp.zeros_like(acc_ref) if add is None else rest[0][...]
        acc_ref[...] += _bdot(a_ref[...], b_ref[...], dims)

        @pl.when(kk == nk - 1)
        def _():
            o_ref[...] = acc_ref[...]

    a_spec = pl.BlockSpec((tk, tm), lambda i, j, q: (q, i)) if mode == "tn" else pl.BlockSpec((tm, tk), lambda i, j, q: (i, q))
    b_spec = pl.BlockSpec((tn, tk), lambda i, j, q: (j, q)) if mode == "nt" else pl.BlockSpec((tk, tn), lambda i, j, q: (q, j))
    o_spec = pl.BlockSpec((tm, tn), lambda i, j, q: (i, j))
    return pl.pallas_call(
        body, name=name, grid=(m // tm, n // tn, nk),
        in_specs=[a_spec, b_spec] + ([] if add is None else [o_spec]), out_specs=o_spec,
        out_shape=SDS((m, n), F32), scratch_shapes=[pltpu.VMEM((tm, tn), F32)],
        compiler_params=_params(("parallel", "parallel", "arbitrary"), VMEM_MID),
    )(a, b, *([] if add is None else [add]))


def make_mm(name, pass_input=False):
    def primal(a, w):
        out = _matmul(a, w, "nn", name + "_fwd")
        return (out, a) if pass_input else out

    @jax.custom_vjp
    def mm(a, w, grad_slot):
        return primal(a, w)

    def fwd(a, w, grad_slot):
        return primal(a, w), (a, w)

    def bwd(res, g):
        a, w = res
        g, g_other = g if pass_input else (g, None)
        return _matmul(g, w, "nt", name + "_dx", add=g_other), jnp.zeros_like(w), _matmul(a, g, "tn", name + "_dw")

    mm.defvjp(fwd, bwd)
    return mm


PROJ_ROWS = 256


def _proj_fwd_call(a, ws, name):
    m, k = a.shape
    tm = _tile(m, PROJ_ROWS)
    nw = len(ws)

    def body(*refs):
        ab = refs[0][...].astype(BF16)
        for w_ref, o_ref in zip(refs[1:1 + nw], refs[1 + nw:]):
            o_ref[...] = lax.dot_general(ab, w_ref[...], NN, preferred_element_type=F32)

    return pl.pallas_call(
        body, name=name, grid=(m // tm,),
        in_specs=[pl.BlockSpec((tm, k), lambda i: (i, 0))] + [pl.BlockSpec(w.shape, lambda i: (0, 0)) for w in ws],
        out_specs=[pl.BlockSpec((tm, w.shape[1]), lambda i: (i, 0)) for w in ws],
        out_shape=[SDS((m, w.shape[1]), F32) for w in ws],
        compiler_params=_params(("parallel",), VMEM_BIG),
    )(a, *ws)


def _proj_dx_call(gs, ws, name):
    m = gs[0].shape[0]
    k = ws[0].shape[0]
    tm = _tile(m, PROJ_ROWS)
    nw = len(ws)

    def body(*refs):
        acc = None
        for g_ref, w_ref in zip(refs[:nw], refs[nw:2 * nw]):
            part = _bdot(g_ref[...], w_ref[...], NT)
            acc = part if acc is None else acc + part
        refs[2 * nw][...] = acc

    return pl.pallas_call(
        body, name=name, grid=(m // tm,),
        in_specs=[pl.BlockSpec((tm, g.shape[1]), lambda i: (i, 0)) for g in gs] + [pl.BlockSpec(w.shape, lambda i: (0, 0)) for w in ws],
        out_specs=pl.BlockSpec((tm, k), lambda i: (i, 0)), out_shape=SDS((m, k), F32),
        compiler_params=_params(("parallel",), VMEM_BIG),
    )(*gs, *ws)


def make_proj(name):
    @jax.custom_vjp
    def proj(a, ws, grad_slots):
        return tuple(_proj_fwd_call(a, ws, name + "_fwd"))

    def fwd(a, ws, grad_slots):
        return tuple(_proj_fwd_call(a, ws, name + "_fwd")), (a, ws)

    def bwd(res, gs):
        a, ws = res
        dws = tuple(_matmul(a, g, "tn", "%s_dw%d" % (name, i)) for i, g in enumerate(gs))
        return _proj_dx_call(tuple(gs), ws, name + "_dx"), tuple(jnp.zeros_like(w) for w in ws), dws

    proj.defvjp(fwd, bwd)
    return proj


def make_rowwise(f, name, tm, n_rows, n_params, vmem=VMEM_MID, pass_first=False):
    def specs_of(arrs, blocked):
        if blocked:
            return [pl.BlockSpec((tm, a.shape[1]), lambda i: (i, 0)) for a in arrs]
        return [pl.BlockSpec(a.shape, lambda i: (0, 0)) for a in arrs]

    def out_structs(rows, params):
        blk = [SDS((tm, r.shape[1]), r.dtype) for r in rows] + [SDS(p.shape, p.dtype) for p in params]
        return jax.eval_shape(f, *blk)

    def run_fwd(rows, params):
        L = rows[0].shape[0]
        outs = out_structs(rows, params)

        def body(*refs):
            ins = [r[...] for r in refs[:n_rows + n_params]]
            res = f(*ins)
            for o_ref, val in zip(refs[n_rows + n_params:], res):
                o_ref[...] = val

        return pl.pallas_call(
            body, name=name + "_fwd", grid=(L // tm,),
            in_specs=specs_of(rows, True) + specs_of(params, False),
            out_specs=[pl.BlockSpec((tm, o.shape[1]), lambda i: (i, 0)) for o in outs],
            out_shape=[SDS((L, o.shape[1]), o.dtype) for o in outs],
            compiler_params=_params(("parallel",), vmem),
        )(*rows, *params)

    def run_bwd(rows, params, gs):
        L = rows[0].shape[0]
        n_g = len(gs)

        def body(*refs):
            i = pl.program_id(0)
            ins = [r[...] for r in refs[:n_rows + n_params]]
            cts = tuple(r[...] for r in refs[n_rows + n_params:n_rows + n_params + n_g])
            outs = refs[n_rows + n_params + n_g:]
            _, vjp = jax.vjp(f, *ins)
            grads = vjp(cts[:-1] if pass_first else cts)
            if pass_first:
                grads = (grads[0] + cts[-1],) + tuple(grads[1:])
            for o_ref, val in zip(outs[:n_rows], grads[:n_rows]):
                o_ref[...] = val

            if n_params:
                @pl.when(i == 0)
                def _():
                    for o_ref in outs[n_rows:]:
                        o_ref[...] = jnp.zeros_like(o_ref)
                for o_ref, val in zip(outs[n_rows:], grads[n_rows:]):
                    o_ref[...] += val

        res = pl.pallas_call(
            body, name=name + "_bwd", grid=(L // tm,),
            in_specs=specs_of(rows, True) + specs_of(params, False) + specs_of(gs, True),
            out_specs=specs_of(rows, True) + specs_of(params, False),
            out_shape=[SDS(r.shape, r.dtype) for r in rows] + [SDS(p.shape, p.dtype) for p in params],
            compiler_params=_params(("arbitrary",), vmem),
        )(*rows, *params, *gs)
        return tuple(res[:n_rows]), tuple(res[n_rows:])

    def outputs(rows, params):
        outs = tuple(run_fwd(rows, params))
        return outs + (rows[0],) if pass_first else outs

    @jax.custom_vjp
    def op(rows, params):
        return outputs(rows, params)

    def fwd(rows, params):
        return outputs(rows, params), (rows, params)

    def bwd(res, gs):
        rows, params = res
        return run_bwd(rows, params, tuple(gs))

    op.defvjp(fwd, bwd)
    op.run_fwd, op.run_bwd = run_fwd, run_bwd
    return op


def make_residual(name, tm):
    full = make_rowwise(_f_res, name, tm, 2, 2)
    branch = make_rowwise(lambda y, gate, bgate: ((gate + bgate) * y,), name + "_branch", tm, 1, 2)

    @jax.custom_vjp
    def op(x, y, gate, bgate):
        return full.run_fwd((x, y), (gate, bgate))[0]

    def fwd(x, y, gate, bgate):
        return full.run_fwd((x, y), (gate, bgate))[0], (y, gate, bgate)

    def bwd(res, g):
        y, gate, bgate = res
        (dy,), (dgate, dbgate) = branch.run_bwd((y,), (gate, bgate), (g,))
        return g, dy, dgate, dbgate

    op.defvjp(fwd, bwd)
    return op


def _s5_scan_rows(xr_ref, xi_ref, ar, ai, x0r, x0i, tl, reverse=False):
    n = xr_ref.shape[1]
    T = SUBLANES
    row = lax.broadcasted_iota(jnp.int32, (T, n), 0)
    pr, pi = [ar], [ai]
    for _ in range(T - 1):
        pr, pi = pr + [pr[-1] * ar - pi[-1] * ai], pi + [pr[-1] * ai + pi[-1] * ar]
    levels = []
    for d in (1, 2, 4):
        mask = (row < T - d) if reverse else (row >= d)
        levels.append((T - d if reverse else d, jnp.where(mask, pr[d - 1], 0.0), jnp.where(mask, pi[d - 1], 0.0)))
    cr = jnp.zeros((T, n), F32)
    ci = jnp.zeros((T, n), F32)
    for r in range(T):
        k = (T - r) if reverse else (r + 1)
        cr = jnp.where(row == r, pr[k - 1], cr)
        ci = jnp.where(row == r, pi[k - 1], ci)
    nt = tl // T
    last = 0 if reverse else T - 1

    def step(t, carry):
        sr, si = carry
        base = pl.multiple_of((nt - 1 - t if reverse else t) * T, T)
        br = xr_ref[pl.ds(base, T), :]
        bi = xi_ref[pl.ds(base, T), :]
        for shift, mr, mi in levels:
            qr = pltpu.roll(br, shift, 0)
            qi = pltpu.roll(bi, shift, 0)
            br, bi = br + (mr * qr - mi * qi), bi + (mr * qi + mi * qr)
        xr = br + (cr * sr - ci * si)
        xi = bi + (cr * si + ci * sr)
        xr_ref[pl.ds(base, T), :] = xr
        xi_ref[pl.ds(base, T), :] = xi
        return xr[last:last + 1, :], xi[last:last + 1, :]
    return lax.fori_loop(0, nt, step, (x0r, x0i))


def _s5_fwd_call(u, bre, bim, cre, cim, a, tl):
    L, e = u.shape
    nb = e // LANES
    ns = bre.shape[2]
    nc = L // tl

    def body(u_ref, bre_ref, bim_ref, cre_ref, cim_ref, a_ref, ys_ref, xb_ref, sr_ref, si_ref, xr_ref, xi_ref, carry_ref):
        c = pl.program_id(1)

        @pl.when(c == 0)
        def _():
            carry_ref[...] = jnp.zeros_like(carry_ref)
        xb_ref[0, 0] = carry_ref[...]
        ub = u_ref[...]
        xr_ref[...] = _bdot(ub, bre_ref[0])
        xi_ref[...] = _bdot(ub, bim_ref[0])
        ar = a_ref[0, 0:1, :]
        ai = a_ref[0, 1:2, :]
        xr, xi = _s5_scan_rows(xr_ref, xi_ref, ar, ai, carry_ref[0:1, :], carry_ref[1:2, :], tl)
        carry_ref[0:1, :] = xr
        carry_ref[1:2, :] = xi
        sr = xr_ref[...].astype(BF16)
        si = xi_ref[...].astype(BF16)
        sr_ref[...] = sr
        si_ref[...] = si
        ys_ref[...] = _bdot(sr, cre_ref[0]) - _bdot(si, cim_ref[0])

    return pl.pallas_call(
        body, name="s5_core_fwd", grid=(nb, nc),
        in_specs=[pl.BlockSpec((tl, LANES), lambda j, c: (c, j)),
                  pl.BlockSpec((1, LANES, ns), lambda j, c: (j, 0, 0)), pl.BlockSpec((1, LANES, ns), lambda j, c: (j, 0, 0)),
                  pl.BlockSpec((1, ns, LANES), lambda j, c: (j, 0, 0)), pl.BlockSpec((1, ns, LANES), lambda j, c: (j, 0, 0)),
                  pl.BlockSpec((1, SUBLANES, ns), lambda j, c: (j, 0, 0))],
        out_specs=[pl.BlockSpec((tl, LANES), lambda j, c: (c, j)),
                   pl.BlockSpec((1, 1, SUBLANES, ns), lambda j, c: (j, c, 0, 0)),
                   pl.BlockSpec((tl, ns), lambda j, c: (c, j)), pl.BlockSpec((tl, ns), lambda j, c: (c, j))],
        out_shape=[SDS((L, e), F32), SDS((nb, nc, SUBLANES, ns), F32), SDS((L, nb * ns), BF16), SDS((L, nb * ns), BF16)],
        scratch_shapes=[pltpu.VMEM((tl, ns), F32), pltpu.VMEM((tl, ns), F32), pltpu.VMEM((SUBLANES, ns), F32)],
        compiler_params=_params(("arbitrary", "arbitrary"), VMEM_MID),
    )(u, bre, bim, cre, cim, a)


def _s5_bwd_call(u, dys, du_other, bre, bim, cre, cim, a, xb, sr, si, tl):
    L, e = u.shape
    nb = e // LANES
    ns = bre.shape[2]
    nc = L // tl

    def body(u_ref, dys_ref, duo_ref, bre_ref, bim_ref, cre_ref, cim_ref, a_ref, xb_ref, sr_ref, si_ref,
             du_ref, dbre_ref, dbim_ref, dcre_ref, dcim_ref, da_ref,
             gr_ref, gi_ref, gcarry_ref):
        c = pl.program_id(1)

        @pl.when(c == 0)
        def _():
            gcarry_ref[...] = jnp.zeros_like(gcarry_ref)
            dbre_ref[...] = jnp.zeros_like(dbre_ref)
            dbim_ref[...] = jnp.zeros_like(dbim_ref)
            dcre_ref[...] = jnp.zeros_like(dcre_ref)
            dcim_ref[...] = jnp.zeros_like(dcim_ref)
            da_ref[...] = jnp.zeros_like(da_ref)

        ub = u_ref[...]
        dy = dys_ref[...]
        ar = a_ref[0, 0:1, :]
        ai = a_ref[0, 1:2, :]
        x0r = xb_ref[0, 0, 0:1, :]
        x0i = xb_ref[0, 0, 1:2, :]
        dcre_ref[0] += _bdot(sr_ref[...], dy, TN)
        dcim_ref[0] -= _bdot(si_ref[...], dy, TN)
        gr_ref[...] = _bdot(dy, cre_ref[0], NT)
        gi_ref[...] = -_bdot(dy, cim_ref[0], NT)

        g0r, g0i = _s5_scan_rows(gr_ref, gi_ref, ar, -ai, gcarry_ref[0:1, :], gcarry_ref[1:2, :], tl, reverse=True)
        gcarry_ref[0:1, :] = g0r
        gcarry_ref[1:2, :] = g0i
        row = lax.broadcasted_iota(jnp.int32, (tl, ns), 0)
        gr = gr_ref[...]
        gi = gi_ref[...]
        xpr = jnp.where(row == 0, x0r, pltpu.roll(sr_ref[...].astype(F32), 1, 0))
        xpi = jnp.where(row == 0, x0i, pltpu.roll(si_ref[...].astype(F32), 1, 0))
        da_ref[0, 0:1, :] += jnp.sum(gr * xpr + gi * xpi, axis=0, keepdims=True)
        da_ref[0, 1:2, :] += jnp.sum(gi * xpr - gr * xpi, axis=0, keepdims=True)
        du_ref[...] = (_bdot(gr, bre_ref[0], NT) + _bdot(gi, bim_ref[0], NT)) + duo_ref[...]
        dbre_ref[0] += _bdot(ub, gr, TN)
        dbim_ref[0] += _bdot(ub, gi, TN)

    rev = lambda c: nc - 1 - c
    return pl.pallas_call(
        body, name="s5_core_bwd", grid=(nb, nc),
        in_specs=[pl.BlockSpec((tl, LANES), lambda j, c: (rev(c), j)), pl.BlockSpec((tl, LANES), lambda j, c: (rev(c), j)),
                  pl.BlockSpec((tl, LANES), lambda j, c: (rev(c), j)),
                  pl.BlockSpec((1, LANES, ns), lambda j, c: (j, 0, 0)), pl.BlockSpec((1, LANES, ns), lambda j, c: (j, 0, 0)),
                  pl.BlockSpec((1, ns, LANES), lambda j, c: (j, 0, 0)), pl.BlockSpec((1, ns, LANES), lambda j, c: (j, 0, 0)),
                  pl.BlockSpec((1, SUBLANES, ns), lambda j, c: (j, 0, 0)),
                  pl.BlockSpec((1, 1, SUBLANES, ns), lambda j, c: (j, rev(c), 0, 0)),
                  pl.BlockSpec((tl, ns), lambda j, c: (rev(c), j)), pl.BlockSpec((tl, ns), lambda j, c: (rev(c), j))],
        out_specs=[pl.BlockSpec((tl, LANES), lambda j, c: (rev(c), j)),
                   pl.BlockSpec((1, LANES, ns), lambda j, c: (j, 0, 0)), pl.BlockSpec((1, LANES, ns), lambda j, c: (j, 0, 0)),
                   pl.BlockSpec((1, ns, LANES), lambda j, c: (j, 0, 0)), pl.BlockSpec((1, ns, LANES), lambda j, c: (j, 0, 0)),
                   pl.BlockSpec((1, SUBLANES, ns), lambda j, c: (j, 0, 0))],
        out_shape=[SDS((L, e), F32), SDS(bre.shape, F32), SDS(bim.shape, F32), SDS(cre.shape, F32), SDS(cim.shape, F32),
                   SDS(a.shape, F32)],
        scratch_shapes=[pltpu.VMEM((tl, ns), F32) for _ in range(2)] + [pltpu.VMEM((SUBLANES, ns), F32)],
        compiler_params=_params(("arbitrary", "arbitrary"), VMEM_MID),
    )(u, dys, du_other, bre, bim, cre, cim, a, xb, sr, si)


def make_s5_core(tl):
    @jax.custom_vjp
    def s5_core(u, bre, bim, cre, cim, a):
        return _s5_fwd_call(u, bre, bim, cre, cim, a, tl)[0], u

    def fwd(u, bre, bim, cre, cim, a):
        ys, xb, sr, si = _s5_fwd_call(u, bre, bim, cre, cim, a, tl)
        return (ys, u), (u, bre, bim, cre, cim, a, xb, sr, si)

    def bwd(res, cts):
        u, bre, bim, cre, cim, a, xb, sr, si = res
        dys, du_other = cts
        return tuple(_s5_bwd_call(u, dys, du_other, bre, bim, cre, cim, a, xb, sr, si, tl))

    s5_core.defvjp(fwd, bwd)
    return s5_core


def _s5_block_params(lam_re, lam_im, log_dt, b_re, b_im, c_re, c_im):
    dt = jnp.exp(log_dt)[:, None]
    mag = jnp.exp(lam_re * dt)
    ab_re = mag * jnp.cos(lam_im * dt)
    ab_im = mag * jnp.sin(lam_im * dt)
    den = lam_re * lam_re + lam_im * lam_im
    nr = ab_re - 1.0
    ni = ab_im
    q_re = (nr * lam_re + ni * lam_im) / den
    q_im = (ni * lam_re - nr * lam_im) / den
    bb_re = q_re[..., None] * b_re - q_im[..., None] * b_im
    bb_im = q_re[..., None] * b_im + q_im[..., None] * b_re
    nb = S5_GROUPS // S5_GB
    eye = jnp.eye(S5_GB, dtype=F32)

    def bdiag_in(bb):
        t = bb.reshape(nb, S5_GB, S5_STATE, S5_GROUP)
        t = jnp.einsum("jgpm,gh->jgmhp", t, eye)
        return t.reshape(nb, S5_GB * S5_GROUP, S5_GB * S5_STATE)

    def bdiag_out(cc):
        t = cc.reshape(nb, S5_GB, S5_GROUP, S5_STATE)
        t = jnp.einsum("jgmp,gh->jgphm", t, eye)
        return t.reshape(nb, S5_GB * S5_STATE, S5_GB * S5_GROUP)

    a = jnp.stack([ab_re.reshape(nb, S5_GB * S5_STATE), ab_im.reshape(nb, S5_GB * S5_STATE)], axis=1)
    a = jnp.concatenate([a, jnp.zeros((nb, SUBLANES - 2, S5_GB * S5_STATE), F32)], axis=1)
    return bdiag_in(bb_re), bdiag_in(bb_im), bdiag_out(c_re), bdiag_out(c_im), a


def _shift_down(x, s, row):
    if s == 0:
        return x
    return jnp.where(row >= s, pltpu.roll(x, s, 0), 0.0)


def _shift_up(x, s, row, n):
    if s == 0:
        return x
    return jnp.where(row < n - s, pltpu.roll(x, n - s, 0), 0.0)


def _causal_conv(xv, w_ref, row):
    acc = jnp.zeros_like(xv)
    for j in range(GDN_CONV):
        acc += w_ref[j:j + 1, :] * _shift_down(xv, GDN_CONV - 1 - j, row)
    return acc


def _conv_fwd_call(x, w, act, name):
    L, ch = x.shape

    def body(x_ref, w_ref, y_ref):
        xv = x_ref[...]
        row = lax.broadcasted_iota(jnp.int32, xv.shape, 0)
        y_ref[...] = act(_causal_conv(xv, w_ref, row))

    return pl.pallas_call(
        body, name=name + "_fwd", grid=(ch // LANES,),
        in_specs=[pl.BlockSpec((L, LANES), lambda j: (0, j)), pl.BlockSpec((SUBLANES, LANES), lambda j: (0, j))],
        out_specs=pl.BlockSpec((L, LANES), lambda j: (0, j)), out_shape=SDS((L, ch), F32),
        compiler_params=_params(("parallel",), VMEM_MID),
    )(x, w)


def _conv_bwd_call(x, w, dy, act, name):
    L, ch = x.shape

    def body(x_ref, w_ref, dy_ref, dx_ref, dw_ref):
        xv = x_ref[...]
        row = lax.broadcasted_iota(jnp.int32, xv.shape, 0)
        _, act_vjp = jax.vjp(act, _causal_conv(xv, w_ref, row))
        (g,) = act_vjp(dy_ref[...])
        acc = jnp.zeros_like(xv)
        dws = []
        for j in range(GDN_CONV):
            s = GDN_CONV - 1 - j
            acc += w_ref[j:j + 1, :] * _shift_up(g, s, row, L)
            dws.append(jnp.sum(g * _shift_down(xv, s, row), axis=0, keepdims=True))
        dx_ref[...] = acc
        dw_ref[...] = jnp.concatenate(dws + [jnp.zeros((SUBLANES - GDN_CONV, LANES), F32)], axis=0)

    return pl.pallas_call(
        body, name=name + "_bwd", grid=(ch // LANES,),
        in_specs=[pl.BlockSpec((L, LANES), lambda j: (0, j)), pl.BlockSpec((SUBLANES, LANES), lambda j: (0, j)),
                  pl.BlockSpec((L, LANES), lambda j: (0, j))],
        out_specs=[pl.BlockSpec((L, LANES), lambda j: (0, j)), pl.BlockSpec((SUBLANES, LANES), lambda j: (0, j))],
        out_shape=[SDS((L, ch), F32), SDS((SUBLANES, ch), F32)],
        compiler_params=_params(("parallel",), VMEM_MID),
    )(x, w, dy)


def make_conv_act(act, name):
    @jax.custom_vjp
    def op(x, w):
        return _conv_fwd_call(x, w, act, name)

    def fwd(x, w):
        return _conv_fwd_call(x, w, act, name), (x, w)

    def bwd(res, dy):
        x, w = res
        return tuple(_conv_bwd_call(x, w, dy, act, name))

    op.defvjp(fwd, bwd)
    return op


gdn_conv = make_conv_act(lambda c: c, "gdn_conv")


BNN = (((2,), (1,)), ((0,), (0,)))
BNT = (((2,), (2,)), ((0,), (0,)))
BTN = (((1,), (1,)), ((0,), (0,)))
GDN_PREP_BATCH = 8


@jax.custom_vjp
def _known_inverse(a, t):
    return t


def _known_inverse_fwd(a, t):
    return t, t


def _known_inverse_bwd(t, g):
    return -_hdot(_hdot(t, g, _BTN), t, _BNT), jnp.zeros_like(t)


_known_inverse.defvjp(_known_inverse_fwd, _known_inverse_bwd)


def _gdn_prep_math(q, k, v, beta, g, t_saved=None):
    B, C = q.shape[0], q.shape[1]
    ri = lax.broadcasted_iota(jnp.int32, (B, C, C), 1)
    ci = lax.broadcasted_iota(jnp.int32, (B, C, C), 2)
    causal = ri >= ci
    strict = ri > ci
    eye = (ri == ci).astype(F32)
    gb = jnp.broadcast_to(g, (B, C, C))
    g_row = jnp.sum(gb * eye, axis=1, keepdims=True)
    gc_col = jnp.sum(jnp.where(causal, jnp.broadcast_to(g_row, (B, C, C)), 0.0), axis=2, keepdims=True)
    gc_row = jnp.sum(jnp.where(ri <= ci, gb, 0.0), axis=1, keepdims=True)
    decay = jnp.exp(jnp.where(causal, gc_col - gc_row, -jnp.inf))
    kk = _bdot(k, k, BNT)
    a_mat = jnp.where(strict, beta * kk * decay, 0.0)
    t = _unit_lower_inverse(a_mat) if t_saved is None else _known_inverse(a_mat, t_saved)
    e_gc = jnp.exp(gc_col)
    w = _hdot(t, beta * e_gc * k, BNN)
    u = _hdot(t, beta * v, BNN)
    qk = _bdot(q, k, BNT) * decay
    q_dec = q * e_gc
    g_last = gc_col[:, C - 1:C, :]
    k_dec = k * jnp.exp(g_last - gc_col)
    return q_dec, w, u, qk, k_dec, gc_col, t


def _gdn_prep_specs(L):
    C = GDN_CHUNK
    nb = min(GDN_PREP_BATCH, L // C)
    R = nb * C
    ins = [pl.BlockSpec((R, GDN_DK), lambda c, h: (c, h)), pl.BlockSpec((R, GDN_DK), lambda c, h: (c, h)),
           pl.BlockSpec((R, GDN_DV), lambda c, h: (c, h)), pl.BlockSpec((R, LANES), lambda c, h: (c, 0))]
    outs = [pl.BlockSpec((1, R, GDN_DK), lambda c, h: (h, c, 0)), pl.BlockSpec((1, R, GDN_DK), lambda c, h: (h, c, 0)),
            pl.BlockSpec((1, R, GDN_DV), lambda c, h: (h, c, 0)), pl.BlockSpec((1, R, C), lambda c, h: (h, c, 0)),
            pl.BlockSpec((1, R, GDN_DK), lambda c, h: (h, c, 0)), pl.BlockSpec((1, R, 1), lambda c, h: (h, c, 0))]
    t_spec = pl.BlockSpec((1, R, C), lambda c, h: (h, c, 0))
    shapes = [SDS((GDN_HEADS, L, GDN_DK), F32), SDS((GDN_HEADS, L, GDN_DK), F32), SDS((GDN_HEADS, L, GDN_DV), F32),
              SDS((GDN_HEADS, L, C), F32), SDS((GDN_HEADS, L, GDN_DK), F32), SDS((GDN_HEADS, L, 1), F32)]
    return ins, outs, t_spec, shapes, nb


def _chunks(x, nb):
    return x.reshape(nb, x.shape[0] // nb, x.shape[1])


def _head_columns(bg, h):
    lane = lax.broadcasted_iota(jnp.int32, bg.shape, 1)
    beta = jnp.sum(jnp.where(lane == h, bg, 0.0), axis=1, keepdims=True)
    g = jnp.sum(jnp.where(lane == h + GDN_HEADS, bg, 0.0), axis=1, keepdims=True)
    return beta, g


def _gdn_prep_fwd_call(q, k, v, bg):
    L = q.shape[0]
    ins, outs, t_spec, shapes, nb = _gdn_prep_specs(L)

    def body(q_ref, k_ref, v_ref, bg_ref, *o_refs):
        beta, g = _head_columns(bg_ref[...], pl.program_id(1))
        res = _gdn_prep_math(_chunks(q_ref[...], nb), _chunks(k_ref[...], nb), _chunks(v_ref[...], nb),
                             _chunks(beta, nb), _chunks(g, nb))
        for o_ref, val in zip(o_refs, res):
            o_ref[0] = val.reshape(val.shape[0] * val.shape[1], val.shape[2])

    return pl.pallas_call(
        body, name="gdn_prep_fwd", grid=(L // (nb * GDN_CHUNK), GDN_HEADS), in_specs=ins, out_specs=outs + [t_spec],
        out_shape=shapes + [SDS((GDN_HEADS, L, GDN_CHUNK), F32)],
        compiler_params=_params(("parallel", "parallel"), VMEM_MID),
    )(q, k, v, bg)


def _gdn_prep_bwd_call(q, k, v, bg, t, cts):
    L = q.shape[0]
    ins, outs, t_spec, _, nb = _gdn_prep_specs(L)

    def body(q_ref, k_ref, v_ref, bg_ref, t_ref, c0, c1, c2, c3, c4, c5, dq_ref, dk_ref, dv_ref, dbg_ref):
        h = pl.program_id(1)
        beta, g = _head_columns(bg_ref[...], h)
        t_saved = _chunks(t_ref[0], nb)
        _, vjp = jax.vjp(lambda *a: _gdn_prep_math(*a, t_saved=t_saved)[:6], _chunks(q_ref[...], nb), _chunks(k_ref[...], nb),
                         _chunks(v_ref[...], nb), _chunks(beta, nb), _chunks(g, nb))
        dq, dk, dv, db, dg = vjp(tuple(_chunks(c[0], nb) for c in (c0, c1, c2, c3, c4, c5)))
        flat = lambda a: a.reshape(a.shape[0] * a.shape[1], a.shape[2])
        dq_ref[...] = flat(dq)
        dk_ref[...] = flat(dk)
        dv_ref[...] = flat(dv)

        @pl.when(h == 0)
        def _():
            dbg_ref[...] = jnp.zeros_like(dbg_ref)
        lane = lax.broadcasted_iota(jnp.int32, dbg_ref.shape, 1)
        dbg_ref[...] += jnp.where(lane == h, flat(db), 0.0) + jnp.where(lane == h + GDN_HEADS, flat(dg), 0.0)

    return pl.pallas_call(
        body, name="gdn_prep_bwd", grid=(L // (nb * GDN_CHUNK), GDN_HEADS), in_specs=ins + [t_spec] + outs, out_specs=ins,
        out_shape=[SDS(q.shape, F32), SDS(k.shape, F32), SDS(v.shape, F32), SDS(bg.shape, F32)],
        compiler_params=_params(("parallel", "arbitrary"), VMEM_MID),
    )(q, k, v, bg, t, *cts)


@jax.custom_vjp
def gdn_prep(q, k, v, bg):
    return tuple(_gdn_prep_fwd_call(q, k, v, bg)[:6])


def _gdn_prep_f(q, k, v, bg):
    res = _gdn_prep_fwd_call(q, k, v, bg)
    return tuple(res[:6]), (q, k, v, bg, res[6])


def _gdn_prep_b(res, cts):
    return tuple(_gdn_prep_bwd_call(*res, tuple(cts)))


gdn_prep.defvjp(_gdn_prep_f, _gdn_prep_b)


def _gdn_step_math(q_dec, w, u, qk, k_dec, gc, z, nw, state):
    H, C = q_dec.shape[0], q_dec.shape[1]
    v_new = u - _bdot(w, state, BNN)
    o = _bdot(q_dec, state, BNN) + _bdot(qk, v_new, BNN)
    gl = gc[:, C - 1:C, :]
    new_state = jnp.exp(gl) * state + _bdot(k_dec, v_new, BTN)
    return _f_gdn_post(jnp.concatenate([o[h] for h in range(H)], axis=1), z, nw)[0], new_state


def _gdn_scan_specs(L, rev):
    C, H = GDN_CHUNK, GDN_HEADS
    nc = L // C
    cc = (lambda c: nc - 1 - c) if rev else (lambda c: c)
    ins = [pl.BlockSpec((H, C, GDN_DK), lambda c: (0, cc(c), 0)), pl.BlockSpec((H, C, GDN_DK), lambda c: (0, cc(c), 0)),
           pl.BlockSpec((H, C, GDN_DV), lambda c: (0, cc(c), 0)), pl.BlockSpec((H, C, C), lambda c: (0, cc(c), 0)),
           pl.BlockSpec((H, C, GDN_DK), lambda c: (0, cc(c), 0)), pl.BlockSpec((H, C, 1), lambda c: (0, cc(c), 0))]
    o_spec = pl.BlockSpec((C, H * GDN_DV), lambda c: (cc(c), 0))
    nw_spec = pl.BlockSpec((1, H * GDN_DV), lambda c: (0, 0))
    s_spec = pl.BlockSpec((1, H, GDN_DK, GDN_DV), lambda c: (cc(c), 0, 0, 0))
    return ins + [o_spec, nw_spec], o_spec, s_spec, nc


def _gdn_scan_fwd_call(q_dec, w, u, qk, k_dec, gc, z, nw):
    L = q_dec.shape[1]
    ins, o_spec, s_spec, nc = _gdn_scan_specs(L, False)

    def body(qd_ref, w_ref, u_ref, qk_ref, kd_ref, gc_ref, z_ref, nw_ref, o_ref, sin_ref, s_ref):
        c = pl.program_id(0)

        @pl.when(c == 0)
        def _():
            s_ref[...] = jnp.zeros_like(s_ref)
        st = s_ref[...]
        sin_ref[0] = st
        o, ns = _gdn_step_math(qd_ref[...], w_ref[...], u_ref[...], qk_ref[...], kd_ref[...], gc_ref[...], z_ref[...], nw_ref[...], st)
        o_ref[...] = o
        s_ref[...] = ns

    return pl.pallas_call(
        body, name="gdn_scan_fwd", grid=(nc,), in_specs=ins, out_specs=[o_spec, s_spec],
        out_shape=[SDS((L, GDN_HEADS * GDN_DV), F32), SDS((nc, GDN_HEADS, GDN_DK, GDN_DV), F32)],
        scratch_shapes=[pltpu.VMEM((GDN_HEADS, GDN_DK, GDN_DV), F32)],
        compiler_params=_params(("arbitrary",), VMEM_MID),
    )(q_dec, w, u, qk, k_dec, gc, z, nw)


def _gdn_scan_bwd_call(q_dec, w, u, qk, k_dec, gc, z, nw, s_in, do):
    L = q_dec.shape[1]
    ins, o_spec, s_spec, nc = _gdn_scan_specs(L, True)

    def body(qd_ref, w_ref, u_ref, qk_ref, kd_ref, gc_ref, z_ref, nw_ref, sin_ref, do_ref,
             dqd_ref, dw_ref, du_ref, dqk_ref, dkd_ref, dgc_ref, dz_ref, dnw_ref, ds_ref):
        c = pl.program_id(0)

        @pl.when(c == 0)
        def _():
            ds_ref[...] = jnp.zeros_like(ds_ref)
            dnw_ref[...] = jnp.zeros_like(dnw_ref)
        _, vjp = jax.vjp(_gdn_step_math, qd_ref[...], w_ref[...], u_ref[...], qk_ref[...], kd_ref[...], gc_ref[...],
                         z_ref[...], nw_ref[...], sin_ref[0])
        dqd, dw, du, dqk, dkd, dgc, dz, dnw, dst = vjp((do_ref[...], ds_ref[...]))
        dqd_ref[...] = dqd
        dw_ref[...] = dw
        du_ref[...] = du
        dqk_ref[...] = dqk
        dkd_ref[...] = dkd
        dgc_ref[...] = dgc
        dz_ref[...] = dz
        dnw_ref[...] += dnw
        ds_ref[...] = dst

    return pl.pallas_call(
        body, name="gdn_scan_bwd", grid=(nc,), in_specs=ins + [s_spec, o_spec], out_specs=ins,
        out_shape=[SDS(t.shape, F32) for t in (q_dec, w, u, qk, k_dec, gc, z, nw)],
        scratch_shapes=[pltpu.VMEM((GDN_HEADS, GDN_DK, GDN_DV), F32)],
        compiler_params=_params(("arbitrary",), VMEM_MID),
    )(q_dec, w, u, qk, k_dec, gc, z, nw, s_in, do)


@jax.custom_vjp
def gdn_scan(q_dec, w, u, qk, k_dec, gc, z, nw):
    return _gdn_scan_fwd_call(q_dec, w, u, qk, k_dec, gc, z, nw)[0]


def _gdn_scan_f(*args):
    o, s_in = _gdn_scan_fwd_call(*args)
    return o, (*args, s_in)


def _gdn_scan_b(res, do):
    return tuple(_gdn_scan_bwd_call(*res, do))


gdn_scan.defvjp(_gdn_scan_f, _gdn_scan_b)


def _silu(x):
    return x * jax.nn.sigmoid(x)


def _gelu_tanh(x):
    return 0.5 * x * (1.0 + jnp.tanh(math.sqrt(2.0 / math.pi) * (x + 0.044715 * (x * x * x))))


def _f_lnmod(x, nw, sc, sh, bsc, bsh):
    xn = x * lax.rsqrt(jnp.mean(x * x, axis=-1, keepdims=True) + NORM_EPS) * nw
    return (xn * (1.0 + (sc + bsc)) + (sh + bsh),)


def _f_s5_act(ys, u, d):
    return (_gelu_tanh(ys + d * u),)


def _f_s5_gate(y2, t, z):
    return (y2 * jax.nn.sigmoid(t) * _silu(z),)


def _f_res(x, y, gate, bgate):
    return (x + (gate + bgate) * y,)


def _heads(x, width, fn):
    return jnp.concatenate([fn(x[:, i * width:(i + 1) * width]) for i in range(x.shape[1] // width)], axis=1)


def _l2n(x):
    return x * lax.rsqrt(jnp.sum(x * x, axis=-1, keepdims=True) + NORM_EPS)


def _f_qnorm(x):
    return (_heads(_silu(x), GDN_DK, _l2n) * (GDN_DK ** -0.5),)


def _f_knorm(x):
    return (_heads(_silu(x), GDN_DK, _l2n),)


def _f_vact(x):
    return (_silu(x),)


def _f_betag(ba, alog, dtb):
    col = lax.broadcasted_iota(jnp.int32, ba.shape, 1)
    t = ba + dtb
    softplus = jnp.maximum(t, 0.0) + jnp.log1p(jnp.exp(-jnp.abs(t)))
    g = -jnp.exp(alog) * softplus
    return (jnp.where(col < GDN_HEADS, jax.nn.sigmoid(ba), jnp.where(col < 2 * GDN_HEADS, g, 0.0)),)


def _f_gdn_post(o, z, nw):
    on = _heads(o, GDN_DV, lambda t: t * lax.rsqrt(jnp.mean(t * t, axis=-1, keepdims=True) + NORM_EPS))
    return (on * nw * _silu(z),)


def _f_loss(x, tgt, fw):
    y = x * lax.rsqrt(jnp.mean(x * x, axis=-1, keepdims=True) + NORM_EPS) * fw
    err = y - tgt
    return (0.5 * jnp.mean(err * err, axis=-1, keepdims=True),)


def _ada_mod_call(c_all, ada_w):
    n = ada_w.shape[2]

    def body(c_ref, w_ref, o_ref):
        ca = _silu(c_ref[...])
        for l in range(ada_w.shape[0]):
            o_ref[l] = _bdot(ca, w_ref[l])

    return pl.pallas_call(body, name="ada_mod", out_shape=SDS((ada_w.shape[0], N_DEV, n), F32),
                          compiler_params=_params(None, VMEM_MID))(c_all, ada_w)


def _ada_grad_call(c_all, dmod):
    nl, _, n = dmod.shape

    def body(c_ref, d_ref, o_ref):
        ca = _silu(c_ref[...])
        for l in range(nl):
            o_ref[l] = _hdot(ca, d_ref[l], TN)

    return pl.pallas_call(body, name="ada_grad", out_shape=SDS((nl, c_all.shape[1], n), F32),
                          compiler_params=_params(None, VMEM_MID))(c_all, dmod)


ADAM_ROWS = 512


def _adamw(g, w, m, v):
    m2 = ADAM_B1 * m + (1.0 - ADAM_B1) * g
    v2 = ADAM_B2 * v + (1.0 - ADAM_B2) * (g * g)
    m_hat = m2 / (1.0 - ADAM_B1 ** ADAM_STEP)
    v_hat = v2 / (1.0 - ADAM_B2 ** ADAM_STEP)
    return g, -ADAM_LR * (m_hat / (jnp.sqrt(v_hat) + ADAM_EPS) + ADAM_WD * w), m2, v2


def _adam_call(gs, w, m, v, name, rows=None):
    n, r, cols = gs.shape
    rows = rows or ADAM_ROWS

    def body(g_ref, w_ref, m_ref, v_ref, go_ref, d_ref, mo_ref, vo_ref):
        g = g_ref[0].astype(F32)
        for s in range(1, n):
            g = g + g_ref[s].astype(F32)
        for o_ref, val in zip((go_ref, d_ref, mo_ref, vo_ref), _adamw(g, w_ref[...], m_ref[...], v_ref[...])):
            o_ref[...] = val

    blk = pl.BlockSpec((rows, cols), lambda i: (i, 0))
    return pl.pallas_call(
        body, name=name, grid=(r // rows,),
        in_specs=[pl.BlockSpec((n, rows, cols), lambda i: (0, i, 0)), blk, blk, blk],
        out_specs=[blk, blk, blk, blk], out_shape=[SDS((r, cols), F32)] * 4,
        compiler_params=_params(("parallel",), VMEM_MID),
    )(gs, w, m, v)


def _sum_call(gs, name, rows):
    n, r, _ = gs.shape

    def body(g_ref, o_ref):
        g = g_ref[0].astype(F32)
        for s in range(1, n):
            g = g + g_ref[s].astype(F32)
        o_ref[...] = g

    return pl.pallas_call(
        body, name=name, grid=(r // rows,),
        in_specs=[pl.BlockSpec((n, rows, LANES), lambda i: (0, i, 0))],
        out_specs=pl.BlockSpec((rows, LANES), lambda i: (i, 0)), out_shape=SDS((r, LANES), F32),
        compiler_params=_params(("parallel",), VMEM_MID),
    )(gs)


def _allgather_call(x_shard, name, in_hbm):
    m_per, n = x_shard.shape

    def body(x_ref, out_ref, send_sems, recv_sems, local_sem):
        x, y, c = lax.axis_index("x"), lax.axis_index("y"), lax.axis_index("c")
        me, sibling = (x, y, c), (x, y, 1 - c)
        chips = [(1 - x, y), (x, 1 - y), (1 - x, 1 - y)]

        def rows(px, py, pc):
            return out_ref.at[pl.ds((4 * px + 2 * py + pc) * m_per, m_per), :]

        def copy(k, block, to, src=None):
            return pltpu.make_async_remote_copy(
                src_ref=rows(*block) if src is None else src, dst_ref=rows(*block),
                send_sem=send_sems.at[k], recv_sem=recv_sems.at[k], device_id=to, device_id_type=pl.DeviceIdType.MESH)

        mine = pltpu.make_async_copy(x_ref, rows(*me), local_sem)
        mine.start()
        first = [copy(0, me, sibling, src=x_ref)]
        first += [copy(1 + j, me, (*chip, c), src=x_ref) for j, chip in enumerate(chips)]
        for cp in first:
            cp.start()
        passed = [copy(4 + j, (*chip, c), sibling) for j, chip in enumerate(chips)]
        for j, chip in enumerate(chips):
            copy(1 + j, (*chip, c), me).wait_recv()
            passed[j].start()
        copy(0, sibling, me).wait_recv()
        for j, chip in enumerate(chips):
            copy(4 + j, (*chip, 1 - c), me).wait_recv()
        for cp in first + passed:
            cp.wait_send()
        mine.wait()

    space = pl.ANY if in_hbm else pltpu.VMEM
    return pl.pallas_call(
        body, name=name, out_shape=SDS((N_DEV * m_per, n), x_shard.dtype),
        in_specs=[pl.BlockSpec(memory_space=space)], out_specs=pl.BlockSpec(memory_space=space),
        scratch_shapes=[pltpu.SemaphoreType.DMA((7,)), pltpu.SemaphoreType.DMA((7,)), pltpu.SemaphoreType.DMA],
        compiler_params=_params(None, None if in_hbm else VMEM_BIG),
    )(x_shard)


def _gather_weights_call(shards, name):
    nw = len(shards)

    def body(*refs):
        x_refs, out_refs = refs[:nw], refs[nw:2 * nw]
        send_sems, recv_sems, local_sems = refs[2 * nw:]
        x, y, c = lax.axis_index("x"), lax.axis_index("y"), lax.axis_index("c")
        me, sibling = (x, y, c), (x, y, 1 - c)
        chips = [(1 - x, y), (x, 1 - y), (1 - x, 1 - y)]

        def slot(w, px, py, pc):
            return out_refs[w].at[4 * px + 2 * py + pc]

        def copy(w, k, block, to, src=None):
            dst = slot(w, *block)
            return pltpu.make_async_remote_copy(
                src_ref=dst if src is None else src, dst_ref=dst, send_sem=send_sems.at[7 * w + k],
                recv_sem=recv_sems.at[7 * w + k], device_id=to, device_id_type=pl.DeviceIdType.MESH)

        mines = [pltpu.make_async_copy(x_refs[w], slot(w, *me), local_sems.at[w]) for w in range(nw)]
        for cp in mines:
            cp.start()
        first = [copy(w, 0, me, sibling, src=x_refs[w]) for w in range(nw)]
        first += [copy(w, 1 + j, me, (*chip, c), src=x_refs[w]) for w in range(nw) for j, chip in enumerate(chips)]
        for cp in first:
            cp.start()
        passed = []
        for w in range(nw):
            for j, chip in enumerate(chips):
                copy(w, 1 + j, (*chip, c), me).wait_recv()
                fwd = copy(w, 4 + j, (*chip, c), sibling)
                fwd.start()
                passed.append(fwd)
        for w in range(nw):
            copy(w, 0, sibling, me).wait_recv()
            for j, chip in enumerate(chips):
                copy(w, 4 + j, (*chip, 1 - c), me).wait_recv()
        for cp in first + passed:
            cp.wait_send()
        for cp in mines:
            cp.wait()

    hbm = pl.BlockSpec(memory_space=pl.ANY)
    return pl.pallas_call(
        body, name=name, out_shape=[SDS((N_DEV,) + s.shape, s.dtype) for s in shards],
        in_specs=[hbm] * nw, out_specs=[hbm] * nw,
        scratch_shapes=[pltpu.SemaphoreType.DMA((7 * nw,)), pltpu.SemaphoreType.DMA((7 * nw,)), pltpu.SemaphoreType.DMA((nw,))],
    )(*shards)


def _pair_exchange_call(grads, name):
    nw = len(grads)

    def body(*refs):
        g_refs, got_refs = refs[:nw], refs[nw:2 * nw]
        send_sems, recv_sems = refs[2 * nw:]
        x, y, c = lax.axis_index("x"), lax.axis_index("y"), lax.axis_index("c")
        copies = []
        for w in range(nw):
            for j in range(4):
                give = pltpu.make_async_remote_copy(
                    src_ref=g_refs[w].at[2 * j + 1 - c], dst_ref=got_refs[w].at[j], send_sem=send_sems.at[4 * w + j],
                    recv_sem=recv_sems.at[4 * w + j], device_id=(x, y, 1 - c), device_id_type=pl.DeviceIdType.MESH)
                give.start()
                copies.append(give)
        for cp in copies:
            cp.wait()

    hbm = pl.BlockSpec(memory_space=pl.ANY)
    return pl.pallas_call(
        body, name=name, out_shape=[SDS((4,) + g.shape[1:], g.dtype) for g in grads], in_specs=[hbm] * nw, out_specs=[hbm] * nw,
        scratch_shapes=[pltpu.SemaphoreType.DMA((4 * nw,)), pltpu.SemaphoreType.DMA((4 * nw,))],
    )(*grads)


def _chip_exchange_call(parts, name):
    nw = len(parts)

    def body(*refs):
        p_refs, out_refs = refs[:nw], refs[nw:2 * nw]
        send_sems, recv_sems = refs[2 * nw:]
        x, y, c = lax.axis_index("x"), lax.axis_index("y"), lax.axis_index("c")
        chips = [(1 - x, y), (x, 1 - y), (1 - x, 1 - y)]
        copies = []
        for w in range(nw):
            for j, (px, py) in enumerate(chips):
                give = pltpu.make_async_remote_copy(
                    src_ref=p_refs[w].at[2 * px + py], dst_ref=out_refs[w].at[j], send_sem=send_sems.at[3 * w + j],
                    recv_sem=recv_sems.at[3 * w + j], device_id=(px, py, c), device_id_type=pl.DeviceIdType.MESH)
                give.start()
                copies.append(give)
        for cp in copies:
            cp.wait()

    hbm = pl.BlockSpec(memory_space=pl.ANY)
    return pl.pallas_call(
        body, name=name, out_shape=[SDS((3,) + p.shape[1:], p.dtype) for p in parts], in_specs=[hbm] * nw, out_specs=[hbm] * nw,
        scratch_shapes=[pltpu.SemaphoreType.DMA((3 * nw,)), pltpu.SemaphoreType.DMA((3 * nw,))],
    )(*parts)


_HBM = pl.BlockSpec(memory_space=pltpu.HBM)
_SEM = pl.BlockSpec(memory_space=pltpu.SEMAPHORE)
_DATAFLOW = pltpu.SideEffectType.DATAFLOW_SIDE_EFFECTING


def _spread_start_call(srcs, per_peer, name, after):
    nw = len(srcs)
    lands = [lax.empty((N_DEV,) + (s.shape[1:] if per_peer else s.shape), s.dtype) for s in srcs]

    def body(*refs):
        src_refs, land_refs = refs[:nw], refs[nw:2 * nw]
        send_sems, recv_sems, token = refs[2 * nw + 1], refs[2 * nw + 2], refs[-1]
        x, y, c = lax.axis_index("x"), lax.axis_index("y"), lax.axis_index("c")
        me = 4 * x + 2 * y + c
        for w in range(nw):
            for k in range(1, N_DEV):
                px = 1 - x if k & 4 else x
                py = 1 - y if k & 2 else y
                pc = 1 - c if k & 1 else c
                src = src_refs[w].at[4 * px + 2 * py + pc] if per_peer else src_refs[w]
                pltpu.make_async_remote_copy(
                    src_ref=src, dst_ref=land_refs[w].at[me], send_sem=send_sems.at[w], recv_sem=recv_sems.at[w],
                    device_id=(px, py, pc), device_id_type=pl.DeviceIdType.MESH).start()
        token[...] = jnp.zeros_like(token)

    hbm = lambda a: pltpu.with_memory_space_constraint(a, pltpu.HBM)
    res = pl.pallas_call(
        body, name=name,
        out_shape=(pltpu.SemaphoreType.DMA((nw,)), pltpu.SemaphoreType.DMA((nw,)))
        + tuple(pltpu.HBM(s.shape, s.dtype) for s in srcs) + tuple(pltpu.HBM(l.shape, l.dtype) for l in lands)
        + (SDS((SUBLANES, LANES), F32),),
        in_specs=[_HBM] * (2 * nw) + [pl.BlockSpec(memory_space=pl.ANY)],
        out_specs=(_SEM, _SEM) + (_HBM,) * (2 * nw) + (pl.BlockSpec(memory_space=pltpu.VMEM),),
        input_output_aliases={i: i + 2 for i in range(2 * nw)},
        compiler_params=pltpu.CompilerParams(has_side_effects=_DATAFLOW),
    )(*[hbm(s) for s in srcs], *[hbm(l) for l in lands], after)
    return res[0], res[1], res[2:2 + nw], res[2 + nw:2 + 2 * nw], res[-1]


def _spread_wait_call(send_sems, recv_sems, srcs, lands, after, name):
    nw = len(lands)

    def body(*refs):
        land_refs = refs[nw:2 * nw]
        s_sems, r_sems = refs[2 * nw], refs[2 * nw + 1]
        x, y, c = lax.axis_index("x"), lax.axis_index("y"), lax.axis_index("c")
        for w in range(nw):
            seven = land_refs[w].at[pl.ds(0, N_DEV - 1)]
            all_seven = pltpu.make_async_remote_copy(
                src_ref=seven, dst_ref=seven, send_sem=s_sems.at[w], recv_sem=r_sems.at[w],
                device_id=(x, y, c), device_id_type=pl.DeviceIdType.MESH)
            all_seven.wait_send()
            all_seven.wait_recv()

    res = pl.pallas_call(
        body, name=name,
        out_shape=tuple(pltpu.HBM(s.shape, s.dtype) for s in srcs) + tuple(pltpu.HBM(l.shape, l.dtype) for l in lands),
        in_specs=[_HBM] * (2 * nw) + [_SEM, _SEM, pl.BlockSpec(memory_space=pl.ANY)], out_specs=(_HBM,) * (2 * nw),
        input_output_aliases={i: i for i in range(2 * nw)},
        compiler_params=pltpu.CompilerParams(has_side_effects=_DATAFLOW),
    )(*srcs, *lands, send_sems, recv_sems, after)
    return res[:nw], res[nw:]


def _pair_sum_call(g, got, core, name):
    _, k, n = got.shape
    tr = _tile(k, 256)

    def body(c_ref, g_ref, got_ref, o_ref):
        o_ref[...] = (g_ref[...] + got_ref[...]).astype(o_ref.dtype)

    spec = pltpu.PrefetchScalarGridSpec(
        num_scalar_prefetch=1, grid=(4, k // tr),
        in_specs=[pl.BlockSpec((1, tr, n), lambda j, i, c: (2 * j + c[0], i, 0)), pl.BlockSpec((1, tr, n), lambda j, i, c: (j, i, 0))],
        out_specs=pl.BlockSpec((1, tr, n), lambda j, i, c: (j, i, 0)))
    return pl.pallas_call(body, name=name, grid_spec=spec, out_shape=SDS(got.shape, BF16),
                          compiler_params=_params(("parallel", "parallel"), VMEM_MID))(core, g, got)


def _adam_own_call(pair, chip, recv, w, m, v, name, rows):
    _, r, cols = recv.shape

    def body(chip_ref, p_ref, g_ref, w_ref, m_ref, v_ref, go_ref, d_ref, mo_ref, vo_ref):
        g = ((p_ref[0].astype(F32) + g_ref[0].astype(F32)) + g_ref[1].astype(F32)) + g_ref[2].astype(F32)
        for o_ref, val in zip((go_ref, d_ref, mo_ref, vo_ref), _adamw(g, w_ref[...], m_ref[...], v_ref[...])):
            o_ref[...] = val

    blk = pl.BlockSpec((rows, cols), lambda i, s: (i, 0))
    spec = pltpu.PrefetchScalarGridSpec(
        num_scalar_prefetch=1, grid=(r // rows,),
        in_specs=[pl.BlockSpec((1, rows, cols), lambda i, s: (s[0], i, 0)), pl.BlockSpec((3, rows, cols), lambda i, s: (0, i, 0)),
                  blk, blk, blk],
        out_specs=[blk, blk, blk, blk])
    return pl.pallas_call(body, name=name, grid_spec=spec, out_shape=[SDS((r, cols), F32)] * 4,
                          compiler_params=_params(("parallel",), VMEM_MID))(chip, pair, recv, w, m, v)


def _join_cols_call(w8, name):
    _, k, n = w8.shape
    tk = _tile(k, 256)

    def body(w_ref, o_ref):
        for s in range(N_DEV):
            o_ref[:, n * s:n * (s + 1)] = w_ref[s]

    return pl.pallas_call(body, name=name, grid=(k // tk,), in_specs=[pl.BlockSpec((N_DEV, tk, n), lambda i: (0, i, 0))],
                          out_specs=pl.BlockSpec((tk, N_DEV * n), lambda i: (i, 0)), out_shape=SDS((k, N_DEV * n), w8.dtype),
                          compiler_params=_params(("parallel",), VMEM_MID))(w8)


def _split_cols_call(g, name, dtype):
    k, n8 = g.shape
    n = n8 // N_DEV
    tk = _tile(k, 256)

    def body(g_ref, o_ref):
        for s in range(N_DEV):
            o_ref[s] = g_ref[:, n * s:n * (s + 1)].astype(dtype)

    return pl.pallas_call(body, name=name, grid=(k // tk,), in_specs=[pl.BlockSpec((tk, n8), lambda i: (i, 0))],
                          out_specs=pl.BlockSpec((N_DEV, tk, n), lambda i: (0, i, 0)), out_shape=SDS((N_DEV, k, n), dtype),
                          compiler_params=_params(("parallel",), VMEM_MID))(g)


def _pack(parts, rows_multiple):
    flat = jnp.concatenate([p.reshape(-1) for p in parts])
    unit = rows_multiple * LANES
    padded = -(-flat.shape[0] // unit) * unit
    flat = jnp.concatenate([flat, jnp.zeros((padded - flat.shape[0],), F32)])
    return flat.reshape(-1, LANES)


def _unpack(buf, shapes):
    flat = buf.reshape(-1)
    out, off = [], 0
    for s in shapes:
        n = math.prod(s)
        out.append(flat[off:off + n].reshape(s))
        off += n
    return out


def _row_tile(L):
    return 256 if L % 256 == 0 else L


def _layer0_mix(diff, const):
    x, mod, norm_w, lam_re, lam_im, log_dt, b_re, b_im, c_re, c_im, s5_d, *slots = diff
    ada_b, weights = const
    L = x.shape[0]
    tm = _row_tile(L)
    mods = mod.reshape(2, 1, D_MODEL)
    biases = ada_b.reshape(2, 1, D_MODEL)
    op_ln0 = make_rowwise(_f_lnmod, "ln0", tm, 1, 5, pass_first=True)
    h, x = op_ln0((x,), (norm_w.reshape(1, D_MODEL), mods[1], mods[0], biases[1], biases[0]))
    u, z = make_proj("s5_in")(h, tuple(weights), tuple(slots))
    blocks = _s5_block_params(lam_re, lam_im, log_dt, b_re, b_im, c_re, c_im)
    ys, u = make_s5_core(min(S5_TL, L))(u, *blocks)
    (y2,) = make_rowwise(_f_s5_act, "s5_act", tm, 2, 1)((ys, u), (s5_d.reshape(1, D_INNER),))
    return x, y2, z


def _layer0_out(diff, weights):
    y2, z, *slots = diff
    tm = _row_tile(y2.shape[0])
    t, y2 = make_mm("s5_glu", pass_input=True)(y2, weights[0], slots[0])
    (y4,) = make_rowwise(_f_s5_gate, "s5_gate", tm, 3, 0)((y2, t, z), ())
    return make_mm("s5_out")(y4, weights[1], slots[1])


def _f_res_lnmod(x, o, gate, bgate, nw, sc, sh, bsc, bsh):
    (x1,) = _f_res(x, o, gate, bgate)
    return _f_lnmod(x1, nw, sc, sh, bsc, bsh) + (x1,)


def _f_res_loss(x, y, tgt, gate, bgate, fw):
    return _f_loss(_f_res(x, y, gate, bgate)[0], tgt, fw)


def _layer1_loss(diff, const):
    x, o, gate0, mod, norm_w, conv_w, a_log, dt_bias, gdn_nw, final_nw, *slots = diff
    tgt, bgate0, ada_b, weights = const
    L = x.shape[0]
    tm = _row_tile(L)
    mods = mod.reshape(3, 1, D_MODEL)
    biases = ada_b.reshape(3, 1, D_MODEL)
    h, x1 = make_rowwise(_f_res_lnmod, "res0_ln1", tm, 2, 7)(
        (x, o), (gate0.reshape(1, D_MODEL), bgate0.reshape(1, D_MODEL), norm_w.reshape(1, D_MODEL), mods[1], mods[0], biases[1], biases[0]))
    q0, k0, v0, gz, ba = make_proj("gdn_in")(h, tuple(weights[0:5]), tuple(slots[0:5]))
    cw = jnp.concatenate([conv_w, jnp.zeros((SUBLANES - GDN_CONV, GDN_CONV_CH), F32)], axis=0)
    q = make_conv_act(lambda t: _l2n(_silu(t)) * (GDN_DK ** -0.5), "gdn_conv_q")(q0, cw[:, :GDN_QK])
    k = make_conv_act(lambda t: _l2n(_silu(t)), "gdn_conv_k")(k0, cw[:, GDN_QK:2 * GDN_QK])
    v = make_conv_act(_silu, "gdn_conv_v")(v0, cw[:, 2 * GDN_QK:])
    pad = jnp.zeros((LANES - 2 * GDN_HEADS,), F32)
    alog_row = jnp.concatenate([jnp.zeros((GDN_HEADS,), F32), a_log, pad]).reshape(1, LANES)
    dtb_row = jnp.concatenate([jnp.zeros((GDN_HEADS,), F32), dt_bias, pad]).reshape(1, LANES)
    (bg,) = make_rowwise(_f_betag, "gdn_bg", tm, 1, 2)((ba,), (alog_row, dtb_row))
    nw_row = jnp.tile(gdn_nw, GDN_HEADS).reshape(1, D_INNER)
    on = gdn_scan(*gdn_prep(q, k, v, bg), gz, nw_row)
    y = make_mm("gdn_out")(on, weights[5], slots[5])
    (lt,) = make_rowwise(_f_res_loss, "res1_loss", tm, 3, 3)((x1, y, tgt), (mods[2], biases[2], final_nw.reshape(1, D_MODEL)))
    return jnp.sum(lt)


VEC_NAMES = ("ada_b", "norm_w", "s5_lambda_re", "s5_lambda_im", "s5_log_dt", "s5_d", "gdn_a_log", "gdn_dt_bias", "final_norm_w")
MAT_NAMES = ("s5_b_re", "s5_b_im", "s5_c_re", "s5_c_im")
S5_BIG = ("s5_w_in", "s5_w_glu", "s5_w_out")
GDN_BIG = ("gdn_w_in", "gdn_w_out")
BIG_NAMES = S5_BIG + GDN_BIG
WEIGHT_ORDER = ("ada_w", "ada_b", "norm_w", "s5_w_in", "s5_lambda_re", "s5_lambda_im", "s5_log_dt", "s5_b_re", "s5_b_im",
                "s5_c_re", "s5_c_im", "s5_d", "s5_w_glu", "s5_w_out", "gdn_w_in", "gdn_conv_w", "gdn_a_log", "gdn_dt_bias",
                "gdn_norm_w", "gdn_w_out", "final_norm_w")


def _step(x, c, W, M, V, tgt):
    L = x.shape[1]
    ix, iy, ic = lax.axis_index("x"), lax.axis_index("y"), lax.axis_index("c")
    me = 4 * ix + 2 * iy + ic
    n_ada = W["ada_w"].shape[2]
    n_conv = W["gdn_conv_w"].shape[2]
    n_gnw = W["gdn_norm_w"].shape[1]

    g1 = _allgather_call(_pack([c, W["gdn_conv_w"], W["gdn_norm_w"]], SUBLANES), "gather_small_in", False)
    g1 = g1.reshape(N_DEV, -1)
    c_all = g1[:, :D_MODEL]
    conv_w = g1[:, D_MODEL:D_MODEL + GDN_CONV * n_conv].reshape(N_DEV, GDN_CONV, n_conv).transpose(1, 0, 2).reshape(GDN_CONV, -1)
    gdn_nw = g1[:, D_MODEL + GDN_CONV * n_conv:D_MODEL + GDN_CONV * n_conv + n_gnw].reshape(-1)
    mod_part = _ada_mod_call(c_all, W["ada_w"])
    g2 = _allgather_call(_pack([mod_part], SUBLANES), "gather_mod", False).reshape(N_DEV, -1)
    mod_all = g2[:, :2 * N_DEV * n_ada].reshape(N_DEV, 2, N_DEV, n_ada)
    mod_raw = lax.dynamic_index_in_dim(mod_all, me, axis=2, keepdims=False)
    mod_raw = mod_raw.transpose(1, 0, 2).reshape(2, 3 * D_MODEL)

    shard = lambda n: W[n][0].astype(BF16)
    (w_in5_parts,) = _gather_weights_call([shard("s5_w_in")], "gather_s5_w_in")
    late = _spread_start_call([shard("s5_w_glu"), shard("s5_w_out")], False, "gather_s5_late_start", w_in5_parts)
    g_send, g_recv, g_srcs, g_lands, g_token = _spread_start_call([shard(n) for n in GDN_BIG], False, "gather_gdn_start", late[4])
    w_in5 = _join_cols_call(w_in5_parts, "join_s5_w_in")
    slot = lambda *s: jnp.zeros(s, F32)
    two = 2 * D_MODEL
    diff_mix = (x[0], mod_raw[0, :two] + g_token[0, 0], W["norm_w"][0], W["s5_lambda_re"][0], W["s5_lambda_im"][0], W["s5_log_dt"][0],
                W["s5_b_re"][0], W["s5_b_im"][0], W["s5_c_re"][0], W["s5_c_im"][0], W["s5_d"][0],
                slot(D_MODEL, D_INNER), slot(D_MODEL, D_INNER))

    (xp, y2, z5), vjp_mix = jax.vjp(lambda d: _layer0_mix(d, (W["ada_b"][0, :two], (w_in5[:, :D_INNER], w_in5[:, D_INNER:]))), diff_mix)
    l_srcs, l_lands = _spread_wait_call(late[0], late[1], late[2], late[3], y2, "gather_s5_late_wait")
    w_glu, w_o5 = [lax.dynamic_update_slice(land, src[None], (me, 0, 0)).reshape(-1, src.shape[1]) for land, src in zip(l_lands, l_srcs)]
    diff_out = (y2, z5, slot(D_INNER, D_INNER), slot(D_INNER, D_MODEL))
    o5, vjp_out = jax.vjp(lambda d: _layer0_out(d, (w_glu, w_o5)), diff_out)
    g_srcs, g_lands = _spread_wait_call(g_send, g_recv, g_srcs, g_lands, o5, "gather_gdn_wait")
    gdn_full = [lax.dynamic_update_slice(land, src[None], (me, 0, 0)) for land, src in zip(g_lands, g_srcs)]
    w_ing = _join_cols_call(gdn_full[0], "join_gdn_w_in")
    w_ba = jnp.concatenate([w_ing[:, GDN_CONV_CH + D_INNER:], jnp.zeros((D_MODEL, LANES - 2 * GDN_HEADS), BF16)], axis=1)
    weights1 = (w_ing[:, :GDN_QK], w_ing[:, GDN_QK:2 * GDN_QK], w_ing[:, 2 * GDN_QK:GDN_CONV_CH],
                w_ing[:, GDN_CONV_CH:GDN_CONV_CH + D_INNER], w_ba, gdn_full[1].reshape(D_INNER, D_MODEL))
    slots1 = tuple(jnp.zeros(w.shape, F32) for w in weights1)
    diff1 = (xp, o5, mod_raw[0, two:], mod_raw[1], W["norm_w"][1], conv_w, W["gdn_a_log"][0], W["gdn_dt_bias"][0], gdn_nw,
             W["final_norm_w"], *slots1)
    loss_local, vjp1 = jax.vjp(lambda d: _layer1_loss(d, (tgt[0], W["ada_b"][0, two:], W["ada_b"][1], weights1)), diff1)
    ((dxp, do5, dmod_gate, dmod1, d_norm_w1, d_conv, d_alog, d_dtb, d_gnw, d_fnw, d_wq, d_wk, d_wv, d_wgz, d_wba, d_wog),) = vjp1(
        jnp.ones((), F32))
    loss = lax.psum(loss_local, MESH_AXES)

    rows = lambda d: d.reshape(N_DEV, d.shape[0] // N_DEV, d.shape[1])
    d_ing = _split_cols_call(jnp.concatenate([d_wq, d_wk, d_wv, d_wgz, d_wba[:, :2 * GDN_HEADS]], axis=1), "split_gdn_w_in", BF16)
    s_send, s_recv, s_srcs, s_lands, s_token = _spread_start_call([d_ing, rows(d_wog).astype(BF16)], True, "scatter_gdn_start", dxp)
    ((dy2, dz5, d_wglu, d_wo5),) = vjp_out(do5.at[0, 0].add(s_token[0, 0]))
    t_send, t_recv, t_srcs, t_lands, t_token = _spread_start_call(
        [rows(d_wglu).astype(BF16), rows(d_wo5).astype(BF16)], True, "scatter_s5_late_start", dy2)
    ((dx, dmod_ss, d_norm_w0, d_lre, d_lim, d_logdt, d_bre, d_bim, d_cre, d_cim, d_s5d, d_wu, d_wz),) = vjp_mix(
        (dxp.at[0, 0].add(t_token[0, 0]), dy2, dz5))
    dmod = jnp.stack([jnp.concatenate([dmod_ss, dmod_gate]), dmod1])
    d_norm_w = jnp.stack([d_norm_w0, d_norm_w1])
    d_in5 = _split_cols_call(jnp.concatenate([d_wu, d_wz], axis=1), "split_s5_w_in", F32)
    (got,) = _pair_exchange_call([d_in5], "scatter_s5_pair")
    core = jnp.reshape(ic, (1,)).astype(jnp.int32)
    chip = jnp.reshape(2 * ix + iy, (1,)).astype(jnp.int32)
    pair = _pair_sum_call(d_in5, got, core, "pair_sum_s5_w_in")
    (recv,) = _chip_exchange_call([pair], "scatter_s5_chips")
    big = {"s5_w_in": _adam_own_call(pair, chip, recv, W["s5_w_in"][0], M["s5_w_in"][0], V["s5_w_in"][0], "adam_s5_w_in", 128)}
    t_srcs, t_lands = _spread_wait_call(t_send, t_recv, t_srcs, t_lands, dx, "scatter_s5_late_wait")
    s_srcs, s_lands = _spread_wait_call(s_send, s_recv, s_srcs, s_lands, t_lands[0], "scatter_gdn_wait")
    for land, src, n in zip(tuple(t_lands) + tuple(s_lands), tuple(t_srcs) + tuple(s_srcs), ("s5_w_glu", "s5_w_out") + GDN_BIG):
        mine = lax.dynamic_index_in_dim(src, me, 0, keepdims=True)
        parts = lax.dynamic_update_slice(land, mine, (me, 0, 0))
        big[n] = _adam_call(parts, W[n][0], M[n][0], V[n][0], "adam_" + n, rows=_tile(W[n].shape[1], 128))
    big = [[o[None] for o in big[n]] for n in BIG_NAMES]

    vec_parts = [dmod, d_norm_w, d_lre, d_lim, d_logdt, d_s5d, d_alog, d_dtb, d_fnw]
    tail_parts = [d_conv, d_gnw]
    mat_parts = [d_bre, d_bim, d_cre, d_cim]
    n_vec = sum(math.prod(p.shape) for p in vec_parts)
    sg_vec, sg_mat = _gather_weights_call(
        [_pack(vec_parts + tail_parts, ADAM_ROWS), _pack(mat_parts, SUBLANES).astype(BF16)], "gather_small_grads")
    tot_vec = _sum_call(sg_vec, "sum_vec_grads", ADAM_ROWS)
    tot_mat = _sum_call(sg_mat, "sum_mat_grads", ADAM_ROWS)
    g_conv, g_gnw = _unpack(tot_vec.reshape(-1)[n_vec:], [d_conv.shape, d_gnw.shape])
    g_conv_mine = lax.dynamic_slice_in_dim(g_conv, me * n_conv, n_conv, axis=1)
    g_gnw_mine = lax.dynamic_slice_in_dim(g_gnw, me * n_gnw, n_gnw, axis=0)
    vec_names = VEC_NAMES + ("gdn_conv_w", "gdn_norm_w")
    vec_g = _pack([tot_vec.reshape(-1)[:n_vec], g_conv_mine, g_gnw_mine], ADAM_ROWS)
    vec = _adam_call(vec_g[None], _pack([W[n] for n in vec_names], ADAM_ROWS), _pack([M[n] for n in vec_names], ADAM_ROWS),
                     _pack([V[n] for n in vec_names], ADAM_ROWS), "adam_vec")
    vec = [_unpack(b, [W[n].shape for n in vec_names]) for b in vec]
    mats = []
    for name, g_mat in zip(MAT_NAMES, _unpack(tot_mat, [p.shape for p in mat_parts])):
        two_d = (-1, W[name].shape[-1])
        outs = _adam_call(g_mat.reshape(two_d)[None], W[name].reshape(two_d), M[name].reshape(two_d), V[name].reshape(two_d),
                          "adam_" + name, rows=1024)
        mats.append([o.reshape(W[name].shape) for o in outs])

    dmod_all = sg_vec[:, :2 * 3 * D_MODEL // LANES].reshape(N_DEV, 2, N_DEV, n_ada // LANES, LANES)
    dmod_mine = lax.dynamic_index_in_dim(dmod_all, me, axis=2, keepdims=False).transpose(1, 0, 2, 3).reshape(2, N_DEV, n_ada)
    g_ada_w = _ada_grad_call(c_all, dmod_mine)
    ada = _adam_call(g_ada_w.reshape(1, -1, LANES), W["ada_w"].reshape(-1, LANES), M["ada_w"].reshape(-1, LANES),
                     V["ada_w"].reshape(-1, LANES), "adam_ada")
    ada = [a.reshape(W["ada_w"].shape) for a in ada]

    res = {}
    for i, n in enumerate(BIG_NAMES):
        res[n] = big[i]
    for i, n in enumerate(vec_names):
        res[n] = [b[i] for b in vec]
    for i, n in enumerate(MAT_NAMES):
        res[n] = mats[i]
    res["ada_w"] = ada
    outs = [loss, dx[None]]
    for j in range(4):
        outs += [res[n][j] for n in WEIGHT_ORDER]
    return tuple(outs)


def kernel(x, c, ada_w, ada_b, norm_w, s5_w_in, s5_lambda_re, s5_lambda_im, s5_log_dt, s5_b_re, s5_b_im, s5_c_re, s5_c_im, s5_d, s5_w_glu, s5_w_out, gdn_w_in, gdn_conv_w, gdn_a_log, gdn_dt_bias, gdn_norm_w, gdn_w_out, final_norm_w, loss_target, m_ada_w, m_ada_b, m_norm_w, m_s5_w_in, m_s5_lambda_re, m_s5_lambda_im, m_s5_log_dt, m_s5_b_re, m_s5_b_im, m_s5_c_re, m_s5_c_im, m_s5_d, m_s5_w_glu, m_s5_w_out, m_gdn_w_in, m_gdn_conv_w, m_gdn_a_log, m_gdn_dt_bias, m_gdn_norm_w, m_gdn_w_out, m_final_norm_w, v_ada_w, v_ada_b, v_norm_w, v_s5_w_in, v_s5_lambda_re, v_s5_lambda_im, v_s5_log_dt, v_s5_b_re, v_s5_b_im, v_s5_c_re, v_s5_c_im, v_s5_d, v_s5_w_glu, v_s5_w_out, v_gdn_w_in, v_gdn_conv_w, v_gdn_a_log, v_gdn_dt_bias, v_gdn_norm_w, v_gdn_w_out, v_final_norm_w):
    W = dict(ada_w=ada_w, ada_b=ada_b, norm_w=norm_w, s5_w_in=s5_w_in, s5_lambda_re=s5_lambda_re, s5_lambda_im=s5_lambda_im,
             s5_log_dt=s5_log_dt, s5_b_re=s5_b_re, s5_b_im=s5_b_im, s5_c_re=s5_c_re, s5_c_im=s5_c_im, s5_d=s5_d,
             s5_w_glu=s5_w_glu, s5_w_out=s5_w_out, gdn_w_in=gdn_w_in, gdn_conv_w=gdn_conv_w, gdn_a_log=gdn_a_log,
             gdn_dt_bias=gdn_dt_bias, gdn_norm_w=gdn_norm_w, gdn_w_out=gdn_w_out, final_norm_w=final_norm_w)
    M = dict(ada_w=m_ada_w, ada_b=m_ada_b, norm_w=m_norm_w, s5_w_in=m_s5_w_in, s5_lambda_re=m_s5_lambda_re,
             s5_lambda_im=m_s5_lambda_im, s5_log_dt=m_s5_log_dt, s5_b_re=m_s5_b_re, s5_b_im=m_s5_b_im, s5_c_re=m_s5_c_re,
             s5_c_im=m_s5_c_im, s5_d=m_s5_d, s5_w_glu=m_s5_w_glu, s5_w_out=m_s5_w_out, gdn_w_in=m_gdn_w_in,
             gdn_conv_w=m_gdn_conv_w, gdn_a_log=m_gdn_a_log, gdn_dt_bias=m_gdn_dt_bias, gdn_norm_w=m_gdn_norm_w,
             gdn_w_out=m_gdn_w_out, final_norm_w=m_final_norm_w)
    V = dict(ada_w=v_ada_w, ada_b=v_ada_b, norm_w=v_norm_w, s5_w_in=v_s5_w_in, s5_lambda_re=v_s5_lambda_re,
             s5_lambda_im=v_s5_lambda_im, s5_log_dt=v_s5_log_dt, s5_b_re=v_s5_b_re, s5_b_im=v_s5_b_im, s5_c_re=v_s5_c_re,
             s5_c_im=v_s5_c_im, s5_d=v_s5_d, s5_w_glu=v_s5_w_glu, s5_w_out=v_s5_w_out, gdn_w_in=v_gdn_w_in,
             gdn_conv_w=v_gdn_conv_w, gdn_a_log=v_gdn_a_log, gdn_dt_bias=v_gdn_dt_bias, gdn_norm_w=v_gdn_norm_w,
             gdn_w_out=v_gdn_w_out, final_norm_w=v_final_norm_w)
    return _step(x, c, W, M, V, loss_target)
```

```python
import functools
import math

import jax
import jax.numpy as jnp
from jax import lax
from jax.experimental import pallas as pl
from jax.experimental.pallas import tpu as pltpu

F32 = jnp.float32
BF16 = jnp.bfloat16
SDS = jax.ShapeDtypeStruct

D_MODEL = 1024
D_INNER = 2048
NORM_EPS = 1e-6
S5_GROUP = 16
S5_GROUPS = 128
S5_STATE = 64
GDN_HEADS = 8
GDN_DK = 128
GDN_DV = 256
GDN_CONV = 4
GDN_CHUNK = 64
GDN_QK = 1024
GDN_CONV_CH = 4096
GDN_PROJ = 6160
ADAM_LR = 0.001
ADAM_B1 = 0.9
ADAM_B2 = 0.999
ADAM_EPS = 1e-08
ADAM_WD = 0.01
ADAM_STEP = 10

N_DEV = 8
LANES = 128
SUBLANES = 8
VMEM_BIG = 56 << 20
VMEM_MID = 40 << 20
S5_GB = 8
S5_TL = 1024
MESH_AXES = ("x", "y", "c")


def _params(sem, vmem=None):
    return pltpu.CompilerParams(dimension_semantics=sem, vmem_limit_bytes=vmem)


def _bdot(a, b, dims=(((1,), (0,)), ((), ()))):
    return lax.dot_general(a.astype(BF16), b.astype(BF16), dims, preferred_element_type=F32)


def _hdot(a, b, dims=(((1,), (0,)), ((), ()))):
    return lax.dot_general(a, b, dims, preferred_element_type=F32, precision=lax.Precision.HIGHEST)


_BNN = (((2,), (1,)), ((0,), (0,)))
_BNT = (((2,), (2,)), ((0,), (0,)))
_BTN = (((1,), (1,)), ((0,), (0,)))


@jax.custom_vjp
def _unit_lower_inverse(a):
    c = a.shape[-1]
    ri = lax.broadcasted_iota(jnp.int32, a.shape, 1)
    ci = lax.broadcasted_iota(jnp.int32, a.shape, 2)
    n = -a
    t = (ri == ci).astype(F32) + n
    for _ in range(int(math.log2(c)) - 1):
        n = _hdot(n, n, _BNN)
        t = t + _hdot(t, n, _BNN)
    return t


def _unit_lower_inverse_fwd(a):
    t = _unit_lower_inverse(a)
    return t, t


def _unit_lower_inverse_bwd(t, g):
    return (-_hdot(_hdot(t, g, _BTN), t, _BNT),)


_unit_lower_inverse.defvjp(_unit_lower_inverse_fwd, _unit_lower_inverse_bwd)


NN = (((1,), (0,)), ((), ()))
NT = (((1,), (1,)), ((), ()))
TN = (((0,), (0,)), ((), ()))


def _tile(n, pref):
    for t in (pref, 512, 256, 128):
        if t <= n and n % t == 0:
            return t
    return n


def _matmul(a, b, mode, name, add=None):
    if mode == "nn":
        (m, k), (_, n) = a.shape, b.shape
    elif mode == "nt":
        (m, k), (n, _) = a.shape, b.shape
    else:
        (k, m), (_, n) = a.shape, b.shape
    tm, tn, tk = _tile(m, 512), _tile(n, 512), (k if k <= 2048 else _tile(k, 512))
    if mode == "tn":
        tm, tn = _tile(m, 1024), _tile(n, 1024)
    nk = k // tk
    dims = {"nn": NN, "nt": NT, "tn": TN}[mode]

    def body(a_ref, b_ref, *rest):
        o_ref, acc_ref = rest[-2], rest[-1]
        kk = pl.program_id(2)

        @pl.when(kk == 0)
        def _():
            acc_ref[...] = jnp.zeros_like(acc_ref) if add is None else rest[0][...]
        acc_ref[...] += _bdot(a_ref[...], b_ref[...], dims)

        @pl.when(kk == nk - 1)
        def _():
            o_ref[...] = acc_ref[...]

    a_spec = pl.BlockSpec((tk, tm), lambda i, j, q: (q, i)) if mode == "tn" else pl.BlockSpec((tm, tk), lambda i, j, q: (i, q))
    b_spec = pl.BlockSpec((tn, tk), lambda i, j, q: (j, q)) if mode == "nt" else pl.BlockSpec((tk, tn), lambda i, j, q: (q, j))
    o_spec = pl.BlockSpec((tm, tn), lambda i, j, q: (i, j))
    return pl.pallas_call(
        body, name=name, grid=(m // tm, n // tn, nk),
        in_specs=[a_spec, b_spec] + ([] if add is None else [o_spec]), out_specs=o_spec,
        out_shape=SDS((m, n), F32), scratch_shapes=[pltpu.VMEM((tm, tn), F32)],
        compiler_params=_params(("parallel", "parallel", "arbitrary"), VMEM_MID),
    )(a, b, *([] if add is None else [add]))


def make_mm(name, pass_input=False):
    def primal(a, w):
        out = _matmul(a, w, "nn", name + "_fwd")
        return (out, a) if pass_input else out

    @jax.custom_vjp
    def mm(a, w, grad_slot):
        return primal(a, w)

    def fwd(a, w, grad_slot):
        return primal(a, w), (a, w)

    def bwd(res, g):
        a, w = res
        g, g_other = g if pass_input else (g, None)
        return _matmul(g, w, "nt", name + "_dx", add=g_other), jnp.zeros_like(w), _matmul(a, g, "tn", name + "_dw")

    mm.defvjp(fwd, bwd)
    return mm


PROJ_ROWS = 256


def _proj_fwd_call(a, ws, name):
    m, k = a.shape
    tm = _tile(m, PROJ_ROWS)
    nw = len(ws)

    def body(*refs):
        ab = refs[0][...].astype(BF16)
        for w_ref, o_ref in zip(refs[1:1 + nw], refs[1 + nw:]):
            o_ref[...] = lax.dot_general(ab, w_ref[...], NN, preferred_element_type=F32)

    return pl.pallas_call(
        body, name=name, grid=(m // tm,),
        in_specs=[pl.BlockSpec((tm, k), lambda i: (i, 0))] + [pl.BlockSpec(w.shape, lambda i: (0, 0)) for w in ws],
        out_specs=[pl.BlockSpec((tm, w.shape[1]), lambda i: (i, 0)) for w in ws],
        out_shape=[SDS((m, w.shape[1]), F32) for w in ws],
        compiler_params=_params(("parallel",), VMEM_BIG),
    )(a, *ws)


def _proj_dx_call(gs, ws, name):
    m = gs[0].shape[0]
    k = ws[0].shape[0]
    tm = _tile(m, PROJ_ROWS)
    nw = len(ws)

    def body(*refs):
        acc = None
        for g_ref, w_ref in zip(refs[:nw], refs[nw:2 * nw]):
            part = _bdot(g_ref[...], w_ref[...], NT)
            acc = part if acc is None else acc + part
        refs[2 * nw][...] = acc

    return pl.pallas_call(
        body, name=name, grid=(m // tm,),
        in_specs=[pl.BlockSpec((tm, g.shape[1]), lambda i: (i, 0)) for g in gs] + [pl.BlockSpec(w.shape, lambda i: (0, 0)) for w in ws],
        out_specs=pl.BlockSpec((tm, k), lambda i: (i, 0)), out_shape=SDS((m, k), F32),
        compiler_params=_params(("parallel",), VMEM_BIG),
    )(*gs, *ws)


def make_proj(name):
    @jax.custom_vjp
    def proj(a, ws, grad_slots):
        return tuple(_proj_fwd_call(a, ws, name + "_fwd"))

    def fwd(a, ws, grad_slots):
        return tuple(_proj_fwd_call(a, ws, name + "_fwd")), (a, ws)

    def bwd(res, gs):
        a, ws = res
        dws = tuple(_matmul(a, g, "tn", "%s_dw%d" % (name, i)) for i, g in enumerate(gs))
        return _proj_dx_call(tuple(gs), ws, name + "_dx"), tuple(jnp.zeros_like(w) for w in ws), dws

    proj.defvjp(fwd, bwd)
    return proj


def make_rowwise(f, name, tm, n_rows, n_params, vmem=VMEM_MID, pass_first=False):
    def specs_of(arrs, blocked):
        if blocked:
            return [pl.BlockSpec((tm, a.shape[1]), lambda i: (i, 0)) for a in arrs]
        return [pl.BlockSpec(a.shape, lambda i: (0, 0)) for a in arrs]

    def out_structs(rows, params):
        blk = [SDS((tm, r.shape[1]), r.dtype) for r in rows] + [SDS(p.shape, p.dtype) for p in params]
        return jax.eval_shape(f, *blk)

    def run_fwd(rows, params):
        L = rows[0].shape[0]
        outs = out_structs(rows, params)

        def body(*refs):
            ins = [r[...] for r in refs[:n_rows + n_params]]
            res = f(*ins)
            for o_ref, val in zip(refs[n_rows + n_params:], res):
                o_ref[...] = val

        return pl.pallas_call(
            body, name=name + "_fwd", grid=(L // tm,),
            in_specs=specs_of(rows, True) + specs_of(params, False),
            out_specs=[pl.BlockSpec((tm, o.shape[1]), lambda i: (i, 0)) for o in outs],
            out_shape=[SDS((L, o.shape[1]), o.dtype) for o in outs],
            compiler_params=_params(("parallel",), vmem),
        )(*rows, *params)

    def run_bwd(rows, params, gs):
        L = rows[0].shape[0]
        n_g = len(gs)

        def body(*refs):
            i = pl.program_id(0)
            ins = [r[...] for r in refs[:n_rows + n_params]]
            cts = tuple(r[...] for r in refs[n_rows + n_params:n_rows + n_params + n_g])
            outs = refs[n_rows + n_params + n_g:]
            _, vjp = jax.vjp(f, *ins)
            grads = vjp(cts[:-1] if pass_first else cts)
            if pass_first:
                grads = (grads[0] + cts[-1],) + tuple(grads[1:])
            for o_ref, val in zip(outs[:n_rows], grads[:n_rows]):
                o_ref[...] = val

            if n_params:
                @pl.when(i == 0)
                def _():
                    for o_ref in outs[n_rows:]:
                        o_ref[...] = jnp.zeros_like(o_ref)
                for o_ref, val in zip(outs[n_rows:], grads[n_rows:]):
                    o_ref[...] += val

        res = pl.pallas_call(
            body, name=name + "_bwd", grid=(L // tm,),
            in_specs=specs_of(rows, True) + specs_of(params, False) + specs_of(gs, True),
            out_specs=specs_of(rows, True) + specs_of(params, False),
            out_shape=[SDS(r.shape, r.dtype) for r in rows] + [SDS(p.shape, p.dtype) for p in params],
            compiler_params=_params(("arbitrary",), vmem),
        )(*rows, *params, *gs)
        return tuple(res[:n_rows]), tuple(res[n_rows:])

    def outputs(rows, params):
        outs = tuple(run_fwd(rows, params))
        return outs + (rows[0],) if pass_first else outs

    @jax.custom_vjp
    def op(rows, params):
        return outputs(rows, params)

    def fwd(rows, params):
        return outputs(rows, params), (rows, params)

    def bwd(res, gs):
        rows, params = res
        return run_bwd(rows, params, tuple(gs))

    op.defvjp(fwd, bwd)
    op.run_fwd, op.run_bwd = run_fwd, run_bwd
    return op


def make_residual(name, tm):
    full = make_rowwise(_f_res, name, tm, 2, 2)
    branch = make_rowwise(lambda y, gate, bgate: ((gate + bgate) * y,), name + "_branch", tm, 1, 2)

    @jax.custom_vjp
    def op(x, y, gate, bgate):
        return full.run_fwd((x, y), (gate, bgate))[0]

    def fwd(x, y, gate, bgate):
        return full.run_fwd((x, y), (gate, bgate))[0], (y, gate, bgate)

    def bwd(res, g):
        y, gate, bgate = res
        (dy,), (dgate, dbgate) = branch.run_bwd((y,), (gate, bgate), (g,))
        return g, dy, dgate, dbgate

    op.defvjp(fwd, bwd)
    return op


def _s5_scan_rows(xr_ref, xi_ref, ar, ai, x0r, x0i, tl, reverse=False):
    n = xr_ref.shape[1]
    T = SUBLANES
    row = lax.broadcasted_iota(jnp.int32, (T, n), 0)
    pr, pi = [ar], [ai]
    for _ in range(T - 1):
        pr, pi = pr + [pr[-1] * ar - pi[-1] * ai], pi + [pr[-1] * ai + pi[-1] * ar]
    levels = []
    for d in (1, 2, 4):
        mask = (row < T - d) if reverse else (row >= d)
        levels.append((T - d if reverse else d, jnp.where(mask, pr[d - 1], 0.0), jnp.where(mask, pi[d - 1], 0.0)))
    cr = jnp.zeros((T, n), F32)
    ci = jnp.zeros((T, n), F32)
    for r in range(T):
        k = (T - r) if reverse else (r + 1)
        cr = jnp.where(row == r, pr[k - 1], cr)
        ci = jnp.where(row == r, pi[k - 1], ci)
    nt = tl // T
    last = 0 if reverse else T - 1

    def step(t, carry):
        sr, si = carry
        base = pl.multiple_of((nt - 1 - t if reverse else t) * T, T)
        br = xr_ref[pl.ds(base, T), :]
        bi = xi_ref[pl.ds(base, T), :]
        for shift, mr, mi in levels:
            qr = pltpu.roll(br, shift, 0)
            qi = pltpu.roll(bi, shift, 0)
            br, bi = br + (mr * qr - mi * qi), bi + (mr * qi + mi * qr)
        xr = br + (cr * sr - ci * si)
        xi = bi + (cr * si + ci * sr)
        xr_ref[pl.ds(base, T), :] = xr
        xi_ref[pl.ds(base, T), :] = xi
        return xr[last:last + 1, :], xi[last:last + 1, :]
    return lax.fori_loop(0, nt, step, (x0r, x0i))


def _s5_fwd_call(u, bre, bim, cre, cim, a, d, tl):
    L, e = u.shape
    nb = e // LANES
    ns = bre.shape[2]
    nc = L // tl

    def body(u_ref, bre_ref, bim_ref, cre_ref, cim_ref, a_ref, d_ref, y_ref, xb_ref, sr_ref, si_ref, xr_ref, xi_ref, carry_ref):
        c = pl.program_id(1)

        @pl.when(c == 0)
        def _():
            carry_ref[...] = jnp.zeros_like(carry_ref)
        xb_ref[0, 0] = carry_ref[...]
        ub = u_ref[...]
        xr_ref[...] = _bdot(ub, bre_ref[0])
        xi_ref[...] = _bdot(ub, bim_ref[0])
        ar = a_ref[0, 0:1, :]
        ai = a_ref[0, 1:2, :]
        xr, xi = _s5_scan_rows(xr_ref, xi_ref, ar, ai, carry_ref[0:1, :], carry_ref[1:2, :], tl)
        carry_ref[0:1, :] = xr
        carry_ref[1:2, :] = xi
        sr = xr_ref[...].astype(BF16)
        si = xi_ref[...].astype(BF16)
        sr_ref[...] = sr
        si_ref[...] = si
        y_ref[...] = _f_s5_act(_bdot(sr, cre_ref[0]) - _bdot(si, cim_ref[0]), ub, d_ref[...])[0]

    return pl.pallas_call(
        body, name="s5_core_fwd", grid=(nb, nc),
        in_specs=[pl.BlockSpec((tl, LANES), lambda j, c: (c, j)),
                  pl.BlockSpec((1, LANES, ns), lambda j, c: (j, 0, 0)), pl.BlockSpec((1, LANES, ns), lambda j, c: (j, 0, 0)),
                  pl.BlockSpec((1, ns, LANES), lambda j, c: (j, 0, 0)), pl.BlockSpec((1, ns, LANES), lambda j, c: (j, 0, 0)),
                  pl.BlockSpec((1, SUBLANES, ns), lambda j, c: (j, 0, 0)), pl.BlockSpec((1, LANES), lambda j, c: (0, j))],
        out_specs=[pl.BlockSpec((tl, LANES), lambda j, c: (c, j)),
                   pl.BlockSpec((1, 1, SUBLANES, ns), lambda j, c: (j, c, 0, 0)),
                   pl.BlockSpec((tl, ns), lambda j, c: (c, j)), pl.BlockSpec((tl, ns), lambda j, c: (c, j))],
        out_shape=[SDS((L, e), F32), SDS((nb, nc, SUBLANES, ns), F32), SDS((L, nb * ns), BF16), SDS((L, nb * ns), BF16)],
        scratch_shapes=[pltpu.VMEM((tl, ns), F32), pltpu.VMEM((tl, ns), F32), pltpu.VMEM((SUBLANES, ns), F32)],
        compiler_params=_params(("arbitrary", "arbitrary"), VMEM_MID),
    )(u, bre, bim, cre, cim, a, d)


def _s5_bwd_call(u, dy2, bre, bim, cre, cim, a, d, xb, sr, si, tl):
    L, e = u.shape
    nb = e // LANES
    ns = bre.shape[2]
    nc = L // tl

    def body(u_ref, dy2_ref, bre_ref, bim_ref, cre_ref, cim_ref, a_ref, d_ref, xb_ref, sr_ref, si_ref,
             du_ref, dbre_ref, dbim_ref, dcre_ref, dcim_ref, da_ref, dd_ref,
             gr_ref, gi_ref, gcarry_ref):
        c = pl.program_id(1)

        @pl.when(c == 0)
        def _():
            gcarry_ref[...] = jnp.zeros_like(gcarry_ref)
            dbre_ref[...] = jnp.zeros_like(dbre_ref)
            dbim_ref[...] = jnp.zeros_like(dbim_ref)
            dcre_ref[...] = jnp.zeros_like(dcre_ref)
            dcim_ref[...] = jnp.zeros_like(dcim_ref)
            da_ref[...] = jnp.zeros_like(da_ref)
            dd_ref[...] = jnp.zeros_like(dd_ref)

        ub = u_ref[...]
        ys = _bdot(sr_ref[...], cre_ref[0]) - _bdot(si_ref[...], cim_ref[0])
        _, act_vjp = jax.vjp(lambda *t: _f_s5_act(*t)[0], ys, ub, d_ref[...])
        dy, du_skip, dd = act_vjp(dy2_ref[...])
        dd_ref[...] += dd
        ar = a_ref[0, 0:1, :]
        ai = a_ref[0, 1:2, :]
        x0r = xb_ref[0, 0, 0:1, :]
        x0i = xb_ref[0, 0, 1:2, :]
        dcre_ref[0] += _bdot(sr_ref[...], dy, TN)
        dcim_ref[0] -= _bdot(si_ref[...], dy, TN)
        gr_ref[...] = _bdot(dy, cre_ref[0], NT)
        gi_ref[...] = -_bdot(dy, cim_ref[0], NT)

        g0r, g0i = _s5_scan_rows(gr_ref, gi_ref, ar, -ai, gcarry_ref[0:1, :], gcarry_ref[1:2, :], tl, reverse=True)
        gcarry_ref[0:1, :] = g0r
        gcarry_ref[1:2, :] = g0i
        row = lax.broadcasted_iota(jnp.int32, (tl, ns), 0)
        gr = gr_ref[...]
        gi = gi_ref[...]
        xpr = jnp.where(row == 0, x0r, pltpu.roll(sr_ref[...].astype(F32), 1, 0))
        xpi = jnp.where(row == 0, x0i, pltpu.roll(si_ref[...].astype(F32), 1, 0))
        da_ref[0, 0:1, :] += jnp.sum(gr * xpr + gi * xpi, axis=0, keepdims=True)
        da_ref[0, 1:2, :] += jnp.sum(gi * xpr - gr * xpi, axis=0, keepdims=True)
        du_ref[...] = (_bdot(gr, bre_ref[0], NT) + _bdot(gi, bim_ref[0], NT)) + du_skip
        dbre_ref[0] += _bdot(ub, gr, TN)
        dbim_ref[0] += _bdot(ub, gi, TN)

    rev = lambda c: nc - 1 - c
    return pl.pallas_call(
        body, name="s5_core_bwd", grid=(nb, nc),
        in_specs=[pl.BlockSpec((tl, LANES), lambda j, c: (rev(c), j)), pl.BlockSpec((tl, LANES), lambda j, c: (rev(c), j)),
                  pl.BlockSpec((1, LANES, ns), lambda j, c: (j, 0, 0)), pl.BlockSpec((1, LANES, ns), lambda j, c: (j, 0, 0)),
                  pl.BlockSpec((1, ns, LANES), lambda j, c: (j, 0, 0)), pl.BlockSpec((1, ns, LANES), lambda j, c: (j, 0, 0)),
                  pl.BlockSpec((1, SUBLANES, ns), lambda j, c: (j, 0, 0)), pl.BlockSpec((1, LANES), lambda j, c: (0, j)),
                  pl.BlockSpec((1, 1, SUBLANES, ns), lambda j, c: (j, rev(c), 0, 0)),
                  pl.BlockSpec((tl, ns), lambda j, c: (rev(c), j)), pl.BlockSpec((tl, ns), lambda j, c: (rev(c), j))],
        out_specs=[pl.BlockSpec((tl, LANES), lambda j, c: (rev(c), j)),
                   pl.BlockSpec((1, LANES, ns), lambda j, c: (j, 0, 0)), pl.BlockSpec((1, LANES, ns), lambda j, c: (j, 0, 0)),
                   pl.BlockSpec((1, ns, LANES), lambda j, c: (j, 0, 0)), pl.BlockSpec((1, ns, LANES), lambda j, c: (j, 0, 0)),
                   pl.BlockSpec((1, SUBLANES, ns), lambda j, c: (j, 0, 0)), pl.BlockSpec((1, LANES), lambda j, c: (0, j))],
        out_shape=[SDS((L, e), F32), SDS(bre.shape, F32), SDS(bim.shape, F32), SDS(cre.shape, F32), SDS(cim.shape, F32),
                   SDS(a.shape, F32), SDS(d.shape, F32)],
        scratch_shapes=[pltpu.VMEM((tl, ns), F32) for _ in range(2)] + [pltpu.VMEM((SUBLANES, ns), F32)],
        compiler_params=_params(("arbitrary", "arbitrary"), VMEM_MID),
    )(u, dy2, bre, bim, cre, cim, a, d, xb, sr, si)


def make_s5_core(tl):
    @jax.custom_vjp
    def s5_core(u, bre, bim, cre, cim, a, d):
        return _s5_fwd_call(u, bre, bim, cre, cim, a, d, tl)[0]

    def fwd(u, bre, bim, cre, cim, a, d):
        y2, xb, sr, si = _s5_fwd_call(u, bre, bim, cre, cim, a, d, tl)
        return y2, (u, bre, bim, cre, cim, a, d, xb, sr, si)

    def bwd(res, dy2):
        u, bre, bim, cre, cim, a, d, xb, sr, si = res
        return tuple(_s5_bwd_call(u, dy2, bre, bim, cre, cim, a, d, xb, sr, si, tl))

    s5_core.defvjp(fwd, bwd)
    return s5_core


def _s5_block_params(lam_re, lam_im, log_dt, b_re, b_im, c_re, c_im):
    dt = jnp.exp(log_dt)[:, None]
    mag = jnp.exp(lam_re * dt)
    ab_re = mag * jnp.cos(lam_im * dt)
    ab_im = mag * jnp.sin(lam_im * dt)
    den = lam_re * lam_re + lam_im * lam_im
    nr = ab_re - 1.0
    ni = ab_im
    q_re = (nr * lam_re + ni * lam_im) / den
    q_im = (ni * lam_re - nr * lam_im) / den
    bb_re = q_re[..., None] * b_re - q_im[..., None] * b_im
    bb_im = q_re[..., None] * b_im + q_im[..., None] * b_re
    nb = S5_GROUPS // S5_GB
    eye = jnp.eye(S5_GB, dtype=F32)

    def bdiag_in(bb):
        t = bb.reshape(nb, S5_GB, S5_STATE, S5_GROUP)
        t = jnp.einsum("jgpm,gh->jgmhp", t, eye)
        return t.reshape(nb, S5_GB * S5_GROUP, S5_GB * S5_STATE)

    def bdiag_out(cc):
        t = cc.reshape(nb, S5_GB, S5_GROUP, S5_STATE)
        t = jnp.einsum("jgmp,gh->jgphm", t, eye)
        return t.reshape(nb, S5_GB * S5_STATE, S5_GB * S5_GROUP)

    a = jnp.stack([ab_re.reshape(nb, S5_GB * S5_STATE), ab_im.reshape(nb, S5_GB * S5_STATE)], axis=1)
    a = jnp.concatenate([a, jnp.zeros((nb, SUBLANES - 2, S5_GB * S5_STATE), F32)], axis=1)
    return bdiag_in(bb_re), bdiag_in(bb_im), bdiag_out(c_re), bdiag_out(c_im), a


def _shift_down(x, s, row):
    if s == 0:
        return x
    return jnp.where(row >= s, pltpu.roll(x, s, 0), 0.0)


def _shift_up(x, s, row, n):
    if s == 0:
        return x
    return jnp.where(row < n - s, pltpu.roll(x, n - s, 0), 0.0)


def _causal_conv(xv, w_ref, row):
    acc = jnp.zeros_like(xv)
    for j in range(GDN_CONV):
        acc += w_ref[j:j + 1, :] * _shift_down(xv, GDN_CONV - 1 - j, row)
    return acc


def _conv_fwd_call(x, w, act, name):
    L, ch = x.shape

    def body(x_ref, w_ref, y_ref):
        xv = x_ref[...]
        row = lax.broadcasted_iota(jnp.int32, xv.shape, 0)
        y_ref[...] = act(_causal_conv(xv, w_ref, row))

    return pl.pallas_call(
        body, name=name + "_fwd", grid=(ch // LANES,),
        in_specs=[pl.BlockSpec((L, LANES), lambda j: (0, j)), pl.BlockSpec((SUBLANES, LANES), lambda j: (0, j))],
        out_specs=pl.BlockSpec((L, LANES), lambda j: (0, j)), out_shape=SDS((L, ch), F32),
        compiler_params=_params(("parallel",), VMEM_MID),
    )(x, w)


def _conv_bwd_call(x, w, dy, act, name):
    L, ch = x.shape

    def body(x_ref, w_ref, dy_ref, dx_ref, dw_ref):
        xv = x_ref[...]
        row = lax.broadcasted_iota(jnp.int32, xv.shape, 0)
        _, act_vjp = jax.vjp(act, _causal_conv(xv, w_ref, row))
        (g,) = act_vjp(dy_ref[...])
        acc = jnp.zeros_like(xv)
        dws = []
        for j in range(GDN_CONV):
            s = GDN_CONV - 1 - j
            acc += w_ref[j:j + 1, :] * _shift_up(g, s, row, L)
            dws.append(jnp.sum(g * _shift_down(xv, s, row), axis=0, keepdims=True))
        dx_ref[...] = acc
        dw_ref[...] = jnp.concatenate(dws + [jnp.zeros((SUBLANES - GDN_CONV, LANES), F32)], axis=0)

    return pl.pallas_call(
        body, name=name + "_bwd", grid=(ch // LANES,),
        in_specs=[pl.BlockSpec((L, LANES), lambda j: (0, j)), pl.BlockSpec((SUBLANES, LANES), lambda j: (0, j)),
                  pl.BlockSpec((L, LANES), lambda j: (0, j))],
        out_specs=[pl.BlockSpec((L, LANES), lambda j: (0, j)), pl.BlockSpec((SUBLANES, LANES), lambda j: (0, j))],
        out_shape=[SDS((L, ch), F32), SDS((SUBLANES, ch), F32)],
        compiler_params=_params(("parallel",), VMEM_MID),
    )(x, w, dy)


def make_conv_act(act, name):
    @jax.custom_vjp
    def op(x, w):
        return _conv_fwd_call(x, w, act, name)

    def fwd(x, w):
        return _conv_fwd_call(x, w, act, name), (x, w)

    def bwd(res, dy):
        x, w = res
        return tuple(_conv_bwd_call(x, w, dy, act, name))

    op.defvjp(fwd, bwd)
    return op


gdn_conv = make_conv_act(lambda c: c, "gdn_conv")


BNN = (((2,), (1,)), ((0,), (0,)))
BNT = (((2,), (2,)), ((0,), (0,)))
BTN = (((1,), (1,)), ((0,), (0,)))
GDN_PREP_BATCH = 8


@jax.custom_vjp
def _known_inverse(a, t):
    return t


def _known_inverse_fwd(a, t):
    return t, t


def _known_inverse_bwd(t, g):
    return -_hdot(_hdot(t, g, _BTN), t, _BNT), jnp.zeros_like(t)


_known_inverse.defvjp(_known_inverse_fwd, _known_inverse_bwd)


def _gdn_prep_math(q, k, v, beta, g, t_saved=None):
    B, C = q.shape[0], q.shape[1]
    ri = lax.broadcasted_iota(jnp.int32, (B, C, C), 1)
    ci = lax.broadcasted_iota(jnp.int32, (B, C, C), 2)
    causal = ri >= ci
    strict = ri > ci
    eye = (ri == ci).astype(F32)
    gb = jnp.broadcast_to(g, (B, C, C))
    g_row = jnp.sum(gb * eye, axis=1, keepdims=True)
    gc_col = jnp.sum(jnp.where(causal, jnp.broadcast_to(g_row, (B, C, C)), 0.0), axis=2, keepdims=True)
    gc_row = jnp.sum(jnp.where(ri <= ci, gb, 0.0), axis=1, keepdims=True)
    decay = jnp.exp(jnp.where(causal, gc_col - gc_row, -jnp.inf))
    kk = _bdot(k, k, BNT)
    a_mat = jnp.where(strict, beta * kk * decay, 0.0)
    t = _unit_lower_inverse(a_mat) if t_saved is None else _known_inverse(a_mat, t_saved)
    e_gc = jnp.exp(gc_col)
    w = _hdot(t, beta * e_gc * k, BNN)
    u = _hdot(t, beta * v, BNN)
    qk = _bdot(q, k, BNT) * decay
    q_dec = q * e_gc
    g_last = gc_col[:, C - 1:C, :]
    k_dec = k * jnp.exp(g_last - gc_col)
    return q_dec, w, u, qk, k_dec, gc_col, t


def _gdn_prep_specs(L):
    C = GDN_CHUNK
    nb = min(GDN_PREP_BATCH, L // C)
    R = nb * C
    ins = [pl.BlockSpec((R, GDN_DK), lambda c, h: (c, h)), pl.BlockSpec((R, GDN_DK), lambda c, h: (c, h)),
           pl.BlockSpec((R, GDN_DV), lambda c, h: (c, h)), pl.BlockSpec((R, LANES), lambda c, h: (c, 0))]
    outs = [pl.BlockSpec((1, R, GDN_DK), lambda c, h: (h, c, 0)), pl.BlockSpec((1, R, GDN_DK), lambda c, h: (h, c, 0)),
            pl.BlockSpec((1, R, GDN_DV), lambda c, h: (h, c, 0)), pl.BlockSpec((1, R, C), lambda c, h: (h, c, 0)),
            pl.BlockSpec((1, R, GDN_DK), lambda c, h: (h, c, 0)), pl.BlockSpec((1, R, 1), lambda c, h: (h, c, 0))]
    t_spec = pl.BlockSpec((1, R, C), lambda c, h: (h, c, 0))
    shapes = [SDS((GDN_HEADS, L, GDN_DK), F32), SDS((GDN_HEADS, L, GDN_DK), F32), SDS((GDN_HEADS, L, GDN_DV), F32),
              SDS((GDN_HEADS, L, C), F32), SDS((GDN_HEADS, L, GDN_DK), F32), SDS((GDN_HEADS, L, 1), F32)]
    return ins, outs, t_spec, shapes, nb


def _chunks(x, nb):
    return x.reshape(nb, x.shape[0] // nb, x.shape[1])


def _head_columns(bg, h):
    lane = lax.broadcasted_iota(jnp.int32, bg.shape, 1)
    beta = jnp.sum(jnp.where(lane == h, bg, 0.0), axis=1, keepdims=True)
    g = jnp.sum(jnp.where(lane == h + GDN_HEADS, bg, 0.0), axis=1, keepdims=True)
    return beta, g


def _gdn_prep_fwd_call(q, k, v, bg):
    L = q.shape[0]
    ins, outs, t_spec, shapes, nb = _gdn_prep_specs(L)

    def body(q_ref, k_ref, v_ref, bg_ref, *o_refs):
        beta, g = _head_columns(bg_ref[...], pl.program_id(1))
        res = _gdn_prep_math(_chunks(q_ref[...], nb), _chunks(k_ref[...], nb), _chunks(v_ref[...], nb),
                             _chunks(beta, nb), _chunks(g, nb))
        for o_ref, val in zip(o_refs, res):
            o_ref[0] = val.reshape(val.shape[0] * val.shape[1], val.shape[2])

    return pl.pallas_call(
        body, name="gdn_prep_fwd", grid=(L // (nb * GDN_CHUNK), GDN_HEADS), in_specs=ins, out_specs=outs + [t_spec],
        out_shape=shapes + [SDS((GDN_HEADS, L, GDN_CHUNK), F32)],
        compiler_params=_params(("parallel", "parallel"), VMEM_MID),
    )(q, k, v, bg)


def _gdn_prep_bwd_call(q, k, v, bg, t, cts):
    L = q.shape[0]
    ins, outs, t_spec, _, nb = _gdn_prep_specs(L)

    def body(q_ref, k_ref, v_ref, bg_ref, t_ref, c0, c1, c2, c3, c4, c5, dq_ref, dk_ref, dv_ref, dbg_ref):
        h = pl.program_id(1)
        beta, g = _head_columns(bg_ref[...], h)
        t_saved = _chunks(t_ref[0], nb)
        _, vjp = jax.vjp(lambda *a: _gdn_prep_math(*a, t_saved=t_saved)[:6], _chunks(q_ref[...], nb), _chunks(k_ref[...], nb),
                         _chunks(v_ref[...], nb), _chunks(beta, nb), _chunks(g, nb))
        dq, dk, dv, db, dg = vjp(tuple(_chunks(c[0], nb) for c in (c0, c1, c2, c3, c4, c5)))
        flat = lambda a: a.reshape(a.shape[0] * a.shape[1], a.shape[2])
        dq_ref[...] = flat(dq)
        dk_ref[...] = flat(dk)
        dv_ref[...] = flat(dv)

        @pl.when(h == 0)
        def _():
            dbg_ref[...] = jnp.zeros_like(dbg_ref)
        lane = lax.broadcasted_iota(jnp.int32, dbg_ref.shape, 1)
        dbg_ref[...] += jnp.where(lane == h, flat(db), 0.0) + jnp.where(lane == h + GDN_HEADS, flat(dg), 0.0)

    return pl.pallas_call(
        body, name="gdn_prep_bwd", grid=(L // (nb * GDN_CHUNK), GDN_HEADS), in_specs=ins + [t_spec] + outs, out_specs=ins,
        out_shape=[SDS(q.shape, F32), SDS(k.shape, F32), SDS(v.shape, F32), SDS(bg.shape, F32)],
        compiler_params=_params(("parallel", "arbitrary"), VMEM_MID),
    )(q, k, v, bg, t, *cts)


@jax.custom_vjp
def gdn_prep(q, k, v, bg):
    return tuple(_gdn_prep_fwd_call(q, k, v, bg)[:6])


def _gdn_prep_f(q, k, v, bg):
    res = _gdn_prep_fwd_call(q, k, v, bg)
    return tuple(res[:6]), (q, k, v, bg, res[6])


def _gdn_prep_b(res, cts):
    return tuple(_gdn_prep_bwd_call(*res, tuple(cts)))


gdn_prep.defvjp(_gdn_prep_f, _gdn_prep_b)


def _gdn_step_math(q_dec, w, u, qk, k_dec, gc, z, nw, state):
    H, C = q_dec.shape[0], q_dec.shape[1]
    v_new = u - _bdot(w, state, BNN)
    o = _bdot(q_dec, state, BNN) + _bdot(qk, v_new, BNN)
    gl = gc[:, C - 1:C, :]
    new_state = jnp.exp(gl) * state + _bdot(k_dec, v_new, BTN)
    return _f_gdn_post(jnp.concatenate([o[h] for h in range(H)], axis=1), z, nw)[0], new_state


def _gdn_scan_specs(L, rev):
    C, H = GDN_CHUNK, GDN_HEADS
    nc = L // C
    cc = (lambda c: nc - 1 - c) if rev else (lambda c: c)
    ins = [pl.BlockSpec((H, C, GDN_DK), lambda c: (0, cc(c), 0)), pl.BlockSpec((H, C, GDN_DK), lambda c: (0, cc(c), 0)),
           pl.BlockSpec((H, C, GDN_DV), lambda c: (0, cc(c), 0)), pl.BlockSpec((H, C, C), lambda c: (0, cc(c), 0)),
           pl.BlockSpec((H, C, GDN_DK), lambda c: (0, cc(c), 0)), pl.BlockSpec((H, C, 1), lambda c: (0, cc(c), 0))]
    o_spec = pl.BlockSpec((C, H * GDN_DV), lambda c: (cc(c), 0))
    nw_spec = pl.BlockSpec((1, H * GDN_DV), lambda c: (0, 0))
    s_spec = pl.BlockSpec((1, H, GDN_DK, GDN_DV), lambda c: (cc(c), 0, 0, 0))
    return ins + [o_spec, nw_spec], o_spec, s_spec, nc


def _gdn_scan_fwd_call(q_dec, w, u, qk, k_dec, gc, z, nw):
    L = q_dec.shape[1]
    ins, o_spec, s_spec, nc = _gdn_scan_specs(L, False)

    def body(qd_ref, w_ref, u_ref, qk_ref, kd_ref, gc_ref, z_ref, nw_ref, o_ref, sin_ref, s_ref):
        c = pl.program_id(0)

        @pl.when(c == 0)
        def _():
            s_ref[...] = jnp.zeros_like(s_ref)
        st = s_ref[...]
        sin_ref[0] = st
        o, ns = _gdn_step_math(qd_ref[...], w_ref[...], u_ref[...], qk_ref[...], kd_ref[...], gc_ref[...], z_ref[...], nw_ref[...], st)
        o_ref[...] = o
        s_ref[...] = ns

    return pl.pallas_call(
        body, name="gdn_scan_fwd", grid=(nc,), in_specs=ins, out_specs=[o_spec, s_spec],
        out_shape=[SDS((L, GDN_HEADS * GDN_DV), F32), SDS((nc, GDN_HEADS, GDN_DK, GDN_DV), F32)],
        scratch_shapes=[pltpu.VMEM((GDN_HEADS, GDN_DK, GDN_DV), F32)],
        compiler_params=_params(("arbitrary",), VMEM_MID),
    )(q_dec, w, u, qk, k_dec, gc, z, nw)


def _gdn_scan_bwd_call(q_dec, w, u, qk, k_dec, gc, z, nw, s_in, do):
    L = q_dec.shape[1]
    ins, o_spec, s_spec, nc = _gdn_scan_specs(L, True)

    def body(qd_ref, w_ref, u_ref, qk_ref, kd_ref, gc_ref, z_ref, nw_ref, sin_ref, do_ref,
             dqd_ref, dw_ref, du_ref, dqk_ref, dkd_ref, dgc_ref, dz_ref, dnw_ref, ds_ref):
        c = pl.program_id(0)

        @pl.when(c == 0)
        def _():
            ds_ref[...] = jnp.zeros_like(ds_ref)
            dnw_ref[...] = jnp.zeros_like(dnw_ref)
        _, vjp = jax.vjp(_gdn_step_math, qd_ref[...], w_ref[...], u_ref[...], qk_ref[...], kd_ref[...], gc_ref[...],
                         z_ref[...], nw_ref[...], sin_ref[0])
        dqd, dw, du, dqk, dkd, dgc, dz, dnw, dst = vjp((do_ref[...], ds_ref[...]))
        dqd_ref[...] = dqd
        dw_ref[...] = dw
        du_ref[...] = du
        dqk_ref[...] = dqk
        dkd_ref[...] = dkd
        dgc_ref[...] = dgc
        dz_ref[...] = dz
        dnw_ref[...] += dnw
        ds_ref[...] = dst

    return pl.pallas_call(
        body, name="gdn_scan_bwd", grid=(nc,), in_specs=ins + [s_spec, o_spec], out_specs=ins,
        out_shape=[SDS(t.shape, F32) for t in (q_dec, w, u, qk, k_dec, gc, z, nw)],
        scratch_shapes=[pltpu.VMEM((GDN_HEADS, GDN_DK, GDN_DV), F32)],
        compiler_params=_params(("arbitrary",), VMEM_MID),
    )(q_dec, w, u, qk, k_dec, gc, z, nw, s_in, do)


@jax.custom_vjp
def gdn_scan(q_dec, w, u, qk, k_dec, gc, z, nw):
    return _gdn_scan_fwd_call(q_dec, w, u, qk, k_dec, gc, z, nw)[0]


def _gdn_scan_f(*args):
    o, s_in = _gdn_scan_fwd_call(*args)
    return o, (*args, s_in)


def _gdn_scan_b(res, do):
    return tuple(_gdn_scan_bwd_call(*res, do))


gdn_scan.defvjp(_gdn_scan_f, _gdn_scan_b)


def _silu(x):
    return x * jax.nn.sigmoid(x)


def _gelu_tanh(x):
    return 0.5 * x * (1.0 + jnp.tanh(math.sqrt(2.0 / math.pi) * (x + 0.044715 * (x * x * x))))


def _f_lnmod(x, nw, sc, sh, bsc, bsh):
    xn = x * lax.rsqrt(jnp.mean(x * x, axis=-1, keepdims=True) + NORM_EPS) * nw
    return (xn * (1.0 + (sc + bsc)) + (sh + bsh),)


def _f_s5_act(ys, u, d):
    return (_gelu_tanh(ys + d * u),)


def _f_s5_gate(y2, t, z):
    return (y2 * jax.nn.sigmoid(t) * _silu(z),)


def _f_res(x, y, gate, bgate):
    return (x + (gate + bgate) * y,)


def _heads(x, width, fn):
    return jnp.concatenate([fn(x[:, i * width:(i + 1) * width]) for i in range(x.shape[1] // width)], axis=1)


def _l2n(x):
    return x * lax.rsqrt(jnp.sum(x * x, axis=-1, keepdims=True) + NORM_EPS)


def _f_qnorm(x):
    return (_heads(_silu(x), GDN_DK, _l2n) * (GDN_DK ** -0.5),)


def _f_knorm(x):
    return (_heads(_silu(x), GDN_DK, _l2n),)


def _f_vact(x):
    return (_silu(x),)


def _f_betag(ba, alog, dtb):
    col = lax.broadcasted_iota(jnp.int32, ba.shape, 1)
    t = ba + dtb
    softplus = jnp.maximum(t, 0.0) + jnp.log1p(jnp.exp(-jnp.abs(t)))
    g = -jnp.exp(alog) * softplus
    return (jnp.where(col < GDN_HEADS, jax.nn.sigmoid(ba), jnp.where(col < 2 * GDN_HEADS, g, 0.0)),)


def _f_gdn_post(o, z, nw):
    on = _heads(o, GDN_DV, lambda t: t * lax.rsqrt(jnp.mean(t * t, axis=-1, keepdims=True) + NORM_EPS))
    return (on * nw * _silu(z),)


def _f_loss(x, tgt, fw):
    y = x * lax.rsqrt(jnp.mean(x * x, axis=-1, keepdims=True) + NORM_EPS) * fw
    err = y - tgt
    return (0.5 * jnp.mean(err * err, axis=-1, keepdims=True),)


def _ada_mod_call(c_all, ada_w):
    n = ada_w.shape[2]

    def body(c_ref, w_ref, o_ref):
        ca = _silu(c_ref[...])
        for l in range(ada_w.shape[0]):
            o_ref[l] = _bdot(ca, w_ref[l])

    return pl.pallas_call(body, name="ada_mod", out_shape=SDS((ada_w.shape[0], N_DEV, n), F32),
                          compiler_params=_params(None, VMEM_MID))(c_all, ada_w)


def _ada_grad_call(c_all, dmod):
    nl, _, n = dmod.shape

    def body(c_ref, d_ref, o_ref):
        ca = _silu(c_ref[...])
        for l in range(nl):
            o_ref[l] = _hdot(ca, d_ref[l], TN)

    return pl.pallas_call(body, name="ada_grad", out_shape=SDS((nl, c_all.shape[1], n), F32),
                          compiler_params=_params(None, VMEM_MID))(c_all, dmod)


ADAM_ROWS = 512


def _adamw(g, w, m, v):
    m2 = ADAM_B1 * m + (1.0 - ADAM_B1) * g
    v2 = ADAM_B2 * v + (1.0 - ADAM_B2) * (g * g)
    m_hat = m2 / (1.0 - ADAM_B1 ** ADAM_STEP)
    v_hat = v2 / (1.0 - ADAM_B2 ** ADAM_STEP)
    return g, -ADAM_LR * (m_hat / (jnp.sqrt(v_hat) + ADAM_EPS) + ADAM_WD * w), m2, v2


def _adam_call(gs, w, m, v, name, rows=None):
    n, r, cols = gs.shape
    rows = rows or ADAM_ROWS

    def body(g_ref, w_ref, m_ref, v_ref, go_ref, d_ref, mo_ref, vo_ref):
        g = g_ref[0].astype(F32)
        for s in range(1, n):
            g = g + g_ref[s].astype(F32)
        for o_ref, val in zip((go_ref, d_ref, mo_ref, vo_ref), _adamw(g, w_ref[...], m_ref[...], v_ref[...])):
            o_ref[...] = val

    blk = pl.BlockSpec((rows, cols), lambda i: (i, 0))
    return pl.pallas_call(
        body, name=name, grid=(r // rows,),
        in_specs=[pl.BlockSpec((n, rows, cols), lambda i: (0, i, 0)), blk, blk, blk],
        out_specs=[blk, blk, blk, blk], out_shape=[SDS((r, cols), F32)] * 4,
        compiler_params=_params(("parallel",), VMEM_MID),
    )(gs, w, m, v)


def _sum_call(gs, name, rows):
    n, r, _ = gs.shape

    def body(g_ref, o_ref):
        g = g_ref[0].astype(F32)
        for s in range(1, n):
            g = g + g_ref[s].astype(F32)
        o_ref[...] = g

    return pl.pallas_call(
        body, name=name, grid=(r // rows,),
        in_specs=[pl.BlockSpec((n, rows, LANES), lambda i: (0, i, 0))],
        out_specs=pl.BlockSpec((rows, LANES), lambda i: (i, 0)), out_shape=SDS((r, LANES), F32),
        compiler_params=_params(("parallel",), VMEM_MID),
    )(gs)


def _allgather_call(x_shard, name, in_hbm):
    m_per, n = x_shard.shape

    def body(x_ref, out_ref, send_sems, recv_sems, local_sem):
        x, y, c = lax.axis_index("x"), lax.axis_index("y"), lax.axis_index("c")
        me, sibling = (x, y, c), (x, y, 1 - c)
        chips = [(1 - x, y), (x, 1 - y), (1 - x, 1 - y)]

        def rows(px, py, pc):
            return out_ref.at[pl.ds((4 * px + 2 * py + pc) * m_per, m_per), :]

        def copy(k, block, to, src=None):
            return pltpu.make_async_remote_copy(
                src_ref=rows(*block) if src is None else src, dst_ref=rows(*block),
                send_sem=send_sems.at[k], recv_sem=recv_sems.at[k], device_id=to, device_id_type=pl.DeviceIdType.MESH)

        mine = pltpu.make_async_copy(x_ref, rows(*me), local_sem)
        mine.start()
        first = [copy(0, me, sibling, src=x_ref)]
        first += [copy(1 + j, me, (*chip, c), src=x_ref) for j, chip in enumerate(chips)]
        for cp in first:
            cp.start()
        passed = [copy(4 + j, (*chip, c), sibling) for j, chip in enumerate(chips)]
        for j, chip in enumerate(chips):
            copy(1 + j, (*chip, c), me).wait_recv()
            passed[j].start()
        copy(0, sibling, me).wait_recv()
        for j, chip in enumerate(chips):
            copy(4 + j, (*chip, 1 - c), me).wait_recv()
        for cp in first + passed:
            cp.wait_send()
        mine.wait()

    space = pl.ANY if in_hbm else pltpu.VMEM
    return pl.pallas_call(
        body, name=name, out_shape=SDS((N_DEV * m_per, n), x_shard.dtype),
        in_specs=[pl.BlockSpec(memory_space=space)], out_specs=pl.BlockSpec(memory_space=space),
        scratch_shapes=[pltpu.SemaphoreType.DMA((7,)), pltpu.SemaphoreType.DMA((7,)), pltpu.SemaphoreType.DMA],
        compiler_params=_params(None, None if in_hbm else VMEM_BIG),
    )(x_shard)


def _gather_weights_call(shards, name):
    nw = len(shards)

    def body(*refs):
        x_refs, out_refs = refs[:nw], refs[nw:2 * nw]
        send_sems, recv_sems, local_sems = refs[2 * nw:]
        x, y, c = lax.axis_index("x"), lax.axis_index("y"), lax.axis_index("c")
        me, sibling = (x, y, c), (x, y, 1 - c)
        chips = [(1 - x, y), (x, 1 - y), (1 - x, 1 - y)]

        def slot(w, px, py, pc):
            return out_refs[w].at[4 * px + 2 * py + pc]

        def copy(w, k, block, to, src=None):
            dst = slot(w, *block)
            return pltpu.make_async_remote_copy(
                src_ref=dst if src is None else src, dst_ref=dst, send_sem=send_sems.at[7 * w + k],
                recv_sem=recv_sems.at[7 * w + k], device_id=to, device_id_type=pl.DeviceIdType.MESH)

        mines = [pltpu.make_async_copy(x_refs[w], slot(w, *me), local_sems.at[w]) for w in range(nw)]
        for cp in mines:
            cp.start()
        first = [copy(w, 0, me, sibling, src=x_refs[w]) for w in range(nw)]
        first += [copy(w, 1 + j, me, (*chip, c), src=x_refs[w]) for w in range(nw) for j, chip in enumerate(chips)]
        for cp in first:
            cp.start()
        passed = []
        for w in range(nw):
            for j, chip in enumerate(chips):
                copy(w, 1 + j, (*chip, c), me).wait_recv()
                fwd = copy(w, 4 + j, (*chip, c), sibling)
                fwd.start()
                passed.append(fwd)
        for w in range(nw):
            copy(w, 0, sibling, me).wait_recv()
            for j, chip in enumerate(chips):
                copy(w, 4 + j, (*chip, 1 - c), me).wait_recv()
        for cp in first + passed:
            cp.wait_send()
        for cp in mines:
            cp.wait()

    hbm = pl.BlockSpec(memory_space=pl.ANY)
    return pl.pallas_call(
        body, name=name, out_shape=[SDS((N_DEV,) + s.shape, s.dtype) for s in shards],
        in_specs=[hbm] * nw, out_specs=[hbm] * nw,
        scratch_shapes=[pltpu.SemaphoreType.DMA((7 * nw,)), pltpu.SemaphoreType.DMA((7 * nw,)), pltpu.SemaphoreType.DMA((nw,))],
    )(*shards)


def _pair_exchange_call(grads, name):
    nw = len(grads)

    def body(*refs):
        g_refs, got_refs = refs[:nw], refs[nw:2 * nw]
        send_sems, recv_sems = refs[2 * nw:]
        x, y, c = lax.axis_index("x"), lax.axis_index("y"), lax.axis_index("c")
        copies = []
        for w in range(nw):
            for j in range(4):
                give = pltpu.make_async_remote_copy(
                    src_ref=g_refs[w].at[2 * j + 1 - c], dst_ref=got_refs[w].at[j], send_sem=send_sems.at[4 * w + j],
                    recv_sem=recv_sems.at[4 * w + j], device_id=(x, y, 1 - c), device_id_type=pl.DeviceIdType.MESH)
                give.start()
                copies.append(give)
        for cp in copies:
            cp.wait()

    hbm = pl.BlockSpec(memory_space=pl.ANY)
    return pl.pallas_call(
        body, name=name, out_shape=[SDS((4,) + g.shape[1:], g.dtype) for g in grads], in_specs=[hbm] * nw, out_specs=[hbm] * nw,
        scratch_shapes=[pltpu.SemaphoreType.DMA((4 * nw,)), pltpu.SemaphoreType.DMA((4 * nw,))],
    )(*grads)


def _chip_exchange_call(parts, name):
    nw = len(parts)

    def body(*refs):
        p_refs, out_refs = refs[:nw], refs[nw:2 * nw]
        send_sems, recv_sems = refs[2 * nw:]
        x, y, c = lax.axis_index("x"), lax.axis_index("y"), lax.axis_index("c")
        chips = [(1 - x, y), (x, 1 - y), (1 - x, 1 - y)]
        copies = []
        for w in range(nw):
            for j, (px, py) in enumerate(chips):
                give = pltpu.make_async_remote_copy(
                    src_ref=p_refs[w].at[2 * px + py], dst_ref=out_refs[w].at[j], send_sem=send_sems.at[3 * w + j],
                    recv_sem=recv_sems.at[3 * w + j], device_id=(px, py, c), device_id_type=pl.DeviceIdType.MESH)
                give.start()
                copies.append(give)
        for cp in copies:
            cp.wait()

    hbm = pl.BlockSpec(memory_space=pl.ANY)
    return pl.pallas_call(
        body, name=name, out_shape=[SDS((3,) + p.shape[1:], p.dtype) for p in parts], in_specs=[hbm] * nw, out_specs=[hbm] * nw,
        scratch_shapes=[pltpu.SemaphoreType.DMA((3 * nw,)), pltpu.SemaphoreType.DMA((3 * nw,))],
    )(*parts)


_HBM = pl.BlockSpec(memory_space=pltpu.HBM)
_SEM = pl.BlockSpec(memory_space=pltpu.SEMAPHORE)
_DATAFLOW = pltpu.SideEffectType.DATAFLOW_SIDE_EFFECTING


def _spread_start_call(srcs, per_peer, name, after):
    nw = len(srcs)
    lands = [lax.empty((N_DEV,) + (s.shape[1:] if per_peer else s.shape), s.dtype) for s in srcs]

    def body(*refs):
        src_refs, land_refs = refs[:nw], refs[nw:2 * nw]
        send_sems, recv_sems, token = refs[2 * nw + 1], refs[2 * nw + 2], refs[-1]
        x, y, c = lax.axis_index("x"), lax.axis_index("y"), lax.axis_index("c")
        me = 4 * x + 2 * y + c
        for w in range(nw):
            for k in range(1, N_DEV):
                px = 1 - x if k & 4 else x
                py = 1 - y if k & 2 else y
                pc = 1 - c if k & 1 else c
                src = src_refs[w].at[4 * px + 2 * py + pc] if per_peer else src_refs[w]
                pltpu.make_async_remote_copy(
                    src_ref=src, dst_ref=land_refs[w].at[me], send_sem=send_sems.at[w], recv_sem=recv_sems.at[w],
                    device_id=(px, py, pc), device_id_type=pl.DeviceIdType.MESH).start()
        token[...] = jnp.zeros_like(token)

    hbm = lambda a: pltpu.with_memory_space_constraint(a, pltpu.HBM)
    res = pl.pallas_call(
        body, name=name,
        out_shape=(pltpu.SemaphoreType.DMA((nw,)), pltpu.SemaphoreType.DMA((nw,)))
        + tuple(pltpu.HBM(s.shape, s.dtype) for s in srcs) + tuple(pltpu.HBM(l.shape, l.dtype) for l in lands)
        + (SDS((SUBLANES, LANES), F32),),
        in_specs=[_HBM] * (2 * nw) + [pl.BlockSpec(memory_space=pl.ANY)],
        out_specs=(_SEM, _SEM) + (_HBM,) * (2 * nw) + (pl.BlockSpec(memory_space=pltpu.VMEM),),
        input_output_aliases={i: i + 2 for i in range(2 * nw)},
        compiler_params=pltpu.CompilerParams(has_side_effects=_DATAFLOW),
    )(*[hbm(s) for s in srcs], *[hbm(l) for l in lands], after)
    return res[0], res[1], res[2:2 + nw], res[2 + nw:2 + 2 * nw], res[-1]


def _spread_wait_call(send_sems, recv_sems, srcs, lands, after, name):
    nw = len(lands)

    def body(*refs):
        land_refs = refs[nw:2 * nw]
        s_sems, r_sems = refs[2 * nw], refs[2 * nw + 1]
        x, y, c = lax.axis_index("x"), lax.axis_index("y"), lax.axis_index("c")
        for w in range(nw):
            seven = land_refs[w].at[pl.ds(0, N_DEV - 1)]
            all_seven = pltpu.make_async_remote_copy(
                src_ref=seven, dst_ref=seven, send_sem=s_sems.at[w], recv_sem=r_sems.at[w],
                device_id=(x, y, c), device_id_type=pl.DeviceIdType.MESH)
            all_seven.wait_send()
            all_seven.wait_recv()

    res = pl.pallas_call(
        body, name=name,
        out_shape=tuple(pltpu.HBM(s.shape, s.dtype) for s in srcs) + tuple(pltpu.HBM(l.shape, l.dtype) for l in lands),
        in_specs=[_HBM] * (2 * nw) + [_SEM, _SEM, pl.BlockSpec(memory_space=pl.ANY)], out_specs=(_HBM,) * (2 * nw),
        input_output_aliases={i: i for i in range(2 * nw)},
        compiler_params=pltpu.CompilerParams(has_side_effects=_DATAFLOW),
    )(*srcs, *lands, send_sems, recv_sems, after)
    return res[:nw], res[nw:]


def _pair_sum_call(g, got, core, name):
    _, k, n = got.shape
    tr = _tile(k, 256)

    def body(c_ref, g_ref, got_ref, o_ref):
        o_ref[...] = (g_ref[...] + got_ref[...]).astype(o_ref.dtype)

    spec = pltpu.PrefetchScalarGridSpec(
        num_scalar_prefetch=1, grid=(4, k // tr),
        in_specs=[pl.BlockSpec((1, tr, n), lambda j, i, c: (2 * j + c[0], i, 0)), pl.BlockSpec((1, tr, n), lambda j, i, c: (j, i, 0))],
        out_specs=pl.BlockSpec((1, tr, n), lambda j, i, c: (j, i, 0)))
    return pl.pallas_call(body, name=name, grid_spec=spec, out_shape=SDS(got.shape, BF16),
                          compiler_params=_params(("parallel", "parallel"), VMEM_MID))(core, g, got)


def _adam_own_call(pair, chip, recv, w, m, v, name, rows):
    _, r, cols = recv.shape

    def body(chip_ref, p_ref, g_ref, w_ref, m_ref, v_ref, go_ref, d_ref, mo_ref, vo_ref):
        g = ((p_ref[0].astype(F32) + g_ref[0].astype(F32)) + g_ref[1].astype(F32)) + g_ref[2].astype(F32)
        for o_ref, val in zip((go_ref, d_ref, mo_ref, vo_ref), _adamw(g, w_ref[...], m_ref[...], v_ref[...])):
            o_ref[...] = val

    blk = pl.BlockSpec((rows, cols), lambda i, s: (i, 0))
    spec = pltpu.PrefetchScalarGridSpec(
        num_scalar_prefetch=1, grid=(r // rows,),
        in_specs=[pl.BlockSpec((1, rows, cols), lambda i, s: (s[0], i, 0)), pl.BlockSpec((3, rows, cols), lambda i, s: (0, i, 0)),
                  blk, blk, blk],
        out_specs=[blk, blk, blk, blk])
    return pl.pallas_call(body, name=name, grid_spec=spec, out_shape=[SDS((r, cols), F32)] * 4,
                          compiler_params=_params(("parallel",), VMEM_MID))(chip, pair, recv, w, m, v)


def _join_cols_call(w8, name):
    _, k, n = w8.shape
    tk = _tile(k, 256)

    def body(w_ref, o_ref):
        for s in range(N_DEV):
            o_ref[:, n * s:n * (s + 1)] = w_ref[s]

    return pl.pallas_call(body, name=name, grid=(k // tk,), in_specs=[pl.BlockSpec((N_DEV, tk, n), lambda i: (0, i, 0))],
                          out_specs=pl.BlockSpec((tk, N_DEV * n), lambda i: (i, 0)), out_shape=SDS((k, N_DEV * n), w8.dtype),
                          compiler_params=_params(("parallel",), VMEM_MID))(w8)


def _split_cols_call(g, name, dtype):
    k, n8 = g.shape
    n = n8 // N_DEV
    tk = _tile(k, 256)

    def body(g_ref, o_ref):
        for s in range(N_DEV):
            o_ref[s] = g_ref[:, n * s:n * (s + 1)].astype(dtype)

    return pl.pallas_call(body, name=name, grid=(k // tk,), in_specs=[pl.BlockSpec((tk, n8), lambda i: (i, 0))],
                          out_specs=pl.BlockSpec((N_DEV, tk, n), lambda i: (0, i, 0)), out_shape=SDS((N_DEV, k, n), dtype),
                          compiler_params=_params(("parallel",), VMEM_MID))(g)


def _pack(parts, rows_multiple):
    flat = jnp.concatenate([p.reshape(-1) for p in parts])
    unit = rows_multiple * LANES
    padded = -(-flat.shape[0] // unit) * unit
    flat = jnp.concatenate([flat, jnp.zeros((padded - flat.shape[0],), F32)])
    return flat.reshape(-1, LANES)


def _unpack(buf, shapes):
    flat = buf.reshape(-1)
    out, off = [], 0
    for s in shapes:
        n = math.prod(s)
        out.append(flat[off:off + n].reshape(s))
        off += n
    return out


def _row_tile(L):
    return 256 if L % 256 == 0 else L


def _layer0_mix(diff, const):
    x, mod, norm_w, lam_re, lam_im, log_dt, b_re, b_im, c_re, c_im, s5_d, *slots = diff
    ada_b, weights = const
    L = x.shape[0]
    tm = _row_tile(L)
    mods = mod.reshape(2, 1, D_MODEL)
    biases = ada_b.reshape(2, 1, D_MODEL)
    op_ln0 = make_rowwise(_f_lnmod, "ln0", tm, 1, 5, pass_first=True)
    h, x = op_ln0((x,), (norm_w.reshape(1, D_MODEL), mods[1], mods[0], biases[1], biases[0]))
    u, z = make_proj("s5_in")(h, tuple(weights), tuple(slots))
    blocks = _s5_block_params(lam_re, lam_im, log_dt, b_re, b_im, c_re, c_im)
    y2 = make_s5_core(min(S5_TL, L))(u, *blocks, s5_d.reshape(1, D_INNER))
    return x, y2, z


def _layer0_out(diff, weights):
    y2, z, *slots = diff
    tm = _row_tile(y2.shape[0])
    t, y2 = make_mm("s5_glu", pass_input=True)(y2, weights[0], slots[0])
    (y4,) = make_rowwise(_f_s5_gate, "s5_gate", tm, 3, 0)((y2, t, z), ())
    return make_mm("s5_out")(y4, weights[1], slots[1])


def _f_res_lnmod(x, o, gate, bgate, nw, sc, sh, bsc, bsh):
    (x1,) = _f_res(x, o, gate, bgate)
    return _f_lnmod(x1, nw, sc, sh, bsc, bsh) + (x1,)


def _f_res_loss(x, y, tgt, gate, bgate, fw):
    return _f_loss(_f_res(x, y, gate, bgate)[0], tgt, fw)


def _layer1_loss(diff, const):
    x, o, gate0, mod, norm_w, conv_w, a_log, dt_bias, gdn_nw, final_nw, *slots = diff
    tgt, bgate0, ada_b, weights = const
    L = x.shape[0]
    tm = _row_tile(L)
    mods = mod.reshape(3, 1, D_MODEL)
    biases = ada_b.reshape(3, 1, D_MODEL)
    h, x1 = make_rowwise(_f_res_lnmod, "res0_ln1", tm, 2, 7)(
        (x, o), (gate0.reshape(1, D_MODEL), bgate0.reshape(1, D_MODEL), norm_w.reshape(1, D_MODEL), mods[1], mods[0], biases[1], biases[0]))
    q0, k0, v0, gz, ba = make_proj("gdn_in")(h, tuple(weights[0:5]), tuple(slots[0:5]))
    cw = jnp.concatenate([conv_w, jnp.zeros((SUBLANES - GDN_CONV, GDN_CONV_CH), F32)], axis=0)
    q = make_conv_act(lambda t: _l2n(_silu(t)) * (GDN_DK ** -0.5), "gdn_conv_q")(q0, cw[:, :GDN_QK])
    k = make_conv_act(lambda t: _l2n(_silu(t)), "gdn_conv_k")(k0, cw[:, GDN_QK:2 * GDN_QK])
    v = make_conv_act(_silu, "gdn_conv_v")(v0, cw[:, 2 * GDN_QK:])
    pad = jnp.zeros((LANES - 2 * GDN_HEADS,), F32)
    alog_row = jnp.concatenate([jnp.zeros((GDN_HEADS,), F32), a_log, pad]).reshape(1, LANES)
    dtb_row = jnp.concatenate([jnp.zeros((GDN_HEADS,), F32), dt_bias, pad]).reshape(1, LANES)
    (bg,) = make_rowwise(_f_betag, "gdn_bg", tm, 1, 2)((ba,), (alog_row, dtb_row))
    nw_row = jnp.tile(gdn_nw, GDN_HEADS).reshape(1, D_INNER)
    on = gdn_scan(*gdn_prep(q, k, v, bg), gz, nw_row)
    y = make_mm("gdn_out")(on, weights[5], slots[5])
    (lt,) = make_rowwise(_f_res_loss, "res1_loss", tm, 3, 3)((x1, y, tgt), (mods[2], biases[2], final_nw.reshape(1, D_MODEL)))
    return jnp.sum(lt)


VEC_NAMES = ("ada_b", "norm_w", "s5_lambda_re", "s5_lambda_im", "s5_log_dt", "s5_d", "gdn_a_log", "gdn_dt_bias", "final_norm_w")
MAT_NAMES = ("s5_b_re", "s5_b_im", "s5_c_re", "s5_c_im")
S5_BIG = ("s5_w_in", "s5_w_glu", "s5_w_out")
GDN_BIG = ("gdn_w_in", "gdn_w_out")
BIG_NAMES = S5_BIG + GDN_BIG
WEIGHT_ORDER = ("ada_w", "ada_b", "norm_w", "s5_w_in", "s5_lambda_re", "s5_lambda_im", "s5_log_dt", "s5_b_re", "s5_b_im",
                "s5_c_re", "s5_c_im", "s5_d", "s5_w_glu", "s5_w_out", "gdn_w_in", "gdn_conv_w", "gdn_a_log", "gdn_dt_bias",
                "gdn_norm_w", "gdn_w_out", "final_norm_w")


def _step(x, c, W, M, V, tgt):
    L = x.shape[1]
    ix, iy, ic = lax.axis_index("x"), lax.axis_index("y"), lax.axis_index("c")
    me = 4 * ix + 2 * iy + ic
    n_ada = W["ada_w"].shape[2]
    n_conv = W["gdn_conv_w"].shape[2]
    n_gnw = W["gdn_norm_w"].shape[1]

    g1 = _allgather_call(_pack([c, W["gdn_conv_w"], W["gdn_norm_w"]], SUBLANES), "gather_small_in", False)
    g1 = g1.reshape(N_DEV, -1)
    c_all = g1[:, :D_MODEL]
    conv_w = g1[:, D_MODEL:D_MODEL + GDN_CONV * n_conv].reshape(N_DEV, GDN_CONV, n_conv).transpose(1, 0, 2).reshape(GDN_CONV, -1)
    gdn_nw = g1[:, D_MODEL + GDN_CONV * n_conv:D_MODEL + GDN_CONV * n_conv + n_gnw].reshape(-1)
    mod_part = _ada_mod_call(c_all, W["ada_w"])
    g2 = _allgather_call(_pack([mod_part], SUBLANES), "gather_mod", False).reshape(N_DEV, -1)
    mod_all = g2[:, :2 * N_DEV * n_ada].reshape(N_DEV, 2, N_DEV, n_ada)
    mod_raw = lax.dynamic_index_in_dim(mod_all, me, axis=2, keepdims=False)
    mod_raw = mod_raw.transpose(1, 0, 2).reshape(2, 3 * D_MODEL)

    shard = lambda n: W[n][0].astype(BF16)
    (w_in5_parts,) = _gather_weights_call([shard("s5_w_in")], "gather_s5_w_in")
    late = _spread_start_call([shard("s5_w_glu"), shard("s5_w_out")], False, "gather_s5_late_start", w_in5_parts)
    g_send, g_recv, g_srcs, g_lands, g_token = _spread_start_call([shard(n) for n in GDN_BIG], False, "gather_gdn_start", late[4])
    w_in5 = _join_cols_call(w_in5_parts, "join_s5_w_in")
    slot = lambda *s: jnp.zeros(s, F32)
    two = 2 * D_MODEL
    diff_mix = (x[0], mod_raw[0, :two] + g_token[0, 0], W["norm_w"][0], W["s5_lambda_re"][0], W["s5_lambda_im"][0], W["s5_log_dt"][0],
                W["s5_b_re"][0], W["s5_b_im"][0], W["s5_c_re"][0], W["s5_c_im"][0], W["s5_d"][0],
                slot(D_MODEL, D_INNER), slot(D_MODEL, D_INNER))

    (xp, y2, z5), vjp_mix = jax.vjp(lambda d: _layer0_mix(d, (W["ada_b"][0, :two], (w_in5[:, :D_INNER], w_in5[:, D_INNER:]))), diff_mix)
    l_srcs, l_lands = _spread_wait_call(late[0], late[1], late[2], late[3], y2, "gather_s5_late_wait")
    w_glu, w_o5 = [lax.dynamic_update_slice(land, src[None], (me, 0, 0)).reshape(-1, src.shape[1]) for land, src in zip(l_lands, l_srcs)]
    diff_out = (y2, z5, slot(D_INNER, D_INNER), slot(D_INNER, D_MODEL))
    o5, vjp_out = jax.vjp(lambda d: _layer0_out(d, (w_glu, w_o5)), diff_out)
    g_srcs, g_lands = _spread_wait_call(g_send, g_recv, g_srcs, g_lands, o5, "gather_gdn_wait")
    gdn_full = [lax.dynamic_update_slice(land, src[None], (me, 0, 0)) for land, src in zip(g_lands, g_srcs)]
    w_ing = _join_cols_call(gdn_full[0], "join_gdn_w_in")
    w_ba = jnp.concatenate([w_ing[:, GDN_CONV_CH + D_INNER:], jnp.zeros((D_MODEL, LANES - 2 * GDN_HEADS), BF16)], axis=1)
    weights1 = (w_ing[:, :GDN_QK], w_ing[:, GDN_QK:2 * GDN_QK], w_ing[:, 2 * GDN_QK:GDN_CONV_CH],
                w_ing[:, GDN_CONV_CH:GDN_CONV_CH + D_INNER], w_ba, gdn_full[1].reshape(D_INNER, D_MODEL))
    slots1 = tuple(jnp.zeros(w.shape, F32) for w in weights1)
    diff1 = (xp, o5, mod_raw[0, two:], mod_raw[1], W["norm_w"][1], conv_w, W["gdn_a_log"][0], W["gdn_dt_bias"][0], gdn_nw,
             W["final_norm_w"], *slots1)
    loss_local, vjp1 = jax.vjp(lambda d: _layer1_loss(d, (tgt[0], W["ada_b"][0, two:], W["ada_b"][1], weights1)), diff1)
    ((dxp, do5, dmod_gate, dmod1, d_norm_w1, d_conv, d_alog, d_dtb, d_gnw, d_fnw, d_wq, d_wk, d_wv, d_wgz, d_wba, d_wog),) = vjp1(
        jnp.ones((), F32))
    loss = lax.psum(loss_local, MESH_AXES)

    rows = lambda d: d.reshape(N_DEV, d.shape[0] // N_DEV, d.shape[1])
    d_ing = _split_cols_call(jnp.concatenate([d_wq, d_wk, d_wv, d_wgz, d_wba[:, :2 * GDN_HEADS]], axis=1), "split_gdn_w_in", BF16)
    s_send, s_recv, s_srcs, s_lands, s_token = _spread_start_call([d_ing, rows(d_wog).astype(BF16)], True, "scatter_gdn_start", dxp)
    ((dy2, dz5, d_wglu, d_wo5),) = vjp_out(do5.at[0, 0].add(s_token[0, 0]))
    t_send, t_recv, t_srcs, t_lands, t_token = _spread_start_call(
        [rows(d_wglu).astype(BF16), rows(d_wo5).astype(BF16)], True, "scatter_s5_late_start", dy2)
    ((dx, dmod_ss, d_norm_w0, d_lre, d_lim, d_logdt, d_bre, d_bim, d_cre, d_cim, d_s5d, d_wu, d_wz),) = vjp_mix(
        (dxp.at[0, 0].add(t_token[0, 0]), dy2, dz5))
    dmod = jnp.stack([jnp.concatenate([dmod_ss, dmod_gate]), dmod1])
    d_norm_w = jnp.stack([d_norm_w0, d_norm_w1])
    d_in5 = _split_cols_call(jnp.concatenate([d_wu, d_wz], axis=1), "split_s5_w_in", F32)
    (got,) = _pair_exchange_call([d_in5], "scatter_s5_pair")
    core = jnp.reshape(ic, (1,)).astype(jnp.int32)
    chip = jnp.reshape(2 * ix + iy, (1,)).astype(jnp.int32)
    pair = _pair_sum_call(d_in5, got, core, "pair_sum_s5_w_in")
    (recv,) = _chip_exchange_call([pair], "scatter_s5_chips")
    big = {"s5_w_in": _adam_own_call(pair, chip, recv, W["s5_w_in"][0], M["s5_w_in"][0], V["s5_w_in"][0], "adam_s5_w_in", 128)}
    t_srcs, t_lands = _spread_wait_call(t_send, t_recv, t_srcs, t_lands, dx, "scatter_s5_late_wait")
    s_srcs, s_lands = _spread_wait_call(s_send, s_recv, s_srcs, s_lands, t_lands[0], "scatter_gdn_wait")
    for land, src, n in zip(tuple(t_lands) + tuple(s_lands), tuple(t_srcs) + tuple(s_srcs), ("s5_w_glu", "s5_w_out") + GDN_BIG):
        mine = lax.dynamic_index_in_dim(src, me, 0, keepdims=True)
        parts = lax.dynamic_update_slice(land, mine, (me, 0, 0))
        big[n] = _adam_call(parts, W[n][0], M[n][0], V[n][0], "adam_" + n, rows=_tile(W[n].shape[1], 128))
    big = [[o[None] for o in big[n]] for n in BIG_NAMES]

    vec_parts = [dmod, d_norm_w, d_lre, d_lim, d_logdt, d_s5d, d_alog, d_dtb, d_fnw]
    tail_parts = [d_conv, d_gnw]
    mat_parts = [d_bre, d_bim, d_cre, d_cim]
    n_vec = sum(math.prod(p.shape) for p in vec_parts)
    sg_vec, sg_mat = _gather_weights_call(
        [_pack(vec_parts + tail_parts, ADAM_ROWS), _pack(mat_parts, SUBLANES).astype(BF16)], "gather_small_grads")
    tot_vec = _sum_call(sg_vec, "sum_vec_grads", ADAM_ROWS)
    tot_mat = _sum_call(sg_mat, "sum_mat_grads", ADAM_ROWS)
    g_conv, g_gnw = _unpack(tot_vec.reshape(-1)[n_vec:], [d_conv.shape, d_gnw.shape])
    g_conv_mine = lax.dynamic_slice_in_dim(g_conv, me * n_conv, n_conv, axis=1)
    g_gnw_mine = lax.dynamic_slice_in_dim(g_gnw, me * n_gnw, n_gnw, axis=0)
    vec_names = VEC_NAMES + ("gdn_conv_w", "gdn_norm_w")
    vec_g = _pack([tot_vec.reshape(-1)[:n_vec], g_conv_mine, g_gnw_mine], ADAM_ROWS)
    vec = _adam_call(vec_g[None], _pack([W[n] for n in vec_names], ADAM_ROWS), _pack([M[n] for n in vec_names], ADAM_ROWS),
                     _pack([V[n] for n in vec_names], ADAM_ROWS), "adam_vec")
    vec = [_unpack(b, [W[n].shape for n in vec_names]) for b in vec]
    mats = []
    for name, g_mat in zip(MAT_NAMES, _unpack(tot_mat, [p.shape for p in mat_parts])):
        two_d = (-1, W[name].shape[-1])
        outs = _adam_call(g_mat.reshape(two_d)[None], W[name].reshape(two_d), M[name].reshape(two_d), V[name].reshape(two_d),
                          "adam_" + name, rows=1024)
        mats.append([o.reshape(W[name].shape) for o in outs])

    dmod_all = sg_vec[:, :2 * 3 * D_MODEL // LANES].reshape(N_DEV, 2, N_DEV, n_ada // LANES, LANES)
    dmod_mine = lax.dynamic_index_in_dim(dmod_all, me, axis=2, keepdims=False).transpose(1, 0, 2, 3).reshape(2, N_DEV, n_ada)
    g_ada_w = _ada_grad_call(c_all, dmod_mine)
    ada = _adam_call(g_ada_w.reshape(1, -1, LANES), W["ada_w"].reshape(-1, LANES), M["ada_w"].reshape(-1, LANES),
                     V["ada_w"].reshape(-1, LANES), "adam_ada")
    ada = [a.reshape(W["ada_w"].shape) for a in ada]

    res = {}
    for i, n in enumerate(BIG_NAMES):
        res[n] = big[i]
    for i, n in enumerate(vec_names):
        res[n] = [b[i] for b in vec]
    for i, n in enumerate(MAT_NAMES):
        res[n] = mats[i]
    res["ada_w"] = ada
    outs = [loss, dx[None]]
    for j in range(4):
        outs += [res[n][j] for n in WEIGHT_ORDER]
    return tuple(outs)


def kernel(x, c, ada_w, ada_b, norm_w, s5_w_in, s5_lambda_re, s5_lambda_im, s5_log_dt, s5_b_re, s5_b_im, s5_c_re, s5_c_im, s5_d, s5_w_glu, s5_w_out, gdn_w_in, gdn_conv_w, gdn_a_log, gdn_dt_bias, gdn_norm_w, gdn_w_out, final_norm_w, loss_target, m_ada_w, m_ada_b, m_norm_w, m_s5_w_in, m_s5_lambda_re, m_s5_lambda_im, m_s5_log_dt, m_s5_b_re, m_s5_b_im, m_s5_c_re, m_s5_c_im, m_s5_d, m_s5_w_glu, m_s5_w_out, m_gdn_w_in, m_gdn_conv_w, m_gdn_a_log, m_gdn_dt_bias, m_gdn_norm_w, m_gdn_w_out, m_final_norm_w, v_ada_w, v_ada_b, v_norm_w, v_s5_w_in, v_s5_lambda_re, v_s5_lambda_im, v_s5_log_dt, v_s5_b_re, v_s5_b_im, v_s5_c_re, v_s5_c_im, v_s5_d, v_s5_w_glu, v_s5_w_out, v_gdn_w_in, v_gdn_conv_w, v_gdn_a_log, v_gdn_dt_bias, v_gdn_norm_w, v_gdn_w_out, v_final_norm_w):
    W = dict(ada_w=ada_w, ada_b=ada_b, norm_w=norm_w, s5_w_in=s5_w_in, s5_lambda_re=s5_lambda_re, s5_lambda_im=s5_lambda_im,
             s5_log_dt=s5_log_dt, s5_b_re=s5_b_re, s5_b_im=s5_b_im, s5_c_re=s5_c_re, s5_c_im=s5_c_im, s5_d=s5_d,
             s5_w_glu=s5_w_glu, s5_w_out=s5_w_out, gdn_w_in=gdn_w_in, gdn_conv_w=gdn_conv_w, gdn_a_log=gdn_a_log,
             gdn_dt_bias=gdn_dt_bias, gdn_norm_w=gdn_norm_w, gdn_w_out=gdn_w_out, final_norm_w=final_norm_w)
    M = dict(ada_w=m_ada_w, ada_b=m_ada_b, norm_w=m_norm_w, s5_w_in=m_s5_w_in, s5_lambda_re=m_s5_lambda_re,
             s5_lambda_im=m_s5_lambda_im, s5_log_dt=m_s5_log_dt, s5_b_re=m_s5_b_re, s5_b_im=m_s5_b_im, s5_c_re=m_s5_c_re,
             s5_c_im=m_s5_c_im, s5_d=m_s5_d, s5_w_glu=m_s5_w_glu, s5_w_out=m_s5_w_out, gdn_w_in=m_gdn_w_in,
             gdn_conv_w=m_gdn_conv_w, gdn_a_log=m_gdn_a_log, gdn_dt_bias=m_gdn_dt_bias, gdn_norm_w=m_gdn_norm_w,
             gdn_w_out=m_gdn_w_out, final_norm_w=m_final_norm_w)
    V = dict(ada_w=v_ada_w, ada_b=v_ada_b, norm_w=v_norm_w, s5_w_in=v_s5_w_in, s5_lambda_re=v_s5_lambda_re,
             s5_lambda_im=v_s5_lambda_im, s5_log_dt=v_s5_log_dt, s5_b_re=v_s5_b_re, s5_b_im=v_s5_b_im, s5_c_re=v_s5_c_re,
             s5_c_im=v_s5_c_im, s5_d=v_s5_d, s5_w_glu=v_s5_w_glu, s5_w_out=v_s5_w_out, gdn_w_in=v_gdn_w_in,
             gdn_conv_w=v_gdn_conv_w, gdn_a_log=v_gdn_a_log, gdn_dt_bias=v_gdn_dt_bias, gdn_norm_w=v_gdn_norm_w,
             gdn_w_out=v_gdn_w_out, final_norm_w=v_final_norm_w)
    return _step(x, c, W, M, V, loss_target)
```

```python
import functools
import math

import jax
import jax.numpy as jnp
from jax import lax
from jax.experimental import pallas as pl
from jax.experimental.pallas import tpu as pltpu

F32 = jnp.float32
BF16 = jnp.bfloat16
SDS = jax.ShapeDtypeStruct

D_MODEL = 1024
D_INNER = 2048
NORM_EPS = 1e-6
S5_GROUP = 16
S5_GROUPS = 128
S5_STATE = 64
GDN_HEADS = 8
GDN_DK = 128
GDN_DV = 256
GDN_CONV = 4
GDN_CHUNK = 64
GDN_QK = 1024
GDN_CONV_CH = 4096
GDN_PROJ = 6160
ADAM_LR = 0.001
ADAM_B1 = 0.9
ADAM_B2 = 0.999
ADAM_EPS = 1e-08
ADAM_WD = 0.01
ADAM_STEP = 10

N_DEV = 8
LANES = 128
SUBLANES = 8
VMEM_BIG = 56 << 20
VMEM_MID = 40 << 20
S5_GB = 8
S5_TL = 1024
MESH_AXES = ("x", "y", "c")


def _params(sem, vmem=None):
    return pltpu.CompilerParams(dimension_semantics=sem, vmem_limit_bytes=vmem)


def _bdot(a, b, dims=(((1,), (0,)), ((), ()))):
    return lax.dot_general(a.astype(BF16), b.astype(BF16), dims, preferred_element_type=F32)


def _hdot(a, b, dims=(((1,), (0,)), ((), ()))):
    return lax.dot_general(a, b, dims, preferred_element_type=F32, precision=lax.Precision.HIGHEST)


_BNN = (((2,), (1,)), ((0,), (0,)))
_BNT = (((2,), (2,)), ((0,), (0,)))
_BTN = (((1,), (1,)), ((0,), (0,)))


@jax.custom_vjp
def _unit_lower_inverse(a):
    c = a.shape[-1]
    ri = lax.broadcasted_iota(jnp.int32, a.shape, 1)
    ci = lax.broadcasted_iota(jnp.int32, a.shape, 2)
    n = -a
    t = (ri == ci).astype(F32) + n
    for _ in range(int(math.log2(c)) - 1):
        n = _hdot(n, n, _BNN)
        t = t + _hdot(t, n, _BNN)
    return t


def _unit_lower_inverse_fwd(a):
    t = _unit_lower_inverse(a)
    return t, t


def _unit_lower_inverse_bwd(t, g):
    return (-_hdot(_hdot(t, g, _BTN), t, _BNT),)


_unit_lower_inverse.defvjp(_unit_lower_inverse_fwd, _unit_lower_inverse_bwd)


NN = (((1,), (0,)), ((), ()))
NT = (((1,), (1,)), ((), ()))
TN = (((0,), (0,)), ((), ()))


def _tile(n, pref):
    for t in (pref, 512, 256, 128):
        if t <= n and n % t == 0:
            return t
    return n


def _matmul(a, b, mode, name, add=None):
    if mode == "nn":
        (m, k), (_, n) = a.shape, b.shape
    elif mode == "nt":
        (m, k), (n, _) = a.shape, b.shape
    else:
        (k, m), (_, n) = a.shape, b.shape
    tm, tn, tk = _tile(m, 512), _tile(n, 512), (k if k <= 2048 else _tile(k, 512))
    if mode == "tn":
        tm, tn, tk = _tile(m, 1024), _tile(n, 1024), _tile(k, 1024)
    nk = k // tk
    dims = {"nn": NN, "nt": NT, "tn": TN}[mode]

    def body(a_ref, b_ref, *rest):
        o_ref, acc_ref = rest[-2], rest[-1]
        part = _bdot(a_ref[...], b_ref[...], dims)
        if nk == 1:
            o_ref[...] = part if add is None else part + rest[0][...]
            return
        kk = pl.program_id(2)

        @pl.when(kk == 0)
        def _():
            acc_ref[...] = part if add is None else part + rest[0][...]

        @pl.when(kk > 0)
        def _():
            acc_ref[...] += part

        @pl.when(kk == nk - 1)
        def _():
            o_ref[...] = acc_ref[...]

    a_spec = pl.BlockSpec((tk, tm), lambda i, j, q: (q, i)) if mode == "tn" else pl.BlockSpec((tm, tk), lambda i, j, q: (i, q))
    b_spec = pl.BlockSpec((tn, tk), lambda i, j, q: (j, q)) if mode == "nt" else pl.BlockSpec((tk, tn), lambda i, j, q: (q, j))
    o_spec = pl.BlockSpec((tm, tn), lambda i, j, q: (i, j))
    return pl.pallas_call(
        body, name=name, grid=(m // tm, n // tn, nk),
        in_specs=[a_spec, b_spec] + ([] if add is None else [o_spec]), out_specs=o_spec,
        out_shape=SDS((m, n), F32), scratch_shapes=[pltpu.VMEM((tm, tn), F32)],
        compiler_params=_params(("parallel", "parallel", "arbitrary"), VMEM_MID),
    )(a, b, *([] if add is None else [add]))


def make_mm(name, pass_input=False):
    def primal(a, w):
        out = _matmul(a, w, "nn", name + "_fwd")
        return (out, a) if pass_input else out

    @jax.custom_vjp
    def mm(a, w, grad_slot):
        return primal(a, w)

    def fwd(a, w, grad_slot):
        return primal(a, w), (a, w)

    def bwd(res, g):
        a, w = res
        g, g_other = g if pass_input else (g, None)
        return _matmul(g, w, "nt", name + "_dx", add=g_other), jnp.zeros_like(w), _matmul(a, g, "tn", name + "_dw")

    mm.defvjp(fwd, bwd)
    return mm


PROJ_ROWS = 256


def _proj_fwd_call(a, ws, name):
    m, k = a.shape
    tm = _tile(m, PROJ_ROWS)
    nw = len(ws)

    def body(*refs):
        ab = refs[0][...].astype(BF16)
        for w_ref, o_ref in zip(refs[1:1 + nw], refs[1 + nw:]):
            o_ref[...] = lax.dot_general(ab, w_ref[...], NN, preferred_element_type=F32)

    return pl.pallas_call(
        body, name=name, grid=(m // tm,),
        in_specs=[pl.BlockSpec((tm, k), lambda i: (i, 0))] + [pl.BlockSpec(w.shape, lambda i: (0, 0)) for w in ws],
        out_specs=[pl.BlockSpec((tm, w.shape[1]), lambda i: (i, 0)) for w in ws],
        out_shape=[SDS((m, w.shape[1]), F32) for w in ws],
        compiler_params=_params(("parallel",), VMEM_BIG),
    )(a, *ws)


def _proj_dx_call(gs, ws, name):
    m = gs[0].shape[0]
    k = ws[0].shape[0]
    tm = _tile(m, PROJ_ROWS)
    nw = len(ws)

    def body(*refs):
        acc = None
        for g_ref, w_ref in zip(refs[:nw], refs[nw:2 * nw]):
            part = _bdot(g_ref[...], w_ref[...], NT)
            acc = part if acc is None else acc + part
        refs[2 * nw][...] = acc

    return pl.pallas_call(
        body, name=name, grid=(m // tm,),
        in_specs=[pl.BlockSpec((tm, g.shape[1]), lambda i: (i, 0)) for g in gs] + [pl.BlockSpec(w.shape, lambda i: (0, 0)) for w in ws],
        out_specs=pl.BlockSpec((tm, k), lambda i: (i, 0)), out_shape=SDS((m, k), F32),
        compiler_params=_params(("parallel",), VMEM_BIG),
    )(*gs, *ws)


def make_proj(name):
    @jax.custom_vjp
    def proj(a, ws, grad_slots):
        return tuple(_proj_fwd_call(a, ws, name + "_fwd"))

    def fwd(a, ws, grad_slots):
        return tuple(_proj_fwd_call(a, ws, name + "_fwd")), (a, ws)

    def bwd(res, gs):
        a, ws = res
        dws = tuple(_matmul(a, g, "tn", "%s_dw%d" % (name, i)) for i, g in enumerate(gs))
        return _proj_dx_call(tuple(gs), ws, name + "_dx"), tuple(jnp.zeros_like(w) for w in ws), dws

    proj.defvjp(fwd, bwd)
    return proj


def make_rowwise(f, name, tm, n_rows, n_params, vmem=VMEM_MID, pass_first=False):
    def specs_of(arrs, blocked):
        if blocked:
            return [pl.BlockSpec((tm, a.shape[1]), lambda i: (i, 0)) for a in arrs]
        return [pl.BlockSpec(a.shape, lambda i: (0, 0)) for a in arrs]

    def out_structs(rows, params):
        blk = [SDS((tm, r.shape[1]), r.dtype) for r in rows] + [SDS(p.shape, p.dtype) for p in params]
        return jax.eval_shape(f, *blk)

    def run_fwd(rows, params):
        L = rows[0].shape[0]
        outs = out_structs(rows, params)

        def body(*refs):
            ins = [r[...] for r in refs[:n_rows + n_params]]
            res = f(*ins)
            for o_ref, val in zip(refs[n_rows + n_params:], res):
                o_ref[...] = val

        return pl.pallas_call(
            body, name=name + "_fwd", grid=(L // tm,),
            in_specs=specs_of(rows, True) + specs_of(params, False),
            out_specs=[pl.BlockSpec((tm, o.shape[1]), lambda i: (i, 0)) for o in outs],
            out_shape=[SDS((L, o.shape[1]), o.dtype) for o in outs],
            compiler_params=_params(("parallel",), vmem),
        )(*rows, *params)

    def run_bwd(rows, params, gs):
        L = rows[0].shape[0]
        n_g = len(gs)

        def body(*refs):
            i = pl.program_id(0)
            ins = [r[...] for r in refs[:n_rows + n_params]]
            cts = tuple(r[...] for r in refs[n_rows + n_params:n_rows + n_params + n_g])
            outs = refs[n_rows + n_params + n_g:]
            _, vjp = jax.vjp(f, *ins)
            grads = vjp(cts[:-1] if pass_first else cts)
            if pass_first:
                grads = (grads[0] + cts[-1],) + tuple(grads[1:])
            for o_ref, val in zip(outs[:n_rows], grads[:n_rows]):
                o_ref[...] = val

            if n_params:
                @pl.when(i == 0)
                def _():
                    for o_ref in outs[n_rows:]:
                        o_ref[...] = jnp.zeros_like(o_ref)
                for o_ref, val in zip(outs[n_rows:], grads[n_rows:]):
                    o_ref[...] += val

        res = pl.pallas_call(
            body, name=name + "_bwd", grid=(L // tm,),
            in_specs=specs_of(rows, True) + specs_of(params, False) + specs_of(gs, True),
            out_specs=specs_of(rows, True) + specs_of(params, False),
            out_shape=[SDS(r.shape, r.dtype) for r in rows] + [SDS(p.shape, p.dtype) for p in params],
            compiler_params=_params(("arbitrary",), vmem),
        )(*rows, *params, *gs)
        return tuple(res[:n_rows]), tuple(res[n_rows:])

    def outputs(rows, params):
        outs = tuple(run_fwd(rows, params))
        return outs + (rows[0],) if pass_first else outs

    @jax.custom_vjp
    def op(rows, params):
        return outputs(rows, params)

    def fwd(rows, params):
        return outputs(rows, params), (rows, params)

    def bwd(res, gs):
        rows, params = res
        return run_bwd(rows, params, tuple(gs))

    op.defvjp(fwd, bwd)
    op.run_fwd, op.run_bwd = run_fwd, run_bwd
    return op


def make_residual(name, tm):
    full = make_rowwise(_f_res, name, tm, 2, 2)
    branch = make_rowwise(lambda y, gate, bgate: ((gate + bgate) * y,), name + "_branch", tm, 1, 2)

    @jax.custom_vjp
    def op(x, y, gate, bgate):
        return full.run_fwd((x, y), (gate, bgate))[0]

    def fwd(x, y, gate, bgate):
        return full.run_fwd((x, y), (gate, bgate))[0], (y, gate, bgate)

    def bwd(res, g):
        y, gate, bgate = res
        (dy,), (dgate, dbgate) = branch.run_bwd((y,), (gate, bgate), (g,))
        return g, dy, dgate, dbgate

    op.defvjp(fwd, bwd)
    return op


def _s5_scan_rows(xr_ref, xi_ref, ar, ai, x0r, x0i, tl, reverse=False):
    n = xr_ref.shape[1]
    T = SUBLANES
    row = lax.broadcasted_iota(jnp.int32, (T, n), 0)
    pr, pi = [ar], [ai]
    for _ in range(T - 1):
        pr, pi = pr + [pr[-1] * ar - pi[-1] * ai], pi + [pr[-1] * ai + pi[-1] * ar]
    levels = []
    for d in (1, 2, 4):
        mask = (row < T - d) if reverse else (row >= d)
        levels.append((T - d if reverse else d, jnp.where(mask, pr[d - 1], 0.0), jnp.where(mask, pi[d - 1], 0.0)))
    cr = jnp.zeros((T, n), F32)
    ci = jnp.zeros((T, n), F32)
    for r in range(T):
        k = (T - r) if reverse else (r + 1)
        cr = jnp.where(row == r, pr[k - 1], cr)
        ci = jnp.where(row == r, pi[k - 1], ci)
    nt = tl // T
    last = 0 if reverse else T - 1

    def step(t, carry):
        sr, si = carry
        base = pl.multiple_of((nt - 1 - t if reverse else t) * T, T)
        br = xr_ref[pl.ds(base, T), :]
        bi = xi_ref[pl.ds(base, T), :]
        for shift, mr, mi in levels:
            qr = pltpu.roll(br, shift, 0)
            qi = pltpu.roll(bi, shift, 0)
            br, bi = br + (mr * qr - mi * qi), bi + (mr * qi + mi * qr)
        xr = br + (cr * sr - ci * si)
        xi = bi + (cr * si + ci * sr)
        xr_ref[pl.ds(base, T), :] = xr
        xi_ref[pl.ds(base, T), :] = xi
        return xr[last:last + 1, :], xi[last:last + 1, :]
    return lax.fori_loop(0, nt, step, (x0r, x0i))


def _s5_fwd_call(u, bre, bim, cre, cim, a, d, tl):
    L, e = u.shape
    nb = e // LANES
    ns = bre.shape[2]
    nc = L // tl

    def body(u_ref, bre_ref, bim_ref, cre_ref, cim_ref, a_ref, d_ref, y_ref, xb_ref, sr_ref, si_ref, xr_ref, xi_ref, carry_ref):
        c = pl.program_id(1)

        @pl.when(c == 0)
        def _():
            carry_ref[...] = jnp.zeros_like(carry_ref)
        xb_ref[0, 0] = carry_ref[...]
        ub = u_ref[...]
        xr_ref[...] = _bdot(ub, bre_ref[0])
        xi_ref[...] = _bdot(ub, bim_ref[0])
        ar = a_ref[0, 0:1, :]
        ai = a_ref[0, 1:2, :]
        xr, xi = _s5_scan_rows(xr_ref, xi_ref, ar, ai, carry_ref[0:1, :], carry_ref[1:2, :], tl)
        carry_ref[0:1, :] = xr
        carry_ref[1:2, :] = xi
        sr = xr_ref[...].astype(BF16)
        si = xi_ref[...].astype(BF16)
        sr_ref[...] = sr
        si_ref[...] = si
        y_ref[...] = _f_s5_act(_bdot(sr, cre_ref[0]) - _bdot(si, cim_ref[0]), ub, d_ref[...])[0]

    return pl.pallas_call(
        body, name="s5_core_fwd", grid=(nb, nc),
        in_specs=[pl.BlockSpec((tl, LANES), lambda j, c: (c, j)),
                  pl.BlockSpec((1, LANES, ns), lambda j, c: (j, 0, 0)), pl.BlockSpec((1, LANES, ns), lambda j, c: (j, 0, 0)),
                  pl.BlockSpec((1, ns, LANES), lambda j, c: (j, 0, 0)), pl.BlockSpec((1, ns, LANES), lambda j, c: (j, 0, 0)),
                  pl.BlockSpec((1, SUBLANES, ns), lambda j, c: (j, 0, 0)), pl.BlockSpec((1, LANES), lambda j, c: (0, j))],
        out_specs=[pl.BlockSpec((tl, LANES), lambda j, c: (c, j)),
                   pl.BlockSpec((1, 1, SUBLANES, ns), lambda j, c: (j, c, 0, 0)),
                   pl.BlockSpec((tl, ns), lambda j, c: (c, j)), pl.BlockSpec((tl, ns), lambda j, c: (c, j))],
        out_shape=[SDS((L, e), F32), SDS((nb, nc, SUBLANES, ns), F32), SDS((L, nb * ns), BF16), SDS((L, nb * ns), BF16)],
        scratch_shapes=[pltpu.VMEM((tl, ns), F32), pltpu.VMEM((tl, ns), F32), pltpu.VMEM((SUBLANES, ns), F32)],
        compiler_params=_params(("arbitrary", "arbitrary"), VMEM_MID),
    )(u, bre, bim, cre, cim, a, d)


def _s5_bwd_call(u, dy2, bre, bim, cre, cim, a, d, xb, sr, si, tl):
    L, e = u.shape
    nb = e // LANES
    ns = bre.shape[2]
    nc = L // tl

    def body(u_ref, dy2_ref, bre_ref, bim_ref, cre_ref, cim_ref, a_ref, d_ref, xb_ref, sr_ref, si_ref,
             du_ref, dbre_ref, dbim_ref, dcre_ref, dcim_ref, da_ref, dd_ref,
             gr_ref, gi_ref, gcarry_ref):
        c = pl.program_id(1)

        @pl.when(c == 0)
        def _():
            gcarry_ref[...] = jnp.zeros_like(gcarry_ref)
            dbre_ref[...] = jnp.zeros_like(dbre_ref)
            dbim_ref[...] = jnp.zeros_like(dbim_ref)
            dcre_ref[...] = jnp.zeros_like(dcre_ref)
            dcim_ref[...] = jnp.zeros_like(dcim_ref)
            da_ref[...] = jnp.zeros_like(da_ref)
            dd_ref[...] = jnp.zeros_like(dd_ref)

        ub = u_ref[...]
        ys = _bdot(sr_ref[...], cre_ref[0]) - _bdot(si_ref[...], cim_ref[0])
        _, act_vjp = jax.vjp(lambda *t: _f_s5_act(*t)[0], ys, ub, d_ref[...])
        dy, du_skip, dd = act_vjp(dy2_ref[...])
        dd_ref[...] += dd
        ar = a_ref[0, 0:1, :]
        ai = a_ref[0, 1:2, :]
        x0r = xb_ref[0, 0, 0:1, :]
        x0i = xb_ref[0, 0, 1:2, :]
        dcre_ref[0] += _bdot(sr_ref[...], dy, TN)
        dcim_ref[0] -= _bdot(si_ref[...], dy, TN)
        gr_ref[...] = _bdot(dy, cre_ref[0], NT)
        gi_ref[...] = -_bdot(dy, cim_ref[0], NT)

        g0r, g0i = _s5_scan_rows(gr_ref, gi_ref, ar, -ai, gcarry_ref[0:1, :], gcarry_ref[1:2, :], tl, reverse=True)
        gcarry_ref[0:1, :] = g0r
        gcarry_ref[1:2, :] = g0i
        row = lax.broadcasted_iota(jnp.int32, (tl, ns), 0)
        gr = gr_ref[...]
        gi = gi_ref[...]
        xpr = jnp.where(row == 0, x0r, pltpu.roll(sr_ref[...].astype(F32), 1, 0))
        xpi = jnp.where(row == 0, x0i, pltpu.roll(si_ref[...].astype(F32), 1, 0))
        da_ref[0, 0:1, :] += jnp.sum(gr * xpr + gi * xpi, axis=0, keepdims=True)
        da_ref[0, 1:2, :] += jnp.sum(gi * xpr - gr * xpi, axis=0, keepdims=True)
        du_ref[...] = (_bdot(gr, bre_ref[0], NT) + _bdot(gi, bim_ref[0], NT)) + du_skip
        dbre_ref[0] += _bdot(ub, gr, TN)
        dbim_ref[0] += _bdot(ub, gi, TN)

    rev = lambda c: nc - 1 - c
    return pl.pallas_call(
        body, name="s5_core_bwd", grid=(nb, nc),
        in_specs=[pl.BlockSpec((tl, LANES), lambda j, c: (rev(c), j)), pl.BlockSpec((tl, LANES), lambda j, c: (rev(c), j)),
                  pl.BlockSpec((1, LANES, ns), lambda j, c: (j, 0, 0)), pl.BlockSpec((1, LANES, ns), lambda j, c: (j, 0, 0)),
                  pl.BlockSpec((1, ns, LANES), lambda j, c: (j, 0, 0)), pl.BlockSpec((1, ns, LANES), lambda j, c: (j, 0, 0)),
                  pl.BlockSpec((1, SUBLANES, ns), lambda j, c: (j, 0, 0)), pl.BlockSpec((1, LANES), lambda j, c: (0, j)),
                  pl.BlockSpec((1, 1, SUBLANES, ns), lambda j, c: (j, rev(c), 0, 0)),
                  pl.BlockSpec((tl, ns), lambda j, c: (rev(c), j)), pl.BlockSpec((tl, ns), lambda j, c: (rev(c), j))],
        out_specs=[pl.BlockSpec((tl, LANES), lambda j, c: (rev(c), j)),
                   pl.BlockSpec((1, LANES, ns), lambda j, c: (j, 0, 0)), pl.BlockSpec((1, LANES, ns), lambda j, c: (j, 0, 0)),
                   pl.BlockSpec((1, ns, LANES), lambda j, c: (j, 0, 0)), pl.BlockSpec((1, ns, LANES), lambda j, c: (j, 0, 0)),
                   pl.BlockSpec((1, SUBLANES, ns), lambda j, c: (j, 0, 0)), pl.BlockSpec((1, LANES), lambda j, c: (0, j))],
        out_shape=[SDS((L, e), F32), SDS(bre.shape, F32), SDS(bim.shape, F32), SDS(cre.shape, F32), SDS(cim.shape, F32),
                   SDS(a.shape, F32), SDS(d.shape, F32)],
        scratch_shapes=[pltpu.VMEM((tl, ns), F32) for _ in range(2)] + [pltpu.VMEM((SUBLANES, ns), F32)],
        compiler_params=_params(("arbitrary", "arbitrary"), VMEM_MID),
    )(u, dy2, bre, bim, cre, cim, a, d, xb, sr, si)


def make_s5_core(tl):
    @jax.custom_vjp
    def s5_core(u, bre, bim, cre, cim, a, d):
        return _s5_fwd_call(u, bre, bim, cre, cim, a, d, tl)[0]

    def fwd(u, bre, bim, cre, cim, a, d):
        y2, xb, sr, si = _s5_fwd_call(u, bre, bim, cre, cim, a, d, tl)
        return y2, (u, bre, bim, cre, cim, a, d, xb, sr, si)

    def bwd(res, dy2):
        u, bre, bim, cre, cim, a, d, xb, sr, si = res
        return tuple(_s5_bwd_call(u, dy2, bre, bim, cre, cim, a, d, xb, sr, si, tl))

    s5_core.defvjp(fwd, bwd)
    return s5_core


def _s5_block_params(lam_re, lam_im, log_dt, b_re, b_im, c_re, c_im):
    dt = jnp.exp(log_dt)[:, None]
    mag = jnp.exp(lam_re * dt)
    ab_re = mag * jnp.cos(lam_im * dt)
    ab_im = mag * jnp.sin(lam_im * dt)
    den = lam_re * lam_re + lam_im * lam_im
    nr = ab_re - 1.0
    ni = ab_im
    q_re = (nr * lam_re + ni * lam_im) / den
    q_im = (ni * lam_re - nr * lam_im) / den
    bb_re = q_re[..., None] * b_re - q_im[..., None] * b_im
    bb_im = q_re[..., None] * b_im + q_im[..., None] * b_re
    nb = S5_GROUPS // S5_GB
    eye = jnp.eye(S5_GB, dtype=F32)

    def bdiag_in(bb):
        t = bb.reshape(nb, S5_GB, S5_STATE, S5_GROUP)
        t = jnp.einsum("jgpm,gh->jgmhp", t, eye)
        return t.reshape(nb, S5_GB * S5_GROUP, S5_GB * S5_STATE)

    def bdiag_out(cc):
        t = cc.reshape(nb, S5_GB, S5_GROUP, S5_STATE)
        t = jnp.einsum("jgmp,gh->jgphm", t, eye)
        return t.reshape(nb, S5_GB * S5_STATE, S5_GB * S5_GROUP)

    a = jnp.stack([ab_re.reshape(nb, S5_GB * S5_STATE), ab_im.reshape(nb, S5_GB * S5_STATE)], axis=1)
    a = jnp.concatenate([a, jnp.zeros((nb, SUBLANES - 2, S5_GB * S5_STATE), F32)], axis=1)
    return bdiag_in(bb_re), bdiag_in(bb_im), bdiag_out(c_re), bdiag_out(c_im), a


def _shift_down(x, s, row):
    if s == 0:
        return x
    return jnp.where(row >= s, pltpu.roll(x, s, 0), 0.0)


def _shift_up(x, s, row, n):
    if s == 0:
        return x
    return jnp.where(row < n - s, pltpu.roll(x, n - s, 0), 0.0)


def _causal_conv(xv, w_ref, row):
    acc = jnp.zeros_like(xv)
    for j in range(GDN_CONV):
        acc += w_ref[j:j + 1, :] * _shift_down(xv, GDN_CONV - 1 - j, row)
    return acc


def _conv_fwd_call(x, w, act, name):
    L, ch = x.shape

    def body(x_ref, w_ref, y_ref):
        xv = x_ref[...]
        row = lax.broadcasted_iota(jnp.int32, xv.shape, 0)
        y_ref[...] = act(_causal_conv(xv, w_ref, row))

    return pl.pallas_call(
        body, name=name + "_fwd", grid=(ch // LANES,),
        in_specs=[pl.BlockSpec((L, LANES), lambda j: (0, j)), pl.BlockSpec((SUBLANES, LANES), lambda j: (0, j))],
        out_specs=pl.BlockSpec((L, LANES), lambda j: (0, j)), out_shape=SDS((L, ch), F32),
        compiler_params=_params(("parallel",), VMEM_MID),
    )(x, w)


def _conv_bwd_call(x, w, dy, act, name):
    L, ch = x.shape

    def body(x_ref, w_ref, dy_ref, dx_ref, dw_ref):
        xv = x_ref[...]
        row = lax.broadcasted_iota(jnp.int32, xv.shape, 0)
        _, act_vjp = jax.vjp(act, _causal_conv(xv, w_ref, row))
        (g,) = act_vjp(dy_ref[...])
        acc = jnp.zeros_like(xv)
        dws = []
        for j in range(GDN_CONV):
            s = GDN_CONV - 1 - j
            acc += w_ref[j:j + 1, :] * _shift_up(g, s, row, L)
            dws.append(jnp.sum(g * _shift_down(xv, s, row), axis=0, keepdims=True))
        dx_ref[...] = acc
        dw_ref[...] = jnp.concatenate(dws + [jnp.zeros((SUBLANES - GDN_CONV, LANES), F32)], axis=0)

    return pl.pallas_call(
        body, name=name + "_bwd", grid=(ch // LANES,),
        in_specs=[pl.BlockSpec((L, LANES), lambda j: (0, j)), pl.BlockSpec((SUBLANES, LANES), lambda j: (0, j)),
                  pl.BlockSpec((L, LANES), lambda j: (0, j))],
        out_specs=[pl.BlockSpec((L, LANES), lambda j: (0, j)), pl.BlockSpec((SUBLANES, LANES), lambda j: (0, j))],
        out_shape=[SDS((L, ch), F32), SDS((SUBLANES, ch), F32)],
        compiler_params=_params(("parallel",), VMEM_MID),
    )(x, w, dy)


def make_conv_act(act, name):
    @jax.custom_vjp
    def op(x, w):
        return _conv_fwd_call(x, w, act, name)

    def fwd(x, w):
        return _conv_fwd_call(x, w, act, name), (x, w)

    def bwd(res, dy):
        x, w = res
        return tuple(_conv_bwd_call(x, w, dy, act, name))

    op.defvjp(fwd, bwd)
    return op


gdn_conv = make_conv_act(lambda c: c, "gdn_conv")


BNN = (((2,), (1,)), ((0,), (0,)))
BNT = (((2,), (2,)), ((0,), (0,)))
BTN = (((1,), (1,)), ((0,), (0,)))
GDN_PREP_BATCH = 8


@jax.custom_vjp
def _known_inverse(a, t):
    return t


def _known_inverse_fwd(a, t):
    return t, t


def _known_inverse_bwd(t, g):
    return -_hdot(_hdot(t, g, _BTN), t, _BNT), jnp.zeros_like(t)


_known_inverse.defvjp(_known_inverse_fwd, _known_inverse_bwd)


def _gdn_prep_math(q, k, v, beta, g, t_saved=None):
    B, C = q.shape[0], q.shape[1]
    ri = lax.broadcasted_iota(jnp.int32, (B, C, C), 1)
    ci = lax.broadcasted_iota(jnp.int32, (B, C, C), 2)
    causal = ri >= ci
    strict = ri > ci
    eye = (ri == ci).astype(F32)
    gb = jnp.broadcast_to(g, (B, C, C))
    g_row = jnp.sum(gb * eye, axis=1, keepdims=True)
    gc_col = jnp.sum(jnp.where(causal, jnp.broadcast_to(g_row, (B, C, C)), 0.0), axis=2, keepdims=True)
    gc_row = jnp.sum(jnp.where(ri <= ci, gb, 0.0), axis=1, keepdims=True)
    decay = jnp.exp(jnp.where(causal, gc_col - gc_row, -jnp.inf))
    kk = _bdot(k, k, BNT)
    a_mat = jnp.where(strict, beta * kk * decay, 0.0)
    t = _unit_lower_inverse(a_mat) if t_saved is None else _known_inverse(a_mat, t_saved)
    e_gc = jnp.exp(gc_col)
    w = _hdot(t, beta * e_gc * k, BNN)
    u = _hdot(t, beta * v, BNN)
    qk = _bdot(q, k, BNT) * decay
    q_dec = q * e_gc
    g_last = gc_col[:, C - 1:C, :]
    k_dec = k * jnp.exp(g_last - gc_col)
    return q_dec, w, u, qk, k_dec, gc_col, t


def _gdn_prep_specs(L):
    C = GDN_CHUNK
    nb = min(GDN_PREP_BATCH, L // C)
    R = nb * C
    ins = [pl.BlockSpec((R, GDN_DK), lambda c, h: (c, h)), pl.BlockSpec((R, GDN_DK), lambda c, h: (c, h)),
           pl.BlockSpec((R, GDN_DV), lambda c, h: (c, h)), pl.BlockSpec((R, LANES), lambda c, h: (c, 0))]
    outs = [pl.BlockSpec((1, R, GDN_DK), lambda c, h: (h, c, 0)), pl.BlockSpec((1, R, GDN_DK), lambda c, h: (h, c, 0)),
            pl.BlockSpec((1, R, GDN_DV), lambda c, h: (h, c, 0)), pl.BlockSpec((1, R, C), lambda c, h: (h, c, 0)),
            pl.BlockSpec((1, R, GDN_DK), lambda c, h: (h, c, 0)), pl.BlockSpec((1, R, 1), lambda c, h: (h, c, 0))]
    t_spec = pl.BlockSpec((1, R, C), lambda c, h: (h, c, 0))
    shapes = [SDS((GDN_HEADS, L, GDN_DK), F32), SDS((GDN_HEADS, L, GDN_DK), F32), SDS((GDN_HEADS, L, GDN_DV), F32),
              SDS((GDN_HEADS, L, C), F32), SDS((GDN_HEADS, L, GDN_DK), F32), SDS((GDN_HEADS, L, 1), F32)]
    return ins, outs, t_spec, shapes, nb


def _chunks(x, nb):
    return x.reshape(nb, x.shape[0] // nb, x.shape[1])


def _head_columns(bg, h):
    lane = lax.broadcasted_iota(jnp.int32, bg.shape, 1)
    beta = jnp.sum(jnp.where(lane == h, bg, 0.0), axis=1, keepdims=True)
    g = jnp.sum(jnp.where(lane == h + GDN_HEADS, bg, 0.0), axis=1, keepdims=True)
    return beta, g


def _gdn_prep_fwd_call(q, k, v, bg):
    L = q.shape[0]
    ins, outs, t_spec, shapes, nb = _gdn_prep_specs(L)

    def body(q_ref, k_ref, v_ref, bg_ref, *o_refs):
        beta, g = _head_columns(bg_ref[...], pl.program_id(1))
        res = _gdn_prep_math(_chunks(q_ref[...], nb), _chunks(k_ref[...], nb), _chunks(v_ref[...], nb),
                             _chunks(beta, nb), _chunks(g, nb))
        for o_ref, val in zip(o_refs, res):
            o_ref[0] = val.reshape(val.shape[0] * val.shape[1], val.shape[2])

    return pl.pallas_call(
        body, name="gdn_prep_fwd", grid=(L // (nb * GDN_CHUNK), GDN_HEADS), in_specs=ins, out_specs=outs + [t_spec],
        out_shape=shapes + [SDS((GDN_HEADS, L, GDN_CHUNK), F32)],
        compiler_params=_params(("parallel", "parallel"), VMEM_MID),
    )(q, k, v, bg)


def _gdn_prep_bwd_call(q, k, v, bg, t, cts):
    L = q.shape[0]
    ins, outs, t_spec, _, nb = _gdn_prep_specs(L)

    def body(q_ref, k_ref, v_ref, bg_ref, t_ref, c0, c1, c2, c3, c4, c5, dq_ref, dk_ref, dv_ref, dbg_ref):
        h = pl.program_id(1)
        beta, g = _head_columns(bg_ref[...], h)
        t_saved = _chunks(t_ref[0], nb)
        _, vjp = jax.vjp(lambda *a: _gdn_prep_math(*a, t_saved=t_saved)[:6], _chunks(q_ref[...], nb), _chunks(k_ref[...], nb),
                         _chunks(v_ref[...], nb), _chunks(beta, nb), _chunks(g, nb))
        dq, dk, dv, db, dg = vjp(tuple(_chunks(c[0], nb) for c in (c0, c1, c2, c3, c4, c5)))
        flat = lambda a: a.reshape(a.shape[0] * a.shape[1], a.shape[2])
        dq_ref[...] = flat(dq)
        dk_ref[...] = flat(dk)
        dv_ref[...] = flat(dv)

        @pl.when(h == 0)
        def _():
            dbg_ref[...] = jnp.zeros_like(dbg_ref)
        lane = lax.broadcasted_iota(jnp.int32, dbg_ref.shape, 1)
        dbg_ref[...] += jnp.where(lane == h, flat(db), 0.0) + jnp.where(lane == h + GDN_HEADS, flat(dg), 0.0)

    return pl.pallas_call(
        body, name="gdn_prep_bwd", grid=(L // (nb * GDN_CHUNK), GDN_HEADS), in_specs=ins + [t_spec] + outs, out_specs=ins,
        out_shape=[SDS(q.shape, F32), SDS(k.shape, F32), SDS(v.shape, F32), SDS(bg.shape, F32)],
        compiler_params=_params(("parallel", "arbitrary"), VMEM_MID),
    )(q, k, v, bg, t, *cts)


@jax.custom_vjp
def gdn_prep(q, k, v, bg):
    return tuple(_gdn_prep_fwd_call(q, k, v, bg)[:6])


def _gdn_prep_f(q, k, v, bg):
    res = _gdn_prep_fwd_call(q, k, v, bg)
    return tuple(res[:6]), (q, k, v, bg, res[6])


def _gdn_prep_b(res, cts):
    return tuple(_gdn_prep_bwd_call(*res, tuple(cts)))


gdn_prep.defvjp(_gdn_prep_f, _gdn_prep_b)


def _gdn_step_math(q_dec, w, u, qk, k_dec, gc, z, nw, state):
    H, C = q_dec.shape[0], q_dec.shape[1]
    v_new = u - _bdot(w, state, BNN)
    o = _bdot(q_dec, state, BNN) + _bdot(qk, v_new, BNN)
    gl = gc[:, C - 1:C, :]
    new_state = jnp.exp(gl) * state + _bdot(k_dec, v_new, BTN)
    return _f_gdn_post(jnp.concatenate([o[h] for h in range(H)], axis=1), z, nw)[0], new_state


def _gdn_scan_specs(L, rev):
    C, H = GDN_CHUNK, GDN_HEADS
    nc = L // C
    cc = (lambda c: nc - 1 - c) if rev else (lambda c: c)
    ins = [pl.BlockSpec((H, C, GDN_DK), lambda c: (0, cc(c), 0)), pl.BlockSpec((H, C, GDN_DK), lambda c: (0, cc(c), 0)),
           pl.BlockSpec((H, C, GDN_DV), lambda c: (0, cc(c), 0)), pl.BlockSpec((H, C, C), lambda c: (0, cc(c), 0)),
           pl.BlockSpec((H, C, GDN_DK), lambda c: (0, cc(c), 0)), pl.BlockSpec((H, C, 1), lambda c: (0, cc(c), 0))]
    o_spec = pl.BlockSpec((C, H * GDN_DV), lambda c: (cc(c), 0))
    nw_spec = pl.BlockSpec((1, H * GDN_DV), lambda c: (0, 0))
    s_spec = pl.BlockSpec((1, H, GDN_DK, GDN_DV), lambda c: (cc(c), 0, 0, 0))
    return ins + [o_spec, nw_spec], o_spec, s_spec, nc


def _gdn_scan_fwd_call(q_dec, w, u, qk, k_dec, gc, z, nw):
    L = q_dec.shape[1]
    ins, o_spec, s_spec, nc = _gdn_scan_specs(L, False)

    def body(qd_ref, w_ref, u_ref, qk_ref, kd_ref, gc_ref, z_ref, nw_ref, o_ref, sin_ref, s_ref):
        c = pl.program_id(0)

        @pl.when(c == 0)
        def _():
            s_ref[...] = jnp.zeros_like(s_ref)
        st = s_ref[...]
        sin_ref[0] = st
        o, ns = _gdn_step_math(qd_ref[...], w_ref[...], u_ref[...], qk_ref[...], kd_ref[...], gc_ref[...], z_ref[...], nw_ref[...], st)
        o_ref[...] = o
        s_ref[...] = ns

    return pl.pallas_call(
        body, name="gdn_scan_fwd", grid=(nc,), in_specs=ins, out_specs=[o_spec, s_spec],
        out_shape=[SDS((L, GDN_HEADS * GDN_DV), F32), SDS((nc, GDN_HEADS, GDN_DK, GDN_DV), F32)],
        scratch_shapes=[pltpu.VMEM((GDN_HEADS, GDN_DK, GDN_DV), F32)],
        compiler_params=_params(("arbitrary",), VMEM_MID),
    )(q_dec, w, u, qk, k_dec, gc, z, nw)


def _gdn_scan_bwd_call(q_dec, w, u, qk, k_dec, gc, z, nw, s_in, do):
    L = q_dec.shape[1]
    ins, o_spec, s_spec, nc = _gdn_scan_specs(L, True)

    def body(qd_ref, w_ref, u_ref, qk_ref, kd_ref, gc_ref, z_ref, nw_ref, sin_ref, do_ref,
             dqd_ref, dw_ref, du_ref, dqk_ref, dkd_ref, dgc_ref, dz_ref, dnw_ref, ds_ref):
        c = pl.program_id(0)

        @pl.when(c == 0)
        def _():
            ds_ref[...] = jnp.zeros_like(ds_ref)
            dnw_ref[...] = jnp.zeros_like(dnw_ref)
        _, vjp = jax.vjp(_gdn_step_math, qd_ref[...], w_ref[...], u_ref[...], qk_ref[...], kd_ref[...], gc_ref[...],
                         z_ref[...], nw_ref[...], sin_ref[0])
        dqd, dw, du, dqk, dkd, dgc, dz, dnw, dst = vjp((do_ref[...], ds_ref[...]))
        dqd_ref[...] = dqd
        dw_ref[...] = dw
        du_ref[...] = du
        dqk_ref[...] = dqk
        dkd_ref[...] = dkd
        dgc_ref[...] = dgc
        dz_ref[...] = dz
        dnw_ref[...] += dnw
        ds_ref[...] = dst

    return pl.pallas_call(
        body, name="gdn_scan_bwd", grid=(nc,), in_specs=ins + [s_spec, o_spec], out_specs=ins,
        out_shape=[SDS(t.shape, F32) for t in (q_dec, w, u, qk, k_dec, gc, z, nw)],
        scratch_shapes=[pltpu.VMEM((GDN_HEADS, GDN_DK, GDN_DV), F32)],
        compiler_params=_params(("arbitrary",), VMEM_MID),
    )(q_dec, w, u, qk, k_dec, gc, z, nw, s_in, do)


@jax.custom_vjp
def gdn_scan(q_dec, w, u, qk, k_dec, gc, z, nw):
    return _gdn_scan_fwd_call(q_dec, w, u, qk, k_dec, gc, z, nw)[0]


def _gdn_scan_f(*args):
    o, s_in = _gdn_scan_fwd_call(*args)
    return o, (*args, s_in)


def _gdn_scan_b(res, do):
    return tuple(_gdn_scan_bwd_call(*res, do))


gdn_scan.defvjp(_gdn_scan_f, _gdn_scan_b)


def _silu(x):
    return x * jax.nn.sigmoid(x)


def _gelu_tanh(x):
    return 0.5 * x * (1.0 + jnp.tanh(math.sqrt(2.0 / math.pi) * (x + 0.044715 * (x * x * x))))


def _f_lnmod(x, nw, sc, sh, bsc, bsh):
    xn = x * lax.rsqrt(jnp.mean(x * x, axis=-1, keepdims=True) + NORM_EPS) * nw
    return (xn * (1.0 + (sc + bsc)) + (sh + bsh),)


def _f_s5_act(ys, u, d):
    return (_gelu_tanh(ys + d * u),)


def _f_s5_gate(y2, t, z):
    return (y2 * jax.nn.sigmoid(t) * _silu(z),)


def _f_res(x, y, gate, bgate):
    return (x + (gate + bgate) * y,)


def _heads(x, width, fn):
    return jnp.concatenate([fn(x[:, i * width:(i + 1) * width]) for i in range(x.shape[1] // width)], axis=1)


def _l2n(x):
    return x * lax.rsqrt(jnp.sum(x * x, axis=-1, keepdims=True) + NORM_EPS)


def _f_qnorm(x):
    return (_heads(_silu(x), GDN_DK, _l2n) * (GDN_DK ** -0.5),)


def _f_knorm(x):
    return (_heads(_silu(x), GDN_DK, _l2n),)


def _f_vact(x):
    return (_silu(x),)


def _f_betag(ba, alog, dtb):
    col = lax.broadcasted_iota(jnp.int32, ba.shape, 1)
    t = ba + dtb
    softplus = jnp.maximum(t, 0.0) + jnp.log1p(jnp.exp(-jnp.abs(t)))
    g = -jnp.exp(alog) * softplus
    return (jnp.where(col < GDN_HEADS, jax.nn.sigmoid(ba), jnp.where(col < 2 * GDN_HEADS, g, 0.0)),)


def _f_gdn_post(o, z, nw):
    on = _heads(o, GDN_DV, lambda t: t * lax.rsqrt(jnp.mean(t * t, axis=-1, keepdims=True) + NORM_EPS))
    return (on * nw * _silu(z),)


def _f_loss(x, tgt, fw):
    y = x * lax.rsqrt(jnp.mean(x * x, axis=-1, keepdims=True) + NORM_EPS) * fw
    err = y - tgt
    return (0.5 * jnp.mean(err * err, axis=-1, keepdims=True),)


def _ada_mod_call(c_all, ada_w):
    n = ada_w.shape[2]

    def body(c_ref, w_ref, o_ref):
        ca = _silu(c_ref[...])
        for l in range(ada_w.shape[0]):
            o_ref[l] = _bdot(ca, w_ref[l])

    return pl.pallas_call(body, name="ada_mod", out_shape=SDS((ada_w.shape[0], N_DEV, n), F32),
                          compiler_params=_params(None, VMEM_MID))(c_all, ada_w)


def _ada_grad_call(c_all, dmod):
    nl, _, n = dmod.shape

    def body(c_ref, d_ref, o_ref):
        ca = _silu(c_ref[...])
        for l in range(nl):
            o_ref[l] = _hdot(ca, d_ref[l], TN)

    return pl.pallas_call(body, name="ada_grad", out_shape=SDS((nl, c_all.shape[1], n), F32),
                          compiler_params=_params(None, VMEM_MID))(c_all, dmod)


ADAM_ROWS = 512


def _adamw(g, w, m, v):
    m2 = ADAM_B1 * m + (1.0 - ADAM_B1) * g
    v2 = ADAM_B2 * v + (1.0 - ADAM_B2) * (g * g)
    m_hat = m2 / (1.0 - ADAM_B1 ** ADAM_STEP)
    v_hat = v2 / (1.0 - ADAM_B2 ** ADAM_STEP)
    return g, -ADAM_LR * (m_hat / (jnp.sqrt(v_hat) + ADAM_EPS) + ADAM_WD * w), m2, v2


def _adam_call(gs, w, m, v, name, rows=None):
    n, r, cols = gs.shape
    rows = rows or ADAM_ROWS

    def body(g_ref, w_ref, m_ref, v_ref, go_ref, d_ref, mo_ref, vo_ref):
        g = g_ref[0].astype(F32)
        for s in range(1, n):
            g = g + g_ref[s].astype(F32)
        for o_ref, val in zip((go_ref, d_ref, mo_ref, vo_ref), _adamw(g, w_ref[...], m_ref[...], v_ref[...])):
            o_ref[...] = val

    blk = pl.BlockSpec((rows, cols), lambda i: (i, 0))
    return pl.pallas_call(
        body, name=name, grid=(r // rows,),
        in_specs=[pl.BlockSpec((n, rows, cols), lambda i: (0, i, 0)), blk, blk, blk],
        out_specs=[blk, blk, blk, blk], out_shape=[SDS((r, cols), F32)] * 4,
        compiler_params=_params(("parallel",), VMEM_MID),
    )(gs, w, m, v)


def _sum_call(gs, name, rows):
    n, r, _ = gs.shape

    def body(g_ref, o_ref):
        g = g_ref[0].astype(F32)
        for s in range(1, n):
            g = g + g_ref[s].astype(F32)
        o_ref[...] = g

    return pl.pallas_call(
        body, name=name, grid=(r // rows,),
        in_specs=[pl.BlockSpec((n, rows, LANES), lambda i: (0, i, 0))],
        out_specs=pl.BlockSpec((rows, LANES), lambda i: (i, 0)), out_shape=SDS((r, LANES), F32),
        compiler_params=_params(("parallel",), VMEM_MID),
    )(gs)


def _allgather_call(x_shard, name, in_hbm):
    m_per, n = x_shard.shape

    def body(x_ref, out_ref, send_sems, recv_sems, local_sem):
        x, y, c = lax.axis_index("x"), lax.axis_index("y"), lax.axis_index("c")
        me, sibling = (x, y, c), (x, y, 1 - c)
        chips = [(1 - x, y), (x, 1 - y), (1 - x, 1 - y)]

        def rows(px, py, pc):
            return out_ref.at[pl.ds((4 * px + 2 * py + pc) * m_per, m_per), :]

        def copy(k, block, to, src=None):
            return pltpu.make_async_remote_copy(
                src_ref=rows(*block) if src is None else src, dst_ref=rows(*block),
                send_sem=send_sems.at[k], recv_sem=recv_sems.at[k], device_id=to, device_id_type=pl.DeviceIdType.MESH)

        mine = pltpu.make_async_copy(x_ref, rows(*me), local_sem)
        mine.start()
        first = [copy(0, me, sibling, src=x_ref)]
        first += [copy(1 + j, me, (*chip, c), src=x_ref) for j, chip in enumerate(chips)]
        for cp in first:
            cp.start()
        passed = [copy(4 + j, (*chip, c), sibling) for j, chip in enumerate(chips)]
        for j, chip in enumerate(chips):
            copy(1 + j, (*chip, c), me).wait_recv()
            passed[j].start()
        copy(0, sibling, me).wait_recv()
        for j, chip in enumerate(chips):
            copy(4 + j, (*chip, 1 - c), me).wait_recv()
        for cp in first + passed:
            cp.wait_send()
        mine.wait()

    space = pl.ANY if in_hbm else pltpu.VMEM
    return pl.pallas_call(
        body, name=name, out_shape=SDS((N_DEV * m_per, n), x_shard.dtype),
        in_specs=[pl.BlockSpec(memory_space=space)], out_specs=pl.BlockSpec(memory_space=space),
        scratch_shapes=[pltpu.SemaphoreType.DMA((7,)), pltpu.SemaphoreType.DMA((7,)), pltpu.SemaphoreType.DMA],
        compiler_params=_params(None, None if in_hbm else VMEM_BIG),
    )(x_shard)


def _gather_weights_call(shards, name):
    nw = len(shards)

    def body(*refs):
        x_refs, out_refs = refs[:nw], refs[nw:2 * nw]
        send_sems, recv_sems, local_sems = refs[2 * nw:]
        x, y, c = lax.axis_index("x"), lax.axis_index("y"), lax.axis_index("c")
        me, sibling = (x, y, c), (x, y, 1 - c)
        chips = [(1 - x, y), (x, 1 - y), (1 - x, 1 - y)]

        def slot(w, px, py, pc):
            return out_refs[w].at[4 * px + 2 * py + pc]

        def copy(w, k, block, to, src=None):
            dst = slot(w, *block)
            return pltpu.make_async_remote_copy(
                src_ref=dst if src is None else src, dst_ref=dst, send_sem=send_sems.at[7 * w + k],
                recv_sem=recv_sems.at[7 * w + k], device_id=to, device_id_type=pl.DeviceIdType.MESH)

        mines = [pltpu.make_async_copy(x_refs[w], slot(w, *me), local_sems.at[w]) for w in range(nw)]
        for cp in mines:
            cp.start()
        first = [copy(w, 0, me, sibling, src=x_refs[w]) for w in range(nw)]
        first += [copy(w, 1 + j, me, (*chip, c), src=x_refs[w]) for w in range(nw) for j, chip in enumerate(chips)]
        for cp in first:
            cp.start()
        passed = []
        for w in range(nw):
            for j, chip in enumerate(chips):
                copy(w, 1 + j, (*chip, c), me).wait_recv()
                fwd = copy(w, 4 + j, (*chip, c), sibling)
                fwd.start()
                passed.append(fwd)
        for w in range(nw):
            copy(w, 0, sibling, me).wait_recv()
            for j, chip in enumerate(chips):
                copy(w, 4 + j, (*chip, 1 - c), me).wait_recv()
        for cp in first + passed:
            cp.wait_send()
        for cp in mines:
            cp.wait()

    hbm = pl.BlockSpec(memory_space=pl.ANY)
    return pl.pallas_call(
        body, name=name, out_shape=[SDS((N_DEV,) + s.shape, s.dtype) for s in shards],
        in_specs=[hbm] * nw, out_specs=[hbm] * nw,
        scratch_shapes=[pltpu.SemaphoreType.DMA((7 * nw,)), pltpu.SemaphoreType.DMA((7 * nw,)), pltpu.SemaphoreType.DMA((nw,))],
    )(*shards)


def _pair_exchange_call(grads, name):
    nw = len(grads)

    def body(*refs):
        g_refs, got_refs = refs[:nw], refs[nw:2 * nw]
        send_sems, recv_sems = refs[2 * nw:]
        x, y, c = lax.axis_index("x"), lax.axis_index("y"), lax.axis_index("c")
        copies = []
        for w in range(nw):
            for j in range(4):
                give = pltpu.make_async_remote_copy(
                    src_ref=g_refs[w].at[2 * j + 1 - c], dst_ref=got_refs[w].at[j], send_sem=send_sems.at[4 * w + j],
                    recv_sem=recv_sems.at[4 * w + j], device_id=(x, y, 1 - c), device_id_type=pl.DeviceIdType.MESH)
                give.start()
                copies.append(give)
        for cp in copies:
            cp.wait()

    hbm = pl.BlockSpec(memory_space=pl.ANY)
    return pl.pallas_call(
        body, name=name, out_shape=[SDS((4,) + g.shape[1:], g.dtype) for g in grads], in_specs=[hbm] * nw, out_specs=[hbm] * nw,
        scratch_shapes=[pltpu.SemaphoreType.DMA((4 * nw,)), pltpu.SemaphoreType.DMA((4 * nw,))],
    )(*grads)


def _chip_exchange_call(parts, name):
    nw = len(parts)

    def body(*refs):
        p_refs, out_refs = refs[:nw], refs[nw:2 * nw]
        send_sems, recv_sems = refs[2 * nw:]
        x, y, c = lax.axis_index("x"), lax.axis_index("y"), lax.axis_index("c")
        chips = [(1 - x, y), (x, 1 - y), (1 - x, 1 - y)]
        copies = []
        for w in range(nw):
            for j, (px, py) in enumerate(chips):
                give = pltpu.make_async_remote_copy(
                    src_ref=p_refs[w].at[2 * px + py], dst_ref=out_refs[w].at[j], send_sem=send_sems.at[3 * w + j],
                    recv_sem=recv_sems.at[3 * w + j], device_id=(px, py, c), device_id_type=pl.DeviceIdType.MESH)
                give.start()
                copies.append(give)
        for cp in copies:
            cp.wait()

    hbm = pl.BlockSpec(memory_space=pl.ANY)
    return pl.pallas_call(
        body, name=name, out_shape=[SDS((3,) + p.shape[1:], p.dtype) for p in parts], in_specs=[hbm] * nw, out_specs=[hbm] * nw,
        scratch_shapes=[pltpu.SemaphoreType.DMA((3 * nw,)), pltpu.SemaphoreType.DMA((3 * nw,))],
    )(*parts)


_HBM = pl.BlockSpec(memory_space=pltpu.HBM)
_SEM = pl.BlockSpec(memory_space=pltpu.SEMAPHORE)
_DATAFLOW = pltpu.SideEffectType.DATAFLOW_SIDE_EFFECTING


def _spread_start_call(srcs, per_peer, name, after):
    nw = len(srcs)
    lands = [lax.empty((N_DEV,) + (s.shape[1:] if per_peer else s.shape), s.dtype) for s in srcs]

    def body(*refs):
        src_refs, land_refs = refs[:nw], refs[nw:2 * nw]
        send_sems, recv_sems, token = refs[2 * nw + 1], refs[2 * nw + 2], refs[-1]
        x, y, c = lax.axis_index("x"), lax.axis_index("y"), lax.axis_index("c")
        me = 4 * x + 2 * y + c
        for w in range(nw):
            for k in range(1, N_DEV):
                px = 1 - x if k & 4 else x
                py = 1 - y if k & 2 else y
                pc = 1 - c if k & 1 else c
                src = src_refs[w].at[4 * px + 2 * py + pc] if per_peer else src_refs[w]
                pltpu.make_async_remote_copy(
                    src_ref=src, dst_ref=land_refs[w].at[me], send_sem=send_sems.at[w], recv_sem=recv_sems.at[w],
                    device_id=(px, py, pc), device_id_type=pl.DeviceIdType.MESH).start()
        token[...] = jnp.zeros_like(token)

    hbm = lambda a: pltpu.with_memory_space_constraint(a, pltpu.HBM)
    res = pl.pallas_call(
        body, name=name,
        out_shape=(pltpu.SemaphoreType.DMA((nw,)), pltpu.SemaphoreType.DMA((nw,)))
        + tuple(pltpu.HBM(s.shape, s.dtype) for s in srcs) + tuple(pltpu.HBM(l.shape, l.dtype) for l in lands)
        + (SDS((SUBLANES, LANES), F32),),
        in_specs=[_HBM] * (2 * nw) + [pl.BlockSpec(memory_space=pl.ANY)],
        out_specs=(_SEM, _SEM) + (_HBM,) * (2 * nw) + (pl.BlockSpec(memory_space=pltpu.VMEM),),
        input_output_aliases={i: i + 2 for i in range(2 * nw)},
        compiler_params=pltpu.CompilerParams(has_side_effects=_DATAFLOW),
    )(*[hbm(s) for s in srcs], *[hbm(l) for l in lands], after)
    return res[0], res[1], res[2:2 + nw], res[2 + nw:2 + 2 * nw], res[-1]


def _spread_wait_call(send_sems, recv_sems, srcs, lands, after, name):
    nw = len(lands)

    def body(*refs):
        land_refs = refs[nw:2 * nw]
        s_sems, r_sems = refs[2 * nw], refs[2 * nw + 1]
        x, y, c = lax.axis_index("x"), lax.axis_index("y"), lax.axis_index("c")
        for w in range(nw):
            seven = land_refs[w].at[pl.ds(0, N_DEV - 1)]
            all_seven = pltpu.make_async_remote_copy(
                src_ref=seven, dst_ref=seven, send_sem=s_sems.at[w], recv_sem=r_sems.at[w],
                device_id=(x, y, c), device_id_type=pl.DeviceIdType.MESH)
            all_seven.wait_send()
            all_seven.wait_recv()

    res = pl.pallas_call(
        body, name=name,
        out_shape=tuple(pltpu.HBM(s.shape, s.dtype) for s in srcs) + tuple(pltpu.HBM(l.shape, l.dtype) for l in lands),
        in_specs=[_HBM] * (2 * nw) + [_SEM, _SEM, pl.BlockSpec(memory_space=pl.ANY)], out_specs=(_HBM,) * (2 * nw),
        input_output_aliases={i: i for i in range(2 * nw)},
        compiler_params=pltpu.CompilerParams(has_side_effects=_DATAFLOW),
    )(*srcs, *lands, send_sems, recv_sems, after)
    return res[:nw], res[nw:]


def _pair_sum_call(g, got, core, name):
    _, k, n = got.shape
    tr = _tile(k, 256)

    def body(c_ref, g_ref, got_ref, o_ref):
        o_ref[...] = (g_ref[...] + got_ref[...]).astype(o_ref.dtype)

    spec = pltpu.PrefetchScalarGridSpec(
        num_scalar_prefetch=1, grid=(4, k // tr),
        in_specs=[pl.BlockSpec((1, tr, n), lambda j, i, c: (2 * j + c[0], i, 0)), pl.BlockSpec((1, tr, n), lambda j, i, c: (j, i, 0))],
        out_specs=pl.BlockSpec((1, tr, n), lambda j, i, c: (j, i, 0)))
    return pl.pallas_call(body, name=name, grid_spec=spec, out_shape=SDS(got.shape, BF16),
                          compiler_params=_params(("parallel", "parallel"), VMEM_MID))(core, g, got)


def _adam_own_call(pair, chip, recv, w, m, v, name, rows):
    _, r, cols = recv.shape

    def body(chip_ref, p_ref, g_ref, w_ref, m_ref, v_ref, go_ref, d_ref, mo_ref, vo_ref):
        g = ((p_ref[0].astype(F32) + g_ref[0].astype(F32)) + g_ref[1].astype(F32)) + g_ref[2].astype(F32)
        for o_ref, val in zip((go_ref, d_ref, mo_ref, vo_ref), _adamw(g, w_ref[...], m_ref[...], v_ref[...])):
            o_ref[...] = val

    blk = pl.BlockSpec((rows, cols), lambda i, s: (i, 0))
    spec = pltpu.PrefetchScalarGridSpec(
        num_scalar_prefetch=1, grid=(r // rows,),
        in_specs=[pl.BlockSpec((1, rows, cols), lambda i, s: (s[0], i, 0)), pl.BlockSpec((3, rows, cols), lambda i, s: (0, i, 0)),
                  blk, blk, blk],
        out_specs=[blk, blk, blk, blk])
    return pl.pallas_call(body, name=name, grid_spec=spec, out_shape=[SDS((r, cols), F32)] * 4,
                          compiler_params=_params(("parallel",), VMEM_MID))(chip, pair, recv, w, m, v)


def _join_cols_call(w8, name):
    _, k, n = w8.shape
    tk = _tile(k, 256)

    def body(w_ref, o_ref):
        for s in range(N_DEV):
            o_ref[:, n * s:n * (s + 1)] = w_ref[s]

    return pl.pallas_call(body, name=name, grid=(k // tk,), in_specs=[pl.BlockSpec((N_DEV, tk, n), lambda i: (0, i, 0))],
                          out_specs=pl.BlockSpec((tk, N_DEV * n), lambda i: (i, 0)), out_shape=SDS((k, N_DEV * n), w8.dtype),
                          compiler_params=_params(("parallel",), VMEM_MID))(w8)


def _split_cols_call(g, name, dtype):
    k, n8 = g.shape
    n = n8 // N_DEV
    tk = _tile(k, 256)

    def body(g_ref, o_ref):
        for s in range(N_DEV):
            o_ref[s] = g_ref[:, n * s:n * (s + 1)].astype(dtype)

    return pl.pallas_call(body, name=name, grid=(k // tk,), in_specs=[pl.BlockSpec((tk, n8), lambda i: (i, 0))],
                          out_specs=pl.BlockSpec((N_DEV, tk, n), lambda i: (0, i, 0)), out_shape=SDS((N_DEV, k, n), dtype),
                          compiler_params=_params(("parallel",), VMEM_MID))(g)


def _pack(parts, rows_multiple):
    flat = jnp.concatenate([p.reshape(-1) for p in parts])
    unit = rows_multiple * LANES
    padded = -(-flat.shape[0] // unit) * unit
    flat = jnp.concatenate([flat, jnp.zeros((padded - flat.shape[0],), F32)])
    return flat.reshape(-1, LANES)


def _unpack(buf, shapes):
    flat = buf.reshape(-1)
    out, off = [], 0
    for s in shapes:
        n = math.prod(s)
        out.append(flat[off:off + n].reshape(s))
        off += n
    return out


def _row_tile(L):
    return 256 if L % 256 == 0 else L


def _layer0_mix(diff, const):
    x, mod, norm_w, lam_re, lam_im, log_dt, b_re, b_im, c_re, c_im, s5_d, *slots = diff
    ada_b, weights = const
    L = x.shape[0]
    tm = _row_tile(L)
    mods = mod.reshape(2, 1, D_MODEL)
    biases = ada_b.reshape(2, 1, D_MODEL)
    op_ln0 = make_rowwise(_f_lnmod, "ln0", tm, 1, 5, pass_first=True)
    h, x = op_ln0((x,), (norm_w.reshape(1, D_MODEL), mods[1], mods[0], biases[1], biases[0]))
    u, z = make_proj("s5_in")(h, tuple(weights), tuple(slots))
    blocks = _s5_block_params(lam_re, lam_im, log_dt, b_re, b_im, c_re, c_im)
    y2 = make_s5_core(min(S5_TL, L))(u, *blocks, s5_d.reshape(1, D_INNER))
    return x, y2, z


def _layer0_out(diff, weights):
    y2, z, *slots = diff
    tm = _row_tile(y2.shape[0])
    t, y2 = make_mm("s5_glu", pass_input=True)(y2, weights[0], slots[0])
    (y4,) = make_rowwise(_f_s5_gate, "s5_gate", tm, 3, 0)((y2, t, z), ())
    return make_mm("s5_out")(y4, weights[1], slots[1])


def _f_res_lnmod(x, o, gate, bgate, nw, sc, sh, bsc, bsh):
    (x1,) = _f_res(x, o, gate, bgate)
    return _f_lnmod(x1, nw, sc, sh, bsc, bsh) + (x1,)


def _f_res_loss(x, y, tgt, gate, bgate, fw):
    return _f_loss(_f_res(x, y, gate, bgate)[0], tgt, fw)


def _layer1_loss(diff, const):
    x, o, gate0, mod, norm_w, conv_w, a_log, dt_bias, gdn_nw, final_nw, *slots = diff
    tgt, bgate0, ada_b, weights = const
    L = x.shape[0]
    tm = _row_tile(L)
    mods = mod.reshape(3, 1, D_MODEL)
    biases = ada_b.reshape(3, 1, D_MODEL)
    h, x1 = make_rowwise(_f_res_lnmod, "res0_ln1", tm, 2, 7)(
        (x, o), (gate0.reshape(1, D_MODEL), bgate0.reshape(1, D_MODEL), norm_w.reshape(1, D_MODEL), mods[1], mods[0], biases[1], biases[0]))
    q0, k0, v0, gz, ba = make_proj("gdn_in")(h, tuple(weights[0:5]), tuple(slots[0:5]))
    cw = jnp.concatenate([conv_w, jnp.zeros((SUBLANES - GDN_CONV, GDN_CONV_CH), F32)], axis=0)
    q = make_conv_act(lambda t: _l2n(_silu(t)) * (GDN_DK ** -0.5), "gdn_conv_q")(q0, cw[:, :GDN_QK])
    k = make_conv_act(lambda t: _l2n(_silu(t)), "gdn_conv_k")(k0, cw[:, GDN_QK:2 * GDN_QK])
    v = make_conv_act(_silu, "gdn_conv_v")(v0, cw[:, 2 * GDN_QK:])
    pad = jnp.zeros((LANES - 2 * GDN_HEADS,), F32)
    alog_row = jnp.concatenate([jnp.zeros((GDN_HEADS,), F32), a_log, pad]).reshape(1, LANES)
    dtb_row = jnp.concatenate([jnp.zeros((GDN_HEADS,), F32), dt_bias, pad]).reshape(1, LANES)
    (bg,) = make_rowwise(_f_betag, "gdn_bg", tm, 1, 2)((ba,), (alog_row, dtb_row))
    nw_row = jnp.tile(gdn_nw, GDN_HEADS).reshape(1, D_INNER)
    on = gdn_scan(*gdn_prep(q, k, v, bg), gz, nw_row)
    y = make_mm("gdn_out")(on, weights[5], slots[5])
    (lt,) = make_rowwise(_f_res_loss, "res1_loss", tm, 3, 3)((x1, y, tgt), (mods[2], biases[2], final_nw.reshape(1, D_MODEL)))
    return jnp.sum(lt)


VEC_NAMES = ("ada_b", "norm_w", "s5_lambda_re", "s5_lambda_im", "s5_log_dt", "s5_d", "gdn_a_log", "gdn_dt_bias", "final_norm_w")
MAT_NAMES = ("s5_b_re", "s5_b_im", "s5_c_re", "s5_c_im")
S5_BIG = ("s5_w_in", "s5_w_glu", "s5_w_out")
GDN_BIG = ("gdn_w_in", "gdn_w_out")
BIG_NAMES = S5_BIG + GDN_BIG
WEIGHT_ORDER = ("ada_w", "ada_b", "norm_w", "s5_w_in", "s5_lambda_re", "s5_lambda_im", "s5_log_dt", "s5_b_re", "s5_b_im",
                "s5_c_re", "s5_c_im", "s5_d", "s5_w_glu", "s5_w_out", "gdn_w_in", "gdn_conv_w", "gdn_a_log", "gdn_dt_bias",
                "gdn_norm_w", "gdn_w_out", "final_norm_w")


def _step(x, c, W, M, V, tgt):
    L = x.shape[1]
    ix, iy, ic = lax.axis_index("x"), lax.axis_index("y"), lax.axis_index("c")
    me = 4 * ix + 2 * iy + ic
    n_ada = W["ada_w"].shape[2]
    n_conv = W["gdn_conv_w"].shape[2]
    n_gnw = W["gdn_norm_w"].shape[1]

    g1 = _allgather_call(_pack([c, W["gdn_conv_w"], W["gdn_norm_w"]], SUBLANES), "gather_small_in", False)
    g1 = g1.reshape(N_DEV, -1)
    c_all = g1[:, :D_MODEL]
    conv_w = g1[:, D_MODEL:D_MODEL + GDN_CONV * n_conv].reshape(N_DEV, GDN_CONV, n_conv).transpose(1, 0, 2).reshape(GDN_CONV, -1)
    gdn_nw = g1[:, D_MODEL + GDN_CONV * n_conv:D_MODEL + GDN_CONV * n_conv + n_gnw].reshape(-1)
    mod_part = _ada_mod_call(c_all, W["ada_w"])
    g2 = _allgather_call(_pack([mod_part], SUBLANES), "gather_mod", False).reshape(N_DEV, -1)
    mod_all = g2[:, :2 * N_DEV * n_ada].reshape(N_DEV, 2, N_DEV, n_ada)
    mod_raw = lax.dynamic_index_in_dim(mod_all, me, axis=2, keepdims=False)
    mod_raw = mod_raw.transpose(1, 0, 2).reshape(2, 3 * D_MODEL)

    shard = lambda n: W[n][0].astype(BF16)
    (w_in5_parts,) = _gather_weights_call([shard("s5_w_in")], "gather_s5_w_in")
    late = _spread_start_call([shard("s5_w_glu"), shard("s5_w_out")], False, "gather_s5_late_start", w_in5_parts)
    g_send, g_recv, g_srcs, g_lands, g_token = _spread_start_call([shard(n) for n in GDN_BIG], False, "gather_gdn_start", late[4])
    w_in5 = _join_cols_call(w_in5_parts, "join_s5_w_in")
    slot = lambda *s: jnp.zeros(s, F32)
    two = 2 * D_MODEL
    diff_mix = (x[0], mod_raw[0, :two] + g_token[0, 0], W["norm_w"][0], W["s5_lambda_re"][0], W["s5_lambda_im"][0], W["s5_log_dt"][0],
                W["s5_b_re"][0], W["s5_b_im"][0], W["s5_c_re"][0], W["s5_c_im"][0], W["s5_d"][0],
                slot(D_MODEL, D_INNER), slot(D_MODEL, D_INNER))

    (xp, y2, z5), vjp_mix = jax.vjp(lambda d: _layer0_mix(d, (W["ada_b"][0, :two], (w_in5[:, :D_INNER], w_in5[:, D_INNER:]))), diff_mix)
    l_srcs, l_lands = _spread_wait_call(late[0], late[1], late[2], late[3], y2, "gather_s5_late_wait")
    w_glu, w_o5 = [lax.dynamic_update_slice(land, src[None], (me, 0, 0)).reshape(-1, src.shape[1]) for land, src in zip(l_lands, l_srcs)]
    diff_out = (y2, z5, slot(D_INNER, D_INNER), slot(D_INNER, D_MODEL))
    o5, vjp_out = jax.vjp(lambda d: _layer0_out(d, (w_glu, w_o5)), diff_out)
    g_srcs, g_lands = _spread_wait_call(g_send, g_recv, g_srcs, g_lands, o5, "gather_gdn_wait")
    gdn_full = [lax.dynamic_update_slice(land, src[None], (me, 0, 0)) for land, src in zip(g_lands, g_srcs)]
    w_ing = _join_cols_call(gdn_full[0], "join_gdn_w_in")
    w_ba = jnp.concatenate([w_ing[:, GDN_CONV_CH + D_INNER:], jnp.zeros((D_MODEL, LANES - 2 * GDN_HEADS), BF16)], axis=1)
    weights1 = (w_ing[:, :GDN_QK], w_ing[:, GDN_QK:2 * GDN_QK], w_ing[:, 2 * GDN_QK:GDN_CONV_CH],
                w_ing[:, GDN_CONV_CH:GDN_CONV_CH + D_INNER], w_ba, gdn_full[1].reshape(D_INNER, D_MODEL))
    slots1 = tuple(jnp.zeros(w.shape, F32) for w in weights1)
    diff1 = (xp, o5, mod_raw[0, two:], mod_raw[1], W["norm_w"][1], conv_w, W["gdn_a_log"][0], W["gdn_dt_bias"][0], gdn_nw,
             W["final_norm_w"], *slots1)
    loss_local, vjp1 = jax.vjp(lambda d: _layer1_loss(d, (tgt[0], W["ada_b"][0, two:], W["ada_b"][1], weights1)), diff1)
    ((dxp, do5, dmod_gate, dmod1, d_norm_w1, d_conv, d_alog, d_dtb, d_gnw, d_fnw, d_wq, d_wk, d_wv, d_wgz, d_wba, d_wog),) = vjp1(
        jnp.ones((), F32))
    loss = lax.psum(loss_local, MESH_AXES)

    rows = lambda d: d.reshape(N_DEV, d.shape[0] // N_DEV, d.shape[1])
    d_ing = _split_cols_call(jnp.concatenate([d_wq, d_wk, d_wv, d_wgz, d_wba[:, :2 * GDN_HEADS]], axis=1), "split_gdn_w_in", BF16)
    s_send, s_recv, s_srcs, s_lands, s_token = _spread_start_call([d_ing, rows(d_wog).astype(BF16)], True, "scatter_gdn_start", dxp)
    ((dy2, dz5, d_wglu, d_wo5),) = vjp_out(do5.at[0, 0].add(s_token[0, 0]))
    t_send, t_recv, t_srcs, t_lands, t_token = _spread_start_call(
        [rows(d_wglu).astype(BF16), rows(d_wo5).astype(BF16)], True, "scatter_s5_late_start", dy2)
    ((dx, dmod_ss, d_norm_w0, d_lre, d_lim, d_logdt, d_bre, d_bim, d_cre, d_cim, d_s5d, d_wu, d_wz),) = vjp_mix(
        (dxp.at[0, 0].add(t_token[0, 0]), dy2, dz5))
    dmod = jnp.stack([jnp.concatenate([dmod_ss, dmod_gate]), dmod1])
    d_norm_w = jnp.stack([d_norm_w0, d_norm_w1])
    d_in5 = _split_cols_call(jnp.concatenate([d_wu, d_wz], axis=1), "split_s5_w_in", F32)
    (got,) = _pair_exchange_call([d_in5], "scatter_s5_pair")
    core = jnp.reshape(ic, (1,)).astype(jnp.int32)
    chip = jnp.reshape(2 * ix + iy, (1,)).astype(jnp.int32)
    pair = _pair_sum_call(d_in5, got, core, "pair_sum_s5_w_in")
    (recv,) = _chip_exchange_call([pair], "scatter_s5_chips")
    big = {"s5_w_in": _adam_own_call(pair, chip, recv, W["s5_w_in"][0], M["s5_w_in"][0], V["s5_w_in"][0], "adam_s5_w_in", 128)}
    t_srcs, t_lands = _spread_wait_call(t_send, t_recv, t_srcs, t_lands, dx, "scatter_s5_late_wait")
    s_srcs, s_lands = _spread_wait_call(s_send, s_recv, s_srcs, s_lands, t_lands[0], "scatter_gdn_wait")
    for land, src, n in zip(tuple(t_lands) + tuple(s_lands), tuple(t_srcs) + tuple(s_srcs), ("s5_w_glu", "s5_w_out") + GDN_BIG):
        mine = lax.dynamic_index_in_dim(src, me, 0, keepdims=True)
        parts = lax.dynamic_update_slice(land, mine, (me, 0, 0))
        big[n] = _adam_call(parts, W[n][0], M[n][0], V[n][0], "adam_" + n, rows=_tile(W[n].shape[1], 128))
    big = [[o[None] for o in big[n]] for n in BIG_NAMES]

    vec_parts = [dmod, d_norm_w, d_lre, d_lim, d_logdt, d_s5d, d_alog, d_dtb, d_fnw]
    tail_parts = [d_conv, d_gnw]
    mat_parts = [d_bre, d_bim, d_cre, d_cim]
    n_vec = sum(math.prod(p.shape) for p in vec_parts)
    sg_vec, sg_mat = _gather_weights_call(
        [_pack(vec_parts + tail_parts, ADAM_ROWS), _pack(mat_parts, SUBLANES).astype(BF16)], "gather_small_grads")
    tot_vec = _sum_call(sg_vec, "sum_vec_grads", ADAM_ROWS)
    tot_mat = _sum_call(sg_mat, "sum_mat_grads", ADAM_ROWS)
    g_conv, g_gnw = _unpack(tot_vec.reshape(-1)[n_vec:], [d_conv.shape, d_gnw.shape])
    g_conv_mine = lax.dynamic_slice_in_dim(g_conv, me * n_conv, n_conv, axis=1)
    g_gnw_mine = lax.dynamic_slice_in_dim(g_gnw, me * n_gnw, n_gnw, axis=0)
    vec_names = VEC_NAMES + ("gdn_conv_w", "gdn_norm_w")
    vec_g = _pack([tot_vec.reshape(-1)[:n_vec], g_conv_mine, g_gnw_mine], ADAM_ROWS)
    vec = _adam_call(vec_g[None], _pack([W[n] for n in vec_names], ADAM_ROWS), _pack([M[n] for n in vec_names], ADAM_ROWS),
                     _pack([V[n] for n in vec_names], ADAM_ROWS), "adam_vec")
    vec = [_unpack(b, [W[n].shape for n in vec_names]) for b in vec]
    mats = []
    for name, g_mat in zip(MAT_NAMES, _unpack(tot_mat, [p.shape for p in mat_parts])):
        two_d = (-1, W[name].shape[-1])
        outs = _adam_call(g_mat.reshape(two_d)[None], W[name].reshape(two_d), M[name].reshape(two_d), V[name].reshape(two_d),
                          "adam_" + name, rows=1024)
        mats.append([o.reshape(W[name].shape) for o in outs])

    dmod_all = sg_vec[:, :2 * 3 * D_MODEL // LANES].reshape(N_DEV, 2, N_DEV, n_ada // LANES, LANES)
    dmod_mine = lax.dynamic_index_in_dim(dmod_all, me, axis=2, keepdims=False).transpose(1, 0, 2, 3).reshape(2, N_DEV, n_ada)
    g_ada_w = _ada_grad_call(c_all, dmod_mine)
    ada = _adam_call(g_ada_w.reshape(1, -1, LANES), W["ada_w"].reshape(-1, LANES), M["ada_w"].reshape(-1, LANES),
                     V["ada_w"].reshape(-1, LANES), "adam_ada")
    ada = [a.reshape(W["ada_w"].shape) for a in ada]

    res = {}
    for i, n in enumerate(BIG_NAMES):
        res[n] = big[i]
    for i, n in enumerate(vec_names):
        res[n] = [b[i] for b in vec]
    for i, n in enumerate(MAT_NAMES):
        res[n] = mats[i]
    res["ada_w"] = ada
    outs = [loss, dx[None]]
    for j in range(4):
        outs += [res[n][j] for n in WEIGHT_ORDER]
    return tuple(outs)


def kernel(x, c, ada_w, ada_b, norm_w, s5_w_in, s5_lambda_re, s5_lambda_im, s5_log_dt, s5_b_re, s5_b_im, s5_c_re, s5_c_im, s5_d, s5_w_glu, s5_w_out, gdn_w_in, gdn_conv_w, gdn_a_log, gdn_dt_bias, gdn_norm_w, gdn_w_out, final_norm_w, loss_target, m_ada_w, m_ada_b, m_norm_w, m_s5_w_in, m_s5_lambda_re, m_s5_lambda_im, m_s5_log_dt, m_s5_b_re, m_s5_b_im, m_s5_c_re, m_s5_c_im, m_s5_d, m_s5_w_glu, m_s5_w_out, m_gdn_w_in, m_gdn_conv_w, m_gdn_a_log, m_gdn_dt_bias, m_gdn_norm_w, m_gdn_w_out, m_final_norm_w, v_ada_w, v_ada_b, v_norm_w, v_s5_w_in, v_s5_lambda_re, v_s5_lambda_im, v_s5_log_dt, v_s5_b_re, v_s5_b_im, v_s5_c_re, v_s5_c_im, v_s5_d, v_s5_w_glu, v_s5_w_out, v_gdn_w_in, v_gdn_conv_w, v_gdn_a_log, v_gdn_dt_bias, v_gdn_norm_w, v_gdn_w_out, v_final_norm_w):
    W = dict(ada_w=ada_w, ada_b=ada_b, norm_w=norm_w, s5_w_in=s5_w_in, s5_lambda_re=s5_lambda_re, s5_lambda_im=s5_lambda_im,
             s5_log_dt=s5_log_dt, s5_b_re=s5_b_re, s5_b_im=s5_b_im, s5_c_re=s5_c_re, s5_c_im=s5_c_im, s5_d=s5_d,
             s5_w_glu=s5_w_glu, s5_w_out=s5_w_out, gdn_w_in=gdn_w_in, gdn_conv_w=gdn_conv_w, gdn_a_log=gdn_a_log,
             gdn_dt_bias=gdn_dt_bias, gdn_norm_w=gdn_norm_w, gdn_w_out=gdn_w_out, final_norm_w=final_norm_w)
    M = dict(ada_w=m_ada_w, ada_b=m_ada_b, norm_w=m_norm_w, s5_w_in=m_s5_w_in, s5_lambda_re=m_s5_lambda_re,
             s5_lambda_im=m_s5_lambda_im, s5_log_dt=m_s5_log_dt, s5_b_re=m_s5_b_re, s5_b_im=m_s5_b_im, s5_c_re=m_s5_c_re,
             s5_c_im=m_s5_c_im, s5_d=m_s5_d, s5_w_glu=m_s5_w_glu, s5_w_out=m_s5_w_out, gdn_w_in=m_gdn_w_in,
             gdn_conv_w=m_gdn_conv_w, gdn_a_log=m_gdn_a_log, gdn_dt_bias=m_gdn_dt_bias, gdn_norm_w=m_gdn_norm_w,
             gdn_w_out=m_gdn_w_out, final_norm_w=m_final_norm_w)
    V = dict(ada_w=v_ada_w, ada_b=v_ada_b, norm_w=v_norm_w, s5_w_in=v_s5_w_in, s5_lambda_re=v_s5_lambda_re,
             s5_lambda_im=v_s5_lambda_im, s5_log_dt=v_s5_log_dt, s5_b_re=v_s5_b_re, s5_b_im=v_s5_b_im, s5_c_re=v_s5_c_re,
             s5_c_im=v_s5_c_im, s5_d=v_s5_d, s5_w_glu=v_s5_w_glu, s5_w_out=v_s5_w_out, gdn_w_in=v_gdn_w_in,
             gdn_conv_w=v_gdn_conv_w, gdn_a_log=v_gdn_a_log, gdn_dt_bias=v_gdn_dt_bias, gdn_norm_w=v_gdn_norm_w,
             gdn_w_out=v_gdn_w_out, final_norm_w=v_final_norm_w)
    return _step(x, c, W, M, V, loss_target)
```

```python
import functools
import math

import jax
import jax.numpy as jnp
from jax import lax
from jax.experimental import pallas as pl
from jax.experimental.pallas import tpu as pltpu

F32 = jnp.float32
BF16 = jnp.bfloat16
SDS = jax.ShapeDtypeStruct

D_MODEL = 1024
D_INNER = 2048
NORM_EPS = 1e-6
S5_GROUP = 16
S5_GROUPS = 128
S5_STATE = 64
GDN_HEADS = 8
GDN_DK = 128
GDN_DV = 256
GDN_CONV = 4
GDN_CHUNK = 64
GDN_QK = 1024
GDN_CONV_CH = 4096
GDN_PROJ = 6160
ADAM_LR = 0.001
ADAM_B1 = 0.9
ADAM_B2 = 0.999
ADAM_EPS = 1e-08
ADAM_WD = 0.01
ADAM_STEP = 10

N_DEV = 8
LANES = 128
SUBLANES = 8
VMEM_BIG = 56 << 20
VMEM_MID = 40 << 20
S5_GB = 8
S5_TL = 1024
MESH_AXES = ("x", "y", "c")


def _params(sem, vmem=None):
    return pltpu.CompilerParams(dimension_semantics=sem, vmem_limit_bytes=vmem)


def _bdot(a, b, dims=(((1,), (0,)), ((), ()))):
    return lax.dot_general(a.astype(BF16), b.astype(BF16), dims, preferred_element_type=F32)


def _hdot(a, b, dims=(((1,), (0,)), ((), ()))):
    return lax.dot_general(a, b, dims, preferred_element_type=F32, precision=lax.Precision.HIGHEST)


_BNN = (((2,), (1,)), ((0,), (0,)))
_BNT = (((2,), (2,)), ((0,), (0,)))
_BTN = (((1,), (1,)), ((0,), (0,)))


@jax.custom_vjp
def _unit_lower_inverse(a):
    c = a.shape[-1]
    ri = lax.broadcasted_iota(jnp.int32, a.shape, 1)
    ci = lax.broadcasted_iota(jnp.int32, a.shape, 2)
    n = -a
    t = (ri == ci).astype(F32) + n
    for _ in range(int(math.log2(c)) - 1):
        n = _hdot(n, n, _BNN)
        t = t + _hdot(t, n, _BNN)
    return t


def _unit_lower_inverse_fwd(a):
    t = _unit_lower_inverse(a)
    return t, t


def _unit_lower_inverse_bwd(t, g):
    return (-_hdot(_hdot(t, g, _BTN), t, _BNT),)


_unit_lower_inverse.defvjp(_unit_lower_inverse_fwd, _unit_lower_inverse_bwd)


NN = (((1,), (0,)), ((), ()))
NT = (((1,), (1,)), ((), ()))
TN = (((0,), (0,)), ((), ()))


def _tile(n, pref):
    for t in (pref, 512, 256, 128):
        if t <= n and n % t == 0:
            return t
    return n


def _matmul(a, b, mode, name, add=None):
    if mode == "nn":
        (m, k), (_, n) = a.shape, b.shape
    elif mode == "nt":
        (m, k), (n, _) = a.shape, b.shape
    else:
        (k, m), (_, n) = a.shape, b.shape
    tm, tn, tk = _tile(m, 1024), _tile(n, 512), (k if k <= 2048 else _tile(k, 512))
    if mode == "tn":
        tm, tn, tk = _tile(m, 1024), _tile(n, 1024), _tile(k, 1024)
    nk = k // tk
    dims = {"nn": NN, "nt": NT, "tn": TN}[mode]

    def body(a_ref, b_ref, *rest):
        o_ref, acc_ref = rest[-2], rest[-1]
        part = _bdot(a_ref[...], b_ref[...], dims)
        if nk == 1:
            o_ref[...] = part if add is None else part + rest[0][...]
            return
        kk = pl.program_id(2)

        @pl.when(kk == 0)
        def _():
            acc_ref[...] = part if add is None else part + rest[0][...]

        @pl.when(kk > 0)
        def _():
            acc_ref[...] += part

        @pl.when(kk == nk - 1)
        def _():
            o_ref[...] = acc_ref[...]

    a_spec = pl.BlockSpec((tk, tm), lambda i, j, q: (q, i)) if mode == "tn" else pl.BlockSpec((tm, tk), lambda i, j, q: (i, q))
    b_spec = pl.BlockSpec((tn, tk), lambda i, j, q: (j, q)) if mode == "nt" else pl.BlockSpec((tk, tn), lambda i, j, q: (q, j))
    o_spec = pl.BlockSpec((tm, tn), lambda i, j, q: (i, j))
    return pl.pallas_call(
        body, name=name, grid=(m // tm, n // tn, nk),
        in_specs=[a_spec, b_spec] + ([] if add is None else [o_spec]), out_specs=o_spec,
        out_shape=SDS((m, n), F32), scratch_shapes=[pltpu.VMEM((tm, tn), F32)],
        compiler_params=_params(("parallel", "parallel", "arbitrary"), VMEM_MID),
    )(a, b, *([] if add is None else [add]))


def make_mm(name, pass_input=False):
    def primal(a, w):
        out = _matmul(a, w, "nn", name + "_fwd")
        return (out, a) if pass_input else out

    @jax.custom_vjp
    def mm(a, w, grad_slot):
        return primal(a, w)

    def fwd(a, w, grad_slot):
        return primal(a, w), (a, w)

    def bwd(res, g):
        a, w = res
        g, g_other = g if pass_input else (g, None)
        return _matmul(g, w, "nt", name + "_dx", add=g_other), jnp.zeros_like(w), _matmul(a, g, "tn", name + "_dw")

    mm.defvjp(fwd, bwd)
    return mm


PROJ_ROWS = 256


def _proj_fwd_call(a, ws, name):
    m, k = a.shape
    tm = _tile(m, PROJ_ROWS)
    nw = len(ws)

    def body(*refs):
        ab = refs[0][...].astype(BF16)
        for w_ref, o_ref in zip(refs[1:1 + nw], refs[1 + nw:]):
            o_ref[...] = lax.dot_general(ab, w_ref[...], NN, preferred_element_type=F32)

    return pl.pallas_call(
        body, name=name, grid=(m // tm,),
        in_specs=[pl.BlockSpec((tm, k), lambda i: (i, 0))] + [pl.BlockSpec(w.shape, lambda i: (0, 0)) for w in ws],
        out_specs=[pl.BlockSpec((tm, w.shape[1]), lambda i: (i, 0)) for w in ws],
        out_shape=[SDS((m, w.shape[1]), F32) for w in ws],
        compiler_params=_params(("parallel",), VMEM_BIG),
    )(a, *ws)


def _proj_dx_call(gs, ws, name):
    m = gs[0].shape[0]
    k = ws[0].shape[0]
    tm = _tile(m, PROJ_ROWS)
    nw = len(ws)

    def body(*refs):
        acc = None
        for g_ref, w_ref in zip(refs[:nw], refs[nw:2 * nw]):
            part = _bdot(g_ref[...], w_ref[...], NT)
            acc = part if acc is None else acc + part
        refs[2 * nw][...] = acc

    return pl.pallas_call(
        body, name=name, grid=(m // tm,),
        in_specs=[pl.BlockSpec((tm, g.shape[1]), lambda i: (i, 0)) for g in gs] + [pl.BlockSpec(w.shape, lambda i: (0, 0)) for w in ws],
        out_specs=pl.BlockSpec((tm, k), lambda i: (i, 0)), out_shape=SDS((m, k), F32),
        compiler_params=_params(("parallel",), VMEM_BIG),
    )(*gs, *ws)


def make_proj(name):
    @jax.custom_vjp
    def proj(a, ws, grad_slots):
        return tuple(_proj_fwd_call(a, ws, name + "_fwd"))

    def fwd(a, ws, grad_slots):
        return tuple(_proj_fwd_call(a, ws, name + "_fwd")), (a, ws)

    def bwd(res, gs):
        a, ws = res
        dws = tuple(_matmul(a, g, "tn", "%s_dw%d" % (name, i)) for i, g in enumerate(gs))
        return _proj_dx_call(tuple(gs), ws, name + "_dx"), tuple(jnp.zeros_like(w) for w in ws), dws

    proj.defvjp(fwd, bwd)
    return proj


def make_rowwise(f, name, tm, n_rows, n_params, vmem=VMEM_MID, pass_first=False):
    def specs_of(arrs, blocked):
        if blocked:
            return [pl.BlockSpec((tm, a.shape[1]), lambda i: (i, 0)) for a in arrs]
        return [pl.BlockSpec(a.shape, lambda i: (0, 0)) for a in arrs]

    def out_structs(rows, params):
        blk = [SDS((tm, r.shape[1]), r.dtype) for r in rows] + [SDS(p.shape, p.dtype) for p in params]
        return jax.eval_shape(f, *blk)

    def run_fwd(rows, params):
        L = rows[0].shape[0]
        outs = out_structs(rows, params)

        def body(*refs):
            ins = [r[...] for r in refs[:n_rows + n_params]]
            res = f(*ins)
            for o_ref, val in zip(refs[n_rows + n_params:], res):
                o_ref[...] = val

        return pl.pallas_call(
            body, name=name + "_fwd", grid=(L // tm,),
            in_specs=specs_of(rows, True) + specs_of(params, False),
            out_specs=[pl.BlockSpec((tm, o.shape[1]), lambda i: (i, 0)) for o in outs],
            out_shape=[SDS((L, o.shape[1]), o.dtype) for o in outs],
            compiler_params=_params(("parallel",), vmem),
        )(*rows, *params)

    def run_bwd(rows, params, gs):
        L = rows[0].shape[0]
        n_g = len(gs)

        def body(*refs):
            i = pl.program_id(0)
            ins = [r[...] for r in refs[:n_rows + n_params]]
            cts = tuple(r[...] for r in refs[n_rows + n_params:n_rows + n_params + n_g])
            outs = refs[n_rows + n_params + n_g:]
            _, vjp = jax.vjp(f, *ins)
            grads = vjp(cts[:-1] if pass_first else cts)
            if pass_first:
                grads = (grads[0] + cts[-1],) + tuple(grads[1:])
            for o_ref, val in zip(outs[:n_rows], grads[:n_rows]):
                o_ref[...] = val

            if n_params:
                @pl.when(i == 0)
                def _():
                    for o_ref in outs[n_rows:]:
                        o_ref[...] = jnp.zeros_like(o_ref)
                for o_ref, val in zip(outs[n_rows:], grads[n_rows:]):
                    o_ref[...] += val

        res = pl.pallas_call(
            body, name=name + "_bwd", grid=(L // tm,),
            in_specs=specs_of(rows, True) + specs_of(params, False) + specs_of(gs, True),
            out_specs=specs_of(rows, True) + specs_of(params, False),
            out_shape=[SDS(r.shape, r.dtype) for r in rows] + [SDS(p.shape, p.dtype) for p in params],
            compiler_params=_params(("arbitrary",), vmem),
        )(*rows, *params, *gs)
        return tuple(res[:n_rows]), tuple(res[n_rows:])

    def outputs(rows, params):
        outs = tuple(run_fwd(rows, params))
        return outs + (rows[0],) if pass_first else outs

    @jax.custom_vjp
    def op(rows, params):
        return outputs(rows, params)

    def fwd(rows, params):
        return outputs(rows, params), (rows, params)

    def bwd(res, gs):
        rows, params = res
        return run_bwd(rows, params, tuple(gs))

    op.defvjp(fwd, bwd)
    op.run_fwd, op.run_bwd = run_fwd, run_bwd
    return op


def make_residual(name, tm):
    full = make_rowwise(_f_res, name, tm, 2, 2)
    branch = make_rowwise(lambda y, gate, bgate: ((gate + bgate) * y,), name + "_branch", tm, 1, 2)

    @jax.custom_vjp
    def op(x, y, gate, bgate):
        return full.run_fwd((x, y), (gate, bgate))[0]

    def fwd(x, y, gate, bgate):
        return full.run_fwd((x, y), (gate, bgate))[0], (y, gate, bgate)

    def bwd(res, g):
        y, gate, bgate = res
        (dy,), (dgate, dbgate) = branch.run_bwd((y,), (gate, bgate), (g,))
        return g, dy, dgate, dbgate

    op.defvjp(fwd, bwd)
    return op


def _s5_scan_rows(xr_ref, xi_ref, ar, ai, x0r, x0i, tl, reverse=False):
    n = xr_ref.shape[1]
    T = SUBLANES
    row = lax.broadcasted_iota(jnp.int32, (T, n), 0)
    pr, pi = [ar], [ai]
    for _ in range(T - 1):
        pr, pi = pr + [pr[-1] * ar - pi[-1] * ai], pi + [pr[-1] * ai + pi[-1] * ar]
    levels = []
    for d in (1, 2, 4):
        mask = (row < T - d) if reverse else (row >= d)
        levels.append((T - d if reverse else d, jnp.where(mask, pr[d - 1], 0.0), jnp.where(mask, pi[d - 1], 0.0)))
    cr = jnp.zeros((T, n), F32)
    ci = jnp.zeros((T, n), F32)
    for r in range(T):
        k = (T - r) if reverse else (r + 1)
        cr = jnp.where(row == r, pr[k - 1], cr)
        ci = jnp.where(row == r, pi[k - 1], ci)
    nt = tl // T
    last = 0 if reverse else T - 1

    def step(t, carry):
        sr, si = carry
        base = pl.multiple_of((nt - 1 - t if reverse else t) * T, T)
        br = xr_ref[pl.ds(base, T), :]
        bi = xi_ref[pl.ds(base, T), :]
        for shift, mr, mi in levels:
            qr = pltpu.roll(br, shift, 0)
            qi = pltpu.roll(bi, shift, 0)
            br, bi = br + (mr * qr - mi * qi), bi + (mr * qi + mi * qr)
        xr = br + (cr * sr - ci * si)
        xi = bi + (cr * si + ci * sr)
        xr_ref[pl.ds(base, T), :] = xr
        xi_ref[pl.ds(base, T), :] = xi
        return xr[last:last + 1, :], xi[last:last + 1, :]
    return lax.fori_loop(0, nt, step, (x0r, x0i))


def _s5_fwd_call(u, bre, bim, cre, cim, a, d, tl):
    L, e = u.shape
    nb = e // LANES
    ns = bre.shape[2]
    nc = L // tl

    def body(u_ref, bre_ref, bim_ref, cre_ref, cim_ref, a_ref, d_ref, y_ref, xb_ref, sr_ref, si_ref, xr_ref, xi_ref, carry_ref):
        c = pl.program_id(1)

        @pl.when(c == 0)
        def _():
            carry_ref[...] = jnp.zeros_like(carry_ref)
        xb_ref[0, 0] = carry_ref[...]
        ub = u_ref[...]
        xr_ref[...] = _bdot(ub, bre_ref[0])
        xi_ref[...] = _bdot(ub, bim_ref[0])
        ar = a_ref[0, 0:1, :]
        ai = a_ref[0, 1:2, :]
        xr, xi = _s5_scan_rows(xr_ref, xi_ref, ar, ai, carry_ref[0:1, :], carry_ref[1:2, :], tl)
        carry_ref[0:1, :] = xr
        carry_ref[1:2, :] = xi
        sr = xr_ref[...].astype(BF16)
        si = xi_ref[...].astype(BF16)
        sr_ref[...] = sr
        si_ref[...] = si
        y_ref[...] = _f_s5_act(_bdot(sr, cre_ref[0]) - _bdot(si, cim_ref[0]), ub, d_ref[...])[0]

    return pl.pallas_call(
        body, name="s5_core_fwd", grid=(nb, nc),
        in_specs=[pl.BlockSpec((tl, LANES), lambda j, c: (c, j)),
                  pl.BlockSpec((1, LANES, ns), lambda j, c: (j, 0, 0)), pl.BlockSpec((1, LANES, ns), lambda j, c: (j, 0, 0)),
                  pl.BlockSpec((1, ns, LANES), lambda j, c: (j, 0, 0)), pl.BlockSpec((1, ns, LANES), lambda j, c: (j, 0, 0)),
                  pl.BlockSpec((1, SUBLANES, ns), lambda j, c: (j, 0, 0)), pl.BlockSpec((1, LANES), lambda j, c: (0, j))],
        out_specs=[pl.BlockSpec((tl, LANES), lambda j, c: (c, j)),
                   pl.BlockSpec((1, 1, SUBLANES, ns), lambda j, c: (j, c, 0, 0)),
                   pl.BlockSpec((tl, ns), lambda j, c: (c, j)), pl.BlockSpec((tl, ns), lambda j, c: (c, j))],
        out_shape=[SDS((L, e), F32), SDS((nb, nc, SUBLANES, ns), F32), SDS((L, nb * ns), BF16), SDS((L, nb * ns), BF16)],
        scratch_shapes=[pltpu.VMEM((tl, ns), F32), pltpu.VMEM((tl, ns), F32), pltpu.VMEM((SUBLANES, ns), F32)],
        compiler_params=_params(("arbitrary", "arbitrary"), VMEM_MID),
    )(u, bre, bim, cre, cim, a, d)


def _s5_bwd_call(u, dy2, bre, bim, cre, cim, a, d, xb, sr, si, tl):
    L, e = u.shape
    nb = e // LANES
    ns = bre.shape[2]
    nc = L // tl

    def body(u_ref, dy2_ref, bre_ref, bim_ref, cre_ref, cim_ref, a_ref, d_ref, xb_ref, sr_ref, si_ref,
             du_ref, dbre_ref, dbim_ref, dcre_ref, dcim_ref, da_ref, dd_ref,
             gr_ref, gi_ref, gcarry_ref):
        c = pl.program_id(1)

        @pl.when(c == 0)
        def _():
            gcarry_ref[...] = jnp.zeros_like(gcarry_ref)
            dbre_ref[...] = jnp.zeros_like(dbre_ref)
            dbim_ref[...] = jnp.zeros_like(dbim_ref)
            dcre_ref[...] = jnp.zeros_like(dcre_ref)
            dcim_ref[...] = jnp.zeros_like(dcim_ref)
            da_ref[...] = jnp.zeros_like(da_ref)
            dd_ref[...] = jnp.zeros_like(dd_ref)

        ub = u_ref[...]
        ys = _bdot(sr_ref[...], cre_ref[0]) - _bdot(si_ref[...], cim_ref[0])
        _, act_vjp = jax.vjp(lambda *t: _f_s5_act(*t)[0], ys, ub, d_ref[...])
        dy, du_skip, dd = act_vjp(dy2_ref[...])
        dd_ref[...] += dd
        ar = a_ref[0, 0:1, :]
        ai = a_ref[0, 1:2, :]
        x0r = xb_ref[0, 0, 0:1, :]
        x0i = xb_ref[0, 0, 1:2, :]
        dcre_ref[0] += _bdot(sr_ref[...], dy, TN)
        dcim_ref[0] -= _bdot(si_ref[...], dy, TN)
        gr_ref[...] = _bdot(dy, cre_ref[0], NT)
        gi_ref[...] = -_bdot(dy, cim_ref[0], NT)

        g0r, g0i = _s5_scan_rows(gr_ref, gi_ref, ar, -ai, gcarry_ref[0:1, :], gcarry_ref[1:2, :], tl, reverse=True)
        gcarry_ref[0:1, :] = g0r
        gcarry_ref[1:2, :] = g0i
        row = lax.broadcasted_iota(jnp.int32, (tl, ns), 0)
        gr = gr_ref[...]
        gi = gi_ref[...]
        xpr = jnp.where(row == 0, x0r, pltpu.roll(sr_ref[...].astype(F32), 1, 0))
        xpi = jnp.where(row == 0, x0i, pltpu.roll(si_ref[...].astype(F32), 1, 0))
        da_ref[0, 0:1, :] += jnp.sum(gr * xpr + gi * xpi, axis=0, keepdims=True)
        da_ref[0, 1:2, :] += jnp.sum(gi * xpr - gr * xpi, axis=0, keepdims=True)
        du_ref[...] = (_bdot(gr, bre_ref[0], NT) + _bdot(gi, bim_ref[0], NT)) + du_skip
        dbre_ref[0] += _bdot(ub, gr, TN)
        dbim_ref[0] += _bdot(ub, gi, TN)

    rev = lambda c: nc - 1 - c
    return pl.pallas_call(
        body, name="s5_core_bwd", grid=(nb, nc),
        in_specs=[pl.BlockSpec((tl, LANES), lambda j, c: (rev(c), j)), pl.BlockSpec((tl, LANES), lambda j, c: (rev(c), j)),
                  pl.BlockSpec((1, LANES, ns), lambda j, c: (j, 0, 0)), pl.BlockSpec((1, LANES, ns), lambda j, c: (j, 0, 0)),
                  pl.BlockSpec((1, ns, LANES), lambda j, c: (j, 0, 0)), pl.BlockSpec((1, ns, LANES), lambda j, c: (j, 0, 0)),
                  pl.BlockSpec((1, SUBLANES, ns), lambda j, c: (j, 0, 0)), pl.BlockSpec((1, LANES), lambda j, c: (0, j)),
                  pl.BlockSpec((1, 1, SUBLANES, ns), lambda j, c: (j, rev(c), 0, 0)),
                  pl.BlockSpec((tl, ns), lambda j, c: (rev(c), j)), pl.BlockSpec((tl, ns), lambda j, c: (rev(c), j))],
        out_specs=[pl.BlockSpec((tl, LANES), lambda j, c: (rev(c), j)),
                   pl.BlockSpec((1, LANES, ns), lambda j, c: (j, 0, 0)), pl.BlockSpec((1, LANES, ns), lambda j, c: (j, 0, 0)),
                   pl.BlockSpec((1, ns, LANES), lambda j, c: (j, 0, 0)), pl.BlockSpec((1, ns, LANES), lambda j, c: (j, 0, 0)),
                   pl.BlockSpec((1, SUBLANES, ns), lambda j, c: (j, 0, 0)), pl.BlockSpec((1, LANES), lambda j, c: (0, j))],
        out_shape=[SDS((L, e), F32), SDS(bre.shape, F32), SDS(bim.shape, F32), SDS(cre.shape, F32), SDS(cim.shape, F32),
                   SDS(a.shape, F32), SDS(d.shape, F32)],
        scratch_shapes=[pltpu.VMEM((tl, ns), F32) for _ in range(2)] + [pltpu.VMEM((SUBLANES, ns), F32)],
        compiler_params=_params(("arbitrary", "arbitrary"), VMEM_MID),
    )(u, dy2, bre, bim, cre, cim, a, d, xb, sr, si)


def make_s5_core(tl):
    @jax.custom_vjp
    def s5_core(u, bre, bim, cre, cim, a, d):
        return _s5_fwd_call(u, bre, bim, cre, cim, a, d, tl)[0]

    def fwd(u, bre, bim, cre, cim, a, d):
        y2, xb, sr, si = _s5_fwd_call(u, bre, bim, cre, cim, a, d, tl)
        return y2, (u, bre, bim, cre, cim, a, d, xb, sr, si)

    def bwd(res, dy2):
        u, bre, bim, cre, cim, a, d, xb, sr, si = res
        return tuple(_s5_bwd_call(u, dy2, bre, bim, cre, cim, a, d, xb, sr, si, tl))

    s5_core.defvjp(fwd, bwd)
    return s5_core


def _s5_block_params(lam_re, lam_im, log_dt, b_re, b_im, c_re, c_im):
    dt = jnp.exp(log_dt)[:, None]
    mag = jnp.exp(lam_re * dt)
    ab_re = mag * jnp.cos(lam_im * dt)
    ab_im = mag * jnp.sin(lam_im * dt)
    den = lam_re * lam_re + lam_im * lam_im
    nr = ab_re - 1.0
    ni = ab_im
    q_re = (nr * lam_re + ni * lam_im) / den
    q_im = (ni * lam_re - nr * lam_im) / den
    bb_re = q_re[..., None] * b_re - q_im[..., None] * b_im
    bb_im = q_re[..., None] * b_im + q_im[..., None] * b_re
    nb = S5_GROUPS // S5_GB
    eye = jnp.eye(S5_GB, dtype=F32)

    def bdiag_in(bb):
        t = bb.reshape(nb, S5_GB, S5_STATE, S5_GROUP)
        t = jnp.einsum("jgpm,gh->jgmhp", t, eye)
        return t.reshape(nb, S5_GB * S5_GROUP, S5_GB * S5_STATE)

    def bdiag_out(cc):
        t = cc.reshape(nb, S5_GB, S5_GROUP, S5_STATE)
        t = jnp.einsum("jgmp,gh->jgphm", t, eye)
        return t.reshape(nb, S5_GB * S5_STATE, S5_GB * S5_GROUP)

    a = jnp.stack([ab_re.reshape(nb, S5_GB * S5_STATE), ab_im.reshape(nb, S5_GB * S5_STATE)], axis=1)
    a = jnp.concatenate([a, jnp.zeros((nb, SUBLANES - 2, S5_GB * S5_STATE), F32)], axis=1)
    return bdiag_in(bb_re), bdiag_in(bb_im), bdiag_out(c_re), bdiag_out(c_im), a


def _shift_down(x, s, row):
    if s == 0:
        return x
    return jnp.where(row >= s, pltpu.roll(x, s, 0), 0.0)


def _shift_up(x, s, row, n):
    if s == 0:
        return x
    return jnp.where(row < n - s, pltpu.roll(x, n - s, 0), 0.0)


def _causal_conv(xv, w_ref, row):
    acc = jnp.zeros_like(xv)
    for j in range(GDN_CONV):
        acc += w_ref[j:j + 1, :] * _shift_down(xv, GDN_CONV - 1 - j, row)
    return acc


def _conv_fwd_call(x, w, act, name):
    L, ch = x.shape

    def body(x_ref, w_ref, y_ref):
        xv = x_ref[...]
        row = lax.broadcasted_iota(jnp.int32, xv.shape, 0)
        y_ref[...] = act(_causal_conv(xv, w_ref, row))

    return pl.pallas_call(
        body, name=name + "_fwd", grid=(ch // LANES,),
        in_specs=[pl.BlockSpec((L, LANES), lambda j: (0, j)), pl.BlockSpec((SUBLANES, LANES), lambda j: (0, j))],
        out_specs=pl.BlockSpec((L, LANES), lambda j: (0, j)), out_shape=SDS((L, ch), F32),
        compiler_params=_params(("parallel",), VMEM_MID),
    )(x, w)


def _conv_bwd_call(x, w, dy, act, name):
    L, ch = x.shape

    def body(x_ref, w_ref, dy_ref, dx_ref, dw_ref):
        xv = x_ref[...]
        row = lax.broadcasted_iota(jnp.int32, xv.shape, 0)
        _, act_vjp = jax.vjp(act, _causal_conv(xv, w_ref, row))
        (g,) = act_vjp(dy_ref[...])
        acc = jnp.zeros_like(xv)
        dws = []
        for j in range(GDN_CONV):
            s = GDN_CONV - 1 - j
            acc += w_ref[j:j + 1, :] * _shift_up(g, s, row, L)
            dws.append(jnp.sum(g * _shift_down(xv, s, row), axis=0, keepdims=True))
        dx_ref[...] = acc
        dw_ref[...] = jnp.concatenate(dws + [jnp.zeros((SUBLANES - GDN_CONV, LANES), F32)], axis=0)

    return pl.pallas_call(
        body, name=name + "_bwd", grid=(ch // LANES,),
        in_specs=[pl.BlockSpec((L, LANES), lambda j: (0, j)), pl.BlockSpec((SUBLANES, LANES), lambda j: (0, j)),
                  pl.BlockSpec((L, LANES), lambda j: (0, j))],
        out_specs=[pl.BlockSpec((L, LANES), lambda j: (0, j)), pl.BlockSpec((SUBLANES, LANES), lambda j: (0, j))],
        out_shape=[SDS((L, ch), F32), SDS((SUBLANES, ch), F32)],
        compiler_params=_params(("parallel",), VMEM_MID),
    )(x, w, dy)


def make_conv_act(act, name):
    @jax.custom_vjp
    def op(x, w):
        return _conv_fwd_call(x, w, act, name)

    def fwd(x, w):
        return _conv_fwd_call(x, w, act, name), (x, w)

    def bwd(res, dy):
        x, w = res
        return tuple(_conv_bwd_call(x, w, dy, act, name))

    op.defvjp(fwd, bwd)
    return op


gdn_conv = make_conv_act(lambda c: c, "gdn_conv")


BNN = (((2,), (1,)), ((0,), (0,)))
BNT = (((2,), (2,)), ((0,), (0,)))
BTN = (((1,), (1,)), ((0,), (0,)))
GDN_PREP_BATCH = 8


@jax.custom_vjp
def _known_inverse(a, t):
    return t


def _known_inverse_fwd(a, t):
    return t, t


def _known_inverse_bwd(t, g):
    return -_hdot(_hdot(t, g, _BTN), t, _BNT), jnp.zeros_like(t)


_known_inverse.defvjp(_known_inverse_fwd, _known_inverse_bwd)


def _gdn_prep_math(q, k, v, beta, g, t_saved=None):
    B, C = q.shape[0], q.shape[1]
    ri = lax.broadcasted_iota(jnp.int32, (B, C, C), 1)
    ci = lax.broadcasted_iota(jnp.int32, (B, C, C), 2)
    causal = ri >= ci
    strict = ri > ci
    eye = (ri == ci).astype(F32)
    gb = jnp.broadcast_to(g, (B, C, C))
    g_row = jnp.sum(gb * eye, axis=1, keepdims=True)
    gc_col = jnp.sum(jnp.where(causal, jnp.broadcast_to(g_row, (B, C, C)), 0.0), axis=2, keepdims=True)
    gc_row = jnp.sum(jnp.where(ri <= ci, gb, 0.0), axis=1, keepdims=True)
    decay = jnp.exp(jnp.where(causal, gc_col - gc_row, -jnp.inf))
    kk = _bdot(k, k, BNT)
    a_mat = jnp.where(strict, beta * kk * decay, 0.0)
    t = _unit_lower_inverse(a_mat) if t_saved is None else _known_inverse(a_mat, t_saved)
    e_gc = jnp.exp(gc_col)
    w = _hdot(t, beta * e_gc * k, BNN)
    u = _hdot(t, beta * v, BNN)
    qk = _bdot(q, k, BNT) * decay
    q_dec = q * e_gc
    g_last = gc_col[:, C - 1:C, :]
    k_dec = k * jnp.exp(g_last - gc_col)
    return q_dec, w, u, qk, k_dec, gc_col, t


def _gdn_prep_specs(L):
    C = GDN_CHUNK
    nb = min(GDN_PREP_BATCH, L // C)
    R = nb * C
    ins = [pl.BlockSpec((R, GDN_DK), lambda c, h: (c, h)), pl.BlockSpec((R, GDN_DK), lambda c, h: (c, h)),
           pl.BlockSpec((R, GDN_DV), lambda c, h: (c, h)), pl.BlockSpec((R, LANES), lambda c, h: (c, 0))]
    outs = [pl.BlockSpec((1, R, GDN_DK), lambda c, h: (h, c, 0)), pl.BlockSpec((1, R, GDN_DK), lambda c, h: (h, c, 0)),
            pl.BlockSpec((1, R, GDN_DV), lambda c, h: (h, c, 0)), pl.BlockSpec((1, R, C), lambda c, h: (h, c, 0)),
            pl.BlockSpec((1, R, GDN_DK), lambda c, h: (h, c, 0)), pl.BlockSpec((1, R, 1), lambda c, h: (h, c, 0))]
    t_spec = pl.BlockSpec((1, R, C), lambda c, h: (h, c, 0))
    shapes = [SDS((GDN_HEADS, L, GDN_DK), F32), SDS((GDN_HEADS, L, GDN_DK), F32), SDS((GDN_HEADS, L, GDN_DV), F32),
              SDS((GDN_HEADS, L, C), F32), SDS((GDN_HEADS, L, GDN_DK), F32), SDS((GDN_HEADS, L, 1), F32)]
    return ins, outs, t_spec, shapes, nb


def _chunks(x, nb):
    return x.reshape(nb, x.shape[0] // nb, x.shape[1])


def _head_columns(bg, h):
    lane = lax.broadcasted_iota(jnp.int32, bg.shape, 1)
    beta = jnp.sum(jnp.where(lane == h, bg, 0.0), axis=1, keepdims=True)
    g = jnp.sum(jnp.where(lane == h + GDN_HEADS, bg, 0.0), axis=1, keepdims=True)
    return beta, g


def _gdn_prep_fwd_call(q, k, v, bg):
    L = q.shape[0]
    ins, outs, t_spec, shapes, nb = _gdn_prep_specs(L)

    def body(q_ref, k_ref, v_ref, bg_ref, *o_refs):
        beta, g = _head_columns(bg_ref[...], pl.program_id(1))
        res = _gdn_prep_math(_chunks(q_ref[...], nb), _chunks(k_ref[...], nb), _chunks(v_ref[...], nb),
                             _chunks(beta, nb), _chunks(g, nb))
        for o_ref, val in zip(o_refs, res):
            o_ref[0] = val.reshape(val.shape[0] * val.shape[1], val.shape[2])

    return pl.pallas_call(
        body, name="gdn_prep_fwd", grid=(L // (nb * GDN_CHUNK), GDN_HEADS), in_specs=ins, out_specs=outs + [t_spec],
        out_shape=shapes + [SDS((GDN_HEADS, L, GDN_CHUNK), F32)],
        compiler_params=_params(("parallel", "parallel"), VMEM_MID),
    )(q, k, v, bg)


def _gdn_prep_bwd_call(q, k, v, bg, t, cts):
    L = q.shape[0]
    ins, outs, t_spec, _, nb = _gdn_prep_specs(L)

    def body(q_ref, k_ref, v_ref, bg_ref, t_ref, c0, c1, c2, c3, c4, c5, dq_ref, dk_ref, dv_ref, dbg_ref):
        h = pl.program_id(1)
        beta, g = _head_columns(bg_ref[...], h)
        t_saved = _chunks(t_ref[0], nb)
        _, vjp = jax.vjp(lambda *a: _gdn_prep_math(*a, t_saved=t_saved)[:6], _chunks(q_ref[...], nb), _chunks(k_ref[...], nb),
                         _chunks(v_ref[...], nb), _chunks(beta, nb), _chunks(g, nb))
        dq, dk, dv, db, dg = vjp(tuple(_chunks(c[0], nb) for c in (c0, c1, c2, c3, c4, c5)))
        flat = lambda a: a.reshape(a.shape[0] * a.shape[1], a.shape[2])
        dq_ref[...] = flat(dq)
        dk_ref[...] = flat(dk)
        dv_ref[...] = flat(dv)

        @pl.when(h == 0)
        def _():
            dbg_ref[...] = jnp.zeros_like(dbg_ref)
        lane = lax.broadcasted_iota(jnp.int32, dbg_ref.shape, 1)
        dbg_ref[...] += jnp.where(lane == h, flat(db), 0.0) + jnp.where(lane == h + GDN_HEADS, flat(dg), 0.0)

    return pl.pallas_call(
        body, name="gdn_prep_bwd", grid=(L // (nb * GDN_CHUNK), GDN_HEADS), in_specs=ins + [t_spec] + outs, out_specs=ins,
        out_shape=[SDS(q.shape, F32), SDS(k.shape, F32), SDS(v.shape, F32), SDS(bg.shape, F32)],
        compiler_params=_params(("parallel", "arbitrary"), VMEM_MID),
    )(q, k, v, bg, t, *cts)


@jax.custom_vjp
def gdn_prep(q, k, v, bg):
    return tuple(_gdn_prep_fwd_call(q, k, v, bg)[:6])


def _gdn_prep_f(q, k, v, bg):
    res = _gdn_prep_fwd_call(q, k, v, bg)
    return tuple(res[:6]), (q, k, v, bg, res[6])


def _gdn_prep_b(res, cts):
    return tuple(_gdn_prep_bwd_call(*res, tuple(cts)))


gdn_prep.defvjp(_gdn_prep_f, _gdn_prep_b)


def _gdn_step_math(q_dec, w, u, qk, k_dec, gc, z, nw, state):
    H, C = q_dec.shape[0], q_dec.shape[1]
    v_new = u - _bdot(w, state, BNN)
    o = _bdot(q_dec, state, BNN) + _bdot(qk, v_new, BNN)
    gl = gc[:, C - 1:C, :]
    new_state = jnp.exp(gl) * state + _bdot(k_dec, v_new, BTN)
    return _f_gdn_post(jnp.concatenate([o[h] for h in range(H)], axis=1), z, nw)[0], new_state


def _gdn_scan_specs(L, rev):
    C, H = GDN_CHUNK, GDN_HEADS
    nc = L // C
    cc = (lambda c: nc - 1 - c) if rev else (lambda c: c)
    ins = [pl.BlockSpec((H, C, GDN_DK), lambda c: (0, cc(c), 0)), pl.BlockSpec((H, C, GDN_DK), lambda c: (0, cc(c), 0)),
           pl.BlockSpec((H, C, GDN_DV), lambda c: (0, cc(c), 0)), pl.BlockSpec((H, C, C), lambda c: (0, cc(c), 0)),
           pl.BlockSpec((H, C, GDN_DK), lambda c: (0, cc(c), 0)), pl.BlockSpec((H, C, 1), lambda c: (0, cc(c), 0))]
    o_spec = pl.BlockSpec((C, H * GDN_DV), lambda c: (cc(c), 0))
    nw_spec = pl.BlockSpec((1, H * GDN_DV), lambda c: (0, 0))
    s_spec = pl.BlockSpec((1, H, GDN_DK, GDN_DV), lambda c: (cc(c), 0, 0, 0))
    return ins + [o_spec, nw_spec], o_spec, s_spec, nc


def _gdn_scan_fwd_call(q_dec, w, u, qk, k_dec, gc, z, nw):
    L = q_dec.shape[1]
    ins, o_spec, s_spec, nc = _gdn_scan_specs(L, False)

    def body(qd_ref, w_ref, u_ref, qk_ref, kd_ref, gc_ref, z_ref, nw_ref, o_ref, sin_ref, s_ref):
        c = pl.program_id(0)

        @pl.when(c == 0)
        def _():
            s_ref[...] = jnp.zeros_like(s_ref)
        st = s_ref[...]
        sin_ref[0] = st
        o, ns = _gdn_step_math(qd_ref[...], w_ref[...], u_ref[...], qk_ref[...], kd_ref[...], gc_ref[...], z_ref[...], nw_ref[...], st)
        o_ref[...] = o
        s_ref[...] = ns

    return pl.pallas_call(
        body, name="gdn_scan_fwd", grid=(nc,), in_specs=ins, out_specs=[o_spec, s_spec],
        out_shape=[SDS((L, GDN_HEADS * GDN_DV), F32), SDS((nc, GDN_HEADS, GDN_DK, GDN_DV), F32)],
        scratch_shapes=[pltpu.VMEM((GDN_HEADS, GDN_DK, GDN_DV), F32)],
        compiler_params=_params(("arbitrary",), VMEM_MID),
    )(q_dec, w, u, qk, k_dec, gc, z, nw)


def _gdn_scan_bwd_call(q_dec, w, u, qk, k_dec, gc, z, nw, s_in, do):
    L = q_dec.shape[1]
    ins, o_spec, s_spec, nc = _gdn_scan_specs(L, True)

    def body(qd_ref, w_ref, u_ref, qk_ref, kd_ref, gc_ref, z_ref, nw_ref, sin_ref, do_ref,
             dqd_ref, dw_ref, du_ref, dqk_ref, dkd_ref, dgc_ref, dz_ref, dnw_ref, ds_ref):
        c = pl.program_id(0)

        @pl.when(c == 0)
        def _():
            ds_ref[...] = jnp.zeros_like(ds_ref)
            dnw_ref[...] = jnp.zeros_like(dnw_ref)
        _, vjp = jax.vjp(_gdn_step_math, qd_ref[...], w_ref[...], u_ref[...], qk_ref[...], kd_ref[...], gc_ref[...],
                         z_ref[...], nw_ref[...], sin_ref[0])
        dqd, dw, du, dqk, dkd, dgc, dz, dnw, dst = vjp((do_ref[...], ds_ref[...]))
        dqd_ref[...] = dqd
        dw_ref[...] = dw
        du_ref[...] = du
        dqk_ref[...] = dqk
        dkd_ref[...] = dkd
        dgc_ref[...] = dgc
        dz_ref[...] = dz
        dnw_ref[...] += dnw
        ds_ref[...] = dst

    return pl.pallas_call(
        body, name="gdn_scan_bwd", grid=(nc,), in_specs=ins + [s_spec, o_spec], out_specs=ins,
        out_shape=[SDS(t.shape, F32) for t in (q_dec, w, u, qk, k_dec, gc, z, nw)],
        scratch_shapes=[pltpu.VMEM((GDN_HEADS, GDN_DK, GDN_DV), F32)],
        compiler_params=_params(("arbitrary",), VMEM_MID),
    )(q_dec, w, u, qk, k_dec, gc, z, nw, s_in, do)


@jax.custom_vjp
def gdn_scan(q_dec, w, u, qk, k_dec, gc, z, nw):
    return _gdn_scan_fwd_call(q_dec, w, u, qk, k_dec, gc, z, nw)[0]


def _gdn_scan_f(*args):
    o, s_in = _gdn_scan_fwd_call(*args)
    return o, (*args, s_in)


def _gdn_scan_b(res, do):
    return tuple(_gdn_scan_bwd_call(*res, do))


gdn_scan.defvjp(_gdn_scan_f, _gdn_scan_b)


def _silu(x):
    return x * jax.nn.sigmoid(x)


def _gelu_tanh(x):
    return 0.5 * x * (1.0 + jnp.tanh(math.sqrt(2.0 / math.pi) * (x + 0.044715 * (x * x * x))))


def _f_lnmod(x, nw, sc, sh, bsc, bsh):
    xn = x * lax.rsqrt(jnp.mean(x * x, axis=-1, keepdims=True) + NORM_EPS) * nw
    return (xn * (1.0 + (sc + bsc)) + (sh + bsh),)


def _f_s5_act(ys, u, d):
    return (_gelu_tanh(ys + d * u),)


def _f_s5_gate(y2, t, z):
    return (y2 * jax.nn.sigmoid(t) * _silu(z),)


def _f_res(x, y, gate, bgate):
    return (x + (gate + bgate) * y,)


def _heads(x, width, fn):
    return jnp.concatenate([fn(x[:, i * width:(i + 1) * width]) for i in range(x.shape[1] // width)], axis=1)


def _l2n(x):
    return x * lax.rsqrt(jnp.sum(x * x, axis=-1, keepdims=True) + NORM_EPS)


def _f_qnorm(x):
    return (_heads(_silu(x), GDN_DK, _l2n) * (GDN_DK ** -0.5),)


def _f_knorm(x):
    return (_heads(_silu(x), GDN_DK, _l2n),)


def _f_vact(x):
    return (_silu(x),)


def _f_betag(ba, alog, dtb):
    col = lax.broadcasted_iota(jnp.int32, ba.shape, 1)
    t = ba + dtb
    softplus = jnp.maximum(t, 0.0) + jnp.log1p(jnp.exp(-jnp.abs(t)))
    g = -jnp.exp(alog) * softplus
    return (jnp.where(col < GDN_HEADS, jax.nn.sigmoid(ba), jnp.where(col < 2 * GDN_HEADS, g, 0.0)),)


def _f_gdn_post(o, z, nw):
    on = _heads(o, GDN_DV, lambda t: t * lax.rsqrt(jnp.mean(t * t, axis=-1, keepdims=True) + NORM_EPS))
    return (on * nw * _silu(z),)


def _f_loss(x, tgt, fw):
    y = x * lax.rsqrt(jnp.mean(x * x, axis=-1, keepdims=True) + NORM_EPS) * fw
    err = y - tgt
    return (0.5 * jnp.mean(err * err, axis=-1, keepdims=True),)


def _ada_mod_call(c_all, ada_w):
    n = ada_w.shape[2]

    def body(c_ref, w_ref, o_ref):
        ca = _silu(c_ref[...])
        for l in range(ada_w.shape[0]):
            o_ref[l] = _bdot(ca, w_ref[l])

    return pl.pallas_call(body, name="ada_mod", out_shape=SDS((ada_w.shape[0], N_DEV, n), F32),
                          compiler_params=_params(None, VMEM_MID))(c_all, ada_w)


def _ada_grad_call(c_all, dmod):
    nl, _, n = dmod.shape

    def body(c_ref, d_ref, o_ref):
        ca = _silu(c_ref[...])
        for l in range(nl):
            o_ref[l] = _hdot(ca, d_ref[l], TN)

    return pl.pallas_call(body, name="ada_grad", out_shape=SDS((nl, c_all.shape[1], n), F32),
                          compiler_params=_params(None, VMEM_MID))(c_all, dmod)


ADAM_ROWS = 512


def _adamw(g, w, m, v):
    m2 = ADAM_B1 * m + (1.0 - ADAM_B1) * g
    v2 = ADAM_B2 * v + (1.0 - ADAM_B2) * (g * g)
    m_hat = m2 / (1.0 - ADAM_B1 ** ADAM_STEP)
    v_hat = v2 / (1.0 - ADAM_B2 ** ADAM_STEP)
    return g, -ADAM_LR * (m_hat / (jnp.sqrt(v_hat) + ADAM_EPS) + ADAM_WD * w), m2, v2


def _adam_call(gs, w, m, v, name, rows=None):
    n, r, cols = gs.shape
    rows = rows or ADAM_ROWS

    def body(g_ref, w_ref, m_ref, v_ref, go_ref, d_ref, mo_ref, vo_ref):
        g = g_ref[0].astype(F32)
        for s in range(1, n):
            g = g + g_ref[s].astype(F32)
        for o_ref, val in zip((go_ref, d_ref, mo_ref, vo_ref), _adamw(g, w_ref[...], m_ref[...], v_ref[...])):
            o_ref[...] = val

    blk = pl.BlockSpec((rows, cols), lambda i: (i, 0))
    return pl.pallas_call(
        body, name=name, grid=(r // rows,),
        in_specs=[pl.BlockSpec((n, rows, cols), lambda i: (0, i, 0)), blk, blk, blk],
        out_specs=[blk, blk, blk, blk], out_shape=[SDS((r, cols), F32)] * 4,
        compiler_params=_params(("parallel",), VMEM_MID),
    )(gs, w, m, v)


def _sum_call(gs, name, rows):
    n, r, _ = gs.shape

    def body(g_ref, o_ref):
        g = g_ref[0].astype(F32)
        for s in range(1, n):
            g = g + g_ref[s].astype(F32)
        o_ref[...] = g

    return pl.pallas_call(
        body, name=name, grid=(r // rows,),
        in_specs=[pl.BlockSpec((n, rows, LANES), lambda i: (0, i, 0))],
        out_specs=pl.BlockSpec((rows, LANES), lambda i: (i, 0)), out_shape=SDS((r, LANES), F32),
        compiler_params=_params(("parallel",), VMEM_MID),
    )(gs)


def _allgather_call(x_shard, name, in_hbm):
    m_per, n = x_shard.shape

    def body(x_ref, out_ref, send_sems, recv_sems, local_sem):
        x, y, c = lax.axis_index("x"), lax.axis_index("y"), lax.axis_index("c")
        me, sibling = (x, y, c), (x, y, 1 - c)
        chips = [(1 - x, y), (x, 1 - y), (1 - x, 1 - y)]

        def rows(px, py, pc):
            return out_ref.at[pl.ds((4 * px + 2 * py + pc) * m_per, m_per), :]

        def copy(k, block, to, src=None):
            return pltpu.make_async_remote_copy(
                src_ref=rows(*block) if src is None else src, dst_ref=rows(*block),
                send_sem=send_sems.at[k], recv_sem=recv_sems.at[k], device_id=to, device_id_type=pl.DeviceIdType.MESH)

        mine = pltpu.make_async_copy(x_ref, rows(*me), local_sem)
        mine.start()
        first = [copy(0, me, sibling, src=x_ref)]
        first += [copy(1 + j, me, (*chip, c), src=x_ref) for j, chip in enumerate(chips)]
        for cp in first:
            cp.start()
        passed = [copy(4 + j, (*chip, c), sibling) for j, chip in enumerate(chips)]
        for j, chip in enumerate(chips):
            copy(1 + j, (*chip, c), me).wait_recv()
            passed[j].start()
        copy(0, sibling, me).wait_recv()
        for j, chip in enumerate(chips):
            copy(4 + j, (*chip, 1 - c), me).wait_recv()
        for cp in first + passed:
            cp.wait_send()
        mine.wait()

    space = pl.ANY if in_hbm else pltpu.VMEM
    return pl.pallas_call(
        body, name=name, out_shape=SDS((N_DEV * m_per, n), x_shard.dtype),
        in_specs=[pl.BlockSpec(memory_space=space)], out_specs=pl.BlockSpec(memory_space=space),
        scratch_shapes=[pltpu.SemaphoreType.DMA((7,)), pltpu.SemaphoreType.DMA((7,)), pltpu.SemaphoreType.DMA],
        compiler_params=_params(None, None if in_hbm else VMEM_BIG),
    )(x_shard)


def _gather_weights_call(shards, name):
    nw = len(shards)

    def body(*refs):
        x_refs, out_refs = refs[:nw], refs[nw:2 * nw]
        send_sems, recv_sems, local_sems = refs[2 * nw:]
        x, y, c = lax.axis_index("x"), lax.axis_index("y"), lax.axis_index("c")
        me, sibling = (x, y, c), (x, y, 1 - c)
        chips = [(1 - x, y), (x, 1 - y), (1 - x, 1 - y)]

        def slot(w, px, py, pc):
            return out_refs[w].at[4 * px + 2 * py + pc]

        def copy(w, k, block, to, src=None):
            dst = slot(w, *block)
            return pltpu.make_async_remote_copy(
                src_ref=dst if src is None else src, dst_ref=dst, send_sem=send_sems.at[7 * w + k],
                recv_sem=recv_sems.at[7 * w + k], device_id=to, device_id_type=pl.DeviceIdType.MESH)

        mines = [pltpu.make_async_copy(x_refs[w], slot(w, *me), local_sems.at[w]) for w in range(nw)]
        for cp in mines:
            cp.start()
        first = [copy(w, 0, me, sibling, src=x_refs[w]) for w in range(nw)]
        first += [copy(w, 1 + j, me, (*chip, c), src=x_refs[w]) for w in range(nw) for j, chip in enumerate(chips)]
        for cp in first:
            cp.start()
        passed = []
        for w in range(nw):
            for j, chip in enumerate(chips):
                copy(w, 1 + j, (*chip, c), me).wait_recv()
                fwd = copy(w, 4 + j, (*chip, c), sibling)
                fwd.start()
                passed.append(fwd)
        for w in range(nw):
            copy(w, 0, sibling, me).wait_recv()
            for j, chip in enumerate(chips):
                copy(w, 4 + j, (*chip, 1 - c), me).wait_recv()
        for cp in first + passed:
            cp.wait_send()
        for cp in mines:
            cp.wait()

    hbm = pl.BlockSpec(memory_space=pl.ANY)
    return pl.pallas_call(
        body, name=name, out_shape=[SDS((N_DEV,) + s.shape, s.dtype) for s in shards],
        in_specs=[hbm] * nw, out_specs=[hbm] * nw,
        scratch_shapes=[pltpu.SemaphoreType.DMA((7 * nw,)), pltpu.SemaphoreType.DMA((7 * nw,)), pltpu.SemaphoreType.DMA((nw,))],
    )(*shards)


def _pair_exchange_call(grads, name):
    nw = len(grads)

    def body(*refs):
        g_refs, got_refs = refs[:nw], refs[nw:2 * nw]
        send_sems, recv_sems = refs[2 * nw:]
        x, y, c = lax.axis_index("x"), lax.axis_index("y"), lax.axis_index("c")
        copies = []
        for w in range(nw):
            for j in range(4):
                give = pltpu.make_async_remote_copy(
                    src_ref=g_refs[w].at[2 * j + 1 - c], dst_ref=got_refs[w].at[j], send_sem=send_sems.at[4 * w + j],
                    recv_sem=recv_sems.at[4 * w + j], device_id=(x, y, 1 - c), device_id_type=pl.DeviceIdType.MESH)
                give.start()
                copies.append(give)
        for cp in copies:
            cp.wait()

    hbm = pl.BlockSpec(memory_space=pl.ANY)
    return pl.pallas_call(
        body, name=name, out_shape=[SDS((4,) + g.shape[1:], g.dtype) for g in grads], in_specs=[hbm] * nw, out_specs=[hbm] * nw,
        scratch_shapes=[pltpu.SemaphoreType.DMA((4 * nw,)), pltpu.SemaphoreType.DMA((4 * nw,))],
    )(*grads)


def _chip_exchange_call(parts, name):
    nw = len(parts)

    def body(*refs):
        p_refs, out_refs = refs[:nw], refs[nw:2 * nw]
        send_sems, recv_sems = refs[2 * nw:]
        x, y, c = lax.axis_index("x"), lax.axis_index("y"), lax.axis_index("c")
        chips = [(1 - x, y), (x, 1 - y), (1 - x, 1 - y)]
        copies = []
        for w in range(nw):
            for j, (px, py) in enumerate(chips):
                give = pltpu.make_async_remote_copy(
                    src_ref=p_refs[w].at[2 * px + py], dst_ref=out_refs[w].at[j], send_sem=send_sems.at[3 * w + j],
                    recv_sem=recv_sems.at[3 * w + j], device_id=(px, py, c), device_id_type=pl.DeviceIdType.MESH)
                give.start()
                copies.append(give)
        for cp in copies:
            cp.wait()

    hbm = pl.BlockSpec(memory_space=pl.ANY)
    return pl.pallas_call(
        body, name=name, out_shape=[SDS((3,) + p.shape[1:], p.dtype) for p in parts], in_specs=[hbm] * nw, out_specs=[hbm] * nw,
        scratch_shapes=[pltpu.SemaphoreType.DMA((3 * nw,)), pltpu.SemaphoreType.DMA((3 * nw,))],
    )(*parts)


_HBM = pl.BlockSpec(memory_space=pltpu.HBM)
_SEM = pl.BlockSpec(memory_space=pltpu.SEMAPHORE)
_DATAFLOW = pltpu.SideEffectType.DATAFLOW_SIDE_EFFECTING


def _spread_start_call(srcs, per_peer, name, after):
    nw = len(srcs)
    lands = [lax.empty((N_DEV,) + (s.shape[1:] if per_peer else s.shape), s.dtype) for s in srcs]

    def body(*refs):
        src_refs, land_refs = refs[:nw], refs[nw:2 * nw]
        send_sems, recv_sems, token = refs[2 * nw + 1], refs[2 * nw + 2], refs[-1]
        x, y, c = lax.axis_index("x"), lax.axis_index("y"), lax.axis_index("c")
        me = 4 * x + 2 * y + c
        for w in range(nw):
            for k in range(1, N_DEV):
                px = 1 - x if k & 4 else x
                py = 1 - y if k & 2 else y
                pc = 1 - c if k & 1 else c
                src = src_refs[w].at[4 * px + 2 * py + pc] if per_peer else src_refs[w]
                pltpu.make_async_remote_copy(
                    src_ref=src, dst_ref=land_refs[w].at[me], send_sem=send_sems.at[w], recv_sem=recv_sems.at[w],
                    device_id=(px, py, pc), device_id_type=pl.DeviceIdType.MESH).start()
        token[...] = jnp.zeros_like(token)

    hbm = lambda a: pltpu.with_memory_space_constraint(a, pltpu.HBM)
    res = pl.pallas_call(
        body, name=name,
        out_shape=(pltpu.SemaphoreType.DMA((nw,)), pltpu.SemaphoreType.DMA((nw,)))
        + tuple(pltpu.HBM(s.shape, s.dtype) for s in srcs) + tuple(pltpu.HBM(l.shape, l.dtype) for l in lands)
        + (SDS((SUBLANES, LANES), F32),),
        in_specs=[_HBM] * (2 * nw) + [pl.BlockSpec(memory_space=pl.ANY)],
        out_specs=(_SEM, _SEM) + (_HBM,) * (2 * nw) + (pl.BlockSpec(memory_space=pltpu.VMEM),),
        input_output_aliases={i: i + 2 for i in range(2 * nw)},
        compiler_params=pltpu.CompilerParams(has_side_effects=_DATAFLOW),
    )(*[hbm(s) for s in srcs], *[hbm(l) for l in lands], after)
    return res[0], res[1], res[2:2 + nw], res[2 + nw:2 + 2 * nw], res[-1]


def _spread_wait_call(send_sems, recv_sems, srcs, lands, after, name):
    nw = len(lands)

    def body(*refs):
        land_refs = refs[nw:2 * nw]
        s_sems, r_sems = refs[2 * nw], refs[2 * nw + 1]
        x, y, c = lax.axis_index("x"), lax.axis_index("y"), lax.axis_index("c")
        for w in range(nw):
            seven = land_refs[w].at[pl.ds(0, N_DEV - 1)]
            all_seven = pltpu.make_async_remote_copy(
                src_ref=seven, dst_ref=seven, send_sem=s_sems.at[w], recv_sem=r_sems.at[w],
                device_id=(x, y, c), device_id_type=pl.DeviceIdType.MESH)
            all_seven.wait_send()
            all_seven.wait_recv()

    res = pl.pallas_call(
        body, name=name,
        out_shape=tuple(pltpu.HBM(s.shape, s.dtype) for s in srcs) + tuple(pltpu.HBM(l.shape, l.dtype) for l in lands),
        in_specs=[_HBM] * (2 * nw) + [_SEM, _SEM, pl.BlockSpec(memory_space=pl.ANY)], out_specs=(_HBM,) * (2 * nw),
        input_output_aliases={i: i for i in range(2 * nw)},
        compiler_params=pltpu.CompilerParams(has_side_effects=_DATAFLOW),
    )(*srcs, *lands, send_sems, recv_sems, after)
    return res[:nw], res[nw:]


def _pair_sum_call(g, got, core, name):
    _, k, n = got.shape
    tr = _tile(k, 256)

    def body(c_ref, g_ref, got_ref, o_ref):
        o_ref[...] = (g_ref[...] + got_ref[...]).astype(o_ref.dtype)

    spec = pltpu.PrefetchScalarGridSpec(
        num_scalar_prefetch=1, grid=(4, k // tr),
        in_specs=[pl.BlockSpec((1, tr, n), lambda j, i, c: (2 * j + c[0], i, 0)), pl.BlockSpec((1, tr, n), lambda j, i, c: (j, i, 0))],
        out_specs=pl.BlockSpec((1, tr, n), lambda j, i, c: (j, i, 0)))
    return pl.pallas_call(body, name=name, grid_spec=spec, out_shape=SDS(got.shape, BF16),
                          compiler_params=_params(("parallel", "parallel"), VMEM_MID))(core, g, got)


def _adam_own_call(pair, chip, recv, w, m, v, name, rows):
    _, r, cols = recv.shape

    def body(chip_ref, p_ref, g_ref, w_ref, m_ref, v_ref, go_ref, d_ref, mo_ref, vo_ref):
        g = ((p_ref[0].astype(F32) + g_ref[0].astype(F32)) + g_ref[1].astype(F32)) + g_ref[2].astype(F32)
        for o_ref, val in zip((go_ref, d_ref, mo_ref, vo_ref), _adamw(g, w_ref[...], m_ref[...], v_ref[...])):
            o_ref[...] = val

    blk = pl.BlockSpec((rows, cols), lambda i, s: (i, 0))
    spec = pltpu.PrefetchScalarGridSpec(
        num_scalar_prefetch=1, grid=(r // rows,),
        in_specs=[pl.BlockSpec((1, rows, cols), lambda i, s: (s[0], i, 0)), pl.BlockSpec((3, rows, cols), lambda i, s: (0, i, 0)),
                  blk, blk, blk],
        out_specs=[blk, blk, blk, blk])
    return pl.pallas_call(body, name=name, grid_spec=spec, out_shape=[SDS((r, cols), F32)] * 4,
                          compiler_params=_params(("parallel",), VMEM_MID))(chip, pair, recv, w, m, v)


def _join_cols_call(w8, name):
    _, k, n = w8.shape
    tk = _tile(k, 256)

    def body(w_ref, o_ref):
        for s in range(N_DEV):
            o_ref[:, n * s:n * (s + 1)] = w_ref[s]

    return pl.pallas_call(body, name=name, grid=(k // tk,), in_specs=[pl.BlockSpec((N_DEV, tk, n), lambda i: (0, i, 0))],
                          out_specs=pl.BlockSpec((tk, N_DEV * n), lambda i: (i, 0)), out_shape=SDS((k, N_DEV * n), w8.dtype),
                          compiler_params=_params(("parallel",), VMEM_MID))(w8)


def _split_cols_call(g, name, dtype):
    k, n8 = g.shape
    n = n8 // N_DEV
    tk = _tile(k, 256)

    def body(g_ref, o_ref):
        for s in range(N_DEV):
            o_ref[s] = g_ref[:, n * s:n * (s + 1)].astype(dtype)

    return pl.pallas_call(body, name=name, grid=(k // tk,), in_specs=[pl.BlockSpec((tk, n8), lambda i: (i, 0))],
                          out_specs=pl.BlockSpec((N_DEV, tk, n), lambda i: (0, i, 0)), out_shape=SDS((N_DEV, k, n), dtype),
                          compiler_params=_params(("parallel",), VMEM_MID))(g)


def _pack(parts, rows_multiple):
    flat = jnp.concatenate([p.reshape(-1) for p in parts])
    unit = rows_multiple * LANES
    padded = -(-flat.shape[0] // unit) * unit
    flat = jnp.concatenate([flat, jnp.zeros((padded - flat.shape[0],), F32)])
    return flat.reshape(-1, LANES)


def _unpack(buf, shapes):
    flat = buf.reshape(-1)
    out, off = [], 0
    for s in shapes:
        n = math.prod(s)
        out.append(flat[off:off + n].reshape(s))
        off += n
    return out


def _row_tile(L):
    return 256 if L % 256 == 0 else L


def _layer0_mix(diff, const):
    x, mod, norm_w, lam_re, lam_im, log_dt, b_re, b_im, c_re, c_im, s5_d, *slots = diff
    ada_b, weights = const
    L = x.shape[0]
    tm = _row_tile(L)
    mods = mod.reshape(2, 1, D_MODEL)
    biases = ada_b.reshape(2, 1, D_MODEL)
    op_ln0 = make_rowwise(_f_lnmod, "ln0", tm, 1, 5, pass_first=True)
    h, x = op_ln0((x,), (norm_w.reshape(1, D_MODEL), mods[1], mods[0], biases[1], biases[0]))
    u, z = make_proj("s5_in")(h, tuple(weights), tuple(slots))
    blocks = _s5_block_params(lam_re, lam_im, log_dt, b_re, b_im, c_re, c_im)
    y2 = make_s5_core(min(S5_TL, L))(u, *blocks, s5_d.reshape(1, D_INNER))
    return x, y2, z


def _layer0_out(diff, weights):
    y2, z, *slots = diff
    tm = _row_tile(y2.shape[0])
    t, y2 = make_mm("s5_glu", pass_input=True)(y2, weights[0], slots[0])
    (y4,) = make_rowwise(_f_s5_gate, "s5_gate", tm, 3, 0)((y2, t, z), ())
    return make_mm("s5_out")(y4, weights[1], slots[1])


def _f_res_lnmod(x, o, gate, bgate, nw, sc, sh, bsc, bsh):
    (x1,) = _f_res(x, o, gate, bgate)
    return _f_lnmod(x1, nw, sc, sh, bsc, bsh) + (x1,)


def _f_res_loss(x, y, tgt, gate, bgate, fw):
    return _f_loss(_f_res(x, y, gate, bgate)[0], tgt, fw)


def _layer1_loss(diff, const):
    x, o, gate0, mod, norm_w, conv_w, a_log, dt_bias, gdn_nw, final_nw, *slots = diff
    tgt, bgate0, ada_b, weights = const
    L = x.shape[0]
    tm = _row_tile(L)
    mods = mod.reshape(3, 1, D_MODEL)
    biases = ada_b.reshape(3, 1, D_MODEL)
    h, x1 = make_rowwise(_f_res_lnmod, "res0_ln1", tm, 2, 7)(
        (x, o), (gate0.reshape(1, D_MODEL), bgate0.reshape(1, D_MODEL), norm_w.reshape(1, D_MODEL), mods[1], mods[0], biases[1], biases[0]))
    q0, k0, v0, gz, ba = make_proj("gdn_in")(h, tuple(weights[0:5]), tuple(slots[0:5]))
    cw = jnp.concatenate([conv_w, jnp.zeros((SUBLANES - GDN_CONV, GDN_CONV_CH), F32)], axis=0)
    q = make_conv_act(lambda t: _l2n(_silu(t)) * (GDN_DK ** -0.5), "gdn_conv_q")(q0, cw[:, :GDN_QK])
    k = make_conv_act(lambda t: _l2n(_silu(t)), "gdn_conv_k")(k0, cw[:, GDN_QK:2 * GDN_QK])
    v = make_conv_act(_silu, "gdn_conv_v")(v0, cw[:, 2 * GDN_QK:])
    pad = jnp.zeros((LANES - 2 * GDN_HEADS,), F32)
    alog_row = jnp.concatenate([jnp.zeros((GDN_HEADS,), F32), a_log, pad]).reshape(1, LANES)
    dtb_row = jnp.concatenate([jnp.zeros((GDN_HEADS,), F32), dt_bias, pad]).reshape(1, LANES)
    (bg,) = make_rowwise(_f_betag, "gdn_bg", tm, 1, 2)((ba,), (alog_row, dtb_row))
    nw_row = jnp.tile(gdn_nw, GDN_HEADS).reshape(1, D_INNER)
    on = gdn_scan(*gdn_prep(q, k, v, bg), gz, nw_row)
    y = make_mm("gdn_out")(on, weights[5], slots[5])
    (lt,) = make_rowwise(_f_res_loss, "res1_loss", tm, 3, 3)((x1, y, tgt), (mods[2], biases[2], final_nw.reshape(1, D_MODEL)))
    return jnp.sum(lt)


VEC_NAMES = ("ada_b", "norm_w", "s5_lambda_re", "s5_lambda_im", "s5_log_dt", "s5_d", "gdn_a_log", "gdn_dt_bias", "final_norm_w")
MAT_NAMES = ("s5_b_re", "s5_b_im", "s5_c_re", "s5_c_im")
S5_BIG = ("s5_w_in", "s5_w_glu", "s5_w_out")
GDN_BIG = ("gdn_w_in", "gdn_w_out")
BIG_NAMES = S5_BIG + GDN_BIG
WEIGHT_ORDER = ("ada_w", "ada_b", "norm_w", "s5_w_in", "s5_lambda_re", "s5_lambda_im", "s5_log_dt", "s5_b_re", "s5_b_im",
                "s5_c_re", "s5_c_im", "s5_d", "s5_w_glu", "s5_w_out", "gdn_w_in", "gdn_conv_w", "gdn_a_log", "gdn_dt_bias",
                "gdn_norm_w", "gdn_w_out", "final_norm_w")


def _step(x, c, W, M, V, tgt):
    L = x.shape[1]
    ix, iy, ic = lax.axis_index("x"), lax.axis_index("y"), lax.axis_index("c")
    me = 4 * ix + 2 * iy + ic
    n_ada = W["ada_w"].shape[2]
    n_conv = W["gdn_conv_w"].shape[2]
    n_gnw = W["gdn_norm_w"].shape[1]

    g1 = _allgather_call(_pack([c, W["gdn_conv_w"], W["gdn_norm_w"]], SUBLANES), "gather_small_in", False)
    g1 = g1.reshape(N_DEV, -1)
    c_all = g1[:, :D_MODEL]
    conv_w = g1[:, D_MODEL:D_MODEL + GDN_CONV * n_conv].reshape(N_DEV, GDN_CONV, n_conv).transpose(1, 0, 2).reshape(GDN_CONV, -1)
    gdn_nw = g1[:, D_MODEL + GDN_CONV * n_conv:D_MODEL + GDN_CONV * n_conv + n_gnw].reshape(-1)
    mod_part = _ada_mod_call(c_all, W["ada_w"])
    g2 = _allgather_call(_pack([mod_part], SUBLANES), "gather_mod", False).reshape(N_DEV, -1)
    mod_all = g2[:, :2 * N_DEV * n_ada].reshape(N_DEV, 2, N_DEV, n_ada)
    mod_raw = lax.dynamic_index_in_dim(mod_all, me, axis=2, keepdims=False)
    mod_raw = mod_raw.transpose(1, 0, 2).reshape(2, 3 * D_MODEL)

    shard = lambda n: W[n][0].astype(BF16)
    (w_in5_parts,) = _gather_weights_call([shard("s5_w_in")], "gather_s5_w_in")
    late = _spread_start_call([shard("s5_w_glu"), shard("s5_w_out")], False, "gather_s5_late_start", w_in5_parts)
    g_send, g_recv, g_srcs, g_lands, g_token = _spread_start_call([shard(n) for n in GDN_BIG], False, "gather_gdn_start", late[4])
    w_in5 = _join_cols_call(w_in5_parts, "join_s5_w_in")
    slot = lambda *s: jnp.zeros(s, F32)
    two = 2 * D_MODEL
    diff_mix = (x[0], mod_raw[0, :two] + g_token[0, 0], W["norm_w"][0], W["s5_lambda_re"][0], W["s5_lambda_im"][0], W["s5_log_dt"][0],
                W["s5_b_re"][0], W["s5_b_im"][0], W["s5_c_re"][0], W["s5_c_im"][0], W["s5_d"][0],
                slot(D_MODEL, D_INNER), slot(D_MODEL, D_INNER))

    (xp, y2, z5), vjp_mix = jax.vjp(lambda d: _layer0_mix(d, (W["ada_b"][0, :two], (w_in5[:, :D_INNER], w_in5[:, D_INNER:]))), diff_mix)
    l_srcs, l_lands = _spread_wait_call(late[0], late[1], late[2], late[3], y2, "gather_s5_late_wait")
    w_glu, w_o5 = [lax.dynamic_update_slice(land, src[None], (me, 0, 0)).reshape(-1, src.shape[1]) for land, src in zip(l_lands, l_srcs)]
    diff_out = (y2, z5, slot(D_INNER, D_INNER), slot(D_INNER, D_MODEL))
    o5, vjp_out = jax.vjp(lambda d: _layer0_out(d, (w_glu, w_o5)), diff_out)
    g_srcs, g_lands = _spread_wait_call(g_send, g_recv, g_srcs, g_lands, o5, "gather_gdn_wait")
    gdn_full = [lax.dynamic_update_slice(land, src[None], (me, 0, 0)) for land, src in zip(g_lands, g_srcs)]
    w_ing = _join_cols_call(gdn_full[0], "join_gdn_w_in")
    w_ba = jnp.concatenate([w_ing[:, GDN_CONV_CH + D_INNER:], jnp.zeros((D_MODEL, LANES - 2 * GDN_HEADS), BF16)], axis=1)
    weights1 = (w_ing[:, :GDN_QK], w_ing[:, GDN_QK:2 * GDN_QK], w_ing[:, 2 * GDN_QK:GDN_CONV_CH],
                w_ing[:, GDN_CONV_CH:GDN_CONV_CH + D_INNER], w_ba, gdn_full[1].reshape(D_INNER, D_MODEL))
    slots1 = tuple(jnp.zeros(w.shape, F32) for w in weights1)
    diff1 = (xp, o5, mod_raw[0, two:], mod_raw[1], W["norm_w"][1], conv_w, W["gdn_a_log"][0], W["gdn_dt_bias"][0], gdn_nw,
             W["final_norm_w"], *slots1)
    loss_local, vjp1 = jax.vjp(lambda d: _layer1_loss(d, (tgt[0], W["ada_b"][0, two:], W["ada_b"][1], weights1)), diff1)
    ((dxp, do5, dmod_gate, dmod1, d_norm_w1, d_conv, d_alog, d_dtb, d_gnw, d_fnw, d_wq, d_wk, d_wv, d_wgz, d_wba, d_wog),) = vjp1(
        jnp.ones((), F32))
    loss = lax.psum(loss_local, MESH_AXES)

    rows = lambda d: d.reshape(N_DEV, d.shape[0] // N_DEV, d.shape[1])
    d_ing = _split_cols_call(jnp.concatenate([d_wq, d_wk, d_wv, d_wgz, d_wba[:, :2 * GDN_HEADS]], axis=1), "split_gdn_w_in", BF16)
    s_send, s_recv, s_srcs, s_lands, s_token = _spread_start_call([d_ing, rows(d_wog).astype(BF16)], True, "scatter_gdn_start", dxp)
    ((dy2, dz5, d_wglu, d_wo5),) = vjp_out(do5.at[0, 0].add(s_token[0, 0]))
    t_send, t_recv, t_srcs, t_lands, t_token = _spread_start_call(
        [rows(d_wglu).astype(BF16), rows(d_wo5).astype(BF16)], True, "scatter_s5_late_start", dy2)
    ((dx, dmod_ss, d_norm_w0, d_lre, d_lim, d_logdt, d_bre, d_bim, d_cre, d_cim, d_s5d, d_wu, d_wz),) = vjp_mix(
        (dxp.at[0, 0].add(t_token[0, 0]), dy2, dz5))
    dmod = jnp.stack([jnp.concatenate([dmod_ss, dmod_gate]), dmod1])
    d_norm_w = jnp.stack([d_norm_w0, d_norm_w1])
    vec_parts = [dmod, d_norm_w, d_lre, d_lim, d_logdt, d_s5d, d_alog, d_dtb, d_fnw]
    tail_parts = [d_conv, d_gnw]
    mat_parts = [d_bre, d_bim, d_cre, d_cim]
    n_vec = sum(math.prod(p.shape) for p in vec_parts)
    m_send, m_recv, m_srcs, m_lands, m_token = _spread_start_call(
        [_pack(vec_parts + tail_parts, ADAM_ROWS), _pack(mat_parts, SUBLANES).astype(BF16)], False, "gather_small_grads_start", dx)
    d_in5 = _split_cols_call(jnp.concatenate([d_wu.at[0, 0].add(m_token[0, 0]), d_wz], axis=1), "split_s5_w_in", F32)
    (got,) = _pair_exchange_call([d_in5], "scatter_s5_pair")
    core = jnp.reshape(ic, (1,)).astype(jnp.int32)
    chip = jnp.reshape(2 * ix + iy, (1,)).astype(jnp.int32)
    pair = _pair_sum_call(d_in5, got, core, "pair_sum_s5_w_in")
    (recv,) = _chip_exchange_call([pair], "scatter_s5_chips")
    big = {"s5_w_in": _adam_own_call(pair, chip, recv, W["s5_w_in"][0], M["s5_w_in"][0], V["s5_w_in"][0], "adam_s5_w_in", 128)}
    t_srcs, t_lands = _spread_wait_call(t_send, t_recv, t_srcs, t_lands, dx, "scatter_s5_late_wait")
    s_srcs, s_lands = _spread_wait_call(s_send, s_recv, s_srcs, s_lands, t_lands[0], "scatter_gdn_wait")
    for land, src, n in zip(tuple(t_lands) + tuple(s_lands), tuple(t_srcs) + tuple(s_srcs), ("s5_w_glu", "s5_w_out") + GDN_BIG):
        mine = lax.dynamic_index_in_dim(src, me, 0, keepdims=True)
        parts = lax.dynamic_update_slice(land, mine, (me, 0, 0))
        big[n] = _adam_call(parts, W[n][0], M[n][0], V[n][0], "adam_" + n, rows=_tile(W[n].shape[1], 128))
    big = [[o[None] for o in big[n]] for n in BIG_NAMES]

    m_srcs, m_lands = _spread_wait_call(m_send, m_recv, m_srcs, m_lands, big[-1][0], "gather_small_grads_wait")
    sg_vec, sg_mat = [lax.dynamic_update_slice(land, src[None], (me, 0, 0)) for land, src in zip(m_lands, m_srcs)]
    tot_vec = _sum_call(sg_vec, "sum_vec_grads", ADAM_ROWS)
    tot_mat = _sum_call(sg_mat, "sum_mat_grads", ADAM_ROWS)
    g_conv, g_gnw = _unpack(tot_vec.reshape(-1)[n_vec:], [d_conv.shape, d_gnw.shape])
    g_conv_mine = lax.dynamic_slice_in_dim(g_conv, me * n_conv, n_conv, axis=1)
    g_gnw_mine = lax.dynamic_slice_in_dim(g_gnw, me * n_gnw, n_gnw, axis=0)
    vec_names = VEC_NAMES + ("gdn_conv_w", "gdn_norm_w")
    vec_g = _pack([tot_vec.reshape(-1)[:n_vec], g_conv_mine, g_gnw_mine], ADAM_ROWS)
    vec = _adam_call(vec_g[None], _pack([W[n] for n in vec_names], ADAM_ROWS), _pack([M[n] for n in vec_names], ADAM_ROWS),
                     _pack([V[n] for n in vec_names], ADAM_ROWS), "adam_vec")
    vec = [_unpack(b, [W[n].shape for n in vec_names]) for b in vec]
    mats = []
    for name, g_mat in zip(MAT_NAMES, _unpack(tot_mat, [p.shape for p in mat_parts])):
        two_d = (-1, W[name].shape[-1])
        outs = _adam_call(g_mat.reshape(two_d)[None], W[name].reshape(two_d), M[name].reshape(two_d), V[name].reshape(two_d),
                          "adam_" + name, rows=1024)
        mats.append([o.reshape(W[name].shape) for o in outs])

    dmod_all = sg_vec[:, :2 * 3 * D_MODEL // LANES].reshape(N_DEV, 2, N_DEV, n_ada // LANES, LANES)
    dmod_mine = lax.dynamic_index_in_dim(dmod_all, me, axis=2, keepdims=False).transpose(1, 0, 2, 3).reshape(2, N_DEV, n_ada)
    g_ada_w = _ada_grad_call(c_all, dmod_mine)
    ada = _adam_call(g_ada_w.reshape(1, -1, LANES), W["ada_w"].reshape(-1, LANES), M["ada_w"].reshape(-1, LANES),
                     V["ada_w"].reshape(-1, LANES), "adam_ada")
    ada = [a.reshape(W["ada_w"].shape) for a in ada]

    res = {}
    for i, n in enumerate(BIG_NAMES):
        res[n] = big[i]
    for i, n in enumerate(vec_names):
        res[n] = [b[i] for b in vec]
    for i, n in enumerate(MAT_NAMES):
        res[n] = mats[i]
    res["ada_w"] = ada
    outs = [loss, dx[None]]
    for j in range(4):
        outs += [res[n][j] for n in WEIGHT_ORDER]
    return tuple(outs)


def kernel(x, c, ada_w, ada_b, norm_w, s5_w_in, s5_lambda_re, s5_lambda_im, s5_log_dt, s5_b_re, s5_b_im, s5_c_re, s5_c_im, s5_d, s5_w_glu, s5_w_out, gdn_w_in, gdn_conv_w, gdn_a_log, gdn_dt_bias, gdn_norm_w, gdn_w_out, final_norm_w, loss_target, m_ada_w, m_ada_b, m_norm_w, m_s5_w_in, m_s5_lambda_re, m_s5_lambda_im, m_s5_log_dt, m_s5_b_re, m_s5_b_im, m_s5_c_re, m_s5_c_im, m_s5_d, m_s5_w_glu, m_s5_w_out, m_gdn_w_in, m_gdn_conv_w, m_gdn_a_log, m_gdn_dt_bias, m_gdn_norm_w, m_gdn_w_out, m_final_norm_w, v_ada_w, v_ada_b, v_norm_w, v_s5_w_in, v_s5_lambda_re, v_s5_lambda_im, v_s5_log_dt, v_s5_b_re, v_s5_b_im, v_s5_c_re, v_s5_c_im, v_s5_d, v_s5_w_glu, v_s5_w_out, v_gdn_w_in, v_gdn_conv_w, v_gdn_a_log, v_gdn_dt_bias, v_gdn_norm_w, v_gdn_w_out, v_final_norm_w):
    W = dict(ada_w=ada_w, ada_b=ada_b, norm_w=norm_w, s5_w_in=s5_w_in, s5_lambda_re=s5_lambda_re, s5_lambda_im=s5_lambda_im,
             s5_log_dt=s5_log_dt, s5_b_re=s5_b_re, s5_b_im=s5_b_im, s5_c_re=s5_c_re, s5_c_im=s5_c_im, s5_d=s5_d,
             s5_w_glu=s5_w_glu, s5_w_out=s5_w_out, gdn_w_in=gdn_w_in, gdn_conv_w=gdn_conv_w, gdn_a_log=gdn_a_log,
             gdn_dt_bias=gdn_dt_bias, gdn_norm_w=gdn_norm_w, gdn_w_out=gdn_w_out, final_norm_w=final_norm_w)
    M = dict(ada_w=m_ada_w, ada_b=m_ada_b, norm_w=m_norm_w, s5_w_in=m_s5_w_in, s5_lambda_re=m_s5_lambda_re,
             s5_lambda_im=m_s5_lambda_im, s5_log_dt=m_s5_log_dt, s5_b_re=m_s5_b_re, s5_b_im=m_s5_b_im, s5_c_re=m_s5_c_re,
             s5_c_im=m_s5_c_im, s5_d=m_s5_d, s5_w_glu=m_s5_w_glu, s5_w_out=m_s5_w_out, gdn_w_in=m_gdn_w_in,
             gdn_conv_w=m_gdn_conv_w, gdn_a_log=m_gdn_a_log, gdn_dt_bias=m_gdn_dt_bias, gdn_norm_w=m_gdn_norm_w,
             gdn_w_out=m_gdn_w_out, final_norm_w=m_final_norm_w)
    V = dict(ada_w=v_ada_w, ada_b=v_ada_b, norm_w=v_norm_w, s5_w_in=v_s5_w_in, s5_lambda_re=v_s5_lambda_re,
             s5_lambda_im=v_s5_lambda_im, s5_log_dt=v_s5_log_dt, s5_b_re=v_s5_b_re, s5_b_im=v_s5_b_im, s5_c_re=v_s5_c_re,
             s5_c_im=v_s5_c_im, s5_d=v_s5_d, s5_w_glu=v_s5_w_glu, s5_w_out=v_s5_w_out, gdn_w_in=v_gdn_w_in,
             gdn_conv_w=v_gdn_conv_w, gdn_a_log=v_gdn_a_log, gdn_dt_bias=v_gdn_dt_bias, gdn_norm_w=v_gdn_norm_w,
             gdn_w_out=v_gdn_w_out, final_norm_w=v_final_norm_w)
    return _step(x, c, W, M, V, loss_target)
```

```python
import functools
import math

import jax
import jax.numpy as jnp
from jax import lax
from jax.experimental import pallas as pl
from jax.experimental.pallas import tpu as pltpu

F32 = jnp.float32
BF16 = jnp.bfloat16
SDS = jax.ShapeDtypeStruct

D_MODEL = 1024
D_INNER = 2048
NORM_EPS = 1e-6
S5_GROUP = 16
S5_GROUPS = 128
S5_STATE = 64
GDN_HEADS = 8
GDN_DK = 128
GDN_DV = 256
GDN_CONV = 4
GDN_CHUNK = 64
GDN_QK = 1024
GDN_CONV_CH = 4096
GDN_PROJ = 6160
ADAM_LR = 0.001
ADAM_B1 = 0.9
ADAM_B2 = 0.999
ADAM_EPS = 1e-08
ADAM_WD = 0.01
ADAM_STEP = 10

N_DEV = 8
LANES = 128
SUBLANES = 8
VMEM_BIG = 56 << 20
VMEM_MID = 40 << 20
S5_GB = 8
S5_TL = 1024
MESH_AXES = ("x", "y", "c")


def _params(sem, vmem=None):
    return pltpu.CompilerParams(dimension_semantics=sem, vmem_limit_bytes=vmem)


def _bdot(a, b, dims=(((1,), (0,)), ((), ()))):
    return lax.dot_general(a.astype(BF16), b.astype(BF16), dims, preferred_element_type=F32)


def _hdot(a, b, dims=(((1,), (0,)), ((), ()))):
    return lax.dot_general(a, b, dims, preferred_element_type=F32, precision=lax.Precision.HIGHEST)


_BNN = (((2,), (1,)), ((0,), (0,)))
_BNT = (((2,), (2,)), ((0,), (0,)))
_BTN = (((1,), (1,)), ((0,), (0,)))


@jax.custom_vjp
def _unit_lower_inverse(a):
    c = a.shape[-1]
    ri = lax.broadcasted_iota(jnp.int32, a.shape, 1)
    ci = lax.broadcasted_iota(jnp.int32, a.shape, 2)
    n = -a
    t = (ri == ci).astype(F32) + n
    for _ in range(int(math.log2(c)) - 1):
        n = _hdot(n, n, _BNN)
        t = t + _hdot(t, n, _BNN)
    return t


def _unit_lower_inverse_fwd(a):
    t = _unit_lower_inverse(a)
    return t, t


def _unit_lower_inverse_bwd(t, g):
    return (-_hdot(_hdot(t, g, _BTN), t, _BNT),)


_unit_lower_inverse.defvjp(_unit_lower_inverse_fwd, _unit_lower_inverse_bwd)


NN = (((1,), (0,)), ((), ()))
NT = (((1,), (1,)), ((), ()))
TN = (((0,), (0,)), ((), ()))


def _tile(n, pref):
    for t in (pref, 512, 256, 128):
        if t <= n and n % t == 0:
            return t
    return n


def _matmul(a, b, mode, name, add=None):
    if mode == "nn":
        (m, k), (_, n) = a.shape, b.shape
    elif mode == "nt":
        (m, k), (n, _) = a.shape, b.shape
    else:
        (k, m), (_, n) = a.shape, b.shape
    tm, tn, tk = _tile(m, 1024), _tile(n, 512), (k if k <= 2048 else _tile(k, 512))
    if mode == "tn":
        tm, tn, tk = _tile(m, 1024), _tile(n, 1024), _tile(k, 1024)
    nk = k // tk
    dims = {"nn": NN, "nt": NT, "tn": TN}[mode]

    def body(a_ref, b_ref, *rest):
        o_ref, acc_ref = rest[-2], rest[-1]
        part = _bdot(a_ref[...], b_ref[...], dims)
        if nk == 1:
            o_ref[...] = part if add is None else part + rest[0][...]
            return
        kk = pl.program_id(2)

        @pl.when(kk == 0)
        def _():
            acc_ref[...] = part if add is None else part + rest[0][...]

        @pl.when(kk > 0)
        def _():
            acc_ref[...] += part

        @pl.when(kk == nk - 1)
        def _():
            o_ref[...] = acc_ref[...]

    a_spec = pl.BlockSpec((tk, tm), lambda i, j, q: (q, i)) if mode == "tn" else pl.BlockSpec((tm, tk), lambda i, j, q: (i, q))
    b_spec = pl.BlockSpec((tn, tk), lambda i, j, q: (j, q)) if mode == "nt" else pl.BlockSpec((tk, tn), lambda i, j, q: (q, j))
    o_spec = pl.BlockSpec((tm, tn), lambda i, j, q: (i, j))
    return pl.pallas_call(
        body, name=name, grid=(m // tm, n // tn, nk),
        in_specs=[a_spec, b_spec] + ([] if add is None else [o_spec]), out_specs=o_spec,
        out_shape=SDS((m, n), F32), scratch_shapes=[pltpu.VMEM((tm, tn), F32)],
        compiler_params=_params(("parallel", "parallel", "arbitrary"), VMEM_MID),
    )(a, b, *([] if add is None else [add]))


def make_mm(name, pass_input=False):
    def primal(a, w):
        out = _matmul(a, w, "nn", name + "_fwd")
        return (out, a) if pass_input else out

    @jax.custom_vjp
    def mm(a, w, grad_slot):
        return primal(a, w)

    def fwd(a, w, grad_slot):
        return primal(a, w), (a, w)

    def bwd(res, g):
        a, w = res
        g, g_other = g if pass_input else (g, None)
        return _matmul(g, w, "nt", name + "_dx", add=g_other), jnp.zeros_like(w), _matmul(a, g, "tn", name + "_dw")

    mm.defvjp(fwd, bwd)
    return mm


PROJ_ROWS = 256


def _proj_fwd_call(a, ws, name):
    m, k = a.shape
    tm = _tile(m, PROJ_ROWS)
    nw = len(ws)

    def body(*refs):
        ab = refs[0][...].astype(BF16)
        for w_ref, o_ref in zip(refs[1:1 + nw], refs[1 + nw:]):
            o_ref[...] = lax.dot_general(ab, w_ref[...], NN, preferred_element_type=F32)

    return pl.pallas_call(
        body, name=name, grid=(m // tm,),
        in_specs=[pl.BlockSpec((tm, k), lambda i: (i, 0))] + [pl.BlockSpec(w.shape, lambda i: (0, 0)) for w in ws],
        out_specs=[pl.BlockSpec((tm, w.shape[1]), lambda i: (i, 0)) for w in ws],
        out_shape=[SDS((m, w.shape[1]), F32) for w in ws],
        compiler_params=_params(("parallel",), VMEM_BIG),
    )(a, *ws)


def _proj_dx_call(gs, ws, name):
    m = gs[0].shape[0]
    k = ws[0].shape[0]
    tm = _tile(m, PROJ_ROWS)
    nw = len(ws)

    def body(*refs):
        acc = None
        for g_ref, w_ref in zip(refs[:nw], refs[nw:2 * nw]):
            part = _bdot(g_ref[...], w_ref[...], NT)
            acc = part if acc is None else acc + part
        refs[2 * nw][...] = acc

    return pl.pallas_call(
        body, name=name, grid=(m // tm,),
        in_specs=[pl.BlockSpec((tm, g.shape[1]), lambda i: (i, 0)) for g in gs] + [pl.BlockSpec(w.shape, lambda i: (0, 0)) for w in ws],
        out_specs=pl.BlockSpec((tm, k), lambda i: (i, 0)), out_shape=SDS((m, k), F32),
        compiler_params=_params(("parallel",), VMEM_BIG),
    )(*gs, *ws)


def make_proj(name):
    @jax.custom_vjp
    def proj(a, ws, grad_slots):
        return tuple(_proj_fwd_call(a, ws, name + "_fwd"))

    def fwd(a, ws, grad_slots):
        return tuple(_proj_fwd_call(a, ws, name + "_fwd")), (a, ws)

    def bwd(res, gs):
        a, ws = res
        dws = tuple(_matmul(a, g, "tn", "%s_dw%d" % (name, i)) for i, g in enumerate(gs))
        return _proj_dx_call(tuple(gs), ws, name + "_dx"), tuple(jnp.zeros_like(w) for w in ws), dws

    proj.defvjp(fwd, bwd)
    return proj


def make_rowwise(f, name, tm, n_rows, n_params, vmem=VMEM_MID, pass_first=False):
    def specs_of(arrs, blocked):
        if blocked:
            return [pl.BlockSpec((tm, a.shape[1]), lambda i: (i, 0)) for a in arrs]
        return [pl.BlockSpec(a.shape, lambda i: (0, 0)) for a in arrs]

    def out_structs(rows, params):
        blk = [SDS((tm, r.shape[1]), r.dtype) for r in rows] + [SDS(p.shape, p.dtype) for p in params]
        return jax.eval_shape(f, *blk)

    def run_fwd(rows, params):
        L = rows[0].shape[0]
        outs = out_structs(rows, params)

        def body(*refs):
            ins = [r[...] for r in refs[:n_rows + n_params]]
            res = f(*ins)
            for o_ref, val in zip(refs[n_rows + n_params:], res):
                o_ref[...] = val

        return pl.pallas_call(
            body, name=name + "_fwd", grid=(L // tm,),
            in_specs=specs_of(rows, True) + specs_of(params, False),
            out_specs=[pl.BlockSpec((tm, o.shape[1]), lambda i: (i, 0)) for o in outs],
            out_shape=[SDS((L, o.shape[1]), o.dtype) for o in outs],
            compiler_params=_params(("parallel",), vmem),
        )(*rows, *params)

    def run_bwd(rows, params, gs):
        L = rows[0].shape[0]
        n_g = len(gs)

        def body(*refs):
            i = pl.program_id(0)
            ins = [r[...] for r in refs[:n_rows + n_params]]
            cts = tuple(r[...] for r in refs[n_rows + n_params:n_rows + n_params + n_g])
            outs = refs[n_rows + n_params + n_g:]
            _, vjp = jax.vjp(f, *ins)
            grads = vjp(cts[:-1] if pass_first else cts)
            if pass_first:
                grads = (grads[0] + cts[-1],) + tuple(grads[1:])
            for o_ref, val in zip(outs[:n_rows], grads[:n_rows]):
                o_ref[...] = val

            if n_params:
                @pl.when(i == 0)
                def _():
                    for o_ref in outs[n_rows:]:
                        o_ref[...] = jnp.zeros_like(o_ref)
                for o_ref, val in zip(outs[n_rows:], grads[n_rows:]):
                    o_ref[...] += val

        res = pl.pallas_call(
            body, name=name + "_bwd", grid=(L // tm,),
            in_specs=specs_of(rows, True) + specs_of(params, False) + specs_of(gs, True),
            out_specs=specs_of(rows, True) + specs_of(params, False),
            out_shape=[SDS(r.shape, r.dtype) for r in rows] + [SDS(p.shape, p.dtype) for p in params],
            compiler_params=_params(("arbitrary",), vmem),
        )(*rows, *params, *gs)
        return tuple(res[:n_rows]), tuple(res[n_rows:])

    def outputs(rows, params):
        outs = tuple(run_fwd(rows, params))
        return outs + (rows[0],) if pass_first else outs

    @jax.custom_vjp
    def op(rows, params):
        return outputs(rows, params)

    def fwd(rows, params):
        return outputs(rows, params), (rows, params)

    def bwd(res, gs):
        rows, params = res
        return run_bwd(rows, params, tuple(gs))

    op.defvjp(fwd, bwd)
    op.run_fwd, op.run_bwd = run_fwd, run_bwd
    return op


def make_residual(name, tm):
    full = make_rowwise(_f_res, name, tm, 2, 2)
    branch = make_rowwise(lambda y, gate, bgate: ((gate + bgate) * y,), name + "_branch", tm, 1, 2)

    @jax.custom_vjp
    def op(x, y, gate, bgate):
        return full.run_fwd((x, y), (gate, bgate))[0]

    def fwd(x, y, gate, bgate):
        return full.run_fwd((x, y), (gate, bgate))[0], (y, gate, bgate)

    def bwd(res, g):
        y, gate, bgate = res
        (dy,), (dgate, dbgate) = branch.run_bwd((y,), (gate, bgate), (g,))
        return g, dy, dgate, dbgate

    op.defvjp(fwd, bwd)
    return op


def _s5_scan_rows(xr_ref, xi_ref, ar, ai, x0r, x0i, tl, reverse=False):
    n = xr_ref.shape[1]
    T = SUBLANES
    row = lax.broadcasted_iota(jnp.int32, (T, n), 0)
    pr, pi = [ar], [ai]
    for _ in range(T - 1):
        pr, pi = pr + [pr[-1] * ar - pi[-1] * ai], pi + [pr[-1] * ai + pi[-1] * ar]
    levels = []
    for d in (1, 2, 4):
        mask = (row < T - d) if reverse else (row >= d)
        levels.append((T - d if reverse else d, jnp.where(mask, pr[d - 1], 0.0), jnp.where(mask, pi[d - 1], 0.0)))
    cr = jnp.zeros((T, n), F32)
    ci = jnp.zeros((T, n), F32)
    for r in range(T):
        k = (T - r) if reverse else (r + 1)
        cr = jnp.where(row == r, pr[k - 1], cr)
        ci = jnp.where(row == r, pi[k - 1], ci)
    nt = tl // T
    last = 0 if reverse else T - 1

    def step(t, carry):
        sr, si = carry
        base = pl.multiple_of((nt - 1 - t if reverse else t) * T, T)
        br = xr_ref[pl.ds(base, T), :]
        bi = xi_ref[pl.ds(base, T), :]
        for shift, mr, mi in levels:
            qr = pltpu.roll(br, shift, 0)
            qi = pltpu.roll(bi, shift, 0)
            br, bi = br + (mr * qr - mi * qi), bi + (mr * qi + mi * qr)
        xr = br + (cr * sr - ci * si)
        xi = bi + (cr * si + ci * sr)
        xr_ref[pl.ds(base, T), :] = xr
        xi_ref[pl.ds(base, T), :] = xi
        return xr[last:last + 1, :], xi[last:last + 1, :]
    return lax.fori_loop(0, nt, step, (x0r, x0i))


def _s5_fwd_call(u, bre, bim, cre, cim, a, d, tl):
    L, e = u.shape
    nb = e // LANES
    ns = bre.shape[2]
    nc = L // tl

    def body(u_ref, bre_ref, bim_ref, cre_ref, cim_ref, a_ref, d_ref, y_ref, xb_ref, sr_ref, si_ref, xr_ref, xi_ref, carry_ref):
        c = pl.program_id(1)

        @pl.when(c == 0)
        def _():
            carry_ref[...] = jnp.zeros_like(carry_ref)
        xb_ref[0, 0] = carry_ref[...]
        ub = u_ref[...]
        xr_ref[...] = _bdot(ub, bre_ref[0])
        xi_ref[...] = _bdot(ub, bim_ref[0])
        ar = a_ref[0, 0:1, :]
        ai = a_ref[0, 1:2, :]
        xr, xi = _s5_scan_rows(xr_ref, xi_ref, ar, ai, carry_ref[0:1, :], carry_ref[1:2, :], tl)
        carry_ref[0:1, :] = xr
        carry_ref[1:2, :] = xi
        sr = xr_ref[...].astype(BF16)
        si = xi_ref[...].astype(BF16)
        sr_ref[...] = sr
        si_ref[...] = si
        y_ref[...] = _f_s5_act(_bdot(sr, cre_ref[0]) - _bdot(si, cim_ref[0]), ub, d_ref[...])[0]

    return pl.pallas_call(
        body, name="s5_core_fwd", grid=(nb, nc),
        in_specs=[pl.BlockSpec((tl, LANES), lambda j, c: (c, j)),
                  pl.BlockSpec((1, LANES, ns), lambda j, c: (j, 0, 0)), pl.BlockSpec((1, LANES, ns), lambda j, c: (j, 0, 0)),
                  pl.BlockSpec((1, ns, LANES), lambda j, c: (j, 0, 0)), pl.BlockSpec((1, ns, LANES), lambda j, c: (j, 0, 0)),
                  pl.BlockSpec((1, SUBLANES, ns), lambda j, c: (j, 0, 0)), pl.BlockSpec((1, LANES), lambda j, c: (0, j))],
        out_specs=[pl.BlockSpec((tl, LANES), lambda j, c: (c, j)),
                   pl.BlockSpec((1, 1, SUBLANES, ns), lambda j, c: (j, c, 0, 0)),
                   pl.BlockSpec((tl, ns), lambda j, c: (c, j)), pl.BlockSpec((tl, ns), lambda j, c: (c, j))],
        out_shape=[SDS((L, e), F32), SDS((nb, nc, SUBLANES, ns), F32), SDS((L, nb * ns), BF16), SDS((L, nb * ns), BF16)],
        scratch_shapes=[pltpu.VMEM((tl, ns), F32), pltpu.VMEM((tl, ns), F32), pltpu.VMEM((SUBLANES, ns), F32)],
        compiler_params=_params(("arbitrary", "arbitrary"), VMEM_MID),
    )(u, bre, bim, cre, cim, a, d)


def _s5_bwd_call(u, dy2, bre, bim, cre, cim, a, d, xb, sr, si, tl):
    L, e = u.shape
    nb = e // LANES
    ns = bre.shape[2]
    nc = L // tl

    def body(u_ref, dy2_ref, bre_ref, bim_ref, cre_ref, cim_ref, a_ref, d_ref, xb_ref, sr_ref, si_ref,
             du_ref, dbre_ref, dbim_ref, dcre_ref, dcim_ref, da_ref, dd_ref,
             gr_ref, gi_ref, gcarry_ref):
        c = pl.program_id(1)

        @pl.when(c == 0)
        def _():
            gcarry_ref[...] = jnp.zeros_like(gcarry_ref)
            dbre_ref[...] = jnp.zeros_like(dbre_ref)
            dbim_ref[...] = jnp.zeros_like(dbim_ref)
            dcre_ref[...] = jnp.zeros_like(dcre_ref)
            dcim_ref[...] = jnp.zeros_like(dcim_ref)
            da_ref[...] = jnp.zeros_like(da_ref)
            dd_ref[...] = jnp.zeros_like(dd_ref)

        ub = u_ref[...]
        ys = _bdot(sr_ref[...], cre_ref[0]) - _bdot(si_ref[...], cim_ref[0])
        _, act_vjp = jax.vjp(lambda *t: _f_s5_act(*t)[0], ys, ub, d_ref[...])
        dy, du_skip, dd = act_vjp(dy2_ref[...])
        dd_ref[...] += dd
        ar = a_ref[0, 0:1, :]
        ai = a_ref[0, 1:2, :]
        x0r = xb_ref[0, 0, 0:1, :]
        x0i = xb_ref[0, 0, 1:2, :]
        dcre_ref[0] += _bdot(sr_ref[...], dy, TN)
        dcim_ref[0] -= _bdot(si_ref[...], dy, TN)
        gr_ref[...] = _bdot(dy, cre_ref[0], NT)
        gi_ref[...] = -_bdot(dy, cim_ref[0], NT)

        g0r, g0i = _s5_scan_rows(gr_ref, gi_ref, ar, -ai, gcarry_ref[0:1, :], gcarry_ref[1:2, :], tl, reverse=True)
        gcarry_ref[0:1, :] = g0r
        gcarry_ref[1:2, :] = g0i
        row = lax.broadcasted_iota(jnp.int32, (tl, ns), 0)
        gr = gr_ref[...]
        gi = gi_ref[...]
        xpr = jnp.where(row == 0, x0r, pltpu.roll(sr_ref[...].astype(F32), 1, 0))
        xpi = jnp.where(row == 0, x0i, pltpu.roll(si_ref[...].astype(F32), 1, 0))
        da_ref[0, 0:1, :] += jnp.sum(gr * xpr + gi * xpi, axis=0, keepdims=True)
        da_ref[0, 1:2, :] += jnp.sum(gi * xpr - gr * xpi, axis=0, keepdims=True)
        du_ref[...] = (_bdot(gr, bre_ref[0], NT) + _bdot(gi, bim_ref[0], NT)) + du_skip
        dbre_ref[0] += _bdot(ub, gr, TN)
        dbim_ref[0] += _bdot(ub, gi, TN)

    rev = lambda c: nc - 1 - c
    return pl.pallas_call(
        body, name="s5_core_bwd", grid=(nb, nc),
        in_specs=[pl.BlockSpec((tl, LANES), lambda j, c: (rev(c), j)), pl.BlockSpec((tl, LANES), lambda j, c: (rev(c), j)),
                  pl.BlockSpec((1, LANES, ns), lambda j, c: (j, 0, 0)), pl.BlockSpec((1, LANES, ns), lambda j, c: (j, 0, 0)),
                  pl.BlockSpec((1, ns, LANES), lambda j, c: (j, 0, 0)), pl.BlockSpec((1, ns, LANES), lambda j, c: (j, 0, 0)),
                  pl.BlockSpec((1, SUBLANES, ns), lambda j, c: (j, 0, 0)), pl.BlockSpec((1, LANES), lambda j, c: (0, j)),
                  pl.BlockSpec((1, 1, SUBLANES, ns), lambda j, c: (j, rev(c), 0, 0)),
                  pl.BlockSpec((tl, ns), lambda j, c: (rev(c), j)), pl.BlockSpec((tl, ns), lambda j, c: (rev(c), j))],
        out_specs=[pl.BlockSpec((tl, LANES), lambda j, c: (rev(c), j)),
                   pl.BlockSpec((1, LANES, ns), lambda j, c: (j, 0, 0)), pl.BlockSpec((1, LANES, ns), lambda j, c: (j, 0, 0)),
                   pl.BlockSpec((1, ns, LANES), lambda j, c: (j, 0, 0)), pl.BlockSpec((1, ns, LANES), lambda j, c: (j, 0, 0)),
                   pl.BlockSpec((1, SUBLANES, ns), lambda j, c: (j, 0, 0)), pl.BlockSpec((1, LANES), lambda j, c: (0, j))],
        out_shape=[SDS((L, e), F32), SDS(bre.shape, F32), SDS(bim.shape, F32), SDS(cre.shape, F32), SDS(cim.shape, F32),
                   SDS(a.shape, F32), SDS(d.shape, F32)],
        scratch_shapes=[pltpu.VMEM((tl, ns), F32) for _ in range(2)] + [pltpu.VMEM((SUBLANES, ns), F32)],
        compiler_params=_params(("arbitrary", "arbitrary"), VMEM_MID),
    )(u, dy2, bre, bim, cre, cim, a, d, xb, sr, si)


def make_s5_core(tl):
    @jax.custom_vjp
    def s5_core(u, bre, bim, cre, cim, a, d):
        return _s5_fwd_call(u, bre, bim, cre, cim, a, d, tl)[0]

    def fwd(u, bre, bim, cre, cim, a, d):
        y2, xb, sr, si = _s5_fwd_call(u, bre, bim, cre, cim, a, d, tl)
        return y2, (u, bre, bim, cre, cim, a, d, xb, sr, si)

    def bwd(res, dy2):
        u, bre, bim, cre, cim, a, d, xb, sr, si = res
        return tuple(_s5_bwd_call(u, dy2, bre, bim, cre, cim, a, d, xb, sr, si, tl))

    s5_core.defvjp(fwd, bwd)
    return s5_core


def _s5_block_params(lam_re, lam_im, log_dt, b_re, b_im, c_re, c_im):
    dt = jnp.exp(log_dt)[:, None]
    mag = jnp.exp(lam_re * dt)
    ab_re = mag * jnp.cos(lam_im * dt)
    ab_im = mag * jnp.sin(lam_im * dt)
    den = lam_re * lam_re + lam_im * lam_im
    nr = ab_re - 1.0
    ni = ab_im
    q_re = (nr * lam_re + ni * lam_im) / den
    q_im = (ni * lam_re - nr * lam_im) / den
    bb_re = q_re[..., None] * b_re - q_im[..., None] * b_im
    bb_im = q_re[..., None] * b_im + q_im[..., None] * b_re
    nb = S5_GROUPS // S5_GB
    eye = jnp.eye(S5_GB, dtype=F32)

    def bdiag_in(bb):
        t = bb.reshape(nb, S5_GB, S5_STATE, S5_GROUP)
        t = jnp.einsum("jgpm,gh->jgmhp", t, eye)
        return t.reshape(nb, S5_GB * S5_GROUP, S5_GB * S5_STATE)

    def bdiag_out(cc):
        t = cc.reshape(nb, S5_GB, S5_GROUP, S5_STATE)
        t = jnp.einsum("jgmp,gh->jgphm", t, eye)
        return t.reshape(nb, S5_GB * S5_STATE, S5_GB * S5_GROUP)

    a = jnp.stack([ab_re.reshape(nb, S5_GB * S5_STATE), ab_im.reshape(nb, S5_GB * S5_STATE)], axis=1)
    a = jnp.concatenate([a, jnp.zeros((nb, SUBLANES - 2, S5_GB * S5_STATE), F32)], axis=1)
    return bdiag_in(bb_re), bdiag_in(bb_im), bdiag_out(c_re), bdiag_out(c_im), a


def _shift_down(x, s, row):
    if s == 0:
        return x
    return jnp.where(row >= s, pltpu.roll(x, s, 0), 0.0)


def _shift_up(x, s, row, n):
    if s == 0:
        return x
    return jnp.where(row < n - s, pltpu.roll(x, n - s, 0), 0.0)


def _causal_conv(xv, w_ref, row):
    acc = jnp.zeros_like(xv)
    for j in range(GDN_CONV):
        acc += w_ref[j:j + 1, :] * _shift_down(xv, GDN_CONV - 1 - j, row)
    return acc


def _conv_fwd_call(x, w, act, name):
    L, ch = x.shape

    def body(x_ref, w_ref, y_ref):
        xv = x_ref[...]
        row = lax.broadcasted_iota(jnp.int32, xv.shape, 0)
        y_ref[...] = act(_causal_conv(xv, w_ref, row))

    return pl.pallas_call(
        body, name=name + "_fwd", grid=(ch // LANES,),
        in_specs=[pl.BlockSpec((L, LANES), lambda j: (0, j)), pl.BlockSpec((SUBLANES, LANES), lambda j: (0, j))],
        out_specs=pl.BlockSpec((L, LANES), lambda j: (0, j)), out_shape=SDS((L, ch), F32),
        compiler_params=_params(("parallel",), VMEM_MID),
    )(x, w)


def _conv_bwd_call(x, w, dy, act, name):
    L, ch = x.shape

    def body(x_ref, w_ref, dy_ref, dx_ref, dw_ref):
        xv = x_ref[...]
        row = lax.broadcasted_iota(jnp.int32, xv.shape, 0)
        _, act_vjp = jax.vjp(act, _causal_conv(xv, w_ref, row))
        (g,) = act_vjp(dy_ref[...])
        acc = jnp.zeros_like(xv)
        dws = []
        for j in range(GDN_CONV):
            s = GDN_CONV - 1 - j
            acc += w_ref[j:j + 1, :] * _shift_up(g, s, row, L)
            dws.append(jnp.sum(g * _shift_down(xv, s, row), axis=0, keepdims=True))
        dx_ref[...] = acc
        dw_ref[...] = jnp.concatenate(dws + [jnp.zeros((SUBLANES - GDN_CONV, LANES), F32)], axis=0)

    return pl.pallas_call(
        body, name=name + "_bwd", grid=(ch // LANES,),
        in_specs=[pl.BlockSpec((L, LANES), lambda j: (0, j)), pl.BlockSpec((SUBLANES, LANES), lambda j: (0, j)),
                  pl.BlockSpec((L, LANES), lambda j: (0, j))],
        out_specs=[pl.BlockSpec((L, LANES), lambda j: (0, j)), pl.BlockSpec((SUBLANES, LANES), lambda j: (0, j))],
        out_shape=[SDS((L, ch), F32), SDS((SUBLANES, ch), F32)],
        compiler_params=_params(("parallel",), VMEM_MID),
    )(x, w, dy)


def make_conv_act(act, name):
    @jax.custom_vjp
    def op(x, w):
        return _conv_fwd_call(x, w, act, name)

    def fwd(x, w):
        return _conv_fwd_call(x, w, act, name), (x, w)

    def bwd(res, dy):
        x, w = res
        return tuple(_conv_bwd_call(x, w, dy, act, name))

    op.defvjp(fwd, bwd)
    return op


gdn_conv = make_conv_act(lambda c: c, "gdn_conv")


BNN = (((2,), (1,)), ((0,), (0,)))
BNT = (((2,), (2,)), ((0,), (0,)))
BTN = (((1,), (1,)), ((0,), (0,)))
GDN_PREP_BATCH = 8


@jax.custom_vjp
def _known_inverse(a, t):
    return t


def _known_inverse_fwd(a, t):
    return t, t


def _known_inverse_bwd(t, g):
    return -_hdot(_hdot(t, g, _BTN), t, _BNT), jnp.zeros_like(t)


_known_inverse.defvjp(_known_inverse_fwd, _known_inverse_bwd)


def _gdn_prep_math(q, k, v, beta, g, t_saved=None):
    B, C = q.shape[0], q.shape[1]
    ri = lax.broadcasted_iota(jnp.int32, (B, C, C), 1)
    ci = lax.broadcasted_iota(jnp.int32, (B, C, C), 2)
    causal = ri >= ci
    strict = ri > ci
    eye = (ri == ci).astype(F32)
    gb = jnp.broadcast_to(g, (B, C, C))
    g_row = jnp.sum(gb * eye, axis=1, keepdims=True)
    gc_col = jnp.sum(jnp.where(causal, jnp.broadcast_to(g_row, (B, C, C)), 0.0), axis=2, keepdims=True)
    gc_row = jnp.sum(jnp.where(ri <= ci, gb, 0.0), axis=1, keepdims=True)
    decay = jnp.exp(jnp.where(causal, gc_col - gc_row, -jnp.inf))
    kk = _bdot(k, k, BNT)
    a_mat = jnp.where(strict, beta * kk * decay, 0.0)
    t = _unit_lower_inverse(a_mat) if t_saved is None else _known_inverse(a_mat, t_saved)
    e_gc = jnp.exp(gc_col)
    w = _hdot(t, beta * e_gc * k, BNN)
    u = _hdot(t, beta * v, BNN)
    qk = _bdot(q, k, BNT) * decay
    q_dec = q * e_gc
    g_last = gc_col[:, C - 1:C, :]
    k_dec = k * jnp.exp(g_last - gc_col)
    return q_dec, w, u, qk, k_dec, gc_col, t


def _gdn_prep_specs(L):
    C = GDN_CHUNK
    nb = min(GDN_PREP_BATCH, L // C)
    R = nb * C
    ins = [pl.BlockSpec((R, GDN_DK), lambda c, h: (c, h)), pl.BlockSpec((R, GDN_DK), lambda c, h: (c, h)),
           pl.BlockSpec((R, GDN_DV), lambda c, h: (c, h)), pl.BlockSpec((R, LANES), lambda c, h: (c, 0))]
    outs = [pl.BlockSpec((1, R, GDN_DK), lambda c, h: (h, c, 0)), pl.BlockSpec((1, R, GDN_DK), lambda c, h: (h, c, 0)),
            pl.BlockSpec((1, R, GDN_DV), lambda c, h: (h, c, 0)), pl.BlockSpec((1, R, C), lambda c, h: (h, c, 0)),
            pl.BlockSpec((1, R, GDN_DK), lambda c, h: (h, c, 0)), pl.BlockSpec((1, R, 1), lambda c, h: (h, c, 0))]
    t_spec = pl.BlockSpec((1, R, C), lambda c, h: (h, c, 0))
    shapes = [SDS((GDN_HEADS, L, GDN_DK), F32), SDS((GDN_HEADS, L, GDN_DK), F32), SDS((GDN_HEADS, L, GDN_DV), F32),
              SDS((GDN_HEADS, L, C), F32), SDS((GDN_HEADS, L, GDN_DK), F32), SDS((GDN_HEADS, L, 1), F32)]
    return ins, outs, t_spec, shapes, nb


def _chunks(x, nb):
    return x.reshape(nb, x.shape[0] // nb, x.shape[1])


def _head_columns(bg, h):
    lane = lax.broadcasted_iota(jnp.int32, bg.shape, 1)
    beta = jnp.sum(jnp.where(lane == h, bg, 0.0), axis=1, keepdims=True)
    g = jnp.sum(jnp.where(lane == h + GDN_HEADS, bg, 0.0), axis=1, keepdims=True)
    return beta, g


def _gdn_prep_fwd_call(q, k, v, bg):
    L = q.shape[0]
    ins, outs, t_spec, shapes, nb = _gdn_prep_specs(L)

    def body(q_ref, k_ref, v_ref, bg_ref, *o_refs):
        beta, g = _head_columns(bg_ref[...], pl.program_id(1))
        res = _gdn_prep_math(_chunks(q_ref[...], nb), _chunks(k_ref[...], nb), _chunks(v_ref[...], nb),
                             _chunks(beta, nb), _chunks(g, nb))
        for o_ref, val in zip(o_refs, res):
            o_ref[0] = val.reshape(val.shape[0] * val.shape[1], val.shape[2])

    return pl.pallas_call(
        body, name="gdn_prep_fwd", grid=(L // (nb * GDN_CHUNK), GDN_HEADS), in_specs=ins, out_specs=outs + [t_spec],
        out_shape=shapes + [SDS((GDN_HEADS, L, GDN_CHUNK), F32)],
        compiler_params=_params(("parallel", "parallel"), VMEM_MID),
    )(q, k, v, bg)


def _gdn_prep_bwd_call(q, k, v, bg, t, cts):
    L = q.shape[0]
    ins, outs, t_spec, _, nb = _gdn_prep_specs(L)

    def body(q_ref, k_ref, v_ref, bg_ref, t_ref, c0, c1, c2, c3, c4, c5, dq_ref, dk_ref, dv_ref, dbg_ref):
        h = pl.program_id(1)
        beta, g = _head_columns(bg_ref[...], h)
        t_saved = _chunks(t_ref[0], nb)
        _, vjp = jax.vjp(lambda *a: _gdn_prep_math(*a, t_saved=t_saved)[:6], _chunks(q_ref[...], nb), _chunks(k_ref[...], nb),
                         _chunks(v_ref[...], nb), _chunks(beta, nb), _chunks(g, nb))
        dq, dk, dv, db, dg = vjp(tuple(_chunks(c[0], nb) for c in (c0, c1, c2, c3, c4, c5)))
        flat = lambda a: a.reshape(a.shape[0] * a.shape[1], a.shape[2])
        dq_ref[...] = flat(dq)
        dk_ref[...] = flat(dk)
        dv_ref[...] = flat(dv)

        @pl.when(h == 0)
        def _():
            dbg_ref[...] = jnp.zeros_like(dbg_ref)
        lane = lax.broadcasted_iota(jnp.int32, dbg_ref.shape, 1)
        dbg_ref[...] += jnp.where(lane == h, flat(db), 0.0) + jnp.where(lane == h + GDN_HEADS, flat(dg), 0.0)

    return pl.pallas_call(
        body, name="gdn_prep_bwd", grid=(L // (nb * GDN_CHUNK), GDN_HEADS), in_specs=ins + [t_spec] + outs, out_specs=ins,
        out_shape=[SDS(q.shape, F32), SDS(k.shape, F32), SDS(v.shape, F32), SDS(bg.shape, F32)],
        compiler_params=_params(("parallel", "arbitrary"), VMEM_MID),
    )(q, k, v, bg, t, *cts)


@jax.custom_vjp
def gdn_prep(q, k, v, bg):
    return tuple(_gdn_prep_fwd_call(q, k, v, bg)[:6])


def _gdn_prep_f(q, k, v, bg):
    res = _gdn_prep_fwd_call(q, k, v, bg)
    return tuple(res[:6]), (q, k, v, bg, res[6])


def _gdn_prep_b(res, cts):
    return tuple(_gdn_prep_bwd_call(*res, tuple(cts)))


gdn_prep.defvjp(_gdn_prep_f, _gdn_prep_b)


def _gdn_step_math(q_dec, w, u, qk, k_dec, gc, z, nw, state):
    H, C = q_dec.shape[0], q_dec.shape[1]
    v_new = u - _bdot(w, state, BNN)
    o = _bdot(q_dec, state, BNN) + _bdot(qk, v_new, BNN)
    gl = gc[:, C - 1:C, :]
    new_state = jnp.exp(gl) * state + _bdot(k_dec, v_new, BTN)
    return _f_gdn_post(jnp.concatenate([o[h] for h in range(H)], axis=1), z, nw)[0], new_state


def _gdn_scan_specs(L, rev):
    C, H = GDN_CHUNK, GDN_HEADS
    nc = L // C
    cc = (lambda c: nc - 1 - c) if rev else (lambda c: c)
    ins = [pl.BlockSpec((H, C, GDN_DK), lambda c: (0, cc(c), 0)), pl.BlockSpec((H, C, GDN_DK), lambda c: (0, cc(c), 0)),
           pl.BlockSpec((H, C, GDN_DV), lambda c: (0, cc(c), 0)), pl.BlockSpec((H, C, C), lambda c: (0, cc(c), 0)),
           pl.BlockSpec((H, C, GDN_DK), lambda c: (0, cc(c), 0)), pl.BlockSpec((H, C, 1), lambda c: (0, cc(c), 0))]
    o_spec = pl.BlockSpec((C, H * GDN_DV), lambda c: (cc(c), 0))
    nw_spec = pl.BlockSpec((1, H * GDN_DV), lambda c: (0, 0))
    s_spec = pl.BlockSpec((1, H, GDN_DK, GDN_DV), lambda c: (cc(c), 0, 0, 0))
    return ins + [o_spec, nw_spec], o_spec, s_spec, nc


def _gdn_scan_fwd_call(q_dec, w, u, qk, k_dec, gc, z, nw):
    L = q_dec.shape[1]
    ins, o_spec, s_spec, nc = _gdn_scan_specs(L, False)

    def body(qd_ref, w_ref, u_ref, qk_ref, kd_ref, gc_ref, z_ref, nw_ref, o_ref, sin_ref, s_ref):
        c = pl.program_id(0)

        @pl.when(c == 0)
        def _():
            s_ref[...] = jnp.zeros_like(s_ref)
        st = s_ref[...]
        sin_ref[0] = st
        o, ns = _gdn_step_math(qd_ref[...], w_ref[...], u_ref[...], qk_ref[...], kd_ref[...], gc_ref[...], z_ref[...], nw_ref[...], st)
        o_ref[...] = o
        s_ref[...] = ns

    return pl.pallas_call(
        body, name="gdn_scan_fwd", grid=(nc,), in_specs=ins, out_specs=[o_spec, s_spec],
        out_shape=[SDS((L, GDN_HEADS * GDN_DV), F32), SDS((nc, GDN_HEADS, GDN_DK, GDN_DV), F32)],
        scratch_shapes=[pltpu.VMEM((GDN_HEADS, GDN_DK, GDN_DV), F32)],
        compiler_params=_params(("arbitrary",), VMEM_MID),
    )(q_dec, w, u, qk, k_dec, gc, z, nw)


def _gdn_scan_bwd_call(q_dec, w, u, qk, k_dec, gc, z, nw, s_in, do):
    L = q_dec.shape[1]
    ins, o_spec, s_spec, nc = _gdn_scan_specs(L, True)

    def body(qd_ref, w_ref, u_ref, qk_ref, kd_ref, gc_ref, z_ref, nw_ref, sin_ref, do_ref,
             dqd_ref, dw_ref, du_ref, dqk_ref, dkd_ref, dgc_ref, dz_ref, dnw_ref, ds_ref):
        c = pl.program_id(0)

        @pl.when(c == 0)
        def _():
            ds_ref[...] = jnp.zeros_like(ds_ref)
            dnw_ref[...] = jnp.zeros_like(dnw_ref)
        _, vjp = jax.vjp(_gdn_step_math, qd_ref[...], w_ref[...], u_ref[...], qk_ref[...], kd_ref[...], gc_ref[...],
                         z_ref[...], nw_ref[...], sin_ref[0])
        dqd, dw, du, dqk, dkd, dgc, dz, dnw, dst = vjp((do_ref[...], ds_ref[...]))
        dqd_ref[...] = dqd
        dw_ref[...] = dw
        du_ref[...] = du
        dqk_ref[...] = dqk
        dkd_ref[...] = dkd
        dgc_ref[...] = dgc
        dz_ref[...] = dz
        dnw_ref[...] += dnw
        ds_ref[...] = dst

    return pl.pallas_call(
        body, name="gdn_scan_bwd", grid=(nc,), in_specs=ins + [s_spec, o_spec], out_specs=ins,
        out_shape=[SDS(t.shape, F32) for t in (q_dec, w, u, qk, k_dec, gc, z, nw)],
        scratch_shapes=[pltpu.VMEM((GDN_HEADS, GDN_DK, GDN_DV), F32)],
        compiler_params=_params(("arbitrary",), VMEM_MID),
    )(q_dec, w, u, qk, k_dec, gc, z, nw, s_in, do)


@jax.custom_vjp
def gdn_scan(q_dec, w, u, qk, k_dec, gc, z, nw):
    return _gdn_scan_fwd_call(q_dec, w, u, qk, k_dec, gc, z, nw)[0]


def _gdn_scan_f(*args):
    o, s_in = _gdn_scan_fwd_call(*args)
    return o, (*args, s_in)


def _gdn_scan_b(res, do):
    return tuple(_gdn_scan_bwd_call(*res, do))


gdn_scan.defvjp(_gdn_scan_f, _gdn_scan_b)


def _silu(x):
    return x * jax.nn.sigmoid(x)


def _gelu_tanh(x):
    return 0.5 * x * (1.0 + jnp.tanh(math.sqrt(2.0 / math.pi) * (x + 0.044715 * (x * x * x))))


def _f_lnmod(x, nw, sc, sh, bsc, bsh):
    xn = x * lax.rsqrt(jnp.mean(x * x, axis=-1, keepdims=True) + NORM_EPS) * nw
    return (xn * (1.0 + (sc + bsc)) + (sh + bsh),)


def _f_s5_act(ys, u, d):
    return (_gelu_tanh(ys + d * u),)


def _f_s5_gate(y2, t, z):
    return (y2 * jax.nn.sigmoid(t) * _silu(z),)


def _f_res(x, y, gate, bgate):
    return (x + (gate + bgate) * y,)


def _heads(x, width, fn):
    return jnp.concatenate([fn(x[:, i * width:(i + 1) * width]) for i in range(x.shape[1] // width)], axis=1)


def _l2n(x):
    return x * lax.rsqrt(jnp.sum(x * x, axis=-1, keepdims=True) + NORM_EPS)


def _f_qnorm(x):
    return (_heads(_silu(x), GDN_DK, _l2n) * (GDN_DK ** -0.5),)


def _f_knorm(x):
    return (_heads(_silu(x), GDN_DK, _l2n),)


def _f_vact(x):
    return (_silu(x),)


def _f_betag(ba, alog, dtb):
    col = lax.broadcasted_iota(jnp.int32, ba.shape, 1)
    t = ba + dtb
    softplus = jnp.maximum(t, 0.0) + jnp.log1p(jnp.exp(-jnp.abs(t)))
    g = -jnp.exp(alog) * softplus
    return (jnp.where(col < GDN_HEADS, jax.nn.sigmoid(ba), jnp.where(col < 2 * GDN_HEADS, g, 0.0)),)


def _f_gdn_post(o, z, nw):
    on = _heads(o, GDN_DV, lambda t: t * lax.rsqrt(jnp.mean(t * t, axis=-1, keepdims=True) + NORM_EPS))
    return (on * nw * _silu(z),)


def _f_loss(x, tgt, fw):
    y = x * lax.rsqrt(jnp.mean(x * x, axis=-1, keepdims=True) + NORM_EPS) * fw
    err = y - tgt
    return (0.5 * jnp.mean(err * err, axis=-1, keepdims=True),)


def _ada_mod_call(c_all, ada_w):
    n = ada_w.shape[2]

    def body(c_ref, w_ref, o_ref):
        ca = _silu(c_ref[...])
        for l in range(ada_w.shape[0]):
            o_ref[l] = _bdot(ca, w_ref[l])

    return pl.pallas_call(body, name="ada_mod", out_shape=SDS((ada_w.shape[0], N_DEV, n), F32),
                          compiler_params=_params(None, VMEM_MID))(c_all, ada_w)


def _ada_grad_call(c_all, dmod):
    nl, _, n = dmod.shape

    def body(c_ref, d_ref, o_ref):
        ca = _silu(c_ref[...])
        for l in range(nl):
            o_ref[l] = _hdot(ca, d_ref[l], TN)

    return pl.pallas_call(body, name="ada_grad", out_shape=SDS((nl, c_all.shape[1], n), F32),
                          compiler_params=_params(None, VMEM_MID))(c_all, dmod)


ADAM_ROWS = 512


def _adamw(g, w, m, v):
    m2 = ADAM_B1 * m + (1.0 - ADAM_B1) * g
    v2 = ADAM_B2 * v + (1.0 - ADAM_B2) * (g * g)
    m_hat = m2 / (1.0 - ADAM_B1 ** ADAM_STEP)
    v_hat = v2 / (1.0 - ADAM_B2 ** ADAM_STEP)
    return g, -ADAM_LR * (m_hat / (jnp.sqrt(v_hat) + ADAM_EPS) + ADAM_WD * w), m2, v2


def _adam_call(gs, w, m, v, name, rows=None):
    n, r, cols = gs.shape
    rows = rows or ADAM_ROWS

    def body(g_ref, w_ref, m_ref, v_ref, go_ref, d_ref, mo_ref, vo_ref):
        g = g_ref[0].astype(F32)
        for s in range(1, n):
            g = g + g_ref[s].astype(F32)
        for o_ref, val in zip((go_ref, d_ref, mo_ref, vo_ref), _adamw(g, w_ref[...], m_ref[...], v_ref[...])):
            o_ref[...] = val

    blk = pl.BlockSpec((rows, cols), lambda i: (i, 0))
    return pl.pallas_call(
        body, name=name, grid=(r // rows,),
        in_specs=[pl.BlockSpec((n, rows, cols), lambda i: (0, i, 0)), blk, blk, blk],
        out_specs=[blk, blk, blk, blk], out_shape=[SDS((r, cols), F32)] * 4,
        compiler_params=_params(("parallel",), VMEM_MID),
    )(gs, w, m, v)


def _sum_call(gs, name, rows):
    n, r, _ = gs.shape

    def body(g_ref, o_ref):
        g = g_ref[0].astype(F32)
        for s in range(1, n):
            g = g + g_ref[s].astype(F32)
        o_ref[...] = g

    return pl.pallas_call(
        body, name=name, grid=(r // rows,),
        in_specs=[pl.BlockSpec((n, rows, LANES), lambda i: (0, i, 0))],
        out_specs=pl.BlockSpec((rows, LANES), lambda i: (i, 0)), out_shape=SDS((r, LANES), F32),
        compiler_params=_params(("parallel",), VMEM_MID),
    )(gs)


def _allgather_call(x_shard, name, in_hbm):
    m_per, n = x_shard.shape

    def body(x_ref, out_ref, send_sems, recv_sems, local_sem):
        x, y, c = lax.axis_index("x"), lax.axis_index("y"), lax.axis_index("c")
        me, sibling = (x, y, c), (x, y, 1 - c)
        chips = [(1 - x, y), (x, 1 - y), (1 - x, 1 - y)]

        def rows(px, py, pc):
            return out_ref.at[pl.ds((4 * px + 2 * py + pc) * m_per, m_per), :]

        def copy(k, block, to, src=None):
            return pltpu.make_async_remote_copy(
                src_ref=rows(*block) if src is None else src, dst_ref=rows(*block),
                send_sem=send_sems.at[k], recv_sem=recv_sems.at[k], device_id=to, device_id_type=pl.DeviceIdType.MESH)

        mine = pltpu.make_async_copy(x_ref, rows(*me), local_sem)
        mine.start()
        first = [copy(0, me, sibling, src=x_ref)]
        first += [copy(1 + j, me, (*chip, c), src=x_ref) for j, chip in enumerate(chips)]
        for cp in first:
            cp.start()
        passed = [copy(4 + j, (*chip, c), sibling) for j, chip in enumerate(chips)]
        for j, chip in enumerate(chips):
            copy(1 + j, (*chip, c), me).wait_recv()
            passed[j].start()
        copy(0, sibling, me).wait_recv()
        for j, chip in enumerate(chips):
            copy(4 + j, (*chip, 1 - c), me).wait_recv()
        for cp in first + passed:
            cp.wait_send()
        mine.wait()

    space = pl.ANY if in_hbm else pltpu.VMEM
    return pl.pallas_call(
        body, name=name, out_shape=SDS((N_DEV * m_per, n), x_shard.dtype),
        in_specs=[pl.BlockSpec(memory_space=space)], out_specs=pl.BlockSpec(memory_space=space),
        scratch_shapes=[pltpu.SemaphoreType.DMA((7,)), pltpu.SemaphoreType.DMA((7,)), pltpu.SemaphoreType.DMA],
        compiler_params=_params(None, None if in_hbm else VMEM_BIG),
    )(x_shard)


def _gather_weights_call(shards, name):
    nw = len(shards)

    def body(*refs):
        x_refs, out_refs = refs[:nw], refs[nw:2 * nw]
        send_sems, recv_sems, local_sems = refs[2 * nw:]
        x, y, c = lax.axis_index("x"), lax.axis_index("y"), lax.axis_index("c")
        me, sibling = (x, y, c), (x, y, 1 - c)
        chips = [(1 - x, y), (x, 1 - y), (1 - x, 1 - y)]

        def slot(w, px, py, pc):
            return out_refs[w].at[4 * px + 2 * py + pc]

        def copy(w, k, block, to, src=None):
            dst = slot(w, *block)
            return pltpu.make_async_remote_copy(
                src_ref=dst if src is None else src, dst_ref=dst, send_sem=send_sems.at[7 * w + k],
                recv_sem=recv_sems.at[7 * w + k], device_id=to, device_id_type=pl.DeviceIdType.MESH)

        mines = [pltpu.make_async_copy(x_refs[w], slot(w, *me), local_sems.at[w]) for w in range(nw)]
        for cp in mines:
            cp.start()
        first = [copy(w, 0, me, sibling, src=x_refs[w]) for w in range(nw)]
        first += [copy(w, 1 + j, me, (*chip, c), src=x_refs[w]) for w in range(nw) for j, chip in enumerate(chips)]
        for cp in first:
            cp.start()
        passed = []
        for w in range(nw):
            for j, chip in enumerate(chips):
                copy(w, 1 + j, (*chip, c), me).wait_recv()
                fwd = copy(w, 4 + j, (*chip, c), sibling)
                fwd.start()
                passed.append(fwd)
        for w in range(nw):
            copy(w, 0, sibling, me).wait_recv()
            for j, chip in enumerate(chips):
                copy(w, 4 + j, (*chip, 1 - c), me).wait_recv()
        for cp in first + passed:
            cp.wait_send()
        for cp in mines:
            cp.wait()

    hbm = pl.BlockSpec(memory_space=pl.ANY)
    return pl.pallas_call(
        body, name=name, out_shape=[SDS((N_DEV,) + s.shape, s.dtype) for s in shards],
        in_specs=[hbm] * nw, out_specs=[hbm] * nw,
        scratch_shapes=[pltpu.SemaphoreType.DMA((7 * nw,)), pltpu.SemaphoreType.DMA((7 * nw,)), pltpu.SemaphoreType.DMA((nw,))],
    )(*shards)


def _pair_exchange_call(grads, name):
    nw = len(grads)

    def body(*refs):
        g_refs, got_refs = refs[:nw], refs[nw:2 * nw]
        send_sems, recv_sems = refs[2 * nw:]
        x, y, c = lax.axis_index("x"), lax.axis_index("y"), lax.axis_index("c")
        copies = []
        for w in range(nw):
            for j in range(4):
                give = pltpu.make_async_remote_copy(
                    src_ref=g_refs[w].at[2 * j + 1 - c], dst_ref=got_refs[w].at[j], send_sem=send_sems.at[4 * w + j],
                    recv_sem=recv_sems.at[4 * w + j], device_id=(x, y, 1 - c), device_id_type=pl.DeviceIdType.MESH)
                give.start()
                copies.append(give)
        for cp in copies:
            cp.wait()

    hbm = pl.BlockSpec(memory_space=pl.ANY)
    return pl.pallas_call(
        body, name=name, out_shape=[SDS((4,) + g.shape[1:], g.dtype) for g in grads], in_specs=[hbm] * nw, out_specs=[hbm] * nw,
        scratch_shapes=[pltpu.SemaphoreType.DMA((4 * nw,)), pltpu.SemaphoreType.DMA((4 * nw,))],
    )(*grads)


def _chip_exchange_call(parts, name):
    nw = len(parts)

    def body(*refs):
        p_refs, out_refs = refs[:nw], refs[nw:2 * nw]
        send_sems, recv_sems = refs[2 * nw:]
        x, y, c = lax.axis_index("x"), lax.axis_index("y"), lax.axis_index("c")
        chips = [(1 - x, y), (x, 1 - y), (1 - x, 1 - y)]
        copies = []
        for w in range(nw):
            for j, (px, py) in enumerate(chips):
                give = pltpu.make_async_remote_copy(
                    src_ref=p_refs[w].at[2 * px + py], dst_ref=out_refs[w].at[j], send_sem=send_sems.at[3 * w + j],
                    recv_sem=recv_sems.at[3 * w + j], device_id=(px, py, c), device_id_type=pl.DeviceIdType.MESH)
                give.start()
                copies.append(give)
        for cp in copies:
            cp.wait()

    hbm = pl.BlockSpec(memory_space=pl.ANY)
    return pl.pallas_call(
        body, name=name, out_shape=[SDS((3,) + p.shape[1:], p.dtype) for p in parts], in_specs=[hbm] * nw, out_specs=[hbm] * nw,
        scratch_shapes=[pltpu.SemaphoreType.DMA((3 * nw,)), pltpu.SemaphoreType.DMA((3 * nw,))],
    )(*parts)


_HBM = pl.BlockSpec(memory_space=pltpu.HBM)
_SEM = pl.BlockSpec(memory_space=pltpu.SEMAPHORE)
_DATAFLOW = pltpu.SideEffectType.DATAFLOW_SIDE_EFFECTING


def _spread_start_call(srcs, per_peer, name, after):
    nw = len(srcs)
    lands = [lax.empty((N_DEV,) + (s.shape[1:] if per_peer else s.shape), s.dtype) for s in srcs]

    def body(*refs):
        src_refs, land_refs = refs[:nw], refs[nw:2 * nw]
        send_sems, recv_sems, token = refs[2 * nw + 1], refs[2 * nw + 2], refs[-1]
        x, y, c = lax.axis_index("x"), lax.axis_index("y"), lax.axis_index("c")
        me = 4 * x + 2 * y + c
        for w in range(nw):
            for k in range(1, N_DEV):
                px = 1 - x if k & 4 else x
                py = 1 - y if k & 2 else y
                pc = 1 - c if k & 1 else c
                src = src_refs[w].at[4 * px + 2 * py + pc] if per_peer else src_refs[w]
                pltpu.make_async_remote_copy(
                    src_ref=src, dst_ref=land_refs[w].at[me], send_sem=send_sems.at[w], recv_sem=recv_sems.at[w],
                    device_id=(px, py, pc), device_id_type=pl.DeviceIdType.MESH).start()
        token[...] = jnp.zeros_like(token)

    hbm = lambda a: pltpu.with_memory_space_constraint(a, pltpu.HBM)
    res = pl.pallas_call(
        body, name=name,
        out_shape=(pltpu.SemaphoreType.DMA((nw,)), pltpu.SemaphoreType.DMA((nw,)))
        + tuple(pltpu.HBM(s.shape, s.dtype) for s in srcs) + tuple(pltpu.HBM(l.shape, l.dtype) for l in lands)
        + (SDS((SUBLANES, LANES), F32),),
        in_specs=[_HBM] * (2 * nw) + [pl.BlockSpec(memory_space=pl.ANY)],
        out_specs=(_SEM, _SEM) + (_HBM,) * (2 * nw) + (pl.BlockSpec(memory_space=pltpu.VMEM),),
        input_output_aliases={i: i + 2 for i in range(2 * nw)},
        compiler_params=pltpu.CompilerParams(has_side_effects=_DATAFLOW),
    )(*[hbm(s) for s in srcs], *[hbm(l) for l in lands], after)
    return res[0], res[1], res[2:2 + nw], res[2 + nw:2 + 2 * nw], res[-1]


def _spread_wait_call(send_sems, recv_sems, srcs, lands, after, name):
    nw = len(lands)

    def body(*refs):
        land_refs = refs[nw:2 * nw]
        s_sems, r_sems = refs[2 * nw], refs[2 * nw + 1]
        x, y, c = lax.axis_index("x"), lax.axis_index("y"), lax.axis_index("c")
        for w in range(nw):
            seven = land_refs[w].at[pl.ds(0, N_DEV - 1)]
            all_seven = pltpu.make_async_remote_copy(
                src_ref=seven, dst_ref=seven, send_sem=s_sems.at[w], recv_sem=r_sems.at[w],
                device_id=(x, y, c), device_id_type=pl.DeviceIdType.MESH)
            all_seven.wait_send()
            all_seven.wait_recv()

    res = pl.pallas_call(
        body, name=name,
        out_shape=tuple(pltpu.HBM(s.shape, s.dtype) for s in srcs) + tuple(pltpu.HBM(l.shape, l.dtype) for l in lands),
        in_specs=[_HBM] * (2 * nw) + [_SEM, _SEM, pl.BlockSpec(memory_space=pl.ANY)], out_specs=(_HBM,) * (2 * nw),
        input_output_aliases={i: i for i in range(2 * nw)},
        compiler_params=pltpu.CompilerParams(has_side_effects=_DATAFLOW),
    )(*srcs, *lands, send_sems, recv_sems, after)
    return res[:nw], res[nw:]


def _pair_sum_call(g, got, core, name):
    _, k, n = got.shape
    tr = _tile(k, 256)

    def body(c_ref, g_ref, got_ref, o_ref):
        o_ref[...] = (g_ref[...] + got_ref[...]).astype(o_ref.dtype)

    spec = pltpu.PrefetchScalarGridSpec(
        num_scalar_prefetch=1, grid=(4, k // tr),
        in_specs=[pl.BlockSpec((1, tr, n), lambda j, i, c: (2 * j + c[0], i, 0)), pl.BlockSpec((1, tr, n), lambda j, i, c: (j, i, 0))],
        out_specs=pl.BlockSpec((1, tr, n), lambda j, i, c: (j, i, 0)))
    return pl.pallas_call(body, name=name, grid_spec=spec, out_shape=SDS(got.shape, BF16),
                          compiler_params=_params(("parallel", "parallel"), VMEM_MID))(core, g, got)


def _adam_own_call(pair, chip, recv, w, m, v, name, rows):
    _, r, cols = recv.shape

    def body(chip_ref, p_ref, g_ref, w_ref, m_ref, v_ref, go_ref, d_ref, mo_ref, vo_ref):
        g = ((p_ref[0].astype(F32) + g_ref[0].astype(F32)) + g_ref[1].astype(F32)) + g_ref[2].astype(F32)
        for o_ref, val in zip((go_ref, d_ref, mo_ref, vo_ref), _adamw(g, w_ref[...], m_ref[...], v_ref[...])):
            o_ref[...] = val

    blk = pl.BlockSpec((rows, cols), lambda i, s: (i, 0))
    spec = pltpu.PrefetchScalarGridSpec(
        num_scalar_prefetch=1, grid=(r // rows,),
        in_specs=[pl.BlockSpec((1, rows, cols), lambda i, s: (s[0], i, 0)), pl.BlockSpec((3, rows, cols), lambda i, s: (0, i, 0)),
                  blk, blk, blk],
        out_specs=[blk, blk, blk, blk])
    return pl.pallas_call(body, name=name, grid_spec=spec, out_shape=[SDS((r, cols), F32)] * 4,
                          compiler_params=_params(("parallel",), VMEM_MID))(chip, pair, recv, w, m, v)


def _join_cols_call(w8, name):
    _, k, n = w8.shape
    tk = _tile(k, 256)

    def body(w_ref, o_ref):
        for s in range(N_DEV):
            o_ref[:, n * s:n * (s + 1)] = w_ref[s]

    return pl.pallas_call(body, name=name, grid=(k // tk,), in_specs=[pl.BlockSpec((N_DEV, tk, n), lambda i: (0, i, 0))],
                          out_specs=pl.BlockSpec((tk, N_DEV * n), lambda i: (i, 0)), out_shape=SDS((k, N_DEV * n), w8.dtype),
                          compiler_params=_params(("parallel",), VMEM_MID))(w8)


def _split_cols_call(g, name, dtype):
    k, n8 = g.shape
    n = n8 // N_DEV
    tk = _tile(k, 256)

    def body(g_ref, o_ref):
        for s in range(N_DEV):
            o_ref[s] = g_ref[:, n * s:n * (s + 1)].astype(dtype)

    return pl.pallas_call(body, name=name, grid=(k // tk,), in_specs=[pl.BlockSpec((tk, n8), lambda i: (i, 0))],
                          out_specs=pl.BlockSpec((N_DEV, tk, n), lambda i: (0, i, 0)), out_shape=SDS((N_DEV, k, n), dtype),
                          compiler_params=_params(("parallel",), VMEM_MID))(g)


def _pack(parts, rows_multiple):
    flat = jnp.concatenate([p.reshape(-1) for p in parts])
    unit = rows_multiple * LANES
    padded = -(-flat.shape[0] // unit) * unit
    flat = jnp.concatenate([flat, jnp.zeros((padded - flat.shape[0],), F32)])
    return flat.reshape(-1, LANES)


def _unpack(buf, shapes):
    flat = buf.reshape(-1)
    out, off = [], 0
    for s in shapes:
        n = math.prod(s)
        out.append(flat[off:off + n].reshape(s))
        off += n
    return out


def _row_tile(L):
    return 256 if L % 256 == 0 else L


def _layer0_mix(diff, const):
    x, mod, norm_w, lam_re, lam_im, log_dt, b_re, b_im, c_re, c_im, s5_d, *slots = diff
    ada_b, weights = const
    L = x.shape[0]
    tm = _row_tile(L)
    mods = mod.reshape(2, 1, D_MODEL)
    biases = ada_b.reshape(2, 1, D_MODEL)
    op_ln0 = make_rowwise(_f_lnmod, "ln0", tm, 1, 5, pass_first=True)
    h, x = op_ln0((x,), (norm_w.reshape(1, D_MODEL), mods[1], mods[0], biases[1], biases[0]))
    u, z = make_proj("s5_in")(h, tuple(weights), tuple(slots))
    blocks = _s5_block_params(lam_re, lam_im, log_dt, b_re, b_im, c_re, c_im)
    y2 = make_s5_core(min(S5_TL, L))(u, *blocks, s5_d.reshape(1, D_INNER))
    return x, y2, z


def _layer0_out(diff, weights):
    y2, z, *slots = diff
    tm = _row_tile(y2.shape[0])
    t, y2 = make_mm("s5_glu", pass_input=True)(y2, weights[0], slots[0])
    (y4,) = make_rowwise(_f_s5_gate, "s5_gate", tm, 3, 0)((y2, t, z), ())
    return make_mm("s5_out")(y4, weights[1], slots[1])


def _f_res_lnmod(x, o, gate, bgate, nw, sc, sh, bsc, bsh):
    (x1,) = _f_res(x, o, gate, bgate)
    return _f_lnmod(x1, nw, sc, sh, bsc, bsh) + (x1,)


def _f_res_loss(x, y, tgt, gate, bgate, fw):
    return _f_loss(_f_res(x, y, gate, bgate)[0], tgt, fw)


def _layer1_loss(diff, const):
    x, o, gate0, mod, norm_w, conv_w, a_log, dt_bias, gdn_nw, final_nw, *slots = diff
    tgt, bgate0, ada_b, weights = const
    L = x.shape[0]
    tm = _row_tile(L)
    mods = mod.reshape(3, 1, D_MODEL)
    biases = ada_b.reshape(3, 1, D_MODEL)
    h, x1 = make_rowwise(_f_res_lnmod, "res0_ln1", tm, 2, 7)(
        (x, o), (gate0.reshape(1, D_MODEL), bgate0.reshape(1, D_MODEL), norm_w.reshape(1, D_MODEL), mods[1], mods[0], biases[1], biases[0]))
    q0, k0, v0, gz, ba = make_proj("gdn_in")(h, tuple(weights[0:5]), tuple(slots[0:5]))
    cw = jnp.concatenate([conv_w, jnp.zeros((SUBLANES - GDN_CONV, GDN_CONV_CH), F32)], axis=0)
    q = make_conv_act(lambda t: _l2n(_silu(t)) * (GDN_DK ** -0.5), "gdn_conv_q")(q0, cw[:, :GDN_QK])
    k = make_conv_act(lambda t: _l2n(_silu(t)), "gdn_conv_k")(k0, cw[:, GDN_QK:2 * GDN_QK])
    v = make_conv_act(_silu, "gdn_conv_v")(v0, cw[:, 2 * GDN_QK:])
    pad = jnp.zeros((LANES - 2 * GDN_HEADS,), F32)
    alog_row = jnp.concatenate([jnp.zeros((GDN_HEADS,), F32), a_log, pad]).reshape(1, LANES)
    dtb_row = jnp.concatenate([jnp.zeros((GDN_HEADS,), F32), dt_bias, pad]).reshape(1, LANES)
    (bg,) = make_rowwise(_f_betag, "gdn_bg", tm, 1, 2)((ba,), (alog_row, dtb_row))
    nw_row = jnp.tile(gdn_nw, GDN_HEADS).reshape(1, D_INNER)
    on = gdn_scan(*gdn_prep(q, k, v, bg), gz, nw_row)
    y = make_mm("gdn_out")(on, weights[5], slots[5])
    (lt,) = make_rowwise(_f_res_loss, "res1_loss", tm, 3, 3)((x1, y, tgt), (mods[2], biases[2], final_nw.reshape(1, D_MODEL)))
    return jnp.sum(lt)


VEC_NAMES = ("ada_b", "norm_w", "s5_lambda_re", "s5_lambda_im", "s5_log_dt", "s5_d", "gdn_a_log", "gdn_dt_bias", "final_norm_w")
MAT_NAMES = ("s5_b_re", "s5_b_im", "s5_c_re", "s5_c_im")
S5_BIG = ("s5_w_in", "s5_w_glu", "s5_w_out")
GDN_BIG = ("gdn_w_in", "gdn_w_out")
BIG_NAMES = S5_BIG + GDN_BIG
WEIGHT_ORDER = ("ada_w", "ada_b", "norm_w", "s5_w_in", "s5_lambda_re", "s5_lambda_im", "s5_log_dt", "s5_b_re", "s5_b_im",
                "s5_c_re", "s5_c_im", "s5_d", "s5_w_glu", "s5_w_out", "gdn_w_in", "gdn_conv_w", "gdn_a_log", "gdn_dt_bias",
                "gdn_norm_w", "gdn_w_out", "final_norm_w")


def _step(x, c, W, M, V, tgt):
    L = x.shape[1]
    ix, iy, ic = lax.axis_index("x"), lax.axis_index("y"), lax.axis_index("c")
    me = 4 * ix + 2 * iy + ic
    n_ada = W["ada_w"].shape[2]
    n_conv = W["gdn_conv_w"].shape[2]
    n_gnw = W["gdn_norm_w"].shape[1]

    g1 = _allgather_call(_pack([c, W["gdn_conv_w"], W["gdn_norm_w"]], SUBLANES), "gather_small_in", False)
    g1 = g1.reshape(N_DEV, -1)
    c_all = g1[:, :D_MODEL]
    conv_w = g1[:, D_MODEL:D_MODEL + GDN_CONV * n_conv].reshape(N_DEV, GDN_CONV, n_conv).transpose(1, 0, 2).reshape(GDN_CONV, -1)
    gdn_nw = g1[:, D_MODEL + GDN_CONV * n_conv:D_MODEL + GDN_CONV * n_conv + n_gnw].reshape(-1)
    mod_part = _ada_mod_call(c_all, W["ada_w"])
    g2 = _allgather_call(_pack([mod_part], SUBLANES), "gather_mod", False).reshape(N_DEV, -1)
    mod_all = g2[:, :2 * N_DEV * n_ada].reshape(N_DEV, 2, N_DEV, n_ada)
    mod_raw = lax.dynamic_index_in_dim(mod_all, me, axis=2, keepdims=False)
    mod_raw = mod_raw.transpose(1, 0, 2).reshape(2, 3 * D_MODEL)

    shard = lambda n: W[n][0].astype(BF16)
    (w_in5_parts,) = _gather_weights_call([shard("s5_w_in")], "gather_s5_w_in")
    late = _spread_start_call([shard("s5_w_glu"), shard("s5_w_out")], False, "gather_s5_late_start", w_in5_parts)
    g_send, g_recv, g_srcs, g_lands, g_token = _spread_start_call([shard(n) for n in GDN_BIG], False, "gather_gdn_start", late[4])
    w_in5 = _join_cols_call(w_in5_parts, "join_s5_w_in")
    slot = lambda *s: jnp.zeros(s, F32)
    two = 2 * D_MODEL
    diff_mix = (x[0], mod_raw[0, :two] + g_token[0, 0], W["norm_w"][0], W["s5_lambda_re"][0], W["s5_lambda_im"][0], W["s5_log_dt"][0],
                W["s5_b_re"][0], W["s5_b_im"][0], W["s5_c_re"][0], W["s5_c_im"][0], W["s5_d"][0],
                slot(D_MODEL, D_INNER), slot(D_MODEL, D_INNER))

    (xp, y2, z5), vjp_mix = jax.vjp(lambda d: _layer0_mix(d, (W["ada_b"][0, :two], (w_in5[:, :D_INNER], w_in5[:, D_INNER:]))), diff_mix)
    l_srcs, l_lands = _spread_wait_call(late[0], late[1], late[2], late[3], y2, "gather_s5_late_wait")
    w_glu, w_o5 = [lax.dynamic_update_slice(land, src[None], (me, 0, 0)).reshape(-1, src.shape[1]) for land, src in zip(l_lands, l_srcs)]
    diff_out = (y2, z5, slot(D_INNER, D_INNER), slot(D_INNER, D_MODEL))
    o5, vjp_out = jax.vjp(lambda d: _layer0_out(d, (w_glu, w_o5)), diff_out)
    g_srcs, g_lands = _spread_wait_call(g_send, g_recv, g_srcs, g_lands, o5, "gather_gdn_wait")
    gdn_full = [lax.dynamic_update_slice(land, src[None], (me, 0, 0)) for land, src in zip(g_lands, g_srcs)]
    w_ing = _join_cols_call(gdn_full[0], "join_gdn_w_in")
    w_ba = jnp.concatenate([w_ing[:, GDN_CONV_CH + D_INNER:], jnp.zeros((D_MODEL, LANES - 2 * GDN_HEADS), BF16)], axis=1)
    weights1 = (w_ing[:, :GDN_QK], w_ing[:, GDN_QK:2 * GDN_QK], w_ing[:, 2 * GDN_QK:GDN_CONV_CH],
                w_ing[:, GDN_CONV_CH:GDN_CONV_CH + D_INNER], w_ba, gdn_full[1].reshape(D_INNER, D_MODEL))
    slots1 = tuple(jnp.zeros(w.shape, F32) for w in weights1)
    diff1 = (xp, o5, mod_raw[0, two:], mod_raw[1], W["norm_w"][1], conv_w, W["gdn_a_log"][0], W["gdn_dt_bias"][0], gdn_nw,
             W["final_norm_w"], *slots1)
    loss_local, vjp1 = jax.vjp(lambda d: _layer1_loss(d, (tgt[0], W["ada_b"][0, two:], W["ada_b"][1], weights1)), diff1)
    ((dxp, do5, dmod_gate, dmod1, d_norm_w1, d_conv, d_alog, d_dtb, d_gnw, d_fnw, d_wq, d_wk, d_wv, d_wgz, d_wba, d_wog),) = vjp1(
        jnp.ones((), F32))
    loss = lax.psum(loss_local, MESH_AXES)

    rows = lambda d: d.reshape(N_DEV, d.shape[0] // N_DEV, d.shape[1])
    d_ing = _split_cols_call(jnp.concatenate([d_wq, d_wk, d_wv, d_wgz, d_wba[:, :2 * GDN_HEADS]], axis=1), "split_gdn_w_in", BF16)
    s_send, s_recv, s_srcs, s_lands, s_token = _spread_start_call([d_ing, rows(d_wog).astype(BF16)], True, "scatter_gdn_start", dxp)
    ((dy2, dz5, d_wglu, d_wo5),) = vjp_out(do5.at[0, 0].add(s_token[0, 0]))
    t_send, t_recv, t_srcs, t_lands, t_token = _spread_start_call(
        [rows(d_wglu).astype(BF16), rows(d_wo5).astype(BF16)], True, "scatter_s5_late_start", dy2)
    ((dx, dmod_ss, d_norm_w0, d_lre, d_lim, d_logdt, d_bre, d_bim, d_cre, d_cim, d_s5d, d_wu, d_wz),) = vjp_mix(
        (dxp.at[0, 0].add(t_token[0, 0]), dy2, dz5))
    dmod = jnp.stack([jnp.concatenate([dmod_ss, dmod_gate]), dmod1])
    d_norm_w = jnp.stack([d_norm_w0, d_norm_w1])
    vec_parts = [dmod, d_norm_w, d_lre, d_lim, d_logdt, d_s5d, d_alog, d_dtb, d_fnw]
    tail_parts = [d_conv, d_gnw]
    mat_parts = [d_bre, d_bim, d_cre, d_cim]
    n_vec = sum(math.prod(p.shape) for p in vec_parts)
    m_send, m_recv, m_srcs, m_lands, m_token = _spread_start_call(
        [_pack(vec_parts + tail_parts, ADAM_ROWS), _pack(mat_parts, SUBLANES).astype(BF16)], False, "gather_small_grads_start", dx)
    d_in5 = _split_cols_call(jnp.concatenate([d_wu.at[0, 0].add(m_token[0, 0]), d_wz], axis=1), "split_s5_w_in", F32)
    (got,) = _pair_exchange_call([d_in5], "scatter_s5_pair")
    core = jnp.reshape(ic, (1,)).astype(jnp.int32)
    chip = jnp.reshape(2 * ix + iy, (1,)).astype(jnp.int32)
    pair = _pair_sum_call(d_in5, got, core, "pair_sum_s5_w_in")
    (recv,) = _chip_exchange_call([pair], "scatter_s5_chips")
    big = {"s5_w_in": _adam_own_call(pair, chip, recv, W["s5_w_in"][0], M["s5_w_in"][0], V["s5_w_in"][0], "adam_s5_w_in", 128)}
    t_srcs, t_lands = _spread_wait_call(t_send, t_recv, t_srcs, t_lands, dx, "scatter_s5_late_wait")
    s_srcs, s_lands = _spread_wait_call(s_send, s_recv, s_srcs, s_lands, t_lands[0], "scatter_gdn_wait")
    for land, src, n in zip(tuple(t_lands) + tuple(s_lands), tuple(t_srcs) + tuple(s_srcs), ("s5_w_glu", "s5_w_out") + GDN_BIG):
        mine = lax.dynamic_index_in_dim(src, me, 0, keepdims=True)
        parts = lax.dynamic_update_slice(land, mine, (me, 0, 0))
        big[n] = _adam_call(parts, W[n][0], M[n][0], V[n][0], "adam_" + n, rows=_tile(W[n].shape[1], 128))
    big = [[o[None] for o in big[n]] for n in BIG_NAMES]

    m_srcs, m_lands = _spread_wait_call(m_send, m_recv, m_srcs, m_lands, big[0][0], "gather_small_grads_wait")
    sg_vec, sg_mat = [lax.dynamic_update_slice(land, src[None], (me, 0, 0)) for land, src in zip(m_lands, m_srcs)]
    tot_vec = _sum_call(sg_vec, "sum_vec_grads", ADAM_ROWS)
    tot_mat = _sum_call(sg_mat, "sum_mat_grads", ADAM_ROWS)
    g_conv, g_gnw = _unpack(tot_vec.reshape(-1)[n_vec:], [d_conv.shape, d_gnw.shape])
    g_conv_mine = lax.dynamic_slice_in_dim(g_conv, me * n_conv, n_conv, axis=1)
    g_gnw_mine = lax.dynamic_slice_in_dim(g_gnw, me * n_gnw, n_gnw, axis=0)
    vec_names = VEC_NAMES + ("gdn_conv_w", "gdn_norm_w")
    vec_g = _pack([tot_vec.reshape(-1)[:n_vec], g_conv_mine, g_gnw_mine], ADAM_ROWS)
    vec = _adam_call(vec_g[None], _pack([W[n] for n in vec_names], ADAM_ROWS), _pack([M[n] for n in vec_names], ADAM_ROWS),
                     _pack([V[n] for n in vec_names], ADAM_ROWS), "adam_vec")
    vec = [_unpack(b, [W[n].shape for n in vec_names]) for b in vec]
    mats = []
    for name, g_mat in zip(MAT_NAMES, _unpack(tot_mat, [p.shape for p in mat_parts])):
        two_d = (-1, W[name].shape[-1])
        outs = _adam_call(g_mat.reshape(two_d)[None], W[name].reshape(two_d), M[name].reshape(two_d), V[name].reshape(two_d),
                          "adam_" + name, rows=1024)
        mats.append([o.reshape(W[name].shape) for o in outs])

    dmod_all = sg_vec[:, :2 * 3 * D_MODEL // LANES].reshape(N_DEV, 2, N_DEV, n_ada // LANES, LANES)
    dmod_mine = lax.dynamic_index_in_dim(dmod_all, me, axis=2, keepdims=False).transpose(1, 0, 2, 3).reshape(2, N_DEV, n_ada)
    g_ada_w = _ada_grad_call(c_all, dmod_mine)
    ada = _adam_call(g_ada_w.reshape(1, -1, LANES), W["ada_w"].reshape(-1, LANES), M["ada_w"].reshape(-1, LANES),
                     V["ada_w"].reshape(-1, LANES), "adam_ada")
    ada = [a.reshape(W["ada_w"].shape) for a in ada]

    res = {}
    for i, n in enumerate(BIG_NAMES):
        res[n] = big[i]
    for i, n in enumerate(vec_names):
        res[n] = [b[i] for b in vec]
    for i, n in enumerate(MAT_NAMES):
        res[n] = mats[i]
    res["ada_w"] = ada
    outs = [loss, dx[None]]
    for j in range(4):
        outs += [res[n][j] for n in WEIGHT_ORDER]
    return tuple(outs)


def kernel(x, c, ada_w, ada_b, norm_w, s5_w_in, s5_lambda_re, s5_lambda_im, s5_log_dt, s5_b_re, s5_b_im, s5_c_re, s5_c_im, s5_d, s5_w_glu, s5_w_out, gdn_w_in, gdn_conv_w, gdn_a_log, gdn_dt_bias, gdn_norm_w, gdn_w_out, final_norm_w, loss_target, m_ada_w, m_ada_b, m_norm_w, m_s5_w_in, m_s5_lambda_re, m_s5_lambda_im, m_s5_log_dt, m_s5_b_re, m_s5_b_im, m_s5_c_re, m_s5_c_im, m_s5_d, m_s5_w_glu, m_s5_w_out, m_gdn_w_in, m_gdn_conv_w, m_gdn_a_log, m_gdn_dt_bias, m_gdn_norm_w, m_gdn_w_out, m_final_norm_w, v_ada_w, v_ada_b, v_norm_w, v_s5_w_in, v_s5_lambda_re, v_s5_lambda_im, v_s5_log_dt, v_s5_b_re, v_s5_b_im, v_s5_c_re, v_s5_c_im, v_s5_d, v_s5_w_glu, v_s5_w_out, v_gdn_w_in, v_gdn_conv_w, v_gdn_a_log, v_gdn_dt_bias, v_gdn_norm_w, v_gdn_w_out, v_final_norm_w):
    W = dict(ada_w=ada_w, ada_b=ada_b, norm_w=norm_w, s5_w_in=s5_w_in, s5_lambda_re=s5_lambda_re, s5_lambda_im=s5_lambda_im,
             s5_log_dt=s5_log_dt, s5_b_re=s5_b_re, s5_b_im=s5_b_im, s5_c_re=s5_c_re, s5_c_im=s5_c_im, s5_d=s5_d,
             s5_w_glu=s5_w_glu, s5_w_out=s5_w_out, gdn_w_in=gdn_w_in, gdn_conv_w=gdn_conv_w, gdn_a_log=gdn_a_log,
             gdn_dt_bias=gdn_dt_bias, gdn_norm_w=gdn_norm_w, gdn_w_out=gdn_w_out, final_norm_w=final_norm_w)
    M = dict(ada_w=m_ada_w, ada_b=m_ada_b, norm_w=m_norm_w, s5_w_in=m_s5_w_in, s5_lambda_re=m_s5_lambda_re,
             s5_lambda_im=m_s5_lambda_im, s5_log_dt=m_s5_log_dt, s5_b_re=m_s5_b_re, s5_b_im=m_s5_b_im, s5_c_re=m_s5_c_re,
             s5_c_im=m_s5_c_im, s5_d=m_s5_d, s5_w_glu=m_s5_w_glu, s5_w_out=m_s5_w_out, gdn_w_in=m_gdn_w_in,
             gdn_conv_w=m_gdn_conv_w, gdn_a_log=m_gdn_a_log, gdn_dt_bias=m_gdn_dt_bias, gdn_norm_w=m_gdn_norm_w,
             gdn_w_out=m_gdn_w_out, final_norm_w=m_final_norm_w)
    V = dict(ada_w=v_ada_w, ada_b=v_ada_b, norm_w=v_norm_w, s5_w_in=v_s5_w_in, s5_lambda_re=v_s5_lambda_re,
             s5_lambda_im=v_s5_lambda_im, s5_log_dt=v_s5_log_dt, s5_b_re=v_s5_b_re, s5_b_im=v_s5_b_im, s5_c_re=v_s5_c_re,
             s5_c_im=v_s5_c_im, s5_d=v_s5_d, s5_w_glu=v_s5_w_glu, s5_w_out=v_s5_w_out, gdn_w_in=v_gdn_w_in,
             gdn_conv_w=v_gdn_conv_w, gdn_a_log=v_gdn_a_log, gdn_dt_bias=v_gdn_dt_bias, gdn_norm_w=v_gdn_norm_w,
             gdn_w_out=v_gdn_w_out, final_norm_w=v_final_norm_w)
    return _step(x, c, W, M, V, loss_target)
```

```python
import functools
import math

import jax
import jax.numpy as jnp
from jax import lax
from jax.experimental import pallas as pl
from jax.experimental.pallas import tpu as pltpu

F32 = jnp.float32
BF16 = jnp.bfloat16
SDS = jax.ShapeDtypeStruct

D_MODEL = 1024
D_INNER = 2048
NORM_EPS = 1e-6
S5_GROUP = 16
S5_GROUPS = 128
S5_STATE = 64
GDN_HEADS = 8
GDN_DK = 128
GDN_DV = 256
GDN_CONV = 4
GDN_CHUNK = 64
GDN_QK = 1024
GDN_CONV_CH = 4096
GDN_PROJ = 6160
ADAM_LR = 0.001
ADAM_B1 = 0.9
ADAM_B2 = 0.999
ADAM_EPS = 1e-08
ADAM_WD = 0.01
ADAM_STEP = 10

N_DEV = 8
LANES = 128
SUBLANES = 8
VMEM_BIG = 56 << 20
VMEM_MID = 40 << 20
S5_GB = 8
S5_TL = 1024
MESH_AXES = ("x", "y", "c")


def _params(sem, vmem=None):
    return pltpu.CompilerParams(dimension_semantics=sem, vmem_limit_bytes=vmem)


def _bdot(a, b, dims=(((1,), (0,)), ((), ()))):
    return lax.dot_general(a.astype(BF16), b.astype(BF16), dims, preferred_element_type=F32)


def _hdot(a, b, dims=(((1,), (0,)), ((), ()))):
    return lax.dot_general(a, b, dims, preferred_element_type=F32, precision=lax.Precision.HIGHEST)


_BNN = (((2,), (1,)), ((0,), (0,)))
_BNT = (((2,), (2,)), ((0,), (0,)))
_BTN = (((1,), (1,)), ((0,), (0,)))


@jax.custom_vjp
def _unit_lower_inverse(a):
    c = a.shape[-1]
    ri = lax.broadcasted_iota(jnp.int32, a.shape, 1)
    ci = lax.broadcasted_iota(jnp.int32, a.shape, 2)
    n = -a
    t = (ri == ci).astype(F32) + n
    for _ in range(int(math.log2(c)) - 1):
        n = _hdot(n, n, _BNN)
        t = t + _hdot(t, n, _BNN)
    return t


def _unit_lower_inverse_fwd(a):
    t = _unit_lower_inverse(a)
    return t, t


def _unit_lower_inverse_bwd(t, g):
    return (-_hdot(_hdot(t, g, _BTN), t, _BNT),)


_unit_lower_inverse.defvjp(_unit_lower_inverse_fwd, _unit_lower_inverse_bwd)


NN = (((1,), (0,)), ((), ()))
NT = (((1,), (1,)), ((), ()))
TN = (((0,), (0,)), ((), ()))


def _tile(n, pref):
    for t in (pref, 512, 256, 128):
        if t <= n and n % t == 0:
            return t
    return n


def _matmul(a, b, mode, name, add=None):
    if mode == "nn":
        (m, k), (_, n) = a.shape, b.shape
    elif mode == "nt":
        (m, k), (n, _) = a.shape, b.shape
    else:
        (k, m), (_, n) = a.shape, b.shape
    tm, tn, tk = _tile(m, 1024), _tile(n, 512), (k if k <= 2048 else _tile(k, 512))
    if mode == "tn":
        tm, tn, tk = _tile(m, 1024), _tile(n, 1024), _tile(k, 1024)
    nk = k // tk
    dims = {"nn": NN, "nt": NT, "tn": TN}[mode]

    def body(a_ref, b_ref, *rest):
        o_ref, acc_ref = rest[-2], rest[-1]
        part = _bdot(a_ref[...], b_ref[...], dims)
        if nk == 1:
            o_ref[...] = part if add is None else part + rest[0][...]
            return
        kk = pl.program_id(2)

        @pl.when(kk == 0)
        def _():
            acc_ref[...] = part if add is None else part + rest[0][...]

        @pl.when(kk > 0)
        def _():
            acc_ref[...] += part

        @pl.when(kk == nk - 1)
        def _():
            o_ref[...] = acc_ref[...]

    a_spec = pl.BlockSpec((tk, tm), lambda i, j, q: (q, i)) if mode == "tn" else pl.BlockSpec((tm, tk), lambda i, j, q: (i, q))
    b_spec = pl.BlockSpec((tn, tk), lambda i, j, q: (j, q)) if mode == "nt" else pl.BlockSpec((tk, tn), lambda i, j, q: (q, j))
    o_spec = pl.BlockSpec((tm, tn), lambda i, j, q: (i, j))
    return pl.pallas_call(
        body, name=name, grid=(m // tm, n // tn, nk),
        in_specs=[a_spec, b_spec] + ([] if add is None else [o_spec]), out_specs=o_spec,
        out_shape=SDS((m, n), F32), scratch_shapes=[pltpu.VMEM((tm, tn), F32)],
        compiler_params=_params(("parallel", "parallel", "arbitrary"), VMEM_MID),
    )(a, b, *([] if add is None else [add]))


def make_mm(name, pass_input=False):
    def primal(a, w):
        out = _matmul(a, w, "nn", name + "_fwd")
        return (out, a) if pass_input else out

    @jax.custom_vjp
    def mm(a, w, grad_slot):
        return primal(a, w)

    def fwd(a, w, grad_slot):
        return primal(a, w), (a, w)

    def bwd(res, g):
        a, w = res
        g, g_other = g if pass_input else (g, None)
        return _matmul(g, w, "nt", name + "_dx", add=g_other), jnp.zeros_like(w), _matmul(a, g, "tn", name + "_dw")

    mm.defvjp(fwd, bwd)
    return mm


PROJ_ROWS = 256


def _proj_fwd_call(a, ws, name):
    m, k = a.shape
    tm = _tile(m, PROJ_ROWS)
    nw = len(ws)

    def body(*refs):
        ab = refs[0][...].astype(BF16)
        for w_ref, o_ref in zip(refs[1:1 + nw], refs[1 + nw:]):
            o_ref[...] = lax.dot_general(ab, w_ref[...], NN, preferred_element_type=F32)

    return pl.pallas_call(
        body, name=name, grid=(m // tm,),
        in_specs=[pl.BlockSpec((tm, k), lambda i: (i, 0))] + [pl.BlockSpec(w.shape, lambda i: (0, 0)) for w in ws],
        out_specs=[pl.BlockSpec((tm, w.shape[1]), lambda i: (i, 0)) for w in ws],
        out_shape=[SDS((m, w.shape[1]), F32) for w in ws],
        compiler_params=_params(("parallel",), VMEM_BIG),
    )(a, *ws)


def _proj_dx_call(gs, ws, name):
    m = gs[0].shape[0]
    k = ws[0].shape[0]
    tm = _tile(m, PROJ_ROWS)
    nw = len(ws)

    def body(*refs):
        acc = None
        for g_ref, w_ref in zip(refs[:nw], refs[nw:2 * nw]):
            part = _bdot(g_ref[...], w_ref[...], NT)
            acc = part if acc is None else acc + part
        refs[2 * nw][...] = acc

    return pl.pallas_call(
        body, name=name, grid=(m // tm,),
        in_specs=[pl.BlockSpec((tm, g.shape[1]), lambda i: (i, 0)) for g in gs] + [pl.BlockSpec(w.shape, lambda i: (0, 0)) for w in ws],
        out_specs=pl.BlockSpec((tm, k), lambda i: (i, 0)), out_shape=SDS((m, k), F32),
        compiler_params=_params(("parallel",), VMEM_BIG),
    )(*gs, *ws)


def make_proj(name):
    @jax.custom_vjp
    def proj(a, ws, grad_slots):
        return tuple(_proj_fwd_call(a, ws, name + "_fwd"))

    def fwd(a, ws, grad_slots):
        return tuple(_proj_fwd_call(a, ws, name + "_fwd")), (a, ws)

    def bwd(res, gs):
        a, ws = res
        dws = tuple(_matmul(a, g, "tn", "%s_dw%d" % (name, i)) for i, g in enumerate(gs))
        return _proj_dx_call(tuple(gs), ws, name + "_dx"), tuple(jnp.zeros_like(w) for w in ws), dws

    proj.defvjp(fwd, bwd)
    return proj


def make_rowwise(f, name, tm, n_rows, n_params, vmem=VMEM_MID, pass_first=False):
    def specs_of(arrs, blocked):
        if blocked:
            return [pl.BlockSpec((tm, a.shape[1]), lambda i: (i, 0)) for a in arrs]
        return [pl.BlockSpec(a.shape, lambda i: (0, 0)) for a in arrs]

    def out_structs(rows, params):
        blk = [SDS((tm, r.shape[1]), r.dtype) for r in rows] + [SDS(p.shape, p.dtype) for p in params]
        return jax.eval_shape(f, *blk)

    def run_fwd(rows, params):
        L = rows[0].shape[0]
        outs = out_structs(rows, params)

        def body(*refs):
            ins = [r[...] for r in refs[:n_rows + n_params]]
            res = f(*ins)
            for o_ref, val in zip(refs[n_rows + n_params:], res):
                o_ref[...] = val

        return pl.pallas_call(
            body, name=name + "_fwd", grid=(L // tm,),
            in_specs=specs_of(rows, True) + specs_of(params, False),
            out_specs=[pl.BlockSpec((tm, o.shape[1]), lambda i: (i, 0)) for o in outs],
            out_shape=[SDS((L, o.shape[1]), o.dtype) for o in outs],
            compiler_params=_params(("parallel",), vmem),
        )(*rows, *params)

    def run_bwd(rows, params, gs):
        L = rows[0].shape[0]
        n_g = len(gs)

        def body(*refs):
            i = pl.program_id(0)
            ins = [r[...] for r in refs[:n_rows + n_params]]
            cts = tuple(r[...] for r in refs[n_rows + n_params:n_rows + n_params + n_g])
            outs = refs[n_rows + n_params + n_g:]
            _, vjp = jax.vjp(f, *ins)
            grads = vjp(cts[:-1] if pass_first else cts)
            if pass_first:
                grads = (grads[0] + cts[-1],) + tuple(grads[1:])
            for o_ref, val in zip(outs[:n_rows], grads[:n_rows]):
                o_ref[...] = val

            if n_params:
                @pl.when(i == 0)
                def _():
                    for o_ref in outs[n_rows:]:
                        o_ref[...] = jnp.zeros_like(o_ref)
                for o_ref, val in zip(outs[n_rows:], grads[n_rows:]):
                    o_ref[...] += val

        res = pl.pallas_call(
            body, name=name + "_bwd", grid=(L // tm,),
            in_specs=specs_of(rows, True) + specs_of(params, False) + specs_of(gs, True),
            out_specs=specs_of(rows, True) + specs_of(params, False),
            out_shape=[SDS(r.shape, r.dtype) for r in rows] + [SDS(p.shape, p.dtype) for p in params],
            compiler_params=_params(("arbitrary",), vmem),
        )(*rows, *params, *gs)
        return tuple(res[:n_rows]), tuple(res[n_rows:])

    def outputs(rows, params):
        outs = tuple(run_fwd(rows, params))
        return outs + (rows[0],) if pass_first else outs

    @jax.custom_vjp
    def op(rows, params):
        return outputs(rows, params)

    def fwd(rows, params):
        return outputs(rows, params), (rows, params)

    def bwd(res, gs):
        rows, params = res
        return run_bwd(rows, params, tuple(gs))

    op.defvjp(fwd, bwd)
    op.run_fwd, op.run_bwd = run_fwd, run_bwd
    return op


def make_residual(name, tm):
    full = make_rowwise(_f_res, name, tm, 2, 2)
    branch = make_rowwise(lambda y, gate, bgate: ((gate + bgate) * y,), name + "_branch", tm, 1, 2)

    @jax.custom_vjp
    def op(x, y, gate, bgate):
        return full.run_fwd((x, y), (gate, bgate))[0]

    def fwd(x, y, gate, bgate):
        return full.run_fwd((x, y), (gate, bgate))[0], (y, gate, bgate)

    def bwd(res, g):
        y, gate, bgate = res
        (dy,), (dgate, dbgate) = branch.run_bwd((y,), (gate, bgate), (g,))
        return g, dy, dgate, dbgate

    op.defvjp(fwd, bwd)
    return op


def _s5_scan_rows(xr_ref, xi_ref, ar, ai, x0r, x0i, tl, reverse=False):
    n = xr_ref.shape[1]
    T = SUBLANES
    row = lax.broadcasted_iota(jnp.int32, (T, n), 0)
    pr, pi = [ar], [ai]
    for _ in range(T - 1):
        pr, pi = pr + [pr[-1] * ar - pi[-1] * ai], pi + [pr[-1] * ai + pi[-1] * ar]
    levels = []
    for d in (1, 2, 4):
        mask = (row < T - d) if reverse else (row >= d)
        levels.append((T - d if reverse else d, jnp.where(mask, pr[d - 1], 0.0), jnp.where(mask, pi[d - 1], 0.0)))
    cr = jnp.zeros((T, n), F32)
    ci = jnp.zeros((T, n), F32)
    for r in range(T):
        k = (T - r) if reverse else (r + 1)
        cr = jnp.where(row == r, pr[k - 1], cr)
        ci = jnp.where(row == r, pi[k - 1], ci)
    nt = tl // T
    last = 0 if reverse else T - 1

    def step(t, carry):
        sr, si = carry
        base = pl.multiple_of((nt - 1 - t if reverse else t) * T, T)
        br = xr_ref[pl.ds(base, T), :]
        bi = xi_ref[pl.ds(base, T), :]
        for shift, mr, mi in levels:
            qr = pltpu.roll(br, shift, 0)
            qi = pltpu.roll(bi, shift, 0)
            br, bi = br + (mr * qr - mi * qi), bi + (mr * qi + mi * qr)
        xr = br + (cr * sr - ci * si)
        xi = bi + (cr * si + ci * sr)
        xr_ref[pl.ds(base, T), :] = xr
        xi_ref[pl.ds(base, T), :] = xi
        return xr[last:last + 1, :], xi[last:last + 1, :]
    return lax.fori_loop(0, nt, step, (x0r, x0i))


def _s5_fwd_call(u, bre, bim, cre, cim, a, d, tl):
    L, e = u.shape
    nb = e // LANES
    ns = bre.shape[2]
    nc = L // tl

    def body(u_ref, bre_ref, bim_ref, cre_ref, cim_ref, a_ref, d_ref, y_ref, xb_ref, sr_ref, si_ref, xr_ref, xi_ref, carry_ref):
        c = pl.program_id(1)

        @pl.when(c == 0)
        def _():
            carry_ref[...] = jnp.zeros_like(carry_ref)
        xb_ref[0, 0] = carry_ref[...]
        ub = u_ref[...]
        xr_ref[...] = _bdot(ub, bre_ref[0])
        xi_ref[...] = _bdot(ub, bim_ref[0])
        ar = a_ref[0, 0:1, :]
        ai = a_ref[0, 1:2, :]
        xr, xi = _s5_scan_rows(xr_ref, xi_ref, ar, ai, carry_ref[0:1, :], carry_ref[1:2, :], tl)
        carry_ref[0:1, :] = xr
        carry_ref[1:2, :] = xi
        sr = xr_ref[...].astype(BF16)
        si = xi_ref[...].astype(BF16)
        sr_ref[...] = sr
        si_ref[...] = si
        y_ref[...] = _f_s5_act(_bdot(sr, cre_ref[0]) - _bdot(si, cim_ref[0]), ub, d_ref[...])[0]

    return pl.pallas_call(
        body, name="s5_core_fwd", grid=(nb, nc),
        in_specs=[pl.BlockSpec((tl, LANES), lambda j, c: (c, j)),
                  pl.BlockSpec((1, LANES, ns), lambda j, c: (j, 0, 0)), pl.BlockSpec((1, LANES, ns), lambda j, c: (j, 0, 0)),
                  pl.BlockSpec((1, ns, LANES), lambda j, c: (j, 0, 0)), pl.BlockSpec((1, ns, LANES), lambda j, c: (j, 0, 0)),
                  pl.BlockSpec((1, SUBLANES, ns), lambda j, c: (j, 0, 0)), pl.BlockSpec((1, LANES), lambda j, c: (0, j))],
        out_specs=[pl.BlockSpec((tl, LANES), lambda j, c: (c, j)),
                   pl.BlockSpec((1, 1, SUBLANES, ns), lambda j, c: (j, c, 0, 0)),
                   pl.BlockSpec((tl, ns), lambda j, c: (c, j)), pl.BlockSpec((tl, ns), lambda j, c: (c, j))],
        out_shape=[SDS((L, e), F32), SDS((nb, nc, SUBLANES, ns), F32), SDS((L, nb * ns), BF16), SDS((L, nb * ns), BF16)],
        scratch_shapes=[pltpu.VMEM((tl, ns), F32), pltpu.VMEM((tl, ns), F32), pltpu.VMEM((SUBLANES, ns), F32)],
        compiler_params=_params(("arbitrary", "arbitrary"), VMEM_MID),
    )(u, bre, bim, cre, cim, a, d)


def _s5_bwd_call(u, dy2, bre, bim, cre, cim, a, d, xb, sr, si, tl):
    L, e = u.shape
    nb = e // LANES
    ns = bre.shape[2]
    nc = L // tl

    def body(u_ref, dy2_ref, bre_ref, bim_ref, cre_ref, cim_ref, a_ref, d_ref, xb_ref, sr_ref, si_ref,
             du_ref, dbre_ref, dbim_ref, dcre_ref, dcim_ref, da_ref, dd_ref,
             gr_ref, gi_ref, gcarry_ref):
        c = pl.program_id(1)

        @pl.when(c == 0)
        def _():
            gcarry_ref[...] = jnp.zeros_like(gcarry_ref)
            dbre_ref[...] = jnp.zeros_like(dbre_ref)
            dbim_ref[...] = jnp.zeros_like(dbim_ref)
            dcre_ref[...] = jnp.zeros_like(dcre_ref)
            dcim_ref[...] = jnp.zeros_like(dcim_ref)
            da_ref[...] = jnp.zeros_like(da_ref)
            dd_ref[...] = jnp.zeros_like(dd_ref)

        ub = u_ref[...]
        ys = _bdot(sr_ref[...], cre_ref[0]) - _bdot(si_ref[...], cim_ref[0])
        _, act_vjp = jax.vjp(lambda *t: _f_s5_act(*t)[0], ys, ub, d_ref[...])
        dy, du_skip, dd = act_vjp(dy2_ref[...])
        dd_ref[...] += dd
        ar = a_ref[0, 0:1, :]
        ai = a_ref[0, 1:2, :]
        x0r = xb_ref[0, 0, 0:1, :]
        x0i = xb_ref[0, 0, 1:2, :]
        dcre_ref[0] += _bdot(sr_ref[...], dy, TN)
        dcim_ref[0] -= _bdot(si_ref[...], dy, TN)
        gr_ref[...] = _bdot(dy, cre_ref[0], NT)
        gi_ref[...] = -_bdot(dy, cim_ref[0], NT)

        g0r, g0i = _s5_scan_rows(gr_ref, gi_ref, ar, -ai, gcarry_ref[0:1, :], gcarry_ref[1:2, :], tl, reverse=True)
        gcarry_ref[0:1, :] = g0r
        gcarry_ref[1:2, :] = g0i
        row = lax.broadcasted_iota(jnp.int32, (tl, ns), 0)
        gr = gr_ref[...]
        gi = gi_ref[...]
        xpr = jnp.where(row == 0, x0r, pltpu.roll(sr_ref[...].astype(F32), 1, 0))
        xpi = jnp.where(row == 0, x0i, pltpu.roll(si_ref[...].astype(F32), 1, 0))
        da_ref[0, 0:1, :] += jnp.sum(gr * xpr + gi * xpi, axis=0, keepdims=True)
        da_ref[0, 1:2, :] += jnp.sum(gi * xpr - gr * xpi, axis=0, keepdims=True)
        du_ref[...] = (_bdot(gr, bre_ref[0], NT) + _bdot(gi, bim_ref[0], NT)) + du_skip
        dbre_ref[0] += _bdot(ub, gr, TN)
        dbim_ref[0] += _bdot(ub, gi, TN)

    rev = lambda c: nc - 1 - c
    return pl.pallas_call(
        body, name="s5_core_bwd", grid=(nb, nc),
        in_specs=[pl.BlockSpec((tl, LANES), lambda j, c: (rev(c), j)), pl.BlockSpec((tl, LANES), lambda j, c: (rev(c), j)),
                  pl.BlockSpec((1, LANES, ns), lambda j, c: (j, 0, 0)), pl.BlockSpec((1, LANES, ns), lambda j, c: (j, 0, 0)),
                  pl.BlockSpec((1, ns, LANES), lambda j, c: (j, 0, 0)), pl.BlockSpec((1, ns, LANES), lambda j, c: (j, 0, 0)),
                  pl.BlockSpec((1, SUBLANES, ns), lambda j, c: (j, 0, 0)), pl.BlockSpec((1, LANES), lambda j, c: (0, j)),
                  pl.BlockSpec((1, 1, SUBLANES, ns), lambda j, c: (j, rev(c), 0, 0)),
                  pl.BlockSpec((tl, ns), lambda j, c: (rev(c), j)), pl.BlockSpec((tl, ns), lambda j, c: (rev(c), j))],
        out_specs=[pl.BlockSpec((tl, LANES), lambda j, c: (rev(c), j)),
                   pl.BlockSpec((1, LANES, ns), lambda j, c: (j, 0, 0)), pl.BlockSpec((1, LANES, ns), lambda j, c: (j, 0, 0)),
                   pl.BlockSpec((1, ns, LANES), lambda j, c: (j, 0, 0)), pl.BlockSpec((1, ns, LANES), lambda j, c: (j, 0, 0)),
                   pl.BlockSpec((1, SUBLANES, ns), lambda j, c: (j, 0, 0)), pl.BlockSpec((1, LANES), lambda j, c: (0, j))],
        out_shape=[SDS((L, e), F32), SDS(bre.shape, F32), SDS(bim.shape, F32), SDS(cre.shape, F32), SDS(cim.shape, F32),
                   SDS(a.shape, F32), SDS(d.shape, F32)],
        scratch_shapes=[pltpu.VMEM((tl, ns), F32) for _ in range(2)] + [pltpu.VMEM((SUBLANES, ns), F32)],
        compiler_params=_params(("arbitrary", "arbitrary"), VMEM_MID),
    )(u, dy2, bre, bim, cre, cim, a, d, xb, sr, si)


def make_s5_core(tl):
    @jax.custom_vjp
    def s5_core(u, bre, bim, cre, cim, a, d):
        return _s5_fwd_call(u, bre, bim, cre, cim, a, d, tl)[0]

    def fwd(u, bre, bim, cre, cim, a, d):
        y2, xb, sr, si = _s5_fwd_call(u, bre, bim, cre, cim, a, d, tl)
        return y2, (u, bre, bim, cre, cim, a, d, xb, sr, si)

    def bwd(res, dy2):
        u, bre, bim, cre, cim, a, d, xb, sr, si = res
        return tuple(_s5_bwd_call(u, dy2, bre, bim, cre, cim, a, d, xb, sr, si, tl))

    s5_core.defvjp(fwd, bwd)
    return s5_core


def _s5_block_params(lam_re, lam_im, log_dt, b_re, b_im, c_re, c_im):
    dt = jnp.exp(log_dt)[:, None]
    mag = jnp.exp(lam_re * dt)
    ab_re = mag * jnp.cos(lam_im * dt)
    ab_im = mag * jnp.sin(lam_im * dt)
    den = lam_re * lam_re + lam_im * lam_im
    nr = ab_re - 1.0
    ni = ab_im
    q_re = (nr * lam_re + ni * lam_im) / den
    q_im = (ni * lam_re - nr * lam_im) / den
    bb_re = q_re[..., None] * b_re - q_im[..., None] * b_im
    bb_im = q_re[..., None] * b_im + q_im[..., None] * b_re
    nb = S5_GROUPS // S5_GB
    eye = jnp.eye(S5_GB, dtype=F32)

    def bdiag_in(bb):
        t = bb.reshape(nb, S5_GB, S5_STATE, S5_GROUP)
        t = jnp.einsum("jgpm,gh->jgmhp", t, eye)
        return t.reshape(nb, S5_GB * S5_GROUP, S5_GB * S5_STATE)

    def bdiag_out(cc):
        t = cc.reshape(nb, S5_GB, S5_GROUP, S5_STATE)
        t = jnp.einsum("jgmp,gh->jgphm", t, eye)
        return t.reshape(nb, S5_GB * S5_STATE, S5_GB * S5_GROUP)

    a = jnp.stack([ab_re.reshape(nb, S5_GB * S5_STATE), ab_im.reshape(nb, S5_GB * S5_STATE)], axis=1)
    a = jnp.concatenate([a, jnp.zeros((nb, SUBLANES - 2, S5_GB * S5_STATE), F32)], axis=1)
    return bdiag_in(bb_re), bdiag_in(bb_im), bdiag_out(c_re), bdiag_out(c_im), a


def _shift_down(x, s, row):
    if s == 0:
        return x
    return jnp.where(row >= s, pltpu.roll(x, s, 0), 0.0)


def _shift_up(x, s, row, n):
    if s == 0:
        return x
    return jnp.where(row < n - s, pltpu.roll(x, n - s, 0), 0.0)


def _causal_conv(xv, w_ref, row):
    acc = jnp.zeros_like(xv)
    for j in range(GDN_CONV):
        acc += w_ref[j:j + 1, :] * _shift_down(xv, GDN_CONV - 1 - j, row)
    return acc


def _conv_fwd_call(x, w, act, name):
    L, ch = x.shape

    def body(x_ref, w_ref, y_ref):
        xv = x_ref[...]
        row = lax.broadcasted_iota(jnp.int32, xv.shape, 0)
        y_ref[...] = act(_causal_conv(xv, w_ref, row))

    return pl.pallas_call(
        body, name=name + "_fwd", grid=(ch // LANES,),
        in_specs=[pl.BlockSpec((L, LANES), lambda j: (0, j)), pl.BlockSpec((SUBLANES, LANES), lambda j: (0, j))],
        out_specs=pl.BlockSpec((L, LANES), lambda j: (0, j)), out_shape=SDS((L, ch), F32),
        compiler_params=_params(("parallel",), VMEM_MID),
    )(x, w)


def _conv_bwd_call(x, w, dy, act, name):
    L, ch = x.shape

    def body(x_ref, w_ref, dy_ref, dx_ref, dw_ref):
        xv = x_ref[...]
        row = lax.broadcasted_iota(jnp.int32, xv.shape, 0)
        _, act_vjp = jax.vjp(act, _causal_conv(xv, w_ref, row))
        (g,) = act_vjp(dy_ref[...])
        acc = jnp.zeros_like(xv)
        dws = []
        for j in range(GDN_CONV):
            s = GDN_CONV - 1 - j
            acc += w_ref[j:j + 1, :] * _shift_up(g, s, row, L)
            dws.append(jnp.sum(g * _shift_down(xv, s, row), axis=0, keepdims=True))
        dx_ref[...] = acc
        dw_ref[...] = jnp.concatenate(dws + [jnp.zeros((SUBLANES - GDN_CONV, LANES), F32)], axis=0)

    return pl.pallas_call(
        body, name=name + "_bwd", grid=(ch // LANES,),
        in_specs=[pl.BlockSpec((L, LANES), lambda j: (0, j)), pl.BlockSpec((SUBLANES, LANES), lambda j: (0, j)),
                  pl.BlockSpec((L, LANES), lambda j: (0, j))],
        out_specs=[pl.BlockSpec((L, LANES), lambda j: (0, j)), pl.BlockSpec((SUBLANES, LANES), lambda j: (0, j))],
        out_shape=[SDS((L, ch), F32), SDS((SUBLANES, ch), F32)],
        compiler_params=_params(("parallel",), VMEM_MID),
    )(x, w, dy)


def make_conv_act(act, name):
    @jax.custom_vjp
    def op(x, w):
        return _conv_fwd_call(x, w, act, name)

    def fwd(x, w):
        return _conv_fwd_call(x, w, act, name), (x, w)

    def bwd(res, dy):
        x, w = res
        return tuple(_conv_bwd_call(x, w, dy, act, name))

    op.defvjp(fwd, bwd)
    return op


gdn_conv = make_conv_act(lambda c: c, "gdn_conv")


BNN = (((2,), (1,)), ((0,), (0,)))
BNT = (((2,), (2,)), ((0,), (0,)))
BTN = (((1,), (1,)), ((0,), (0,)))
GDN_PREP_BATCH = 8


@jax.custom_vjp
def _known_inverse(a, t):
    return t


def _known_inverse_fwd(a, t):
    return t, t


def _known_inverse_bwd(t, g):
    return -_hdot(_hdot(t, g, _BTN), t, _BNT), jnp.zeros_like(t)


_known_inverse.defvjp(_known_inverse_fwd, _known_inverse_bwd)


def _gdn_prep_math(q, k, v, beta, g, t_saved=None):
    B, C = q.shape[0], q.shape[1]
    ri = lax.broadcasted_iota(jnp.int32, (B, C, C), 1)
    ci = lax.broadcasted_iota(jnp.int32, (B, C, C), 2)
    causal = ri >= ci
    strict = ri > ci
    eye = (ri == ci).astype(F32)
    gb = jnp.broadcast_to(g, (B, C, C))
    g_row = jnp.sum(gb * eye, axis=1, keepdims=True)
    gc_col = jnp.sum(jnp.where(causal, jnp.broadcast_to(g_row, (B, C, C)), 0.0), axis=2, keepdims=True)
    gc_row = jnp.sum(jnp.where(ri <= ci, gb, 0.0), axis=1, keepdims=True)
    decay = jnp.exp(jnp.where(causal, gc_col - gc_row, -jnp.inf))
    kk = _bdot(k, k, BNT)
    a_mat = jnp.where(strict, beta * kk * decay, 0.0)
    t = _unit_lower_inverse(a_mat) if t_saved is None else _known_inverse(a_mat, t_saved)
    e_gc = jnp.exp(gc_col)
    w = _hdot(t, beta * e_gc * k, BNN)
    u = _hdot(t, beta * v, BNN)
    qk = _bdot(q, k, BNT) * decay
    q_dec = q * e_gc
    g_last = gc_col[:, C - 1:C, :]
    k_dec = k * jnp.exp(g_last - gc_col)
    return q_dec, w, u, qk, k_dec, gc_col, t


def _gdn_prep_specs(L):
    C = GDN_CHUNK
    nb = min(GDN_PREP_BATCH, L // C)
    R = nb * C
    ins = [pl.BlockSpec((R, GDN_DK), lambda c, h: (c, h)), pl.BlockSpec((R, GDN_DK), lambda c, h: (c, h)),
           pl.BlockSpec((R, GDN_DV), lambda c, h: (c, h)), pl.BlockSpec((R, LANES), lambda c, h: (c, 0))]
    outs = [pl.BlockSpec((1, R, GDN_DK), lambda c, h: (h, c, 0)), pl.BlockSpec((1, R, GDN_DK), lambda c, h: (h, c, 0)),
            pl.BlockSpec((1, R, GDN_DV), lambda c, h: (h, c, 0)), pl.BlockSpec((1, R, C), lambda c, h: (h, c, 0)),
            pl.BlockSpec((1, R, GDN_DK), lambda c, h: (h, c, 0)), pl.BlockSpec((1, R, 1), lambda c, h: (h, c, 0))]
    t_spec = pl.BlockSpec((1, R, C), lambda c, h: (h, c, 0))
    shapes = [SDS((GDN_HEADS, L, GDN_DK), F32), SDS((GDN_HEADS, L, GDN_DK), F32), SDS((GDN_HEADS, L, GDN_DV), F32),
              SDS((GDN_HEADS, L, C), F32), SDS((GDN_HEADS, L, GDN_DK), F32), SDS((GDN_HEADS, L, 1), F32)]
    return ins, outs, t_spec, shapes, nb


def _chunks(x, nb):
    return x.reshape(nb, x.shape[0] // nb, x.shape[1])


def _head_columns(bg, h):
    lane = lax.broadcasted_iota(jnp.int32, bg.shape, 1)
    beta = jnp.sum(jnp.where(lane == h, bg, 0.0), axis=1, keepdims=True)
    g = jnp.sum(jnp.where(lane == h + GDN_HEADS, bg, 0.0), axis=1, keepdims=True)
    return beta, g


def _gdn_prep_fwd_call(q, k, v, bg):
    L = q.shape[0]
    ins, outs, t_spec, shapes, nb = _gdn_prep_specs(L)

    def body(q_ref, k_ref, v_ref, bg_ref, *o_refs):
        beta, g = _head_columns(bg_ref[...], pl.program_id(1))
        res = _gdn_prep_math(_chunks(q_ref[...], nb), _chunks(k_ref[...], nb), _chunks(v_ref[...], nb),
                             _chunks(beta, nb), _chunks(g, nb))
        for o_ref, val in zip(o_refs, res):
            o_ref[0] = val.reshape(val.shape[0] * val.shape[1], val.shape[2])

    return pl.pallas_call(
        body, name="gdn_prep_fwd", grid=(L // (nb * GDN_CHUNK), GDN_HEADS), in_specs=ins, out_specs=outs + [t_spec],
        out_shape=shapes + [SDS((GDN_HEADS, L, GDN_CHUNK), F32)],
        compiler_params=_params(("parallel", "parallel"), VMEM_MID),
    )(q, k, v, bg)


def _gdn_prep_bwd_call(q, k, v, bg, t, cts):
    L = q.shape[0]
    ins, outs, t_spec, _, nb = _gdn_prep_specs(L)

    def body(q_ref, k_ref, v_ref, bg_ref, t_ref, c0, c1, c2, c3, c4, c5, dq_ref, dk_ref, dv_ref, dbg_ref):
        h = pl.program_id(1)
        beta, g = _head_columns(bg_ref[...], h)
        t_saved = _chunks(t_ref[0], nb)
        _, vjp = jax.vjp(lambda *a: _gdn_prep_math(*a, t_saved=t_saved)[:6], _chunks(q_ref[...], nb), _chunks(k_ref[...], nb),
                         _chunks(v_ref[...], nb), _chunks(beta, nb), _chunks(g, nb))
        dq, dk, dv, db, dg = vjp(tuple(_chunks(c[0], nb) for c in (c0, c1, c2, c3, c4, c5)))
        flat = lambda a: a.reshape(a.shape[0] * a.shape[1], a.shape[2])
        dq_ref[...] = flat(dq)
        dk_ref[...] = flat(dk)
        dv_ref[...] = flat(dv)

        @pl.when(h == 0)
        def _():
            dbg_ref[...] = jnp.zeros_like(dbg_ref)
        lane = lax.broadcasted_iota(jnp.int32, dbg_ref.shape, 1)
        dbg_ref[...] += jnp.where(lane == h, flat(db), 0.0) + jnp.where(lane == h + GDN_HEADS, flat(dg), 0.0)

    return pl.pallas_call(
        body, name="gdn_prep_bwd", grid=(L // (nb * GDN_CHUNK), GDN_HEADS), in_specs=ins + [t_spec] + outs, out_specs=ins,
        out_shape=[SDS(q.shape, F32), SDS(k.shape, F32), SDS(v.shape, F32), SDS(bg.shape, F32)],
        compiler_params=_params(("parallel", "arbitrary"), VMEM_MID),
    )(q, k, v, bg, t, *cts)


@jax.custom_vjp
def gdn_prep(q, k, v, bg):
    return tuple(_gdn_prep_fwd_call(q, k, v, bg)[:6])


def _gdn_prep_f(q, k, v, bg):
    res = _gdn_prep_fwd_call(q, k, v, bg)
    return tuple(res[:6]), (q, k, v, bg, res[6])


def _gdn_prep_b(res, cts):
    return tuple(_gdn_prep_bwd_call(*res, tuple(cts)))


gdn_prep.defvjp(_gdn_prep_f, _gdn_prep_b)


def _gdn_step_math(q_dec, w, u, qk, k_dec, gc, z, nw, state):
    H, C = q_dec.shape[0], q_dec.shape[1]
    v_new = u - _bdot(w, state, BNN)
    o = _bdot(q_dec, state, BNN) + _bdot(qk, v_new, BNN)
    gl = gc[:, C - 1:C, :]
    new_state = jnp.exp(gl) * state + _bdot(k_dec, v_new, BTN)
    return _f_gdn_post(jnp.concatenate([o[h] for h in range(H)], axis=1), z, nw)[0], new_state


def _gdn_scan_specs(L, rev):
    C, H = GDN_CHUNK, GDN_HEADS
    nc = L // C
    cc = (lambda c: nc - 1 - c) if rev else (lambda c: c)
    ins = [pl.BlockSpec((H, C, GDN_DK), lambda c: (0, cc(c), 0)), pl.BlockSpec((H, C, GDN_DK), lambda c: (0, cc(c), 0)),
           pl.BlockSpec((H, C, GDN_DV), lambda c: (0, cc(c), 0)), pl.BlockSpec((H, C, C), lambda c: (0, cc(c), 0)),
           pl.BlockSpec((H, C, GDN_DK), lambda c: (0, cc(c), 0)), pl.BlockSpec((H, C, 1), lambda c: (0, cc(c), 0))]
    o_spec = pl.BlockSpec((C, H * GDN_DV), lambda c: (cc(c), 0))
    nw_spec = pl.BlockSpec((1, H * GDN_DV), lambda c: (0, 0))
    s_spec = pl.BlockSpec((1, H, GDN_DK, GDN_DV), lambda c: (cc(c), 0, 0, 0))
    return ins + [o_spec, nw_spec], o_spec, s_spec, nc


def _gdn_scan_fwd_call(q_dec, w, u, qk, k_dec, gc, z, nw):
    L = q_dec.shape[1]
    ins, o_spec, s_spec, nc = _gdn_scan_specs(L, False)

    def body(qd_ref, w_ref, u_ref, qk_ref, kd_ref, gc_ref, z_ref, nw_ref, o_ref, sin_ref, s_ref):
        c = pl.program_id(0)

        @pl.when(c == 0)
        def _():
            s_ref[...] = jnp.zeros_like(s_ref)
        st = s_ref[...]
        sin_ref[0] = st
        o, ns = _gdn_step_math(qd_ref[...], w_ref[...], u_ref[...], qk_ref[...], kd_ref[...], gc_ref[...], z_ref[...], nw_ref[...], st)
        o_ref[...] = o
        s_ref[...] = ns

    return pl.pallas_call(
        body, name="gdn_scan_fwd", grid=(nc,), in_specs=ins, out_specs=[o_spec, s_spec],
        out_shape=[SDS((L, GDN_HEADS * GDN_DV), F32), SDS((nc, GDN_HEADS, GDN_DK, GDN_DV), F32)],
        scratch_shapes=[pltpu.VMEM((GDN_HEADS, GDN_DK, GDN_DV), F32)],
        compiler_params=_params(("arbitrary",), VMEM_MID),
    )(q_dec, w, u, qk, k_dec, gc, z, nw)


def _gdn_scan_bwd_call(q_dec, w, u, qk, k_dec, gc, z, nw, s_in, do):
    L = q_dec.shape[1]
    ins, o_spec, s_spec, nc = _gdn_scan_specs(L, True)

    def body(qd_ref, w_ref, u_ref, qk_ref, kd_ref, gc_ref, z_ref, nw_ref, sin_ref, do_ref,
             dqd_ref, dw_ref, du_ref, dqk_ref, dkd_ref, dgc_ref, dz_ref, dnw_ref, ds_ref):
        c = pl.program_id(0)

        @pl.when(c == 0)
        def _():
            ds_ref[...] = jnp.zeros_like(ds_ref)
            dnw_ref[...] = jnp.zeros_like(dnw_ref)
        _, vjp = jax.vjp(_gdn_step_math, qd_ref[...], w_ref[...], u_ref[...], qk_ref[...], kd_ref[...], gc_ref[...],
                         z_ref[...], nw_ref[...], sin_ref[0])
        dqd, dw, du, dqk, dkd, dgc, dz, dnw, dst = vjp((do_ref[...], ds_ref[...]))
        dqd_ref[...] = dqd
        dw_ref[...] = dw
        du_ref[...] = du
        dqk_ref[...] = dqk
        dkd_ref[...] = dkd
        dgc_ref[...] = dgc
        dz_ref[...] = dz
        dnw_ref[...] += dnw
        ds_ref[...] = dst

    return pl.pallas_call(
        body, name="gdn_scan_bwd", grid=(nc,), in_specs=ins + [s_spec, o_spec], out_specs=ins,
        out_shape=[SDS(t.shape, F32) for t in (q_dec, w, u, qk, k_dec, gc, z, nw)],
        scratch_shapes=[pltpu.VMEM((GDN_HEADS, GDN_DK, GDN_DV), F32)],
        compiler_params=_params(("arbitrary",), VMEM_MID),
    )(q_dec, w, u, qk, k_dec, gc, z, nw, s_in, do)


@jax.custom_vjp
def gdn_scan(q_dec, w, u, qk, k_dec, gc, z, nw):
    return _gdn_scan_fwd_call(q_dec, w, u, qk, k_dec, gc, z, nw)[0]


def _gdn_scan_f(*args):
    o, s_in = _gdn_scan_fwd_call(*args)
    return o, (*args, s_in)


def _gdn_scan_b(res, do):
    return tuple(_gdn_scan_bwd_call(*res, do))


gdn_scan.defvjp(_gdn_scan_f, _gdn_scan_b)


def _silu(x):
    return x * jax.nn.sigmoid(x)


def _gelu_tanh(x):
    return 0.5 * x * (1.0 + jnp.tanh(math.sqrt(2.0 / math.pi) * (x + 0.044715 * (x * x * x))))


def _f_lnmod(x, nw, sc, sh, bsc, bsh):
    xn = x * lax.rsqrt(jnp.mean(x * x, axis=-1, keepdims=True) + NORM_EPS) * nw
    return (xn * (1.0 + (sc + bsc)) + (sh + bsh),)


def _f_s5_act(ys, u, d):
    return (_gelu_tanh(ys + d * u),)


def _f_s5_gate(y2, t, z):
    return (y2 * jax.nn.sigmoid(t) * _silu(z),)


def _f_res(x, y, gate, bgate):
    return (x + (gate + bgate) * y,)


def _heads(x, width, fn):
    return jnp.concatenate([fn(x[:, i * width:(i + 1) * width]) for i in range(x.shape[1] // width)], axis=1)


def _l2n(x):
    return x * lax.rsqrt(jnp.sum(x * x, axis=-1, keepdims=True) + NORM_EPS)


def _f_qnorm(x):
    return (_heads(_silu(x), GDN_DK, _l2n) * (GDN_DK ** -0.5),)


def _f_knorm(x):
    return (_heads(_silu(x), GDN_DK, _l2n),)


def _f_vact(x):
    return (_silu(x),)


def _f_betag(ba, alog, dtb):
    col = lax.broadcasted_iota(jnp.int32, ba.shape, 1)
    t = ba + dtb
    softplus = jnp.maximum(t, 0.0) + jnp.log1p(jnp.exp(-jnp.abs(t)))
    g = -jnp.exp(alog) * softplus
    return (jnp.where(col < GDN_HEADS, jax.nn.sigmoid(ba), jnp.where(col < 2 * GDN_HEADS, g, 0.0)),)


def _f_gdn_post(o, z, nw):
    on = _heads(o, GDN_DV, lambda t: t * lax.rsqrt(jnp.mean(t * t, axis=-1, keepdims=True) + NORM_EPS))
    return (on * nw * _silu(z),)


def _f_loss(x, tgt, fw):
    y = x * lax.rsqrt(jnp.mean(x * x, axis=-1, keepdims=True) + NORM_EPS) * fw
    err = y - tgt
    return (0.5 * jnp.mean(err * err, axis=-1, keepdims=True),)


def _ada_mod_call(c_all, ada_w):
    n = ada_w.shape[2]

    def body(c_ref, w_ref, o_ref):
        ca = _silu(c_ref[...])
        for l in range(ada_w.shape[0]):
            o_ref[l] = _bdot(ca, w_ref[l])

    return pl.pallas_call(body, name="ada_mod", out_shape=SDS((ada_w.shape[0], N_DEV, n), F32),
                          compiler_params=_params(None, VMEM_MID))(c_all, ada_w)


def _ada_grad_call(c_all, dmod):
    nl, _, n = dmod.shape

    def body(c_ref, d_ref, o_ref):
        ca = _silu(c_ref[...])
        for l in range(nl):
            o_ref[l] = _hdot(ca, d_ref[l], TN)

    return pl.pallas_call(body, name="ada_grad", out_shape=SDS((nl, c_all.shape[1], n), F32),
                          compiler_params=_params(None, VMEM_MID))(c_all, dmod)


ADAM_ROWS = 512


def _adamw(g, w, m, v):
    m2 = ADAM_B1 * m + (1.0 - ADAM_B1) * g
    v2 = ADAM_B2 * v + (1.0 - ADAM_B2) * (g * g)
    m_hat = m2 / (1.0 - ADAM_B1 ** ADAM_STEP)
    v_hat = v2 / (1.0 - ADAM_B2 ** ADAM_STEP)
    return g, -ADAM_LR * (m_hat / (jnp.sqrt(v_hat) + ADAM_EPS) + ADAM_WD * w), m2, v2


def _adam_call(gs, w, m, v, name, rows=None):
    n, r, cols = gs.shape
    rows = rows or ADAM_ROWS

    def body(g_ref, w_ref, m_ref, v_ref, go_ref, d_ref, mo_ref, vo_ref):
        g = g_ref[0].astype(F32)
        for s in range(1, n):
            g = g + g_ref[s].astype(F32)
        for o_ref, val in zip((go_ref, d_ref, mo_ref, vo_ref), _adamw(g, w_ref[...], m_ref[...], v_ref[...])):
            o_ref[...] = val

    blk = pl.BlockSpec((rows, cols), lambda i: (i, 0))
    return pl.pallas_call(
        body, name=name, grid=(r // rows,),
        in_specs=[pl.BlockSpec((n, rows, cols), lambda i: (0, i, 0)), blk, blk, blk],
        out_specs=[blk, blk, blk, blk], out_shape=[SDS((r, cols), F32)] * 4,
        compiler_params=_params(("parallel",), VMEM_MID),
    )(gs, w, m, v)


def _sum_call(gs, name, rows):
    n, r, _ = gs.shape

    def body(g_ref, o_ref):
        g = g_ref[0].astype(F32)
        for s in range(1, n):
            g = g + g_ref[s].astype(F32)
        o_ref[...] = g

    return pl.pallas_call(
        body, name=name, grid=(r // rows,),
        in_specs=[pl.BlockSpec((n, rows, LANES), lambda i: (0, i, 0))],
        out_specs=pl.BlockSpec((rows, LANES), lambda i: (i, 0)), out_shape=SDS((r, LANES), F32),
        compiler_params=_params(("parallel",), VMEM_MID),
    )(gs)


def _allgather_call(x_shard, name, in_hbm):
    m_per, n = x_shard.shape

    def body(x_ref, out_ref, send_sems, recv_sems, local_sem):
        x, y, c = lax.axis_index("x"), lax.axis_index("y"), lax.axis_index("c")
        me, sibling = (x, y, c), (x, y, 1 - c)
        chips = [(1 - x, y), (x, 1 - y), (1 - x, 1 - y)]

        def rows(px, py, pc):
            return out_ref.at[pl.ds((4 * px + 2 * py + pc) * m_per, m_per), :]

        def copy(k, block, to, src=None):
            return pltpu.make_async_remote_copy(
                src_ref=rows(*block) if src is None else src, dst_ref=rows(*block),
                send_sem=send_sems.at[k], recv_sem=recv_sems.at[k], device_id=to, device_id_type=pl.DeviceIdType.MESH)

        mine = pltpu.make_async_copy(x_ref, rows(*me), local_sem)
        mine.start()
        first = [copy(0, me, sibling, src=x_ref)]
        first += [copy(1 + j, me, (*chip, c), src=x_ref) for j, chip in enumerate(chips)]
        for cp in first:
            cp.start()
        passed = [copy(4 + j, (*chip, c), sibling) for j, chip in enumerate(chips)]
        for j, chip in enumerate(chips):
            copy(1 + j, (*chip, c), me).wait_recv()
            passed[j].start()
        copy(0, sibling, me).wait_recv()
        for j, chip in enumerate(chips):
            copy(4 + j, (*chip, 1 - c), me).wait_recv()
        for cp in first + passed:
            cp.wait_send()
        mine.wait()

    space = pl.ANY if in_hbm else pltpu.VMEM
    return pl.pallas_call(
        body, name=name, out_shape=SDS((N_DEV * m_per, n), x_shard.dtype),
        in_specs=[pl.BlockSpec(memory_space=space)], out_specs=pl.BlockSpec(memory_space=space),
        scratch_shapes=[pltpu.SemaphoreType.DMA((7,)), pltpu.SemaphoreType.DMA((7,)), pltpu.SemaphoreType.DMA],
        compiler_params=_params(None, None if in_hbm else VMEM_BIG),
    )(x_shard)


def _gather_weights_call(shards, name):
    nw = len(shards)

    def body(*refs):
        x_refs, out_refs = refs[:nw], refs[nw:2 * nw]
        send_sems, recv_sems, local_sems = refs[2 * nw:]
        x, y, c = lax.axis_index("x"), lax.axis_index("y"), lax.axis_index("c")
        me, sibling = (x, y, c), (x, y, 1 - c)
        chips = [(1 - x, y), (x, 1 - y), (1 - x, 1 - y)]

        def slot(w, px, py, pc):
            return out_refs[w].at[4 * px + 2 * py + pc]

        def copy(w, k, block, to, src=None):
            dst = slot(w, *block)
            return pltpu.make_async_remote_copy(
                src_ref=dst if src is None else src, dst_ref=dst, send_sem=send_sems.at[7 * w + k],
                recv_sem=recv_sems.at[7 * w + k], device_id=to, device_id_type=pl.DeviceIdType.MESH)

        mines = [pltpu.make_async_copy(x_refs[w], slot(w, *me), local_sems.at[w]) for w in range(nw)]
        for cp in mines:
            cp.start()
        first = [copy(w, 0, me, sibling, src=x_refs[w]) for w in range(nw)]
        first += [copy(w, 1 + j, me, (*chip, c), src=x_refs[w]) for w in range(nw) for j, chip in enumerate(chips)]
        for cp in first:
            cp.start()
        passed = []
        for w in range(nw):
            for j, chip in enumerate(chips):
                copy(w, 1 + j, (*chip, c), me).wait_recv()
                fwd = copy(w, 4 + j, (*chip, c), sibling)
                fwd.start()
                passed.append(fwd)
        for w in range(nw):
            copy(w, 0, sibling, me).wait_recv()
            for j, chip in enumerate(chips):
                copy(w, 4 + j, (*chip, 1 - c), me).wait_recv()
        for cp in first + passed:
            cp.wait_send()
        for cp in mines:
            cp.wait()

    hbm = pl.BlockSpec(memory_space=pl.ANY)
    return pl.pallas_call(
        body, name=name, out_shape=[SDS((N_DEV,) + s.shape, s.dtype) for s in shards],
        in_specs=[hbm] * nw, out_specs=[hbm] * nw,
        scratch_shapes=[pltpu.SemaphoreType.DMA((7 * nw,)), pltpu.SemaphoreType.DMA((7 * nw,)), pltpu.SemaphoreType.DMA((nw,))],
    )(*shards)


def _pair_exchange_call(grads, name):
    nw = len(grads)

    def body(*refs):
        g_refs, got_refs = refs[:nw], refs[nw:2 * nw]
        send_sems, recv_sems = refs[2 * nw:]
        x, y, c = lax.axis_index("x"), lax.axis_index("y"), lax.axis_index("c")
        copies = []
        for w in range(nw):
            for j in range(4):
                give = pltpu.make_async_remote_copy(
                    src_ref=g_refs[w].at[2 * j + 1 - c], dst_ref=got_refs[w].at[j], send_sem=send_sems.at[4 * w + j],
                    recv_sem=recv_sems.at[4 * w + j], device_id=(x, y, 1 - c), device_id_type=pl.DeviceIdType.MESH)
                give.start()
                copies.append(give)
        for cp in copies:
            cp.wait()

    hbm = pl.BlockSpec(memory_space=pl.ANY)
    return pl.pallas_call(
        body, name=name, out_shape=[SDS((4,) + g.shape[1:], g.dtype) for g in grads], in_specs=[hbm] * nw, out_specs=[hbm] * nw,
        scratch_shapes=[pltpu.SemaphoreType.DMA((4 * nw,)), pltpu.SemaphoreType.DMA((4 * nw,))],
    )(*grads)


def _chip_exchange_call(parts, name):
    nw = len(parts)

    def body(*refs):
        p_refs, out_refs = refs[:nw], refs[nw:2 * nw]
        send_sems, recv_sems = refs[2 * nw:]
        x, y, c = lax.axis_index("x"), lax.axis_index("y"), lax.axis_index("c")
        chips = [(1 - x, y), (x, 1 - y), (1 - x, 1 - y)]
        copies = []
        for w in range(nw):
            for j, (px, py) in enumerate(chips):
                give = pltpu.make_async_remote_copy(
                    src_ref=p_refs[w].at[2 * px + py], dst_ref=out_refs[w].at[j], send_sem=send_sems.at[3 * w + j],
                    recv_sem=recv_sems.at[3 * w + j], device_id=(px, py, c), device_id_type=pl.DeviceIdType.MESH)
                give.start()
                copies.append(give)
        for cp in copies:
            cp.wait()

    hbm = pl.BlockSpec(memory_space=pl.ANY)
    return pl.pallas_call(
        body, name=name, out_shape=[SDS((3,) + p.shape[1:], p.dtype) for p in parts], in_specs=[hbm] * nw, out_specs=[hbm] * nw,
        scratch_shapes=[pltpu.SemaphoreType.DMA((3 * nw,)), pltpu.SemaphoreType.DMA((3 * nw,))],
    )(*parts)


_HBM = pl.BlockSpec(memory_space=pltpu.HBM)
_SEM = pl.BlockSpec(memory_space=pltpu.SEMAPHORE)
_DATAFLOW = pltpu.SideEffectType.DATAFLOW_SIDE_EFFECTING


def _spread_start_call(srcs, per_peer, name, after):
    nw = len(srcs)
    lands = [lax.empty((N_DEV,) + (s.shape[1:] if per_peer else s.shape), s.dtype) for s in srcs]

    def body(*refs):
        src_refs, land_refs = refs[:nw], refs[nw:2 * nw]
        send_sems, recv_sems, token = refs[2 * nw + 1], refs[2 * nw + 2], refs[-1]
        x, y, c = lax.axis_index("x"), lax.axis_index("y"), lax.axis_index("c")
        me = 4 * x + 2 * y + c
        for w in range(nw):
            for k in range(1, N_DEV):
                px = 1 - x if k & 4 else x
                py = 1 - y if k & 2 else y
                pc = 1 - c if k & 1 else c
                src = src_refs[w].at[4 * px + 2 * py + pc] if per_peer else src_refs[w]
                pltpu.make_async_remote_copy(
                    src_ref=src, dst_ref=land_refs[w].at[me], send_sem=send_sems.at[w], recv_sem=recv_sems.at[w],
                    device_id=(px, py, pc), device_id_type=pl.DeviceIdType.MESH).start()
        token[...] = jnp.zeros_like(token)

    hbm = lambda a: pltpu.with_memory_space_constraint(a, pltpu.HBM)
    res = pl.pallas_call(
        body, name=name,
        out_shape=(pltpu.SemaphoreType.DMA((nw,)), pltpu.SemaphoreType.DMA((nw,)))
        + tuple(pltpu.HBM(s.shape, s.dtype) for s in srcs) + tuple(pltpu.HBM(l.shape, l.dtype) for l in lands)
        + (SDS((SUBLANES, LANES), F32),),
        in_specs=[_HBM] * (2 * nw) + [pl.BlockSpec(memory_space=pl.ANY)],
        out_specs=(_SEM, _SEM) + (_HBM,) * (2 * nw) + (pl.BlockSpec(memory_space=pltpu.VMEM),),
        input_output_aliases={i: i + 2 for i in range(2 * nw)},
        compiler_params=pltpu.CompilerParams(has_side_effects=_DATAFLOW),
    )(*[hbm(s) for s in srcs], *[hbm(l) for l in lands], after)
    return res[0], res[1], res[2:2 + nw], res[2 + nw:2 + 2 * nw], res[-1]


def _spread_wait_call(send_sems, recv_sems, srcs, lands, after, name):
    nw = len(lands)

    def body(*refs):
        land_refs = refs[nw:2 * nw]
        s_sems, r_sems = refs[2 * nw], refs[2 * nw + 1]
        x, y, c = lax.axis_index("x"), lax.axis_index("y"), lax.axis_index("c")
        for w in range(nw):
            seven = land_refs[w].at[pl.ds(0, N_DEV - 1)]
            all_seven = pltpu.make_async_remote_copy(
                src_ref=seven, dst_ref=seven, send_sem=s_sems.at[w], recv_sem=r_sems.at[w],
                device_id=(x, y, c), device_id_type=pl.DeviceIdType.MESH)
            all_seven.wait_send()
            all_seven.wait_recv()

    res = pl.pallas_call(
        body, name=name,
        out_shape=tuple(pltpu.HBM(s.shape, s.dtype) for s in srcs) + tuple(pltpu.HBM(l.shape, l.dtype) for l in lands),
        in_specs=[_HBM] * (2 * nw) + [_SEM, _SEM, pl.BlockSpec(memory_space=pl.ANY)], out_specs=(_HBM,) * (2 * nw),
        input_output_aliases={i: i for i in range(2 * nw)},
        compiler_params=pltpu.CompilerParams(has_side_effects=_DATAFLOW),
    )(*srcs, *lands, send_sems, recv_sems, after)
    return res[:nw], res[nw:]


def _pair_sum_call(g, got, core, name):
    _, k, n = got.shape
    tr = _tile(k, 256)

    def body(c_ref, g_ref, got_ref, o_ref):
        o_ref[...] = (g_ref[...] + got_ref[...]).astype(o_ref.dtype)

    spec = pltpu.PrefetchScalarGridSpec(
        num_scalar_prefetch=1, grid=(4, k // tr),
        in_specs=[pl.BlockSpec((1, tr, n), lambda j, i, c: (2 * j + c[0], i, 0)), pl.BlockSpec((1, tr, n), lambda j, i, c: (j, i, 0))],
        out_specs=pl.BlockSpec((1, tr, n), lambda j, i, c: (j, i, 0)))
    return pl.pallas_call(body, name=name, grid_spec=spec, out_shape=SDS(got.shape, BF16),
                          compiler_params=_params(("parallel", "parallel"), VMEM_MID))(core, g, got)


def _adam_own_call(pair, chip, recv, w, m, v, name, rows):
    _, r, cols = recv.shape

    def body(chip_ref, p_ref, g_ref, w_ref, m_ref, v_ref, go_ref, d_ref, mo_ref, vo_ref):
        g = ((p_ref[0].astype(F32) + g_ref[0].astype(F32)) + g_ref[1].astype(F32)) + g_ref[2].astype(F32)
        for o_ref, val in zip((go_ref, d_ref, mo_ref, vo_ref), _adamw(g, w_ref[...], m_ref[...], v_ref[...])):
            o_ref[...] = val

    blk = pl.BlockSpec((rows, cols), lambda i, s: (i, 0))
    spec = pltpu.PrefetchScalarGridSpec(
        num_scalar_prefetch=1, grid=(r // rows,),
        in_specs=[pl.BlockSpec((1, rows, cols), lambda i, s: (s[0], i, 0)), pl.BlockSpec((3, rows, cols), lambda i, s: (0, i, 0)),
                  blk, blk, blk],
        out_specs=[blk, blk, blk, blk])
    return pl.pallas_call(body, name=name, grid_spec=spec, out_shape=[SDS((r, cols), F32)] * 4,
                          compiler_params=_params(("parallel",), VMEM_MID))(chip, pair, recv, w, m, v)


def _join_cols_call(w8, name):
    _, k, n = w8.shape
    tk = _tile(k, 256)

    def body(w_ref, o_ref):
        for s in range(N_DEV):
            o_ref[:, n * s:n * (s + 1)] = w_ref[s]

    return pl.pallas_call(body, name=name, grid=(k // tk,), in_specs=[pl.BlockSpec((N_DEV, tk, n), lambda i: (0, i, 0))],
                          out_specs=pl.BlockSpec((tk, N_DEV * n), lambda i: (i, 0)), out_shape=SDS((k, N_DEV * n), w8.dtype),
                          compiler_params=_params(("parallel",), VMEM_MID))(w8)


def _split_cols_call(g, name, dtype):
    k, n8 = g.shape
    n = n8 // N_DEV
    tk = _tile(k, 256)

    def body(g_ref, o_ref):
        for s in range(N_DEV):
            o_ref[s] = g_ref[:, n * s:n * (s + 1)].astype(dtype)

    return pl.pallas_call(body, name=name, grid=(k // tk,), in_specs=[pl.BlockSpec((tk, n8), lambda i: (i, 0))],
                          out_specs=pl.BlockSpec((N_DEV, tk, n), lambda i: (0, i, 0)), out_shape=SDS((N_DEV, k, n), dtype),
                          compiler_params=_params(("parallel",), VMEM_MID))(g)


def _pack(parts, rows_multiple):
    flat = jnp.concatenate([p.reshape(-1) for p in parts])
    unit = rows_multiple * LANES
    padded = -(-flat.shape[0] // unit) * unit
    flat = jnp.concatenate([flat, jnp.zeros((padded - flat.shape[0],), F32)])
    return flat.reshape(-1, LANES)


def _groups_last(a):
    x, y = a.shape[-2:]
    return jnp.transpose(a.reshape(S5_GROUPS, x, y), (1, 2, 0)).reshape(x * y, S5_GROUPS)


def _groups_first(a, shape):
    x, y = shape[-2:]
    return jnp.transpose(a.reshape(x, y, S5_GROUPS), (2, 0, 1)).reshape(shape)


def _unpack(buf, shapes):
    flat = buf.reshape(-1)
    out, off = [], 0
    for s in shapes:
        n = math.prod(s)
        out.append(flat[off:off + n].reshape(s))
        off += n
    return out


def _row_tile(L):
    return 256 if L % 256 == 0 else L


def _layer0_mix(diff, const):
    x, mod, norm_w, lam_re, lam_im, log_dt, b_re, b_im, c_re, c_im, s5_d, *slots = diff
    ada_b, weights = const
    L = x.shape[0]
    tm = _row_tile(L)
    mods = mod.reshape(2, 1, D_MODEL)
    biases = ada_b.reshape(2, 1, D_MODEL)
    op_ln0 = make_rowwise(_f_lnmod, "ln0", tm, 1, 5, pass_first=True)
    h, x = op_ln0((x,), (norm_w.reshape(1, D_MODEL), mods[1], mods[0], biases[1], biases[0]))
    u, z = make_proj("s5_in")(h, tuple(weights), tuple(slots))
    blocks = _s5_block_params(lam_re, lam_im, log_dt, b_re, b_im, c_re, c_im)
    y2 = make_s5_core(min(S5_TL, L))(u, *blocks, s5_d.reshape(1, D_INNER))
    return x, y2, z


def _layer0_out(diff, weights):
    y2, z, *slots = diff
    tm = _row_tile(y2.shape[0])
    t, y2 = make_mm("s5_glu", pass_input=True)(y2, weights[0], slots[0])
    (y4,) = make_rowwise(_f_s5_gate, "s5_gate", tm, 3, 0)((y2, t, z), ())
    return make_mm("s5_out")(y4, weights[1], slots[1])


def _f_res_lnmod(x, o, gate, bgate, nw, sc, sh, bsc, bsh):
    (x1,) = _f_res(x, o, gate, bgate)
    return _f_lnmod(x1, nw, sc, sh, bsc, bsh) + (x1,)


def _f_res_loss(x, y, tgt, gate, bgate, fw):
    return _f_loss(_f_res(x, y, gate, bgate)[0], tgt, fw)


def _layer1_loss(diff, const):
    x, o, gate0, mod, norm_w, conv_w, a_log, dt_bias, gdn_nw, final_nw, *slots = diff
    tgt, bgate0, ada_b, weights = const
    L = x.shape[0]
    tm = _row_tile(L)
    mods = mod.reshape(3, 1, D_MODEL)
    biases = ada_b.reshape(3, 1, D_MODEL)
    h, x1 = make_rowwise(_f_res_lnmod, "res0_ln1", tm, 2, 7)(
        (x, o), (gate0.reshape(1, D_MODEL), bgate0.reshape(1, D_MODEL), norm_w.reshape(1, D_MODEL), mods[1], mods[0], biases[1], biases[0]))
    q0, k0, v0, gz, ba = make_proj("gdn_in")(h, tuple(weights[0:5]), tuple(slots[0:5]))
    cw = jnp.concatenate([conv_w, jnp.zeros((SUBLANES - GDN_CONV, GDN_CONV_CH), F32)], axis=0)
    q = make_conv_act(lambda t: _l2n(_silu(t)) * (GDN_DK ** -0.5), "gdn_conv_q")(q0, cw[:, :GDN_QK])
    k = make_conv_act(lambda t: _l2n(_silu(t)), "gdn_conv_k")(k0, cw[:, GDN_QK:2 * GDN_QK])
    v = make_conv_act(_silu, "gdn_conv_v")(v0, cw[:, 2 * GDN_QK:])
    pad = jnp.zeros((LANES - 2 * GDN_HEADS,), F32)
    alog_row = jnp.concatenate([jnp.zeros((GDN_HEADS,), F32), a_log, pad]).reshape(1, LANES)
    dtb_row = jnp.concatenate([jnp.zeros((GDN_HEADS,), F32), dt_bias, pad]).reshape(1, LANES)
    (bg,) = make_rowwise(_f_betag, "gdn_bg", tm, 1, 2)((ba,), (alog_row, dtb_row))
    nw_row = jnp.tile(gdn_nw, GDN_HEADS).reshape(1, D_INNER)
    on = gdn_scan(*gdn_prep(q, k, v, bg), gz, nw_row)
    y = make_mm("gdn_out")(on, weights[5], slots[5])
    (lt,) = make_rowwise(_f_res_loss, "res1_loss", tm, 3, 3)((x1, y, tgt), (mods[2], biases[2], final_nw.reshape(1, D_MODEL)))
    return jnp.sum(lt)


VEC_NAMES = ("ada_b", "norm_w", "s5_lambda_re", "s5_lambda_im", "s5_log_dt", "s5_d", "gdn_a_log", "gdn_dt_bias", "final_norm_w")
MAT_NAMES = ("s5_b_re", "s5_b_im", "s5_c_re", "s5_c_im")
S5_BIG = ("s5_w_in", "s5_w_glu", "s5_w_out")
GDN_BIG = ("gdn_w_in", "gdn_w_out")
BIG_NAMES = S5_BIG + GDN_BIG
WEIGHT_ORDER = ("ada_w", "ada_b", "norm_w", "s5_w_in", "s5_lambda_re", "s5_lambda_im", "s5_log_dt", "s5_b_re", "s5_b_im",
                "s5_c_re", "s5_c_im", "s5_d", "s5_w_glu", "s5_w_out", "gdn_w_in", "gdn_conv_w", "gdn_a_log", "gdn_dt_bias",
                "gdn_norm_w", "gdn_w_out", "final_norm_w")


def _step(x, c, W, M, V, tgt):
    L = x.shape[1]
    ix, iy, ic = lax.axis_index("x"), lax.axis_index("y"), lax.axis_index("c")
    me = 4 * ix + 2 * iy + ic
    n_ada = W["ada_w"].shape[2]
    n_conv = W["gdn_conv_w"].shape[2]
    n_gnw = W["gdn_norm_w"].shape[1]

    g1 = _allgather_call(_pack([c, W["gdn_conv_w"], W["gdn_norm_w"]], SUBLANES), "gather_small_in", False)
    g1 = g1.reshape(N_DEV, -1)
    c_all = g1[:, :D_MODEL]
    conv_w = g1[:, D_MODEL:D_MODEL + GDN_CONV * n_conv].reshape(N_DEV, GDN_CONV, n_conv).transpose(1, 0, 2).reshape(GDN_CONV, -1)
    gdn_nw = g1[:, D_MODEL + GDN_CONV * n_conv:D_MODEL + GDN_CONV * n_conv + n_gnw].reshape(-1)
    mod_part = _ada_mod_call(c_all, W["ada_w"])
    g2 = _allgather_call(_pack([mod_part], SUBLANES), "gather_mod", False).reshape(N_DEV, -1)
    mod_all = g2[:, :2 * N_DEV * n_ada].reshape(N_DEV, 2, N_DEV, n_ada)
    mod_raw = lax.dynamic_index_in_dim(mod_all, me, axis=2, keepdims=False)
    mod_raw = mod_raw.transpose(1, 0, 2).reshape(2, 3 * D_MODEL)

    shard = lambda n: W[n][0].astype(BF16)
    (w_in5_parts,) = _gather_weights_call([shard("s5_w_in")], "gather_s5_w_in")
    late = _spread_start_call([shard("s5_w_glu"), shard("s5_w_out")], False, "gather_s5_late_start", w_in5_parts)
    g_send, g_recv, g_srcs, g_lands, g_token = _spread_start_call([shard(n) for n in GDN_BIG], False, "gather_gdn_start", late[4])
    w_in5 = _join_cols_call(w_in5_parts, "join_s5_w_in")
    slot = lambda *s: jnp.zeros(s, F32)
    two = 2 * D_MODEL
    diff_mix = (x[0], mod_raw[0, :two] + g_token[0, 0], W["norm_w"][0], W["s5_lambda_re"][0], W["s5_lambda_im"][0], W["s5_log_dt"][0],
                W["s5_b_re"][0], W["s5_b_im"][0], W["s5_c_re"][0], W["s5_c_im"][0], W["s5_d"][0],
                slot(D_MODEL, D_INNER), slot(D_MODEL, D_INNER))

    (xp, y2, z5), vjp_mix = jax.vjp(lambda d: _layer0_mix(d, (W["ada_b"][0, :two], (w_in5[:, :D_INNER], w_in5[:, D_INNER:]))), diff_mix)
    l_srcs, l_lands = _spread_wait_call(late[0], late[1], late[2], late[3], y2, "gather_s5_late_wait")
    w_glu, w_o5 = [lax.dynamic_update_slice(land, src[None], (me, 0, 0)).reshape(-1, src.shape[1]) for land, src in zip(l_lands, l_srcs)]
    diff_out = (y2, z5, slot(D_INNER, D_INNER), slot(D_INNER, D_MODEL))
    o5, vjp_out = jax.vjp(lambda d: _layer0_out(d, (w_glu, w_o5)), diff_out)
    g_srcs, g_lands = _spread_wait_call(g_send, g_recv, g_srcs, g_lands, o5, "gather_gdn_wait")
    gdn_full = [lax.dynamic_update_slice(land, src[None], (me, 0, 0)) for land, src in zip(g_lands, g_srcs)]
    w_ing = _join_cols_call(gdn_full[0], "join_gdn_w_in")
    w_ba = jnp.concatenate([w_ing[:, GDN_CONV_CH + D_INNER:], jnp.zeros((D_MODEL, LANES - 2 * GDN_HEADS), BF16)], axis=1)
    weights1 = (w_ing[:, :GDN_QK], w_ing[:, GDN_QK:2 * GDN_QK], w_ing[:, 2 * GDN_QK:GDN_CONV_CH],
                w_ing[:, GDN_CONV_CH:GDN_CONV_CH + D_INNER], w_ba, gdn_full[1].reshape(D_INNER, D_MODEL))
    slots1 = tuple(jnp.zeros(w.shape, F32) for w in weights1)
    diff1 = (xp, o5, mod_raw[0, two:], mod_raw[1], W["norm_w"][1], conv_w, W["gdn_a_log"][0], W["gdn_dt_bias"][0], gdn_nw,
             W["final_norm_w"], *slots1)
    loss_local, vjp1 = jax.vjp(lambda d: _layer1_loss(d, (tgt[0], W["ada_b"][0, two:], W["ada_b"][1], weights1)), diff1)
    ((dxp, do5, dmod_gate, dmod1, d_norm_w1, d_conv, d_alog, d_dtb, d_gnw, d_fnw, d_wq, d_wk, d_wv, d_wgz, d_wba, d_wog),) = vjp1(
        jnp.ones((), F32))
    loss = lax.psum(loss_local, MESH_AXES)

    rows = lambda d: d.reshape(N_DEV, d.shape[0] // N_DEV, d.shape[1])
    d_ing = _split_cols_call(jnp.concatenate([d_wq, d_wk, d_wv, d_wgz, d_wba[:, :2 * GDN_HEADS]], axis=1), "split_gdn_w_in", BF16)
    s_send, s_recv, s_srcs, s_lands, s_token = _spread_start_call([d_ing, rows(d_wog).astype(BF16)], True, "scatter_gdn_start", dxp)
    ((dy2, dz5, d_wglu, d_wo5),) = vjp_out(do5.at[0, 0].add(s_token[0, 0]))
    t_send, t_recv, t_srcs, t_lands, t_token = _spread_start_call(
        [rows(d_wglu).astype(BF16), rows(d_wo5).astype(BF16)], True, "scatter_s5_late_start", dy2)
    ((dx, dmod_ss, d_norm_w0, d_lre, d_lim, d_logdt, d_bre, d_bim, d_cre, d_cim, d_s5d, d_wu, d_wz),) = vjp_mix(
        (dxp.at[0, 0].add(t_token[0, 0]), dy2, dz5))
    dmod = jnp.stack([jnp.concatenate([dmod_ss, dmod_gate]), dmod1])
    d_norm_w = jnp.stack([d_norm_w0, d_norm_w1])
    vec_parts = [dmod, d_norm_w, d_lre, d_lim, d_logdt, d_s5d, d_alog, d_dtb, d_fnw]
    tail_parts = [d_conv, d_gnw]
    mat_parts = [_groups_last(d) for d in (d_bre, d_bim, d_cre, d_cim)]
    n_vec = sum(math.prod(p.shape) for p in vec_parts)
    m_send, m_recv, m_srcs, m_lands, m_token = _spread_start_call(
        [_pack(vec_parts + tail_parts, ADAM_ROWS), _pack(mat_parts, SUBLANES).astype(BF16)], False, "gather_small_grads_start", dx)
    d_in5 = _split_cols_call(jnp.concatenate([d_wu.at[0, 0].add(m_token[0, 0]), d_wz], axis=1), "split_s5_w_in", F32)
    (got,) = _pair_exchange_call([d_in5], "scatter_s5_pair")
    core = jnp.reshape(ic, (1,)).astype(jnp.int32)
    chip = jnp.reshape(2 * ix + iy, (1,)).astype(jnp.int32)
    pair = _pair_sum_call(d_in5, got, core, "pair_sum_s5_w_in")
    (recv,) = _chip_exchange_call([pair], "scatter_s5_chips")
    big = {"s5_w_in": _adam_own_call(pair, chip, recv, W["s5_w_in"][0], M["s5_w_in"][0], V["s5_w_in"][0], "adam_s5_w_in", 128)}
    t_srcs, t_lands = _spread_wait_call(t_send, t_recv, t_srcs, t_lands, dx, "scatter_s5_late_wait")
    s_srcs, s_lands = _spread_wait_call(s_send, s_recv, s_srcs, s_lands, t_lands[0], "scatter_gdn_wait")
    for land, src, n in zip(tuple(t_lands) + tuple(s_lands), tuple(t_srcs) + tuple(s_srcs), ("s5_w_glu", "s5_w_out") + GDN_BIG):
        mine = lax.dynamic_index_in_dim(src, me, 0, keepdims=True)
        parts = lax.dynamic_update_slice(land, mine, (me, 0, 0))
        big[n] = _adam_call(parts, W[n][0], M[n][0], V[n][0], "adam_" + n, rows=_tile(W[n].shape[1], 128))
    big = [[o[None] for o in big[n]] for n in BIG_NAMES]

    m_srcs, m_lands = _spread_wait_call(m_send, m_recv, m_srcs, m_lands, big[0][0], "gather_small_grads_wait")
    sg_vec, sg_mat = [lax.dynamic_update_slice(land, src[None], (me, 0, 0)) for land, src in zip(m_lands, m_srcs)]
    tot_vec = _sum_call(sg_vec, "sum_vec_grads", ADAM_ROWS)
    tot_mat = _sum_call(sg_mat, "sum_mat_grads", ADAM_ROWS)
    g_conv, g_gnw = _unpack(tot_vec.reshape(-1)[n_vec:], [d_conv.shape, d_gnw.shape])
    g_conv_mine = lax.dynamic_slice_in_dim(g_conv, me * n_conv, n_conv, axis=1)
    g_gnw_mine = lax.dynamic_slice_in_dim(g_gnw, me * n_gnw, n_gnw, axis=0)
    vec_names = VEC_NAMES + ("gdn_conv_w", "gdn_norm_w")
    vec_g = _pack([tot_vec.reshape(-1)[:n_vec], g_conv_mine, g_gnw_mine], ADAM_ROWS)
    vec = _adam_call(vec_g[None], _pack([W[n] for n in vec_names], ADAM_ROWS), _pack([M[n] for n in vec_names], ADAM_ROWS),
                     _pack([V[n] for n in vec_names], ADAM_ROWS), "adam_vec")
    vec = [_unpack(b, [W[n].shape for n in vec_names]) for b in vec]
    mats = []
    for name, g_mat in zip(MAT_NAMES, _unpack(tot_mat, [p.shape for p in mat_parts])):
        outs = _adam_call(g_mat[None], _groups_last(W[name]), _groups_last(M[name]), _groups_last(V[name]), "adam_" + name)
        mats.append([_groups_first(o, W[name].shape) for o in outs])

    dmod_all = sg_vec[:, :2 * 3 * D_MODEL // LANES].reshape(N_DEV, 2, N_DEV, n_ada // LANES, LANES)
    dmod_mine = lax.dynamic_index_in_dim(dmod_all, me, axis=2, keepdims=False).transpose(1, 0, 2, 3).reshape(2, N_DEV, n_ada)
    g_ada_w = _ada_grad_call(c_all, dmod_mine)
    ada = _adam_call(g_ada_w.reshape(1, -1, LANES), W["ada_w"].reshape(-1, LANES), M["ada_w"].reshape(-1, LANES),
                     V["ada_w"].reshape(-1, LANES), "adam_ada")
    ada = [a.reshape(W["ada_w"].shape) for a in ada]

    res = {}
    for i, n in enumerate(BIG_NAMES):
        res[n] = big[i]
    for i, n in enumerate(vec_names):
        res[n] = [b[i] for b in vec]
    for i, n in enumerate(MAT_NAMES):
        res[n] = mats[i]
    res["ada_w"] = ada
    outs = [loss, dx[None]]
    for j in range(4):
        outs += [res[n][j] for n in WEIGHT_ORDER]
    return tuple(outs)


def kernel(x, c, ada_w, ada_b, norm_w, s5_w_in, s5_lambda_re, s5_lambda_im, s5_log_dt, s5_b_re, s5_b_im, s5_c_re, s5_c_im, s5_d, s5_w_glu, s5_w_out, gdn_w_in, gdn_conv_w, gdn_a_log, gdn_dt_bias, gdn_norm_w, gdn_w_out, final_norm_w, loss_target, m_ada_w, m_ada_b, m_norm_w, m_s5_w_in, m_s5_lambda_re, m_s5_lambda_im, m_s5_log_dt, m_s5_b_re, m_s5_b_im, m_s5_c_re, m_s5_c_im, m_s5_d, m_s5_w_glu, m_s5_w_out, m_gdn_w_in, m_gdn_conv_w, m_gdn_a_log, m_gdn_dt_bias, m_gdn_norm_w, m_gdn_w_out, m_final_norm_w, v_ada_w, v_ada_b, v_norm_w, v_s5_w_in, v_s5_lambda_re, v_s5_lambda_im, v_s5_log_dt, v_s5_b_re, v_s5_b_im, v_s5_c_re, v_s5_c_im, v_s5_d, v_s5_w_glu, v_s5_w_out, v_gdn_w_in, v_gdn_conv_w, v_gdn_a_log, v_gdn_dt_bias, v_gdn_norm_w, v_gdn_w_out, v_final_norm_w):
    W = dict(ada_w=ada_w, ada_b=ada_b, norm_w=norm_w, s5_w_in=s5_w_in, s5_lambda_re=s5_lambda_re, s5_lambda_im=s5_lambda_im,
             s5_log_dt=s5_log_dt, s5_b_re=s5_b_re, s5_b_im=s5_b_im, s5_c_re=s5_c_re, s5_c_im=s5_c_im, s5_d=s5_d,
             s5_w_glu=s5_w_glu, s5_w_out=s5_w_out, gdn_w_in=gdn_w_in, gdn_conv_w=gdn_conv_w, gdn_a_log=gdn_a_log,
             gdn_dt_bias=gdn_dt_bias, gdn_norm_w=gdn_norm_w, gdn_w_out=gdn_w_out, final_norm_w=final_norm_w)
    M = dict(ada_w=m_ada_w, ada_b=m_ada_b, norm_w=m_norm_w, s5_w_in=m_s5_w_in, s5_lambda_re=m_s5_lambda_re,
             s5_lambda_im=m_s5_lambda_im, s5_log_dt=m_s5_log_dt, s5_b_re=m_s5_b_re, s5_b_im=m_s5_b_im, s5_c_re=m_s5_c_re,
             s5_c_im=m_s5_c_im, s5_d=m_s5_d, s5_w_glu=m_s5_w_glu, s5_w_out=m_s5_w_out, gdn_w_in=m_gdn_w_in,
             gdn_conv_w=m_gdn_conv_w, gdn_a_log=m_gdn_a_log, gdn_dt_bias=m_gdn_dt_bias, gdn_norm_w=m_gdn_norm_w,
             gdn_w_out=m_gdn_w_out, final_norm_w=m_final_norm_w)
    V = dict(ada_w=v_ada_w, ada_b=v_ada_b, norm_w=v_norm_w, s5_w_in=v_s5_w_in, s5_lambda_re=v_s5_lambda_re,
             s5_lambda_im=v_s5_lambda_im, s5_log_dt=v_s5_log_dt, s5_b_re=v_s5_b_re, s5_b_im=v_s5_b_im, s5_c_re=v_s5_c_re,
             s5_c_im=v_s5_c_im, s5_d=v_s5_d, s5_w_glu=v_s5_w_glu, s5_w_out=v_s5_w_out, gdn_w_in=v_gdn_w_in,
             gdn_conv_w=v_gdn_conv_w, gdn_a_log=v_gdn_a_log, gdn_dt_bias=v_gdn_dt_bias, gdn_norm_w=v_gdn_norm_w,
             gdn_w_out=v_gdn_w_out, final_norm_w=v_final_norm_w)
    return _step(x, c, W, M, V, loss_target)
```

```python
import functools
import math

import jax
import jax.numpy as jnp
from jax import lax
from jax.experimental import pallas as pl
from jax.experimental.pallas import tpu as pltpu

F32 = jnp.float32
BF16 = jnp.bfloat16
SDS = jax.ShapeDtypeStruct

D_MODEL = 1024
D_INNER = 2048
NORM_EPS = 1e-6
S5_GROUP = 16
S5_GROUPS = 128
S5_STATE = 64
GDN_HEADS = 8
GDN_DK = 128
GDN_DV = 256
GDN_CONV = 4
GDN_CHUNK = 64
GDN_QK = 1024
GDN_CONV_CH = 4096
GDN_PROJ = 6160
ADAM_LR = 0.001
ADAM_B1 = 0.9
ADAM_B2 = 0.999
ADAM_EPS = 1e-08
ADAM_WD = 0.01
ADAM_STEP = 10

N_DEV = 8
LANES = 128
SUBLANES = 8
VMEM_BIG = 56 << 20
VMEM_MID = 40 << 20
S5_GB = 8
S5_TL = 1024
MESH_AXES = ("x", "y", "c")


def _params(sem, vmem=None):
    return pltpu.CompilerParams(dimension_semantics=sem, vmem_limit_bytes=vmem)


def _bdot(a, b, dims=(((1,), (0,)), ((), ()))):
    return lax.dot_general(a.astype(BF16), b.astype(BF16), dims, preferred_element_type=F32)


def _hdot(a, b, dims=(((1,), (0,)), ((), ()))):
    return lax.dot_general(a, b, dims, preferred_element_type=F32, precision=lax.Precision.HIGHEST)


_BNN = (((2,), (1,)), ((0,), (0,)))
_BNT = (((2,), (2,)), ((0,), (0,)))
_BTN = (((1,), (1,)), ((0,), (0,)))


@jax.custom_vjp
def _unit_lower_inverse(a):
    c = a.shape[-1]
    ri = lax.broadcasted_iota(jnp.int32, a.shape, 1)
    ci = lax.broadcasted_iota(jnp.int32, a.shape, 2)
    n = -a
    t = (ri == ci).astype(F32) + n
    for _ in range(int(math.log2(c)) - 1):
        n = _hdot(n, n, _BNN)
        t = t + _hdot(t, n, _BNN)
    return t


def _unit_lower_inverse_fwd(a):
    t = _unit_lower_inverse(a)
    return t, t


def _unit_lower_inverse_bwd(t, g):
    return (-_hdot(_hdot(t, g, _BTN), t, _BNT),)


_unit_lower_inverse.defvjp(_unit_lower_inverse_fwd, _unit_lower_inverse_bwd)


NN = (((1,), (0,)), ((), ()))
NT = (((1,), (1,)), ((), ()))
TN = (((0,), (0,)), ((), ()))


def _tile(n, pref):
    for t in (pref, 512, 256, 128):
        if t <= n and n % t == 0:
            return t
    return n


def _matmul(a, b, mode, name, add=None):
    if mode == "nn":
        (m, k), (_, n) = a.shape, b.shape
    elif mode == "nt":
        (m, k), (n, _) = a.shape, b.shape
    else:
        (k, m), (_, n) = a.shape, b.shape
    tm, tn, tk = _tile(m, 1024), _tile(n, 512), (k if k <= 2048 else _tile(k, 512))
    if mode == "tn":
        tm, tn, tk = _tile(m, 1024), _tile(n, 1024), _tile(k, 1024)
    nk = k // tk
    dims = {"nn": NN, "nt": NT, "tn": TN}[mode]

    def body(a_ref, b_ref, *rest):
        o_ref, acc_ref = rest[-2], rest[-1]
        part = _bdot(a_ref[...], b_ref[...], dims)
        if nk == 1:
            o_ref[...] = part if add is None else part + rest[0][...]
            return
        kk = pl.program_id(2)

        @pl.when(kk == 0)
        def _():
            acc_ref[...] = part if add is None else part + rest[0][...]

        @pl.when(kk > 0)
        def _():
            acc_ref[...] += part

        @pl.when(kk == nk - 1)
        def _():
            o_ref[...] = acc_ref[...]

    a_spec = pl.BlockSpec((tk, tm), lambda i, j, q: (q, i)) if mode == "tn" else pl.BlockSpec((tm, tk), lambda i, j, q: (i, q))
    b_spec = pl.BlockSpec((tn, tk), lambda i, j, q: (j, q)) if mode == "nt" else pl.BlockSpec((tk, tn), lambda i, j, q: (q, j))
    o_spec = pl.BlockSpec((tm, tn), lambda i, j, q: (i, j))
    return pl.pallas_call(
        body, name=name, grid=(m // tm, n // tn, nk),
        in_specs=[a_spec, b_spec] + ([] if add is None else [o_spec]), out_specs=o_spec,
        out_shape=SDS((m, n), F32), scratch_shapes=[pltpu.VMEM((tm, tn), F32)],
        compiler_params=_params(("parallel", "parallel", "arbitrary"), VMEM_MID),
    )(a, b, *([] if add is None else [add]))


def make_mm(name, pass_input=False):
    def primal(a, w):
        out = _matmul(a, w, "nn", name + "_fwd")
        return (out, a) if pass_input else out

    @jax.custom_vjp
    def mm(a, w, grad_slot):
        return primal(a, w)

    def fwd(a, w, grad_slot):
        return primal(a, w), (a, w)

    def bwd(res, g):
        a, w = res
        g, g_other = g if pass_input else (g, None)
        return _matmul(g, w, "nt", name + "_dx", add=g_other), jnp.zeros_like(w), _matmul(a, g, "tn", name + "_dw")

    mm.defvjp(fwd, bwd)
    return mm


PROJ_ROWS = 256


def _proj_fwd_call(a, ws, name, w_rows):
    m, k = a.shape
    tm = _tile(m, PROJ_ROWS)
    nw = len(ws)
    widths = [w.shape[0] if w_rows else w.shape[1] for w in ws]

    def body(*refs):
        ab = refs[0][...].astype(BF16)
        for w_ref, o_ref in zip(refs[1:1 + nw], refs[1 + nw:]):
            o_ref[...] = lax.dot_general(ab, w_ref[...], NT if w_rows else NN, preferred_element_type=F32)

    return pl.pallas_call(
        body, name=name, grid=(m // tm,),
        in_specs=[pl.BlockSpec((tm, k), lambda i: (i, 0))] + [pl.BlockSpec(w.shape, lambda i: (0, 0)) for w in ws],
        out_specs=[pl.BlockSpec((tm, n), lambda i: (i, 0)) for n in widths],
        out_shape=[SDS((m, n), F32) for n in widths],
        compiler_params=_params(("parallel",), VMEM_BIG),
    )(a, *ws)


def _proj_dx_call(gs, ws, name, w_rows):
    m = gs[0].shape[0]
    k = ws[0].shape[1] if w_rows else ws[0].shape[0]
    tm = _tile(m, PROJ_ROWS)
    nw = len(ws)

    def body(*refs):
        acc = None
        for g_ref, w_ref in zip(refs[:nw], refs[nw:2 * nw]):
            part = _bdot(g_ref[...], w_ref[...], NN if w_rows else NT)
            acc = part if acc is None else acc + part
        refs[2 * nw][...] = acc

    return pl.pallas_call(
        body, name=name, grid=(m // tm,),
        in_specs=[pl.BlockSpec((tm, g.shape[1]), lambda i: (i, 0)) for g in gs] + [pl.BlockSpec(w.shape, lambda i: (0, 0)) for w in ws],
        out_specs=pl.BlockSpec((tm, k), lambda i: (i, 0)), out_shape=SDS((m, k), F32),
        compiler_params=_params(("parallel",), VMEM_BIG),
    )(*gs, *ws)


def make_proj(name, w_rows=False):
    @jax.custom_vjp
    def proj(a, ws, grad_slots):
        return tuple(_proj_fwd_call(a, ws, name + "_fwd", w_rows))

    def fwd(a, ws, grad_slots):
        return tuple(_proj_fwd_call(a, ws, name + "_fwd", w_rows)), (a, ws)

    def bwd(res, gs):
        a, ws = res
        dws = tuple(_matmul(g, a, "tn", "%s_dw%d" % (name, i)) if w_rows else _matmul(a, g, "tn", "%s_dw%d" % (name, i))
                    for i, g in enumerate(gs))
        return _proj_dx_call(tuple(gs), ws, name + "_dx", w_rows), tuple(jnp.zeros_like(w) for w in ws), dws

    proj.defvjp(fwd, bwd)
    return proj


def make_rowwise(f, name, tm, n_rows, n_params, vmem=VMEM_MID, pass_first=False):
    def specs_of(arrs, blocked):
        if blocked:
            return [pl.BlockSpec((tm, a.shape[1]), lambda i: (i, 0)) for a in arrs]
        return [pl.BlockSpec(a.shape, lambda i: (0, 0)) for a in arrs]

    def out_structs(rows, params):
        blk = [SDS((tm, r.shape[1]), r.dtype) for r in rows] + [SDS(p.shape, p.dtype) for p in params]
        return jax.eval_shape(f, *blk)

    def run_fwd(rows, params):
        L = rows[0].shape[0]
        outs = out_structs(rows, params)

        def body(*refs):
            ins = [r[...] for r in refs[:n_rows + n_params]]
            res = f(*ins)
            for o_ref, val in zip(refs[n_rows + n_params:], res):
                o_ref[...] = val

        return pl.pallas_call(
            body, name=name + "_fwd", grid=(L // tm,),
            in_specs=specs_of(rows, True) + specs_of(params, False),
            out_specs=[pl.BlockSpec((tm, o.shape[1]), lambda i: (i, 0)) for o in outs],
            out_shape=[SDS((L, o.shape[1]), o.dtype) for o in outs],
            compiler_params=_params(("parallel",), vmem),
        )(*rows, *params)

    def run_bwd(rows, params, gs):
        L = rows[0].shape[0]
        n_g = len(gs)

        def body(*refs):
            i = pl.program_id(0)
            ins = [r[...] for r in refs[:n_rows + n_params]]
            cts = tuple(r[...] for r in refs[n_rows + n_params:n_rows + n_params + n_g])
            outs = refs[n_rows + n_params + n_g:]
            _, vjp = jax.vjp(f, *ins)
            grads = vjp(cts[:-1] if pass_first else cts)
            if pass_first:
                grads = (grads[0] + cts[-1],) + tuple(grads[1:])
            for o_ref, val in zip(outs[:n_rows], grads[:n_rows]):
                o_ref[...] = val

            if n_params:
                @pl.when(i == 0)
                def _():
                    for o_ref in outs[n_rows:]:
                        o_ref[...] = jnp.zeros_like(o_ref)
                for o_ref, val in zip(outs[n_rows:], grads[n_rows:]):
                    o_ref[...] += val

        res = pl.pallas_call(
            body, name=name + "_bwd", grid=(L // tm,),
            in_specs=specs_of(rows, True) + specs_of(params, False) + specs_of(gs, True),
            out_specs=specs_of(rows, True) + specs_of(params, False),
            out_shape=[SDS(r.shape, r.dtype) for r in rows] + [SDS(p.shape, p.dtype) for p in params],
            compiler_params=_params(("arbitrary",), vmem),
        )(*rows, *params, *gs)
        return tuple(res[:n_rows]), tuple(res[n_rows:])

    def outputs(rows, params):
        outs = tuple(run_fwd(rows, params))
        return outs + (rows[0],) if pass_first else outs

    @jax.custom_vjp
    def op(rows, params):
        return outputs(rows, params)

    def fwd(rows, params):
        return outputs(rows, params), (rows, params)

    def bwd(res, gs):
        rows, params = res
        return run_bwd(rows, params, tuple(gs))

    op.defvjp(fwd, bwd)
    op.run_fwd, op.run_bwd = run_fwd, run_bwd
    return op


def make_residual(name, tm):
    full = make_rowwise(_f_res, name, tm, 2, 2)
    branch = make_rowwise(lambda y, gate, bgate: ((gate + bgate) * y,), name + "_branch", tm, 1, 2)

    @jax.custom_vjp
    def op(x, y, gate, bgate):
        return full.run_fwd((x, y), (gate, bgate))[0]

    def fwd(x, y, gate, bgate):
        return full.run_fwd((x, y), (gate, bgate))[0], (y, gate, bgate)

    def bwd(res, g):
        y, gate, bgate = res
        (dy,), (dgate, dbgate) = branch.run_bwd((y,), (gate, bgate), (g,))
        return g, dy, dgate, dbgate

    op.defvjp(fwd, bwd)
    return op


def _s5_scan_rows(xr_ref, xi_ref, ar, ai, x0r, x0i, tl, reverse=False):
    n = xr_ref.shape[1]
    T = SUBLANES
    row = lax.broadcasted_iota(jnp.int32, (T, n), 0)
    pr, pi = [ar], [ai]
    for _ in range(T - 1):
        pr, pi = pr + [pr[-1] * ar - pi[-1] * ai], pi + [pr[-1] * ai + pi[-1] * ar]
    levels = []
    for d in (1, 2, 4):
        mask = (row < T - d) if reverse else (row >= d)
        levels.append((T - d if reverse else d, jnp.where(mask, pr[d - 1], 0.0), jnp.where(mask, pi[d - 1], 0.0)))
    cr = jnp.zeros((T, n), F32)
    ci = jnp.zeros((T, n), F32)
    for r in range(T):
        k = (T - r) if reverse else (r + 1)
        cr = jnp.where(row == r, pr[k - 1], cr)
        ci = jnp.where(row == r, pi[k - 1], ci)
    nt = tl // T
    last = 0 if reverse else T - 1

    def step(t, carry):
        sr, si = carry
        base = pl.multiple_of((nt - 1 - t if reverse else t) * T, T)
        br = xr_ref[pl.ds(base, T), :]
        bi = xi_ref[pl.ds(base, T), :]
        for shift, mr, mi in levels:
            qr = pltpu.roll(br, shift, 0)
            qi = pltpu.roll(bi, shift, 0)
            br, bi = br + (mr * qr - mi * qi), bi + (mr * qi + mi * qr)
        xr = br + (cr * sr - ci * si)
        xi = bi + (cr * si + ci * sr)
        xr_ref[pl.ds(base, T), :] = xr
        xi_ref[pl.ds(base, T), :] = xi
        return xr[last:last + 1, :], xi[last:last + 1, :]
    return lax.fori_loop(0, nt, step, (x0r, x0i))


def _s5_fwd_call(u, bre, bim, cre, cim, a, d, tl):
    L, e = u.shape
    nb = e // LANES
    ns = bre.shape[2]
    nc = L // tl

    def body(u_ref, bre_ref, bim_ref, cre_ref, cim_ref, a_ref, d_ref, y_ref, xb_ref, sr_ref, si_ref, xr_ref, xi_ref, carry_ref):
        c = pl.program_id(1)

        @pl.when(c == 0)
        def _():
            carry_ref[...] = jnp.zeros_like(carry_ref)
        xb_ref[0, 0] = carry_ref[...]
        ub = u_ref[...]
        xr_ref[...] = _bdot(ub, bre_ref[0])
        xi_ref[...] = _bdot(ub, bim_ref[0])
        ar = a_ref[0, 0:1, :]
        ai = a_ref[0, 1:2, :]
        xr, xi = _s5_scan_rows(xr_ref, xi_ref, ar, ai, carry_ref[0:1, :], carry_ref[1:2, :], tl)
        carry_ref[0:1, :] = xr
        carry_ref[1:2, :] = xi
        sr = xr_ref[...].astype(BF16)
        si = xi_ref[...].astype(BF16)
        sr_ref[...] = sr
        si_ref[...] = si
        y_ref[...] = _f_s5_act(_bdot(sr, cre_ref[0]) - _bdot(si, cim_ref[0]), ub, d_ref[...])[0]

    return pl.pallas_call(
        body, name="s5_core_fwd", grid=(nb, nc),
        in_specs=[pl.BlockSpec((tl, LANES), lambda j, c: (c, j)),
                  pl.BlockSpec((1, LANES, ns), lambda j, c: (j, 0, 0)), pl.BlockSpec((1, LANES, ns), lambda j, c: (j, 0, 0)),
                  pl.BlockSpec((1, ns, LANES), lambda j, c: (j, 0, 0)), pl.BlockSpec((1, ns, LANES), lambda j, c: (j, 0, 0)),
                  pl.BlockSpec((1, SUBLANES, ns), lambda j, c: (j, 0, 0)), pl.BlockSpec((1, LANES), lambda j, c: (0, j))],
        out_specs=[pl.BlockSpec((tl, LANES), lambda j, c: (c, j)),
                   pl.BlockSpec((1, 1, SUBLANES, ns), lambda j, c: (j, c, 0, 0)),
                   pl.BlockSpec((tl, ns), lambda j, c: (c, j)), pl.BlockSpec((tl, ns), lambda j, c: (c, j))],
        out_shape=[SDS((L, e), F32), SDS((nb, nc, SUBLANES, ns), F32), SDS((L, nb * ns), BF16), SDS((L, nb * ns), BF16)],
        scratch_shapes=[pltpu.VMEM((tl, ns), F32), pltpu.VMEM((tl, ns), F32), pltpu.VMEM((SUBLANES, ns), F32)],
        compiler_params=_params(("arbitrary", "arbitrary"), VMEM_MID),
    )(u, bre, bim, cre, cim, a, d)


def _s5_bwd_call(u, dy2, bre, bim, cre, cim, a, d, xb, sr, si, tl):
    L, e = u.shape
    nb = e // LANES
    ns = bre.shape[2]
    nc = L // tl

    def body(u_ref, dy2_ref, bre_ref, bim_ref, cre_ref, cim_ref, a_ref, d_ref, xb_ref, sr_ref, si_ref,
             du_ref, dbre_ref, dbim_ref, dcre_ref, dcim_ref, da_ref, dd_ref,
             gr_ref, gi_ref, gcarry_ref):
        c = pl.program_id(1)

        @pl.when(c == 0)
        def _():
            gcarry_ref[...] = jnp.zeros_like(gcarry_ref)
            dbre_ref[...] = jnp.zeros_like(dbre_ref)
            dbim_ref[...] = jnp.zeros_like(dbim_ref)
            dcre_ref[...] = jnp.zeros_like(dcre_ref)
            dcim_ref[...] = jnp.zeros_like(dcim_ref)
            da_ref[...] = jnp.zeros_like(da_ref)
            dd_ref[...] = jnp.zeros_like(dd_ref)

        ub = u_ref[...]
        ys = _bdot(sr_ref[...], cre_ref[0]) - _bdot(si_ref[...], cim_ref[0])
        _, act_vjp = jax.vjp(lambda *t: _f_s5_act(*t)[0], ys, ub, d_ref[...])
        dy, du_skip, dd = act_vjp(dy2_ref[...])
        dd_ref[...] += dd
        ar = a_ref[0, 0:1, :]
        ai = a_ref[0, 1:2, :]
        x0r = xb_ref[0, 0, 0:1, :]
        x0i = xb_ref[0, 0, 1:2, :]
        dcre_ref[0] += _bdot(sr_ref[...], dy, TN)
        dcim_ref[0] -= _bdot(si_ref[...], dy, TN)
        gr_ref[...] = _bdot(dy, cre_ref[0], NT)
        gi_ref[...] = -_bdot(dy, cim_ref[0], NT)

        g0r, g0i = _s5_scan_rows(gr_ref, gi_ref, ar, -ai, gcarry_ref[0:1, :], gcarry_ref[1:2, :], tl, reverse=True)
        gcarry_ref[0:1, :] = g0r
        gcarry_ref[1:2, :] = g0i
        row = lax.broadcasted_iota(jnp.int32, (tl, ns), 0)
        gr = gr_ref[...]
        gi = gi_ref[...]
        xpr = jnp.where(row == 0, x0r, pltpu.roll(sr_ref[...].astype(F32), 1, 0))
        xpi = jnp.where(row == 0, x0i, pltpu.roll(si_ref[...].astype(F32), 1, 0))
        da_ref[0, 0:1, :] += jnp.sum(gr * xpr + gi * xpi, axis=0, keepdims=True)
        da_ref[0, 1:2, :] += jnp.sum(gi * xpr - gr * xpi, axis=0, keepdims=True)
        du_ref[...] = (_bdot(gr, bre_ref[0], NT) + _bdot(gi, bim_ref[0], NT)) + du_skip
        dbre_ref[0] += _bdot(ub, gr, TN)
        dbim_ref[0] += _bdot(ub, gi, TN)

    rev = lambda c: nc - 1 - c
    return pl.pallas_call(
        body, name="s5_core_bwd", grid=(nb, nc),
        in_specs=[pl.BlockSpec((tl, LANES), lambda j, c: (rev(c), j)), pl.BlockSpec((tl, LANES), lambda j, c: (rev(c), j)),
                  pl.BlockSpec((1, LANES, ns), lambda j, c: (j, 0, 0)), pl.BlockSpec((1, LANES, ns), lambda j, c: (j, 0, 0)),
                  pl.BlockSpec((1, ns, LANES), lambda j, c: (j, 0, 0)), pl.BlockSpec((1, ns, LANES), lambda j, c: (j, 0, 0)),
                  pl.BlockSpec((1, SUBLANES, ns), lambda j, c: (j, 0, 0)), pl.BlockSpec((1, LANES), lambda j, c: (0, j)),
                  pl.BlockSpec((1, 1, SUBLANES, ns), lambda j, c: (j, rev(c), 0, 0)),
                  pl.BlockSpec((tl, ns), lambda j, c: (rev(c), j)), pl.BlockSpec((tl, ns), lambda j, c: (rev(c), j))],
        out_specs=[pl.BlockSpec((tl, LANES), lambda j, c: (rev(c), j)),
                   pl.BlockSpec((1, LANES, ns), lambda j, c: (j, 0, 0)), pl.BlockSpec((1, LANES, ns), lambda j, c: (j, 0, 0)),
                   pl.BlockSpec((1, ns, LANES), lambda j, c: (j, 0, 0)), pl.BlockSpec((1, ns, LANES), lambda j, c: (j, 0, 0)),
                   pl.BlockSpec((1, SUBLANES, ns), lambda j, c: (j, 0, 0)), pl.BlockSpec((1, LANES), lambda j, c: (0, j))],
        out_shape=[SDS((L, e), F32), SDS(bre.shape, F32), SDS(bim.shape, F32), SDS(cre.shape, F32), SDS(cim.shape, F32),
                   SDS(a.shape, F32), SDS(d.shape, F32)],
        scratch_shapes=[pltpu.VMEM((tl, ns), F32) for _ in range(2)] + [pltpu.VMEM((SUBLANES, ns), F32)],
        compiler_params=_params(("arbitrary", "arbitrary"), VMEM_MID),
    )(u, dy2, bre, bim, cre, cim, a, d, xb, sr, si)


def make_s5_core(tl):
    @jax.custom_vjp
    def s5_core(u, bre, bim, cre, cim, a, d):
        return _s5_fwd_call(u, bre, bim, cre, cim, a, d, tl)[0]

    def fwd(u, bre, bim, cre, cim, a, d):
        y2, xb, sr, si = _s5_fwd_call(u, bre, bim, cre, cim, a, d, tl)
        return y2, (u, bre, bim, cre, cim, a, d, xb, sr, si)

    def bwd(res, dy2):
        u, bre, bim, cre, cim, a, d, xb, sr, si = res
        return tuple(_s5_bwd_call(u, dy2, bre, bim, cre, cim, a, d, xb, sr, si, tl))

    s5_core.defvjp(fwd, bwd)
    return s5_core


def _s5_block_params(lam_re, lam_im, log_dt, b_re, b_im, c_re, c_im):
    dt = jnp.exp(log_dt)[:, None]
    mag = jnp.exp(lam_re * dt)
    ab_re = mag * jnp.cos(lam_im * dt)
    ab_im = mag * jnp.sin(lam_im * dt)
    den = lam_re * lam_re + lam_im * lam_im
    nr = ab_re - 1.0
    ni = ab_im
    q_re = (nr * lam_re + ni * lam_im) / den
    q_im = (ni * lam_re - nr * lam_im) / den
    bb_re = q_re[..., None] * b_re - q_im[..., None] * b_im
    bb_im = q_re[..., None] * b_im + q_im[..., None] * b_re
    nb = S5_GROUPS // S5_GB
    eye = jnp.eye(S5_GB, dtype=F32)

    def bdiag_in(bb):
        t = bb.reshape(nb, S5_GB, S5_STATE, S5_GROUP)
        t = jnp.einsum("jgpm,gh->jgmhp", t, eye)
        return t.reshape(nb, S5_GB * S5_GROUP, S5_GB * S5_STATE)

    def bdiag_out(cc):
        t = cc.reshape(nb, S5_GB, S5_GROUP, S5_STATE)
        t = jnp.einsum("jgmp,gh->jgphm", t, eye)
        return t.reshape(nb, S5_GB * S5_STATE, S5_GB * S5_GROUP)

    a = jnp.stack([ab_re.reshape(nb, S5_GB * S5_STATE), ab_im.reshape(nb, S5_GB * S5_STATE)], axis=1)
    a = jnp.concatenate([a, jnp.zeros((nb, SUBLANES - 2, S5_GB * S5_STATE), F32)], axis=1)
    return bdiag_in(bb_re), bdiag_in(bb_im), bdiag_out(c_re), bdiag_out(c_im), a


def _shift_down(x, s, row):
    if s == 0:
        return x
    return jnp.where(row >= s, pltpu.roll(x, s, 0), 0.0)


def _shift_up(x, s, row, n):
    if s == 0:
        return x
    return jnp.where(row < n - s, pltpu.roll(x, n - s, 0), 0.0)


def _causal_conv(xv, w_ref, row):
    acc = jnp.zeros_like(xv)
    for j in range(GDN_CONV):
        acc += w_ref[j:j + 1, :] * _shift_down(xv, GDN_CONV - 1 - j, row)
    return acc


def _conv_fwd_call(x, w, act, name):
    L, ch = x.shape

    def body(x_ref, w_ref, y_ref):
        xv = x_ref[...]
        row = lax.broadcasted_iota(jnp.int32, xv.shape, 0)
        y_ref[...] = act(_causal_conv(xv, w_ref, row))

    return pl.pallas_call(
        body, name=name + "_fwd", grid=(ch // LANES,),
        in_specs=[pl.BlockSpec((L, LANES), lambda j: (0, j)), pl.BlockSpec((SUBLANES, LANES), lambda j: (0, j))],
        out_specs=pl.BlockSpec((L, LANES), lambda j: (0, j)), out_shape=SDS((L, ch), F32),
        compiler_params=_params(("parallel",), VMEM_MID),
    )(x, w)


def _conv_bwd_call(x, w, dy, act, name):
    L, ch = x.shape

    def body(x_ref, w_ref, dy_ref, dx_ref, dw_ref):
        xv = x_ref[...]
        row = lax.broadcasted_iota(jnp.int32, xv.shape, 0)
        _, act_vjp = jax.vjp(act, _causal_conv(xv, w_ref, row))
        (g,) = act_vjp(dy_ref[...])
        acc = jnp.zeros_like(xv)
        dws = []
        for j in range(GDN_CONV):
            s = GDN_CONV - 1 - j
            acc += w_ref[j:j + 1, :] * _shift_up(g, s, row, L)
            dws.append(jnp.sum(g * _shift_down(xv, s, row), axis=0, keepdims=True))
        dx_ref[...] = acc
        dw_ref[...] = jnp.concatenate(dws + [jnp.zeros((SUBLANES - GDN_CONV, LANES), F32)], axis=0)

    return pl.pallas_call(
        body, name=name + "_bwd", grid=(ch // LANES,),
        in_specs=[pl.BlockSpec((L, LANES), lambda j: (0, j)), pl.BlockSpec((SUBLANES, LANES), lambda j: (0, j)),
                  pl.BlockSpec((L, LANES), lambda j: (0, j))],
        out_specs=[pl.BlockSpec((L, LANES), lambda j: (0, j)), pl.BlockSpec((SUBLANES, LANES), lambda j: (0, j))],
        out_shape=[SDS((L, ch), F32), SDS((SUBLANES, ch), F32)],
        compiler_params=_params(("parallel",), VMEM_MID),
    )(x, w, dy)


def make_conv_act(act, name):
    @jax.custom_vjp
    def op(x, w):
        return _conv_fwd_call(x, w, act, name)

    def fwd(x, w):
        return _conv_fwd_call(x, w, act, name), (x, w)

    def bwd(res, dy):
        x, w = res
        return tuple(_conv_bwd_call(x, w, dy, act, name))

    op.defvjp(fwd, bwd)
    return op


gdn_conv = make_conv_act(lambda c: c, "gdn_conv")


BNN = (((2,), (1,)), ((0,), (0,)))
BNT = (((2,), (2,)), ((0,), (0,)))
BTN = (((1,), (1,)), ((0,), (0,)))
GDN_PREP_BATCH = 8


@jax.custom_vjp
def _known_inverse(a, t):
    return t


def _known_inverse_fwd(a, t):
    return t, t


def _known_inverse_bwd(t, g):
    return -_hdot(_hdot(t, g, _BTN), t, _BNT), jnp.zeros_like(t)


_known_inverse.defvjp(_known_inverse_fwd, _known_inverse_bwd)


def _gdn_prep_math(q, k, v, beta, g, t_saved=None):
    B, C = q.shape[0], q.shape[1]
    ri = lax.broadcasted_iota(jnp.int32, (B, C, C), 1)
    ci = lax.broadcasted_iota(jnp.int32, (B, C, C), 2)
    causal = ri >= ci
    strict = ri > ci
    eye = (ri == ci).astype(F32)
    gb = jnp.broadcast_to(g, (B, C, C))
    g_row = jnp.sum(gb * eye, axis=1, keepdims=True)
    gc_col = jnp.sum(jnp.where(causal, jnp.broadcast_to(g_row, (B, C, C)), 0.0), axis=2, keepdims=True)
    gc_row = jnp.sum(jnp.where(ri <= ci, gb, 0.0), axis=1, keepdims=True)
    decay = jnp.exp(jnp.where(causal, gc_col - gc_row, -jnp.inf))
    kk = _bdot(k, k, BNT)
    a_mat = jnp.where(strict, beta * kk * decay, 0.0)
    t = _unit_lower_inverse(a_mat) if t_saved is None else _known_inverse(a_mat, t_saved)
    e_gc = jnp.exp(gc_col)
    w = _hdot(t, beta * e_gc * k, BNN)
    u = _hdot(t, beta * v, BNN)
    qk = _bdot(q, k, BNT) * decay
    q_dec = q * e_gc
    g_last = gc_col[:, C - 1:C, :]
    k_dec = k * jnp.exp(g_last - gc_col)
    return q_dec, w, u, qk, k_dec, gc_col, t


def _gdn_prep_specs(L):
    C = GDN_CHUNK
    nb = min(GDN_PREP_BATCH, L // C)
    R = nb * C
    ins = [pl.BlockSpec((R, GDN_DK), lambda c, h: (c, h)), pl.BlockSpec((R, GDN_DK), lambda c, h: (c, h)),
           pl.BlockSpec((R, GDN_DV), lambda c, h: (c, h)), pl.BlockSpec((R, LANES), lambda c, h: (c, 0))]
    outs = [pl.BlockSpec((1, R, GDN_DK), lambda c, h: (h, c, 0)), pl.BlockSpec((1, R, GDN_DK), lambda c, h: (h, c, 0)),
            pl.BlockSpec((1, R, GDN_DV), lambda c, h: (h, c, 0)), pl.BlockSpec((1, R, C), lambda c, h: (h, c, 0)),
            pl.BlockSpec((1, R, GDN_DK), lambda c, h: (h, c, 0)), pl.BlockSpec((1, R, 1), lambda c, h: (h, c, 0))]
    t_spec = pl.BlockSpec((1, R, C), lambda c, h: (h, c, 0))
    shapes = [SDS((GDN_HEADS, L, GDN_DK), F32), SDS((GDN_HEADS, L, GDN_DK), F32), SDS((GDN_HEADS, L, GDN_DV), F32),
              SDS((GDN_HEADS, L, C), F32), SDS((GDN_HEADS, L, GDN_DK), F32), SDS((GDN_HEADS, L, 1), F32)]
    return ins, outs, t_spec, shapes, nb


def _chunks(x, nb):
    return x.reshape(nb, x.shape[0] // nb, x.shape[1])


def _head_columns(bg, h):
    lane = lax.broadcasted_iota(jnp.int32, bg.shape, 1)
    beta = jnp.sum(jnp.where(lane == h, bg, 0.0), axis=1, keepdims=True)
    g = jnp.sum(jnp.where(lane == h + GDN_HEADS, bg, 0.0), axis=1, keepdims=True)
    return beta, g


def _gdn_prep_fwd_call(q, k, v, bg):
    L = q.shape[0]
    ins, outs, t_spec, shapes, nb = _gdn_prep_specs(L)

    def body(q_ref, k_ref, v_ref, bg_ref, *o_refs):
        beta, g = _head_columns(bg_ref[...], pl.program_id(1))
        res = _gdn_prep_math(_chunks(q_ref[...], nb), _chunks(k_ref[...], nb), _chunks(v_ref[...], nb),
                             _chunks(beta, nb), _chunks(g, nb))
        for o_ref, val in zip(o_refs, res):
            o_ref[0] = val.reshape(val.shape[0] * val.shape[1], val.shape[2])

    return pl.pallas_call(
        body, name="gdn_prep_fwd", grid=(L // (nb * GDN_CHUNK), GDN_HEADS), in_specs=ins, out_specs=outs + [t_spec],
        out_shape=shapes + [SDS((GDN_HEADS, L, GDN_CHUNK), F32)],
        compiler_params=_params(("parallel", "parallel"), VMEM_MID),
    )(q, k, v, bg)


def _gdn_prep_bwd_call(q, k, v, bg, t, cts):
    L = q.shape[0]
    ins, outs, t_spec, _, nb = _gdn_prep_specs(L)

    def body(q_ref, k_ref, v_ref, bg_ref, t_ref, c0, c1, c2, c3, c4, c5, dq_ref, dk_ref, dv_ref, dbg_ref):
        h = pl.program_id(1)
        beta, g = _head_columns(bg_ref[...], h)
        t_saved = _chunks(t_ref[0], nb)
        _, vjp = jax.vjp(lambda *a: _gdn_prep_math(*a, t_saved=t_saved)[:6], _chunks(q_ref[...], nb), _chunks(k_ref[...], nb),
                         _chunks(v_ref[...], nb), _chunks(beta, nb), _chunks(g, nb))
        dq, dk, dv, db, dg = vjp(tuple(_chunks(c[0], nb) for c in (c0, c1, c2, c3, c4, c5)))
        flat = lambda a: a.reshape(a.shape[0] * a.shape[1], a.shape[2])
        dq_ref[...] = flat(dq)
        dk_ref[...] = flat(dk)
        dv_ref[...] = flat(dv)

        @pl.when(h == 0)
        def _():
            dbg_ref[...] = jnp.zeros_like(dbg_ref)
        lane = lax.broadcasted_iota(jnp.int32, dbg_ref.shape, 1)
        dbg_ref[...] += jnp.where(lane == h, flat(db), 0.0) + jnp.where(lane == h + GDN_HEADS, flat(dg), 0.0)

    return pl.pallas_call(
        body, name="gdn_prep_bwd", grid=(L // (nb * GDN_CHUNK), GDN_HEADS), in_specs=ins + [t_spec] + outs, out_specs=ins,
        out_shape=[SDS(q.shape, F32), SDS(k.shape, F32), SDS(v.shape, F32), SDS(bg.shape, F32)],
        compiler_params=_params(("parallel", "arbitrary"), VMEM_MID),
    )(q, k, v, bg, t, *cts)


@jax.custom_vjp
def gdn_prep(q, k, v, bg):
    return tuple(_gdn_prep_fwd_call(q, k, v, bg)[:6])


def _gdn_prep_f(q, k, v, bg):
    res = _gdn_prep_fwd_call(q, k, v, bg)
    return tuple(res[:6]), (q, k, v, bg, res[6])


def _gdn_prep_b(res, cts):
    return tuple(_gdn_prep_bwd_call(*res, tuple(cts)))


gdn_prep.defvjp(_gdn_prep_f, _gdn_prep_b)


def _gdn_step_math(q_dec, w, u, qk, k_dec, gc, z, nw, state):
    H, C = q_dec.shape[0], q_dec.shape[1]
    v_new = u - _bdot(w, state, BNN)
    o = _bdot(q_dec, state, BNN) + _bdot(qk, v_new, BNN)
    gl = gc[:, C - 1:C, :]
    new_state = jnp.exp(gl) * state + _bdot(k_dec, v_new, BTN)
    return _f_gdn_post(jnp.concatenate([o[h] for h in range(H)], axis=1), z, nw)[0], new_state


def _gdn_scan_specs(L, rev):
    C, H = GDN_CHUNK, GDN_HEADS
    nc = L // C
    cc = (lambda c: nc - 1 - c) if rev else (lambda c: c)
    ins = [pl.BlockSpec((H, C, GDN_DK), lambda c: (0, cc(c), 0)), pl.BlockSpec((H, C, GDN_DK), lambda c: (0, cc(c), 0)),
           pl.BlockSpec((H, C, GDN_DV), lambda c: (0, cc(c), 0)), pl.BlockSpec((H, C, C), lambda c: (0, cc(c), 0)),
           pl.BlockSpec((H, C, GDN_DK), lambda c: (0, cc(c), 0)), pl.BlockSpec((H, C, 1), lambda c: (0, cc(c), 0))]
    o_spec = pl.BlockSpec((C, H * GDN_DV), lambda c: (cc(c), 0))
    nw_spec = pl.BlockSpec((1, H * GDN_DV), lambda c: (0, 0))
    s_spec = pl.BlockSpec((1, H, GDN_DK, GDN_DV), lambda c: (cc(c), 0, 0, 0))
    return ins + [o_spec, nw_spec], o_spec, s_spec, nc


def _gdn_scan_fwd_call(q_dec, w, u, qk, k_dec, gc, z, nw):
    L = q_dec.shape[1]
    ins, o_spec, s_spec, nc = _gdn_scan_specs(L, False)

    def body(qd_ref, w_ref, u_ref, qk_ref, kd_ref, gc_ref, z_ref, nw_ref, o_ref, sin_ref, s_ref):
        c = pl.program_id(0)

        @pl.when(c == 0)
        def _():
            s_ref[...] = jnp.zeros_like(s_ref)
        st = s_ref[...]
        sin_ref[0] = st
        o, ns = _gdn_step_math(qd_ref[...], w_ref[...], u_ref[...], qk_ref[...], kd_ref[...], gc_ref[...], z_ref[...], nw_ref[...], st)
        o_ref[...] = o
        s_ref[...] = ns

    return pl.pallas_call(
        body, name="gdn_scan_fwd", grid=(nc,), in_specs=ins, out_specs=[o_spec, s_spec],
        out_shape=[SDS((L, GDN_HEADS * GDN_DV), F32), SDS((nc, GDN_HEADS, GDN_DK, GDN_DV), F32)],
        scratch_shapes=[pltpu.VMEM((GDN_HEADS, GDN_DK, GDN_DV), F32)],
        compiler_params=_params(("arbitrary",), VMEM_MID),
    )(q_dec, w, u, qk, k_dec, gc, z, nw)


def _gdn_scan_bwd_call(q_dec, w, u, qk, k_dec, gc, z, nw, s_in, do):
    L = q_dec.shape[1]
    ins, o_spec, s_spec, nc = _gdn_scan_specs(L, True)

    def body(qd_ref, w_ref, u_ref, qk_ref, kd_ref, gc_ref, z_ref, nw_ref, sin_ref, do_ref,
             dqd_ref, dw_ref, du_ref, dqk_ref, dkd_ref, dgc_ref, dz_ref, dnw_ref, ds_ref):
        c = pl.program_id(0)

        @pl.when(c == 0)
        def _():
            ds_ref[...] = jnp.zeros_like(ds_ref)
            dnw_ref[...] = jnp.zeros_like(dnw_ref)
        _, vjp = jax.vjp(_gdn_step_math, qd_ref[...], w_ref[...], u_ref[...], qk_ref[...], kd_ref[...], gc_ref[...],
                         z_ref[...], nw_ref[...], sin_ref[0])
        dqd, dw, du, dqk, dkd, dgc, dz, dnw, dst = vjp((do_ref[...], ds_ref[...]))
        dqd_ref[...] = dqd
        dw_ref[...] = dw
        du_ref[...] = du
        dqk_ref[...] = dqk
        dkd_ref[...] = dkd
        dgc_ref[...] = dgc
        dz_ref[...] = dz
        dnw_ref[...] += dnw
        ds_ref[...] = dst

    return pl.pallas_call(
        body, name="gdn_scan_bwd", grid=(nc,), in_specs=ins + [s_spec, o_spec], out_specs=ins,
        out_shape=[SDS(t.shape, F32) for t in (q_dec, w, u, qk, k_dec, gc, z, nw)],
        scratch_shapes=[pltpu.VMEM((GDN_HEADS, GDN_DK, GDN_DV), F32)],
        compiler_params=_params(("arbitrary",), VMEM_MID),
    )(q_dec, w, u, qk, k_dec, gc, z, nw, s_in, do)


@jax.custom_vjp
def gdn_scan(q_dec, w, u, qk, k_dec, gc, z, nw):
    return _gdn_scan_fwd_call(q_dec, w, u, qk, k_dec, gc, z, nw)[0]


def _gdn_scan_f(*args):
    o, s_in = _gdn_scan_fwd_call(*args)
    return o, (*args, s_in)


def _gdn_scan_b(res, do):
    return tuple(_gdn_scan_bwd_call(*res, do))


gdn_scan.defvjp(_gdn_scan_f, _gdn_scan_b)


def _silu(x):
    return x * jax.nn.sigmoid(x)


def _gelu_tanh(x):
    return 0.5 * x * (1.0 + jnp.tanh(math.sqrt(2.0 / math.pi) * (x + 0.044715 * (x * x * x))))


def _f_lnmod(x, nw, sc, sh, bsc, bsh):
    xn = x * lax.rsqrt(jnp.mean(x * x, axis=-1, keepdims=True) + NORM_EPS) * nw
    return (xn * (1.0 + (sc + bsc)) + (sh + bsh),)


def _f_s5_act(ys, u, d):
    return (_gelu_tanh(ys + d * u),)


def _f_s5_gate(y2, t, z):
    return (y2 * jax.nn.sigmoid(t) * _silu(z),)


def _f_res(x, y, gate, bgate):
    return (x + (gate + bgate) * y,)


def _heads(x, width, fn):
    return jnp.concatenate([fn(x[:, i * width:(i + 1) * width]) for i in range(x.shape[1] // width)], axis=1)


def _l2n(x):
    return x * lax.rsqrt(jnp.sum(x * x, axis=-1, keepdims=True) + NORM_EPS)


def _f_qnorm(x):
    return (_heads(_silu(x), GDN_DK, _l2n) * (GDN_DK ** -0.5),)


def _f_knorm(x):
    return (_heads(_silu(x), GDN_DK, _l2n),)


def _f_vact(x):
    return (_silu(x),)


def _f_betag(ba, alog, dtb):
    col = lax.broadcasted_iota(jnp.int32, ba.shape, 1)
    t = ba + dtb
    softplus = jnp.maximum(t, 0.0) + jnp.log1p(jnp.exp(-jnp.abs(t)))
    g = -jnp.exp(alog) * softplus
    return (jnp.where(col < GDN_HEADS, jax.nn.sigmoid(ba), jnp.where(col < 2 * GDN_HEADS, g, 0.0)),)


def _f_gdn_post(o, z, nw):
    on = _heads(o, GDN_DV, lambda t: t * lax.rsqrt(jnp.mean(t * t, axis=-1, keepdims=True) + NORM_EPS))
    return (on * nw * _silu(z),)


def _f_loss(x, tgt, fw):
    y = x * lax.rsqrt(jnp.mean(x * x, axis=-1, keepdims=True) + NORM_EPS) * fw
    err = y - tgt
    return (0.5 * jnp.mean(err * err, axis=-1, keepdims=True),)


def _ada_mod_call(c_all, ada_w):
    n = ada_w.shape[2]

    def body(c_ref, w_ref, o_ref):
        ca = _silu(c_ref[...])
        for l in range(ada_w.shape[0]):
            o_ref[l] = _bdot(ca, w_ref[l])

    return pl.pallas_call(body, name="ada_mod", out_shape=SDS((ada_w.shape[0], N_DEV, n), F32),
                          compiler_params=_params(None, VMEM_MID))(c_all, ada_w)


def _ada_grad_call(c_all, dmod):
    nl, _, n = dmod.shape

    def body(c_ref, d_ref, o_ref):
        ca = _silu(c_ref[...])
        for l in range(nl):
            o_ref[l] = _hdot(ca, d_ref[l], TN)

    return pl.pallas_call(body, name="ada_grad", out_shape=SDS((nl, c_all.shape[1], n), F32),
                          compiler_params=_params(None, VMEM_MID))(c_all, dmod)


ADAM_ROWS = 512


def _adamw(g, w, m, v):
    m2 = ADAM_B1 * m + (1.0 - ADAM_B1) * g
    v2 = ADAM_B2 * v + (1.0 - ADAM_B2) * (g * g)
    m_hat = m2 / (1.0 - ADAM_B1 ** ADAM_STEP)
    v_hat = v2 / (1.0 - ADAM_B2 ** ADAM_STEP)
    return g, -ADAM_LR * (m_hat / (jnp.sqrt(v_hat) + ADAM_EPS) + ADAM_WD * w), m2, v2


def _adam_call(gs, w, m, v, name, rows=None, by_cols=False):
    n, r, cols = gs.shape
    if by_cols:
        blk = pl.BlockSpec((r, LANES), lambda i: (0, i))
        g_blk, grid = pl.BlockSpec((n, r, LANES), lambda i: (0, 0, i)), (cols // LANES,)
    else:
        rows = rows or ADAM_ROWS
        blk = pl.BlockSpec((rows, cols), lambda i: (i, 0))
        g_blk, grid = pl.BlockSpec((n, rows, cols), lambda i: (0, i, 0)), (r // rows,)

    def body(g_ref, w_ref, m_ref, v_ref, go_ref, d_ref, mo_ref, vo_ref):
        g = g_ref[0].astype(F32)
        for s in range(1, n):
            g = g + g_ref[s].astype(F32)
        for o_ref, val in zip((go_ref, d_ref, mo_ref, vo_ref), _adamw(g, w_ref[...], m_ref[...], v_ref[...])):
            o_ref[...] = val

    return pl.pallas_call(
        body, name=name, grid=grid, in_specs=[g_blk, blk, blk, blk],
        out_specs=[blk, blk, blk, blk], out_shape=[SDS((r, cols), F32)] * 4,
        compiler_params=_params(("parallel",), VMEM_MID),
    )(gs, w, m, v)


def _sum_call(gs, name, rows):
    n, r, _ = gs.shape

    def body(g_ref, o_ref):
        g = g_ref[0].astype(F32)
        for s in range(1, n):
            g = g + g_ref[s].astype(F32)
        o_ref[...] = g

    return pl.pallas_call(
        body, name=name, grid=(r // rows,),
        in_specs=[pl.BlockSpec((n, rows, LANES), lambda i: (0, i, 0))],
        out_specs=pl.BlockSpec((rows, LANES), lambda i: (i, 0)), out_shape=SDS((r, LANES), F32),
        compiler_params=_params(("parallel",), VMEM_MID),
    )(gs)


def _allgather_call(x_shard, name, in_hbm):
    m_per, n = x_shard.shape

    def body(x_ref, out_ref, send_sems, recv_sems, local_sem):
        x, y, c = lax.axis_index("x"), lax.axis_index("y"), lax.axis_index("c")
        me, sibling = (x, y, c), (x, y, 1 - c)
        chips = [(1 - x, y), (x, 1 - y), (1 - x, 1 - y)]

        def rows(px, py, pc):
            return out_ref.at[pl.ds((4 * px + 2 * py + pc) * m_per, m_per), :]

        def copy(k, block, to, src=None):
            return pltpu.make_async_remote_copy(
                src_ref=rows(*block) if src is None else src, dst_ref=rows(*block),
                send_sem=send_sems.at[k], recv_sem=recv_sems.at[k], device_id=to, device_id_type=pl.DeviceIdType.MESH)

        mine = pltpu.make_async_copy(x_ref, rows(*me), local_sem)
        mine.start()
        first = [copy(0, me, sibling, src=x_ref)]
        first += [copy(1 + j, me, (*chip, c), src=x_ref) for j, chip in enumerate(chips)]
        for cp in first:
            cp.start()
        passed = [copy(4 + j, (*chip, c), sibling) for j, chip in enumerate(chips)]
        for j, chip in enumerate(chips):
            copy(1 + j, (*chip, c), me).wait_recv()
            passed[j].start()
        copy(0, sibling, me).wait_recv()
        for j, chip in enumerate(chips):
            copy(4 + j, (*chip, 1 - c), me).wait_recv()
        for cp in first + passed:
            cp.wait_send()
        mine.wait()

    space = pl.ANY if in_hbm else pltpu.VMEM
    return pl.pallas_call(
        body, name=name, out_shape=SDS((N_DEV * m_per, n), x_shard.dtype),
        in_specs=[pl.BlockSpec(memory_space=space)], out_specs=pl.BlockSpec(memory_space=space),
        scratch_shapes=[pltpu.SemaphoreType.DMA((7,)), pltpu.SemaphoreType.DMA((7,)), pltpu.SemaphoreType.DMA],
        compiler_params=_params(None, None if in_hbm else VMEM_BIG),
    )(x_shard)


def _gather_weights_call(shards, name):
    nw = len(shards)

    def body(*refs):
        x_refs, out_refs = refs[:nw], refs[nw:2 * nw]
        send_sems, recv_sems, local_sems = refs[2 * nw:]
        x, y, c = lax.axis_index("x"), lax.axis_index("y"), lax.axis_index("c")
        me, sibling = (x, y, c), (x, y, 1 - c)
        chips = [(1 - x, y), (x, 1 - y), (1 - x, 1 - y)]

        def slot(w, px, py, pc):
            return out_refs[w].at[4 * px + 2 * py + pc]

        def copy(w, k, block, to, src=None):
            dst = slot(w, *block)
            return pltpu.make_async_remote_copy(
                src_ref=dst if src is None else src, dst_ref=dst, send_sem=send_sems.at[7 * w + k],
                recv_sem=recv_sems.at[7 * w + k], device_id=to, device_id_type=pl.DeviceIdType.MESH)

        mines = [pltpu.make_async_copy(x_refs[w], slot(w, *me), local_sems.at[w]) for w in range(nw)]
        for cp in mines:
            cp.start()
        first = [copy(w, 0, me, sibling, src=x_refs[w]) for w in range(nw)]
        first += [copy(w, 1 + j, me, (*chip, c), src=x_refs[w]) for w in range(nw) for j, chip in enumerate(chips)]
        for cp in first:
            cp.start()
        passed = []
        for w in range(nw):
            for j, chip in enumerate(chips):
                copy(w, 1 + j, (*chip, c), me).wait_recv()
                fwd = copy(w, 4 + j, (*chip, c), sibling)
                fwd.start()
                passed.append(fwd)
        for w in range(nw):
            copy(w, 0, sibling, me).wait_recv()
            for j, chip in enumerate(chips):
                copy(w, 4 + j, (*chip, 1 - c), me).wait_recv()
        for cp in first + passed:
            cp.wait_send()
        for cp in mines:
            cp.wait()

    hbm = pl.BlockSpec(memory_space=pl.ANY)
    return pl.pallas_call(
        body, name=name, out_shape=[SDS((N_DEV,) + s.shape, s.dtype) for s in shards],
        in_specs=[hbm] * nw, out_specs=[hbm] * nw,
        scratch_shapes=[pltpu.SemaphoreType.DMA((7 * nw,)), pltpu.SemaphoreType.DMA((7 * nw,)), pltpu.SemaphoreType.DMA((nw,))],
    )(*shards)


def _pair_exchange_call(grads, name):
    nw = len(grads)

    def body(*refs):
        g_refs, got_refs = refs[:nw], refs[nw:2 * nw]
        send_sems, recv_sems = refs[2 * nw:]
        x, y, c = lax.axis_index("x"), lax.axis_index("y"), lax.axis_index("c")
        copies = []
        for w in range(nw):
            for j in range(4):
                give = pltpu.make_async_remote_copy(
                    src_ref=g_refs[w].at[2 * j + 1 - c], dst_ref=got_refs[w].at[j], send_sem=send_sems.at[4 * w + j],
                    recv_sem=recv_sems.at[4 * w + j], device_id=(x, y, 1 - c), device_id_type=pl.DeviceIdType.MESH)
                give.start()
                copies.append(give)
        for cp in copies:
            cp.wait()

    hbm = pl.BlockSpec(memory_space=pl.ANY)
    return pl.pallas_call(
        body, name=name, out_shape=[SDS((4,) + g.shape[1:], g.dtype) for g in grads], in_specs=[hbm] * nw, out_specs=[hbm] * nw,
        scratch_shapes=[pltpu.SemaphoreType.DMA((4 * nw,)), pltpu.SemaphoreType.DMA((4 * nw,))],
    )(*grads)


def _chip_exchange_call(parts, name):
    nw = len(parts)

    def body(*refs):
        p_refs, out_refs = refs[:nw], refs[nw:2 * nw]
        send_sems, recv_sems = refs[2 * nw:]
        x, y, c = lax.axis_index("x"), lax.axis_index("y"), lax.axis_index("c")
        chips = [(1 - x, y), (x, 1 - y), (1 - x, 1 - y)]
        copies = []
        for w in range(nw):
            for j, (px, py) in enumerate(chips):
                give = pltpu.make_async_remote_copy(
                    src_ref=p_refs[w].at[2 * px + py], dst_ref=out_refs[w].at[j], send_sem=send_sems.at[3 * w + j],
                    recv_sem=recv_sems.at[3 * w + j], device_id=(px, py, c), device_id_type=pl.DeviceIdType.MESH)
                give.start()
                copies.append(give)
        for cp in copies:
            cp.wait()

    hbm = pl.BlockSpec(memory_space=pl.ANY)
    return pl.pallas_call(
        body, name=name, out_shape=[SDS((3,) + p.shape[1:], p.dtype) for p in parts], in_specs=[hbm] * nw, out_specs=[hbm] * nw,
        scratch_shapes=[pltpu.SemaphoreType.DMA((3 * nw,)), pltpu.SemaphoreType.DMA((3 * nw,))],
    )(*parts)


_HBM = pl.BlockSpec(memory_space=pltpu.HBM)
_SEM = pl.BlockSpec(memory_space=pltpu.SEMAPHORE)
_DATAFLOW = pltpu.SideEffectType.DATAFLOW_SIDE_EFFECTING


def _spread_start_call(srcs, per_peer, name, after):
    nw = len(srcs)
    lands = [lax.empty((N_DEV,) + (s.shape[1:] if per_peer else s.shape), s.dtype) for s in srcs]

    def body(*refs):
        src_refs, land_refs = refs[:nw], refs[nw:2 * nw]
        send_sems, recv_sems, token = refs[2 * nw + 1], refs[2 * nw + 2], refs[-1]
        x, y, c = lax.axis_index("x"), lax.axis_index("y"), lax.axis_index("c")
        me = 4 * x + 2 * y + c
        for w in range(nw):
            for k in range(1, N_DEV):
                px = 1 - x if k & 4 else x
                py = 1 - y if k & 2 else y
                pc = 1 - c if k & 1 else c
                src = src_refs[w].at[4 * px + 2 * py + pc] if per_peer else src_refs[w]
                pltpu.make_async_remote_copy(
                    src_ref=src, dst_ref=land_refs[w].at[me], send_sem=send_sems.at[w], recv_sem=recv_sems.at[w],
                    device_id=(px, py, pc), device_id_type=pl.DeviceIdType.MESH).start()
        token[...] = jnp.zeros_like(token)

    hbm = lambda a: pltpu.with_memory_space_constraint(a, pltpu.HBM)
    res = pl.pallas_call(
        body, name=name,
        out_shape=(pltpu.SemaphoreType.DMA((nw,)), pltpu.SemaphoreType.DMA((nw,)))
        + tuple(pltpu.HBM(s.shape, s.dtype) for s in srcs) + tuple(pltpu.HBM(l.shape, l.dtype) for l in lands)
        + (SDS((SUBLANES, LANES), F32),),
        in_specs=[_HBM] * (2 * nw) + [pl.BlockSpec(memory_space=pl.ANY)],
        out_specs=(_SEM, _SEM) + (_HBM,) * (2 * nw) + (pl.BlockSpec(memory_space=pltpu.VMEM),),
        input_output_aliases={i: i + 2 for i in range(2 * nw)},
        compiler_params=pltpu.CompilerParams(has_side_effects=_DATAFLOW),
    )(*[hbm(s) for s in srcs], *[hbm(l) for l in lands], after)
    return res[0], res[1], res[2:2 + nw], res[2 + nw:2 + 2 * nw], res[-1]


def _spread_wait_call(send_sems, recv_sems, srcs, lands, after, name):
    nw = len(lands)

    def body(*refs):
        land_refs = refs[nw:2 * nw]
        s_sems, r_sems = refs[2 * nw], refs[2 * nw + 1]
        x, y, c = lax.axis_index("x"), lax.axis_index("y"), lax.axis_index("c")
        for w in range(nw):
            seven = land_refs[w].at[pl.ds(0, N_DEV - 1)]
            all_seven = pltpu.make_async_remote_copy(
                src_ref=seven, dst_ref=seven, send_sem=s_sems.at[w], recv_sem=r_sems.at[w],
                device_id=(x, y, c), device_id_type=pl.DeviceIdType.MESH)
            all_seven.wait_send()
            all_seven.wait_recv()

    res = pl.pallas_call(
        body, name=name,
        out_shape=tuple(pltpu.HBM(s.shape, s.dtype) for s in srcs) + tuple(pltpu.HBM(l.shape, l.dtype) for l in lands),
        in_specs=[_HBM] * (2 * nw) + [_SEM, _SEM, pl.BlockSpec(memory_space=pl.ANY)], out_specs=(_HBM,) * (2 * nw),
        input_output_aliases={i: i for i in range(2 * nw)},
        compiler_params=pltpu.CompilerParams(has_side_effects=_DATAFLOW),
    )(*srcs, *lands, send_sems, recv_sems, after)
    return res[:nw], res[nw:]


def _pair_sum_call(g, got, core, name):
    _, k, n = got.shape
    tr = _tile(k, 256)

    def body(c_ref, g_ref, got_ref, o_ref):
        o_ref[...] = (g_ref[...] + got_ref[...]).astype(o_ref.dtype)

    spec = pltpu.PrefetchScalarGridSpec(
        num_scalar_prefetch=1, grid=(4, k // tr),
        in_specs=[pl.BlockSpec((1, tr, n), lambda j, i, c: (2 * j + c[0], i, 0)), pl.BlockSpec((1, tr, n), lambda j, i, c: (j, i, 0))],
        out_specs=pl.BlockSpec((1, tr, n), lambda j, i, c: (j, i, 0)))
    return pl.pallas_call(body, name=name, grid_spec=spec, out_shape=SDS(got.shape, BF16),
                          compiler_params=_params(("parallel", "parallel"), VMEM_MID))(core, g, got)


def _adam_own_call(pair, chip, recv, w, m, v, name, rows):
    _, r, cols = recv.shape

    def body(chip_ref, p_ref, g_ref, w_ref, m_ref, v_ref, go_ref, d_ref, mo_ref, vo_ref):
        g = ((p_ref[0].astype(F32) + g_ref[0].astype(F32)) + g_ref[1].astype(F32)) + g_ref[2].astype(F32)
        for o_ref, val in zip((go_ref, d_ref, mo_ref, vo_ref), _adamw(g, w_ref[...], m_ref[...], v_ref[...])):
            o_ref[...] = val

    blk = pl.BlockSpec((rows, cols), lambda i, s: (i, 0))
    spec = pltpu.PrefetchScalarGridSpec(
        num_scalar_prefetch=1, grid=(r // rows,),
        in_specs=[pl.BlockSpec((1, rows, cols), lambda i, s: (s[0], i, 0)), pl.BlockSpec((3, rows, cols), lambda i, s: (0, i, 0)),
                  blk, blk, blk],
        out_specs=[blk, blk, blk, blk])
    return pl.pallas_call(body, name=name, grid_spec=spec, out_shape=[SDS((r, cols), F32)] * 4,
                          compiler_params=_params(("parallel",), VMEM_MID))(chip, pair, recv, w, m, v)


def _join_cols_call(w8, name):
    _, k, n = w8.shape
    tk = _tile(k, 256)

    def body(w_ref, o_ref):
        for s in range(N_DEV):
            o_ref[:, n * s:n * (s + 1)] = w_ref[s]

    return pl.pallas_call(body, name=name, grid=(k // tk,), in_specs=[pl.BlockSpec((N_DEV, tk, n), lambda i: (0, i, 0))],
                          out_specs=pl.BlockSpec((tk, N_DEV * n), lambda i: (i, 0)), out_shape=SDS((k, N_DEV * n), w8.dtype),
                          compiler_params=_params(("parallel",), VMEM_MID))(w8)


def _split_cols_call(g, name, dtype):
    k, n8 = g.shape
    n = n8 // N_DEV
    tk = _tile(k, 256)

    def body(g_ref, o_ref):
        for s in range(N_DEV):
            o_ref[s] = g_ref[:, n * s:n * (s + 1)].astype(dtype)

    return pl.pallas_call(body, name=name, grid=(k // tk,), in_specs=[pl.BlockSpec((tk, n8), lambda i: (i, 0))],
                          out_specs=pl.BlockSpec((N_DEV, tk, n), lambda i: (0, i, 0)), out_shape=SDS((N_DEV, k, n), dtype),
                          compiler_params=_params(("parallel",), VMEM_MID))(g)


def _pack(parts, rows_multiple):
    flat = jnp.concatenate([p.reshape(-1) for p in parts])
    unit = rows_multiple * LANES
    padded = -(-flat.shape[0] // unit) * unit
    flat = jnp.concatenate([flat, jnp.zeros((padded - flat.shape[0],), F32)])
    return flat.reshape(-1, LANES)


def _groups_last(a):
    x, y = a.shape[-2:]
    return jnp.transpose(a.reshape(S5_GROUPS, x, y), (1, 2, 0)).reshape(x * y, S5_GROUPS)


def _groups_first(a, shape):
    x, y = shape[-2:]
    return jnp.transpose(a.reshape(x, y, S5_GROUPS), (2, 0, 1)).reshape(shape)


def _unpack(buf, shapes):
    flat = buf.reshape(-1)
    out, off = [], 0
    for s in shapes:
        n = math.prod(s)
        out.append(flat[off:off + n].reshape(s))
        off += n
    return out


def _row_tile(L):
    return 256 if L % 256 == 0 else L


def _layer0_mix(diff, const):
    x, mod, norm_w, lam_re, lam_im, log_dt, b_re, b_im, c_re, c_im, s5_d, *slots = diff
    ada_b, weights = const
    L = x.shape[0]
    tm = _row_tile(L)
    mods = mod.reshape(2, 1, D_MODEL)
    biases = ada_b.reshape(2, 1, D_MODEL)
    op_ln0 = make_rowwise(_f_lnmod, "ln0", tm, 1, 5, pass_first=True)
    h, x = op_ln0((x,), (norm_w.reshape(1, D_MODEL), mods[1], mods[0], biases[1], biases[0]))
    u, z = make_proj("s5_in")(h, tuple(weights), tuple(slots))
    blocks = _s5_block_params(lam_re, lam_im, log_dt, b_re, b_im, c_re, c_im)
    y2 = make_s5_core(min(S5_TL, L))(u, *blocks, s5_d.reshape(1, D_INNER))
    return x, y2, z


def _layer0_out(diff, weights):
    y2, z, *slots = diff
    tm = _row_tile(y2.shape[0])
    t, y2 = make_mm("s5_glu", pass_input=True)(y2, weights[0], slots[0])
    (y4,) = make_rowwise(_f_s5_gate, "s5_gate", tm, 3, 0)((y2, t, z), ())
    return make_mm("s5_out")(y4, weights[1], slots[1])


def _f_res_lnmod(x, o, gate, bgate, nw, sc, sh, bsc, bsh):
    (x1,) = _f_res(x, o, gate, bgate)
    return _f_lnmod(x1, nw, sc, sh, bsc, bsh) + (x1,)


def _f_res_loss(x, y, tgt, gate, bgate, fw):
    return _f_loss(_f_res(x, y, gate, bgate)[0], tgt, fw)


def _layer1_loss(diff, const):
    x, o, gate0, mod, norm_w, conv_w, a_log, dt_bias, gdn_nw, final_nw, *slots = diff
    tgt, bgate0, ada_b, weights = const
    L = x.shape[0]
    tm = _row_tile(L)
    mods = mod.reshape(3, 1, D_MODEL)
    biases = ada_b.reshape(3, 1, D_MODEL)
    h, x1 = make_rowwise(_f_res_lnmod, "res0_ln1", tm, 2, 7)(
        (x, o), (gate0.reshape(1, D_MODEL), bgate0.reshape(1, D_MODEL), norm_w.reshape(1, D_MODEL), mods[1], mods[0], biases[1], biases[0]))
    q0, k0, v0, gz, ba = make_proj("gdn_in", w_rows=True)(h, tuple(weights[0:5]), tuple(slots[0:5]))
    cw = jnp.concatenate([conv_w, jnp.zeros((SUBLANES - GDN_CONV, GDN_CONV_CH), F32)], axis=0)
    q = make_conv_act(lambda t: _l2n(_silu(t)) * (GDN_DK ** -0.5), "gdn_conv_q")(q0, cw[:, :GDN_QK])
    k = make_conv_act(lambda t: _l2n(_silu(t)), "gdn_conv_k")(k0, cw[:, GDN_QK:2 * GDN_QK])
    v = make_conv_act(_silu, "gdn_conv_v")(v0, cw[:, 2 * GDN_QK:])
    pad = jnp.zeros((LANES - 2 * GDN_HEADS,), F32)
    alog_row = jnp.concatenate([jnp.zeros((GDN_HEADS,), F32), a_log, pad]).reshape(1, LANES)
    dtb_row = jnp.concatenate([jnp.zeros((GDN_HEADS,), F32), dt_bias, pad]).reshape(1, LANES)
    (bg,) = make_rowwise(_f_betag, "gdn_bg", tm, 1, 2)((ba,), (alog_row, dtb_row))
    nw_row = jnp.tile(gdn_nw, GDN_HEADS).reshape(1, D_INNER)
    on = gdn_scan(*gdn_prep(q, k, v, bg), gz, nw_row)
    y = make_mm("gdn_out")(on, weights[5], slots[5])
    (lt,) = make_rowwise(_f_res_loss, "res1_loss", tm, 3, 3)((x1, y, tgt), (mods[2], biases[2], final_nw.reshape(1, D_MODEL)))
    return jnp.sum(lt)


VEC_NAMES = ("ada_b", "norm_w", "s5_lambda_re", "s5_lambda_im", "s5_log_dt", "s5_d", "gdn_a_log", "gdn_dt_bias", "final_norm_w")
MAT_NAMES = ("s5_b_re", "s5_b_im", "s5_c_re", "s5_c_im")
S5_BIG = ("s5_w_in", "s5_w_glu", "s5_w_out")
GDN_BIG = ("gdn_w_in", "gdn_w_out")
BIG_NAMES = S5_BIG + GDN_BIG
WEIGHT_ORDER = ("ada_w", "ada_b", "norm_w", "s5_w_in", "s5_lambda_re", "s5_lambda_im", "s5_log_dt", "s5_b_re", "s5_b_im",
                "s5_c_re", "s5_c_im", "s5_d", "s5_w_glu", "s5_w_out", "gdn_w_in", "gdn_conv_w", "gdn_a_log", "gdn_dt_bias",
                "gdn_norm_w", "gdn_w_out", "final_norm_w")


def _step(x, c, W, M, V, tgt):
    L = x.shape[1]
    ix, iy, ic = lax.axis_index("x"), lax.axis_index("y"), lax.axis_index("c")
    me = 4 * ix + 2 * iy + ic
    n_ada = W["ada_w"].shape[2]
    n_conv = W["gdn_conv_w"].shape[2]
    n_gnw = W["gdn_norm_w"].shape[1]

    g1 = _allgather_call(_pack([c, W["gdn_conv_w"], W["gdn_norm_w"]], SUBLANES), "gather_small_in", False)
    g1 = g1.reshape(N_DEV, -1)
    c_all = g1[:, :D_MODEL]
    conv_w = g1[:, D_MODEL:D_MODEL + GDN_CONV * n_conv].reshape(N_DEV, GDN_CONV, n_conv).transpose(1, 0, 2).reshape(GDN_CONV, -1)
    gdn_nw = g1[:, D_MODEL + GDN_CONV * n_conv:D_MODEL + GDN_CONV * n_conv + n_gnw].reshape(-1)
    mod_part = _ada_mod_call(c_all, W["ada_w"])
    g2 = _allgather_call(_pack([mod_part], SUBLANES), "gather_mod", False).reshape(N_DEV, -1)
    mod_all = g2[:, :2 * N_DEV * n_ada].reshape(N_DEV, 2, N_DEV, n_ada)
    mod_raw = lax.dynamic_index_in_dim(mod_all, me, axis=2, keepdims=False)
    mod_raw = mod_raw.transpose(1, 0, 2).reshape(2, 3 * D_MODEL)

    shard = lambda n: W[n][0].astype(BF16)
    (w_in5_parts,) = _gather_weights_call([shard("s5_w_in")], "gather_s5_w_in")
    late = _spread_start_call([shard("s5_w_glu"), shard("s5_w_out")], False, "gather_s5_late_start", w_in5_parts)
    turned = lambda a: jnp.transpose(a[0])
    g_send, g_recv, g_srcs, g_lands, g_token = _spread_start_call(
        [turned(W["gdn_w_in"]).astype(BF16), shard("gdn_w_out")], False, "gather_gdn_start", late[4])
    w_in5 = _join_cols_call(w_in5_parts, "join_s5_w_in")
    slot = lambda *s: jnp.zeros(s, F32)
    two = 2 * D_MODEL
    diff_mix = (x[0], mod_raw[0, :two] + g_token[0, 0], W["norm_w"][0], W["s5_lambda_re"][0], W["s5_lambda_im"][0], W["s5_log_dt"][0],
                W["s5_b_re"][0], W["s5_b_im"][0], W["s5_c_re"][0], W["s5_c_im"][0], W["s5_d"][0],
                slot(D_MODEL, D_INNER), slot(D_MODEL, D_INNER))

    (xp, y2, z5), vjp_mix = jax.vjp(lambda d: _layer0_mix(d, (W["ada_b"][0, :two], (w_in5[:, :D_INNER], w_in5[:, D_INNER:]))), diff_mix)
    l_srcs, l_lands = _spread_wait_call(late[0], late[1], late[2], late[3], y2, "gather_s5_late_wait")
    w_glu, w_o5 = [lax.dynamic_update_slice(land, src[None], (me, 0, 0)).reshape(-1, src.shape[1]) for land, src in zip(l_lands, l_srcs)]
    diff_out = (y2, z5, slot(D_INNER, D_INNER), slot(D_INNER, D_MODEL))
    o5, vjp_out = jax.vjp(lambda d: _layer0_out(d, (w_glu, w_o5)), diff_out)
    g_srcs, g_lands = _spread_wait_call(g_send, g_recv, g_srcs, g_lands, o5, "gather_gdn_wait")
    gdn_full = [lax.dynamic_update_slice(land, src[None], (me, 0, 0)) for land, src in zip(g_lands, g_srcs)]
    w_ing = gdn_full[0].reshape(GDN_PROJ, D_MODEL)
    w_ba = jnp.concatenate([w_ing[GDN_CONV_CH + D_INNER:], jnp.zeros((LANES - 2 * GDN_HEADS, D_MODEL), BF16)], axis=0)
    weights1 = (w_ing[:GDN_QK], w_ing[GDN_QK:2 * GDN_QK], w_ing[2 * GDN_QK:GDN_CONV_CH],
                w_ing[GDN_CONV_CH:GDN_CONV_CH + D_INNER], w_ba, gdn_full[1].reshape(D_INNER, D_MODEL))
    slots1 = tuple(jnp.zeros(w.shape, F32) for w in weights1)
    diff1 = (xp, o5, mod_raw[0, two:], mod_raw[1], W["norm_w"][1], conv_w, W["gdn_a_log"][0], W["gdn_dt_bias"][0], gdn_nw,
             W["final_norm_w"], *slots1)
    loss_local, vjp1 = jax.vjp(lambda d: _layer1_loss(d, (tgt[0], W["ada_b"][0, two:], W["ada_b"][1], weights1)), diff1)
    ((dxp, do5, dmod_gate, dmod1, d_norm_w1, d_conv, d_alog, d_dtb, d_gnw, d_fnw, d_wq, d_wk, d_wv, d_wgz, d_wba, d_wog),) = vjp1(
        jnp.ones((), F32))
    loss = lax.psum(loss_local, MESH_AXES)

    rows = lambda d: d.reshape(N_DEV, d.shape[0] // N_DEV, d.shape[1])
    d_ing = jnp.concatenate([d_wq, d_wk, d_wv, d_wgz, d_wba[:2 * GDN_HEADS]], axis=0).astype(BF16).reshape(N_DEV, -1, D_MODEL)
    s_send, s_recv, s_srcs, s_lands, s_token = _spread_start_call([d_ing, rows(d_wog).astype(BF16)], True, "scatter_gdn_start", dxp)
    ((dy2, dz5, d_wglu, d_wo5),) = vjp_out(do5.at[0, 0].add(s_token[0, 0]))
    t_send, t_recv, t_srcs, t_lands, t_token = _spread_start_call(
        [rows(d_wglu).astype(BF16), rows(d_wo5).astype(BF16)], True, "scatter_s5_late_start", dy2)
    ((dx, dmod_ss, d_norm_w0, d_lre, d_lim, d_logdt, d_bre, d_bim, d_cre, d_cim, d_s5d, d_wu, d_wz),) = vjp_mix(
        (dxp.at[0, 0].add(t_token[0, 0]), dy2, dz5))
    dmod = jnp.stack([jnp.concatenate([dmod_ss, dmod_gate]), dmod1])
    d_norm_w = jnp.stack([d_norm_w0, d_norm_w1])
    vec_parts = [dmod, d_norm_w, d_lre, d_lim, d_logdt, d_s5d, d_alog, d_dtb, d_fnw]
    tail_parts = [d_conv, d_gnw]
    mat_parts = [_groups_last(d) for d in (d_bre, d_bim, d_cre, d_cim)]
    n_vec = sum(math.prod(p.shape) for p in vec_parts)
    m_send, m_recv, m_srcs, m_lands, m_token = _spread_start_call(
        [_pack(vec_parts + tail_parts, ADAM_ROWS), _pack(mat_parts, SUBLANES).astype(BF16)], False, "gather_small_grads_start", dx)
    d_in5 = _split_cols_call(jnp.concatenate([d_wu.at[0, 0].add(m_token[0, 0]), d_wz], axis=1), "split_s5_w_in", F32)
    (got,) = _pair_exchange_call([d_in5], "scatter_s5_pair")
    core = jnp.reshape(ic, (1,)).astype(jnp.int32)
    chip = jnp.reshape(2 * ix + iy, (1,)).astype(jnp.int32)
    pair = _pair_sum_call(d_in5, got, core, "pair_sum_s5_w_in")
    (recv,) = _chip_exchange_call([pair], "scatter_s5_chips")
    big = {"s5_w_in": _adam_own_call(pair, chip, recv, W["s5_w_in"][0], M["s5_w_in"][0], V["s5_w_in"][0], "adam_s5_w_in", 128)}
    t_srcs, t_lands = _spread_wait_call(t_send, t_recv, t_srcs, t_lands, dx, "scatter_s5_late_wait")
    s_srcs, s_lands = _spread_wait_call(s_send, s_recv, s_srcs, s_lands, t_lands[0], "scatter_gdn_wait")
    for land, src, n in zip(tuple(t_lands) + tuple(s_lands), tuple(t_srcs) + tuple(s_srcs), ("s5_w_glu", "s5_w_out") + GDN_BIG):
        mine = lax.dynamic_index_in_dim(src, me, 0, keepdims=True)
        parts = lax.dynamic_update_slice(land, mine, (me, 0, 0))
        if n == "gdn_w_in":
            outs = _adam_call(parts, turned(W[n]), turned(M[n]), turned(V[n]), "adam_" + n, by_cols=True)
            big[n] = [jnp.transpose(o) for o in outs]
        else:
            big[n] = _adam_call(parts, W[n][0], M[n][0], V[n][0], "adam_" + n, rows=_tile(W[n].shape[1], 128))
    big = [[o[None] for o in big[n]] for n in BIG_NAMES]

    m_srcs, m_lands = _spread_wait_call(m_send, m_recv, m_srcs, m_lands, big[0][0], "gather_small_grads_wait")
    sg_vec, sg_mat = [lax.dynamic_update_slice(land, src[None], (me, 0, 0)) for land, src in zip(m_lands, m_srcs)]
    tot_vec = _sum_call(sg_vec, "sum_vec_grads", ADAM_ROWS)
    tot_mat = _sum_call(sg_mat, "sum_mat_grads", ADAM_ROWS)
    g_conv, g_gnw = _unpack(tot_vec.reshape(-1)[n_vec:], [d_conv.shape, d_gnw.shape])
    g_conv_mine = lax.dynamic_slice_in_dim(g_conv, me * n_conv, n_conv, axis=1)
    g_gnw_mine = lax.dynamic_slice_in_dim(g_gnw, me * n_gnw, n_gnw, axis=0)
    vec_names = VEC_NAMES + ("gdn_conv_w", "gdn_norm_w")
    vec_g = _pack([tot_vec.reshape(-1)[:n_vec], g_conv_mine, g_gnw_mine], ADAM_ROWS)
    vec = _adam_call(vec_g[None], _pack([W[n] for n in vec_names], ADAM_ROWS), _pack([M[n] for n in vec_names], ADAM_ROWS),
                     _pack([V[n] for n in vec_names], ADAM_ROWS), "adam_vec")
    vec = [_unpack(b, [W[n].shape for n in vec_names]) for b in vec]
    mats = []
    for name, g_mat in zip(MAT_NAMES, _unpack(tot_mat, [p.shape for p in mat_parts])):
        outs = _adam_call(g_mat[None], _groups_last(W[name]), _groups_last(M[name]), _groups_last(V[name]), "adam_" + name)
        mats.append([_groups_first(o, W[name].shape) for o in outs])

    dmod_all = sg_vec[:, :2 * 3 * D_MODEL // LANES].reshape(N_DEV, 2, N_DEV, n_ada // LANES, LANES)
    dmod_mine = lax.dynamic_index_in_dim(dmod_all, me, axis=2, keepdims=False).transpose(1, 0, 2, 3).reshape(2, N_DEV, n_ada)
    g_ada_w = _ada_grad_call(c_all, dmod_mine)
    ada = _adam_call(g_ada_w.reshape(1, -1, LANES), W["ada_w"].reshape(-1, LANES), M["ada_w"].reshape(-1, LANES),
                     V["ada_w"].reshape(-1, LANES), "adam_ada")
    ada = [a.reshape(W["ada_w"].shape) for a in ada]

    res = {}
    for i, n in enumerate(BIG_NAMES):
        res[n] = big[i]
    for i, n in enumerate(vec_names):
        res[n] = [b[i] for b in vec]
    for i, n in enumerate(MAT_NAMES):
        res[n] = mats[i]
    res["ada_w"] = ada
    outs = [loss, dx[None]]
    for j in range(4):
        outs += [res[n][j] for n in WEIGHT_ORDER]
    return tuple(outs)


def kernel(x, c, ada_w, ada_b, norm_w, s5_w_in, s5_lambda_re, s5_lambda_im, s5_log_dt, s5_b_re, s5_b_im, s5_c_re, s5_c_im, s5_d, s5_w_glu, s5_w_out, gdn_w_in, gdn_conv_w, gdn_a_log, gdn_dt_bias, gdn_norm_w, gdn_w_out, final_norm_w, loss_target, m_ada_w, m_ada_b, m_norm_w, m_s5_w_in, m_s5_lambda_re, m_s5_lambda_im, m_s5_log_dt, m_s5_b_re, m_s5_b_im, m_s5_c_re, m_s5_c_im, m_s5_d, m_s5_w_glu, m_s5_w_out, m_gdn_w_in, m_gdn_conv_w, m_gdn_a_log, m_gdn_dt_bias, m_gdn_norm_w, m_gdn_w_out, m_final_norm_w, v_ada_w, v_ada_b, v_norm_w, v_s5_w_in, v_s5_lambda_re, v_s5_lambda_im, v_s5_log_dt, v_s5_b_re, v_s5_b_im, v_s5_c_re, v_s5_c_im, v_s5_d, v_s5_w_glu, v_s5_w_out, v_gdn_w_in, v_gdn_conv_w, v_gdn_a_log, v_gdn_dt_bias, v_gdn_norm_w, v_gdn_w_out, v_final_norm_w):
    W = dict(ada_w=ada_w, ada_b=ada_b, norm_w=norm_w, s5_w_in=s5_w_in, s5_lambda_re=s5_lambda_re, s5_lambda_im=s5_lambda_im,
             s5_log_dt=s5_log_dt, s5_b_re=s5_b_re, s5_b_im=s5_b_im, s5_c_re=s5_c_re, s5_c_im=s5_c_im, s5_d=s5_d,
             s5_w_glu=s5_w_glu, s5_w_out=s5_w_out, gdn_w_in=gdn_w_in, gdn_conv_w=gdn_conv_w, gdn_a_log=gdn_a_log,
             gdn_dt_bias=gdn_dt_bias, gdn_norm_w=gdn_norm_w, gdn_w_out=gdn_w_out, final_norm_w=final_norm_w)
    M = dict(ada_w=m_ada_w, ada_b=m_ada_b, norm_w=m_norm_w, s5_w_in=m_s5_w_in, s5_lambda_re=m_s5_lambda_re,
             s5_lambda_im=m_s5_lambda_im, s5_log_dt=m_s5_log_dt, s5_b_re=m_s5_b_re, s5_b_im=m_s5_b_im, s5_c_re=m_s5_c_re,
             s5_c_im=m_s5_c_im, s5_d=m_s5_d, s5_w_glu=m_s5_w_glu, s5_w_out=m_s5_w_out, gdn_w_in=m_gdn_w_in,
             gdn_conv_w=m_gdn_conv_w, gdn_a_log=m_gdn_a_log, gdn_dt_bias=m_gdn_dt_bias, gdn_norm_w=m_gdn_norm_w,
             gdn_w_out=m_gdn_w_out, final_norm_w=m_final_norm_w)
    V = dict(ada_w=v_ada_w, ada_b=v_ada_b, norm_w=v_norm_w, s5_w_in=v_s5_w_in, s5_lambda_re=v_s5_lambda_re,
             s5_lambda_im=v_s5_lambda_im, s5_log_dt=v_s5_log_dt, s5_b_re=v_s5_b_re, s5_b_im=v_s5_b_im, s5_c_re=v_s5_c_re,
             s5_c_im=v_s5_c_im, s5_d=v_s5_d, s5_w_glu=v_s5_w_glu, s5_w_out=v_s5_w_out, gdn_w_in=v_gdn_w_in,
             gdn_conv_w=v_gdn_conv_w, gdn_a_log=v_gdn_a_log, gdn_dt_bias=v_gdn_dt_bias, gdn_norm_w=v_gdn_norm_w,
             gdn_w_out=v_gdn_w_out, final_norm_w=v_final_norm_w)
    return _step(x, c, W, M, V, loss_target)
```

```python
import functools
import math

import jax
import jax.numpy as jnp
from jax import lax
from jax.experimental import pallas as pl
from jax.experimental.pallas import tpu as pltpu

F32 = jnp.float32
BF16 = jnp.bfloat16
SDS = jax.ShapeDtypeStruct

D_MODEL = 1024
D_INNER = 2048
NORM_EPS = 1e-6
S5_GROUP = 16
S5_GROUPS = 128
S5_STATE = 64
GDN_HEADS = 8
GDN_DK = 128
GDN_DV = 256
GDN_CONV = 4
GDN_CHUNK = 64
GDN_QK = 1024
GDN_CONV_CH = 4096
GDN_PROJ = 6160
ADAM_LR = 0.001
ADAM_B1 = 0.9
ADAM_B2 = 0.999
ADAM_EPS = 1e-08
ADAM_WD = 0.01
ADAM_STEP = 10

N_DEV = 8
LANES = 128
SUBLANES = 8
VMEM_BIG = 56 << 20
VMEM_MID = 40 << 20
S5_GB = 8
S5_TL = 2048
MESH_AXES = ("x", "y", "c")


def _params(sem, vmem=None):
    return pltpu.CompilerParams(dimension_semantics=sem, vmem_limit_bytes=vmem)


def _bdot(a, b, dims=(((1,), (0,)), ((), ()))):
    return lax.dot_general(a.astype(BF16), b.astype(BF16), dims, preferred_element_type=F32)


def _hdot(a, b, dims=(((1,), (0,)), ((), ()))):
    return lax.dot_general(a, b, dims, preferred_element_type=F32, precision=lax.Precision.HIGHEST)


_BNN = (((2,), (1,)), ((0,), (0,)))
_BNT = (((2,), (2,)), ((0,), (0,)))
_BTN = (((1,), (1,)), ((0,), (0,)))


@jax.custom_vjp
def _unit_lower_inverse(a):
    c = a.shape[-1]
    ri = lax.broadcasted_iota(jnp.int32, a.shape, 1)
    ci = lax.broadcasted_iota(jnp.int32, a.shape, 2)
    n = -a
    t = (ri == ci).astype(F32) + n
    for _ in range(int(math.log2(c)) - 1):
        n = _hdot(n, n, _BNN)
        t = t + _hdot(t, n, _BNN)
    return t


def _unit_lower_inverse_fwd(a):
    t = _unit_lower_inverse(a)
    return t, t


def _unit_lower_inverse_bwd(t, g):
    return (-_hdot(_hdot(t, g, _BTN), t, _BNT),)


_unit_lower_inverse.defvjp(_unit_lower_inverse_fwd, _unit_lower_inverse_bwd)


NN = (((1,), (0,)), ((), ()))
NT = (((1,), (1,)), ((), ()))
TN = (((0,), (0,)), ((), ()))


def _tile(n, pref):
    for t in (pref, 512, 256, 128):
        if t <= n and n % t == 0:
            return t
    return n


def _matmul(a, b, mode, name, add=None):
    if mode == "nn":
        (m, k), (_, n) = a.shape, b.shape
    elif mode == "nt":
        (m, k), (n, _) = a.shape, b.shape
    else:
        (k, m), (_, n) = a.shape, b.shape
    tm, tn, tk = _tile(m, 1024), _tile(n, 512), (k if k <= 2048 else _tile(k, 512))
    if mode == "tn":
        tm, tn, tk = _tile(m, 1024), _tile(n, 1024), _tile(k, 1024)
    nk = k // tk
    dims = {"nn": NN, "nt": NT, "tn": TN}[mode]

    def body(a_ref, b_ref, *rest):
        o_ref, acc_ref = rest[-2], rest[-1]
        part = _bdot(a_ref[...], b_ref[...], dims)
        if nk == 1:
            o_ref[...] = part if add is None else part + rest[0][...]
            return
        kk = pl.program_id(2)

        @pl.when(kk == 0)
        def _():
            acc_ref[...] = part if add is None else part + rest[0][...]

        @pl.when(kk > 0)
        def _():
            acc_ref[...] += part

        @pl.when(kk == nk - 1)
        def _():
            o_ref[...] = acc_ref[...]

    a_spec = pl.BlockSpec((tk, tm), lambda i, j, q: (q, i)) if mode == "tn" else pl.BlockSpec((tm, tk), lambda i, j, q: (i, q))
    b_spec = pl.BlockSpec((tn, tk), lambda i, j, q: (j, q)) if mode == "nt" else pl.BlockSpec((tk, tn), lambda i, j, q: (q, j))
    o_spec = pl.BlockSpec((tm, tn), lambda i, j, q: (i, j))
    return pl.pallas_call(
        body, name=name, grid=(m // tm, n // tn, nk),
        in_specs=[a_spec, b_spec] + ([] if add is None else [o_spec]), out_specs=o_spec,
        out_shape=SDS((m, n), F32), scratch_shapes=[pltpu.VMEM((tm, tn), F32)],
        compiler_params=_params(("parallel", "parallel", "arbitrary"), VMEM_MID),
    )(a, b, *([] if add is None else [add]))


def make_mm(name, pass_input=False):
    def primal(a, w):
        out = _matmul(a, w, "nn", name + "_fwd")
        return (out, a) if pass_input else out

    @jax.custom_vjp
    def mm(a, w, grad_slot):
        return primal(a, w)

    def fwd(a, w, grad_slot):
        return primal(a, w), (a, w)

    def bwd(res, g):
        a, w = res
        g, g_other = g if pass_input else (g, None)
        return _matmul(g, w, "nt", name + "_dx", add=g_other), jnp.zeros_like(w), _matmul(a, g, "tn", name + "_dw")

    mm.defvjp(fwd, bwd)
    return mm


PROJ_ROWS = 256


def _proj_fwd_call(a, ws, name, w_rows):
    m, k = a.shape
    tm = _tile(m, PROJ_ROWS)
    nw = len(ws)
    widths = [w.shape[0] if w_rows else w.shape[1] for w in ws]

    def body(*refs):
        ab = refs[0][...].astype(BF16)
        for w_ref, o_ref in zip(refs[1:1 + nw], refs[1 + nw:]):
            o_ref[...] = lax.dot_general(ab, w_ref[...], NT if w_rows else NN, preferred_element_type=F32)

    return pl.pallas_call(
        body, name=name, grid=(m // tm,),
        in_specs=[pl.BlockSpec((tm, k), lambda i: (i, 0))] + [pl.BlockSpec(w.shape, lambda i: (0, 0)) for w in ws],
        out_specs=[pl.BlockSpec((tm, n), lambda i: (i, 0)) for n in widths],
        out_shape=[SDS((m, n), F32) for n in widths],
        compiler_params=_params(("parallel",), VMEM_BIG),
    )(a, *ws)


def _proj_dx_call(gs, ws, name, w_rows):
    m = gs[0].shape[0]
    k = ws[0].shape[1] if w_rows else ws[0].shape[0]
    tm = _tile(m, PROJ_ROWS)
    nw = len(ws)

    def body(*refs):
        acc = None
        for g_ref, w_ref in zip(refs[:nw], refs[nw:2 * nw]):
            part = _bdot(g_ref[...], w_ref[...], NN if w_rows else NT)
            acc = part if acc is None else acc + part
        refs[2 * nw][...] = acc

    return pl.pallas_call(
        body, name=name, grid=(m // tm,),
        in_specs=[pl.BlockSpec((tm, g.shape[1]), lambda i: (i, 0)) for g in gs] + [pl.BlockSpec(w.shape, lambda i: (0, 0)) for w in ws],
        out_specs=pl.BlockSpec((tm, k), lambda i: (i, 0)), out_shape=SDS((m, k), F32),
        compiler_params=_params(("parallel",), VMEM_BIG),
    )(*gs, *ws)


def make_proj(name, w_rows=False):
    @jax.custom_vjp
    def proj(a, ws, grad_slots):
        return tuple(_proj_fwd_call(a, ws, name + "_fwd", w_rows))

    def fwd(a, ws, grad_slots):
        return tuple(_proj_fwd_call(a, ws, name + "_fwd", w_rows)), (a, ws)

    def bwd(res, gs):
        a, ws = res
        dws = tuple(_matmul(g, a, "tn", "%s_dw%d" % (name, i)) if w_rows else _matmul(a, g, "tn", "%s_dw%d" % (name, i))
                    for i, g in enumerate(gs))
        return _proj_dx_call(tuple(gs), ws, name + "_dx", w_rows), tuple(jnp.zeros_like(w) for w in ws), dws

    proj.defvjp(fwd, bwd)
    return proj


def make_rowwise(f, name, tm, n_rows, n_params, vmem=VMEM_MID, pass_first=False):
    def specs_of(arrs, blocked):
        if blocked:
            return [pl.BlockSpec((tm, a.shape[1]), lambda i: (i, 0)) for a in arrs]
        return [pl.BlockSpec(a.shape, lambda i: (0, 0)) for a in arrs]

    def out_structs(rows, params):
        blk = [SDS((tm, r.shape[1]), r.dtype) for r in rows] + [SDS(p.shape, p.dtype) for p in params]
        return jax.eval_shape(f, *blk)

    def run_fwd(rows, params):
        L = rows[0].shape[0]
        outs = out_structs(rows, params)

        def body(*refs):
            ins = [r[...] for r in refs[:n_rows + n_params]]
            res = f(*ins)
            for o_ref, val in zip(refs[n_rows + n_params:], res):
                o_ref[...] = val

        return pl.pallas_call(
            body, name=name + "_fwd", grid=(L // tm,),
            in_specs=specs_of(rows, True) + specs_of(params, False),
            out_specs=[pl.BlockSpec((tm, o.shape[1]), lambda i: (i, 0)) for o in outs],
            out_shape=[SDS((L, o.shape[1]), o.dtype) for o in outs],
            compiler_params=_params(("parallel",), vmem),
        )(*rows, *params)

    def run_bwd(rows, params, gs):
        L = rows[0].shape[0]
        n_g = len(gs)

        def body(*refs):
            i = pl.program_id(0)
            ins = [r[...] for r in refs[:n_rows + n_params]]
            cts = tuple(r[...] for r in refs[n_rows + n_params:n_rows + n_params + n_g])
            outs = refs[n_rows + n_params + n_g:]
            _, vjp = jax.vjp(f, *ins)
            grads = vjp(cts[:-1] if pass_first else cts)
            if pass_first:
                grads = (grads[0] + cts[-1],) + tuple(grads[1:])
            for o_ref, val in zip(outs[:n_rows], grads[:n_rows]):
                o_ref[...] = val

            if n_params:
                @pl.when(i == 0)
                def _():
                    for o_ref in outs[n_rows:]:
                        o_ref[...] = jnp.zeros_like(o_ref)
                for o_ref, val in zip(outs[n_rows:], grads[n_rows:]):
                    o_ref[...] += val

        res = pl.pallas_call(
            body, name=name + "_bwd", grid=(L // tm,),
            in_specs=specs_of(rows, True) + specs_of(params, False) + specs_of(gs, True),
            out_specs=specs_of(rows, True) + specs_of(params, False),
            out_shape=[SDS(r.shape, r.dtype) for r in rows] + [SDS(p.shape, p.dtype) for p in params],
            compiler_params=_params(("arbitrary",), vmem),
        )(*rows, *params, *gs)
        return tuple(res[:n_rows]), tuple(res[n_rows:])

    def outputs(rows, params):
        outs = tuple(run_fwd(rows, params))
        return outs + (rows[0],) if pass_first else outs

    @jax.custom_vjp
    def op(rows, params):
        return outputs(rows, params)

    def fwd(rows, params):
        return outputs(rows, params), (rows, params)

    def bwd(res, gs):
        rows, params = res
        return run_bwd(rows, params, tuple(gs))

    op.defvjp(fwd, bwd)
    op.run_fwd, op.run_bwd = run_fwd, run_bwd
    return op


def _s5_scan_rows(xr_ref, xi_ref, ar, ai, x0r, x0i, tl, reverse=False):
    n = xr_ref.shape[1]
    T = SUBLANES
    row = lax.broadcasted_iota(jnp.int32, (T, n), 0)
    pr, pi = [ar], [ai]
    for _ in range(T - 1):
        pr, pi = pr + [pr[-1] * ar - pi[-1] * ai], pi + [pr[-1] * ai + pi[-1] * ar]
    levels = []
    for d in (1, 2, 4):
        mask = (row < T - d) if reverse else (row >= d)
        levels.append((T - d if reverse else d, jnp.where(mask, pr[d - 1], 0.0), jnp.where(mask, pi[d - 1], 0.0)))
    cr = jnp.zeros((T, n), F32)
    ci = jnp.zeros((T, n), F32)
    for r in range(T):
        k = (T - r) if reverse else (r + 1)
        cr = jnp.where(row == r, pr[k - 1], cr)
        ci = jnp.where(row == r, pi[k - 1], ci)
    nt = tl // T
    last = 0 if reverse else T - 1

    def step(t, carry):
        sr, si = carry
        base = pl.multiple_of((nt - 1 - t if reverse else t) * T, T)
        br = xr_ref[pl.ds(base, T), :]
        bi = xi_ref[pl.ds(base, T), :]
        for shift, mr, mi in levels:
            qr = pltpu.roll(br, shift, 0)
            qi = pltpu.roll(bi, shift, 0)
            br, bi = br + (mr * qr - mi * qi), bi + (mr * qi + mi * qr)
        xr = br + (cr * sr - ci * si)
        xi = bi + (cr * si + ci * sr)
        xr_ref[pl.ds(base, T), :] = xr
        xi_ref[pl.ds(base, T), :] = xi
        return xr[last:last + 1, :], xi[last:last + 1, :]
    return lax.fori_loop(0, nt, step, (x0r, x0i))


def _s5_fwd_call(u, bre, bim, cre, cim, a, d, tl):
    L, e = u.shape
    nb = e // LANES
    ns = bre.shape[2]
    nc = L // tl

    def body(u_ref, bre_ref, bim_ref, cre_ref, cim_ref, a_ref, d_ref, y_ref, xb_ref, sr_ref, si_ref, xr_ref, xi_ref, carry_ref):
        c = pl.program_id(1)

        @pl.when(c == 0)
        def _():
            carry_ref[...] = jnp.zeros_like(carry_ref)
        xb_ref[0, 0] = carry_ref[...]
        ub = u_ref[...]
        xr_ref[...] = _bdot(ub, bre_ref[0])
        xi_ref[...] = _bdot(ub, bim_ref[0])
        ar = a_ref[0, 0:1, :]
        ai = a_ref[0, 1:2, :]
        xr, xi = _s5_scan_rows(xr_ref, xi_ref, ar, ai, carry_ref[0:1, :], carry_ref[1:2, :], tl)
        carry_ref[0:1, :] = xr
        carry_ref[1:2, :] = xi
        sr = xr_ref[...].astype(BF16)
        si = xi_ref[...].astype(BF16)
        sr_ref[...] = sr
        si_ref[...] = si
        y_ref[...] = _f_s5_act(_bdot(sr, cre_ref[0]) - _bdot(si, cim_ref[0]), ub, d_ref[...])[0]

    return pl.pallas_call(
        body, name="s5_core_fwd", grid=(nb, nc),
        in_specs=[pl.BlockSpec((tl, LANES), lambda j, c: (c, j)),
                  pl.BlockSpec((1, LANES, ns), lambda j, c: (j, 0, 0)), pl.BlockSpec((1, LANES, ns), lambda j, c: (j, 0, 0)),
                  pl.BlockSpec((1, ns, LANES), lambda j, c: (j, 0, 0)), pl.BlockSpec((1, ns, LANES), lambda j, c: (j, 0, 0)),
                  pl.BlockSpec((1, SUBLANES, ns), lambda j, c: (j, 0, 0)), pl.BlockSpec((1, LANES), lambda j, c: (0, j))],
        out_specs=[pl.BlockSpec((tl, LANES), lambda j, c: (c, j)),
                   pl.BlockSpec((1, 1, SUBLANES, ns), lambda j, c: (j, c, 0, 0)),
                   pl.BlockSpec((tl, ns), lambda j, c: (c, j)), pl.BlockSpec((tl, ns), lambda j, c: (c, j))],
        out_shape=[SDS((L, e), F32), SDS((nb, nc, SUBLANES, ns), F32), SDS((L, nb * ns), BF16), SDS((L, nb * ns), BF16)],
        scratch_shapes=[pltpu.VMEM((tl, ns), F32), pltpu.VMEM((tl, ns), F32), pltpu.VMEM((SUBLANES, ns), F32)],
        compiler_params=_params(("arbitrary", "arbitrary"), VMEM_MID),
    )(u, bre, bim, cre, cim, a, d)


def _s5_bwd_call(u, dy2, bre, bim, cre, cim, a, d, xb, sr, si, tl):
    L, e = u.shape
    nb = e // LANES
    ns = bre.shape[2]
    nc = L // tl

    def body(u_ref, dy2_ref, bre_ref, bim_ref, cre_ref, cim_ref, a_ref, d_ref, xb_ref, sr_ref, si_ref,
             du_ref, dbre_ref, dbim_ref, dcre_ref, dcim_ref, da_ref, dd_ref,
             gr_ref, gi_ref, gcarry_ref):
        c = pl.program_id(1)

        @pl.when(c == 0)
        def _():
            gcarry_ref[...] = jnp.zeros_like(gcarry_ref)
            dbre_ref[...] = jnp.zeros_like(dbre_ref)
            dbim_ref[...] = jnp.zeros_like(dbim_ref)
            dcre_ref[...] = jnp.zeros_like(dcre_ref)
            dcim_ref[...] = jnp.zeros_like(dcim_ref)
            da_ref[...] = jnp.zeros_like(da_ref)
            dd_ref[...] = jnp.zeros_like(dd_ref)

        ub = u_ref[...]
        ys = _bdot(sr_ref[...], cre_ref[0]) - _bdot(si_ref[...], cim_ref[0])
        _, act_vjp = jax.vjp(lambda *t: _f_s5_act(*t)[0], ys, ub, d_ref[...])
        dy, du_skip, dd = act_vjp(dy2_ref[...])
        dd_ref[...] += dd
        ar = a_ref[0, 0:1, :]
        ai = a_ref[0, 1:2, :]
        x0r = xb_ref[0, 0, 0:1, :]
        x0i = xb_ref[0, 0, 1:2, :]
        dcre_ref[0] += _bdot(sr_ref[...], dy, TN)
        dcim_ref[0] -= _bdot(si_ref[...], dy, TN)
        gr_ref[...] = _bdot(dy, cre_ref[0], NT)
        gi_ref[...] = -_bdot(dy, cim_ref[0], NT)

        g0r, g0i = _s5_scan_rows(gr_ref, gi_ref, ar, -ai, gcarry_ref[0:1, :], gcarry_ref[1:2, :], tl, reverse=True)
        gcarry_ref[0:1, :] = g0r
        gcarry_ref[1:2, :] = g0i
        row = lax.broadcasted_iota(jnp.int32, (tl, ns), 0)
        gr = gr_ref[...]
        gi = gi_ref[...]
        xpr = jnp.where(row == 0, x0r, pltpu.roll(sr_ref[...].astype(F32), 1, 0))
        xpi = jnp.where(row == 0, x0i, pltpu.roll(si_ref[...].astype(F32), 1, 0))
        da_ref[0, 0:1, :] += jnp.sum(gr * xpr + gi * xpi, axis=0, keepdims=True)
        da_ref[0, 1:2, :] += jnp.sum(gi * xpr - gr * xpi, axis=0, keepdims=True)
        du_ref[...] = (_bdot(gr, bre_ref[0], NT) + _bdot(gi, bim_ref[0], NT)) + du_skip
        dbre_ref[0] += _bdot(ub, gr, TN)
        dbim_ref[0] += _bdot(ub, gi, TN)

    rev = lambda c: nc - 1 - c
    return pl.pallas_call(
        body, name="s5_core_bwd", grid=(nb, nc),
        in_specs=[pl.BlockSpec((tl, LANES), lambda j, c: (rev(c), j)), pl.BlockSpec((tl, LANES), lambda j, c: (rev(c), j)),
                  pl.BlockSpec((1, LANES, ns), lambda j, c: (j, 0, 0)), pl.BlockSpec((1, LANES, ns), lambda j, c: (j, 0, 0)),
                  pl.BlockSpec((1, ns, LANES), lambda j, c: (j, 0, 0)), pl.BlockSpec((1, ns, LANES), lambda j, c: (j, 0, 0)),
                  pl.BlockSpec((1, SUBLANES, ns), lambda j, c: (j, 0, 0)), pl.BlockSpec((1, LANES), lambda j, c: (0, j)),
                  pl.BlockSpec((1, 1, SUBLANES, ns), lambda j, c: (j, rev(c), 0, 0)),
                  pl.BlockSpec((tl, ns), lambda j, c: (rev(c), j)), pl.BlockSpec((tl, ns), lambda j, c: (rev(c), j))],
        out_specs=[pl.BlockSpec((tl, LANES), lambda j, c: (rev(c), j)),
                   pl.BlockSpec((1, LANES, ns), lambda j, c: (j, 0, 0)), pl.BlockSpec((1, LANES, ns), lambda j, c: (j, 0, 0)),
                   pl.BlockSpec((1, ns, LANES), lambda j, c: (j, 0, 0)), pl.BlockSpec((1, ns, LANES), lambda j, c: (j, 0, 0)),
                   pl.BlockSpec((1, SUBLANES, ns), lambda j, c: (j, 0, 0)), pl.BlockSpec((1, LANES), lambda j, c: (0, j))],
        out_shape=[SDS((L, e), F32), SDS(bre.shape, F32), SDS(bim.shape, F32), SDS(cre.shape, F32), SDS(cim.shape, F32),
                   SDS(a.shape, F32), SDS(d.shape, F32)],
        scratch_shapes=[pltpu.VMEM((tl, ns), F32) for _ in range(2)] + [pltpu.VMEM((SUBLANES, ns), F32)],
        compiler_params=_params(("arbitrary", "arbitrary"), VMEM_MID),
    )(u, dy2, bre, bim, cre, cim, a, d, xb, sr, si)


def make_s5_core(tl):
    @jax.custom_vjp
    def s5_core(u, bre, bim, cre, cim, a, d):
        return _s5_fwd_call(u, bre, bim, cre, cim, a, d, tl)[0]

    def fwd(u, bre, bim, cre, cim, a, d):
        y2, xb, sr, si = _s5_fwd_call(u, bre, bim, cre, cim, a, d, tl)
        return y2, (u, bre, bim, cre, cim, a, d, xb, sr, si)

    def bwd(res, dy2):
        u, bre, bim, cre, cim, a, d, xb, sr, si = res
        return tuple(_s5_bwd_call(u, dy2, bre, bim, cre, cim, a, d, xb, sr, si, tl))

    s5_core.defvjp(fwd, bwd)
    return s5_core


def _s5_block_params(lam_re, lam_im, log_dt, b_re, b_im, c_re, c_im):
    dt = jnp.exp(log_dt)[:, None]
    mag = jnp.exp(lam_re * dt)
    ab_re = mag * jnp.cos(lam_im * dt)
    ab_im = mag * jnp.sin(lam_im * dt)
    den = lam_re * lam_re + lam_im * lam_im
    nr = ab_re - 1.0
    ni = ab_im
    q_re = (nr * lam_re + ni * lam_im) / den
    q_im = (ni * lam_re - nr * lam_im) / den
    bb_re = q_re[..., None] * b_re - q_im[..., None] * b_im
    bb_im = q_re[..., None] * b_im + q_im[..., None] * b_re
    nb = S5_GROUPS // S5_GB
    eye = jnp.eye(S5_GB, dtype=F32)

    def bdiag_in(bb):
        t = bb.reshape(nb, S5_GB, S5_STATE, S5_GROUP)
        t = jnp.einsum("jgpm,gh->jgmhp", t, eye)
        return t.reshape(nb, S5_GB * S5_GROUP, S5_GB * S5_STATE)

    def bdiag_out(cc):
        t = cc.reshape(nb, S5_GB, S5_GROUP, S5_STATE)
        t = jnp.einsum("jgmp,gh->jgphm", t, eye)
        return t.reshape(nb, S5_GB * S5_STATE, S5_GB * S5_GROUP)

    a = jnp.stack([ab_re.reshape(nb, S5_GB * S5_STATE), ab_im.reshape(nb, S5_GB * S5_STATE)], axis=1)
    a = jnp.concatenate([a, jnp.zeros((nb, SUBLANES - 2, S5_GB * S5_STATE), F32)], axis=1)
    return bdiag_in(bb_re), bdiag_in(bb_im), bdiag_out(c_re), bdiag_out(c_im), a


def _shift_down(x, s, row):
    if s == 0:
        return x
    return jnp.where(row >= s, pltpu.roll(x, s, 0), 0.0)


def _shift_up(x, s, row, n):
    if s == 0:
        return x
    return jnp.where(row < n - s, pltpu.roll(x, n - s, 0), 0.0)


def _causal_conv(xv, w_ref, row):
    acc = jnp.zeros_like(xv)
    for j in range(GDN_CONV):
        acc += w_ref[j:j + 1, :] * _shift_down(xv, GDN_CONV - 1 - j, row)
    return acc


def _conv_fwd_call(x, w, act, name):
    L, ch = x.shape

    def body(x_ref, w_ref, y_ref):
        xv = x_ref[...]
        row = lax.broadcasted_iota(jnp.int32, xv.shape, 0)
        y_ref[...] = act(_causal_conv(xv, w_ref, row))

    return pl.pallas_call(
        body, name=name + "_fwd", grid=(ch // LANES,),
        in_specs=[pl.BlockSpec((L, LANES), lambda j: (0, j)), pl.BlockSpec((SUBLANES, LANES), lambda j: (0, j))],
        out_specs=pl.BlockSpec((L, LANES), lambda j: (0, j)), out_shape=SDS((L, ch), F32),
        compiler_params=_params(("parallel",), VMEM_MID),
    )(x, w)


def _conv_bwd_call(x, w, dy, act, name):
    L, ch = x.shape

    def body(x_ref, w_ref, dy_ref, dx_ref, dw_ref):
        xv = x_ref[...]
        row = lax.broadcasted_iota(jnp.int32, xv.shape, 0)
        _, act_vjp = jax.vjp(act, _causal_conv(xv, w_ref, row))
        (g,) = act_vjp(dy_ref[...])
        acc = jnp.zeros_like(xv)
        dws = []
        for j in range(GDN_CONV):
            s = GDN_CONV - 1 - j
            acc += w_ref[j:j + 1, :] * _shift_up(g, s, row, L)
            dws.append(jnp.sum(g * _shift_down(xv, s, row), axis=0, keepdims=True))
        dx_ref[...] = acc
        dw_ref[...] = jnp.concatenate(dws + [jnp.zeros((SUBLANES - GDN_CONV, LANES), F32)], axis=0)

    return pl.pallas_call(
        body, name=name + "_bwd", grid=(ch // LANES,),
        in_specs=[pl.BlockSpec((L, LANES), lambda j: (0, j)), pl.BlockSpec((SUBLANES, LANES), lambda j: (0, j)),
                  pl.BlockSpec((L, LANES), lambda j: (0, j))],
        out_specs=[pl.BlockSpec((L, LANES), lambda j: (0, j)), pl.BlockSpec((SUBLANES, LANES), lambda j: (0, j))],
        out_shape=[SDS((L, ch), F32), SDS((SUBLANES, ch), F32)],
        compiler_params=_params(("parallel",), VMEM_MID),
    )(x, w, dy)


def make_conv_act(act, name):
    @jax.custom_vjp
    def op(x, w):
        return _conv_fwd_call(x, w, act, name)

    def fwd(x, w):
        return _conv_fwd_call(x, w, act, name), (x, w)

    def bwd(res, dy):
        x, w = res
        return tuple(_conv_bwd_call(x, w, dy, act, name))

    op.defvjp(fwd, bwd)
    return op


BNN, BNT, BTN = _BNN, _BNT, _BTN
GDN_PREP_BATCH = 8


@jax.custom_vjp
def _known_inverse(a, t):
    return t


def _known_inverse_fwd(a, t):
    return t, t


def _known_inverse_bwd(t, g):
    return -_hdot(_hdot(t, g, _BTN), t, _BNT), jnp.zeros_like(t)


_known_inverse.defvjp(_known_inverse_fwd, _known_inverse_bwd)


def _gdn_prep_math(q, k, v, beta, g, t_saved=None):
    B, C = q.shape[0], q.shape[1]
    ri = lax.broadcasted_iota(jnp.int32, (B, C, C), 1)
    ci = lax.broadcasted_iota(jnp.int32, (B, C, C), 2)
    causal = ri >= ci
    strict = ri > ci
    eye = (ri == ci).astype(F32)
    gb = jnp.broadcast_to(g, (B, C, C))
    g_row = jnp.sum(gb * eye, axis=1, keepdims=True)
    gc_col = jnp.sum(jnp.where(causal, jnp.broadcast_to(g_row, (B, C, C)), 0.0), axis=2, keepdims=True)
    gc_row = jnp.sum(jnp.where(ri <= ci, gb, 0.0), axis=1, keepdims=True)
    decay = jnp.exp(jnp.where(causal, gc_col - gc_row, -jnp.inf))
    kk = _bdot(k, k, BNT)
    a_mat = jnp.where(strict, beta * kk * decay, 0.0)
    t = _unit_lower_inverse(a_mat) if t_saved is None else _known_inverse(a_mat, t_saved)
    e_gc = jnp.exp(gc_col)
    w = _hdot(t, beta * e_gc * k, BNN)
    u = _hdot(t, beta * v, BNN)
    qk = _bdot(q, k, BNT) * decay
    q_dec = q * e_gc
    g_last = gc_col[:, C - 1:C, :]
    k_dec = k * jnp.exp(g_last - gc_col)
    return q_dec, w, u, qk, k_dec, gc_col, t


def _gdn_prep_specs(L):
    C = GDN_CHUNK
    nb = min(GDN_PREP_BATCH, L // C)
    R = nb * C
    ins = [pl.BlockSpec((R, GDN_DK), lambda c, h: (c, h)), pl.BlockSpec((R, GDN_DK), lambda c, h: (c, h)),
           pl.BlockSpec((R, GDN_DV), lambda c, h: (c, h)), pl.BlockSpec((R, LANES), lambda c, h: (c, 0))]
    outs = [pl.BlockSpec((1, R, GDN_DK), lambda c, h: (h, c, 0)), pl.BlockSpec((1, R, GDN_DK), lambda c, h: (h, c, 0)),
            pl.BlockSpec((1, R, GDN_DV), lambda c, h: (h, c, 0)), pl.BlockSpec((1, R, C), lambda c, h: (h, c, 0)),
            pl.BlockSpec((1, R, GDN_DK), lambda c, h: (h, c, 0)), pl.BlockSpec((1, R, 1), lambda c, h: (h, c, 0))]
    t_spec = pl.BlockSpec((1, R, C), lambda c, h: (h, c, 0))
    shapes = [SDS((GDN_HEADS, L, GDN_DK), F32), SDS((GDN_HEADS, L, GDN_DK), F32), SDS((GDN_HEADS, L, GDN_DV), F32),
              SDS((GDN_HEADS, L, C), F32), SDS((GDN_HEADS, L, GDN_DK), F32), SDS((GDN_HEADS, L, 1), F32)]
    return ins, outs, t_spec, shapes, nb


def _chunks(x, nb):
    return x.reshape(nb, x.shape[0] // nb, x.shape[1])


def _head_columns(bg, h):
    lane = lax.broadcasted_iota(jnp.int32, bg.shape, 1)
    beta = jnp.sum(jnp.where(lane == h, bg, 0.0), axis=1, keepdims=True)
    g = jnp.sum(jnp.where(lane == h + GDN_HEADS, bg, 0.0), axis=1, keepdims=True)
    return beta, g


def _gdn_prep_fwd_call(q, k, v, bg):
    L = q.shape[0]
    ins, outs, t_spec, shapes, nb = _gdn_prep_specs(L)

    def body(q_ref, k_ref, v_ref, bg_ref, *o_refs):
        beta, g = _head_columns(bg_ref[...], pl.program_id(1))
        res = _gdn_prep_math(_chunks(q_ref[...], nb), _chunks(k_ref[...], nb), _chunks(v_ref[...], nb),
                             _chunks(beta, nb), _chunks(g, nb))
        for o_ref, val in zip(o_refs, res):
            o_ref[0] = val.reshape(val.shape[0] * val.shape[1], val.shape[2])

    return pl.pallas_call(
        body, name="gdn_prep_fwd", grid=(L // (nb * GDN_CHUNK), GDN_HEADS), in_specs=ins, out_specs=outs + [t_spec],
        out_shape=shapes + [SDS((GDN_HEADS, L, GDN_CHUNK), F32)],
        compiler_params=_params(("parallel", "parallel"), VMEM_MID),
    )(q, k, v, bg)


def _gdn_prep_bwd_call(q, k, v, bg, t, cts):
    L = q.shape[0]
    ins, outs, t_spec, _, nb = _gdn_prep_specs(L)

    def body(q_ref, k_ref, v_ref, bg_ref, t_ref, c0, c1, c2, c3, c4, c5, dq_ref, dk_ref, dv_ref, dbg_ref):
        h = pl.program_id(1)
        beta, g = _head_columns(bg_ref[...], h)
        t_saved = _chunks(t_ref[0], nb)
        _, vjp = jax.vjp(lambda *a: _gdn_prep_math(*a, t_saved=t_saved)[:6], _chunks(q_ref[...], nb), _chunks(k_ref[...], nb),
                         _chunks(v_ref[...], nb), _chunks(beta, nb), _chunks(g, nb))
        dq, dk, dv, db, dg = vjp(tuple(_chunks(c[0], nb) for c in (c0, c1, c2, c3, c4, c5)))
        flat = lambda a: a.reshape(a.shape[0] * a.shape[1], a.shape[2])
        dq_ref[...] = flat(dq)
        dk_ref[...] = flat(dk)
        dv_ref[...] = flat(dv)

        @pl.when(h == 0)
        def _():
            dbg_ref[...] = jnp.zeros_like(dbg_ref)
        lane = lax.broadcasted_iota(jnp.int32, dbg_ref.shape, 1)
        dbg_ref[...] += jnp.where(lane == h, flat(db), 0.0) + jnp.where(lane == h + GDN_HEADS, flat(dg), 0.0)

    return pl.pallas_call(
        body, name="gdn_prep_bwd", grid=(L // (nb * GDN_CHUNK), GDN_HEADS), in_specs=ins + [t_spec] + outs, out_specs=ins,
        out_shape=[SDS(q.shape, F32), SDS(k.shape, F32), SDS(v.shape, F32), SDS(bg.shape, F32)],
        compiler_params=_params(("parallel", "arbitrary"), VMEM_MID),
    )(q, k, v, bg, t, *cts)


@jax.custom_vjp
def gdn_prep(q, k, v, bg):
    return tuple(_gdn_prep_fwd_call(q, k, v, bg)[:6])


def _gdn_prep_f(q, k, v, bg):
    res = _gdn_prep_fwd_call(q, k, v, bg)
    return tuple(res[:6]), (q, k, v, bg, res[6])


def _gdn_prep_b(res, cts):
    return tuple(_gdn_prep_bwd_call(*res, tuple(cts)))


gdn_prep.defvjp(_gdn_prep_f, _gdn_prep_b)


def _gdn_step_math(q_dec, w, u, qk, k_dec, gc, z, nw, state):
    H, C = q_dec.shape[0], q_dec.shape[1]
    v_new = u - _bdot(w, state, BNN)
    o = _bdot(q_dec, state, BNN) + _bdot(qk, v_new, BNN)
    gl = gc[:, C - 1:C, :]
    new_state = jnp.exp(gl) * state + _bdot(k_dec, v_new, BTN)
    return _f_gdn_post(jnp.concatenate([o[h] for h in range(H)], axis=1), z, nw)[0], new_state


def _gdn_scan_specs(L, rev):
    C, H = GDN_CHUNK, GDN_HEADS
    nc = L // C
    cc = (lambda c: nc - 1 - c) if rev else (lambda c: c)
    ins = [pl.BlockSpec((H, C, GDN_DK), lambda c: (0, cc(c), 0)), pl.BlockSpec((H, C, GDN_DK), lambda c: (0, cc(c), 0)),
           pl.BlockSpec((H, C, GDN_DV), lambda c: (0, cc(c), 0)), pl.BlockSpec((H, C, C), lambda c: (0, cc(c), 0)),
           pl.BlockSpec((H, C, GDN_DK), lambda c: (0, cc(c), 0)), pl.BlockSpec((H, C, 1), lambda c: (0, cc(c), 0))]
    o_spec = pl.BlockSpec((C, H * GDN_DV), lambda c: (cc(c), 0))
    nw_spec = pl.BlockSpec((1, H * GDN_DV), lambda c: (0, 0))
    s_spec = pl.BlockSpec((1, H, GDN_DK, GDN_DV), lambda c: (cc(c), 0, 0, 0))
    return ins + [o_spec, nw_spec], o_spec, s_spec, nc


def _gdn_scan_fwd_call(q_dec, w, u, qk, k_dec, gc, z, nw):
    L = q_dec.shape[1]
    ins, o_spec, s_spec, nc = _gdn_scan_specs(L, False)

    def body(qd_ref, w_ref, u_ref, qk_ref, kd_ref, gc_ref, z_ref, nw_ref, o_ref, sin_ref, s_ref):
        c = pl.program_id(0)

        @pl.when(c == 0)
        def _():
            s_ref[...] = jnp.zeros_like(s_ref)
        st = s_ref[...]
        sin_ref[0] = st
        o, ns = _gdn_step_math(qd_ref[...], w_ref[...], u_ref[...], qk_ref[...], kd_ref[...], gc_ref[...], z_ref[...], nw_ref[...], st)
        o_ref[...] = o
        s_ref[...] = ns

    return pl.pallas_call(
        body, name="gdn_scan_fwd", grid=(nc,), in_specs=ins, out_specs=[o_spec, s_spec],
        out_shape=[SDS((L, GDN_HEADS * GDN_DV), F32), SDS((nc, GDN_HEADS, GDN_DK, GDN_DV), F32)],
        scratch_shapes=[pltpu.VMEM((GDN_HEADS, GDN_DK, GDN_DV), F32)],
        compiler_params=_params(("arbitrary",), VMEM_MID),
    )(q_dec, w, u, qk, k_dec, gc, z, nw)


def _gdn_scan_bwd_call(q_dec, w, u, qk, k_dec, gc, z, nw, s_in, do):
    L = q_dec.shape[1]
    ins, o_spec, s_spec, nc = _gdn_scan_specs(L, True)

    def body(qd_ref, w_ref, u_ref, qk_ref, kd_ref, gc_ref, z_ref, nw_ref, sin_ref, do_ref,
             dqd_ref, dw_ref, du_ref, dqk_ref, dkd_ref, dgc_ref, dz_ref, dnw_ref, ds_ref):
        c = pl.program_id(0)

        @pl.when(c == 0)
        def _():
            ds_ref[...] = jnp.zeros_like(ds_ref)
            dnw_ref[...] = jnp.zeros_like(dnw_ref)
        _, vjp = jax.vjp(_gdn_step_math, qd_ref[...], w_ref[...], u_ref[...], qk_ref[...], kd_ref[...], gc_ref[...],
                         z_ref[...], nw_ref[...], sin_ref[0])
        dqd, dw, du, dqk, dkd, dgc, dz, dnw, dst = vjp((do_ref[...], ds_ref[...]))
        dqd_ref[...] = dqd
        dw_ref[...] = dw
        du_ref[...] = du
        dqk_ref[...] = dqk
        dkd_ref[...] = dkd
        dgc_ref[...] = dgc
        dz_ref[...] = dz
        dnw_ref[...] += dnw
        ds_ref[...] = dst

    return pl.pallas_call(
        body, name="gdn_scan_bwd", grid=(nc,), in_specs=ins + [s_spec, o_spec], out_specs=ins,
        out_shape=[SDS(t.shape, F32) for t in (q_dec, w, u, qk, k_dec, gc, z, nw)],
        scratch_shapes=[pltpu.VMEM((GDN_HEADS, GDN_DK, GDN_DV), F32)],
        compiler_params=_params(("arbitrary",), VMEM_MID),
    )(q_dec, w, u, qk, k_dec, gc, z, nw, s_in, do)


@jax.custom_vjp
def gdn_scan(q_dec, w, u, qk, k_dec, gc, z, nw):
    return _gdn_scan_fwd_call(q_dec, w, u, qk, k_dec, gc, z, nw)[0]


def _gdn_scan_f(*args):
    o, s_in = _gdn_scan_fwd_call(*args)
    return o, (*args, s_in)


def _gdn_scan_b(res, do):
    return tuple(_gdn_scan_bwd_call(*res, do))


gdn_scan.defvjp(_gdn_scan_f, _gdn_scan_b)


def _silu(x):
    return x * jax.nn.sigmoid(x)


def _gelu_tanh(x):
    return 0.5 * x * (1.0 + jnp.tanh(math.sqrt(2.0 / math.pi) * (x + 0.044715 * (x * x * x))))


def _f_lnmod(x, nw, sc, sh, bsc, bsh):
    xn = x * lax.rsqrt(jnp.mean(x * x, axis=-1, keepdims=True) + NORM_EPS) * nw
    return (xn * (1.0 + (sc + bsc)) + (sh + bsh),)


def _f_s5_act(ys, u, d):
    return (_gelu_tanh(ys + d * u),)


def _f_s5_gate(y2, t, z):
    return (y2 * jax.nn.sigmoid(t) * _silu(z),)


def _f_res(x, y, gate, bgate):
    return (x + (gate + bgate) * y,)


def _heads(x, width, fn):
    return jnp.concatenate([fn(x[:, i * width:(i + 1) * width]) for i in range(x.shape[1] // width)], axis=1)


def _l2n(x):
    return x * lax.rsqrt(jnp.sum(x * x, axis=-1, keepdims=True) + NORM_EPS)


def _f_betag(ba, alog, dtb):
    col = lax.broadcasted_iota(jnp.int32, ba.shape, 1)
    t = ba + dtb
    softplus = jnp.maximum(t, 0.0) + jnp.log1p(jnp.exp(-jnp.abs(t)))
    g = -jnp.exp(alog) * softplus
    return (jnp.where(col < GDN_HEADS, jax.nn.sigmoid(ba), jnp.where(col < 2 * GDN_HEADS, g, 0.0)),)


def _f_gdn_post(o, z, nw):
    on = _heads(o, GDN_DV, lambda t: t * lax.rsqrt(jnp.mean(t * t, axis=-1, keepdims=True) + NORM_EPS))
    return (on * nw * _silu(z),)


def _f_loss(x, tgt, fw):
    y = x * lax.rsqrt(jnp.mean(x * x, axis=-1, keepdims=True) + NORM_EPS) * fw
    err = y - tgt
    return (0.5 * jnp.mean(err * err, axis=-1, keepdims=True),)


def _ada_mod_call(c_all, ada_w):
    n = ada_w.shape[2]

    def body(c_ref, w_ref, o_ref):
        ca = _silu(c_ref[...])
        for l in range(ada_w.shape[0]):
            o_ref[l] = _bdot(ca, w_ref[l])

    return pl.pallas_call(body, name="ada_mod", out_shape=SDS((ada_w.shape[0], N_DEV, n), F32),
                          compiler_params=_params(None, VMEM_MID))(c_all, ada_w)


def _ada_grad_call(c_all, dmod):
    nl, _, n = dmod.shape

    def body(c_ref, d_ref, o_ref):
        ca = _silu(c_ref[...])
        for l in range(nl):
            o_ref[l] = _hdot(ca, d_ref[l], TN)

    return pl.pallas_call(body, name="ada_grad", out_shape=SDS((nl, c_all.shape[1], n), F32),
                          compiler_params=_params(None, VMEM_MID))(c_all, dmod)


ADAM_ROWS = 512


def _adamw(g, w, m, v):
    m2 = ADAM_B1 * m + (1.0 - ADAM_B1) * g
    v2 = ADAM_B2 * v + (1.0 - ADAM_B2) * (g * g)
    m_hat = m2 / (1.0 - ADAM_B1 ** ADAM_STEP)
    v_hat = v2 / (1.0 - ADAM_B2 ** ADAM_STEP)
    return g, -ADAM_LR * (m_hat / (jnp.sqrt(v_hat) + ADAM_EPS) + ADAM_WD * w), m2, v2


def _adam_call(gs, w, m, v, name, rows=None, by_cols=False):
    n, r, cols = gs.shape
    if by_cols:
        blk = pl.BlockSpec((r, LANES), lambda i: (0, i))
        g_blk, grid = pl.BlockSpec((n, r, LANES), lambda i: (0, 0, i)), (cols // LANES,)
    else:
        rows = rows or ADAM_ROWS
        blk = pl.BlockSpec((rows, cols), lambda i: (i, 0))
        g_blk, grid = pl.BlockSpec((n, rows, cols), lambda i: (0, i, 0)), (r // rows,)

    def body(g_ref, w_ref, m_ref, v_ref, go_ref, d_ref, mo_ref, vo_ref):
        g = g_ref[0].astype(F32)
        for s in range(1, n):
            g = g + g_ref[s].astype(F32)
        for o_ref, val in zip((go_ref, d_ref, mo_ref, vo_ref), _adamw(g, w_ref[...], m_ref[...], v_ref[...])):
            o_ref[...] = val

    return pl.pallas_call(
        body, name=name, grid=grid, in_specs=[g_blk, blk, blk, blk],
        out_specs=[blk, blk, blk, blk], out_shape=[SDS((r, cols), F32)] * 4,
        compiler_params=_params(("parallel",), VMEM_MID),
    )(gs, w, m, v)


def _sum_call(gs, name, rows):
    n, r, _ = gs.shape

    def body(g_ref, o_ref):
        g = g_ref[0].astype(F32)
        for s in range(1, n):
            g = g + g_ref[s].astype(F32)
        o_ref[...] = g

    return pl.pallas_call(
        body, name=name, grid=(r // rows,),
        in_specs=[pl.BlockSpec((n, rows, LANES), lambda i: (0, i, 0))],
        out_specs=pl.BlockSpec((rows, LANES), lambda i: (i, 0)), out_shape=SDS((r, LANES), F32),
        compiler_params=_params(("parallel",), VMEM_MID),
    )(gs)


def _allgather_call(x_shard, name):
    m_per, n = x_shard.shape

    def body(x_ref, out_ref, send_sems, recv_sems, local_sem):
        x, y, c = lax.axis_index("x"), lax.axis_index("y"), lax.axis_index("c")
        me, sibling = (x, y, c), (x, y, 1 - c)
        chips = [(1 - x, y), (x, 1 - y), (1 - x, 1 - y)]

        def rows(px, py, pc):
            return out_ref.at[pl.ds((4 * px + 2 * py + pc) * m_per, m_per), :]

        def copy(k, block, to, src=None):
            return pltpu.make_async_remote_copy(
                src_ref=rows(*block) if src is None else src, dst_ref=rows(*block),
                send_sem=send_sems.at[k], recv_sem=recv_sems.at[k], device_id=to, device_id_type=pl.DeviceIdType.MESH)

        mine = pltpu.make_async_copy(x_ref, rows(*me), local_sem)
        mine.start()
        first = [copy(0, me, sibling, src=x_ref)]
        first += [copy(1 + j, me, (*chip, c), src=x_ref) for j, chip in enumerate(chips)]
        for cp in first:
            cp.start()
        passed = [copy(4 + j, (*chip, c), sibling) for j, chip in enumerate(chips)]
        for j, chip in enumerate(chips):
            copy(1 + j, (*chip, c), me).wait_recv()
            passed[j].start()
        copy(0, sibling, me).wait_recv()
        for j, chip in enumerate(chips):
            copy(4 + j, (*chip, 1 - c), me).wait_recv()
        for cp in first + passed:
            cp.wait_send()
        mine.wait()

    vmem = pl.BlockSpec(memory_space=pltpu.VMEM)
    return pl.pallas_call(
        body, name=name, out_shape=SDS((N_DEV * m_per, n), x_shard.dtype), in_specs=[vmem], out_specs=vmem,
        scratch_shapes=[pltpu.SemaphoreType.DMA((7,)), pltpu.SemaphoreType.DMA((7,)), pltpu.SemaphoreType.DMA],
    )(x_shard)


def _gather_weights_call(shards, name):
    nw = len(shards)

    def body(*refs):
        x_refs, out_refs = refs[:nw], refs[nw:2 * nw]
        send_sems, recv_sems, local_sems = refs[2 * nw:]
        x, y, c = lax.axis_index("x"), lax.axis_index("y"), lax.axis_index("c")
        me, sibling = (x, y, c), (x, y, 1 - c)
        chips = [(1 - x, y), (x, 1 - y), (1 - x, 1 - y)]

        def slot(w, px, py, pc):
            return out_refs[w].at[4 * px + 2 * py + pc]

        def copy(w, k, block, to, src=None):
            dst = slot(w, *block)
            return pltpu.make_async_remote_copy(
                src_ref=dst if src is None else src, dst_ref=dst, send_sem=send_sems.at[7 * w + k],
                recv_sem=recv_sems.at[7 * w + k], device_id=to, device_id_type=pl.DeviceIdType.MESH)

        mines = [pltpu.make_async_copy(x_refs[w], slot(w, *me), local_sems.at[w]) for w in range(nw)]
        for cp in mines:
            cp.start()
        first = [copy(w, 0, me, sibling, src=x_refs[w]) for w in range(nw)]
        first += [copy(w, 1 + j, me, (*chip, c), src=x_refs[w]) for w in range(nw) for j, chip in enumerate(chips)]
        for cp in first:
            cp.start()
        passed = []
        for w in range(nw):
            for j, chip in enumerate(chips):
                copy(w, 1 + j, (*chip, c), me).wait_recv()
                fwd = copy(w, 4 + j, (*chip, c), sibling)
                fwd.start()
                passed.append(fwd)
        for w in range(nw):
            copy(w, 0, sibling, me).wait_recv()
            for j, chip in enumerate(chips):
                copy(w, 4 + j, (*chip, 1 - c), me).wait_recv()
        for cp in first + passed:
            cp.wait_send()
        for cp in mines:
            cp.wait()

    hbm = pl.BlockSpec(memory_space=pl.ANY)
    return pl.pallas_call(
        body, name=name, out_shape=[SDS((N_DEV,) + s.shape, s.dtype) for s in shards],
        in_specs=[hbm] * nw, out_specs=[hbm] * nw,
        scratch_shapes=[pltpu.SemaphoreType.DMA((7 * nw,)), pltpu.SemaphoreType.DMA((7 * nw,)), pltpu.SemaphoreType.DMA((nw,))],
    )(*shards)


def _pair_exchange_call(grads, name):
    nw = len(grads)

    def body(*refs):
        g_refs, got_refs = refs[:nw], refs[nw:2 * nw]
        send_sems, recv_sems = refs[2 * nw:]
        x, y, c = lax.axis_index("x"), lax.axis_index("y"), lax.axis_index("c")
        copies = []
        for w in range(nw):
            for j in range(4):
                give = pltpu.make_async_remote_copy(
                    src_ref=g_refs[w].at[2 * j + 1 - c], dst_ref=got_refs[w].at[j], send_sem=send_sems.at[4 * w + j],
                    recv_sem=recv_sems.at[4 * w + j], device_id=(x, y, 1 - c), device_id_type=pl.DeviceIdType.MESH)
                give.start()
                copies.append(give)
        for cp in copies:
            cp.wait()

    hbm = pl.BlockSpec(memory_space=pl.ANY)
    return pl.pallas_call(
        body, name=name, out_shape=[SDS((4,) + g.shape[1:], g.dtype) for g in grads], in_specs=[hbm] * nw, out_specs=[hbm] * nw,
        scratch_shapes=[pltpu.SemaphoreType.DMA((4 * nw,)), pltpu.SemaphoreType.DMA((4 * nw,))],
    )(*grads)


def _chip_exchange_call(parts, name):
    nw = len(parts)

    def body(*refs):
        p_refs, out_refs = refs[:nw], refs[nw:2 * nw]
        send_sems, recv_sems = refs[2 * nw:]
        x, y, c = lax.axis_index("x"), lax.axis_index("y"), lax.axis_index("c")
        chips = [(1 - x, y), (x, 1 - y), (1 - x, 1 - y)]
        copies = []
        for w in range(nw):
            for j, (px, py) in enumerate(chips):
                give = pltpu.make_async_remote_copy(
                    src_ref=p_refs[w].at[2 * px + py], dst_ref=out_refs[w].at[j], send_sem=send_sems.at[3 * w + j],
                    recv_sem=recv_sems.at[3 * w + j], device_id=(px, py, c), device_id_type=pl.DeviceIdType.MESH)
                give.start()
                copies.append(give)
        for cp in copies:
            cp.wait()

    hbm = pl.BlockSpec(memory_space=pl.ANY)
    return pl.pallas_call(
        body, name=name, out_shape=[SDS((3,) + p.shape[1:], p.dtype) for p in parts], in_specs=[hbm] * nw, out_specs=[hbm] * nw,
        scratch_shapes=[pltpu.SemaphoreType.DMA((3 * nw,)), pltpu.SemaphoreType.DMA((3 * nw,))],
    )(*parts)


_HBM = pl.BlockSpec(memory_space=pltpu.HBM)
_SEM = pl.BlockSpec(memory_space=pltpu.SEMAPHORE)
_DATAFLOW = pltpu.SideEffectType.DATAFLOW_SIDE_EFFECTING


def _spread_start_call(srcs, per_peer, name, after):
    nw = len(srcs)
    lands = [lax.empty((N_DEV,) + (s.shape[1:] if per_peer else s.shape), s.dtype) for s in srcs]

    def body(*refs):
        src_refs, land_refs = refs[:nw], refs[nw:2 * nw]
        send_sems, recv_sems, token = refs[2 * nw + 1], refs[2 * nw + 2], refs[-1]
        x, y, c = lax.axis_index("x"), lax.axis_index("y"), lax.axis_index("c")
        me = 4 * x + 2 * y + c
        for w in range(nw):
            for k in range(1, N_DEV):
                px = 1 - x if k & 4 else x
                py = 1 - y if k & 2 else y
                pc = 1 - c if k & 1 else c
                src = src_refs[w].at[4 * px + 2 * py + pc] if per_peer else src_refs[w]
                pltpu.make_async_remote_copy(
                    src_ref=src, dst_ref=land_refs[w].at[me], send_sem=send_sems.at[w], recv_sem=recv_sems.at[w],
                    device_id=(px, py, pc), device_id_type=pl.DeviceIdType.MESH).start()
        token[...] = jnp.zeros_like(token)

    hbm = lambda a: pltpu.with_memory_space_constraint(a, pltpu.HBM)
    res = pl.pallas_call(
        body, name=name,
        out_shape=(pltpu.SemaphoreType.DMA((nw,)), pltpu.SemaphoreType.DMA((nw,)))
        + tuple(pltpu.HBM(s.shape, s.dtype) for s in srcs) + tuple(pltpu.HBM(l.shape, l.dtype) for l in lands)
        + (SDS((SUBLANES, LANES), F32),),
        in_specs=[_HBM] * (2 * nw) + [pl.BlockSpec(memory_space=pl.ANY)],
        out_specs=(_SEM, _SEM) + (_HBM,) * (2 * nw) + (pl.BlockSpec(memory_space=pltpu.VMEM),),
        input_output_aliases={i: i + 2 for i in range(2 * nw)},
        compiler_params=pltpu.CompilerParams(has_side_effects=_DATAFLOW),
    )(*[hbm(s) for s in srcs], *[hbm(l) for l in lands], after)
    return res[0], res[1], res[2:2 + nw], res[2 + nw:2 + 2 * nw], res[-1]


def _spread_wait_call(send_sems, recv_sems, srcs, lands, after, name):
    nw = len(lands)

    def body(*refs):
        land_refs = refs[nw:2 * nw]
        s_sems, r_sems = refs[2 * nw], refs[2 * nw + 1]
        x, y, c = lax.axis_index("x"), lax.axis_index("y"), lax.axis_index("c")
        for w in range(nw):
            seven = land_refs[w].at[pl.ds(0, N_DEV - 1)]
            all_seven = pltpu.make_async_remote_copy(
                src_ref=seven, dst_ref=seven, send_sem=s_sems.at[w], recv_sem=r_sems.at[w],
                device_id=(x, y, c), device_id_type=pl.DeviceIdType.MESH)
            all_seven.wait_send()
            all_seven.wait_recv()

    res = pl.pallas_call(
        body, name=name,
        out_shape=tuple(pltpu.HBM(s.shape, s.dtype) for s in srcs) + tuple(pltpu.HBM(l.shape, l.dtype) for l in lands),
        in_specs=[_HBM] * (2 * nw) + [_SEM, _SEM, pl.BlockSpec(memory_space=pl.ANY)], out_specs=(_HBM,) * (2 * nw),
        input_output_aliases={i: i for i in range(2 * nw)},
        compiler_params=pltpu.CompilerParams(has_side_effects=_DATAFLOW),
    )(*srcs, *lands, send_sems, recv_sems, after)
    return res[:nw], res[nw:]


def _pair_sum_call(g, got, core, name):
    _, k, n = got.shape
    tr = _tile(k, 256)

    def body(c_ref, g_ref, got_ref, o_ref):
        o_ref[...] = (g_ref[...] + got_ref[...]).astype(o_ref.dtype)

    spec = pltpu.PrefetchScalarGridSpec(
        num_scalar_prefetch=1, grid=(4, k // tr),
        in_specs=[pl.BlockSpec((1, tr, n), lambda j, i, c: (2 * j + c[0], i, 0)), pl.BlockSpec((1, tr, n), lambda j, i, c: (j, i, 0))],
        out_specs=pl.BlockSpec((1, tr, n), lambda j, i, c: (j, i, 0)))
    return pl.pallas_call(body, name=name, grid_spec=spec, out_shape=SDS(got.shape, BF16),
                          compiler_params=_params(("parallel", "parallel"), VMEM_MID))(core, g, got)


def _adam_own_call(pair, chip, recv, w, m, v, name, rows):
    _, r, cols = recv.shape

    def body(chip_ref, p_ref, g_ref, w_ref, m_ref, v_ref, go_ref, d_ref, mo_ref, vo_ref):
        g = ((p_ref[0].astype(F32) + g_ref[0].astype(F32)) + g_ref[1].astype(F32)) + g_ref[2].astype(F32)
        for o_ref, val in zip((go_ref, d_ref, mo_ref, vo_ref), _adamw(g, w_ref[...], m_ref[...], v_ref[...])):
            o_ref[...] = val

    blk = pl.BlockSpec((rows, cols), lambda i, s: (i, 0))
    spec = pltpu.PrefetchScalarGridSpec(
        num_scalar_prefetch=1, grid=(r // rows,),
        in_specs=[pl.BlockSpec((1, rows, cols), lambda i, s: (s[0], i, 0)), pl.BlockSpec((3, rows, cols), lambda i, s: (0, i, 0)),
                  blk, blk, blk],
        out_specs=[blk, blk, blk, blk])
    return pl.pallas_call(body, name=name, grid_spec=spec, out_shape=[SDS((r, cols), F32)] * 4,
                          compiler_params=_params(("parallel",), VMEM_MID))(chip, pair, recv, w, m, v)


def _join_cols_call(w8, name):
    _, k, n = w8.shape
    tk = _tile(k, 256)

    def body(w_ref, o_ref):
        for s in range(N_DEV):
            o_ref[:, n * s:n * (s + 1)] = w_ref[s]

    return pl.pallas_call(body, name=name, grid=(k // tk,), in_specs=[pl.BlockSpec((N_DEV, tk, n), lambda i: (0, i, 0))],
                          out_specs=pl.BlockSpec((tk, N_DEV * n), lambda i: (i, 0)), out_shape=SDS((k, N_DEV * n), w8.dtype),
                          compiler_params=_params(("parallel",), VMEM_MID))(w8)


def _split_cols_call(g, name, dtype):
    k, n8 = g.shape
    n = n8 // N_DEV
    tk = _tile(k, 256)

    def body(g_ref, o_ref):
        for s in range(N_DEV):
            o_ref[s] = g_ref[:, n * s:n * (s + 1)].astype(dtype)

    return pl.pallas_call(body, name=name, grid=(k // tk,), in_specs=[pl.BlockSpec((tk, n8), lambda i: (i, 0))],
                          out_specs=pl.BlockSpec((N_DEV, tk, n), lambda i: (0, i, 0)), out_shape=SDS((N_DEV, k, n), dtype),
                          compiler_params=_params(("parallel",), VMEM_MID))(g)


def _pack(parts, rows_multiple):
    flat = jnp.concatenate([p.reshape(-1) for p in parts])
    unit = rows_multiple * LANES
    padded = -(-flat.shape[0] // unit) * unit
    flat = jnp.concatenate([flat, jnp.zeros((padded - flat.shape[0],), F32)])
    return flat.reshape(-1, LANES)


def _groups_last(a):
    x, y = a.shape[-2:]
    return jnp.transpose(a.reshape(S5_GROUPS, x, y), (1, 2, 0)).reshape(x * y, S5_GROUPS)


def _groups_first(a, shape):
    x, y = shape[-2:]
    return jnp.transpose(a.reshape(x, y, S5_GROUPS), (2, 0, 1)).reshape(shape)


def _unpack(buf, shapes):
    flat = buf.reshape(-1)
    out, off = [], 0
    for s in shapes:
        n = math.prod(s)
        out.append(flat[off:off + n].reshape(s))
        off += n
    return out


def _row_tile(L):
    return 256 if L % 256 == 0 else L


def _layer0_mix(diff, const):
    x, mod, norm_w, lam_re, lam_im, log_dt, b_re, b_im, c_re, c_im, s5_d, *slots = diff
    ada_b, weights = const
    L = x.shape[0]
    tm = _row_tile(L)
    mods = mod.reshape(2, 1, D_MODEL)
    biases = ada_b.reshape(2, 1, D_MODEL)
    op_ln0 = make_rowwise(_f_lnmod, "ln0", tm, 1, 5, pass_first=True)
    h, x = op_ln0((x,), (norm_w.reshape(1, D_MODEL), mods[1], mods[0], biases[1], biases[0]))
    u, z = make_proj("s5_in")(h, tuple(weights), tuple(slots))
    blocks = _s5_block_params(lam_re, lam_im, log_dt, b_re, b_im, c_re, c_im)
    y2 = make_s5_core(min(S5_TL, L))(u, *blocks, s5_d.reshape(1, D_INNER))
    return x, y2, z


def _layer0_out(diff, weights):
    y2, z, *slots = diff
    tm = _row_tile(y2.shape[0])
    t, y2 = make_mm("s5_glu", pass_input=True)(y2, weights[0], slots[0])
    (y4,) = make_rowwise(_f_s5_gate, "s5_gate", tm, 3, 0)((y2, t, z), ())
    return make_mm("s5_out")(y4, weights[1], slots[1])


def _f_res_lnmod(x, o, gate, bgate, nw, sc, sh, bsc, bsh):
    (x1,) = _f_res(x, o, gate, bgate)
    return _f_lnmod(x1, nw, sc, sh, bsc, bsh) + (x1,)


def _f_res_loss(x, y, tgt, gate, bgate, fw):
    return _f_loss(_f_res(x, y, gate, bgate)[0], tgt, fw)


def _layer1_loss(diff, const):
    x, o, gate0, mod, norm_w, conv_w, a_log, dt_bias, gdn_nw, final_nw, *slots = diff
    tgt, bgate0, ada_b, weights = const
    L = x.shape[0]
    tm = _row_tile(L)
    mods = mod.reshape(3, 1, D_MODEL)
    biases = ada_b.reshape(3, 1, D_MODEL)
    h, x1 = make_rowwise(_f_res_lnmod, "res0_ln1", tm, 2, 7)(
        (x, o), (gate0.reshape(1, D_MODEL), bgate0.reshape(1, D_MODEL), norm_w.reshape(1, D_MODEL), mods[1], mods[0], biases[1], biases[0]))
    q0, k0, v0, gz, ba = make_proj("gdn_in", w_rows=True)(h, tuple(weights[0:5]), tuple(slots[0:5]))
    cw = jnp.concatenate([conv_w, jnp.zeros((SUBLANES - GDN_CONV, GDN_CONV_CH), F32)], axis=0)
    q = make_conv_act(lambda t: _l2n(_silu(t)) * (GDN_DK ** -0.5), "gdn_conv_q")(q0, cw[:, :GDN_QK])
    k = make_conv_act(lambda t: _l2n(_silu(t)), "gdn_conv_k")(k0, cw[:, GDN_QK:2 * GDN_QK])
    v = make_conv_act(_silu, "gdn_conv_v")(v0, cw[:, 2 * GDN_QK:])
    pad = jnp.zeros((LANES - 2 * GDN_HEADS,), F32)
    alog_row = jnp.concatenate([jnp.zeros((GDN_HEADS,), F32), a_log, pad]).reshape(1, LANES)
    dtb_row = jnp.concatenate([jnp.zeros((GDN_HEADS,), F32), dt_bias, pad]).reshape(1, LANES)
    (bg,) = make_rowwise(_f_betag, "gdn_bg", tm, 1, 2)((ba,), (alog_row, dtb_row))
    nw_row = jnp.tile(gdn_nw, GDN_HEADS).reshape(1, D_INNER)
    on = gdn_scan(*gdn_prep(q, k, v, bg), gz, nw_row)
    y = make_mm("gdn_out")(on, weights[5], slots[5])
    (lt,) = make_rowwise(_f_res_loss, "res1_loss", tm, 3, 3)((x1, y, tgt), (mods[2], biases[2], final_nw.reshape(1, D_MODEL)))
    return jnp.sum(lt)


VEC_NAMES = ("ada_b", "norm_w", "s5_lambda_re", "s5_lambda_im", "s5_log_dt", "s5_d", "gdn_a_log", "gdn_dt_bias", "final_norm_w")
MAT_NAMES = ("s5_b_re", "s5_b_im", "s5_c_re", "s5_c_im")
S5_BIG = ("s5_w_in", "s5_w_glu", "s5_w_out")
GDN_BIG = ("gdn_w_in", "gdn_w_out")
BIG_NAMES = S5_BIG + GDN_BIG
WEIGHT_ORDER = ("ada_w", "ada_b", "norm_w", "s5_w_in", "s5_lambda_re", "s5_lambda_im", "s5_log_dt", "s5_b_re", "s5_b_im",
                "s5_c_re", "s5_c_im", "s5_d", "s5_w_glu", "s5_w_out", "gdn_w_in", "gdn_conv_w", "gdn_a_log", "gdn_dt_bias",
                "gdn_norm_w", "gdn_w_out", "final_norm_w")


def _step(x, c, W, M, V, tgt):
    L = x.shape[1]
    ix, iy, ic = lax.axis_index("x"), lax.axis_index("y"), lax.axis_index("c")
    me = 4 * ix + 2 * iy + ic
    n_ada = W["ada_w"].shape[2]
    n_conv = W["gdn_conv_w"].shape[2]
    n_gnw = W["gdn_norm_w"].shape[1]

    g1 = _allgather_call(_pack([c, W["gdn_conv_w"], W["gdn_norm_w"]], SUBLANES), "gather_small_in")
    g1 = g1.reshape(N_DEV, -1)
    c_all = g1[:, :D_MODEL]
    conv_w = g1[:, D_MODEL:D_MODEL + GDN_CONV * n_conv].reshape(N_DEV, GDN_CONV, n_conv).transpose(1, 0, 2).reshape(GDN_CONV, -1)
    gdn_nw = g1[:, D_MODEL + GDN_CONV * n_conv:D_MODEL + GDN_CONV * n_conv + n_gnw].reshape(-1)
    mod_part = _ada_mod_call(c_all, W["ada_w"])
    g2 = _allgather_call(_pack([mod_part], SUBLANES), "gather_mod").reshape(N_DEV, -1)
    mod_all = g2[:, :2 * N_DEV * n_ada].reshape(N_DEV, 2, N_DEV, n_ada)
    mod_raw = lax.dynamic_index_in_dim(mod_all, me, axis=2, keepdims=False)
    mod_raw = mod_raw.transpose(1, 0, 2).reshape(2, 3 * D_MODEL)

    shard = lambda n: W[n][0].astype(BF16)
    (w_in5_parts,) = _gather_weights_call([shard("s5_w_in")], "gather_s5_w_in")
    late = _spread_start_call([shard("s5_w_glu"), shard("s5_w_out")], False, "gather_s5_late_start", w_in5_parts)
    turned = lambda a: jnp.transpose(a[0])
    g_send, g_recv, g_srcs, g_lands, g_token = _spread_start_call(
        [turned(W["gdn_w_in"]).astype(BF16), shard("gdn_w_out")], False, "gather_gdn_start", late[4])
    w_in5 = _join_cols_call(w_in5_parts, "join_s5_w_in")
    slot = lambda *s: jnp.zeros(s, F32)
    two = 2 * D_MODEL
    diff_mix = (x[0], mod_raw[0, :two] + g_token[0, 0], W["norm_w"][0], W["s5_lambda_re"][0], W["s5_lambda_im"][0], W["s5_log_dt"][0],
                W["s5_b_re"][0], W["s5_b_im"][0], W["s5_c_re"][0], W["s5_c_im"][0], W["s5_d"][0],
                slot(D_MODEL, D_INNER), slot(D_MODEL, D_INNER))

    (xp, y2, z5), vjp_mix = jax.vjp(lambda d: _layer0_mix(d, (W["ada_b"][0, :two], (w_in5[:, :D_INNER], w_in5[:, D_INNER:]))), diff_mix)
    l_srcs, l_lands = _spread_wait_call(late[0], late[1], late[2], late[3], y2, "gather_s5_late_wait")
    w_glu, w_o5 = [lax.dynamic_update_slice(land, src[None], (me, 0, 0)).reshape(-1, src.shape[1]) for land, src in zip(l_lands, l_srcs)]
    diff_out = (y2, z5, slot(D_INNER, D_INNER), slot(D_INNER, D_MODEL))
    o5, vjp_out = jax.vjp(lambda d: _layer0_out(d, (w_glu, w_o5)), diff_out)
    g_srcs, g_lands = _spread_wait_call(g_send, g_recv, g_srcs, g_lands, o5, "gather_gdn_wait")
    gdn_full = [lax.dynamic_update_slice(land, src[None], (me, 0, 0)) for land, src in zip(g_lands, g_srcs)]
    w_ing = gdn_full[0].reshape(GDN_PROJ, D_MODEL)
    w_ba = jnp.concatenate([w_ing[GDN_CONV_CH + D_INNER:], jnp.zeros((LANES - 2 * GDN_HEADS, D_MODEL), BF16)], axis=0)
    weights1 = (w_ing[:GDN_QK], w_ing[GDN_QK:2 * GDN_QK], w_ing[2 * GDN_QK:GDN_CONV_CH],
                w_ing[GDN_CONV_CH:GDN_CONV_CH + D_INNER], w_ba, gdn_full[1].reshape(D_INNER, D_MODEL))
    slots1 = tuple(jnp.zeros(w.shape, F32) for w in weights1)
    diff1 = (xp, o5, mod_raw[0, two:], mod_raw[1], W["norm_w"][1], conv_w, W["gdn_a_log"][0], W["gdn_dt_bias"][0], gdn_nw,
             W["final_norm_w"], *slots1)
    loss_local, vjp1 = jax.vjp(lambda d: _layer1_loss(d, (tgt[0], W["ada_b"][0, two:], W["ada_b"][1], weights1)), diff1)
    ((dxp, do5, dmod_gate, dmod1, d_norm_w1, d_conv, d_alog, d_dtb, d_gnw, d_fnw, d_wq, d_wk, d_wv, d_wgz, d_wba, d_wog),) = vjp1(
        jnp.ones((), F32))
    loss = lax.psum(loss_local, MESH_AXES)

    rows = lambda d: d.reshape(N_DEV, d.shape[0] // N_DEV, d.shape[1])
    d_ing = jnp.concatenate([d_wq, d_wk, d_wv, d_wgz, d_wba[:2 * GDN_HEADS]], axis=0).astype(BF16).reshape(N_DEV, -1, D_MODEL)
    s_send, s_recv, s_srcs, s_lands, s_token = _spread_start_call([d_ing, rows(d_wog).astype(BF16)], True, "scatter_gdn_start", dxp)
    ((dy2, dz5, d_wglu, d_wo5),) = vjp_out(do5.at[0, 0].add(s_token[0, 0]))
    t_send, t_recv, t_srcs, t_lands, t_token = _spread_start_call(
        [rows(d_wglu).astype(BF16), rows(d_wo5).astype(BF16)], True, "scatter_s5_late_start", dy2)
    ((dx, dmod_ss, d_norm_w0, d_lre, d_lim, d_logdt, d_bre, d_bim, d_cre, d_cim, d_s5d, d_wu, d_wz),) = vjp_mix(
        (dxp.at[0, 0].add(t_token[0, 0]), dy2, dz5))
    dmod = jnp.stack([jnp.concatenate([dmod_ss, dmod_gate]), dmod1])
    d_norm_w = jnp.stack([d_norm_w0, d_norm_w1])
    vec_parts = [dmod, d_norm_w, d_lre, d_lim, d_logdt, d_s5d, d_alog, d_dtb, d_fnw]
    tail_parts = [d_conv, d_gnw]
    mat_parts = [_groups_last(d) for d in (d_bre, d_bim, d_cre, d_cim)]
    n_vec = sum(math.prod(p.shape) for p in vec_parts)
    m_send, m_recv, m_srcs, m_lands, m_token = _spread_start_call(
        [_pack(vec_parts + tail_parts, ADAM_ROWS), _pack(mat_parts, SUBLANES).astype(BF16)], False, "gather_small_grads_start", dx)
    d_in5 = _split_cols_call(jnp.concatenate([d_wu.at[0, 0].add(m_token[0, 0]), d_wz], axis=1), "split_s5_w_in", F32)
    (got,) = _pair_exchange_call([d_in5], "scatter_s5_pair")
    core = jnp.reshape(ic, (1,)).astype(jnp.int32)
    chip = jnp.reshape(2 * ix + iy, (1,)).astype(jnp.int32)
    pair = _pair_sum_call(d_in5, got, core, "pair_sum_s5_w_in")
    (recv,) = _chip_exchange_call([pair], "scatter_s5_chips")
    big = {"s5_w_in": _adam_own_call(pair, chip, recv, W["s5_w_in"][0], M["s5_w_in"][0], V["s5_w_in"][0], "adam_s5_w_in", 128)}
    t_srcs, t_lands = _spread_wait_call(t_send, t_recv, t_srcs, t_lands, dx, "scatter_s5_late_wait")
    s_srcs, s_lands = _spread_wait_call(s_send, s_recv, s_srcs, s_lands, t_lands[0], "scatter_gdn_wait")
    for land, src, n in zip(tuple(t_lands) + tuple(s_lands), tuple(t_srcs) + tuple(s_srcs), ("s5_w_glu", "s5_w_out") + GDN_BIG):
        mine = lax.dynamic_index_in_dim(src, me, 0, keepdims=True)
        parts = lax.dynamic_update_slice(land, mine, (me, 0, 0))
        if n == "gdn_w_in":
            outs = _adam_call(parts, turned(W[n]), turned(M[n]), turned(V[n]), "adam_" + n, by_cols=True)
            big[n] = [jnp.transpose(o) for o in outs]
        else:
            big[n] = _adam_call(parts, W[n][0], M[n][0], V[n][0], "adam_" + n, rows=_tile(W[n].shape[1], 128))
    big = [[o[None] for o in big[n]] for n in BIG_NAMES]

    m_srcs, m_lands = _spread_wait_call(m_send, m_recv, m_srcs, m_lands, big[0][0], "gather_small_grads_wait")
    sg_vec, sg_mat = [lax.dynamic_update_slice(land, src[None], (me, 0, 0)) for land, src in zip(m_lands, m_srcs)]
    tot_vec = _sum_call(sg_vec, "sum_vec_grads", ADAM_ROWS)
    tot_mat = _sum_call(sg_mat, "sum_mat_grads", ADAM_ROWS)
    g_conv, g_gnw = _unpack(tot_vec.reshape(-1)[n_vec:], [d_conv.shape, d_gnw.shape])
    g_conv_mine = lax.dynamic_slice_in_dim(g_conv, me * n_conv, n_conv, axis=1)
    g_gnw_mine = lax.dynamic_slice_in_dim(g_gnw, me * n_gnw, n_gnw, axis=0)
    vec_names = VEC_NAMES + ("gdn_conv_w", "gdn_norm_w")
    vec_g = _pack([tot_vec.reshape(-1)[:n_vec], g_conv_mine, g_gnw_mine], ADAM_ROWS)
    vec = _adam_call(vec_g[None], _pack([W[n] for n in vec_names], ADAM_ROWS), _pack([M[n] for n in vec_names], ADAM_ROWS),
                     _pack([V[n] for n in vec_names], ADAM_ROWS), "adam_vec")
    vec = [_unpack(b, [W[n].shape for n in vec_names]) for b in vec]
    mats = []
    for name, g_mat in zip(MAT_NAMES, _unpack(tot_mat, [p.shape for p in mat_parts])):
        outs = _adam_call(g_mat[None], _groups_last(W[name]), _groups_last(M[name]), _groups_last(V[name]), "adam_" + name)
        mats.append([_groups_first(o, W[name].shape) for o in outs])

    dmod_all = sg_vec[:, :2 * 3 * D_MODEL // LANES].reshape(N_DEV, 2, N_DEV, n_ada // LANES, LANES)
    dmod_mine = lax.dynamic_index_in_dim(dmod_all, me, axis=2, keepdims=False).transpose(1, 0, 2, 3).reshape(2, N_DEV, n_ada)
    g_ada_w = _ada_grad_call(c_all, dmod_mine)
    ada = _adam_call(g_ada_w.reshape(1, -1, LANES), W["ada_w"].reshape(-1, LANES), M["ada_w"].reshape(-1, LANES),
                     V["ada_w"].reshape(-1, LANES), "adam_ada")
    ada = [a.reshape(W["ada_w"].shape) for a in ada]

    res = {}
    for i, n in enumerate(BIG_NAMES):
        res[n] = big[i]
    for i, n in enumerate(vec_names):
        res[n] = [b[i] for b in vec]
    for i, n in enumerate(MAT_NAMES):
        res[n] = mats[i]
    res["ada_w"] = ada
    outs = [loss, dx[None]]
    for j in range(4):
        outs += [res[n][j] for n in WEIGHT_ORDER]
    return tuple(outs)


def kernel(x, c, ada_w, ada_b, norm_w, s5_w_in, s5_lambda_re, s5_lambda_im, s5_log_dt, s5_b_re, s5_b_im, s5_c_re, s5_c_im, s5_d, s5_w_glu, s5_w_out, gdn_w_in, gdn_conv_w, gdn_a_log, gdn_dt_bias, gdn_norm_w, gdn_w_out, final_norm_w, loss_target, m_ada_w, m_ada_b, m_norm_w, m_s5_w_in, m_s5_lambda_re, m_s5_lambda_im, m_s5_log_dt, m_s5_b_re, m_s5_b_im, m_s5_c_re, m_s5_c_im, m_s5_d, m_s5_w_glu, m_s5_w_out, m_gdn_w_in, m_gdn_conv_w, m_gdn_a_log, m_gdn_dt_bias, m_gdn_norm_w, m_gdn_w_out, m_final_norm_w, v_ada_w, v_ada_b, v_norm_w, v_s5_w_in, v_s5_lambda_re, v_s5_lambda_im, v_s5_log_dt, v_s5_b_re, v_s5_b_im, v_s5_c_re, v_s5_c_im, v_s5_d, v_s5_w_glu, v_s5_w_out, v_gdn_w_in, v_gdn_conv_w, v_gdn_a_log, v_gdn_dt_bias, v_gdn_norm_w, v_gdn_w_out, v_final_norm_w):
    W = dict(ada_w=ada_w, ada_b=ada_b, norm_w=norm_w, s5_w_in=s5_w_in, s5_lambda_re=s5_lambda_re, s5_lambda_im=s5_lambda_im,
             s5_log_dt=s5_log_dt, s5_b_re=s5_b_re, s5_b_im=s5_b_im, s5_c_re=s5_c_re, s5_c_im=s5_c_im, s5_d=s5_d,
             s5_w_glu=s5_w_glu, s5_w_out=s5_w_out, gdn_w_in=gdn_w_in, gdn_conv_w=gdn_conv_w, gdn_a_log=gdn_a_log,
             gdn_dt_bias=gdn_dt_bias, gdn_norm_w=gdn_norm_w, gdn_w_out=gdn_w_out, final_norm_w=final_norm_w)
    M = dict(ada_w=m_ada_w, ada_b=m_ada_b, norm_w=m_norm_w, s5_w_in=m_s5_w_in, s5_lambda_re=m_s5_lambda_re,
             s5_lambda_im=m_s5_lambda_im, s5_log_dt=m_s5_log_dt, s5_b_re=m_s5_b_re, s5_b_im=m_s5_b_im, s5_c_re=m_s5_c_re,
             s5_c_im=m_s5_c_im, s5_d=m_s5_d, s5_w_glu=m_s5_w_glu, s5_w_out=m_s5_w_out, gdn_w_in=m_gdn_w_in,
             gdn_conv_w=m_gdn_conv_w, gdn_a_log=m_gdn_a_log, gdn_dt_bias=m_gdn_dt_bias, gdn_norm_w=m_gdn_norm_w,
             gdn_w_out=m_gdn_w_out, final_norm_w=m_final_norm_w)
    V = dict(ada_w=v_ada_w, ada_b=v_ada_b, norm_w=v_norm_w, s5_w_in=v_s5_w_in, s5_lambda_re=v_s5_lambda_re,
             s5_lambda_im=v_s5_lambda_im, s5_log_dt=v_s5_log_dt, s5_b_re=v_s5_b_re, s5_b_im=v_s5_b_im, s5_c_re=v_s5_c_re,
             s5_c_im=v_s5_c_im, s5_d=v_s5_d, s5_w_glu=v_s5_w_glu, s5_w_out=v_s5_w_out, gdn_w_in=v_gdn_w_in,
             gdn_conv_w=v_gdn_conv_w, gdn_a_log=v_gdn_a_log, gdn_dt_bias=v_gdn_dt_bias, gdn_norm_w=v_gdn_norm_w,
             gdn_w_out=v_gdn_w_out, final_norm_w=v_final_norm_w)
    return _step(x, c, W, M, V, loss_target)
```

```python
import functools
import math

import jax
import jax.numpy as jnp
from jax import lax
from jax.experimental import pallas as pl
from jax.experimental.pallas import tpu as pltpu

F32 = jnp.float32
BF16 = jnp.bfloat16
SDS = jax.ShapeDtypeStruct

D_MODEL = 1024
D_INNER = 2048
NORM_EPS = 1e-6
S5_GROUP = 16
S5_GROUPS = 128
S5_STATE = 64
GDN_HEADS = 8
GDN_DK = 128
GDN_DV = 256
GDN_CONV = 4
GDN_CHUNK = 64
GDN_QK = 1024
GDN_CONV_CH = 4096
GDN_PROJ = 6160
ADAM_LR = 0.001
ADAM_B1 = 0.9
ADAM_B2 = 0.999
ADAM_EPS = 1e-08
ADAM_WD = 0.01
ADAM_STEP = 10

N_DEV = 8
LANES = 128
SUBLANES = 8
VMEM_BIG = 56 << 20
VMEM_MID = 40 << 20
S5_GB = 8
S5_TL = 2048
MESH_AXES = ("x", "y", "c")


def _params(sem, vmem=None):
    return pltpu.CompilerParams(dimension_semantics=sem, vmem_limit_bytes=vmem)


def _bdot(a, b, dims=(((1,), (0,)), ((), ()))):
    return lax.dot_general(a.astype(BF16), b.astype(BF16), dims, preferred_element_type=F32)


def _hdot(a, b, dims=(((1,), (0,)), ((), ()))):
    return lax.dot_general(a, b, dims, preferred_element_type=F32, precision=lax.Precision.HIGHEST)


_BNN = (((2,), (1,)), ((0,), (0,)))
_BNT = (((2,), (2,)), ((0,), (0,)))
_BTN = (((1,), (1,)), ((0,), (0,)))


@jax.custom_vjp
def _unit_lower_inverse(a):
    c = a.shape[-1]
    ri = lax.broadcasted_iota(jnp.int32, a.shape, 1)
    ci = lax.broadcasted_iota(jnp.int32, a.shape, 2)
    n = -a
    t = (ri == ci).astype(F32) + n
    for _ in range(int(math.log2(c)) - 1):
        n = _hdot(n, n, _BNN)
        t = t + _hdot(t, n, _BNN)
    return t


def _unit_lower_inverse_fwd(a):
    t = _unit_lower_inverse(a)
    return t, t


def _unit_lower_inverse_bwd(t, g):
    return (-_hdot(_hdot(t, g, _BTN), t, _BNT),)


_unit_lower_inverse.defvjp(_unit_lower_inverse_fwd, _unit_lower_inverse_bwd)


NN = (((1,), (0,)), ((), ()))
NT = (((1,), (1,)), ((), ()))
TN = (((0,), (0,)), ((), ()))


def _tile(n, pref):
    for t in (pref, 512, 256, 128):
        if t <= n and n % t == 0:
            return t
    return n


def _matmul(a, b, mode, name, add=None):
    if mode == "nn":
        (m, k), (_, n) = a.shape, b.shape
    elif mode == "nt":
        (m, k), (n, _) = a.shape, b.shape
    else:
        (k, m), (_, n) = a.shape, b.shape
    tm, tn, tk = _tile(m, 1024), _tile(n, 512), (k if k <= 2048 else _tile(k, 512))
    if mode == "tn":
        tm, tn, tk = _tile(m, 1024), _tile(n, 1024), _tile(k, 1024)
    nk = k // tk
    dims = {"nn": NN, "nt": NT, "tn": TN}[mode]

    def body(a_ref, b_ref, *rest):
        o_ref, acc_ref = rest[-2], rest[-1]
        part = _bdot(a_ref[...], b_ref[...], dims)
        if nk == 1:
            o_ref[...] = part if add is None else part + rest[0][...]
            return
        kk = pl.program_id(2)

        @pl.when(kk == 0)
        def _():
            acc_ref[...] = part if add is None else part + rest[0][...]

        @pl.when(kk > 0)
        def _():
            acc_ref[...] += part

        @pl.when(kk == nk - 1)
        def _():
            o_ref[...] = acc_ref[...]

    a_spec = pl.BlockSpec((tk, tm), lambda i, j, q: (q, i)) if mode == "tn" else pl.BlockSpec((tm, tk), lambda i, j, q: (i, q))
    b_spec = pl.BlockSpec((tn, tk), lambda i, j, q: (j, q)) if mode == "nt" else pl.BlockSpec((tk, tn), lambda i, j, q: (q, j))
    o_spec = pl.BlockSpec((tm, tn), lambda i, j, q: (i, j))
    return pl.pallas_call(
        body, name=name, grid=(m // tm, n // tn, nk),
        in_specs=[a_spec, b_spec] + ([] if add is None else [o_spec]), out_specs=o_spec,
        out_shape=SDS((m, n), F32), scratch_shapes=[pltpu.VMEM((tm, tn), F32)],
        compiler_params=_params(("parallel", "parallel", "arbitrary"), VMEM_MID),
    )(a, b, *([] if add is None else [add]))


def make_mm(name, pass_input=False):
    def primal(a, w):
        out = _matmul(a, w, "nn", name + "_fwd")
        return (out, a) if pass_input else out

    @jax.custom_vjp
    def mm(a, w, grad_slot):
        return primal(a, w)

    def fwd(a, w, grad_slot):
        return primal(a, w), (a, w)

    def bwd(res, g):
        a, w = res
        g, g_other = g if pass_input else (g, None)
        return _matmul(g, w, "nt", name + "_dx", add=g_other), jnp.zeros_like(w), _matmul(a, g, "tn", name + "_dw")

    mm.defvjp(fwd, bwd)
    return mm


PROJ_ROWS = 256


def _proj_fwd_call(a, ws, name, w_rows):
    m, k = a.shape
    tm = _tile(m, PROJ_ROWS)
    nw = len(ws)
    widths = [w.shape[0] if w_rows else w.shape[1] for w in ws]

    def body(*refs):
        ab = refs[0][...].astype(BF16)
        for w_ref, o_ref in zip(refs[1:1 + nw], refs[1 + nw:]):
            o_ref[...] = lax.dot_general(ab, w_ref[...], NT if w_rows else NN, preferred_element_type=F32)

    return pl.pallas_call(
        body, name=name, grid=(m // tm,),
        in_specs=[pl.BlockSpec((tm, k), lambda i: (i, 0))] + [pl.BlockSpec(w.shape, lambda i: (0, 0)) for w in ws],
        out_specs=[pl.BlockSpec((tm, n), lambda i: (i, 0)) for n in widths],
        out_shape=[SDS((m, n), F32) for n in widths],
        compiler_params=_params(("parallel",), VMEM_BIG),
    )(a, *ws)


def _proj_dx_call(gs, ws, name, w_rows):
    m = gs[0].shape[0]
    k = ws[0].shape[1] if w_rows else ws[0].shape[0]
    tm = _tile(m, PROJ_ROWS)
    nw = len(ws)

    def body(*refs):
        acc = None
        for g_ref, w_ref in zip(refs[:nw], refs[nw:2 * nw]):
            part = _bdot(g_ref[...], w_ref[...], NN if w_rows else NT)
            acc = part if acc is None else acc + part
        refs[2 * nw][...] = acc

    return pl.pallas_call(
        body, name=name, grid=(m // tm,),
        in_specs=[pl.BlockSpec((tm, g.shape[1]), lambda i: (i, 0)) for g in gs] + [pl.BlockSpec(w.shape, lambda i: (0, 0)) for w in ws],
        out_specs=pl.BlockSpec((tm, k), lambda i: (i, 0)), out_shape=SDS((m, k), F32),
        compiler_params=_params(("parallel",), VMEM_BIG),
    )(*gs, *ws)


def make_proj(name, w_rows=False):
    @jax.custom_vjp
    def proj(a, ws, grad_slots):
        return tuple(_proj_fwd_call(a, ws, name + "_fwd", w_rows))

    def fwd(a, ws, grad_slots):
        return tuple(_proj_fwd_call(a, ws, name + "_fwd", w_rows)), (a, ws)

    def bwd(res, gs):
        a, ws = res
        dws = tuple(_matmul(g, a, "tn", "%s_dw%d" % (name, i)) if w_rows else _matmul(a, g, "tn", "%s_dw%d" % (name, i))
                    for i, g in enumerate(gs))
        return _proj_dx_call(tuple(gs), ws, name + "_dx", w_rows), tuple(jnp.zeros_like(w) for w in ws), dws

    proj.defvjp(fwd, bwd)
    return proj


def make_rowwise(f, name, tm, n_rows, n_params, vmem=VMEM_MID, pass_first=False):
    def specs_of(arrs, blocked):
        if blocked:
            return [pl.BlockSpec((tm, a.shape[1]), lambda i: (i, 0)) for a in arrs]
        return [pl.BlockSpec(a.shape, lambda i: (0, 0)) for a in arrs]

    def out_structs(rows, params):
        blk = [SDS((tm, r.shape[1]), r.dtype) for r in rows] + [SDS(p.shape, p.dtype) for p in params]
        return jax.eval_shape(f, *blk)

    def run_fwd(rows, params):
        L = rows[0].shape[0]
        outs = out_structs(rows, params)

        def body(*refs):
            ins = [r[...] for r in refs[:n_rows + n_params]]
            res = f(*ins)
            for o_ref, val in zip(refs[n_rows + n_params:], res):
                o_ref[...] = val

        return pl.pallas_call(
            body, name=name + "_fwd", grid=(L // tm,),
            in_specs=specs_of(rows, True) + specs_of(params, False),
            out_specs=[pl.BlockSpec((tm, o.shape[1]), lambda i: (i, 0)) for o in outs],
            out_shape=[SDS((L, o.shape[1]), o.dtype) for o in outs],
            compiler_params=_params(("parallel",), vmem),
        )(*rows, *params)

    def run_bwd(rows, params, gs):
        L = rows[0].shape[0]
        n_g = len(gs)

        def body(*refs):
            i = pl.program_id(0)
            ins = [r[...] for r in refs[:n_rows + n_params]]
            cts = tuple(r[...] for r in refs[n_rows + n_params:n_rows + n_params + n_g])
            outs = refs[n_rows + n_params + n_g:]
            _, vjp = jax.vjp(f, *ins)
            grads = vjp(cts[:-1] if pass_first else cts)
            if pass_first:
                grads = (grads[0] + cts[-1],) + tuple(grads[1:])
            for o_ref, val in zip(outs[:n_rows], grads[:n_rows]):
                o_ref[...] = val

            if n_params:
                @pl.when(i == 0)
                def _():
                    for o_ref in outs[n_rows:]:
                        o_ref[...] = jnp.zeros_like(o_ref)
                for o_ref, val in zip(outs[n_rows:], grads[n_rows:]):
                    o_ref[...] += val

        res = pl.pallas_call(
            body, name=name + "_bwd", grid=(L // tm,),
            in_specs=specs_of(rows, True) + specs_of(params, False) + specs_of(gs, True),
            out_specs=specs_of(rows, True) + specs_of(params, False),
            out_shape=[SDS(r.shape, r.dtype) for r in rows] + [SDS(p.shape, p.dtype) for p in params],
            compiler_params=_params(("arbitrary",), vmem),
        )(*rows, *params, *gs)
        return tuple(res[:n_rows]), tuple(res[n_rows:])

    def outputs(rows, params):
        outs = tuple(run_fwd(rows, params))
        return outs + (rows[0],) if pass_first else outs

    @jax.custom_vjp
    def op(rows, params):
        return outputs(rows, params)

    def fwd(rows, params):
        return outputs(rows, params), (rows, params)

    def bwd(res, gs):
        rows, params = res
        return run_bwd(rows, params, tuple(gs))

    op.defvjp(fwd, bwd)
    op.run_fwd, op.run_bwd = run_fwd, run_bwd
    return op


def _s5_scan_rows(xr_ref, xi_ref, ar, ai, x0r, x0i, tl, reverse=False):
    n = xr_ref.shape[1]
    T = SUBLANES
    row = lax.broadcasted_iota(jnp.int32, (T, n), 0)
    pr, pi = [ar], [ai]
    for _ in range(T - 1):
        pr, pi = pr + [pr[-1] * ar - pi[-1] * ai], pi + [pr[-1] * ai + pi[-1] * ar]
    levels = []
    for d in (1, 2, 4):
        mask = (row < T - d) if reverse else (row >= d)
        levels.append((T - d if reverse else d, jnp.where(mask, pr[d - 1], 0.0), jnp.where(mask, pi[d - 1], 0.0)))
    cr = jnp.zeros((T, n), F32)
    ci = jnp.zeros((T, n), F32)
    for r in range(T):
        k = (T - r) if reverse else (r + 1)
        cr = jnp.where(row == r, pr[k - 1], cr)
        ci = jnp.where(row == r, pi[k - 1], ci)
    nt = tl // T
    last = 0 if reverse else T - 1

    def step(t, carry):
        sr, si = carry
        base = pl.multiple_of((nt - 1 - t if reverse else t) * T, T)
        br = xr_ref[pl.ds(base, T), :]
        bi = xi_ref[pl.ds(base, T), :]
        for shift, mr, mi in levels:
            qr = pltpu.roll(br, shift, 0)
            qi = pltpu.roll(bi, shift, 0)
            br, bi = br + (mr * qr - mi * qi), bi + (mr * qi + mi * qr)
        xr = br + (cr * sr - ci * si)
        xi = bi + (cr * si + ci * sr)
        xr_ref[pl.ds(base, T), :] = xr
        xi_ref[pl.ds(base, T), :] = xi
        return xr[last:last + 1, :], xi[last:last + 1, :]
    return lax.fori_loop(0, nt, step, (x0r, x0i))


def _s5_fwd_call(u, bre, bim, cre, cim, a, d, tl):
    L, e = u.shape
    nb = e // LANES
    ns = bre.shape[2]
    nc = L // tl

    def body(u_ref, bre_ref, bim_ref, cre_ref, cim_ref, a_ref, d_ref, y_ref, xb_ref, sr_ref, si_ref, xr_ref, xi_ref, carry_ref):
        c = pl.program_id(1)

        @pl.when(c == 0)
        def _():
            carry_ref[...] = jnp.zeros_like(carry_ref)
        xb_ref[0, 0] = carry_ref[...]
        ub = u_ref[...]
        xr_ref[...] = _bdot(ub, bre_ref[0])
        xi_ref[...] = _bdot(ub, bim_ref[0])
        ar = a_ref[0, 0:1, :]
        ai = a_ref[0, 1:2, :]
        xr, xi = _s5_scan_rows(xr_ref, xi_ref, ar, ai, carry_ref[0:1, :], carry_ref[1:2, :], tl)
        carry_ref[0:1, :] = xr
        carry_ref[1:2, :] = xi
        sr = xr_ref[...].astype(BF16)
        si = xi_ref[...].astype(BF16)
        sr_ref[...] = sr
        si_ref[...] = si
        y_ref[...] = _f_s5_act(_bdot(sr, cre_ref[0]) - _bdot(si, cim_ref[0]), ub, d_ref[...])[0]

    return pl.pallas_call(
        body, name="s5_core_fwd", grid=(nb, nc),
        in_specs=[pl.BlockSpec((tl, LANES), lambda j, c: (c, j)),
                  pl.BlockSpec((1, LANES, ns), lambda j, c: (j, 0, 0)), pl.BlockSpec((1, LANES, ns), lambda j, c: (j, 0, 0)),
                  pl.BlockSpec((1, ns, LANES), lambda j, c: (j, 0, 0)), pl.BlockSpec((1, ns, LANES), lambda j, c: (j, 0, 0)),
                  pl.BlockSpec((1, SUBLANES, ns), lambda j, c: (j, 0, 0)), pl.BlockSpec((1, LANES), lambda j, c: (0, j))],
        out_specs=[pl.BlockSpec((tl, LANES), lambda j, c: (c, j)),
                   pl.BlockSpec((1, 1, SUBLANES, ns), lambda j, c: (j, c, 0, 0)),
                   pl.BlockSpec((tl, ns), lambda j, c: (c, j)), pl.BlockSpec((tl, ns), lambda j, c: (c, j))],
        out_shape=[SDS((L, e), F32), SDS((nb, nc, SUBLANES, ns), F32), SDS((L, nb * ns), BF16), SDS((L, nb * ns), BF16)],
        scratch_shapes=[pltpu.VMEM((tl, ns), F32), pltpu.VMEM((tl, ns), F32), pltpu.VMEM((SUBLANES, ns), F32)],
        compiler_params=_params(("arbitrary", "arbitrary"), VMEM_MID),
    )(u, bre, bim, cre, cim, a, d)


def _s5_bwd_call(u, dy2, bre, bim, cre, cim, a, d, xb, sr, si, tl):
    L, e = u.shape
    nb = e // LANES
    ns = bre.shape[2]
    nc = L // tl

    def body(u_ref, dy2_ref, bre_ref, bim_ref, cre_ref, cim_ref, a_ref, d_ref, xb_ref, sr_ref, si_ref,
             du_ref, dbre_ref, dbim_ref, dcre_ref, dcim_ref, da_ref, dd_ref,
             gr_ref, gi_ref, gcarry_ref):
        c = pl.program_id(1)

        @pl.when(c == 0)
        def _():
            gcarry_ref[...] = jnp.zeros_like(gcarry_ref)
            dbre_ref[...] = jnp.zeros_like(dbre_ref)
            dbim_ref[...] = jnp.zeros_like(dbim_ref)
            dcre_ref[...] = jnp.zeros_like(dcre_ref)
            dcim_ref[...] = jnp.zeros_like(dcim_ref)
            da_ref[...] = jnp.zeros_like(da_ref)
            dd_ref[...] = jnp.zeros_like(dd_ref)

        ub = u_ref[...]
        ys = _bdot(sr_ref[...], cre_ref[0]) - _bdot(si_ref[...], cim_ref[0])
        _, act_vjp = jax.vjp(lambda *t: _f_s5_act(*t)[0], ys, ub, d_ref[...])
        dy, du_skip, dd = act_vjp(dy2_ref[...])
        dd_ref[...] += dd
        ar = a_ref[0, 0:1, :]
        ai = a_ref[0, 1:2, :]
        x0r = xb_ref[0, 0, 0:1, :]
        x0i = xb_ref[0, 0, 1:2, :]
        dcre_ref[0] += _bdot(sr_ref[...], dy, TN)
        dcim_ref[0] -= _bdot(si_ref[...], dy, TN)
        gr_ref[...] = _bdot(dy, cre_ref[0], NT)
        gi_ref[...] = -_bdot(dy, cim_ref[0], NT)

        g0r, g0i = _s5_scan_rows(gr_ref, gi_ref, ar, -ai, gcarry_ref[0:1, :], gcarry_ref[1:2, :], tl, reverse=True)
        gcarry_ref[0:1, :] = g0r
        gcarry_ref[1:2, :] = g0i
        row = lax.broadcasted_iota(jnp.int32, (tl, ns), 0)
        gr = gr_ref[...]
        gi = gi_ref[...]
        xpr = jnp.where(row == 0, x0r, pltpu.roll(sr_ref[...].astype(F32), 1, 0))
        xpi = jnp.where(row == 0, x0i, pltpu.roll(si_ref[...].astype(F32), 1, 0))
        da_ref[0, 0:1, :] += jnp.sum(gr * xpr + gi * xpi, axis=0, keepdims=True)
        da_ref[0, 1:2, :] += jnp.sum(gi * xpr - gr * xpi, axis=0, keepdims=True)
        du_ref[...] = (_bdot(gr, bre_ref[0], NT) + _bdot(gi, bim_ref[0], NT)) + du_skip
        dbre_ref[0] += _bdot(ub, gr, TN)
        dbim_ref[0] += _bdot(ub, gi, TN)

    rev = lambda c: nc - 1 - c
    return pl.pallas_call(
        body, name="s5_core_bwd", grid=(nb, nc),
        in_specs=[pl.BlockSpec((tl, LANES), lambda j, c: (rev(c), j)), pl.BlockSpec((tl, LANES), lambda j, c: (rev(c), j)),
                  pl.BlockSpec((1, LANES, ns), lambda j, c: (j, 0, 0)), pl.BlockSpec((1, LANES, ns), lambda j, c: (j, 0, 0)),
                  pl.BlockSpec((1, ns, LANES), lambda j, c: (j, 0, 0)), pl.BlockSpec((1, ns, LANES), lambda j, c: (j, 0, 0)),
                  pl.BlockSpec((1, SUBLANES, ns), lambda j, c: (j, 0, 0)), pl.BlockSpec((1, LANES), lambda j, c: (0, j)),
                  pl.BlockSpec((1, 1, SUBLANES, ns), lambda j, c: (j, rev(c), 0, 0)),
                  pl.BlockSpec((tl, ns), lambda j, c: (rev(c), j)), pl.BlockSpec((tl, ns), lambda j, c: (rev(c), j))],
        out_specs=[pl.BlockSpec((tl, LANES), lambda j, c: (rev(c), j)),
                   pl.BlockSpec((1, LANES, ns), lambda j, c: (j, 0, 0)), pl.BlockSpec((1, LANES, ns), lambda j, c: (j, 0, 0)),
                   pl.BlockSpec((1, ns, LANES), lambda j, c: (j, 0, 0)), pl.BlockSpec((1, ns, LANES), lambda j, c: (j, 0, 0)),
                   pl.BlockSpec((1, SUBLANES, ns), lambda j, c: (j, 0, 0)), pl.BlockSpec((1, LANES), lambda j, c: (0, j))],
        out_shape=[SDS((L, e), F32), SDS(bre.shape, F32), SDS(bim.shape, F32), SDS(cre.shape, F32), SDS(cim.shape, F32),
                   SDS(a.shape, F32), SDS(d.shape, F32)],
        scratch_shapes=[pltpu.VMEM((tl, ns), F32) for _ in range(2)] + [pltpu.VMEM((SUBLANES, ns), F32)],
        compiler_params=_params(("arbitrary", "arbitrary"), VMEM_MID),
    )(u, dy2, bre, bim, cre, cim, a, d, xb, sr, si)


def make_s5_core(tl):
    @jax.custom_vjp
    def s5_core(u, bre, bim, cre, cim, a, d):
        return _s5_fwd_call(u, bre, bim, cre, cim, a, d, tl)[0]

    def fwd(u, bre, bim, cre, cim, a, d):
        y2, xb, sr, si = _s5_fwd_call(u, bre, bim, cre, cim, a, d, tl)
        return y2, (u, bre, bim, cre, cim, a, d, xb, sr, si)

    def bwd(res, dy2):
        u, bre, bim, cre, cim, a, d, xb, sr, si = res
        return tuple(_s5_bwd_call(u, dy2, bre, bim, cre, cim, a, d, xb, sr, si, tl))

    s5_core.defvjp(fwd, bwd)
    return s5_core


def _s5_block_params(lam_re, lam_im, log_dt, b_re, b_im, c_re, c_im):
    dt = jnp.exp(log_dt)[:, None]
    mag = jnp.exp(lam_re * dt)
    ab_re = mag * jnp.cos(lam_im * dt)
    ab_im = mag * jnp.sin(lam_im * dt)
    den = lam_re * lam_re + lam_im * lam_im
    nr = ab_re - 1.0
    ni = ab_im
    q_re = (nr * lam_re + ni * lam_im) / den
    q_im = (ni * lam_re - nr * lam_im) / den
    bb_re = q_re[..., None] * b_re - q_im[..., None] * b_im
    bb_im = q_re[..., None] * b_im + q_im[..., None] * b_re
    nb = S5_GROUPS // S5_GB
    eye = jnp.eye(S5_GB, dtype=F32)

    def bdiag_in(bb):
        t = bb.reshape(nb, S5_GB, S5_STATE, S5_GROUP)
        t = jnp.einsum("jgpm,gh->jgmhp", t, eye)
        return t.reshape(nb, S5_GB * S5_GROUP, S5_GB * S5_STATE)

    def bdiag_out(cc):
        t = cc.reshape(nb, S5_GB, S5_GROUP, S5_STATE)
        t = jnp.einsum("jgmp,gh->jgphm", t, eye)
        return t.reshape(nb, S5_GB * S5_STATE, S5_GB * S5_GROUP)

    a = jnp.stack([ab_re.reshape(nb, S5_GB * S5_STATE), ab_im.reshape(nb, S5_GB * S5_STATE)], axis=1)
    a = jnp.concatenate([a, jnp.zeros((nb, SUBLANES - 2, S5_GB * S5_STATE), F32)], axis=1)
    return bdiag_in(bb_re), bdiag_in(bb_im), bdiag_out(c_re), bdiag_out(c_im), a


def _shift_down(x, s, row):
    if s == 0:
        return x
    return jnp.where(row >= s, pltpu.roll(x, s, 0), 0.0)


def _shift_up(x, s, row, n):
    if s == 0:
        return x
    return jnp.where(row < n - s, pltpu.roll(x, n - s, 0), 0.0)


def _causal_conv(xv, w_ref, row):
    acc = jnp.zeros_like(xv)
    for j in range(GDN_CONV):
        acc += w_ref[j:j + 1, :] * _shift_down(xv, GDN_CONV - 1 - j, row)
    return acc


def _conv_fwd_call(x, w, act, name):
    L, ch = x.shape

    def body(x_ref, w_ref, y_ref):
        xv = x_ref[...]
        row = lax.broadcasted_iota(jnp.int32, xv.shape, 0)
        y_ref[...] = act(_causal_conv(xv, w_ref, row))

    return pl.pallas_call(
        body, name=name + "_fwd", grid=(ch // LANES,),
        in_specs=[pl.BlockSpec((L, LANES), lambda j: (0, j)), pl.BlockSpec((SUBLANES, LANES), lambda j: (0, j))],
        out_specs=pl.BlockSpec((L, LANES), lambda j: (0, j)), out_shape=SDS((L, ch), F32),
        compiler_params=_params(("parallel",), VMEM_MID),
    )(x, w)


def _conv_bwd_call(x, w, dy, act, name):
    L, ch = x.shape

    def body(x_ref, w_ref, dy_ref, dx_ref, dw_ref):
        xv = x_ref[...]
        row = lax.broadcasted_iota(jnp.int32, xv.shape, 0)
        _, act_vjp = jax.vjp(act, _causal_conv(xv, w_ref, row))
        (g,) = act_vjp(dy_ref[...])
        acc = jnp.zeros_like(xv)
        dws = []
        for j in range(GDN_CONV):
            s = GDN_CONV - 1 - j
            acc += w_ref[j:j + 1, :] * _shift_up(g, s, row, L)
            dws.append(jnp.sum(g * _shift_down(xv, s, row), axis=0, keepdims=True))
        dx_ref[...] = acc
        dw_ref[...] = jnp.concatenate(dws + [jnp.zeros((SUBLANES - GDN_CONV, LANES), F32)], axis=0)

    return pl.pallas_call(
        body, name=name + "_bwd", grid=(ch // LANES,),
        in_specs=[pl.BlockSpec((L, LANES), lambda j: (0, j)), pl.BlockSpec((SUBLANES, LANES), lambda j: (0, j)),
                  pl.BlockSpec((L, LANES), lambda j: (0, j))],
        out_specs=[pl.BlockSpec((L, LANES), lambda j: (0, j)), pl.BlockSpec((SUBLANES, LANES), lambda j: (0, j))],
        out_shape=[SDS((L, ch), F32), SDS((SUBLANES, ch), F32)],
        compiler_params=_params(("parallel",), VMEM_MID),
    )(x, w, dy)


def make_conv_act(act, name):
    @jax.custom_vjp
    def op(x, w):
        return _conv_fwd_call(x, w, act, name)

    def fwd(x, w):
        return _conv_fwd_call(x, w, act, name), (x, w)

    def bwd(res, dy):
        x, w = res
        return tuple(_conv_bwd_call(x, w, dy, act, name))

    op.defvjp(fwd, bwd)
    return op


BNN, BNT, BTN = _BNN, _BNT, _BTN
GDN_PREP_BATCH = 8


@jax.custom_vjp
def _known_inverse(a, t):
    return t


def _known_inverse_fwd(a, t):
    return t, t


def _known_inverse_bwd(t, g):
    return -_hdot(_hdot(t, g, _BTN), t, _BNT), jnp.zeros_like(t)


_known_inverse.defvjp(_known_inverse_fwd, _known_inverse_bwd)


def _gdn_prep_math(q, k, v, beta, g, t_saved=None):
    B, C = q.shape[0], q.shape[1]
    ri = lax.broadcasted_iota(jnp.int32, (B, C, C), 1)
    ci = lax.broadcasted_iota(jnp.int32, (B, C, C), 2)
    causal = ri >= ci
    strict = ri > ci
    eye = (ri == ci).astype(F32)
    gb = jnp.broadcast_to(g, (B, C, C))
    g_row = jnp.sum(gb * eye, axis=1, keepdims=True)
    gc_col = jnp.sum(jnp.where(causal, jnp.broadcast_to(g_row, (B, C, C)), 0.0), axis=2, keepdims=True)
    gc_row = jnp.sum(jnp.where(ri <= ci, gb, 0.0), axis=1, keepdims=True)
    decay = jnp.exp(jnp.where(causal, gc_col - gc_row, -jnp.inf))
    kk = _bdot(k, k, BNT)
    a_mat = jnp.where(strict, beta * kk * decay, 0.0)
    t = _unit_lower_inverse(a_mat) if t_saved is None else _known_inverse(a_mat, t_saved)
    e_gc = jnp.exp(gc_col)
    w = _hdot(t, beta * e_gc * k, BNN)
    u = _hdot(t, beta * v, BNN)
    qk = _bdot(q, k, BNT) * decay
    q_dec = q * e_gc
    g_last = gc_col[:, C - 1:C, :]
    k_dec = k * jnp.exp(g_last - gc_col)
    return q_dec, w, u, qk, k_dec, gc_col, t


def _gdn_prep_specs(L):
    C = GDN_CHUNK
    nb = min(GDN_PREP_BATCH, L // C)
    R = nb * C
    ins = [pl.BlockSpec((R, GDN_DK), lambda c, h: (c, h)), pl.BlockSpec((R, GDN_DK), lambda c, h: (c, h)),
           pl.BlockSpec((R, GDN_DV), lambda c, h: (c, h)), pl.BlockSpec((R, LANES), lambda c, h: (c, 0))]
    outs = [pl.BlockSpec((1, R, GDN_DK), lambda c, h: (h, c, 0)), pl.BlockSpec((1, R, GDN_DK), lambda c, h: (h, c, 0)),
            pl.BlockSpec((1, R, GDN_DV), lambda c, h: (h, c, 0)), pl.BlockSpec((1, R, C), lambda c, h: (h, c, 0)),
            pl.BlockSpec((1, R, GDN_DK), lambda c, h: (h, c, 0)), pl.BlockSpec((1, R, 1), lambda c, h: (h, c, 0))]
    t_spec = pl.BlockSpec((1, R, C), lambda c, h: (h, c, 0))
    shapes = [SDS((GDN_HEADS, L, GDN_DK), F32), SDS((GDN_HEADS, L, GDN_DK), F32), SDS((GDN_HEADS, L, GDN_DV), F32),
              SDS((GDN_HEADS, L, C), F32), SDS((GDN_HEADS, L, GDN_DK), F32), SDS((GDN_HEADS, L, 1), F32)]
    return ins, outs, t_spec, shapes, nb


def _chunks(x, nb):
    return x.reshape(nb, x.shape[0] // nb, x.shape[1])


def _head_columns(bg, h):
    lane = lax.broadcasted_iota(jnp.int32, bg.shape, 1)
    beta = jnp.sum(jnp.where(lane == h, bg, 0.0), axis=1, keepdims=True)
    g = jnp.sum(jnp.where(lane == h + GDN_HEADS, bg, 0.0), axis=1, keepdims=True)
    return beta, g


def _gdn_prep_fwd_call(q, k, v, bg):
    L = q.shape[0]
    ins, outs, t_spec, shapes, nb = _gdn_prep_specs(L)

    def body(q_ref, k_ref, v_ref, bg_ref, *o_refs):
        beta, g = _head_columns(bg_ref[...], pl.program_id(1))
        res = _gdn_prep_math(_chunks(q_ref[...], nb), _chunks(k_ref[...], nb), _chunks(v_ref[...], nb),
                             _chunks(beta, nb), _chunks(g, nb))
        for o_ref, val in zip(o_refs, res):
            o_ref[0] = val.reshape(val.shape[0] * val.shape[1], val.shape[2])

    return pl.pallas_call(
        body, name="gdn_prep_fwd", grid=(L // (nb * GDN_CHUNK), GDN_HEADS), in_specs=ins, out_specs=outs + [t_spec],
        out_shape=shapes + [SDS((GDN_HEADS, L, GDN_CHUNK), F32)],
        compiler_params=_params(("parallel", "parallel"), VMEM_MID),
    )(q, k, v, bg)


def _gdn_prep_bwd_call(q, k, v, bg, t, cts):
    L = q.shape[0]
    ins, outs, t_spec, _, nb = _gdn_prep_specs(L)

    def body(q_ref, k_ref, v_ref, bg_ref, t_ref, c0, c1, c2, c3, c4, c5, dq_ref, dk_ref, dv_ref, dbg_ref):
        h = pl.program_id(1)
        beta, g = _head_columns(bg_ref[...], h)
        t_saved = _chunks(t_ref[0], nb)
        _, vjp = jax.vjp(lambda *a: _gdn_prep_math(*a, t_saved=t_saved)[:6], _chunks(q_ref[...], nb), _chunks(k_ref[...], nb),
                         _chunks(v_ref[...], nb), _chunks(beta, nb), _chunks(g, nb))
        dq, dk, dv, db, dg = vjp(tuple(_chunks(c[0], nb) for c in (c0, c1, c2, c3, c4, c5)))
        flat = lambda a: a.reshape(a.shape[0] * a.shape[1], a.shape[2])
        dq_ref[...] = flat(dq)
        dk_ref[...] = flat(dk)
        dv_ref[...] = flat(dv)

        @pl.when(h == 0)
        def _():
            dbg_ref[...] = jnp.zeros_like(dbg_ref)
        lane = lax.broadcasted_iota(jnp.int32, dbg_ref.shape, 1)
        dbg_ref[...] += jnp.where(lane == h, flat(db), 0.0) + jnp.where(lane == h + GDN_HEADS, flat(dg), 0.0)

    return pl.pallas_call(
        body, name="gdn_prep_bwd", grid=(L // (nb * GDN_CHUNK), GDN_HEADS), in_specs=ins + [t_spec] + outs, out_specs=ins,
        out_shape=[SDS(q.shape, F32), SDS(k.shape, F32), SDS(v.shape, F32), SDS(bg.shape, F32)],
        compiler_params=_params(("parallel", "arbitrary"), VMEM_MID),
    )(q, k, v, bg, t, *cts)


@jax.custom_vjp
def gdn_prep(q, k, v, bg):
    return tuple(_gdn_prep_fwd_call(q, k, v, bg)[:6])


def _gdn_prep_f(q, k, v, bg):
    res = _gdn_prep_fwd_call(q, k, v, bg)
    return tuple(res[:6]), (q, k, v, bg, res[6])


def _gdn_prep_b(res, cts):
    return tuple(_gdn_prep_bwd_call(*res, tuple(cts)))


gdn_prep.defvjp(_gdn_prep_f, _gdn_prep_b)


def _gdn_step_math(q_dec, w, u, qk, k_dec, gc, z, nw, state):
    H, C = q_dec.shape[0], q_dec.shape[1]
    v_new = u - _bdot(w, state, BNN)
    o = _bdot(q_dec, state, BNN) + _bdot(qk, v_new, BNN)
    gl = gc[:, C - 1:C, :]
    new_state = jnp.exp(gl) * state + _bdot(k_dec, v_new, BTN)
    return _f_gdn_post(jnp.concatenate([o[h] for h in range(H)], axis=1), z, nw)[0], new_state


def _gdn_scan_specs(L, rev):
    C, H = GDN_CHUNK, GDN_HEADS
    nc = L // C
    cc = (lambda c: nc - 1 - c) if rev else (lambda c: c)
    ins = [pl.BlockSpec((H, C, GDN_DK), lambda c: (0, cc(c), 0)), pl.BlockSpec((H, C, GDN_DK), lambda c: (0, cc(c), 0)),
           pl.BlockSpec((H, C, GDN_DV), lambda c: (0, cc(c), 0)), pl.BlockSpec((H, C, C), lambda c: (0, cc(c), 0)),
           pl.BlockSpec((H, C, GDN_DK), lambda c: (0, cc(c), 0)), pl.BlockSpec((H, C, 1), lambda c: (0, cc(c), 0))]
    o_spec = pl.BlockSpec((C, H * GDN_DV), lambda c: (cc(c), 0))
    nw_spec = pl.BlockSpec((1, H * GDN_DV), lambda c: (0, 0))
    s_spec = pl.BlockSpec((1, H, GDN_DK, GDN_DV), lambda c: (cc(c), 0, 0, 0))
    return ins + [o_spec, nw_spec], o_spec, s_spec, nc


def _gdn_scan_fwd_call(q_dec, w, u, qk, k_dec, gc, z, nw):
    L = q_dec.shape[1]
    ins, o_spec, s_spec, nc = _gdn_scan_specs(L, False)

    def body(qd_ref, w_ref, u_ref, qk_ref, kd_ref, gc_ref, z_ref, nw_ref, o_ref, sin_ref, s_ref):
        c = pl.program_id(0)

        @pl.when(c == 0)
        def _():
            s_ref[...] = jnp.zeros_like(s_ref)
        st = s_ref[...]
        sin_ref[0] = st
        o, ns = _gdn_step_math(qd_ref[...], w_ref[...], u_ref[...], qk_ref[...], kd_ref[...], gc_ref[...], z_ref[...], nw_ref[...], st)
        o_ref[...] = o
        s_ref[...] = ns

    return pl.pallas_call(
        body, name="gdn_scan_fwd", grid=(nc,), in_specs=ins, out_specs=[o_spec, s_spec],
        out_shape=[SDS((L, GDN_HEADS * GDN_DV), F32), SDS((nc, GDN_HEADS, GDN_DK, GDN_DV), F32)],
        scratch_shapes=[pltpu.VMEM((GDN_HEADS, GDN_DK, GDN_DV), F32)],
        compiler_params=_params(("arbitrary",), VMEM_MID),
    )(q_dec, w, u, qk, k_dec, gc, z, nw)


def _gdn_scan_bwd_call(q_dec, w, u, qk, k_dec, gc, z, nw, s_in, do):
    L = q_dec.shape[1]
    ins, o_spec, s_spec, nc = _gdn_scan_specs(L, True)

    def body(qd_ref, w_ref, u_ref, qk_ref, kd_ref, gc_ref, z_ref, nw_ref, sin_ref, do_ref,
             dqd_ref, dw_ref, du_ref, dqk_ref, dkd_ref, dgc_ref, dz_ref, dnw_ref, ds_ref):
        c = pl.program_id(0)

        @pl.when(c == 0)
        def _():
            ds_ref[...] = jnp.zeros_like(ds_ref)
            dnw_ref[...] = jnp.zeros_like(dnw_ref)
        _, vjp = jax.vjp(_gdn_step_math, qd_ref[...], w_ref[...], u_ref[...], qk_ref[...], kd_ref[...], gc_ref[...],
                         z_ref[...], nw_ref[...], sin_ref[0])
        dqd, dw, du, dqk, dkd, dgc, dz, dnw, dst = vjp((do_ref[...], ds_ref[...]))
        dqd_ref[...] = dqd
        dw_ref[...] = dw
        du_ref[...] = du
        dqk_ref[...] = dqk
        dkd_ref[...] = dkd
        dgc_ref[...] = dgc
        dz_ref[...] = dz
        dnw_ref[...] += dnw
        ds_ref[...] = dst

    return pl.pallas_call(
        body, name="gdn_scan_bwd", grid=(nc,), in_specs=ins + [s_spec, o_spec], out_specs=ins,
        out_shape=[SDS(t.shape, F32) for t in (q_dec, w, u, qk, k_dec, gc, z, nw)],
        scratch_shapes=[pltpu.VMEM((GDN_HEADS, GDN_DK, GDN_DV), F32)],
        compiler_params=_params(("arbitrary",), VMEM_MID),
    )(q_dec, w, u, qk, k_dec, gc, z, nw, s_in, do)


@jax.custom_vjp
def gdn_scan(q_dec, w, u, qk, k_dec, gc, z, nw):
    return _gdn_scan_fwd_call(q_dec, w, u, qk, k_dec, gc, z, nw)[0]


def _gdn_scan_f(*args):
    o, s_in = _gdn_scan_fwd_call(*args)
    return o, (*args, s_in)


def _gdn_scan_b(res, do):
    return tuple(_gdn_scan_bwd_call(*res, do))


gdn_scan.defvjp(_gdn_scan_f, _gdn_scan_b)


def _silu(x):
    return x * jax.nn.sigmoid(x)


def _gelu_tanh(x):
    return 0.5 * x * (1.0 + jnp.tanh(math.sqrt(2.0 / math.pi) * (x + 0.044715 * (x * x * x))))


def _f_lnmod(x, nw, sc, sh, bsc, bsh):
    xn = x * lax.rsqrt(jnp.mean(x * x, axis=-1, keepdims=True) + NORM_EPS) * nw
    return (xn * (1.0 + (sc + bsc)) + (sh + bsh),)


def _f_s5_act(ys, u, d):
    return (_gelu_tanh(ys + d * u),)


def _f_s5_gate(y2, t, z):
    return (y2 * jax.nn.sigmoid(t) * _silu(z),)


def _f_res(x, y, gate, bgate):
    return (x + (gate + bgate) * y,)


def _heads(x, width, fn):
    return jnp.concatenate([fn(x[:, i * width:(i + 1) * width]) for i in range(x.shape[1] // width)], axis=1)


def _l2n(x):
    return x * lax.rsqrt(jnp.sum(x * x, axis=-1, keepdims=True) + NORM_EPS)


def _f_betag(ba, alog, dtb):
    col = lax.broadcasted_iota(jnp.int32, ba.shape, 1)
    t = ba + dtb
    softplus = jnp.maximum(t, 0.0) + jnp.log1p(jnp.exp(-jnp.abs(t)))
    g = -jnp.exp(alog) * softplus
    return (jnp.where(col < GDN_HEADS, jax.nn.sigmoid(ba), jnp.where(col < 2 * GDN_HEADS, g, 0.0)),)


def _f_gdn_post(o, z, nw):
    on = _heads(o, GDN_DV, lambda t: t * lax.rsqrt(jnp.mean(t * t, axis=-1, keepdims=True) + NORM_EPS))
    return (on * nw * _silu(z),)


def _f_loss(x, tgt, fw):
    y = x * lax.rsqrt(jnp.mean(x * x, axis=-1, keepdims=True) + NORM_EPS) * fw
    err = y - tgt
    return (0.5 * jnp.mean(err * err, axis=-1, keepdims=True),)


def _ada_mod_call(c_all, ada_w):
    n = ada_w.shape[2]

    def body(c_ref, w_ref, o_ref):
        ca = _silu(c_ref[...])
        for l in range(ada_w.shape[0]):
            o_ref[l] = _bdot(ca, w_ref[l])

    return pl.pallas_call(body, name="ada_mod", out_shape=SDS((ada_w.shape[0], N_DEV, n), F32),
                          compiler_params=_params(None, VMEM_MID))(c_all, ada_w)


def _ada_grad_call(c_all, dmod):
    nl, _, n = dmod.shape

    def body(c_ref, d_ref, o_ref):
        ca = _silu(c_ref[...])
        for l in range(nl):
            o_ref[l] = _hdot(ca, d_ref[l], TN)

    return pl.pallas_call(body, name="ada_grad", out_shape=SDS((nl, c_all.shape[1], n), F32),
                          compiler_params=_params(None, VMEM_MID))(c_all, dmod)


ADAM_ROWS = 512


def _adamw(g, w, m, v):
    m2 = ADAM_B1 * m + (1.0 - ADAM_B1) * g
    v2 = ADAM_B2 * v + (1.0 - ADAM_B2) * (g * g)
    m_hat = m2 / (1.0 - ADAM_B1 ** ADAM_STEP)
    v_hat = v2 / (1.0 - ADAM_B2 ** ADAM_STEP)
    return g, -ADAM_LR * (m_hat / (jnp.sqrt(v_hat) + ADAM_EPS) + ADAM_WD * w), m2, v2


def _adam_call(gs, w, m, v, name, rows=None, by_cols=False):
    n, r, cols = gs.shape
    if by_cols:
        blk = pl.BlockSpec((r, LANES), lambda i: (0, i))
        g_blk, grid = pl.BlockSpec((n, r, LANES), lambda i: (0, 0, i)), (cols // LANES,)
    else:
        rows = rows or ADAM_ROWS
        blk = pl.BlockSpec((rows, cols), lambda i: (i, 0))
        g_blk, grid = pl.BlockSpec((n, rows, cols), lambda i: (0, i, 0)), (r // rows,)

    def body(g_ref, w_ref, m_ref, v_ref, go_ref, d_ref, mo_ref, vo_ref):
        g = g_ref[0].astype(F32)
        for s in range(1, n):
            g = g + g_ref[s].astype(F32)
        for o_ref, val in zip((go_ref, d_ref, mo_ref, vo_ref), _adamw(g, w_ref[...], m_ref[...], v_ref[...])):
            o_ref[...] = val

    return pl.pallas_call(
        body, name=name, grid=grid, in_specs=[g_blk, blk, blk, blk],
        out_specs=[blk, blk, blk, blk], out_shape=[SDS((r, cols), F32)] * 4,
        compiler_params=_params(("parallel",), VMEM_MID),
    )(gs, w, m, v)


def _sum_call(gs, name, rows):
    n, r, _ = gs.shape

    def body(g_ref, o_ref):
        g = g_ref[0].astype(F32)
        for s in range(1, n):
            g = g + g_ref[s].astype(F32)
        o_ref[...] = g

    return pl.pallas_call(
        body, name=name, grid=(r // rows,),
        in_specs=[pl.BlockSpec((n, rows, LANES), lambda i: (0, i, 0))],
        out_specs=pl.BlockSpec((rows, LANES), lambda i: (i, 0)), out_shape=SDS((r, LANES), F32),
        compiler_params=_params(("parallel",), VMEM_MID),
    )(gs)


def _allgather_call(x_shard, name):
    m_per, n = x_shard.shape

    def body(x_ref, out_ref, send_sems, recv_sems, local_sem):
        x, y, c = lax.axis_index("x"), lax.axis_index("y"), lax.axis_index("c")
        me, sibling = (x, y, c), (x, y, 1 - c)
        chips = [(1 - x, y), (x, 1 - y), (1 - x, 1 - y)]

        def rows(px, py, pc):
            return out_ref.at[pl.ds((4 * px + 2 * py + pc) * m_per, m_per), :]

        def copy(k, block, to, src=None):
            return pltpu.make_async_remote_copy(
                src_ref=rows(*block) if src is None else src, dst_ref=rows(*block),
                send_sem=send_sems.at[k], recv_sem=recv_sems.at[k], device_id=to, device_id_type=pl.DeviceIdType.MESH)

        mine = pltpu.make_async_copy(x_ref, rows(*me), local_sem)
        mine.start()
        first = [copy(0, me, sibling, src=x_ref)]
        first += [copy(1 + j, me, (*chip, c), src=x_ref) for j, chip in enumerate(chips)]
        for cp in first:
            cp.start()
        passed = [copy(4 + j, (*chip, c), sibling) for j, chip in enumerate(chips)]
        for j, chip in enumerate(chips):
            copy(1 + j, (*chip, c), me).wait_recv()
            passed[j].start()
        copy(0, sibling, me).wait_recv()
        for j, chip in enumerate(chips):
            copy(4 + j, (*chip, 1 - c), me).wait_recv()
        for cp in first + passed:
            cp.wait_send()
        mine.wait()

    vmem = pl.BlockSpec(memory_space=pltpu.VMEM)
    return pl.pallas_call(
        body, name=name, out_shape=SDS((N_DEV * m_per, n), x_shard.dtype), in_specs=[vmem], out_specs=vmem,
        scratch_shapes=[pltpu.SemaphoreType.DMA((7,)), pltpu.SemaphoreType.DMA((7,)), pltpu.SemaphoreType.DMA],
    )(x_shard)


def _gather_weights_call(shards, name):
    nw = len(shards)

    def body(*refs):
        x_refs, out_refs = refs[:nw], refs[nw:2 * nw]
        send_sems, recv_sems, local_sems = refs[2 * nw:]
        x, y, c = lax.axis_index("x"), lax.axis_index("y"), lax.axis_index("c")
        me, sibling = (x, y, c), (x, y, 1 - c)
        chips = [(1 - x, y), (x, 1 - y), (1 - x, 1 - y)]

        def slot(w, px, py, pc):
            return out_refs[w].at[4 * px + 2 * py + pc]

        def copy(w, k, block, to, src=None):
            dst = slot(w, *block)
            return pltpu.make_async_remote_copy(
                src_ref=dst if src is None else src, dst_ref=dst, send_sem=send_sems.at[7 * w + k],
                recv_sem=recv_sems.at[7 * w + k], device_id=to, device_id_type=pl.DeviceIdType.MESH)

        mines = [pltpu.make_async_copy(x_refs[w], slot(w, *me), local_sems.at[w]) for w in range(nw)]
        for cp in mines:
            cp.start()
        first = [copy(w, 0, me, sibling, src=x_refs[w]) for w in range(nw)]
        first += [copy(w, 1 + j, me, (*chip, c), src=x_refs[w]) for w in range(nw) for j, chip in enumerate(chips)]
        for cp in first:
            cp.start()
        passed = []
        for w in range(nw):
            for j, chip in enumerate(chips):
                copy(w, 1 + j, (*chip, c), me).wait_recv()
                fwd = copy(w, 4 + j, (*chip, c), sibling)
                fwd.start()
                passed.append(fwd)
        for w in range(nw):
            copy(w, 0, sibling, me).wait_recv()
            for j, chip in enumerate(chips):
                copy(w, 4 + j, (*chip, 1 - c), me).wait_recv()
        for cp in first + passed:
            cp.wait_send()
        for cp in mines:
            cp.wait()

    hbm = pl.BlockSpec(memory_space=pl.ANY)
    return pl.pallas_call(
        body, name=name, out_shape=[SDS((N_DEV,) + s.shape, s.dtype) for s in shards],
        in_specs=[hbm] * nw, out_specs=[hbm] * nw,
        scratch_shapes=[pltpu.SemaphoreType.DMA((7 * nw,)), pltpu.SemaphoreType.DMA((7 * nw,)), pltpu.SemaphoreType.DMA((nw,))],
    )(*shards)


def _pair_exchange_call(grads, name):
    nw = len(grads)

    def body(*refs):
        g_refs, got_refs = refs[:nw], refs[nw:2 * nw]
        send_sems, recv_sems = refs[2 * nw:]
        x, y, c = lax.axis_index("x"), lax.axis_index("y"), lax.axis_index("c")
        copies = []
        for w in range(nw):
            for j in range(4):
                give = pltpu.make_async_remote_copy(
                    src_ref=g_refs[w].at[2 * j + 1 - c], dst_ref=got_refs[w].at[j], send_sem=send_sems.at[4 * w + j],
                    recv_sem=recv_sems.at[4 * w + j], device_id=(x, y, 1 - c), device_id_type=pl.DeviceIdType.MESH)
                give.start()
                copies.append(give)
        for cp in copies:
            cp.wait()

    hbm = pl.BlockSpec(memory_space=pl.ANY)
    return pl.pallas_call(
        body, name=name, out_shape=[SDS((4,) + g.shape[1:], g.dtype) for g in grads], in_specs=[hbm] * nw, out_specs=[hbm] * nw,
        scratch_shapes=[pltpu.SemaphoreType.DMA((4 * nw,)), pltpu.SemaphoreType.DMA((4 * nw,))],
    )(*grads)


def _chip_exchange_call(parts, name):
    nw = len(parts)

    def body(*refs):
        p_refs, out_refs = refs[:nw], refs[nw:2 * nw]
        send_sems, recv_sems = refs[2 * nw:]
        x, y, c = lax.axis_index("x"), lax.axis_index("y"), lax.axis_index("c")
        chips = [(1 - x, y), (x, 1 - y), (1 - x, 1 - y)]
        copies = []
        for w in range(nw):
            for j, (px, py) in enumerate(chips):
                give = pltpu.make_async_remote_copy(
                    src_ref=p_refs[w].at[2 * px + py], dst_ref=out_refs[w].at[j], send_sem=send_sems.at[3 * w + j],
                    recv_sem=recv_sems.at[3 * w + j], device_id=(px, py, c), device_id_type=pl.DeviceIdType.MESH)
                give.start()
                copies.append(give)
        for cp in copies:
            cp.wait()

    hbm = pl.BlockSpec(memory_space=pl.ANY)
    return pl.pallas_call(
        body, name=name, out_shape=[SDS((3,) + p.shape[1:], p.dtype) for p in parts], in_specs=[hbm] * nw, out_specs=[hbm] * nw,
        scratch_shapes=[pltpu.SemaphoreType.DMA((3 * nw,)), pltpu.SemaphoreType.DMA((3 * nw,))],
    )(*parts)


_HBM = pl.BlockSpec(memory_space=pltpu.HBM)
_SEM = pl.BlockSpec(memory_space=pltpu.SEMAPHORE)
_DATAFLOW = pltpu.SideEffectType.DATAFLOW_SIDE_EFFECTING


def _spread_start_call(srcs, per_peer, name, after):
    nw = len(srcs)
    lands = [lax.empty((N_DEV,) + (s.shape[1:] if per_peer else s.shape), s.dtype) for s in srcs]

    def body(*refs):
        src_refs, land_refs = refs[:nw], refs[nw:2 * nw]
        send_sems, recv_sems, token = refs[2 * nw + 1], refs[2 * nw + 2], refs[-1]
        x, y, c = lax.axis_index("x"), lax.axis_index("y"), lax.axis_index("c")
        me = 4 * x + 2 * y + c
        for w in range(nw):
            for k in range(1, N_DEV):
                px = 1 - x if k & 4 else x
                py = 1 - y if k & 2 else y
                pc = 1 - c if k & 1 else c
                src = src_refs[w].at[4 * px + 2 * py + pc] if per_peer else src_refs[w]
                pltpu.make_async_remote_copy(
                    src_ref=src, dst_ref=land_refs[w].at[me], send_sem=send_sems.at[w], recv_sem=recv_sems.at[w],
                    device_id=(px, py, pc), device_id_type=pl.DeviceIdType.MESH).start()
        token[...] = jnp.zeros_like(token)

    hbm = lambda a: pltpu.with_memory_space_constraint(a, pltpu.HBM)
    res = pl.pallas_call(
        body, name=name,
        out_shape=(pltpu.SemaphoreType.DMA((nw,)), pltpu.SemaphoreType.DMA((nw,)))
        + tuple(pltpu.HBM(s.shape, s.dtype) for s in srcs) + tuple(pltpu.HBM(l.shape, l.dtype) for l in lands)
        + (SDS((SUBLANES, LANES), F32),),
        in_specs=[_HBM] * (2 * nw) + [pl.BlockSpec(memory_space=pl.ANY)],
        out_specs=(_SEM, _SEM) + (_HBM,) * (2 * nw) + (pl.BlockSpec(memory_space=pltpu.VMEM),),
        input_output_aliases={i: i + 2 for i in range(2 * nw)},
        compiler_params=pltpu.CompilerParams(has_side_effects=_DATAFLOW),
    )(*[hbm(s) for s in srcs], *[hbm(l) for l in lands], after)
    return res[0], res[1], res[2:2 + nw], res[2 + nw:2 + 2 * nw], res[-1]


def _spread_wait_call(send_sems, recv_sems, srcs, lands, after, name):
    nw = len(lands)

    def body(*refs):
        land_refs = refs[nw:2 * nw]
        s_sems, r_sems = refs[2 * nw], refs[2 * nw + 1]
        x, y, c = lax.axis_index("x"), lax.axis_index("y"), lax.axis_index("c")
        for w in range(nw):
            seven = land_refs[w].at[pl.ds(0, N_DEV - 1)]
            all_seven = pltpu.make_async_remote_copy(
                src_ref=seven, dst_ref=seven, send_sem=s_sems.at[w], recv_sem=r_sems.at[w],
                device_id=(x, y, c), device_id_type=pl.DeviceIdType.MESH)
            all_seven.wait_send()
            all_seven.wait_recv()

    res = pl.pallas_call(
        body, name=name,
        out_shape=tuple(pltpu.HBM(s.shape, s.dtype) for s in srcs) + tuple(pltpu.HBM(l.shape, l.dtype) for l in lands),
        in_specs=[_HBM] * (2 * nw) + [_SEM, _SEM, pl.BlockSpec(memory_space=pl.ANY)], out_specs=(_HBM,) * (2 * nw),
        input_output_aliases={i: i for i in range(2 * nw)},
        compiler_params=pltpu.CompilerParams(has_side_effects=_DATAFLOW),
    )(*srcs, *lands, send_sems, recv_sems, after)
    return res[:nw], res[nw:]


def _pair_sum_call(g, got, core, name):
    _, k, n = got.shape
    tr = _tile(k, 256)

    def body(c_ref, g_ref, got_ref, o_ref):
        o_ref[...] = (g_ref[...] + got_ref[...]).astype(o_ref.dtype)

    spec = pltpu.PrefetchScalarGridSpec(
        num_scalar_prefetch=1, grid=(4, k // tr),
        in_specs=[pl.BlockSpec((1, tr, n), lambda j, i, c: (2 * j + c[0], i, 0)), pl.BlockSpec((1, tr, n), lambda j, i, c: (j, i, 0))],
        out_specs=pl.BlockSpec((1, tr, n), lambda j, i, c: (j, i, 0)))
    return pl.pallas_call(body, name=name, grid_spec=spec, out_shape=SDS(got.shape, BF16),
                          compiler_params=_params(("parallel", "parallel"), VMEM_MID))(core, g, got)


def _adam_own_call(pair, chip, recv, w, m, v, name, rows):
    _, r, cols = recv.shape

    def body(chip_ref, p_ref, g_ref, w_ref, m_ref, v_ref, go_ref, d_ref, mo_ref, vo_ref):
        g = ((p_ref[0].astype(F32) + g_ref[0].astype(F32)) + g_ref[1].astype(F32)) + g_ref[2].astype(F32)
        for o_ref, val in zip((go_ref, d_ref, mo_ref, vo_ref), _adamw(g, w_ref[...], m_ref[...], v_ref[...])):
            o_ref[...] = val

    blk = pl.BlockSpec((rows, cols), lambda i, s: (i, 0))
    spec = pltpu.PrefetchScalarGridSpec(
        num_scalar_prefetch=1, grid=(r // rows,),
        in_specs=[pl.BlockSpec((1, rows, cols), lambda i, s: (s[0], i, 0)), pl.BlockSpec((3, rows, cols), lambda i, s: (0, i, 0)),
                  blk, blk, blk],
        out_specs=[blk, blk, blk, blk])
    return pl.pallas_call(body, name=name, grid_spec=spec, out_shape=[SDS((r, cols), F32)] * 4,
                          compiler_params=_params(("parallel",), VMEM_MID))(chip, pair, recv, w, m, v)


def _join_cols_call(w8, name):
    _, k, n = w8.shape
    tk = _tile(k, 256)

    def body(w_ref, o_ref):
        for s in range(N_DEV):
            o_ref[:, n * s:n * (s + 1)] = w_ref[s]

    return pl.pallas_call(body, name=name, grid=(k // tk,), in_specs=[pl.BlockSpec((N_DEV, tk, n), lambda i: (0, i, 0))],
                          out_specs=pl.BlockSpec((tk, N_DEV * n), lambda i: (i, 0)), out_shape=SDS((k, N_DEV * n), w8.dtype),
                          compiler_params=_params(("parallel",), VMEM_MID))(w8)


def _split_cols_call(g, name, dtype):
    k, n8 = g.shape
    n = n8 // N_DEV
    tk = _tile(k, 256)

    def body(g_ref, o_ref):
        for s in range(N_DEV):
            o_ref[s] = g_ref[:, n * s:n * (s + 1)].astype(dtype)

    return pl.pallas_call(body, name=name, grid=(k // tk,), in_specs=[pl.BlockSpec((tk, n8), lambda i: (i, 0))],
                          out_specs=pl.BlockSpec((N_DEV, tk, n), lambda i: (0, i, 0)), out_shape=SDS((N_DEV, k, n), dtype),
                          compiler_params=_params(("parallel",), VMEM_MID))(g)


def _pack(parts, rows_multiple):
    flat = jnp.concatenate([p.reshape(-1) for p in parts])
    unit = rows_multiple * LANES
    padded = -(-flat.shape[0] // unit) * unit
    flat = jnp.concatenate([flat, jnp.zeros((padded - flat.shape[0],), F32)])
    return flat.reshape(-1, LANES)


def _groups_last(a):
    x, y = a.shape[-2:]
    return jnp.transpose(a.reshape(S5_GROUPS, x, y), (1, 2, 0)).reshape(x * y, S5_GROUPS)


def _groups_first(a, shape):
    x, y = shape[-2:]
    return jnp.transpose(a.reshape(x, y, S5_GROUPS), (2, 0, 1)).reshape(shape)


def _unpack(buf, shapes):
    flat = buf.reshape(-1)
    out, off = [], 0
    for s in shapes:
        n = math.prod(s)
        out.append(flat[off:off + n].reshape(s))
        off += n
    return out


def _row_tile(L):
    return 256 if L % 256 == 0 else L


def _layer0_mix(diff, const):
    x, mod, norm_w, lam_re, lam_im, log_dt, b_re, b_im, c_re, c_im, s5_d, *slots = diff
    ada_b, weights = const
    L = x.shape[0]
    tm = _row_tile(L)
    mods = mod.reshape(2, 1, D_MODEL)
    biases = ada_b.reshape(2, 1, D_MODEL)
    op_ln0 = make_rowwise(_f_lnmod, "ln0", tm, 1, 5, pass_first=True)
    h, x = op_ln0((x,), (norm_w.reshape(1, D_MODEL), mods[1], mods[0], biases[1], biases[0]))
    u, z = make_proj("s5_in")(h, tuple(weights), tuple(slots))
    blocks = _s5_block_params(lam_re, lam_im, log_dt, b_re, b_im, c_re, c_im)
    y2 = make_s5_core(min(S5_TL, L))(u, *blocks, s5_d.reshape(1, D_INNER))
    return x, y2, z


def _layer0_out(diff, weights):
    y2, z, *slots = diff
    tm = _row_tile(y2.shape[0])
    t, y2 = make_mm("s5_glu", pass_input=True)(y2, weights[0], slots[0])
    (y4,) = make_rowwise(_f_s5_gate, "s5_gate", tm, 3, 0)((y2, t, z), ())
    return make_mm("s5_out")(y4, weights[1], slots[1])


def _f_res_lnmod(x, o, gate, bgate, nw, sc, sh, bsc, bsh):
    (x1,) = _f_res(x, o, gate, bgate)
    return _f_lnmod(x1, nw, sc, sh, bsc, bsh) + (x1,)


def _f_res_loss(x, y, tgt, gate, bgate, fw):
    return _f_loss(_f_res(x, y, gate, bgate)[0], tgt, fw)


def _layer1_loss(diff, const):
    x, o, gate0, mod, norm_w, conv_w, a_log, dt_bias, gdn_nw, final_nw, *slots = diff
    tgt, bgate0, ada_b, weights = const
    L = x.shape[0]
    tm = _row_tile(L)
    mods = mod.reshape(3, 1, D_MODEL)
    biases = ada_b.reshape(3, 1, D_MODEL)
    h, x1 = make_rowwise(_f_res_lnmod, "res0_ln1", tm, 2, 7)(
        (x, o), (gate0.reshape(1, D_MODEL), bgate0.reshape(1, D_MODEL), norm_w.reshape(1, D_MODEL), mods[1], mods[0], biases[1], biases[0]))
    q0, k0, v0, gz, ba = make_proj("gdn_in", w_rows=True)(h, tuple(weights[0:5]), tuple(slots[0:5]))
    cw = jnp.concatenate([conv_w, jnp.zeros((SUBLANES - GDN_CONV, GDN_CONV_CH), F32)], axis=0)
    q = make_conv_act(lambda t: _l2n(_silu(t)) * (GDN_DK ** -0.5), "gdn_conv_q")(q0, cw[:, :GDN_QK])
    k = make_conv_act(lambda t: _l2n(_silu(t)), "gdn_conv_k")(k0, cw[:, GDN_QK:2 * GDN_QK])
    v = make_conv_act(_silu, "gdn_conv_v")(v0, cw[:, 2 * GDN_QK:])
    pad = jnp.zeros((LANES - 2 * GDN_HEADS,), F32)
    alog_row = jnp.concatenate([jnp.zeros((GDN_HEADS,), F32), a_log, pad]).reshape(1, LANES)
    dtb_row = jnp.concatenate([jnp.zeros((GDN_HEADS,), F32), dt_bias, pad]).reshape(1, LANES)
    (bg,) = make_rowwise(_f_betag, "gdn_bg", tm, 1, 2)((ba,), (alog_row, dtb_row))
    nw_row = jnp.tile(gdn_nw, GDN_HEADS).reshape(1, D_INNER)
    on = gdn_scan(*gdn_prep(q, k, v, bg), gz, nw_row)
    y = make_mm("gdn_out")(on, weights[5], slots[5])
    (lt,) = make_rowwise(_f_res_loss, "res1_loss", tm, 3, 3)((x1, y, tgt), (mods[2], biases[2], final_nw.reshape(1, D_MODEL)))
    return jnp.sum(lt)


VEC_NAMES = ("ada_b", "norm_w", "s5_lambda_re", "s5_lambda_im", "s5_log_dt", "s5_d", "gdn_a_log", "gdn_dt_bias", "final_norm_w")
MAT_NAMES = ("s5_b_re", "s5_b_im", "s5_c_re", "s5_c_im")
S5_BIG = ("s5_w_in", "s5_w_glu", "s5_w_out")
GDN_BIG = ("gdn_w_in", "gdn_w_out")
BIG_NAMES = S5_BIG + GDN_BIG
WEIGHT_ORDER = ("ada_w", "ada_b", "norm_w", "s5_w_in", "s5_lambda_re", "s5_lambda_im", "s5_log_dt", "s5_b_re", "s5_b_im",
                "s5_c_re", "s5_c_im", "s5_d", "s5_w_glu", "s5_w_out", "gdn_w_in", "gdn_conv_w", "gdn_a_log", "gdn_dt_bias",
                "gdn_norm_w", "gdn_w_out", "final_norm_w")


def _step(x, c, W, M, V, tgt):
    L = x.shape[1]
    ix, iy, ic = lax.axis_index("x"), lax.axis_index("y"), lax.axis_index("c")
    me = 4 * ix + 2 * iy + ic
    n_ada = W["ada_w"].shape[2]
    n_conv = W["gdn_conv_w"].shape[2]
    n_gnw = W["gdn_norm_w"].shape[1]

    g1 = _allgather_call(_pack([c, W["gdn_conv_w"], W["gdn_norm_w"]], SUBLANES), "gather_small_in")
    g1 = g1.reshape(N_DEV, -1)
    c_all = g1[:, :D_MODEL]
    conv_w = g1[:, D_MODEL:D_MODEL + GDN_CONV * n_conv].reshape(N_DEV, GDN_CONV, n_conv).transpose(1, 0, 2).reshape(GDN_CONV, -1)
    gdn_nw = g1[:, D_MODEL + GDN_CONV * n_conv:D_MODEL + GDN_CONV * n_conv + n_gnw].reshape(-1)
    mod_part = _ada_mod_call(c_all, W["ada_w"])
    g2 = _allgather_call(_pack([mod_part], SUBLANES), "gather_mod").reshape(N_DEV, -1)
    mod_all = g2[:, :2 * N_DEV * n_ada].reshape(N_DEV, 2, N_DEV, n_ada)
    mod_raw = lax.dynamic_index_in_dim(mod_all, me, axis=2, keepdims=False)
    mod_raw = mod_raw.transpose(1, 0, 2).reshape(2, 3 * D_MODEL)

    shard = lambda n: W[n][0].astype(BF16)
    (w_in5_parts,) = _gather_weights_call([shard("s5_w_in")], "gather_s5_w_in")
    late = _spread_start_call([shard("s5_w_glu"), shard("s5_w_out")], False, "gather_s5_late_start", w_in5_parts)
    turned = lambda a: jnp.transpose(a[0])
    g_send, g_recv, g_srcs, g_lands, g_token = _spread_start_call(
        [turned(W["gdn_w_in"]).astype(BF16), shard("gdn_w_out")], False, "gather_gdn_start", late[4])
    w_in5 = _join_cols_call(w_in5_parts, "join_s5_w_in")
    slot = lambda *s: jnp.zeros(s, F32)
    two = 2 * D_MODEL
    diff_mix = (x[0], mod_raw[0, :two] + g_token[0, 0], W["norm_w"][0], W["s5_lambda_re"][0], W["s5_lambda_im"][0], W["s5_log_dt"][0],
                W["s5_b_re"][0], W["s5_b_im"][0], W["s5_c_re"][0], W["s5_c_im"][0], W["s5_d"][0],
                slot(D_MODEL, D_INNER), slot(D_MODEL, D_INNER))

    (xp, y2, z5), vjp_mix = jax.vjp(lambda d: _layer0_mix(d, (W["ada_b"][0, :two], (w_in5[:, :D_INNER], w_in5[:, D_INNER:]))), diff_mix)
    l_srcs, l_lands = _spread_wait_call(late[0], late[1], late[2], late[3], y2, "gather_s5_late_wait")
    w_glu, w_o5 = [lax.dynamic_update_slice(land, src[None], (me, 0, 0)).reshape(-1, src.shape[1]) for land, src in zip(l_lands, l_srcs)]
    diff_out = (y2, z5, slot(D_INNER, D_INNER), slot(D_INNER, D_MODEL))
    o5, vjp_out = jax.vjp(lambda d: _layer0_out(d, (w_glu, w_o5)), diff_out)
    g_srcs, g_lands = _spread_wait_call(g_send, g_recv, g_srcs, g_lands, o5, "gather_gdn_wait")
    gdn_full = [lax.dynamic_update_slice(land, src[None], (me, 0, 0)) for land, src in zip(g_lands, g_srcs)]
    w_ing = gdn_full[0].reshape(GDN_PROJ, D_MODEL)
    w_ba = jnp.concatenate([w_ing[GDN_CONV_CH + D_INNER:], jnp.zeros((LANES - 2 * GDN_HEADS, D_MODEL), BF16)], axis=0)
    weights1 = (w_ing[:GDN_QK], w_ing[GDN_QK:2 * GDN_QK], w_ing[2 * GDN_QK:GDN_CONV_CH],
                w_ing[GDN_CONV_CH:GDN_CONV_CH + D_INNER], w_ba, gdn_full[1].reshape(D_INNER, D_MODEL))
    slots1 = tuple(jnp.zeros(w.shape, F32) for w in weights1)
    diff1 = (xp, o5, mod_raw[0, two:], mod_raw[1], W["norm_w"][1], conv_w, W["gdn_a_log"][0], W["gdn_dt_bias"][0], gdn_nw,
             W["final_norm_w"], *slots1)
    loss_local, vjp1 = jax.vjp(lambda d: _layer1_loss(d, (tgt[0], W["ada_b"][0, two:], W["ada_b"][1], weights1)), diff1)
    ((dxp, do5, dmod_gate, dmod1, d_norm_w1, d_conv, d_alog, d_dtb, d_gnw, d_fnw, d_wq, d_wk, d_wv, d_wgz, d_wba, d_wog),) = vjp1(
        jnp.ones((), F32))
    loss = lax.psum(loss_local, MESH_AXES)

    rows = lambda d: d.reshape(N_DEV, d.shape[0] // N_DEV, d.shape[1])
    d_ing = jnp.concatenate([d_wq, d_wk, d_wv, d_wgz, d_wba[:2 * GDN_HEADS]], axis=0).astype(BF16).reshape(N_DEV, -1, D_MODEL)
    s_send, s_recv, s_srcs, s_lands, s_token = _spread_start_call([d_ing, rows(d_wog).astype(BF16)], True, "scatter_gdn_start", dxp)
    ((dy2, dz5, d_wglu, d_wo5),) = vjp_out(do5.at[0, 0].add(s_token[0, 0]))
    t_send, t_recv, t_srcs, t_lands, t_token = _spread_start_call(
        [rows(d_wglu).astype(BF16), rows(d_wo5).astype(BF16)], True, "scatter_s5_late_start", dy2)
    ((dx, dmod_ss, d_norm_w0, d_lre, d_lim, d_logdt, d_bre, d_bim, d_cre, d_cim, d_s5d, d_wu, d_wz),) = vjp_mix(
        (dxp.at[0, 0].add(t_token[0, 0]), dy2, dz5))
    dmod = jnp.stack([jnp.concatenate([dmod_ss, dmod_gate]), dmod1])
    d_norm_w = jnp.stack([d_norm_w0, d_norm_w1])
    vec_parts = [dmod, d_norm_w, d_lre, d_lim, d_logdt, d_s5d, d_alog, d_dtb, d_fnw]
    tail_parts = [d_conv, d_gnw]
    mat_parts = [_groups_last(d) for d in (d_bre, d_bim, d_cre, d_cim)]
    n_vec = sum(math.prod(p.shape) for p in vec_parts)
    m_send, m_recv, m_srcs, m_lands, m_token = _spread_start_call(
        [_pack(vec_parts + tail_parts, ADAM_ROWS), _pack(mat_parts, SUBLANES).astype(BF16)], False, "gather_small_grads_start", dx)
    d_in5 = _split_cols_call(jnp.concatenate([d_wu.at[0, 0].add(m_token[0, 0]), d_wz], axis=1), "split_s5_w_in", BF16)
    u_send, u_recv, u_srcs, u_lands, u_token = _spread_start_call([d_in5], True, "scatter_s5_in_start", m_token)
    t_srcs, t_lands = _spread_wait_call(t_send, t_recv, t_srcs, t_lands, u_token, "scatter_s5_late_wait")
    s_srcs, s_lands = _spread_wait_call(s_send, s_recv, s_srcs, s_lands, t_lands[0], "scatter_gdn_wait")
    big = {}

    def owner_update(land, src, n):
        mine = lax.dynamic_index_in_dim(src, me, 0, keepdims=True)
        parts = lax.dynamic_update_slice(land, mine, (me, 0, 0))
        if n == "gdn_w_in":
            outs = _adam_call(parts, turned(W[n]), turned(M[n]), turned(V[n]), "adam_" + n, by_cols=True)
            return [jnp.transpose(o) for o in outs]
        return _adam_call(parts, W[n][0], M[n][0], V[n][0], "adam_" + n, rows=_tile(W[n].shape[1], 128))

    for land, src, n in zip(tuple(t_lands) + tuple(s_lands), tuple(t_srcs) + tuple(s_srcs), ("s5_w_glu", "s5_w_out") + GDN_BIG):
        big[n] = owner_update(land, src, n)

    m_srcs, m_lands = _spread_wait_call(m_send, m_recv, m_srcs, m_lands, big["gdn_w_out"][0], "gather_small_grads_wait")
    sg_vec, sg_mat = [lax.dynamic_update_slice(land, src[None], (me, 0, 0)) for land, src in zip(m_lands, m_srcs)]
    tot_vec = _sum_call(sg_vec, "sum_vec_grads", ADAM_ROWS)
    tot_mat = _sum_call(sg_mat, "sum_mat_grads", ADAM_ROWS)
    g_conv, g_gnw = _unpack(tot_vec.reshape(-1)[n_vec:], [d_conv.shape, d_gnw.shape])
    g_conv_mine = lax.dynamic_slice_in_dim(g_conv, me * n_conv, n_conv, axis=1)
    g_gnw_mine = lax.dynamic_slice_in_dim(g_gnw, me * n_gnw, n_gnw, axis=0)
    vec_names = VEC_NAMES + ("gdn_conv_w", "gdn_norm_w")
    vec_g = _pack([tot_vec.reshape(-1)[:n_vec], g_conv_mine, g_gnw_mine], ADAM_ROWS)
    vec = _adam_call(vec_g[None], _pack([W[n] for n in vec_names], ADAM_ROWS), _pack([M[n] for n in vec_names], ADAM_ROWS),
                     _pack([V[n] for n in vec_names], ADAM_ROWS), "adam_vec")
    vec = [_unpack(b, [W[n].shape for n in vec_names]) for b in vec]
    mats = []
    for name, g_mat in zip(MAT_NAMES, _unpack(tot_mat, [p.shape for p in mat_parts])):
        outs = _adam_call(g_mat[None], _groups_last(W[name]), _groups_last(M[name]), _groups_last(V[name]), "adam_" + name)
        mats.append([_groups_first(o, W[name].shape) for o in outs])

    dmod_all = sg_vec[:, :2 * 3 * D_MODEL // LANES].reshape(N_DEV, 2, N_DEV, n_ada // LANES, LANES)
    dmod_mine = lax.dynamic_index_in_dim(dmod_all, me, axis=2, keepdims=False).transpose(1, 0, 2, 3).reshape(2, N_DEV, n_ada)
    g_ada_w = _ada_grad_call(c_all, dmod_mine)
    ada = _adam_call(g_ada_w.reshape(1, -1, LANES), W["ada_w"].reshape(-1, LANES), M["ada_w"].reshape(-1, LANES),
                     V["ada_w"].reshape(-1, LANES), "adam_ada")
    u_srcs, u_lands = _spread_wait_call(u_send, u_recv, u_srcs, u_lands, ada[0], "scatter_s5_in_wait")
    big["s5_w_in"] = owner_update(u_lands[0], u_srcs[0], "s5_w_in")
    ada = [a.reshape(W["ada_w"].shape) for a in ada]

    res = {}
    for n in BIG_NAMES:
        res[n] = [o[None] for o in big[n]]
    for i, n in enumerate(vec_names):
        res[n] = [b[i] for b in vec]
    for i, n in enumerate(MAT_NAMES):
        res[n] = mats[i]
    res["ada_w"] = ada
    outs = [loss, dx[None]]
    for j in range(4):
        outs += [res[n][j] for n in WEIGHT_ORDER]
    return tuple(outs)


def kernel(x, c, ada_w, ada_b, norm_w, s5_w_in, s5_lambda_re, s5_lambda_im, s5_log_dt, s5_b_re, s5_b_im, s5_c_re, s5_c_im, s5_d, s5_w_glu, s5_w_out, gdn_w_in, gdn_conv_w, gdn_a_log, gdn_dt_bias, gdn_norm_w, gdn_w_out, final_norm_w, loss_target, m_ada_w, m_ada_b, m_norm_w, m_s5_w_in, m_s5_lambda_re, m_s5_lambda_im, m_s5_log_dt, m_s5_b_re, m_s5_b_im, m_s5_c_re, m_s5_c_im, m_s5_d, m_s5_w_glu, m_s5_w_out, m_gdn_w_in, m_gdn_conv_w, m_gdn_a_log, m_gdn_dt_bias, m_gdn_norm_w, m_gdn_w_out, m_final_norm_w, v_ada_w, v_ada_b, v_norm_w, v_s5_w_in, v_s5_lambda_re, v_s5_lambda_im, v_s5_log_dt, v_s5_b_re, v_s5_b_im, v_s5_c_re, v_s5_c_im, v_s5_d, v_s5_w_glu, v_s5_w_out, v_gdn_w_in, v_gdn_conv_w, v_gdn_a_log, v_gdn_dt_bias, v_gdn_norm_w, v_gdn_w_out, v_final_norm_w):
    W = dict(ada_w=ada_w, ada_b=ada_b, norm_w=norm_w, s5_w_in=s5_w_in, s5_lambda_re=s5_lambda_re, s5_lambda_im=s5_lambda_im,
             s5_log_dt=s5_log_dt, s5_b_re=s5_b_re, s5_b_im=s5_b_im, s5_c_re=s5_c_re, s5_c_im=s5_c_im, s5_d=s5_d,
             s5_w_glu=s5_w_glu, s5_w_out=s5_w_out, gdn_w_in=gdn_w_in, gdn_conv_w=gdn_conv_w, gdn_a_log=gdn_a_log,
             gdn_dt_bias=gdn_dt_bias, gdn_norm_w=gdn_norm_w, gdn_w_out=gdn_w_out, final_norm_w=final_norm_w)
    M = dict(ada_w=m_ada_w, ada_b=m_ada_b, norm_w=m_norm_w, s5_w_in=m_s5_w_in, s5_lambda_re=m_s5_lambda_re,
             s5_lambda_im=m_s5_lambda_im, s5_log_dt=m_s5_log_dt, s5_b_re=m_s5_b_re, s5_b_im=m_s5_b_im, s5_c_re=m_s5_c_re,
             s5_c_im=m_s5_c_im, s5_d=m_s5_d, s5_w_glu=m_s5_w_glu, s5_w_out=m_s5_w_out, gdn_w_in=m_gdn_w_in,
             gdn_conv_w=m_gdn_conv_w, gdn_a_log=m_gdn_a_log, gdn_dt_bias=m_gdn_dt_bias, gdn_norm_w=m_gdn_norm_w,
             gdn_w_out=m_gdn_w_out, final_norm_w=m_final_norm_w)
    V = dict(ada_w=v_ada_w, ada_b=v_ada_b, norm_w=v_norm_w, s5_w_in=v_s5_w_in, s5_lambda_re=v_s5_lambda_re,
             s5_lambda_im=v_s5_lambda_im, s5_log_dt=v_s5_log_dt, s5_b_re=v_s5_b_re, s5_b_im=v_s5_b_im, s5_c_re=v_s5_c_re,
             s5_c_im=v_s5_c_im, s5_d=v_s5_d, s5_w_glu=v_s5_w_glu, s5_w_out=v_s5_w_out, gdn_w_in=v_gdn_w_in,
             gdn_conv_w=v_gdn_conv_w, gdn_a_log=v_gdn_a_log, gdn_dt_bias=v_gdn_dt_bias, gdn_norm_w=v_gdn_norm_w,
             gdn_w_out=v_gdn_w_out, final_norm_w=v_final_norm_w)
    return _step(x, c, W, M, V, loss_target)
```

```python
import functools
import math

import jax
import jax.numpy as jnp
from jax import lax
from jax.experimental import pallas as pl
from jax.experimental.pallas import tpu as pltpu

F32 = jnp.float32
BF16 = jnp.bfloat16
SDS = jax.ShapeDtypeStruct

D_MODEL = 1024
D_INNER = 2048
NORM_EPS = 1e-6
S5_GROUP = 16
S5_GROUPS = 128
S5_STATE = 64
GDN_HEADS = 8
GDN_DK = 128
GDN_DV = 256
GDN_CONV = 4
GDN_CHUNK = 64
GDN_QK = 1024
GDN_CONV_CH = 4096
GDN_PROJ = 6160
ADAM_LR = 0.001
ADAM_B1 = 0.9
ADAM_B2 = 0.999
ADAM_EPS = 1e-08
ADAM_WD = 0.01
ADAM_STEP = 10

N_DEV = 8
LANES = 128
SUBLANES = 8
VMEM_BIG = 56 << 20
VMEM_MID = 40 << 20
S5_GB = 8
S5_TL = 2048
MESH_AXES = ("x", "y", "c")


def _params(sem, vmem=None):
    return pltpu.CompilerParams(dimension_semantics=sem, vmem_limit_bytes=vmem)


def _bdot(a, b, dims=(((1,), (0,)), ((), ()))):
    return lax.dot_general(a.astype(BF16), b.astype(BF16), dims, preferred_element_type=F32)


def _hdot(a, b, dims=(((1,), (0,)), ((), ()))):
    return lax.dot_general(a, b, dims, preferred_element_type=F32, precision=lax.Precision.HIGHEST)


_BNN = (((2,), (1,)), ((0,), (0,)))
_BNT = (((2,), (2,)), ((0,), (0,)))
_BTN = (((1,), (1,)), ((0,), (0,)))


def _dot3(a, b, dims):
    a_hi = a.astype(BF16)
    b_hi = b.astype(BF16)
    a_lo = (a - a_hi.astype(F32)).astype(BF16)
    b_lo = (b - b_hi.astype(F32)).astype(BF16)
    dot = functools.partial(lax.dot_general, dimension_numbers=dims, preferred_element_type=F32)
    return dot(a_hi, b_hi) + (dot(a_hi, b_lo) + dot(a_lo, b_hi))


@jax.custom_vjp
def _unit_lower_inverse(a):
    c = a.shape[-1]
    ri = lax.broadcasted_iota(jnp.int32, a.shape, 1)
    ci = lax.broadcasted_iota(jnp.int32, a.shape, 2)
    eye = (ri == ci).astype(F32)
    n = -a
    t = eye + n
    for _ in range(int(math.log2(c)) - 1):
        n = _dot3(n, n, _BNN)
        t = t + _dot3(t, n, _BNN)
    return _hdot(t, 2.0 * eye - _hdot(eye + a, t, _BNN), _BNN)


def _unit_lower_inverse_fwd(a):
    t = _unit_lower_inverse(a)
    return t, t


def _unit_lower_inverse_bwd(t, g):
    return (-_hdot(_hdot(t, g, _BTN), t, _BNT),)


_unit_lower_inverse.defvjp(_unit_lower_inverse_fwd, _unit_lower_inverse_bwd)


NN = (((1,), (0,)), ((), ()))
NT = (((1,), (1,)), ((), ()))
TN = (((0,), (0,)), ((), ()))


def _tile(n, pref):
    for t in (pref, 512, 256, 128):
        if t <= n and n % t == 0:
            return t
    return n


def _matmul(a, b, mode, name, add=None):
    if mode == "nn":
        (m, k), (_, n) = a.shape, b.shape
    elif mode == "nt":
        (m, k), (n, _) = a.shape, b.shape
    else:
        (k, m), (_, n) = a.shape, b.shape
    tm, tn, tk = _tile(m, 1024), _tile(n, 512), (k if k <= 2048 else _tile(k, 512))
    if mode == "tn":
        tm, tn, tk = _tile(m, 1024), _tile(n, 1024), _tile(k, 1024)
    nk = k // tk
    dims = {"nn": NN, "nt": NT, "tn": TN}[mode]

    def body(a_ref, b_ref, *rest):
        o_ref, acc_ref = rest[-2], rest[-1]
        part = _bdot(a_ref[...], b_ref[...], dims)
        if nk == 1:
            o_ref[...] = part if add is None else part + rest[0][...]
            return
        kk = pl.program_id(2)

        @pl.when(kk == 0)
        def _():
            acc_ref[...] = part if add is None else part + rest[0][...]

        @pl.when(kk > 0)
        def _():
            acc_ref[...] += part

        @pl.when(kk == nk - 1)
        def _():
            o_ref[...] = acc_ref[...]

    a_spec = pl.BlockSpec((tk, tm), lambda i, j, q: (q, i)) if mode == "tn" else pl.BlockSpec((tm, tk), lambda i, j, q: (i, q))
    b_spec = pl.BlockSpec((tn, tk), lambda i, j, q: (j, q)) if mode == "nt" else pl.BlockSpec((tk, tn), lambda i, j, q: (q, j))
    o_spec = pl.BlockSpec((tm, tn), lambda i, j, q: (i, j))
    return pl.pallas_call(
        body, name=name, grid=(m // tm, n // tn, nk),
        in_specs=[a_spec, b_spec] + ([] if add is None else [o_spec]), out_specs=o_spec,
        out_shape=SDS((m, n), F32), scratch_shapes=[pltpu.VMEM((tm, tn), F32)],
        compiler_params=_params(("parallel", "parallel", "arbitrary"), VMEM_MID),
    )(a, b, *([] if add is None else [add]))


def make_mm(name, pass_input=False):
    def primal(a, w):
        out = _matmul(a, w, "nn", name + "_fwd")
        return (out, a) if pass_input else out

    @jax.custom_vjp
    def mm(a, w, grad_slot):
        return primal(a, w)

    def fwd(a, w, grad_slot):
        return primal(a, w), (a, w)

    def bwd(res, g):
        a, w = res
        g, g_other = g if pass_input else (g, None)
        return _matmul(g, w, "nt", name + "_dx", add=g_other), jnp.zeros_like(w), _matmul(a, g, "tn", name + "_dw")

    mm.defvjp(fwd, bwd)
    return mm


PROJ_ROWS = 256


def _proj_fwd_call(a, ws, name, w_rows):
    m, k = a.shape
    tm = _tile(m, PROJ_ROWS)
    nw = len(ws)
    widths = [w.shape[0] if w_rows else w.shape[1] for w in ws]

    def body(*refs):
        ab = refs[0][...].astype(BF16)
        for w_ref, o_ref in zip(refs[1:1 + nw], refs[1 + nw:]):
            o_ref[...] = lax.dot_general(ab, w_ref[...], NT if w_rows else NN, preferred_element_type=F32)

    return pl.pallas_call(
        body, name=name, grid=(m // tm,),
        in_specs=[pl.BlockSpec((tm, k), lambda i: (i, 0))] + [pl.BlockSpec(w.shape, lambda i: (0, 0)) for w in ws],
        out_specs=[pl.BlockSpec((tm, n), lambda i: (i, 0)) for n in widths],
        out_shape=[SDS((m, n), F32) for n in widths],
        compiler_params=_params(("parallel",), VMEM_BIG),
    )(a, *ws)


def _proj_dx_call(gs, ws, name, w_rows):
    m = gs[0].shape[0]
    k = ws[0].shape[1] if w_rows else ws[0].shape[0]
    tm = _tile(m, PROJ_ROWS)
    nw = len(ws)

    def body(*refs):
        acc = None
        for g_ref, w_ref in zip(refs[:nw], refs[nw:2 * nw]):
            part = _bdot(g_ref[...], w_ref[...], NN if w_rows else NT)
            acc = part if acc is None else acc + part
        refs[2 * nw][...] = acc

    return pl.pallas_call(
        body, name=name, grid=(m // tm,),
        in_specs=[pl.BlockSpec((tm, g.shape[1]), lambda i: (i, 0)) for g in gs] + [pl.BlockSpec(w.shape, lambda i: (0, 0)) for w in ws],
        out_specs=pl.BlockSpec((tm, k), lambda i: (i, 0)), out_shape=SDS((m, k), F32),
        compiler_params=_params(("parallel",), VMEM_BIG),
    )(*gs, *ws)


def make_proj(name, w_rows=False):
    @jax.custom_vjp
    def proj(a, ws, grad_slots):
        return tuple(_proj_fwd_call(a, ws, name + "_fwd", w_rows))

    def fwd(a, ws, grad_slots):
        return tuple(_proj_fwd_call(a, ws, name + "_fwd", w_rows)), (a, ws)

    def bwd(res, gs):
        a, ws = res
        dws = tuple(_matmul(g, a, "tn", "%s_dw%d" % (name, i)) if w_rows else _matmul(a, g, "tn", "%s_dw%d" % (name, i))
                    for i, g in enumerate(gs))
        return _proj_dx_call(tuple(gs), ws, name + "_dx", w_rows), tuple(jnp.zeros_like(w) for w in ws), dws

    proj.defvjp(fwd, bwd)
    return proj


def make_rowwise(f, name, tm, n_rows, n_params, vmem=VMEM_MID, pass_first=False):
    def specs_of(arrs, blocked):
        if blocked:
            return [pl.BlockSpec((tm, a.shape[1]), lambda i: (i, 0)) for a in arrs]
        return [pl.BlockSpec(a.shape, lambda i: (0, 0)) for a in arrs]

    def out_structs(rows, params):
        blk = [SDS((tm, r.shape[1]), r.dtype) for r in rows] + [SDS(p.shape, p.dtype) for p in params]
        return jax.eval_shape(f, *blk)

    def run_fwd(rows, params):
        L = rows[0].shape[0]
        outs = out_structs(rows, params)

        def body(*refs):
            ins = [r[...] for r in refs[:n_rows + n_params]]
            res = f(*ins)
            for o_ref, val in zip(refs[n_rows + n_params:], res):
                o_ref[...] = val

        return pl.pallas_call(
            body, name=name + "_fwd", grid=(L // tm,),
            in_specs=specs_of(rows, True) + specs_of(params, False),
            out_specs=[pl.BlockSpec((tm, o.shape[1]), lambda i: (i, 0)) for o in outs],
            out_shape=[SDS((L, o.shape[1]), o.dtype) for o in outs],
            compiler_params=_params(("parallel",), vmem),
        )(*rows, *params)

    def run_bwd(rows, params, gs):
        L = rows[0].shape[0]
        n_g = len(gs)

        def body(*refs):
            i = pl.program_id(0)
            ins = [r[...] for r in refs[:n_rows + n_params]]
            cts = tuple(r[...] for r in refs[n_rows + n_params:n_rows + n_params + n_g])
            outs = refs[n_rows + n_params + n_g:]
            _, vjp = jax.vjp(f, *ins)
            grads = vjp(cts[:-1] if pass_first else cts)
            if pass_first:
                grads = (grads[0] + cts[-1],) + tuple(grads[1:])
            for o_ref, val in zip(outs[:n_rows], grads[:n_rows]):
                o_ref[...] = val

            if n_params:
                @pl.when(i == 0)
                def _():
                    for o_ref in outs[n_rows:]:
                        o_ref[...] = jnp.zeros_like(o_ref)
                for o_ref, val in zip(outs[n_rows:], grads[n_rows:]):
                    o_ref[...] += val

        res = pl.pallas_call(
            body, name=name + "_bwd", grid=(L // tm,),
            in_specs=specs_of(rows, True) + specs_of(params, False) + specs_of(gs, True),
            out_specs=specs_of(rows, True) + specs_of(params, False),
            out_shape=[SDS(r.shape, r.dtype) for r in rows] + [SDS(p.shape, p.dtype) for p in params],
            compiler_params=_params(("arbitrary",), vmem),
        )(*rows, *params, *gs)
        return tuple(res[:n_rows]), tuple(res[n_rows:])

    def outputs(rows, params):
        outs = tuple(run_fwd(rows, params))
        return outs + (rows[0],) if pass_first else outs

    @jax.custom_vjp
    def op(rows, params):
        return outputs(rows, params)

    def fwd(rows, params):
        return outputs(rows, params), (rows, params)

    def bwd(res, gs):
        rows, params = res
        return run_bwd(rows, params, tuple(gs))

    op.defvjp(fwd, bwd)
    op.run_fwd, op.run_bwd = run_fwd, run_bwd
    return op


def _s5_scan_rows(xr_ref, xi_ref, ar, ai, x0r, x0i, tl, reverse=False):
    n = xr_ref.shape[1]
    T = SUBLANES
    row = lax.broadcasted_iota(jnp.int32, (T, n), 0)
    pr, pi = [ar], [ai]
    for _ in range(T - 1):
        pr, pi = pr + [pr[-1] * ar - pi[-1] * ai], pi + [pr[-1] * ai + pi[-1] * ar]
    levels = []
    for d in (1, 2, 4):
        mask = (row < T - d) if reverse else (row >= d)
        levels.append((T - d if reverse else d, jnp.where(mask, pr[d - 1], 0.0), jnp.where(mask, pi[d - 1], 0.0)))
    cr = jnp.zeros((T, n), F32)
    ci = jnp.zeros((T, n), F32)
    for r in range(T):
        k = (T - r) if reverse else (r + 1)
        cr = jnp.where(row == r, pr[k - 1], cr)
        ci = jnp.where(row == r, pi[k - 1], ci)
    nt = tl // T
    last = 0 if reverse else T - 1

    def step(t, carry):
        sr, si = carry
        base = pl.multiple_of((nt - 1 - t if reverse else t) * T, T)
        br = xr_ref[pl.ds(base, T), :]
        bi = xi_ref[pl.ds(base, T), :]
        for shift, mr, mi in levels:
            qr = pltpu.roll(br, shift, 0)
            qi = pltpu.roll(bi, shift, 0)
            br, bi = br + (mr * qr - mi * qi), bi + (mr * qi + mi * qr)
        xr = br + (cr * sr - ci * si)
        xi = bi + (cr * si + ci * sr)
        xr_ref[pl.ds(base, T), :] = xr
        xi_ref[pl.ds(base, T), :] = xi
        return xr[last:last + 1, :], xi[last:last + 1, :]
    return lax.fori_loop(0, nt, step, (x0r, x0i))


def _s5_fwd_call(u, bre, bim, cre, cim, a, d, tl):
    L, e = u.shape
    nb = e // LANES
    ns = bre.shape[2]
    nc = L // tl

    def body(u_ref, bre_ref, bim_ref, cre_ref, cim_ref, a_ref, d_ref, y_ref, xb_ref, sr_ref, si_ref, xr_ref, xi_ref, carry_ref):
        c = pl.program_id(1)

        @pl.when(c == 0)
        def _():
            carry_ref[...] = jnp.zeros_like(carry_ref)
        xb_ref[0, 0] = carry_ref[...]
        ub = u_ref[...]
        xr_ref[...] = _bdot(ub, bre_ref[0])
        xi_ref[...] = _bdot(ub, bim_ref[0])
        ar = a_ref[0, 0:1, :]
        ai = a_ref[0, 1:2, :]
        xr, xi = _s5_scan_rows(xr_ref, xi_ref, ar, ai, carry_ref[0:1, :], carry_ref[1:2, :], tl)
        carry_ref[0:1, :] = xr
        carry_ref[1:2, :] = xi
        sr = xr_ref[...].astype(BF16)
        si = xi_ref[...].astype(BF16)
        sr_ref[...] = sr
        si_ref[...] = si
        y_ref[...] = _f_s5_act(_bdot(sr, cre_ref[0]) - _bdot(si, cim_ref[0]), ub, d_ref[...])[0]

    return pl.pallas_call(
        body, name="s5_core_fwd", grid=(nb, nc),
        in_specs=[pl.BlockSpec((tl, LANES), lambda j, c: (c, j)),
                  pl.BlockSpec((1, LANES, ns), lambda j, c: (j, 0, 0)), pl.BlockSpec((1, LANES, ns), lambda j, c: (j, 0, 0)),
                  pl.BlockSpec((1, ns, LANES), lambda j, c: (j, 0, 0)), pl.BlockSpec((1, ns, LANES), lambda j, c: (j, 0, 0)),
                  pl.BlockSpec((1, SUBLANES, ns), lambda j, c: (j, 0, 0)), pl.BlockSpec((1, LANES), lambda j, c: (0, j))],
        out_specs=[pl.BlockSpec((tl, LANES), lambda j, c: (c, j)),
                   pl.BlockSpec((1, 1, SUBLANES, ns), lambda j, c: (j, c, 0, 0)),
                   pl.BlockSpec((tl, ns), lambda j, c: (c, j)), pl.BlockSpec((tl, ns), lambda j, c: (c, j))],
        out_shape=[SDS((L, e), F32), SDS((nb, nc, SUBLANES, ns), F32), SDS((L, nb * ns), BF16), SDS((L, nb * ns), BF16)],
        scratch_shapes=[pltpu.VMEM((tl, ns), F32), pltpu.VMEM((tl, ns), F32), pltpu.VMEM((SUBLANES, ns), F32)],
        compiler_params=_params(("arbitrary", "arbitrary"), VMEM_MID),
    )(u, bre, bim, cre, cim, a, d)


def _s5_bwd_call(u, dy2, bre, bim, cre, cim, a, d, xb, sr, si, tl):
    L, e = u.shape
    nb = e // LANES
    ns = bre.shape[2]
    nc = L // tl

    def body(u_ref, dy2_ref, bre_ref, bim_ref, cre_ref, cim_ref, a_ref, d_ref, xb_ref, sr_ref, si_ref,
             du_ref, dbre_ref, dbim_ref, dcre_ref, dcim_ref, da_ref, dd_ref,
             gr_ref, gi_ref, gcarry_ref):
        c = pl.program_id(1)

        @pl.when(c == 0)
        def _():
            gcarry_ref[...] = jnp.zeros_like(gcarry_ref)
            dbre_ref[...] = jnp.zeros_like(dbre_ref)
            dbim_ref[...] = jnp.zeros_like(dbim_ref)
            dcre_ref[...] = jnp.zeros_like(dcre_ref)
            dcim_ref[...] = jnp.zeros_like(dcim_ref)
            da_ref[...] = jnp.zeros_like(da_ref)
            dd_ref[...] = jnp.zeros_like(dd_ref)

        ub = u_ref[...]
        ys = _bdot(sr_ref[...], cre_ref[0]) - _bdot(si_ref[...], cim_ref[0])
        _, act_vjp = jax.vjp(lambda *t: _f_s5_act(*t)[0], ys, ub, d_ref[...])
        dy, du_skip, dd = act_vjp(dy2_ref[...])
        dd_ref[...] += dd
        ar = a_ref[0, 0:1, :]
        ai = a_ref[0, 1:2, :]
        x0r = xb_ref[0, 0, 0:1, :]
        x0i = xb_ref[0, 0, 1:2, :]
        dcre_ref[0] += _bdot(sr_ref[...], dy, TN)
        dcim_ref[0] -= _bdot(si_ref[...], dy, TN)
        gr_ref[...] = _bdot(dy, cre_ref[0], NT)
        gi_ref[...] = -_bdot(dy, cim_ref[0], NT)

        g0r, g0i = _s5_scan_rows(gr_ref, gi_ref, ar, -ai, gcarry_ref[0:1, :], gcarry_ref[1:2, :], tl, reverse=True)
        gcarry_ref[0:1, :] = g0r
        gcarry_ref[1:2, :] = g0i
        row = lax.broadcasted_iota(jnp.int32, (tl, ns), 0)
        gr = gr_ref[...]
        gi = gi_ref[...]
        xpr = jnp.where(row == 0, x0r, pltpu.roll(sr_ref[...].astype(F32), 1, 0))
        xpi = jnp.where(row == 0, x0i, pltpu.roll(si_ref[...].astype(F32), 1, 0))
        da_ref[0, 0:1, :] += jnp.sum(gr * xpr + gi * xpi, axis=0, keepdims=True)
        da_ref[0, 1:2, :] += jnp.sum(gi * xpr - gr * xpi, axis=0, keepdims=True)
        du_ref[...] = (_bdot(gr, bre_ref[0], NT) + _bdot(gi, bim_ref[0], NT)) + du_skip
        dbre_ref[0] += _bdot(ub, gr, TN)
        dbim_ref[0] += _bdot(ub, gi, TN)

    rev = lambda c: nc - 1 - c
    return pl.pallas_call(
        body, name="s5_core_bwd", grid=(nb, nc),
        in_specs=[pl.BlockSpec((tl, LANES), lambda j, c: (rev(c), j)), pl.BlockSpec((tl, LANES), lambda j, c: (rev(c), j)),
                  pl.BlockSpec((1, LANES, ns), lambda j, c: (j, 0, 0)), pl.BlockSpec((1, LANES, ns), lambda j, c: (j, 0, 0)),
                  pl.BlockSpec((1, ns, LANES), lambda j, c: (j, 0, 0)), pl.BlockSpec((1, ns, LANES), lambda j, c: (j, 0, 0)),
                  pl.BlockSpec((1, SUBLANES, ns), lambda j, c: (j, 0, 0)), pl.BlockSpec((1, LANES), lambda j, c: (0, j)),
                  pl.BlockSpec((1, 1, SUBLANES, ns), lambda j, c: (j, rev(c), 0, 0)),
                  pl.BlockSpec((tl, ns), lambda j, c: (rev(c), j)), pl.BlockSpec((tl, ns), lambda j, c: (rev(c), j))],
        out_specs=[pl.BlockSpec((tl, LANES), lambda j, c: (rev(c), j)),
                   pl.BlockSpec((1, LANES, ns), lambda j, c: (j, 0, 0)), pl.BlockSpec((1, LANES, ns), lambda j, c: (j, 0, 0)),
                   pl.BlockSpec((1, ns, LANES), lambda j, c: (j, 0, 0)), pl.BlockSpec((1, ns, LANES), lambda j, c: (j, 0, 0)),
                   pl.BlockSpec((1, SUBLANES, ns), lambda j, c: (j, 0, 0)), pl.BlockSpec((1, LANES), lambda j, c: (0, j))],
        out_shape=[SDS((L, e), F32), SDS(bre.shape, F32), SDS(bim.shape, F32), SDS(cre.shape, F32), SDS(cim.shape, F32),
                   SDS(a.shape, F32), SDS(d.shape, F32)],
        scratch_shapes=[pltpu.VMEM((tl, ns), F32) for _ in range(2)] + [pltpu.VMEM((SUBLANES, ns), F32)],
        compiler_params=_params(("arbitrary", "arbitrary"), VMEM_MID),
    )(u, dy2, bre, bim, cre, cim, a, d, xb, sr, si)


def make_s5_core(tl):
    @jax.custom_vjp
    def s5_core(u, bre, bim, cre, cim, a, d):
        return _s5_fwd_call(u, bre, bim, cre, cim, a, d, tl)[0]

    def fwd(u, bre, bim, cre, cim, a, d):
        y2, xb, sr, si = _s5_fwd_call(u, bre, bim, cre, cim, a, d, tl)
        return y2, (u, bre, bim, cre, cim, a, d, xb, sr, si)

    def bwd(res, dy2):
        u, bre, bim, cre, cim, a, d, xb, sr, si = res
        return tuple(_s5_bwd_call(u, dy2, bre, bim, cre, cim, a, d, xb, sr, si, tl))

    s5_core.defvjp(fwd, bwd)
    return s5_core


def _s5_block_params(lam_re, lam_im, log_dt, b_re, b_im, c_re, c_im):
    dt = jnp.exp(log_dt)[:, None]
    mag = jnp.exp(lam_re * dt)
    ab_re = mag * jnp.cos(lam_im * dt)
    ab_im = mag * jnp.sin(lam_im * dt)
    den = lam_re * lam_re + lam_im * lam_im
    nr = ab_re - 1.0
    ni = ab_im
    q_re = (nr * lam_re + ni * lam_im) / den
    q_im = (ni * lam_re - nr * lam_im) / den
    bb_re = q_re[..., None] * b_re - q_im[..., None] * b_im
    bb_im = q_re[..., None] * b_im + q_im[..., None] * b_re
    nb = S5_GROUPS // S5_GB
    eye = jnp.eye(S5_GB, dtype=F32)

    def bdiag_in(bb):
        t = bb.reshape(nb, S5_GB, S5_STATE, S5_GROUP)
        t = jnp.einsum("jgpm,gh->jgmhp", t, eye)
        return t.reshape(nb, S5_GB * S5_GROUP, S5_GB * S5_STATE)

    def bdiag_out(cc):
        t = cc.reshape(nb, S5_GB, S5_GROUP, S5_STATE)
        t = jnp.einsum("jgmp,gh->jgphm", t, eye)
        return t.reshape(nb, S5_GB * S5_STATE, S5_GB * S5_GROUP)

    a = jnp.stack([ab_re.reshape(nb, S5_GB * S5_STATE), ab_im.reshape(nb, S5_GB * S5_STATE)], axis=1)
    a = jnp.concatenate([a, jnp.zeros((nb, SUBLANES - 2, S5_GB * S5_STATE), F32)], axis=1)
    return bdiag_in(bb_re), bdiag_in(bb_im), bdiag_out(c_re), bdiag_out(c_im), a


def _shift_down(x, s, row):
    if s == 0:
        return x
    return jnp.where(row >= s, pltpu.roll(x, s, 0), 0.0)


def _shift_up(x, s, row, n):
    if s == 0:
        return x
    return jnp.where(row < n - s, pltpu.roll(x, n - s, 0), 0.0)


def _causal_conv(xv, w_ref, row):
    acc = jnp.zeros_like(xv)
    for j in range(GDN_CONV):
        acc += w_ref[j:j + 1, :] * _shift_down(xv, GDN_CONV - 1 - j, row)
    return acc


def _conv_fwd_call(x, w, act, name):
    L, ch = x.shape

    def body(x_ref, w_ref, y_ref):
        xv = x_ref[...]
        row = lax.broadcasted_iota(jnp.int32, xv.shape, 0)
        y_ref[...] = act(_causal_conv(xv, w_ref, row))

    return pl.pallas_call(
        body, name=name + "_fwd", grid=(ch // LANES,),
        in_specs=[pl.BlockSpec((L, LANES), lambda j: (0, j)), pl.BlockSpec((SUBLANES, LANES), lambda j: (0, j))],
        out_specs=pl.BlockSpec((L, LANES), lambda j: (0, j)), out_shape=SDS((L, ch), F32),
        compiler_params=_params(("parallel",), VMEM_MID),
    )(x, w)


def _conv_bwd_call(x, w, dy, act, name):
    L, ch = x.shape

    def body(x_ref, w_ref, dy_ref, dx_ref, dw_ref):
        xv = x_ref[...]
        row = lax.broadcasted_iota(jnp.int32, xv.shape, 0)
        _, act_vjp = jax.vjp(act, _causal_conv(xv, w_ref, row))
        (g,) = act_vjp(dy_ref[...])
        acc = jnp.zeros_like(xv)
        dws = []
        for j in range(GDN_CONV):
            s = GDN_CONV - 1 - j
            acc += w_ref[j:j + 1, :] * _shift_up(g, s, row, L)
            dws.append(jnp.sum(g * _shift_down(xv, s, row), axis=0, keepdims=True))
        dx_ref[...] = acc
        dw_ref[...] = jnp.concatenate(dws + [jnp.zeros((SUBLANES - GDN_CONV, LANES), F32)], axis=0)

    return pl.pallas_call(
        body, name=name + "_bwd", grid=(ch // LANES,),
        in_specs=[pl.BlockSpec((L, LANES), lambda j: (0, j)), pl.BlockSpec((SUBLANES, LANES), lambda j: (0, j)),
                  pl.BlockSpec((L, LANES), lambda j: (0, j))],
        out_specs=[pl.BlockSpec((L, LANES), lambda j: (0, j)), pl.BlockSpec((SUBLANES, LANES), lambda j: (0, j))],
        out_shape=[SDS((L, ch), F32), SDS((SUBLANES, ch), F32)],
        compiler_params=_params(("parallel",), VMEM_MID),
    )(x, w, dy)


def make_conv_act(act, name):
    @jax.custom_vjp
    def op(x, w):
        return _conv_fwd_call(x, w, act, name)

    def fwd(x, w):
        return _conv_fwd_call(x, w, act, name), (x, w)

    def bwd(res, dy):
        x, w = res
        return tuple(_conv_bwd_call(x, w, dy, act, name))

    op.defvjp(fwd, bwd)
    return op


BNN, BNT, BTN = _BNN, _BNT, _BTN
GDN_PREP_BATCH = 8


@jax.custom_vjp
def _known_inverse(a, t):
    return t


def _known_inverse_fwd(a, t):
    return t, t


def _known_inverse_bwd(t, g):
    return -_hdot(_hdot(t, g, _BTN), t, _BNT), jnp.zeros_like(t)


_known_inverse.defvjp(_known_inverse_fwd, _known_inverse_bwd)


def _gdn_prep_math(q, k, v, beta, g, t_saved=None):
    B, C = q.shape[0], q.shape[1]
    ri = lax.broadcasted_iota(jnp.int32, (B, C, C), 1)
    ci = lax.broadcasted_iota(jnp.int32, (B, C, C), 2)
    causal = ri >= ci
    strict = ri > ci
    eye = (ri == ci).astype(F32)
    gb = jnp.broadcast_to(g, (B, C, C))
    g_row = jnp.sum(gb * eye, axis=1, keepdims=True)
    gc_col = jnp.sum(jnp.where(causal, jnp.broadcast_to(g_row, (B, C, C)), 0.0), axis=2, keepdims=True)
    gc_row = jnp.sum(jnp.where(ri <= ci, gb, 0.0), axis=1, keepdims=True)
    decay = jnp.exp(jnp.where(causal, gc_col - gc_row, -jnp.inf))
    kk = _bdot(k, k, BNT)
    a_mat = jnp.where(strict, beta * kk * decay, 0.0)
    t = _unit_lower_inverse(a_mat) if t_saved is None else _known_inverse(a_mat, t_saved)
    e_gc = jnp.exp(gc_col)
    w = _hdot(t, beta * e_gc * k, BNN)
    u = _hdot(t, beta * v, BNN)
    qk = _bdot(q, k, BNT) * decay
    q_dec = q * e_gc
    g_last = gc_col[:, C - 1:C, :]
    k_dec = k * jnp.exp(g_last - gc_col)
    return q_dec, w, u, qk, k_dec, gc_col, t


def _gdn_prep_specs(L):
    C = GDN_CHUNK
    nb = min(GDN_PREP_BATCH, L // C)
    R = nb * C
    ins = [pl.BlockSpec((R, GDN_DK), lambda c, h: (c, h)), pl.BlockSpec((R, GDN_DK), lambda c, h: (c, h)),
           pl.BlockSpec((R, GDN_DV), lambda c, h: (c, h)), pl.BlockSpec((R, LANES), lambda c, h: (c, 0))]
    outs = [pl.BlockSpec((1, R, GDN_DK), lambda c, h: (h, c, 0)), pl.BlockSpec((1, R, GDN_DK), lambda c, h: (h, c, 0)),
            pl.BlockSpec((1, R, GDN_DV), lambda c, h: (h, c, 0)), pl.BlockSpec((1, R, C), lambda c, h: (h, c, 0)),
            pl.BlockSpec((1, R, GDN_DK), lambda c, h: (h, c, 0)), pl.BlockSpec((1, R, 1), lambda c, h: (h, c, 0))]
    t_spec = pl.BlockSpec((1, R, C), lambda c, h: (h, c, 0))
    shapes = [SDS((GDN_HEADS, L, GDN_DK), F32), SDS((GDN_HEADS, L, GDN_DK), F32), SDS((GDN_HEADS, L, GDN_DV), F32),
              SDS((GDN_HEADS, L, C), F32), SDS((GDN_HEADS, L, GDN_DK), F32), SDS((GDN_HEADS, L, 1), F32)]
    return ins, outs, t_spec, shapes, nb


def _chunks(x, nb):
    return x.reshape(nb, x.shape[0] // nb, x.shape[1])


def _head_columns(bg, h):
    lane = lax.broadcasted_iota(jnp.int32, bg.shape, 1)
    beta = jnp.sum(jnp.where(lane == h, bg, 0.0), axis=1, keepdims=True)
    g = jnp.sum(jnp.where(lane == h + GDN_HEADS, bg, 0.0), axis=1, keepdims=True)
    return beta, g


def _gdn_prep_fwd_call(q, k, v, bg):
    L = q.shape[0]
    ins, outs, t_spec, shapes, nb = _gdn_prep_specs(L)

    def body(q_ref, k_ref, v_ref, bg_ref, *o_refs):
        beta, g = _head_columns(bg_ref[...], pl.program_id(1))
        res = _gdn_prep_math(_chunks(q_ref[...], nb), _chunks(k_ref[...], nb), _chunks(v_ref[...], nb),
                             _chunks(beta, nb), _chunks(g, nb))
        for o_ref, val in zip(o_refs, res):
            o_ref[0] = val.reshape(val.shape[0] * val.shape[1], val.shape[2])

    return pl.pallas_call(
        body, name="gdn_prep_fwd", grid=(L // (nb * GDN_CHUNK), GDN_HEADS), in_specs=ins, out_specs=outs + [t_spec],
        out_shape=shapes + [SDS((GDN_HEADS, L, GDN_CHUNK), F32)],
        compiler_params=_params(("parallel", "parallel"), VMEM_MID),
    )(q, k, v, bg)


def _gdn_prep_bwd_call(q, k, v, bg, t, cts):
    L = q.shape[0]
    ins, outs, t_spec, _, nb = _gdn_prep_specs(L)

    def body(q_ref, k_ref, v_ref, bg_ref, t_ref, c0, c1, c2, c3, c4, c5, dq_ref, dk_ref, dv_ref, dbg_ref):
        h = pl.program_id(1)
        beta, g = _head_columns(bg_ref[...], h)
        t_saved = _chunks(t_ref[0], nb)
        _, vjp = jax.vjp(lambda *a: _gdn_prep_math(*a, t_saved=t_saved)[:6], _chunks(q_ref[...], nb), _chunks(k_ref[...], nb),
                         _chunks(v_ref[...], nb), _chunks(beta, nb), _chunks(g, nb))
        dq, dk, dv, db, dg = vjp(tuple(_chunks(c[0], nb) for c in (c0, c1, c2, c3, c4, c5)))
        flat = lambda a: a.reshape(a.shape[0] * a.shape[1], a.shape[2])
        dq_ref[...] = flat(dq)
        dk_ref[...] = flat(dk)
        dv_ref[...] = flat(dv)

        @pl.when(h == 0)
        def _():
            dbg_ref[...] = jnp.zeros_like(dbg_ref)
        lane = lax.broadcasted_iota(jnp.int32, dbg_ref.shape, 1)
        dbg_ref[...] += jnp.where(lane == h, flat(db), 0.0) + jnp.where(lane == h + GDN_HEADS, flat(dg), 0.0)

    return pl.pallas_call(
        body, name="gdn_prep_bwd", grid=(L // (nb * GDN_CHUNK), GDN_HEADS), in_specs=ins + [t_spec] + outs, out_specs=ins,
        out_shape=[SDS(q.shape, F32), SDS(k.shape, F32), SDS(v.shape, F32), SDS(bg.shape, F32)],
        compiler_params=_params(("parallel", "arbitrary"), VMEM_MID),
    )(q, k, v, bg, t, *cts)


@jax.custom_vjp
def gdn_prep(q, k, v, bg):
    return tuple(_gdn_prep_fwd_call(q, k, v, bg)[:6])


def _gdn_prep_f(q, k, v, bg):
    res = _gdn_prep_fwd_call(q, k, v, bg)
    return tuple(res[:6]), (q, k, v, bg, res[6])


def _gdn_prep_b(res, cts):
    return tuple(_gdn_prep_bwd_call(*res, tuple(cts)))


gdn_prep.defvjp(_gdn_prep_f, _gdn_prep_b)


def _gdn_step_math(q_dec, w, u, qk, k_dec, gc, z, nw, state):
    H, C = q_dec.shape[0], q_dec.shape[1]
    v_new = u - _bdot(w, state, BNN)
    o = _bdot(q_dec, state, BNN) + _bdot(qk, v_new, BNN)
    gl = gc[:, C - 1:C, :]
    new_state = jnp.exp(gl) * state + _bdot(k_dec, v_new, BTN)
    return _f_gdn_post(jnp.concatenate([o[h] for h in range(H)], axis=1), z, nw)[0], new_state


def _gdn_scan_specs(L, rev):
    C, H = GDN_CHUNK, GDN_HEADS
    nc = L // C
    cc = (lambda c: nc - 1 - c) if rev else (lambda c: c)
    ins = [pl.BlockSpec((H, C, GDN_DK), lambda c: (0, cc(c), 0)), pl.BlockSpec((H, C, GDN_DK), lambda c: (0, cc(c), 0)),
           pl.BlockSpec((H, C, GDN_DV), lambda c: (0, cc(c), 0)), pl.BlockSpec((H, C, C), lambda c: (0, cc(c), 0)),
           pl.BlockSpec((H, C, GDN_DK), lambda c: (0, cc(c), 0)), pl.BlockSpec((H, C, 1), lambda c: (0, cc(c), 0))]
    o_spec = pl.BlockSpec((C, H * GDN_DV), lambda c: (cc(c), 0))
    nw_spec = pl.BlockSpec((1, H * GDN_DV), lambda c: (0, 0))
    s_spec = pl.BlockSpec((1, H, GDN_DK, GDN_DV), lambda c: (cc(c), 0, 0, 0))
    return ins + [o_spec, nw_spec], o_spec, s_spec, nc


def _gdn_scan_fwd_call(q_dec, w, u, qk, k_dec, gc, z, nw):
    L = q_dec.shape[1]
    ins, o_spec, s_spec, nc = _gdn_scan_specs(L, False)

    def body(qd_ref, w_ref, u_ref, qk_ref, kd_ref, gc_ref, z_ref, nw_ref, o_ref, sin_ref, s_ref):
        c = pl.program_id(0)

        @pl.when(c == 0)
        def _():
            s_ref[...] = jnp.zeros_like(s_ref)
        st = s_ref[...]
        sin_ref[0] = st
        o, ns = _gdn_step_math(qd_ref[...], w_ref[...], u_ref[...], qk_ref[...], kd_ref[...], gc_ref[...], z_ref[...], nw_ref[...], st)
        o_ref[...] = o
        s_ref[...] = ns

    return pl.pallas_call(
        body, name="gdn_scan_fwd", grid=(nc,), in_specs=ins, out_specs=[o_spec, s_spec],
        out_shape=[SDS((L, GDN_HEADS * GDN_DV), F32), SDS((nc, GDN_HEADS, GDN_DK, GDN_DV), F32)],
        scratch_shapes=[pltpu.VMEM((GDN_HEADS, GDN_DK, GDN_DV), F32)],
        compiler_params=_params(("arbitrary",), VMEM_MID),
    )(q_dec, w, u, qk, k_dec, gc, z, nw)


def _gdn_scan_bwd_call(q_dec, w, u, qk, k_dec, gc, z, nw, s_in, do):
    L = q_dec.shape[1]
    ins, o_spec, s_spec, nc = _gdn_scan_specs(L, True)

    def body(qd_ref, w_ref, u_ref, qk_ref, kd_ref, gc_ref, z_ref, nw_ref, sin_ref, do_ref,
             dqd_ref, dw_ref, du_ref, dqk_ref, dkd_ref, dgc_ref, dz_ref, dnw_ref, ds_ref):
        c = pl.program_id(0)

        @pl.when(c == 0)
        def _():
            ds_ref[...] = jnp.zeros_like(ds_ref)
            dnw_ref[...] = jnp.zeros_like(dnw_ref)
        _, vjp = jax.vjp(_gdn_step_math, qd_ref[...], w_ref[...], u_ref[...], qk_ref[...], kd_ref[...], gc_ref[...],
                         z_ref[...], nw_ref[...], sin_ref[0])
        dqd, dw, du, dqk, dkd, dgc, dz, dnw, dst = vjp((do_ref[...], ds_ref[...]))
        dqd_ref[...] = dqd
        dw_ref[...] = dw
        du_ref[...] = du
        dqk_ref[...] = dqk
        dkd_ref[...] = dkd
        dgc_ref[...] = dgc
        dz_ref[...] = dz
        dnw_ref[...] += dnw
        ds_ref[...] = dst

    return pl.pallas_call(
        body, name="gdn_scan_bwd", grid=(nc,), in_specs=ins + [s_spec, o_spec], out_specs=ins,
        out_shape=[SDS(t.shape, F32) for t in (q_dec, w, u, qk, k_dec, gc, z, nw)],
        scratch_shapes=[pltpu.VMEM((GDN_HEADS, GDN_DK, GDN_DV), F32)],
        compiler_params=_params(("arbitrary",), VMEM_MID),
    )(q_dec, w, u, qk, k_dec, gc, z, nw, s_in, do)


@jax.custom_vjp
def gdn_scan(q_dec, w, u, qk, k_dec, gc, z, nw):
    return _gdn_scan_fwd_call(q_dec, w, u, qk, k_dec, gc, z, nw)[0]


def _gdn_scan_f(*args):
    o, s_in = _gdn_scan_fwd_call(*args)
    return o, (*args, s_in)


def _gdn_scan_b(res, do):
    return tuple(_gdn_scan_bwd_call(*res, do))


gdn_scan.defvjp(_gdn_scan_f, _gdn_scan_b)


def _silu(x):
    return x * jax.nn.sigmoid(x)


def _gelu_tanh(x):
    return 0.5 * x * (1.0 + jnp.tanh(math.sqrt(2.0 / math.pi) * (x + 0.044715 * (x * x * x))))


def _f_lnmod(x, nw, sc, sh, bsc, bsh):
    xn = x * lax.rsqrt(jnp.mean(x * x, axis=-1, keepdims=True) + NORM_EPS) * nw
    return (xn * (1.0 + (sc + bsc)) + (sh + bsh),)


def _f_s5_act(ys, u, d):
    return (_gelu_tanh(ys + d * u),)


def _f_s5_gate(y2, t, z):
    return (y2 * jax.nn.sigmoid(t) * _silu(z),)


def _f_res(x, y, gate, bgate):
    return (x + (gate + bgate) * y,)


def _heads(x, width, fn):
    return jnp.concatenate([fn(x[:, i * width:(i + 1) * width]) for i in range(x.shape[1] // width)], axis=1)


def _l2n(x):
    return x * lax.rsqrt(jnp.sum(x * x, axis=-1, keepdims=True) + NORM_EPS)


def _f_betag(ba, alog, dtb):
    col = lax.broadcasted_iota(jnp.int32, ba.shape, 1)
    t = ba + dtb
    softplus = jnp.maximum(t, 0.0) + jnp.log1p(jnp.exp(-jnp.abs(t)))
    g = -jnp.exp(alog) * softplus
    return (jnp.where(col < GDN_HEADS, jax.nn.sigmoid(ba), jnp.where(col < 2 * GDN_HEADS, g, 0.0)),)


def _f_gdn_post(o, z, nw):
    on = _heads(o, GDN_DV, lambda t: t * lax.rsqrt(jnp.mean(t * t, axis=-1, keepdims=True) + NORM_EPS))
    return (on * nw * _silu(z),)


def _f_loss(x, tgt, fw):
    y = x * lax.rsqrt(jnp.mean(x * x, axis=-1, keepdims=True) + NORM_EPS) * fw
    err = y - tgt
    return (0.5 * jnp.mean(err * err, axis=-1, keepdims=True),)


def _ada_mod_call(c_all, ada_w):
    n = ada_w.shape[2]

    def body(c_ref, w_ref, o_ref):
        ca = _silu(c_ref[...])
        for l in range(ada_w.shape[0]):
            o_ref[l] = _bdot(ca, w_ref[l])

    return pl.pallas_call(body, name="ada_mod", out_shape=SDS((ada_w.shape[0], N_DEV, n), F32),
                          compiler_params=_params(None, VMEM_MID))(c_all, ada_w)


def _ada_grad_call(c_all, dmod):
    nl, _, n = dmod.shape

    def body(c_ref, d_ref, o_ref):
        ca = _silu(c_ref[...])
        for l in range(nl):
            o_ref[l] = _hdot(ca, d_ref[l], TN)

    return pl.pallas_call(body, name="ada_grad", out_shape=SDS((nl, c_all.shape[1], n), F32),
                          compiler_params=_params(None, VMEM_MID))(c_all, dmod)


ADAM_ROWS = 512


def _adamw(g, w, m, v):
    m2 = ADAM_B1 * m + (1.0 - ADAM_B1) * g
    v2 = ADAM_B2 * v + (1.0 - ADAM_B2) * (g * g)
    m_hat = m2 / (1.0 - ADAM_B1 ** ADAM_STEP)
    v_hat = v2 / (1.0 - ADAM_B2 ** ADAM_STEP)
    return g, -ADAM_LR * (m_hat / (jnp.sqrt(v_hat) + ADAM_EPS) + ADAM_WD * w), m2, v2


def _adam_call(gs, w, m, v, name, rows=None, by_cols=False):
    n, r, cols = gs.shape
    if by_cols:
        blk = pl.BlockSpec((r, LANES), lambda i: (0, i))
        g_blk, grid = pl.BlockSpec((n, r, LANES), lambda i: (0, 0, i)), (cols // LANES,)
    else:
        rows = rows or ADAM_ROWS
        blk = pl.BlockSpec((rows, cols), lambda i: (i, 0))
        g_blk, grid = pl.BlockSpec((n, rows, cols), lambda i: (0, i, 0)), (r // rows,)

    def body(g_ref, w_ref, m_ref, v_ref, go_ref, d_ref, mo_ref, vo_ref):
        g = g_ref[0].astype(F32)
        for s in range(1, n):
            g = g + g_ref[s].astype(F32)
        for o_ref, val in zip((go_ref, d_ref, mo_ref, vo_ref), _adamw(g, w_ref[...], m_ref[...], v_ref[...])):
            o_ref[...] = val

    return pl.pallas_call(
        body, name=name, grid=grid, in_specs=[g_blk, blk, blk, blk],
        out_specs=[blk, blk, blk, blk], out_shape=[SDS((r, cols), F32)] * 4,
        compiler_params=_params(("parallel",), VMEM_MID),
    )(gs, w, m, v)


def _sum_call(gs, name, rows):
    n, r, _ = gs.shape

    def body(g_ref, o_ref):
        g = g_ref[0].astype(F32)
        for s in range(1, n):
            g = g + g_ref[s].astype(F32)
        o_ref[...] = g

    return pl.pallas_call(
        body, name=name, grid=(r // rows,),
        in_specs=[pl.BlockSpec((n, rows, LANES), lambda i: (0, i, 0))],
        out_specs=pl.BlockSpec((rows, LANES), lambda i: (i, 0)), out_shape=SDS((r, LANES), F32),
        compiler_params=_params(("parallel",), VMEM_MID),
    )(gs)


def _allgather_call(x_shard, name):
    m_per, n = x_shard.shape

    def body(x_ref, out_ref, send_sems, recv_sems, local_sem):
        x, y, c = lax.axis_index("x"), lax.axis_index("y"), lax.axis_index("c")
        me, sibling = (x, y, c), (x, y, 1 - c)
        chips = [(1 - x, y), (x, 1 - y), (1 - x, 1 - y)]

        def rows(px, py, pc):
            return out_ref.at[pl.ds((4 * px + 2 * py + pc) * m_per, m_per), :]

        def copy(k, block, to, src=None):
            return pltpu.make_async_remote_copy(
                src_ref=rows(*block) if src is None else src, dst_ref=rows(*block),
                send_sem=send_sems.at[k], recv_sem=recv_sems.at[k], device_id=to, device_id_type=pl.DeviceIdType.MESH)

        mine = pltpu.make_async_copy(x_ref, rows(*me), local_sem)
        mine.start()
        first = [copy(0, me, sibling, src=x_ref)]
        first += [copy(1 + j, me, (*chip, c), src=x_ref) for j, chip in enumerate(chips)]
        for cp in first:
            cp.start()
        passed = [copy(4 + j, (*chip, c), sibling) for j, chip in enumerate(chips)]
        for j, chip in enumerate(chips):
            copy(1 + j, (*chip, c), me).wait_recv()
            passed[j].start()
        copy(0, sibling, me).wait_recv()
        for j, chip in enumerate(chips):
            copy(4 + j, (*chip, 1 - c), me).wait_recv()
        for cp in first + passed:
            cp.wait_send()
        mine.wait()

    vmem = pl.BlockSpec(memory_space=pltpu.VMEM)
    return pl.pallas_call(
        body, name=name, out_shape=SDS((N_DEV * m_per, n), x_shard.dtype), in_specs=[vmem], out_specs=vmem,
        scratch_shapes=[pltpu.SemaphoreType.DMA((7,)), pltpu.SemaphoreType.DMA((7,)), pltpu.SemaphoreType.DMA],
    )(x_shard)


def _gather_weights_call(shards, name):
    nw = len(shards)

    def body(*refs):
        x_refs, out_refs = refs[:nw], refs[nw:2 * nw]
        send_sems, recv_sems, local_sems = refs[2 * nw:]
        x, y, c = lax.axis_index("x"), lax.axis_index("y"), lax.axis_index("c")
        me, sibling = (x, y, c), (x, y, 1 - c)
        chips = [(1 - x, y), (x, 1 - y), (1 - x, 1 - y)]

        def slot(w, px, py, pc):
            return out_refs[w].at[4 * px + 2 * py + pc]

        def copy(w, k, block, to, src=None):
            dst = slot(w, *block)
            return pltpu.make_async_remote_copy(
                src_ref=dst if src is None else src, dst_ref=dst, send_sem=send_sems.at[7 * w + k],
                recv_sem=recv_sems.at[7 * w + k], device_id=to, device_id_type=pl.DeviceIdType.MESH)

        mines = [pltpu.make_async_copy(x_refs[w], slot(w, *me), local_sems.at[w]) for w in range(nw)]
        for cp in mines:
            cp.start()
        first = [copy(w, 0, me, sibling, src=x_refs[w]) for w in range(nw)]
        first += [copy(w, 1 + j, me, (*chip, c), src=x_refs[w]) for w in range(nw) for j, chip in enumerate(chips)]
        for cp in first:
            cp.start()
        passed = []
        for w in range(nw):
            for j, chip in enumerate(chips):
                copy(w, 1 + j, (*chip, c), me).wait_recv()
                fwd = copy(w, 4 + j, (*chip, c), sibling)
                fwd.start()
                passed.append(fwd)
        for w in range(nw):
            copy(w, 0, sibling, me).wait_recv()
            for j, chip in enumerate(chips):
                copy(w, 4 + j, (*chip, 1 - c), me).wait_recv()
        for cp in first + passed:
            cp.wait_send()
        for cp in mines:
            cp.wait()

    hbm = pl.BlockSpec(memory_space=pl.ANY)
    return pl.pallas_call(
        body, name=name, out_shape=[SDS((N_DEV,) + s.shape, s.dtype) for s in shards],
        in_specs=[hbm] * nw, out_specs=[hbm] * nw,
        scratch_shapes=[pltpu.SemaphoreType.DMA((7 * nw,)), pltpu.SemaphoreType.DMA((7 * nw,)), pltpu.SemaphoreType.DMA((nw,))],
    )(*shards)


_HBM = pl.BlockSpec(memory_space=pltpu.HBM)
_SEM = pl.BlockSpec(memory_space=pltpu.SEMAPHORE)
_DATAFLOW = pltpu.SideEffectType.DATAFLOW_SIDE_EFFECTING


def _spread_start_call(srcs, per_peer, name, after):
    nw = len(srcs)
    lands = [lax.empty((N_DEV,) + (s.shape[1:] if per_peer else s.shape), s.dtype) for s in srcs]

    def body(*refs):
        src_refs, land_refs = refs[:nw], refs[nw:2 * nw]
        send_sems, recv_sems, token = refs[2 * nw + 1], refs[2 * nw + 2], refs[-1]
        x, y, c = lax.axis_index("x"), lax.axis_index("y"), lax.axis_index("c")
        me = 4 * x + 2 * y + c
        for w in range(nw):
            for k in range(1, N_DEV):
                px = 1 - x if k & 4 else x
                py = 1 - y if k & 2 else y
                pc = 1 - c if k & 1 else c
                src = src_refs[w].at[4 * px + 2 * py + pc] if per_peer else src_refs[w]
                pltpu.make_async_remote_copy(
                    src_ref=src, dst_ref=land_refs[w].at[me], send_sem=send_sems.at[w], recv_sem=recv_sems.at[w],
                    device_id=(px, py, pc), device_id_type=pl.DeviceIdType.MESH).start()
        token[...] = jnp.zeros_like(token)

    hbm = lambda a: pltpu.with_memory_space_constraint(a, pltpu.HBM)
    res = pl.pallas_call(
        body, name=name,
        out_shape=(pltpu.SemaphoreType.DMA((nw,)), pltpu.SemaphoreType.DMA((nw,)))
        + tuple(pltpu.HBM(s.shape, s.dtype) for s in srcs) + tuple(pltpu.HBM(l.shape, l.dtype) for l in lands)
        + (SDS((SUBLANES, LANES), F32),),
        in_specs=[_HBM] * (2 * nw) + [pl.BlockSpec(memory_space=pl.ANY)],
        out_specs=(_SEM, _SEM) + (_HBM,) * (2 * nw) + (pl.BlockSpec(memory_space=pltpu.VMEM),),
        input_output_aliases={i: i + 2 for i in range(2 * nw)},
        compiler_params=pltpu.CompilerParams(has_side_effects=_DATAFLOW),
    )(*[hbm(s) for s in srcs], *[hbm(l) for l in lands], after)
    return res[0], res[1], res[2:2 + nw], res[2 + nw:2 + 2 * nw], res[-1]


def _spread_wait_call(send_sems, recv_sems, srcs, lands, after, name):
    nw = len(lands)

    def body(*refs):
        land_refs = refs[nw:2 * nw]
        s_sems, r_sems = refs[2 * nw], refs[2 * nw + 1]
        x, y, c = lax.axis_index("x"), lax.axis_index("y"), lax.axis_index("c")
        for w in range(nw):
            seven = land_refs[w].at[pl.ds(0, N_DEV - 1)]
            all_seven = pltpu.make_async_remote_copy(
                src_ref=seven, dst_ref=seven, send_sem=s_sems.at[w], recv_sem=r_sems.at[w],
                device_id=(x, y, c), device_id_type=pl.DeviceIdType.MESH)
            all_seven.wait_send()
            all_seven.wait_recv()

    res = pl.pallas_call(
        body, name=name,
        out_shape=tuple(pltpu.HBM(s.shape, s.dtype) for s in srcs) + tuple(pltpu.HBM(l.shape, l.dtype) for l in lands),
        in_specs=[_HBM] * (2 * nw) + [_SEM, _SEM, pl.BlockSpec(memory_space=pl.ANY)], out_specs=(_HBM,) * (2 * nw),
        input_output_aliases={i: i for i in range(2 * nw)},
        compiler_params=pltpu.CompilerParams(has_side_effects=_DATAFLOW),
    )(*srcs, *lands, send_sems, recv_sems, after)
    return res[:nw], res[nw:]


def _join_cols_call(w8, name):
    _, k, n = w8.shape
    tk = _tile(k, 256)

    def body(w_ref, o_ref):
        for s in range(N_DEV):
            o_ref[:, n * s:n * (s + 1)] = w_ref[s]

    return pl.pallas_call(body, name=name, grid=(k // tk,), in_specs=[pl.BlockSpec((N_DEV, tk, n), lambda i: (0, i, 0))],
                          out_specs=pl.BlockSpec((tk, N_DEV * n), lambda i: (i, 0)), out_shape=SDS((k, N_DEV * n), w8.dtype),
                          compiler_params=_params(("parallel",), VMEM_MID))(w8)


def _split_cols_call(g, name, dtype):
    k, n8 = g.shape
    n = n8 // N_DEV
    tk = _tile(k, 256)

    def body(g_ref, o_ref):
        for s in range(N_DEV):
            o_ref[s] = g_ref[:, n * s:n * (s + 1)].astype(dtype)

    return pl.pallas_call(body, name=name, grid=(k // tk,), in_specs=[pl.BlockSpec((tk, n8), lambda i: (i, 0))],
                          out_specs=pl.BlockSpec((N_DEV, tk, n), lambda i: (0, i, 0)), out_shape=SDS((N_DEV, k, n), dtype),
                          compiler_params=_params(("parallel",), VMEM_MID))(g)


def _pack(parts, rows_multiple):
    flat = jnp.concatenate([p.reshape(-1) for p in parts])
    unit = rows_multiple * LANES
    padded = -(-flat.shape[0] // unit) * unit
    flat = jnp.concatenate([flat, jnp.zeros((padded - flat.shape[0],), F32)])
    return flat.reshape(-1, LANES)


def _groups_last(a):
    x, y = a.shape[-2:]
    return jnp.transpose(a.reshape(S5_GROUPS, x, y), (1, 2, 0)).reshape(x * y, S5_GROUPS)


def _groups_first(a, shape):
    x, y = shape[-2:]
    return jnp.transpose(a.reshape(x, y, S5_GROUPS), (2, 0, 1)).reshape(shape)


def _unpack(buf, shapes):
    flat = buf.reshape(-1)
    out, off = [], 0
    for s in shapes:
        n = math.prod(s)
        out.append(flat[off:off + n].reshape(s))
        off += n
    return out


def _row_tile(L):
    return 256 if L % 256 == 0 else L


def _layer0_mix(diff, const):
    x, mod, norm_w, lam_re, lam_im, log_dt, b_re, b_im, c_re, c_im, s5_d, *slots = diff
    ada_b, weights = const
    L = x.shape[0]
    tm = _row_tile(L)
    mods = mod.reshape(2, 1, D_MODEL)
    biases = ada_b.reshape(2, 1, D_MODEL)
    op_ln0 = make_rowwise(_f_lnmod, "ln0", tm, 1, 5, pass_first=True)
    h, x = op_ln0((x,), (norm_w.reshape(1, D_MODEL), mods[1], mods[0], biases[1], biases[0]))
    u, z = make_proj("s5_in")(h, tuple(weights), tuple(slots))
    blocks = _s5_block_params(lam_re, lam_im, log_dt, b_re, b_im, c_re, c_im)
    y2 = make_s5_core(min(S5_TL, L))(u, *blocks, s5_d.reshape(1, D_INNER))
    return x, y2, z


def _layer0_out(diff, weights):
    y2, z, *slots = diff
    tm = _row_tile(y2.shape[0])
    t, y2 = make_mm("s5_glu", pass_input=True)(y2, weights[0], slots[0])
    (y4,) = make_rowwise(_f_s5_gate, "s5_gate", tm, 3, 0)((y2, t, z), ())
    return make_mm("s5_out")(y4, weights[1], slots[1])


def _f_res_lnmod(x, o, gate, bgate, nw, sc, sh, bsc, bsh):
    (x1,) = _f_res(x, o, gate, bgate)
    return _f_lnmod(x1, nw, sc, sh, bsc, bsh) + (x1,)


def _f_res_loss(x, y, tgt, gate, bgate, fw):
    return _f_loss(_f_res(x, y, gate, bgate)[0], tgt, fw)


def _layer1_loss(diff, const):
    x, o, gate0, mod, norm_w, conv_w, a_log, dt_bias, gdn_nw, final_nw, *slots = diff
    tgt, bgate0, ada_b, weights = const
    L = x.shape[0]
    tm = _row_tile(L)
    mods = mod.reshape(3, 1, D_MODEL)
    biases = ada_b.reshape(3, 1, D_MODEL)
    h, x1 = make_rowwise(_f_res_lnmod, "res0_ln1", tm, 2, 7)(
        (x, o), (gate0.reshape(1, D_MODEL), bgate0.reshape(1, D_MODEL), norm_w.reshape(1, D_MODEL), mods[1], mods[0], biases[1], biases[0]))
    q0, k0, v0, gz, ba = make_proj("gdn_in", w_rows=True)(h, tuple(weights[0:5]), tuple(slots[0:5]))
    cw = jnp.concatenate([conv_w, jnp.zeros((SUBLANES - GDN_CONV, GDN_CONV_CH), F32)], axis=0)
    q = make_conv_act(lambda t: _l2n(_silu(t)) * (GDN_DK ** -0.5), "gdn_conv_q")(q0, cw[:, :GDN_QK])
    k = make_conv_act(lambda t: _l2n(_silu(t)), "gdn_conv_k")(k0, cw[:, GDN_QK:2 * GDN_QK])
    v = make_conv_act(_silu, "gdn_conv_v")(v0, cw[:, 2 * GDN_QK:])
    pad = jnp.zeros((LANES - 2 * GDN_HEADS,), F32)
    alog_row = jnp.concatenate([jnp.zeros((GDN_HEADS,), F32), a_log, pad]).reshape(1, LANES)
    dtb_row = jnp.concatenate([jnp.zeros((GDN_HEADS,), F32), dt_bias, pad]).reshape(1, LANES)
    (bg,) = make_rowwise(_f_betag, "gdn_bg", tm, 1, 2)((ba,), (alog_row, dtb_row))
    nw_row = jnp.tile(gdn_nw, GDN_HEADS).reshape(1, D_INNER)
    on = gdn_scan(*gdn_prep(q, k, v, bg), gz, nw_row)
    y = make_mm("gdn_out")(on, weights[5], slots[5])
    (lt,) = make_rowwise(_f_res_loss, "res1_loss", tm, 3, 3)((x1, y, tgt), (mods[2], biases[2], final_nw.reshape(1, D_MODEL)))
    return jnp.sum(lt)


VEC_NAMES = ("ada_b", "norm_w", "s5_lambda_re", "s5_lambda_im", "s5_log_dt", "s5_d", "gdn_a_log", "gdn_dt_bias", "final_norm_w")
MAT_NAMES = ("s5_b_re", "s5_b_im", "s5_c_re", "s5_c_im")
S5_BIG = ("s5_w_in", "s5_w_glu", "s5_w_out")
GDN_BIG = ("gdn_w_in", "gdn_w_out")
BIG_NAMES = S5_BIG + GDN_BIG
WEIGHT_ORDER = ("ada_w", "ada_b", "norm_w", "s5_w_in", "s5_lambda_re", "s5_lambda_im", "s5_log_dt", "s5_b_re", "s5_b_im",
                "s5_c_re", "s5_c_im", "s5_d", "s5_w_glu", "s5_w_out", "gdn_w_in", "gdn_conv_w", "gdn_a_log", "gdn_dt_bias",
                "gdn_norm_w", "gdn_w_out", "final_norm_w")


def _step(x, c, W, M, V, tgt):
    L = x.shape[1]
    ix, iy, ic = lax.axis_index("x"), lax.axis_index("y"), lax.axis_index("c")
    me = 4 * ix + 2 * iy + ic
    n_ada = W["ada_w"].shape[2]
    n_conv = W["gdn_conv_w"].shape[2]
    n_gnw = W["gdn_norm_w"].shape[1]

    g1 = _allgather_call(_pack([c, W["gdn_conv_w"], W["gdn_norm_w"]], SUBLANES), "gather_small_in")
    g1 = g1.reshape(N_DEV, -1)
    c_all = g1[:, :D_MODEL]
    conv_w = g1[:, D_MODEL:D_MODEL + GDN_CONV * n_conv].reshape(N_DEV, GDN_CONV, n_conv).transpose(1, 0, 2).reshape(GDN_CONV, -1)
    gdn_nw = g1[:, D_MODEL + GDN_CONV * n_conv:D_MODEL + GDN_CONV * n_conv + n_gnw].reshape(-1)
    mod_part = _ada_mod_call(c_all, W["ada_w"])
    g2 = _allgather_call(_pack([mod_part], SUBLANES), "gather_mod").reshape(N_DEV, -1)
    mod_all = g2[:, :2 * N_DEV * n_ada].reshape(N_DEV, 2, N_DEV, n_ada)
    mod_raw = lax.dynamic_index_in_dim(mod_all, me, axis=2, keepdims=False)
    mod_raw = mod_raw.transpose(1, 0, 2).reshape(2, 3 * D_MODEL)

    shard = lambda n: W[n][0].astype(BF16)
    (w_in5_parts,) = _gather_weights_call([shard("s5_w_in")], "gather_s5_w_in")
    late = _spread_start_call([shard("s5_w_glu"), shard("s5_w_out")], False, "gather_s5_late_start", w_in5_parts)
    turned = lambda a: jnp.transpose(a[0])
    g_send, g_recv, g_srcs, g_lands, g_token = _spread_start_call(
        [turned(W["gdn_w_in"]).astype(BF16), shard("gdn_w_out")], False, "gather_gdn_start", late[4])
    w_in5 = _join_cols_call(w_in5_parts, "join_s5_w_in")
    slot = lambda *s: jnp.zeros(s, F32)
    two = 2 * D_MODEL
    diff_mix = (x[0], mod_raw[0, :two] + g_token[0, 0], W["norm_w"][0], W["s5_lambda_re"][0], W["s5_lambda_im"][0], W["s5_log_dt"][0],
                W["s5_b_re"][0], W["s5_b_im"][0], W["s5_c_re"][0], W["s5_c_im"][0], W["s5_d"][0],
                slot(D_MODEL, D_INNER), slot(D_MODEL, D_INNER))

    (xp, y2, z5), vjp_mix = jax.vjp(lambda d: _layer0_mix(d, (W["ada_b"][0, :two], (w_in5[:, :D_INNER], w_in5[:, D_INNER:]))), diff_mix)
    l_srcs, l_lands = _spread_wait_call(late[0], late[1], late[2], late[3], y2, "gather_s5_late_wait")
    w_glu, w_o5 = [lax.dynamic_update_slice(land, src[None], (me, 0, 0)).reshape(-1, src.shape[1]) for land, src in zip(l_lands, l_srcs)]
    diff_out = (y2, z5, slot(D_INNER, D_INNER), slot(D_INNER, D_MODEL))
    o5, vjp_out = jax.vjp(lambda d: _layer0_out(d, (w_glu, w_o5)), diff_out)
    g_srcs, g_lands = _spread_wait_call(g_send, g_recv, g_srcs, g_lands, o5, "gather_gdn_wait")
    gdn_full = [lax.dynamic_update_slice(land, src[None], (me, 0, 0)) for land, src in zip(g_lands, g_srcs)]
    w_ing = gdn_full[0].reshape(GDN_PROJ, D_MODEL)
    w_ba = jnp.concatenate([w_ing[GDN_CONV_CH + D_INNER:], jnp.zeros((LANES - 2 * GDN_HEADS, D_MODEL), BF16)], axis=0)
    weights1 = (w_ing[:GDN_QK], w_ing[GDN_QK:2 * GDN_QK], w_ing[2 * GDN_QK:GDN_CONV_CH],
                w_ing[GDN_CONV_CH:GDN_CONV_CH + D_INNER], w_ba, gdn_full[1].reshape(D_INNER, D_MODEL))
    slots1 = tuple(jnp.zeros(w.shape, F32) for w in weights1)
    diff1 = (xp, o5, mod_raw[0, two:], mod_raw[1], W["norm_w"][1], conv_w, W["gdn_a_log"][0], W["gdn_dt_bias"][0], gdn_nw,
             W["final_norm_w"], *slots1)
    loss_local, vjp1 = jax.vjp(lambda d: _layer1_loss(d, (tgt[0], W["ada_b"][0, two:], W["ada_b"][1], weights1)), diff1)
    ((dxp, do5, dmod_gate, dmod1, d_norm_w1, d_conv, d_alog, d_dtb, d_gnw, d_fnw, d_wq, d_wk, d_wv, d_wgz, d_wba, d_wog),) = vjp1(
        jnp.ones((), F32))
    loss = lax.psum(loss_local, MESH_AXES)

    rows = lambda d: d.reshape(N_DEV, d.shape[0] // N_DEV, d.shape[1])
    d_ing = jnp.concatenate([d_wq, d_wk, d_wv, d_wgz, d_wba[:2 * GDN_HEADS]], axis=0).astype(BF16).reshape(N_DEV, -1, D_MODEL)
    s_send, s_recv, s_srcs, s_lands, s_token = _spread_start_call([d_ing, rows(d_wog).astype(BF16)], True, "scatter_gdn_start", dxp)
    ((dy2, dz5, d_wglu, d_wo5),) = vjp_out(do5.at[0, 0].add(s_token[0, 0]))
    t_send, t_recv, t_srcs, t_lands, t_token = _spread_start_call(
        [rows(d_wglu).astype(BF16), rows(d_wo5).astype(BF16)], True, "scatter_s5_late_start", dy2)
    ((dx, dmod_ss, d_norm_w0, d_lre, d_lim, d_logdt, d_bre, d_bim, d_cre, d_cim, d_s5d, d_wu, d_wz),) = vjp_mix(
        (dxp.at[0, 0].add(t_token[0, 0]), dy2, dz5))
    dmod = jnp.stack([jnp.concatenate([dmod_ss, dmod_gate]), dmod1])
    d_norm_w = jnp.stack([d_norm_w0, d_norm_w1])
    vec_parts = [dmod, d_norm_w, d_lre, d_lim, d_logdt, d_s5d, d_alog, d_dtb, d_fnw]
    tail_parts = [d_conv, d_gnw]
    mat_parts = [_groups_last(d) for d in (d_bre, d_bim, d_cre, d_cim)]
    n_vec = sum(math.prod(p.shape) for p in vec_parts)
    m_send, m_recv, m_srcs, m_lands, m_token = _spread_start_call(
        [_pack(vec_parts + tail_parts, ADAM_ROWS), _pack(mat_parts, SUBLANES).astype(BF16)], False, "gather_small_grads_start", dx)
    d_in5 = _split_cols_call(jnp.concatenate([d_wu.at[0, 0].add(m_token[0, 0]), d_wz], axis=1), "split_s5_w_in", BF16)
    u_send, u_recv, u_srcs, u_lands, u_token = _spread_start_call([d_in5], True, "scatter_s5_in_start", m_token)
    t_srcs, t_lands = _spread_wait_call(t_send, t_recv, t_srcs, t_lands, u_token, "scatter_s5_late_wait")
    s_srcs, s_lands = _spread_wait_call(s_send, s_recv, s_srcs, s_lands, t_lands[0], "scatter_gdn_wait")
    big = {}

    def owner_update(land, src, n):
        mine = lax.dynamic_index_in_dim(src, me, 0, keepdims=True)
        parts = lax.dynamic_update_slice(land, mine, (me, 0, 0))
        if n == "gdn_w_in":
            outs = _adam_call(parts, turned(W[n]), turned(M[n]), turned(V[n]), "adam_" + n, by_cols=True)
            return [jnp.transpose(o) for o in outs]
        return _adam_call(parts, W[n][0], M[n][0], V[n][0], "adam_" + n, rows=_tile(W[n].shape[1], 128))

    for land, src, n in zip(tuple(t_lands) + tuple(s_lands), tuple(t_srcs) + tuple(s_srcs), ("s5_w_glu", "s5_w_out") + GDN_BIG):
        big[n] = owner_update(land, src, n)

    m_srcs, m_lands = _spread_wait_call(m_send, m_recv, m_srcs, m_lands, big["gdn_w_out"][0], "gather_small_grads_wait")
    sg_vec, sg_mat = [lax.dynamic_update_slice(land, src[None], (me, 0, 0)) for land, src in zip(m_lands, m_srcs)]
    tot_vec = _sum_call(sg_vec, "sum_vec_grads", ADAM_ROWS)
    tot_mat = _sum_call(sg_mat, "sum_mat_grads", ADAM_ROWS)
    g_conv, g_gnw = _unpack(tot_vec.reshape(-1)[n_vec:], [d_conv.shape, d_gnw.shape])
    g_conv_mine = lax.dynamic_slice_in_dim(g_conv, me * n_conv, n_conv, axis=1)
    g_gnw_mine = lax.dynamic_slice_in_dim(g_gnw, me * n_gnw, n_gnw, axis=0)
    vec_names = VEC_NAMES + ("gdn_conv_w", "gdn_norm_w")
    vec_g = _pack([tot_vec.reshape(-1)[:n_vec], g_conv_mine, g_gnw_mine], ADAM_ROWS)
    vec = _adam_call(vec_g[None], _pack([W[n] for n in vec_names], ADAM_ROWS), _pack([M[n] for n in vec_names], ADAM_ROWS),
                     _pack([V[n] for n in vec_names], ADAM_ROWS), "adam_vec")
    vec = [_unpack(b, [W[n].shape for n in vec_names]) for b in vec]
    mats = []
    for name, g_mat in zip(MAT_NAMES, _unpack(tot_mat, [p.shape for p in mat_parts])):
        outs = _adam_call(g_mat[None], _groups_last(W[name]), _groups_last(M[name]), _groups_last(V[name]), "adam_" + name)
        mats.append([_groups_first(o, W[name].shape) for o in outs])

    dmod_all = sg_vec[:, :2 * 3 * D_MODEL // LANES].reshape(N_DEV, 2, N_DEV, n_ada // LANES, LANES)
    dmod_mine = lax.dynamic_index_in_dim(dmod_all, me, axis=2, keepdims=False).transpose(1, 0, 2, 3).reshape(2, N_DEV, n_ada)
    g_ada_w = _ada_grad_call(c_all, dmod_mine)
    ada = _adam_call(g_ada_w.reshape(1, -1, LANES), W["ada_w"].reshape(-1, LANES), M["ada_w"].reshape(-1, LANES),
                     V["ada_w"].reshape(-1, LANES), "adam_ada")
    u_srcs, u_lands = _spread_wait_call(u_send, u_recv, u_srcs, u_lands, ada[0], "scatter_s5_in_wait")
    big["s5_w_in"] = owner_update(u_lands[0], u_srcs[0], "s5_w_in")
    ada = [a.reshape(W["ada_w"].shape) for a in ada]

    res = {}
    for n in BIG_NAMES:
        res[n] = [o[None] for o in big[n]]
    for i, n in enumerate(vec_names):
        res[n] = [b[i] for b in vec]
    for i, n in enumerate(MAT_NAMES):
        res[n] = mats[i]
    res["ada_w"] = ada
    outs = [loss, dx[None]]
    for j in range(4):
        outs += [res[n][j] for n in WEIGHT_ORDER]
    return tuple(outs)


def kernel(x, c, ada_w, ada_b, norm_w, s5_w_in, s5_lambda_re, s5_lambda_im, s5_log_dt, s5_b_re, s5_b_im, s5_c_re, s5_c_im, s5_d, s5_w_glu, s5_w_out, gdn_w_in, gdn_conv_w, gdn_a_log, gdn_dt_bias, gdn_norm_w, gdn_w_out, final_norm_w, loss_target, m_ada_w, m_ada_b, m_norm_w, m_s5_w_in, m_s5_lambda_re, m_s5_lambda_im, m_s5_log_dt, m_s5_b_re, m_s5_b_im, m_s5_c_re, m_s5_c_im, m_s5_d, m_s5_w_glu, m_s5_w_out, m_gdn_w_in, m_gdn_conv_w, m_gdn_a_log, m_gdn_dt_bias, m_gdn_norm_w, m_gdn_w_out, m_final_norm_w, v_ada_w, v_ada_b, v_norm_w, v_s5_w_in, v_s5_lambda_re, v_s5_lambda_im, v_s5_log_dt, v_s5_b_re, v_s5_b_im, v_s5_c_re, v_s5_c_im, v_s5_d, v_s5_w_glu, v_s5_w_out, v_gdn_w_in, v_gdn_conv_w, v_gdn_a_log, v_gdn_dt_bias, v_gdn_norm_w, v_gdn_w_out, v_final_norm_w):
    W = dict(ada_w=ada_w, ada_b=ada_b, norm_w=norm_w, s5_w_in=s5_w_in, s5_lambda_re=s5_lambda_re, s5_lambda_im=s5_lambda_im,
             s5_log_dt=s5_log_dt, s5_b_re=s5_b_re, s5_b_im=s5_b_im, s5_c_re=s5_c_re, s5_c_im=s5_c_im, s5_d=s5_d,
             s5_w_glu=s5_w_glu, s5_w_out=s5_w_out, gdn_w_in=gdn_w_in, gdn_conv_w=gdn_conv_w, gdn_a_log=gdn_a_log,
             gdn_dt_bias=gdn_dt_bias, gdn_norm_w=gdn_norm_w, gdn_w_out=gdn_w_out, final_norm_w=final_norm_w)
    M = dict(ada_w=m_ada_w, ada_b=m_ada_b, norm_w=m_norm_w, s5_w_in=m_s5_w_in, s5_lambda_re=m_s5_lambda_re,
             s5_lambda_im=m_s5_lambda_im, s5_log_dt=m_s5_log_dt, s5_b_re=m_s5_b_re, s5_b_im=m_s5_b_im, s5_c_re=m_s5_c_re,
             s5_c_im=m_s5_c_im, s5_d=m_s5_d, s5_w_glu=m_s5_w_glu, s5_w_out=m_s5_w_out, gdn_w_in=m_gdn_w_in,
             gdn_conv_w=m_gdn_conv_w, gdn_a_log=m_gdn_a_log, gdn_dt_bias=m_gdn_dt_bias, gdn_norm_w=m_gdn_norm_w,
             gdn_w_out=m_gdn_w_out, final_norm_w=m_final_norm_w)
    V = dict(ada_w=v_ada_w, ada_b=v_ada_b, norm_w=v_norm_w, s5_w_in=v_s5_w_in, s5_lambda_re=v_s5_lambda_re,
             s5_lambda_im=v_s5_lambda_im, s5_log_dt=v_s5_log_dt, s5_b_re=v_s5_b_re, s5_b_im=v_s5_b_im, s5_c_re=v_s5_c_re,
             s5_c_im=v_s5_c_im, s5_d=v_s5_d, s5_w_glu=v_s5_w_glu, s5_w_out=v_s5_w_out, gdn_w_in=v_gdn_w_in,
             gdn_conv_w=v_gdn_conv_w, gdn_a_log=v_gdn_a_log, gdn_dt_bias=v_gdn_dt_bias, gdn_norm_w=v_gdn_norm_w,
             gdn_w_out=v_gdn_w_out, final_norm_w=v_final_norm_w)
    return _step(x, c, W, M, V, loss_target)
```

```python
import functools
import math

import jax
import jax.numpy as jnp
from jax import lax
from jax.experimental import pallas as pl
from jax.experimental.pallas import tpu as pltpu

F32 = jnp.float32
BF16 = jnp.bfloat16
SDS = jax.ShapeDtypeStruct

D_MODEL = 1024
D_INNER = 2048
NORM_EPS = 1e-6
S5_GROUP = 16
S5_GROUPS = 128
S5_STATE = 64
GDN_HEADS = 8
GDN_DK = 128
GDN_DV = 256
GDN_CONV = 4
GDN_CHUNK = 64
GDN_QK = 1024
GDN_CONV_CH = 4096
GDN_PROJ = 6160
ADAM_LR = 0.001
ADAM_B1 = 0.9
ADAM_B2 = 0.999
ADAM_EPS = 1e-08
ADAM_WD = 0.01
ADAM_STEP = 10

N_DEV = 8
LANES = 128
SUBLANES = 8
VMEM_BIG = 56 << 20
VMEM_MID = 40 << 20
S5_GB = 16
S5_BW = S5_GB * S5_GROUP
S5_TL = 1024
MESH_AXES = ("x", "y", "c")


def _params(sem, vmem=None):
    return pltpu.CompilerParams(dimension_semantics=sem, vmem_limit_bytes=vmem)


def _bdot(a, b, dims=(((1,), (0,)), ((), ()))):
    return lax.dot_general(a.astype(BF16), b.astype(BF16), dims, preferred_element_type=F32)


def _hdot(a, b, dims=(((1,), (0,)), ((), ()))):
    return lax.dot_general(a, b, dims, preferred_element_type=F32, precision=lax.Precision.HIGHEST)


_BNN = (((2,), (1,)), ((0,), (0,)))
_BNT = (((2,), (2,)), ((0,), (0,)))
_BTN = (((1,), (1,)), ((0,), (0,)))


@jax.custom_vjp
def _unit_lower_inverse(a):
    c = a.shape[-1]
    ri = lax.broadcasted_iota(jnp.int32, a.shape, 1)
    ci = lax.broadcasted_iota(jnp.int32, a.shape, 2)
    n = -a
    t = (ri == ci).astype(F32) + n
    for _ in range(int(math.log2(c)) - 1):
        n = _hdot(n, n, _BNN)
        t = t + _hdot(t, n, _BNN)
    return t


def _unit_lower_inverse_fwd(a):
    t = _unit_lower_inverse(a)
    return t, t


def _unit_lower_inverse_bwd(t, g):
    return (-_hdot(_hdot(t, g, _BTN), t, _BNT),)


_unit_lower_inverse.defvjp(_unit_lower_inverse_fwd, _unit_lower_inverse_bwd)


NN = (((1,), (0,)), ((), ()))
NT = (((1,), (1,)), ((), ()))
TN = (((0,), (0,)), ((), ()))


def _tile(n, pref):
    for t in (pref, 512, 256, 128):
        if t <= n and n % t == 0:
            return t
    return n


def _matmul(a, b, mode, name, add=None):
    if mode == "nn":
        (m, k), (_, n) = a.shape, b.shape
    elif mode == "nt":
        (m, k), (n, _) = a.shape, b.shape
    else:
        (k, m), (_, n) = a.shape, b.shape
    tm, tn, tk = _tile(m, 1024), _tile(n, 512), (k if k <= 2048 else _tile(k, 512))
    if mode == "tn":
        tm, tn, tk = _tile(m, 1024), _tile(n, 1024), _tile(k, 1024)
    nk = k // tk
    dims = {"nn": NN, "nt": NT, "tn": TN}[mode]

    def body(a_ref, b_ref, *rest):
        o_ref, acc_ref = rest[-2], rest[-1]
        part = _bdot(a_ref[...], b_ref[...], dims)
        if nk == 1:
            o_ref[...] = part if add is None else part + rest[0][...]
            return
        kk = pl.program_id(2)

        @pl.when(kk == 0)
        def _():
            acc_ref[...] = part if add is None else part + rest[0][...]

        @pl.when(kk > 0)
        def _():
            acc_ref[...] += part

        @pl.when(kk == nk - 1)
        def _():
            o_ref[...] = acc_ref[...]

    a_spec = pl.BlockSpec((tk, tm), lambda i, j, q: (q, i)) if mode == "tn" else pl.BlockSpec((tm, tk), lambda i, j, q: (i, q))
    b_spec = pl.BlockSpec((tn, tk), lambda i, j, q: (j, q)) if mode == "nt" else pl.BlockSpec((tk, tn), lambda i, j, q: (q, j))
    o_spec = pl.BlockSpec((tm, tn), lambda i, j, q: (i, j))
    return pl.pallas_call(
        body, name=name, grid=(m // tm, n // tn, nk),
        in_specs=[a_spec, b_spec] + ([] if add is None else [o_spec]), out_specs=o_spec,
        out_shape=SDS((m, n), F32), scratch_shapes=[pltpu.VMEM((tm, tn), F32)],
        compiler_params=_params(("parallel", "parallel", "arbitrary"), VMEM_MID),
    )(a, b, *([] if add is None else [add]))


def make_mm(name, pass_input=False):
    def primal(a, w):
        out = _matmul(a, w, "nn", name + "_fwd")
        return (out, a) if pass_input else out

    @jax.custom_vjp
    def mm(a, w, grad_slot):
        return primal(a, w)

    def fwd(a, w, grad_slot):
        return primal(a, w), (a, w)

    def bwd(res, g):
        a, w = res
        g, g_other = g if pass_input else (g, None)
        return _matmul(g, w, "nt", name + "_dx", add=g_other), jnp.zeros_like(w), _matmul(a, g, "tn", name + "_dw")

    mm.defvjp(fwd, bwd)
    return mm


PROJ_ROWS = 256


def _proj_fwd_call(a, ws, name, w_rows):
    m, k = a.shape
    tm = _tile(m, PROJ_ROWS)
    nw = len(ws)
    widths = [w.shape[0] if w_rows else w.shape[1] for w in ws]

    def body(*refs):
        ab = refs[0][...].astype(BF16)
        for w_ref, o_ref in zip(refs[1:1 + nw], refs[1 + nw:]):
            o_ref[...] = lax.dot_general(ab, w_ref[...], NT if w_rows else NN, preferred_element_type=F32)

    return pl.pallas_call(
        body, name=name, grid=(m // tm,),
        in_specs=[pl.BlockSpec((tm, k), lambda i: (i, 0))] + [pl.BlockSpec(w.shape, lambda i: (0, 0)) for w in ws],
        out_specs=[pl.BlockSpec((tm, n), lambda i: (i, 0)) for n in widths],
        out_shape=[SDS((m, n), F32) for n in widths],
        compiler_params=_params(("parallel",), VMEM_BIG),
    )(a, *ws)


def _proj_dx_call(gs, ws, name, w_rows):
    m = gs[0].shape[0]
    k = ws[0].shape[1] if w_rows else ws[0].shape[0]
    tm = _tile(m, PROJ_ROWS)
    nw = len(ws)

    def body(*refs):
        acc = None
        for g_ref, w_ref in zip(refs[:nw], refs[nw:2 * nw]):
            part = _bdot(g_ref[...], w_ref[...], NN if w_rows else NT)
            acc = part if acc is None else acc + part
        refs[2 * nw][...] = acc

    return pl.pallas_call(
        body, name=name, grid=(m // tm,),
        in_specs=[pl.BlockSpec((tm, g.shape[1]), lambda i: (i, 0)) for g in gs] + [pl.BlockSpec(w.shape, lambda i: (0, 0)) for w in ws],
        out_specs=pl.BlockSpec((tm, k), lambda i: (i, 0)), out_shape=SDS((m, k), F32),
        compiler_params=_params(("parallel",), VMEM_BIG),
    )(*gs, *ws)


def make_proj(name, w_rows=False):
    @jax.custom_vjp
    def proj(a, ws, grad_slots):
        return tuple(_proj_fwd_call(a, ws, name + "_fwd", w_rows))

    def fwd(a, ws, grad_slots):
        return tuple(_proj_fwd_call(a, ws, name + "_fwd", w_rows)), (a, ws)

    def bwd(res, gs):
        a, ws = res
        dws = tuple(_matmul(g, a, "tn", "%s_dw%d" % (name, i)) if w_rows else _matmul(a, g, "tn", "%s_dw%d" % (name, i))
                    for i, g in enumerate(gs))
        return _proj_dx_call(tuple(gs), ws, name + "_dx", w_rows), tuple(jnp.zeros_like(w) for w in ws), dws

    proj.defvjp(fwd, bwd)
    return proj


def make_rowwise(f, name, tm, n_rows, n_params, vmem=VMEM_MID, pass_first=False):
    def specs_of(arrs, blocked):
        if blocked:
            return [pl.BlockSpec((tm, a.shape[1]), lambda i: (i, 0)) for a in arrs]
        return [pl.BlockSpec(a.shape, lambda i: (0, 0)) for a in arrs]

    def out_structs(rows, params):
        blk = [SDS((tm, r.shape[1]), r.dtype) for r in rows] + [SDS(p.shape, p.dtype) for p in params]
        return jax.eval_shape(f, *blk)

    def run_fwd(rows, params):
        L = rows[0].shape[0]
        outs = out_structs(rows, params)

        def body(*refs):
            ins = [r[...] for r in refs[:n_rows + n_params]]
            res = f(*ins)
            for o_ref, val in zip(refs[n_rows + n_params:], res):
                o_ref[...] = val

        return pl.pallas_call(
            body, name=name + "_fwd", grid=(L // tm,),
            in_specs=specs_of(rows, True) + specs_of(params, False),
            out_specs=[pl.BlockSpec((tm, o.shape[1]), lambda i: (i, 0)) for o in outs],
            out_shape=[SDS((L, o.shape[1]), o.dtype) for o in outs],
            compiler_params=_params(("parallel",), vmem),
        )(*rows, *params)

    def run_bwd(rows, params, gs):
        L = rows[0].shape[0]
        n_g = len(gs)

        def body(*refs):
            i = pl.program_id(0)
            ins = [r[...] for r in refs[:n_rows + n_params]]
            cts = tuple(r[...] for r in refs[n_rows + n_params:n_rows + n_params + n_g])
            outs = refs[n_rows + n_params + n_g:]
            _, vjp = jax.vjp(f, *ins)
            grads = vjp(cts[:-1] if pass_first else cts)
            if pass_first:
                grads = (grads[0] + cts[-1],) + tuple(grads[1:])
            for o_ref, val in zip(outs[:n_rows], grads[:n_rows]):
                o_ref[...] = val

            if n_params:
                @pl.when(i == 0)
                def _():
                    for o_ref in outs[n_rows:]:
                        o_ref[...] = jnp.zeros_like(o_ref)
                for o_ref, val in zip(outs[n_rows:], grads[n_rows:]):
                    o_ref[...] += val

        res = pl.pallas_call(
            body, name=name + "_bwd", grid=(L // tm,),
            in_specs=specs_of(rows, True) + specs_of(params, False) + specs_of(gs, True),
            out_specs=specs_of(rows, True) + specs_of(params, False),
            out_shape=[SDS(r.shape, r.dtype) for r in rows] + [SDS(p.shape, p.dtype) for p in params],
            compiler_params=_params(("arbitrary",), vmem),
        )(*rows, *params, *gs)
        return tuple(res[:n_rows]), tuple(res[n_rows:])

    def outputs(rows, params):
        outs = tuple(run_fwd(rows, params))
        return outs + (rows[0],) if pass_first else outs

    @jax.custom_vjp
    def op(rows, params):
        return outputs(rows, params)

    def fwd(rows, params):
        return outputs(rows, params), (rows, params)

    def bwd(res, gs):
        rows, params = res
        return run_bwd(rows, params, tuple(gs))

    op.defvjp(fwd, bwd)
    op.run_fwd, op.run_bwd = run_fwd, run_bwd
    return op


def _s5_scan_rows(xr_ref, xi_ref, ar, ai, x0r, x0i, tl, reverse=False):
    n = xr_ref.shape[1]
    T = SUBLANES
    row = lax.broadcasted_iota(jnp.int32, (T, n), 0)
    pr, pi = [ar], [ai]
    for _ in range(T - 1):
        pr, pi = pr + [pr[-1] * ar - pi[-1] * ai], pi + [pr[-1] * ai + pi[-1] * ar]
    levels = []
    for d in (1, 2, 4):
        mask = (row < T - d) if reverse else (row >= d)
        levels.append((T - d if reverse else d, jnp.where(mask, pr[d - 1], 0.0), jnp.where(mask, pi[d - 1], 0.0)))
    cr = jnp.zeros((T, n), F32)
    ci = jnp.zeros((T, n), F32)
    for r in range(T):
        k = (T - r) if reverse else (r + 1)
        cr = jnp.where(row == r, pr[k - 1], cr)
        ci = jnp.where(row == r, pi[k - 1], ci)
    nt = tl // T
    last = 0 if reverse else T - 1

    def step(t, carry):
        sr, si = carry
        base = pl.multiple_of((nt - 1 - t if reverse else t) * T, T)
        br = xr_ref[pl.ds(base, T), :]
        bi = xi_ref[pl.ds(base, T), :]
        for shift, mr, mi in levels:
            qr = pltpu.roll(br, shift, 0)
            qi = pltpu.roll(bi, shift, 0)
            br, bi = br + (mr * qr - mi * qi), bi + (mr * qi + mi * qr)
        xr = br + (cr * sr - ci * si)
        xi = bi + (cr * si + ci * sr)
        xr_ref[pl.ds(base, T), :] = xr
        xi_ref[pl.ds(base, T), :] = xi
        return xr[last:last + 1, :], xi[last:last + 1, :]
    return lax.fori_loop(0, nt, step, (x0r, x0i))


def _s5_fwd_call(u, bre, bim, cre, cim, a, d, tl):
    L, e = u.shape
    nb = e // S5_BW
    ns = bre.shape[2]
    nc = L // tl

    def body(u_ref, bre_ref, bim_ref, cre_ref, cim_ref, a_ref, d_ref, y_ref, xb_ref, sr_ref, si_ref, xr_ref, xi_ref, carry_ref):
        c = pl.program_id(1)

        @pl.when(c == 0)
        def _():
            carry_ref[...] = jnp.zeros_like(carry_ref)
        xb_ref[0, 0] = carry_ref[...]
        ub = u_ref[...]
        xr_ref[...] = _bdot(ub, bre_ref[0])
        xi_ref[...] = _bdot(ub, bim_ref[0])
        ar = a_ref[0, 0:1, :]
        ai = a_ref[0, 1:2, :]
        xr, xi = _s5_scan_rows(xr_ref, xi_ref, ar, ai, carry_ref[0:1, :], carry_ref[1:2, :], tl)
        carry_ref[0:1, :] = xr
        carry_ref[1:2, :] = xi
        sr = xr_ref[...].astype(BF16)
        si = xi_ref[...].astype(BF16)
        sr_ref[...] = sr
        si_ref[...] = si
        y_ref[...] = _f_s5_act(_bdot(sr, cre_ref[0]) - _bdot(si, cim_ref[0]), ub, d_ref[...])[0]

    return pl.pallas_call(
        body, name="s5_core_fwd", grid=(nb, nc),
        in_specs=[pl.BlockSpec((tl, S5_BW), lambda j, c: (c, j)),
                  pl.BlockSpec((1, S5_BW, ns), lambda j, c: (j, 0, 0)), pl.BlockSpec((1, S5_BW, ns), lambda j, c: (j, 0, 0)),
                  pl.BlockSpec((1, ns, S5_BW), lambda j, c: (j, 0, 0)), pl.BlockSpec((1, ns, S5_BW), lambda j, c: (j, 0, 0)),
                  pl.BlockSpec((1, SUBLANES, ns), lambda j, c: (j, 0, 0)), pl.BlockSpec((1, S5_BW), lambda j, c: (0, j))],
        out_specs=[pl.BlockSpec((tl, S5_BW), lambda j, c: (c, j)),
                   pl.BlockSpec((1, 1, SUBLANES, ns), lambda j, c: (j, c, 0, 0)),
                   pl.BlockSpec((tl, ns), lambda j, c: (c, j)), pl.BlockSpec((tl, ns), lambda j, c: (c, j))],
        out_shape=[SDS((L, e), F32), SDS((nb, nc, SUBLANES, ns), F32), SDS((L, nb * ns), BF16), SDS((L, nb * ns), BF16)],
        scratch_shapes=[pltpu.VMEM((tl, ns), F32), pltpu.VMEM((tl, ns), F32), pltpu.VMEM((SUBLANES, ns), F32)],
        compiler_params=_params(("arbitrary", "arbitrary"), VMEM_BIG),
    )(u, bre, bim, cre, cim, a, d)


def _s5_bwd_call(u, dy2, bre, bim, cre, cim, a, d, xb, sr, si, tl):
    L, e = u.shape
    nb = e // S5_BW
    ns = bre.shape[2]
    nc = L // tl

    def body(u_ref, dy2_ref, bre_ref, bim_ref, cre_ref, cim_ref, a_ref, d_ref, xb_ref, sr_ref, si_ref,
             du_ref, dbre_ref, dbim_ref, dcre_ref, dcim_ref, da_ref, dd_ref,
             gr_ref, gi_ref, gcarry_ref):
        c = pl.program_id(1)

        @pl.when(c == 0)
        def _():
            gcarry_ref[...] = jnp.zeros_like(gcarry_ref)
            dbre_ref[...] = jnp.zeros_like(dbre_ref)
            dbim_ref[...] = jnp.zeros_like(dbim_ref)
            dcre_ref[...] = jnp.zeros_like(dcre_ref)
            dcim_ref[...] = jnp.zeros_like(dcim_ref)
            da_ref[...] = jnp.zeros_like(da_ref)
            dd_ref[...] = jnp.zeros_like(dd_ref)

        ub = u_ref[...]
        ys = _bdot(sr_ref[...], cre_ref[0]) - _bdot(si_ref[...], cim_ref[0])
        _, act_vjp = jax.vjp(lambda *t: _f_s5_act(*t)[0], ys, ub, d_ref[...])
        dy, du_skip, dd = act_vjp(dy2_ref[...])
        dd_ref[...] += dd
        ar = a_ref[0, 0:1, :]
        ai = a_ref[0, 1:2, :]
        x0r = xb_ref[0, 0, 0:1, :]
        x0i = xb_ref[0, 0, 1:2, :]
        dcre_ref[0] += _bdot(sr_ref[...], dy, TN)
        dcim_ref[0] -= _bdot(si_ref[...], dy, TN)
        gr_ref[...] = _bdot(dy, cre_ref[0], NT)
        gi_ref[...] = -_bdot(dy, cim_ref[0], NT)

        g0r, g0i = _s5_scan_rows(gr_ref, gi_ref, ar, -ai, gcarry_ref[0:1, :], gcarry_ref[1:2, :], tl, reverse=True)
        gcarry_ref[0:1, :] = g0r
        gcarry_ref[1:2, :] = g0i
        row = lax.broadcasted_iota(jnp.int32, (tl, ns), 0)
        gr = gr_ref[...]
        gi = gi_ref[...]
        xpr = jnp.where(row == 0, x0r, pltpu.roll(sr_ref[...].astype(F32), 1, 0))
        xpi = jnp.where(row == 0, x0i, pltpu.roll(si_ref[...].astype(F32), 1, 0))
        da_ref[0, 0:1, :] += jnp.sum(gr * xpr + gi * xpi, axis=0, keepdims=True)
        da_ref[0, 1:2, :] += jnp.sum(gi * xpr - gr * xpi, axis=0, keepdims=True)
        du_ref[...] = (_bdot(gr, bre_ref[0], NT) + _bdot(gi, bim_ref[0], NT)) + du_skip
        dbre_ref[0] += _bdot(ub, gr, TN)
        dbim_ref[0] += _bdot(ub, gi, TN)

    rev = lambda c: nc - 1 - c
    return pl.pallas_call(
        body, name="s5_core_bwd", grid=(nb, nc),
        in_specs=[pl.BlockSpec((tl, S5_BW), lambda j, c: (rev(c), j)), pl.BlockSpec((tl, S5_BW), lambda j, c: (rev(c), j)),
                  pl.BlockSpec((1, S5_BW, ns), lambda j, c: (j, 0, 0)), pl.BlockSpec((1, S5_BW, ns), lambda j, c: (j, 0, 0)),
                  pl.BlockSpec((1, ns, S5_BW), lambda j, c: (j, 0, 0)), pl.BlockSpec((1, ns, S5_BW), lambda j, c: (j, 0, 0)),
                  pl.BlockSpec((1, SUBLANES, ns), lambda j, c: (j, 0, 0)), pl.BlockSpec((1, S5_BW), lambda j, c: (0, j)),
                  pl.BlockSpec((1, 1, SUBLANES, ns), lambda j, c: (j, rev(c), 0, 0)),
                  pl.BlockSpec((tl, ns), lambda j, c: (rev(c), j)), pl.BlockSpec((tl, ns), lambda j, c: (rev(c), j))],
        out_specs=[pl.BlockSpec((tl, S5_BW), lambda j, c: (rev(c), j)),
                   pl.BlockSpec((1, S5_BW, ns), lambda j, c: (j, 0, 0)), pl.BlockSpec((1, S5_BW, ns), lambda j, c: (j, 0, 0)),
                   pl.BlockSpec((1, ns, S5_BW), lambda j, c: (j, 0, 0)), pl.BlockSpec((1, ns, S5_BW), lambda j, c: (j, 0, 0)),
                   pl.BlockSpec((1, SUBLANES, ns), lambda j, c: (j, 0, 0)), pl.BlockSpec((1, S5_BW), lambda j, c: (0, j))],
        out_shape=[SDS((L, e), F32), SDS(bre.shape, F32), SDS(bim.shape, F32), SDS(cre.shape, F32), SDS(cim.shape, F32),
                   SDS(a.shape, F32), SDS(d.shape, F32)],
        scratch_shapes=[pltpu.VMEM((tl, ns), F32) for _ in range(2)] + [pltpu.VMEM((SUBLANES, ns), F32)],
        compiler_params=_params(("arbitrary", "arbitrary"), VMEM_BIG),
    )(u, dy2, bre, bim, cre, cim, a, d, xb, sr, si)


def make_s5_core(tl):
    @jax.custom_vjp
    def s5_core(u, bre, bim, cre, cim, a, d):
        return _s5_fwd_call(u, bre, bim, cre, cim, a, d, tl)[0]

    def fwd(u, bre, bim, cre, cim, a, d):
        y2, xb, sr, si = _s5_fwd_call(u, bre, bim, cre, cim, a, d, tl)
        return y2, (u, bre, bim, cre, cim, a, d, xb, sr, si)

    def bwd(res, dy2):
        u, bre, bim, cre, cim, a, d, xb, sr, si = res
        return tuple(_s5_bwd_call(u, dy2, bre, bim, cre, cim, a, d, xb, sr, si, tl))

    s5_core.defvjp(fwd, bwd)
    return s5_core


def _s5_block_params(lam_re, lam_im, log_dt, b_re, b_im, c_re, c_im):
    dt = jnp.exp(log_dt)[:, None]
    mag = jnp.exp(lam_re * dt)
    ab_re = mag * jnp.cos(lam_im * dt)
    ab_im = mag * jnp.sin(lam_im * dt)
    den = lam_re * lam_re + lam_im * lam_im
    nr = ab_re - 1.0
    ni = ab_im
    q_re = (nr * lam_re + ni * lam_im) / den
    q_im = (ni * lam_re - nr * lam_im) / den
    bb_re = q_re[..., None] * b_re - q_im[..., None] * b_im
    bb_im = q_re[..., None] * b_im + q_im[..., None] * b_re
    nb = S5_GROUPS // S5_GB
    eye = jnp.eye(S5_GB, dtype=F32)

    def bdiag_in(bb):
        t = bb.reshape(nb, S5_GB, S5_STATE, S5_GROUP)
        t = jnp.einsum("jgpm,gh->jgmhp", t, eye)
        return t.reshape(nb, S5_GB * S5_GROUP, S5_GB * S5_STATE)

    def bdiag_out(cc):
        t = cc.reshape(nb, S5_GB, S5_GROUP, S5_STATE)
        t = jnp.einsum("jgmp,gh->jgphm", t, eye)
        return t.reshape(nb, S5_GB * S5_STATE, S5_GB * S5_GROUP)

    a = jnp.stack([ab_re.reshape(nb, S5_GB * S5_STATE), ab_im.reshape(nb, S5_GB * S5_STATE)], axis=1)
    a = jnp.concatenate([a, jnp.zeros((nb, SUBLANES - 2, S5_GB * S5_STATE), F32)], axis=1)
    return bdiag_in(bb_re), bdiag_in(bb_im), bdiag_out(c_re), bdiag_out(c_im), a


def _shift_down(x, s, row):
    if s == 0:
        return x
    return jnp.where(row >= s, pltpu.roll(x, s, 0), 0.0)


def _shift_up(x, s, row, n):
    if s == 0:
        return x
    return jnp.where(row < n - s, pltpu.roll(x, n - s, 0), 0.0)


def _causal_conv(xv, w_ref, row):
    acc = jnp.zeros_like(xv)
    for j in range(GDN_CONV):
        acc += w_ref[j:j + 1, :] * _shift_down(xv, GDN_CONV - 1 - j, row)
    return acc


def _conv_fwd_call(x, w, act, name):
    L, ch = x.shape

    def body(x_ref, w_ref, y_ref):
        xv = x_ref[...]
        row = lax.broadcasted_iota(jnp.int32, xv.shape, 0)
        y_ref[...] = act(_causal_conv(xv, w_ref, row))

    return pl.pallas_call(
        body, name=name + "_fwd", grid=(ch // LANES,),
        in_specs=[pl.BlockSpec((L, LANES), lambda j: (0, j)), pl.BlockSpec((SUBLANES, LANES), lambda j: (0, j))],
        out_specs=pl.BlockSpec((L, LANES), lambda j: (0, j)), out_shape=SDS((L, ch), F32),
        compiler_params=_params(("parallel",), VMEM_MID),
    )(x, w)


def _conv_bwd_call(x, w, dy, act, name):
    L, ch = x.shape

    def body(x_ref, w_ref, dy_ref, dx_ref, dw_ref):
        xv = x_ref[...]
        row = lax.broadcasted_iota(jnp.int32, xv.shape, 0)
        _, act_vjp = jax.vjp(act, _causal_conv(xv, w_ref, row))
        (g,) = act_vjp(dy_ref[...])
        acc = jnp.zeros_like(xv)
        dws = []
        for j in range(GDN_CONV):
            s = GDN_CONV - 1 - j
            acc += w_ref[j:j + 1, :] * _shift_up(g, s, row, L)
            dws.append(jnp.sum(g * _shift_down(xv, s, row), axis=0, keepdims=True))
        dx_ref[...] = acc
        dw_ref[...] = jnp.concatenate(dws + [jnp.zeros((SUBLANES - GDN_CONV, LANES), F32)], axis=0)

    return pl.pallas_call(
        body, name=name + "_bwd", grid=(ch // LANES,),
        in_specs=[pl.BlockSpec((L, LANES), lambda j: (0, j)), pl.BlockSpec((SUBLANES, LANES), lambda j: (0, j)),
                  pl.BlockSpec((L, LANES), lambda j: (0, j))],
        out_specs=[pl.BlockSpec((L, LANES), lambda j: (0, j)), pl.BlockSpec((SUBLANES, LANES), lambda j: (0, j))],
        out_shape=[SDS((L, ch), F32), SDS((SUBLANES, ch), F32)],
        compiler_params=_params(("parallel",), VMEM_MID),
    )(x, w, dy)


def make_conv_act(act, name):
    @jax.custom_vjp
    def op(x, w):
        return _conv_fwd_call(x, w, act, name)

    def fwd(x, w):
        return _conv_fwd_call(x, w, act, name), (x, w)

    def bwd(res, dy):
        x, w = res
        return tuple(_conv_bwd_call(x, w, dy, act, name))

    op.defvjp(fwd, bwd)
    return op


BNN, BNT, BTN = _BNN, _BNT, _BTN
GDN_PREP_BATCH = 8


@jax.custom_vjp
def _known_inverse(a, t):
    return t


def _known_inverse_fwd(a, t):
    return t, t


def _known_inverse_bwd(t, g):
    return -_hdot(_hdot(t, g, _BTN), t, _BNT), jnp.zeros_like(t)


_known_inverse.defvjp(_known_inverse_fwd, _known_inverse_bwd)


def _gdn_prep_math(q, k, v, beta, g, t_saved=None):
    B, C = q.shape[0], q.shape[1]
    ri = lax.broadcasted_iota(jnp.int32, (B, C, C), 1)
    ci = lax.broadcasted_iota(jnp.int32, (B, C, C), 2)
    causal = ri >= ci
    strict = ri > ci
    eye = (ri == ci).astype(F32)
    gb = jnp.broadcast_to(g, (B, C, C))
    g_row = jnp.sum(gb * eye, axis=1, keepdims=True)
    gc_col = jnp.sum(jnp.where(causal, jnp.broadcast_to(g_row, (B, C, C)), 0.0), axis=2, keepdims=True)
    gc_row = jnp.sum(jnp.where(ri <= ci, gb, 0.0), axis=1, keepdims=True)
    decay = jnp.exp(jnp.where(causal, gc_col - gc_row, -jnp.inf))
    kk = _bdot(k, k, BNT)
    a_mat = jnp.where(strict, beta * kk * decay, 0.0)
    t = _unit_lower_inverse(a_mat) if t_saved is None else _known_inverse(a_mat, t_saved)
    e_gc = jnp.exp(gc_col)
    w = _hdot(t, beta * e_gc * k, BNN)
    u = _hdot(t, beta * v, BNN)
    qk = _bdot(q, k, BNT) * decay
    q_dec = q * e_gc
    g_last = gc_col[:, C - 1:C, :]
    k_dec = k * jnp.exp(g_last - gc_col)
    return q_dec, w, u, qk, k_dec, gc_col, t


def _gdn_prep_specs(L):
    C = GDN_CHUNK
    nb = min(GDN_PREP_BATCH, L // C)
    R = nb * C
    ins = [pl.BlockSpec((R, GDN_DK), lambda c, h: (c, h)), pl.BlockSpec((R, GDN_DK), lambda c, h: (c, h)),
           pl.BlockSpec((R, GDN_DV), lambda c, h: (c, h)), pl.BlockSpec((R, LANES), lambda c, h: (c, 0))]
    outs = [pl.BlockSpec((1, R, GDN_DK), lambda c, h: (h, c, 0)), pl.BlockSpec((1, R, GDN_DK), lambda c, h: (h, c, 0)),
            pl.BlockSpec((1, R, GDN_DV), lambda c, h: (h, c, 0)), pl.BlockSpec((1, R, C), lambda c, h: (h, c, 0)),
            pl.BlockSpec((1, R, GDN_DK), lambda c, h: (h, c, 0)), pl.BlockSpec((1, R, 1), lambda c, h: (h, c, 0))]
    t_spec = pl.BlockSpec((1, R, C), lambda c, h: (h, c, 0))
    shapes = [SDS((GDN_HEADS, L, GDN_DK), F32), SDS((GDN_HEADS, L, GDN_DK), F32), SDS((GDN_HEADS, L, GDN_DV), F32),
              SDS((GDN_HEADS, L, C), F32), SDS((GDN_HEADS, L, GDN_DK), F32), SDS((GDN_HEADS, L, 1), F32)]
    return ins, outs, t_spec, shapes, nb


def _chunks(x, nb):
    return x.reshape(nb, x.shape[0] // nb, x.shape[1])


def _head_columns(bg, h):
    lane = lax.broadcasted_iota(jnp.int32, bg.shape, 1)
    beta = jnp.sum(jnp.where(lane == h, bg, 0.0), axis=1, keepdims=True)
    g = jnp.sum(jnp.where(lane == h + GDN_HEADS, bg, 0.0), axis=1, keepdims=True)
    return beta, g


def _gdn_prep_fwd_call(q, k, v, bg):
    L = q.shape[0]
    ins, outs, t_spec, shapes, nb = _gdn_prep_specs(L)

    def body(q_ref, k_ref, v_ref, bg_ref, *o_refs):
        beta, g = _head_columns(bg_ref[...], pl.program_id(1))
        res = _gdn_prep_math(_chunks(q_ref[...], nb), _chunks(k_ref[...], nb), _chunks(v_ref[...], nb),
                             _chunks(beta, nb), _chunks(g, nb))
        for o_ref, val in zip(o_refs, res):
            o_ref[0] = val.reshape(val.shape[0] * val.shape[1], val.shape[2])

    return pl.pallas_call(
        body, name="gdn_prep_fwd", grid=(L // (nb * GDN_CHUNK), GDN_HEADS), in_specs=ins, out_specs=outs + [t_spec],
        out_shape=shapes + [SDS((GDN_HEADS, L, GDN_CHUNK), F32)],
        compiler_params=_params(("parallel", "parallel"), VMEM_MID),
    )(q, k, v, bg)


def _gdn_prep_bwd_call(q, k, v, bg, t, cts):
    L = q.shape[0]
    ins, outs, t_spec, _, nb = _gdn_prep_specs(L)

    def body(q_ref, k_ref, v_ref, bg_ref, t_ref, c0, c1, c2, c3, c4, c5, dq_ref, dk_ref, dv_ref, dbg_ref):
        h = pl.program_id(1)
        beta, g = _head_columns(bg_ref[...], h)
        t_saved = _chunks(t_ref[0], nb)
        _, vjp = jax.vjp(lambda *a: _gdn_prep_math(*a, t_saved=t_saved)[:6], _chunks(q_ref[...], nb), _chunks(k_ref[...], nb),
                         _chunks(v_ref[...], nb), _chunks(beta, nb), _chunks(g, nb))
        dq, dk, dv, db, dg = vjp(tuple(_chunks(c[0], nb) for c in (c0, c1, c2, c3, c4, c5)))
        flat = lambda a: a.reshape(a.shape[0] * a.shape[1], a.shape[2])
        dq_ref[...] = flat(dq)
        dk_ref[...] = flat(dk)
        dv_ref[...] = flat(dv)

        @pl.when(h == 0)
        def _():
            dbg_ref[...] = jnp.zeros_like(dbg_ref)
        lane = lax.broadcasted_iota(jnp.int32, dbg_ref.shape, 1)
        dbg_ref[...] += jnp.where(lane == h, flat(db), 0.0) + jnp.where(lane == h + GDN_HEADS, flat(dg), 0.0)

    return pl.pallas_call(
        body, name="gdn_prep_bwd", grid=(L // (nb * GDN_CHUNK), GDN_HEADS), in_specs=ins + [t_spec] + outs, out_specs=ins,
        out_shape=[SDS(q.shape, F32), SDS(k.shape, F32), SDS(v.shape, F32), SDS(bg.shape, F32)],
        compiler_params=_params(("parallel", "arbitrary"), VMEM_MID),
    )(q, k, v, bg, t, *cts)


@jax.custom_vjp
def gdn_prep(q, k, v, bg):
    return tuple(_gdn_prep_fwd_call(q, k, v, bg)[:6])


def _gdn_prep_f(q, k, v, bg):
    res = _gdn_prep_fwd_call(q, k, v, bg)
    return tuple(res[:6]), (q, k, v, bg, res[6])


def _gdn_prep_b(res, cts):
    return tuple(_gdn_prep_bwd_call(*res, tuple(cts)))


gdn_prep.defvjp(_gdn_prep_f, _gdn_prep_b)


def _gdn_step_math(q_dec, w, u, qk, k_dec, gc, z, nw, state):
    H, C = q_dec.shape[0], q_dec.shape[1]
    v_new = u - _bdot(w, state, BNN)
    o = _bdot(q_dec, state, BNN) + _bdot(qk, v_new, BNN)
    gl = gc[:, C - 1:C, :]
    new_state = jnp.exp(gl) * state + _bdot(k_dec, v_new, BTN)
    return _f_gdn_post(jnp.concatenate([o[h] for h in range(H)], axis=1), z, nw)[0], new_state


def _gdn_scan_specs(L, rev):
    C, H = GDN_CHUNK, GDN_HEADS
    nc = L // C
    cc = (lambda c: nc - 1 - c) if rev else (lambda c: c)
    ins = [pl.BlockSpec((H, C, GDN_DK), lambda c: (0, cc(c), 0)), pl.BlockSpec((H, C, GDN_DK), lambda c: (0, cc(c), 0)),
           pl.BlockSpec((H, C, GDN_DV), lambda c: (0, cc(c), 0)), pl.BlockSpec((H, C, C), lambda c: (0, cc(c), 0)),
           pl.BlockSpec((H, C, GDN_DK), lambda c: (0, cc(c), 0)), pl.BlockSpec((H, C, 1), lambda c: (0, cc(c), 0))]
    o_spec = pl.BlockSpec((C, H * GDN_DV), lambda c: (cc(c), 0))
    nw_spec = pl.BlockSpec((1, H * GDN_DV), lambda c: (0, 0))
    s_spec = pl.BlockSpec((1, H, GDN_DK, GDN_DV), lambda c: (cc(c), 0, 0, 0))
    return ins + [o_spec, nw_spec], o_spec, s_spec, nc


def _gdn_scan_fwd_call(q_dec, w, u, qk, k_dec, gc, z, nw):
    L = q_dec.shape[1]
    ins, o_spec, s_spec, nc = _gdn_scan_specs(L, False)

    def body(qd_ref, w_ref, u_ref, qk_ref, kd_ref, gc_ref, z_ref, nw_ref, o_ref, sin_ref, s_ref):
        c = pl.program_id(0)

        @pl.when(c == 0)
        def _():
            s_ref[...] = jnp.zeros_like(s_ref)
        st = s_ref[...]
        sin_ref[0] = st
        o, ns = _gdn_step_math(qd_ref[...], w_ref[...], u_ref[...], qk_ref[...], kd_ref[...], gc_ref[...], z_ref[...], nw_ref[...], st)
        o_ref[...] = o
        s_ref[...] = ns

    return pl.pallas_call(
        body, name="gdn_scan_fwd", grid=(nc,), in_specs=ins, out_specs=[o_spec, s_spec],
        out_shape=[SDS((L, GDN_HEADS * GDN_DV), F32), SDS((nc, GDN_HEADS, GDN_DK, GDN_DV), F32)],
        scratch_shapes=[pltpu.VMEM((GDN_HEADS, GDN_DK, GDN_DV), F32)],
        compiler_params=_params(("arbitrary",), VMEM_MID),
    )(q_dec, w, u, qk, k_dec, gc, z, nw)


def _gdn_scan_bwd_call(q_dec, w, u, qk, k_dec, gc, z, nw, s_in, do):
    L = q_dec.shape[1]
    ins, o_spec, s_spec, nc = _gdn_scan_specs(L, True)

    def body(qd_ref, w_ref, u_ref, qk_ref, kd_ref, gc_ref, z_ref, nw_ref, sin_ref, do_ref,
             dqd_ref, dw_ref, du_ref, dqk_ref, dkd_ref, dgc_ref, dz_ref, dnw_ref, ds_ref):
        c = pl.program_id(0)

        @pl.when(c == 0)
        def _():
            ds_ref[...] = jnp.zeros_like(ds_ref)
            dnw_ref[...] = jnp.zeros_like(dnw_ref)
        _, vjp = jax.vjp(_gdn_step_math, qd_ref[...], w_ref[...], u_ref[...], qk_ref[...], kd_ref[...], gc_ref[...],
                         z_ref[...], nw_ref[...], sin_ref[0])
        dqd, dw, du, dqk, dkd, dgc, dz, dnw, dst = vjp((do_ref[...], ds_ref[...]))
        dqd_ref[...] = dqd
        dw_ref[...] = dw
        du_ref[...] = du
        dqk_ref[...] = dqk
        dkd_ref[...] = dkd
        dgc_ref[...] = dgc
        dz_ref[...] = dz
        dnw_ref[...] += dnw
        ds_ref[...] = dst

    return pl.pallas_call(
        body, name="gdn_scan_bwd", grid=(nc,), in_specs=ins + [s_spec, o_spec], out_specs=ins,
        out_shape=[SDS(t.shape, F32) for t in (q_dec, w, u, qk, k_dec, gc, z, nw)],
        scratch_shapes=[pltpu.VMEM((GDN_HEADS, GDN_DK, GDN_DV), F32)],
        compiler_params=_params(("arbitrary",), VMEM_MID),
    )(q_dec, w, u, qk, k_dec, gc, z, nw, s_in, do)


@jax.custom_vjp
def gdn_scan(q_dec, w, u, qk, k_dec, gc, z, nw):
    return _gdn_scan_fwd_call(q_dec, w, u, qk, k_dec, gc, z, nw)[0]


def _gdn_scan_f(*args):
    o, s_in = _gdn_scan_fwd_call(*args)
    return o, (*args, s_in)


def _gdn_scan_b(res, do):
    return tuple(_gdn_scan_bwd_call(*res, do))


gdn_scan.defvjp(_gdn_scan_f, _gdn_scan_b)


def _silu(x):
    return x * jax.nn.sigmoid(x)


def _gelu_tanh(x):
    return 0.5 * x * (1.0 + jnp.tanh(math.sqrt(2.0 / math.pi) * (x + 0.044715 * (x * x * x))))


def _f_lnmod(x, nw, sc, sh, bsc, bsh):
    xn = x * lax.rsqrt(jnp.mean(x * x, axis=-1, keepdims=True) + NORM_EPS) * nw
    return (xn * (1.0 + (sc + bsc)) + (sh + bsh),)


def _f_s5_act(ys, u, d):
    return (_gelu_tanh(ys + d * u),)


def _f_s5_gate(y2, t, z):
    return (y2 * jax.nn.sigmoid(t) * _silu(z),)


def _f_res(x, y, gate, bgate):
    return (x + (gate + bgate) * y,)


def _heads(x, width, fn):
    return jnp.concatenate([fn(x[:, i * width:(i + 1) * width]) for i in range(x.shape[1] // width)], axis=1)


def _l2n(x):
    return x * lax.rsqrt(jnp.sum(x * x, axis=-1, keepdims=True) + NORM_EPS)


def _f_betag(ba, alog, dtb):
    col = lax.broadcasted_iota(jnp.int32, ba.shape, 1)
    t = ba + dtb
    softplus = jnp.maximum(t, 0.0) + jnp.log1p(jnp.exp(-jnp.abs(t)))
    g = -jnp.exp(alog) * softplus
    return (jnp.where(col < GDN_HEADS, jax.nn.sigmoid(ba), jnp.where(col < 2 * GDN_HEADS, g, 0.0)),)


def _f_gdn_post(o, z, nw):
    on = _heads(o, GDN_DV, lambda t: t * lax.rsqrt(jnp.mean(t * t, axis=-1, keepdims=True) + NORM_EPS))
    return (on * nw * _silu(z),)


def _f_loss(x, tgt, fw):
    y = x * lax.rsqrt(jnp.mean(x * x, axis=-1, keepdims=True) + NORM_EPS) * fw
    err = y - tgt
    return (0.5 * jnp.mean(err * err, axis=-1, keepdims=True),)


def _ada_mod_call(c_all, ada_w):
    n = ada_w.shape[2]

    def body(c_ref, w_ref, o_ref):
        ca = _silu(c_ref[...])
        for l in range(ada_w.shape[0]):
            o_ref[l] = _bdot(ca, w_ref[l])

    return pl.pallas_call(body, name="ada_mod", out_shape=SDS((ada_w.shape[0], N_DEV, n), F32),
                          compiler_params=_params(None, VMEM_MID))(c_all, ada_w)


def _ada_grad_call(c_all, dmod):
    nl, _, n = dmod.shape

    def body(c_ref, d_ref, o_ref):
        ca = _silu(c_ref[...])
        for l in range(nl):
            o_ref[l] = _hdot(ca, d_ref[l], TN)

    return pl.pallas_call(body, name="ada_grad", out_shape=SDS((nl, c_all.shape[1], n), F32),
                          compiler_params=_params(None, VMEM_MID))(c_all, dmod)


ADAM_ROWS = 512


def _adamw(g, w, m, v):
    m2 = ADAM_B1 * m + (1.0 - ADAM_B1) * g
    v2 = ADAM_B2 * v + (1.0 - ADAM_B2) * (g * g)
    m_hat = m2 / (1.0 - ADAM_B1 ** ADAM_STEP)
    v_hat = v2 / (1.0 - ADAM_B2 ** ADAM_STEP)
    return g, -ADAM_LR * (m_hat / (jnp.sqrt(v_hat) + ADAM_EPS) + ADAM_WD * w), m2, v2


def _adam_call(gs, w, m, v, name, rows=None, by_cols=False):
    n, r, cols = gs.shape
    if by_cols:
        blk = pl.BlockSpec((r, LANES), lambda i: (0, i))
        g_blk, grid = pl.BlockSpec((n, r, LANES), lambda i: (0, 0, i)), (cols // LANES,)
    else:
        rows = rows or ADAM_ROWS
        blk = pl.BlockSpec((rows, cols), lambda i: (i, 0))
        g_blk, grid = pl.BlockSpec((n, rows, cols), lambda i: (0, i, 0)), (r // rows,)

    def body(g_ref, w_ref, m_ref, v_ref, go_ref, d_ref, mo_ref, vo_ref):
        g = g_ref[0].astype(F32)
        for s in range(1, n):
            g = g + g_ref[s].astype(F32)
        for o_ref, val in zip((go_ref, d_ref, mo_ref, vo_ref), _adamw(g, w_ref[...], m_ref[...], v_ref[...])):
            o_ref[...] = val

    return pl.pallas_call(
        body, name=name, grid=grid, in_specs=[g_blk, blk, blk, blk],
        out_specs=[blk, blk, blk, blk], out_shape=[SDS((r, cols), F32)] * 4,
        compiler_params=_params(("parallel",), VMEM_MID),
    )(gs, w, m, v)


def _sum_call(gs, name, rows):
    n, r, _ = gs.shape

    def body(g_ref, o_ref):
        g = g_ref[0].astype(F32)
        for s in range(1, n):
            g = g + g_ref[s].astype(F32)
        o_ref[...] = g

    return pl.pallas_call(
        body, name=name, grid=(r // rows,),
        in_specs=[pl.BlockSpec((n, rows, LANES), lambda i: (0, i, 0))],
        out_specs=pl.BlockSpec((rows, LANES), lambda i: (i, 0)), out_shape=SDS((r, LANES), F32),
        compiler_params=_params(("parallel",), VMEM_MID),
    )(gs)


def _allgather_call(x_shard, name):
    m_per, n = x_shard.shape

    def body(x_ref, out_ref, send_sems, recv_sems, local_sem):
        x, y, c = lax.axis_index("x"), lax.axis_index("y"), lax.axis_index("c")
        me, sibling = (x, y, c), (x, y, 1 - c)
        chips = [(1 - x, y), (x, 1 - y), (1 - x, 1 - y)]

        def rows(px, py, pc):
            return out_ref.at[pl.ds((4 * px + 2 * py + pc) * m_per, m_per), :]

        def copy(k, block, to, src=None):
            return pltpu.make_async_remote_copy(
                src_ref=rows(*block) if src is None else src, dst_ref=rows(*block),
                send_sem=send_sems.at[k], recv_sem=recv_sems.at[k], device_id=to, device_id_type=pl.DeviceIdType.MESH)

        mine = pltpu.make_async_copy(x_ref, rows(*me), local_sem)
        mine.start()
        first = [copy(0, me, sibling, src=x_ref)]
        first += [copy(1 + j, me, (*chip, c), src=x_ref) for j, chip in enumerate(chips)]
        for cp in first:
            cp.start()
        passed = [copy(4 + j, (*chip, c), sibling) for j, chip in enumerate(chips)]
        for j, chip in enumerate(chips):
            copy(1 + j, (*chip, c), me).wait_recv()
            passed[j].start()
        copy(0, sibling, me).wait_recv()
        for j, chip in enumerate(chips):
            copy(4 + j, (*chip, 1 - c), me).wait_recv()
        for cp in first + passed:
            cp.wait_send()
        mine.wait()

    vmem = pl.BlockSpec(memory_space=pltpu.VMEM)
    return pl.pallas_call(
        body, name=name, out_shape=SDS((N_DEV * m_per, n), x_shard.dtype), in_specs=[vmem], out_specs=vmem,
        scratch_shapes=[pltpu.SemaphoreType.DMA((7,)), pltpu.SemaphoreType.DMA((7,)), pltpu.SemaphoreType.DMA],
    )(x_shard)


def _gather_weights_call(shards, name):
    nw = len(shards)

    def body(*refs):
        x_refs, out_refs = refs[:nw], refs[nw:2 * nw]
        send_sems, recv_sems, local_sems = refs[2 * nw:]
        x, y, c = lax.axis_index("x"), lax.axis_index("y"), lax.axis_index("c")
        me, sibling = (x, y, c), (x, y, 1 - c)
        chips = [(1 - x, y), (x, 1 - y), (1 - x, 1 - y)]

        def slot(w, px, py, pc):
            return out_refs[w].at[4 * px + 2 * py + pc]

        def copy(w, k, block, to, src=None):
            dst = slot(w, *block)
            return pltpu.make_async_remote_copy(
                src_ref=dst if src is None else src, dst_ref=dst, send_sem=send_sems.at[7 * w + k],
                recv_sem=recv_sems.at[7 * w + k], device_id=to, device_id_type=pl.DeviceIdType.MESH)

        mines = [pltpu.make_async_copy(x_refs[w], slot(w, *me), local_sems.at[w]) for w in range(nw)]
        for cp in mines:
            cp.start()
        first = [copy(w, 0, me, sibling, src=x_refs[w]) for w in range(nw)]
        first += [copy(w, 1 + j, me, (*chip, c), src=x_refs[w]) for w in range(nw) for j, chip in enumerate(chips)]
        for cp in first:
            cp.start()
        passed = []
        for w in range(nw):
            for j, chip in enumerate(chips):
                copy(w, 1 + j, (*chip, c), me).wait_recv()
                fwd = copy(w, 4 + j, (*chip, c), sibling)
                fwd.start()
                passed.append(fwd)
        for w in range(nw):
            copy(w, 0, sibling, me).wait_recv()
            for j, chip in enumerate(chips):
                copy(w, 4 + j, (*chip, 1 - c), me).wait_recv()
        for cp in first + passed:
            cp.wait_send()
        for cp in mines:
            cp.wait()

    hbm = pl.BlockSpec(memory_space=pl.ANY)
    return pl.pallas_call(
        body, name=name, out_shape=[SDS((N_DEV,) + s.shape, s.dtype) for s in shards],
        in_specs=[hbm] * nw, out_specs=[hbm] * nw,
        scratch_shapes=[pltpu.SemaphoreType.DMA((7 * nw,)), pltpu.SemaphoreType.DMA((7 * nw,)), pltpu.SemaphoreType.DMA((nw,))],
    )(*shards)


_HBM = pl.BlockSpec(memory_space=pltpu.HBM)
_SEM = pl.BlockSpec(memory_space=pltpu.SEMAPHORE)
_DATAFLOW = pltpu.SideEffectType.DATAFLOW_SIDE_EFFECTING


def _spread_start_call(srcs, per_peer, name, after):
    nw = len(srcs)
    lands = [lax.empty((N_DEV,) + (s.shape[1:] if per_peer else s.shape), s.dtype) for s in srcs]

    def body(*refs):
        src_refs, land_refs = refs[:nw], refs[nw:2 * nw]
        send_sems, recv_sems, token = refs[2 * nw + 1], refs[2 * nw + 2], refs[-1]
        x, y, c = lax.axis_index("x"), lax.axis_index("y"), lax.axis_index("c")
        me = 4 * x + 2 * y + c
        for w in range(nw):
            for k in range(1, N_DEV):
                px = 1 - x if k & 4 else x
                py = 1 - y if k & 2 else y
                pc = 1 - c if k & 1 else c
                src = src_refs[w].at[4 * px + 2 * py + pc] if per_peer else src_refs[w]
                pltpu.make_async_remote_copy(
                    src_ref=src, dst_ref=land_refs[w].at[me], send_sem=send_sems.at[w], recv_sem=recv_sems.at[w],
                    device_id=(px, py, pc), device_id_type=pl.DeviceIdType.MESH).start()
        token[...] = jnp.zeros_like(token)

    hbm = lambda a: pltpu.with_memory_space_constraint(a, pltpu.HBM)
    res = pl.pallas_call(
        body, name=name,
        out_shape=(pltpu.SemaphoreType.DMA((nw,)), pltpu.SemaphoreType.DMA((nw,)))
        + tuple(pltpu.HBM(s.shape, s.dtype) for s in srcs) + tuple(pltpu.HBM(l.shape, l.dtype) for l in lands)
        + (SDS((SUBLANES, LANES), F32),),
        in_specs=[_HBM] * (2 * nw) + [pl.BlockSpec(memory_space=pl.ANY)],
        out_specs=(_SEM, _SEM) + (_HBM,) * (2 * nw) + (pl.BlockSpec(memory_space=pltpu.VMEM),),
        input_output_aliases={i: i + 2 for i in range(2 * nw)},
        compiler_params=pltpu.CompilerParams(has_side_effects=_DATAFLOW),
    )(*[hbm(s) for s in srcs], *[hbm(l) for l in lands], after)
    return res[0], res[1], res[2:2 + nw], res[2 + nw:2 + 2 * nw], res[-1]


def _spread_wait_call(send_sems, recv_sems, srcs, lands, after, name):
    nw = len(lands)

    def body(*refs):
        land_refs = refs[nw:2 * nw]
        s_sems, r_sems = refs[2 * nw], refs[2 * nw + 1]
        x, y, c = lax.axis_index("x"), lax.axis_index("y"), lax.axis_index("c")
        for w in range(nw):
            seven = land_refs[w].at[pl.ds(0, N_DEV - 1)]
            all_seven = pltpu.make_async_remote_copy(
                src_ref=seven, dst_ref=seven, send_sem=s_sems.at[w], recv_sem=r_sems.at[w],
                device_id=(x, y, c), device_id_type=pl.DeviceIdType.MESH)
            all_seven.wait_send()
            all_seven.wait_recv()

    res = pl.pallas_call(
        body, name=name,
        out_shape=tuple(pltpu.HBM(s.shape, s.dtype) for s in srcs) + tuple(pltpu.HBM(l.shape, l.dtype) for l in lands),
        in_specs=[_HBM] * (2 * nw) + [_SEM, _SEM, pl.BlockSpec(memory_space=pl.ANY)], out_specs=(_HBM,) * (2 * nw),
        input_output_aliases={i: i for i in range(2 * nw)},
        compiler_params=pltpu.CompilerParams(has_side_effects=_DATAFLOW),
    )(*srcs, *lands, send_sems, recv_sems, after)
    return res[:nw], res[nw:]


def _join_cols_call(w8, name):
    _, k, n = w8.shape
    tk = _tile(k, 256)

    def body(w_ref, o_ref):
        for s in range(N_DEV):
            o_ref[:, n * s:n * (s + 1)] = w_ref[s]

    return pl.pallas_call(body, name=name, grid=(k // tk,), in_specs=[pl.BlockSpec((N_DEV, tk, n), lambda i: (0, i, 0))],
                          out_specs=pl.BlockSpec((tk, N_DEV * n), lambda i: (i, 0)), out_shape=SDS((k, N_DEV * n), w8.dtype),
                          compiler_params=_params(("parallel",), VMEM_MID))(w8)


def _split_cols_call(g, name, dtype):
    k, n8 = g.shape
    n = n8 // N_DEV
    tk = _tile(k, 256)

    def body(g_ref, o_ref):
        for s in range(N_DEV):
            o_ref[s] = g_ref[:, n * s:n * (s + 1)].astype(dtype)

    return pl.pallas_call(body, name=name, grid=(k // tk,), in_specs=[pl.BlockSpec((tk, n8), lambda i: (i, 0))],
                          out_specs=pl.BlockSpec((N_DEV, tk, n), lambda i: (0, i, 0)), out_shape=SDS((N_DEV, k, n), dtype),
                          compiler_params=_params(("parallel",), VMEM_MID))(g)


def _pack(parts, rows_multiple):
    flat = jnp.concatenate([p.reshape(-1) for p in parts])
    unit = rows_multiple * LANES
    padded = -(-flat.shape[0] // unit) * unit
    flat = jnp.concatenate([flat, jnp.zeros((padded - flat.shape[0],), F32)])
    return flat.reshape(-1, LANES)


def _groups_last(a):
    x, y = a.shape[-2:]
    return jnp.transpose(a.reshape(S5_GROUPS, x, y), (1, 2, 0)).reshape(x * y, S5_GROUPS)


def _groups_first(a, shape):
    x, y = shape[-2:]
    return jnp.transpose(a.reshape(x, y, S5_GROUPS), (2, 0, 1)).reshape(shape)


def _unpack(buf, shapes):
    flat = buf.reshape(-1)
    out, off = [], 0
    for s in shapes:
        n = math.prod(s)
        out.append(flat[off:off + n].reshape(s))
        off += n
    return out


def _row_tile(L):
    return 256 if L % 256 == 0 else L


def _layer0_mix(diff, const):
    x, mod, norm_w, lam_re, lam_im, log_dt, b_re, b_im, c_re, c_im, s5_d, *slots = diff
    ada_b, weights = const
    L = x.shape[0]
    tm = _row_tile(L)
    mods = mod.reshape(2, 1, D_MODEL)
    biases = ada_b.reshape(2, 1, D_MODEL)
    op_ln0 = make_rowwise(_f_lnmod, "ln0", tm, 1, 5, pass_first=True)
    h, x = op_ln0((x,), (norm_w.reshape(1, D_MODEL), mods[1], mods[0], biases[1], biases[0]))
    u, z = make_proj("s5_in")(h, tuple(weights), tuple(slots))
    blocks = _s5_block_params(lam_re, lam_im, log_dt, b_re, b_im, c_re, c_im)
    y2 = make_s5_core(min(S5_TL, L))(u, *blocks, s5_d.reshape(1, D_INNER))
    return x, y2, z


def _layer0_out(diff, weights):
    y2, z, *slots = diff
    tm = _row_tile(y2.shape[0])
    t, y2 = make_mm("s5_glu", pass_input=True)(y2, weights[0], slots[0])
    (y4,) = make_rowwise(_f_s5_gate, "s5_gate", tm, 3, 0)((y2, t, z), ())
    return make_mm("s5_out")(y4, weights[1], slots[1])


def _f_res_lnmod(x, o, gate, bgate, nw, sc, sh, bsc, bsh):
    (x1,) = _f_res(x, o, gate, bgate)
    return _f_lnmod(x1, nw, sc, sh, bsc, bsh) + (x1,)


def _f_res_loss(x, y, tgt, gate, bgate, fw):
    return _f_loss(_f_res(x, y, gate, bgate)[0], tgt, fw)


def _layer1_loss(diff, const):
    x, o, gate0, mod, norm_w, conv_w, a_log, dt_bias, gdn_nw, final_nw, *slots = diff
    tgt, bgate0, ada_b, weights = const
    L = x.shape[0]
    tm = _row_tile(L)
    mods = mod.reshape(3, 1, D_MODEL)
    biases = ada_b.reshape(3, 1, D_MODEL)
    h, x1 = make_rowwise(_f_res_lnmod, "res0_ln1", tm, 2, 7)(
        (x, o), (gate0.reshape(1, D_MODEL), bgate0.reshape(1, D_MODEL), norm_w.reshape(1, D_MODEL), mods[1], mods[0], biases[1], biases[0]))
    q0, k0, v0, gz, ba = make_proj("gdn_in", w_rows=True)(h, tuple(weights[0:5]), tuple(slots[0:5]))
    cw = jnp.concatenate([conv_w, jnp.zeros((SUBLANES - GDN_CONV, GDN_CONV_CH), F32)], axis=0)
    q = make_conv_act(lambda t: _l2n(_silu(t)) * (GDN_DK ** -0.5), "gdn_conv_q")(q0, cw[:, :GDN_QK])
    k = make_conv_act(lambda t: _l2n(_silu(t)), "gdn_conv_k")(k0, cw[:, GDN_QK:2 * GDN_QK])
    v = make_conv_act(_silu, "gdn_conv_v")(v0, cw[:, 2 * GDN_QK:])
    pad = jnp.zeros((LANES - 2 * GDN_HEADS,), F32)
    alog_row = jnp.concatenate([jnp.zeros((GDN_HEADS,), F32), a_log, pad]).reshape(1, LANES)
    dtb_row = jnp.concatenate([jnp.zeros((GDN_HEADS,), F32), dt_bias, pad]).reshape(1, LANES)
    (bg,) = make_rowwise(_f_betag, "gdn_bg", tm, 1, 2)((ba,), (alog_row, dtb_row))
    nw_row = jnp.tile(gdn_nw, GDN_HEADS).reshape(1, D_INNER)
    on = gdn_scan(*gdn_prep(q, k, v, bg), gz, nw_row)
    y = make_mm("gdn_out")(on, weights[5], slots[5])
    (lt,) = make_rowwise(_f_res_loss, "res1_loss", tm, 3, 3)((x1, y, tgt), (mods[2], biases[2], final_nw.reshape(1, D_MODEL)))
    return jnp.sum(lt)


VEC_NAMES = ("ada_b", "norm_w", "s5_lambda_re", "s5_lambda_im", "s5_log_dt", "s5_d", "gdn_a_log", "gdn_dt_bias", "final_norm_w")
MAT_NAMES = ("s5_b_re", "s5_b_im", "s5_c_re", "s5_c_im")
S5_BIG = ("s5_w_in", "s5_w_glu", "s5_w_out")
GDN_BIG = ("gdn_w_in", "gdn_w_out")
BIG_NAMES = S5_BIG + GDN_BIG
WEIGHT_ORDER = ("ada_w", "ada_b", "norm_w", "s5_w_in", "s5_lambda_re", "s5_lambda_im", "s5_log_dt", "s5_b_re", "s5_b_im",
                "s5_c_re", "s5_c_im", "s5_d", "s5_w_glu", "s5_w_out", "gdn_w_in", "gdn_conv_w", "gdn_a_log", "gdn_dt_bias",
                "gdn_norm_w", "gdn_w_out", "final_norm_w")


def _step(x, c, W, M, V, tgt):
    L = x.shape[1]
    ix, iy, ic = lax.axis_index("x"), lax.axis_index("y"), lax.axis_index("c")
    me = 4 * ix + 2 * iy + ic
    n_ada = W["ada_w"].shape[2]
    n_conv = W["gdn_conv_w"].shape[2]
    n_gnw = W["gdn_norm_w"].shape[1]

    g1 = _allgather_call(_pack([c, W["gdn_conv_w"], W["gdn_norm_w"]], SUBLANES), "gather_small_in")
    g1 = g1.reshape(N_DEV, -1)
    c_all = g1[:, :D_MODEL]
    conv_w = g1[:, D_MODEL:D_MODEL + GDN_CONV * n_conv].reshape(N_DEV, GDN_CONV, n_conv).transpose(1, 0, 2).reshape(GDN_CONV, -1)
    gdn_nw = g1[:, D_MODEL + GDN_CONV * n_conv:D_MODEL + GDN_CONV * n_conv + n_gnw].reshape(-1)
    mod_part = _ada_mod_call(c_all, W["ada_w"])
    g2 = _allgather_call(_pack([mod_part], SUBLANES), "gather_mod").reshape(N_DEV, -1)
    mod_all = g2[:, :2 * N_DEV * n_ada].reshape(N_DEV, 2, N_DEV, n_ada)
    mod_raw = lax.dynamic_index_in_dim(mod_all, me, axis=2, keepdims=False)
    mod_raw = mod_raw.transpose(1, 0, 2).reshape(2, 3 * D_MODEL)

    shard = lambda n: W[n][0].astype(BF16)
    (w_in5_parts,) = _gather_weights_call([shard("s5_w_in")], "gather_s5_w_in")
    late = _spread_start_call([shard("s5_w_glu"), shard("s5_w_out")], False, "gather_s5_late_start", w_in5_parts)
    turned = lambda a: jnp.transpose(a[0])
    g_send, g_recv, g_srcs, g_lands, g_token = _spread_start_call(
        [turned(W["gdn_w_in"]).astype(BF16), shard("gdn_w_out")], False, "gather_gdn_start", late[4])
    w_in5 = _join_cols_call(w_in5_parts, "join_s5_w_in")
    slot = lambda *s: jnp.zeros(s, F32)
    two = 2 * D_MODEL
    diff_mix = (x[0], mod_raw[0, :two] + g_token[0, 0], W["norm_w"][0], W["s5_lambda_re"][0], W["s5_lambda_im"][0], W["s5_log_dt"][0],
                W["s5_b_re"][0], W["s5_b_im"][0], W["s5_c_re"][0], W["s5_c_im"][0], W["s5_d"][0],
                slot(D_MODEL, D_INNER), slot(D_MODEL, D_INNER))

    (xp, y2, z5), vjp_mix = jax.vjp(lambda d: _layer0_mix(d, (W["ada_b"][0, :two], (w_in5[:, :D_INNER], w_in5[:, D_INNER:]))), diff_mix)
    l_srcs, l_lands = _spread_wait_call(late[0], late[1], late[2], late[3], y2, "gather_s5_late_wait")
    w_glu, w_o5 = [lax.dynamic_update_slice(land, src[None], (me, 0, 0)).reshape(-1, src.shape[1]) for land, src in zip(l_lands, l_srcs)]
    diff_out = (y2, z5, slot(D_INNER, D_INNER), slot(D_INNER, D_MODEL))
    o5, vjp_out = jax.vjp(lambda d: _layer0_out(d, (w_glu, w_o5)), diff_out)
    g_srcs, g_lands = _spread_wait_call(g_send, g_recv, g_srcs, g_lands, o5, "gather_gdn_wait")
    gdn_full = [lax.dynamic_update_slice(land, src[None], (me, 0, 0)) for land, src in zip(g_lands, g_srcs)]
    w_ing = gdn_full[0].reshape(GDN_PROJ, D_MODEL)
    w_ba = jnp.concatenate([w_ing[GDN_CONV_CH + D_INNER:], jnp.zeros((LANES - 2 * GDN_HEADS, D_MODEL), BF16)], axis=0)
    weights1 = (w_ing[:GDN_QK], w_ing[GDN_QK:2 * GDN_QK], w_ing[2 * GDN_QK:GDN_CONV_CH],
                w_ing[GDN_CONV_CH:GDN_CONV_CH + D_INNER], w_ba, gdn_full[1].reshape(D_INNER, D_MODEL))
    slots1 = tuple(jnp.zeros(w.shape, F32) for w in weights1)
    diff1 = (xp, o5, mod_raw[0, two:], mod_raw[1], W["norm_w"][1], conv_w, W["gdn_a_log"][0], W["gdn_dt_bias"][0], gdn_nw,
             W["final_norm_w"], *slots1)
    loss_local, vjp1 = jax.vjp(lambda d: _layer1_loss(d, (tgt[0], W["ada_b"][0, two:], W["ada_b"][1], weights1)), diff1)
    ((dxp, do5, dmod_gate, dmod1, d_norm_w1, d_conv, d_alog, d_dtb, d_gnw, d_fnw, d_wq, d_wk, d_wv, d_wgz, d_wba, d_wog),) = vjp1(
        jnp.ones((), F32))
    loss = lax.psum(loss_local, MESH_AXES)

    rows = lambda d: d.reshape(N_DEV, d.shape[0] // N_DEV, d.shape[1])
    d_ing = jnp.concatenate([d_wq, d_wk, d_wv, d_wgz, d_wba[:2 * GDN_HEADS]], axis=0).astype(BF16).reshape(N_DEV, -1, D_MODEL)
    s_send, s_recv, s_srcs, s_lands, s_token = _spread_start_call([d_ing, rows(d_wog).astype(BF16)], True, "scatter_gdn_start", dxp)
    ((dy2, dz5, d_wglu, d_wo5),) = vjp_out(do5.at[0, 0].add(s_token[0, 0]))
    t_send, t_recv, t_srcs, t_lands, t_token = _spread_start_call(
        [rows(d_wglu).astype(BF16), rows(d_wo5).astype(BF16)], True, "scatter_s5_late_start", dy2)
    ((dx, dmod_ss, d_norm_w0, d_lre, d_lim, d_logdt, d_bre, d_bim, d_cre, d_cim, d_s5d, d_wu, d_wz),) = vjp_mix(
        (dxp.at[0, 0].add(t_token[0, 0]), dy2, dz5))
    dmod = jnp.stack([jnp.concatenate([dmod_ss, dmod_gate]), dmod1])
    d_norm_w = jnp.stack([d_norm_w0, d_norm_w1])
    vec_parts = [dmod, d_norm_w, d_lre, d_lim, d_logdt, d_s5d, d_alog, d_dtb, d_fnw]
    tail_parts = [d_conv, d_gnw]
    mat_parts = [_groups_last(d) for d in (d_bre, d_bim, d_cre, d_cim)]
    n_vec = sum(math.prod(p.shape) for p in vec_parts)
    m_send, m_recv, m_srcs, m_lands, m_token = _spread_start_call(
        [_pack(vec_parts + tail_parts, ADAM_ROWS), _pack(mat_parts, SUBLANES).astype(BF16)], False, "gather_small_grads_start", dx)
    d_in5 = _split_cols_call(jnp.concatenate([d_wu.at[0, 0].add(m_token[0, 0]), d_wz], axis=1), "split_s5_w_in", BF16)
    u_send, u_recv, u_srcs, u_lands, u_token = _spread_start_call([d_in5], True, "scatter_s5_in_start", m_token)
    t_srcs, t_lands = _spread_wait_call(t_send, t_recv, t_srcs, t_lands, u_token, "scatter_s5_late_wait")
    s_srcs, s_lands = _spread_wait_call(s_send, s_recv, s_srcs, s_lands, t_lands[0], "scatter_gdn_wait")
    big = {}

    def owner_update(land, src, n):
        mine = lax.dynamic_index_in_dim(src, me, 0, keepdims=True)
        parts = lax.dynamic_update_slice(land, mine, (me, 0, 0))
        if n == "gdn_w_in":
            outs = _adam_call(parts, turned(W[n]), turned(M[n]), turned(V[n]), "adam_" + n, by_cols=True)
            return [jnp.transpose(o) for o in outs]
        return _adam_call(parts, W[n][0], M[n][0], V[n][0], "adam_" + n, rows=_tile(W[n].shape[1], 128))

    for land, src, n in zip(tuple(t_lands) + tuple(s_lands), tuple(t_srcs) + tuple(s_srcs), ("s5_w_glu", "s5_w_out") + GDN_BIG):
        big[n] = owner_update(land, src, n)

    m_srcs, m_lands = _spread_wait_call(m_send, m_recv, m_srcs, m_lands, big["gdn_w_out"][0], "gather_small_grads_wait")
    sg_vec, sg_mat = [lax.dynamic_update_slice(land, src[None], (me, 0, 0)) for land, src in zip(m_lands, m_srcs)]
    tot_vec = _sum_call(sg_vec, "sum_vec_grads", ADAM_ROWS)
    tot_mat = _sum_call(sg_mat, "sum_mat_grads", ADAM_ROWS)
    g_conv, g_gnw = _unpack(tot_vec.reshape(-1)[n_vec:], [d_conv.shape, d_gnw.shape])
    g_conv_mine = lax.dynamic_slice_in_dim(g_conv, me * n_conv, n_conv, axis=1)
    g_gnw_mine = lax.dynamic_slice_in_dim(g_gnw, me * n_gnw, n_gnw, axis=0)
    vec_names = VEC_NAMES + ("gdn_conv_w", "gdn_norm_w")
    vec_g = _pack([tot_vec.reshape(-1)[:n_vec], g_conv_mine, g_gnw_mine], ADAM_ROWS)
    vec = _adam_call(vec_g[None], _pack([W[n] for n in vec_names], ADAM_ROWS), _pack([M[n] for n in vec_names], ADAM_ROWS),
                     _pack([V[n] for n in vec_names], ADAM_ROWS), "adam_vec")
    vec = [_unpack(b, [W[n].shape for n in vec_names]) for b in vec]
    mats = []
    for name, g_mat in zip(MAT_NAMES, _unpack(tot_mat, [p.shape for p in mat_parts])):
        outs = _adam_call(g_mat[None], _groups_last(W[name]), _groups_last(M[name]), _groups_last(V[name]), "adam_" + name)
        mats.append([_groups_first(o, W[name].shape) for o in outs])

    dmod_all = sg_vec[:, :2 * 3 * D_MODEL // LANES].reshape(N_DEV, 2, N_DEV, n_ada // LANES, LANES)
    dmod_mine = lax.dynamic_index_in_dim(dmod_all, me, axis=2, keepdims=False).transpose(1, 0, 2, 3).reshape(2, N_DEV, n_ada)
    g_ada_w = _ada_grad_call(c_all, dmod_mine)
    ada = _adam_call(g_ada_w.reshape(1, -1, LANES), W["ada_w"].reshape(-1, LANES), M["ada_w"].reshape(-1, LANES),
                     V["ada_w"].reshape(-1, LANES), "adam_ada")
    u_srcs, u_lands = _spread_wait_call(u_send, u_recv, u_srcs, u_lands, ada[0], "scatter_s5_in_wait")
    big["s5_w_in"] = owner_update(u_lands[0], u_srcs[0], "s5_w_in")
    ada = [a.reshape(W["ada_w"].shape) for a in ada]

    res = {}
    for n in BIG_NAMES:
        res[n] = [o[None] for o in big[n]]
    for i, n in enumerate(vec_names):
        res[n] = [b[i] for b in vec]
    for i, n in enumerate(MAT_NAMES):
        res[n] = mats[i]
    res["ada_w"] = ada
    outs = [loss, dx[None]]
    for j in range(4):
        outs += [res[n][j] for n in WEIGHT_ORDER]
    return tuple(outs)


def kernel(x, c, ada_w, ada_b, norm_w, s5_w_in, s5_lambda_re, s5_lambda_im, s5_log_dt, s5_b_re, s5_b_im, s5_c_re, s5_c_im, s5_d, s5_w_glu, s5_w_out, gdn_w_in, gdn_conv_w, gdn_a_log, gdn_dt_bias, gdn_norm_w, gdn_w_out, final_norm_w, loss_target, m_ada_w, m_ada_b, m_norm_w, m_s5_w_in, m_s5_lambda_re, m_s5_lambda_im, m_s5_log_dt, m_s5_b_re, m_s5_b_im, m_s5_c_re, m_s5_c_im, m_s5_d, m_s5_w_glu, m_s5_w_out, m_gdn_w_in, m_gdn_conv_w, m_gdn_a_log, m_gdn_dt_bias, m_gdn_norm_w, m_gdn_w_out, m_final_norm_w, v_ada_w, v_ada_b, v_norm_w, v_s5_w_in, v_s5_lambda_re, v_s5_lambda_im, v_s5_log_dt, v_s5_b_re, v_s5_b_im, v_s5_c_re, v_s5_c_im, v_s5_d, v_s5_w_glu, v_s5_w_out, v_gdn_w_in, v_gdn_conv_w, v_gdn_a_log, v_gdn_dt_bias, v_gdn_norm_w, v_gdn_w_out, v_final_norm_w):
    W = dict(ada_w=ada_w, ada_b=ada_b, norm_w=norm_w, s5_w_in=s5_w_in, s5_lambda_re=s5_lambda_re, s5_lambda_im=s5_lambda_im,
             s5_log_dt=s5_log_dt, s5_b_re=s5_b_re, s5_b_im=s5_b_im, s5_c_re=s5_c_re, s5_c_im=s5_c_im, s5_d=s5_d,
             s5_w_glu=s5_w_glu, s5_w_out=s5_w_out, gdn_w_in=gdn_w_in, gdn_conv_w=gdn_conv_w, gdn_a_log=gdn_a_log,
             gdn_dt_bias=gdn_dt_bias, gdn_norm_w=gdn_norm_w, gdn_w_out=gdn_w_out, final_norm_w=final_norm_w)
    M = dict(ada_w=m_ada_w, ada_b=m_ada_b, norm_w=m_norm_w, s5_w_in=m_s5_w_in, s5_lambda_re=m_s5_lambda_re,
             s5_lambda_im=m_s5_lambda_im, s5_log_dt=m_s5_log_dt, s5_b_re=m_s5_b_re, s5_b_im=m_s5_b_im, s5_c_re=m_s5_c_re,
             s5_c_im=m_s5_c_im, s5_d=m_s5_d, s5_w_glu=m_s5_w_glu, s5_w_out=m_s5_w_out, gdn_w_in=m_gdn_w_in,
             gdn_conv_w=m_gdn_conv_w, gdn_a_log=m_gdn_a_log, gdn_dt_bias=m_gdn_dt_bias, gdn_norm_w=m_gdn_norm_w,
             gdn_w_out=m_gdn_w_out, final_norm_w=m_final_norm_w)
    V = dict(ada_w=v_ada_w, ada_b=v_ada_b, norm_w=v_norm_w, s5_w_in=v_s5_w_in, s5_lambda_re=v_s5_lambda_re,
             s5_lambda_im=v_s5_lambda_im, s5_log_dt=v_s5_log_dt, s5_b_re=v_s5_b_re, s5_b_im=v_s5_b_im, s5_c_re=v_s5_c_re,
             s5_c_im=v_s5_c_im, s5_d=v_s5_d, s5_w_glu=v_s5_w_glu, s5_w_out=v_s5_w_out, gdn_w_in=v_gdn_w_in,
             gdn_conv_w=v_gdn_conv_w, gdn_a_log=v_gdn_a_log, gdn_dt_bias=v_gdn_dt_bias, gdn_norm_w=v_gdn_norm_w,
             gdn_w_out=v_gdn_w_out, final_norm_w=v_final_norm_w)
    return _step(x, c, W, M, V, loss_target)
```

```python
import functools
import math

import jax
import jax.numpy as jnp
from jax import lax
from jax.experimental import pallas as pl
from jax.experimental.pallas import tpu as pltpu

F32 = jnp.float32
BF16 = jnp.bfloat16
SDS = jax.ShapeDtypeStruct

D_MODEL = 1024
D_INNER = 2048
NORM_EPS = 1e-6
S5_GROUP = 16
S5_GROUPS = 128
S5_STATE = 64
GDN_HEADS = 8
GDN_DK = 128
GDN_DV = 256
GDN_CONV = 4
GDN_CHUNK = 64
GDN_QK = 1024
GDN_CONV_CH = 4096
GDN_PROJ = 6160
ADAM_LR = 0.001
ADAM_B1 = 0.9
ADAM_B2 = 0.999
ADAM_EPS = 1e-08
ADAM_WD = 0.01
ADAM_STEP = 10

N_DEV = 8
LANES = 128
SUBLANES = 8
VMEM_BIG = 56 << 20
VMEM_MID = 40 << 20
S5_GB = 8
S5_BW = S5_GB * S5_GROUP
S5_TL = 2048
MESH_AXES = ("x", "y", "c")


def _params(sem, vmem=None):
    return pltpu.CompilerParams(dimension_semantics=sem, vmem_limit_bytes=vmem)


def _bdot(a, b, dims=(((1,), (0,)), ((), ()))):
    return lax.dot_general(a.astype(BF16), b.astype(BF16), dims, preferred_element_type=F32)


def _hdot(a, b, dims=(((1,), (0,)), ((), ()))):
    return lax.dot_general(a, b, dims, preferred_element_type=F32, precision=lax.Precision.HIGHEST)


_BNN = (((2,), (1,)), ((0,), (0,)))
_BNT = (((2,), (2,)), ((0,), (0,)))
_BTN = (((1,), (1,)), ((0,), (0,)))


@jax.custom_vjp
def _unit_lower_inverse(a):
    c = a.shape[-1]
    ri = lax.broadcasted_iota(jnp.int32, a.shape, 1)
    ci = lax.broadcasted_iota(jnp.int32, a.shape, 2)
    n = -a
    t = (ri == ci).astype(F32) + n
    for _ in range(int(math.log2(c)) - 1):
        n = _hdot(n, n, _BNN)
        t = t + _hdot(t, n, _BNN)
    return t


def _unit_lower_inverse_fwd(a):
    t = _unit_lower_inverse(a)
    return t, t


def _unit_lower_inverse_bwd(t, g):
    return (-_hdot(_hdot(t, g, _BTN), t, _BNT),)


_unit_lower_inverse.defvjp(_unit_lower_inverse_fwd, _unit_lower_inverse_bwd)


NN = (((1,), (0,)), ((), ()))
NT = (((1,), (1,)), ((), ()))
TN = (((0,), (0,)), ((), ()))


def _tile(n, pref):
    for t in (pref, 512, 256, 128):
        if t <= n and n % t == 0:
            return t
    return n


def _matmul(a, b, mode, name, add=None):
    if mode == "nn":
        (m, k), (_, n) = a.shape, b.shape
    elif mode == "nt":
        (m, k), (n, _) = a.shape, b.shape
    else:
        (k, m), (_, n) = a.shape, b.shape
    tm, tn, tk = _tile(m, 1024), _tile(n, 512), (k if k <= 2048 else _tile(k, 512))
    if mode == "tn":
        tm, tn, tk = _tile(m, 1024), _tile(n, 1024), _tile(k, 1024)
    nk = k // tk
    dims = {"nn": NN, "nt": NT, "tn": TN}[mode]

    def body(a_ref, b_ref, *rest):
        o_ref, acc_ref = rest[-2], rest[-1]
        part = _bdot(a_ref[...], b_ref[...], dims)
        if nk == 1:
            o_ref[...] = part if add is None else part + rest[0][...]
            return
        kk = pl.program_id(2)

        @pl.when(kk == 0)
        def _():
            acc_ref[...] = part if add is None else part + rest[0][...]

        @pl.when(kk > 0)
        def _():
            acc_ref[...] += part

        @pl.when(kk == nk - 1)
        def _():
            o_ref[...] = acc_ref[...]

    a_spec = pl.BlockSpec((tk, tm), lambda i, j, q: (q, i)) if mode == "tn" else pl.BlockSpec((tm, tk), lambda i, j, q: (i, q))
    b_spec = pl.BlockSpec((tn, tk), lambda i, j, q: (j, q)) if mode == "nt" else pl.BlockSpec((tk, tn), lambda i, j, q: (q, j))
    o_spec = pl.BlockSpec((tm, tn), lambda i, j, q: (i, j))
    return pl.pallas_call(
        body, name=name, grid=(m // tm, n // tn, nk),
        in_specs=[a_spec, b_spec] + ([] if add is None else [o_spec]), out_specs=o_spec,
        out_shape=SDS((m, n), F32), scratch_shapes=[pltpu.VMEM((tm, tn), F32)],
        compiler_params=_params(("parallel", "parallel", "arbitrary"), VMEM_MID),
    )(a, b, *([] if add is None else [add]))


def make_mm(name, pass_input=False):
    def primal(a, w):
        out = _matmul(a, w, "nn", name + "_fwd")
        return (out, a) if pass_input else out

    @jax.custom_vjp
    def mm(a, w, grad_slot):
        return primal(a, w)

    def fwd(a, w, grad_slot):
        return primal(a, w), (a, w)

    def bwd(res, g):
        a, w = res
        g, g_other = g if pass_input else (g, None)
        return _matmul(g, w, "nt", name + "_dx", add=g_other), jnp.zeros_like(w), _matmul(a, g, "tn", name + "_dw")

    mm.defvjp(fwd, bwd)
    return mm


PROJ_ROWS = 256


def _proj_fwd_call(a, ws, name, w_rows):
    m, k = a.shape
    tm = _tile(m, PROJ_ROWS)
    nw = len(ws)
    widths = [w.shape[0] if w_rows else w.shape[1] for w in ws]

    def body(*refs):
        ab = refs[0][...].astype(BF16)
        for w_ref, o_ref in zip(refs[1:1 + nw], refs[1 + nw:]):
            o_ref[...] = lax.dot_general(ab, w_ref[...], NT if w_rows else NN, preferred_element_type=F32)

    return pl.pallas_call(
        body, name=name, grid=(m // tm,),
        in_specs=[pl.BlockSpec((tm, k), lambda i: (i, 0))] + [pl.BlockSpec(w.shape, lambda i: (0, 0)) for w in ws],
        out_specs=[pl.BlockSpec((tm, n), lambda i: (i, 0)) for n in widths],
        out_shape=[SDS((m, n), F32) for n in widths],
        compiler_params=_params(("parallel",), VMEM_BIG),
    )(a, *ws)


def _proj_dx_call(gs, ws, name, w_rows):
    m = gs[0].shape[0]
    k = ws[0].shape[1] if w_rows else ws[0].shape[0]
    tm = _tile(m, PROJ_ROWS)
    nw = len(ws)

    def body(*refs):
        acc = None
        for g_ref, w_ref in zip(refs[:nw], refs[nw:2 * nw]):
            part = _bdot(g_ref[...], w_ref[...], NN if w_rows else NT)
            acc = part if acc is None else acc + part
        refs[2 * nw][...] = acc

    return pl.pallas_call(
        body, name=name, grid=(m // tm,),
        in_specs=[pl.BlockSpec((tm, g.shape[1]), lambda i: (i, 0)) for g in gs] + [pl.BlockSpec(w.shape, lambda i: (0, 0)) for w in ws],
        out_specs=pl.BlockSpec((tm, k), lambda i: (i, 0)), out_shape=SDS((m, k), F32),
        compiler_params=_params(("parallel",), VMEM_BIG),
    )(*gs, *ws)


def make_proj(name, w_rows=False):
    @jax.custom_vjp
    def proj(a, ws, grad_slots):
        return tuple(_proj_fwd_call(a, ws, name + "_fwd", w_rows))

    def fwd(a, ws, grad_slots):
        return tuple(_proj_fwd_call(a, ws, name + "_fwd", w_rows)), (a, ws)

    def bwd(res, gs):
        a, ws = res
        dws = tuple(_matmul(g, a, "tn", "%s_dw%d" % (name, i)) if w_rows else _matmul(a, g, "tn", "%s_dw%d" % (name, i))
                    for i, g in enumerate(gs))
        return _proj_dx_call(tuple(gs), ws, name + "_dx", w_rows), tuple(jnp.zeros_like(w) for w in ws), dws

    proj.defvjp(fwd, bwd)
    return proj


def make_rowwise(f, name, tm, n_rows, n_params, vmem=VMEM_MID, pass_first=False):
    def specs_of(arrs, blocked):
        if blocked:
            return [pl.BlockSpec((tm, a.shape[1]), lambda i: (i, 0)) for a in arrs]
        return [pl.BlockSpec(a.shape, lambda i: (0, 0)) for a in arrs]

    def out_structs(rows, params):
        blk = [SDS((tm, r.shape[1]), r.dtype) for r in rows] + [SDS(p.shape, p.dtype) for p in params]
        return jax.eval_shape(f, *blk)

    def run_fwd(rows, params):
        L = rows[0].shape[0]
        outs = out_structs(rows, params)

        def body(*refs):
            ins = [r[...] for r in refs[:n_rows + n_params]]
            res = f(*ins)
            for o_ref, val in zip(refs[n_rows + n_params:], res):
                o_ref[...] = val

        return pl.pallas_call(
            body, name=name + "_fwd", grid=(L // tm,),
            in_specs=specs_of(rows, True) + specs_of(params, False),
            out_specs=[pl.BlockSpec((tm, o.shape[1]), lambda i: (i, 0)) for o in outs],
            out_shape=[SDS((L, o.shape[1]), o.dtype) for o in outs],
            compiler_params=_params(("parallel",), vmem),
        )(*rows, *params)

    def run_bwd(rows, params, gs):
        L = rows[0].shape[0]
        n_g = len(gs)

        def body(*refs):
            i = pl.program_id(0)
            ins = [r[...] for r in refs[:n_rows + n_params]]
            cts = tuple(r[...] for r in refs[n_rows + n_params:n_rows + n_params + n_g])
            outs = refs[n_rows + n_params + n_g:]
            _, vjp = jax.vjp(f, *ins)
            grads = vjp(cts[:-1] if pass_first else cts)
            if pass_first:
                grads = (grads[0] + cts[-1],) + tuple(grads[1:])
            for o_ref, val in zip(outs[:n_rows], grads[:n_rows]):
                o_ref[...] = val

            if n_params:
                @pl.when(i == 0)
                def _():
                    for o_ref in outs[n_rows:]:
                        o_ref[...] = jnp.zeros_like(o_ref)
                for o_ref, val in zip(outs[n_rows:], grads[n_rows:]):
                    o_ref[...] += val

        res = pl.pallas_call(
            body, name=name + "_bwd", grid=(L // tm,),
            in_specs=specs_of(rows, True) + specs_of(params, False) + specs_of(gs, True),
            out_specs=specs_of(rows, True) + specs_of(params, False),
            out_shape=[SDS(r.shape, r.dtype) for r in rows] + [SDS(p.shape, p.dtype) for p in params],
            compiler_params=_params(("arbitrary",), vmem),
        )(*rows, *params, *gs)
        return tuple(res[:n_rows]), tuple(res[n_rows:])

    def outputs(rows, params):
        outs = tuple(run_fwd(rows, params))
        return outs + (rows[0],) if pass_first else outs

    @jax.custom_vjp
    def op(rows, params):
        return outputs(rows, params)

    def fwd(rows, params):
        return outputs(rows, params), (rows, params)

    def bwd(res, gs):
        rows, params = res
        return run_bwd(rows, params, tuple(gs))

    op.defvjp(fwd, bwd)
    op.run_fwd, op.run_bwd = run_fwd, run_bwd
    return op


def _s5_scan_rows(xr_ref, xi_ref, ar, ai, x0r, x0i, tl, reverse=False):
    n = xr_ref.shape[1]
    T = SUBLANES
    row = lax.broadcasted_iota(jnp.int32, (T, n), 0)
    pr, pi = [ar], [ai]
    for _ in range(T - 1):
        pr, pi = pr + [pr[-1] * ar - pi[-1] * ai], pi + [pr[-1] * ai + pi[-1] * ar]
    levels = []
    for d in (1, 2, 4):
        mask = (row < T - d) if reverse else (row >= d)
        levels.append((T - d if reverse else d, jnp.where(mask, pr[d - 1], 0.0), jnp.where(mask, pi[d - 1], 0.0)))
    cr = jnp.zeros((T, n), F32)
    ci = jnp.zeros((T, n), F32)
    for r in range(T):
        k = (T - r) if reverse else (r + 1)
        cr = jnp.where(row == r, pr[k - 1], cr)
        ci = jnp.where(row == r, pi[k - 1], ci)
    nt = tl // T
    last = 0 if reverse else T - 1

    def step(t, carry):
        sr, si = carry
        base = pl.multiple_of((nt - 1 - t if reverse else t) * T, T)
        br = xr_ref[pl.ds(base, T), :]
        bi = xi_ref[pl.ds(base, T), :]
        for shift, mr, mi in levels:
            qr = pltpu.roll(br, shift, 0)
            qi = pltpu.roll(bi, shift, 0)
            br, bi = br + (mr * qr - mi * qi), bi + (mr * qi + mi * qr)
        xr = br + (cr * sr - ci * si)
        xi = bi + (cr * si + ci * sr)
        xr_ref[pl.ds(base, T), :] = xr
        xi_ref[pl.ds(base, T), :] = xi
        return xr[last:last + 1, :], xi[last:last + 1, :]
    return lax.fori_loop(0, nt, step, (x0r, x0i))


def _s5_fwd_call(u, bre, bim, cre, cim, a, d, tl):
    L, e = u.shape
    nb = e // S5_BW
    ns = bre.shape[2]
    nc = L // tl

    def body(u_ref, bre_ref, bim_ref, cre_ref, cim_ref, a_ref, d_ref, y_ref, xb_ref, sr_ref, si_ref, xr_ref, xi_ref, carry_ref):
        c = pl.program_id(1)

        @pl.when(c == 0)
        def _():
            carry_ref[...] = jnp.zeros_like(carry_ref)
        xb_ref[0, 0] = carry_ref[...]
        ub = u_ref[...]
        xr_ref[...] = _bdot(ub, bre_ref[0])
        xi_ref[...] = _bdot(ub, bim_ref[0])
        ar = a_ref[0, 0:1, :]
        ai = a_ref[0, 1:2, :]
        xr, xi = _s5_scan_rows(xr_ref, xi_ref, ar, ai, carry_ref[0:1, :], carry_ref[1:2, :], tl)
        carry_ref[0:1, :] = xr
        carry_ref[1:2, :] = xi
        sr = xr_ref[...].astype(BF16)
        si = xi_ref[...].astype(BF16)
        sr_ref[...] = sr
        si_ref[...] = si
        y_ref[...] = _f_s5_act(_bdot(sr, cre_ref[0]) - _bdot(si, cim_ref[0]), ub, d_ref[...])[0]

    return pl.pallas_call(
        body, name="s5_core_fwd", grid=(nb, nc),
        in_specs=[pl.BlockSpec((tl, S5_BW), lambda j, c: (c, j)),
                  pl.BlockSpec((1, S5_BW, ns), lambda j, c: (j, 0, 0)), pl.BlockSpec((1, S5_BW, ns), lambda j, c: (j, 0, 0)),
                  pl.BlockSpec((1, ns, S5_BW), lambda j, c: (j, 0, 0)), pl.BlockSpec((1, ns, S5_BW), lambda j, c: (j, 0, 0)),
                  pl.BlockSpec((1, SUBLANES, ns), lambda j, c: (j, 0, 0)), pl.BlockSpec((1, S5_BW), lambda j, c: (0, j))],
        out_specs=[pl.BlockSpec((tl, S5_BW), lambda j, c: (c, j)),
                   pl.BlockSpec((1, 1, SUBLANES, ns), lambda j, c: (j, c, 0, 0)),
                   pl.BlockSpec((tl, ns), lambda j, c: (c, j)), pl.BlockSpec((tl, ns), lambda j, c: (c, j))],
        out_shape=[SDS((L, e), F32), SDS((nb, nc, SUBLANES, ns), F32), SDS((L, nb * ns), BF16), SDS((L, nb * ns), BF16)],
        scratch_shapes=[pltpu.VMEM((tl, ns), F32), pltpu.VMEM((tl, ns), F32), pltpu.VMEM((SUBLANES, ns), F32)],
        compiler_params=_params(("arbitrary", "arbitrary"), VMEM_MID),
    )(u, bre, bim, cre, cim, a, d)


def _s5_bwd_call(u, dy2, bre, bim, cre, cim, a, d, xb, sr, si, tl):
    L, e = u.shape
    nb = e // S5_BW
    ns = bre.shape[2]
    nc = L // tl

    def body(u_ref, dy2_ref, bre_ref, bim_ref, cre_ref, cim_ref, a_ref, d_ref, xb_ref, sr_ref, si_ref,
             du_ref, dbre_ref, dbim_ref, dcre_ref, dcim_ref, da_ref, dd_ref,
             gr_ref, gi_ref, gcarry_ref):
        c = pl.program_id(1)

        @pl.when(c == 0)
        def _():
            gcarry_ref[...] = jnp.zeros_like(gcarry_ref)
            dbre_ref[...] = jnp.zeros_like(dbre_ref)
            dbim_ref[...] = jnp.zeros_like(dbim_ref)
            dcre_ref[...] = jnp.zeros_like(dcre_ref)
            dcim_ref[...] = jnp.zeros_like(dcim_ref)
            da_ref[...] = jnp.zeros_like(da_ref)
            dd_ref[...] = jnp.zeros_like(dd_ref)

        ub = u_ref[...]
        ys = _bdot(sr_ref[...], cre_ref[0]) - _bdot(si_ref[...], cim_ref[0])
        _, act_vjp = jax.vjp(lambda *t: _f_s5_act(*t)[0], ys, ub, d_ref[...])
        dy, du_skip, dd = act_vjp(dy2_ref[...])
        dd_ref[...] += dd
        ar = a_ref[0, 0:1, :]
        ai = a_ref[0, 1:2, :]
        x0r = xb_ref[0, 0, 0:1, :]
        x0i = xb_ref[0, 0, 1:2, :]
        dcre_ref[0] += _bdot(sr_ref[...], dy, TN)
        dcim_ref[0] -= _bdot(si_ref[...], dy, TN)
        gr_ref[...] = _bdot(dy, cre_ref[0], NT)
        gi_ref[...] = -_bdot(dy, cim_ref[0], NT)

        g0r, g0i = _s5_scan_rows(gr_ref, gi_ref, ar, -ai, gcarry_ref[0:1, :], gcarry_ref[1:2, :], tl, reverse=True)
        gcarry_ref[0:1, :] = g0r
        gcarry_ref[1:2, :] = g0i
        row = lax.broadcasted_iota(jnp.int32, (tl, ns), 0)
        gr = gr_ref[...]
        gi = gi_ref[...]
        xpr = jnp.where(row == 0, x0r, pltpu.roll(sr_ref[...].astype(F32), 1, 0))
        xpi = jnp.where(row == 0, x0i, pltpu.roll(si_ref[...].astype(F32), 1, 0))
        da_ref[0, 0:1, :] += jnp.sum(gr * xpr + gi * xpi, axis=0, keepdims=True)
        da_ref[0, 1:2, :] += jnp.sum(gi * xpr - gr * xpi, axis=0, keepdims=True)
        du_ref[...] = (_bdot(gr, bre_ref[0], NT) + _bdot(gi, bim_ref[0], NT)) + du_skip
        dbre_ref[0] += _bdot(ub, gr, TN)
        dbim_ref[0] += _bdot(ub, gi, TN)

    rev = lambda c: nc - 1 - c
    return pl.pallas_call(
        body, name="s5_core_bwd", grid=(nb, nc),
        in_specs=[pl.BlockSpec((tl, S5_BW), lambda j, c: (rev(c), j)), pl.BlockSpec((tl, S5_BW), lambda j, c: (rev(c), j)),
                  pl.BlockSpec((1, S5_BW, ns), lambda j, c: (j, 0, 0)), pl.BlockSpec((1, S5_BW, ns), lambda j, c: (j, 0, 0)),
                  pl.BlockSpec((1, ns, S5_BW), lambda j, c: (j, 0, 0)), pl.BlockSpec((1, ns, S5_BW), lambda j, c: (j, 0, 0)),
                  pl.BlockSpec((1, SUBLANES, ns), lambda j, c: (j, 0, 0)), pl.BlockSpec((1, S5_BW), lambda j, c: (0, j)),
                  pl.BlockSpec((1, 1, SUBLANES, ns), lambda j, c: (j, rev(c), 0, 0)),
                  pl.BlockSpec((tl, ns), lambda j, c: (rev(c), j)), pl.BlockSpec((tl, ns), lambda j, c: (rev(c), j))],
        out_specs=[pl.BlockSpec((tl, S5_BW), lambda j, c: (rev(c), j)),
                   pl.BlockSpec((1, S5_BW, ns), lambda j, c: (j, 0, 0)), pl.BlockSpec((1, S5_BW, ns), lambda j, c: (j, 0, 0)),
                   pl.BlockSpec((1, ns, S5_BW), lambda j, c: (j, 0, 0)), pl.BlockSpec((1, ns, S5_BW), lambda j, c: (j, 0, 0)),
                   pl.BlockSpec((1, SUBLANES, ns), lambda j, c: (j, 0, 0)), pl.BlockSpec((1, S5_BW), lambda j, c: (0, j))],
        out_shape=[SDS((L, e), F32), SDS(bre.shape, F32), SDS(bim.shape, F32), SDS(cre.shape, F32), SDS(cim.shape, F32),
                   SDS(a.shape, F32), SDS(d.shape, F32)],
        scratch_shapes=[pltpu.VMEM((tl, ns), F32) for _ in range(2)] + [pltpu.VMEM((SUBLANES, ns), F32)],
        compiler_params=_params(("arbitrary", "arbitrary"), VMEM_MID),
    )(u, dy2, bre, bim, cre, cim, a, d, xb, sr, si)


def make_s5_core(tl):
    @jax.custom_vjp
    def s5_core(u, bre, bim, cre, cim, a, d):
        return _s5_fwd_call(u, bre, bim, cre, cim, a, d, tl)[0]

    def fwd(u, bre, bim, cre, cim, a, d):
        y2, xb, sr, si = _s5_fwd_call(u, bre, bim, cre, cim, a, d, tl)
        return y2, (u, bre, bim, cre, cim, a, d, xb, sr, si)

    def bwd(res, dy2):
        u, bre, bim, cre, cim, a, d, xb, sr, si = res
        return tuple(_s5_bwd_call(u, dy2, bre, bim, cre, cim, a, d, xb, sr, si, tl))

    s5_core.defvjp(fwd, bwd)
    return s5_core


def _s5_block_params(lam_re, lam_im, log_dt, b_re, b_im, c_re, c_im):
    dt = jnp.exp(log_dt)[:, None]
    mag = jnp.exp(lam_re * dt)
    ab_re = mag * jnp.cos(lam_im * dt)
    ab_im = mag * jnp.sin(lam_im * dt)
    den = lam_re * lam_re + lam_im * lam_im
    nr = ab_re - 1.0
    ni = ab_im
    q_re = (nr * lam_re + ni * lam_im) / den
    q_im = (ni * lam_re - nr * lam_im) / den
    bb_re = q_re[..., None] * b_re - q_im[..., None] * b_im
    bb_im = q_re[..., None] * b_im + q_im[..., None] * b_re
    nb = S5_GROUPS // S5_GB
    eye = jnp.eye(S5_GB, dtype=F32)

    def bdiag_in(bb):
        t = bb.reshape(nb, S5_GB, S5_STATE, S5_GROUP)
        t = jnp.einsum("jgpm,gh->jgmhp", t, eye)
        return t.reshape(nb, S5_GB * S5_GROUP, S5_GB * S5_STATE)

    def bdiag_out(cc):
        t = cc.reshape(nb, S5_GB, S5_GROUP, S5_STATE)
        t = jnp.einsum("jgmp,gh->jgphm", t, eye)
        return t.reshape(nb, S5_GB * S5_STATE, S5_GB * S5_GROUP)

    a = jnp.stack([ab_re.reshape(nb, S5_GB * S5_STATE), ab_im.reshape(nb, S5_GB * S5_STATE)], axis=1)
    a = jnp.concatenate([a, jnp.zeros((nb, SUBLANES - 2, S5_GB * S5_STATE), F32)], axis=1)
    return bdiag_in(bb_re), bdiag_in(bb_im), bdiag_out(c_re), bdiag_out(c_im), a


def _shift_down(x, s, row):
    if s == 0:
        return x
    return jnp.where(row >= s, pltpu.roll(x, s, 0), 0.0)


def _shift_up(x, s, row, n):
    if s == 0:
        return x
    return jnp.where(row < n - s, pltpu.roll(x, n - s, 0), 0.0)


def _causal_conv(xv, w_ref, row):
    acc = jnp.zeros_like(xv)
    for j in range(GDN_CONV):
        acc += w_ref[j:j + 1, :] * _shift_down(xv, GDN_CONV - 1 - j, row)
    return acc


def _conv_fwd_call(x, w, act, name):
    L, ch = x.shape

    def body(x_ref, w_ref, y_ref):
        xv = x_ref[...]
        row = lax.broadcasted_iota(jnp.int32, xv.shape, 0)
        y_ref[...] = act(_causal_conv(xv, w_ref, row))

    return pl.pallas_call(
        body, name=name + "_fwd", grid=(ch // LANES,),
        in_specs=[pl.BlockSpec((L, LANES), lambda j: (0, j)), pl.BlockSpec((SUBLANES, LANES), lambda j: (0, j))],
        out_specs=pl.BlockSpec((L, LANES), lambda j: (0, j)), out_shape=SDS((L, ch), F32),
        compiler_params=_params(("parallel",), VMEM_MID),
    )(x, w)


def _conv_bwd_call(x, w, dy, act, name):
    L, ch = x.shape

    def body(x_ref, w_ref, dy_ref, dx_ref, dw_ref):
        xv = x_ref[...]
        row = lax.broadcasted_iota(jnp.int32, xv.shape, 0)
        _, act_vjp = jax.vjp(act, _causal_conv(xv, w_ref, row))
        (g,) = act_vjp(dy_ref[...])
        acc = jnp.zeros_like(xv)
        dws = []
        for j in range(GDN_CONV):
            s = GDN_CONV - 1 - j
            acc += w_ref[j:j + 1, :] * _shift_up(g, s, row, L)
            dws.append(jnp.sum(g * _shift_down(xv, s, row), axis=0, keepdims=True))
        dx_ref[...] = acc
        dw_ref[...] = jnp.concatenate(dws + [jnp.zeros((SUBLANES - GDN_CONV, LANES), F32)], axis=0)

    return pl.pallas_call(
        body, name=name + "_bwd", grid=(ch // LANES,),
        in_specs=[pl.BlockSpec((L, LANES), lambda j: (0, j)), pl.BlockSpec((SUBLANES, LANES), lambda j: (0, j)),
                  pl.BlockSpec((L, LANES), lambda j: (0, j))],
        out_specs=[pl.BlockSpec((L, LANES), lambda j: (0, j)), pl.BlockSpec((SUBLANES, LANES), lambda j: (0, j))],
        out_shape=[SDS((L, ch), F32), SDS((SUBLANES, ch), F32)],
        compiler_params=_params(("parallel",), VMEM_MID),
    )(x, w, dy)


def make_conv_act(act, name):
    @jax.custom_vjp
    def op(x, w):
        return _conv_fwd_call(x, w, act, name)

    def fwd(x, w):
        return _conv_fwd_call(x, w, act, name), (x, w)

    def bwd(res, dy):
        x, w = res
        return tuple(_conv_bwd_call(x, w, dy, act, name))

    op.defvjp(fwd, bwd)
    return op


BNN, BNT, BTN = _BNN, _BNT, _BTN
GDN_PREP_BATCH = 16


@jax.custom_vjp
def _known_inverse(a, t):
    return t


def _known_inverse_fwd(a, t):
    return t, t


def _known_inverse_bwd(t, g):
    return -_hdot(_hdot(t, g, _BTN), t, _BNT), jnp.zeros_like(t)


_known_inverse.defvjp(_known_inverse_fwd, _known_inverse_bwd)


def _gdn_prep_math(q, k, v, beta, g, t_saved=None):
    B, C = q.shape[0], q.shape[1]
    ri = lax.broadcasted_iota(jnp.int32, (B, C, C), 1)
    ci = lax.broadcasted_iota(jnp.int32, (B, C, C), 2)
    causal = ri >= ci
    strict = ri > ci
    eye = (ri == ci).astype(F32)
    gb = jnp.broadcast_to(g, (B, C, C))
    g_row = jnp.sum(gb * eye, axis=1, keepdims=True)
    gc_col = jnp.sum(jnp.where(causal, jnp.broadcast_to(g_row, (B, C, C)), 0.0), axis=2, keepdims=True)
    gc_row = jnp.sum(jnp.where(ri <= ci, gb, 0.0), axis=1, keepdims=True)
    decay = jnp.exp(jnp.where(causal, gc_col - gc_row, -jnp.inf))
    kk = _bdot(k, k, BNT)
    a_mat = jnp.where(strict, beta * kk * decay, 0.0)
    t = _unit_lower_inverse(a_mat) if t_saved is None else _known_inverse(a_mat, t_saved)
    e_gc = jnp.exp(gc_col)
    w = _hdot(t, beta * e_gc * k, BNN)
    u = _hdot(t, beta * v, BNN)
    qk = _bdot(q, k, BNT) * decay
    q_dec = q * e_gc
    g_last = gc_col[:, C - 1:C, :]
    k_dec = k * jnp.exp(g_last - gc_col)
    return q_dec, w, u, qk, k_dec, gc_col, t


def _gdn_prep_specs(L):
    C = GDN_CHUNK
    nb = min(GDN_PREP_BATCH, L // C)
    R = nb * C
    ins = [pl.BlockSpec((R, GDN_DK), lambda c, h: (c, h)), pl.BlockSpec((R, GDN_DK), lambda c, h: (c, h)),
           pl.BlockSpec((R, GDN_DV), lambda c, h: (c, h)), pl.BlockSpec((R, LANES), lambda c, h: (c, 0))]
    outs = [pl.BlockSpec((1, R, GDN_DK), lambda c, h: (h, c, 0)), pl.BlockSpec((1, R, GDN_DK), lambda c, h: (h, c, 0)),
            pl.BlockSpec((1, R, GDN_DV), lambda c, h: (h, c, 0)), pl.BlockSpec((1, R, C), lambda c, h: (h, c, 0)),
            pl.BlockSpec((1, R, GDN_DK), lambda c, h: (h, c, 0)), pl.BlockSpec((1, R, 1), lambda c, h: (h, c, 0))]
    t_spec = pl.BlockSpec((1, R, C), lambda c, h: (h, c, 0))
    shapes = [SDS((GDN_HEADS, L, GDN_DK), F32), SDS((GDN_HEADS, L, GDN_DK), F32), SDS((GDN_HEADS, L, GDN_DV), F32),
              SDS((GDN_HEADS, L, C), F32), SDS((GDN_HEADS, L, GDN_DK), F32), SDS((GDN_HEADS, L, 1), F32)]
    return ins, outs, t_spec, shapes, nb


def _chunks(x, nb):
    return x.reshape(nb, x.shape[0] // nb, x.shape[1])


def _head_columns(bg, h):
    lane = lax.broadcasted_iota(jnp.int32, bg.shape, 1)
    beta = jnp.sum(jnp.where(lane == h, bg, 0.0), axis=1, keepdims=True)
    g = jnp.sum(jnp.where(lane == h + GDN_HEADS, bg, 0.0), axis=1, keepdims=True)
    return beta, g


def _gdn_prep_fwd_call(q, k, v, bg):
    L = q.shape[0]
    ins, outs, t_spec, shapes, nb = _gdn_prep_specs(L)

    def body(q_ref, k_ref, v_ref, bg_ref, *o_refs):
        beta, g = _head_columns(bg_ref[...], pl.program_id(1))
        res = _gdn_prep_math(_chunks(q_ref[...], nb), _chunks(k_ref[...], nb), _chunks(v_ref[...], nb),
                             _chunks(beta, nb), _chunks(g, nb))
        for o_ref, val in zip(o_refs, res):
            o_ref[0] = val.reshape(val.shape[0] * val.shape[1], val.shape[2])

    return pl.pallas_call(
        body, name="gdn_prep_fwd", grid=(L // (nb * GDN_CHUNK), GDN_HEADS), in_specs=ins, out_specs=outs + [t_spec],
        out_shape=shapes + [SDS((GDN_HEADS, L, GDN_CHUNK), F32)],
        compiler_params=_params(("parallel", "parallel"), VMEM_MID),
    )(q, k, v, bg)


def _gdn_prep_bwd_call(q, k, v, bg, t, cts):
    L = q.shape[0]
    ins, outs, t_spec, _, nb = _gdn_prep_specs(L)

    def body(q_ref, k_ref, v_ref, bg_ref, t_ref, c0, c1, c2, c3, c4, c5, dq_ref, dk_ref, dv_ref, dbg_ref):
        h = pl.program_id(1)
        beta, g = _head_columns(bg_ref[...], h)
        t_saved = _chunks(t_ref[0], nb)
        _, vjp = jax.vjp(lambda *a: _gdn_prep_math(*a, t_saved=t_saved)[:6], _chunks(q_ref[...], nb), _chunks(k_ref[...], nb),
                         _chunks(v_ref[...], nb), _chunks(beta, nb), _chunks(g, nb))
        dq, dk, dv, db, dg = vjp(tuple(_chunks(c[0], nb) for c in (c0, c1, c2, c3, c4, c5)))
        flat = lambda a: a.reshape(a.shape[0] * a.shape[1], a.shape[2])
        dq_ref[...] = flat(dq)
        dk_ref[...] = flat(dk)
        dv_ref[...] = flat(dv)

        @pl.when(h == 0)
        def _():
            dbg_ref[...] = jnp.zeros_like(dbg_ref)
        lane = lax.broadcasted_iota(jnp.int32, dbg_ref.shape, 1)
        dbg_ref[...] += jnp.where(lane == h, flat(db), 0.0) + jnp.where(lane == h + GDN_HEADS, flat(dg), 0.0)

    return pl.pallas_call(
        body, name="gdn_prep_bwd", grid=(L // (nb * GDN_CHUNK), GDN_HEADS), in_specs=ins + [t_spec] + outs, out_specs=ins,
        out_shape=[SDS(q.shape, F32), SDS(k.shape, F32), SDS(v.shape, F32), SDS(bg.shape, F32)],
        compiler_params=_params(("parallel", "arbitrary"), VMEM_MID),
    )(q, k, v, bg, t, *cts)


@jax.custom_vjp
def gdn_prep(q, k, v, bg):
    return tuple(_gdn_prep_fwd_call(q, k, v, bg)[:6])


def _gdn_prep_f(q, k, v, bg):
    res = _gdn_prep_fwd_call(q, k, v, bg)
    return tuple(res[:6]), (q, k, v, bg, res[6])


def _gdn_prep_b(res, cts):
    return tuple(_gdn_prep_bwd_call(*res, tuple(cts)))


gdn_prep.defvjp(_gdn_prep_f, _gdn_prep_b)


def _gdn_step_math(q_dec, w, u, qk, k_dec, gc, z, nw, state):
    H, C = q_dec.shape[0], q_dec.shape[1]
    v_new = u - _bdot(w, state, BNN)
    o = _bdot(q_dec, state, BNN) + _bdot(qk, v_new, BNN)
    gl = gc[:, C - 1:C, :]
    new_state = jnp.exp(gl) * state + _bdot(k_dec, v_new, BTN)
    return _f_gdn_post(jnp.concatenate([o[h] for h in range(H)], axis=1), z, nw)[0], new_state


def _gdn_scan_specs(L, rev):
    C, H = GDN_CHUNK, GDN_HEADS
    nc = L // C
    cc = (lambda c: nc - 1 - c) if rev else (lambda c: c)
    ins = [pl.BlockSpec((H, C, GDN_DK), lambda c: (0, cc(c), 0)), pl.BlockSpec((H, C, GDN_DK), lambda c: (0, cc(c), 0)),
           pl.BlockSpec((H, C, GDN_DV), lambda c: (0, cc(c), 0)), pl.BlockSpec((H, C, C), lambda c: (0, cc(c), 0)),
           pl.BlockSpec((H, C, GDN_DK), lambda c: (0, cc(c), 0)), pl.BlockSpec((H, C, 1), lambda c: (0, cc(c), 0))]
    o_spec = pl.BlockSpec((C, H * GDN_DV), lambda c: (cc(c), 0))
    nw_spec = pl.BlockSpec((1, H * GDN_DV), lambda c: (0, 0))
    s_spec = pl.BlockSpec((1, H, GDN_DK, GDN_DV), lambda c: (cc(c), 0, 0, 0))
    return ins + [o_spec, nw_spec], o_spec, s_spec, nc


def _gdn_scan_fwd_call(q_dec, w, u, qk, k_dec, gc, z, nw):
    L = q_dec.shape[1]
    ins, o_spec, s_spec, nc = _gdn_scan_specs(L, False)

    def body(qd_ref, w_ref, u_ref, qk_ref, kd_ref, gc_ref, z_ref, nw_ref, o_ref, sin_ref, s_ref):
        c = pl.program_id(0)

        @pl.when(c == 0)
        def _():
            s_ref[...] = jnp.zeros_like(s_ref)
        st = s_ref[...]
        sin_ref[0] = st
        o, ns = _gdn_step_math(qd_ref[...], w_ref[...], u_ref[...], qk_ref[...], kd_ref[...], gc_ref[...], z_ref[...], nw_ref[...], st)
        o_ref[...] = o
        s_ref[...] = ns

    return pl.pallas_call(
        body, name="gdn_scan_fwd", grid=(nc,), in_specs=ins, out_specs=[o_spec, s_spec],
        out_shape=[SDS((L, GDN_HEADS * GDN_DV), F32), SDS((nc, GDN_HEADS, GDN_DK, GDN_DV), F32)],
        scratch_shapes=[pltpu.VMEM((GDN_HEADS, GDN_DK, GDN_DV), F32)],
        compiler_params=_params(("arbitrary",), VMEM_MID),
    )(q_dec, w, u, qk, k_dec, gc, z, nw)


def _gdn_scan_bwd_call(q_dec, w, u, qk, k_dec, gc, z, nw, s_in, do):
    L = q_dec.shape[1]
    ins, o_spec, s_spec, nc = _gdn_scan_specs(L, True)

    def body(qd_ref, w_ref, u_ref, qk_ref, kd_ref, gc_ref, z_ref, nw_ref, sin_ref, do_ref,
             dqd_ref, dw_ref, du_ref, dqk_ref, dkd_ref, dgc_ref, dz_ref, dnw_ref, ds_ref):
        c = pl.program_id(0)

        @pl.when(c == 0)
        def _():
            ds_ref[...] = jnp.zeros_like(ds_ref)
            dnw_ref[...] = jnp.zeros_like(dnw_ref)
        _, vjp = jax.vjp(_gdn_step_math, qd_ref[...], w_ref[...], u_ref[...], qk_ref[...], kd_ref[...], gc_ref[...],
                         z_ref[...], nw_ref[...], sin_ref[0])
        dqd, dw, du, dqk, dkd, dgc, dz, dnw, dst = vjp((do_ref[...], ds_ref[...]))
        dqd_ref[...] = dqd
        dw_ref[...] = dw
        du_ref[...] = du
        dqk_ref[...] = dqk
        dkd_ref[...] = dkd
        dgc_ref[...] = dgc
        dz_ref[...] = dz
        dnw_ref[...] += dnw
        ds_ref[...] = dst

    return pl.pallas_call(
        body, name="gdn_scan_bwd", grid=(nc,), in_specs=ins + [s_spec, o_spec], out_specs=ins,
        out_shape=[SDS(t.shape, F32) for t in (q_dec, w, u, qk, k_dec, gc, z, nw)],
        scratch_shapes=[pltpu.VMEM((GDN_HEADS, GDN_DK, GDN_DV), F32)],
        compiler_params=_params(("arbitrary",), VMEM_MID),
    )(q_dec, w, u, qk, k_dec, gc, z, nw, s_in, do)


@jax.custom_vjp
def gdn_scan(q_dec, w, u, qk, k_dec, gc, z, nw):
    return _gdn_scan_fwd_call(q_dec, w, u, qk, k_dec, gc, z, nw)[0]


def _gdn_scan_f(*args):
    o, s_in = _gdn_scan_fwd_call(*args)
    return o, (*args, s_in)


def _gdn_scan_b(res, do):
    return tuple(_gdn_scan_bwd_call(*res, do))


gdn_scan.defvjp(_gdn_scan_f, _gdn_scan_b)


def _silu(x):
    return x * jax.nn.sigmoid(x)


def _gelu_tanh(x):
    return 0.5 * x * (1.0 + jnp.tanh(math.sqrt(2.0 / math.pi) * (x + 0.044715 * (x * x * x))))


def _f_lnmod(x, nw, sc, sh, bsc, bsh):
    xn = x * lax.rsqrt(jnp.mean(x * x, axis=-1, keepdims=True) + NORM_EPS) * nw
    return (xn * (1.0 + (sc + bsc)) + (sh + bsh),)


def _f_s5_act(ys, u, d):
    return (_gelu_tanh(ys + d * u),)


def _f_s5_gate(y2, t, z):
    return (y2 * jax.nn.sigmoid(t) * _silu(z),)


def _f_res(x, y, gate, bgate):
    return (x + (gate + bgate) * y,)


def _heads(x, width, fn):
    return jnp.concatenate([fn(x[:, i * width:(i + 1) * width]) for i in range(x.shape[1] // width)], axis=1)


def _l2n(x):
    return x * lax.rsqrt(jnp.sum(x * x, axis=-1, keepdims=True) + NORM_EPS)


def _f_betag(ba, alog, dtb):
    col = lax.broadcasted_iota(jnp.int32, ba.shape, 1)
    t = ba + dtb
    softplus = jnp.maximum(t, 0.0) + jnp.log1p(jnp.exp(-jnp.abs(t)))
    g = -jnp.exp(alog) * softplus
    return (jnp.where(col < GDN_HEADS, jax.nn.sigmoid(ba), jnp.where(col < 2 * GDN_HEADS, g, 0.0)),)


def _f_gdn_post(o, z, nw):
    on = _heads(o, GDN_DV, lambda t: t * lax.rsqrt(jnp.mean(t * t, axis=-1, keepdims=True) + NORM_EPS))
    return (on * nw * _silu(z),)


def _f_loss(x, tgt, fw):
    y = x * lax.rsqrt(jnp.mean(x * x, axis=-1, keepdims=True) + NORM_EPS) * fw
    err = y - tgt
    return (0.5 * jnp.mean(err * err, axis=-1, keepdims=True),)


def _ada_mod_call(c_all, ada_w):
    n = ada_w.shape[2]

    def body(c_ref, w_ref, o_ref):
        ca = _silu(c_ref[...])
        for l in range(ada_w.shape[0]):
            o_ref[l] = _bdot(ca, w_ref[l])

    return pl.pallas_call(body, name="ada_mod", out_shape=SDS((ada_w.shape[0], N_DEV, n), F32),
                          compiler_params=_params(None, VMEM_MID))(c_all, ada_w)


def _ada_grad_call(c_all, dmod):
    nl, _, n = dmod.shape

    def body(c_ref, d_ref, o_ref):
        ca = _silu(c_ref[...])
        for l in range(nl):
            o_ref[l] = _hdot(ca, d_ref[l], TN)

    return pl.pallas_call(body, name="ada_grad", out_shape=SDS((nl, c_all.shape[1], n), F32),
                          compiler_params=_params(None, VMEM_MID))(c_all, dmod)


ADAM_ROWS = 512


def _adamw(g, w, m, v):
    m2 = ADAM_B1 * m + (1.0 - ADAM_B1) * g
    v2 = ADAM_B2 * v + (1.0 - ADAM_B2) * (g * g)
    m_hat = m2 / (1.0 - ADAM_B1 ** ADAM_STEP)
    v_hat = v2 / (1.0 - ADAM_B2 ** ADAM_STEP)
    return g, -ADAM_LR * (m_hat / (jnp.sqrt(v_hat) + ADAM_EPS) + ADAM_WD * w), m2, v2


def _adam_call(gs, w, m, v, name, rows=None, by_cols=False):
    n, r, cols = gs.shape
    if by_cols:
        blk = pl.BlockSpec((r, LANES), lambda i: (0, i))
        g_blk, grid = pl.BlockSpec((n, r, LANES), lambda i: (0, 0, i)), (cols // LANES,)
    else:
        rows = rows or ADAM_ROWS
        blk = pl.BlockSpec((rows, cols), lambda i: (i, 0))
        g_blk, grid = pl.BlockSpec((n, rows, cols), lambda i: (0, i, 0)), (r // rows,)

    def body(g_ref, w_ref, m_ref, v_ref, go_ref, d_ref, mo_ref, vo_ref):
        g = g_ref[0].astype(F32)
        for s in range(1, n):
            g = g + g_ref[s].astype(F32)
        for o_ref, val in zip((go_ref, d_ref, mo_ref, vo_ref), _adamw(g, w_ref[...], m_ref[...], v_ref[...])):
            o_ref[...] = val

    return pl.pallas_call(
        body, name=name, grid=grid, in_specs=[g_blk, blk, blk, blk],
        out_specs=[blk, blk, blk, blk], out_shape=[SDS((r, cols), F32)] * 4,
        compiler_params=_params(("parallel",), VMEM_MID),
    )(gs, w, m, v)


def _sum_call(gs, name, rows):
    n, r, _ = gs.shape

    def body(g_ref, o_ref):
        g = g_ref[0].astype(F32)
        for s in range(1, n):
            g = g + g_ref[s].astype(F32)
        o_ref[...] = g

    return pl.pallas_call(
        body, name=name, grid=(r // rows,),
        in_specs=[pl.BlockSpec((n, rows, LANES), lambda i: (0, i, 0))],
        out_specs=pl.BlockSpec((rows, LANES), lambda i: (i, 0)), out_shape=SDS((r, LANES), F32),
        compiler_params=_params(("parallel",), VMEM_MID),
    )(gs)


def _allgather_call(x_shard, name):
    m_per, n = x_shard.shape

    def body(x_ref, out_ref, send_sems, recv_sems, local_sem):
        x, y, c = lax.axis_index("x"), lax.axis_index("y"), lax.axis_index("c")
        me, sibling = (x, y, c), (x, y, 1 - c)
        chips = [(1 - x, y), (x, 1 - y), (1 - x, 1 - y)]

        def rows(px, py, pc):
            return out_ref.at[pl.ds((4 * px + 2 * py + pc) * m_per, m_per), :]

        def copy(k, block, to, src=None):
            return pltpu.make_async_remote_copy(
                src_ref=rows(*block) if src is None else src, dst_ref=rows(*block),
                send_sem=send_sems.at[k], recv_sem=recv_sems.at[k], device_id=to, device_id_type=pl.DeviceIdType.MESH)

        mine = pltpu.make_async_copy(x_ref, rows(*me), local_sem)
        mine.start()
        first = [copy(0, me, sibling, src=x_ref)]
        first += [copy(1 + j, me, (*chip, c), src=x_ref) for j, chip in enumerate(chips)]
        for cp in first:
            cp.start()
        passed = [copy(4 + j, (*chip, c), sibling) for j, chip in enumerate(chips)]
        for j, chip in enumerate(chips):
            copy(1 + j, (*chip, c), me).wait_recv()
            passed[j].start()
        copy(0, sibling, me).wait_recv()
        for j, chip in enumerate(chips):
            copy(4 + j, (*chip, 1 - c), me).wait_recv()
        for cp in first + passed:
            cp.wait_send()
        mine.wait()

    vmem = pl.BlockSpec(memory_space=pltpu.VMEM)
    return pl.pallas_call(
        body, name=name, out_shape=SDS((N_DEV * m_per, n), x_shard.dtype), in_specs=[vmem], out_specs=vmem,
        scratch_shapes=[pltpu.SemaphoreType.DMA((7,)), pltpu.SemaphoreType.DMA((7,)), pltpu.SemaphoreType.DMA],
    )(x_shard)


def _gather_weights_call(shards, name):
    nw = len(shards)

    def body(*refs):
        x_refs, out_refs = refs[:nw], refs[nw:2 * nw]
        send_sems, recv_sems, local_sems = refs[2 * nw:]
        x, y, c = lax.axis_index("x"), lax.axis_index("y"), lax.axis_index("c")
        me, sibling = (x, y, c), (x, y, 1 - c)
        chips = [(1 - x, y), (x, 1 - y), (1 - x, 1 - y)]

        def slot(w, px, py, pc):
            return out_refs[w].at[4 * px + 2 * py + pc]

        def copy(w, k, block, to, src=None):
            dst = slot(w, *block)
            return pltpu.make_async_remote_copy(
                src_ref=dst if src is None else src, dst_ref=dst, send_sem=send_sems.at[7 * w + k],
                recv_sem=recv_sems.at[7 * w + k], device_id=to, device_id_type=pl.DeviceIdType.MESH)

        mines = [pltpu.make_async_copy(x_refs[w], slot(w, *me), local_sems.at[w]) for w in range(nw)]
        for cp in mines:
            cp.start()
        first = [copy(w, 0, me, sibling, src=x_refs[w]) for w in range(nw)]
        first += [copy(w, 1 + j, me, (*chip, c), src=x_refs[w]) for w in range(nw) for j, chip in enumerate(chips)]
        for cp in first:
            cp.start()
        passed = []
        for w in range(nw):
            for j, chip in enumerate(chips):
                copy(w, 1 + j, (*chip, c), me).wait_recv()
                fwd = copy(w, 4 + j, (*chip, c), sibling)
                fwd.start()
                passed.append(fwd)
        for w in range(nw):
            copy(w, 0, sibling, me).wait_recv()
            for j, chip in enumerate(chips):
                copy(w, 4 + j, (*chip, 1 - c), me).wait_recv()
        for cp in first + passed:
            cp.wait_send()
        for cp in mines:
            cp.wait()

    hbm = pl.BlockSpec(memory_space=pl.ANY)
    return pl.pallas_call(
        body, name=name, out_shape=[SDS((N_DEV,) + s.shape, s.dtype) for s in shards],
        in_specs=[hbm] * nw, out_specs=[hbm] * nw,
        scratch_shapes=[pltpu.SemaphoreType.DMA((7 * nw,)), pltpu.SemaphoreType.DMA((7 * nw,)), pltpu.SemaphoreType.DMA((nw,))],
    )(*shards)


_HBM = pl.BlockSpec(memory_space=pltpu.HBM)
_SEM = pl.BlockSpec(memory_space=pltpu.SEMAPHORE)
_DATAFLOW = pltpu.SideEffectType.DATAFLOW_SIDE_EFFECTING


def _spread_start_call(srcs, per_peer, name, after):
    nw = len(srcs)
    lands = [lax.empty((N_DEV,) + (s.shape[1:] if per_peer else s.shape), s.dtype) for s in srcs]

    def body(*refs):
        src_refs, land_refs = refs[:nw], refs[nw:2 * nw]
        send_sems, recv_sems, token = refs[2 * nw + 1], refs[2 * nw + 2], refs[-1]
        x, y, c = lax.axis_index("x"), lax.axis_index("y"), lax.axis_index("c")
        me = 4 * x + 2 * y + c
        for w in range(nw):
            for k in range(1, N_DEV):
                px = 1 - x if k & 4 else x
                py = 1 - y if k & 2 else y
                pc = 1 - c if k & 1 else c
                src = src_refs[w].at[4 * px + 2 * py + pc] if per_peer else src_refs[w]
                pltpu.make_async_remote_copy(
                    src_ref=src, dst_ref=land_refs[w].at[me], send_sem=send_sems.at[w], recv_sem=recv_sems.at[w],
                    device_id=(px, py, pc), device_id_type=pl.DeviceIdType.MESH).start()
        token[...] = jnp.zeros_like(token)

    hbm = lambda a: pltpu.with_memory_space_constraint(a, pltpu.HBM)
    res = pl.pallas_call(
        body, name=name,
        out_shape=(pltpu.SemaphoreType.DMA((nw,)), pltpu.SemaphoreType.DMA((nw,)))
        + tuple(pltpu.HBM(s.shape, s.dtype) for s in srcs) + tuple(pltpu.HBM(l.shape, l.dtype) for l in lands)
        + (SDS((SUBLANES, LANES), F32),),
        in_specs=[_HBM] * (2 * nw) + [pl.BlockSpec(memory_space=pl.ANY)],
        out_specs=(_SEM, _SEM) + (_HBM,) * (2 * nw) + (pl.BlockSpec(memory_space=pltpu.VMEM),),
        input_output_aliases={i: i + 2 for i in range(2 * nw)},
        compiler_params=pltpu.CompilerParams(has_side_effects=_DATAFLOW),
    )(*[hbm(s) for s in srcs], *[hbm(l) for l in lands], after)
    return res[0], res[1], res[2:2 + nw], res[2 + nw:2 + 2 * nw], res[-1]


def _spread_wait_call(send_sems, recv_sems, srcs, lands, after, name):
    nw = len(lands)

    def body(*refs):
        land_refs = refs[nw:2 * nw]
        s_sems, r_sems = refs[2 * nw], refs[2 * nw + 1]
        x, y, c = lax.axis_index("x"), lax.axis_index("y"), lax.axis_index("c")
        for w in range(nw):
            seven = land_refs[w].at[pl.ds(0, N_DEV - 1)]
            all_seven = pltpu.make_async_remote_copy(
                src_ref=seven, dst_ref=seven, send_sem=s_sems.at[w], recv_sem=r_sems.at[w],
                device_id=(x, y, c), device_id_type=pl.DeviceIdType.MESH)
            all_seven.wait_send()
            all_seven.wait_recv()

    res = pl.pallas_call(
        body, name=name,
        out_shape=tuple(pltpu.HBM(s.shape, s.dtype) for s in srcs) + tuple(pltpu.HBM(l.shape, l.dtype) for l in lands),
        in_specs=[_HBM] * (2 * nw) + [_SEM, _SEM, pl.BlockSpec(memory_space=pl.ANY)], out_specs=(_HBM,) * (2 * nw),
        input_output_aliases={i: i for i in range(2 * nw)},
        compiler_params=pltpu.CompilerParams(has_side_effects=_DATAFLOW),
    )(*srcs, *lands, send_sems, recv_sems, after)
    return res[:nw], res[nw:]


def _join_cols_call(w8, name):
    _, k, n = w8.shape
    tk = _tile(k, 256)

    def body(w_ref, o_ref):
        for s in range(N_DEV):
            o_ref[:, n * s:n * (s + 1)] = w_ref[s]

    return pl.pallas_call(body, name=name, grid=(k // tk,), in_specs=[pl.BlockSpec((N_DEV, tk, n), lambda i: (0, i, 0))],
                          out_specs=pl.BlockSpec((tk, N_DEV * n), lambda i: (i, 0)), out_shape=SDS((k, N_DEV * n), w8.dtype),
                          compiler_params=_params(("parallel",), VMEM_MID))(w8)


def _split_cols_call(g, name, dtype):
    k, n8 = g.shape
    n = n8 // N_DEV
    tk = _tile(k, 256)

    def body(g_ref, o_ref):
        for s in range(N_DEV):
            o_ref[s] = g_ref[:, n * s:n * (s + 1)].astype(dtype)

    return pl.pallas_call(body, name=name, grid=(k // tk,), in_specs=[pl.BlockSpec((tk, n8), lambda i: (i, 0))],
                          out_specs=pl.BlockSpec((N_DEV, tk, n), lambda i: (0, i, 0)), out_shape=SDS((N_DEV, k, n), dtype),
                          compiler_params=_params(("parallel",), VMEM_MID))(g)


def _pack(parts, rows_multiple):
    flat = jnp.concatenate([p.reshape(-1) for p in parts])
    unit = rows_multiple * LANES
    padded = -(-flat.shape[0] // unit) * unit
    flat = jnp.concatenate([flat, jnp.zeros((padded - flat.shape[0],), F32)])
    return flat.reshape(-1, LANES)


def _groups_last(a):
    x, y = a.shape[-2:]
    return jnp.transpose(a.reshape(S5_GROUPS, x, y), (1, 2, 0)).reshape(x * y, S5_GROUPS)


def _groups_first(a, shape):
    x, y = shape[-2:]
    return jnp.transpose(a.reshape(x, y, S5_GROUPS), (2, 0, 1)).reshape(shape)


def _unpack(buf, shapes):
    flat = buf.reshape(-1)
    out, off = [], 0
    for s in shapes:
        n = math.prod(s)
        out.append(flat[off:off + n].reshape(s))
        off += n
    return out


def _row_tile(L):
    return 256 if L % 256 == 0 else L


def _layer0_mix(diff, const):
    x, mod, norm_w, lam_re, lam_im, log_dt, b_re, b_im, c_re, c_im, s5_d, *slots = diff
    ada_b, weights = const
    L = x.shape[0]
    tm = _row_tile(L)
    mods = mod.reshape(2, 1, D_MODEL)
    biases = ada_b.reshape(2, 1, D_MODEL)
    op_ln0 = make_rowwise(_f_lnmod, "ln0", tm, 1, 5, pass_first=True)
    h, x = op_ln0((x,), (norm_w.reshape(1, D_MODEL), mods[1], mods[0], biases[1], biases[0]))
    u, z = make_proj("s5_in")(h, tuple(weights), tuple(slots))
    blocks = _s5_block_params(lam_re, lam_im, log_dt, b_re, b_im, c_re, c_im)
    y2 = make_s5_core(min(S5_TL, L))(u, *blocks, s5_d.reshape(1, D_INNER))
    return x, y2, z


def _glu_gate_fwd_call(y2, w, z, name):
    m, k = y2.shape
    n = w.shape[1]
    tm, tn = _tile(m, 1024), _tile(n, 512)

    def body(a_ref, w_ref, z_ref, t_ref, y4_ref):
        j = pl.program_id(1)
        t = _bdot(a_ref[...], w_ref[...])
        t_ref[...] = t
        y2_tile = a_ref[:, pl.ds(pl.multiple_of(j * tn, LANES), tn)]
        y4_ref[...] = _f_s5_gate(y2_tile, t, z_ref[...])[0]

    tile = pl.BlockSpec((tm, tn), lambda i, j: (i, j))
    return pl.pallas_call(
        body, name=name, grid=(m // tm, n // tn),
        in_specs=[pl.BlockSpec((tm, k), lambda i, j: (i, 0)), pl.BlockSpec((k, tn), lambda i, j: (0, j)), tile],
        out_specs=[tile, tile], out_shape=[SDS((m, n), F32), SDS((m, n), F32)],
        compiler_params=_params(("parallel", "parallel"), VMEM_MID),
    )(y2, w, z)


def make_glu_gate(name, tm):
    gate = make_rowwise(_f_s5_gate, name + "_gate", tm, 3, 0)

    @jax.custom_vjp
    def op(y2, w, grad_slot, z):
        return _glu_gate_fwd_call(y2, w, z, name + "_fwd")[1]

    def fwd(y2, w, grad_slot, z):
        t, y4 = _glu_gate_fwd_call(y2, w, z, name + "_fwd")
        return y4, (y2, w, t, z)

    def bwd(res, dy4):
        y2, w, t, z = res
        (dy2_gate, dt, dz), _ = gate.run_bwd((y2, t, z), (), (dy4,))
        return _matmul(dt, w, "nt", name + "_dx", add=dy2_gate), jnp.zeros_like(w), _matmul(y2, dt, "tn", name + "_dw"), dz

    op.defvjp(fwd, bwd)
    return op


def _layer0_out(diff, weights):
    y2, z, *slots = diff
    y4 = make_glu_gate("s5_glu", _row_tile(y2.shape[0]))(y2, weights[0], slots[0], z)
    return make_mm("s5_out")(y4, weights[1], slots[1])


def _f_res_lnmod(x, o, gate, bgate, nw, sc, sh, bsc, bsh):
    (x1,) = _f_res(x, o, gate, bgate)
    return _f_lnmod(x1, nw, sc, sh, bsc, bsh) + (x1,)


def _f_res_loss(x, y, tgt, gate, bgate, fw):
    return _f_loss(_f_res(x, y, gate, bgate)[0], tgt, fw)


def _layer1_loss(diff, const):
    x, o, gate0, mod, norm_w, conv_w, a_log, dt_bias, gdn_nw, final_nw, *slots = diff
    tgt, bgate0, ada_b, weights = const
    L = x.shape[0]
    tm = _row_tile(L)
    mods = mod.reshape(3, 1, D_MODEL)
    biases = ada_b.reshape(3, 1, D_MODEL)
    h, x1 = make_rowwise(_f_res_lnmod, "res0_ln1", tm, 2, 7)(
        (x, o), (gate0.reshape(1, D_MODEL), bgate0.reshape(1, D_MODEL), norm_w.reshape(1, D_MODEL), mods[1], mods[0], biases[1], biases[0]))
    q0, k0, v0, gz, ba = make_proj("gdn_in", w_rows=True)(h, tuple(weights[0:5]), tuple(slots[0:5]))
    cw = jnp.concatenate([conv_w, jnp.zeros((SUBLANES - GDN_CONV, GDN_CONV_CH), F32)], axis=0)
    q = make_conv_act(lambda t: _l2n(_silu(t)) * (GDN_DK ** -0.5), "gdn_conv_q")(q0, cw[:, :GDN_QK])
    k = make_conv_act(lambda t: _l2n(_silu(t)), "gdn_conv_k")(k0, cw[:, GDN_QK:2 * GDN_QK])
    v = make_conv_act(_silu, "gdn_conv_v")(v0, cw[:, 2 * GDN_QK:])
    pad = jnp.zeros((LANES - 2 * GDN_HEADS,), F32)
    alog_row = jnp.concatenate([jnp.zeros((GDN_HEADS,), F32), a_log, pad]).reshape(1, LANES)
    dtb_row = jnp.concatenate([jnp.zeros((GDN_HEADS,), F32), dt_bias, pad]).reshape(1, LANES)
    (bg,) = make_rowwise(_f_betag, "gdn_bg", tm, 1, 2)((ba,), (alog_row, dtb_row))
    nw_row = jnp.tile(gdn_nw, GDN_HEADS).reshape(1, D_INNER)
    on = gdn_scan(*gdn_prep(q, k, v, bg), gz, nw_row)
    y = make_mm("gdn_out")(on, weights[5], slots[5])
    (lt,) = make_rowwise(_f_res_loss, "res1_loss", tm, 3, 3)((x1, y, tgt), (mods[2], biases[2], final_nw.reshape(1, D_MODEL)))
    return jnp.sum(lt)


VEC_NAMES = ("ada_b", "norm_w", "s5_lambda_re", "s5_lambda_im", "s5_log_dt", "s5_d", "gdn_a_log", "gdn_dt_bias", "final_norm_w")
MAT_NAMES = ("s5_b_re", "s5_b_im", "s5_c_re", "s5_c_im")
S5_BIG = ("s5_w_in", "s5_w_glu", "s5_w_out")
GDN_BIG = ("gdn_w_in", "gdn_w_out")
BIG_NAMES = S5_BIG + GDN_BIG
WEIGHT_ORDER = ("ada_w", "ada_b", "norm_w", "s5_w_in", "s5_lambda_re", "s5_lambda_im", "s5_log_dt", "s5_b_re", "s5_b_im",
                "s5_c_re", "s5_c_im", "s5_d", "s5_w_glu", "s5_w_out", "gdn_w_in", "gdn_conv_w", "gdn_a_log", "gdn_dt_bias",
                "gdn_norm_w", "gdn_w_out", "final_norm_w")


def _step(x, c, W, M, V, tgt):
    L = x.shape[1]
    ix, iy, ic = lax.axis_index("x"), lax.axis_index("y"), lax.axis_index("c")
    me = 4 * ix + 2 * iy + ic
    n_ada = W["ada_w"].shape[2]
    n_conv = W["gdn_conv_w"].shape[2]
    n_gnw = W["gdn_norm_w"].shape[1]

    g1 = _allgather_call(_pack([c, W["gdn_conv_w"], W["gdn_norm_w"]], SUBLANES), "gather_small_in")
    g1 = g1.reshape(N_DEV, -1)
    c_all = g1[:, :D_MODEL]
    conv_w = g1[:, D_MODEL:D_MODEL + GDN_CONV * n_conv].reshape(N_DEV, GDN_CONV, n_conv).transpose(1, 0, 2).reshape(GDN_CONV, -1)
    gdn_nw = g1[:, D_MODEL + GDN_CONV * n_conv:D_MODEL + GDN_CONV * n_conv + n_gnw].reshape(-1)
    mod_part = _ada_mod_call(c_all, W["ada_w"])
    g2 = _allgather_call(_pack([mod_part], SUBLANES), "gather_mod").reshape(N_DEV, -1)
    mod_all = g2[:, :2 * N_DEV * n_ada].reshape(N_DEV, 2, N_DEV, n_ada)
    mod_raw = lax.dynamic_index_in_dim(mod_all, me, axis=2, keepdims=False)
    mod_raw = mod_raw.transpose(1, 0, 2).reshape(2, 3 * D_MODEL)

    shard = lambda n: W[n][0].astype(BF16)
    (w_in5_parts,) = _gather_weights_call([shard("s5_w_in")], "gather_s5_w_in")
    late = _spread_start_call([shard("s5_w_glu"), shard("s5_w_out")], False, "gather_s5_late_start", w_in5_parts)
    turned = lambda a: jnp.transpose(a[0])
    g_send, g_recv, g_srcs, g_lands, g_token = _spread_start_call(
        [turned(W["gdn_w_in"]).astype(BF16), shard("gdn_w_out")], False, "gather_gdn_start", late[4])
    w_in5 = _join_cols_call(w_in5_parts, "join_s5_w_in")
    slot = lambda *s: jnp.zeros(s, F32)
    two = 2 * D_MODEL
    diff_mix = (x[0], mod_raw[0, :two] + g_token[0, 0], W["norm_w"][0], W["s5_lambda_re"][0], W["s5_lambda_im"][0], W["s5_log_dt"][0],
                W["s5_b_re"][0], W["s5_b_im"][0], W["s5_c_re"][0], W["s5_c_im"][0], W["s5_d"][0],
                slot(D_MODEL, D_INNER), slot(D_MODEL, D_INNER))

    (xp, y2, z5), vjp_mix = jax.vjp(lambda d: _layer0_mix(d, (W["ada_b"][0, :two], (w_in5[:, :D_INNER], w_in5[:, D_INNER:]))), diff_mix)
    l_srcs, l_lands = _spread_wait_call(late[0], late[1], late[2], late[3], y2, "gather_s5_late_wait")
    w_glu, w_o5 = [lax.dynamic_update_slice(land, src[None], (me, 0, 0)).reshape(-1, src.shape[1]) for land, src in zip(l_lands, l_srcs)]
    diff_out = (y2, z5, slot(D_INNER, D_INNER), slot(D_INNER, D_MODEL))
    o5, vjp_out = jax.vjp(lambda d: _layer0_out(d, (w_glu, w_o5)), diff_out)
    g_srcs, g_lands = _spread_wait_call(g_send, g_recv, g_srcs, g_lands, o5, "gather_gdn_wait")
    gdn_full = [lax.dynamic_update_slice(land, src[None], (me, 0, 0)) for land, src in zip(g_lands, g_srcs)]
    w_ing = gdn_full[0].reshape(GDN_PROJ, D_MODEL)
    w_ba = jnp.concatenate([w_ing[GDN_CONV_CH + D_INNER:], jnp.zeros((LANES - 2 * GDN_HEADS, D_MODEL), BF16)], axis=0)
    weights1 = (w_ing[:GDN_QK], w_ing[GDN_QK:2 * GDN_QK], w_ing[2 * GDN_QK:GDN_CONV_CH],
                w_ing[GDN_CONV_CH:GDN_CONV_CH + D_INNER], w_ba, gdn_full[1].reshape(D_INNER, D_MODEL))
    slots1 = tuple(jnp.zeros(w.shape, F32) for w in weights1)
    diff1 = (xp, o5, mod_raw[0, two:], mod_raw[1], W["norm_w"][1], conv_w, W["gdn_a_log"][0], W["gdn_dt_bias"][0], gdn_nw,
             W["final_norm_w"], *slots1)
    loss_local, vjp1 = jax.vjp(lambda d: _layer1_loss(d, (tgt[0], W["ada_b"][0, two:], W["ada_b"][1], weights1)), diff1)
    ((dxp, do5, dmod_gate, dmod1, d_norm_w1, d_conv, d_alog, d_dtb, d_gnw, d_fnw, d_wq, d_wk, d_wv, d_wgz, d_wba, d_wog),) = vjp1(
        jnp.ones((), F32))
    loss = lax.psum(loss_local, MESH_AXES)

    rows = lambda d: d.reshape(N_DEV, d.shape[0] // N_DEV, d.shape[1])
    d_ing = jnp.concatenate([d_wq, d_wk, d_wv, d_wgz, d_wba[:2 * GDN_HEADS]], axis=0).astype(BF16).reshape(N_DEV, -1, D_MODEL)
    s_send, s_recv, s_srcs, s_lands, s_token = _spread_start_call([d_ing, rows(d_wog).astype(BF16)], True, "scatter_gdn_start", dxp)
    ((dy2, dz5, d_wglu, d_wo5),) = vjp_out(do5.at[0, 0].add(s_token[0, 0]))
    t_send, t_recv, t_srcs, t_lands, t_token = _spread_start_call(
        [rows(d_wglu).astype(BF16), rows(d_wo5).astype(BF16)], True, "scatter_s5_late_start", dy2)
    ((dx, dmod_ss, d_norm_w0, d_lre, d_lim, d_logdt, d_bre, d_bim, d_cre, d_cim, d_s5d, d_wu, d_wz),) = vjp_mix(
        (dxp.at[0, 0].add(t_token[0, 0]), dy2, dz5))
    dmod = jnp.stack([jnp.concatenate([dmod_ss, dmod_gate]), dmod1])
    d_norm_w = jnp.stack([d_norm_w0, d_norm_w1])
    vec_parts = [dmod, d_norm_w, d_lre, d_lim, d_logdt, d_s5d, d_alog, d_dtb, d_fnw]
    tail_parts = [d_conv, d_gnw]
    mat_parts = [_groups_last(d) for d in (d_bre, d_bim, d_cre, d_cim)]
    n_vec = sum(math.prod(p.shape) for p in vec_parts)
    m_send, m_recv, m_srcs, m_lands, m_token = _spread_start_call(
        [_pack(vec_parts + tail_parts, ADAM_ROWS), _pack(mat_parts, SUBLANES).astype(BF16)], False, "gather_small_grads_start", dx)
    d_in5 = _split_cols_call(jnp.concatenate([d_wu.at[0, 0].add(m_token[0, 0]), d_wz], axis=1), "split_s5_w_in", BF16)
    u_send, u_recv, u_srcs, u_lands, u_token = _spread_start_call([d_in5], True, "scatter_s5_in_start", m_token)
    t_srcs, t_lands = _spread_wait_call(t_send, t_recv, t_srcs, t_lands, u_token, "scatter_s5_late_wait")
    s_srcs, s_lands = _spread_wait_call(s_send, s_recv, s_srcs, s_lands, t_lands[0], "scatter_gdn_wait")
    big = {}

    def owner_update(land, src, n):
        mine = lax.dynamic_index_in_dim(src, me, 0, keepdims=True)
        parts = lax.dynamic_update_slice(land, mine, (me, 0, 0))
        if n == "gdn_w_in":
            outs = _adam_call(parts, turned(W[n]), turned(M[n]), turned(V[n]), "adam_" + n, by_cols=True)
            return [jnp.transpose(o) for o in outs]
        return _adam_call(parts, W[n][0], M[n][0], V[n][0], "adam_" + n, rows=_tile(W[n].shape[1], 128))

    for land, src, n in zip(tuple(t_lands) + tuple(s_lands), tuple(t_srcs) + tuple(s_srcs), ("s5_w_glu", "s5_w_out") + GDN_BIG):
        big[n] = owner_update(land, src, n)

    m_srcs, m_lands = _spread_wait_call(m_send, m_recv, m_srcs, m_lands, big["gdn_w_out"][0], "gather_small_grads_wait")
    sg_vec, sg_mat = [lax.dynamic_update_slice(land, src[None], (me, 0, 0)) for land, src in zip(m_lands, m_srcs)]
    tot_vec = _sum_call(sg_vec, "sum_vec_grads", ADAM_ROWS)
    tot_mat = _sum_call(sg_mat, "sum_mat_grads", ADAM_ROWS)
    g_conv, g_gnw = _unpack(tot_vec.reshape(-1)[n_vec:], [d_conv.shape, d_gnw.shape])
    g_conv_mine = lax.dynamic_slice_in_dim(g_conv, me * n_conv, n_conv, axis=1)
    g_gnw_mine = lax.dynamic_slice_in_dim(g_gnw, me * n_gnw, n_gnw, axis=0)
    vec_names = VEC_NAMES + ("gdn_conv_w", "gdn_norm_w")
    vec_g = _pack([tot_vec.reshape(-1)[:n_vec], g_conv_mine, g_gnw_mine], ADAM_ROWS)
    vec = _adam_call(vec_g[None], _pack([W[n] for n in vec_names], ADAM_ROWS), _pack([M[n] for n in vec_names], ADAM_ROWS),
                     _pack([V[n] for n in vec_names], ADAM_ROWS), "adam_vec")
    vec = [_unpack(b, [W[n].shape for n in vec_names]) for b in vec]
    mats = []
    for name, g_mat in zip(MAT_NAMES, _unpack(tot_mat, [p.shape for p in mat_parts])):
        outs = _adam_call(g_mat[None], _groups_last(W[name]), _groups_last(M[name]), _groups_last(V[name]), "adam_" + name)
        mats.append([_groups_first(o, W[name].shape) for o in outs])

    dmod_all = sg_vec[:, :2 * 3 * D_MODEL // LANES].reshape(N_DEV, 2, N_DEV, n_ada // LANES, LANES)
    dmod_mine = lax.dynamic_index_in_dim(dmod_all, me, axis=2, keepdims=False).transpose(1, 0, 2, 3).reshape(2, N_DEV, n_ada)
    g_ada_w = _ada_grad_call(c_all, dmod_mine)
    ada = _adam_call(g_ada_w.reshape(1, -1, LANES), W["ada_w"].reshape(-1, LANES), M["ada_w"].reshape(-1, LANES),
                     V["ada_w"].reshape(-1, LANES), "adam_ada")
    u_srcs, u_lands = _spread_wait_call(u_send, u_recv, u_srcs, u_lands, ada[0], "scatter_s5_in_wait")
    big["s5_w_in"] = owner_update(u_lands[0], u_srcs[0], "s5_w_in")
    ada = [a.reshape(W["ada_w"].shape) for a in ada]

    res = {}
    for n in BIG_NAMES:
        res[n] = [o[None] for o in big[n]]
    for i, n in enumerate(vec_names):
        res[n] = [b[i] for b in vec]
    for i, n in enumerate(MAT_NAMES):
        res[n] = mats[i]
    res["ada_w"] = ada
    outs = [loss, dx[None]]
    for j in range(4):
        outs += [res[n][j] for n in WEIGHT_ORDER]
    return tuple(outs)


def kernel(x, c, ada_w, ada_b, norm_w, s5_w_in, s5_lambda_re, s5_lambda_im, s5_log_dt, s5_b_re, s5_b_im, s5_c_re, s5_c_im, s5_d, s5_w_glu, s5_w_out, gdn_w_in, gdn_conv_w, gdn_a_log, gdn_dt_bias, gdn_norm_w, gdn_w_out, final_norm_w, loss_target, m_ada_w, m_ada_b, m_norm_w, m_s5_w_in, m_s5_lambda_re, m_s5_lambda_im, m_s5_log_dt, m_s5_b_re, m_s5_b_im, m_s5_c_re, m_s5_c_im, m_s5_d, m_s5_w_glu, m_s5_w_out, m_gdn_w_in, m_gdn_conv_w, m_gdn_a_log, m_gdn_dt_bias, m_gdn_norm_w, m_gdn_w_out, m_final_norm_w, v_ada_w, v_ada_b, v_norm_w, v_s5_w_in, v_s5_lambda_re, v_s5_lambda_im, v_s5_log_dt, v_s5_b_re, v_s5_b_im, v_s5_c_re, v_s5_c_im, v_s5_d, v_s5_w_glu, v_s5_w_out, v_gdn_w_in, v_gdn_conv_w, v_gdn_a_log, v_gdn_dt_bias, v_gdn_norm_w, v_gdn_w_out, v_final_norm_w):
    W = dict(ada_w=ada_w, ada_b=ada_b, norm_w=norm_w, s5_w_in=s5_w_in, s5_lambda_re=s5_lambda_re, s5_lambda_im=s5_lambda_im,
             s5_log_dt=s5_log_dt, s5_b_re=s5_b_re, s5_b_im=s5_b_im, s5_c_re=s5_c_re, s5_c_im=s5_c_im, s5_d=s5_d,
             s5_w_glu=s5_w_glu, s5_w_out=s5_w_out, gdn_w_in=gdn_w_in, gdn_conv_w=gdn_conv_w, gdn_a_log=gdn_a_log,
             gdn_dt_bias=gdn_dt_bias, gdn_norm_w=gdn_norm_w, gdn_w_out=gdn_w_out, final_norm_w=final_norm_w)
    M = dict(ada_w=m_ada_w, ada_b=m_ada_b, norm_w=m_norm_w, s5_w_in=m_s5_w_in, s5_lambda_re=m_s5_lambda_re,
             s5_lambda_im=m_s5_lambda_im, s5_log_dt=m_s5_log_dt, s5_b_re=m_s5_b_re, s5_b_im=m_s5_b_im, s5_c_re=m_s5_c_re,
             s5_c_im=m_s5_c_im, s5_d=m_s5_d, s5_w_glu=m_s5_w_glu, s5_w_out=m_s5_w_out, gdn_w_in=m_gdn_w_in,
             gdn_conv_w=m_gdn_conv_w, gdn_a_log=m_gdn_a_log, gdn_dt_bias=m_gdn_dt_bias, gdn_norm_w=m_gdn_norm_w,
             gdn_w_out=m_gdn_w_out, final_norm_w=m_final_norm_w)
    V = dict(ada_w=v_ada_w, ada_b=v_ada_b, norm_w=v_norm_w, s5_w_in=v_s5_w_in, s5_lambda_re=v_s5_lambda_re,
             s5_lambda_im=v_s5_lambda_im, s5_log_dt=v_s5_log_dt, s5_b_re=v_s5_b_re, s5_b_im=v_s5_b_im, s5_c_re=v_s5_c_re,
             s5_c_im=v_s5_c_im, s5_d=v_s5_d, s5_w_glu=v_s5_w_glu, s5_w_out=v_s5_w_out, gdn_w_in=v_gdn_w_in,
             gdn_conv_w=v_gdn_conv_w, gdn_a_log=v_gdn_a_log, gdn_dt_bias=v_gdn_dt_bias, gdn_norm_w=v_gdn_norm_w,
             gdn_w_out=v_gdn_w_out, final_norm_w=v_final_norm_w)
    return _step(x, c, W, M, V, loss_target)
```

```python
import functools
import math

import jax
import jax.numpy as jnp
from jax import lax
from jax.experimental import pallas as pl
from jax.experimental.pallas import tpu as pltpu

F32 = jnp.float32
BF16 = jnp.bfloat16
SDS = jax.ShapeDtypeStruct

D_MODEL = 1024
D_INNER = 2048
NORM_EPS = 1e-6
S5_GROUP = 16
S5_GROUPS = 128
S5_STATE = 64
GDN_HEADS = 8
GDN_DK = 128
GDN_DV = 256
GDN_CONV = 4
GDN_CHUNK = 64
GDN_QK = 1024
GDN_CONV_CH = 4096
GDN_PROJ = 6160
ADAM_LR = 0.001
ADAM_B1 = 0.9
ADAM_B2 = 0.999
ADAM_EPS = 1e-08
ADAM_WD = 0.01
ADAM_STEP = 10

N_DEV = 8
LANES = 128
SUBLANES = 8
VMEM_BIG = 56 << 20
VMEM_MID = 40 << 20
S5_GB = 8
S5_BW = S5_GB * S5_GROUP
S5_TL = 2048
MESH_AXES = ("x", "y", "c")


def _params(sem, vmem=None):
    return pltpu.CompilerParams(dimension_semantics=sem, vmem_limit_bytes=vmem)


def _bdot(a, b, dims=(((1,), (0,)), ((), ()))):
    return lax.dot_general(a.astype(BF16), b.astype(BF16), dims, preferred_element_type=F32)


def _hdot(a, b, dims=(((1,), (0,)), ((), ()))):
    return lax.dot_general(a, b, dims, preferred_element_type=F32, precision=lax.Precision.HIGHEST)


_BNN = (((2,), (1,)), ((0,), (0,)))
_BNT = (((2,), (2,)), ((0,), (0,)))
_BTN = (((1,), (1,)), ((0,), (0,)))


@jax.custom_vjp
def _unit_lower_inverse(a):
    c = a.shape[-1]
    ri = lax.broadcasted_iota(jnp.int32, a.shape, 1)
    ci = lax.broadcasted_iota(jnp.int32, a.shape, 2)
    n = -a
    t = (ri == ci).astype(F32) + n
    for _ in range(int(math.log2(c)) - 1):
        n = _hdot(n, n, _BNN)
        t = t + _hdot(t, n, _BNN)
    return t


def _unit_lower_inverse_fwd(a):
    t = _unit_lower_inverse(a)
    return t, t


def _unit_lower_inverse_bwd(t, g):
    return (-_hdot(_hdot(t, g, _BTN), t, _BNT),)


_unit_lower_inverse.defvjp(_unit_lower_inverse_fwd, _unit_lower_inverse_bwd)


NN = (((1,), (0,)), ((), ()))
NT = (((1,), (1,)), ((), ()))
TN = (((0,), (0,)), ((), ()))


def _tile(n, pref):
    for t in (pref, 512, 256, 128):
        if t <= n and n % t == 0:
            return t
    return n


def _matmul(a, b, mode, name, add=None):
    if mode == "nn":
        (m, k), (_, n) = a.shape, b.shape
    elif mode == "nt":
        (m, k), (n, _) = a.shape, b.shape
    else:
        (k, m), (_, n) = a.shape, b.shape
    tm, tn, tk = _tile(m, 1024), _tile(n, 512), (k if k <= 2048 else _tile(k, 512))
    if mode == "tn":
        tm, tn, tk = _tile(m, 1024), _tile(n, 1024), _tile(k, 1024)
    nk = k // tk
    dims = {"nn": NN, "nt": NT, "tn": TN}[mode]

    def body(a_ref, b_ref, *rest):
        o_ref, acc_ref = rest[-2], rest[-1]
        part = _bdot(a_ref[...], b_ref[...], dims)
        if nk == 1:
            o_ref[...] = part if add is None else part + rest[0][...]
            return
        kk = pl.program_id(2)

        @pl.when(kk == 0)
        def _():
            acc_ref[...] = part if add is None else part + rest[0][...]

        @pl.when(kk > 0)
        def _():
            acc_ref[...] += part

        @pl.when(kk == nk - 1)
        def _():
            o_ref[...] = acc_ref[...]

    a_spec = pl.BlockSpec((tk, tm), lambda i, j, q: (q, i)) if mode == "tn" else pl.BlockSpec((tm, tk), lambda i, j, q: (i, q))
    b_spec = pl.BlockSpec((tn, tk), lambda i, j, q: (j, q)) if mode == "nt" else pl.BlockSpec((tk, tn), lambda i, j, q: (q, j))
    o_spec = pl.BlockSpec((tm, tn), lambda i, j, q: (i, j))
    return pl.pallas_call(
        body, name=name, grid=(m // tm, n // tn, nk),
        in_specs=[a_spec, b_spec] + ([] if add is None else [o_spec]), out_specs=o_spec,
        out_shape=SDS((m, n), F32), scratch_shapes=[pltpu.VMEM((tm, tn), F32)],
        compiler_params=_params(("parallel", "parallel", "arbitrary"), VMEM_MID),
    )(a, b, *([] if add is None else [add]))


def make_mm(name, pass_input=False):
    def primal(a, w):
        out = _matmul(a, w, "nn", name + "_fwd")
        return (out, a) if pass_input else out

    @jax.custom_vjp
    def mm(a, w, grad_slot):
        return primal(a, w)

    def fwd(a, w, grad_slot):
        return primal(a, w), (a, w)

    def bwd(res, g):
        a, w = res
        g, g_other = g if pass_input else (g, None)
        return _matmul(g, w, "nt", name + "_dx", add=g_other), jnp.zeros_like(w), _matmul(a, g, "tn", name + "_dw")

    mm.defvjp(fwd, bwd)
    return mm


PROJ_ROWS = 256


def _proj_fwd_call(a, ws, name, w_rows):
    m, k = a.shape
    tm = _tile(m, PROJ_ROWS)
    nw = len(ws)
    widths = [w.shape[0] if w_rows else w.shape[1] for w in ws]

    def body(*refs):
        ab = refs[0][...].astype(BF16)
        for w_ref, o_ref in zip(refs[1:1 + nw], refs[1 + nw:]):
            o_ref[...] = lax.dot_general(ab, w_ref[...], NT if w_rows else NN, preferred_element_type=F32)

    return pl.pallas_call(
        body, name=name, grid=(m // tm,),
        in_specs=[pl.BlockSpec((tm, k), lambda i: (i, 0))] + [pl.BlockSpec(w.shape, lambda i: (0, 0)) for w in ws],
        out_specs=[pl.BlockSpec((tm, n), lambda i: (i, 0)) for n in widths],
        out_shape=[SDS((m, n), F32) for n in widths],
        compiler_params=_params(("parallel",), VMEM_BIG),
    )(a, *ws)


def _proj_dx_call(gs, ws, name, w_rows):
    m = gs[0].shape[0]
    k = ws[0].shape[1] if w_rows else ws[0].shape[0]
    tm = _tile(m, PROJ_ROWS)
    nw = len(ws)

    def body(*refs):
        acc = None
        for g_ref, w_ref in zip(refs[:nw], refs[nw:2 * nw]):
            part = _bdot(g_ref[...], w_ref[...], NN if w_rows else NT)
            acc = part if acc is None else acc + part
        refs[2 * nw][...] = acc

    return pl.pallas_call(
        body, name=name, grid=(m // tm,),
        in_specs=[pl.BlockSpec((tm, g.shape[1]), lambda i: (i, 0)) for g in gs] + [pl.BlockSpec(w.shape, lambda i: (0, 0)) for w in ws],
        out_specs=pl.BlockSpec((tm, k), lambda i: (i, 0)), out_shape=SDS((m, k), F32),
        compiler_params=_params(("parallel",), VMEM_BIG),
    )(*gs, *ws)


def make_proj(name, w_rows=False):
    @jax.custom_vjp
    def proj(a, ws, grad_slots):
        return tuple(_proj_fwd_call(a, ws, name + "_fwd", w_rows))

    def fwd(a, ws, grad_slots):
        return tuple(_proj_fwd_call(a, ws, name + "_fwd", w_rows)), (a, ws)

    def bwd(res, gs):
        a, ws = res
        dws = tuple(_matmul(g, a, "tn", "%s_dw%d" % (name, i)) if w_rows else _matmul(a, g, "tn", "%s_dw%d" % (name, i))
                    for i, g in enumerate(gs))
        return _proj_dx_call(tuple(gs), ws, name + "_dx", w_rows), tuple(jnp.zeros_like(w) for w in ws), dws

    proj.defvjp(fwd, bwd)
    return proj


def make_rowwise(f, name, tm, n_rows, n_params, vmem=VMEM_MID, pass_first=False):
    def specs_of(arrs, blocked):
        if blocked:
            return [pl.BlockSpec((tm, a.shape[1]), lambda i: (i, 0)) for a in arrs]
        return [pl.BlockSpec(a.shape, lambda i: (0, 0)) for a in arrs]

    def out_structs(rows, params):
        blk = [SDS((tm, r.shape[1]), r.dtype) for r in rows] + [SDS(p.shape, p.dtype) for p in params]
        return jax.eval_shape(f, *blk)

    def run_fwd(rows, params):
        L = rows[0].shape[0]
        outs = out_structs(rows, params)

        def body(*refs):
            ins = [r[...] for r in refs[:n_rows + n_params]]
            res = f(*ins)
            for o_ref, val in zip(refs[n_rows + n_params:], res):
                o_ref[...] = val

        return pl.pallas_call(
            body, name=name + "_fwd", grid=(L // tm,),
            in_specs=specs_of(rows, True) + specs_of(params, False),
            out_specs=[pl.BlockSpec((tm, o.shape[1]), lambda i: (i, 0)) for o in outs],
            out_shape=[SDS((L, o.shape[1]), o.dtype) for o in outs],
            compiler_params=_params(("parallel",), vmem),
        )(*rows, *params)

    def run_bwd(rows, params, gs):
        L = rows[0].shape[0]
        n_g = len(gs)

        def body(*refs):
            i = pl.program_id(0)
            ins = [r[...] for r in refs[:n_rows + n_params]]
            cts = tuple(r[...] for r in refs[n_rows + n_params:n_rows + n_params + n_g])
            outs = refs[n_rows + n_params + n_g:]
            _, vjp = jax.vjp(f, *ins)
            grads = vjp(cts[:-1] if pass_first else cts)
            if pass_first:
                grads = (grads[0] + cts[-1],) + tuple(grads[1:])
            for o_ref, val in zip(outs[:n_rows], grads[:n_rows]):
                o_ref[...] = val

            if n_params:
                @pl.when(i == 0)
                def _():
                    for o_ref in outs[n_rows:]:
                        o_ref[...] = jnp.zeros_like(o_ref)
                for o_ref, val in zip(outs[n_rows:], grads[n_rows:]):
                    o_ref[...] += val

        res = pl.pallas_call(
            body, name=name + "_bwd", grid=(L // tm,),
            in_specs=specs_of(rows, True) + specs_of(params, False) + specs_of(gs, True),
            out_specs=specs_of(rows, True) + specs_of(params, False),
            out_shape=[SDS(r.shape, r.dtype) for r in rows] + [SDS(p.shape, p.dtype) for p in params],
            compiler_params=_params(("arbitrary",), vmem),
        )(*rows, *params, *gs)
        return tuple(res[:n_rows]), tuple(res[n_rows:])

    def outputs(rows, params):
        outs = tuple(run_fwd(rows, params))
        return outs + (rows[0],) if pass_first else outs

    @jax.custom_vjp
    def op(rows, params):
        return outputs(rows, params)

    def fwd(rows, params):
        return outputs(rows, params), (rows, params)

    def bwd(res, gs):
        rows, params = res
        return run_bwd(rows, params, tuple(gs))

    op.defvjp(fwd, bwd)
    op.run_fwd, op.run_bwd = run_fwd, run_bwd
    return op


def _s5_scan_rows(xr_ref, xi_ref, ar, ai, x0r, x0i, tl, reverse=False):
    n = xr_ref.shape[1]
    T = SUBLANES
    row = lax.broadcasted_iota(jnp.int32, (T, n), 0)
    pr, pi = [ar], [ai]
    for _ in range(T - 1):
        pr, pi = pr + [pr[-1] * ar - pi[-1] * ai], pi + [pr[-1] * ai + pi[-1] * ar]
    levels = []
    for d in (1, 2, 4):
        mask = (row < T - d) if reverse else (row >= d)
        levels.append((T - d if reverse else d, jnp.where(mask, pr[d - 1], 0.0), jnp.where(mask, pi[d - 1], 0.0)))
    cr = jnp.zeros((T, n), F32)
    ci = jnp.zeros((T, n), F32)
    for r in range(T):
        k = (T - r) if reverse else (r + 1)
        cr = jnp.where(row == r, pr[k - 1], cr)
        ci = jnp.where(row == r, pi[k - 1], ci)
    nt = tl // T
    last = 0 if reverse else T - 1

    def step(t, carry):
        sr, si = carry
        base = pl.multiple_of((nt - 1 - t if reverse else t) * T, T)
        br = xr_ref[pl.ds(base, T), :]
        bi = xi_ref[pl.ds(base, T), :]
        for shift, mr, mi in levels:
            qr = pltpu.roll(br, shift, 0)
            qi = pltpu.roll(bi, shift, 0)
            br, bi = br + (mr * qr - mi * qi), bi + (mr * qi + mi * qr)
        xr = br + (cr * sr - ci * si)
        xi = bi + (cr * si + ci * sr)
        xr_ref[pl.ds(base, T), :] = xr
        xi_ref[pl.ds(base, T), :] = xi
        return xr[last:last + 1, :], xi[last:last + 1, :]
    return lax.fori_loop(0, nt, step, (x0r, x0i))


def _s5_fwd_call(u, bre, bim, cre, cim, a, d, tl):
    L, e = u.shape
    nb = e // S5_BW
    ns = bre.shape[2]
    nc = L // tl

    def body(u_ref, bre_ref, bim_ref, cre_ref, cim_ref, a_ref, d_ref, y_ref, xb_ref, sr_ref, si_ref, xr_ref, xi_ref, carry_ref):
        c = pl.program_id(1)

        @pl.when(c == 0)
        def _():
            carry_ref[...] = jnp.zeros_like(carry_ref)
        xb_ref[0, 0] = carry_ref[...]
        ub = u_ref[...]
        xr_ref[...] = _bdot(ub, bre_ref[0])
        xi_ref[...] = _bdot(ub, bim_ref[0])
        ar = a_ref[0, 0:1, :]
        ai = a_ref[0, 1:2, :]
        xr, xi = _s5_scan_rows(xr_ref, xi_ref, ar, ai, carry_ref[0:1, :], carry_ref[1:2, :], tl)
        carry_ref[0:1, :] = xr
        carry_ref[1:2, :] = xi
        sr = xr_ref[...].astype(BF16)
        si = xi_ref[...].astype(BF16)
        sr_ref[...] = sr
        si_ref[...] = si
        y_ref[...] = _f_s5_act(_bdot(sr, cre_ref[0]) - _bdot(si, cim_ref[0]), ub, d_ref[...])[0]

    return pl.pallas_call(
        body, name="s5_core_fwd", grid=(nb, nc),
        in_specs=[pl.BlockSpec((tl, S5_BW), lambda j, c: (c, j)),
                  pl.BlockSpec((1, S5_BW, ns), lambda j, c: (j, 0, 0)), pl.BlockSpec((1, S5_BW, ns), lambda j, c: (j, 0, 0)),
                  pl.BlockSpec((1, ns, S5_BW), lambda j, c: (j, 0, 0)), pl.BlockSpec((1, ns, S5_BW), lambda j, c: (j, 0, 0)),
                  pl.BlockSpec((1, SUBLANES, ns), lambda j, c: (j, 0, 0)), pl.BlockSpec((1, S5_BW), lambda j, c: (0, j))],
        out_specs=[pl.BlockSpec((tl, S5_BW), lambda j, c: (c, j)),
                   pl.BlockSpec((1, 1, SUBLANES, ns), lambda j, c: (j, c, 0, 0)),
                   pl.BlockSpec((tl, ns), lambda j, c: (c, j)), pl.BlockSpec((tl, ns), lambda j, c: (c, j))],
        out_shape=[SDS((L, e), F32), SDS((nb, nc, SUBLANES, ns), F32), SDS((L, nb * ns), BF16), SDS((L, nb * ns), BF16)],
        scratch_shapes=[pltpu.VMEM((tl, ns), F32), pltpu.VMEM((tl, ns), F32), pltpu.VMEM((SUBLANES, ns), F32)],
        compiler_params=_params(("arbitrary", "arbitrary"), VMEM_MID),
    )(u, bre, bim, cre, cim, a, d)


def _s5_bwd_call(u, dy2, bre, bim, cre, cim, a, d, xb, sr, si, tl):
    L, e = u.shape
    nb = e // S5_BW
    ns = bre.shape[2]
    nc = L // tl

    def body(u_ref, dy2_ref, bre_ref, bim_ref, cre_ref, cim_ref, a_ref, d_ref, xb_ref, sr_ref, si_ref,
             du_ref, dbre_ref, dbim_ref, dcre_ref, dcim_ref, da_ref, dd_ref,
             gr_ref, gi_ref, gcarry_ref):
        c = pl.program_id(1)

        @pl.when(c == 0)
        def _():
            gcarry_ref[...] = jnp.zeros_like(gcarry_ref)
            dbre_ref[...] = jnp.zeros_like(dbre_ref)
            dbim_ref[...] = jnp.zeros_like(dbim_ref)
            dcre_ref[...] = jnp.zeros_like(dcre_ref)
            dcim_ref[...] = jnp.zeros_like(dcim_ref)
            da_ref[...] = jnp.zeros_like(da_ref)
            dd_ref[...] = jnp.zeros_like(dd_ref)

        ub = u_ref[...]
        ys = _bdot(sr_ref[...], cre_ref[0]) - _bdot(si_ref[...], cim_ref[0])
        _, act_vjp = jax.vjp(lambda *t: _f_s5_act(*t)[0], ys, ub, d_ref[...])
        dy, du_skip, dd = act_vjp(dy2_ref[...])
        dd_ref[...] += dd
        ar = a_ref[0, 0:1, :]
        ai = a_ref[0, 1:2, :]
        x0r = xb_ref[0, 0, 0:1, :]
        x0i = xb_ref[0, 0, 1:2, :]
        dcre_ref[0] += _bdot(sr_ref[...], dy, TN)
        dcim_ref[0] -= _bdot(si_ref[...], dy, TN)
        gr_ref[...] = _bdot(dy, cre_ref[0], NT)
        gi_ref[...] = -_bdot(dy, cim_ref[0], NT)

        g0r, g0i = _s5_scan_rows(gr_ref, gi_ref, ar, -ai, gcarry_ref[0:1, :], gcarry_ref[1:2, :], tl, reverse=True)
        gcarry_ref[0:1, :] = g0r
        gcarry_ref[1:2, :] = g0i
        row = lax.broadcasted_iota(jnp.int32, (tl, ns), 0)
        gr = gr_ref[...]
        gi = gi_ref[...]
        xpr = jnp.where(row == 0, x0r, pltpu.roll(sr_ref[...].astype(F32), 1, 0))
        xpi = jnp.where(row == 0, x0i, pltpu.roll(si_ref[...].astype(F32), 1, 0))
        da_ref[0, 0:1, :] += jnp.sum(gr * xpr + gi * xpi, axis=0, keepdims=True)
        da_ref[0, 1:2, :] += jnp.sum(gi * xpr - gr * xpi, axis=0, keepdims=True)
        du_ref[...] = (_bdot(gr, bre_ref[0], NT) + _bdot(gi, bim_ref[0], NT)) + du_skip
        dbre_ref[0] += _bdot(ub, gr, TN)
        dbim_ref[0] += _bdot(ub, gi, TN)

    rev = lambda c: nc - 1 - c
    return pl.pallas_call(
        body, name="s5_core_bwd", grid=(nb, nc),
        in_specs=[pl.BlockSpec((tl, S5_BW), lambda j, c: (rev(c), j)), pl.BlockSpec((tl, S5_BW), lambda j, c: (rev(c), j)),
                  pl.BlockSpec((1, S5_BW, ns), lambda j, c: (j, 0, 0)), pl.BlockSpec((1, S5_BW, ns), lambda j, c: (j, 0, 0)),
                  pl.BlockSpec((1, ns, S5_BW), lambda j, c: (j, 0, 0)), pl.BlockSpec((1, ns, S5_BW), lambda j, c: (j, 0, 0)),
                  pl.BlockSpec((1, SUBLANES, ns), lambda j, c: (j, 0, 0)), pl.BlockSpec((1, S5_BW), lambda j, c: (0, j)),
                  pl.BlockSpec((1, 1, SUBLANES, ns), lambda j, c: (j, rev(c), 0, 0)),
                  pl.BlockSpec((tl, ns), lambda j, c: (rev(c), j)), pl.BlockSpec((tl, ns), lambda j, c: (rev(c), j))],
        out_specs=[pl.BlockSpec((tl, S5_BW), lambda j, c: (rev(c), j)),
                   pl.BlockSpec((1, S5_BW, ns), lambda j, c: (j, 0, 0)), pl.BlockSpec((1, S5_BW, ns), lambda j, c: (j, 0, 0)),
                   pl.BlockSpec((1, ns, S5_BW), lambda j, c: (j, 0, 0)), pl.BlockSpec((1, ns, S5_BW), lambda j, c: (j, 0, 0)),
                   pl.BlockSpec((1, SUBLANES, ns), lambda j, c: (j, 0, 0)), pl.BlockSpec((1, S5_BW), lambda j, c: (0, j))],
        out_shape=[SDS((L, e), F32), SDS(bre.shape, F32), SDS(bim.shape, F32), SDS(cre.shape, F32), SDS(cim.shape, F32),
                   SDS(a.shape, F32), SDS(d.shape, F32)],
        scratch_shapes=[pltpu.VMEM((tl, ns), F32) for _ in range(2)] + [pltpu.VMEM((SUBLANES, ns), F32)],
        compiler_params=_params(("arbitrary", "arbitrary"), VMEM_MID),
    )(u, dy2, bre, bim, cre, cim, a, d, xb, sr, si)


def make_s5_core(tl):
    @jax.custom_vjp
    def s5_core(u, bre, bim, cre, cim, a, d):
        return _s5_fwd_call(u, bre, bim, cre, cim, a, d, tl)[0]

    def fwd(u, bre, bim, cre, cim, a, d):
        y2, xb, sr, si = _s5_fwd_call(u, bre, bim, cre, cim, a, d, tl)
        return y2, (u, bre, bim, cre, cim, a, d, xb, sr, si)

    def bwd(res, dy2):
        u, bre, bim, cre, cim, a, d, xb, sr, si = res
        return tuple(_s5_bwd_call(u, dy2, bre, bim, cre, cim, a, d, xb, sr, si, tl))

    s5_core.defvjp(fwd, bwd)
    return s5_core


def _s5_block_params(lam_re, lam_im, log_dt, b_re, b_im, c_re, c_im):
    dt = jnp.exp(log_dt)[:, None]
    mag = jnp.exp(lam_re * dt)
    ab_re = mag * jnp.cos(lam_im * dt)
    ab_im = mag * jnp.sin(lam_im * dt)
    den = lam_re * lam_re + lam_im * lam_im
    nr = ab_re - 1.0
    ni = ab_im
    q_re = (nr * lam_re + ni * lam_im) / den
    q_im = (ni * lam_re - nr * lam_im) / den
    bb_re = q_re[..., None] * b_re - q_im[..., None] * b_im
    bb_im = q_re[..., None] * b_im + q_im[..., None] * b_re
    nb = S5_GROUPS // S5_GB
    eye = jnp.eye(S5_GB, dtype=F32)

    def bdiag_in(bb):
        t = bb.reshape(nb, S5_GB, S5_STATE, S5_GROUP)
        t = jnp.einsum("jgpm,gh->jgmhp", t, eye)
        return t.reshape(nb, S5_GB * S5_GROUP, S5_GB * S5_STATE)

    def bdiag_out(cc):
        t = cc.reshape(nb, S5_GB, S5_GROUP, S5_STATE)
        t = jnp.einsum("jgmp,gh->jgphm", t, eye)
        return t.reshape(nb, S5_GB * S5_STATE, S5_GB * S5_GROUP)

    a = jnp.stack([ab_re.reshape(nb, S5_GB * S5_STATE), ab_im.reshape(nb, S5_GB * S5_STATE)], axis=1)
    a = jnp.concatenate([a, jnp.zeros((nb, SUBLANES - 2, S5_GB * S5_STATE), F32)], axis=1)
    return bdiag_in(bb_re), bdiag_in(bb_im), bdiag_out(c_re), bdiag_out(c_im), a


def _shift_down(x, s, row):
    if s == 0:
        return x
    return jnp.where(row >= s, pltpu.roll(x, s, 0), 0.0)


def _shift_up(x, s, row, n):
    if s == 0:
        return x
    return jnp.where(row < n - s, pltpu.roll(x, n - s, 0), 0.0)


def _causal_conv(xv, w_ref, row):
    acc = jnp.zeros_like(xv)
    for j in range(GDN_CONV):
        acc += w_ref[j:j + 1, :] * _shift_down(xv, GDN_CONV - 1 - j, row)
    return acc


def _conv_fwd_call(x, w, act, name):
    L, ch = x.shape

    def body(x_ref, w_ref, y_ref):
        xv = x_ref[...]
        row = lax.broadcasted_iota(jnp.int32, xv.shape, 0)
        y_ref[...] = act(_causal_conv(xv, w_ref, row))

    return pl.pallas_call(
        body, name=name + "_fwd", grid=(ch // LANES,),
        in_specs=[pl.BlockSpec((L, LANES), lambda j: (0, j)), pl.BlockSpec((SUBLANES, LANES), lambda j: (0, j))],
        out_specs=pl.BlockSpec((L, LANES), lambda j: (0, j)), out_shape=SDS((L, ch), F32),
        compiler_params=_params(("parallel",), VMEM_MID),
    )(x, w)


def _conv_bwd_call(x, w, dy, act, name):
    L, ch = x.shape

    def body(x_ref, w_ref, dy_ref, dx_ref, dw_ref):
        xv = x_ref[...]
        row = lax.broadcasted_iota(jnp.int32, xv.shape, 0)
        _, act_vjp = jax.vjp(act, _causal_conv(xv, w_ref, row))
        (g,) = act_vjp(dy_ref[...])
        acc = jnp.zeros_like(xv)
        dws = []
        for j in range(GDN_CONV):
            s = GDN_CONV - 1 - j
            acc += w_ref[j:j + 1, :] * _shift_up(g, s, row, L)
            dws.append(jnp.sum(g * _shift_down(xv, s, row), axis=0, keepdims=True))
        dx_ref[...] = acc
        dw_ref[...] = jnp.concatenate(dws + [jnp.zeros((SUBLANES - GDN_CONV, LANES), F32)], axis=0)

    return pl.pallas_call(
        body, name=name + "_bwd", grid=(ch // LANES,),
        in_specs=[pl.BlockSpec((L, LANES), lambda j: (0, j)), pl.BlockSpec((SUBLANES, LANES), lambda j: (0, j)),
                  pl.BlockSpec((L, LANES), lambda j: (0, j))],
        out_specs=[pl.BlockSpec((L, LANES), lambda j: (0, j)), pl.BlockSpec((SUBLANES, LANES), lambda j: (0, j))],
        out_shape=[SDS((L, ch), F32), SDS((SUBLANES, ch), F32)],
        compiler_params=_params(("parallel",), VMEM_MID),
    )(x, w, dy)


def make_conv_act(act, name):
    @jax.custom_vjp
    def op(x, w):
        return _conv_fwd_call(x, w, act, name)

    def fwd(x, w):
        return _conv_fwd_call(x, w, act, name), (x, w)

    def bwd(res, dy):
        x, w = res
        return tuple(_conv_bwd_call(x, w, dy, act, name))

    op.defvjp(fwd, bwd)
    return op


BNN, BNT, BTN = _BNN, _BNT, _BTN
GDN_PREP_BATCH = 16


@jax.custom_vjp
def _known_inverse(a, t):
    return t


def _known_inverse_fwd(a, t):
    return t, t


def _known_inverse_bwd(t, g):
    return -_hdot(_hdot(t, g, _BTN), t, _BNT), jnp.zeros_like(t)


_known_inverse.defvjp(_known_inverse_fwd, _known_inverse_bwd)


def _gdn_prep_math(q, k, v, beta, g, t_saved=None):
    B, C = q.shape[0], q.shape[1]
    ri = lax.broadcasted_iota(jnp.int32, (B, C, C), 1)
    ci = lax.broadcasted_iota(jnp.int32, (B, C, C), 2)
    causal = ri >= ci
    strict = ri > ci
    eye = (ri == ci).astype(F32)
    gb = jnp.broadcast_to(g, (B, C, C))
    g_row = jnp.sum(gb * eye, axis=1, keepdims=True)
    gc_col = jnp.sum(jnp.where(causal, jnp.broadcast_to(g_row, (B, C, C)), 0.0), axis=2, keepdims=True)
    gc_row = jnp.sum(jnp.where(ri <= ci, gb, 0.0), axis=1, keepdims=True)
    decay = jnp.exp(jnp.where(causal, gc_col - gc_row, -jnp.inf))
    kk = _bdot(k, k, BNT)
    a_mat = jnp.where(strict, beta * kk * decay, 0.0)
    t = _unit_lower_inverse(a_mat) if t_saved is None else _known_inverse(a_mat, t_saved)
    e_gc = jnp.exp(gc_col)
    w = _hdot(t, beta * e_gc * k, BNN)
    u = _hdot(t, beta * v, BNN)
    qk = _bdot(q, k, BNT) * decay
    q_dec = q * e_gc
    g_last = gc_col[:, C - 1:C, :]
    k_dec = k * jnp.exp(g_last - gc_col)
    return q_dec, w, u, qk, k_dec, gc_col, t


def _gdn_prep_specs(L):
    C = GDN_CHUNK
    nb = min(GDN_PREP_BATCH, L // C)
    R = nb * C
    ins = [pl.BlockSpec((R, GDN_DK), lambda c, h: (c, h)), pl.BlockSpec((R, GDN_DK), lambda c, h: (c, h)),
           pl.BlockSpec((R, GDN_DV), lambda c, h: (c, h)), pl.BlockSpec((R, LANES), lambda c, h: (c, 0))]
    outs = [pl.BlockSpec((1, R, GDN_DK), lambda c, h: (h, c, 0)), pl.BlockSpec((1, R, GDN_DK), lambda c, h: (h, c, 0)),
            pl.BlockSpec((1, R, GDN_DV), lambda c, h: (h, c, 0)), pl.BlockSpec((1, R, C), lambda c, h: (h, c, 0)),
            pl.BlockSpec((1, R, GDN_DK), lambda c, h: (h, c, 0)), pl.BlockSpec((1, R, 1), lambda c, h: (h, c, 0))]
    t_spec = pl.BlockSpec((1, R, C), lambda c, h: (h, c, 0))
    shapes = [SDS((GDN_HEADS, L, GDN_DK), F32), SDS((GDN_HEADS, L, GDN_DK), F32), SDS((GDN_HEADS, L, GDN_DV), F32),
              SDS((GDN_HEADS, L, C), F32), SDS((GDN_HEADS, L, GDN_DK), F32), SDS((GDN_HEADS, L, 1), F32)]
    return ins, outs, t_spec, shapes, nb


def _chunks(x, nb):
    return x.reshape(nb, x.shape[0] // nb, x.shape[1])


def _head_columns(bg, h):
    lane = lax.broadcasted_iota(jnp.int32, bg.shape, 1)
    beta = jnp.sum(jnp.where(lane == h, bg, 0.0), axis=1, keepdims=True)
    g = jnp.sum(jnp.where(lane == h + GDN_HEADS, bg, 0.0), axis=1, keepdims=True)
    return beta, g


def _gdn_prep_fwd_call(q, k, v, bg):
    L = q.shape[0]
    ins, outs, t_spec, shapes, nb = _gdn_prep_specs(L)

    def body(q_ref, k_ref, v_ref, bg_ref, *o_refs):
        beta, g = _head_columns(bg_ref[...], pl.program_id(1))
        res = _gdn_prep_math(_chunks(q_ref[...], nb), _chunks(k_ref[...], nb), _chunks(v_ref[...], nb),
                             _chunks(beta, nb), _chunks(g, nb))
        for o_ref, val in zip(o_refs, res):
            o_ref[0] = val.reshape(val.shape[0] * val.shape[1], val.shape[2])

    return pl.pallas_call(
        body, name="gdn_prep_fwd", grid=(L // (nb * GDN_CHUNK), GDN_HEADS), in_specs=ins, out_specs=outs + [t_spec],
        out_shape=shapes + [SDS((GDN_HEADS, L, GDN_CHUNK), F32)],
        compiler_params=_params(("parallel", "parallel"), VMEM_MID),
    )(q, k, v, bg)


def _gdn_prep_bwd_call(q, k, v, bg, t, cts):
    L = q.shape[0]
    ins, outs, t_spec, _, nb = _gdn_prep_specs(L)

    def body(q_ref, k_ref, v_ref, bg_ref, t_ref, c0, c1, c2, c3, c4, c5, dq_ref, dk_ref, dv_ref, dbg_ref):
        h = pl.program_id(1)
        beta, g = _head_columns(bg_ref[...], h)
        t_saved = _chunks(t_ref[0], nb)
        _, vjp = jax.vjp(lambda *a: _gdn_prep_math(*a, t_saved=t_saved)[:6], _chunks(q_ref[...], nb), _chunks(k_ref[...], nb),
                         _chunks(v_ref[...], nb), _chunks(beta, nb), _chunks(g, nb))
        dq, dk, dv, db, dg = vjp(tuple(_chunks(c[0], nb) for c in (c0, c1, c2, c3, c4, c5)))
        flat = lambda a: a.reshape(a.shape[0] * a.shape[1], a.shape[2])
        dq_ref[...] = flat(dq)
        dk_ref[...] = flat(dk)
        dv_ref[...] = flat(dv)

        @pl.when(h == 0)
        def _():
            dbg_ref[...] = jnp.zeros_like(dbg_ref)
        lane = lax.broadcasted_iota(jnp.int32, dbg_ref.shape, 1)
        dbg_ref[...] += jnp.where(lane == h, flat(db), 0.0) + jnp.where(lane == h + GDN_HEADS, flat(dg), 0.0)

    return pl.pallas_call(
        body, name="gdn_prep_bwd", grid=(L // (nb * GDN_CHUNK), GDN_HEADS), in_specs=ins + [t_spec] + outs, out_specs=ins,
        out_shape=[SDS(q.shape, F32), SDS(k.shape, F32), SDS(v.shape, F32), SDS(bg.shape, F32)],
        compiler_params=_params(("parallel", "arbitrary"), VMEM_MID),
    )(q, k, v, bg, t, *cts)


@jax.custom_vjp
def gdn_prep(q, k, v, bg):
    return tuple(_gdn_prep_fwd_call(q, k, v, bg)[:6])


def _gdn_prep_f(q, k, v, bg):
    res = _gdn_prep_fwd_call(q, k, v, bg)
    return tuple(res[:6]), (q, k, v, bg, res[6])


def _gdn_prep_b(res, cts):
    return tuple(_gdn_prep_bwd_call(*res, tuple(cts)))


gdn_prep.defvjp(_gdn_prep_f, _gdn_prep_b)


def _gdn_step_math(q_dec, w, u, qk, k_dec, gc, z, nw, state):
    H, C = q_dec.shape[0], q_dec.shape[1]
    v_new = u - _bdot(w, state, BNN)
    o = _bdot(q_dec, state, BNN) + _bdot(qk, v_new, BNN)
    gl = gc[:, C - 1:C, :]
    new_state = jnp.exp(gl) * state + _bdot(k_dec, v_new, BTN)
    return _f_gdn_post(jnp.concatenate([o[h] for h in range(H)], axis=1), z, nw)[0], new_state


GDN_SCAN_CHUNKS = 2


def _gdn_steps_math(q_dec, w, u, qk, k_dec, gc, z, nw, state):
    outs = []
    for i in range(q_dec.shape[1] // GDN_CHUNK):
        s = slice(i * GDN_CHUNK, (i + 1) * GDN_CHUNK)
        o, state = _gdn_step_math(q_dec[:, s], w[:, s], u[:, s], qk[:, s], k_dec[:, s], gc[:, s], z[s], nw, state)
        outs.append(o)
    return jnp.concatenate(outs, axis=0), state


def _gdn_scan_specs(L, rev):
    H = GDN_HEADS
    C = GDN_CHUNK * min(GDN_SCAN_CHUNKS, L // GDN_CHUNK)
    nc = L // C
    cc = (lambda c: nc - 1 - c) if rev else (lambda c: c)
    ins = [pl.BlockSpec((H, C, GDN_DK), lambda c: (0, cc(c), 0)), pl.BlockSpec((H, C, GDN_DK), lambda c: (0, cc(c), 0)),
           pl.BlockSpec((H, C, GDN_DV), lambda c: (0, cc(c), 0)), pl.BlockSpec((H, C, GDN_CHUNK), lambda c: (0, cc(c), 0)),
           pl.BlockSpec((H, C, GDN_DK), lambda c: (0, cc(c), 0)), pl.BlockSpec((H, C, 1), lambda c: (0, cc(c), 0))]
    o_spec = pl.BlockSpec((C, H * GDN_DV), lambda c: (cc(c), 0))
    nw_spec = pl.BlockSpec((1, H * GDN_DV), lambda c: (0, 0))
    s_spec = pl.BlockSpec((1, H, GDN_DK, GDN_DV), lambda c: (cc(c), 0, 0, 0))
    return ins + [o_spec, nw_spec], o_spec, s_spec, nc


def _gdn_scan_fwd_call(q_dec, w, u, qk, k_dec, gc, z, nw):
    L = q_dec.shape[1]
    ins, o_spec, s_spec, nc = _gdn_scan_specs(L, False)

    def body(qd_ref, w_ref, u_ref, qk_ref, kd_ref, gc_ref, z_ref, nw_ref, o_ref, sin_ref, s_ref):
        c = pl.program_id(0)

        @pl.when(c == 0)
        def _():
            s_ref[...] = jnp.zeros_like(s_ref)
        st = s_ref[...]
        sin_ref[0] = st
        o, ns = _gdn_steps_math(qd_ref[...], w_ref[...], u_ref[...], qk_ref[...], kd_ref[...], gc_ref[...], z_ref[...], nw_ref[...], st)
        o_ref[...] = o
        s_ref[...] = ns

    return pl.pallas_call(
        body, name="gdn_scan_fwd", grid=(nc,), in_specs=ins, out_specs=[o_spec, s_spec],
        out_shape=[SDS((L, GDN_HEADS * GDN_DV), F32), SDS((nc, GDN_HEADS, GDN_DK, GDN_DV), F32)],
        scratch_shapes=[pltpu.VMEM((GDN_HEADS, GDN_DK, GDN_DV), F32)],
        compiler_params=_params(("arbitrary",), VMEM_MID),
    )(q_dec, w, u, qk, k_dec, gc, z, nw)


def _gdn_scan_bwd_call(q_dec, w, u, qk, k_dec, gc, z, nw, s_in, do):
    L = q_dec.shape[1]
    ins, o_spec, s_spec, nc = _gdn_scan_specs(L, True)

    def body(qd_ref, w_ref, u_ref, qk_ref, kd_ref, gc_ref, z_ref, nw_ref, sin_ref, do_ref,
             dqd_ref, dw_ref, du_ref, dqk_ref, dkd_ref, dgc_ref, dz_ref, dnw_ref, ds_ref):
        c = pl.program_id(0)

        @pl.when(c == 0)
        def _():
            ds_ref[...] = jnp.zeros_like(ds_ref)
            dnw_ref[...] = jnp.zeros_like(dnw_ref)
        _, vjp = jax.vjp(_gdn_steps_math, qd_ref[...], w_ref[...], u_ref[...], qk_ref[...], kd_ref[...], gc_ref[...],
                         z_ref[...], nw_ref[...], sin_ref[0])
        dqd, dw, du, dqk, dkd, dgc, dz, dnw, dst = vjp((do_ref[...], ds_ref[...]))
        dqd_ref[...] = dqd
        dw_ref[...] = dw
        du_ref[...] = du
        dqk_ref[...] = dqk
        dkd_ref[...] = dkd
        dgc_ref[...] = dgc
        dz_ref[...] = dz
        dnw_ref[...] += dnw
        ds_ref[...] = dst

    return pl.pallas_call(
        body, name="gdn_scan_bwd", grid=(nc,), in_specs=ins + [s_spec, o_spec], out_specs=ins,
        out_shape=[SDS(t.shape, F32) for t in (q_dec, w, u, qk, k_dec, gc, z, nw)],
        scratch_shapes=[pltpu.VMEM((GDN_HEADS, GDN_DK, GDN_DV), F32)],
        compiler_params=_params(("arbitrary",), VMEM_MID),
    )(q_dec, w, u, qk, k_dec, gc, z, nw, s_in, do)


@jax.custom_vjp
def gdn_scan(q_dec, w, u, qk, k_dec, gc, z, nw):
    return _gdn_scan_fwd_call(q_dec, w, u, qk, k_dec, gc, z, nw)[0]


def _gdn_scan_f(*args):
    o, s_in = _gdn_scan_fwd_call(*args)
    return o, (*args, s_in)


def _gdn_scan_b(res, do):
    return tuple(_gdn_scan_bwd_call(*res, do))


gdn_scan.defvjp(_gdn_scan_f, _gdn_scan_b)


def _silu(x):
    return x * jax.nn.sigmoid(x)


def _gelu_tanh(x):
    return 0.5 * x * (1.0 + jnp.tanh(math.sqrt(2.0 / math.pi) * (x + 0.044715 * (x * x * x))))


def _f_lnmod(x, nw, sc, sh, bsc, bsh):
    xn = x * lax.rsqrt(jnp.mean(x * x, axis=-1, keepdims=True) + NORM_EPS) * nw
    return (xn * (1.0 + (sc + bsc)) + (sh + bsh),)


def _f_s5_act(ys, u, d):
    return (_gelu_tanh(ys + d * u),)


def _f_s5_gate(y2, t, z):
    return (y2 * jax.nn.sigmoid(t) * _silu(z),)


def _f_res(x, y, gate, bgate):
    return (x + (gate + bgate) * y,)


def _heads(x, width, fn):
    return jnp.concatenate([fn(x[:, i * width:(i + 1) * width]) for i in range(x.shape[1] // width)], axis=1)


def _l2n(x):
    return x * lax.rsqrt(jnp.sum(x * x, axis=-1, keepdims=True) + NORM_EPS)


def _f_betag(ba, alog, dtb):
    col = lax.broadcasted_iota(jnp.int32, ba.shape, 1)
    t = ba + dtb
    softplus = jnp.maximum(t, 0.0) + jnp.log1p(jnp.exp(-jnp.abs(t)))
    g = -jnp.exp(alog) * softplus
    return (jnp.where(col < GDN_HEADS, jax.nn.sigmoid(ba), jnp.where(col < 2 * GDN_HEADS, g, 0.0)),)


def _f_gdn_post(o, z, nw):
    on = _heads(o, GDN_DV, lambda t: t * lax.rsqrt(jnp.mean(t * t, axis=-1, keepdims=True) + NORM_EPS))
    return (on * nw * _silu(z),)


def _f_loss(x, tgt, fw):
    y = x * lax.rsqrt(jnp.mean(x * x, axis=-1, keepdims=True) + NORM_EPS) * fw
    err = y - tgt
    return (0.5 * jnp.mean(err * err, axis=-1, keepdims=True),)


def _ada_mod_call(c_all, ada_w):
    n = ada_w.shape[2]

    def body(c_ref, w_ref, o_ref):
        ca = _silu(c_ref[...])
        for l in range(ada_w.shape[0]):
            o_ref[l] = _bdot(ca, w_ref[l])

    return pl.pallas_call(body, name="ada_mod", out_shape=SDS((ada_w.shape[0], N_DEV, n), F32),
                          compiler_params=_params(None, VMEM_MID))(c_all, ada_w)


def _ada_grad_call(c_all, dmod):
    nl, _, n = dmod.shape

    def body(c_ref, d_ref, o_ref):
        ca = _silu(c_ref[...])
        for l in range(nl):
            o_ref[l] = _hdot(ca, d_ref[l], TN)

    return pl.pallas_call(body, name="ada_grad", out_shape=SDS((nl, c_all.shape[1], n), F32),
                          compiler_params=_params(None, VMEM_MID))(c_all, dmod)


ADAM_ROWS = 512


def _adamw(g, w, m, v):
    m2 = ADAM_B1 * m + (1.0 - ADAM_B1) * g
    v2 = ADAM_B2 * v + (1.0 - ADAM_B2) * (g * g)
    m_hat = m2 / (1.0 - ADAM_B1 ** ADAM_STEP)
    v_hat = v2 / (1.0 - ADAM_B2 ** ADAM_STEP)
    return g, -ADAM_LR * (m_hat / (jnp.sqrt(v_hat) + ADAM_EPS) + ADAM_WD * w), m2, v2


def _adam_call(gs, w, m, v, name, rows=None, by_cols=False):
    n, r, cols = gs.shape
    if by_cols:
        blk = pl.BlockSpec((r, LANES), lambda i: (0, i))
        g_blk, grid = pl.BlockSpec((n, r, LANES), lambda i: (0, 0, i)), (cols // LANES,)
    else:
        rows = rows or ADAM_ROWS
        blk = pl.BlockSpec((rows, cols), lambda i: (i, 0))
        g_blk, grid = pl.BlockSpec((n, rows, cols), lambda i: (0, i, 0)), (r // rows,)

    def body(g_ref, w_ref, m_ref, v_ref, go_ref, d_ref, mo_ref, vo_ref):
        g = g_ref[0].astype(F32)
        for s in range(1, n):
            g = g + g_ref[s].astype(F32)
        for o_ref, val in zip((go_ref, d_ref, mo_ref, vo_ref), _adamw(g, w_ref[...], m_ref[...], v_ref[...])):
            o_ref[...] = val

    return pl.pallas_call(
        body, name=name, grid=grid, in_specs=[g_blk, blk, blk, blk],
        out_specs=[blk, blk, blk, blk], out_shape=[SDS((r, cols), F32)] * 4,
        compiler_params=_params(("parallel",), VMEM_MID),
    )(gs, w, m, v)


def _sum_call(gs, name, rows):
    n, r, _ = gs.shape

    def body(g_ref, o_ref):
        g = g_ref[0].astype(F32)
        for s in range(1, n):
            g = g + g_ref[s].astype(F32)
        o_ref[...] = g

    return pl.pallas_call(
        body, name=name, grid=(r // rows,),
        in_specs=[pl.BlockSpec((n, rows, LANES), lambda i: (0, i, 0))],
        out_specs=pl.BlockSpec((rows, LANES), lambda i: (i, 0)), out_shape=SDS((r, LANES), F32),
        compiler_params=_params(("parallel",), VMEM_MID),
    )(gs)


def _allgather_call(x_shard, name):
    m_per, n = x_shard.shape

    def body(x_ref, out_ref, send_sems, recv_sems, local_sem):
        x, y, c = lax.axis_index("x"), lax.axis_index("y"), lax.axis_index("c")
        me, sibling = (x, y, c), (x, y, 1 - c)
        chips = [(1 - x, y), (x, 1 - y), (1 - x, 1 - y)]

        def rows(px, py, pc):
            return out_ref.at[pl.ds((4 * px + 2 * py + pc) * m_per, m_per), :]

        def copy(k, block, to, src=None):
            return pltpu.make_async_remote_copy(
                src_ref=rows(*block) if src is None else src, dst_ref=rows(*block),
                send_sem=send_sems.at[k], recv_sem=recv_sems.at[k], device_id=to, device_id_type=pl.DeviceIdType.MESH)

        mine = pltpu.make_async_copy(x_ref, rows(*me), local_sem)
        mine.start()
        first = [copy(0, me, sibling, src=x_ref)]
        first += [copy(1 + j, me, (*chip, c), src=x_ref) for j, chip in enumerate(chips)]
        for cp in first:
            cp.start()
        passed = [copy(4 + j, (*chip, c), sibling) for j, chip in enumerate(chips)]
        for j, chip in enumerate(chips):
            copy(1 + j, (*chip, c), me).wait_recv()
            passed[j].start()
        copy(0, sibling, me).wait_recv()
        for j, chip in enumerate(chips):
            copy(4 + j, (*chip, 1 - c), me).wait_recv()
        for cp in first + passed:
            cp.wait_send()
        mine.wait()

    vmem = pl.BlockSpec(memory_space=pltpu.VMEM)
    return pl.pallas_call(
        body, name=name, out_shape=SDS((N_DEV * m_per, n), x_shard.dtype), in_specs=[vmem], out_specs=vmem,
        scratch_shapes=[pltpu.SemaphoreType.DMA((7,)), pltpu.SemaphoreType.DMA((7,)), pltpu.SemaphoreType.DMA],
    )(x_shard)


def _gather_weights_call(shards, name):
    nw = len(shards)

    def body(*refs):
        x_refs, out_refs = refs[:nw], refs[nw:2 * nw]
        send_sems, recv_sems, local_sems = refs[2 * nw:]
        x, y, c = lax.axis_index("x"), lax.axis_index("y"), lax.axis_index("c")
        me, sibling = (x, y, c), (x, y, 1 - c)
        chips = [(1 - x, y), (x, 1 - y), (1 - x, 1 - y)]

        def slot(w, px, py, pc):
            return out_refs[w].at[4 * px + 2 * py + pc]

        def copy(w, k, block, to, src=None):
            dst = slot(w, *block)
            return pltpu.make_async_remote_copy(
                src_ref=dst if src is None else src, dst_ref=dst, send_sem=send_sems.at[7 * w + k],
                recv_sem=recv_sems.at[7 * w + k], device_id=to, device_id_type=pl.DeviceIdType.MESH)

        mines = [pltpu.make_async_copy(x_refs[w], slot(w, *me), local_sems.at[w]) for w in range(nw)]
        for cp in mines:
            cp.start()
        first = [copy(w, 0, me, sibling, src=x_refs[w]) for w in range(nw)]
        first += [copy(w, 1 + j, me, (*chip, c), src=x_refs[w]) for w in range(nw) for j, chip in enumerate(chips)]
        for cp in first:
            cp.start()
        passed = []
        for w in range(nw):
            for j, chip in enumerate(chips):
                copy(w, 1 + j, (*chip, c), me).wait_recv()
                fwd = copy(w, 4 + j, (*chip, c), sibling)
                fwd.start()
                passed.append(fwd)
        for w in range(nw):
            copy(w, 0, sibling, me).wait_recv()
            for j, chip in enumerate(chips):
                copy(w, 4 + j, (*chip, 1 - c), me).wait_recv()
        for cp in first + passed:
            cp.wait_send()
        for cp in mines:
            cp.wait()

    hbm = pl.BlockSpec(memory_space=pl.ANY)
    return pl.pallas_call(
        body, name=name, out_shape=[SDS((N_DEV,) + s.shape, s.dtype) for s in shards],
        in_specs=[hbm] * nw, out_specs=[hbm] * nw,
        scratch_shapes=[pltpu.SemaphoreType.DMA((7 * nw,)), pltpu.SemaphoreType.DMA((7 * nw,)), pltpu.SemaphoreType.DMA((nw,))],
    )(*shards)


_HBM = pl.BlockSpec(memory_space=pltpu.HBM)
_SEM = pl.BlockSpec(memory_space=pltpu.SEMAPHORE)
_DATAFLOW = pltpu.SideEffectType.DATAFLOW_SIDE_EFFECTING


def _spread_start_call(srcs, per_peer, name, after):
    nw = len(srcs)
    lands = [lax.empty((N_DEV,) + (s.shape[1:] if per_peer else s.shape), s.dtype) for s in srcs]

    def body(*refs):
        src_refs, land_refs = refs[:nw], refs[nw:2 * nw]
        send_sems, recv_sems, token = refs[2 * nw + 1], refs[2 * nw + 2], refs[-1]
        x, y, c = lax.axis_index("x"), lax.axis_index("y"), lax.axis_index("c")
        me = 4 * x + 2 * y + c
        for w in range(nw):
            for k in range(1, N_DEV):
                px = 1 - x if k & 4 else x
                py = 1 - y if k & 2 else y
                pc = 1 - c if k & 1 else c
                src = src_refs[w].at[4 * px + 2 * py + pc] if per_peer else src_refs[w]
                pltpu.make_async_remote_copy(
                    src_ref=src, dst_ref=land_refs[w].at[me], send_sem=send_sems.at[w], recv_sem=recv_sems.at[w],
                    device_id=(px, py, pc), device_id_type=pl.DeviceIdType.MESH).start()
        token[...] = jnp.zeros_like(token)

    hbm = lambda a: pltpu.with_memory_space_constraint(a, pltpu.HBM)
    res = pl.pallas_call(
        body, name=name,
        out_shape=(pltpu.SemaphoreType.DMA((nw,)), pltpu.SemaphoreType.DMA((nw,)))
        + tuple(pltpu.HBM(s.shape, s.dtype) for s in srcs) + tuple(pltpu.HBM(l.shape, l.dtype) for l in lands)
        + (SDS((SUBLANES, LANES), F32),),
        in_specs=[_HBM] * (2 * nw) + [pl.BlockSpec(memory_space=pl.ANY)],
        out_specs=(_SEM, _SEM) + (_HBM,) * (2 * nw) + (pl.BlockSpec(memory_space=pltpu.VMEM),),
        input_output_aliases={i: i + 2 for i in range(2 * nw)},
        compiler_params=pltpu.CompilerParams(has_side_effects=_DATAFLOW),
    )(*[hbm(s) for s in srcs], *[hbm(l) for l in lands], after)
    return res[0], res[1], res[2:2 + nw], res[2 + nw:2 + 2 * nw], res[-1]


def _spread_wait_call(send_sems, recv_sems, srcs, lands, after, name):
    nw = len(lands)

    def body(*refs):
        land_refs = refs[nw:2 * nw]
        s_sems, r_sems = refs[2 * nw], refs[2 * nw + 1]
        x, y, c = lax.axis_index("x"), lax.axis_index("y"), lax.axis_index("c")
        for w in range(nw):
            seven = land_refs[w].at[pl.ds(0, N_DEV - 1)]
            all_seven = pltpu.make_async_remote_copy(
                src_ref=seven, dst_ref=seven, send_sem=s_sems.at[w], recv_sem=r_sems.at[w],
                device_id=(x, y, c), device_id_type=pl.DeviceIdType.MESH)
            all_seven.wait_send()
            all_seven.wait_recv()

    res = pl.pallas_call(
        body, name=name,
        out_shape=tuple(pltpu.HBM(s.shape, s.dtype) for s in srcs) + tuple(pltpu.HBM(l.shape, l.dtype) for l in lands),
        in_specs=[_HBM] * (2 * nw) + [_SEM, _SEM, pl.BlockSpec(memory_space=pl.ANY)], out_specs=(_HBM,) * (2 * nw),
        input_output_aliases={i: i for i in range(2 * nw)},
        compiler_params=pltpu.CompilerParams(has_side_effects=_DATAFLOW),
    )(*srcs, *lands, send_sems, recv_sems, after)
    return res[:nw], res[nw:]


def _join_cols_call(w8, name):
    _, k, n = w8.shape
    tk = _tile(k, 256)

    def body(w_ref, o_ref):
        for s in range(N_DEV):
            o_ref[:, n * s:n * (s + 1)] = w_ref[s]

    return pl.pallas_call(body, name=name, grid=(k // tk,), in_specs=[pl.BlockSpec((N_DEV, tk, n), lambda i: (0, i, 0))],
                          out_specs=pl.BlockSpec((tk, N_DEV * n), lambda i: (i, 0)), out_shape=SDS((k, N_DEV * n), w8.dtype),
                          compiler_params=_params(("parallel",), VMEM_MID))(w8)


def _split_cols_call(g, name, dtype):
    k, n8 = g.shape
    n = n8 // N_DEV
    tk = _tile(k, 256)

    def body(g_ref, o_ref):
        for s in range(N_DEV):
            o_ref[s] = g_ref[:, n * s:n * (s + 1)].astype(dtype)

    return pl.pallas_call(body, name=name, grid=(k // tk,), in_specs=[pl.BlockSpec((tk, n8), lambda i: (i, 0))],
                          out_specs=pl.BlockSpec((N_DEV, tk, n), lambda i: (0, i, 0)), out_shape=SDS((N_DEV, k, n), dtype),
                          compiler_params=_params(("parallel",), VMEM_MID))(g)


def _pack(parts, rows_multiple):
    flat = jnp.concatenate([p.reshape(-1) for p in parts])
    unit = rows_multiple * LANES
    padded = -(-flat.shape[0] // unit) * unit
    flat = jnp.concatenate([flat, jnp.zeros((padded - flat.shape[0],), F32)])
    return flat.reshape(-1, LANES)


def _groups_last(a):
    x, y = a.shape[-2:]
    return jnp.transpose(a.reshape(S5_GROUPS, x, y), (1, 2, 0)).reshape(x * y, S5_GROUPS)


def _groups_first(a, shape):
    x, y = shape[-2:]
    return jnp.transpose(a.reshape(x, y, S5_GROUPS), (2, 0, 1)).reshape(shape)


def _unpack(buf, shapes):
    flat = buf.reshape(-1)
    out, off = [], 0
    for s in shapes:
        n = math.prod(s)
        out.append(flat[off:off + n].reshape(s))
        off += n
    return out


def _row_tile(L):
    return 256 if L % 256 == 0 else L


def _layer0_mix(diff, const):
    x, mod, norm_w, lam_re, lam_im, log_dt, b_re, b_im, c_re, c_im, s5_d, *slots = diff
    ada_b, weights = const
    L = x.shape[0]
    tm = _row_tile(L)
    mods = mod.reshape(2, 1, D_MODEL)
    biases = ada_b.reshape(2, 1, D_MODEL)
    op_ln0 = make_rowwise(_f_lnmod, "ln0", tm, 1, 5, pass_first=True)
    h, x = op_ln0((x,), (norm_w.reshape(1, D_MODEL), mods[1], mods[0], biases[1], biases[0]))
    u, z = make_proj("s5_in")(h, tuple(weights), tuple(slots))
    blocks = _s5_block_params(lam_re, lam_im, log_dt, b_re, b_im, c_re, c_im)
    y2 = make_s5_core(min(S5_TL, L))(u, *blocks, s5_d.reshape(1, D_INNER))
    return x, y2, z


def _glu_gate_fwd_call(y2, w, z, name):
    m, k = y2.shape
    n = w.shape[1]
    tm, tn = _tile(m, 1024), _tile(n, 512)

    def body(a_ref, w_ref, z_ref, t_ref, y4_ref):
        j = pl.program_id(1)
        t = _bdot(a_ref[...], w_ref[...])
        t_ref[...] = t
        y2_tile = a_ref[:, pl.ds(pl.multiple_of(j * tn, LANES), tn)]
        y4_ref[...] = _f_s5_gate(y2_tile, t, z_ref[...])[0]

    tile = pl.BlockSpec((tm, tn), lambda i, j: (i, j))
    return pl.pallas_call(
        body, name=name, grid=(m // tm, n // tn),
        in_specs=[pl.BlockSpec((tm, k), lambda i, j: (i, 0)), pl.BlockSpec((k, tn), lambda i, j: (0, j)), tile],
        out_specs=[tile, tile], out_shape=[SDS((m, n), F32), SDS((m, n), F32)],
        compiler_params=_params(("parallel", "parallel"), VMEM_MID),
    )(y2, w, z)


def make_glu_gate(name, tm):
    gate = make_rowwise(_f_s5_gate, name + "_gate", tm, 3, 0)

    @jax.custom_vjp
    def op(y2, w, grad_slot, z):
        return _glu_gate_fwd_call(y2, w, z, name + "_fwd")[1]

    def fwd(y2, w, grad_slot, z):
        t, y4 = _glu_gate_fwd_call(y2, w, z, name + "_fwd")
        return y4, (y2, w, t, z)

    def bwd(res, dy4):
        y2, w, t, z = res
        (dy2_gate, dt, dz), _ = gate.run_bwd((y2, t, z), (), (dy4,))
        return _matmul(dt, w, "nt", name + "_dx", add=dy2_gate), jnp.zeros_like(w), _matmul(y2, dt, "tn", name + "_dw"), dz

    op.defvjp(fwd, bwd)
    return op


def _layer0_out(diff, weights):
    y2, z, *slots = diff
    y4 = make_glu_gate("s5_glu", _row_tile(y2.shape[0]))(y2, weights[0], slots[0], z)
    return make_mm("s5_out")(y4, weights[1], slots[1])


def _f_res_lnmod(x, o, gate, bgate, nw, sc, sh, bsc, bsh):
    (x1,) = _f_res(x, o, gate, bgate)
    return _f_lnmod(x1, nw, sc, sh, bsc, bsh) + (x1,)


def _f_res_loss(x, y, tgt, gate, bgate, fw):
    return _f_loss(_f_res(x, y, gate, bgate)[0], tgt, fw)


def _layer1_loss(diff, const):
    x, o, gate0, mod, norm_w, conv_w, a_log, dt_bias, gdn_nw, final_nw, *slots = diff
    tgt, bgate0, ada_b, weights = const
    L = x.shape[0]
    tm = _row_tile(L)
    mods = mod.reshape(3, 1, D_MODEL)
    biases = ada_b.reshape(3, 1, D_MODEL)
    h, x1 = make_rowwise(_f_res_lnmod, "res0_ln1", tm, 2, 7)(
        (x, o), (gate0.reshape(1, D_MODEL), bgate0.reshape(1, D_MODEL), norm_w.reshape(1, D_MODEL), mods[1], mods[0], biases[1], biases[0]))
    q0, k0, v0, gz, ba = make_proj("gdn_in", w_rows=True)(h, tuple(weights[0:5]), tuple(slots[0:5]))
    cw = jnp.concatenate([conv_w, jnp.zeros((SUBLANES - GDN_CONV, GDN_CONV_CH), F32)], axis=0)
    q = make_conv_act(lambda t: _l2n(_silu(t)) * (GDN_DK ** -0.5), "gdn_conv_q")(q0, cw[:, :GDN_QK])
    k = make_conv_act(lambda t: _l2n(_silu(t)), "gdn_conv_k")(k0, cw[:, GDN_QK:2 * GDN_QK])
    v = make_conv_act(_silu, "gdn_conv_v")(v0, cw[:, 2 * GDN_QK:])
    pad = jnp.zeros((LANES - 2 * GDN_HEADS,), F32)
    alog_row = jnp.concatenate([jnp.zeros((GDN_HEADS,), F32), a_log, pad]).reshape(1, LANES)
    dtb_row = jnp.concatenate([jnp.zeros((GDN_HEADS,), F32), dt_bias, pad]).reshape(1, LANES)
    (bg,) = make_rowwise(_f_betag, "gdn_bg", tm, 1, 2)((ba,), (alog_row, dtb_row))
    nw_row = jnp.tile(gdn_nw, GDN_HEADS).reshape(1, D_INNER)
    on = gdn_scan(*gdn_prep(q, k, v, bg), gz, nw_row)
    y = make_mm("gdn_out")(on, weights[5], slots[5])
    (lt,) = make_rowwise(_f_res_loss, "res1_loss", tm, 3, 3)((x1, y, tgt), (mods[2], biases[2], final_nw.reshape(1, D_MODEL)))
    return jnp.sum(lt)


VEC_NAMES = ("ada_b", "norm_w", "s5_lambda_re", "s5_lambda_im", "s5_log_dt", "s5_d", "gdn_a_log", "gdn_dt_bias", "final_norm_w")
MAT_NAMES = ("s5_b_re", "s5_b_im", "s5_c_re", "s5_c_im")
S5_BIG = ("s5_w_in", "s5_w_glu", "s5_w_out")
GDN_BIG = ("gdn_w_in", "gdn_w_out")
BIG_NAMES = S5_BIG + GDN_BIG
WEIGHT_ORDER = ("ada_w", "ada_b", "norm_w", "s5_w_in", "s5_lambda_re", "s5_lambda_im", "s5_log_dt", "s5_b_re", "s5_b_im",
                "s5_c_re", "s5_c_im", "s5_d", "s5_w_glu", "s5_w_out", "gdn_w_in", "gdn_conv_w", "gdn_a_log", "gdn_dt_bias",
                "gdn_norm_w", "gdn_w_out", "final_norm_w")


def _step(x, c, W, M, V, tgt):
    L = x.shape[1]
    ix, iy, ic = lax.axis_index("x"), lax.axis_index("y"), lax.axis_index("c")
    me = 4 * ix + 2 * iy + ic
    n_ada = W["ada_w"].shape[2]
    n_conv = W["gdn_conv_w"].shape[2]
    n_gnw = W["gdn_norm_w"].shape[1]

    g1 = _allgather_call(_pack([c, W["gdn_conv_w"], W["gdn_norm_w"]], SUBLANES), "gather_small_in")
    g1 = g1.reshape(N_DEV, -1)
    c_all = g1[:, :D_MODEL]
    conv_w = g1[:, D_MODEL:D_MODEL + GDN_CONV * n_conv].reshape(N_DEV, GDN_CONV, n_conv).transpose(1, 0, 2).reshape(GDN_CONV, -1)
    gdn_nw = g1[:, D_MODEL + GDN_CONV * n_conv:D_MODEL + GDN_CONV * n_conv + n_gnw].reshape(-1)
    mod_part = _ada_mod_call(c_all, W["ada_w"])
    g2 = _allgather_call(_pack([mod_part], SUBLANES), "gather_mod").reshape(N_DEV, -1)
    mod_all = g2[:, :2 * N_DEV * n_ada].reshape(N_DEV, 2, N_DEV, n_ada)
    mod_raw = lax.dynamic_index_in_dim(mod_all, me, axis=2, keepdims=False)
    mod_raw = mod_raw.transpose(1, 0, 2).reshape(2, 3 * D_MODEL)

    shard = lambda n: W[n][0].astype(BF16)
    (w_in5_parts,) = _gather_weights_call([shard("s5_w_in")], "gather_s5_w_in")
    late = _spread_start_call([shard("s5_w_glu"), shard("s5_w_out")], False, "gather_s5_late_start", w_in5_parts)
    turned = lambda a: jnp.transpose(a[0])
    g_send, g_recv, g_srcs, g_lands, g_token = _spread_start_call(
        [turned(W["gdn_w_in"]).astype(BF16), shard("gdn_w_out")], False, "gather_gdn_start", late[4])
    w_in5 = _join_cols_call(w_in5_parts, "join_s5_w_in")
    slot = lambda *s: jnp.zeros(s, F32)
    two = 2 * D_MODEL
    diff_mix = (x[0], mod_raw[0, :two] + g_token[0, 0], W["norm_w"][0], W["s5_lambda_re"][0], W["s5_lambda_im"][0], W["s5_log_dt"][0],
                W["s5_b_re"][0], W["s5_b_im"][0], W["s5_c_re"][0], W["s5_c_im"][0], W["s5_d"][0],
                slot(D_MODEL, D_INNER), slot(D_MODEL, D_INNER))

    (xp, y2, z5), vjp_mix = jax.vjp(lambda d: _layer0_mix(d, (W["ada_b"][0, :two], (w_in5[:, :D_INNER], w_in5[:, D_INNER:]))), diff_mix)
    l_srcs, l_lands = _spread_wait_call(late[0], late[1], late[2], late[3], y2, "gather_s5_late_wait")
    w_glu, w_o5 = [lax.dynamic_update_slice(land, src[None], (me, 0, 0)).reshape(-1, src.shape[1]) for land, src in zip(l_lands, l_srcs)]
    diff_out = (y2, z5, slot(D_INNER, D_INNER), slot(D_INNER, D_MODEL))
    o5, vjp_out = jax.vjp(lambda d: _layer0_out(d, (w_glu, w_o5)), diff_out)
    g_srcs, g_lands = _spread_wait_call(g_send, g_recv, g_srcs, g_lands, o5, "gather_gdn_wait")
    gdn_full = [lax.dynamic_update_slice(land, src[None], (me, 0, 0)) for land, src in zip(g_lands, g_srcs)]
    w_ing = gdn_full[0].reshape(GDN_PROJ, D_MODEL)
    w_ba = jnp.concatenate([w_ing[GDN_CONV_CH + D_INNER:], jnp.zeros((LANES - 2 * GDN_HEADS, D_MODEL), BF16)], axis=0)
    weights1 = (w_ing[:GDN_QK], w_ing[GDN_QK:2 * GDN_QK], w_ing[2 * GDN_QK:GDN_CONV_CH],
                w_ing[GDN_CONV_CH:GDN_CONV_CH + D_INNER], w_ba, gdn_full[1].reshape(D_INNER, D_MODEL))
    slots1 = tuple(jnp.zeros(w.shape, F32) for w in weights1)
    diff1 = (xp, o5, mod_raw[0, two:], mod_raw[1], W["norm_w"][1], conv_w, W["gdn_a_log"][0], W["gdn_dt_bias"][0], gdn_nw,
             W["final_norm_w"], *slots1)
    loss_local, vjp1 = jax.vjp(lambda d: _layer1_loss(d, (tgt[0], W["ada_b"][0, two:], W["ada_b"][1], weights1)), diff1)
    ((dxp, do5, dmod_gate, dmod1, d_norm_w1, d_conv, d_alog, d_dtb, d_gnw, d_fnw, d_wq, d_wk, d_wv, d_wgz, d_wba, d_wog),) = vjp1(
        jnp.ones((), F32))
    loss = lax.psum(loss_local, MESH_AXES)

    rows = lambda d: d.reshape(N_DEV, d.shape[0] // N_DEV, d.shape[1])
    d_ing = jnp.concatenate([d_wq, d_wk, d_wv, d_wgz, d_wba[:2 * GDN_HEADS]], axis=0).astype(BF16).reshape(N_DEV, -1, D_MODEL)
    s_send, s_recv, s_srcs, s_lands, s_token = _spread_start_call([d_ing, rows(d_wog).astype(BF16)], True, "scatter_gdn_start", dxp)
    ((dy2, dz5, d_wglu, d_wo5),) = vjp_out(do5.at[0, 0].add(s_token[0, 0]))
    t_send, t_recv, t_srcs, t_lands, t_token = _spread_start_call(
        [rows(d_wglu).astype(BF16), rows(d_wo5).astype(BF16)], True, "scatter_s5_late_start", dy2)
    ((dx, dmod_ss, d_norm_w0, d_lre, d_lim, d_logdt, d_bre, d_bim, d_cre, d_cim, d_s5d, d_wu, d_wz),) = vjp_mix(
        (dxp.at[0, 0].add(t_token[0, 0]), dy2, dz5))
    dmod = jnp.stack([jnp.concatenate([dmod_ss, dmod_gate]), dmod1])
    d_norm_w = jnp.stack([d_norm_w0, d_norm_w1])
    vec_parts = [dmod, d_norm_w, d_lre, d_lim, d_logdt, d_s5d, d_alog, d_dtb, d_fnw]
    tail_parts = [d_conv, d_gnw]
    mat_parts = [_groups_last(d) for d in (d_bre, d_bim, d_cre, d_cim)]
    n_vec = sum(math.prod(p.shape) for p in vec_parts)
    m_send, m_recv, m_srcs, m_lands, m_token = _spread_start_call(
        [_pack(vec_parts + tail_parts, ADAM_ROWS), _pack(mat_parts, SUBLANES).astype(BF16)], False, "gather_small_grads_start", dx)
    d_in5 = _split_cols_call(jnp.concatenate([d_wu.at[0, 0].add(m_token[0, 0]), d_wz], axis=1), "split_s5_w_in", BF16)
    u_send, u_recv, u_srcs, u_lands, u_token = _spread_start_call([d_in5], True, "scatter_s5_in_start", m_token)
    t_srcs, t_lands = _spread_wait_call(t_send, t_recv, t_srcs, t_lands, u_token, "scatter_s5_late_wait")
    s_srcs, s_lands = _spread_wait_call(s_send, s_recv, s_srcs, s_lands, t_lands[0], "scatter_gdn_wait")
    big = {}

    def owner_update(land, src, n):
        mine = lax.dynamic_index_in_dim(src, me, 0, keepdims=True)
        parts = lax.dynamic_update_slice(land, mine, (me, 0, 0))
        if n == "gdn_w_in":
            outs = _adam_call(parts, turned(W[n]), turned(M[n]), turned(V[n]), "adam_" + n, by_cols=True)
            return [jnp.transpose(o) for o in outs]
        return _adam_call(parts, W[n][0], M[n][0], V[n][0], "adam_" + n, rows=_tile(W[n].shape[1], 128))

    for land, src, n in zip(tuple(t_lands) + tuple(s_lands), tuple(t_srcs) + tuple(s_srcs), ("s5_w_glu", "s5_w_out") + GDN_BIG):
        big[n] = owner_update(land, src, n)

    m_srcs, m_lands = _spread_wait_call(m_send, m_recv, m_srcs, m_lands, big["gdn_w_out"][0], "gather_small_grads_wait")
    sg_vec, sg_mat = [lax.dynamic_update_slice(land, src[None], (me, 0, 0)) for land, src in zip(m_lands, m_srcs)]
    tot_vec = _sum_call(sg_vec, "sum_vec_grads", ADAM_ROWS)
    tot_mat = _sum_call(sg_mat, "sum_mat_grads", ADAM_ROWS)
    g_conv, g_gnw = _unpack(tot_vec.reshape(-1)[n_vec:], [d_conv.shape, d_gnw.shape])
    g_conv_mine = lax.dynamic_slice_in_dim(g_conv, me * n_conv, n_conv, axis=1)
    g_gnw_mine = lax.dynamic_slice_in_dim(g_gnw, me * n_gnw, n_gnw, axis=0)
    vec_names = VEC_NAMES + ("gdn_conv_w", "gdn_norm_w")
    vec_g = _pack([tot_vec.reshape(-1)[:n_vec], g_conv_mine, g_gnw_mine], ADAM_ROWS)
    vec = _adam_call(vec_g[None], _pack([W[n] for n in vec_names], ADAM_ROWS), _pack([M[n] for n in vec_names], ADAM_ROWS),
                     _pack([V[n] for n in vec_names], ADAM_ROWS), "adam_vec")
    vec = [_unpack(b, [W[n].shape for n in vec_names]) for b in vec]
    mats = []
    for name, g_mat in zip(MAT_NAMES, _unpack(tot_mat, [p.shape for p in mat_parts])):
        outs = _adam_call(g_mat[None], _groups_last(W[name]), _groups_last(M[name]), _groups_last(V[name]), "adam_" + name)
        mats.append([_groups_first(o, W[name].shape) for o in outs])

    dmod_all = sg_vec[:, :2 * 3 * D_MODEL // LANES].reshape(N_DEV, 2, N_DEV, n_ada // LANES, LANES)
    dmod_mine = lax.dynamic_index_in_dim(dmod_all, me, axis=2, keepdims=False).transpose(1, 0, 2, 3).reshape(2, N_DEV, n_ada)
    g_ada_w = _ada_grad_call(c_all, dmod_mine)
    ada = _adam_call(g_ada_w.reshape(1, -1, LANES), W["ada_w"].reshape(-1, LANES), M["ada_w"].reshape(-1, LANES),
                     V["ada_w"].reshape(-1, LANES), "adam_ada")
    u_srcs, u_lands = _spread_wait_call(u_send, u_recv, u_srcs, u_lands, ada[0], "scatter_s5_in_wait")
    big["s5_w_in"] = owner_update(u_lands[0], u_srcs[0], "s5_w_in")
    ada = [a.reshape(W["ada_w"].shape) for a in ada]

    res = {}
    for n in BIG_NAMES:
        res[n] = [o[None] for o in big[n]]
    for i, n in enumerate(vec_names):
        res[n] = [b[i] for b in vec]
    for i, n in enumerate(MAT_NAMES):
        res[n] = mats[i]
    res["ada_w"] = ada
    outs = [loss, dx[None]]
    for j in range(4):
        outs += [res[n][j] for n in WEIGHT_ORDER]
    return tuple(outs)


def kernel(x, c, ada_w, ada_b, norm_w, s5_w_in, s5_lambda_re, s5_lambda_im, s5_log_dt, s5_b_re, s5_b_im, s5_c_re, s5_c_im, s5_d, s5_w_glu, s5_w_out, gdn_w_in, gdn_conv_w, gdn_a_log, gdn_dt_bias, gdn_norm_w, gdn_w_out, final_norm_w, loss_target, m_ada_w, m_ada_b, m_norm_w, m_s5_w_in, m_s5_lambda_re, m_s5_lambda_im, m_s5_log_dt, m_s5_b_re, m_s5_b_im, m_s5_c_re, m_s5_c_im, m_s5_d, m_s5_w_glu, m_s5_w_out, m_gdn_w_in, m_gdn_conv_w, m_gdn_a_log, m_gdn_dt_bias, m_gdn_norm_w, m_gdn_w_out, m_final_norm_w, v_ada_w, v_ada_b, v_norm_w, v_s5_w_in, v_s5_lambda_re, v_s5_lambda_im, v_s5_log_dt, v_s5_b_re, v_s5_b_im, v_s5_c_re, v_s5_c_im, v_s5_d, v_s5_w_glu, v_s5_w_out, v_gdn_w_in, v_gdn_conv_w, v_gdn_a_log, v_gdn_dt_bias, v_gdn_norm_w, v_gdn_w_out, v_final_norm_w):
    W = dict(ada_w=ada_w, ada_b=ada_b, norm_w=norm_w, s5_w_in=s5_w_in, s5_lambda_re=s5_lambda_re, s5_lambda_im=s5_lambda_im,
             s5_log_dt=s5_log_dt, s5_b_re=s5_b_re, s5_b_im=s5_b_im, s5_c_re=s5_c_re, s5_c_im=s5_c_im, s5_d=s5_d,
             s5_w_glu=s5_w_glu, s5_w_out=s5_w_out, gdn_w_in=gdn_w_in, gdn_conv_w=gdn_conv_w, gdn_a_log=gdn_a_log,
             gdn_dt_bias=gdn_dt_bias, gdn_norm_w=gdn_norm_w, gdn_w_out=gdn_w_out, final_norm_w=final_norm_w)
    M = dict(ada_w=m_ada_w, ada_b=m_ada_b, norm_w=m_norm_w, s5_w_in=m_s5_w_in, s5_lambda_re=m_s5_lambda_re,
             s5_lambda_im=m_s5_lambda_im, s5_log_dt=m_s5_log_dt, s5_b_re=m_s5_b_re, s5_b_im=m_s5_b_im, s5_c_re=m_s5_c_re,
             s5_c_im=m_s5_c_im, s5_d=m_s5_d, s5_w_glu=m_s5_w_glu, s5_w_out=m_s5_w_out, gdn_w_in=m_gdn_w_in,
             gdn_conv_w=m_gdn_conv_w, gdn_a_log=m_gdn_a_log, gdn_dt_bias=m_gdn_dt_bias, gdn_norm_w=m_gdn_norm_w,
             gdn_w_out=m_gdn_w_out, final_norm_w=m_final_norm_w)
    V = dict(ada_w=v_ada_w, ada_b=v_ada_b, norm_w=v_norm_w, s5_w_in=v_s5_w_in, s5_lambda_re=v_s5_lambda_re,
             s5_lambda_im=v_s5_lambda_im, s5_log_dt=v_s5_log_dt, s5_b_re=v_s5_b_re, s5_b_im=v_s5_b_im, s5_c_re=v_s5_c_re,
             s5_c_im=v_s5_c_im, s5_d=v_s5_d, s5_w_glu=v_s5_w_glu, s5_w_out=v_s5_w_out, gdn_w_in=v_gdn_w_in,
             gdn_conv_w=v_gdn_conv_w, gdn_a_log=v_gdn_a_log, gdn_dt_bias=v_gdn_dt_bias, gdn_norm_w=v_gdn_norm_w,
             gdn_w_out=v_gdn_w_out, final_norm_w=v_final_norm_w)
    return _step(x, c, W, M, V, loss_target)
```

```python
import functools
import math

import jax
import jax.numpy as jnp
from jax import lax
from jax.experimental import pallas as pl
from jax.experimental.pallas import tpu as pltpu

F32 = jnp.float32
BF16 = jnp.bfloat16
SDS = jax.ShapeDtypeStruct

D_MODEL = 1024
D_INNER = 2048
NORM_EPS = 1e-6
S5_GROUP = 16
S5_GROUPS = 128
S5_STATE = 64
GDN_HEADS = 8
GDN_DK = 128
GDN_DV = 256
GDN_CONV = 4
GDN_CHUNK = 64
GDN_QK = 1024
GDN_CONV_CH = 4096
GDN_PROJ = 6160
ADAM_LR = 0.001
ADAM_B1 = 0.9
ADAM_B2 = 0.999
ADAM_EPS = 1e-08
ADAM_WD = 0.01
ADAM_STEP = 10

N_DEV = 8
LANES = 128
SUBLANES = 8
VMEM_BIG = 56 << 20
VMEM_MID = 40 << 20
S5_GB = 8
S5_BW = S5_GB * S5_GROUP
S5_TL = 2048
MESH_AXES = ("x", "y", "c")


def _params(sem, vmem=None):
    return pltpu.CompilerParams(dimension_semantics=sem, vmem_limit_bytes=vmem)


def _bdot(a, b, dims=(((1,), (0,)), ((), ()))):
    return lax.dot_general(a.astype(BF16), b.astype(BF16), dims, preferred_element_type=F32)


def _hdot(a, b, dims=(((1,), (0,)), ((), ()))):
    return lax.dot_general(a, b, dims, preferred_element_type=F32, precision=lax.Precision.HIGHEST)


_BNN = (((2,), (1,)), ((0,), (0,)))
_BNT = (((2,), (2,)), ((0,), (0,)))
_BTN = (((1,), (1,)), ((0,), (0,)))


@jax.custom_vjp
def _unit_lower_inverse(a):
    c = a.shape[-1]
    ri = lax.broadcasted_iota(jnp.int32, a.shape, 1)
    ci = lax.broadcasted_iota(jnp.int32, a.shape, 2)
    n = -a
    t = (ri == ci).astype(F32) + n
    for _ in range(int(math.log2(c)) - 1):
        n = _hdot(n, n, _BNN)
        t = t + _hdot(t, n, _BNN)
    return t


def _unit_lower_inverse_fwd(a):
    t = _unit_lower_inverse(a)
    return t, t


def _unit_lower_inverse_bwd(t, g):
    return (-_hdot(_hdot(t, g, _BTN), t, _BNT),)


_unit_lower_inverse.defvjp(_unit_lower_inverse_fwd, _unit_lower_inverse_bwd)


NN = (((1,), (0,)), ((), ()))
NT = (((1,), (1,)), ((), ()))
TN = (((0,), (0,)), ((), ()))


def _tile(n, pref):
    for t in (pref, 512, 256, 128):
        if t <= n and n % t == 0:
            return t
    return n


def _matmul(a, b, mode, name, add=None):
    if mode == "nn":
        (m, k), (_, n) = a.shape, b.shape
    elif mode == "nt":
        (m, k), (n, _) = a.shape, b.shape
    else:
        (k, m), (_, n) = a.shape, b.shape
    tm, tn, tk = _tile(m, 1024), _tile(n, 512), (k if k <= 2048 else _tile(k, 512))
    if mode == "tn":
        tm, tn, tk = _tile(m, 1024), _tile(n, 1024), _tile(k, 2048)
    nk = k // tk
    dims = {"nn": NN, "nt": NT, "tn": TN}[mode]

    def body(a_ref, b_ref, *rest):
        o_ref, acc_ref = rest[-2], rest[-1]
        part = _bdot(a_ref[...], b_ref[...], dims)
        if nk == 1:
            o_ref[...] = part if add is None else part + rest[0][...]
            return
        kk = pl.program_id(2)

        @pl.when(kk == 0)
        def _():
            acc_ref[...] = part if add is None else part + rest[0][...]

        @pl.when(kk > 0)
        def _():
            acc_ref[...] += part

        @pl.when(kk == nk - 1)
        def _():
            o_ref[...] = acc_ref[...]

    a_spec = pl.BlockSpec((tk, tm), lambda i, j, q: (q, i)) if mode == "tn" else pl.BlockSpec((tm, tk), lambda i, j, q: (i, q))
    b_spec = pl.BlockSpec((tn, tk), lambda i, j, q: (j, q)) if mode == "nt" else pl.BlockSpec((tk, tn), lambda i, j, q: (q, j))
    o_spec = pl.BlockSpec((tm, tn), lambda i, j, q: (i, j))
    return pl.pallas_call(
        body, name=name, grid=(m // tm, n // tn, nk),
        in_specs=[a_spec, b_spec] + ([] if add is None else [o_spec]), out_specs=o_spec,
        out_shape=SDS((m, n), F32), scratch_shapes=[pltpu.VMEM((tm, tn), F32)],
        compiler_params=_params(("parallel", "parallel", "arbitrary"), VMEM_BIG if mode == "tn" else VMEM_MID),
    )(a, b, *([] if add is None else [add]))


def make_mm(name, pass_input=False):
    def primal(a, w):
        out = _matmul(a, w, "nn", name + "_fwd")
        return (out, a) if pass_input else out

    @jax.custom_vjp
    def mm(a, w, grad_slot):
        return primal(a, w)

    def fwd(a, w, grad_slot):
        return primal(a, w), (a, w)

    def bwd(res, g):
        a, w = res
        g, g_other = g if pass_input else (g, None)
        return _matmul(g, w, "nt", name + "_dx", add=g_other), jnp.zeros_like(w), _matmul(a, g, "tn", name + "_dw")

    mm.defvjp(fwd, bwd)
    return mm


PROJ_ROWS = 256


def _proj_fwd_call(a, ws, name, w_rows):
    m, k = a.shape
    tm = _tile(m, PROJ_ROWS)
    nw = len(ws)
    widths = [w.shape[0] if w_rows else w.shape[1] for w in ws]

    def body(*refs):
        ab = refs[0][...].astype(BF16)
        for w_ref, o_ref in zip(refs[1:1 + nw], refs[1 + nw:]):
            o_ref[...] = lax.dot_general(ab, w_ref[...], NT if w_rows else NN, preferred_element_type=F32)

    return pl.pallas_call(
        body, name=name, grid=(m // tm,),
        in_specs=[pl.BlockSpec((tm, k), lambda i: (i, 0))] + [pl.BlockSpec(w.shape, lambda i: (0, 0)) for w in ws],
        out_specs=[pl.BlockSpec((tm, n), lambda i: (i, 0)) for n in widths],
        out_shape=[SDS((m, n), F32) for n in widths],
        compiler_params=_params(("parallel",), VMEM_BIG),
    )(a, *ws)


def _proj_dx_call(gs, ws, name, w_rows):
    m = gs[0].shape[0]
    k = ws[0].shape[1] if w_rows else ws[0].shape[0]
    tm = _tile(m, PROJ_ROWS)
    nw = len(ws)

    def body(*refs):
        acc = None
        for g_ref, w_ref in zip(refs[:nw], refs[nw:2 * nw]):
            part = _bdot(g_ref[...], w_ref[...], NN if w_rows else NT)
            acc = part if acc is None else acc + part
        refs[2 * nw][...] = acc

    return pl.pallas_call(
        body, name=name, grid=(m // tm,),
        in_specs=[pl.BlockSpec((tm, g.shape[1]), lambda i: (i, 0)) for g in gs] + [pl.BlockSpec(w.shape, lambda i: (0, 0)) for w in ws],
        out_specs=pl.BlockSpec((tm, k), lambda i: (i, 0)), out_shape=SDS((m, k), F32),
        compiler_params=_params(("parallel",), VMEM_BIG),
    )(*gs, *ws)


def make_proj(name, w_rows=False):
    @jax.custom_vjp
    def proj(a, ws, grad_slots):
        return tuple(_proj_fwd_call(a, ws, name + "_fwd", w_rows))

    def fwd(a, ws, grad_slots):
        return tuple(_proj_fwd_call(a, ws, name + "_fwd", w_rows)), (a, ws)

    def bwd(res, gs):
        a, ws = res
        dws = tuple(_matmul(g, a, "tn", "%s_dw%d" % (name, i)) if w_rows else _matmul(a, g, "tn", "%s_dw%d" % (name, i))
                    for i, g in enumerate(gs))
        return _proj_dx_call(tuple(gs), ws, name + "_dx", w_rows), tuple(jnp.zeros_like(w) for w in ws), dws

    proj.defvjp(fwd, bwd)
    return proj


def make_rowwise(f, name, tm, n_rows, n_params, vmem=VMEM_MID, pass_first=False):
    def specs_of(arrs, blocked):
        if blocked:
            return [pl.BlockSpec((tm, a.shape[1]), lambda i: (i, 0)) for a in arrs]
        return [pl.BlockSpec(a.shape, lambda i: (0, 0)) for a in arrs]

    def out_structs(rows, params):
        blk = [SDS((tm, r.shape[1]), r.dtype) for r in rows] + [SDS(p.shape, p.dtype) for p in params]
        return jax.eval_shape(f, *blk)

    def run_fwd(rows, params):
        L = rows[0].shape[0]
        outs = out_structs(rows, params)

        def body(*refs):
            ins = [r[...] for r in refs[:n_rows + n_params]]
            res = f(*ins)
            for o_ref, val in zip(refs[n_rows + n_params:], res):
                o_ref[...] = val

        return pl.pallas_call(
            body, name=name + "_fwd", grid=(L // tm,),
            in_specs=specs_of(rows, True) + specs_of(params, False),
            out_specs=[pl.BlockSpec((tm, o.shape[1]), lambda i: (i, 0)) for o in outs],
            out_shape=[SDS((L, o.shape[1]), o.dtype) for o in outs],
            compiler_params=_params(("parallel",), vmem),
        )(*rows, *params)

    def run_bwd(rows, params, gs):
        L = rows[0].shape[0]
        n_g = len(gs)

        def body(*refs):
            i = pl.program_id(0)
            ins = [r[...] for r in refs[:n_rows + n_params]]
            cts = tuple(r[...] for r in refs[n_rows + n_params:n_rows + n_params + n_g])
            outs = refs[n_rows + n_params + n_g:]
            _, vjp = jax.vjp(f, *ins)
            grads = vjp(cts[:-1] if pass_first else cts)
            if pass_first:
                grads = (grads[0] + cts[-1],) + tuple(grads[1:])
            for o_ref, val in zip(outs[:n_rows], grads[:n_rows]):
                o_ref[...] = val

            if n_params:
                @pl.when(i == 0)
                def _():
                    for o_ref in outs[n_rows:]:
                        o_ref[...] = jnp.zeros_like(o_ref)
                for o_ref, val in zip(outs[n_rows:], grads[n_rows:]):
                    o_ref[...] += val

        res = pl.pallas_call(
            body, name=name + "_bwd", grid=(L // tm,),
            in_specs=specs_of(rows, True) + specs_of(params, False) + specs_of(gs, True),
            out_specs=specs_of(rows, True) + specs_of(params, False),
            out_shape=[SDS(r.shape, r.dtype) for r in rows] + [SDS(p.shape, p.dtype) for p in params],
            compiler_params=_params(("arbitrary",), vmem),
        )(*rows, *params, *gs)
        return tuple(res[:n_rows]), tuple(res[n_rows:])

    def outputs(rows, params):
        outs = tuple(run_fwd(rows, params))
        return outs + (rows[0],) if pass_first else outs

    @jax.custom_vjp
    def op(rows, params):
        return outputs(rows, params)

    def fwd(rows, params):
        return outputs(rows, params), (rows, params)

    def bwd(res, gs):
        rows, params = res
        return run_bwd(rows, params, tuple(gs))

    op.defvjp(fwd, bwd)
    op.run_fwd, op.run_bwd = run_fwd, run_bwd
    return op


def _s5_scan_rows(xr_ref, xi_ref, ar, ai, x0r, x0i, tl, reverse=False):
    n = xr_ref.shape[1]
    T = SUBLANES
    row = lax.broadcasted_iota(jnp.int32, (T, n), 0)
    pr, pi = [ar], [ai]
    for _ in range(T - 1):
        pr, pi = pr + [pr[-1] * ar - pi[-1] * ai], pi + [pr[-1] * ai + pi[-1] * ar]
    levels = []
    for d in (1, 2, 4):
        mask = (row < T - d) if reverse else (row >= d)
        levels.append((T - d if reverse else d, jnp.where(mask, pr[d - 1], 0.0), jnp.where(mask, pi[d - 1], 0.0)))
    cr = jnp.zeros((T, n), F32)
    ci = jnp.zeros((T, n), F32)
    for r in range(T):
        k = (T - r) if reverse else (r + 1)
        cr = jnp.where(row == r, pr[k - 1], cr)
        ci = jnp.where(row == r, pi[k - 1], ci)
    nt = tl // T
    last = 0 if reverse else T - 1

    def step(t, carry):
        sr, si = carry
        base = pl.multiple_of((nt - 1 - t if reverse else t) * T, T)
        br = xr_ref[pl.ds(base, T), :]
        bi = xi_ref[pl.ds(base, T), :]
        for shift, mr, mi in levels:
            qr = pltpu.roll(br, shift, 0)
            qi = pltpu.roll(bi, shift, 0)
            br, bi = br + (mr * qr - mi * qi), bi + (mr * qi + mi * qr)
        xr = br + (cr * sr - ci * si)
        xi = bi + (cr * si + ci * sr)
        xr_ref[pl.ds(base, T), :] = xr
        xi_ref[pl.ds(base, T), :] = xi
        return xr[last:last + 1, :], xi[last:last + 1, :]
    return lax.fori_loop(0, nt, step, (x0r, x0i))


def _s5_fwd_call(u, bre, bim, cre, cim, a, d, tl):
    L, e = u.shape
    nb = e // S5_BW
    ns = bre.shape[2]
    nc = L // tl

    def body(u_ref, bre_ref, bim_ref, cre_ref, cim_ref, a_ref, d_ref, y_ref, xb_ref, sr_ref, si_ref, xr_ref, xi_ref, carry_ref):
        c = pl.program_id(1)

        @pl.when(c == 0)
        def _():
            carry_ref[...] = jnp.zeros_like(carry_ref)
        xb_ref[0, 0] = carry_ref[...]
        ub = u_ref[...]
        xr_ref[...] = _bdot(ub, bre_ref[0])
        xi_ref[...] = _bdot(ub, bim_ref[0])
        ar = a_ref[0, 0:1, :]
        ai = a_ref[0, 1:2, :]
        xr, xi = _s5_scan_rows(xr_ref, xi_ref, ar, ai, carry_ref[0:1, :], carry_ref[1:2, :], tl)
        carry_ref[0:1, :] = xr
        carry_ref[1:2, :] = xi
        sr = xr_ref[...].astype(BF16)
        si = xi_ref[...].astype(BF16)
        sr_ref[...] = sr
        si_ref[...] = si
        y_ref[...] = _f_s5_act(_bdot(sr, cre_ref[0]) - _bdot(si, cim_ref[0]), ub, d_ref[...])[0]

    return pl.pallas_call(
        body, name="s5_core_fwd", grid=(nb, nc),
        in_specs=[pl.BlockSpec((tl, S5_BW), lambda j, c: (c, j)),
                  pl.BlockSpec((1, S5_BW, ns), lambda j, c: (j, 0, 0)), pl.BlockSpec((1, S5_BW, ns), lambda j, c: (j, 0, 0)),
                  pl.BlockSpec((1, ns, S5_BW), lambda j, c: (j, 0, 0)), pl.BlockSpec((1, ns, S5_BW), lambda j, c: (j, 0, 0)),
                  pl.BlockSpec((1, SUBLANES, ns), lambda j, c: (j, 0, 0)), pl.BlockSpec((1, S5_BW), lambda j, c: (0, j))],
        out_specs=[pl.BlockSpec((tl, S5_BW), lambda j, c: (c, j)),
                   pl.BlockSpec((1, 1, SUBLANES, ns), lambda j, c: (j, c, 0, 0)),
                   pl.BlockSpec((tl, ns), lambda j, c: (c, j)), pl.BlockSpec((tl, ns), lambda j, c: (c, j))],
        out_shape=[SDS((L, e), F32), SDS((nb, nc, SUBLANES, ns), F32), SDS((L, nb * ns), BF16), SDS((L, nb * ns), BF16)],
        scratch_shapes=[pltpu.VMEM((tl, ns), F32), pltpu.VMEM((tl, ns), F32), pltpu.VMEM((SUBLANES, ns), F32)],
        compiler_params=_params(("arbitrary", "arbitrary"), VMEM_MID),
    )(u, bre, bim, cre, cim, a, d)


def _s5_bwd_call(u, dy2, bre, bim, cre, cim, a, d, xb, sr, si, tl):
    L, e = u.shape
    nb = e // S5_BW
    ns = bre.shape[2]
    nc = L // tl

    def body(u_ref, dy2_ref, bre_ref, bim_ref, cre_ref, cim_ref, a_ref, d_ref, xb_ref, sr_ref, si_ref,
             du_ref, dbre_ref, dbim_ref, dcre_ref, dcim_ref, da_ref, dd_ref,
             gr_ref, gi_ref, gcarry_ref):
        c = pl.program_id(1)

        @pl.when(c == 0)
        def _():
            gcarry_ref[...] = jnp.zeros_like(gcarry_ref)
            dbre_ref[...] = jnp.zeros_like(dbre_ref)
            dbim_ref[...] = jnp.zeros_like(dbim_ref)
            dcre_ref[...] = jnp.zeros_like(dcre_ref)
            dcim_ref[...] = jnp.zeros_like(dcim_ref)
            da_ref[...] = jnp.zeros_like(da_ref)
            dd_ref[...] = jnp.zeros_like(dd_ref)

        ub = u_ref[...]
        ys = _bdot(sr_ref[...], cre_ref[0]) - _bdot(si_ref[...], cim_ref[0])
        _, act_vjp = jax.vjp(lambda *t: _f_s5_act(*t)[0], ys, ub, d_ref[...])
        dy, du_skip, dd = act_vjp(dy2_ref[...])
        dd_ref[...] += dd
        ar = a_ref[0, 0:1, :]
        ai = a_ref[0, 1:2, :]
        x0r = xb_ref[0, 0, 0:1, :]
        x0i = xb_ref[0, 0, 1:2, :]
        dcre_ref[0] += _bdot(sr_ref[...], dy, TN)
        dcim_ref[0] -= _bdot(si_ref[...], dy, TN)
        gr_ref[...] = _bdot(dy, cre_ref[0], NT)
        gi_ref[...] = -_bdot(dy, cim_ref[0], NT)

        g0r, g0i = _s5_scan_rows(gr_ref, gi_ref, ar, -ai, gcarry_ref[0:1, :], gcarry_ref[1:2, :], tl, reverse=True)
        gcarry_ref[0:1, :] = g0r
        gcarry_ref[1:2, :] = g0i
        row = lax.broadcasted_iota(jnp.int32, (tl, ns), 0)
        gr = gr_ref[...]
        gi = gi_ref[...]
        xpr = jnp.where(row == 0, x0r, pltpu.roll(sr_ref[...].astype(F32), 1, 0))
        xpi = jnp.where(row == 0, x0i, pltpu.roll(si_ref[...].astype(F32), 1, 0))
        da_ref[0, 0:1, :] += jnp.sum(gr * xpr + gi * xpi, axis=0, keepdims=True)
        da_ref[0, 1:2, :] += jnp.sum(gi * xpr - gr * xpi, axis=0, keepdims=True)
        du_ref[...] = (_bdot(gr, bre_ref[0], NT) + _bdot(gi, bim_ref[0], NT)) + du_skip
        dbre_ref[0] += _bdot(ub, gr, TN)
        dbim_ref[0] += _bdot(ub, gi, TN)

    rev = lambda c: nc - 1 - c
    return pl.pallas_call(
        body, name="s5_core_bwd", grid=(nb, nc),
        in_specs=[pl.BlockSpec((tl, S5_BW), lambda j, c: (rev(c), j)), pl.BlockSpec((tl, S5_BW), lambda j, c: (rev(c), j)),
                  pl.BlockSpec((1, S5_BW, ns), lambda j, c: (j, 0, 0)), pl.BlockSpec((1, S5_BW, ns), lambda j, c: (j, 0, 0)),
                  pl.BlockSpec((1, ns, S5_BW), lambda j, c: (j, 0, 0)), pl.BlockSpec((1, ns, S5_BW), lambda j, c: (j, 0, 0)),
                  pl.BlockSpec((1, SUBLANES, ns), lambda j, c: (j, 0, 0)), pl.BlockSpec((1, S5_BW), lambda j, c: (0, j)),
                  pl.BlockSpec((1, 1, SUBLANES, ns), lambda j, c: (j, rev(c), 0, 0)),
                  pl.BlockSpec((tl, ns), lambda j, c: (rev(c), j)), pl.BlockSpec((tl, ns), lambda j, c: (rev(c), j))],
        out_specs=[pl.BlockSpec((tl, S5_BW), lambda j, c: (rev(c), j)),
                   pl.BlockSpec((1, S5_BW, ns), lambda j, c: (j, 0, 0)), pl.BlockSpec((1, S5_BW, ns), lambda j, c: (j, 0, 0)),
                   pl.BlockSpec((1, ns, S5_BW), lambda j, c: (j, 0, 0)), pl.BlockSpec((1, ns, S5_BW), lambda j, c: (j, 0, 0)),
                   pl.BlockSpec((1, SUBLANES, ns), lambda j, c: (j, 0, 0)), pl.BlockSpec((1, S5_BW), lambda j, c: (0, j))],
        out_shape=[SDS((L, e), F32), SDS(bre.shape, F32), SDS(bim.shape, F32), SDS(cre.shape, F32), SDS(cim.shape, F32),
                   SDS(a.shape, F32), SDS(d.shape, F32)],
        scratch_shapes=[pltpu.VMEM((tl, ns), F32) for _ in range(2)] + [pltpu.VMEM((SUBLANES, ns), F32)],
        compiler_params=_params(("arbitrary", "arbitrary"), VMEM_MID),
    )(u, dy2, bre, bim, cre, cim, a, d, xb, sr, si)


def make_s5_core(tl):
    @jax.custom_vjp
    def s5_core(u, bre, bim, cre, cim, a, d):
        return _s5_fwd_call(u, bre, bim, cre, cim, a, d, tl)[0]

    def fwd(u, bre, bim, cre, cim, a, d):
        y2, xb, sr, si = _s5_fwd_call(u, bre, bim, cre, cim, a, d, tl)
        return y2, (u, bre, bim, cre, cim, a, d, xb, sr, si)

    def bwd(res, dy2):
        u, bre, bim, cre, cim, a, d, xb, sr, si = res
        return tuple(_s5_bwd_call(u, dy2, bre, bim, cre, cim, a, d, xb, sr, si, tl))

    s5_core.defvjp(fwd, bwd)
    return s5_core


def _s5_block_params(lam_re, lam_im, log_dt, b_re, b_im, c_re, c_im):
    dt = jnp.exp(log_dt)[:, None]
    mag = jnp.exp(lam_re * dt)
    ab_re = mag * jnp.cos(lam_im * dt)
    ab_im = mag * jnp.sin(lam_im * dt)
    den = lam_re * lam_re + lam_im * lam_im
    nr = ab_re - 1.0
    ni = ab_im
    q_re = (nr * lam_re + ni * lam_im) / den
    q_im = (ni * lam_re - nr * lam_im) / den
    bb_re = q_re[..., None] * b_re - q_im[..., None] * b_im
    bb_im = q_re[..., None] * b_im + q_im[..., None] * b_re
    nb = S5_GROUPS // S5_GB
    eye = jnp.eye(S5_GB, dtype=F32)

    def bdiag_in(bb):
        t = bb.reshape(nb, S5_GB, S5_STATE, S5_GROUP)
        t = jnp.einsum("jgpm,gh->jgmhp", t, eye)
        return t.reshape(nb, S5_GB * S5_GROUP, S5_GB * S5_STATE)

    def bdiag_out(cc):
        t = cc.reshape(nb, S5_GB, S5_GROUP, S5_STATE)
        t = jnp.einsum("jgmp,gh->jgphm", t, eye)
        return t.reshape(nb, S5_GB * S5_STATE, S5_GB * S5_GROUP)

    a = jnp.stack([ab_re.reshape(nb, S5_GB * S5_STATE), ab_im.reshape(nb, S5_GB * S5_STATE)], axis=1)
    a = jnp.concatenate([a, jnp.zeros((nb, SUBLANES - 2, S5_GB * S5_STATE), F32)], axis=1)
    return bdiag_in(bb_re), bdiag_in(bb_im), bdiag_out(c_re), bdiag_out(c_im), a


def _shift_down(x, s, row):
    if s == 0:
        return x
    return jnp.where(row >= s, pltpu.roll(x, s, 0), 0.0)


def _shift_up(x, s, row, n):
    if s == 0:
        return x
    return jnp.where(row < n - s, pltpu.roll(x, n - s, 0), 0.0)


def _causal_conv(xv, w_ref, row):
    acc = jnp.zeros_like(xv)
    for j in range(GDN_CONV):
        acc += w_ref[j:j + 1, :] * _shift_down(xv, GDN_CONV - 1 - j, row)
    return acc


def _conv_fwd_call(x, w, act, name):
    L, ch = x.shape

    def body(x_ref, w_ref, y_ref):
        xv = x_ref[...]
        row = lax.broadcasted_iota(jnp.int32, xv.shape, 0)
        y_ref[...] = act(_causal_conv(xv, w_ref, row))

    return pl.pallas_call(
        body, name=name + "_fwd", grid=(ch // LANES,),
        in_specs=[pl.BlockSpec((L, LANES), lambda j: (0, j)), pl.BlockSpec((SUBLANES, LANES), lambda j: (0, j))],
        out_specs=pl.BlockSpec((L, LANES), lambda j: (0, j)), out_shape=SDS((L, ch), F32),
        compiler_params=_params(("parallel",), VMEM_MID),
    )(x, w)


def _conv_bwd_call(x, w, dy, act, name):
    L, ch = x.shape

    def body(x_ref, w_ref, dy_ref, dx_ref, dw_ref):
        xv = x_ref[...]
        row = lax.broadcasted_iota(jnp.int32, xv.shape, 0)
        _, act_vjp = jax.vjp(act, _causal_conv(xv, w_ref, row))
        (g,) = act_vjp(dy_ref[...])
        acc = jnp.zeros_like(xv)
        dws = []
        for j in range(GDN_CONV):
            s = GDN_CONV - 1 - j
            acc += w_ref[j:j + 1, :] * _shift_up(g, s, row, L)
            dws.append(jnp.sum(g * _shift_down(xv, s, row), axis=0, keepdims=True))
        dx_ref[...] = acc
        dw_ref[...] = jnp.concatenate(dws + [jnp.zeros((SUBLANES - GDN_CONV, LANES), F32)], axis=0)

    return pl.pallas_call(
        body, name=name + "_bwd", grid=(ch // LANES,),
        in_specs=[pl.BlockSpec((L, LANES), lambda j: (0, j)), pl.BlockSpec((SUBLANES, LANES), lambda j: (0, j)),
                  pl.BlockSpec((L, LANES), lambda j: (0, j))],
        out_specs=[pl.BlockSpec((L, LANES), lambda j: (0, j)), pl.BlockSpec((SUBLANES, LANES), lambda j: (0, j))],
        out_shape=[SDS((L, ch), F32), SDS((SUBLANES, ch), F32)],
        compiler_params=_params(("parallel",), VMEM_MID),
    )(x, w, dy)


def make_conv_act(act, name):
    @jax.custom_vjp
    def op(x, w):
        return _conv_fwd_call(x, w, act, name)

    def fwd(x, w):
        return _conv_fwd_call(x, w, act, name), (x, w)

    def bwd(res, dy):
        x, w = res
        return tuple(_conv_bwd_call(x, w, dy, act, name))

    op.defvjp(fwd, bwd)
    return op


BNN, BNT, BTN = _BNN, _BNT, _BTN
GDN_PREP_BATCH = 16


@jax.custom_vjp
def _known_inverse(a, t):
    return t


def _known_inverse_fwd(a, t):
    return t, t


def _known_inverse_bwd(t, g):
    return -_hdot(_hdot(t, g, _BTN), t, _BNT), jnp.zeros_like(t)


_known_inverse.defvjp(_known_inverse_fwd, _known_inverse_bwd)


def _gdn_prep_math(q, k, v, beta, g, t_saved=None):
    B, C = q.shape[0], q.shape[1]
    ri = lax.broadcasted_iota(jnp.int32, (B, C, C), 1)
    ci = lax.broadcasted_iota(jnp.int32, (B, C, C), 2)
    causal = ri >= ci
    strict = ri > ci
    eye = (ri == ci).astype(F32)
    gb = jnp.broadcast_to(g, (B, C, C))
    g_row = jnp.sum(gb * eye, axis=1, keepdims=True)
    gc_col = jnp.sum(jnp.where(causal, jnp.broadcast_to(g_row, (B, C, C)), 0.0), axis=2, keepdims=True)
    gc_row = jnp.sum(jnp.where(ri <= ci, gb, 0.0), axis=1, keepdims=True)
    decay = jnp.exp(jnp.where(causal, gc_col - gc_row, -jnp.inf))
    kk = _bdot(k, k, BNT)
    a_mat = jnp.where(strict, beta * kk * decay, 0.0)
    t = _unit_lower_inverse(a_mat) if t_saved is None else _known_inverse(a_mat, t_saved)
    e_gc = jnp.exp(gc_col)
    w = _hdot(t, beta * e_gc * k, BNN)
    u = _hdot(t, beta * v, BNN)
    qk = _bdot(q, k, BNT) * decay
    q_dec = q * e_gc
    g_last = gc_col[:, C - 1:C, :]
    k_dec = k * jnp.exp(g_last - gc_col)
    return q_dec, w, u, qk, k_dec, gc_col, t


def _gdn_prep_specs(L):
    C = GDN_CHUNK
    nb = min(GDN_PREP_BATCH, L // C)
    R = nb * C
    ins = [pl.BlockSpec((R, GDN_DK), lambda c, h: (c, h)), pl.BlockSpec((R, GDN_DK), lambda c, h: (c, h)),
           pl.BlockSpec((R, GDN_DV), lambda c, h: (c, h)), pl.BlockSpec((R, LANES), lambda c, h: (c, 0))]
    outs = [pl.BlockSpec((1, R, GDN_DK), lambda c, h: (h, c, 0)), pl.BlockSpec((1, R, GDN_DK), lambda c, h: (h, c, 0)),
            pl.BlockSpec((1, R, GDN_DV), lambda c, h: (h, c, 0)), pl.BlockSpec((1, R, C), lambda c, h: (h, c, 0)),
            pl.BlockSpec((1, R, GDN_DK), lambda c, h: (h, c, 0)), pl.BlockSpec((1, R, 1), lambda c, h: (h, c, 0))]
    t_spec = pl.BlockSpec((1, R, C), lambda c, h: (h, c, 0))
    shapes = [SDS((GDN_HEADS, L, GDN_DK), F32), SDS((GDN_HEADS, L, GDN_DK), F32), SDS((GDN_HEADS, L, GDN_DV), F32),
              SDS((GDN_HEADS, L, C), F32), SDS((GDN_HEADS, L, GDN_DK), F32), SDS((GDN_HEADS, L, 1), F32)]
    return ins, outs, t_spec, shapes, nb


def _chunks(x, nb):
    return x.reshape(nb, x.shape[0] // nb, x.shape[1])


def _head_columns(bg, h):
    lane = lax.broadcasted_iota(jnp.int32, bg.shape, 1)
    beta = jnp.sum(jnp.where(lane == h, bg, 0.0), axis=1, keepdims=True)
    g = jnp.sum(jnp.where(lane == h + GDN_HEADS, bg, 0.0), axis=1, keepdims=True)
    return beta, g


def _gdn_prep_fwd_call(q, k, v, bg):
    L = q.shape[0]
    ins, outs, t_spec, shapes, nb = _gdn_prep_specs(L)

    def body(q_ref, k_ref, v_ref, bg_ref, *o_refs):
        beta, g = _head_columns(bg_ref[...], pl.program_id(1))
        res = _gdn_prep_math(_chunks(q_ref[...], nb), _chunks(k_ref[...], nb), _chunks(v_ref[...], nb),
                             _chunks(beta, nb), _chunks(g, nb))
        for o_ref, val in zip(o_refs, res):
            o_ref[0] = val.reshape(val.shape[0] * val.shape[1], val.shape[2])

    return pl.pallas_call(
        body, name="gdn_prep_fwd", grid=(L // (nb * GDN_CHUNK), GDN_HEADS), in_specs=ins, out_specs=outs + [t_spec],
        out_shape=shapes + [SDS((GDN_HEADS, L, GDN_CHUNK), F32)],
        compiler_params=_params(("parallel", "parallel"), VMEM_MID),
    )(q, k, v, bg)


def _gdn_prep_bwd_call(q, k, v, bg, t, cts):
    L = q.shape[0]
    ins, outs, t_spec, _, nb = _gdn_prep_specs(L)

    def body(q_ref, k_ref, v_ref, bg_ref, t_ref, c0, c1, c2, c3, c4, c5, dq_ref, dk_ref, dv_ref, dbg_ref):
        h = pl.program_id(1)
        beta, g = _head_columns(bg_ref[...], h)
        t_saved = _chunks(t_ref[0], nb)
        _, vjp = jax.vjp(lambda *a: _gdn_prep_math(*a, t_saved=t_saved)[:6], _chunks(q_ref[...], nb), _chunks(k_ref[...], nb),
                         _chunks(v_ref[...], nb), _chunks(beta, nb), _chunks(g, nb))
        dq, dk, dv, db, dg = vjp(tuple(_chunks(c[0], nb) for c in (c0, c1, c2, c3, c4, c5)))
        flat = lambda a: a.reshape(a.shape[0] * a.shape[1], a.shape[2])
        dq_ref[...] = flat(dq)
        dk_ref[...] = flat(dk)
        dv_ref[...] = flat(dv)

        @pl.when(h == 0)
        def _():
            dbg_ref[...] = jnp.zeros_like(dbg_ref)
        lane = lax.broadcasted_iota(jnp.int32, dbg_ref.shape, 1)
        dbg_ref[...] += jnp.where(lane == h, flat(db), 0.0) + jnp.where(lane == h + GDN_HEADS, flat(dg), 0.0)

    return pl.pallas_call(
        body, name="gdn_prep_bwd", grid=(L // (nb * GDN_CHUNK), GDN_HEADS), in_specs=ins + [t_spec] + outs, out_specs=ins,
        out_shape=[SDS(q.shape, F32), SDS(k.shape, F32), SDS(v.shape, F32), SDS(bg.shape, F32)],
        compiler_params=_params(("parallel", "arbitrary"), VMEM_MID),
    )(q, k, v, bg, t, *cts)


@jax.custom_vjp
def gdn_prep(q, k, v, bg):
    return tuple(_gdn_prep_fwd_call(q, k, v, bg)[:6])


def _gdn_prep_f(q, k, v, bg):
    res = _gdn_prep_fwd_call(q, k, v, bg)
    return tuple(res[:6]), (q, k, v, bg, res[6])


def _gdn_prep_b(res, cts):
    return tuple(_gdn_prep_bwd_call(*res, tuple(cts)))


gdn_prep.defvjp(_gdn_prep_f, _gdn_prep_b)


def _gdn_step_math(q_dec, w, u, qk, k_dec, gc, z, nw, state):
    H, C = q_dec.shape[0], q_dec.shape[1]
    v_new = u - _bdot(w, state, BNN)
    o = _bdot(q_dec, state, BNN) + _bdot(qk, v_new, BNN)
    gl = gc[:, C - 1:C, :]
    new_state = jnp.exp(gl) * state + _bdot(k_dec, v_new, BTN)
    return _f_gdn_post(jnp.concatenate([o[h] for h in range(H)], axis=1), z, nw)[0], new_state


GDN_SCAN_CHUNKS = 2


def _gdn_steps_math(q_dec, w, u, qk, k_dec, gc, z, nw, state):
    outs = []
    for i in range(q_dec.shape[1] // GDN_CHUNK):
        s = slice(i * GDN_CHUNK, (i + 1) * GDN_CHUNK)
        o, state = _gdn_step_math(q_dec[:, s], w[:, s], u[:, s], qk[:, s], k_dec[:, s], gc[:, s], z[s], nw, state)
        outs.append(o)
    return jnp.concatenate(outs, axis=0), state


def _gdn_scan_specs(L, rev):
    H = GDN_HEADS
    C = GDN_CHUNK * min(GDN_SCAN_CHUNKS, L // GDN_CHUNK)
    nc = L // C
    cc = (lambda c: nc - 1 - c) if rev else (lambda c: c)
    ins = [pl.BlockSpec((H, C, GDN_DK), lambda c: (0, cc(c), 0)), pl.BlockSpec((H, C, GDN_DK), lambda c: (0, cc(c), 0)),
           pl.BlockSpec((H, C, GDN_DV), lambda c: (0, cc(c), 0)), pl.BlockSpec((H, C, GDN_CHUNK), lambda c: (0, cc(c), 0)),
           pl.BlockSpec((H, C, GDN_DK), lambda c: (0, cc(c), 0)), pl.BlockSpec((H, C, 1), lambda c: (0, cc(c), 0))]
    o_spec = pl.BlockSpec((C, H * GDN_DV), lambda c: (cc(c), 0))
    nw_spec = pl.BlockSpec((1, H * GDN_DV), lambda c: (0, 0))
    s_spec = pl.BlockSpec((1, H, GDN_DK, GDN_DV), lambda c: (cc(c), 0, 0, 0))
    return ins + [o_spec, nw_spec], o_spec, s_spec, nc


def _gdn_scan_fwd_call(q_dec, w, u, qk, k_dec, gc, z, nw):
    L = q_dec.shape[1]
    ins, o_spec, s_spec, nc = _gdn_scan_specs(L, False)

    def body(qd_ref, w_ref, u_ref, qk_ref, kd_ref, gc_ref, z_ref, nw_ref, o_ref, sin_ref, s_ref):
        c = pl.program_id(0)

        @pl.when(c == 0)
        def _():
            s_ref[...] = jnp.zeros_like(s_ref)
        st = s_ref[...]
        sin_ref[0] = st
        o, ns = _gdn_steps_math(qd_ref[...], w_ref[...], u_ref[...], qk_ref[...], kd_ref[...], gc_ref[...], z_ref[...], nw_ref[...], st)
        o_ref[...] = o
        s_ref[...] = ns

    return pl.pallas_call(
        body, name="gdn_scan_fwd", grid=(nc,), in_specs=ins, out_specs=[o_spec, s_spec],
        out_shape=[SDS((L, GDN_HEADS * GDN_DV), F32), SDS((nc, GDN_HEADS, GDN_DK, GDN_DV), F32)],
        scratch_shapes=[pltpu.VMEM((GDN_HEADS, GDN_DK, GDN_DV), F32)],
        compiler_params=_params(("arbitrary",), VMEM_MID),
    )(q_dec, w, u, qk, k_dec, gc, z, nw)


def _gdn_scan_bwd_call(q_dec, w, u, qk, k_dec, gc, z, nw, s_in, do):
    L = q_dec.shape[1]
    ins, o_spec, s_spec, nc = _gdn_scan_specs(L, True)

    def body(qd_ref, w_ref, u_ref, qk_ref, kd_ref, gc_ref, z_ref, nw_ref, sin_ref, do_ref,
             dqd_ref, dw_ref, du_ref, dqk_ref, dkd_ref, dgc_ref, dz_ref, dnw_ref, ds_ref):
        c = pl.program_id(0)

        @pl.when(c == 0)
        def _():
            ds_ref[...] = jnp.zeros_like(ds_ref)
            dnw_ref[...] = jnp.zeros_like(dnw_ref)
        _, vjp = jax.vjp(_gdn_steps_math, qd_ref[...], w_ref[...], u_ref[...], qk_ref[...], kd_ref[...], gc_ref[...],
                         z_ref[...], nw_ref[...], sin_ref[0])
        dqd, dw, du, dqk, dkd, dgc, dz, dnw, dst = vjp((do_ref[...], ds_ref[...]))
        dqd_ref[...] = dqd
        dw_ref[...] = dw
        du_ref[...] = du
        dqk_ref[...] = dqk
        dkd_ref[...] = dkd
        dgc_ref[...] = dgc
        dz_ref[...] = dz
        dnw_ref[...] += dnw
        ds_ref[...] = dst

    return pl.pallas_call(
        body, name="gdn_scan_bwd", grid=(nc,), in_specs=ins + [s_spec, o_spec], out_specs=ins,
        out_shape=[SDS(t.shape, F32) for t in (q_dec, w, u, qk, k_dec, gc, z, nw)],
        scratch_shapes=[pltpu.VMEM((GDN_HEADS, GDN_DK, GDN_DV), F32)],
        compiler_params=_params(("arbitrary",), VMEM_MID),
    )(q_dec, w, u, qk, k_dec, gc, z, nw, s_in, do)


@jax.custom_vjp
def gdn_scan(q_dec, w, u, qk, k_dec, gc, z, nw):
    return _gdn_scan_fwd_call(q_dec, w, u, qk, k_dec, gc, z, nw)[0]


def _gdn_scan_f(*args):
    o, s_in = _gdn_scan_fwd_call(*args)
    return o, (*args, s_in)


def _gdn_scan_b(res, do):
    return tuple(_gdn_scan_bwd_call(*res, do))


gdn_scan.defvjp(_gdn_scan_f, _gdn_scan_b)


def _silu(x):
    return x * jax.nn.sigmoid(x)


def _gelu_tanh(x):
    return 0.5 * x * (1.0 + jnp.tanh(math.sqrt(2.0 / math.pi) * (x + 0.044715 * (x * x * x))))


def _f_lnmod(x, nw, sc, sh, bsc, bsh):
    xn = x * lax.rsqrt(jnp.mean(x * x, axis=-1, keepdims=True) + NORM_EPS) * nw
    return (xn * (1.0 + (sc + bsc)) + (sh + bsh),)


def _f_s5_act(ys, u, d):
    return (_gelu_tanh(ys + d * u),)


def _f_s5_gate(y2, t, z):
    return (y2 * jax.nn.sigmoid(t) * _silu(z),)


def _f_res(x, y, gate, bgate):
    return (x + (gate + bgate) * y,)


def _heads(x, width, fn):
    return jnp.concatenate([fn(x[:, i * width:(i + 1) * width]) for i in range(x.shape[1] // width)], axis=1)


def _l2n(x):
    return x * lax.rsqrt(jnp.sum(x * x, axis=-1, keepdims=True) + NORM_EPS)


def _f_betag(ba, alog, dtb):
    col = lax.broadcasted_iota(jnp.int32, ba.shape, 1)
    t = ba + dtb
    softplus = jnp.maximum(t, 0.0) + jnp.log1p(jnp.exp(-jnp.abs(t)))
    g = -jnp.exp(alog) * softplus
    return (jnp.where(col < GDN_HEADS, jax.nn.sigmoid(ba), jnp.where(col < 2 * GDN_HEADS, g, 0.0)),)


def _f_gdn_post(o, z, nw):
    on = _heads(o, GDN_DV, lambda t: t * lax.rsqrt(jnp.mean(t * t, axis=-1, keepdims=True) + NORM_EPS))
    return (on * nw * _silu(z),)


def _f_loss(x, tgt, fw):
    y = x * lax.rsqrt(jnp.mean(x * x, axis=-1, keepdims=True) + NORM_EPS) * fw
    err = y - tgt
    return (0.5 * jnp.mean(err * err, axis=-1, keepdims=True),)


def _ada_mod_call(c_all, ada_w):
    n = ada_w.shape[2]

    def body(c_ref, w_ref, o_ref):
        ca = _silu(c_ref[...])
        for l in range(ada_w.shape[0]):
            o_ref[l] = _bdot(ca, w_ref[l])

    return pl.pallas_call(body, name="ada_mod", out_shape=SDS((ada_w.shape[0], N_DEV, n), F32),
                          compiler_params=_params(None, VMEM_MID))(c_all, ada_w)


def _ada_grad_call(c_all, dmod):
    nl, _, n = dmod.shape

    def body(c_ref, d_ref, o_ref):
        ca = _silu(c_ref[...])
        for l in range(nl):
            o_ref[l] = _hdot(ca, d_ref[l], TN)

    return pl.pallas_call(body, name="ada_grad", out_shape=SDS((nl, c_all.shape[1], n), F32),
                          compiler_params=_params(None, VMEM_MID))(c_all, dmod)


ADAM_ROWS = 512


def _adamw(g, w, m, v):
    m2 = ADAM_B1 * m + (1.0 - ADAM_B1) * g
    v2 = ADAM_B2 * v + (1.0 - ADAM_B2) * (g * g)
    m_hat = m2 / (1.0 - ADAM_B1 ** ADAM_STEP)
    v_hat = v2 / (1.0 - ADAM_B2 ** ADAM_STEP)
    return g, -ADAM_LR * (m_hat / (jnp.sqrt(v_hat) + ADAM_EPS) + ADAM_WD * w), m2, v2


def _adam_call(gs, w, m, v, name, rows=None, by_cols=False):
    n, r, cols = gs.shape
    if by_cols:
        blk = pl.BlockSpec((r, LANES), lambda i: (0, i))
        g_blk, grid = pl.BlockSpec((n, r, LANES), lambda i: (0, 0, i)), (cols // LANES,)
    else:
        rows = rows or ADAM_ROWS
        blk = pl.BlockSpec((rows, cols), lambda i: (i, 0))
        g_blk, grid = pl.BlockSpec((n, rows, cols), lambda i: (0, i, 0)), (r // rows,)

    def body(g_ref, w_ref, m_ref, v_ref, go_ref, d_ref, mo_ref, vo_ref):
        g = g_ref[0].astype(F32)
        for s in range(1, n):
            g = g + g_ref[s].astype(F32)
        for o_ref, val in zip((go_ref, d_ref, mo_ref, vo_ref), _adamw(g, w_ref[...], m_ref[...], v_ref[...])):
            o_ref[...] = val

    return pl.pallas_call(
        body, name=name, grid=grid, in_specs=[g_blk, blk, blk, blk],
        out_specs=[blk, blk, blk, blk], out_shape=[SDS((r, cols), F32)] * 4,
        compiler_params=_params(("parallel",), VMEM_MID),
    )(gs, w, m, v)


def _sum_call(gs, name, rows):
    n, r, _ = gs.shape

    def body(g_ref, o_ref):
        g = g_ref[0].astype(F32)
        for s in range(1, n):
            g = g + g_ref[s].astype(F32)
        o_ref[...] = g

    return pl.pallas_call(
        body, name=name, grid=(r // rows,),
        in_specs=[pl.BlockSpec((n, rows, LANES), lambda i: (0, i, 0))],
        out_specs=pl.BlockSpec((rows, LANES), lambda i: (i, 0)), out_shape=SDS((r, LANES), F32),
        compiler_params=_params(("parallel",), VMEM_MID),
    )(gs)


def _allgather_call(x_shard, name):
    m_per, n = x_shard.shape

    def body(x_ref, out_ref, send_sems, recv_sems, local_sem):
        x, y, c = lax.axis_index("x"), lax.axis_index("y"), lax.axis_index("c")
        me, sibling = (x, y, c), (x, y, 1 - c)
        chips = [(1 - x, y), (x, 1 - y), (1 - x, 1 - y)]

        def rows(px, py, pc):
            return out_ref.at[pl.ds((4 * px + 2 * py + pc) * m_per, m_per), :]

        def copy(k, block, to, src=None):
            return pltpu.make_async_remote_copy(
                src_ref=rows(*block) if src is None else src, dst_ref=rows(*block),
                send_sem=send_sems.at[k], recv_sem=recv_sems.at[k], device_id=to, device_id_type=pl.DeviceIdType.MESH)

        mine = pltpu.make_async_copy(x_ref, rows(*me), local_sem)
        mine.start()
        first = [copy(0, me, sibling, src=x_ref)]
        first += [copy(1 + j, me, (*chip, c), src=x_ref) for j, chip in enumerate(chips)]
        for cp in first:
            cp.start()
        passed = [copy(4 + j, (*chip, c), sibling) for j, chip in enumerate(chips)]
        for j, chip in enumerate(chips):
            copy(1 + j, (*chip, c), me).wait_recv()
            passed[j].start()
        copy(0, sibling, me).wait_recv()
        for j, chip in enumerate(chips):
            copy(4 + j, (*chip, 1 - c), me).wait_recv()
        for cp in first + passed:
            cp.wait_send()
        mine.wait()

    vmem = pl.BlockSpec(memory_space=pltpu.VMEM)
    return pl.pallas_call(
        body, name=name, out_shape=SDS((N_DEV * m_per, n), x_shard.dtype), in_specs=[vmem], out_specs=vmem,
        scratch_shapes=[pltpu.SemaphoreType.DMA((7,)), pltpu.SemaphoreType.DMA((7,)), pltpu.SemaphoreType.DMA],
    )(x_shard)


def _gather_weights_call(shards, name):
    nw = len(shards)

    def body(*refs):
        x_refs, out_refs = refs[:nw], refs[nw:2 * nw]
        send_sems, recv_sems, local_sems = refs[2 * nw:]
        x, y, c = lax.axis_index("x"), lax.axis_index("y"), lax.axis_index("c")
        me, sibling = (x, y, c), (x, y, 1 - c)
        chips = [(1 - x, y), (x, 1 - y), (1 - x, 1 - y)]

        def slot(w, px, py, pc):
            return out_refs[w].at[4 * px + 2 * py + pc]

        def copy(w, k, block, to, src=None):
            dst = slot(w, *block)
            return pltpu.make_async_remote_copy(
                src_ref=dst if src is None else src, dst_ref=dst, send_sem=send_sems.at[7 * w + k],
                recv_sem=recv_sems.at[7 * w + k], device_id=to, device_id_type=pl.DeviceIdType.MESH)

        mines = [pltpu.make_async_copy(x_refs[w], slot(w, *me), local_sems.at[w]) for w in range(nw)]
        for cp in mines:
            cp.start()
        first = [copy(w, 0, me, sibling, src=x_refs[w]) for w in range(nw)]
        first += [copy(w, 1 + j, me, (*chip, c), src=x_refs[w]) for w in range(nw) for j, chip in enumerate(chips)]
        for cp in first:
            cp.start()
        passed = []
        for w in range(nw):
            for j, chip in enumerate(chips):
                copy(w, 1 + j, (*chip, c), me).wait_recv()
                fwd = copy(w, 4 + j, (*chip, c), sibling)
                fwd.start()
                passed.append(fwd)
        for w in range(nw):
            copy(w, 0, sibling, me).wait_recv()
            for j, chip in enumerate(chips):
                copy(w, 4 + j, (*chip, 1 - c), me).wait_recv()
        for cp in first + passed:
            cp.wait_send()
        for cp in mines:
            cp.wait()

    hbm = pl.BlockSpec(memory_space=pl.ANY)
    return pl.pallas_call(
        body, name=name, out_shape=[SDS((N_DEV,) + s.shape, s.dtype) for s in shards],
        in_specs=[hbm] * nw, out_specs=[hbm] * nw,
        scratch_shapes=[pltpu.SemaphoreType.DMA((7 * nw,)), pltpu.SemaphoreType.DMA((7 * nw,)), pltpu.SemaphoreType.DMA((nw,))],
    )(*shards)


_HBM = pl.BlockSpec(memory_space=pltpu.HBM)
_SEM = pl.BlockSpec(memory_space=pltpu.SEMAPHORE)
_DATAFLOW = pltpu.SideEffectType.DATAFLOW_SIDE_EFFECTING


def _spread_start_call(srcs, per_peer, name, after):
    nw = len(srcs)
    lands = [lax.empty((N_DEV,) + (s.shape[1:] if per_peer else s.shape), s.dtype) for s in srcs]

    def body(*refs):
        src_refs, land_refs = refs[:nw], refs[nw:2 * nw]
        send_sems, recv_sems, token = refs[2 * nw + 1], refs[2 * nw + 2], refs[-1]
        x, y, c = lax.axis_index("x"), lax.axis_index("y"), lax.axis_index("c")
        me = 4 * x + 2 * y + c
        for w in range(nw):
            for k in range(1, N_DEV):
                px = 1 - x if k & 4 else x
                py = 1 - y if k & 2 else y
                pc = 1 - c if k & 1 else c
                src = src_refs[w].at[4 * px + 2 * py + pc] if per_peer else src_refs[w]
                pltpu.make_async_remote_copy(
                    src_ref=src, dst_ref=land_refs[w].at[me], send_sem=send_sems.at[w], recv_sem=recv_sems.at[w],
                    device_id=(px, py, pc), device_id_type=pl.DeviceIdType.MESH).start()
        token[...] = jnp.zeros_like(token)

    hbm = lambda a: pltpu.with_memory_space_constraint(a, pltpu.HBM)
    res = pl.pallas_call(
        body, name=name,
        out_shape=(pltpu.SemaphoreType.DMA((nw,)), pltpu.SemaphoreType.DMA((nw,)))
        + tuple(pltpu.HBM(s.shape, s.dtype) for s in srcs) + tuple(pltpu.HBM(l.shape, l.dtype) for l in lands)
        + (SDS((SUBLANES, LANES), F32),),
        in_specs=[_HBM] * (2 * nw) + [pl.BlockSpec(memory_space=pl.ANY)],
        out_specs=(_SEM, _SEM) + (_HBM,) * (2 * nw) + (pl.BlockSpec(memory_space=pltpu.VMEM),),
        input_output_aliases={i: i + 2 for i in range(2 * nw)},
        compiler_params=pltpu.CompilerParams(has_side_effects=_DATAFLOW),
    )(*[hbm(s) for s in srcs], *[hbm(l) for l in lands], after)
    return res[0], res[1], res[2:2 + nw], res[2 + nw:2 + 2 * nw], res[-1]


def _spread_wait_call(send_sems, recv_sems, srcs, lands, after, name):
    nw = len(lands)

    def body(*refs):
        land_refs = refs[nw:2 * nw]
        s_sems, r_sems = refs[2 * nw], refs[2 * nw + 1]
        x, y, c = lax.axis_index("x"), lax.axis_index("y"), lax.axis_index("c")
        for w in range(nw):
            seven = land_refs[w].at[pl.ds(0, N_DEV - 1)]
            all_seven = pltpu.make_async_remote_copy(
                src_ref=seven, dst_ref=seven, send_sem=s_sems.at[w], recv_sem=r_sems.at[w],
                device_id=(x, y, c), device_id_type=pl.DeviceIdType.MESH)
            all_seven.wait_send()
            all_seven.wait_recv()

    res = pl.pallas_call(
        body, name=name,
        out_shape=tuple(pltpu.HBM(s.shape, s.dtype) for s in srcs) + tuple(pltpu.HBM(l.shape, l.dtype) for l in lands),
        in_specs=[_HBM] * (2 * nw) + [_SEM, _SEM, pl.BlockSpec(memory_space=pl.ANY)], out_specs=(_HBM,) * (2 * nw),
        input_output_aliases={i: i for i in range(2 * nw)},
        compiler_params=pltpu.CompilerParams(has_side_effects=_DATAFLOW),
    )(*srcs, *lands, send_sems, recv_sems, after)
    return res[:nw], res[nw:]


def _join_cols_call(w8, name):
    _, k, n = w8.shape
    tk = _tile(k, 256)

    def body(w_ref, o_ref):
        for s in range(N_DEV):
            o_ref[:, n * s:n * (s + 1)] = w_ref[s]

    return pl.pallas_call(body, name=name, grid=(k // tk,), in_specs=[pl.BlockSpec((N_DEV, tk, n), lambda i: (0, i, 0))],
                          out_specs=pl.BlockSpec((tk, N_DEV * n), lambda i: (i, 0)), out_shape=SDS((k, N_DEV * n), w8.dtype),
                          compiler_params=_params(("parallel",), VMEM_MID))(w8)


def _split_cols_call(g, name, dtype):
    k, n8 = g.shape
    n = n8 // N_DEV
    tk = _tile(k, 256)

    def body(g_ref, o_ref):
        for s in range(N_DEV):
            o_ref[s] = g_ref[:, n * s:n * (s + 1)].astype(dtype)

    return pl.pallas_call(body, name=name, grid=(k // tk,), in_specs=[pl.BlockSpec((tk, n8), lambda i: (i, 0))],
                          out_specs=pl.BlockSpec((N_DEV, tk, n), lambda i: (0, i, 0)), out_shape=SDS((N_DEV, k, n), dtype),
                          compiler_params=_params(("parallel",), VMEM_MID))(g)


def _pack(parts, rows_multiple):
    flat = jnp.concatenate([p.reshape(-1) for p in parts])
    unit = rows_multiple * LANES
    padded = -(-flat.shape[0] // unit) * unit
    flat = jnp.concatenate([flat, jnp.zeros((padded - flat.shape[0],), F32)])
    return flat.reshape(-1, LANES)


def _groups_last(a):
    x, y = a.shape[-2:]
    return jnp.transpose(a.reshape(S5_GROUPS, x, y), (1, 2, 0)).reshape(x * y, S5_GROUPS)


def _groups_first(a, shape):
    x, y = shape[-2:]
    return jnp.transpose(a.reshape(x, y, S5_GROUPS), (2, 0, 1)).reshape(shape)


def _unpack(buf, shapes):
    flat = buf.reshape(-1)
    out, off = [], 0
    for s in shapes:
        n = math.prod(s)
        out.append(flat[off:off + n].reshape(s))
        off += n
    return out


def _row_tile(L):
    return 256 if L % 256 == 0 else L


def _layer0_mix(diff, const):
    x, mod, norm_w, lam_re, lam_im, log_dt, b_re, b_im, c_re, c_im, s5_d, *slots = diff
    ada_b, weights = const
    L = x.shape[0]
    tm = _row_tile(L)
    mods = mod.reshape(2, 1, D_MODEL)
    biases = ada_b.reshape(2, 1, D_MODEL)
    op_ln0 = make_rowwise(_f_lnmod, "ln0", tm, 1, 5, pass_first=True)
    h, x = op_ln0((x,), (norm_w.reshape(1, D_MODEL), mods[1], mods[0], biases[1], biases[0]))
    u, z = make_proj("s5_in")(h, tuple(weights), tuple(slots))
    blocks = _s5_block_params(lam_re, lam_im, log_dt, b_re, b_im, c_re, c_im)
    y2 = make_s5_core(min(S5_TL, L))(u, *blocks, s5_d.reshape(1, D_INNER))
    return x, y2, z


def _glu_gate_fwd_call(y2, w, z, name):
    m, k = y2.shape
    n = w.shape[1]
    tm, tn = _tile(m, 1024), _tile(n, 512)

    def body(a_ref, w_ref, z_ref, t_ref, y4_ref):
        j = pl.program_id(1)
        t = _bdot(a_ref[...], w_ref[...])
        t_ref[...] = t
        y2_tile = a_ref[:, pl.ds(pl.multiple_of(j * tn, LANES), tn)]
        y4_ref[...] = _f_s5_gate(y2_tile, t, z_ref[...])[0]

    tile = pl.BlockSpec((tm, tn), lambda i, j: (i, j))
    return pl.pallas_call(
        body, name=name, grid=(m // tm, n // tn),
        in_specs=[pl.BlockSpec((tm, k), lambda i, j: (i, 0)), pl.BlockSpec((k, tn), lambda i, j: (0, j)), tile],
        out_specs=[tile, tile], out_shape=[SDS((m, n), F32), SDS((m, n), F32)],
        compiler_params=_params(("parallel", "parallel"), VMEM_MID),
    )(y2, w, z)


def make_glu_gate(name, tm):
    gate = make_rowwise(_f_s5_gate, name + "_gate", tm, 3, 0)

    @jax.custom_vjp
    def op(y2, w, grad_slot, z):
        return _glu_gate_fwd_call(y2, w, z, name + "_fwd")[1]

    def fwd(y2, w, grad_slot, z):
        t, y4 = _glu_gate_fwd_call(y2, w, z, name + "_fwd")
        return y4, (y2, w, t, z)

    def bwd(res, dy4):
        y2, w, t, z = res
        (dy2_gate, dt, dz), _ = gate.run_bwd((y2, t, z), (), (dy4,))
        return _matmul(dt, w, "nt", name + "_dx", add=dy2_gate), jnp.zeros_like(w), _matmul(y2, dt, "tn", name + "_dw"), dz

    op.defvjp(fwd, bwd)
    return op


def _layer0_out(diff, weights):
    y2, z, *slots = diff
    y4 = make_glu_gate("s5_glu", _row_tile(y2.shape[0]))(y2, weights[0], slots[0], z)
    return make_mm("s5_out")(y4, weights[1], slots[1])


def _f_res_lnmod(x, o, gate, bgate, nw, sc, sh, bsc, bsh):
    (x1,) = _f_res(x, o, gate, bgate)
    return _f_lnmod(x1, nw, sc, sh, bsc, bsh) + (x1,)


def _f_res_loss(x, y, tgt, gate, bgate, fw):
    return _f_loss(_f_res(x, y, gate, bgate)[0], tgt, fw)


def _layer1_loss(diff, const):
    x, o, gate0, mod, norm_w, conv_w, a_log, dt_bias, gdn_nw, final_nw, *slots = diff
    tgt, bgate0, ada_b, weights = const
    L = x.shape[0]
    tm = _row_tile(L)
    mods = mod.reshape(3, 1, D_MODEL)
    biases = ada_b.reshape(3, 1, D_MODEL)
    h, x1 = make_rowwise(_f_res_lnmod, "res0_ln1", tm, 2, 7)(
        (x, o), (gate0.reshape(1, D_MODEL), bgate0.reshape(1, D_MODEL), norm_w.reshape(1, D_MODEL), mods[1], mods[0], biases[1], biases[0]))
    q0, k0, v0, gz, ba = make_proj("gdn_in", w_rows=True)(h, tuple(weights[0:5]), tuple(slots[0:5]))
    cw = jnp.concatenate([conv_w, jnp.zeros((SUBLANES - GDN_CONV, GDN_CONV_CH), F32)], axis=0)
    q = make_conv_act(lambda t: _l2n(_silu(t)) * (GDN_DK ** -0.5), "gdn_conv_q")(q0, cw[:, :GDN_QK])
    k = make_conv_act(lambda t: _l2n(_silu(t)), "gdn_conv_k")(k0, cw[:, GDN_QK:2 * GDN_QK])
    v = make_conv_act(_silu, "gdn_conv_v")(v0, cw[:, 2 * GDN_QK:])
    pad = jnp.zeros((LANES - 2 * GDN_HEADS,), F32)
    alog_row = jnp.concatenate([jnp.zeros((GDN_HEADS,), F32), a_log, pad]).reshape(1, LANES)
    dtb_row = jnp.concatenate([jnp.zeros((GDN_HEADS,), F32), dt_bias, pad]).reshape(1, LANES)
    (bg,) = make_rowwise(_f_betag, "gdn_bg", tm, 1, 2)((ba,), (alog_row, dtb_row))
    nw_row = jnp.tile(gdn_nw, GDN_HEADS).reshape(1, D_INNER)
    on = gdn_scan(*gdn_prep(q, k, v, bg), gz, nw_row)
    y = make_mm("gdn_out")(on, weights[5], slots[5])
    (lt,) = make_rowwise(_f_res_loss, "res1_loss", tm, 3, 3)((x1, y, tgt), (mods[2], biases[2], final_nw.reshape(1, D_MODEL)))
    return jnp.sum(lt)


VEC_NAMES = ("ada_b", "norm_w", "s5_lambda_re", "s5_lambda_im", "s5_log_dt", "s5_d", "gdn_a_log", "gdn_dt_bias", "final_norm_w")
MAT_NAMES = ("s5_b_re", "s5_b_im", "s5_c_re", "s5_c_im")
S5_BIG = ("s5_w_in", "s5_w_glu", "s5_w_out")
GDN_BIG = ("gdn_w_in", "gdn_w_out")
BIG_NAMES = S5_BIG + GDN_BIG
WEIGHT_ORDER = ("ada_w", "ada_b", "norm_w", "s5_w_in", "s5_lambda_re", "s5_lambda_im", "s5_log_dt", "s5_b_re", "s5_b_im",
                "s5_c_re", "s5_c_im", "s5_d", "s5_w_glu", "s5_w_out", "gdn_w_in", "gdn_conv_w", "gdn_a_log", "gdn_dt_bias",
                "gdn_norm_w", "gdn_w_out", "final_norm_w")


def _step(x, c, W, M, V, tgt):
    L = x.shape[1]
    ix, iy, ic = lax.axis_index("x"), lax.axis_index("y"), lax.axis_index("c")
    me = 4 * ix + 2 * iy + ic
    n_ada = W["ada_w"].shape[2]
    n_conv = W["gdn_conv_w"].shape[2]
    n_gnw = W["gdn_norm_w"].shape[1]

    g1 = _allgather_call(_pack([c, W["gdn_conv_w"], W["gdn_norm_w"]], SUBLANES), "gather_small_in")
    g1 = g1.reshape(N_DEV, -1)
    c_all = g1[:, :D_MODEL]
    conv_w = g1[:, D_MODEL:D_MODEL + GDN_CONV * n_conv].reshape(N_DEV, GDN_CONV, n_conv).transpose(1, 0, 2).reshape(GDN_CONV, -1)
    gdn_nw = g1[:, D_MODEL + GDN_CONV * n_conv:D_MODEL + GDN_CONV * n_conv + n_gnw].reshape(-1)
    mod_part = _ada_mod_call(c_all, W["ada_w"])
    g2 = _allgather_call(_pack([mod_part], SUBLANES), "gather_mod").reshape(N_DEV, -1)
    mod_all = g2[:, :2 * N_DEV * n_ada].reshape(N_DEV, 2, N_DEV, n_ada)
    mod_raw = lax.dynamic_index_in_dim(mod_all, me, axis=2, keepdims=False)
    mod_raw = mod_raw.transpose(1, 0, 2).reshape(2, 3 * D_MODEL)

    shard = lambda n: W[n][0].astype(BF16)
    (w_in5_parts,) = _gather_weights_call([shard("s5_w_in")], "gather_s5_w_in")
    late = _spread_start_call([shard("s5_w_glu"), shard("s5_w_out")], False, "gather_s5_late_start", w_in5_parts)
    turned = lambda a: jnp.transpose(a[0])
    g_send, g_recv, g_srcs, g_lands, g_token = _spread_start_call(
        [turned(W["gdn_w_in"]).astype(BF16), shard("gdn_w_out")], False, "gather_gdn_start", late[4])
    w_in5 = _join_cols_call(w_in5_parts, "join_s5_w_in")
    slot = lambda *s: jnp.zeros(s, F32)
    two = 2 * D_MODEL
    diff_mix = (x[0], mod_raw[0, :two] + g_token[0, 0], W["norm_w"][0], W["s5_lambda_re"][0], W["s5_lambda_im"][0], W["s5_log_dt"][0],
                W["s5_b_re"][0], W["s5_b_im"][0], W["s5_c_re"][0], W["s5_c_im"][0], W["s5_d"][0],
                slot(D_MODEL, D_INNER), slot(D_MODEL, D_INNER))

    (xp, y2, z5), vjp_mix = jax.vjp(lambda d: _layer0_mix(d, (W["ada_b"][0, :two], (w_in5[:, :D_INNER], w_in5[:, D_INNER:]))), diff_mix)
    l_srcs, l_lands = _spread_wait_call(late[0], late[1], late[2], late[3], y2, "gather_s5_late_wait")
    w_glu, w_o5 = [lax.dynamic_update_slice(land, src[None], (me, 0, 0)).reshape(-1, src.shape[1]) for land, src in zip(l_lands, l_srcs)]
    diff_out = (y2, z5, slot(D_INNER, D_INNER), slot(D_INNER, D_MODEL))
    o5, vjp_out = jax.vjp(lambda d: _layer0_out(d, (w_glu, w_o5)), diff_out)
    g_srcs, g_lands = _spread_wait_call(g_send, g_recv, g_srcs, g_lands, o5, "gather_gdn_wait")
    gdn_full = [lax.dynamic_update_slice(land, src[None], (me, 0, 0)) for land, src in zip(g_lands, g_srcs)]
    w_ing = gdn_full[0].reshape(GDN_PROJ, D_MODEL)
    w_ba = jnp.concatenate([w_ing[GDN_CONV_CH + D_INNER:], jnp.zeros((LANES - 2 * GDN_HEADS, D_MODEL), BF16)], axis=0)
    weights1 = (w_ing[:GDN_QK], w_ing[GDN_QK:2 * GDN_QK], w_ing[2 * GDN_QK:GDN_CONV_CH],
                w_ing[GDN_CONV_CH:GDN_CONV_CH + D_INNER], w_ba, gdn_full[1].reshape(D_INNER, D_MODEL))
    slots1 = tuple(jnp.zeros(w.shape, F32) for w in weights1)
    diff1 = (xp, o5, mod_raw[0, two:], mod_raw[1], W["norm_w"][1], conv_w, W["gdn_a_log"][0], W["gdn_dt_bias"][0], gdn_nw,
             W["final_norm_w"], *slots1)
    loss_local, vjp1 = jax.vjp(lambda d: _layer1_loss(d, (tgt[0], W["ada_b"][0, two:], W["ada_b"][1], weights1)), diff1)
    ((dxp, do5, dmod_gate, dmod1, d_norm_w1, d_conv, d_alog, d_dtb, d_gnw, d_fnw, d_wq, d_wk, d_wv, d_wgz, d_wba, d_wog),) = vjp1(
        jnp.ones((), F32))
    loss = lax.psum(loss_local, MESH_AXES)

    rows = lambda d: d.reshape(N_DEV, d.shape[0] // N_DEV, d.shape[1])
    d_ing = jnp.concatenate([d_wq, d_wk, d_wv, d_wgz, d_wba[:2 * GDN_HEADS]], axis=0).astype(BF16).reshape(N_DEV, -1, D_MODEL)
    s_send, s_recv, s_srcs, s_lands, s_token = _spread_start_call([d_ing, rows(d_wog).astype(BF16)], True, "scatter_gdn_start", dxp)
    ((dy2, dz5, d_wglu, d_wo5),) = vjp_out(do5.at[0, 0].add(s_token[0, 0]))
    t_send, t_recv, t_srcs, t_lands, t_token = _spread_start_call(
        [rows(d_wglu).astype(BF16), rows(d_wo5).astype(BF16)], True, "scatter_s5_late_start", dy2)
    ((dx, dmod_ss, d_norm_w0, d_lre, d_lim, d_logdt, d_bre, d_bim, d_cre, d_cim, d_s5d, d_wu, d_wz),) = vjp_mix(
        (dxp.at[0, 0].add(t_token[0, 0]), dy2, dz5))
    dmod = jnp.stack([jnp.concatenate([dmod_ss, dmod_gate]), dmod1])
    d_norm_w = jnp.stack([d_norm_w0, d_norm_w1])
    vec_parts = [dmod, d_norm_w, d_lre, d_lim, d_logdt, d_s5d, d_alog, d_dtb, d_fnw]
    tail_parts = [d_conv, d_gnw]
    mat_parts = [_groups_last(d) for d in (d_bre, d_bim, d_cre, d_cim)]
    n_vec = sum(math.prod(p.shape) for p in vec_parts)
    m_send, m_recv, m_srcs, m_lands, m_token = _spread_start_call(
        [_pack(vec_parts + tail_parts, ADAM_ROWS), _pack(mat_parts, SUBLANES).astype(BF16)], False, "gather_small_grads_start", dx)
    d_in5 = _split_cols_call(jnp.concatenate([d_wu.at[0, 0].add(m_token[0, 0]), d_wz], axis=1), "split_s5_w_in", BF16)
    u_send, u_recv, u_srcs, u_lands, u_token = _spread_start_call([d_in5], True, "scatter_s5_in_start", m_token)
    t_srcs, t_lands = _spread_wait_call(t_send, t_recv, t_srcs, t_lands, u_token, "scatter_s5_late_wait")
    s_srcs, s_lands = _spread_wait_call(s_send, s_recv, s_srcs, s_lands, t_lands[0], "scatter_gdn_wait")
    big = {}

    def owner_update(land, src, n):
        mine = lax.dynamic_index_in_dim(src, me, 0, keepdims=True)
        parts = lax.dynamic_update_slice(land, mine, (me, 0, 0))
        if n == "gdn_w_in":
            outs = _adam_call(parts, turned(W[n]), turned(M[n]), turned(V[n]), "adam_" + n, by_cols=True)
            return [jnp.transpose(o) for o in outs]
        return _adam_call(parts, W[n][0], M[n][0], V[n][0], "adam_" + n, rows=_tile(W[n].shape[1], 128))

    for land, src, n in zip(tuple(t_lands) + tuple(s_lands), tuple(t_srcs) + tuple(s_srcs), ("s5_w_glu", "s5_w_out") + GDN_BIG):
        big[n] = owner_update(land, src, n)

    m_srcs, m_lands = _spread_wait_call(m_send, m_recv, m_srcs, m_lands, big["gdn_w_out"][0], "gather_small_grads_wait")
    sg_vec, sg_mat = [lax.dynamic_update_slice(land, src[None], (me, 0, 0)) for land, src in zip(m_lands, m_srcs)]
    tot_vec = _sum_call(sg_vec, "sum_vec_grads", ADAM_ROWS)
    tot_mat = _sum_call(sg_mat, "sum_mat_grads", ADAM_ROWS)
    g_conv, g_gnw = _unpack(tot_vec.reshape(-1)[n_vec:], [d_conv.shape, d_gnw.shape])
    g_conv_mine = lax.dynamic_slice_in_dim(g_conv, me * n_conv, n_conv, axis=1)
    g_gnw_mine = lax.dynamic_slice_in_dim(g_gnw, me * n_gnw, n_gnw, axis=0)
    vec_names = VEC_NAMES + ("gdn_conv_w", "gdn_norm_w")
    vec_g = _pack([tot_vec.reshape(-1)[:n_vec], g_conv_mine, g_gnw_mine], ADAM_ROWS)
    vec = _adam_call(vec_g[None], _pack([W[n] for n in vec_names], ADAM_ROWS), _pack([M[n] for n in vec_names], ADAM_ROWS),
                     _pack([V[n] for n in vec_names], ADAM_ROWS), "adam_vec")
    vec = [_unpack(b, [W[n].shape for n in vec_names]) for b in vec]
    mats = []
    for name, g_mat in zip(MAT_NAMES, _unpack(tot_mat, [p.shape for p in mat_parts])):
        outs = _adam_call(g_mat[None], _groups_last(W[name]), _groups_last(M[name]), _groups_last(V[name]), "adam_" + name)
        mats.append([_groups_first(o, W[name].shape) for o in outs])

    dmod_all = sg_vec[:, :2 * 3 * D_MODEL // LANES].reshape(N_DEV, 2, N_DEV, n_ada // LANES, LANES)
    dmod_mine = lax.dynamic_index_in_dim(dmod_all, me, axis=2, keepdims=False).transpose(1, 0, 2, 3).reshape(2, N_DEV, n_ada)
    g_ada_w = _ada_grad_call(c_all, dmod_mine)
    ada = _adam_call(g_ada_w.reshape(1, -1, LANES), W["ada_w"].reshape(-1, LANES), M["ada_w"].reshape(-1, LANES),
                     V["ada_w"].reshape(-1, LANES), "adam_ada")
    u_srcs, u_lands = _spread_wait_call(u_send, u_recv, u_srcs, u_lands, ada[0], "scatter_s5_in_wait")
    big["s5_w_in"] = owner_update(u_lands[0], u_srcs[0], "s5_w_in")
    ada = [a.reshape(W["ada_w"].shape) for a in ada]

    res = {}
    for n in BIG_NAMES:
        res[n] = [o[None] for o in big[n]]
    for i, n in enumerate(vec_names):
        res[n] = [b[i] for b in vec]
    for i, n in enumerate(MAT_NAMES):
        res[n] = mats[i]
    res["ada_w"] = ada
    outs = [loss, dx[None]]
    for j in range(4):
        outs += [res[n][j] for n in WEIGHT_ORDER]
    return tuple(outs)


def kernel(x, c, ada_w, ada_b, norm_w, s5_w_in, s5_lambda_re, s5_lambda_im, s5_log_dt, s5_b_re, s5_b_im, s5_c_re, s5_c_im, s5_d, s5_w_glu, s5_w_out, gdn_w_in, gdn_conv_w, gdn_a_log, gdn_dt_bias, gdn_norm_w, gdn_w_out, final_norm_w, loss_target, m_ada_w, m_ada_b, m_norm_w, m_s5_w_in, m_s5_lambda_re, m_s5_lambda_im, m_s5_log_dt, m_s5_b_re, m_s5_b_im, m_s5_c_re, m_s5_c_im, m_s5_d, m_s5_w_glu, m_s5_w_out, m_gdn_w_in, m_gdn_conv_w, m_gdn_a_log, m_gdn_dt_bias, m_gdn_norm_w, m_gdn_w_out, m_final_norm_w, v_ada_w, v_ada_b, v_norm_w, v_s5_w_in, v_s5_lambda_re, v_s5_lambda_im, v_s5_log_dt, v_s5_b_re, v_s5_b_im, v_s5_c_re, v_s5_c_im, v_s5_d, v_s5_w_glu, v_s5_w_out, v_gdn_w_in, v_gdn_conv_w, v_gdn_a_log, v_gdn_dt_bias, v_gdn_norm_w, v_gdn_w_out, v_final_norm_w):
    W = dict(ada_w=ada_w, ada_b=ada_b, norm_w=norm_w, s5_w_in=s5_w_in, s5_lambda_re=s5_lambda_re, s5_lambda_im=s5_lambda_im,
             s5_log_dt=s5_log_dt, s5_b_re=s5_b_re, s5_b_im=s5_b_im, s5_c_re=s5_c_re, s5_c_im=s5_c_im, s5_d=s5_d,
             s5_w_glu=s5_w_glu, s5_w_out=s5_w_out, gdn_w_in=gdn_w_in, gdn_conv_w=gdn_conv_w, gdn_a_log=gdn_a_log,
             gdn_dt_bias=gdn_dt_bias, gdn_norm_w=gdn_norm_w, gdn_w_out=gdn_w_out, final_norm_w=final_norm_w)
    M = dict(ada_w=m_ada_w, ada_b=m_ada_b, norm_w=m_norm_w, s5_w_in=m_s5_w_in, s5_lambda_re=m_s5_lambda_re,
             s5_lambda_im=m_s5_lambda_im, s5_log_dt=m_s5_log_dt, s5_b_re=m_s5_b_re, s5_b_im=m_s5_b_im, s5_c_re=m_s5_c_re,
             s5_c_im=m_s5_c_im, s5_d=m_s5_d, s5_w_glu=m_s5_w_glu, s5_w_out=m_s5_w_out, gdn_w_in=m_gdn_w_in,
             gdn_conv_w=m_gdn_conv_w, gdn_a_log=m_gdn_a_log, gdn_dt_bias=m_gdn_dt_bias, gdn_norm_w=m_gdn_norm_w,
             gdn_w_out=m_gdn_w_out, final_norm_w=m_final_norm_w)
    V = dict(ada_w=v_ada_w, ada_b=v_ada_b, norm_w=v_norm_w, s5_w_in=v_s5_w_in, s5_lambda_re=v_s5_lambda_re,
             s5_lambda_im=v_s5_lambda_im, s5_log_dt=v_s5_log_dt, s5_b_re=v_s5_b_re, s5_b_im=v_s5_b_im, s5_c_re=v_s5_c_re,
             s5_c_im=v_s5_c_im, s5_d=v_s5_d, s5_w_glu=v_s5_w_glu, s5_w_out=v_s5_w_out, gdn_w_in=v_gdn_w_in,
             gdn_conv_w=v_gdn_conv_w, gdn_a_log=v_gdn_a_log, gdn_dt_bias=v_gdn_dt_bias, gdn_norm_w=v_gdn_norm_w,
             gdn_w_out=v_gdn_w_out, final_norm_w=v_final_norm_w)
    return _step(x, c, W, M, V, loss_target)
```

```python
import functools
import math

import jax
import jax.numpy as jnp
from jax import lax
from jax.experimental import pallas as pl
from jax.experimental.pallas import tpu as pltpu

F32 = jnp.float32
BF16 = jnp.bfloat16
SDS = jax.ShapeDtypeStruct

D_MODEL = 1024
D_INNER = 2048
NORM_EPS = 1e-6
S5_GROUP = 16
S5_GROUPS = 128
S5_STATE = 64
GDN_HEADS = 8
GDN_DK = 128
GDN_DV = 256
GDN_CONV = 4
GDN_CHUNK = 64
GDN_QK = 1024
GDN_CONV_CH = 4096
GDN_PROJ = 6160
ADAM_LR = 0.001
ADAM_B1 = 0.9
ADAM_B2 = 0.999
ADAM_EPS = 1e-08
ADAM_WD = 0.01
ADAM_STEP = 10

N_DEV = 8
LANES = 128
SUBLANES = 8
VMEM_BIG = 56 << 20
VMEM_MID = 40 << 20
S5_GB = 8
S5_BW = S5_GB * S5_GROUP
S5_TL = 2048
MESH_AXES = ("x", "y", "c")


def _params(sem, vmem=None):
    return pltpu.CompilerParams(dimension_semantics=sem, vmem_limit_bytes=vmem)


def _bdot(a, b, dims=(((1,), (0,)), ((), ()))):
    return lax.dot_general(a.astype(BF16), b.astype(BF16), dims, preferred_element_type=F32)


def _hdot(a, b, dims=(((1,), (0,)), ((), ()))):
    return lax.dot_general(a, b, dims, preferred_element_type=F32, precision=lax.Precision.HIGHEST)


_BNN = (((2,), (1,)), ((0,), (0,)))
_BNT = (((2,), (2,)), ((0,), (0,)))
_BTN = (((1,), (1,)), ((0,), (0,)))


@jax.custom_vjp
def _unit_lower_inverse(a):
    c = a.shape[-1]
    ri = lax.broadcasted_iota(jnp.int32, a.shape, 1)
    ci = lax.broadcasted_iota(jnp.int32, a.shape, 2)
    n = -a
    t = (ri == ci).astype(F32) + n
    for _ in range(int(math.log2(c)) - 1):
        n = _hdot(n, n, _BNN)
        t = t + _hdot(t, n, _BNN)
    return t


def _unit_lower_inverse_fwd(a):
    t = _unit_lower_inverse(a)
    return t, t


def _unit_lower_inverse_bwd(t, g):
    return (-_hdot(_hdot(t, g, _BTN), t, _BNT),)


_unit_lower_inverse.defvjp(_unit_lower_inverse_fwd, _unit_lower_inverse_bwd)


NN = (((1,), (0,)), ((), ()))
NT = (((1,), (1,)), ((), ()))
TN = (((0,), (0,)), ((), ()))


def _tile(n, pref):
    for t in (pref, 512, 256, 128):
        if t <= n and n % t == 0:
            return t
    return n


def _matmul(a, b, mode, name, add=None):
    if mode == "nn":
        (m, k), (_, n) = a.shape, b.shape
    elif mode == "nt":
        (m, k), (n, _) = a.shape, b.shape
    else:
        (k, m), (_, n) = a.shape, b.shape
    tm, tn, tk = _tile(m, 1024), _tile(n, 512), (k if k <= 2048 else _tile(k, 512))
    if mode == "tn":
        tm, tn, tk = _tile(m, 1024), _tile(n, 1024), _tile(k, 1024)
    nk = k // tk
    dims = {"nn": NN, "nt": NT, "tn": TN}[mode]

    def body(a_ref, b_ref, *rest):
        o_ref, acc_ref = rest[-2], rest[-1]
        part = _bdot(a_ref[...], b_ref[...], dims)
        if nk == 1:
            o_ref[...] = part if add is None else part + rest[0][...]
            return
        kk = pl.program_id(2)

        @pl.when(kk == 0)
        def _():
            acc_ref[...] = part if add is None else part + rest[0][...]

        @pl.when(kk > 0)
        def _():
            acc_ref[...] += part

        @pl.when(kk == nk - 1)
        def _():
            o_ref[...] = acc_ref[...]

    a_spec = pl.BlockSpec((tk, tm), lambda i, j, q: (q, i)) if mode == "tn" else pl.BlockSpec((tm, tk), lambda i, j, q: (i, q))
    b_spec = pl.BlockSpec((tn, tk), lambda i, j, q: (j, q)) if mode == "nt" else pl.BlockSpec((tk, tn), lambda i, j, q: (q, j))
    o_spec = pl.BlockSpec((tm, tn), lambda i, j, q: (i, j))
    return pl.pallas_call(
        body, name=name, grid=(m // tm, n // tn, nk),
        in_specs=[a_spec, b_spec] + ([] if add is None else [o_spec]), out_specs=o_spec,
        out_shape=SDS((m, n), F32), scratch_shapes=[pltpu.VMEM((tm, tn), F32)],
        compiler_params=_params(("parallel", "parallel", "arbitrary"), VMEM_MID),
    )(a, b, *([] if add is None else [add]))


def make_mm(name, pass_input=False):
    def primal(a, w):
        out = _matmul(a, w, "nn", name + "_fwd")
        return (out, a) if pass_input else out

    @jax.custom_vjp
    def mm(a, w, grad_slot):
        return primal(a, w)

    def fwd(a, w, grad_slot):
        return primal(a, w), (a, w)

    def bwd(res, g):
        a, w = res
        g, g_other = g if pass_input else (g, None)
        return _matmul(g, w, "nt", name + "_dx", add=g_other), jnp.zeros_like(w), _matmul(a, g, "tn", name + "_dw")

    mm.defvjp(fwd, bwd)
    return mm


PROJ_ROWS = 256


def _proj_fwd_call(a, ws, name, w_rows):
    m, k = a.shape
    tm = _tile(m, PROJ_ROWS)
    nw = len(ws)
    widths = [w.shape[0] if w_rows else w.shape[1] for w in ws]

    def body(*refs):
        ab = refs[0][...].astype(BF16)
        for w_ref, o_ref in zip(refs[1:1 + nw], refs[1 + nw:]):
            o_ref[...] = lax.dot_general(ab, w_ref[...], NT if w_rows else NN, preferred_element_type=F32)

    return pl.pallas_call(
        body, name=name, grid=(m // tm,),
        in_specs=[pl.BlockSpec((tm, k), lambda i: (i, 0))] + [pl.BlockSpec(w.shape, lambda i: (0, 0)) for w in ws],
        out_specs=[pl.BlockSpec((tm, n), lambda i: (i, 0)) for n in widths],
        out_shape=[SDS((m, n), F32) for n in widths],
        compiler_params=_params(("parallel",), VMEM_BIG),
    )(a, *ws)


def _proj_dx_call(gs, ws, name, w_rows):
    m = gs[0].shape[0]
    k = ws[0].shape[1] if w_rows else ws[0].shape[0]
    tm = _tile(m, PROJ_ROWS)
    nw = len(ws)

    def body(*refs):
        acc = None
        for g_ref, w_ref in zip(refs[:nw], refs[nw:2 * nw]):
            part = _bdot(g_ref[...], w_ref[...], NN if w_rows else NT)
            acc = part if acc is None else acc + part
        refs[2 * nw][...] = acc

    return pl.pallas_call(
        body, name=name, grid=(m // tm,),
        in_specs=[pl.BlockSpec((tm, g.shape[1]), lambda i: (i, 0)) for g in gs] + [pl.BlockSpec(w.shape, lambda i: (0, 0)) for w in ws],
        out_specs=pl.BlockSpec((tm, k), lambda i: (i, 0)), out_shape=SDS((m, k), F32),
        compiler_params=_params(("parallel",), VMEM_BIG),
    )(*gs, *ws)


def make_proj(name, w_rows=False):
    @jax.custom_vjp
    def proj(a, ws, grad_slots):
        return tuple(_proj_fwd_call(a, ws, name + "_fwd", w_rows))

    def fwd(a, ws, grad_slots):
        return tuple(_proj_fwd_call(a, ws, name + "_fwd", w_rows)), (a, ws)

    def bwd(res, gs):
        a, ws = res
        dws = tuple(_matmul(g, a, "tn", "%s_dw%d" % (name, i)) if w_rows else _matmul(a, g, "tn", "%s_dw%d" % (name, i))
                    for i, g in enumerate(gs))
        return _proj_dx_call(tuple(gs), ws, name + "_dx", w_rows), tuple(jnp.zeros_like(w) for w in ws), dws

    proj.defvjp(fwd, bwd)
    return proj


def make_rowwise(f, name, tm, n_rows, n_params, vmem=VMEM_MID, pass_first=False):
    def specs_of(arrs, blocked):
        if blocked:
            return [pl.BlockSpec((tm, a.shape[1]), lambda i: (i, 0)) for a in arrs]
        return [pl.BlockSpec(a.shape, lambda i: (0, 0)) for a in arrs]

    def out_structs(rows, params):
        blk = [SDS((tm, r.shape[1]), r.dtype) for r in rows] + [SDS(p.shape, p.dtype) for p in params]
        return jax.eval_shape(f, *blk)

    def run_fwd(rows, params):
        L = rows[0].shape[0]
        outs = out_structs(rows, params)

        def body(*refs):
            ins = [r[...] for r in refs[:n_rows + n_params]]
            res = f(*ins)
            for o_ref, val in zip(refs[n_rows + n_params:], res):
                o_ref[...] = val

        return pl.pallas_call(
            body, name=name + "_fwd", grid=(L // tm,),
            in_specs=specs_of(rows, True) + specs_of(params, False),
            out_specs=[pl.BlockSpec((tm, o.shape[1]), lambda i: (i, 0)) for o in outs],
            out_shape=[SDS((L, o.shape[1]), o.dtype) for o in outs],
            compiler_params=_params(("parallel",), vmem),
        )(*rows, *params)

    def run_bwd(rows, params, gs):
        L = rows[0].shape[0]
        n_g = len(gs)

        def body(*refs):
            i = pl.program_id(0)
            ins = [r[...] for r in refs[:n_rows + n_params]]
            cts = tuple(r[...] for r in refs[n_rows + n_params:n_rows + n_params + n_g])
            outs = refs[n_rows + n_params + n_g:]
            _, vjp = jax.vjp(f, *ins)
            grads = vjp(cts[:-1] if pass_first else cts)
            if pass_first:
                grads = (grads[0] + cts[-1],) + tuple(grads[1:])
            for o_ref, val in zip(outs[:n_rows], grads[:n_rows]):
                o_ref[...] = val

            if n_params:
                @pl.when(i == 0)
                def _():
                    for o_ref in outs[n_rows:]:
                        o_ref[...] = jnp.zeros_like(o_ref)
                for o_ref, val in zip(outs[n_rows:], grads[n_rows:]):
                    o_ref[...] += val

        res = pl.pallas_call(
            body, name=name + "_bwd", grid=(L // tm,),
            in_specs=specs_of(rows, True) + specs_of(params, False) + specs_of(gs, True),
            out_specs=specs_of(rows, True) + specs_of(params, False),
            out_shape=[SDS(r.shape, r.dtype) for r in rows] + [SDS(p.shape, p.dtype) for p in params],
            compiler_params=_params(("arbitrary",), vmem),
        )(*rows, *params, *gs)
        return tuple(res[:n_rows]), tuple(res[n_rows:])

    def outputs(rows, params):
        outs = tuple(run_fwd(rows, params))
        return outs + (rows[0],) if pass_first else outs

    @jax.custom_vjp
    def op(rows, params):
        return outputs(rows, params)

    def fwd(rows, params):
        return outputs(rows, params), (rows, params)

    def bwd(res, gs):
        rows, params = res
        return run_bwd(rows, params, tuple(gs))

    op.defvjp(fwd, bwd)
    op.run_fwd, op.run_bwd = run_fwd, run_bwd
    return op


def _s5_scan_rows(xr_ref, xi_ref, ar, ai, x0r, x0i, tl, reverse=False):
    n = xr_ref.shape[1]
    T = SUBLANES
    row = lax.broadcasted_iota(jnp.int32, (T, n), 0)
    pr, pi = [ar], [ai]
    for _ in range(T - 1):
        pr, pi = pr + [pr[-1] * ar - pi[-1] * ai], pi + [pr[-1] * ai + pi[-1] * ar]
    levels = []
    for d in (1, 2, 4):
        mask = (row < T - d) if reverse else (row >= d)
        levels.append((T - d if reverse else d, jnp.where(mask, pr[d - 1], 0.0), jnp.where(mask, pi[d - 1], 0.0)))
    cr = jnp.zeros((T, n), F32)
    ci = jnp.zeros((T, n), F32)
    for r in range(T):
        k = (T - r) if reverse else (r + 1)
        cr = jnp.where(row == r, pr[k - 1], cr)
        ci = jnp.where(row == r, pi[k - 1], ci)
    nt = tl // T
    last = 0 if reverse else T - 1

    def step(t, carry):
        sr, si = carry
        base = pl.multiple_of((nt - 1 - t if reverse else t) * T, T)
        br = xr_ref[pl.ds(base, T), :]
        bi = xi_ref[pl.ds(base, T), :]
        for shift, mr, mi in levels:
            qr = pltpu.roll(br, shift, 0)
            qi = pltpu.roll(bi, shift, 0)
            br, bi = br + (mr * qr - mi * qi), bi + (mr * qi + mi * qr)
        xr = br + (cr * sr - ci * si)
        xi = bi + (cr * si + ci * sr)
        xr_ref[pl.ds(base, T), :] = xr
        xi_ref[pl.ds(base, T), :] = xi
        return xr[last:last + 1, :], xi[last:last + 1, :]
    return lax.fori_loop(0, nt, step, (x0r, x0i))


def _s5_fwd_call(u, bre, bim, cre, cim, a, d, tl):
    L, e = u.shape
    nb = e // S5_BW
    ns = bre.shape[2]
    nc = L // tl

    def body(u_ref, bre_ref, bim_ref, cre_ref, cim_ref, a_ref, d_ref, y_ref, xb_ref, sr_ref, si_ref, xr_ref, xi_ref, carry_ref):
        c = pl.program_id(1)

        @pl.when(c == 0)
        def _():
            carry_ref[...] = jnp.zeros_like(carry_ref)
        xb_ref[0, 0] = carry_ref[...]
        ub = u_ref[...]
        xr_ref[...] = _bdot(ub, bre_ref[0])
        xi_ref[...] = _bdot(ub, bim_ref[0])
        ar = a_ref[0, 0:1, :]
        ai = a_ref[0, 1:2, :]
        xr, xi = _s5_scan_rows(xr_ref, xi_ref, ar, ai, carry_ref[0:1, :], carry_ref[1:2, :], tl)
        carry_ref[0:1, :] = xr
        carry_ref[1:2, :] = xi
        sr = xr_ref[...].astype(BF16)
        si = xi_ref[...].astype(BF16)
        sr_ref[...] = sr
        si_ref[...] = si
        y_ref[...] = _f_s5_act(_bdot(sr, cre_ref[0]) - _bdot(si, cim_ref[0]), ub, d_ref[...])[0]

    return pl.pallas_call(
        body, name="s5_core_fwd", grid=(nb, nc),
        in_specs=[pl.BlockSpec((tl, S5_BW), lambda j, c: (c, j)),
                  pl.BlockSpec((1, S5_BW, ns), lambda j, c: (j, 0, 0)), pl.BlockSpec((1, S5_BW, ns), lambda j, c: (j, 0, 0)),
                  pl.BlockSpec((1, ns, S5_BW), lambda j, c: (j, 0, 0)), pl.BlockSpec((1, ns, S5_BW), lambda j, c: (j, 0, 0)),
                  pl.BlockSpec((1, SUBLANES, ns), lambda j, c: (j, 0, 0)), pl.BlockSpec((1, S5_BW), lambda j, c: (0, j))],
        out_specs=[pl.BlockSpec((tl, S5_BW), lambda j, c: (c, j)),
                   pl.BlockSpec((1, 1, SUBLANES, ns), lambda j, c: (j, c, 0, 0)),
                   pl.BlockSpec((tl, ns), lambda j, c: (c, j)), pl.BlockSpec((tl, ns), lambda j, c: (c, j))],
        out_shape=[SDS((L, e), F32), SDS((nb, nc, SUBLANES, ns), F32), SDS((L, nb * ns), BF16), SDS((L, nb * ns), BF16)],
        scratch_shapes=[pltpu.VMEM((tl, ns), F32), pltpu.VMEM((tl, ns), F32), pltpu.VMEM((SUBLANES, ns), F32)],
        compiler_params=_params(("arbitrary", "arbitrary"), VMEM_MID),
    )(u, bre, bim, cre, cim, a, d)


def _s5_bwd_call(u, dy2, bre, bim, cre, cim, a, d, xb, sr, si, tl):
    L, e = u.shape
    nb = e // S5_BW
    ns = bre.shape[2]
    nc = L // tl

    def body(u_ref, dy2_ref, bre_ref, bim_ref, cre_ref, cim_ref, a_ref, d_ref, xb_ref, sr_ref, si_ref,
             du_ref, dbre_ref, dbim_ref, dcre_ref, dcim_ref, da_ref, dd_ref,
             gr_ref, gi_ref, gcarry_ref):
        c = pl.program_id(1)

        @pl.when(c == 0)
        def _():
            gcarry_ref[...] = jnp.zeros_like(gcarry_ref)
            dbre_ref[...] = jnp.zeros_like(dbre_ref)
            dbim_ref[...] = jnp.zeros_like(dbim_ref)
            dcre_ref[...] = jnp.zeros_like(dcre_ref)
            dcim_ref[...] = jnp.zeros_like(dcim_ref)
            da_ref[...] = jnp.zeros_like(da_ref)
            dd_ref[...] = jnp.zeros_like(dd_ref)

        ub = u_ref[...]
        ys = _bdot(sr_ref[...], cre_ref[0]) - _bdot(si_ref[...], cim_ref[0])
        _, act_vjp = jax.vjp(lambda *t: _f_s5_act(*t)[0], ys, ub, d_ref[...])
        dy, du_skip, dd = act_vjp(dy2_ref[...])
        dd_ref[...] += dd
        ar = a_ref[0, 0:1, :]
        ai = a_ref[0, 1:2, :]
        x0r = xb_ref[0, 0, 0:1, :]
        x0i = xb_ref[0, 0, 1:2, :]
        dcre_ref[0] += _bdot(sr_ref[...], dy, TN)
        dcim_ref[0] -= _bdot(si_ref[...], dy, TN)
        gr_ref[...] = _bdot(dy, cre_ref[0], NT)
        gi_ref[...] = -_bdot(dy, cim_ref[0], NT)

        g0r, g0i = _s5_scan_rows(gr_ref, gi_ref, ar, -ai, gcarry_ref[0:1, :], gcarry_ref[1:2, :], tl, reverse=True)
        gcarry_ref[0:1, :] = g0r
        gcarry_ref[1:2, :] = g0i
        row = lax.broadcasted_iota(jnp.int32, (tl, ns), 0)
        gr = gr_ref[...]
        gi = gi_ref[...]
        xpr = jnp.where(row == 0, x0r, pltpu.roll(sr_ref[...].astype(F32), 1, 0))
        xpi = jnp.where(row == 0, x0i, pltpu.roll(si_ref[...].astype(F32), 1, 0))
        da_ref[0, 0:1, :] += jnp.sum(gr * xpr + gi * xpi, axis=0, keepdims=True)
        da_ref[0, 1:2, :] += jnp.sum(gi * xpr - gr * xpi, axis=0, keepdims=True)
        du_ref[...] = (_bdot(gr, bre_ref[0], NT) + _bdot(gi, bim_ref[0], NT)) + du_skip
        dbre_ref[0] += _bdot(ub, gr, TN)
        dbim_ref[0] += _bdot(ub, gi, TN)

    rev = lambda c: nc - 1 - c
    return pl.pallas_call(
        body, name="s5_core_bwd", grid=(nb, nc),
        in_specs=[pl.BlockSpec((tl, S5_BW), lambda j, c: (rev(c), j)), pl.BlockSpec((tl, S5_BW), lambda j, c: (rev(c), j)),
                  pl.BlockSpec((1, S5_BW, ns), lambda j, c: (j, 0, 0)), pl.BlockSpec((1, S5_BW, ns), lambda j, c: (j, 0, 0)),
                  pl.BlockSpec((1, ns, S5_BW), lambda j, c: (j, 0, 0)), pl.BlockSpec((1, ns, S5_BW), lambda j, c: (j, 0, 0)),
                  pl.BlockSpec((1, SUBLANES, ns), lambda j, c: (j, 0, 0)), pl.BlockSpec((1, S5_BW), lambda j, c: (0, j)),
                  pl.BlockSpec((1, 1, SUBLANES, ns), lambda j, c: (j, rev(c), 0, 0)),
                  pl.BlockSpec((tl, ns), lambda j, c: (rev(c), j)), pl.BlockSpec((tl, ns), lambda j, c: (rev(c), j))],
        out_specs=[pl.BlockSpec((tl, S5_BW), lambda j, c: (rev(c), j)),
                   pl.BlockSpec((1, S5_BW, ns), lambda j, c: (j, 0, 0)), pl.BlockSpec((1, S5_BW, ns), lambda j, c: (j, 0, 0)),
                   pl.BlockSpec((1, ns, S5_BW), lambda j, c: (j, 0, 0)), pl.BlockSpec((1, ns, S5_BW), lambda j, c: (j, 0, 0)),
                   pl.BlockSpec((1, SUBLANES, ns), lambda j, c: (j, 0, 0)), pl.BlockSpec((1, S5_BW), lambda j, c: (0, j))],
        out_shape=[SDS((L, e), F32), SDS(bre.shape, F32), SDS(bim.shape, F32), SDS(cre.shape, F32), SDS(cim.shape, F32),
                   SDS(a.shape, F32), SDS(d.shape, F32)],
        scratch_shapes=[pltpu.VMEM((tl, ns), F32) for _ in range(2)] + [pltpu.VMEM((SUBLANES, ns), F32)],
        compiler_params=_params(("arbitrary", "arbitrary"), VMEM_MID),
    )(u, dy2, bre, bim, cre, cim, a, d, xb, sr, si)


def make_s5_core(tl):
    @jax.custom_vjp
    def s5_core(u, bre, bim, cre, cim, a, d):
        return _s5_fwd_call(u, bre, bim, cre, cim, a, d, tl)[0]

    def fwd(u, bre, bim, cre, cim, a, d):
        y2, xb, sr, si = _s5_fwd_call(u, bre, bim, cre, cim, a, d, tl)
        return y2, (u, bre, bim, cre, cim, a, d, xb, sr, si)

    def bwd(res, dy2):
        u, bre, bim, cre, cim, a, d, xb, sr, si = res
        return tuple(_s5_bwd_call(u, dy2, bre, bim, cre, cim, a, d, xb, sr, si, tl))

    s5_core.defvjp(fwd, bwd)
    return s5_core


def _s5_block_params(lam_re, lam_im, log_dt, b_re, b_im, c_re, c_im):
    dt = jnp.exp(log_dt)[:, None]
    mag = jnp.exp(lam_re * dt)
    ab_re = mag * jnp.cos(lam_im * dt)
    ab_im = mag * jnp.sin(lam_im * dt)
    den = lam_re * lam_re + lam_im * lam_im
    nr = ab_re - 1.0
    ni = ab_im
    q_re = (nr * lam_re + ni * lam_im) / den
    q_im = (ni * lam_re - nr * lam_im) / den
    bb_re = q_re[..., None] * b_re - q_im[..., None] * b_im
    bb_im = q_re[..., None] * b_im + q_im[..., None] * b_re
    nb = S5_GROUPS // S5_GB
    eye = jnp.eye(S5_GB, dtype=F32)

    def bdiag_in(bb):
        t = bb.reshape(nb, S5_GB, S5_STATE, S5_GROUP)
        t = jnp.einsum("jgpm,gh->jgmhp", t, eye)
        return t.reshape(nb, S5_GB * S5_GROUP, S5_GB * S5_STATE)

    def bdiag_out(cc):
        t = cc.reshape(nb, S5_GB, S5_GROUP, S5_STATE)
        t = jnp.einsum("jgmp,gh->jgphm", t, eye)
        return t.reshape(nb, S5_GB * S5_STATE, S5_GB * S5_GROUP)

    a = jnp.stack([ab_re.reshape(nb, S5_GB * S5_STATE), ab_im.reshape(nb, S5_GB * S5_STATE)], axis=1)
    a = jnp.concatenate([a, jnp.zeros((nb, SUBLANES - 2, S5_GB * S5_STATE), F32)], axis=1)
    return bdiag_in(bb_re), bdiag_in(bb_im), bdiag_out(c_re), bdiag_out(c_im), a


def _shift_down(x, s, row):
    if s == 0:
        return x
    return jnp.where(row >= s, pltpu.roll(x, s, 0), 0.0)


def _shift_up(x, s, row, n):
    if s == 0:
        return x
    return jnp.where(row < n - s, pltpu.roll(x, n - s, 0), 0.0)


def _causal_conv(xv, w_ref, row):
    acc = jnp.zeros_like(xv)
    for j in range(GDN_CONV):
        acc += w_ref[j:j + 1, :] * _shift_down(xv, GDN_CONV - 1 - j, row)
    return acc


def _conv_fwd_call(x, w, act, name):
    L, ch = x.shape

    def body(x_ref, w_ref, y_ref):
        xv = x_ref[...]
        row = lax.broadcasted_iota(jnp.int32, xv.shape, 0)
        y_ref[...] = act(_causal_conv(xv, w_ref, row))

    return pl.pallas_call(
        body, name=name + "_fwd", grid=(ch // LANES,),
        in_specs=[pl.BlockSpec((L, LANES), lambda j: (0, j)), pl.BlockSpec((SUBLANES, LANES), lambda j: (0, j))],
        out_specs=pl.BlockSpec((L, LANES), lambda j: (0, j)), out_shape=SDS((L, ch), F32),
        compiler_params=_params(("parallel",), VMEM_MID),
    )(x, w)


def _conv_bwd_call(x, w, dy, act, name):
    L, ch = x.shape

    def body(x_ref, w_ref, dy_ref, dx_ref, dw_ref):
        xv = x_ref[...]
        row = lax.broadcasted_iota(jnp.int32, xv.shape, 0)
        _, act_vjp = jax.vjp(act, _causal_conv(xv, w_ref, row))
        (g,) = act_vjp(dy_ref[...])
        acc = jnp.zeros_like(xv)
        dws = []
        for j in range(GDN_CONV):
            s = GDN_CONV - 1 - j
            acc += w_ref[j:j + 1, :] * _shift_up(g, s, row, L)
            dws.append(jnp.sum(g * _shift_down(xv, s, row), axis=0, keepdims=True))
        dx_ref[...] = acc
        dw_ref[...] = jnp.concatenate(dws + [jnp.zeros((SUBLANES - GDN_CONV, LANES), F32)], axis=0)

    return pl.pallas_call(
        body, name=name + "_bwd", grid=(ch // LANES,),
        in_specs=[pl.BlockSpec((L, LANES), lambda j: (0, j)), pl.BlockSpec((SUBLANES, LANES), lambda j: (0, j)),
                  pl.BlockSpec((L, LANES), lambda j: (0, j))],
        out_specs=[pl.BlockSpec((L, LANES), lambda j: (0, j)), pl.BlockSpec((SUBLANES, LANES), lambda j: (0, j))],
        out_shape=[SDS((L, ch), F32), SDS((SUBLANES, ch), F32)],
        compiler_params=_params(("parallel",), VMEM_MID),
    )(x, w, dy)


def make_conv_act(act, name):
    @jax.custom_vjp
    def op(x, w):
        return _conv_fwd_call(x, w, act, name)

    def fwd(x, w):
        return _conv_fwd_call(x, w, act, name), (x, w)

    def bwd(res, dy):
        x, w = res
        return tuple(_conv_bwd_call(x, w, dy, act, name))

    op.defvjp(fwd, bwd)
    return op


BNN, BNT, BTN = _BNN, _BNT, _BTN
GDN_PREP_BATCH = 16


@jax.custom_vjp
def _known_inverse(a, t):
    return t


def _known_inverse_fwd(a, t):
    return t, t


def _known_inverse_bwd(t, g):
    return -_hdot(_hdot(t, g, _BTN), t, _BNT), jnp.zeros_like(t)


_known_inverse.defvjp(_known_inverse_fwd, _known_inverse_bwd)


def _gdn_prep_math(q, k, v, beta, g, t_saved=None):
    B, C = q.shape[0], q.shape[1]
    ri = lax.broadcasted_iota(jnp.int32, (B, C, C), 1)
    ci = lax.broadcasted_iota(jnp.int32, (B, C, C), 2)
    causal = ri >= ci
    strict = ri > ci
    eye = (ri == ci).astype(F32)
    gb = jnp.broadcast_to(g, (B, C, C))
    g_row = jnp.sum(gb * eye, axis=1, keepdims=True)
    gc_col = jnp.sum(jnp.where(causal, jnp.broadcast_to(g_row, (B, C, C)), 0.0), axis=2, keepdims=True)
    gc_row = jnp.sum(jnp.where(ri <= ci, gb, 0.0), axis=1, keepdims=True)
    decay = jnp.exp(jnp.where(causal, gc_col - gc_row, -jnp.inf))
    kk = _bdot(k, k, BNT)
    a_mat = jnp.where(strict, beta * kk * decay, 0.0)
    t = _unit_lower_inverse(a_mat) if t_saved is None else _known_inverse(a_mat, t_saved)
    e_gc = jnp.exp(gc_col)
    w = _hdot(t, beta * e_gc * k, BNN)
    u = _hdot(t, beta * v, BNN)
    qk = _bdot(q, k, BNT) * decay
    q_dec = q * e_gc
    g_last = gc_col[:, C - 1:C, :]
    k_dec = k * jnp.exp(g_last - gc_col)
    return q_dec, w, u, qk, k_dec, gc_col, t


def _gdn_prep_specs(L):
    C = GDN_CHUNK
    nb = min(GDN_PREP_BATCH, L // C)
    R = nb * C
    ins = [pl.BlockSpec((R, GDN_DK), lambda c, h: (c, h)), pl.BlockSpec((R, GDN_DK), lambda c, h: (c, h)),
           pl.BlockSpec((R, GDN_DV), lambda c, h: (c, h)), pl.BlockSpec((R, LANES), lambda c, h: (c, 0))]
    outs = [pl.BlockSpec((1, R, GDN_DK), lambda c, h: (h, c, 0)), pl.BlockSpec((1, R, GDN_DK), lambda c, h: (h, c, 0)),
            pl.BlockSpec((1, R, GDN_DV), lambda c, h: (h, c, 0)), pl.BlockSpec((1, R, C), lambda c, h: (h, c, 0)),
            pl.BlockSpec((1, R, GDN_DK), lambda c, h: (h, c, 0)), pl.BlockSpec((1, R, 1), lambda c, h: (h, c, 0))]
    t_spec = pl.BlockSpec((1, R, C), lambda c, h: (h, c, 0))
    shapes = [SDS((GDN_HEADS, L, GDN_DK), F32), SDS((GDN_HEADS, L, GDN_DK), F32), SDS((GDN_HEADS, L, GDN_DV), F32),
              SDS((GDN_HEADS, L, C), F32), SDS((GDN_HEADS, L, GDN_DK), F32), SDS((GDN_HEADS, L, 1), F32)]
    return ins, outs, t_spec, shapes, nb


def _chunks(x, nb):
    return x.reshape(nb, x.shape[0] // nb, x.shape[1])


def _head_columns(bg, h):
    lane = lax.broadcasted_iota(jnp.int32, bg.shape, 1)
    beta = jnp.sum(jnp.where(lane == h, bg, 0.0), axis=1, keepdims=True)
    g = jnp.sum(jnp.where(lane == h + GDN_HEADS, bg, 0.0), axis=1, keepdims=True)
    return beta, g


def _gdn_prep_fwd_call(q, k, v, bg):
    L = q.shape[0]
    ins, outs, t_spec, shapes, nb = _gdn_prep_specs(L)

    def body(q_ref, k_ref, v_ref, bg_ref, *o_refs):
        beta, g = _head_columns(bg_ref[...], pl.program_id(1))
        res = _gdn_prep_math(_chunks(q_ref[...], nb), _chunks(k_ref[...], nb), _chunks(v_ref[...], nb),
                             _chunks(beta, nb), _chunks(g, nb))
        for o_ref, val in zip(o_refs, res):
            o_ref[0] = val.reshape(val.shape[0] * val.shape[1], val.shape[2])

    return pl.pallas_call(
        body, name="gdn_prep_fwd", grid=(L // (nb * GDN_CHUNK), GDN_HEADS), in_specs=ins, out_specs=outs + [t_spec],
        out_shape=shapes + [SDS((GDN_HEADS, L, GDN_CHUNK), F32)],
        compiler_params=_params(("parallel", "parallel"), VMEM_MID),
    )(q, k, v, bg)


def _gdn_prep_bwd_call(q, k, v, bg, t, cts):
    L = q.shape[0]
    ins, outs, t_spec, _, nb = _gdn_prep_specs(L)

    def body(q_ref, k_ref, v_ref, bg_ref, t_ref, c0, c1, c2, c3, c4, c5, dq_ref, dk_ref, dv_ref, dbg_ref):
        h = pl.program_id(1)
        beta, g = _head_columns(bg_ref[...], h)
        t_saved = _chunks(t_ref[0], nb)
        _, vjp = jax.vjp(lambda *a: _gdn_prep_math(*a, t_saved=t_saved)[:6], _chunks(q_ref[...], nb), _chunks(k_ref[...], nb),
                         _chunks(v_ref[...], nb), _chunks(beta, nb), _chunks(g, nb))
        dq, dk, dv, db, dg = vjp(tuple(_chunks(c[0], nb) for c in (c0, c1, c2, c3, c4, c5)))
        flat = lambda a: a.reshape(a.shape[0] * a.shape[1], a.shape[2])
        dq_ref[...] = flat(dq)
        dk_ref[...] = flat(dk)
        dv_ref[...] = flat(dv)

        @pl.when(h == 0)
        def _():
            dbg_ref[...] = jnp.zeros_like(dbg_ref)
        lane = lax.broadcasted_iota(jnp.int32, dbg_ref.shape, 1)
        dbg_ref[...] += jnp.where(lane == h, flat(db), 0.0) + jnp.where(lane == h + GDN_HEADS, flat(dg), 0.0)

    return pl.pallas_call(
        body, name="gdn_prep_bwd", grid=(L // (nb * GDN_CHUNK), GDN_HEADS), in_specs=ins + [t_spec] + outs, out_specs=ins,
        out_shape=[SDS(q.shape, F32), SDS(k.shape, F32), SDS(v.shape, F32), SDS(bg.shape, F32)],
        compiler_params=_params(("parallel", "arbitrary"), VMEM_MID),
    )(q, k, v, bg, t, *cts)


@jax.custom_vjp
def gdn_prep(q, k, v, bg):
    return tuple(_gdn_prep_fwd_call(q, k, v, bg)[:6])


def _gdn_prep_f(q, k, v, bg):
    res = _gdn_prep_fwd_call(q, k, v, bg)
    return tuple(res[:6]), (q, k, v, bg, res[6])


def _gdn_prep_b(res, cts):
    return tuple(_gdn_prep_bwd_call(*res, tuple(cts)))


gdn_prep.defvjp(_gdn_prep_f, _gdn_prep_b)


def _gdn_step_math(q_dec, w, u, qk, k_dec, gc, z, nw, state):
    H, C = q_dec.shape[0], q_dec.shape[1]
    v_new = u - _bdot(w, state, BNN)
    o = _bdot(q_dec, state, BNN) + _bdot(qk, v_new, BNN)
    gl = gc[:, C - 1:C, :]
    new_state = jnp.exp(gl) * state + _bdot(k_dec, v_new, BTN)
    return _f_gdn_post(jnp.concatenate([o[h] for h in range(H)], axis=1), z, nw)[0], new_state


GDN_SCAN_CHUNKS = 2


def _gdn_steps_math(q_dec, w, u, qk, k_dec, gc, z, nw, state):
    outs = []
    for i in range(q_dec.shape[1] // GDN_CHUNK):
        s = slice(i * GDN_CHUNK, (i + 1) * GDN_CHUNK)
        o, state = _gdn_step_math(q_dec[:, s], w[:, s], u[:, s], qk[:, s], k_dec[:, s], gc[:, s], z[s], nw, state)
        outs.append(o)
    return jnp.concatenate(outs, axis=0), state


def _gdn_scan_specs(L, rev):
    H = GDN_HEADS
    C = GDN_CHUNK * min(GDN_SCAN_CHUNKS, L // GDN_CHUNK)
    nc = L // C
    cc = (lambda c: nc - 1 - c) if rev else (lambda c: c)
    ins = [pl.BlockSpec((H, C, GDN_DK), lambda c: (0, cc(c), 0)), pl.BlockSpec((H, C, GDN_DK), lambda c: (0, cc(c), 0)),
           pl.BlockSpec((H, C, GDN_DV), lambda c: (0, cc(c), 0)), pl.BlockSpec((H, C, GDN_CHUNK), lambda c: (0, cc(c), 0)),
           pl.BlockSpec((H, C, GDN_DK), lambda c: (0, cc(c), 0)), pl.BlockSpec((H, C, 1), lambda c: (0, cc(c), 0))]
    o_spec = pl.BlockSpec((C, H * GDN_DV), lambda c: (cc(c), 0))
    nw_spec = pl.BlockSpec((1, H * GDN_DV), lambda c: (0, 0))
    s_spec = pl.BlockSpec((1, H, GDN_DK, GDN_DV), lambda c: (cc(c), 0, 0, 0))
    return ins + [o_spec, nw_spec], o_spec, s_spec, nc


def _gdn_scan_fwd_call(q_dec, w, u, qk, k_dec, gc, z, nw):
    L = q_dec.shape[1]
    ins, o_spec, s_spec, nc = _gdn_scan_specs(L, False)

    def body(qd_ref, w_ref, u_ref, qk_ref, kd_ref, gc_ref, z_ref, nw_ref, o_ref, sin_ref, s_ref):
        c = pl.program_id(0)

        @pl.when(c == 0)
        def _():
            s_ref[...] = jnp.zeros_like(s_ref)
        st = s_ref[...]
        sin_ref[0] = st
        o, ns = _gdn_steps_math(qd_ref[...], w_ref[...], u_ref[...], qk_ref[...], kd_ref[...], gc_ref[...], z_ref[...], nw_ref[...], st)
        o_ref[...] = o
        s_ref[...] = ns

    return pl.pallas_call(
        body, name="gdn_scan_fwd", grid=(nc,), in_specs=ins, out_specs=[o_spec, s_spec],
        out_shape=[SDS((L, GDN_HEADS * GDN_DV), F32), SDS((nc, GDN_HEADS, GDN_DK, GDN_DV), F32)],
        scratch_shapes=[pltpu.VMEM((GDN_HEADS, GDN_DK, GDN_DV), F32)],
        compiler_params=_params(("arbitrary",), VMEM_MID),
    )(q_dec, w, u, qk, k_dec, gc, z, nw)


def _gdn_scan_bwd_call(q_dec, w, u, qk, k_dec, gc, z, nw, s_in, do):
    L = q_dec.shape[1]
    ins, o_spec, s_spec, nc = _gdn_scan_specs(L, True)

    def body(qd_ref, w_ref, u_ref, qk_ref, kd_ref, gc_ref, z_ref, nw_ref, sin_ref, do_ref,
             dqd_ref, dw_ref, du_ref, dqk_ref, dkd_ref, dgc_ref, dz_ref, dnw_ref, ds_ref):
        c = pl.program_id(0)

        @pl.when(c == 0)
        def _():
            ds_ref[...] = jnp.zeros_like(ds_ref)
            dnw_ref[...] = jnp.zeros_like(dnw_ref)
        _, vjp = jax.vjp(_gdn_steps_math, qd_ref[...], w_ref[...], u_ref[...], qk_ref[...], kd_ref[...], gc_ref[...],
                         z_ref[...], nw_ref[...], sin_ref[0])
        dqd, dw, du, dqk, dkd, dgc, dz, dnw, dst = vjp((do_ref[...], ds_ref[...]))
        dqd_ref[...] = dqd
        dw_ref[...] = dw
        du_ref[...] = du
        dqk_ref[...] = dqk
        dkd_ref[...] = dkd
        dgc_ref[...] = dgc
        dz_ref[...] = dz
        dnw_ref[...] += dnw
        ds_ref[...] = dst

    return pl.pallas_call(
        body, name="gdn_scan_bwd", grid=(nc,), in_specs=ins + [s_spec, o_spec], out_specs=ins,
        out_shape=[SDS(t.shape, F32) for t in (q_dec, w, u, qk, k_dec, gc, z, nw)],
        scratch_shapes=[pltpu.VMEM((GDN_HEADS, GDN_DK, GDN_DV), F32)],
        compiler_params=_params(("arbitrary",), VMEM_MID),
    )(q_dec, w, u, qk, k_dec, gc, z, nw, s_in, do)


@jax.custom_vjp
def gdn_scan(q_dec, w, u, qk, k_dec, gc, z, nw):
    return _gdn_scan_fwd_call(q_dec, w, u, qk, k_dec, gc, z, nw)[0]


def _gdn_scan_f(*args):
    o, s_in = _gdn_scan_fwd_call(*args)
    return o, (*args, s_in)


def _gdn_scan_b(res, do):
    return tuple(_gdn_scan_bwd_call(*res, do))


gdn_scan.defvjp(_gdn_scan_f, _gdn_scan_b)


def _silu(x):
    return x * jax.nn.sigmoid(x)


def _gelu_tanh(x):
    return 0.5 * x * (1.0 + jnp.tanh(math.sqrt(2.0 / math.pi) * (x + 0.044715 * (x * x * x))))


def _f_lnmod(x, nw, sc, sh, bsc, bsh):
    xn = x * lax.rsqrt(jnp.mean(x * x, axis=-1, keepdims=True) + NORM_EPS) * nw
    return (xn * (1.0 + (sc + bsc)) + (sh + bsh),)


def _f_s5_act(ys, u, d):
    return (_gelu_tanh(ys + d * u),)


def _f_s5_gate(y2, t, z):
    return (y2 * jax.nn.sigmoid(t) * _silu(z),)


def _f_res(x, y, gate, bgate):
    return (x + (gate + bgate) * y,)


def _heads(x, width, fn):
    return jnp.concatenate([fn(x[:, i * width:(i + 1) * width]) for i in range(x.shape[1] // width)], axis=1)


def _l2n(x):
    return x * lax.rsqrt(jnp.sum(x * x, axis=-1, keepdims=True) + NORM_EPS)


def _f_betag(ba, alog, dtb):
    col = lax.broadcasted_iota(jnp.int32, ba.shape, 1)
    t = ba + dtb
    softplus = jnp.maximum(t, 0.0) + jnp.log1p(jnp.exp(-jnp.abs(t)))
    g = -jnp.exp(alog) * softplus
    return (jnp.where(col < GDN_HEADS, jax.nn.sigmoid(ba), jnp.where(col < 2 * GDN_HEADS, g, 0.0)),)


def _f_gdn_post(o, z, nw):
    on = _heads(o, GDN_DV, lambda t: t * lax.rsqrt(jnp.mean(t * t, axis=-1, keepdims=True) + NORM_EPS))
    return (on * nw * _silu(z),)


def _f_loss(x, tgt, fw):
    y = x * lax.rsqrt(jnp.mean(x * x, axis=-1, keepdims=True) + NORM_EPS) * fw
    err = y - tgt
    return (0.5 * jnp.mean(err * err, axis=-1, keepdims=True),)


def _ada_mod_call(c_all, ada_w):
    n = ada_w.shape[2]

    def body(c_ref, w_ref, o_ref):
        ca = _silu(c_ref[...])
        for l in range(ada_w.shape[0]):
            o_ref[l] = _bdot(ca, w_ref[l])

    return pl.pallas_call(body, name="ada_mod", out_shape=SDS((ada_w.shape[0], N_DEV, n), F32),
                          compiler_params=_params(None, VMEM_MID))(c_all, ada_w)


def _ada_grad_call(c_all, dmod):
    nl, _, n = dmod.shape

    def body(c_ref, d_ref, o_ref):
        ca = _silu(c_ref[...])
        for l in range(nl):
            o_ref[l] = _hdot(ca, d_ref[l], TN)

    return pl.pallas_call(body, name="ada_grad", out_shape=SDS((nl, c_all.shape[1], n), F32),
                          compiler_params=_params(None, VMEM_MID))(c_all, dmod)


ADAM_ROWS = 512


def _adamw(g, w, m, v):
    m2 = ADAM_B1 * m + (1.0 - ADAM_B1) * g
    v2 = ADAM_B2 * v + (1.0 - ADAM_B2) * (g * g)
    m_hat = m2 / (1.0 - ADAM_B1 ** ADAM_STEP)
    v_hat = v2 / (1.0 - ADAM_B2 ** ADAM_STEP)
    return g, -ADAM_LR * (m_hat / (jnp.sqrt(v_hat) + ADAM_EPS) + ADAM_WD * w), m2, v2


def _adam_call(gs, w, m, v, name, rows=None, by_cols=False):
    n, r, cols = gs.shape
    if by_cols:
        blk = pl.BlockSpec((r, LANES), lambda i: (0, i))
        g_blk, grid = pl.BlockSpec((n, r, LANES), lambda i: (0, 0, i)), (cols // LANES,)
    else:
        rows = rows or ADAM_ROWS
        blk = pl.BlockSpec((rows, cols), lambda i: (i, 0))
        g_blk, grid = pl.BlockSpec((n, rows, cols), lambda i: (0, i, 0)), (r // rows,)

    def body(g_ref, w_ref, m_ref, v_ref, go_ref, d_ref, mo_ref, vo_ref):
        g = g_ref[0].astype(F32)
        for s in range(1, n):
            g = g + g_ref[s].astype(F32)
        for o_ref, val in zip((go_ref, d_ref, mo_ref, vo_ref), _adamw(g, w_ref[...], m_ref[...], v_ref[...])):
            o_ref[...] = val

    return pl.pallas_call(
        body, name=name, grid=grid, in_specs=[g_blk, blk, blk, blk],
        out_specs=[blk, blk, blk, blk], out_shape=[SDS((r, cols), F32)] * 4,
        compiler_params=_params(("parallel",), VMEM_MID),
    )(gs, w, m, v)


def _sum_call(gs, name, rows):
    n, r, _ = gs.shape

    def body(g_ref, o_ref):
        g = g_ref[0].astype(F32)
        for s in range(1, n):
            g = g + g_ref[s].astype(F32)
        o_ref[...] = g

    return pl.pallas_call(
        body, name=name, grid=(r // rows,),
        in_specs=[pl.BlockSpec((n, rows, LANES), lambda i: (0, i, 0))],
        out_specs=pl.BlockSpec((rows, LANES), lambda i: (i, 0)), out_shape=SDS((r, LANES), F32),
        compiler_params=_params(("parallel",), VMEM_MID),
    )(gs)


def _allgather_call(x_shard, name):
    m_per, n = x_shard.shape

    def body(x_ref, out_ref, send_sems, recv_sems, local_sem):
        x, y, c = lax.axis_index("x"), lax.axis_index("y"), lax.axis_index("c")
        me, sibling = (x, y, c), (x, y, 1 - c)
        chips = [(1 - x, y), (x, 1 - y), (1 - x, 1 - y)]

        def rows(px, py, pc):
            return out_ref.at[pl.ds((4 * px + 2 * py + pc) * m_per, m_per), :]

        def copy(k, block, to, src=None):
            return pltpu.make_async_remote_copy(
                src_ref=rows(*block) if src is None else src, dst_ref=rows(*block),
                send_sem=send_sems.at[k], recv_sem=recv_sems.at[k], device_id=to, device_id_type=pl.DeviceIdType.MESH)

        mine = pltpu.make_async_copy(x_ref, rows(*me), local_sem)
        mine.start()
        first = [copy(0, me, sibling, src=x_ref)]
        first += [copy(1 + j, me, (*chip, c), src=x_ref) for j, chip in enumerate(chips)]
        for cp in first:
            cp.start()
        passed = [copy(4 + j, (*chip, c), sibling) for j, chip in enumerate(chips)]
        for j, chip in enumerate(chips):
            copy(1 + j, (*chip, c), me).wait_recv()
            passed[j].start()
        copy(0, sibling, me).wait_recv()
        for j, chip in enumerate(chips):
            copy(4 + j, (*chip, 1 - c), me).wait_recv()
        for cp in first + passed:
            cp.wait_send()
        mine.wait()

    vmem = pl.BlockSpec(memory_space=pltpu.VMEM)
    return pl.pallas_call(
        body, name=name, out_shape=SDS((N_DEV * m_per, n), x_shard.dtype), in_specs=[vmem], out_specs=vmem,
        scratch_shapes=[pltpu.SemaphoreType.DMA((7,)), pltpu.SemaphoreType.DMA((7,)), pltpu.SemaphoreType.DMA],
    )(x_shard)


def _gather_weights_call(shards, name):
    nw = len(shards)

    def body(*refs):
        x_refs, out_refs = refs[:nw], refs[nw:2 * nw]
        send_sems, recv_sems, local_sems = refs[2 * nw:]
        x, y, c = lax.axis_index("x"), lax.axis_index("y"), lax.axis_index("c")
        me, sibling = (x, y, c), (x, y, 1 - c)
        chips = [(1 - x, y), (x, 1 - y), (1 - x, 1 - y)]

        def slot(w, px, py, pc):
            return out_refs[w].at[4 * px + 2 * py + pc]

        def copy(w, k, block, to, src=None):
            dst = slot(w, *block)
            return pltpu.make_async_remote_copy(
                src_ref=dst if src is None else src, dst_ref=dst, send_sem=send_sems.at[7 * w + k],
                recv_sem=recv_sems.at[7 * w + k], device_id=to, device_id_type=pl.DeviceIdType.MESH)

        mines = [pltpu.make_async_copy(x_refs[w], slot(w, *me), local_sems.at[w]) for w in range(nw)]
        for cp in mines:
            cp.start()
        first = [copy(w, 0, me, sibling, src=x_refs[w]) for w in range(nw)]
        first += [copy(w, 1 + j, me, (*chip, c), src=x_refs[w]) for w in range(nw) for j, chip in enumerate(chips)]
        for cp in first:
            cp.start()
        passed = []
        for w in range(nw):
            for j, chip in enumerate(chips):
                copy(w, 1 + j, (*chip, c), me).wait_recv()
                fwd = copy(w, 4 + j, (*chip, c), sibling)
                fwd.start()
                passed.append(fwd)
        for w in range(nw):
            copy(w, 0, sibling, me).wait_recv()
            for j, chip in enumerate(chips):
                copy(w, 4 + j, (*chip, 1 - c), me).wait_recv()
        for cp in first + passed:
            cp.wait_send()
        for cp in mines:
            cp.wait()

    hbm = pl.BlockSpec(memory_space=pl.ANY)
    return pl.pallas_call(
        body, name=name, out_shape=[SDS((N_DEV,) + s.shape, s.dtype) for s in shards],
        in_specs=[hbm] * nw, out_specs=[hbm] * nw,
        scratch_shapes=[pltpu.SemaphoreType.DMA((7 * nw,)), pltpu.SemaphoreType.DMA((7 * nw,)), pltpu.SemaphoreType.DMA((nw,))],
    )(*shards)


_HBM = pl.BlockSpec(memory_space=pltpu.HBM)
_SEM = pl.BlockSpec(memory_space=pltpu.SEMAPHORE)
_DATAFLOW = pltpu.SideEffectType.DATAFLOW_SIDE_EFFECTING


def _spread_start_call(srcs, per_peer, name, after):
    nw = len(srcs)
    lands = [lax.empty((N_DEV,) + (s.shape[1:] if per_peer else s.shape), s.dtype) for s in srcs]

    def body(*refs):
        src_refs, land_refs = refs[:nw], refs[nw:2 * nw]
        send_sems, recv_sems, token = refs[2 * nw + 1], refs[2 * nw + 2], refs[-1]
        x, y, c = lax.axis_index("x"), lax.axis_index("y"), lax.axis_index("c")
        me = 4 * x + 2 * y + c
        for w in range(nw):
            for k in range(1, N_DEV):
                px = 1 - x if k & 4 else x
                py = 1 - y if k & 2 else y
                pc = 1 - c if k & 1 else c
                src = src_refs[w].at[4 * px + 2 * py + pc] if per_peer else src_refs[w]
                pltpu.make_async_remote_copy(
                    src_ref=src, dst_ref=land_refs[w].at[me], send_sem=send_sems.at[w], recv_sem=recv_sems.at[w],
                    device_id=(px, py, pc), device_id_type=pl.DeviceIdType.MESH).start()
        token[...] = jnp.zeros_like(token)

    hbm = lambda a: pltpu.with_memory_space_constraint(a, pltpu.HBM)
    res = pl.pallas_call(
        body, name=name,
        out_shape=(pltpu.SemaphoreType.DMA((nw,)), pltpu.SemaphoreType.DMA((nw,)))
        + tuple(pltpu.HBM(s.shape, s.dtype) for s in srcs) + tuple(pltpu.HBM(l.shape, l.dtype) for l in lands)
        + (SDS((SUBLANES, LANES), F32),),
        in_specs=[_HBM] * (2 * nw) + [pl.BlockSpec(memory_space=pl.ANY)],
        out_specs=(_SEM, _SEM) + (_HBM,) * (2 * nw) + (pl.BlockSpec(memory_space=pltpu.VMEM),),
        input_output_aliases={i: i + 2 for i in range(2 * nw)},
        compiler_params=pltpu.CompilerParams(has_side_effects=_DATAFLOW),
    )(*[hbm(s) for s in srcs], *[hbm(l) for l in lands], after)
    return res[0], res[1], res[2:2 + nw], res[2 + nw:2 + 2 * nw], res[-1]


def _spread_wait_call(send_sems, recv_sems, srcs, lands, after, name):
    nw = len(lands)

    def body(*refs):
        land_refs = refs[nw:2 * nw]
        s_sems, r_sems = refs[2 * nw], refs[2 * nw + 1]
        x, y, c = lax.axis_index("x"), lax.axis_index("y"), lax.axis_index("c")
        for w in range(nw):
            seven = land_refs[w].at[pl.ds(0, N_DEV - 1)]
            all_seven = pltpu.make_async_remote_copy(
                src_ref=seven, dst_ref=seven, send_sem=s_sems.at[w], recv_sem=r_sems.at[w],
                device_id=(x, y, c), device_id_type=pl.DeviceIdType.MESH)
            all_seven.wait_send()
            all_seven.wait_recv()

    res = pl.pallas_call(
        body, name=name,
        out_shape=tuple(pltpu.HBM(s.shape, s.dtype) for s in srcs) + tuple(pltpu.HBM(l.shape, l.dtype) for l in lands),
        in_specs=[_HBM] * (2 * nw) + [_SEM, _SEM, pl.BlockSpec(memory_space=pl.ANY)], out_specs=(_HBM,) * (2 * nw),
        input_output_aliases={i: i for i in range(2 * nw)},
        compiler_params=pltpu.CompilerParams(has_side_effects=_DATAFLOW),
    )(*srcs, *lands, send_sems, recv_sems, after)
    return res[:nw], res[nw:]


def _join_cols_call(w8, name):
    _, k, n = w8.shape
    tk = _tile(k, 256)

    def body(w_ref, o_ref):
        for s in range(N_DEV):
            o_ref[:, n * s:n * (s + 1)] = w_ref[s]

    return pl.pallas_call(body, name=name, grid=(k // tk,), in_specs=[pl.BlockSpec((N_DEV, tk, n), lambda i: (0, i, 0))],
                          out_specs=pl.BlockSpec((tk, N_DEV * n), lambda i: (i, 0)), out_shape=SDS((k, N_DEV * n), w8.dtype),
                          compiler_params=_params(("parallel",), VMEM_MID))(w8)


def _split_cols_call(g, name, dtype):
    k, n8 = g.shape
    n = n8 // N_DEV
    tk = _tile(k, 256)

    def body(g_ref, o_ref):
        for s in range(N_DEV):
            o_ref[s] = g_ref[:, n * s:n * (s + 1)].astype(dtype)

    return pl.pallas_call(body, name=name, grid=(k // tk,), in_specs=[pl.BlockSpec((tk, n8), lambda i: (i, 0))],
                          out_specs=pl.BlockSpec((N_DEV, tk, n), lambda i: (0, i, 0)), out_shape=SDS((N_DEV, k, n), dtype),
                          compiler_params=_params(("parallel",), VMEM_MID))(g)


def _pack(parts, rows_multiple):
    flat = jnp.concatenate([p.reshape(-1) for p in parts])
    unit = rows_multiple * LANES
    padded = -(-flat.shape[0] // unit) * unit
    flat = jnp.concatenate([flat, jnp.zeros((padded - flat.shape[0],), F32)])
    return flat.reshape(-1, LANES)


def _groups_last(a):
    x, y = a.shape[-2:]
    return jnp.transpose(a.reshape(S5_GROUPS, x, y), (1, 2, 0)).reshape(x * y, S5_GROUPS)


def _groups_first(a, shape):
    x, y = shape[-2:]
    return jnp.transpose(a.reshape(x, y, S5_GROUPS), (2, 0, 1)).reshape(shape)


def _unpack(buf, shapes):
    flat = buf.reshape(-1)
    out, off = [], 0
    for s in shapes:
        n = math.prod(s)
        out.append(flat[off:off + n].reshape(s))
        off += n
    return out


def _row_tile(L):
    return 256 if L % 256 == 0 else L


def _layer0_mix(diff, const):
    x, mod, norm_w, lam_re, lam_im, log_dt, b_re, b_im, c_re, c_im, s5_d, *slots = diff
    ada_b, weights = const
    L = x.shape[0]
    tm = _row_tile(L)
    mods = mod.reshape(2, 1, D_MODEL)
    biases = ada_b.reshape(2, 1, D_MODEL)
    op_ln0 = make_rowwise(_f_lnmod, "ln0", tm, 1, 5, pass_first=True)
    h, x = op_ln0((x,), (norm_w.reshape(1, D_MODEL), mods[1], mods[0], biases[1], biases[0]))
    u, z = make_proj("s5_in")(h, tuple(weights), tuple(slots))
    blocks = _s5_block_params(lam_re, lam_im, log_dt, b_re, b_im, c_re, c_im)
    y2 = make_s5_core(min(S5_TL, L))(u, *blocks, s5_d.reshape(1, D_INNER))
    return x, y2, z


def _glu_gate_fwd_call(y2, w, z, name):
    m, k = y2.shape
    n = w.shape[1]
    tm, tn = _tile(m, 1024), _tile(n, 512)

    def body(a_ref, w_ref, z_ref, t_ref, y4_ref):
        j = pl.program_id(1)
        t = _bdot(a_ref[...], w_ref[...])
        t_ref[...] = t
        y2_tile = a_ref[:, pl.ds(pl.multiple_of(j * tn, LANES), tn)]
        y4_ref[...] = _f_s5_gate(y2_tile, t, z_ref[...])[0]

    tile = pl.BlockSpec((tm, tn), lambda i, j: (i, j))
    return pl.pallas_call(
        body, name=name, grid=(m // tm, n // tn),
        in_specs=[pl.BlockSpec((tm, k), lambda i, j: (i, 0)), pl.BlockSpec((k, tn), lambda i, j: (0, j)), tile],
        out_specs=[tile, tile], out_shape=[SDS((m, n), F32), SDS((m, n), F32)],
        compiler_params=_params(("parallel", "parallel"), VMEM_MID),
    )(y2, w, z)


def make_glu_gate(name, tm):
    gate = make_rowwise(_f_s5_gate, name + "_gate", tm, 3, 0)

    @jax.custom_vjp
    def op(y2, w, grad_slot, z):
        return _glu_gate_fwd_call(y2, w, z, name + "_fwd")[1]

    def fwd(y2, w, grad_slot, z):
        t, y4 = _glu_gate_fwd_call(y2, w, z, name + "_fwd")
        return y4, (y2, w, t, z)

    def bwd(res, dy4):
        y2, w, t, z = res
        (dy2_gate, dt, dz), _ = gate.run_bwd((y2, t, z), (), (dy4,))
        return _matmul(dt, w, "nt", name + "_dx", add=dy2_gate), jnp.zeros_like(w), _matmul(y2, dt, "tn", name + "_dw"), dz

    op.defvjp(fwd, bwd)
    return op


def _layer0_out(diff, weights):
    y2, z, *slots = diff
    y4 = make_glu_gate("s5_glu", _row_tile(y2.shape[0]))(y2, weights[0], slots[0], z)
    return make_mm("s5_out")(y4, weights[1], slots[1])


def _f_res_lnmod(x, o, gate, bgate, nw, sc, sh, bsc, bsh):
    (x1,) = _f_res(x, o, gate, bgate)
    return _f_lnmod(x1, nw, sc, sh, bsc, bsh) + (x1,)


def _f_res_loss(x, y, tgt, gate, bgate, fw):
    return _f_loss(_f_res(x, y, gate, bgate)[0], tgt, fw)


def _layer1_loss(diff, const):
    x, o, gate0, mod, norm_w, conv_w, a_log, dt_bias, gdn_nw, final_nw, *slots = diff
    tgt, bgate0, ada_b, weights = const
    L = x.shape[0]
    tm = _row_tile(L)
    mods = mod.reshape(3, 1, D_MODEL)
    biases = ada_b.reshape(3, 1, D_MODEL)
    h, x1 = make_rowwise(_f_res_lnmod, "res0_ln1", tm, 2, 7)(
        (x, o), (gate0.reshape(1, D_MODEL), bgate0.reshape(1, D_MODEL), norm_w.reshape(1, D_MODEL), mods[1], mods[0], biases[1], biases[0]))
    q0, k0, v0, gz, ba = make_proj("gdn_in", w_rows=True)(h, tuple(weights[0:5]), tuple(slots[0:5]))
    cw = jnp.concatenate([conv_w, jnp.zeros((SUBLANES - GDN_CONV, GDN_CONV_CH), F32)], axis=0)
    q = make_conv_act(lambda t: _l2n(_silu(t)) * (GDN_DK ** -0.5), "gdn_conv_q")(q0, cw[:, :GDN_QK])
    k = make_conv_act(lambda t: _l2n(_silu(t)), "gdn_conv_k")(k0, cw[:, GDN_QK:2 * GDN_QK])
    v = make_conv_act(_silu, "gdn_conv_v")(v0, cw[:, 2 * GDN_QK:])
    pad = jnp.zeros((LANES - 2 * GDN_HEADS,), F32)
    alog_row = jnp.concatenate([jnp.zeros((GDN_HEADS,), F32), a_log, pad]).reshape(1, LANES)
    dtb_row = jnp.concatenate([jnp.zeros((GDN_HEADS,), F32), dt_bias, pad]).reshape(1, LANES)
    (bg,) = make_rowwise(_f_betag, "gdn_bg", tm, 1, 2)((ba,), (alog_row, dtb_row))
    nw_row = jnp.tile(gdn_nw, GDN_HEADS).reshape(1, D_INNER)
    on = gdn_scan(*gdn_prep(q, k, v, bg), gz, nw_row)
    y = make_mm("gdn_out")(on, weights[5], slots[5])
    (lt,) = make_rowwise(_f_res_loss, "res1_loss", tm, 3, 3)((x1, y, tgt), (mods[2], biases[2], final_nw.reshape(1, D_MODEL)))
    return jnp.sum(lt)


VEC_NAMES = ("ada_b", "norm_w", "s5_lambda_re", "s5_lambda_im", "s5_log_dt", "s5_d", "gdn_a_log", "gdn_dt_bias", "final_norm_w")
MAT_NAMES = ("s5_b_re", "s5_b_im", "s5_c_re", "s5_c_im")
S5_BIG = ("s5_w_in", "s5_w_glu", "s5_w_out")
GDN_BIG = ("gdn_w_in", "gdn_w_out")
BIG_NAMES = S5_BIG + GDN_BIG
WEIGHT_ORDER = ("ada_w", "ada_b", "norm_w", "s5_w_in", "s5_lambda_re", "s5_lambda_im", "s5_log_dt", "s5_b_re", "s5_b_im",
                "s5_c_re", "s5_c_im", "s5_d", "s5_w_glu", "s5_w_out", "gdn_w_in", "gdn_conv_w", "gdn_a_log", "gdn_dt_bias",
                "gdn_norm_w", "gdn_w_out", "final_norm_w")


def _step(x, c, W, M, V, tgt):
    L = x.shape[1]
    ix, iy, ic = lax.axis_index("x"), lax.axis_index("y"), lax.axis_index("c")
    me = 4 * ix + 2 * iy + ic
    n_ada = W["ada_w"].shape[2]
    n_conv = W["gdn_conv_w"].shape[2]
    n_gnw = W["gdn_norm_w"].shape[1]

    g1 = _allgather_call(_pack([c, W["gdn_conv_w"], W["gdn_norm_w"]], SUBLANES), "gather_small_in")
    g1 = g1.reshape(N_DEV, -1)
    c_all = g1[:, :D_MODEL]
    conv_w = g1[:, D_MODEL:D_MODEL + GDN_CONV * n_conv].reshape(N_DEV, GDN_CONV, n_conv).transpose(1, 0, 2).reshape(GDN_CONV, -1)
    gdn_nw = g1[:, D_MODEL + GDN_CONV * n_conv:D_MODEL + GDN_CONV * n_conv + n_gnw].reshape(-1)
    mod_part = _ada_mod_call(c_all, W["ada_w"])
    g2 = _allgather_call(_pack([mod_part], SUBLANES), "gather_mod").reshape(N_DEV, -1)
    mod_all = g2[:, :2 * N_DEV * n_ada].reshape(N_DEV, 2, N_DEV, n_ada)
    mod_raw = lax.dynamic_index_in_dim(mod_all, me, axis=2, keepdims=False)
    mod_raw = mod_raw.transpose(1, 0, 2).reshape(2, 3 * D_MODEL)

    shard = lambda n: W[n][0].astype(BF16)
    (w_in5_parts,) = _gather_weights_call([shard("s5_w_in")], "gather_s5_w_in")
    late = _spread_start_call([shard("s5_w_glu"), shard("s5_w_out")], False, "gather_s5_late_start", w_in5_parts)
    turned = lambda a: jnp.transpose(a[0])
    g_send, g_recv, g_srcs, g_lands, g_token = _spread_start_call(
        [turned(W["gdn_w_in"]).astype(BF16), shard("gdn_w_out")], False, "gather_gdn_start", late[4])
    w_in5 = _join_cols_call(w_in5_parts, "join_s5_w_in")
    slot = lambda *s: jnp.zeros(s, F32)
    two = 2 * D_MODEL
    diff_mix = (x[0], mod_raw[0, :two] + g_token[0, 0], W["norm_w"][0], W["s5_lambda_re"][0], W["s5_lambda_im"][0], W["s5_log_dt"][0],
                W["s5_b_re"][0], W["s5_b_im"][0], W["s5_c_re"][0], W["s5_c_im"][0], W["s5_d"][0],
                slot(D_MODEL, D_INNER), slot(D_MODEL, D_INNER))

    (xp, y2, z5), vjp_mix = jax.vjp(lambda d: _layer0_mix(d, (W["ada_b"][0, :two], (w_in5[:, :D_INNER], w_in5[:, D_INNER:]))), diff_mix)
    l_srcs, l_lands = _spread_wait_call(late[0], late[1], late[2], late[3], y2, "gather_s5_late_wait")
    w_glu, w_o5 = [lax.dynamic_update_slice(land, src[None], (me, 0, 0)).reshape(-1, src.shape[1]) for land, src in zip(l_lands, l_srcs)]
    diff_out = (y2, z5, slot(D_INNER, D_INNER), slot(D_INNER, D_MODEL))
    o5, vjp_out = jax.vjp(lambda d: _layer0_out(d, (w_glu, w_o5)), diff_out)
    g_srcs, g_lands = _spread_wait_call(g_send, g_recv, g_srcs, g_lands, o5, "gather_gdn_wait")
    gdn_full = [lax.dynamic_update_slice(land, src[None], (me, 0, 0)) for land, src in zip(g_lands, g_srcs)]
    w_ing = gdn_full[0].reshape(GDN_PROJ, D_MODEL)
    w_ba = jnp.concatenate([w_ing[GDN_CONV_CH + D_INNER:], jnp.zeros((LANES - 2 * GDN_HEADS, D_MODEL), BF16)], axis=0)
    weights1 = (w_ing[:GDN_QK], w_ing[GDN_QK:2 * GDN_QK], w_ing[2 * GDN_QK:GDN_CONV_CH],
                w_ing[GDN_CONV_CH:GDN_CONV_CH + D_INNER], w_ba, gdn_full[1].reshape(D_INNER, D_MODEL))
    slots1 = tuple(jnp.zeros(w.shape, F32) for w in weights1)
    diff1 = (xp, o5, mod_raw[0, two:], mod_raw[1], W["norm_w"][1], conv_w, W["gdn_a_log"][0], W["gdn_dt_bias"][0], gdn_nw,
             W["final_norm_w"], *slots1)
    loss_local, vjp1 = jax.vjp(lambda d: _layer1_loss(d, (tgt[0], W["ada_b"][0, two:], W["ada_b"][1], weights1)), diff1)
    ((dxp, do5, dmod_gate, dmod1, d_norm_w1, d_conv, d_alog, d_dtb, d_gnw, d_fnw, d_wq, d_wk, d_wv, d_wgz, d_wba, d_wog),) = vjp1(
        jnp.ones((), F32))
    loss = lax.psum(loss_local, MESH_AXES)

    rows = lambda d: d.reshape(N_DEV, d.shape[0] // N_DEV, d.shape[1])
    d_ing = jnp.concatenate([d_wq, d_wk, d_wv, d_wgz, d_wba[:2 * GDN_HEADS]], axis=0).astype(BF16).reshape(N_DEV, -1, D_MODEL)
    s_send, s_recv, s_srcs, s_lands, s_token = _spread_start_call([d_ing, rows(d_wog).astype(BF16)], True, "scatter_gdn_start", dxp)
    ((dy2, dz5, d_wglu, d_wo5),) = vjp_out(do5.at[0, 0].add(s_token[0, 0]))
    t_send, t_recv, t_srcs, t_lands, t_token = _spread_start_call(
        [rows(d_wglu).astype(BF16), rows(d_wo5).astype(BF16)], True, "scatter_s5_late_start", dy2)
    ((dx, dmod_ss, d_norm_w0, d_lre, d_lim, d_logdt, d_bre, d_bim, d_cre, d_cim, d_s5d, d_wu, d_wz),) = vjp_mix(
        (dxp.at[0, 0].add(t_token[0, 0]), dy2, dz5))
    dmod = jnp.stack([jnp.concatenate([dmod_ss, dmod_gate]), dmod1])
    d_norm_w = jnp.stack([d_norm_w0, d_norm_w1])
    vec_parts = [dmod, d_norm_w, d_lre, d_lim, d_logdt, d_s5d, d_alog, d_dtb, d_fnw]
    tail_parts = [d_conv, d_gnw]
    mat_parts = [_groups_last(d) for d in (d_bre, d_bim, d_cre, d_cim)]
    n_vec = sum(math.prod(p.shape) for p in vec_parts)
    m_send, m_recv, m_srcs, m_lands, m_token = _spread_start_call(
        [_pack(vec_parts + tail_parts, ADAM_ROWS), _pack(mat_parts, SUBLANES).astype(BF16)], False, "gather_small_grads_start", dx)
    d_in5 = _split_cols_call(jnp.concatenate([d_wu.at[0, 0].add(m_token[0, 0]), d_wz], axis=1), "split_s5_w_in", BF16)
    u_send, u_recv, u_srcs, u_lands, u_token = _spread_start_call([d_in5], True, "scatter_s5_in_start", m_token)
    t_srcs, t_lands = _spread_wait_call(t_send, t_recv, t_srcs, t_lands, u_token, "scatter_s5_late_wait")
    s_srcs, s_lands = _spread_wait_call(s_send, s_recv, s_srcs, s_lands, t_lands[0], "scatter_gdn_wait")
    big = {}

    def owner_update(land, src, n):
        mine = lax.dynamic_index_in_dim(src, me, 0, keepdims=True)
        parts = lax.dynamic_update_slice(land, mine, (me, 0, 0))
        if n == "gdn_w_in":
            outs = _adam_call(parts, turned(W[n]), turned(M[n]), turned(V[n]), "adam_" + n, by_cols=True)
            return [jnp.transpose(o) for o in outs]
        return _adam_call(parts, W[n][0], M[n][0], V[n][0], "adam_" + n, rows=_tile(W[n].shape[1], 128))

    for land, src, n in zip(tuple(t_lands) + tuple(s_lands), tuple(t_srcs) + tuple(s_srcs), ("s5_w_glu", "s5_w_out") + GDN_BIG):
        big[n] = owner_update(land, src, n)

    m_srcs, m_lands = _spread_wait_call(m_send, m_recv, m_srcs, m_lands, big["gdn_w_out"][0], "gather_small_grads_wait")
    sg_vec, sg_mat = [lax.dynamic_update_slice(land, src[None], (me, 0, 0)) for land, src in zip(m_lands, m_srcs)]
    tot_vec = _sum_call(sg_vec, "sum_vec_grads", ADAM_ROWS)
    tot_mat = _sum_call(sg_mat, "sum_mat_grads", ADAM_ROWS)
    g_conv, g_gnw = _unpack(tot_vec.reshape(-1)[n_vec:], [d_conv.shape, d_gnw.shape])
    g_conv_mine = lax.dynamic_slice_in_dim(g_conv, me * n_conv, n_conv, axis=1)
    g_gnw_mine = lax.dynamic_slice_in_dim(g_gnw, me * n_gnw, n_gnw, axis=0)
    vec_names = VEC_NAMES + ("gdn_conv_w", "gdn_norm_w")
    vec_g = _pack([tot_vec.reshape(-1)[:n_vec], g_conv_mine, g_gnw_mine], ADAM_ROWS)
    vec = _adam_call(vec_g[None], _pack([W[n] for n in vec_names], ADAM_ROWS), _pack([M[n] for n in vec_names], ADAM_ROWS),
                     _pack([V[n] for n in vec_names], ADAM_ROWS), "adam_vec")
    vec = [_unpack(b, [W[n].shape for n in vec_names]) for b in vec]
    mats = []
    for name, g_mat in zip(MAT_NAMES, _unpack(tot_mat, [p.shape for p in mat_parts])):
        outs = _adam_call(g_mat[None], _groups_last(W[name]), _groups_last(M[name]), _groups_last(V[name]), "adam_" + name)
        mats.append([_groups_first(o, W[name].shape) for o in outs])

    dmod_all = sg_vec[:, :2 * 3 * D_MODEL // LANES].reshape(N_DEV, 2, N_DEV, n_ada // LANES, LANES)
    dmod_mine = lax.dynamic_index_in_dim(dmod_all, me, axis=2, keepdims=False).transpose(1, 0, 2, 3).reshape(2, N_DEV, n_ada)
    g_ada_w = _ada_grad_call(c_all, dmod_mine)
    ada = _adam_call(g_ada_w.reshape(1, -1, LANES), W["ada_w"].reshape(-1, LANES), M["ada_w"].reshape(-1, LANES),
                     V["ada_w"].reshape(-1, LANES), "adam_ada")
    u_srcs, u_lands = _spread_wait_call(u_send, u_recv, u_srcs, u_lands, ada[0][:SUBLANES] + loss, "scatter_s5_in_wait")
    big["s5_w_in"] = owner_update(u_lands[0], u_srcs[0], "s5_w_in")
    ada = [a.reshape(W["ada_w"].shape) for a in ada]

    res = {}
    for n in BIG_NAMES:
        res[n] = [o[None] for o in big[n]]
    for i, n in enumerate(vec_names):
        res[n] = [b[i] for b in vec]
    for i, n in enumerate(MAT_NAMES):
        res[n] = mats[i]
    res["ada_w"] = ada
    outs = [loss, dx[None]]
    for j in range(4):
        outs += [res[n][j] for n in WEIGHT_ORDER]
    return tuple(outs)


def kernel(x, c, ada_w, ada_b, norm_w, s5_w_in, s5_lambda_re, s5_lambda_im, s5_log_dt, s5_b_re, s5_b_im, s5_c_re, s5_c_im, s5_d, s5_w_glu, s5_w_out, gdn_w_in, gdn_conv_w, gdn_a_log, gdn_dt_bias, gdn_norm_w, gdn_w_out, final_norm_w, loss_target, m_ada_w, m_ada_b, m_norm_w, m_s5_w_in, m_s5_lambda_re, m_s5_lambda_im, m_s5_log_dt, m_s5_b_re, m_s5_b_im, m_s5_c_re, m_s5_c_im, m_s5_d, m_s5_w_glu, m_s5_w_out, m_gdn_w_in, m_gdn_conv_w, m_gdn_a_log, m_gdn_dt_bias, m_gdn_norm_w, m_gdn_w_out, m_final_norm_w, v_ada_w, v_ada_b, v_norm_w, v_s5_w_in, v_s5_lambda_re, v_s5_lambda_im, v_s5_log_dt, v_s5_b_re, v_s5_b_im, v_s5_c_re, v_s5_c_im, v_s5_d, v_s5_w_glu, v_s5_w_out, v_gdn_w_in, v_gdn_conv_w, v_gdn_a_log, v_gdn_dt_bias, v_gdn_norm_w, v_gdn_w_out, v_final_norm_w):
    W = dict(ada_w=ada_w, ada_b=ada_b, norm_w=norm_w, s5_w_in=s5_w_in, s5_lambda_re=s5_lambda_re, s5_lambda_im=s5_lambda_im,
             s5_log_dt=s5_log_dt, s5_b_re=s5_b_re, s5_b_im=s5_b_im, s5_c_re=s5_c_re, s5_c_im=s5_c_im, s5_d=s5_d,
             s5_w_glu=s5_w_glu, s5_w_out=s5_w_out, gdn_w_in=gdn_w_in, gdn_conv_w=gdn_conv_w, gdn_a_log=gdn_a_log,
             gdn_dt_bias=gdn_dt_bias, gdn_norm_w=gdn_norm_w, gdn_w_out=gdn_w_out, final_norm_w=final_norm_w)
    M = dict(ada_w=m_ada_w, ada_b=m_ada_b, norm_w=m_norm_w, s5_w_in=m_s5_w_in, s5_lambda_re=m_s5_lambda_re,
             s5_lambda_im=m_s5_lambda_im, s5_log_dt=m_s5_log_dt, s5_b_re=m_s5_b_re, s5_b_im=m_s5_b_im, s5_c_re=m_s5_c_re,
             s5_c_im=m_s5_c_im, s5_d=m_s5_d, s5_w_glu=m_s5_w_glu, s5_w_out=m_s5_w_out, gdn_w_in=m_gdn_w_in,
             gdn_conv_w=m_gdn_conv_w, gdn_a_log=m_gdn_a_log, gdn_dt_bias=m_gdn_dt_bias, gdn_norm_w=m_gdn_norm_w,
             gdn_w_out=m_gdn_w_out, final_norm_w=m_final_norm_w)
    V = dict(ada_w=v_ada_w, ada_b=v_ada_b, norm_w=v_norm_w, s5_w_in=v_s5_w_in, s5_lambda_re=v_s5_lambda_re,
             s5_lambda_im=v_s5_lambda_im, s5_log_dt=v_s5_log_dt, s5_b_re=v_s5_b_re, s5_b_im=v_s5_b_im, s5_c_re=v_s5_c_re,
             s5_c_im=v_s5_c_im, s5_d=v_s5_d, s5_w_glu=v_s5_w_glu, s5_w_out=v_s5_w_out, gdn_w_in=v_gdn_w_in,
             gdn_conv_w=v_gdn_conv_w, gdn_a_log=v_gdn_a_log, gdn_dt_bias=v_gdn_dt_bias, gdn_norm_w=v_gdn_norm_w,
             gdn_w_out=v_gdn_w_out, final_norm_w=v_final_norm_w)
    return _step(x, c, W, M, V, loss_target)
```

```python
import functools
import math

import jax
import jax.numpy as jnp
from jax import lax
from jax.experimental import pallas as pl
from jax.experimental.pallas import tpu as pltpu

F32 = jnp.float32
BF16 = jnp.bfloat16
SDS = jax.ShapeDtypeStruct

D_MODEL = 1024
D_INNER = 2048
NORM_EPS = 1e-6
S5_GROUP = 16
S5_GROUPS = 128
S5_STATE = 64
GDN_HEADS = 8
GDN_DK = 128
GDN_DV = 256
GDN_CONV = 4
GDN_CHUNK = 64
GDN_QK = 1024
GDN_CONV_CH = 4096
GDN_PROJ = 6160
ADAM_LR = 0.001
ADAM_B1 = 0.9
ADAM_B2 = 0.999
ADAM_EPS = 1e-08
ADAM_WD = 0.01
ADAM_STEP = 10

N_DEV = 8
LANES = 128
SUBLANES = 8
VMEM_BIG = 56 << 20
VMEM_MID = 40 << 20
S5_GB = 8
S5_BW = S5_GB * S5_GROUP
S5_TL = 2048
MESH_AXES = ("x", "y", "c")


def _params(sem, vmem=None):
    return pltpu.CompilerParams(dimension_semantics=sem, vmem_limit_bytes=vmem)


def _bdot(a, b, dims=(((1,), (0,)), ((), ()))):
    return lax.dot_general(a.astype(BF16), b.astype(BF16), dims, preferred_element_type=F32)


def _hdot(a, b, dims=(((1,), (0,)), ((), ()))):
    return lax.dot_general(a, b, dims, preferred_element_type=F32, precision=lax.Precision.HIGHEST)


_BNN = (((2,), (1,)), ((0,), (0,)))
_BNT = (((2,), (2,)), ((0,), (0,)))
_BTN = (((1,), (1,)), ((0,), (0,)))


@jax.custom_vjp
def _unit_lower_inverse(a):
    c = a.shape[-1]
    ri = lax.broadcasted_iota(jnp.int32, a.shape, 1)
    ci = lax.broadcasted_iota(jnp.int32, a.shape, 2)
    n = -a
    t = (ri == ci).astype(F32) + n
    for _ in range(int(math.log2(c)) - 1):
        n = _hdot(n, n, _BNN)
        t = t + _hdot(t, n, _BNN)
    return t


def _unit_lower_inverse_fwd(a):
    t = _unit_lower_inverse(a)
    return t, t


def _unit_lower_inverse_bwd(t, g):
    return (-_hdot(_hdot(t, g, _BTN), t, _BNT),)


_unit_lower_inverse.defvjp(_unit_lower_inverse_fwd, _unit_lower_inverse_bwd)


NN = (((1,), (0,)), ((), ()))
NT = (((1,), (1,)), ((), ()))
TN = (((0,), (0,)), ((), ()))


def _tile(n, pref):
    for t in (pref, 512, 256, 128):
        if t <= n and n % t == 0:
            return t
    return n


def _matmul(a, b, mode, name, add=None):
    if mode == "nn":
        (m, k), (_, n) = a.shape, b.shape
    elif mode == "nt":
        (m, k), (n, _) = a.shape, b.shape
    else:
        (k, m), (_, n) = a.shape, b.shape
    tm, tn, tk = _tile(m, 1024), _tile(n, 512), (k if k <= 2048 else _tile(k, 512))
    if mode == "tn":
        tm, tn, tk = _tile(m, 1024), _tile(n, 1024), _tile(k, 1024)
    nk = k // tk
    dims = {"nn": NN, "nt": NT, "tn": TN}[mode]

    def body(a_ref, b_ref, *rest):
        o_ref, acc_ref = rest[-2], rest[-1]
        part = _bdot(a_ref[...], b_ref[...], dims)
        if nk == 1:
            o_ref[...] = part if add is None else part + rest[0][...]
            return
        kk = pl.program_id(2)

        @pl.when(kk == 0)
        def _():
            acc_ref[...] = part if add is None else part + rest[0][...]

        @pl.when(kk > 0)
        def _():
            acc_ref[...] += part

        @pl.when(kk == nk - 1)
        def _():
            o_ref[...] = acc_ref[...]

    if nk == 1 and mode != "tn":
        def inner(a_ref, b_ref, *rest):
            part = _bdot(a_ref[...], b_ref[...], dims)
            rest[-1][...] = part if add is None else part + rest[0][...]

        tile = pl.BlockSpec((tm, tn), lambda i, j: (i, j))
        b_blk = (tn, tk) if mode == "nt" else (tk, tn)
        b_map = (lambda i, j: (j, 0)) if mode == "nt" else (lambda i, j: (0, j))

        def piped(*refs):
            pltpu.emit_pipeline(
                inner, grid=(m // tm, n // tn),
                in_specs=[pl.BlockSpec((tm, tk), lambda i, j: (i, 0)),
                          pl.BlockSpec(b_blk, b_map, pipeline_mode=pl.Buffered(MM_STREAM_BUFFERS))] + ([] if add is None else [tile]),
                out_specs=[tile],
            )(*refs)

        hbm = pl.BlockSpec(memory_space=pl.ANY)
        extra = [] if add is None else [add]
        return pl.pallas_call(
            piped, name=name, out_shape=SDS((m, n), F32), in_specs=[hbm] * (2 + len(extra)), out_specs=hbm,
            compiler_params=_params(None, VMEM_BIG),
        )(a, b, *extra)

    a_spec = pl.BlockSpec((tk, tm), lambda i, j, q: (q, i)) if mode == "tn" else pl.BlockSpec((tm, tk), lambda i, j, q: (i, q))
    b_spec = pl.BlockSpec((tn, tk), lambda i, j, q: (j, q)) if mode == "nt" else pl.BlockSpec((tk, tn), lambda i, j, q: (q, j))
    o_spec = pl.BlockSpec((tm, tn), lambda i, j, q: (i, j))
    return pl.pallas_call(
        body, name=name, grid=(m // tm, n // tn, nk),
        in_specs=[a_spec, b_spec] + ([] if add is None else [o_spec]), out_specs=o_spec,
        out_shape=SDS((m, n), F32), scratch_shapes=[pltpu.VMEM((tm, tn), F32)],
        compiler_params=_params(("parallel", "parallel", "arbitrary"), VMEM_MID),
    )(a, b, *([] if add is None else [add]))


def make_mm(name, pass_input=False):
    def primal(a, w):
        out = _matmul(a, w, "nn", name + "_fwd")
        return (out, a) if pass_input else out

    @jax.custom_vjp
    def mm(a, w, grad_slot):
        return primal(a, w)

    def fwd(a, w, grad_slot):
        return primal(a, w), (a, w)

    def bwd(res, g):
        a, w = res
        g, g_other = g if pass_input else (g, None)
        return _matmul(g, w, "nt", name + "_dx", add=g_other), jnp.zeros_like(w), _matmul(a, g, "tn", name + "_dw")

    mm.defvjp(fwd, bwd)
    return mm


PROJ_ROWS = 256
MM_STREAM_BUFFERS = 3


def _proj_fwd_call(a, ws, name, w_rows):
    m, k = a.shape
    tm = _tile(m, PROJ_ROWS)
    nw = len(ws)
    widths = [w.shape[0] if w_rows else w.shape[1] for w in ws]

    def body(*refs):
        ab = refs[0][...].astype(BF16)
        for w_ref, o_ref in zip(refs[1:1 + nw], refs[1 + nw:]):
            o_ref[...] = lax.dot_general(ab, w_ref[...], NT if w_rows else NN, preferred_element_type=F32)

    return pl.pallas_call(
        body, name=name, grid=(m // tm,),
        in_specs=[pl.BlockSpec((tm, k), lambda i: (i, 0))] + [pl.BlockSpec(w.shape, lambda i: (0, 0)) for w in ws],
        out_specs=[pl.BlockSpec((tm, n), lambda i: (i, 0)) for n in widths],
        out_shape=[SDS((m, n), F32) for n in widths],
        compiler_params=_params(("parallel",), VMEM_BIG),
    )(a, *ws)


def _proj_dx_call(gs, ws, name, w_rows):
    m = gs[0].shape[0]
    k = ws[0].shape[1] if w_rows else ws[0].shape[0]
    tm = _tile(m, PROJ_ROWS)
    nw = len(ws)

    def body(*refs):
        acc = None
        for g_ref, w_ref in zip(refs[:nw], refs[nw:2 * nw]):
            part = _bdot(g_ref[...], w_ref[...], NN if w_rows else NT)
            acc = part if acc is None else acc + part
        refs[2 * nw][...] = acc

    return pl.pallas_call(
        body, name=name, grid=(m // tm,),
        in_specs=[pl.BlockSpec((tm, g.shape[1]), lambda i: (i, 0)) for g in gs] + [pl.BlockSpec(w.shape, lambda i: (0, 0)) for w in ws],
        out_specs=pl.BlockSpec((tm, k), lambda i: (i, 0)), out_shape=SDS((m, k), F32),
        compiler_params=_params(("parallel",), VMEM_BIG),
    )(*gs, *ws)


def make_proj(name, w_rows=False):
    @jax.custom_vjp
    def proj(a, ws, grad_slots):
        return tuple(_proj_fwd_call(a, ws, name + "_fwd", w_rows))

    def fwd(a, ws, grad_slots):
        return tuple(_proj_fwd_call(a, ws, name + "_fwd", w_rows)), (a, ws)

    def bwd(res, gs):
        a, ws = res
        dws = tuple(_matmul(g, a, "tn", "%s_dw%d" % (name, i)) if w_rows else _matmul(a, g, "tn", "%s_dw%d" % (name, i))
                    for i, g in enumerate(gs))
        return _proj_dx_call(tuple(gs), ws, name + "_dx", w_rows), tuple(jnp.zeros_like(w) for w in ws), dws

    proj.defvjp(fwd, bwd)
    return proj


def make_rowwise(f, name, tm, n_rows, n_params, vmem=VMEM_MID, pass_first=False):
    def specs_of(arrs, blocked):
        if blocked:
            return [pl.BlockSpec((tm, a.shape[1]), lambda i: (i, 0)) for a in arrs]
        return [pl.BlockSpec(a.shape, lambda i: (0, 0)) for a in arrs]

    def out_structs(rows, params):
        blk = [SDS((tm, r.shape[1]), r.dtype) for r in rows] + [SDS(p.shape, p.dtype) for p in params]
        return jax.eval_shape(f, *blk)

    def run_fwd(rows, params):
        L = rows[0].shape[0]
        outs = out_structs(rows, params)

        def body(*refs):
            ins = [r[...] for r in refs[:n_rows + n_params]]
            res = f(*ins)
            for o_ref, val in zip(refs[n_rows + n_params:], res):
                o_ref[...] = val

        return pl.pallas_call(
            body, name=name + "_fwd", grid=(L // tm,),
            in_specs=specs_of(rows, True) + specs_of(params, False),
            out_specs=[pl.BlockSpec((tm, o.shape[1]), lambda i: (i, 0)) for o in outs],
            out_shape=[SDS((L, o.shape[1]), o.dtype) for o in outs],
            compiler_params=_params(("parallel",), vmem),
        )(*rows, *params)

    def run_bwd(rows, params, gs):
        L = rows[0].shape[0]
        n_g = len(gs)

        def body(*refs):
            i = pl.program_id(0)
            ins = [r[...] for r in refs[:n_rows + n_params]]
            cts = tuple(r[...] for r in refs[n_rows + n_params:n_rows + n_params + n_g])
            outs = refs[n_rows + n_params + n_g:]
            _, vjp = jax.vjp(f, *ins)
            grads = vjp(cts[:-1] if pass_first else cts)
            if pass_first:
                grads = (grads[0] + cts[-1],) + tuple(grads[1:])
            for o_ref, val in zip(outs[:n_rows], grads[:n_rows]):
                o_ref[...] = val

            if n_params:
                @pl.when(i == 0)
                def _():
                    for o_ref in outs[n_rows:]:
                        o_ref[...] = jnp.zeros_like(o_ref)
                for o_ref, val in zip(outs[n_rows:], grads[n_rows:]):
                    o_ref[...] += val

        res = pl.pallas_call(
            body, name=name + "_bwd", grid=(L // tm,),
            in_specs=specs_of(rows, True) + specs_of(params, False) + specs_of(gs, True),
            out_specs=specs_of(rows, True) + specs_of(params, False),
            out_shape=[SDS(r.shape, r.dtype) for r in rows] + [SDS(p.shape, p.dtype) for p in params],
            compiler_params=_params(("arbitrary",), vmem),
        )(*rows, *params, *gs)
        return tuple(res[:n_rows]), tuple(res[n_rows:])

    def outputs(rows, params):
        outs = tuple(run_fwd(rows, params))
        return outs + (rows[0],) if pass_first else outs

    @jax.custom_vjp
    def op(rows, params):
        return outputs(rows, params)

    def fwd(rows, params):
        return outputs(rows, params), (rows, params)

    def bwd(res, gs):
        rows, params = res
        return run_bwd(rows, params, tuple(gs))

    op.defvjp(fwd, bwd)
    op.run_fwd, op.run_bwd = run_fwd, run_bwd
    return op


def _s5_scan_rows(xr_ref, xi_ref, ar, ai, x0r, x0i, tl, reverse=False):
    n = xr_ref.shape[1]
    T = SUBLANES
    row = lax.broadcasted_iota(jnp.int32, (T, n), 0)
    pr, pi = [ar], [ai]
    for _ in range(T - 1):
        pr, pi = pr + [pr[-1] * ar - pi[-1] * ai], pi + [pr[-1] * ai + pi[-1] * ar]
    levels = []
    for d in (1, 2, 4):
        mask = (row < T - d) if reverse else (row >= d)
        levels.append((T - d if reverse else d, jnp.where(mask, pr[d - 1], 0.0), jnp.where(mask, pi[d - 1], 0.0)))
    cr = jnp.zeros((T, n), F32)
    ci = jnp.zeros((T, n), F32)
    for r in range(T):
        k = (T - r) if reverse else (r + 1)
        cr = jnp.where(row == r, pr[k - 1], cr)
        ci = jnp.where(row == r, pi[k - 1], ci)
    nt = tl // T
    last = 0 if reverse else T - 1

    def step(t, carry):
        sr, si = carry
        base = pl.multiple_of((nt - 1 - t if reverse else t) * T, T)
        br = xr_ref[pl.ds(base, T), :]
        bi = xi_ref[pl.ds(base, T), :]
        for shift, mr, mi in levels:
            qr = pltpu.roll(br, shift, 0)
            qi = pltpu.roll(bi, shift, 0)
            br, bi = br + (mr * qr - mi * qi), bi + (mr * qi + mi * qr)
        xr = br + (cr * sr - ci * si)
        xi = bi + (cr * si + ci * sr)
        xr_ref[pl.ds(base, T), :] = xr
        xi_ref[pl.ds(base, T), :] = xi
        return xr[last:last + 1, :], xi[last:last + 1, :]
    return lax.fori_loop(0, nt, step, (x0r, x0i))


def _s5_fwd_call(u, bre, bim, cre, cim, a, d, tl):
    L, e = u.shape
    nb = e // S5_BW
    ns = bre.shape[2]
    nc = L // tl

    def body(u_ref, bre_ref, bim_ref, cre_ref, cim_ref, a_ref, d_ref, y_ref, xb_ref, sr_ref, si_ref, xr_ref, xi_ref, carry_ref):
        c = pl.program_id(1)

        @pl.when(c == 0)
        def _():
            carry_ref[...] = jnp.zeros_like(carry_ref)
        xb_ref[0, 0] = carry_ref[...]
        ub = u_ref[...]
        xr_ref[...] = _bdot(ub, bre_ref[0])
        xi_ref[...] = _bdot(ub, bim_ref[0])
        ar = a_ref[0, 0:1, :]
        ai = a_ref[0, 1:2, :]
        xr, xi = _s5_scan_rows(xr_ref, xi_ref, ar, ai, carry_ref[0:1, :], carry_ref[1:2, :], tl)
        carry_ref[0:1, :] = xr
        carry_ref[1:2, :] = xi
        sr = xr_ref[...].astype(BF16)
        si = xi_ref[...].astype(BF16)
        sr_ref[...] = sr
        si_ref[...] = si
        y_ref[...] = _f_s5_act(_bdot(sr, cre_ref[0]) - _bdot(si, cim_ref[0]), ub, d_ref[...])[0]

    return pl.pallas_call(
        body, name="s5_core_fwd", grid=(nb, nc),
        in_specs=[pl.BlockSpec((tl, S5_BW), lambda j, c: (c, j)),
                  pl.BlockSpec((1, S5_BW, ns), lambda j, c: (j, 0, 0)), pl.BlockSpec((1, S5_BW, ns), lambda j, c: (j, 0, 0)),
                  pl.BlockSpec((1, ns, S5_BW), lambda j, c: (j, 0, 0)), pl.BlockSpec((1, ns, S5_BW), lambda j, c: (j, 0, 0)),
                  pl.BlockSpec((1, SUBLANES, ns), lambda j, c: (j, 0, 0)), pl.BlockSpec((1, S5_BW), lambda j, c: (0, j))],
        out_specs=[pl.BlockSpec((tl, S5_BW), lambda j, c: (c, j)),
                   pl.BlockSpec((1, 1, SUBLANES, ns), lambda j, c: (j, c, 0, 0)),
                   pl.BlockSpec((tl, ns), lambda j, c: (c, j)), pl.BlockSpec((tl, ns), lambda j, c: (c, j))],
        out_shape=[SDS((L, e), F32), SDS((nb, nc, SUBLANES, ns), F32), SDS((L, nb * ns), BF16), SDS((L, nb * ns), BF16)],
        scratch_shapes=[pltpu.VMEM((tl, ns), F32), pltpu.VMEM((tl, ns), F32), pltpu.VMEM((SUBLANES, ns), F32)],
        compiler_params=_params(("arbitrary", "arbitrary"), VMEM_MID),
    )(u, bre, bim, cre, cim, a, d)


def _s5_bwd_call(u, dy2, bre, bim, cre, cim, a, d, xb, sr, si, tl):
    L, e = u.shape
    nb = e // S5_BW
    ns = bre.shape[2]
    nc = L // tl

    def body(u_ref, dy2_ref, bre_ref, bim_ref, cre_ref, cim_ref, a_ref, d_ref, xb_ref, sr_ref, si_ref,
             du_ref, dbre_ref, dbim_ref, dcre_ref, dcim_ref, da_ref, dd_ref,
             gr_ref, gi_ref, gcarry_ref):
        c = pl.program_id(1)

        @pl.when(c == 0)
        def _():
            gcarry_ref[...] = jnp.zeros_like(gcarry_ref)
            dbre_ref[...] = jnp.zeros_like(dbre_ref)
            dbim_ref[...] = jnp.zeros_like(dbim_ref)
            dcre_ref[...] = jnp.zeros_like(dcre_ref)
            dcim_ref[...] = jnp.zeros_like(dcim_ref)
            da_ref[...] = jnp.zeros_like(da_ref)
            dd_ref[...] = jnp.zeros_like(dd_ref)

        ub = u_ref[...]
        ys = _bdot(sr_ref[...], cre_ref[0]) - _bdot(si_ref[...], cim_ref[0])
        _, act_vjp = jax.vjp(lambda *t: _f_s5_act(*t)[0], ys, ub, d_ref[...])
        dy, du_skip, dd = act_vjp(dy2_ref[...])
        dd_ref[...] += dd
        ar = a_ref[0, 0:1, :]
        ai = a_ref[0, 1:2, :]
        x0r = xb_ref[0, 0, 0:1, :]
        x0i = xb_ref[0, 0, 1:2, :]
        dcre_ref[0] += _bdot(sr_ref[...], dy, TN)
        dcim_ref[0] -= _bdot(si_ref[...], dy, TN)
        gr_ref[...] = _bdot(dy, cre_ref[0], NT)
        gi_ref[...] = -_bdot(dy, cim_ref[0], NT)

        g0r, g0i = _s5_scan_rows(gr_ref, gi_ref, ar, -ai, gcarry_ref[0:1, :], gcarry_ref[1:2, :], tl, reverse=True)
        gcarry_ref[0:1, :] = g0r
        gcarry_ref[1:2, :] = g0i
        row = lax.broadcasted_iota(jnp.int32, (tl, ns), 0)
        gr = gr_ref[...]
        gi = gi_ref[...]
        xpr = jnp.where(row == 0, x0r, pltpu.roll(sr_ref[...].astype(F32), 1, 0))
        xpi = jnp.where(row == 0, x0i, pltpu.roll(si_ref[...].astype(F32), 1, 0))
        da_ref[0, 0:1, :] += jnp.sum(gr * xpr + gi * xpi, axis=0, keepdims=True)
        da_ref[0, 1:2, :] += jnp.sum(gi * xpr - gr * xpi, axis=0, keepdims=True)
        du_ref[...] = (_bdot(gr, bre_ref[0], NT) + _bdot(gi, bim_ref[0], NT)) + du_skip
        dbre_ref[0] += _bdot(ub, gr, TN)
        dbim_ref[0] += _bdot(ub, gi, TN)

    rev = lambda c: nc - 1 - c
    return pl.pallas_call(
        body, name="s5_core_bwd", grid=(nb, nc),
        in_specs=[pl.BlockSpec((tl, S5_BW), lambda j, c: (rev(c), j)), pl.BlockSpec((tl, S5_BW), lambda j, c: (rev(c), j)),
                  pl.BlockSpec((1, S5_BW, ns), lambda j, c: (j, 0, 0)), pl.BlockSpec((1, S5_BW, ns), lambda j, c: (j, 0, 0)),
                  pl.BlockSpec((1, ns, S5_BW), lambda j, c: (j, 0, 0)), pl.BlockSpec((1, ns, S5_BW), lambda j, c: (j, 0, 0)),
                  pl.BlockSpec((1, SUBLANES, ns), lambda j, c: (j, 0, 0)), pl.BlockSpec((1, S5_BW), lambda j, c: (0, j)),
                  pl.BlockSpec((1, 1, SUBLANES, ns), lambda j, c: (j, rev(c), 0, 0)),
                  pl.BlockSpec((tl, ns), lambda j, c: (rev(c), j)), pl.BlockSpec((tl, ns), lambda j, c: (rev(c), j))],
        out_specs=[pl.BlockSpec((tl, S5_BW), lambda j, c: (rev(c), j)),
                   pl.BlockSpec((1, S5_BW, ns), lambda j, c: (j, 0, 0)), pl.BlockSpec((1, S5_BW, ns), lambda j, c: (j, 0, 0)),
                   pl.BlockSpec((1, ns, S5_BW), lambda j, c: (j, 0, 0)), pl.BlockSpec((1, ns, S5_BW), lambda j, c: (j, 0, 0)),
                   pl.BlockSpec((1, SUBLANES, ns), lambda j, c: (j, 0, 0)), pl.BlockSpec((1, S5_BW), lambda j, c: (0, j))],
        out_shape=[SDS((L, e), F32), SDS(bre.shape, F32), SDS(bim.shape, F32), SDS(cre.shape, F32), SDS(cim.shape, F32),
                   SDS(a.shape, F32), SDS(d.shape, F32)],
        scratch_shapes=[pltpu.VMEM((tl, ns), F32) for _ in range(2)] + [pltpu.VMEM((SUBLANES, ns), F32)],
        compiler_params=_params(("arbitrary", "arbitrary"), VMEM_MID),
    )(u, dy2, bre, bim, cre, cim, a, d, xb, sr, si)


def make_s5_core(tl):
    @jax.custom_vjp
    def s5_core(u, bre, bim, cre, cim, a, d):
        return _s5_fwd_call(u, bre, bim, cre, cim, a, d, tl)[0]

    def fwd(u, bre, bim, cre, cim, a, d):
        y2, xb, sr, si = _s5_fwd_call(u, bre, bim, cre, cim, a, d, tl)
        return y2, (u, bre, bim, cre, cim, a, d, xb, sr, si)

    def bwd(res, dy2):
        u, bre, bim, cre, cim, a, d, xb, sr, si = res
        return tuple(_s5_bwd_call(u, dy2, bre, bim, cre, cim, a, d, xb, sr, si, tl))

    s5_core.defvjp(fwd, bwd)
    return s5_core


def _s5_block_params(lam_re, lam_im, log_dt, b_re, b_im, c_re, c_im):
    dt = jnp.exp(log_dt)[:, None]
    mag = jnp.exp(lam_re * dt)
    ab_re = mag * jnp.cos(lam_im * dt)
    ab_im = mag * jnp.sin(lam_im * dt)
    den = lam_re * lam_re + lam_im * lam_im
    nr = ab_re - 1.0
    ni = ab_im
    q_re = (nr * lam_re + ni * lam_im) / den
    q_im = (ni * lam_re - nr * lam_im) / den
    bb_re = q_re[..., None] * b_re - q_im[..., None] * b_im
    bb_im = q_re[..., None] * b_im + q_im[..., None] * b_re
    nb = S5_GROUPS // S5_GB
    eye = jnp.eye(S5_GB, dtype=F32)

    def bdiag_in(bb):
        t = bb.reshape(nb, S5_GB, S5_STATE, S5_GROUP)
        t = jnp.einsum("jgpm,gh->jgmhp", t, eye)
        return t.reshape(nb, S5_GB * S5_GROUP, S5_GB * S5_STATE)

    def bdiag_out(cc):
        t = cc.reshape(nb, S5_GB, S5_GROUP, S5_STATE)
        t = jnp.einsum("jgmp,gh->jgphm", t, eye)
        return t.reshape(nb, S5_GB * S5_STATE, S5_GB * S5_GROUP)

    a = jnp.stack([ab_re.reshape(nb, S5_GB * S5_STATE), ab_im.reshape(nb, S5_GB * S5_STATE)], axis=1)
    a = jnp.concatenate([a, jnp.zeros((nb, SUBLANES - 2, S5_GB * S5_STATE), F32)], axis=1)
    return bdiag_in(bb_re), bdiag_in(bb_im), bdiag_out(c_re), bdiag_out(c_im), a


def _shift_down(x, s, row):
    if s == 0:
        return x
    return jnp.where(row >= s, pltpu.roll(x, s, 0), 0.0)


def _shift_up(x, s, row, n):
    if s == 0:
        return x
    return jnp.where(row < n - s, pltpu.roll(x, n - s, 0), 0.0)


def _causal_conv(xv, w_ref, row):
    acc = jnp.zeros_like(xv)
    for j in range(GDN_CONV):
        acc += w_ref[j:j + 1, :] * _shift_down(xv, GDN_CONV - 1 - j, row)
    return acc


def _conv_fwd_call(x, w, act, name):
    L, ch = x.shape

    def body(x_ref, w_ref, y_ref):
        xv = x_ref[...]
        row = lax.broadcasted_iota(jnp.int32, xv.shape, 0)
        y_ref[...] = act(_causal_conv(xv, w_ref, row))

    return pl.pallas_call(
        body, name=name + "_fwd", grid=(ch // LANES,),
        in_specs=[pl.BlockSpec((L, LANES), lambda j: (0, j)), pl.BlockSpec((SUBLANES, LANES), lambda j: (0, j))],
        out_specs=pl.BlockSpec((L, LANES), lambda j: (0, j)), out_shape=SDS((L, ch), F32),
        compiler_params=_params(("parallel",), VMEM_MID),
    )(x, w)


def _conv_bwd_call(x, w, dy, act, name):
    L, ch = x.shape

    def body(x_ref, w_ref, dy_ref, dx_ref, dw_ref):
        xv = x_ref[...]
        row = lax.broadcasted_iota(jnp.int32, xv.shape, 0)
        _, act_vjp = jax.vjp(act, _causal_conv(xv, w_ref, row))
        (g,) = act_vjp(dy_ref[...])
        acc = jnp.zeros_like(xv)
        dws = []
        for j in range(GDN_CONV):
            s = GDN_CONV - 1 - j
            acc += w_ref[j:j + 1, :] * _shift_up(g, s, row, L)
            dws.append(jnp.sum(g * _shift_down(xv, s, row), axis=0, keepdims=True))
        dx_ref[...] = acc
        dw_ref[...] = jnp.concatenate(dws + [jnp.zeros((SUBLANES - GDN_CONV, LANES), F32)], axis=0)

    return pl.pallas_call(
        body, name=name + "_bwd", grid=(ch // LANES,),
        in_specs=[pl.BlockSpec((L, LANES), lambda j: (0, j)), pl.BlockSpec((SUBLANES, LANES), lambda j: (0, j)),
                  pl.BlockSpec((L, LANES), lambda j: (0, j))],
        out_specs=[pl.BlockSpec((L, LANES), lambda j: (0, j)), pl.BlockSpec((SUBLANES, LANES), lambda j: (0, j))],
        out_shape=[SDS((L, ch), F32), SDS((SUBLANES, ch), F32)],
        compiler_params=_params(("parallel",), VMEM_MID),
    )(x, w, dy)


def make_conv_act(act, name):
    @jax.custom_vjp
    def op(x, w):
        return _conv_fwd_call(x, w, act, name)

    def fwd(x, w):
        return _conv_fwd_call(x, w, act, name), (x, w)

    def bwd(res, dy):
        x, w = res
        return tuple(_conv_bwd_call(x, w, dy, act, name))

    op.defvjp(fwd, bwd)
    return op


BNN, BNT, BTN = _BNN, _BNT, _BTN
GDN_PREP_BATCH = 16


@jax.custom_vjp
def _known_inverse(a, t):
    return t


def _known_inverse_fwd(a, t):
    return t, t


def _known_inverse_bwd(t, g):
    return -_hdot(_hdot(t, g, _BTN), t, _BNT), jnp.zeros_like(t)


_known_inverse.defvjp(_known_inverse_fwd, _known_inverse_bwd)


def _gdn_prep_math(q, k, v, beta, g, t_saved=None):
    B, C = q.shape[0], q.shape[1]
    ri = lax.broadcasted_iota(jnp.int32, (B, C, C), 1)
    ci = lax.broadcasted_iota(jnp.int32, (B, C, C), 2)
    causal = ri >= ci
    strict = ri > ci
    eye = (ri == ci).astype(F32)
    gb = jnp.broadcast_to(g, (B, C, C))
    g_row = jnp.sum(gb * eye, axis=1, keepdims=True)
    gc_col = jnp.sum(jnp.where(causal, jnp.broadcast_to(g_row, (B, C, C)), 0.0), axis=2, keepdims=True)
    gc_row = jnp.sum(jnp.where(ri <= ci, gb, 0.0), axis=1, keepdims=True)
    decay = jnp.exp(jnp.where(causal, gc_col - gc_row, -jnp.inf))
    kk = _bdot(k, k, BNT)
    a_mat = jnp.where(strict, beta * kk * decay, 0.0)
    t = _unit_lower_inverse(a_mat) if t_saved is None else _known_inverse(a_mat, t_saved)
    e_gc = jnp.exp(gc_col)
    w = _hdot(t, beta * e_gc * k, BNN)
    u = _hdot(t, beta * v, BNN)
    qk = _bdot(q, k, BNT) * decay
    q_dec = q * e_gc
    g_last = gc_col[:, C - 1:C, :]
    k_dec = k * jnp.exp(g_last - gc_col)
    return q_dec, w, u, qk, k_dec, gc_col, t


def _gdn_prep_specs(L):
    C = GDN_CHUNK
    nb = min(GDN_PREP_BATCH, L // C)
    R = nb * C
    ins = [pl.BlockSpec((R, GDN_DK), lambda c, h: (c, h)), pl.BlockSpec((R, GDN_DK), lambda c, h: (c, h)),
           pl.BlockSpec((R, GDN_DV), lambda c, h: (c, h)), pl.BlockSpec((R, LANES), lambda c, h: (c, 0))]
    outs = [pl.BlockSpec((1, R, GDN_DK), lambda c, h: (h, c, 0)), pl.BlockSpec((1, R, GDN_DK), lambda c, h: (h, c, 0)),
            pl.BlockSpec((1, R, GDN_DV), lambda c, h: (h, c, 0)), pl.BlockSpec((1, R, C), lambda c, h: (h, c, 0)),
            pl.BlockSpec((1, R, GDN_DK), lambda c, h: (h, c, 0)), pl.BlockSpec((1, R, 1), lambda c, h: (h, c, 0))]
    t_spec = pl.BlockSpec((1, R, C), lambda c, h: (h, c, 0))
    shapes = [SDS((GDN_HEADS, L, GDN_DK), F32), SDS((GDN_HEADS, L, GDN_DK), F32), SDS((GDN_HEADS, L, GDN_DV), F32),
              SDS((GDN_HEADS, L, C), F32), SDS((GDN_HEADS, L, GDN_DK), F32), SDS((GDN_HEADS, L, 1), F32)]
    return ins, outs, t_spec, shapes, nb


def _chunks(x, nb):
    return x.reshape(nb, x.shape[0] // nb, x.shape[1])


def _head_columns(bg, h):
    lane = lax.broadcasted_iota(jnp.int32, bg.shape, 1)
    beta = jnp.sum(jnp.where(lane == h, bg, 0.0), axis=1, keepdims=True)
    g = jnp.sum(jnp.where(lane == h + GDN_HEADS, bg, 0.0), axis=1, keepdims=True)
    return beta, g


def _gdn_prep_fwd_call(q, k, v, bg):
    L = q.shape[0]
    ins, outs, t_spec, shapes, nb = _gdn_prep_specs(L)

    def body(q_ref, k_ref, v_ref, bg_ref, *o_refs):
        beta, g = _head_columns(bg_ref[...], pl.program_id(1))
        res = _gdn_prep_math(_chunks(q_ref[...], nb), _chunks(k_ref[...], nb), _chunks(v_ref[...], nb),
                             _chunks(beta, nb), _chunks(g, nb))
        for o_ref, val in zip(o_refs, res):
            o_ref[0] = val.reshape(val.shape[0] * val.shape[1], val.shape[2])

    return pl.pallas_call(
        body, name="gdn_prep_fwd", grid=(L // (nb * GDN_CHUNK), GDN_HEADS), in_specs=ins, out_specs=outs + [t_spec],
        out_shape=shapes + [SDS((GDN_HEADS, L, GDN_CHUNK), F32)],
        compiler_params=_params(("parallel", "parallel"), VMEM_MID),
    )(q, k, v, bg)


def _gdn_prep_bwd_call(q, k, v, bg, t, cts):
    L = q.shape[0]
    ins, outs, t_spec, _, nb = _gdn_prep_specs(L)

    def body(q_ref, k_ref, v_ref, bg_ref, t_ref, c0, c1, c2, c3, c4, c5, dq_ref, dk_ref, dv_ref, dbg_ref):
        h = pl.program_id(1)
        beta, g = _head_columns(bg_ref[...], h)
        t_saved = _chunks(t_ref[0], nb)
        _, vjp = jax.vjp(lambda *a: _gdn_prep_math(*a, t_saved=t_saved)[:6], _chunks(q_ref[...], nb), _chunks(k_ref[...], nb),
                         _chunks(v_ref[...], nb), _chunks(beta, nb), _chunks(g, nb))
        dq, dk, dv, db, dg = vjp(tuple(_chunks(c[0], nb) for c in (c0, c1, c2, c3, c4, c5)))
        flat = lambda a: a.reshape(a.shape[0] * a.shape[1], a.shape[2])
        dq_ref[...] = flat(dq)
        dk_ref[...] = flat(dk)
        dv_ref[...] = flat(dv)

        @pl.when(h == 0)
        def _():
            dbg_ref[...] = jnp.zeros_like(dbg_ref)
        lane = lax.broadcasted_iota(jnp.int32, dbg_ref.shape, 1)
        dbg_ref[...] += jnp.where(lane == h, flat(db), 0.0) + jnp.where(lane == h + GDN_HEADS, flat(dg), 0.0)

    return pl.pallas_call(
        body, name="gdn_prep_bwd", grid=(L // (nb * GDN_CHUNK), GDN_HEADS), in_specs=ins + [t_spec] + outs, out_specs=ins,
        out_shape=[SDS(q.shape, F32), SDS(k.shape, F32), SDS(v.shape, F32), SDS(bg.shape, F32)],
        compiler_params=_params(("parallel", "arbitrary"), VMEM_MID),
    )(q, k, v, bg, t, *cts)


@jax.custom_vjp
def gdn_prep(q, k, v, bg):
    return tuple(_gdn_prep_fwd_call(q, k, v, bg)[:6])


def _gdn_prep_f(q, k, v, bg):
    res = _gdn_prep_fwd_call(q, k, v, bg)
    return tuple(res[:6]), (q, k, v, bg, res[6])


def _gdn_prep_b(res, cts):
    return tuple(_gdn_prep_bwd_call(*res, tuple(cts)))


gdn_prep.defvjp(_gdn_prep_f, _gdn_prep_b)


def _gdn_step_math(q_dec, w, u, qk, k_dec, gc, z, nw, state):
    H, C = q_dec.shape[0], q_dec.shape[1]
    v_new = u - _bdot(w, state, BNN)
    o = _bdot(q_dec, state, BNN) + _bdot(qk, v_new, BNN)
    gl = gc[:, C - 1:C, :]
    new_state = jnp.exp(gl) * state + _bdot(k_dec, v_new, BTN)
    return _f_gdn_post(jnp.concatenate([o[h] for h in range(H)], axis=1), z, nw)[0], new_state


GDN_SCAN_CHUNKS = 2


def _gdn_steps_math(q_dec, w, u, qk, k_dec, gc, z, nw, state):
    outs = []
    for i in range(q_dec.shape[1] // GDN_CHUNK):
        s = slice(i * GDN_CHUNK, (i + 1) * GDN_CHUNK)
        o, state = _gdn_step_math(q_dec[:, s], w[:, s], u[:, s], qk[:, s], k_dec[:, s], gc[:, s], z[s], nw, state)
        outs.append(o)
    return jnp.concatenate(outs, axis=0), state


def _gdn_scan_specs(L, rev):
    H = GDN_HEADS
    C = GDN_CHUNK * min(GDN_SCAN_CHUNKS, L // GDN_CHUNK)
    nc = L // C
    cc = (lambda c: nc - 1 - c) if rev else (lambda c: c)
    ins = [pl.BlockSpec((H, C, GDN_DK), lambda c: (0, cc(c), 0)), pl.BlockSpec((H, C, GDN_DK), lambda c: (0, cc(c), 0)),
           pl.BlockSpec((H, C, GDN_DV), lambda c: (0, cc(c), 0)), pl.BlockSpec((H, C, GDN_CHUNK), lambda c: (0, cc(c), 0)),
           pl.BlockSpec((H, C, GDN_DK), lambda c: (0, cc(c), 0)), pl.BlockSpec((H, C, 1), lambda c: (0, cc(c), 0))]
    o_spec = pl.BlockSpec((C, H * GDN_DV), lambda c: (cc(c), 0))
    nw_spec = pl.BlockSpec((1, H * GDN_DV), lambda c: (0, 0))
    s_spec = pl.BlockSpec((1, H, GDN_DK, GDN_DV), lambda c: (cc(c), 0, 0, 0))
    return ins + [o_spec, nw_spec], o_spec, s_spec, nc


def _gdn_scan_fwd_call(q_dec, w, u, qk, k_dec, gc, z, nw):
    L = q_dec.shape[1]
    ins, o_spec, s_spec, nc = _gdn_scan_specs(L, False)

    def body(qd_ref, w_ref, u_ref, qk_ref, kd_ref, gc_ref, z_ref, nw_ref, o_ref, sin_ref, s_ref):
        c = pl.program_id(0)

        @pl.when(c == 0)
        def _():
            s_ref[...] = jnp.zeros_like(s_ref)
        st = s_ref[...]
        sin_ref[0] = st
        o, ns = _gdn_steps_math(qd_ref[...], w_ref[...], u_ref[...], qk_ref[...], kd_ref[...], gc_ref[...], z_ref[...], nw_ref[...], st)
        o_ref[...] = o
        s_ref[...] = ns

    return pl.pallas_call(
        body, name="gdn_scan_fwd", grid=(nc,), in_specs=ins, out_specs=[o_spec, s_spec],
        out_shape=[SDS((L, GDN_HEADS * GDN_DV), F32), SDS((nc, GDN_HEADS, GDN_DK, GDN_DV), F32)],
        scratch_shapes=[pltpu.VMEM((GDN_HEADS, GDN_DK, GDN_DV), F32)],
        compiler_params=_params(("arbitrary",), VMEM_MID),
    )(q_dec, w, u, qk, k_dec, gc, z, nw)


def _gdn_scan_bwd_call(q_dec, w, u, qk, k_dec, gc, z, nw, s_in, do):
    L = q_dec.shape[1]
    ins, o_spec, s_spec, nc = _gdn_scan_specs(L, True)

    def body(qd_ref, w_ref, u_ref, qk_ref, kd_ref, gc_ref, z_ref, nw_ref, sin_ref, do_ref,
             dqd_ref, dw_ref, du_ref, dqk_ref, dkd_ref, dgc_ref, dz_ref, dnw_ref, ds_ref):
        c = pl.program_id(0)

        @pl.when(c == 0)
        def _():
            ds_ref[...] = jnp.zeros_like(ds_ref)
            dnw_ref[...] = jnp.zeros_like(dnw_ref)
        _, vjp = jax.vjp(_gdn_steps_math, qd_ref[...], w_ref[...], u_ref[...], qk_ref[...], kd_ref[...], gc_ref[...],
                         z_ref[...], nw_ref[...], sin_ref[0])
        dqd, dw, du, dqk, dkd, dgc, dz, dnw, dst = vjp((do_ref[...], ds_ref[...]))
        dqd_ref[...] = dqd
        dw_ref[...] = dw
        du_ref[...] = du
        dqk_ref[...] = dqk
        dkd_ref[...] = dkd
        dgc_ref[...] = dgc
        dz_ref[...] = dz
        dnw_ref[...] += dnw
        ds_ref[...] = dst

    return pl.pallas_call(
        body, name="gdn_scan_bwd", grid=(nc,), in_specs=ins + [s_spec, o_spec], out_specs=ins,
        out_shape=[SDS(t.shape, F32) for t in (q_dec, w, u, qk, k_dec, gc, z, nw)],
        scratch_shapes=[pltpu.VMEM((GDN_HEADS, GDN_DK, GDN_DV), F32)],
        compiler_params=_params(("arbitrary",), VMEM_MID),
    )(q_dec, w, u, qk, k_dec, gc, z, nw, s_in, do)


@jax.custom_vjp
def gdn_scan(q_dec, w, u, qk, k_dec, gc, z, nw):
    return _gdn_scan_fwd_call(q_dec, w, u, qk, k_dec, gc, z, nw)[0]


def _gdn_scan_f(*args):
    o, s_in = _gdn_scan_fwd_call(*args)
    return o, (*args, s_in)


def _gdn_scan_b(res, do):
    return tuple(_gdn_scan_bwd_call(*res, do))


gdn_scan.defvjp(_gdn_scan_f, _gdn_scan_b)


def _silu(x):
    return x * jax.nn.sigmoid(x)


def _gelu_tanh(x):
    return 0.5 * x * (1.0 + jnp.tanh(math.sqrt(2.0 / math.pi) * (x + 0.044715 * (x * x * x))))


def _f_lnmod(x, nw, sc, sh, bsc, bsh):
    xn = x * lax.rsqrt(jnp.mean(x * x, axis=-1, keepdims=True) + NORM_EPS) * nw
    return (xn * (1.0 + (sc + bsc)) + (sh + bsh),)


def _f_s5_act(ys, u, d):
    return (_gelu_tanh(ys + d * u),)


def _f_s5_gate(y2, t, z):
    return (y2 * jax.nn.sigmoid(t) * _silu(z),)


def _f_res(x, y, gate, bgate):
    return (x + (gate + bgate) * y,)


def _heads(x, width, fn):
    return jnp.concatenate([fn(x[:, i * width:(i + 1) * width]) for i in range(x.shape[1] // width)], axis=1)


def _l2n(x):
    return x * lax.rsqrt(jnp.sum(x * x, axis=-1, keepdims=True) + NORM_EPS)


def _f_betag(ba, alog, dtb):
    col = lax.broadcasted_iota(jnp.int32, ba.shape, 1)
    t = ba + dtb
    softplus = jnp.maximum(t, 0.0) + jnp.log1p(jnp.exp(-jnp.abs(t)))
    g = -jnp.exp(alog) * softplus
    return (jnp.where(col < GDN_HEADS, jax.nn.sigmoid(ba), jnp.where(col < 2 * GDN_HEADS, g, 0.0)),)


def _f_gdn_post(o, z, nw):
    on = _heads(o, GDN_DV, lambda t: t * lax.rsqrt(jnp.mean(t * t, axis=-1, keepdims=True) + NORM_EPS))
    return (on * nw * _silu(z),)


def _f_loss(x, tgt, fw):
    y = x * lax.rsqrt(jnp.mean(x * x, axis=-1, keepdims=True) + NORM_EPS) * fw
    err = y - tgt
    return (0.5 * jnp.mean(err * err, axis=-1, keepdims=True),)


def _ada_mod_call(c_all, ada_w):
    n = ada_w.shape[2]

    def body(c_ref, w_ref, o_ref):
        ca = _silu(c_ref[...])
        for l in range(ada_w.shape[0]):
            o_ref[l] = _bdot(ca, w_ref[l])

    return pl.pallas_call(body, name="ada_mod", out_shape=SDS((ada_w.shape[0], N_DEV, n), F32),
                          compiler_params=_params(None, VMEM_MID))(c_all, ada_w)


def _ada_grad_call(c_all, dmod):
    nl, _, n = dmod.shape

    def body(c_ref, d_ref, o_ref):
        ca = _silu(c_ref[...])
        for l in range(nl):
            o_ref[l] = _hdot(ca, d_ref[l], TN)

    return pl.pallas_call(body, name="ada_grad", out_shape=SDS((nl, c_all.shape[1], n), F32),
                          compiler_params=_params(None, VMEM_MID))(c_all, dmod)


ADAM_ROWS = 512


def _adamw(g, w, m, v):
    m2 = ADAM_B1 * m + (1.0 - ADAM_B1) * g
    v2 = ADAM_B2 * v + (1.0 - ADAM_B2) * (g * g)
    m_hat = m2 / (1.0 - ADAM_B1 ** ADAM_STEP)
    v_hat = v2 / (1.0 - ADAM_B2 ** ADAM_STEP)
    return g, -ADAM_LR * (m_hat / (jnp.sqrt(v_hat) + ADAM_EPS) + ADAM_WD * w), m2, v2


def _adam_call(gs, w, m, v, name, rows=None, by_cols=False):
    n, r, cols = gs.shape
    if by_cols:
        blk = pl.BlockSpec((r, LANES), lambda i: (0, i))
        g_blk, grid = pl.BlockSpec((n, r, LANES), lambda i: (0, 0, i)), (cols // LANES,)
    else:
        rows = rows or ADAM_ROWS
        blk = pl.BlockSpec((rows, cols), lambda i: (i, 0))
        g_blk, grid = pl.BlockSpec((n, rows, cols), lambda i: (0, i, 0)), (r // rows,)

    def body(g_ref, w_ref, m_ref, v_ref, go_ref, d_ref, mo_ref, vo_ref):
        g = g_ref[0].astype(F32)
        for s in range(1, n):
            g = g + g_ref[s].astype(F32)
        for o_ref, val in zip((go_ref, d_ref, mo_ref, vo_ref), _adamw(g, w_ref[...], m_ref[...], v_ref[...])):
            o_ref[...] = val

    return pl.pallas_call(
        body, name=name, grid=grid, in_specs=[g_blk, blk, blk, blk],
        out_specs=[blk, blk, blk, blk], out_shape=[SDS((r, cols), F32)] * 4,
        compiler_params=_params(("parallel",), VMEM_MID),
    )(gs, w, m, v)


def _sum_call(gs, name, rows):
    n, r, _ = gs.shape

    def body(g_ref, o_ref):
        g = g_ref[0].astype(F32)
        for s in range(1, n):
            g = g + g_ref[s].astype(F32)
        o_ref[...] = g

    return pl.pallas_call(
        body, name=name, grid=(r // rows,),
        in_specs=[pl.BlockSpec((n, rows, LANES), lambda i: (0, i, 0))],
        out_specs=pl.BlockSpec((rows, LANES), lambda i: (i, 0)), out_shape=SDS((r, LANES), F32),
        compiler_params=_params(("parallel",), VMEM_MID),
    )(gs)


def _allgather_call(x_shard, name):
    m_per, n = x_shard.shape

    def body(x_ref, out_ref, send_sems, recv_sems, local_sem):
        x, y, c = lax.axis_index("x"), lax.axis_index("y"), lax.axis_index("c")
        me, sibling = (x, y, c), (x, y, 1 - c)
        chips = [(1 - x, y), (x, 1 - y), (1 - x, 1 - y)]

        def rows(px, py, pc):
            return out_ref.at[pl.ds((4 * px + 2 * py + pc) * m_per, m_per), :]

        def copy(k, block, to, src=None):
            return pltpu.make_async_remote_copy(
                src_ref=rows(*block) if src is None else src, dst_ref=rows(*block),
                send_sem=send_sems.at[k], recv_sem=recv_sems.at[k], device_id=to, device_id_type=pl.DeviceIdType.MESH)

        mine = pltpu.make_async_copy(x_ref, rows(*me), local_sem)
        mine.start()
        first = [copy(0, me, sibling, src=x_ref)]
        first += [copy(1 + j, me, (*chip, c), src=x_ref) for j, chip in enumerate(chips)]
        for cp in first:
            cp.start()
        passed = [copy(4 + j, (*chip, c), sibling) for j, chip in enumerate(chips)]
        for j, chip in enumerate(chips):
            copy(1 + j, (*chip, c), me).wait_recv()
            passed[j].start()
        copy(0, sibling, me).wait_recv()
        for j, chip in enumerate(chips):
            copy(4 + j, (*chip, 1 - c), me).wait_recv()
        for cp in first + passed:
            cp.wait_send()
        mine.wait()

    vmem = pl.BlockSpec(memory_space=pltpu.VMEM)
    return pl.pallas_call(
        body, name=name, out_shape=SDS((N_DEV * m_per, n), x_shard.dtype), in_specs=[vmem], out_specs=vmem,
        scratch_shapes=[pltpu.SemaphoreType.DMA((7,)), pltpu.SemaphoreType.DMA((7,)), pltpu.SemaphoreType.DMA],
    )(x_shard)


def _gather_weights_call(shards, name):
    nw = len(shards)

    def body(*refs):
        x_refs, out_refs = refs[:nw], refs[nw:2 * nw]
        send_sems, recv_sems, local_sems = refs[2 * nw:]
        x, y, c = lax.axis_index("x"), lax.axis_index("y"), lax.axis_index("c")
        me, sibling = (x, y, c), (x, y, 1 - c)
        chips = [(1 - x, y), (x, 1 - y), (1 - x, 1 - y)]

        def slot(w, px, py, pc):
            return out_refs[w].at[4 * px + 2 * py + pc]

        def copy(w, k, block, to, src=None):
            dst = slot(w, *block)
            return pltpu.make_async_remote_copy(
                src_ref=dst if src is None else src, dst_ref=dst, send_sem=send_sems.at[7 * w + k],
                recv_sem=recv_sems.at[7 * w + k], device_id=to, device_id_type=pl.DeviceIdType.MESH)

        mines = [pltpu.make_async_copy(x_refs[w], slot(w, *me), local_sems.at[w]) for w in range(nw)]
        for cp in mines:
            cp.start()
        first = [copy(w, 0, me, sibling, src=x_refs[w]) for w in range(nw)]
        first += [copy(w, 1 + j, me, (*chip, c), src=x_refs[w]) for w in range(nw) for j, chip in enumerate(chips)]
        for cp in first:
            cp.start()
        passed = []
        for w in range(nw):
            for j, chip in enumerate(chips):
                copy(w, 1 + j, (*chip, c), me).wait_recv()
                fwd = copy(w, 4 + j, (*chip, c), sibling)
                fwd.start()
                passed.append(fwd)
        for w in range(nw):
            copy(w, 0, sibling, me).wait_recv()
            for j, chip in enumerate(chips):
                copy(w, 4 + j, (*chip, 1 - c), me).wait_recv()
        for cp in first + passed:
            cp.wait_send()
        for cp in mines:
            cp.wait()

    hbm = pl.BlockSpec(memory_space=pl.ANY)
    return pl.pallas_call(
        body, name=name, out_shape=[SDS((N_DEV,) + s.shape, s.dtype) for s in shards],
        in_specs=[hbm] * nw, out_specs=[hbm] * nw,
        scratch_shapes=[pltpu.SemaphoreType.DMA((7 * nw,)), pltpu.SemaphoreType.DMA((7 * nw,)), pltpu.SemaphoreType.DMA((nw,))],
    )(*shards)


_HBM = pl.BlockSpec(memory_space=pltpu.HBM)
_SEM = pl.BlockSpec(memory_space=pltpu.SEMAPHORE)
_DATAFLOW = pltpu.SideEffectType.DATAFLOW_SIDE_EFFECTING


def _spread_start_call(srcs, per_peer, name, after):
    nw = len(srcs)
    lands = [lax.empty((N_DEV,) + (s.shape[1:] if per_peer else s.shape), s.dtype) for s in srcs]

    def body(*refs):
        src_refs, land_refs = refs[:nw], refs[nw:2 * nw]
        send_sems, recv_sems, token = refs[2 * nw + 1], refs[2 * nw + 2], refs[-1]
        x, y, c = lax.axis_index("x"), lax.axis_index("y"), lax.axis_index("c")
        me = 4 * x + 2 * y + c
        for w in range(nw):
            for k in range(1, N_DEV):
                px = 1 - x if k & 4 else x
                py = 1 - y if k & 2 else y
                pc = 1 - c if k & 1 else c
                src = src_refs[w].at[4 * px + 2 * py + pc] if per_peer else src_refs[w]
                pltpu.make_async_remote_copy(
                    src_ref=src, dst_ref=land_refs[w].at[me], send_sem=send_sems.at[w], recv_sem=recv_sems.at[w],
                    device_id=(px, py, pc), device_id_type=pl.DeviceIdType.MESH).start()
        token[...] = jnp.zeros_like(token)

    hbm = lambda a: pltpu.with_memory_space_constraint(a, pltpu.HBM)
    res = pl.pallas_call(
        body, name=name,
        out_shape=(pltpu.SemaphoreType.DMA((nw,)), pltpu.SemaphoreType.DMA((nw,)))
        + tuple(pltpu.HBM(s.shape, s.dtype) for s in srcs) + tuple(pltpu.HBM(l.shape, l.dtype) for l in lands)
        + (SDS((SUBLANES, LANES), F32),),
        in_specs=[_HBM] * (2 * nw) + [pl.BlockSpec(memory_space=pl.ANY)],
        out_specs=(_SEM, _SEM) + (_HBM,) * (2 * nw) + (pl.BlockSpec(memory_space=pltpu.VMEM),),
        input_output_aliases={i: i + 2 for i in range(2 * nw)},
        compiler_params=pltpu.CompilerParams(has_side_effects=_DATAFLOW),
    )(*[hbm(s) for s in srcs], *[hbm(l) for l in lands], after)
    return res[0], res[1], res[2:2 + nw], res[2 + nw:2 + 2 * nw], res[-1]


def _spread_wait_call(send_sems, recv_sems, srcs, lands, after, name):
    nw = len(lands)

    def body(*refs):
        land_refs = refs[nw:2 * nw]
        s_sems, r_sems = refs[2 * nw], refs[2 * nw + 1]
        x, y, c = lax.axis_index("x"), lax.axis_index("y"), lax.axis_index("c")
        for w in range(nw):
            seven = land_refs[w].at[pl.ds(0, N_DEV - 1)]
            all_seven = pltpu.make_async_remote_copy(
                src_ref=seven, dst_ref=seven, send_sem=s_sems.at[w], recv_sem=r_sems.at[w],
                device_id=(x, y, c), device_id_type=pl.DeviceIdType.MESH)
            all_seven.wait_send()
            all_seven.wait_recv()

    res = pl.pallas_call(
        body, name=name,
        out_shape=tuple(pltpu.HBM(s.shape, s.dtype) for s in srcs) + tuple(pltpu.HBM(l.shape, l.dtype) for l in lands),
        in_specs=[_HBM] * (2 * nw) + [_SEM, _SEM, pl.BlockSpec(memory_space=pl.ANY)], out_specs=(_HBM,) * (2 * nw),
        input_output_aliases={i: i for i in range(2 * nw)},
        compiler_params=pltpu.CompilerParams(has_side_effects=_DATAFLOW),
    )(*srcs, *lands, send_sems, recv_sems, after)
    return res[:nw], res[nw:]


def _join_cols_call(w8, name):
    _, k, n = w8.shape
    tk = _tile(k, 256)

    def body(w_ref, o_ref):
        for s in range(N_DEV):
            o_ref[:, n * s:n * (s + 1)] = w_ref[s]

    return pl.pallas_call(body, name=name, grid=(k // tk,), in_specs=[pl.BlockSpec((N_DEV, tk, n), lambda i: (0, i, 0))],
                          out_specs=pl.BlockSpec((tk, N_DEV * n), lambda i: (i, 0)), out_shape=SDS((k, N_DEV * n), w8.dtype),
                          compiler_params=_params(("parallel",), VMEM_MID))(w8)


def _split_cols_call(g, name, dtype):
    k, n8 = g.shape
    n = n8 // N_DEV
    tk = _tile(k, 256)

    def body(g_ref, o_ref):
        for s in range(N_DEV):
            o_ref[s] = g_ref[:, n * s:n * (s + 1)].astype(dtype)

    return pl.pallas_call(body, name=name, grid=(k // tk,), in_specs=[pl.BlockSpec((tk, n8), lambda i: (i, 0))],
                          out_specs=pl.BlockSpec((N_DEV, tk, n), lambda i: (0, i, 0)), out_shape=SDS((N_DEV, k, n), dtype),
                          compiler_params=_params(("parallel",), VMEM_MID))(g)


def _pack(parts, rows_multiple):
    flat = jnp.concatenate([p.reshape(-1) for p in parts])
    unit = rows_multiple * LANES
    padded = -(-flat.shape[0] // unit) * unit
    flat = jnp.concatenate([flat, jnp.zeros((padded - flat.shape[0],), F32)])
    return flat.reshape(-1, LANES)


def _groups_last(a):
    x, y = a.shape[-2:]
    return jnp.transpose(a.reshape(S5_GROUPS, x, y), (1, 2, 0)).reshape(x * y, S5_GROUPS)


def _groups_first(a, shape):
    x, y = shape[-2:]
    return jnp.transpose(a.reshape(x, y, S5_GROUPS), (2, 0, 1)).reshape(shape)


def _unpack(buf, shapes):
    flat = buf.reshape(-1)
    out, off = [], 0
    for s in shapes:
        n = math.prod(s)
        out.append(flat[off:off + n].reshape(s))
        off += n
    return out


def _row_tile(L):
    return 256 if L % 256 == 0 else L


def _layer0_mix(diff, const):
    x, mod, norm_w, lam_re, lam_im, log_dt, b_re, b_im, c_re, c_im, s5_d, *slots = diff
    ada_b, weights = const
    L = x.shape[0]
    tm = _row_tile(L)
    mods = mod.reshape(2, 1, D_MODEL)
    biases = ada_b.reshape(2, 1, D_MODEL)
    op_ln0 = make_rowwise(_f_lnmod, "ln0", tm, 1, 5, pass_first=True)
    h, x = op_ln0((x,), (norm_w.reshape(1, D_MODEL), mods[1], mods[0], biases[1], biases[0]))
    u, z = make_proj("s5_in")(h, tuple(weights), tuple(slots))
    blocks = _s5_block_params(lam_re, lam_im, log_dt, b_re, b_im, c_re, c_im)
    y2 = make_s5_core(min(S5_TL, L))(u, *blocks, s5_d.reshape(1, D_INNER))
    return x, y2, z


def _glu_gate_fwd_call(y2, w, z, name):
    m, k = y2.shape
    n = w.shape[1]
    tm, tn = _tile(m, 1024), _tile(n, 512)

    def body(a_ref, w_ref, z_ref, t_ref, y4_ref):
        j = pl.program_id(1)
        t = _bdot(a_ref[...], w_ref[...])
        t_ref[...] = t
        y2_tile = a_ref[:, pl.ds(pl.multiple_of(j * tn, LANES), tn)]
        y4_ref[...] = _f_s5_gate(y2_tile, t, z_ref[...])[0]

    tile = pl.BlockSpec((tm, tn), lambda i, j: (i, j))
    return pl.pallas_call(
        body, name=name, grid=(m // tm, n // tn),
        in_specs=[pl.BlockSpec((tm, k), lambda i, j: (i, 0)), pl.BlockSpec((k, tn), lambda i, j: (0, j)), tile],
        out_specs=[tile, tile], out_shape=[SDS((m, n), F32), SDS((m, n), F32)],
        compiler_params=_params(("parallel", "parallel"), VMEM_MID),
    )(y2, w, z)


def make_glu_gate(name, tm):
    gate = make_rowwise(_f_s5_gate, name + "_gate", tm, 3, 0)

    @jax.custom_vjp
    def op(y2, w, grad_slot, z):
        return _glu_gate_fwd_call(y2, w, z, name + "_fwd")[1]

    def fwd(y2, w, grad_slot, z):
        t, y4 = _glu_gate_fwd_call(y2, w, z, name + "_fwd")
        return y4, (y2, w, t, z)

    def bwd(res, dy4):
        y2, w, t, z = res
        (dy2_gate, dt, dz), _ = gate.run_bwd((y2, t, z), (), (dy4,))
        return _matmul(dt, w, "nt", name + "_dx", add=dy2_gate), jnp.zeros_like(w), _matmul(y2, dt, "tn", name + "_dw"), dz

    op.defvjp(fwd, bwd)
    return op


def _layer0_out(diff, weights):
    y2, z, *slots = diff
    y4 = make_glu_gate("s5_glu", _row_tile(y2.shape[0]))(y2, weights[0], slots[0], z)
    return make_mm("s5_out")(y4, weights[1], slots[1])


def _f_res_lnmod(x, o, gate, bgate, nw, sc, sh, bsc, bsh):
    (x1,) = _f_res(x, o, gate, bgate)
    return _f_lnmod(x1, nw, sc, sh, bsc, bsh) + (x1,)


def _f_res_loss(x, y, tgt, gate, bgate, fw):
    return _f_loss(_f_res(x, y, gate, bgate)[0], tgt, fw)


def _layer1_loss(diff, const):
    x, o, gate0, mod, norm_w, conv_w, a_log, dt_bias, gdn_nw, final_nw, *slots = diff
    tgt, bgate0, ada_b, weights = const
    L = x.shape[0]
    tm = _row_tile(L)
    mods = mod.reshape(3, 1, D_MODEL)
    biases = ada_b.reshape(3, 1, D_MODEL)
    h, x1 = make_rowwise(_f_res_lnmod, "res0_ln1", tm, 2, 7)(
        (x, o), (gate0.reshape(1, D_MODEL), bgate0.reshape(1, D_MODEL), norm_w.reshape(1, D_MODEL), mods[1], mods[0], biases[1], biases[0]))
    q0, k0, v0, gz, ba = make_proj("gdn_in", w_rows=True)(h, tuple(weights[0:5]), tuple(slots[0:5]))
    cw = jnp.concatenate([conv_w, jnp.zeros((SUBLANES - GDN_CONV, GDN_CONV_CH), F32)], axis=0)
    q = make_conv_act(lambda t: _l2n(_silu(t)) * (GDN_DK ** -0.5), "gdn_conv_q")(q0, cw[:, :GDN_QK])
    k = make_conv_act(lambda t: _l2n(_silu(t)), "gdn_conv_k")(k0, cw[:, GDN_QK:2 * GDN_QK])
    v = make_conv_act(_silu, "gdn_conv_v")(v0, cw[:, 2 * GDN_QK:])
    pad = jnp.zeros((LANES - 2 * GDN_HEADS,), F32)
    alog_row = jnp.concatenate([jnp.zeros((GDN_HEADS,), F32), a_log, pad]).reshape(1, LANES)
    dtb_row = jnp.concatenate([jnp.zeros((GDN_HEADS,), F32), dt_bias, pad]).reshape(1, LANES)
    (bg,) = make_rowwise(_f_betag, "gdn_bg", tm, 1, 2)((ba,), (alog_row, dtb_row))
    nw_row = jnp.tile(gdn_nw, GDN_HEADS).reshape(1, D_INNER)
    on = gdn_scan(*gdn_prep(q, k, v, bg), gz, nw_row)
    y = make_mm("gdn_out")(on, weights[5], slots[5])
    (lt,) = make_rowwise(_f_res_loss, "res1_loss", tm, 3, 3)((x1, y, tgt), (mods[2], biases[2], final_nw.reshape(1, D_MODEL)))
    return jnp.sum(lt)


VEC_NAMES = ("ada_b", "norm_w", "s5_lambda_re", "s5_lambda_im", "s5_log_dt", "s5_d", "gdn_a_log", "gdn_dt_bias", "final_norm_w")
MAT_NAMES = ("s5_b_re", "s5_b_im", "s5_c_re", "s5_c_im")
S5_BIG = ("s5_w_in", "s5_w_glu", "s5_w_out")
GDN_BIG = ("gdn_w_in", "gdn_w_out")
BIG_NAMES = S5_BIG + GDN_BIG
WEIGHT_ORDER = ("ada_w", "ada_b", "norm_w", "s5_w_in", "s5_lambda_re", "s5_lambda_im", "s5_log_dt", "s5_b_re", "s5_b_im",
                "s5_c_re", "s5_c_im", "s5_d", "s5_w_glu", "s5_w_out", "gdn_w_in", "gdn_conv_w", "gdn_a_log", "gdn_dt_bias",
                "gdn_norm_w", "gdn_w_out", "final_norm_w")


def _step(x, c, W, M, V, tgt):
    L = x.shape[1]
    ix, iy, ic = lax.axis_index("x"), lax.axis_index("y"), lax.axis_index("c")
    me = 4 * ix + 2 * iy + ic
    n_ada = W["ada_w"].shape[2]
    n_conv = W["gdn_conv_w"].shape[2]
    n_gnw = W["gdn_norm_w"].shape[1]

    g1 = _allgather_call(_pack([c, W["gdn_conv_w"], W["gdn_norm_w"]], SUBLANES), "gather_small_in")
    g1 = g1.reshape(N_DEV, -1)
    c_all = g1[:, :D_MODEL]
    conv_w = g1[:, D_MODEL:D_MODEL + GDN_CONV * n_conv].reshape(N_DEV, GDN_CONV, n_conv).transpose(1, 0, 2).reshape(GDN_CONV, -1)
    gdn_nw = g1[:, D_MODEL + GDN_CONV * n_conv:D_MODEL + GDN_CONV * n_conv + n_gnw].reshape(-1)
    mod_part = _ada_mod_call(c_all, W["ada_w"])
    g2 = _allgather_call(_pack([mod_part], SUBLANES), "gather_mod").reshape(N_DEV, -1)
    mod_all = g2[:, :2 * N_DEV * n_ada].reshape(N_DEV, 2, N_DEV, n_ada)
    mod_raw = lax.dynamic_index_in_dim(mod_all, me, axis=2, keepdims=False)
    mod_raw = mod_raw.transpose(1, 0, 2).reshape(2, 3 * D_MODEL)

    shard = lambda n: W[n][0].astype(BF16)
    (w_in5_parts,) = _gather_weights_call([shard("s5_w_in")], "gather_s5_w_in")
    late = _spread_start_call([shard("s5_w_glu"), shard("s5_w_out")], False, "gather_s5_late_start", w_in5_parts)
    turned = lambda a: jnp.transpose(a[0])
    g_send, g_recv, g_srcs, g_lands, g_token = _spread_start_call(
        [turned(W["gdn_w_in"]).astype(BF16), shard("gdn_w_out")], False, "gather_gdn_start", late[4])
    w_in5 = _join_cols_call(w_in5_parts, "join_s5_w_in")
    slot = lambda *s: jnp.zeros(s, F32)
    two = 2 * D_MODEL
    diff_mix = (x[0], mod_raw[0, :two] + g_token[0, 0], W["norm_w"][0], W["s5_lambda_re"][0], W["s5_lambda_im"][0], W["s5_log_dt"][0],
                W["s5_b_re"][0], W["s5_b_im"][0], W["s5_c_re"][0], W["s5_c_im"][0], W["s5_d"][0],
                slot(D_MODEL, D_INNER), slot(D_MODEL, D_INNER))

    (xp, y2, z5), vjp_mix = jax.vjp(lambda d: _layer0_mix(d, (W["ada_b"][0, :two], (w_in5[:, :D_INNER], w_in5[:, D_INNER:]))), diff_mix)
    l_srcs, l_lands = _spread_wait_call(late[0], late[1], late[2], late[3], y2, "gather_s5_late_wait")
    w_glu, w_o5 = [lax.dynamic_update_slice(land, src[None], (me, 0, 0)).reshape(-1, src.shape[1]) for land, src in zip(l_lands, l_srcs)]
    diff_out = (y2, z5, slot(D_INNER, D_INNER), slot(D_INNER, D_MODEL))
    o5, vjp_out = jax.vjp(lambda d: _layer0_out(d, (w_glu, w_o5)), diff_out)
    g_srcs, g_lands = _spread_wait_call(g_send, g_recv, g_srcs, g_lands, o5, "gather_gdn_wait")
    gdn_full = [lax.dynamic_update_slice(land, src[None], (me, 0, 0)) for land, src in zip(g_lands, g_srcs)]
    w_ing = gdn_full[0].reshape(GDN_PROJ, D_MODEL)
    w_ba = jnp.concatenate([w_ing[GDN_CONV_CH + D_INNER:], jnp.zeros((LANES - 2 * GDN_HEADS, D_MODEL), BF16)], axis=0)
    weights1 = (w_ing[:GDN_QK], w_ing[GDN_QK:2 * GDN_QK], w_ing[2 * GDN_QK:GDN_CONV_CH],
                w_ing[GDN_CONV_CH:GDN_CONV_CH + D_INNER], w_ba, gdn_full[1].reshape(D_INNER, D_MODEL))
    slots1 = tuple(jnp.zeros(w.shape, F32) for w in weights1)
    diff1 = (xp, o5, mod_raw[0, two:], mod_raw[1], W["norm_w"][1], conv_w, W["gdn_a_log"][0], W["gdn_dt_bias"][0], gdn_nw,
             W["final_norm_w"], *slots1)
    loss_local, vjp1 = jax.vjp(lambda d: _layer1_loss(d, (tgt[0], W["ada_b"][0, two:], W["ada_b"][1], weights1)), diff1)
    ((dxp, do5, dmod_gate, dmod1, d_norm_w1, d_conv, d_alog, d_dtb, d_gnw, d_fnw, d_wq, d_wk, d_wv, d_wgz, d_wba, d_wog),) = vjp1(
        jnp.ones((), F32))
    loss = lax.psum(loss_local, MESH_AXES)

    rows = lambda d: d.reshape(N_DEV, d.shape[0] // N_DEV, d.shape[1])
    d_ing = jnp.concatenate([d_wq, d_wk, d_wv, d_wgz, d_wba[:2 * GDN_HEADS]], axis=0).astype(BF16).reshape(N_DEV, -1, D_MODEL)
    s_send, s_recv, s_srcs, s_lands, s_token = _spread_start_call([d_ing, rows(d_wog).astype(BF16)], True, "scatter_gdn_start", dxp)
    ((dy2, dz5, d_wglu, d_wo5),) = vjp_out(do5.at[0, 0].add(s_token[0, 0]))
    t_send, t_recv, t_srcs, t_lands, t_token = _spread_start_call(
        [rows(d_wglu).astype(BF16), rows(d_wo5).astype(BF16)], True, "scatter_s5_late_start", dy2)
    ((dx, dmod_ss, d_norm_w0, d_lre, d_lim, d_logdt, d_bre, d_bim, d_cre, d_cim, d_s5d, d_wu, d_wz),) = vjp_mix(
        (dxp.at[0, 0].add(t_token[0, 0]), dy2, dz5))
    dmod = jnp.stack([jnp.concatenate([dmod_ss, dmod_gate]), dmod1])
    d_norm_w = jnp.stack([d_norm_w0, d_norm_w1])
    vec_parts = [dmod, d_norm_w, d_lre, d_lim, d_logdt, d_s5d, d_alog, d_dtb, d_fnw]
    tail_parts = [d_conv, d_gnw]
    mat_parts = [_groups_last(d) for d in (d_bre, d_bim, d_cre, d_cim)]
    n_vec = sum(math.prod(p.shape) for p in vec_parts)
    m_send, m_recv, m_srcs, m_lands, m_token = _spread_start_call(
        [_pack(vec_parts + tail_parts, ADAM_ROWS), _pack(mat_parts, SUBLANES).astype(BF16)], False, "gather_small_grads_start", dx)
    d_in5 = _split_cols_call(jnp.concatenate([d_wu.at[0, 0].add(m_token[0, 0]), d_wz], axis=1), "split_s5_w_in", BF16)
    u_send, u_recv, u_srcs, u_lands, u_token = _spread_start_call([d_in5], True, "scatter_s5_in_start", m_token)
    t_srcs, t_lands = _spread_wait_call(t_send, t_recv, t_srcs, t_lands, u_token, "scatter_s5_late_wait")
    s_srcs, s_lands = _spread_wait_call(s_send, s_recv, s_srcs, s_lands, t_lands[0], "scatter_gdn_wait")
    big = {}

    def owner_update(land, src, n):
        mine = lax.dynamic_index_in_dim(src, me, 0, keepdims=True)
        parts = lax.dynamic_update_slice(land, mine, (me, 0, 0))
        if n == "gdn_w_in":
            outs = _adam_call(parts, turned(W[n]), turned(M[n]), turned(V[n]), "adam_" + n, by_cols=True)
            return [jnp.transpose(o) for o in outs]
        return _adam_call(parts, W[n][0], M[n][0], V[n][0], "adam_" + n, rows=_tile(W[n].shape[1], 128))

    for land, src, n in zip(tuple(t_lands) + tuple(s_lands), tuple(t_srcs) + tuple(s_srcs), ("s5_w_glu", "s5_w_out") + GDN_BIG):
        big[n] = owner_update(land, src, n)

    m_srcs, m_lands = _spread_wait_call(m_send, m_recv, m_srcs, m_lands, big["gdn_w_out"][0], "gather_small_grads_wait")
    sg_vec, sg_mat = [lax.dynamic_update_slice(land, src[None], (me, 0, 0)) for land, src in zip(m_lands, m_srcs)]
    tot_vec = _sum_call(sg_vec, "sum_vec_grads", ADAM_ROWS)
    tot_mat = _sum_call(sg_mat, "sum_mat_grads", ADAM_ROWS)
    g_conv, g_gnw = _unpack(tot_vec.reshape(-1)[n_vec:], [d_conv.shape, d_gnw.shape])
    g_conv_mine = lax.dynamic_slice_in_dim(g_conv, me * n_conv, n_conv, axis=1)
    g_gnw_mine = lax.dynamic_slice_in_dim(g_gnw, me * n_gnw, n_gnw, axis=0)
    vec_names = VEC_NAMES + ("gdn_conv_w", "gdn_norm_w")
    vec_g = _pack([tot_vec.reshape(-1)[:n_vec], g_conv_mine, g_gnw_mine], ADAM_ROWS)
    vec = _adam_call(vec_g[None], _pack([W[n] for n in vec_names], ADAM_ROWS), _pack([M[n] for n in vec_names], ADAM_ROWS),
                     _pack([V[n] for n in vec_names], ADAM_ROWS), "adam_vec")
    vec = [_unpack(b, [W[n].shape for n in vec_names]) for b in vec]
    mats = []
    for name, g_mat in zip(MAT_NAMES, _unpack(tot_mat, [p.shape for p in mat_parts])):
        outs = _adam_call(g_mat[None], _groups_last(W[name]), _groups_last(M[name]), _groups_last(V[name]), "adam_" + name)
        mats.append([_groups_first(o, W[name].shape) for o in outs])

    dmod_all = sg_vec[:, :2 * 3 * D_MODEL // LANES].reshape(N_DEV, 2, N_DEV, n_ada // LANES, LANES)
    dmod_mine = lax.dynamic_index_in_dim(dmod_all, me, axis=2, keepdims=False).transpose(1, 0, 2, 3).reshape(2, N_DEV, n_ada)
    g_ada_w = _ada_grad_call(c_all, dmod_mine)
    ada = _adam_call(g_ada_w.reshape(1, -1, LANES), W["ada_w"].reshape(-1, LANES), M["ada_w"].reshape(-1, LANES),
                     V["ada_w"].reshape(-1, LANES), "adam_ada")
    u_srcs, u_lands = _spread_wait_call(u_send, u_recv, u_srcs, u_lands, ada[0][:SUBLANES] + loss, "scatter_s5_in_wait")
    big["s5_w_in"] = owner_update(u_lands[0], u_srcs[0], "s5_w_in")
    ada = [a.reshape(W["ada_w"].shape) for a in ada]

    res = {}
    for n in BIG_NAMES:
        res[n] = [o[None] for o in big[n]]
    for i, n in enumerate(vec_names):
        res[n] = [b[i] for b in vec]
    for i, n in enumerate(MAT_NAMES):
        res[n] = mats[i]
    res["ada_w"] = ada
    outs = [loss, dx[None]]
    for j in range(4):
        outs += [res[n][j] for n in WEIGHT_ORDER]
    return tuple(outs)


def kernel(x, c, ada_w, ada_b, norm_w, s5_w_in, s5_lambda_re, s5_lambda_im, s5_log_dt, s5_b_re, s5_b_im, s5_c_re, s5_c_im, s5_d, s5_w_glu, s5_w_out, gdn_w_in, gdn_conv_w, gdn_a_log, gdn_dt_bias, gdn_norm_w, gdn_w_out, final_norm_w, loss_target, m_ada_w, m_ada_b, m_norm_w, m_s5_w_in, m_s5_lambda_re, m_s5_lambda_im, m_s5_log_dt, m_s5_b_re, m_s5_b_im, m_s5_c_re, m_s5_c_im, m_s5_d, m_s5_w_glu, m_s5_w_out, m_gdn_w_in, m_gdn_conv_w, m_gdn_a_log, m_gdn_dt_bias, m_gdn_norm_w, m_gdn_w_out, m_final_norm_w, v_ada_w, v_ada_b, v_norm_w, v_s5_w_in, v_s5_lambda_re, v_s5_lambda_im, v_s5_log_dt, v_s5_b_re, v_s5_b_im, v_s5_c_re, v_s5_c_im, v_s5_d, v_s5_w_glu, v_s5_w_out, v_gdn_w_in, v_gdn_conv_w, v_gdn_a_log, v_gdn_dt_bias, v_gdn_norm_w, v_gdn_w_out, v_final_norm_w):
    W = dict(ada_w=ada_w, ada_b=ada_b, norm_w=norm_w, s5_w_in=s5_w_in, s5_lambda_re=s5_lambda_re, s5_lambda_im=s5_lambda_im,
             s5_log_dt=s5_log_dt, s5_b_re=s5_b_re, s5_b_im=s5_b_im, s5_c_re=s5_c_re, s5_c_im=s5_c_im, s5_d=s5_d,
             s5_w_glu=s5_w_glu, s5_w_out=s5_w_out, gdn_w_in=gdn_w_in, gdn_conv_w=gdn_conv_w, gdn_a_log=gdn_a_log,
             gdn_dt_bias=gdn_dt_bias, gdn_norm_w=gdn_norm_w, gdn_w_out=gdn_w_out, final_norm_w=final_norm_w)
    M = dict(ada_w=m_ada_w, ada_b=m_ada_b, norm_w=m_norm_w, s5_w_in=m_s5_w_in, s5_lambda_re=m_s5_lambda_re,
             s5_lambda_im=m_s5_lambda_im, s5_log_dt=m_s5_log_dt, s5_b_re=m_s5_b_re, s5_b_im=m_s5_b_im, s5_c_re=m_s5_c_re,
             s5_c_im=m_s5_c_im, s5_d=m_s5_d, s5_w_glu=m_s5_w_glu, s5_w_out=m_s5_w_out, gdn_w_in=m_gdn_w_in,
             gdn_conv_w=m_gdn_conv_w, gdn_a_log=m_gdn_a_log, gdn_dt_bias=m_gdn_dt_bias, gdn_norm_w=m_gdn_norm_w,
             gdn_w_out=m_gdn_w_out, final_norm_w=m_final_norm_w)
    V = dict(ada_w=v_ada_w, ada_b=v_ada_b, norm_w=v_norm_w, s5_w_in=v_s5_w_in, s5_lambda_re=v_s5_lambda_re,
             s5_lambda_im=v_s5_lambda_im, s5_log_dt=v_s5_log_dt, s5_b_re=v_s5_b_re, s5_b_im=v_s5_b_im, s5_c_re=v_s5_c_re,
             s5_c_im=v_s5_c_im, s5_d=v_s5_d, s5_w_glu=v_s5_w_glu, s5_w_out=v_s5_w_out, gdn_w_in=v_gdn_w_in,
             gdn_conv_w=v_gdn_conv_w, gdn_a_log=v_gdn_a_log, gdn_dt_bias=v_gdn_dt_bias, gdn_norm_w=v_gdn_norm_w,
             gdn_w_out=v_gdn_w_out, final_norm_w=v_final_norm_w)
    return _step(x, c, W, M, V, loss_target)
```

```python
import functools
import math

import jax
import jax.numpy as jnp
from jax import lax
from jax.experimental import pallas as pl
from jax.experimental.pallas import tpu as pltpu

F32 = jnp.float32
BF16 = jnp.bfloat16
SDS = jax.ShapeDtypeStruct

D_MODEL = 1024
D_INNER = 2048
NORM_EPS = 1e-6
S5_GROUP = 16
S5_GROUPS = 128
S5_STATE = 64
GDN_HEADS = 8
GDN_DK = 128
GDN_DV = 256
GDN_CONV = 4
GDN_CHUNK = 64
GDN_QK = 1024
GDN_CONV_CH = 4096
GDN_PROJ = 6160
ADAM_LR = 0.001
ADAM_B1 = 0.9
ADAM_B2 = 0.999
ADAM_EPS = 1e-08
ADAM_WD = 0.01
ADAM_STEP = 10

N_DEV = 8
LANES = 128
SUBLANES = 8
VMEM_BIG = 56 << 20
VMEM_MID = 40 << 20
S5_GB = 8
S5_BW = S5_GB * S5_GROUP
S5_TL = 2048
MESH_AXES = ("x", "y", "c")


def _params(sem, vmem=None):
    return pltpu.CompilerParams(dimension_semantics=sem, vmem_limit_bytes=vmem)


def _bdot(a, b, dims=(((1,), (0,)), ((), ()))):
    return lax.dot_general(a.astype(BF16), b.astype(BF16), dims, preferred_element_type=F32)


def _hdot(a, b, dims=(((1,), (0,)), ((), ()))):
    return lax.dot_general(a, b, dims, preferred_element_type=F32, precision=lax.Precision.HIGHEST)


_BNN = (((2,), (1,)), ((0,), (0,)))
_BNT = (((2,), (2,)), ((0,), (0,)))
_BTN = (((1,), (1,)), ((0,), (0,)))


@jax.custom_vjp
def _unit_lower_inverse(a):
    c = a.shape[-1]
    ri = lax.broadcasted_iota(jnp.int32, a.shape, 1)
    ci = lax.broadcasted_iota(jnp.int32, a.shape, 2)
    n = -a
    t = (ri == ci).astype(F32) + n
    for _ in range(int(math.log2(c)) - 1):
        n = _hdot(n, n, _BNN)
        t = t + _hdot(t, n, _BNN)
    return t


def _unit_lower_inverse_fwd(a):
    t = _unit_lower_inverse(a)
    return t, t


def _unit_lower_inverse_bwd(t, g):
    return (-_hdot(_hdot(t, g, _BTN), t, _BNT),)


_unit_lower_inverse.defvjp(_unit_lower_inverse_fwd, _unit_lower_inverse_bwd)


NN = (((1,), (0,)), ((), ()))
NT = (((1,), (1,)), ((), ()))
TN = (((0,), (0,)), ((), ()))


def _tile(n, pref):
    for t in (pref, 512, 256, 128):
        if t <= n and n % t == 0:
            return t
    return n


def _matmul(a, b, mode, name, add=None):
    if mode == "nn":
        (m, k), (_, n) = a.shape, b.shape
    elif mode == "nt":
        (m, k), (n, _) = a.shape, b.shape
    else:
        (k, m), (_, n) = a.shape, b.shape
    tm, tn, tk = _tile(m, 1024), _tile(n, 512), (k if k <= 2048 else _tile(k, 512))
    if mode == "tn":
        tm, tn, tk = _tile(m, 1024), _tile(n, 1024), _tile(k, 1024)
    nk = k // tk
    dims = {"nn": NN, "nt": NT, "tn": TN}[mode]

    def body(a_ref, b_ref, *rest):
        o_ref, acc_ref = rest[-2], rest[-1]
        part = _bdot(a_ref[...], b_ref[...], dims)
        if nk == 1:
            o_ref[...] = part if add is None else part + rest[0][...]
            return
        kk = pl.program_id(2)

        @pl.when(kk == 0)
        def _():
            acc_ref[...] = part if add is None else part + rest[0][...]

        @pl.when(kk > 0)
        def _():
            acc_ref[...] += part

        @pl.when(kk == nk - 1)
        def _():
            o_ref[...] = acc_ref[...]

    a_spec = pl.BlockSpec((tk, tm), lambda i, j, q: (q, i)) if mode == "tn" else pl.BlockSpec((tm, tk), lambda i, j, q: (i, q))
    b_spec = pl.BlockSpec((tn, tk), lambda i, j, q: (j, q)) if mode == "nt" else pl.BlockSpec((tk, tn), lambda i, j, q: (q, j))
    o_spec = pl.BlockSpec((tm, tn), lambda i, j, q: (i, j))
    return pl.pallas_call(
        body, name=name, grid=(m // tm, n // tn, nk),
        in_specs=[a_spec, b_spec] + ([] if add is None else [o_spec]), out_specs=o_spec,
        out_shape=SDS((m, n), F32), scratch_shapes=[pltpu.VMEM((tm, tn), F32)],
        compiler_params=_params(("parallel", "parallel", "arbitrary"), VMEM_MID),
    )(a, b, *([] if add is None else [add]))


def make_mm(name, pass_input=False):
    def primal(a, w):
        out = _matmul(a, w, "nn", name + "_fwd")
        return (out, a) if pass_input else out

    @jax.custom_vjp
    def mm(a, w, grad_slot):
        return primal(a, w)

    def fwd(a, w, grad_slot):
        return primal(a, w), (a, w)

    def bwd(res, g):
        a, w = res
        g, g_other = g if pass_input else (g, None)
        return _matmul(g, w, "nt", name + "_dx", add=g_other), jnp.zeros_like(w), _matmul(a, g, "tn", name + "_dw")

    mm.defvjp(fwd, bwd)
    return mm


PROJ_ROWS = 256


def _proj_fwd_call(a, ws, name, w_rows):
    m, k = a.shape
    tm = _tile(m, PROJ_ROWS)
    nw = len(ws)
    widths = [w.shape[0] if w_rows else w.shape[1] for w in ws]

    def body(*refs):
        ab = refs[0][...].astype(BF16)
        for w_ref, o_ref in zip(refs[1:1 + nw], refs[1 + nw:]):
            o_ref[...] = lax.dot_general(ab, w_ref[...], NT if w_rows else NN, preferred_element_type=F32)

    return pl.pallas_call(
        body, name=name, grid=(m // tm,),
        in_specs=[pl.BlockSpec((tm, k), lambda i: (i, 0))] + [pl.BlockSpec(w.shape, lambda i: (0, 0)) for w in ws],
        out_specs=[pl.BlockSpec((tm, n), lambda i: (i, 0)) for n in widths],
        out_shape=[SDS((m, n), F32) for n in widths],
        compiler_params=_params(("parallel",), VMEM_BIG),
    )(a, *ws)


def _proj_dx_call(gs, ws, name, w_rows):
    m = gs[0].shape[0]
    k = ws[0].shape[1] if w_rows else ws[0].shape[0]
    tm = _tile(m, PROJ_ROWS)
    nw = len(ws)

    def body(*refs):
        acc = None
        for g_ref, w_ref in zip(refs[:nw], refs[nw:2 * nw]):
            part = _bdot(g_ref[...], w_ref[...], NN if w_rows else NT)
            acc = part if acc is None else acc + part
        refs[2 * nw][...] = acc

    return pl.pallas_call(
        body, name=name, grid=(m // tm,),
        in_specs=[pl.BlockSpec((tm, g.shape[1]), lambda i: (i, 0)) for g in gs] + [pl.BlockSpec(w.shape, lambda i: (0, 0)) for w in ws],
        out_specs=pl.BlockSpec((tm, k), lambda i: (i, 0)), out_shape=SDS((m, k), F32),
        compiler_params=_params(("parallel",), VMEM_BIG),
    )(*gs, *ws)


def make_proj(name, w_rows=False):
    @jax.custom_vjp
    def proj(a, ws, grad_slots):
        return tuple(_proj_fwd_call(a, ws, name + "_fwd", w_rows))

    def fwd(a, ws, grad_slots):
        return tuple(_proj_fwd_call(a, ws, name + "_fwd", w_rows)), (a, ws)

    def bwd(res, gs):
        a, ws = res
        dws = tuple(_matmul(g, a, "tn", "%s_dw%d" % (name, i)) if w_rows else _matmul(a, g, "tn", "%s_dw%d" % (name, i))
                    for i, g in enumerate(gs))
        return _proj_dx_call(tuple(gs), ws, name + "_dx", w_rows), tuple(jnp.zeros_like(w) for w in ws), dws

    proj.defvjp(fwd, bwd)
    return proj


def make_rowwise(f, name, tm, n_rows, n_params, vmem=VMEM_MID, pass_first=False):
    def specs_of(arrs, blocked):
        if blocked:
            return [pl.BlockSpec((tm, a.shape[1]), lambda i: (i, 0)) for a in arrs]
        return [pl.BlockSpec(a.shape, lambda i: (0, 0)) for a in arrs]

    def out_structs(rows, params):
        blk = [SDS((tm, r.shape[1]), r.dtype) for r in rows] + [SDS(p.shape, p.dtype) for p in params]
        return jax.eval_shape(f, *blk)

    def run_fwd(rows, params):
        L = rows[0].shape[0]
        outs = out_structs(rows, params)

        def body(*refs):
            ins = [r[...] for r in refs[:n_rows + n_params]]
            res = f(*ins)
            for o_ref, val in zip(refs[n_rows + n_params:], res):
                o_ref[...] = val

        return pl.pallas_call(
            body, name=name + "_fwd", grid=(L // tm,),
            in_specs=specs_of(rows, True) + specs_of(params, False),
            out_specs=[pl.BlockSpec((tm, o.shape[1]), lambda i: (i, 0)) for o in outs],
            out_shape=[SDS((L, o.shape[1]), o.dtype) for o in outs],
            compiler_params=_params(("parallel",), vmem),
        )(*rows, *params)

    def run_bwd(rows, params, gs):
        L = rows[0].shape[0]
        n_g = len(gs)

        def body(*refs):
            i = pl.program_id(0)
            ins = [r[...] for r in refs[:n_rows + n_params]]
            cts = tuple(r[...] for r in refs[n_rows + n_params:n_rows + n_params + n_g])
            outs = refs[n_rows + n_params + n_g:]
            _, vjp = jax.vjp(f, *ins)
            grads = vjp(cts[:-1] if pass_first else cts)
            if pass_first:
                grads = (grads[0] + cts[-1],) + tuple(grads[1:])
            for o_ref, val in zip(outs[:n_rows], grads[:n_rows]):
                o_ref[...] = val

            if n_params:
                @pl.when(i == 0)
                def _():
                    for o_ref in outs[n_rows:]:
                        o_ref[...] = jnp.zeros_like(o_ref)
                for o_ref, val in zip(outs[n_rows:], grads[n_rows:]):
                    o_ref[...] += val

        res = pl.pallas_call(
            body, name=name + "_bwd", grid=(L // tm,),
            in_specs=specs_of(rows, True) + specs_of(params, False) + specs_of(gs, True),
            out_specs=specs_of(rows, True) + specs_of(params, False),
            out_shape=[SDS(r.shape, r.dtype) for r in rows] + [SDS(p.shape, p.dtype) for p in params],
            compiler_params=_params(("arbitrary",), vmem),
        )(*rows, *params, *gs)
        return tuple(res[:n_rows]), tuple(res[n_rows:])

    def outputs(rows, params):
        outs = tuple(run_fwd(rows, params))
        return outs + (rows[0],) if pass_first else outs

    @jax.custom_vjp
    def op(rows, params):
        return outputs(rows, params)

    def fwd(rows, params):
        return outputs(rows, params), (rows, params)

    def bwd(res, gs):
        rows, params = res
        return run_bwd(rows, params, tuple(gs))

    op.defvjp(fwd, bwd)
    op.run_fwd, op.run_bwd = run_fwd, run_bwd
    return op


def _s5_scan_rows(xr_ref, xi_ref, ar, ai, x0r, x0i, tl, reverse=False):
    n = xr_ref.shape[1]
    T = SUBLANES
    row = lax.broadcasted_iota(jnp.int32, (T, n), 0)
    pr, pi = [ar], [ai]
    for _ in range(T - 1):
        pr, pi = pr + [pr[-1] * ar - pi[-1] * ai], pi + [pr[-1] * ai + pi[-1] * ar]
    levels = []
    for d in (1, 2, 4):
        mask = (row < T - d) if reverse else (row >= d)
        levels.append((T - d if reverse else d, jnp.where(mask, pr[d - 1], 0.0), jnp.where(mask, pi[d - 1], 0.0)))
    cr = jnp.zeros((T, n), F32)
    ci = jnp.zeros((T, n), F32)
    for r in range(T):
        k = (T - r) if reverse else (r + 1)
        cr = jnp.where(row == r, pr[k - 1], cr)
        ci = jnp.where(row == r, pi[k - 1], ci)
    nt = tl // T
    last = 0 if reverse else T - 1

    def step(t, carry):
        sr, si = carry
        base = pl.multiple_of((nt - 1 - t if reverse else t) * T, T)
        br = xr_ref[pl.ds(base, T), :]
        bi = xi_ref[pl.ds(base, T), :]
        for shift, mr, mi in levels:
            qr = pltpu.roll(br, shift, 0)
            qi = pltpu.roll(bi, shift, 0)
            br, bi = br + (mr * qr - mi * qi), bi + (mr * qi + mi * qr)
        xr = br + (cr * sr - ci * si)
        xi = bi + (cr * si + ci * sr)
        xr_ref[pl.ds(base, T), :] = xr
        xi_ref[pl.ds(base, T), :] = xi
        return xr[last:last + 1, :], xi[last:last + 1, :]
    return lax.fori_loop(0, nt, step, (x0r, x0i))


def _s5_fwd_call(u, bre, bim, cre, cim, a, d, tl):
    L, e = u.shape
    nb = e // S5_BW
    ns = bre.shape[2]
    nc = L // tl

    def body(u_ref, bre_ref, bim_ref, cre_ref, cim_ref, a_ref, d_ref, y_ref, xb_ref, sr_ref, si_ref, xr_ref, xi_ref, carry_ref):
        c = pl.program_id(1)

        @pl.when(c == 0)
        def _():
            carry_ref[...] = jnp.zeros_like(carry_ref)
        xb_ref[0, 0] = carry_ref[...]
        ub = u_ref[...]
        xr_ref[...] = _bdot(ub, bre_ref[0])
        xi_ref[...] = _bdot(ub, bim_ref[0])
        ar = a_ref[0, 0:1, :]
        ai = a_ref[0, 1:2, :]
        xr, xi = _s5_scan_rows(xr_ref, xi_ref, ar, ai, carry_ref[0:1, :], carry_ref[1:2, :], tl)
        carry_ref[0:1, :] = xr
        carry_ref[1:2, :] = xi
        sr = xr_ref[...].astype(BF16)
        si = xi_ref[...].astype(BF16)
        sr_ref[...] = sr
        si_ref[...] = si
        y_ref[...] = _f_s5_act(_bdot(sr, cre_ref[0]) - _bdot(si, cim_ref[0]), ub, d_ref[...])[0]

    return pl.pallas_call(
        body, name="s5_core_fwd", grid=(nb, nc),
        in_specs=[pl.BlockSpec((tl, S5_BW), lambda j, c: (c, j)),
                  pl.BlockSpec((1, S5_BW, ns), lambda j, c: (j, 0, 0)), pl.BlockSpec((1, S5_BW, ns), lambda j, c: (j, 0, 0)),
                  pl.BlockSpec((1, ns, S5_BW), lambda j, c: (j, 0, 0)), pl.BlockSpec((1, ns, S5_BW), lambda j, c: (j, 0, 0)),
                  pl.BlockSpec((1, SUBLANES, ns), lambda j, c: (j, 0, 0)), pl.BlockSpec((1, S5_BW), lambda j, c: (0, j))],
        out_specs=[pl.BlockSpec((tl, S5_BW), lambda j, c: (c, j)),
                   pl.BlockSpec((1, 1, SUBLANES, ns), lambda j, c: (j, c, 0, 0)),
                   pl.BlockSpec((tl, ns), lambda j, c: (c, j)), pl.BlockSpec((tl, ns), lambda j, c: (c, j))],
        out_shape=[SDS((L, e), F32), SDS((nb, nc, SUBLANES, ns), F32), SDS((L, nb * ns), BF16), SDS((L, nb * ns), BF16)],
        scratch_shapes=[pltpu.VMEM((tl, ns), F32), pltpu.VMEM((tl, ns), F32), pltpu.VMEM((SUBLANES, ns), F32)],
        compiler_params=_params(("arbitrary", "arbitrary"), VMEM_MID),
    )(u, bre, bim, cre, cim, a, d)


def _s5_bwd_call(u, dy2, bre, bim, cre, cim, a, d, xb, sr, si, tl):
    L, e = u.shape
    nb = e // S5_BW
    ns = bre.shape[2]
    nc = L // tl

    def body(u_ref, dy2_ref, bre_ref, bim_ref, cre_ref, cim_ref, a_ref, d_ref, xb_ref, sr_ref, si_ref,
             du_ref, dbre_ref, dbim_ref, dcre_ref, dcim_ref, da_ref, dd_ref,
             gr_ref, gi_ref, gcarry_ref):
        c = pl.program_id(1)

        @pl.when(c == 0)
        def _():
            gcarry_ref[...] = jnp.zeros_like(gcarry_ref)
            dbre_ref[...] = jnp.zeros_like(dbre_ref)
            dbim_ref[...] = jnp.zeros_like(dbim_ref)
            dcre_ref[...] = jnp.zeros_like(dcre_ref)
            dcim_ref[...] = jnp.zeros_like(dcim_ref)
            da_ref[...] = jnp.zeros_like(da_ref)
            dd_ref[...] = jnp.zeros_like(dd_ref)

        ub = u_ref[...]
        ys = _bdot(sr_ref[...], cre_ref[0]) - _bdot(si_ref[...], cim_ref[0])
        _, act_vjp = jax.vjp(lambda *t: _f_s5_act(*t)[0], ys, ub, d_ref[...])
        dy, du_skip, dd = act_vjp(dy2_ref[...])
        dd_ref[...] += dd
        ar = a_ref[0, 0:1, :]
        ai = a_ref[0, 1:2, :]
        x0r = xb_ref[0, 0, 0:1, :]
        x0i = xb_ref[0, 0, 1:2, :]
        dcre_ref[0] += _bdot(sr_ref[...], dy, TN)
        dcim_ref[0] -= _bdot(si_ref[...], dy, TN)
        gr_ref[...] = _bdot(dy, cre_ref[0], NT)
        gi_ref[...] = -_bdot(dy, cim_ref[0], NT)

        g0r, g0i = _s5_scan_rows(gr_ref, gi_ref, ar, -ai, gcarry_ref[0:1, :], gcarry_ref[1:2, :], tl, reverse=True)
        gcarry_ref[0:1, :] = g0r
        gcarry_ref[1:2, :] = g0i
        row = lax.broadcasted_iota(jnp.int32, (tl, ns), 0)
        gr = gr_ref[...]
        gi = gi_ref[...]
        xpr = jnp.where(row == 0, x0r, pltpu.roll(sr_ref[...].astype(F32), 1, 0))
        xpi = jnp.where(row == 0, x0i, pltpu.roll(si_ref[...].astype(F32), 1, 0))
        da_ref[0, 0:1, :] += jnp.sum(gr * xpr + gi * xpi, axis=0, keepdims=True)
        da_ref[0, 1:2, :] += jnp.sum(gi * xpr - gr * xpi, axis=0, keepdims=True)
        du_ref[...] = (_bdot(gr, bre_ref[0], NT) + _bdot(gi, bim_ref[0], NT)) + du_skip
        dbre_ref[0] += _bdot(ub, gr, TN)
        dbim_ref[0] += _bdot(ub, gi, TN)

    rev = lambda c: nc - 1 - c
    return pl.pallas_call(
        body, name="s5_core_bwd", grid=(nb, nc),
        in_specs=[pl.BlockSpec((tl, S5_BW), lambda j, c: (rev(c), j)), pl.BlockSpec((tl, S5_BW), lambda j, c: (rev(c), j)),
                  pl.BlockSpec((1, S5_BW, ns), lambda j, c: (j, 0, 0)), pl.BlockSpec((1, S5_BW, ns), lambda j, c: (j, 0, 0)),
                  pl.BlockSpec((1, ns, S5_BW), lambda j, c: (j, 0, 0)), pl.BlockSpec((1, ns, S5_BW), lambda j, c: (j, 0, 0)),
                  pl.BlockSpec((1, SUBLANES, ns), lambda j, c: (j, 0, 0)), pl.BlockSpec((1, S5_BW), lambda j, c: (0, j)),
                  pl.BlockSpec((1, 1, SUBLANES, ns), lambda j, c: (j, rev(c), 0, 0)),
                  pl.BlockSpec((tl, ns), lambda j, c: (rev(c), j)), pl.BlockSpec((tl, ns), lambda j, c: (rev(c), j))],
        out_specs=[pl.BlockSpec((tl, S5_BW), lambda j, c: (rev(c), j)),
                   pl.BlockSpec((1, S5_BW, ns), lambda j, c: (j, 0, 0)), pl.BlockSpec((1, S5_BW, ns), lambda j, c: (j, 0, 0)),
                   pl.BlockSpec((1, ns, S5_BW), lambda j, c: (j, 0, 0)), pl.BlockSpec((1, ns, S5_BW), lambda j, c: (j, 0, 0)),
                   pl.BlockSpec((1, SUBLANES, ns), lambda j, c: (j, 0, 0)), pl.BlockSpec((1, S5_BW), lambda j, c: (0, j))],
        out_shape=[SDS((L, e), F32), SDS(bre.shape, F32), SDS(bim.shape, F32), SDS(cre.shape, F32), SDS(cim.shape, F32),
                   SDS(a.shape, F32), SDS(d.shape, F32)],
        scratch_shapes=[pltpu.VMEM((tl, ns), F32) for _ in range(2)] + [pltpu.VMEM((SUBLANES, ns), F32)],
        compiler_params=_params(("arbitrary", "arbitrary"), VMEM_MID),
    )(u, dy2, bre, bim, cre, cim, a, d, xb, sr, si)


def make_s5_core(tl):
    @jax.custom_vjp
    def s5_core(u, bre, bim, cre, cim, a, d):
        return _s5_fwd_call(u, bre, bim, cre, cim, a, d, tl)[0]

    def fwd(u, bre, bim, cre, cim, a, d):
        y2, xb, sr, si = _s5_fwd_call(u, bre, bim, cre, cim, a, d, tl)
        return y2, (u, bre, bim, cre, cim, a, d, xb, sr, si)

    def bwd(res, dy2):
        u, bre, bim, cre, cim, a, d, xb, sr, si = res
        return tuple(_s5_bwd_call(u, dy2, bre, bim, cre, cim, a, d, xb, sr, si, tl))

    s5_core.defvjp(fwd, bwd)
    return s5_core


def _s5_block_params(lam_re, lam_im, log_dt, b_re, b_im, c_re, c_im):
    dt = jnp.exp(log_dt)[:, None]
    mag = jnp.exp(lam_re * dt)
    ab_re = mag * jnp.cos(lam_im * dt)
    ab_im = mag * jnp.sin(lam_im * dt)
    den = lam_re * lam_re + lam_im * lam_im
    nr = ab_re - 1.0
    ni = ab_im
    q_re = (nr * lam_re + ni * lam_im) / den
    q_im = (ni * lam_re - nr * lam_im) / den
    bb_re = q_re[..., None] * b_re - q_im[..., None] * b_im
    bb_im = q_re[..., None] * b_im + q_im[..., None] * b_re
    nb = S5_GROUPS // S5_GB
    eye = jnp.eye(S5_GB, dtype=F32)

    def bdiag_in(bb):
        t = bb.reshape(nb, S5_GB, S5_STATE, S5_GROUP)
        t = jnp.einsum("jgpm,gh->jgmhp", t, eye)
        return t.reshape(nb, S5_GB * S5_GROUP, S5_GB * S5_STATE)

    def bdiag_out(cc):
        t = cc.reshape(nb, S5_GB, S5_GROUP, S5_STATE)
        t = jnp.einsum("jgmp,gh->jgphm", t, eye)
        return t.reshape(nb, S5_GB * S5_STATE, S5_GB * S5_GROUP)

    a = jnp.stack([ab_re.reshape(nb, S5_GB * S5_STATE), ab_im.reshape(nb, S5_GB * S5_STATE)], axis=1)
    a = jnp.concatenate([a, jnp.zeros((nb, SUBLANES - 2, S5_GB * S5_STATE), F32)], axis=1)
    return bdiag_in(bb_re), bdiag_in(bb_im), bdiag_out(c_re), bdiag_out(c_im), a


def _shift_down(x, s, row):
    if s == 0:
        return x
    return jnp.where(row >= s, pltpu.roll(x, s, 0), 0.0)


def _shift_up(x, s, row, n):
    if s == 0:
        return x
    return jnp.where(row < n - s, pltpu.roll(x, n - s, 0), 0.0)


def _causal_conv(xv, w_ref, row):
    acc = jnp.zeros_like(xv)
    for j in range(GDN_CONV):
        acc += w_ref[j:j + 1, :] * _shift_down(xv, GDN_CONV - 1 - j, row)
    return acc


def _conv_fwd_call(x, w, act, name):
    L, ch = x.shape

    def body(x_ref, w_ref, y_ref):
        xv = x_ref[...]
        row = lax.broadcasted_iota(jnp.int32, xv.shape, 0)
        y_ref[...] = act(_causal_conv(xv, w_ref, row))

    return pl.pallas_call(
        body, name=name + "_fwd", grid=(ch // LANES,),
        in_specs=[pl.BlockSpec((L, LANES), lambda j: (0, j)), pl.BlockSpec((SUBLANES, LANES), lambda j: (0, j))],
        out_specs=pl.BlockSpec((L, LANES), lambda j: (0, j)), out_shape=SDS((L, ch), F32),
        compiler_params=_params(("parallel",), VMEM_MID),
    )(x, w)


def _conv_bwd_call(x, w, dy, act, name):
    L, ch = x.shape

    def body(x_ref, w_ref, dy_ref, dx_ref, dw_ref):
        xv = x_ref[...]
        row = lax.broadcasted_iota(jnp.int32, xv.shape, 0)
        _, act_vjp = jax.vjp(act, _causal_conv(xv, w_ref, row))
        (g,) = act_vjp(dy_ref[...])
        acc = jnp.zeros_like(xv)
        dws = []
        for j in range(GDN_CONV):
            s = GDN_CONV - 1 - j
            acc += w_ref[j:j + 1, :] * _shift_up(g, s, row, L)
            dws.append(jnp.sum(g * _shift_down(xv, s, row), axis=0, keepdims=True))
        dx_ref[...] = acc
        dw_ref[...] = jnp.concatenate(dws + [jnp.zeros((SUBLANES - GDN_CONV, LANES), F32)], axis=0)

    return pl.pallas_call(
        body, name=name + "_bwd", grid=(ch // LANES,),
        in_specs=[pl.BlockSpec((L, LANES), lambda j: (0, j)), pl.BlockSpec((SUBLANES, LANES), lambda j: (0, j)),
                  pl.BlockSpec((L, LANES), lambda j: (0, j))],
        out_specs=[pl.BlockSpec((L, LANES), lambda j: (0, j)), pl.BlockSpec((SUBLANES, LANES), lambda j: (0, j))],
        out_shape=[SDS((L, ch), F32), SDS((SUBLANES, ch), F32)],
        compiler_params=_params(("parallel",), VMEM_MID),
    )(x, w, dy)


def make_conv_act(act, name):
    @jax.custom_vjp
    def op(x, w):
        return _conv_fwd_call(x, w, act, name)

    def fwd(x, w):
        return _conv_fwd_call(x, w, act, name), (x, w)

    def bwd(res, dy):
        x, w = res
        return tuple(_conv_bwd_call(x, w, dy, act, name))

    op.defvjp(fwd, bwd)
    return op


BNN, BNT, BTN = _BNN, _BNT, _BTN
GDN_PREP_BATCH = 16


@jax.custom_vjp
def _known_inverse(a, t):
    return t


def _known_inverse_fwd(a, t):
    return t, t


def _known_inverse_bwd(t, g):
    return -_hdot(_hdot(t, g, _BTN), t, _BNT), jnp.zeros_like(t)


_known_inverse.defvjp(_known_inverse_fwd, _known_inverse_bwd)


def _gdn_prep_math(q, k, v, beta, g, t_saved=None):
    B, C = q.shape[0], q.shape[1]
    ri = lax.broadcasted_iota(jnp.int32, (B, C, C), 1)
    ci = lax.broadcasted_iota(jnp.int32, (B, C, C), 2)
    causal = ri >= ci
    strict = ri > ci
    eye = (ri == ci).astype(F32)
    gb = jnp.broadcast_to(g, (B, C, C))
    g_row = jnp.sum(gb * eye, axis=1, keepdims=True)
    gc_col = jnp.sum(jnp.where(causal, jnp.broadcast_to(g_row, (B, C, C)), 0.0), axis=2, keepdims=True)
    gc_row = jnp.sum(jnp.where(ri <= ci, gb, 0.0), axis=1, keepdims=True)
    decay = jnp.exp(jnp.where(causal, gc_col - gc_row, -jnp.inf))
    kk = _bdot(k, k, BNT)
    a_mat = jnp.where(strict, beta * kk * decay, 0.0)
    t = _unit_lower_inverse(a_mat) if t_saved is None else _known_inverse(a_mat, t_saved)
    e_gc = jnp.exp(gc_col)
    w = _hdot(t, beta * e_gc * k, BNN)
    u = _hdot(t, beta * v, BNN)
    qk = _bdot(q, k, BNT) * decay
    q_dec = q * e_gc
    g_last = gc_col[:, C - 1:C, :]
    k_dec = k * jnp.exp(g_last - gc_col)
    return q_dec, w, u, qk, k_dec, gc_col, t


def _gdn_prep_specs(L):
    C = GDN_CHUNK
    nb = min(GDN_PREP_BATCH, L // C)
    R = nb * C
    ins = [pl.BlockSpec((R, GDN_DK), lambda c, h: (c, h)), pl.BlockSpec((R, GDN_DK), lambda c, h: (c, h)),
           pl.BlockSpec((R, GDN_DV), lambda c, h: (c, h)), pl.BlockSpec((R, LANES), lambda c, h: (c, 0))]
    outs = [pl.BlockSpec((1, R, GDN_DK), lambda c, h: (h, c, 0)), pl.BlockSpec((1, R, GDN_DK), lambda c, h: (h, c, 0)),
            pl.BlockSpec((1, R, GDN_DV), lambda c, h: (h, c, 0)), pl.BlockSpec((1, R, C), lambda c, h: (h, c, 0)),
            pl.BlockSpec((1, R, GDN_DK), lambda c, h: (h, c, 0)), pl.BlockSpec((1, R, 1), lambda c, h: (h, c, 0))]
    t_spec = pl.BlockSpec((1, R, C), lambda c, h: (h, c, 0))
    shapes = [SDS((GDN_HEADS, L, GDN_DK), F32), SDS((GDN_HEADS, L, GDN_DK), F32), SDS((GDN_HEADS, L, GDN_DV), F32),
              SDS((GDN_HEADS, L, C), F32), SDS((GDN_HEADS, L, GDN_DK), F32), SDS((GDN_HEADS, L, 1), F32)]
    return ins, outs, t_spec, shapes, nb


def _chunks(x, nb):
    return x.reshape(nb, x.shape[0] // nb, x.shape[1])


def _head_columns(bg, h):
    lane = lax.broadcasted_iota(jnp.int32, bg.shape, 1)
    beta = jnp.sum(jnp.where(lane == h, bg, 0.0), axis=1, keepdims=True)
    g = jnp.sum(jnp.where(lane == h + GDN_HEADS, bg, 0.0), axis=1, keepdims=True)
    return beta, g


def _gdn_prep_fwd_call(q, k, v, bg):
    L = q.shape[0]
    ins, outs, t_spec, shapes, nb = _gdn_prep_specs(L)

    def body(q_ref, k_ref, v_ref, bg_ref, *o_refs):
        beta, g = _head_columns(bg_ref[...], pl.program_id(1))
        res = _gdn_prep_math(_chunks(q_ref[...], nb), _chunks(k_ref[...], nb), _chunks(v_ref[...], nb),
                             _chunks(beta, nb), _chunks(g, nb))
        for o_ref, val in zip(o_refs, res):
            o_ref[0] = val.reshape(val.shape[0] * val.shape[1], val.shape[2])

    return pl.pallas_call(
        body, name="gdn_prep_fwd", grid=(L // (nb * GDN_CHUNK), GDN_HEADS), in_specs=ins, out_specs=outs + [t_spec],
        out_shape=shapes + [SDS((GDN_HEADS, L, GDN_CHUNK), F32)],
        compiler_params=_params(("parallel", "parallel"), VMEM_MID),
    )(q, k, v, bg)


def _gdn_prep_bwd_call(q, k, v, bg, t, cts):
    L = q.shape[0]
    ins, outs, t_spec, _, nb = _gdn_prep_specs(L)

    def body(q_ref, k_ref, v_ref, bg_ref, t_ref, c0, c1, c2, c3, c4, c5, dq_ref, dk_ref, dv_ref, dbg_ref):
        h = pl.program_id(1)
        beta, g = _head_columns(bg_ref[...], h)
        t_saved = _chunks(t_ref[0], nb)
        _, vjp = jax.vjp(lambda *a: _gdn_prep_math(*a, t_saved=t_saved)[:6], _chunks(q_ref[...], nb), _chunks(k_ref[...], nb),
                         _chunks(v_ref[...], nb), _chunks(beta, nb), _chunks(g, nb))
        dq, dk, dv, db, dg = vjp(tuple(_chunks(c[0], nb) for c in (c0, c1, c2, c3, c4, c5)))
        flat = lambda a: a.reshape(a.shape[0] * a.shape[1], a.shape[2])
        dq_ref[...] = flat(dq)
        dk_ref[...] = flat(dk)
        dv_ref[...] = flat(dv)

        @pl.when(h == 0)
        def _():
            dbg_ref[...] = jnp.zeros_like(dbg_ref)
        lane = lax.broadcasted_iota(jnp.int32, dbg_ref.shape, 1)
        dbg_ref[...] += jnp.where(lane == h, flat(db), 0.0) + jnp.where(lane == h + GDN_HEADS, flat(dg), 0.0)

    return pl.pallas_call(
        body, name="gdn_prep_bwd", grid=(L // (nb * GDN_CHUNK), GDN_HEADS), in_specs=ins + [t_spec] + outs, out_specs=ins,
        out_shape=[SDS(q.shape, F32), SDS(k.shape, F32), SDS(v.shape, F32), SDS(bg.shape, F32)],
        compiler_params=_params(("parallel", "arbitrary"), VMEM_MID),
    )(q, k, v, bg, t, *cts)


@jax.custom_vjp
def gdn_prep(q, k, v, bg):
    return tuple(_gdn_prep_fwd_call(q, k, v, bg)[:6])


def _gdn_prep_f(q, k, v, bg):
    res = _gdn_prep_fwd_call(q, k, v, bg)
    return tuple(res[:6]), (q, k, v, bg, res[6])


def _gdn_prep_b(res, cts):
    return tuple(_gdn_prep_bwd_call(*res, tuple(cts)))


gdn_prep.defvjp(_gdn_prep_f, _gdn_prep_b)


def _gdn_step_math(q_dec, w, u, qk, k_dec, gc, z, nw, state):
    H, C = q_dec.shape[0], q_dec.shape[1]
    v_new = u - _bdot(w, state, BNN)
    o = _bdot(q_dec, state, BNN) + _bdot(qk, v_new, BNN)
    gl = gc[:, C - 1:C, :]
    new_state = jnp.exp(gl) * state + _bdot(k_dec, v_new, BTN)
    return _f_gdn_post(jnp.concatenate([o[h] for h in range(H)], axis=1), z, nw)[0], new_state


GDN_SCAN_CHUNKS = 2


def _gdn_steps_math(q_dec, w, u, qk, k_dec, gc, z, nw, state):
    outs = []
    for i in range(q_dec.shape[1] // GDN_CHUNK):
        s = slice(i * GDN_CHUNK, (i + 1) * GDN_CHUNK)
        o, state = _gdn_step_math(q_dec[:, s], w[:, s], u[:, s], qk[:, s], k_dec[:, s], gc[:, s], z[s], nw, state)
        outs.append(o)
    return jnp.concatenate(outs, axis=0), state


def _gdn_scan_specs(L, rev):
    H = GDN_HEADS
    C = GDN_CHUNK * min(GDN_SCAN_CHUNKS, L // GDN_CHUNK)
    nc = L // C
    cc = (lambda c: nc - 1 - c) if rev else (lambda c: c)
    ins = [pl.BlockSpec((H, C, GDN_DK), lambda c: (0, cc(c), 0)), pl.BlockSpec((H, C, GDN_DK), lambda c: (0, cc(c), 0)),
           pl.BlockSpec((H, C, GDN_DV), lambda c: (0, cc(c), 0)), pl.BlockSpec((H, C, GDN_CHUNK), lambda c: (0, cc(c), 0)),
           pl.BlockSpec((H, C, GDN_DK), lambda c: (0, cc(c), 0)), pl.BlockSpec((H, C, 1), lambda c: (0, cc(c), 0))]
    o_spec = pl.BlockSpec((C, H * GDN_DV), lambda c: (cc(c), 0))
    nw_spec = pl.BlockSpec((1, H * GDN_DV), lambda c: (0, 0))
    s_spec = pl.BlockSpec((1, H, GDN_DK, GDN_DV), lambda c: (cc(c), 0, 0, 0))
    return ins + [o_spec, nw_spec], o_spec, s_spec, nc


def _gdn_scan_fwd_call(q_dec, w, u, qk, k_dec, gc, z, nw):
    L = q_dec.shape[1]
    ins, o_spec, s_spec, nc = _gdn_scan_specs(L, False)

    def body(qd_ref, w_ref, u_ref, qk_ref, kd_ref, gc_ref, z_ref, nw_ref, o_ref, sin_ref, s_ref):
        c = pl.program_id(0)

        @pl.when(c == 0)
        def _():
            s_ref[...] = jnp.zeros_like(s_ref)
        st = s_ref[...]
        sin_ref[0] = st
        o, ns = _gdn_steps_math(qd_ref[...], w_ref[...], u_ref[...], qk_ref[...], kd_ref[...], gc_ref[...], z_ref[...], nw_ref[...], st)
        o_ref[...] = o
        s_ref[...] = ns

    return pl.pallas_call(
        body, name="gdn_scan_fwd", grid=(nc,), in_specs=ins, out_specs=[o_spec, s_spec],
        out_shape=[SDS((L, GDN_HEADS * GDN_DV), F32), SDS((nc, GDN_HEADS, GDN_DK, GDN_DV), F32)],
        scratch_shapes=[pltpu.VMEM((GDN_HEADS, GDN_DK, GDN_DV), F32)],
        compiler_params=_params(("arbitrary",), VMEM_MID),
    )(q_dec, w, u, qk, k_dec, gc, z, nw)


def _gdn_scan_bwd_call(q_dec, w, u, qk, k_dec, gc, z, nw, s_in, do):
    L = q_dec.shape[1]
    ins, o_spec, s_spec, nc = _gdn_scan_specs(L, True)

    def body(qd_ref, w_ref, u_ref, qk_ref, kd_ref, gc_ref, z_ref, nw_ref, sin_ref, do_ref,
             dqd_ref, dw_ref, du_ref, dqk_ref, dkd_ref, dgc_ref, dz_ref, dnw_ref, ds_ref):
        c = pl.program_id(0)

        @pl.when(c == 0)
        def _():
            ds_ref[...] = jnp.zeros_like(ds_ref)
            dnw_ref[...] = jnp.zeros_like(dnw_ref)
        _, vjp = jax.vjp(_gdn_steps_math, qd_ref[...], w_ref[...], u_ref[...], qk_ref[...], kd_ref[...], gc_ref[...],
                         z_ref[...], nw_ref[...], sin_ref[0])
        dqd, dw, du, dqk, dkd, dgc, dz, dnw, dst = vjp((do_ref[...], ds_ref[...]))
        dqd_ref[...] = dqd
        dw_ref[...] = dw
        du_ref[...] = du
        dqk_ref[...] = dqk
        dkd_ref[...] = dkd
        dgc_ref[...] = dgc
        dz_ref[...] = dz
        dnw_ref[...] += dnw
        ds_ref[...] = dst

    return pl.pallas_call(
        body, name="gdn_scan_bwd", grid=(nc,), in_specs=ins + [s_spec, o_spec], out_specs=ins,
        out_shape=[SDS(t.shape, F32) for t in (q_dec, w, u, qk, k_dec, gc, z, nw)],
        scratch_shapes=[pltpu.VMEM((GDN_HEADS, GDN_DK, GDN_DV), F32)],
        compiler_params=_params(("arbitrary",), VMEM_MID),
    )(q_dec, w, u, qk, k_dec, gc, z, nw, s_in, do)


@jax.custom_vjp
def gdn_scan(q_dec, w, u, qk, k_dec, gc, z, nw):
    return _gdn_scan_fwd_call(q_dec, w, u, qk, k_dec, gc, z, nw)[0]


def _gdn_scan_f(*args):
    o, s_in = _gdn_scan_fwd_call(*args)
    return o, (*args, s_in)


def _gdn_scan_b(res, do):
    return tuple(_gdn_scan_bwd_call(*res, do))


gdn_scan.defvjp(_gdn_scan_f, _gdn_scan_b)


def _silu(x):
    return x * jax.nn.sigmoid(x)


def _gelu_tanh(x):
    return 0.5 * x * (1.0 + jnp.tanh(math.sqrt(2.0 / math.pi) * (x + 0.044715 * (x * x * x))))


def _f_lnmod(x, nw, sc, sh, bsc, bsh):
    xn = x * lax.rsqrt(jnp.mean(x * x, axis=-1, keepdims=True) + NORM_EPS) * nw
    return (xn * (1.0 + (sc + bsc)) + (sh + bsh),)


def _f_s5_act(ys, u, d):
    return (_gelu_tanh(ys + d * u),)


def _f_s5_gate(y2, t, z):
    return (y2 * jax.nn.sigmoid(t) * _silu(z),)


def _f_res(x, y, gate, bgate):
    return (x + (gate + bgate) * y,)


def _heads(x, width, fn):
    return jnp.concatenate([fn(x[:, i * width:(i + 1) * width]) for i in range(x.shape[1] // width)], axis=1)


def _l2n(x):
    return x * lax.rsqrt(jnp.sum(x * x, axis=-1, keepdims=True) + NORM_EPS)


def _f_betag(ba, alog, dtb):
    col = lax.broadcasted_iota(jnp.int32, ba.shape, 1)
    t = ba + dtb
    softplus = jnp.maximum(t, 0.0) + jnp.log1p(jnp.exp(-jnp.abs(t)))
    g = -jnp.exp(alog) * softplus
    return (jnp.where(col < GDN_HEADS, jax.nn.sigmoid(ba), jnp.where(col < 2 * GDN_HEADS, g, 0.0)),)


def _f_gdn_post(o, z, nw):
    on = _heads(o, GDN_DV, lambda t: t * lax.rsqrt(jnp.mean(t * t, axis=-1, keepdims=True) + NORM_EPS))
    return (on * nw * _silu(z),)


def _f_loss(x, tgt, fw):
    y = x * lax.rsqrt(jnp.mean(x * x, axis=-1, keepdims=True) + NORM_EPS) * fw
    err = y - tgt
    return (0.5 * jnp.mean(err * err, axis=-1, keepdims=True),)


def _ada_mod_call(c_all, ada_w):
    n = ada_w.shape[2]

    def body(c_ref, w_ref, o_ref):
        ca = _silu(c_ref[...])
        for l in range(ada_w.shape[0]):
            o_ref[l] = _bdot(ca, w_ref[l])

    return pl.pallas_call(body, name="ada_mod", out_shape=SDS((ada_w.shape[0], N_DEV, n), F32),
                          compiler_params=_params(None, VMEM_MID))(c_all, ada_w)


def _ada_grad_call(c_all, dmod):
    nl, _, n = dmod.shape

    def body(c_ref, d_ref, o_ref):
        ca = _silu(c_ref[...])
        for l in range(nl):
            o_ref[l] = _hdot(ca, d_ref[l], TN)

    return pl.pallas_call(body, name="ada_grad", out_shape=SDS((nl, c_all.shape[1], n), F32),
                          compiler_params=_params(None, VMEM_MID))(c_all, dmod)


ADAM_ROWS = 512


def _adamw(g, w, m, v):
    m2 = ADAM_B1 * m + (1.0 - ADAM_B1) * g
    v2 = ADAM_B2 * v + (1.0 - ADAM_B2) * (g * g)
    m_hat = m2 / (1.0 - ADAM_B1 ** ADAM_STEP)
    v_hat = v2 / (1.0 - ADAM_B2 ** ADAM_STEP)
    return g, -ADAM_LR * (m_hat / (jnp.sqrt(v_hat) + ADAM_EPS) + ADAM_WD * w), m2, v2


def _adam_call(gs, w, m, v, name, rows=None, by_cols=False):
    n, r, cols = gs.shape
    if by_cols:
        blk = pl.BlockSpec((r, LANES), lambda i: (0, i))
        g_blk, grid = pl.BlockSpec((n, r, LANES), lambda i: (0, 0, i)), (cols // LANES,)
    else:
        rows = rows or ADAM_ROWS
        blk = pl.BlockSpec((rows, cols), lambda i: (i, 0))
        g_blk, grid = pl.BlockSpec((n, rows, cols), lambda i: (0, i, 0)), (r // rows,)

    def body(g_ref, w_ref, m_ref, v_ref, go_ref, d_ref, mo_ref, vo_ref):
        g = g_ref[0].astype(F32)
        for s in range(1, n):
            g = g + g_ref[s].astype(F32)
        for o_ref, val in zip((go_ref, d_ref, mo_ref, vo_ref), _adamw(g, w_ref[...], m_ref[...], v_ref[...])):
            o_ref[...] = val

    return pl.pallas_call(
        body, name=name, grid=grid, in_specs=[g_blk, blk, blk, blk],
        out_specs=[blk, blk, blk, blk], out_shape=[SDS((r, cols), F32)] * 4,
        compiler_params=_params(("parallel",), VMEM_MID),
    )(gs, w, m, v)


def _sum_call(gs, name, rows):
    n, r, _ = gs.shape

    def body(g_ref, o_ref):
        g = g_ref[0].astype(F32)
        for s in range(1, n):
            g = g + g_ref[s].astype(F32)
        o_ref[...] = g

    return pl.pallas_call(
        body, name=name, grid=(r // rows,),
        in_specs=[pl.BlockSpec((n, rows, LANES), lambda i: (0, i, 0))],
        out_specs=pl.BlockSpec((rows, LANES), lambda i: (i, 0)), out_shape=SDS((r, LANES), F32),
        compiler_params=_params(("parallel",), VMEM_MID),
    )(gs)


def _allgather_call(x_shard, name):
    m_per, n = x_shard.shape

    def body(x_ref, out_ref, send_sems, recv_sems, local_sem):
        x, y, c = lax.axis_index("x"), lax.axis_index("y"), lax.axis_index("c")
        me, sibling = (x, y, c), (x, y, 1 - c)
        chips = [(1 - x, y), (x, 1 - y), (1 - x, 1 - y)]

        def rows(px, py, pc):
            return out_ref.at[pl.ds((4 * px + 2 * py + pc) * m_per, m_per), :]

        def copy(k, block, to, src=None):
            return pltpu.make_async_remote_copy(
                src_ref=rows(*block) if src is None else src, dst_ref=rows(*block),
                send_sem=send_sems.at[k], recv_sem=recv_sems.at[k], device_id=to, device_id_type=pl.DeviceIdType.MESH)

        mine = pltpu.make_async_copy(x_ref, rows(*me), local_sem)
        mine.start()
        first = [copy(0, me, sibling, src=x_ref)]
        first += [copy(1 + j, me, (*chip, c), src=x_ref) for j, chip in enumerate(chips)]
        for cp in first:
            cp.start()
        passed = [copy(4 + j, (*chip, c), sibling) for j, chip in enumerate(chips)]
        for j, chip in enumerate(chips):
            copy(1 + j, (*chip, c), me).wait_recv()
            passed[j].start()
        copy(0, sibling, me).wait_recv()
        for j, chip in enumerate(chips):
            copy(4 + j, (*chip, 1 - c), me).wait_recv()
        for cp in first + passed:
            cp.wait_send()
        mine.wait()

    vmem = pl.BlockSpec(memory_space=pltpu.VMEM)
    return pl.pallas_call(
        body, name=name, out_shape=SDS((N_DEV * m_per, n), x_shard.dtype), in_specs=[vmem], out_specs=vmem,
        scratch_shapes=[pltpu.SemaphoreType.DMA((7,)), pltpu.SemaphoreType.DMA((7,)), pltpu.SemaphoreType.DMA],
    )(x_shard)


def _gather_weights_call(shards, name):
    nw = len(shards)

    def body(*refs):
        x_refs, out_refs = refs[:nw], refs[nw:2 * nw]
        send_sems, recv_sems, local_sems = refs[2 * nw:]
        x, y, c = lax.axis_index("x"), lax.axis_index("y"), lax.axis_index("c")
        me, sibling = (x, y, c), (x, y, 1 - c)
        chips = [(1 - x, y), (x, 1 - y), (1 - x, 1 - y)]

        def slot(w, px, py, pc):
            return out_refs[w].at[4 * px + 2 * py + pc]

        def copy(w, k, block, to, src=None):
            dst = slot(w, *block)
            return pltpu.make_async_remote_copy(
                src_ref=dst if src is None else src, dst_ref=dst, send_sem=send_sems.at[7 * w + k],
                recv_sem=recv_sems.at[7 * w + k], device_id=to, device_id_type=pl.DeviceIdType.MESH)

        mines = [pltpu.make_async_copy(x_refs[w], slot(w, *me), local_sems.at[w]) for w in range(nw)]
        for cp in mines:
            cp.start()
        first = [copy(w, 0, me, sibling, src=x_refs[w]) for w in range(nw)]
        first += [copy(w, 1 + j, me, (*chip, c), src=x_refs[w]) for w in range(nw) for j, chip in enumerate(chips)]
        for cp in first:
            cp.start()
        passed = []
        for w in range(nw):
            for j, chip in enumerate(chips):
                copy(w, 1 + j, (*chip, c), me).wait_recv()
                fwd = copy(w, 4 + j, (*chip, c), sibling)
                fwd.start()
                passed.append(fwd)
        for w in range(nw):
            copy(w, 0, sibling, me).wait_recv()
            for j, chip in enumerate(chips):
                copy(w, 4 + j, (*chip, 1 - c), me).wait_recv()
        for cp in first + passed:
            cp.wait_send()
        for cp in mines:
            cp.wait()

    hbm = pl.BlockSpec(memory_space=pl.ANY)
    return pl.pallas_call(
        body, name=name, out_shape=[SDS((N_DEV,) + s.shape, s.dtype) for s in shards],
        in_specs=[hbm] * nw, out_specs=[hbm] * nw,
        scratch_shapes=[pltpu.SemaphoreType.DMA((7 * nw,)), pltpu.SemaphoreType.DMA((7 * nw,)), pltpu.SemaphoreType.DMA((nw,))],
    )(*shards)


_HBM = pl.BlockSpec(memory_space=pltpu.HBM)
_SEM = pl.BlockSpec(memory_space=pltpu.SEMAPHORE)
_DATAFLOW = pltpu.SideEffectType.DATAFLOW_SIDE_EFFECTING


def _spread_start_call(srcs, per_peer, name, after):
    nw = len(srcs)
    lands = [lax.empty((N_DEV,) + (s.shape[1:] if per_peer else s.shape), s.dtype) for s in srcs]

    def body(*refs):
        src_refs, land_refs = refs[:nw], refs[nw:2 * nw]
        send_sems, recv_sems, token = refs[2 * nw + 1], refs[2 * nw + 2], refs[-1]
        x, y, c = lax.axis_index("x"), lax.axis_index("y"), lax.axis_index("c")
        me = 4 * x + 2 * y + c
        for w in range(nw):
            for k in range(1, N_DEV):
                px = 1 - x if k & 4 else x
                py = 1 - y if k & 2 else y
                pc = 1 - c if k & 1 else c
                src = src_refs[w].at[4 * px + 2 * py + pc] if per_peer else src_refs[w]
                pltpu.make_async_remote_copy(
                    src_ref=src, dst_ref=land_refs[w].at[me], send_sem=send_sems.at[w], recv_sem=recv_sems.at[w],
                    device_id=(px, py, pc), device_id_type=pl.DeviceIdType.MESH).start()
        token[...] = jnp.zeros_like(token)

    hbm = lambda a: pltpu.with_memory_space_constraint(a, pltpu.HBM)
    res = pl.pallas_call(
        body, name=name,
        out_shape=(pltpu.SemaphoreType.DMA((nw,)), pltpu.SemaphoreType.DMA((nw,)))
        + tuple(pltpu.HBM(s.shape, s.dtype) for s in srcs) + tuple(pltpu.HBM(l.shape, l.dtype) for l in lands)
        + (SDS((SUBLANES, LANES), F32),),
        in_specs=[_HBM] * (2 * nw) + [pl.BlockSpec(memory_space=pl.ANY)],
        out_specs=(_SEM, _SEM) + (_HBM,) * (2 * nw) + (pl.BlockSpec(memory_space=pltpu.VMEM),),
        input_output_aliases={i: i + 2 for i in range(2 * nw)},
        compiler_params=pltpu.CompilerParams(has_side_effects=_DATAFLOW),
    )(*[hbm(s) for s in srcs], *[hbm(l) for l in lands], after)
    return res[0], res[1], res[2:2 + nw], res[2 + nw:2 + 2 * nw], res[-1]


def _spread_wait_call(send_sems, recv_sems, srcs, lands, after, name):
    nw = len(lands)

    def body(*refs):
        land_refs = refs[nw:2 * nw]
        s_sems, r_sems = refs[2 * nw], refs[2 * nw + 1]
        x, y, c = lax.axis_index("x"), lax.axis_index("y"), lax.axis_index("c")
        for w in range(nw):
            seven = land_refs[w].at[pl.ds(0, N_DEV - 1)]
            all_seven = pltpu.make_async_remote_copy(
                src_ref=seven, dst_ref=seven, send_sem=s_sems.at[w], recv_sem=r_sems.at[w],
                device_id=(x, y, c), device_id_type=pl.DeviceIdType.MESH)
            all_seven.wait_send()
            all_seven.wait_recv()

    res = pl.pallas_call(
        body, name=name,
        out_shape=tuple(pltpu.HBM(s.shape, s.dtype) for s in srcs) + tuple(pltpu.HBM(l.shape, l.dtype) for l in lands),
        in_specs=[_HBM] * (2 * nw) + [_SEM, _SEM, pl.BlockSpec(memory_space=pl.ANY)], out_specs=(_HBM,) * (2 * nw),
        input_output_aliases={i: i for i in range(2 * nw)},
        compiler_params=pltpu.CompilerParams(has_side_effects=_DATAFLOW),
    )(*srcs, *lands, send_sems, recv_sems, after)
    return res[:nw], res[nw:]


def _join_cols_call(w8, name):
    _, k, n = w8.shape
    tk = _tile(k, 256)

    def body(w_ref, o_ref):
        for s in range(N_DEV):
            o_ref[:, n * s:n * (s + 1)] = w_ref[s]

    return pl.pallas_call(body, name=name, grid=(k // tk,), in_specs=[pl.BlockSpec((N_DEV, tk, n), lambda i: (0, i, 0))],
                          out_specs=pl.BlockSpec((tk, N_DEV * n), lambda i: (i, 0)), out_shape=SDS((k, N_DEV * n), w8.dtype),
                          compiler_params=_params(("parallel",), VMEM_MID))(w8)


def _split_cols_call(g, name, dtype):
    k, n8 = g.shape
    n = n8 // N_DEV
    tk = _tile(k, 256)

    def body(g_ref, o_ref):
        for s in range(N_DEV):
            o_ref[s] = g_ref[:, n * s:n * (s + 1)].astype(dtype)

    return pl.pallas_call(body, name=name, grid=(k // tk,), in_specs=[pl.BlockSpec((tk, n8), lambda i: (i, 0))],
                          out_specs=pl.BlockSpec((N_DEV, tk, n), lambda i: (0, i, 0)), out_shape=SDS((N_DEV, k, n), dtype),
                          compiler_params=_params(("parallel",), VMEM_MID))(g)


def _pack(parts, rows_multiple):
    flat = jnp.concatenate([p.reshape(-1) for p in parts])
    unit = rows_multiple * LANES
    padded = -(-flat.shape[0] // unit) * unit
    flat = jnp.concatenate([flat, jnp.zeros((padded - flat.shape[0],), F32)])
    return flat.reshape(-1, LANES)


def _groups_last(a):
    x, y = a.shape[-2:]
    return jnp.transpose(a.reshape(S5_GROUPS, x, y), (1, 2, 0)).reshape(x * y, S5_GROUPS)


def _groups_first(a, shape):
    x, y = shape[-2:]
    return jnp.transpose(a.reshape(x, y, S5_GROUPS), (2, 0, 1)).reshape(shape)


def _unpack(buf, shapes):
    flat = buf.reshape(-1)
    out, off = [], 0
    for s in shapes:
        n = math.prod(s)
        out.append(flat[off:off + n].reshape(s))
        off += n
    return out


def _row_tile(L):
    return 256 if L % 256 == 0 else L


def _layer0_mix(diff, const):
    x, mod, norm_w, lam_re, lam_im, log_dt, b_re, b_im, c_re, c_im, s5_d, *slots = diff
    ada_b, weights = const
    L = x.shape[0]
    tm = _row_tile(L)
    mods = mod.reshape(2, 1, D_MODEL)
    biases = ada_b.reshape(2, 1, D_MODEL)
    op_ln0 = make_rowwise(_f_lnmod, "ln0", tm, 1, 5, pass_first=True)
    h, x = op_ln0((x,), (norm_w.reshape(1, D_MODEL), mods[1], mods[0], biases[1], biases[0]))
    u, z = make_proj("s5_in")(h, tuple(weights), tuple(slots))
    blocks = _s5_block_params(lam_re, lam_im, log_dt, b_re, b_im, c_re, c_im)
    y2 = make_s5_core(min(S5_TL, L))(u, *blocks, s5_d.reshape(1, D_INNER))
    return x, y2, z


def _glu_gate_fwd_call(y2, w, z, name):
    m, k = y2.shape
    n = w.shape[1]
    tm, tn = _tile(m, 1024), _tile(n, 512)

    def body(a_ref, w_ref, z_ref, t_ref, y4_ref):
        j = pl.program_id(1)
        t = _bdot(a_ref[...], w_ref[...])
        t_ref[...] = t
        y2_tile = a_ref[:, pl.ds(pl.multiple_of(j * tn, LANES), tn)]
        y4_ref[...] = _f_s5_gate(y2_tile, t, z_ref[...])[0]

    tile = pl.BlockSpec((tm, tn), lambda i, j: (i, j))
    return pl.pallas_call(
        body, name=name, grid=(m // tm, n // tn),
        in_specs=[pl.BlockSpec((tm, k), lambda i, j: (i, 0)), pl.BlockSpec((k, tn), lambda i, j: (0, j)), tile],
        out_specs=[tile, tile], out_shape=[SDS((m, n), F32), SDS((m, n), F32)],
        compiler_params=_params(("parallel", "parallel"), VMEM_MID),
    )(y2, w, z)


def make_glu_gate(name, tm):
    gate = make_rowwise(_f_s5_gate, name + "_gate", tm, 3, 0)

    @jax.custom_vjp
    def op(y2, w, grad_slot, z):
        return _glu_gate_fwd_call(y2, w, z, name + "_fwd")[1]

    def fwd(y2, w, grad_slot, z):
        t, y4 = _glu_gate_fwd_call(y2, w, z, name + "_fwd")
        return y4, (y2, w, t, z)

    def bwd(res, dy4):
        y2, w, t, z = res
        (dy2_gate, dt, dz), _ = gate.run_bwd((y2, t, z), (), (dy4,))
        return _matmul(dt, w, "nt", name + "_dx", add=dy2_gate), jnp.zeros_like(w), _matmul(y2, dt, "tn", name + "_dw"), dz

    op.defvjp(fwd, bwd)
    return op


def _layer0_out(diff, weights):
    y2, z, *slots = diff
    y4 = make_glu_gate("s5_glu", _row_tile(y2.shape[0]))(y2, weights[0], slots[0], z)
    return make_mm("s5_out")(y4, weights[1], slots[1])


def _f_res_lnmod(x, o, gate, bgate, nw, sc, sh, bsc, bsh):
    (x1,) = _f_res(x, o, gate, bgate)
    return _f_lnmod(x1, nw, sc, sh, bsc, bsh) + (x1,)


def _f_res_loss(x, y, tgt, gate, bgate, fw):
    return _f_loss(_f_res(x, y, gate, bgate)[0], tgt, fw)


def _layer1_loss(diff, const):
    x, o, gate0, mod, norm_w, conv_w, a_log, dt_bias, gdn_nw, final_nw, *slots = diff
    tgt, bgate0, ada_b, weights = const
    L = x.shape[0]
    tm = _row_tile(L)
    mods = mod.reshape(3, 1, D_MODEL)
    biases = ada_b.reshape(3, 1, D_MODEL)
    h, x1 = make_rowwise(_f_res_lnmod, "res0_ln1", tm, 2, 7)(
        (x, o), (gate0.reshape(1, D_MODEL), bgate0.reshape(1, D_MODEL), norm_w.reshape(1, D_MODEL), mods[1], mods[0], biases[1], biases[0]))
    q0, k0, v0, gz, ba = make_proj("gdn_in", w_rows=True)(h, tuple(weights[0:5]), tuple(slots[0:5]))
    cw = jnp.concatenate([conv_w, jnp.zeros((SUBLANES - GDN_CONV, GDN_CONV_CH), F32)], axis=0)
    q = make_conv_act(lambda t: _l2n(_silu(t)) * (GDN_DK ** -0.5), "gdn_conv_q")(q0, cw[:, :GDN_QK])
    k = make_conv_act(lambda t: _l2n(_silu(t)), "gdn_conv_k")(k0, cw[:, GDN_QK:2 * GDN_QK])
    v = make_conv_act(_silu, "gdn_conv_v")(v0, cw[:, 2 * GDN_QK:])
    pad = jnp.zeros((LANES - 2 * GDN_HEADS,), F32)
    alog_row = jnp.concatenate([jnp.zeros((GDN_HEADS,), F32), a_log, pad]).reshape(1, LANES)
    dtb_row = jnp.concatenate([jnp.zeros((GDN_HEADS,), F32), dt_bias, pad]).reshape(1, LANES)
    (bg,) = make_rowwise(_f_betag, "gdn_bg", tm, 1, 2)((ba,), (alog_row, dtb_row))
    nw_row = jnp.tile(gdn_nw, GDN_HEADS).reshape(1, D_INNER)
    on = gdn_scan(*gdn_prep(q, k, v, bg), gz, nw_row)
    y = make_mm("gdn_out")(on, weights[5], slots[5])
    (lt,) = make_rowwise(_f_res_loss, "res1_loss", tm, 3, 3)((x1, y, tgt), (mods[2], biases[2], final_nw.reshape(1, D_MODEL)))
    return jnp.sum(lt)


VEC_NAMES = ("ada_b", "norm_w", "s5_lambda_re", "s5_lambda_im", "s5_log_dt", "s5_d", "gdn_a_log", "gdn_dt_bias", "final_norm_w")
MAT_NAMES = ("s5_b_re", "s5_b_im", "s5_c_re", "s5_c_im")
S5_BIG = ("s5_w_in", "s5_w_glu", "s5_w_out")
GDN_BIG = ("gdn_w_in", "gdn_w_out")
BIG_NAMES = S5_BIG + GDN_BIG
WEIGHT_ORDER = ("ada_w", "ada_b", "norm_w", "s5_w_in", "s5_lambda_re", "s5_lambda_im", "s5_log_dt", "s5_b_re", "s5_b_im",
                "s5_c_re", "s5_c_im", "s5_d", "s5_w_glu", "s5_w_out", "gdn_w_in", "gdn_conv_w", "gdn_a_log", "gdn_dt_bias",
                "gdn_norm_w", "gdn_w_out", "final_norm_w")


def _step(x, c, W, M, V, tgt):
    L = x.shape[1]
    ix, iy, ic = lax.axis_index("x"), lax.axis_index("y"), lax.axis_index("c")
    me = 4 * ix + 2 * iy + ic
    n_ada = W["ada_w"].shape[2]
    n_conv = W["gdn_conv_w"].shape[2]
    n_gnw = W["gdn_norm_w"].shape[1]

    g1 = _allgather_call(_pack([c, W["gdn_conv_w"], W["gdn_norm_w"]], SUBLANES), "gather_small_in")
    g1 = g1.reshape(N_DEV, -1)
    c_all = g1[:, :D_MODEL]
    conv_w = g1[:, D_MODEL:D_MODEL + GDN_CONV * n_conv].reshape(N_DEV, GDN_CONV, n_conv).transpose(1, 0, 2).reshape(GDN_CONV, -1)
    gdn_nw = g1[:, D_MODEL + GDN_CONV * n_conv:D_MODEL + GDN_CONV * n_conv + n_gnw].reshape(-1)
    mod_part = _ada_mod_call(c_all, W["ada_w"])
    g2 = _allgather_call(_pack([mod_part], SUBLANES), "gather_mod").reshape(N_DEV, -1)
    mod_all = g2[:, :2 * N_DEV * n_ada].reshape(N_DEV, 2, N_DEV, n_ada)
    mod_raw = lax.dynamic_index_in_dim(mod_all, me, axis=2, keepdims=False)
    mod_raw = mod_raw.transpose(1, 0, 2).reshape(2, 3 * D_MODEL)

    shard = lambda n: W[n][0].astype(BF16)
    (w_in5_parts,) = _gather_weights_call([shard("s5_w_in")], "gather_s5_w_in")
    turned = lambda a: jnp.transpose(a[0])
    late = _spread_start_call([shard("s5_w_glu"), shard("s5_w_out"), turned(W["gdn_w_in"]).astype(BF16), shard("gdn_w_out")],
                              False, "gather_late_start", w_in5_parts)
    g_token = late[4]
    w_in5 = _join_cols_call(w_in5_parts, "join_s5_w_in")
    slot = lambda *s: jnp.zeros(s, F32)
    two = 2 * D_MODEL
    diff_mix = (x[0], mod_raw[0, :two] + g_token[0, 0], W["norm_w"][0], W["s5_lambda_re"][0], W["s5_lambda_im"][0], W["s5_log_dt"][0],
                W["s5_b_re"][0], W["s5_b_im"][0], W["s5_c_re"][0], W["s5_c_im"][0], W["s5_d"][0],
                slot(D_MODEL, D_INNER), slot(D_MODEL, D_INNER))

    (xp, y2, z5), vjp_mix = jax.vjp(lambda d: _layer0_mix(d, (W["ada_b"][0, :two], (w_in5[:, :D_INNER], w_in5[:, D_INNER:]))), diff_mix)
    l_srcs, l_lands = _spread_wait_call(late[0], late[1], late[2], late[3], y2, "gather_late_wait")
    arrived = [lax.dynamic_update_slice(land, src[None], (me, 0, 0)) for land, src in zip(l_lands, l_srcs)]
    w_glu, w_o5 = [a.reshape(-1, a.shape[2]) for a in arrived[:2]]
    diff_out = (y2, z5, slot(D_INNER, D_INNER), slot(D_INNER, D_MODEL))
    o5, vjp_out = jax.vjp(lambda d: _layer0_out(d, (w_glu, w_o5)), diff_out)
    gdn_full = arrived[2:]
    w_ing = gdn_full[0].reshape(GDN_PROJ, D_MODEL)
    w_ba = jnp.concatenate([w_ing[GDN_CONV_CH + D_INNER:], jnp.zeros((LANES - 2 * GDN_HEADS, D_MODEL), BF16)], axis=0)
    weights1 = (w_ing[:GDN_QK], w_ing[GDN_QK:2 * GDN_QK], w_ing[2 * GDN_QK:GDN_CONV_CH],
                w_ing[GDN_CONV_CH:GDN_CONV_CH + D_INNER], w_ba, gdn_full[1].reshape(D_INNER, D_MODEL))
    slots1 = tuple(jnp.zeros(w.shape, F32) for w in weights1)
    diff1 = (xp, o5, mod_raw[0, two:], mod_raw[1], W["norm_w"][1], conv_w, W["gdn_a_log"][0], W["gdn_dt_bias"][0], gdn_nw,
             W["final_norm_w"], *slots1)
    loss_local, vjp1 = jax.vjp(lambda d: _layer1_loss(d, (tgt[0], W["ada_b"][0, two:], W["ada_b"][1], weights1)), diff1)
    ((dxp, do5, dmod_gate, dmod1, d_norm_w1, d_conv, d_alog, d_dtb, d_gnw, d_fnw, d_wq, d_wk, d_wv, d_wgz, d_wba, d_wog),) = vjp1(
        jnp.ones((), F32))
    loss = lax.psum(loss_local, MESH_AXES)

    rows = lambda d: d.reshape(N_DEV, d.shape[0] // N_DEV, d.shape[1])
    d_ing = jnp.concatenate([d_wq, d_wk, d_wv, d_wgz, d_wba[:2 * GDN_HEADS]], axis=0).astype(BF16).reshape(N_DEV, -1, D_MODEL)
    s_send, s_recv, s_srcs, s_lands, s_token = _spread_start_call([d_ing, rows(d_wog).astype(BF16)], True, "scatter_gdn_start", dxp)
    ((dy2, dz5, d_wglu, d_wo5),) = vjp_out(do5.at[0, 0].add(s_token[0, 0]))
    t_send, t_recv, t_srcs, t_lands, t_token = _spread_start_call(
        [rows(d_wglu).astype(BF16), rows(d_wo5).astype(BF16)], True, "scatter_s5_late_start", dy2)
    ((dx, dmod_ss, d_norm_w0, d_lre, d_lim, d_logdt, d_bre, d_bim, d_cre, d_cim, d_s5d, d_wu, d_wz),) = vjp_mix(
        (dxp.at[0, 0].add(t_token[0, 0]), dy2, dz5))
    dmod = jnp.stack([jnp.concatenate([dmod_ss, dmod_gate]), dmod1])
    d_norm_w = jnp.stack([d_norm_w0, d_norm_w1])
    vec_parts = [dmod, d_norm_w, d_lre, d_lim, d_logdt, d_s5d, d_alog, d_dtb, d_fnw]
    tail_parts = [d_conv, d_gnw]
    mat_parts = [_groups_last(d) for d in (d_bre, d_bim, d_cre, d_cim)]
    n_vec = sum(math.prod(p.shape) for p in vec_parts)
    m_send, m_recv, m_srcs, m_lands, m_token = _spread_start_call(
        [_pack(vec_parts + tail_parts, ADAM_ROWS), _pack(mat_parts, SUBLANES).astype(BF16)], False, "gather_small_grads_start", dx)
    d_in5 = _split_cols_call(jnp.concatenate([d_wu.at[0, 0].add(m_token[0, 0]), d_wz], axis=1), "split_s5_w_in", BF16)
    u_send, u_recv, u_srcs, u_lands, u_token = _spread_start_call([d_in5], True, "scatter_s5_in_start", m_token)
    t_srcs, t_lands = _spread_wait_call(t_send, t_recv, t_srcs, t_lands, u_token, "scatter_s5_late_wait")
    s_srcs, s_lands = _spread_wait_call(s_send, s_recv, s_srcs, s_lands, t_lands[0], "scatter_gdn_wait")
    big = {}

    def owner_update(land, src, n):
        mine = lax.dynamic_index_in_dim(src, me, 0, keepdims=True)
        parts = lax.dynamic_update_slice(land, mine, (me, 0, 0))
        if n == "gdn_w_in":
            outs = _adam_call(parts, turned(W[n]), turned(M[n]), turned(V[n]), "adam_" + n, by_cols=True)
            return [jnp.transpose(o) for o in outs]
        return _adam_call(parts, W[n][0], M[n][0], V[n][0], "adam_" + n, rows=_tile(W[n].shape[1], 128))

    for land, src, n in zip(tuple(t_lands) + tuple(s_lands), tuple(t_srcs) + tuple(s_srcs), ("s5_w_glu", "s5_w_out") + GDN_BIG):
        big[n] = owner_update(land, src, n)

    m_srcs, m_lands = _spread_wait_call(m_send, m_recv, m_srcs, m_lands, big["gdn_w_out"][0], "gather_small_grads_wait")
    sg_vec, sg_mat = [lax.dynamic_update_slice(land, src[None], (me, 0, 0)) for land, src in zip(m_lands, m_srcs)]
    tot_vec = _sum_call(sg_vec, "sum_vec_grads", ADAM_ROWS)
    tot_mat = _sum_call(sg_mat, "sum_mat_grads", ADAM_ROWS)
    g_conv, g_gnw = _unpack(tot_vec.reshape(-1)[n_vec:], [d_conv.shape, d_gnw.shape])
    g_conv_mine = lax.dynamic_slice_in_dim(g_conv, me * n_conv, n_conv, axis=1)
    g_gnw_mine = lax.dynamic_slice_in_dim(g_gnw, me * n_gnw, n_gnw, axis=0)
    vec_names = VEC_NAMES + ("gdn_conv_w", "gdn_norm_w")
    vec_g = _pack([tot_vec.reshape(-1)[:n_vec], g_conv_mine, g_gnw_mine], ADAM_ROWS)
    vec = _adam_call(vec_g[None], _pack([W[n] for n in vec_names], ADAM_ROWS), _pack([M[n] for n in vec_names], ADAM_ROWS),
                     _pack([V[n] for n in vec_names], ADAM_ROWS), "adam_vec")
    vec = [_unpack(b, [W[n].shape for n in vec_names]) for b in vec]
    mats = []
    for name, g_mat in zip(MAT_NAMES, _unpack(tot_mat, [p.shape for p in mat_parts])):
        outs = _adam_call(g_mat[None], _groups_last(W[name]), _groups_last(M[name]), _groups_last(V[name]), "adam_" + name)
        mats.append([_groups_first(o, W[name].shape) for o in outs])

    dmod_all = sg_vec[:, :2 * 3 * D_MODEL // LANES].reshape(N_DEV, 2, N_DEV, n_ada // LANES, LANES)
    dmod_mine = lax.dynamic_index_in_dim(dmod_all, me, axis=2, keepdims=False).transpose(1, 0, 2, 3).reshape(2, N_DEV, n_ada)
    g_ada_w = _ada_grad_call(c_all, dmod_mine)
    ada = _adam_call(g_ada_w.reshape(1, -1, LANES), W["ada_w"].reshape(-1, LANES), M["ada_w"].reshape(-1, LANES),
                     V["ada_w"].reshape(-1, LANES), "adam_ada")
    u_srcs, u_lands = _spread_wait_call(u_send, u_recv, u_srcs, u_lands, ada[0][:SUBLANES] + loss, "scatter_s5_in_wait")
    big["s5_w_in"] = owner_update(u_lands[0], u_srcs[0], "s5_w_in")
    ada = [a.reshape(W["ada_w"].shape) for a in ada]

    res = {}
    for n in BIG_NAMES:
        res[n] = [o[None] for o in big[n]]
    for i, n in enumerate(vec_names):
        res[n] = [b[i] for b in vec]
    for i, n in enumerate(MAT_NAMES):
        res[n] = mats[i]
    res["ada_w"] = ada
    outs = [loss, dx[None]]
    for j in range(4):
        outs += [res[n][j] for n in WEIGHT_ORDER]
    return tuple(outs)


def kernel(x, c, ada_w, ada_b, norm_w, s5_w_in, s5_lambda_re, s5_lambda_im, s5_log_dt, s5_b_re, s5_b_im, s5_c_re, s5_c_im, s5_d, s5_w_glu, s5_w_out, gdn_w_in, gdn_conv_w, gdn_a_log, gdn_dt_bias, gdn_norm_w, gdn_w_out, final_norm_w, loss_target, m_ada_w, m_ada_b, m_norm_w, m_s5_w_in, m_s5_lambda_re, m_s5_lambda_im, m_s5_log_dt, m_s5_b_re, m_s5_b_im, m_s5_c_re, m_s5_c_im, m_s5_d, m_s5_w_glu, m_s5_w_out, m_gdn_w_in, m_gdn_conv_w, m_gdn_a_log, m_gdn_dt_bias, m_gdn_norm_w, m_gdn_w_out, m_final_norm_w, v_ada_w, v_ada_b, v_norm_w, v_s5_w_in, v_s5_lambda_re, v_s5_lambda_im, v_s5_log_dt, v_s5_b_re, v_s5_b_im, v_s5_c_re, v_s5_c_im, v_s5_d, v_s5_w_glu, v_s5_w_out, v_gdn_w_in, v_gdn_conv_w, v_gdn_a_log, v_gdn_dt_bias, v_gdn_norm_w, v_gdn_w_out, v_final_norm_w):
    W = dict(ada_w=ada_w, ada_b=ada_b, norm_w=norm_w, s5_w_in=s5_w_in, s5_lambda_re=s5_lambda_re, s5_lambda_im=s5_lambda_im,
             s5_log_dt=s5_log_dt, s5_b_re=s5_b_re, s5_b_im=s5_b_im, s5_c_re=s5_c_re, s5_c_im=s5_c_im, s5_d=s5_d,
             s5_w_glu=s5_w_glu, s5_w_out=s5_w_out, gdn_w_in=gdn_w_in, gdn_conv_w=gdn_conv_w, gdn_a_log=gdn_a_log,
             gdn_dt_bias=gdn_dt_bias, gdn_norm_w=gdn_norm_w, gdn_w_out=gdn_w_out, final_norm_w=final_norm_w)
    M = dict(ada_w=m_ada_w, ada_b=m_ada_b, norm_w=m_norm_w, s5_w_in=m_s5_w_in, s5_lambda_re=m_s5_lambda_re,
             s5_lambda_im=m_s5_lambda_im, s5_log_dt=m_s5_log_dt, s5_b_re=m_s5_b_re, s5_b_im=m_s5_b_im, s5_c_re=m_s5_c_re,
             s5_c_im=m_s5_c_im, s5_d=m_s5_d, s5_w_glu=m_s5_w_glu, s5_w_out=m_s5_w_out, gdn_w_in=m_gdn_w_in,
             gdn_conv_w=m_gdn_conv_w, gdn_a_log=m_gdn_a_log, gdn_dt_bias=m_gdn_dt_bias, gdn_norm_w=m_gdn_norm_w,
             gdn_w_out=m_gdn_w_out, final_norm_w=m_final_norm_w)
    V = dict(ada_w=v_ada_w, ada_b=v_ada_b, norm_w=v_norm_w, s5_w_in=v_s5_w_in, s5_lambda_re=v_s5_lambda_re,
             s5_lambda_im=v_s5_lambda_im, s5_log_dt=v_s5_log_dt, s5_b_re=v_s5_b_re, s5_b_im=v_s5_b_im, s5_c_re=v_s5_c_re,
             s5_c_im=v_s5_c_im, s5_d=v_s5_d, s5_w_glu=v_s5_w_glu, s5_w_out=v_s5_w_out, gdn_w_in=v_gdn_w_in,
             gdn_conv_w=v_gdn_conv_w, gdn_a_log=v_gdn_a_log, gdn_dt_bias=v_gdn_dt_bias, gdn_norm_w=v_gdn_norm_w,
             gdn_w_out=v_gdn_w_out, final_norm_w=v_final_norm_w)
    return _step(x, c, W, M, V, loss_target)
```
